```python
import math
import jax, jax.numpy as jnp
from jax import lax
import numpy as np

D_MODEL = 1024
BATCH = 8
SEQ = 2048
DEPTH = 1

N_META = 16
BLOCK = 128
WINDOW = 128
PAD = (-N_META) % BLOCK
NORM_EPS = 1e-6

ATT_HEAD_DIM = 64
ATT_Q_HEADS = D_MODEL // ATT_HEAD_DIM
ATT_KV_HEADS = 4
ATT_GROUP = ATT_Q_HEADS // ATT_KV_HEADS
ATT_WIDTH = ATT_Q_HEADS * ATT_HEAD_DIM
KV_WIDTH = ATT_KV_HEADS * ATT_HEAD_DIM

SSM_INNER = 2 * D_MODEL
SSM_HEAD_DIM = 64
SSM_HEADS = SSM_INNER // SSM_HEAD_DIM
SSM_GROUPS = 4
SSM_HEADS_PER_GROUP = SSM_HEADS // SSM_GROUPS
SSM_STATE = 128
CONV_WIDTH = 4
CONV_DIM = SSM_INNER + 2 * SSM_GROUPS * SSM_STATE

SPLIT_SIZES = (ATT_WIDTH, KV_WIDTH, KV_WIDTH, ATT_WIDTH, SSM_INNER, CONV_DIM, SSM_HEADS, D_MODEL, D_MODEL)
SPLIT_POINTS = tuple(int(s) for s in np.cumsum(SPLIT_SIZES)[:-1])
IN_PROJ_DIM = int(sum(SPLIT_SIZES))

kernel_name = 'hybrid_swa_sink_alibi_ssd_gated_merge'


def rmsnorm(x, g):
    xf = x.astype(jnp.float32)
    y = xf * lax.rsqrt(jnp.mean(xf * xf, axis=-1, keepdims=True) + NORM_EPS) * g.astype(jnp.float32)
    return y.astype(x.dtype)


def alibi_slopes():
    return jnp.asarray(np.array([2.0 ** (-8.0 * (h + 1) / ATT_Q_HEADS) for h in range(ATT_Q_HEADS)], np.float32))


def sliding_window_attention(q, k, v, sinks):
    b, lp, _ = q.shape
    nb = lp // BLOCK
    km = k[:, PAD:PAD + N_META].reshape(b, N_META, ATT_KV_HEADS, ATT_HEAD_DIM)
    vm = v[:, PAD:PAD + N_META].reshape(b, N_META, ATT_KV_HEADS, ATT_HEAD_DIM)
    qb = q.reshape(b, nb, BLOCK, ATT_KV_HEADS, ATT_GROUP, ATT_HEAD_DIM) * (ATT_HEAD_DIM ** -0.5)

    def with_prev(t):
        t = t.reshape(b, nb, BLOCK, ATT_KV_HEADS, ATT_HEAD_DIM)
        prev = jnp.concatenate([jnp.zeros_like(t[:, :1]), t[:, :-1]], axis=1)
        return jnp.concatenate([prev, t], axis=2)

    kb, vb = with_prev(k), with_prev(v)

    q_pos = jnp.arange(nb)[:, None] * BLOCK + jnp.arange(BLOCK)[None, :]
    k_pos = jnp.arange(nb)[:, None] * BLOCK - BLOCK + jnp.arange(2 * BLOCK)[None, :]
    rel = q_pos[:, :, None] - k_pos[:, None, :]
    band_ok = (rel >= 0) & (rel < WINDOW) & (k_pos[:, None, :] >= PAD + N_META)
    meta_pos = PAD + jnp.arange(N_META)
    meta_ok = meta_pos[None, None, :] <= q_pos[:, :, None]

    slopes = alibi_slopes().reshape(1, 1, ATT_KV_HEADS, ATT_GROUP, 1, 1)
    s_band = jnp.einsum('bnqkgd,bnskd->bnkgqs', qb, kb).astype(jnp.float32)
    s_band = s_band - slopes * rel[None, :, None, None].astype(jnp.float32)
    s_band = jnp.where(band_ok[None, :, None, None], s_band, -jnp.inf)
    s_meta = jnp.einsum('bnqkgd,bmkd->bnkgqm', qb, km).astype(jnp.float32)
    s_meta = jnp.where(meta_ok[None, :, None, None], s_meta, -jnp.inf)
    sink = jnp.broadcast_to(sinks.astype(jnp.float32).reshape(1, 1, ATT_KV_HEADS, ATT_GROUP, 1, 1),
                            s_meta.shape[:-1] + (1,))
    p = jax.nn.softmax(jnp.concatenate([sink, s_meta, s_band], axis=-1), axis=-1).astype(v.dtype)
    p_meta, p_band = p[..., 1:1 + N_META], p[..., 1 + N_META:]
    o = jnp.einsum('bnkgqm,bmkd->bnqkgd', p_meta, vm) + jnp.einsum('bnkgqs,bnskd->bnqkgd', p_band, vb)
    return o.reshape(b, lp, ATT_WIDTH)


def causal_depthwise_conv(u, w, bias):
    out = lax.conv_general_dilated(u, w[:, None, :].astype(u.dtype), window_strides=(1,),
                                   padding=[(CONV_WIDTH - 1, 0)],
                                   dimension_numbers=('NWC', 'WIO', 'NWC'),
                                   feature_group_count=u.shape[-1])
    return out + bias.astype(u.dtype)


def segsum(a):
    T = a.shape[-1]
    cs = jnp.cumsum(a, axis=-1)
    diff = cs[..., :, None] - cs[..., None, :]
    return jnp.where(jnp.tril(jnp.ones((T, T), bool)), diff, -jnp.inf)


def ssd_chunked(x, dt, A, Bm, Cm):
    b, L = x.shape[:2]
    nc = L // BLOCK
    G, R, P, N = SSM_GROUPS, SSM_HEADS_PER_GROUP, SSM_HEAD_DIM, SSM_STATE
    xr = (x * dt[..., None]).reshape(b, nc, BLOCK, G, R, P)
    a = (dt * A).reshape(b, nc, BLOCK, G, R).transpose(0, 1, 3, 4, 2)
    Br = Bm.reshape(b, nc, BLOCK, G, N)
    Cr = Cm.reshape(b, nc, BLOCK, G, N)
    a_cs = jnp.cumsum(a, axis=-1)
    decay = jnp.exp(segsum(a))
    cb = jnp.einsum('bclgn,bcsgn->bcgls', Cr, Br)
    y_diag = jnp.einsum('bcgls,bcgrls,bcsgrp->bclgrp', cb, decay, xr)
    decay_states = jnp.exp(a_cs[..., -1:] - a_cs)
    states = jnp.einsum('bclgn,bcgrl,bclgrp->bcgrpn', Br, decay_states, xr)
    chunk_decay = jnp.exp(a_cs[..., -1])

    def step(carry, inp):
        s_c, d_c = inp
        return carry * d_c[..., None, None] + s_c, carry

    init = jnp.zeros((b, G, R, P, N), jnp.float32)
    _, prev = lax.scan(step, init, (jnp.moveaxis(states, 1, 0), jnp.moveaxis(chunk_decay, 1, 0)))
    prev = jnp.moveaxis(prev, 0, 1)
    y_off = jnp.einsum('bclgn,bcgrpn,bcgrl->bclgrp', Cr, prev, jnp.exp(a_cs))
    return (y_diag + y_off).reshape(b, L, SSM_HEADS, P)


def ssd_branch(z, xbc, dt_raw, conv_w, conv_b, dt_bias, a_log, d_skip, g_norm, valid):
    b, L, _ = xbc.shape
    xbc = jax.nn.silu(causal_depthwise_conv(xbc, conv_w, conv_b)) * valid[None, :, None]
    xbc = xbc.astype(jnp.float32)
    xs = xbc[..., :SSM_INNER].reshape(b, L, SSM_HEADS, SSM_HEAD_DIM)
    Bm = xbc[..., SSM_INNER:SSM_INNER + SSM_GROUPS * SSM_STATE].reshape(b, L, SSM_GROUPS, SSM_STATE)
    Cm = xbc[..., SSM_INNER + SSM_GROUPS * SSM_STATE:].reshape(b, L, SSM_GROUPS, SSM_STATE)
    dt = jax.nn.softplus(dt_raw.astype(jnp.float32) + dt_bias.astype(jnp.float32))
    A = -jnp.exp(a_log.astype(jnp.float32))
    y = ssd_chunked(xs, dt, A, Bm, Cm) + d_skip.astype(jnp.float32)[:, None] * xs
    y = y.reshape(b, L, SSM_INNER) * jax.nn.silu(z.astype(jnp.float32))
    yg = y.reshape(b, L, SSM_GROUPS, SSM_INNER // SSM_GROUPS)
    yg = yg * lax.rsqrt(jnp.mean(yg * yg, axis=-1, keepdims=True) + NORM_EPS)
    y = yg.reshape(b, L, SSM_INNER) * g_norm.astype(jnp.float32)
    return y.astype(z.dtype)


def hybrid_layer(h, valid, g_pre, w_in, conv_w, conv_b, dt_bias, a_log, d_skip, attn_sinks,
                 g_ssm_norm, w_out_att, w_out_ssm, w_out, g_post):
    u = rmsnorm(h, g_pre)
    proj = u @ w_in
    q, k, v, z_att, z_ssm, xbc, dt_raw, gate_att, gate_ssm = jnp.split(proj, SPLIT_POINTS, axis=-1)
    y_att = (sliding_window_attention(q, k, v, attn_sinks) * jax.nn.silu(z_att)) @ w_out_att
    y_ssm = ssd_branch(z_ssm, xbc, dt_raw, conv_w, conv_b, dt_bias, a_log, d_skip, g_ssm_norm, valid) @ w_out_ssm
    merged = jax.nn.sigmoid(gate_att) * y_att + jax.nn.sigmoid(gate_ssm) * y_ssm
    out = merged @ w_out
    return h + rmsnorm(out, g_post) * valid[None, :, None]


def _fwd_setup_inputs(seed: int = 0) -> dict:
    key = jax.random.key(seed)
    ks = jax.random.split(key, 16)
    f32 = jnp.float32

    def nrm(k, shape, scale):
        return jax.random.normal(k, shape, f32) * scale

    x = nrm(ks[0], (BATCH, SEQ, D_MODEL), 1.0)
    meta_tokens = nrm(ks[1], (N_META, D_MODEL), 1.0)
    g_pre = 1.0 + nrm(ks[2], (DEPTH, D_MODEL), 0.01)
    w_in = nrm(ks[3], (DEPTH, D_MODEL, IN_PROJ_DIM), D_MODEL ** -0.5)
    conv_w = nrm(ks[4], (DEPTH, CONV_WIDTH, CONV_DIM), CONV_WIDTH ** -0.5)
    conv_b = nrm(ks[5], (DEPTH, CONV_DIM), 0.02)
    dt0 = jnp.exp(jax.random.uniform(ks[6], (DEPTH, SSM_HEADS), f32, math.log(1e-3), math.log(1e-1)))
    dt_bias = dt0 + jnp.log(-jnp.expm1(-dt0))
    a_log = jnp.log(jax.random.uniform(ks[7], (DEPTH, SSM_HEADS), f32, 1.0, 16.0))
    d_skip = 1.0 + nrm(ks[8], (DEPTH, SSM_HEADS), 0.1)
    attn_sinks = nrm(ks[9], (DEPTH, ATT_Q_HEADS), 0.5)
    g_ssm_norm = 1.0 + nrm(ks[10], (DEPTH, SSM_INNER), 0.01)
    w_out_att = nrm(ks[11], (DEPTH, ATT_WIDTH, D_MODEL), ATT_WIDTH ** -0.5)
    w_out_ssm = nrm(ks[12], (DEPTH, SSM_INNER, D_MODEL), SSM_INNER ** -0.5)
    w_out = nrm(ks[13], (DEPTH, D_MODEL, D_MODEL), D_MODEL ** -0.5)
    g_post = 1.0 + nrm(ks[14], (DEPTH, D_MODEL), 0.01)
    return {'x': x, 'meta_tokens': meta_tokens, 'g_pre': g_pre, 'w_in': w_in, 'conv_w': conv_w,
            'conv_b': conv_b, 'dt_bias': dt_bias, 'a_log': a_log, 'd_skip': d_skip,
            'attn_sinks': attn_sinks, 'g_ssm_norm': g_ssm_norm, 'w_out_att': w_out_att,
            'w_out_ssm': w_out_ssm, 'w_out': w_out, 'g_post': g_post}


def _fwd_reference(x, meta_tokens, g_pre, w_in, conv_w, conv_b, dt_bias, a_log, d_skip, attn_sinks,
              g_ssm_norm, w_out_att, w_out_ssm, w_out, g_post):
    b = x.shape[0]
    lp = PAD + N_META + x.shape[1]
    h = jnp.concatenate([jnp.zeros((b, PAD, D_MODEL), x.dtype),
                         jnp.broadcast_to(meta_tokens.astype(x.dtype)[None], (b, N_META, D_MODEL)),
                         x], axis=1)
    valid = (jnp.arange(lp) >= PAD).astype(x.dtype)
    for i in range(DEPTH):
        h = hybrid_layer(h, valid, g_pre[i], w_in[i], conv_w[i], conv_b[i], dt_bias[i], a_log[i],
                         d_skip[i], attn_sinks[i], g_ssm_norm[i], w_out_att[i], w_out_ssm[i],
                         w_out[i], g_post[i])
    return h[:, PAD + N_META:]


import jax as _jax
import jax.numpy as _jnp

TWIN_FORMAT = 'train_step'
FWD_PARAMS = ['x', 'meta_tokens', 'g_pre', 'w_in', 'conv_w', 'conv_b', 'dt_bias', 'a_log', 'd_skip', 'attn_sinks', 'g_ssm_norm', 'w_out_att', 'w_out_ssm', 'w_out', 'g_post']
TWIN_WEIGHTS = ['meta_tokens', 'g_pre', 'w_in', 'conv_w', 'conv_b', 'dt_bias', 'a_log', 'd_skip', 'attn_sinks', 'g_ssm_norm', 'w_out_att', 'w_out_ssm', 'w_out', 'g_post']
TWIN_DIFF_INPUT = 'x'
TWIN_INPUTS = ['x', 'meta_tokens', 'g_pre', 'w_in', 'conv_w', 'conv_b', 'dt_bias', 'a_log', 'd_skip', 'attn_sinks', 'g_ssm_norm', 'w_out_att', 'w_out_ssm', 'w_out', 'g_post', 'loss_target', 'm_meta_tokens', 'm_g_pre', 'm_w_in', 'm_conv_w', 'm_conv_b', 'm_dt_bias', 'm_a_log', 'm_d_skip', 'm_attn_sinks', 'm_g_ssm_norm', 'm_w_out_att', 'm_w_out_ssm', 'm_w_out', 'm_g_post', 'v_meta_tokens', 'v_g_pre', 'v_w_in', 'v_conv_w', 'v_conv_b', 'v_dt_bias', 'v_a_log', 'v_d_skip', 'v_attn_sinks', 'v_g_ssm_norm', 'v_w_out_att', 'v_w_out_ssm', 'v_w_out', 'v_g_post']
TWIN_OUTPUTS = ['loss', 'grad_x', 'grad_meta_tokens', 'grad_g_pre', 'grad_w_in', 'grad_conv_w', 'grad_conv_b', 'grad_dt_bias', 'grad_a_log', 'grad_d_skip', 'grad_attn_sinks', 'grad_g_ssm_norm', 'grad_w_out_att', 'grad_w_out_ssm', 'grad_w_out', 'grad_g_post', 'delta_meta_tokens', 'delta_g_pre', 'delta_w_in', 'delta_conv_w', 'delta_conv_b', 'delta_dt_bias', 'delta_a_log', 'delta_d_skip', 'delta_attn_sinks', 'delta_g_ssm_norm', 'delta_w_out_att', 'delta_w_out_ssm', 'delta_w_out', 'delta_g_post', 'new_m_meta_tokens', 'new_m_g_pre', 'new_m_w_in', 'new_m_conv_w', 'new_m_conv_b', 'new_m_dt_bias', 'new_m_a_log', 'new_m_d_skip', 'new_m_attn_sinks', 'new_m_g_ssm_norm', 'new_m_w_out_att', 'new_m_w_out_ssm', 'new_m_w_out', 'new_m_g_post', 'new_v_meta_tokens', 'new_v_g_pre', 'new_v_w_in', 'new_v_conv_w', 'new_v_conv_b', 'new_v_dt_bias', 'new_v_a_log', 'new_v_d_skip', 'new_v_attn_sinks', 'new_v_g_ssm_norm', 'new_v_w_out_att', 'new_v_w_out_ssm', 'new_v_w_out', 'new_v_g_post']
TWIN_LEAF_KINDS = {'loss': 'loss', 'grad_x': 'grad_x', 'grad_meta_tokens': 'grad_w', 'grad_g_pre': 'grad_w', 'grad_w_in': 'grad_w', 'grad_conv_w': 'grad_w', 'grad_conv_b': 'grad_w', 'grad_dt_bias': 'grad_w', 'grad_a_log': 'grad_w', 'grad_d_skip': 'grad_w', 'grad_attn_sinks': 'grad_w', 'grad_g_ssm_norm': 'grad_w', 'grad_w_out_att': 'grad_w', 'grad_w_out_ssm': 'grad_w', 'grad_w_out': 'grad_w', 'grad_g_post': 'grad_w', 'delta_meta_tokens': 'delta_w', 'delta_g_pre': 'delta_w', 'delta_w_in': 'delta_w', 'delta_conv_w': 'delta_w', 'delta_conv_b': 'delta_w', 'delta_dt_bias': 'delta_w', 'delta_a_log': 'delta_w', 'delta_d_skip': 'delta_w', 'delta_attn_sinks': 'delta_w', 'delta_g_ssm_norm': 'delta_w', 'delta_w_out_att': 'delta_w', 'delta_w_out_ssm': 'delta_w', 'delta_w_out': 'delta_w', 'delta_g_post': 'delta_w', 'new_m_meta_tokens': 'new_m', 'new_m_g_pre': 'new_m', 'new_m_w_in': 'new_m', 'new_m_conv_w': 'new_m', 'new_m_conv_b': 'new_m', 'new_m_dt_bias': 'new_m', 'new_m_a_log': 'new_m', 'new_m_d_skip': 'new_m', 'new_m_attn_sinks': 'new_m', 'new_m_g_ssm_norm': 'new_m', 'new_m_w_out_att': 'new_m', 'new_m_w_out_ssm': 'new_m', 'new_m_w_out': 'new_m', 'new_m_g_post': 'new_m', 'new_v_meta_tokens': 'new_v', 'new_v_g_pre': 'new_v', 'new_v_w_in': 'new_v', 'new_v_conv_w': 'new_v', 'new_v_conv_b': 'new_v', 'new_v_dt_bias': 'new_v', 'new_v_a_log': 'new_v', 'new_v_d_skip': 'new_v', 'new_v_attn_sinks': 'new_v', 'new_v_g_ssm_norm': 'new_v', 'new_v_w_out_att': 'new_v', 'new_v_w_out_ssm': 'new_v', 'new_v_w_out': 'new_v', 'new_v_g_post': 'new_v'}


def _forward(args):
    return _fwd_reference(*[args[k] for k in FWD_PARAMS])


def _output_shape():
    out = _jax.eval_shape(lambda: _forward(_fwd_setup_inputs(0)))
    return out.shape, out.dtype

N_MICROBATCH = 1
ADAM_LR = 0.001
ADAM_B1 = 0.9
ADAM_B2 = 0.999
ADAM_EPS = 1e-08
ADAM_WD = 0.01
ADAM_STEP = 10
PER_EXAMPLE_BATCH_AXIS = {'x': 0, 'loss_target': 0}
SHARED_INPUTS = []
_WEIGHT_DTYPES = {'meta_tokens': _jnp.float32, 'g_pre': _jnp.float32, 'w_in': _jnp.float32, 'conv_w': _jnp.float32, 'conv_b': _jnp.float32, 'dt_bias': _jnp.float32, 'a_log': _jnp.float32, 'd_skip': _jnp.float32, 'attn_sinks': _jnp.float32, 'g_ssm_norm': _jnp.float32, 'w_out_att': _jnp.float32, 'w_out_ssm': _jnp.float32, 'w_out': _jnp.float32, 'g_post': _jnp.float32}
MOMENT_SCALE = {'meta_tokens': 9.007526e-03, 'g_pre': 3.289400e-01, 'w_in': 1.036923e-01, 'conv_w': 1.239294e-01, 'conv_b': 2.564392e-01, 'dt_bias': 2.658116e-01, 'a_log': 3.085174e-01, 'd_skip': 8.043705e-01, 'attn_sinks': 8.958505e-03, 'g_ssm_norm': 1.630091e-01, 'w_out_att': 3.505851e-02, 'w_out_ssm': 2.407267e-01, 'w_out': 2.434227e-01, 'g_post': 1.602224e+01}


def _to_microbatches(a, axis):
    t = _jnp.moveaxis(a, axis, 0)
    t = t.reshape((N_MICROBATCH, t.shape[0] // N_MICROBATCH) + t.shape[1:])
    return _jnp.moveaxis(t, 1, axis + 1)


def setup_inputs(seed: int = 0) -> dict:
    inp = _fwd_setup_inputs(seed)
    key = _jax.random.fold_in(_jax.random.key(seed), 7919)
    shape, _ = _output_shape()
    out = dict(inp)
    out["loss_target"] = _jax.random.normal(_jax.random.fold_in(key, 0), shape, _jnp.float32)
    for i, name in enumerate(TWIN_WEIGHTS):
        w = inp[name].astype(_jnp.float32)
        if MOMENT_SCALE is None:
            s = _jnp.sqrt(_jnp.mean(_jnp.square(w)) + 1e-30)
        else:
            s = MOMENT_SCALE[name]
        km, kv = _jax.random.split(_jax.random.fold_in(key, i + 1))
        out[name] = w
        out["m_" + name] = s * _jax.random.normal(km, w.shape, _jnp.float32)
        out["v_" + name] = (s * s) * _jax.random.uniform(kv, w.shape, _jnp.float32, 0.5, 1.5)
    if N_MICROBATCH > 1:
        for name, axis in PER_EXAMPLE_BATCH_AXIS.items():
            out[name] = _to_microbatches(out[name], axis)
    return {'x': out['x'], 'meta_tokens': out['meta_tokens'], 'g_pre': out['g_pre'], 'w_in': out['w_in'], 'conv_w': out['conv_w'], 'conv_b': out['conv_b'], 'dt_bias': out['dt_bias'], 'a_log': out['a_log'], 'd_skip': out['d_skip'], 'attn_sinks': out['attn_sinks'], 'g_ssm_norm': out['g_ssm_norm'], 'w_out_att': out['w_out_att'], 'w_out_ssm': out['w_out_ssm'], 'w_out': out['w_out'], 'g_post': out['g_post'], 'loss_target': out['loss_target'], 'm_meta_tokens': out['m_meta_tokens'], 'm_g_pre': out['m_g_pre'], 'm_w_in': out['m_w_in'], 'm_conv_w': out['m_conv_w'], 'm_conv_b': out['m_conv_b'], 'm_dt_bias': out['m_dt_bias'], 'm_a_log': out['m_a_log'], 'm_d_skip': out['m_d_skip'], 'm_attn_sinks': out['m_attn_sinks'], 'm_g_ssm_norm': out['m_g_ssm_norm'], 'm_w_out_att': out['m_w_out_att'], 'm_w_out_ssm': out['m_w_out_ssm'], 'm_w_out': out['m_w_out'], 'm_g_post': out['m_g_post'], 'v_meta_tokens': out['v_meta_tokens'], 'v_g_pre': out['v_g_pre'], 'v_w_in': out['v_w_in'], 'v_conv_w': out['v_conv_w'], 'v_conv_b': out['v_conv_b'], 'v_dt_bias': out['v_dt_bias'], 'v_a_log': out['v_a_log'], 'v_d_skip': out['v_d_skip'], 'v_attn_sinks': out['v_attn_sinks'], 'v_g_ssm_norm': out['v_g_ssm_norm'], 'v_w_out_att': out['v_w_out_att'], 'v_w_out_ssm': out['v_w_out_ssm'], 'v_w_out': out['v_w_out'], 'v_g_post': out['v_g_post']}


def _loss(weights, diff, rest, loss_target):
    with _jax.named_scope("forward"):
        args = {**rest, TWIN_DIFF_INPUT: diff, **{k: w.astype(_WEIGHT_DTYPES[k]) for k, w in weights.items()}}
        y = _forward(args)
    with _jax.named_scope("loss_head"):
        err = _jnp.square(y.astype(_jnp.float32) - loss_target)
        return 0.5 * _jnp.sum(_jnp.mean(err, axis=-1)) if err.ndim else 0.5 * err


def _adamw(w, g, m, v):
    m = ADAM_B1 * m + (1.0 - ADAM_B1) * g
    v = ADAM_B2 * v + (1.0 - ADAM_B2) * _jnp.square(g)
    m_hat = m / (1.0 - ADAM_B1 ** ADAM_STEP)
    v_hat = v / (1.0 - ADAM_B2 ** ADAM_STEP)
    delta = -ADAM_LR * (m_hat / (_jnp.sqrt(v_hat) + ADAM_EPS) + ADAM_WD * w)
    return delta, m, v


def reference(x, meta_tokens, g_pre, w_in, conv_w, conv_b, dt_bias, a_log, d_skip, attn_sinks, g_ssm_norm, w_out_att, w_out_ssm, w_out, g_post, loss_target, m_meta_tokens, m_g_pre, m_w_in, m_conv_w, m_conv_b, m_dt_bias, m_a_log, m_d_skip, m_attn_sinks, m_g_ssm_norm, m_w_out_att, m_w_out_ssm, m_w_out, m_g_post, v_meta_tokens, v_g_pre, v_w_in, v_conv_w, v_conv_b, v_dt_bias, v_a_log, v_d_skip, v_attn_sinks, v_g_ssm_norm, v_w_out_att, v_w_out_ssm, v_w_out, v_g_post):
    given = dict(x=x, meta_tokens=meta_tokens, g_pre=g_pre, w_in=w_in, conv_w=conv_w, conv_b=conv_b, dt_bias=dt_bias, a_log=a_log, d_skip=d_skip, attn_sinks=attn_sinks, g_ssm_norm=g_ssm_norm, w_out_att=w_out_att, w_out_ssm=w_out_ssm, w_out=w_out, g_post=g_post, loss_target=loss_target, m_meta_tokens=m_meta_tokens, m_g_pre=m_g_pre, m_w_in=m_w_in, m_conv_w=m_conv_w, m_conv_b=m_conv_b, m_dt_bias=m_dt_bias, m_a_log=m_a_log, m_d_skip=m_d_skip, m_attn_sinks=m_attn_sinks, m_g_ssm_norm=m_g_ssm_norm, m_w_out_att=m_w_out_att, m_w_out_ssm=m_w_out_ssm, m_w_out=m_w_out, m_g_post=m_g_post, v_meta_tokens=v_meta_tokens, v_g_pre=v_g_pre, v_w_in=v_w_in, v_conv_w=v_conv_w, v_conv_b=v_conv_b, v_dt_bias=v_dt_bias, v_a_log=v_a_log, v_d_skip=v_d_skip, v_attn_sinks=v_attn_sinks, v_g_ssm_norm=v_g_ssm_norm, v_w_out_att=v_w_out_att, v_w_out_ssm=v_w_out_ssm, v_w_out=v_w_out, v_g_post=v_g_post)
    weights = {n: given[n] for n in TWIN_WEIGHTS}
    shared = {n: given[n] for n in SHARED_INPUTS}
    per_example = {n: given[n] for n in ['x']}
    grad_fn = _jax.value_and_grad(_loss, argnums=(0, 1))

    def one_microbatch(ex, loss_target):
        ex = dict(ex)
        diff = ex.pop(TWIN_DIFF_INPUT)
        return grad_fn(weights, diff, {**shared, **ex}, loss_target)

    if N_MICROBATCH == 1:
        loss, (grad_w, grad_x) = one_microbatch(per_example, given["loss_target"])
    else:
        def body(carry, xs):
            loss_sum, grad_sum = carry
            l_k, (gw_k, gx_k) = one_microbatch(xs[0], xs[1])
            with _jax.named_scope("update"):
                return (loss_sum + l_k, _jax.tree.map(_jnp.add, grad_sum, gw_k)), gx_k

        init = (_jnp.zeros((), _jnp.float32), _jax.tree.map(_jnp.zeros_like, weights))
        (loss, grad_w), grad_x = _jax.lax.scan(body, init, (per_example, given["loss_target"]))
    with _jax.named_scope("update"):
        delta_w, new_m, new_v = {}, {}, {}
        for n in TWIN_WEIGHTS:
            delta_w[n], new_m[n], new_v[n] = _adamw(weights[n], grad_w[n], given["m_" + n], given["v_" + n])
    return (loss, grad_x, *[grad_w[n] for n in TWIN_WEIGHTS], *[delta_w[n] for n in TWIN_WEIGHTS],
            *[new_m[n] for n in TWIN_WEIGHTS], *[new_v[n] for n in TWIN_WEIGHTS])
```

```python
import functools
import math

import jax
import jax.numpy as jnp
from jax import lax
from jax.experimental import pallas as pl
from jax.experimental.pallas import tpu as pltpu

F32 = jnp.float32
BF16 = jnp.bfloat16
SDS = jax.ShapeDtypeStruct
HI = lax.Precision.HIGHEST
MESH = pl.DeviceIdType.MESH
ANY = pl.BlockSpec(memory_space=pl.ANY)

N_DEV = 8
D_MODEL = 1024
SEQ = 2048
N_META = 16
BLK = 128
PAD = 112
T = PAD + N_META + SEQ
NB = T // BLK
EPS = 1e-6
HEAD = 64
Q_HEADS = 16
KV_HEADS = 4
GROUP = 4
KV_W = 256
SSM_INNER = 2048
SSM_HEADS = 32
SSM_GROUPS = 4
GRP_W = 512
SSM_STATE = 128
CONV_DIM = 3072
IN_PROJ = 9760
SHARD_IN = IN_PROJ // N_DEV
NEG = -1e30

C_Q, C_ZA, C_GA, C_GS, C_ZS, C_XBC, C_K, C_V, C_DT = 0, 1024, 2048, 3072, 4096, 6144, 9216, 9472, 9728
PW = 9856
R_Q, R_K, R_V, R_ZA, R_ZS, R_XBC, R_DT, R_GA, R_GS = 0, 1024, 1280, 1536, 2560, 4608, 7680, 7712, 8736

ADAM_LR, ADAM_B1, ADAM_B2, ADAM_EPS, ADAM_WD, ADAM_STEP = 0.001, 0.9, 0.999, 1e-08, 0.01, 10

VMEM_LIMIT = 56 * 1024 * 1024


def _cparams():
    return pltpu.CompilerParams(vmem_limit_bytes=VMEM_LIMIT)


def _silu(x):
    return x * jax.nn.sigmoid(x)


def _dsilu(x):
    s = jax.nn.sigmoid(x)
    return s * (1.0 + x * (1.0 - s))


def _matmul(a, b, mode, out_dtype, tm, tn, tk, name):
    if mode == "nn":
        (m, k), n = a.shape, b.shape[1]
        a_spec = pl.BlockSpec((tm, tk), lambda i, j, kk: (i, kk))
        b_spec = pl.BlockSpec((tk, tn), lambda i, j, kk: (kk, j))
        dims = (((1,), (0,)), ((), ()))
    elif mode == "nt":
        (m, k), n = a.shape, b.shape[0]
        a_spec = pl.BlockSpec((tm, tk), lambda i, j, kk: (i, kk))
        b_spec = pl.BlockSpec((tn, tk), lambda i, j, kk: (j, kk))
        dims = (((1,), (1,)), ((), ()))
    else:
        (k, m), n = a.shape, b.shape[1]
        a_spec = pl.BlockSpec((tk, tm), lambda i, j, kk: (kk, i))
        b_spec = pl.BlockSpec((tk, tn), lambda i, j, kk: (kk, j))
        dims = (((0,), (0,)), ((), ()))
    assert m % tm == 0 and n % tn == 0 and k % tk == 0, (a.shape, b.shape, tm, tn, tk)
    nk = k // tk

    def body(a_ref, b_ref, o_ref, acc_ref):
        kk = pl.program_id(2)
        part = lax.dot_general(a_ref[...], b_ref[...], dims, preferred_element_type=F32)

        @pl.when(kk == 0)
        def _():
            acc_ref[...] = part

        @pl.when(kk > 0)
        def _():
            acc_ref[...] += part

        @pl.when(kk == nk - 1)
        def _():
            o_ref[...] = acc_ref[...].astype(out_dtype)

    return pl.pallas_call(
        body, grid=(m // tm, n // tn, nk), in_specs=[a_spec, b_spec],
        out_specs=pl.BlockSpec((tm, tn), lambda i, j, kk: (i, j)),
        out_shape=SDS((m, n), out_dtype), scratch_shapes=[pltpu.VMEM((tm, tn), F32)],
        compiler_params=_cparams(), name=name)(a, b)


def _norm_u(h, g_pre):
    def body(h_ref, g_ref, u_ref):
        x = h_ref[...]
        r = lax.rsqrt(jnp.mean(x * x, axis=-1, keepdims=True) + EPS)
        u_ref[...] = (x * r * g_ref[...]).astype(BF16)

    return pl.pallas_call(
        body, grid=(NB,),
        in_specs=[pl.BlockSpec((BLK, D_MODEL), lambda i: (i, 0)), pl.BlockSpec((1, D_MODEL), lambda i: (0, 0))],
        out_specs=pl.BlockSpec((BLK, D_MODEL), lambda i: (i, 0)),
        out_shape=SDS((T, D_MODEL), BF16), name="norm_u")(h, g_pre)


def _norm_bwd(h, g_pre, du, dres):
    def body(h_ref, g_ref, du_ref, dres_ref, dh_ref, dg_ref):
        i = pl.program_id(0)
        x = h_ref[...]
        g = g_ref[...]
        du_ = du_ref[...]
        r = lax.rsqrt(jnp.mean(x * x, axis=-1, keepdims=True) + EPS)
        gd = g * du_
        dx = r * gd - x * (r * r * r) * jnp.mean(x * gd, axis=-1, keepdims=True)
        dh_ref[...] = dx + dres_ref[...]
        part = jnp.sum(du_ * x * r, axis=0, keepdims=True)

        @pl.when(i == 0)
        def _():
            dg_ref[...] = jnp.zeros_like(dg_ref)

        dg_ref[0:1, :] += part

    row = pl.BlockSpec((BLK, D_MODEL), lambda i: (i, 0))
    return pl.pallas_call(
        body, grid=(NB,),
        in_specs=[row, pl.BlockSpec((1, D_MODEL), lambda i: (0, 0)), row, row],
        out_specs=[row, pl.BlockSpec((8, D_MODEL), lambda i: (0, 0))],
        out_shape=[SDS((T, D_MODEL), F32), SDS((8, D_MODEL), F32)], name="norm_bwd")(h, g_pre, du, dres)


def _lane_pick(row, h):
    lane = lax.broadcasted_iota(jnp.int32, row.shape, 1)
    return jnp.sum(jnp.where(lane == h, row, 0.0), axis=1, keepdims=True)


def _attn_fn(q4s, kcats, vcats, kms, vms, sinks, n):
    r = lax.broadcasted_iota(jnp.int32, (GROUP * BLK, 2 * BLK), 0)
    s = lax.broadcasted_iota(jnp.int32, (GROUP * BLK, 2 * BLK), 1)
    i = jnp.bitwise_and(r, BLK - 1)
    gi = jnp.right_shift(r, 7)
    rel = i - s + BLK
    k_pos = n * BLK - BLK + s
    band_ok = (rel >= 0) & (rel < BLK) & (k_pos >= PAD + N_META)
    relf = rel.astype(F32)
    rm = lax.broadcasted_iota(jnp.int32, (GROUP * BLK, N_META), 0)
    mm = lax.broadcasted_iota(jnp.int32, (GROUP * BLK, N_META), 1)
    meta_ok = (PAD + mm) <= (n * BLK + jnp.bitwise_and(rm, BLK - 1))
    gcol = jnp.right_shift(lax.broadcasted_iota(jnp.int32, (GROUP * BLK, 1), 0), 7)
    outs = []
    for kh in range(KV_HEADS):
        slopes = [2.0 ** (-8.0 * (kh * GROUP + g + 1) / Q_HEADS) for g in range(GROUP)]
        slope = jnp.where(gi == 0, slopes[0], jnp.where(gi == 1, slopes[1], jnp.where(gi == 2, slopes[2], slopes[3])))
        sk = [_lane_pick(sinks, kh * GROUP + g) for g in range(GROUP)]
        sink = jnp.where(gcol == 0, sk[0], jnp.where(gcol == 1, sk[1], jnp.where(gcol == 2, sk[2], sk[3])))
        qb = (q4s[kh] * (HEAD ** -0.5)).astype(BF16)
        sb = lax.dot_general(qb, kcats[kh].astype(BF16), (((1,), (1,)), ((), ())), preferred_element_type=F32)
        sb = jnp.where(band_ok, sb - slope * relf, NEG)
        sm = lax.dot_general(qb, kms[kh].astype(BF16), (((1,), (1,)), ((), ())), preferred_element_type=F32)
        sm = jnp.where(meta_ok, sm, NEG)
        mx = jnp.maximum(jnp.maximum(jnp.max(sb, axis=1, keepdims=True), jnp.max(sm, axis=1, keepdims=True)), sink)
        mx = lax.stop_gradient(mx)
        eb = jnp.exp(sb - mx)
        em = jnp.exp(sm - mx)
        es = jnp.exp(sink - mx)
        inv = 1.0 / (jnp.sum(eb, axis=1, keepdims=True) + jnp.sum(em, axis=1, keepdims=True) + es)
        pb = (eb * inv).astype(BF16)
        pm = (em * inv).astype(BF16)
        o4 = (jnp.dot(pm, vms[kh].astype(BF16), preferred_element_type=F32)
              + jnp.dot(pb, vcats[kh].astype(BF16), preferred_element_type=F32))
        outs.append(o4)
    return outs


def _attn_specs():
    prev = lambda n: jnp.maximum(n - 1, 0)
    return [
        pl.BlockSpec((BLK, D_MODEL), lambda n: (n, C_Q // D_MODEL)),
        pl.BlockSpec((BLK, KV_W), lambda n: (prev(n), C_K // KV_W)),
        pl.BlockSpec((BLK, KV_W), lambda n: (n, C_K // KV_W)),
        pl.BlockSpec((BLK, KV_W), lambda n: (prev(n), C_V // KV_W)),
        pl.BlockSpec((BLK, KV_W), lambda n: (n, C_V // KV_W)),
        pl.BlockSpec((N_META, KV_W), lambda n: (PAD // N_META, C_K // KV_W)),
        pl.BlockSpec((N_META, KV_W), lambda n: (PAD // N_META, C_V // KV_W)),
        pl.BlockSpec((1, 128), lambda n: (0, 0)),
    ]


def _attn_load(q_ref, kp_ref, kc_ref, vp_ref, vc_ref, km_ref, vm_ref):
    q4s, kcats, vcats, kms, vms = [], [], [], [], []
    for kh in range(KV_HEADS):
        q4s.append(jnp.concatenate(
            [q_ref[:, (kh * GROUP + g) * HEAD:(kh * GROUP + g + 1) * HEAD] for g in range(GROUP)], axis=0))
        cs = slice(kh * HEAD, (kh + 1) * HEAD)
        kcats.append(jnp.concatenate([kp_ref[:, cs], kc_ref[:, cs]], axis=0))
        vcats.append(jnp.concatenate([vp_ref[:, cs], vc_ref[:, cs]], axis=0))
        kms.append(km_ref[:, cs])
        vms.append(vm_ref[:, cs])
    return q4s, kcats, vcats, kms, vms


def _attn_fwd(proj, sinks):
    def body(q_ref, kp_ref, kc_ref, vp_ref, vc_ref, km_ref, vm_ref, s_ref, o_ref):
        n = pl.program_id(0)
        args = _attn_load(q_ref, kp_ref, kc_ref, vp_ref, vc_ref, km_ref, vm_ref)
        outs = _attn_fn(*args, s_ref[...], n)
        for kh in range(KV_HEADS):
            for g in range(GROUP):
                hh = kh * GROUP + g
                o_ref[:, hh * HEAD:(hh + 1) * HEAD] = outs[kh][g * BLK:(g + 1) * BLK]

    return pl.pallas_call(
        body, grid=(NB,), in_specs=_attn_specs(),
        out_specs=pl.BlockSpec((BLK, D_MODEL), lambda n: (n, 0)),
        out_shape=SDS((T, D_MODEL), F32), name="attn_fwd")(proj, proj, proj, proj, proj, proj, proj, sinks)


def _attn_bwd(proj, sinks, do):
    def body(q_ref, kp_ref, kc_ref, vp_ref, vc_ref, km_ref, vm_ref, s_ref, do_ref, dq_ref, dk_ref, dv_ref, ds_ref):
        n = pl.program_id(0)

        @pl.when(n == 0)
        def _():
            dk_ref[...] = jnp.zeros_like(dk_ref)
            dv_ref[...] = jnp.zeros_like(dv_ref)
            ds_ref[...] = jnp.zeros_like(ds_ref)

        args = _attn_load(q_ref, kp_ref, kc_ref, vp_ref, vc_ref, km_ref, vm_ref)
        _, vjp = jax.vjp(lambda a, b, c, d, e, f: _attn_fn(a, b, c, d, e, f, n), *args, s_ref[...])
        cot = [jnp.concatenate([do_ref[:, (kh * GROUP + g) * HEAD:(kh * GROUP + g + 1) * HEAD] for g in range(GROUP)],
                               axis=0) for kh in range(KV_HEADS)]
        dq4s, dkcats, dvcats, dkms, dvms, dsk = vjp(cot)
        ds_ref[0:1, :] += dsk
        cur = pl.ds(pl.multiple_of(n * BLK, BLK), BLK)
        meta = slice(PAD, PAD + N_META)
        for kh in range(KV_HEADS):
            cs = slice(kh * HEAD, (kh + 1) * HEAD)
            for g in range(GROUP):
                hh = kh * GROUP + g
                dq_ref[:, hh * HEAD:(hh + 1) * HEAD] = dq4s[kh][g * BLK:(g + 1) * BLK]
            dk_ref[cur, cs] += dkcats[kh][BLK:]
            dv_ref[cur, cs] += dvcats[kh][BLK:]
            dk_ref[meta, cs] += dkms[kh]
            dv_ref[meta, cs] += dvms[kh]

        @pl.when(n > 0)
        def _():
            prv = pl.ds(pl.multiple_of((n - 1) * BLK, BLK), BLK)
            for kh in range(KV_HEADS):
                cs = slice(kh * HEAD, (kh + 1) * HEAD)
                dk_ref[prv, cs] += dkcats[kh][:BLK]
                dv_ref[prv, cs] += dvcats[kh][:BLK]

    full_kv = pl.BlockSpec((T, KV_W), lambda n: (0, 0))
    return pl.pallas_call(
        body, grid=(NB,),
        in_specs=_attn_specs() + [pl.BlockSpec((BLK, D_MODEL), lambda n: (n, 0))],
        out_specs=[pl.BlockSpec((BLK, D_MODEL), lambda n: (n, 0)), full_kv, full_kv,
                   pl.BlockSpec((8, 128), lambda n: (0, 0))],
        out_shape=[SDS((T, D_MODEL), F32), SDS((T, KV_W), F32), SDS((T, KV_W), F32), SDS((8, 128), F32)],
        name="attn_bwd")(proj, proj, proj, proj, proj, proj, proj, sinks, do)


CONV_CB = 512


def _conv_taps(xp, w, rows):
    return (w[0:1] * xp[5:5 + rows] + w[1:2] * xp[6:6 + rows] + w[2:3] * xp[7:7 + rows] + w[3:4] * xp[8:8 + rows])


def _conv_fwd(proj, conv_w, conv_b):
    ncb = CONV_DIM // CONV_CB
    cb0 = C_XBC // CONV_CB

    def body(tail_ref, cur_ref, w_ref, b_ref, o_ref):
        n = pl.program_id(1)
        tail = jnp.where(n > 0, tail_ref[...], 0.0)
        xp = jnp.concatenate([tail, cur_ref[...]], axis=0)
        conv = _conv_taps(xp, w_ref[...], BLK) + b_ref[...]
        row = n * BLK + lax.broadcasted_iota(jnp.int32, (BLK, 1), 0)
        o_ref[...] = jnp.where(row >= PAD, _silu(conv), 0.0)

    return pl.pallas_call(
        body, grid=(ncb, NB),
        in_specs=[pl.BlockSpec((8, CONV_CB), lambda j, n: (jnp.maximum(n * (BLK // 8) - 1, 0), cb0 + j)),
                  pl.BlockSpec((BLK, CONV_CB), lambda j, n: (n, cb0 + j)),
                  pl.BlockSpec((8, CONV_CB), lambda j, n: (0, j)),
                  pl.BlockSpec((1, CONV_CB), lambda j, n: (0, j))],
        out_specs=pl.BlockSpec((BLK, CONV_CB), lambda j, n: (n, j)),
        out_shape=SDS((T, CONV_DIM), F32), name="conv_fwd")(proj, proj, conv_w, conv_b)


def _conv_bwd(proj, conv_w, conv_b, dact, ch0, name):
    width = dact.shape[1]
    ncb = width // CONV_CB
    cb0 = (C_XBC + ch0) // CONV_CB
    wb0 = ch0 // CONV_CB
    last8 = T // 8 - 1

    def body(tail_ref, cur_ref, nxt_ref, w_ref, b_ref, dcur_ref, dnxt_ref, dx_ref, dw_ref, db_ref):
        n = pl.program_id(1)
        w = w_ref[...]
        tail = jnp.where(n > 0, tail_ref[...], 0.0)
        xp = jnp.concatenate([tail, cur_ref[...], nxt_ref[...]], axis=0)
        conv = _conv_taps(xp, w, BLK + 8) + b_ref[...]
        dext = jnp.concatenate([dcur_ref[...], jnp.where(n < NB - 1, dnxt_ref[...], 0.0)], axis=0)
        row = n * BLK + lax.broadcasted_iota(jnp.int32, (BLK + 8, 1), 0)
        dconv = jnp.where(row >= PAD, dext * _dsilu(conv), 0.0)
        dx = (w[0:1] * dconv[3:3 + BLK] + w[1:2] * dconv[2:2 + BLK] + w[2:3] * dconv[1:1 + BLK]
              + w[3:4] * dconv[0:BLK])
        dx_ref[...] = dx.astype(BF16)
        dc = dconv[0:BLK]
        dws = [jnp.sum(dc * xp[5 + k:5 + k + BLK], axis=0, keepdims=True) for k in range(4)]
        dwp = jnp.concatenate(dws + [jnp.zeros((4, CONV_CB), F32)], axis=0)
        dbp = jnp.sum(dc, axis=0, keepdims=True)

        @pl.when(n == 0)
        def _():
            dw_ref[...] = dwp
            db_ref[...] = jnp.concatenate([dbp, jnp.zeros((7, CONV_CB), F32)], axis=0)

        @pl.when(n > 0)
        def _():
            dw_ref[...] += dwp
            db_ref[0:1, :] += dbp

    return pl.pallas_call(
        body, grid=(ncb, NB),
        in_specs=[pl.BlockSpec((8, CONV_CB), lambda j, n: (jnp.maximum(n * (BLK // 8) - 1, 0), cb0 + j)),
                  pl.BlockSpec((BLK, CONV_CB), lambda j, n: (n, cb0 + j)),
                  pl.BlockSpec((8, CONV_CB), lambda j, n: (jnp.minimum((n + 1) * (BLK // 8), last8), cb0 + j)),
                  pl.BlockSpec((8, CONV_CB), lambda j, n: (0, wb0 + j)),
                  pl.BlockSpec((1, CONV_CB), lambda j, n: (0, wb0 + j)),
                  pl.BlockSpec((BLK, CONV_CB), lambda j, n: (n, j)),
                  pl.BlockSpec((8, CONV_CB), lambda j, n: (jnp.minimum((n + 1) * (BLK // 8), last8), j))],
        out_specs=[pl.BlockSpec((BLK, CONV_CB), lambda j, n: (n, j)),
                   pl.BlockSpec((8, CONV_CB), lambda j, n: (0, j)),
                   pl.BlockSpec((8, CONV_CB), lambda j, n: (0, j))],
        out_shape=[SDS((T, width), BF16), SDS((8, width), F32), SDS((8, width), F32)],
        name=name)(proj, proj, proj, conv_w, conv_b, dact, dact)


HPG = SSM_HEADS // SSM_GROUPS


def _ssd_fn(xs, zs, bm, cm, dt_raw, s_prev, dt_bias, a_log, d_skip, gns, g):
    dt_all = jax.nn.softplus(dt_raw + dt_bias)
    a_all = dt_all * (-jnp.exp(a_log))
    li = lax.broadcasted_iota(jnp.int32, (BLK, BLK), 0)
    si = lax.broadcasted_iota(jnp.int32, (BLK, BLK), 1)
    tril = li >= si
    cs_all = jnp.dot(tril.astype(F32), a_all, precision=HI, preferred_element_type=F32)
    cs_t = cs_all.T
    rowi = lax.broadcasted_iota(jnp.int32, (BLK, 1), 0)
    gmat = lax.dot_general(cm.astype(BF16), bm.astype(BF16), (((1,), (1,)), ((), ())), preferred_element_type=F32)
    ys, s_next = [], []
    for j in range(HPG):
        h = g * HPG + j
        dtc = jnp.sum(jnp.where(si == h, dt_all, 0.0), axis=1, keepdims=True)
        csc = jnp.sum(jnp.where(si == h, cs_all, 0.0), axis=1, keepdims=True)
        csr = jnp.sum(jnp.where(li == h, cs_t, 0.0), axis=0, keepdims=True)
        cs_last = jnp.sum(jnp.where(rowi == BLK - 1, csc, 0.0), axis=0, keepdims=True)
        dsk = _lane_pick(d_skip, h)
        lam = jnp.exp(jnp.where(tril, csc - csr, NEG))
        xr = xs[j] * dtc
        y = jnp.dot((gmat * lam).astype(BF16), xr.astype(BF16), preferred_element_type=F32)
        y_off = lax.dot_general(cm.astype(BF16), s_prev[j].astype(BF16), (((1,), (1,)), ((), ())),
                                preferred_element_type=F32)
        y = y + y_off * jnp.exp(csc) + dsk * xs[j]
        st = lax.dot_general((xr * jnp.exp(cs_last - csc)).astype(BF16), bm.astype(BF16), (((0,), (0,)), ((), ())),
                             preferred_element_type=F32)
        s_next.append(s_prev[j] * jnp.exp(cs_last) + st)
        ys.append(y * _silu(zs[j]))
    ss = ys[0] * ys[0]
    for j in range(1, HPG):
        ss = ss + ys[j] * ys[j]
    rn = lax.rsqrt(jnp.sum(ss, axis=1, keepdims=True) / GRP_W + EPS)
    outs = [ys[j] * rn * gns[j] for j in range(HPG)]
    return outs, s_next


def _ssd_in_specs(rev):
    cidx = (lambda c: NB - 1 - c) if rev else (lambda c: c)
    return [
        pl.BlockSpec((BLK, GRP_W), lambda g, c: (cidx(c), g)),
        pl.BlockSpec((BLK, SSM_STATE), lambda g, c: (cidx(c), SSM_INNER // SSM_STATE + g)),
        pl.BlockSpec((BLK, SSM_STATE), lambda g, c: (cidx(c), SSM_INNER // SSM_STATE + SSM_GROUPS + g)),
        pl.BlockSpec((BLK, 128), lambda g, c: (cidx(c), C_DT // 128)),
        pl.BlockSpec((BLK, GRP_W), lambda g, c: (cidx(c), C_ZS // GRP_W + g)),
        pl.BlockSpec((1, 128), lambda g, c: (0, 0)),
        pl.BlockSpec((1, 128), lambda g, c: (0, 0)),
        pl.BlockSpec((1, 128), lambda g, c: (0, 0)),
        pl.BlockSpec((1, GRP_W), lambda g, c: (0, g)),
    ]


def _ssd_load(xs_ref, z_ref, gn_ref):
    xs = [xs_ref[:, j * HEAD:(j + 1) * HEAD] for j in range(HPG)]
    zs = [z_ref[:, j * HEAD:(j + 1) * HEAD] for j in range(HPG)]
    gns = [gn_ref[:, j * HEAD:(j + 1) * HEAD] for j in range(HPG)]
    return xs, zs, gns


def _ssd_fwd(xbc_act, proj, dt_bias, a_log, d_skip, g_norm):
    def body(xs_ref, b_ref, c_ref, dt_ref, z_ref, dtb_ref, al_ref, dsk_ref, gn_ref, y_ref, st_ref, s_scr):
        g = pl.program_id(0)
        c = pl.program_id(1)

        @pl.when(c == 0)
        def _():
            s_scr[...] = jnp.zeros_like(s_scr)

        st_ref[0, 0] = s_scr[...]
        xs, zs, gns = _ssd_load(xs_ref, z_ref, gn_ref)
        s_prev = [s_scr[j * HEAD:(j + 1) * HEAD, :] for j in range(HPG)]
        outs, s_next = _ssd_fn(xs, zs, b_ref[...], c_ref[...], dt_ref[...], s_prev, dtb_ref[...], al_ref[...],
                               dsk_ref[...], gns, g)
        for j in range(HPG):
            y_ref[:, j * HEAD:(j + 1) * HEAD] = outs[j].astype(BF16)
            s_scr[j * HEAD:(j + 1) * HEAD, :] = s_next[j]

    return pl.pallas_call(
        body, grid=(SSM_GROUPS, NB), in_specs=_ssd_in_specs(False),
        out_specs=[pl.BlockSpec((BLK, GRP_W), lambda g, c: (c, g)),
                   pl.BlockSpec((1, 1, GRP_W, SSM_STATE), lambda g, c: (g, c, 0, 0))],
        out_shape=[SDS((T, SSM_INNER), BF16), SDS((SSM_GROUPS, NB, GRP_W, SSM_STATE), F32)],
        scratch_shapes=[pltpu.VMEM((GRP_W, SSM_STATE), F32)],
        name="ssd_fwd")(xbc_act, xbc_act, xbc_act, proj, proj, dt_bias, a_log, d_skip, g_norm)


def _ssd_bwd(xbc_act, proj, dt_bias, a_log, d_skip, g_norm, states, dy):
    def body(xs_ref, b_ref, c_ref, dt_ref, z_ref, dtb_ref, al_ref, dsk_ref, gn_ref, st_ref, dy_ref,
             dxs_ref, db_ref, dc_ref, ddt_ref, dz_ref, ddtb_ref, dal_ref, ddsk_ref, dgn_ref, ds_scr):
        g = pl.program_id(0)
        c = pl.program_id(1)

        @pl.when(c == 0)
        def _():
            ds_scr[...] = jnp.zeros_like(ds_scr)
            dgn_ref[...] = jnp.zeros_like(dgn_ref)

        @pl.when((c == 0) & (g == 0))
        def _():
            ddtb_ref[...] = jnp.zeros_like(ddtb_ref)
            dal_ref[...] = jnp.zeros_like(dal_ref)
            ddsk_ref[...] = jnp.zeros_like(ddsk_ref)

        xs, zs, gns = _ssd_load(xs_ref, z_ref, gn_ref)
        s_prev = [st_ref[0, 0, j * HEAD:(j + 1) * HEAD, :] for j in range(HPG)]
        fn = lambda a1, a2, a3, a4, a5, a6, a7, a8, a9, a10: _ssd_fn(a1, a2, a3, a4, a5, a6, a7, a8, a9, a10, g)
        _, vjp = jax.vjp(fn, xs, zs, b_ref[...], c_ref[...], dt_ref[...], s_prev, dtb_ref[...], al_ref[...],
                         dsk_ref[...], gns)
        cot_y = [dy_ref[:, j * HEAD:(j + 1) * HEAD] for j in range(HPG)]
        cot_s = [ds_scr[j * HEAD:(j + 1) * HEAD, :] for j in range(HPG)]
        dxs, dzs, dbm, dcm, ddt, dsp, ddtb, dal, ddsk, dgns = vjp((cot_y, cot_s))
        for j in range(HPG):
            dxs_ref[:, j * HEAD:(j + 1) * HEAD] = dxs[j]
            dz_ref[:, j * HEAD:(j + 1) * HEAD] = dzs[j].astype(BF16)
            ds_scr[j * HEAD:(j + 1) * HEAD, :] = dsp[j]
            dgn_ref[0:1, j * HEAD:(j + 1) * HEAD] += dgns[j]
        db_ref[...] = dbm
        dc_ref[...] = dcm
        ddt_ref[...] = ddt
        ddtb_ref[0:1, :] += ddtb
        dal_ref[0:1, :] += dal
        ddsk_ref[0:1, :] += ddsk

    rc = lambda c: NB - 1 - c
    small = pl.BlockSpec((8, 128), lambda g, c: (0, 0))
    return pl.pallas_call(
        body, grid=(SSM_GROUPS, NB),
        in_specs=_ssd_in_specs(True) + [
            pl.BlockSpec((1, 1, GRP_W, SSM_STATE), lambda g, c: (g, rc(c), 0, 0)),
            pl.BlockSpec((BLK, GRP_W), lambda g, c: (rc(c), g))],
        out_specs=[pl.BlockSpec((BLK, GRP_W), lambda g, c: (rc(c), g)),
                   pl.BlockSpec((BLK, SSM_STATE), lambda g, c: (rc(c), g)),
                   pl.BlockSpec((BLK, SSM_STATE), lambda g, c: (rc(c), g)),
                   pl.BlockSpec((BLK, 128), lambda g, c: (rc(c), g)),
                   pl.BlockSpec((BLK, GRP_W), lambda g, c: (rc(c), g)),
                   small, small, small,
                   pl.BlockSpec((8, GRP_W), lambda g, c: (0, g))],
        out_shape=[SDS((T, SSM_INNER), F32), SDS((T, GRP_W), F32), SDS((T, GRP_W), F32), SDS((T, GRP_W), F32),
                   SDS((T, SSM_INNER), BF16), SDS((8, 128), F32), SDS((8, 128), F32), SDS((8, 128), F32),
                   SDS((8, SSM_INNER), F32)],
        scratch_shapes=[pltpu.VMEM((GRP_W, SSM_STATE), F32)],
        compiler_params=_cparams(),
        name="ssd_bwd")(xbc_act, xbc_act, xbc_act, proj, proj, dt_bias, a_log, d_skip, g_norm, states, dy)


def _post_a(o, proj, sn, w_att, w_ssm, w_o):
    def body(o_ref, za_ref, ga_ref, gs_ref, sn_ref, wa_ref, ws_ref, wo_ref, a_ref, mg_ref, ya_ref, ys_ref, out_ref):
        a = (o_ref[...] * _silu(za_ref[...])).astype(BF16)
        a_ref[...] = a
        ya = jnp.dot(a, wa_ref[...], preferred_element_type=F32)
        ys = jnp.dot(sn_ref[...], ws_ref[...], preferred_element_type=F32)
        ya_ref[...] = ya
        ys_ref[...] = ys
        mg = (jax.nn.sigmoid(ga_ref[...]) * ya + jax.nn.sigmoid(gs_ref[...]) * ys).astype(BF16)
        mg_ref[...] = mg
        out_ref[...] = jnp.dot(mg, wo_ref[...], preferred_element_type=F32)

    row = pl.BlockSpec((BLK, D_MODEL), lambda i: (i, 0))
    pcol = lambda c0: pl.BlockSpec((BLK, D_MODEL), lambda i: (i, c0 // D_MODEL))
    full = lambda r: pl.BlockSpec((r, D_MODEL), lambda i: (0, 0))
    return pl.pallas_call(
        body, grid=(NB,),
        in_specs=[row, pcol(C_ZA), pcol(C_GA), pcol(C_GS), pl.BlockSpec((BLK, SSM_INNER), lambda i: (i, 0)),
                  full(D_MODEL), full(SSM_INNER), full(D_MODEL)],
        out_specs=[row, row, row, row, row],
        out_shape=[SDS((T, D_MODEL), BF16), SDS((T, D_MODEL), BF16), SDS((T, D_MODEL), F32), SDS((T, D_MODEL), F32),
                   SDS((T, D_MODEL), F32)],
        compiler_params=_cparams(), name="post_a")(o, proj, proj, proj, sn, w_att, w_ssm, w_o)


def _post_b(out, h, tgt, proj, ya, ys, o, g_post, w_att, w_ssm, w_o):
    def body(out_ref, h_ref, t_ref, za_ref, ga_ref, gs_ref, ya_ref, ys_ref, o_ref, gp_ref, wa_ref, ws_ref, wo_ref,
             loss_ref, dres_ref, dout_ref, dya_ref, dys_ref, dga_ref, dgs_ref, do_ref, dza_ref, dsn_ref, dgp_ref):
        i = pl.program_id(0)
        x = out_ref[...]
        gp = gp_ref[...]
        r = lax.rsqrt(jnp.mean(x * x, axis=-1, keepdims=True) + EPS)
        row = i * BLK + lax.broadcasted_iota(jnp.int32, (BLK, 1), 0)
        res = h_ref[...] + jnp.where(row >= PAD, x * r * gp, 0.0)
        live = row >= PAD + N_META
        err = jnp.where(live, res - t_ref[...], 0.0)
        lpart = 0.5 * jnp.sum(jnp.sum(err * err, axis=1, keepdims=True) / D_MODEL, axis=0, keepdims=True)
        dres = err / D_MODEL
        dres_ref[...] = dres
        gpart = jnp.sum(dres * x * r, axis=0, keepdims=True)

        @pl.when(i == 0)
        def _():
            loss_ref[...] = jnp.zeros_like(loss_ref)
            dgp_ref[...] = jnp.zeros_like(dgp_ref)

        loss_ref[...] += jnp.broadcast_to(lpart, loss_ref.shape)
        dgp_ref[0:1, :] += gpart
        gd = gp * dres
        dout = (r * gd - x * (r * r * r) * jnp.mean(x * gd, axis=-1, keepdims=True)).astype(BF16)
        dout_ref[...] = dout
        dmg = lax.dot_general(dout, wo_ref[...], (((1,), (1,)), ((), ())), preferred_element_type=F32)
        sga = jax.nn.sigmoid(ga_ref[...])
        sgs = jax.nn.sigmoid(gs_ref[...])
        dya = (dmg * sga).astype(BF16)
        dys = (dmg * sgs).astype(BF16)
        dya_ref[...] = dya
        dys_ref[...] = dys
        dga_ref[...] = (dmg * ya_ref[...] * sga * (1.0 - sga)).astype(BF16)
        dgs_ref[...] = (dmg * ys_ref[...] * sgs * (1.0 - sgs)).astype(BF16)
        da = lax.dot_general(dya, wa_ref[...], (((1,), (1,)), ((), ())), preferred_element_type=F32)
        za = za_ref[...]
        do_ref[...] = da * _silu(za)
        dza_ref[...] = (da * o_ref[...] * _dsilu(za)).astype(BF16)
        dsn_ref[...] = lax.dot_general(dys, ws_ref[...], (((1,), (1,)), ((), ())), preferred_element_type=F32)

    row = pl.BlockSpec((BLK, D_MODEL), lambda i: (i, 0))
    pcol = lambda c0: pl.BlockSpec((BLK, D_MODEL), lambda i: (i, c0 // D_MODEL))
    full = lambda r: pl.BlockSpec((r, D_MODEL), lambda i: (0, 0))
    small = pl.BlockSpec((8, D_MODEL), lambda i: (0, 0))
    return pl.pallas_call(
        body, grid=(NB,),
        in_specs=[row, row, row, pcol(C_ZA), pcol(C_GA), pcol(C_GS), row, row, row,
                  pl.BlockSpec((1, D_MODEL), lambda i: (0, 0)), full(D_MODEL), full(SSM_INNER), full(D_MODEL)],
        out_specs=[pl.BlockSpec((8, 128), lambda i: (0, 0)), row, row, row, row, row, row, row, row,
                   pl.BlockSpec((BLK, SSM_INNER), lambda i: (i, 0)), small],
        out_shape=[SDS((8, 128), F32), SDS((T, D_MODEL), F32), SDS((T, D_MODEL), BF16), SDS((T, D_MODEL), BF16),
                   SDS((T, D_MODEL), BF16), SDS((T, D_MODEL), BF16), SDS((T, D_MODEL), BF16), SDS((T, D_MODEL), F32),
                   SDS((T, D_MODEL), BF16), SDS((T, SSM_INNER), F32), SDS((8, D_MODEL), F32)],
        compiler_params=_cparams(), name="post_b")(out, h, tgt, proj, proj, proj, ya, ys, o, g_post, w_att, w_ssm, w_o)


def _assemble(dq, dza, dga, dgs, dzs, dxx, dxb, dxc, dk, dv, ddt4):
    def body(dq_ref, dza_ref, dga_ref, dgs_ref, dzs_ref, dxx_ref, dxb_ref, dxc_ref, dk_ref, dv_ref, ddt_ref, o_ref):
        o_ref[:, C_Q:C_Q + D_MODEL] = dq_ref[...].astype(BF16)
        o_ref[:, C_ZA:C_ZA + D_MODEL] = dza_ref[...]
        o_ref[:, C_GA:C_GA + D_MODEL] = dga_ref[...]
        o_ref[:, C_GS:C_GS + D_MODEL] = dgs_ref[...]
        o_ref[:, C_ZS:C_ZS + SSM_INNER] = dzs_ref[...]
        o_ref[:, C_XBC:C_XBC + SSM_INNER] = dxx_ref[...]
        o_ref[:, C_XBC + SSM_INNER:C_XBC + SSM_INNER + GRP_W] = dxb_ref[...]
        o_ref[:, C_XBC + SSM_INNER + GRP_W:C_XBC + CONV_DIM] = dxc_ref[...]
        o_ref[:, C_K:C_K + KV_W] = dk_ref[...].astype(BF16)
        o_ref[:, C_V:C_V + KV_W] = dv_ref[...].astype(BF16)
        d4 = ddt_ref[...]
        o_ref[:, C_DT:C_DT + 128] = (d4[:, 0:128] + d4[:, 128:256] + d4[:, 256:384] + d4[:, 384:512]).astype(BF16)

    spec = lambda w: pl.BlockSpec((BLK, w), lambda i: (i, 0))
    ins = [dq, dza, dga, dgs, dzs, dxx, dxb, dxc, dk, dv, ddt4]
    return pl.pallas_call(
        body, grid=(NB,), in_specs=[spec(a.shape[1]) for a in ins], out_specs=spec(PW),
        out_shape=SDS((T, PW), BF16), name="assemble")(*ins)


def _adamw_math(w, g, m, v):
    m = ADAM_B1 * m + (1.0 - ADAM_B1) * g
    v = ADAM_B2 * v + (1.0 - ADAM_B2) * (g * g)
    m_hat = m / (1.0 - ADAM_B1 ** ADAM_STEP)
    v_hat = v / (1.0 - ADAM_B2 ** ADAM_STEP)
    delta = -ADAM_LR * (m_hat / (jnp.sqrt(v_hat) + ADAM_EPS) + ADAM_WD * w)
    return delta, m, v


def _sum_adamw(recv, row0, w, m, v, tr, name):
    rows, cols = w.shape
    assert rows % tr == 0 and row0 % tr == 0
    rb0 = row0 // tr

    def body(r_ref, w_ref, m_ref, v_ref, g_ref, d_ref, nm_ref, nv_ref):
        g = r_ref[0].astype(F32)
        for d in range(1, N_DEV):
            g = g + r_ref[d].astype(F32)
        g_ref[...] = g
        delta, nm, nv = _adamw_math(w_ref[...], g, m_ref[...], v_ref[...])
        d_ref[...] = delta
        nm_ref[...] = nm
        nv_ref[...] = nv

    blk = pl.BlockSpec((tr, cols), lambda i: (i, 0))
    return pl.pallas_call(
        body, grid=(rows // tr,),
        in_specs=[pl.BlockSpec((N_DEV, tr, cols), lambda i: (0, rb0 + i, 0)), blk, blk, blk],
        out_specs=[blk, blk, blk, blk], out_shape=[SDS((rows, cols), F32)] * 4,
        compiler_params=_cparams(), name=name)(recv, w, m, v)


def _small_sum_adamw(recv, w, m, v):
    def body(r_ref, w_ref, m_ref, v_ref, g_ref, d_ref, nm_ref, nv_ref):
        g = r_ref[0]
        for d in range(1, N_DEV):
            g = g + r_ref[d]
        g_ref[...] = g
        delta, nm, nv = _adamw_math(w_ref[...], g[0:REP_ROWS], m_ref[...], v_ref[...])
        d_ref[...] = delta
        nm_ref[...] = nm
        nv_ref[...] = nv

    return pl.pallas_call(
        body, out_shape=[SDS((SM_ROWS, 1024), F32)] + [SDS((REP_ROWS, 1024), F32)] * 3,
        name="small_sum_adamw")(recv, w, m, v)


def _adamw(w, g, m, v, name):
    def body(w_ref, g_ref, m_ref, v_ref, d_ref, nm_ref, nv_ref):
        delta, nm, nv = _adamw_math(w_ref[...], g_ref[...], m_ref[...], v_ref[...])
        d_ref[...] = delta
        nm_ref[...] = nm
        nv_ref[...] = nv

    return pl.pallas_call(body, out_shape=[SDS(w.shape, F32)] * 3, name=name)(w, g, m, v)


def _slab(ref, px, py, pc):
    return ref.at[4 * px + 2 * py + pc]


def _all_gather(shards):
    na = len(shards)

    def body(*refs):
        ins, outs = refs[:na], refs[na:2 * na]
        send_sems, recv_sems, local_sems = refs[2 * na:]
        x, y, c = lax.axis_index("x"), lax.axis_index("y"), lax.axis_index("c")
        me, sibling = (x, y, c), (x, y, 1 - c)
        chips = [(1 - x, y), (x, 1 - y), (1 - x, 1 - y)]

        def copy(a, k, block, to, src=None):
            dst = _slab(outs[a], *block)
            return pltpu.make_async_remote_copy(
                src_ref=dst if src is None else src, dst_ref=dst, send_sem=send_sems.at[a, k],
                recv_sem=recv_sems.at[a, k], device_id=to, device_id_type=MESH)

        mine = [pltpu.make_async_copy(ins[a], _slab(outs[a], *me), local_sems.at[a]) for a in range(na)]
        for cp in mine:
            cp.start()
        first = []
        for a in range(na):
            first.append(copy(a, 0, me, sibling, src=ins[a]))
            first += [copy(a, 1 + j, me, (*chip, c), src=ins[a]) for j, chip in enumerate(chips)]
        for cp in first:
            cp.start()
        passed = []
        for j, chip in enumerate(chips):
            for a in range(na):
                copy(a, 1 + j, (*chip, c), me).wait_recv()
                cp = copy(a, 4 + j, (*chip, c), sibling)
                cp.start()
                passed.append(cp)
        for a in range(na):
            copy(a, 0, sibling, me).wait_recv()
            for j, chip in enumerate(chips):
                copy(a, 4 + j, (*chip, 1 - c), me).wait_recv()
        for cp in first + passed:
            cp.wait_send()
        for cp in mine:
            cp.wait()

    return pl.pallas_call(
        body, in_specs=[ANY] * na, out_specs=[ANY] * na,
        out_shape=[SDS((N_DEV,) + s.shape, s.dtype) for s in shards],
        scratch_shapes=[pltpu.SemaphoreType.DMA((na, 7)), pltpu.SemaphoreType.DMA((na, 7)),
                        pltpu.SemaphoreType.DMA((na,))],
        name="all_gather")(*shards)


def _exchange(parts):
    na = len(parts)

    def body(*refs):
        ins, outs = refs[:na], refs[na:2 * na]
        send_sems, recv_sems, local_sems = refs[2 * na:]
        x, y, c = lax.axis_index("x"), lax.axis_index("y"), lax.axis_index("c")
        me = (x, y, c)
        mine = [pltpu.make_async_copy(_slab(ins[a], *me), _slab(outs[a], *me), local_sems.at[a]) for a in range(na)]
        for cp in mine:
            cp.start()
        peers = []
        for k in range(1, N_DEV):
            dx, dy, dc = (k >> 2) & 1, (k >> 1) & 1, k & 1
            peers.append(((1 - x) if dx else x, (1 - y) if dy else y, (1 - c) if dc else c))
        sent = []
        for a in range(na):
            for k, peer in enumerate(peers):
                cp = pltpu.make_async_remote_copy(
                    src_ref=_slab(ins[a], *peer), dst_ref=_slab(outs[a], *me), send_sem=send_sems.at[a, k],
                    recv_sem=recv_sems.at[a, k], device_id=peer, device_id_type=MESH)
                cp.start()
                sent.append(cp)
        for a in range(na):
            for k, peer in enumerate(peers):
                pltpu.make_async_remote_copy(
                    src_ref=_slab(ins[a], *peer), dst_ref=_slab(outs[a], *peer), send_sem=send_sems.at[a, k],
                    recv_sem=recv_sems.at[a, k], device_id=peer, device_id_type=MESH).wait_recv()
        for cp in sent:
            cp.wait_send()
        for cp in mine:
            cp.wait()

    return pl.pallas_call(
        body, in_specs=[ANY] * na, out_specs=[ANY] * na,
        out_shape=[SDS(p.shape, p.dtype) for p in parts],
        scratch_shapes=[pltpu.SemaphoreType.DMA((na, 7)), pltpu.SemaphoreType.DMA((na, 7)),
                        pltpu.SemaphoreType.DMA((na,))],
        name="exchange")(*parts)


def _cast_shards(w_in, w_att, w_ssm, w_o):
    def body(wi_ref, wa_ref, ws_ref, wo_ref, a_ref, b_ref):
        a_ref[...] = wi_ref[...].astype(BF16)
        b_ref[0:128, :] = wa_ref[...].astype(BF16)
        b_ref[128:384, :] = ws_ref[...].astype(BF16)
        b_ref[384:512, :] = wo_ref[...].astype(BF16)

    return pl.pallas_call(
        body, out_shape=[SDS((D_MODEL, SHARD_IN), BF16), SDS((512, D_MODEL), BF16)],
        compiler_params=_cparams(), name="cast_shards")(w_in, w_att, w_ssm, w_o)


_SEGS = [
    (R_Q, C_Q, 1024), (R_K, C_K, 256), (R_V, C_V, 256), (R_ZA, C_ZA, 1024), (R_ZS, C_ZS, 2048),
    (R_XBC, C_XBC, 3072), (R_DT, C_DT, 32), (R_GA, C_GA, 1024), (R_GS, C_GS, 1024)]


def _to_aligned(wf):
    by_al = sorted(_SEGS, key=lambda s: s[1])
    cols = []
    for r0, c0, w in by_al:
        cols.append(wf[:, r0:r0 + w])
        if w == 32:
            cols.append(jnp.zeros((wf.shape[0], 96), wf.dtype))
    return jnp.concatenate(cols, axis=1)


def _from_aligned(wa):
    return jnp.concatenate([wa[:, c0:c0 + w] for _, c0, w in _SEGS], axis=1)


def _pad_lanes(v, n=128):
    return jnp.pad(v, ((0, 0), (0, n - v.shape[1])))


ROW_GPRE, ROW_CONVB, ROW_DTB, ROW_ALOG, ROW_DSKIP, ROW_SINK, ROW_GSSM, ROW_GPOST = 0, 1, 4, 5, 6, 7, 8, 10
REP_ROWS = 16
ROW_CONVW, ROW_META, SM_ROWS = 16, 28, 48


def _row1024(v):
    n = v.shape[1]
    rows = -(-n // 1024)
    return jnp.pad(v, ((0, 0), (0, rows * 1024 - n))).reshape(rows, 1024)


def _pack_rep(g_pre, conv_b, dt_bias, a_log, d_skip, sinks, g_ssm, g_post):
    rows = [_row1024(g_pre), _row1024(conv_b), _row1024(dt_bias), _row1024(a_log), _row1024(d_skip),
            _row1024(sinks), _row1024(g_ssm), _row1024(g_post), jnp.zeros((REP_ROWS - 11, 1024), F32)]
    return jnp.concatenate(rows, axis=0)


def _unpack_rep(p):
    return dict(
        g_pre=p[ROW_GPRE:ROW_GPRE + 1], conv_b=p[ROW_CONVB:ROW_CONVB + 3].reshape(1, CONV_DIM),
        dt_bias=p[ROW_DTB:ROW_DTB + 1, :SSM_HEADS], a_log=p[ROW_ALOG:ROW_ALOG + 1, :SSM_HEADS],
        d_skip=p[ROW_DSKIP:ROW_DSKIP + 1, :SSM_HEADS], attn_sinks=p[ROW_SINK:ROW_SINK + 1, :Q_HEADS],
        g_ssm_norm=p[ROW_GSSM:ROW_GSSM + 2].reshape(1, SSM_INNER), g_post=p[ROW_GPOST:ROW_GPOST + 1])


def _local_step(h, tgt, w_al, w_att, w_ssm, w_o, g_pre, conv_w8, conv_b, dt_bias, a_log, d_skip, sinks,
                g_ssm, g_post):
    dtb, al, dsk, snk = _pad_lanes(dt_bias), _pad_lanes(a_log), _pad_lanes(d_skip), _pad_lanes(sinks)
    u = _norm_u(h, g_pre)
    proj = _matmul(u, w_al, "nn", F32, 1088, 896, D_MODEL, "in_proj")
    o = _attn_fwd(proj, snk)
    xbc_act = _conv_fwd(proj, conv_w8, conv_b)
    sn, states = _ssd_fwd(xbc_act, proj, dtb, al, dsk, g_ssm)
    a_in, mg, ya, ys, out = _post_a(o, proj, sn, w_att, w_ssm, w_o)
    (loss, dres, dout, dya, dys, dga, dgs, do, dza, dsn, dgp) = _post_b(
        out, h, tgt, proj, ya, ys, o, g_post, w_att, w_ssm, w_o)
    dxs, dbm, dcm, ddt4, dzs, ddtb, dal, ddsk, dgn = _ssd_bwd(xbc_act, proj, dtb, al, dsk, g_ssm, states, dsn)
    dxx, dwx, dbx = _conv_bwd(proj, conv_w8, conv_b, dxs, 0, "conv_bwd_x")
    dxb, dwb, dbb = _conv_bwd(proj, conv_w8, conv_b, dbm, SSM_INNER, "conv_bwd_b")
    dxc, dwc, dbc = _conv_bwd(proj, conv_w8, conv_b, dcm, SSM_INNER + GRP_W, "conv_bwd_c")
    dq, dk, dv, dsink = _attn_bwd(proj, snk, do)
    dproj = _assemble(dq, dza, dga, dgs, dzs, dxx, dxb, dxc, dk, dv, ddt4)
    du = _matmul(dproj, w_al, "nt", F32, 1088, D_MODEL, 896, "d_u")
    dw_al = _matmul(u, dproj, "tn", BF16, D_MODEL, 896, T, "d_w_in")
    dh, dgpre = _norm_bwd(h, g_pre, du, dres)
    dw_att = _matmul(a_in, dya, "tn", BF16, D_MODEL, D_MODEL, T, "d_w_att")
    dw_ssm = _matmul(sn, dys, "tn", BF16, D_MODEL, D_MODEL, T, "d_w_ssm")
    dw_o = _matmul(mg, dout, "tn", BF16, D_MODEL, D_MODEL, T, "d_w_o")
    return dict(
        loss=loss[0, 0], dh=dh, dw_al=dw_al, dw_att=dw_att, dw_ssm=dw_ssm, dw_o=dw_o,
        g_pre=dgpre[0:1], conv_w=jnp.concatenate([dwx[0:4], dwb[0:4], dwc[0:4]], axis=1),
        conv_b=jnp.concatenate([dbx[0:1], dbb[0:1], dbc[0:1]], axis=1),
        dt_bias=ddtb[0:1, :SSM_HEADS], a_log=dal[0:1, :SSM_HEADS], d_skip=ddsk[0:1, :SSM_HEADS],
        attn_sinks=dsink[0:1, :Q_HEADS], g_ssm_norm=dgn[0:1], g_post=dgp[0:1])


def kernel(x, meta_tokens, g_pre, w_in, conv_w, conv_b, dt_bias, a_log, d_skip, attn_sinks, g_ssm_norm, w_out_att, w_out_ssm, w_out, g_post, loss_target, m_meta_tokens, m_g_pre, m_w_in, m_conv_w, m_conv_b, m_dt_bias, m_a_log, m_d_skip, m_attn_sinks, m_g_ssm_norm, m_w_out_att, m_w_out_ssm, m_w_out, m_g_post, v_meta_tokens, v_g_pre, v_w_in, v_conv_w, v_conv_b, v_dt_bias, v_a_log, v_d_skip, v_attn_sinks, v_g_ssm_norm, v_w_out_att, v_w_out_ssm, v_w_out, v_g_post):
    me = 4 * lax.axis_index("x") + 2 * lax.axis_index("y") + lax.axis_index("c")

    a_sh, b_sh = _cast_shards(w_in[0], w_out_att[0], w_out_ssm[0], w_out[0])
    cw_sh = jnp.pad(conv_w[0], ((0, 4), (0, 0)))
    a_all, b_all, meta_all, cw_all = _all_gather([a_sh, b_sh, meta_tokens, cw_sh])
    w_al = _to_aligned(a_all.transpose(1, 0, 2).reshape(D_MODEL, IN_PROJ))
    w_att = b_all[:, 0:128].reshape(D_MODEL, D_MODEL)
    w_ssm = b_all[:, 128:384].reshape(SSM_INNER, D_MODEL)
    w_o = b_all[:, 384:512].reshape(D_MODEL, D_MODEL)
    meta_full = meta_all.transpose(1, 0, 2).reshape(N_META, D_MODEL)
    conv_w8 = cw_all.transpose(1, 0, 2).reshape(8, CONV_DIM)

    h = jnp.concatenate([jnp.zeros((PAD, D_MODEL), F32), meta_full, x[0]], axis=0)
    tgt = jnp.concatenate([jnp.zeros((PAD + N_META, D_MODEL), F32), loss_target[0]], axis=0)
    r = _local_step(h, tgt, w_al, w_att, w_ssm, w_o, g_pre, conv_w8, conv_b, dt_bias, a_log, d_skip, attn_sinks,
                    g_ssm_norm, g_post)
    loss = lax.psum(r["loss"], ("x", "y", "c"))
    grad_x = r["dh"][PAD + N_META:][None]

    ga = _from_aligned(r["dw_al"]).reshape(D_MODEL, N_DEV, SHARD_IN).transpose(1, 0, 2)
    gb = jnp.concatenate([r["dw_att"].reshape(N_DEV, 128, D_MODEL), r["dw_ssm"].reshape(N_DEV, 256, D_MODEL),
                          r["dw_o"].reshape(N_DEV, 128, D_MODEL)], axis=1)
    small = jnp.concatenate([
        _pack_rep(r["g_pre"], r["conv_b"], r["dt_bias"], r["a_log"], r["d_skip"], r["attn_sinks"],
                  r["g_ssm_norm"], r["g_post"]),
        r["conv_w"].reshape(12, 1024), r["dh"][PAD:PAD + N_META],
        jnp.zeros((SM_ROWS - ROW_META - N_META, 1024), F32)], axis=0)
    small8 = jnp.broadcast_to(small[None], (N_DEV, SM_ROWS, 1024))
    ra, rb, rs = _exchange([ga, gb, small8])

    g_w_in, d_w_in, nm_w_in, nv_w_in = _sum_adamw(ra, 0, w_in[0], m_w_in[0], v_w_in[0], 128, "adamw_w_in")
    g_w_att, d_w_att, nm_w_att, nv_w_att = _sum_adamw(rb, 0, w_out_att[0], m_w_out_att[0], v_w_out_att[0], 128,
                                                      "adamw_w_att")
    g_w_ssm, d_w_ssm, nm_w_ssm, nv_w_ssm = _sum_adamw(rb, 128, w_out_ssm[0], m_w_out_ssm[0], v_w_out_ssm[0], 128,
                                                      "adamw_w_ssm")
    g_w_o, d_w_o, nm_w_o, nv_w_o = _sum_adamw(rb, 384, w_out[0], m_w_out[0], v_w_out[0], 128, "adamw_w_o")
    wp = _pack_rep(g_pre, conv_b, dt_bias, a_log, d_skip, attn_sinks, g_ssm_norm, g_post)
    mp = _pack_rep(m_g_pre, m_conv_b, m_dt_bias, m_a_log, m_d_skip, m_attn_sinks, m_g_ssm_norm, m_g_post)
    vp = _pack_rep(v_g_pre, v_conv_b, v_dt_bias, v_a_log, v_d_skip, v_attn_sinks, v_g_ssm_norm, v_g_post)
    gsum, dp, nmp, nvp = _small_sum_adamw(rs, wp, mp, vp)
    g_rep, d_rep, nm_rep, nv_rep = _unpack_rep(gsum), _unpack_rep(dp), _unpack_rep(nmp), _unpack_rep(nvp)
    g_cw = lax.dynamic_slice_in_dim(gsum[ROW_CONVW:ROW_CONVW + 12].reshape(4, CONV_DIM), me * 384, 384, axis=1)
    g_meta = lax.dynamic_slice_in_dim(gsum[ROW_META:ROW_META + N_META], me * 128, 128, axis=1)
    pad8 = lambda t: jnp.pad(t, ((0, 4), (0, 0)))
    d_cw, nm_cw, nv_cw = _adamw(pad8(conv_w[0]), pad8(g_cw), pad8(m_conv_w[0]), pad8(v_conv_w[0]), "adamw_conv_w")
    d_meta, nm_meta, nv_meta = _adamw(meta_tokens, g_meta, m_meta_tokens, v_meta_tokens, "adamw_meta")

    def outs(meta, rep, cw, win, watt, wssm, wo):
        return (meta, rep["g_pre"], win[None], cw[None], rep["conv_b"], rep["dt_bias"], rep["a_log"],
                rep["d_skip"], rep["attn_sinks"], rep["g_ssm_norm"], watt[None], wssm[None], wo[None], rep["g_post"])

    return (loss, grad_x,
            *outs(g_meta, g_rep, g_cw, g_w_in, g_w_att, g_w_ssm, g_w_o),
            *outs(d_meta, d_rep, d_cw[0:4], d_w_in, d_w_att, d_w_ssm, d_w_o),
            *outs(nm_meta, nm_rep, nm_cw[0:4], nm_w_in, nm_w_att, nm_w_ssm, nm_w_o),
            *outs(nv_meta, nv_rep, nv_cw[0:4], nv_w_in, nv_w_att, nv_w_ssm, nv_w_o))
```

```python
import functools
import math

import jax
import jax.numpy as jnp
from jax import lax
from jax.experimental import pallas as pl
from jax.experimental.pallas import tpu as pltpu

F32 = jnp.float32
BF16 = jnp.bfloat16
SDS = jax.ShapeDtypeStruct
HI = lax.Precision.HIGHEST
MESH = pl.DeviceIdType.MESH
ANY = pl.BlockSpec(memory_space=pl.ANY)

N_DEV = 8
D_MODEL = 1024
SEQ = 2048
N_META = 16
BLK = 128
PAD = 112
T = PAD + N_META + SEQ
NB = T // BLK
EPS = 1e-6
HEAD = 64
Q_HEADS = 16
KV_HEADS = 4
GROUP = 4
KV_W = 256
SSM_INNER = 2048
SSM_HEADS = 32
SSM_GROUPS = 4
GRP_W = 512
SSM_STATE = 128
CONV_DIM = 3072
IN_PROJ = 9760
SHARD_IN = IN_PROJ // N_DEV
NEG = -1e30

C_Q, C_ZA, C_GA, C_GS, C_ZS, C_XBC, C_K, C_V, C_DT = 0, 1024, 2048, 3072, 4096, 6144, 9216, 9472, 9728
PW = 9856
R_Q, R_K, R_V, R_ZA, R_ZS, R_XBC, R_DT, R_GA, R_GS = 0, 1024, 1280, 1536, 2560, 4608, 7680, 7712, 8736

ADAM_LR, ADAM_B1, ADAM_B2, ADAM_EPS, ADAM_WD, ADAM_STEP = 0.001, 0.9, 0.999, 1e-08, 0.01, 10

VMEM_LIMIT = 56 * 1024 * 1024


def _cparams():
    return pltpu.CompilerParams(vmem_limit_bytes=VMEM_LIMIT)


def _silu(x):
    return x * jax.nn.sigmoid(x)


def _dsilu(x):
    s = jax.nn.sigmoid(x)
    return s * (1.0 + x * (1.0 - s))


def _matmul(a, b, mode, out_dtype, tm, tn, tk, name):
    if mode == "nn":
        (m, k), n = a.shape, b.shape[1]
        a_spec = pl.BlockSpec((tm, tk), lambda i, j, kk: (i, kk))
        b_spec = pl.BlockSpec((tk, tn), lambda i, j, kk: (kk, j))
        dims = (((1,), (0,)), ((), ()))
    elif mode == "nt":
        (m, k), n = a.shape, b.shape[0]
        a_spec = pl.BlockSpec((tm, tk), lambda i, j, kk: (i, kk))
        b_spec = pl.BlockSpec((tn, tk), lambda i, j, kk: (j, kk))
        dims = (((1,), (1,)), ((), ()))
    else:
        (k, m), n = a.shape, b.shape[1]
        a_spec = pl.BlockSpec((tk, tm), lambda i, j, kk: (kk, i))
        b_spec = pl.BlockSpec((tk, tn), lambda i, j, kk: (kk, j))
        dims = (((0,), (0,)), ((), ()))
    assert m % tm == 0 and n % tn == 0 and k % tk == 0, (a.shape, b.shape, tm, tn, tk)
    nk = k // tk

    def body(a_ref, b_ref, o_ref, acc_ref):
        kk = pl.program_id(2)
        part = lax.dot_general(a_ref[...], b_ref[...], dims, preferred_element_type=F32)

        @pl.when(kk == 0)
        def _():
            acc_ref[...] = part

        @pl.when(kk > 0)
        def _():
            acc_ref[...] += part

        @pl.when(kk == nk - 1)
        def _():
            o_ref[...] = acc_ref[...].astype(out_dtype)

    return pl.pallas_call(
        body, grid=(m // tm, n // tn, nk), in_specs=[a_spec, b_spec],
        out_specs=pl.BlockSpec((tm, tn), lambda i, j, kk: (i, j)),
        out_shape=SDS((m, n), out_dtype), scratch_shapes=[pltpu.VMEM((tm, tn), F32)],
        compiler_params=_cparams(), name=name)(a, b)


def _norm_u(h, g_pre):
    def body(h_ref, g_ref, u_ref):
        x = h_ref[...]
        r = lax.rsqrt(jnp.mean(x * x, axis=-1, keepdims=True) + EPS)
        u_ref[...] = (x * r * g_ref[...]).astype(BF16)

    return pl.pallas_call(
        body, grid=(NB,),
        in_specs=[pl.BlockSpec((BLK, D_MODEL), lambda i: (i, 0)), pl.BlockSpec((1, D_MODEL), lambda i: (0, 0))],
        out_specs=pl.BlockSpec((BLK, D_MODEL), lambda i: (i, 0)),
        out_shape=SDS((T, D_MODEL), BF16), name="norm_u")(h, g_pre)


def _norm_bwd(h, g_pre, du, dres):
    def body(h_ref, g_ref, du_ref, dres_ref, dh_ref, dg_ref):
        i = pl.program_id(0)
        x = h_ref[...]
        g = g_ref[...]
        du_ = du_ref[...]
        r = lax.rsqrt(jnp.mean(x * x, axis=-1, keepdims=True) + EPS)
        gd = g * du_
        dx = r * gd - x * (r * r * r) * jnp.mean(x * gd, axis=-1, keepdims=True)
        dh_ref[...] = dx + dres_ref[...]
        part = jnp.sum(du_ * x * r, axis=0, keepdims=True)

        @pl.when(i == 0)
        def _():
            dg_ref[...] = jnp.zeros_like(dg_ref)

        dg_ref[0:1, :] += part

    row = pl.BlockSpec((BLK, D_MODEL), lambda i: (i, 0))
    return pl.pallas_call(
        body, grid=(NB,),
        in_specs=[row, pl.BlockSpec((1, D_MODEL), lambda i: (0, 0)), row, row],
        out_specs=[row, pl.BlockSpec((8, D_MODEL), lambda i: (0, 0))],
        out_shape=[SDS((T, D_MODEL), F32), SDS((8, D_MODEL), F32)], name="norm_bwd")(h, g_pre, du, dres)


def _lane_pick(row, h):
    lane = lax.broadcasted_iota(jnp.int32, row.shape, 1)
    return jnp.sum(jnp.where(lane == h, row, 0.0), axis=1, keepdims=True)


def _attn_fn(q4s, kcats, vcats, kms, vms, sinks, n):
    r = lax.broadcasted_iota(jnp.int32, (GROUP * BLK, 2 * BLK), 0)
    s = lax.broadcasted_iota(jnp.int32, (GROUP * BLK, 2 * BLK), 1)
    i = jnp.bitwise_and(r, BLK - 1)
    gi = jnp.right_shift(r, 7)
    rel = i - s + BLK
    k_pos = n * BLK - BLK + s
    band_ok = (rel >= 0) & (rel < BLK) & (k_pos >= PAD + N_META)
    relf = rel.astype(F32)
    rm = lax.broadcasted_iota(jnp.int32, (GROUP * BLK, N_META), 0)
    mm = lax.broadcasted_iota(jnp.int32, (GROUP * BLK, N_META), 1)
    meta_ok = (PAD + mm) <= (n * BLK + jnp.bitwise_and(rm, BLK - 1))
    gcol = jnp.right_shift(lax.broadcasted_iota(jnp.int32, (GROUP * BLK, 1), 0), 7)
    outs = []
    for kh in range(KV_HEADS):
        slopes = [2.0 ** (-8.0 * (kh * GROUP + g + 1) / Q_HEADS) for g in range(GROUP)]
        slope = jnp.where(gi == 0, slopes[0], jnp.where(gi == 1, slopes[1], jnp.where(gi == 2, slopes[2], slopes[3])))
        sk = [_lane_pick(sinks, kh * GROUP + g) for g in range(GROUP)]
        sink = jnp.where(gcol == 0, sk[0], jnp.where(gcol == 1, sk[1], jnp.where(gcol == 2, sk[2], sk[3])))
        qb = (q4s[kh] * (HEAD ** -0.5)).astype(BF16)
        sb = lax.dot_general(qb, kcats[kh].astype(BF16), (((1,), (1,)), ((), ())), preferred_element_type=F32)
        sb = jnp.where(band_ok, sb - slope * relf, NEG)
        sm = lax.dot_general(qb, kms[kh].astype(BF16), (((1,), (1,)), ((), ())), preferred_element_type=F32)
        sm = jnp.where(meta_ok, sm, NEG)
        mx = jnp.maximum(jnp.maximum(jnp.max(sb, axis=1, keepdims=True), jnp.max(sm, axis=1, keepdims=True)), sink)
        mx = lax.stop_gradient(mx)
        eb = jnp.exp(sb - mx)
        em = jnp.exp(sm - mx)
        es = jnp.exp(sink - mx)
        inv = 1.0 / (jnp.sum(eb, axis=1, keepdims=True) + jnp.sum(em, axis=1, keepdims=True) + es)
        pb = (eb * inv).astype(BF16)
        pm = (em * inv).astype(BF16)
        o4 = (jnp.dot(pm, vms[kh].astype(BF16), preferred_element_type=F32)
              + jnp.dot(pb, vcats[kh].astype(BF16), preferred_element_type=F32))
        outs.append(o4)
    return outs


def _attn_specs():
    prev = lambda n: jnp.maximum(n - 1, 0)
    return [
        pl.BlockSpec((BLK, D_MODEL), lambda n: (n, C_Q // D_MODEL)),
        pl.BlockSpec((BLK, KV_W), lambda n: (prev(n), C_K // KV_W)),
        pl.BlockSpec((BLK, KV_W), lambda n: (n, C_K // KV_W)),
        pl.BlockSpec((BLK, KV_W), lambda n: (prev(n), C_V // KV_W)),
        pl.BlockSpec((BLK, KV_W), lambda n: (n, C_V // KV_W)),
        pl.BlockSpec((N_META, KV_W), lambda n: (PAD // N_META, C_K // KV_W)),
        pl.BlockSpec((N_META, KV_W), lambda n: (PAD // N_META, C_V // KV_W)),
        pl.BlockSpec((1, 128), lambda n: (0, 0)),
    ]


def _attn_load(q_ref, kp_ref, kc_ref, vp_ref, vc_ref, km_ref, vm_ref):
    q4s, kcats, vcats, kms, vms = [], [], [], [], []
    for kh in range(KV_HEADS):
        q4s.append(jnp.concatenate(
            [q_ref[:, (kh * GROUP + g) * HEAD:(kh * GROUP + g + 1) * HEAD] for g in range(GROUP)], axis=0))
        cs = slice(kh * HEAD, (kh + 1) * HEAD)
        kcats.append(jnp.concatenate([kp_ref[:, cs], kc_ref[:, cs]], axis=0))
        vcats.append(jnp.concatenate([vp_ref[:, cs], vc_ref[:, cs]], axis=0))
        kms.append(km_ref[:, cs])
        vms.append(vm_ref[:, cs])
    return q4s, kcats, vcats, kms, vms


def _attn_fwd(proj, sinks):
    def body(q_ref, kp_ref, kc_ref, vp_ref, vc_ref, km_ref, vm_ref, s_ref, o_ref):
        n = pl.program_id(0)
        args = _attn_load(q_ref, kp_ref, kc_ref, vp_ref, vc_ref, km_ref, vm_ref)
        outs = _attn_fn(*args, s_ref[...], n)
        for kh in range(KV_HEADS):
            for g in range(GROUP):
                hh = kh * GROUP + g
                o_ref[:, hh * HEAD:(hh + 1) * HEAD] = outs[kh][g * BLK:(g + 1) * BLK]

    return pl.pallas_call(
        body, grid=(NB,), in_specs=_attn_specs(),
        out_specs=pl.BlockSpec((BLK, D_MODEL), lambda n: (n, 0)),
        out_shape=SDS((T, D_MODEL), F32), name="attn_fwd")(proj, proj, proj, proj, proj, proj, proj, sinks)


def _attn_bwd(proj, sinks, do):
    def body(q_ref, kp_ref, kc_ref, vp_ref, vc_ref, km_ref, vm_ref, s_ref, do_ref, dq_ref, dk_ref, dv_ref, ds_ref):
        n = pl.program_id(0)

        @pl.when(n == 0)
        def _():
            dk_ref[...] = jnp.zeros_like(dk_ref)
            dv_ref[...] = jnp.zeros_like(dv_ref)
            ds_ref[...] = jnp.zeros_like(ds_ref)

        args = _attn_load(q_ref, kp_ref, kc_ref, vp_ref, vc_ref, km_ref, vm_ref)
        _, vjp = jax.vjp(lambda a, b, c, d, e, f: _attn_fn(a, b, c, d, e, f, n), *args, s_ref[...])
        cot = [jnp.concatenate([do_ref[:, (kh * GROUP + g) * HEAD:(kh * GROUP + g + 1) * HEAD] for g in range(GROUP)],
                               axis=0) for kh in range(KV_HEADS)]
        dq4s, dkcats, dvcats, dkms, dvms, dsk = vjp(cot)
        ds_ref[0:1, :] += dsk
        cur = pl.ds(pl.multiple_of(n * BLK, BLK), BLK)
        meta = slice(PAD, PAD + N_META)
        for kh in range(KV_HEADS):
            cs = slice(kh * HEAD, (kh + 1) * HEAD)
            for g in range(GROUP):
                hh = kh * GROUP + g
                dq_ref[:, hh * HEAD:(hh + 1) * HEAD] = dq4s[kh][g * BLK:(g + 1) * BLK]
            dk_ref[cur, cs] += dkcats[kh][BLK:]
            dv_ref[cur, cs] += dvcats[kh][BLK:]
            dk_ref[meta, cs] += dkms[kh]
            dv_ref[meta, cs] += dvms[kh]

        @pl.when(n > 0)
        def _():
            prv = pl.ds(pl.multiple_of((n - 1) * BLK, BLK), BLK)
            for kh in range(KV_HEADS):
                cs = slice(kh * HEAD, (kh + 1) * HEAD)
                dk_ref[prv, cs] += dkcats[kh][:BLK]
                dv_ref[prv, cs] += dvcats[kh][:BLK]

    full_kv = pl.BlockSpec((T, KV_W), lambda n: (0, 0))
    return pl.pallas_call(
        body, grid=(NB,),
        in_specs=_attn_specs() + [pl.BlockSpec((BLK, D_MODEL), lambda n: (n, 0))],
        out_specs=[pl.BlockSpec((BLK, D_MODEL), lambda n: (n, 0)), full_kv, full_kv,
                   pl.BlockSpec((8, 128), lambda n: (0, 0))],
        out_shape=[SDS((T, D_MODEL), F32), SDS((T, KV_W), F32), SDS((T, KV_W), F32), SDS((8, 128), F32)],
        name="attn_bwd")(proj, proj, proj, proj, proj, proj, proj, sinks, do)


CONV_CB = 512


def _conv_taps(xp, w, rows):
    return (w[0:1] * xp[5:5 + rows] + w[1:2] * xp[6:6 + rows] + w[2:3] * xp[7:7 + rows] + w[3:4] * xp[8:8 + rows])


def _conv_fwd(proj, conv_w, conv_b):
    ncb = CONV_DIM // CONV_CB
    cb0 = C_XBC // CONV_CB

    def body(tail_ref, cur_ref, w_ref, b_ref, o_ref):
        n = pl.program_id(1)
        tail = jnp.where(n > 0, tail_ref[...], 0.0)
        xp = jnp.concatenate([tail, cur_ref[...]], axis=0)
        conv = _conv_taps(xp, w_ref[...], BLK) + b_ref[...]
        row = n * BLK + lax.broadcasted_iota(jnp.int32, (BLK, 1), 0)
        o_ref[...] = jnp.where(row >= PAD, _silu(conv), 0.0)

    return pl.pallas_call(
        body, grid=(ncb, NB),
        in_specs=[pl.BlockSpec((8, CONV_CB), lambda j, n: (jnp.maximum(n * (BLK // 8) - 1, 0), cb0 + j)),
                  pl.BlockSpec((BLK, CONV_CB), lambda j, n: (n, cb0 + j)),
                  pl.BlockSpec((8, CONV_CB), lambda j, n: (0, j)),
                  pl.BlockSpec((1, CONV_CB), lambda j, n: (0, j))],
        out_specs=pl.BlockSpec((BLK, CONV_CB), lambda j, n: (n, j)),
        out_shape=SDS((T, CONV_DIM), F32), name="conv_fwd")(proj, proj, conv_w, conv_b)


def _conv_bwd(proj, conv_w, conv_b, dact, ch0, name):
    width = dact.shape[1]
    ncb = width // CONV_CB
    cb0 = (C_XBC + ch0) // CONV_CB
    wb0 = ch0 // CONV_CB
    last8 = T // 8 - 1

    def body(tail_ref, cur_ref, nxt_ref, w_ref, b_ref, dcur_ref, dnxt_ref, dx_ref, dw_ref, db_ref):
        n = pl.program_id(1)
        w = w_ref[...]
        tail = jnp.where(n > 0, tail_ref[...], 0.0)
        xp = jnp.concatenate([tail, cur_ref[...], nxt_ref[...]], axis=0)
        conv = _conv_taps(xp, w, BLK + 8) + b_ref[...]
        dext = jnp.concatenate([dcur_ref[...], jnp.where(n < NB - 1, dnxt_ref[...], 0.0)], axis=0)
        row = n * BLK + lax.broadcasted_iota(jnp.int32, (BLK + 8, 1), 0)
        dconv = jnp.where(row >= PAD, dext * _dsilu(conv), 0.0)
        dx = (w[0:1] * dconv[3:3 + BLK] + w[1:2] * dconv[2:2 + BLK] + w[2:3] * dconv[1:1 + BLK]
              + w[3:4] * dconv[0:BLK])
        dx_ref[...] = dx.astype(BF16)
        dc = dconv[0:BLK]
        dws = [jnp.sum(dc * xp[5 + k:5 + k + BLK], axis=0, keepdims=True) for k in range(4)]
        dwp = jnp.concatenate(dws + [jnp.zeros((4, CONV_CB), F32)], axis=0)
        dbp = jnp.sum(dc, axis=0, keepdims=True)

        @pl.when(n == 0)
        def _():
            dw_ref[...] = dwp
            db_ref[...] = jnp.concatenate([dbp, jnp.zeros((7, CONV_CB), F32)], axis=0)

        @pl.when(n > 0)
        def _():
            dw_ref[...] += dwp
            db_ref[0:1, :] += dbp

    return pl.pallas_call(
        body, grid=(ncb, NB),
        in_specs=[pl.BlockSpec((8, CONV_CB), lambda j, n: (jnp.maximum(n * (BLK // 8) - 1, 0), cb0 + j)),
                  pl.BlockSpec((BLK, CONV_CB), lambda j, n: (n, cb0 + j)),
                  pl.BlockSpec((8, CONV_CB), lambda j, n: (jnp.minimum((n + 1) * (BLK // 8), last8), cb0 + j)),
                  pl.BlockSpec((8, CONV_CB), lambda j, n: (0, wb0 + j)),
                  pl.BlockSpec((1, CONV_CB), lambda j, n: (0, wb0 + j)),
                  pl.BlockSpec((BLK, CONV_CB), lambda j, n: (n, j)),
                  pl.BlockSpec((8, CONV_CB), lambda j, n: (jnp.minimum((n + 1) * (BLK // 8), last8), j))],
        out_specs=[pl.BlockSpec((BLK, CONV_CB), lambda j, n: (n, j)),
                   pl.BlockSpec((8, CONV_CB), lambda j, n: (0, j)),
                   pl.BlockSpec((8, CONV_CB), lambda j, n: (0, j))],
        out_shape=[SDS((T, width), BF16), SDS((8, width), F32), SDS((8, width), F32)],
        name=name)(proj, proj, proj, conv_w, conv_b, dact, dact)


HPG = SSM_HEADS // SSM_GROUPS


def _ssd_fn(xs, zs, bm, cm, dt_raw, s_prev, dt_bias, a_log, d_skip, gns, g):
    dt_all = jax.nn.softplus(dt_raw + dt_bias)
    a_all = dt_all * (-jnp.exp(a_log))
    li = lax.broadcasted_iota(jnp.int32, (BLK, BLK), 0)
    si = lax.broadcasted_iota(jnp.int32, (BLK, BLK), 1)
    tril = li >= si
    cs_all = jnp.dot(tril.astype(F32), a_all, precision=HI, preferred_element_type=F32)
    cs_t = cs_all.T
    rowi = lax.broadcasted_iota(jnp.int32, (BLK, 1), 0)
    gmat = lax.dot_general(cm.astype(BF16), bm.astype(BF16), (((1,), (1,)), ((), ())), preferred_element_type=F32)
    ys, s_next = [], []
    for j in range(HPG):
        h = g * HPG + j
        dtc = jnp.sum(jnp.where(si == h, dt_all, 0.0), axis=1, keepdims=True)
        csc = jnp.sum(jnp.where(si == h, cs_all, 0.0), axis=1, keepdims=True)
        csr = jnp.sum(jnp.where(li == h, cs_t, 0.0), axis=0, keepdims=True)
        cs_last = jnp.sum(jnp.where(rowi == BLK - 1, csc, 0.0), axis=0, keepdims=True)
        dsk = _lane_pick(d_skip, h)
        lam = jnp.exp(jnp.where(tril, csc - csr, NEG))
        xr = xs[j] * dtc
        y = jnp.dot((gmat * lam).astype(BF16), xr.astype(BF16), preferred_element_type=F32)
        y_off = lax.dot_general(cm.astype(BF16), s_prev[j].astype(BF16), (((1,), (1,)), ((), ())),
                                preferred_element_type=F32)
        y = y + y_off * jnp.exp(csc) + dsk * xs[j]
        st = lax.dot_general((xr * jnp.exp(cs_last - csc)).astype(BF16), bm.astype(BF16), (((0,), (0,)), ((), ())),
                             preferred_element_type=F32)
        s_next.append(s_prev[j] * jnp.exp(cs_last) + st)
        ys.append(y * _silu(zs[j]))
    ss = ys[0] * ys[0]
    for j in range(1, HPG):
        ss = ss + ys[j] * ys[j]
    rn = lax.rsqrt(jnp.sum(ss, axis=1, keepdims=True) / GRP_W + EPS)
    outs = [ys[j] * rn * gns[j] for j in range(HPG)]
    return outs, s_next


def _ssd_in_specs(rev):
    cidx = (lambda c: NB - 1 - c) if rev else (lambda c: c)
    return [
        pl.BlockSpec((BLK, GRP_W), lambda g, c: (cidx(c), g)),
        pl.BlockSpec((BLK, SSM_STATE), lambda g, c: (cidx(c), SSM_INNER // SSM_STATE + g)),
        pl.BlockSpec((BLK, SSM_STATE), lambda g, c: (cidx(c), SSM_INNER // SSM_STATE + SSM_GROUPS + g)),
        pl.BlockSpec((BLK, 128), lambda g, c: (cidx(c), C_DT // 128)),
        pl.BlockSpec((BLK, GRP_W), lambda g, c: (cidx(c), C_ZS // GRP_W + g)),
        pl.BlockSpec((1, 128), lambda g, c: (0, 0)),
        pl.BlockSpec((1, 128), lambda g, c: (0, 0)),
        pl.BlockSpec((1, 128), lambda g, c: (0, 0)),
        pl.BlockSpec((1, GRP_W), lambda g, c: (0, g)),
    ]


def _ssd_load(xs_ref, z_ref, gn_ref):
    xs = [xs_ref[:, j * HEAD:(j + 1) * HEAD] for j in range(HPG)]
    zs = [z_ref[:, j * HEAD:(j + 1) * HEAD] for j in range(HPG)]
    gns = [gn_ref[:, j * HEAD:(j + 1) * HEAD] for j in range(HPG)]
    return xs, zs, gns


def _ssd_fwd(xbc_act, proj, dt_bias, a_log, d_skip, g_norm):
    def body(xs_ref, b_ref, c_ref, dt_ref, z_ref, dtb_ref, al_ref, dsk_ref, gn_ref, y_ref, st_ref, s_scr):
        g = pl.program_id(0)
        c = pl.program_id(1)

        @pl.when(c == 0)
        def _():
            s_scr[...] = jnp.zeros_like(s_scr)

        st_ref[0, 0] = s_scr[...]
        xs, zs, gns = _ssd_load(xs_ref, z_ref, gn_ref)
        s_prev = [s_scr[j * HEAD:(j + 1) * HEAD, :] for j in range(HPG)]
        outs, s_next = _ssd_fn(xs, zs, b_ref[...], c_ref[...], dt_ref[...], s_prev, dtb_ref[...], al_ref[...],
                               dsk_ref[...], gns, g)
        for j in range(HPG):
            y_ref[:, j * HEAD:(j + 1) * HEAD] = outs[j].astype(BF16)
            s_scr[j * HEAD:(j + 1) * HEAD, :] = s_next[j]

    return pl.pallas_call(
        body, grid=(SSM_GROUPS, NB), in_specs=_ssd_in_specs(False),
        out_specs=[pl.BlockSpec((BLK, GRP_W), lambda g, c: (c, g)),
                   pl.BlockSpec((1, 1, GRP_W, SSM_STATE), lambda g, c: (g, c, 0, 0))],
        out_shape=[SDS((T, SSM_INNER), BF16), SDS((SSM_GROUPS, NB, GRP_W, SSM_STATE), F32)],
        scratch_shapes=[pltpu.VMEM((GRP_W, SSM_STATE), F32)],
        name="ssd_fwd")(xbc_act, xbc_act, xbc_act, proj, proj, dt_bias, a_log, d_skip, g_norm)


def _ssd_bwd(xbc_act, proj, dt_bias, a_log, d_skip, g_norm, states, dy):
    def body(xs_ref, b_ref, c_ref, dt_ref, z_ref, dtb_ref, al_ref, dsk_ref, gn_ref, st_ref, dy_ref,
             dxs_ref, db_ref, dc_ref, ddt_ref, dz_ref, ddtb_ref, dal_ref, ddsk_ref, dgn_ref, ds_scr):
        g = pl.program_id(0)
        c = pl.program_id(1)

        @pl.when(c == 0)
        def _():
            ds_scr[...] = jnp.zeros_like(ds_scr)
            dgn_ref[...] = jnp.zeros_like(dgn_ref)

        @pl.when((c == 0) & (g == 0))
        def _():
            ddtb_ref[...] = jnp.zeros_like(ddtb_ref)
            dal_ref[...] = jnp.zeros_like(dal_ref)
            ddsk_ref[...] = jnp.zeros_like(ddsk_ref)

        xs, zs, gns = _ssd_load(xs_ref, z_ref, gn_ref)
        s_prev = [st_ref[0, 0, j * HEAD:(j + 1) * HEAD, :] for j in range(HPG)]
        fn = lambda a1, a2, a3, a4, a5, a6, a7, a8, a9, a10: _ssd_fn(a1, a2, a3, a4, a5, a6, a7, a8, a9, a10, g)
        _, vjp = jax.vjp(fn, xs, zs, b_ref[...], c_ref[...], dt_ref[...], s_prev, dtb_ref[...], al_ref[...],
                         dsk_ref[...], gns)
        cot_y = [dy_ref[:, j * HEAD:(j + 1) * HEAD] for j in range(HPG)]
        cot_s = [ds_scr[j * HEAD:(j + 1) * HEAD, :] for j in range(HPG)]
        dxs, dzs, dbm, dcm, ddt, dsp, ddtb, dal, ddsk, dgns = vjp((cot_y, cot_s))
        for j in range(HPG):
            dxs_ref[:, j * HEAD:(j + 1) * HEAD] = dxs[j]
            dz_ref[:, j * HEAD:(j + 1) * HEAD] = dzs[j].astype(BF16)
            ds_scr[j * HEAD:(j + 1) * HEAD, :] = dsp[j]
            dgn_ref[0:1, j * HEAD:(j + 1) * HEAD] += dgns[j]
        db_ref[...] = dbm
        dc_ref[...] = dcm
        ddt_ref[...] = ddt
        ddtb_ref[0:1, :] += ddtb
        dal_ref[0:1, :] += dal
        ddsk_ref[0:1, :] += ddsk

    rc = lambda c: NB - 1 - c
    small = pl.BlockSpec((8, 128), lambda g, c: (0, 0))
    return pl.pallas_call(
        body, grid=(SSM_GROUPS, NB),
        in_specs=_ssd_in_specs(True) + [
            pl.BlockSpec((1, 1, GRP_W, SSM_STATE), lambda g, c: (g, rc(c), 0, 0)),
            pl.BlockSpec((BLK, GRP_W), lambda g, c: (rc(c), g))],
        out_specs=[pl.BlockSpec((BLK, GRP_W), lambda g, c: (rc(c), g)),
                   pl.BlockSpec((BLK, SSM_STATE), lambda g, c: (rc(c), g)),
                   pl.BlockSpec((BLK, SSM_STATE), lambda g, c: (rc(c), g)),
                   pl.BlockSpec((BLK, 128), lambda g, c: (rc(c), g)),
                   pl.BlockSpec((BLK, GRP_W), lambda g, c: (rc(c), g)),
                   small, small, small,
                   pl.BlockSpec((8, GRP_W), lambda g, c: (0, g))],
        out_shape=[SDS((T, SSM_INNER), F32), SDS((T, GRP_W), F32), SDS((T, GRP_W), F32), SDS((T, GRP_W), F32),
                   SDS((T, SSM_INNER), BF16), SDS((8, 128), F32), SDS((8, 128), F32), SDS((8, 128), F32),
                   SDS((8, SSM_INNER), F32)],
        scratch_shapes=[pltpu.VMEM((GRP_W, SSM_STATE), F32)],
        compiler_params=_cparams(),
        name="ssd_bwd")(xbc_act, xbc_act, xbc_act, proj, proj, dt_bias, a_log, d_skip, g_norm, states, dy)


def _post_a(o, proj, sn, w_att, w_ssm, w_o):
    def body(o_ref, za_ref, ga_ref, gs_ref, sn_ref, wa_ref, ws_ref, wo_ref, a_ref, mg_ref, ya_ref, ys_ref, out_ref):
        a = (o_ref[...] * _silu(za_ref[...])).astype(BF16)
        a_ref[...] = a
        ya = jnp.dot(a, wa_ref[...], preferred_element_type=F32)
        ys = jnp.dot(sn_ref[...], ws_ref[...], preferred_element_type=F32)
        ya_ref[...] = ya
        ys_ref[...] = ys
        mg = (jax.nn.sigmoid(ga_ref[...]) * ya + jax.nn.sigmoid(gs_ref[...]) * ys).astype(BF16)
        mg_ref[...] = mg
        out_ref[...] = jnp.dot(mg, wo_ref[...], preferred_element_type=F32)

    row = pl.BlockSpec((BLK, D_MODEL), lambda i: (i, 0))
    pcol = lambda c0: pl.BlockSpec((BLK, D_MODEL), lambda i: (i, c0 // D_MODEL))
    full = lambda r: pl.BlockSpec((r, D_MODEL), lambda i: (0, 0))
    return pl.pallas_call(
        body, grid=(NB,),
        in_specs=[row, pcol(C_ZA), pcol(C_GA), pcol(C_GS), pl.BlockSpec((BLK, SSM_INNER), lambda i: (i, 0)),
                  full(D_MODEL), full(SSM_INNER), full(D_MODEL)],
        out_specs=[row, row, row, row, row],
        out_shape=[SDS((T, D_MODEL), BF16), SDS((T, D_MODEL), BF16), SDS((T, D_MODEL), F32), SDS((T, D_MODEL), F32),
                   SDS((T, D_MODEL), F32)],
        compiler_params=_cparams(), name="post_a")(o, proj, proj, proj, sn, w_att, w_ssm, w_o)


def _post_b(out, h, tgt, proj, ya, ys, o, g_post, w_att, w_ssm, w_o):
    def body(out_ref, h_ref, t_ref, za_ref, ga_ref, gs_ref, ya_ref, ys_ref, o_ref, gp_ref, wa_ref, ws_ref, wo_ref,
             loss_ref, dres_ref, dout_ref, dya_ref, dys_ref, dga_ref, dgs_ref, do_ref, dza_ref, dsn_ref, dgp_ref):
        i = pl.program_id(0)
        x = out_ref[...]
        gp = gp_ref[...]
        r = lax.rsqrt(jnp.mean(x * x, axis=-1, keepdims=True) + EPS)
        row = i * BLK + lax.broadcasted_iota(jnp.int32, (BLK, 1), 0)
        res = h_ref[...] + jnp.where(row >= PAD, x * r * gp, 0.0)
        live = row >= PAD + N_META
        err = jnp.where(live, res - t_ref[...], 0.0)
        lpart = 0.5 * jnp.sum(jnp.sum(err * err, axis=1, keepdims=True) / D_MODEL, axis=0, keepdims=True)
        dres = err / D_MODEL
        dres_ref[...] = dres
        gpart = jnp.sum(dres * x * r, axis=0, keepdims=True)

        @pl.when(i == 0)
        def _():
            loss_ref[...] = jnp.zeros_like(loss_ref)
            dgp_ref[...] = jnp.zeros_like(dgp_ref)

        loss_ref[...] += jnp.broadcast_to(lpart, loss_ref.shape)
        dgp_ref[0:1, :] += gpart
        gd = gp * dres
        dout = (r * gd - x * (r * r * r) * jnp.mean(x * gd, axis=-1, keepdims=True)).astype(BF16)
        dout_ref[...] = dout
        dmg = lax.dot_general(dout, wo_ref[...], (((1,), (1,)), ((), ())), preferred_element_type=F32)
        sga = jax.nn.sigmoid(ga_ref[...])
        sgs = jax.nn.sigmoid(gs_ref[...])
        dya = (dmg * sga).astype(BF16)
        dys = (dmg * sgs).astype(BF16)
        dya_ref[...] = dya
        dys_ref[...] = dys
        dga_ref[...] = (dmg * ya_ref[...] * sga * (1.0 - sga)).astype(BF16)
        dgs_ref[...] = (dmg * ys_ref[...] * sgs * (1.0 - sgs)).astype(BF16)
        da = lax.dot_general(dya, wa_ref[...], (((1,), (1,)), ((), ())), preferred_element_type=F32)
        za = za_ref[...]
        do_ref[...] = da * _silu(za)
        dza_ref[...] = (da * o_ref[...] * _dsilu(za)).astype(BF16)
        dsn_ref[...] = lax.dot_general(dys, ws_ref[...], (((1,), (1,)), ((), ())), preferred_element_type=F32)

    row = pl.BlockSpec((BLK, D_MODEL), lambda i: (i, 0))
    pcol = lambda c0: pl.BlockSpec((BLK, D_MODEL), lambda i: (i, c0 // D_MODEL))
    full = lambda r: pl.BlockSpec((r, D_MODEL), lambda i: (0, 0))
    small = pl.BlockSpec((8, D_MODEL), lambda i: (0, 0))
    return pl.pallas_call(
        body, grid=(NB,),
        in_specs=[row, row, row, pcol(C_ZA), pcol(C_GA), pcol(C_GS), row, row, row,
                  pl.BlockSpec((1, D_MODEL), lambda i: (0, 0)), full(D_MODEL), full(SSM_INNER), full(D_MODEL)],
        out_specs=[pl.BlockSpec((8, 128), lambda i: (0, 0)), row, row, row, row, row, row, row, row,
                   pl.BlockSpec((BLK, SSM_INNER), lambda i: (i, 0)), small],
        out_shape=[SDS((8, 128), F32), SDS((T, D_MODEL), F32), SDS((T, D_MODEL), BF16), SDS((T, D_MODEL), BF16),
                   SDS((T, D_MODEL), BF16), SDS((T, D_MODEL), BF16), SDS((T, D_MODEL), BF16), SDS((T, D_MODEL), F32),
                   SDS((T, D_MODEL), BF16), SDS((T, SSM_INNER), F32), SDS((8, D_MODEL), F32)],
        compiler_params=_cparams(), name="post_b")(out, h, tgt, proj, proj, proj, ya, ys, o, g_post, w_att, w_ssm, w_o)


def _assemble(dq, dza, dga, dgs, dzs, dxx, dxb, dxc, dk, dv, ddt4):
    def body(dq_ref, dza_ref, dga_ref, dgs_ref, dzs_ref, dxx_ref, dxb_ref, dxc_ref, dk_ref, dv_ref, ddt_ref, o_ref):
        o_ref[:, C_Q:C_Q + D_MODEL] = dq_ref[...].astype(BF16)
        o_ref[:, C_ZA:C_ZA + D_MODEL] = dza_ref[...]
        o_ref[:, C_GA:C_GA + D_MODEL] = dga_ref[...]
        o_ref[:, C_GS:C_GS + D_MODEL] = dgs_ref[...]
        o_ref[:, C_ZS:C_ZS + SSM_INNER] = dzs_ref[...]
        o_ref[:, C_XBC:C_XBC + SSM_INNER] = dxx_ref[...]
        o_ref[:, C_XBC + SSM_INNER:C_XBC + SSM_INNER + GRP_W] = dxb_ref[...]
        o_ref[:, C_XBC + SSM_INNER + GRP_W:C_XBC + CONV_DIM] = dxc_ref[...]
        o_ref[:, C_K:C_K + KV_W] = dk_ref[...].astype(BF16)
        o_ref[:, C_V:C_V + KV_W] = dv_ref[...].astype(BF16)
        d4 = ddt_ref[...]
        o_ref[:, C_DT:C_DT + 128] = (d4[:, 0:128] + d4[:, 128:256] + d4[:, 256:384] + d4[:, 384:512]).astype(BF16)

    spec = lambda w: pl.BlockSpec((BLK, w), lambda i: (i, 0))
    ins = [dq, dza, dga, dgs, dzs, dxx, dxb, dxc, dk, dv, ddt4]
    return pl.pallas_call(
        body, grid=(NB,), in_specs=[spec(a.shape[1]) for a in ins], out_specs=spec(PW),
        out_shape=SDS((T, PW), BF16), name="assemble")(*ins)


def _adamw_math(w, g, m, v):
    m = ADAM_B1 * m + (1.0 - ADAM_B1) * g
    v = ADAM_B2 * v + (1.0 - ADAM_B2) * (g * g)
    m_hat = m / (1.0 - ADAM_B1 ** ADAM_STEP)
    v_hat = v / (1.0 - ADAM_B2 ** ADAM_STEP)
    delta = -ADAM_LR * (m_hat / (jnp.sqrt(v_hat) + ADAM_EPS) + ADAM_WD * w)
    return delta, m, v


def _sum_adamw(recv, w, m, v, tc, name):
    rows, cols = w.shape
    assert cols % tc == 0

    def body(r_ref, w_ref, m_ref, v_ref, g_ref, d_ref, nm_ref, nv_ref):
        g = r_ref[0].astype(F32)
        for d in range(1, N_DEV):
            g = g + r_ref[d].astype(F32)
        g_ref[...] = g
        delta, nm, nv = _adamw_math(w_ref[...], g, m_ref[...], v_ref[...])
        d_ref[...] = delta
        nm_ref[...] = nm
        nv_ref[...] = nv

    blk = pl.BlockSpec((rows, tc), lambda i: (0, i))
    return pl.pallas_call(
        body, grid=(cols // tc,),
        in_specs=[pl.BlockSpec((N_DEV, rows, tc), lambda i: (0, 0, i)), blk, blk, blk],
        out_specs=[blk, blk, blk, blk], out_shape=[SDS((rows, cols), F32)] * 4,
        compiler_params=_cparams(), name=name)(recv, w, m, v)


ROW_GPRE, ROW_CONVB, ROW_DTB, ROW_ALOG, ROW_DSKIP, ROW_SINK, ROW_GSSM, ROW_GPOST = 0, 1, 4, 5, 6, 7, 8, 10
REP_ROWS, ROW_CONVW, ROW_META, SM_ROWS = 16, 16, 24, 40
CW_SHARD = CONV_DIM // N_DEV
META_SHARD = D_MODEL // N_DEV


def _small_pack(dgpre, dbx, dbb, dbc, ddtb, dal, ddsk, dsink, dgn, dgp, dwx, dwb, dwc, dh):
    def body(dgpre_ref, dbx_ref, dbb_ref, dbc_ref, ddtb_ref, dal_ref, ddsk_ref, dsink_ref, dgn_ref, dgp_ref,
             dwx_ref, dwb_ref, dwc_ref, dh_ref, o_ref, rep):
        rep[...] = jnp.zeros_like(rep)
        rep[ROW_GPRE:ROW_GPRE + 1, :] = dgpre_ref[0:1, :]
        rep[ROW_CONVB:ROW_CONVB + 1, :] = dbx_ref[0:1, 0:1024]
        rep[ROW_CONVB + 1:ROW_CONVB + 2, :] = dbx_ref[0:1, 1024:2048]
        rep[ROW_CONVB + 2:ROW_CONVB + 3, 0:512] = dbb_ref[0:1, :]
        rep[ROW_CONVB + 2:ROW_CONVB + 3, 512:1024] = dbc_ref[0:1, :]
        rep[ROW_DTB:ROW_DTB + 1, 0:128] = ddtb_ref[0:1, :]
        rep[ROW_ALOG:ROW_ALOG + 1, 0:128] = dal_ref[0:1, :]
        rep[ROW_DSKIP:ROW_DSKIP + 1, 0:128] = ddsk_ref[0:1, :]
        rep[ROW_SINK:ROW_SINK + 1, 0:128] = dsink_ref[0:1, :]
        rep[ROW_GSSM:ROW_GSSM + 1, :] = dgn_ref[0:1, 0:1024]
        rep[ROW_GSSM + 1:ROW_GSSM + 2, :] = dgn_ref[0:1, 1024:2048]
        rep[ROW_GPOST:ROW_GPOST + 1, :] = dgp_ref[0:1, :]
        cw = jnp.concatenate([dwx_ref[...], dwb_ref[...], dwc_ref[...]], axis=1)
        mh = dh_ref[...]
        o_ref[...] = jnp.zeros_like(o_ref)
        for p in range(N_DEV):
            o_ref[p, 0:REP_ROWS, :] = rep[...]
            o_ref[p, ROW_CONVW:ROW_CONVW + 8, 0:CW_SHARD] = cw[:, p * CW_SHARD:(p + 1) * CW_SHARD]
            o_ref[p, ROW_META:ROW_META + N_META, 0:META_SHARD] = mh[:, p * META_SHARD:(p + 1) * META_SHARD]

    ins = [dgpre, dbx, dbb, dbc, ddtb, dal, ddsk, dsink, dgn, dgp, dwx, dwb, dwc]
    return pl.pallas_call(
        body, grid=(1,),
        in_specs=[pl.BlockSpec(a.shape, lambda i: (0, 0)) for a in ins]
        + [pl.BlockSpec((N_META, D_MODEL), lambda i: (PAD // N_META, 0))],
        out_specs=pl.BlockSpec((N_DEV, SM_ROWS, 1024), lambda i: (0, 0, 0)),
        out_shape=SDS((N_DEV, SM_ROWS, 1024), F32), scratch_shapes=[pltpu.VMEM((REP_ROWS, 1024), F32)],
        name="small_pack")(*ins, dh)


def _small_finish(recv, params):
    npar = len(params)

    def body(*refs):
        r_ref = refs[0]
        wmv = refs[1:1 + 3 * npar]
        outs = refs[1 + 3 * npar:1 + 7 * npar]
        gs = refs[-1]
        g = r_ref[0]
        for d in range(1, N_DEV):
            g = g + r_ref[d]
        gs[...] = g
        grads = [
            gs[ROW_GPRE:ROW_GPRE + 1, :],
            jnp.concatenate([gs[ROW_CONVB + k:ROW_CONVB + k + 1, :] for k in range(3)], axis=1),
            gs[ROW_DTB:ROW_DTB + 1, 0:SSM_HEADS], gs[ROW_ALOG:ROW_ALOG + 1, 0:SSM_HEADS],
            gs[ROW_DSKIP:ROW_DSKIP + 1, 0:SSM_HEADS], gs[ROW_SINK:ROW_SINK + 1, 0:Q_HEADS],
            jnp.concatenate([gs[ROW_GSSM:ROW_GSSM + 1, :], gs[ROW_GSSM + 1:ROW_GSSM + 2, :]], axis=1),
            gs[ROW_GPOST:ROW_GPOST + 1, :],
            gs[ROW_CONVW:ROW_CONVW + 4, 0:CW_SHARD],
            gs[ROW_META:ROW_META + N_META, 0:META_SHARD]]
        for i in range(npar):
            w_ref, m_ref, v_ref = wmv[3 * i:3 * i + 3]
            delta, nm, nv = _adamw_math(w_ref[...], grads[i], m_ref[...], v_ref[...])
            outs[4 * i][...] = grads[i]
            outs[4 * i + 1][...] = delta
            outs[4 * i + 2][...] = nm
            outs[4 * i + 3][...] = nv

    flat = [a for wmv in params for a in wmv]
    res = pl.pallas_call(
        body, out_shape=[SDS(wmv[0].shape, F32) for wmv in params for _ in range(4)],
        scratch_shapes=[pltpu.VMEM((SM_ROWS, 1024), F32)], name="small_finish")(recv, *flat)
    return [tuple(res[4 * i:4 * i + 4]) for i in range(npar)]


def _slab(ref, px, py, pc):
    return ref.at[4 * px + 2 * py + pc]


def _all_gather(shards):
    na = len(shards)

    def body(*refs):
        ins, outs = refs[:na], refs[na:2 * na]
        send_sems, recv_sems, local_sems = refs[2 * na:]
        x, y, c = lax.axis_index("x"), lax.axis_index("y"), lax.axis_index("c")
        me, sibling = (x, y, c), (x, y, 1 - c)
        chips = [(1 - x, y), (x, 1 - y), (1 - x, 1 - y)]

        def copy(a, k, block, to, src=None):
            dst = _slab(outs[a], *block)
            return pltpu.make_async_remote_copy(
                src_ref=dst if src is None else src, dst_ref=dst, send_sem=send_sems.at[a, k],
                recv_sem=recv_sems.at[a, k], device_id=to, device_id_type=MESH)

        mine = [pltpu.make_async_copy(ins[a], _slab(outs[a], *me), local_sems.at[a]) for a in range(na)]
        for cp in mine:
            cp.start()
        first = []
        for a in range(na):
            first.append(copy(a, 0, me, sibling, src=ins[a]))
            first += [copy(a, 1 + j, me, (*chip, c), src=ins[a]) for j, chip in enumerate(chips)]
        for cp in first:
            cp.start()
        passed = []
        for j, chip in enumerate(chips):
            for a in range(na):
                copy(a, 1 + j, (*chip, c), me).wait_recv()
                cp = copy(a, 4 + j, (*chip, c), sibling)
                cp.start()
                passed.append(cp)
        for a in range(na):
            copy(a, 0, sibling, me).wait_recv()
            for j, chip in enumerate(chips):
                copy(a, 4 + j, (*chip, 1 - c), me).wait_recv()
        for cp in first + passed:
            cp.wait_send()
        for cp in mine:
            cp.wait()

    return pl.pallas_call(
        body, in_specs=[ANY] * na, out_specs=[ANY] * na,
        out_shape=[SDS((N_DEV,) + s.shape, s.dtype) for s in shards],
        scratch_shapes=[pltpu.SemaphoreType.DMA((na, 7)), pltpu.SemaphoreType.DMA((na, 7)),
                        pltpu.SemaphoreType.DMA((na,))],
        name="all_gather")(*shards)


def _exchange(parts):
    na = len(parts)

    def body(*refs):
        ins, outs = refs[:na], refs[na:2 * na]
        send_sems, recv_sems, local_sems = refs[2 * na:]
        x, y, c = lax.axis_index("x"), lax.axis_index("y"), lax.axis_index("c")
        me = (x, y, c)
        mine = [pltpu.make_async_copy(_slab(ins[a], *me), _slab(outs[a], *me), local_sems.at[a]) for a in range(na)]
        for cp in mine:
            cp.start()
        peers = []
        for k in range(1, N_DEV):
            dx, dy, dc = (k >> 2) & 1, (k >> 1) & 1, k & 1
            peers.append(((1 - x) if dx else x, (1 - y) if dy else y, (1 - c) if dc else c))
        sent = []
        for a in range(na):
            for k, peer in enumerate(peers):
                cp = pltpu.make_async_remote_copy(
                    src_ref=_slab(ins[a], *peer), dst_ref=_slab(outs[a], *me), send_sem=send_sems.at[a, k],
                    recv_sem=recv_sems.at[a, k], device_id=peer, device_id_type=MESH)
                cp.start()
                sent.append(cp)
        for a in range(na):
            for k, peer in enumerate(peers):
                pltpu.make_async_remote_copy(
                    src_ref=_slab(ins[a], *peer), dst_ref=_slab(outs[a], *peer), send_sem=send_sems.at[a, k],
                    recv_sem=recv_sems.at[a, k], device_id=peer, device_id_type=MESH).wait_recv()
        for cp in sent:
            cp.wait_send()
        for cp in mine:
            cp.wait()

    return pl.pallas_call(
        body, in_specs=[ANY] * na, out_specs=[ANY] * na,
        out_shape=[SDS(p.shape, p.dtype) for p in parts],
        scratch_shapes=[pltpu.SemaphoreType.DMA((na, 7)), pltpu.SemaphoreType.DMA((na, 7)),
                        pltpu.SemaphoreType.DMA((na,))],
        name="exchange")(*parts)


def _cast_shards(w_in_t, w_att, w_ssm, w_o):
    def body(wi_ref, wa_ref, ws_ref, wo_ref, a_ref, b_ref, c_ref, d_ref):
        a_ref[...] = wi_ref[...].astype(BF16)
        b_ref[...] = wa_ref[...].astype(BF16)
        c_ref[...] = ws_ref[...].astype(BF16)
        d_ref[...] = wo_ref[...].astype(BF16)

    return pl.pallas_call(
        body, out_shape=[SDS(w_in_t.shape, BF16), SDS(w_att.shape, BF16), SDS(w_ssm.shape, BF16), SDS(w_o.shape, BF16)],
        compiler_params=_cparams(), name="cast_shards")(w_in_t, w_att, w_ssm, w_o)


def _pieces():
    out = []
    for r0, c0, w in _SEGS:
        r = r0
        while r < r0 + w:
            d = r // SHARD_IN
            n = min(r0 + w, (d + 1) * SHARD_IN) - r
            out.append((c0 + (r - r0), d, r - d * SHARD_IN, n))
            r += n
    return out


def _to_aligned_t(slabs):
    def body(a_ref, o_ref):
        for (t, d, s, n) in _pieces():
            o_ref[t:t + n, :] = a_ref[d, s:s + n, :]
        o_ref[C_DT + 32:C_DT + 128, :] = jnp.zeros((96, D_MODEL), slabs.dtype)

    return pl.pallas_call(body, out_shape=SDS((PW, D_MODEL), slabs.dtype), compiler_params=_cparams(),
                          name="to_aligned")(slabs)


def _from_aligned_t(g):
    def body(g_ref, o_ref):
        for (t, d, s, n) in _pieces():
            o_ref[d, s:s + n, :] = g_ref[t:t + n, :]

    return pl.pallas_call(body, out_shape=SDS((N_DEV, SHARD_IN, D_MODEL), g.dtype), compiler_params=_cparams(),
                          name="from_aligned")(g)


_SEGS = [
    (R_Q, C_Q, 1024), (R_K, C_K, 256), (R_V, C_V, 256), (R_ZA, C_ZA, 1024), (R_ZS, C_ZS, 2048),
    (R_XBC, C_XBC, 3072), (R_DT, C_DT, 32), (R_GA, C_GA, 1024), (R_GS, C_GS, 1024)]


def _pad_lanes(v, n=128):
    return jnp.pad(v, ((0, 0), (0, n - v.shape[1])))


def _local_step(h, tgt, w_alt, w_att, w_ssm, w_o, g_pre, conv_w8, conv_b, dt_bias, a_log, d_skip, sinks,
                g_ssm, g_post):
    dtb, al, dsk, snk = _pad_lanes(dt_bias), _pad_lanes(a_log), _pad_lanes(d_skip), _pad_lanes(sinks)
    u = _norm_u(h, g_pre)
    proj = _matmul(u, w_alt, "nt", F32, 1088, 896, D_MODEL, "in_proj")
    o = _attn_fwd(proj, snk)
    xbc_act = _conv_fwd(proj, conv_w8, conv_b)
    sn, states = _ssd_fwd(xbc_act, proj, dtb, al, dsk, g_ssm)
    a_in, mg, ya, ys, out = _post_a(o, proj, sn, w_att, w_ssm, w_o)
    (loss, dres, dout, dya, dys, dga, dgs, do, dza, dsn, dgp) = _post_b(
        out, h, tgt, proj, ya, ys, o, g_post, w_att, w_ssm, w_o)
    dxs, dbm, dcm, ddt4, dzs, ddtb, dal, ddsk, dgn = _ssd_bwd(xbc_act, proj, dtb, al, dsk, g_ssm, states, dsn)
    dxx, dwx, dbx = _conv_bwd(proj, conv_w8, conv_b, dxs, 0, "conv_bwd_x")
    dxb, dwb, dbb = _conv_bwd(proj, conv_w8, conv_b, dbm, SSM_INNER, "conv_bwd_b")
    dxc, dwc, dbc = _conv_bwd(proj, conv_w8, conv_b, dcm, SSM_INNER + GRP_W, "conv_bwd_c")
    dq, dk, dv, dsink = _attn_bwd(proj, snk, do)
    dproj = _assemble(dq, dza, dga, dgs, dzs, dxx, dxb, dxc, dk, dv, ddt4)
    du = _matmul(dproj, w_alt, "nn", F32, 1088, D_MODEL, 896, "d_u")
    dw_alt = _matmul(dproj, u, "tn", BF16, 896, D_MODEL, T, "d_w_in")
    dh, dgpre = _norm_bwd(h, g_pre, du, dres)
    dw_att = _matmul(a_in, dya, "tn", BF16, D_MODEL, D_MODEL, T, "d_w_att")
    dw_ssm = _matmul(sn, dys, "tn", BF16, D_MODEL, D_MODEL, T, "d_w_ssm")
    dw_o = _matmul(mg, dout, "tn", BF16, D_MODEL, D_MODEL, T, "d_w_o")
    return dict(
        loss=loss[0, 0], dh=dh, dw_alt=dw_alt, dw_att=dw_att, dw_ssm=dw_ssm, dw_o=dw_o,
        small=(dgpre, dbx, dbb, dbc, ddtb, dal, ddsk, dsink, dgn, dgp, dwx, dwb, dwc))


def kernel(x, meta_tokens, g_pre, w_in, conv_w, conv_b, dt_bias, a_log, d_skip, attn_sinks, g_ssm_norm, w_out_att, w_out_ssm, w_out, g_post, loss_target, m_meta_tokens, m_g_pre, m_w_in, m_conv_w, m_conv_b, m_dt_bias, m_a_log, m_d_skip, m_attn_sinks, m_g_ssm_norm, m_w_out_att, m_w_out_ssm, m_w_out, m_g_post, v_meta_tokens, v_g_pre, v_w_in, v_conv_w, v_conv_b, v_dt_bias, v_a_log, v_d_skip, v_attn_sinks, v_g_ssm_norm, v_w_out_att, v_w_out_ssm, v_w_out, v_g_post):
    w_in_t, m_in_t, v_in_t = jnp.transpose(w_in[0]), jnp.transpose(m_w_in[0]), jnp.transpose(v_w_in[0])
    a_sh, att_sh, ssm_sh, o_sh = _cast_shards(w_in_t, w_out_att[0], w_out_ssm[0], w_out[0])
    cw_sh = jnp.pad(conv_w[0], ((0, 4), (0, 0)))
    a_all, att_all, ssm_all, o_all, meta_all, cw_all = _all_gather([a_sh, att_sh, ssm_sh, o_sh, meta_tokens, cw_sh])
    w_alt = _to_aligned_t(a_all)
    w_att = att_all.reshape(D_MODEL, D_MODEL)
    w_ssm = ssm_all.reshape(SSM_INNER, D_MODEL)
    w_o = o_all.reshape(D_MODEL, D_MODEL)
    meta_full = meta_all.transpose(1, 0, 2).reshape(N_META, D_MODEL)
    conv_w8 = cw_all.transpose(1, 0, 2).reshape(8, CONV_DIM)

    h = jnp.concatenate([jnp.zeros((PAD, D_MODEL), F32), meta_full, x[0]], axis=0)
    tgt = jnp.concatenate([jnp.zeros((PAD + N_META, D_MODEL), F32), loss_target[0]], axis=0)
    r = _local_step(h, tgt, w_alt, w_att, w_ssm, w_o, g_pre, conv_w8, conv_b, dt_bias, a_log, d_skip, attn_sinks,
                    g_ssm_norm, g_post)
    loss = lax.psum(r["loss"], ("x", "y", "c"))
    grad_x = r["dh"][PAD + N_META:][None]

    small8 = _small_pack(*r["small"], r["dh"])
    ra, r_att, r_ssm, r_o, rs = _exchange([
        _from_aligned_t(r["dw_alt"]), r["dw_att"].reshape(N_DEV, 128, D_MODEL),
        r["dw_ssm"].reshape(N_DEV, 256, D_MODEL), r["dw_o"].reshape(N_DEV, 128, D_MODEL), small8])

    res_in = [jnp.transpose(t)[None] for t in _sum_adamw(ra, w_in_t, m_in_t, v_in_t, 128, "adamw_w_in")]
    res_att = [t[None] for t in _sum_adamw(r_att, w_out_att[0], m_w_out_att[0], v_w_out_att[0], 512, "adamw_w_att")]
    res_ssm = [t[None] for t in _sum_adamw(r_ssm, w_out_ssm[0], m_w_out_ssm[0], v_w_out_ssm[0], 512, "adamw_w_ssm")]
    res_o = [t[None] for t in _sum_adamw(r_o, w_out[0], m_w_out[0], v_w_out[0], 512, "adamw_w_o")]
    (res_gpre, res_convb, res_dtb, res_alog, res_dskip, res_sink, res_gssm, res_gpost, res_cw, res_meta) = _small_finish(
        rs, [(g_pre, m_g_pre, v_g_pre), (conv_b, m_conv_b, v_conv_b), (dt_bias, m_dt_bias, v_dt_bias),
             (a_log, m_a_log, v_a_log), (d_skip, m_d_skip, v_d_skip), (attn_sinks, m_attn_sinks, v_attn_sinks),
             (g_ssm_norm, m_g_ssm_norm, v_g_ssm_norm), (g_post, m_g_post, v_g_post),
             (conv_w[0], m_conv_w[0], v_conv_w[0]), (meta_tokens, m_meta_tokens, v_meta_tokens)])
    res_cw = [t[None] for t in res_cw]
    per_weight = [res_meta, res_gpre, res_in, res_cw, res_convb, res_dtb, res_alog, res_dskip, res_sink, res_gssm,
                  res_att, res_ssm, res_o, res_gpost]
    return (loss, grad_x, *[p[0] for p in per_weight], *[p[1] for p in per_weight], *[p[2] for p in per_weight],
            *[p[3] for p in per_weight])
```

```python
import functools
import math

import jax
import jax.numpy as jnp
from jax import lax
from jax.experimental import pallas as pl
from jax.experimental.pallas import tpu as pltpu

F32 = jnp.float32
BF16 = jnp.bfloat16
SDS = jax.ShapeDtypeStruct
HI = lax.Precision.HIGHEST
MESH = pl.DeviceIdType.MESH
ANY = pl.BlockSpec(memory_space=pl.ANY)

N_DEV = 8
D_MODEL = 1024
SEQ = 2048
N_META = 16
BLK = 128
PAD = 112
T = PAD + N_META + SEQ
NB = T // BLK
EPS = 1e-6
HEAD = 64
Q_HEADS = 16
KV_HEADS = 4
GROUP = 4
KV_W = 256
SSM_INNER = 2048
SSM_HEADS = 32
SSM_GROUPS = 4
GRP_W = 512
SSM_STATE = 128
CONV_DIM = 3072
IN_PROJ = 9760
SHARD_IN = IN_PROJ // N_DEV
NEG = -1e30

C_Q, C_ZA, C_GA, C_GS, C_ZS, C_XBC, C_K, C_V, C_DT = 0, 1024, 2048, 3072, 4096, 6144, 9216, 9472, 9728
PW = 9856
R_Q, R_K, R_V, R_ZA, R_ZS, R_XBC, R_DT, R_GA, R_GS = 0, 1024, 1280, 1536, 2560, 4608, 7680, 7712, 8736

ADAM_LR, ADAM_B1, ADAM_B2, ADAM_EPS, ADAM_WD, ADAM_STEP = 0.001, 0.9, 0.999, 1e-08, 0.01, 10

VMEM_LIMIT = 56 * 1024 * 1024


def _cparams():
    return pltpu.CompilerParams(vmem_limit_bytes=VMEM_LIMIT)


def _silu(x):
    return x * jax.nn.sigmoid(x)


def _dsilu(x):
    s = jax.nn.sigmoid(x)
    return s * (1.0 + x * (1.0 - s))


def _matmul(a, b, mode, out_dtype, tm, tn, tk, name):
    if mode == "nn":
        (m, k), n = a.shape, b.shape[1]
        a_spec = pl.BlockSpec((tm, tk), lambda i, j, kk: (i, kk))
        b_spec = pl.BlockSpec((tk, tn), lambda i, j, kk: (kk, j))
        dims = (((1,), (0,)), ((), ()))
    elif mode == "nt":
        (m, k), n = a.shape, b.shape[0]
        a_spec = pl.BlockSpec((tm, tk), lambda i, j, kk: (i, kk))
        b_spec = pl.BlockSpec((tn, tk), lambda i, j, kk: (j, kk))
        dims = (((1,), (1,)), ((), ()))
    else:
        (k, m), n = a.shape, b.shape[1]
        a_spec = pl.BlockSpec((tk, tm), lambda i, j, kk: (kk, i))
        b_spec = pl.BlockSpec((tk, tn), lambda i, j, kk: (kk, j))
        dims = (((0,), (0,)), ((), ()))
    assert m % tm == 0 and n % tn == 0 and k % tk == 0, (a.shape, b.shape, tm, tn, tk)
    nk = k // tk

    def body(a_ref, b_ref, o_ref, acc_ref):
        kk = pl.program_id(2)
        part = lax.dot_general(a_ref[...], b_ref[...], dims, preferred_element_type=F32)

        @pl.when(kk == 0)
        def _():
            acc_ref[...] = part

        @pl.when(kk > 0)
        def _():
            acc_ref[...] += part

        @pl.when(kk == nk - 1)
        def _():
            o_ref[...] = acc_ref[...].astype(out_dtype)

    return pl.pallas_call(
        body, grid=(m // tm, n // tn, nk), in_specs=[a_spec, b_spec],
        out_specs=pl.BlockSpec((tm, tn), lambda i, j, kk: (i, j)),
        out_shape=SDS((m, n), out_dtype), scratch_shapes=[pltpu.VMEM((tm, tn), F32)],
        compiler_params=_cparams(), name=name)(a, b)


def _norm_u(h, g_pre):
    def body(h_ref, g_ref, u_ref):
        x = h_ref[...]
        r = lax.rsqrt(jnp.mean(x * x, axis=-1, keepdims=True) + EPS)
        u_ref[...] = (x * r * g_ref[...]).astype(BF16)

    return pl.pallas_call(
        body, grid=(NB,),
        in_specs=[pl.BlockSpec((BLK, D_MODEL), lambda i: (i, 0)), pl.BlockSpec((1, D_MODEL), lambda i: (0, 0))],
        out_specs=pl.BlockSpec((BLK, D_MODEL), lambda i: (i, 0)),
        out_shape=SDS((T, D_MODEL), BF16), name="norm_u")(h, g_pre)


def _norm_bwd(h, g_pre, du, dres):
    def body(h_ref, g_ref, du_ref, dres_ref, dh_ref, dg_ref):
        i = pl.program_id(0)
        x = h_ref[...]
        g = g_ref[...]
        du_ = du_ref[...]
        r = lax.rsqrt(jnp.mean(x * x, axis=-1, keepdims=True) + EPS)
        gd = g * du_
        dx = r * gd - x * (r * r * r) * jnp.mean(x * gd, axis=-1, keepdims=True)
        dh_ref[...] = dx + dres_ref[...]
        part = jnp.sum(du_ * x * r, axis=0, keepdims=True)

        @pl.when(i == 0)
        def _():
            dg_ref[...] = jnp.zeros_like(dg_ref)

        dg_ref[0:1, :] += part

    row = pl.BlockSpec((BLK, D_MODEL), lambda i: (i, 0))
    return pl.pallas_call(
        body, grid=(NB,),
        in_specs=[row, pl.BlockSpec((1, D_MODEL), lambda i: (0, 0)), row, row],
        out_specs=[row, pl.BlockSpec((8, D_MODEL), lambda i: (0, 0))],
        out_shape=[SDS((T, D_MODEL), F32), SDS((8, D_MODEL), F32)], name="norm_bwd")(h, g_pre, du, dres)


def _lane_pick(row, h):
    lane = lax.broadcasted_iota(jnp.int32, row.shape, 1)
    return jnp.sum(jnp.where(lane == h, row, 0.0), axis=1, keepdims=True)


def _attn_fn(q4s, kcats, vcats, kms, vms, sinks, n):
    r = lax.broadcasted_iota(jnp.int32, (GROUP * BLK, 2 * BLK), 0)
    s = lax.broadcasted_iota(jnp.int32, (GROUP * BLK, 2 * BLK), 1)
    i = jnp.bitwise_and(r, BLK - 1)
    gi = jnp.right_shift(r, 7)
    rel = i - s + BLK
    k_pos = n * BLK - BLK + s
    band_ok = (rel >= 0) & (rel < BLK) & (k_pos >= PAD + N_META)
    relf = rel.astype(F32)
    rm = lax.broadcasted_iota(jnp.int32, (GROUP * BLK, N_META), 0)
    mm = lax.broadcasted_iota(jnp.int32, (GROUP * BLK, N_META), 1)
    meta_ok = (PAD + mm) <= (n * BLK + jnp.bitwise_and(rm, BLK - 1))
    gcol = jnp.right_shift(lax.broadcasted_iota(jnp.int32, (GROUP * BLK, 1), 0), 7)
    outs = []
    for kh in range(KV_HEADS):
        slopes = [2.0 ** (-8.0 * (kh * GROUP + g + 1) / Q_HEADS) for g in range(GROUP)]
        slope = jnp.where(gi == 0, slopes[0], jnp.where(gi == 1, slopes[1], jnp.where(gi == 2, slopes[2], slopes[3])))
        sk = [_lane_pick(sinks, kh * GROUP + g) for g in range(GROUP)]
        sink = jnp.where(gcol == 0, sk[0], jnp.where(gcol == 1, sk[1], jnp.where(gcol == 2, sk[2], sk[3])))
        qb = (q4s[kh] * (HEAD ** -0.5)).astype(BF16)
        sb = lax.dot_general(qb, kcats[kh].astype(BF16), (((1,), (1,)), ((), ())), preferred_element_type=F32)
        sb = jnp.where(band_ok, sb - slope * relf, NEG)
        sm = lax.dot_general(qb, kms[kh].astype(BF16), (((1,), (1,)), ((), ())), preferred_element_type=F32)
        sm = jnp.where(meta_ok, sm, NEG)
        mx = jnp.maximum(jnp.maximum(jnp.max(sb, axis=1, keepdims=True), jnp.max(sm, axis=1, keepdims=True)), sink)
        mx = lax.stop_gradient(mx)
        eb = jnp.exp(sb - mx)
        em = jnp.exp(sm - mx)
        es = jnp.exp(sink - mx)
        inv = 1.0 / (jnp.sum(eb, axis=1, keepdims=True) + jnp.sum(em, axis=1, keepdims=True) + es)
        pb = (eb * inv).astype(BF16)
        pm = (em * inv).astype(BF16)
        o4 = (jnp.dot(pm, vms[kh].astype(BF16), preferred_element_type=F32)
              + jnp.dot(pb, vcats[kh].astype(BF16), preferred_element_type=F32))
        outs.append(o4)
    return outs


def _attn_specs():
    prev = lambda n: jnp.maximum(n - 1, 0)
    return [
        pl.BlockSpec((BLK, D_MODEL), lambda n: (n, C_Q // D_MODEL)),
        pl.BlockSpec((BLK, KV_W), lambda n: (prev(n), C_K // KV_W)),
        pl.BlockSpec((BLK, KV_W), lambda n: (n, C_K // KV_W)),
        pl.BlockSpec((BLK, KV_W), lambda n: (prev(n), C_V // KV_W)),
        pl.BlockSpec((BLK, KV_W), lambda n: (n, C_V // KV_W)),
        pl.BlockSpec((N_META, KV_W), lambda n: (PAD // N_META, C_K // KV_W)),
        pl.BlockSpec((N_META, KV_W), lambda n: (PAD // N_META, C_V // KV_W)),
        pl.BlockSpec((1, 128), lambda n: (0, 0)),
    ]


def _attn_load(q_ref, kp_ref, kc_ref, vp_ref, vc_ref, km_ref, vm_ref):
    q4s, kcats, vcats, kms, vms = [], [], [], [], []
    for kh in range(KV_HEADS):
        q4s.append(jnp.concatenate(
            [q_ref[:, (kh * GROUP + g) * HEAD:(kh * GROUP + g + 1) * HEAD] for g in range(GROUP)], axis=0))
        cs = slice(kh * HEAD, (kh + 1) * HEAD)
        kcats.append(jnp.concatenate([kp_ref[:, cs], kc_ref[:, cs]], axis=0))
        vcats.append(jnp.concatenate([vp_ref[:, cs], vc_ref[:, cs]], axis=0))
        kms.append(km_ref[:, cs])
        vms.append(vm_ref[:, cs])
    return q4s, kcats, vcats, kms, vms


def _attn_fwd(proj, sinks):
    def body(q_ref, kp_ref, kc_ref, vp_ref, vc_ref, km_ref, vm_ref, s_ref, o_ref):
        n = pl.program_id(0)
        args = _attn_load(q_ref, kp_ref, kc_ref, vp_ref, vc_ref, km_ref, vm_ref)
        outs = _attn_fn(*args, s_ref[...], n)
        for kh in range(KV_HEADS):
            for g in range(GROUP):
                hh = kh * GROUP + g
                o_ref[:, hh * HEAD:(hh + 1) * HEAD] = outs[kh][g * BLK:(g + 1) * BLK]

    return pl.pallas_call(
        body, grid=(NB,), in_specs=_attn_specs(),
        out_specs=pl.BlockSpec((BLK, D_MODEL), lambda n: (n, 0)),
        out_shape=SDS((T, D_MODEL), F32), name="attn_fwd")(proj, proj, proj, proj, proj, proj, proj, sinks)


def _attn_bwd(proj, sinks, do):
    def body(q_ref, kp_ref, kc_ref, vp_ref, vc_ref, km_ref, vm_ref, s_ref, do_ref, dq_ref, dk_ref, dv_ref, ds_ref):
        n = pl.program_id(0)

        @pl.when(n == 0)
        def _():
            dk_ref[...] = jnp.zeros_like(dk_ref)
            dv_ref[...] = jnp.zeros_like(dv_ref)
            ds_ref[...] = jnp.zeros_like(ds_ref)

        args = _attn_load(q_ref, kp_ref, kc_ref, vp_ref, vc_ref, km_ref, vm_ref)
        _, vjp = jax.vjp(lambda a, b, c, d, e, f: _attn_fn(a, b, c, d, e, f, n), *args, s_ref[...])
        cot = [jnp.concatenate([do_ref[:, (kh * GROUP + g) * HEAD:(kh * GROUP + g + 1) * HEAD] for g in range(GROUP)],
                               axis=0) for kh in range(KV_HEADS)]
        dq4s, dkcats, dvcats, dkms, dvms, dsk = vjp(cot)
        ds_ref[0:1, :] += dsk
        cur = pl.ds(pl.multiple_of(n * BLK, BLK), BLK)
        meta = slice(PAD, PAD + N_META)
        for kh in range(KV_HEADS):
            cs = slice(kh * HEAD, (kh + 1) * HEAD)
            for g in range(GROUP):
                hh = kh * GROUP + g
                dq_ref[:, hh * HEAD:(hh + 1) * HEAD] = dq4s[kh][g * BLK:(g + 1) * BLK]
            dk_ref[cur, cs] += dkcats[kh][BLK:]
            dv_ref[cur, cs] += dvcats[kh][BLK:]
            dk_ref[meta, cs] += dkms[kh]
            dv_ref[meta, cs] += dvms[kh]

        @pl.when(n > 0)
        def _():
            prv = pl.ds(pl.multiple_of((n - 1) * BLK, BLK), BLK)
            for kh in range(KV_HEADS):
                cs = slice(kh * HEAD, (kh + 1) * HEAD)
                dk_ref[prv, cs] += dkcats[kh][:BLK]
                dv_ref[prv, cs] += dvcats[kh][:BLK]

    full_kv = pl.BlockSpec((T, KV_W), lambda n: (0, 0))
    return pl.pallas_call(
        body, grid=(NB,),
        in_specs=_attn_specs() + [pl.BlockSpec((BLK, D_MODEL), lambda n: (n, 0))],
        out_specs=[pl.BlockSpec((BLK, D_MODEL), lambda n: (n, 0)), full_kv, full_kv,
                   pl.BlockSpec((8, 128), lambda n: (0, 0))],
        out_shape=[SDS((T, D_MODEL), F32), SDS((T, KV_W), F32), SDS((T, KV_W), F32), SDS((8, 128), F32)],
        name="attn_bwd")(proj, proj, proj, proj, proj, proj, proj, sinks, do)


CONV_CB = 512


def _conv_taps(xp, w, rows):
    return (w[0:1] * xp[5:5 + rows] + w[1:2] * xp[6:6 + rows] + w[2:3] * xp[7:7 + rows] + w[3:4] * xp[8:8 + rows])


def _conv_fwd(proj, conv_w, conv_b):
    ncb = CONV_DIM // CONV_CB
    cb0 = C_XBC // CONV_CB

    def body(tail_ref, cur_ref, w_ref, b_ref, o_ref):
        n = pl.program_id(1)
        tail = jnp.where(n > 0, tail_ref[...], 0.0)
        xp = jnp.concatenate([tail, cur_ref[...]], axis=0)
        conv = _conv_taps(xp, w_ref[...], BLK) + b_ref[...]
        row = n * BLK + lax.broadcasted_iota(jnp.int32, (BLK, 1), 0)
        o_ref[...] = jnp.where(row >= PAD, _silu(conv), 0.0)

    return pl.pallas_call(
        body, grid=(ncb, NB),
        in_specs=[pl.BlockSpec((8, CONV_CB), lambda j, n: (jnp.maximum(n * (BLK // 8) - 1, 0), cb0 + j)),
                  pl.BlockSpec((BLK, CONV_CB), lambda j, n: (n, cb0 + j)),
                  pl.BlockSpec((8, CONV_CB), lambda j, n: (0, j)),
                  pl.BlockSpec((1, CONV_CB), lambda j, n: (0, j))],
        out_specs=pl.BlockSpec((BLK, CONV_CB), lambda j, n: (n, j)),
        out_shape=SDS((T, CONV_DIM), F32), name="conv_fwd")(proj, proj, conv_w, conv_b)


def _conv_bwd(proj, conv_w, conv_b, dact, ch0, name):
    width = dact.shape[1]
    ncb = width // CONV_CB
    cb0 = (C_XBC + ch0) // CONV_CB
    wb0 = ch0 // CONV_CB
    last8 = T // 8 - 1

    def body(tail_ref, cur_ref, nxt_ref, w_ref, b_ref, dcur_ref, dnxt_ref, dx_ref, dw_ref, db_ref):
        n = pl.program_id(1)
        w = w_ref[...]
        tail = jnp.where(n > 0, tail_ref[...], 0.0)
        xp = jnp.concatenate([tail, cur_ref[...], nxt_ref[...]], axis=0)
        conv = _conv_taps(xp, w, BLK + 8) + b_ref[...]
        dext = jnp.concatenate([dcur_ref[...], jnp.where(n < NB - 1, dnxt_ref[...], 0.0)], axis=0)
        row = n * BLK + lax.broadcasted_iota(jnp.int32, (BLK + 8, 1), 0)
        dconv = jnp.where(row >= PAD, dext * _dsilu(conv), 0.0)
        dx = (w[0:1] * dconv[3:3 + BLK] + w[1:2] * dconv[2:2 + BLK] + w[2:3] * dconv[1:1 + BLK]
              + w[3:4] * dconv[0:BLK])
        dx_ref[...] = dx.astype(BF16)
        dc = dconv[0:BLK]
        dws = [jnp.sum(dc * xp[5 + k:5 + k + BLK], axis=0, keepdims=True) for k in range(4)]
        dwp = jnp.concatenate(dws + [jnp.zeros((4, CONV_CB), F32)], axis=0)
        dbp = jnp.sum(dc, axis=0, keepdims=True)

        @pl.when(n == 0)
        def _():
            dw_ref[...] = dwp
            db_ref[...] = jnp.concatenate([dbp, jnp.zeros((7, CONV_CB), F32)], axis=0)

        @pl.when(n > 0)
        def _():
            dw_ref[...] += dwp
            db_ref[0:1, :] += dbp

    return pl.pallas_call(
        body, grid=(ncb, NB),
        in_specs=[pl.BlockSpec((8, CONV_CB), lambda j, n: (jnp.maximum(n * (BLK // 8) - 1, 0), cb0 + j)),
                  pl.BlockSpec((BLK, CONV_CB), lambda j, n: (n, cb0 + j)),
                  pl.BlockSpec((8, CONV_CB), lambda j, n: (jnp.minimum((n + 1) * (BLK // 8), last8), cb0 + j)),
                  pl.BlockSpec((8, CONV_CB), lambda j, n: (0, wb0 + j)),
                  pl.BlockSpec((1, CONV_CB), lambda j, n: (0, wb0 + j)),
                  pl.BlockSpec((BLK, CONV_CB), lambda j, n: (n, j)),
                  pl.BlockSpec((8, CONV_CB), lambda j, n: (jnp.minimum((n + 1) * (BLK // 8), last8), j))],
        out_specs=[pl.BlockSpec((BLK, CONV_CB), lambda j, n: (n, j)),
                   pl.BlockSpec((8, CONV_CB), lambda j, n: (0, j)),
                   pl.BlockSpec((8, CONV_CB), lambda j, n: (0, j))],
        out_shape=[SDS((T, width), BF16), SDS((8, width), F32), SDS((8, width), F32)],
        name=name)(proj, proj, proj, conv_w, conv_b, dact, dact)


HPG = SSM_HEADS // SSM_GROUPS


def _iota(shape, dim):
    return lax.broadcasted_iota(jnp.int32, shape, dim)


def _mm(a, b, ca=1, cb=0):
    return lax.dot_general(a.astype(BF16), b.astype(BF16), (((ca,), (cb,)), ((), ())), preferred_element_type=F32)


def _split3(v):
    hi = v.astype(BF16)
    r1 = v - hi.astype(F32)
    mid = r1.astype(BF16)
    lo = (r1 - mid.astype(F32)).astype(BF16)
    return hi, mid, lo


def _sel_r(parts, onehot, ca=1, cb=0):
    out = lax.dot_general(parts[0], onehot, (((ca,), (cb,)), ((), ())), preferred_element_type=F32)
    for p in parts[1:]:
        out = out + lax.dot_general(p, onehot, (((ca,), (cb,)), ((), ())), preferred_element_type=F32)
    return out


def _sel_l(onehot, parts):
    out = jnp.dot(onehot, parts[0], preferred_element_type=F32)
    for p in parts[1:]:
        out = out + jnp.dot(onehot, p, preferred_element_type=F32)
    return out


def _rows8(*rows):
    r = _iota((8, rows[0].shape[1]), 0)
    out = jnp.zeros((8, rows[0].shape[1]), F32)
    for k, v in enumerate(rows):
        out = jnp.where(r == k, v, out)
    return out


def _ssd_forward(x, z, bm, cm, dt_raw, st_prev, dtb, alog, dskip, gn, g, cst_scr):
    li, si = _iota((BLK, BLK), 0), _iota((BLK, BLK), 1)
    dt_all = jax.nn.softplus(dt_raw + dtb)
    a_row = -jnp.exp(alog)
    a_all = dt_all * a_row
    cs_all = _sel_l((li >= si).astype(BF16), _split3(a_all))
    cs_parts = _split3(cs_all)
    spread = (_iota((BLK, GRP_W), 0) == g * HPG + jnp.right_shift(_iota((BLK, GRP_W), 1), 6)).astype(BF16)
    dt_e = _sel_r(_split3(dt_all), spread)
    cs_e = _sel_r(cs_parts, spread)
    d_e = _sel_r(_split3(_rows8(dskip)), spread)[0:1]
    cs_last_e = jnp.sum(jnp.where(_iota((BLK, GRP_W), 0) == BLK - 1, cs_e, 0.0), axis=0, keepdims=True)
    p_e = jnp.exp(cs_e)
    w_e = jnp.exp(cs_last_e - cs_e)
    cd_e = jnp.exp(cs_last_e)
    xr = x * dt_e
    cst_scr[...] = cs_all.T
    cst_g = cst_scr[pl.ds(pl.multiple_of(g * HPG, HPG), HPG), :]
    own = jnp.right_shift(_iota((HPG, HPG * BLK), 1), 7) == _iota((HPG, HPG * BLK), 0)
    ownf = own.astype(F32)
    q_rows = [ownf, ownf, ownf] + [jnp.where(own, jnp.concatenate([p.astype(F32)] * HPG, axis=1), 0.0)
                                   for p in _split3(cst_g)]
    q2 = jnp.concatenate(q_rows + [jnp.zeros((BLK - 6 * HPG, HPG * BLK), F32)], axis=0).astype(BF16)
    lane1 = _iota((1, BLK), 1)
    p2 = jnp.where((lane1 >= 3 * HPG) & (lane1 < 6 * HPG), -1.0, 0.0)
    for k, part in enumerate(cs_parts):
        pick = ((li == g * HPG + si - k * HPG) & (si >= k * HPG) & (si < (k + 1) * HPG)).astype(BF16)
        p2 = p2 + jnp.dot(part, pick, preferred_element_type=F32)
    dmat = jnp.dot(p2.astype(BF16), q2, preferred_element_type=F32)
    causal = _iota((BLK, HPG * BLK), 0) >= jnp.bitwise_and(_iota((BLK, HPG * BLK), 1), BLK - 1)
    lam = jnp.exp(jnp.where(causal, dmat, NEG))
    gmat = _mm(cm, bm, 1, 1)
    m_all = lam * jnp.concatenate([gmat] * HPG, axis=1)
    mb = m_all.astype(BF16)
    lo = _iota((BLK, BLK), 1) < HEAD
    xrb = xr.astype(BF16)
    zero = jnp.zeros((BLK, BLK), BF16)
    bds, yd = [], []
    for i in range(HPG // 2):
        t = xrb[:, BLK * i:BLK * (i + 1)]
        bd = jnp.concatenate([jnp.where(lo, t, zero), jnp.where(lo, zero, t)], axis=0)
        bds.append(bd)
        yd.append(jnp.dot(mb[:, 2 * BLK * i:2 * BLK * (i + 1)], bd, preferred_element_type=F32))
    cs_st = _mm(cm, st_prev)
    y = jnp.concatenate(yd, axis=1) + cs_st * p_e + d_e * x
    xrw = xr * w_e
    st_new = cd_e * st_prev + _mm(bm, xrw, 0, 0)
    yz = y * _silu(z)
    rn = lax.rsqrt(jnp.sum(yz * yz, axis=1, keepdims=True) / GRP_W + EPS)
    return dict(out=yz * rn * gn, st_new=st_new, dt_all=dt_all, a_row=a_row, dt_e=dt_e, d_e=d_e, p_e=p_e, w_e=w_e,
                cd_e=cd_e, xr=xr, xrw=xrw, lam=lam, m_all=m_all, mb=mb, bds=bds, cs_st=cs_st, y=y, yz=yz, rn=rn, lo=lo)


def _ssd_backward(f, x, z, bm, cm, dt_raw, st_prev, dtb, gn, g, dout, dst_next, cst_scr):
    li, si = _iota((BLK, BLK), 0), _iota((BLK, BLK), 1)
    yz, rn, y, p_e, w_e, cd_e, xr = f["yz"], f["rn"], f["y"], f["p_e"], f["w_e"], f["cd_e"], f["xr"]
    dgn = jnp.sum(dout * yz * rn, axis=0, keepdims=True)
    t = dout * gn
    dyz = rn * t - yz * (rn * rn * rn) * (jnp.sum(yz * t, axis=1, keepdims=True) / GRP_W)
    dy = dyz * _silu(z)
    dz = dyz * y * _dsilu(z)
    dx = f["d_e"] * dy
    dd_e = jnp.sum(dy * x, axis=0, keepdims=True)
    dcsst = dy * p_e
    dp_e = dy * f["cs_st"]
    dcm = _mm(dcsst, st_prev, 1, 1)
    dst_prev = _mm(cm, dcsst, 0, 0) + cd_e * dst_next
    dcd_e = jnp.sum(dst_next * st_prev, axis=0, keepdims=True)
    dbm = _mm(f["xrw"], dst_next, 1, 1)
    dxrw = _mm(bm, dst_next)
    dxr = dxrw * w_e
    dw_e = dxrw * xr
    dyb = dy.astype(BF16)
    dms, dxr_d = [], []
    for i in range(HPG // 2):
        dyp = dyb[:, BLK * i:BLK * (i + 1)]
        dms.append(lax.dot_general(dyp, f["bds"][i], (((1,), (1,)), ((), ())), preferred_element_type=F32))
        r = lax.dot_general(f["mb"][:, 2 * BLK * i:2 * BLK * (i + 1)], dyp, (((0,), (0,)), ((), ())),
                            preferred_element_type=F32)
        dxr_d.append(jnp.where(f["lo"], r[0:BLK], r[BLK:2 * BLK]))
    dm_all = jnp.concatenate(dms, axis=1)
    dxr = dxr + jnp.concatenate(dxr_d, axis=1)
    dlg = dm_all * f["lam"]
    dg = dlg[:, 0:BLK]
    for j in range(1, HPG):
        dg = dg + dlg[:, BLK * j:BLK * (j + 1)]
    dcm = dcm + _mm(dg, bm)
    dbm = dbm + _mm(dg, cm, 0, 0)
    q_all = dm_all * f["m_all"]
    col_sums = jnp.sum(q_all, axis=0, keepdims=True)
    cst_scr[...] = jnp.zeros_like(cst_scr)
    cst_scr[pl.ds(pl.multiple_of(g * HPG, HPG), HPG), :] = _rows8(
        *[col_sums[:, BLK * j:BLK * (j + 1)] for j in range(HPG)])
    dcs = -cst_scr[...].T
    for j in range(HPG):
        dcs = dcs + jnp.where(si == g * HPG + j,
                              jnp.sum(q_all[:, BLK * j:BLK * (j + 1)], axis=1, keepdims=True), 0.0)
    unspread = (_iota((GRP_W, BLK), 1) == g * HPG + jnp.right_shift(_iota((GRP_W, BLK), 0), 6)).astype(BF16)
    dww = dw_e * w_e
    per_head = _sel_r(_split3(jnp.concatenate([dp_e * p_e - dww, dxr * x], axis=0)), unspread)
    last = _sel_r(_split3(_rows8(jnp.sum(dww, axis=0, keepdims=True) + dcd_e * cd_e, dd_e)), unspread)
    dcs = dcs + per_head[0:BLK] + jnp.where(li == BLK - 1, last[0:1], 0.0)
    da = _sel_l((si >= li).astype(BF16), _split3(dcs))
    ddt_all = da * f["a_row"] + per_head[BLK:2 * BLK]
    dalog = jnp.sum(da * f["dt_all"], axis=0, keepdims=True) * f["a_row"]
    dx = dx + dxr * f["dt_e"]
    ddt_raw = ddt_all * jax.nn.sigmoid(dt_raw + dtb)
    ddtb = jnp.sum(ddt_raw, axis=0, keepdims=True)
    ddskip = last[1:2]
    return dict(dx=dx, dz=dz, dbm=dbm, dcm=dcm, ddt_raw=ddt_raw, dst_prev=dst_prev, ddtb=ddtb, dalog=dalog,
                ddskip=ddskip, dgn=dgn)


def _ssd_in_specs(rev):
    cidx = (lambda c: NB - 1 - c) if rev else (lambda c: c)
    return [
        pl.BlockSpec((BLK, GRP_W), lambda g, c: (cidx(c), g)),
        pl.BlockSpec((BLK, SSM_STATE), lambda g, c: (cidx(c), SSM_INNER // SSM_STATE + g)),
        pl.BlockSpec((BLK, SSM_STATE), lambda g, c: (cidx(c), SSM_INNER // SSM_STATE + SSM_GROUPS + g)),
        pl.BlockSpec((BLK, 128), lambda g, c: (cidx(c), C_DT // 128)),
        pl.BlockSpec((BLK, GRP_W), lambda g, c: (cidx(c), C_ZS // GRP_W + g)),
        pl.BlockSpec((1, 128), lambda g, c: (0, 0)),
        pl.BlockSpec((1, 128), lambda g, c: (0, 0)),
        pl.BlockSpec((1, 128), lambda g, c: (0, 0)),
        pl.BlockSpec((1, GRP_W), lambda g, c: (0, g)),
    ]


def _ssd_fwd(xbc_act, proj, dt_bias, a_log, d_skip, g_norm):
    def body(xs_ref, b_ref, c_ref, dt_ref, z_ref, dtb_ref, al_ref, dsk_ref, gn_ref, y_ref, st_ref, s_scr, cst_scr):
        g = pl.program_id(0)
        c = pl.program_id(1)

        @pl.when(c == 0)
        def _():
            s_scr[...] = jnp.zeros_like(s_scr)

        st_prev = s_scr[...]
        st_ref[0, 0] = st_prev
        f = _ssd_forward(xs_ref[...], z_ref[...], b_ref[...], c_ref[...], dt_ref[...], st_prev, dtb_ref[...],
                         al_ref[...], dsk_ref[...], gn_ref[...], g, cst_scr)
        y_ref[...] = f["out"].astype(BF16)
        s_scr[...] = f["st_new"]

    return pl.pallas_call(
        body, grid=(SSM_GROUPS, NB), in_specs=_ssd_in_specs(False),
        out_specs=[pl.BlockSpec((BLK, GRP_W), lambda g, c: (c, g)),
                   pl.BlockSpec((1, 1, SSM_STATE, GRP_W), lambda g, c: (g, c, 0, 0))],
        out_shape=[SDS((T, SSM_INNER), BF16), SDS((SSM_GROUPS, NB, SSM_STATE, GRP_W), F32)],
        scratch_shapes=[pltpu.VMEM((SSM_STATE, GRP_W), F32), pltpu.VMEM((BLK, BLK), F32)],
        compiler_params=_cparams(),
        name="ssd_fwd")(xbc_act, xbc_act, xbc_act, proj, proj, dt_bias, a_log, d_skip, g_norm)


def _ssd_bwd(xbc_act, proj, dt_bias, a_log, d_skip, g_norm, states, dy):
    def body(xs_ref, b_ref, c_ref, dt_ref, z_ref, dtb_ref, al_ref, dsk_ref, gn_ref, st_ref, dy_ref,
             dxs_ref, db_ref, dc_ref, ddt_ref, dz_ref, ddtb_ref, dal_ref, ddsk_ref, dgn_ref, ds_scr, cst_scr):
        g = pl.program_id(0)
        c = pl.program_id(1)

        @pl.when(c == 0)
        def _():
            ds_scr[...] = jnp.zeros_like(ds_scr)
            dgn_ref[...] = jnp.zeros_like(dgn_ref)

        @pl.when((c == 0) & (g == 0))
        def _():
            ddtb_ref[...] = jnp.zeros_like(ddtb_ref)
            dal_ref[...] = jnp.zeros_like(dal_ref)
            ddsk_ref[...] = jnp.zeros_like(ddsk_ref)

        x, z, bm, cm, dt_raw, st_prev = xs_ref[...], z_ref[...], b_ref[...], c_ref[...], dt_ref[...], st_ref[0, 0]
        f = _ssd_forward(x, z, bm, cm, dt_raw, st_prev, dtb_ref[...], al_ref[...], dsk_ref[...], gn_ref[...], g,
                         cst_scr)
        d = _ssd_backward(f, x, z, bm, cm, dt_raw, st_prev, dtb_ref[...], gn_ref[...], g, dy_ref[...], ds_scr[...],
                          cst_scr)
        dxs_ref[...] = d["dx"]
        dz_ref[...] = d["dz"].astype(BF16)
        ds_scr[...] = d["dst_prev"]
        db_ref[...] = d["dbm"]
        dc_ref[...] = d["dcm"]
        ddt_ref[...] = d["ddt_raw"]
        dgn_ref[0:1, :] += d["dgn"]
        ddtb_ref[0:1, :] += d["ddtb"]
        dal_ref[0:1, :] += d["dalog"]
        ddsk_ref[0:1, :] += d["ddskip"]

    rc = lambda c: NB - 1 - c
    small = pl.BlockSpec((8, 128), lambda g, c: (0, 0))
    return pl.pallas_call(
        body, grid=(SSM_GROUPS, NB),
        in_specs=_ssd_in_specs(True) + [
            pl.BlockSpec((1, 1, SSM_STATE, GRP_W), lambda g, c: (g, rc(c), 0, 0)),
            pl.BlockSpec((BLK, GRP_W), lambda g, c: (rc(c), g))],
        out_specs=[pl.BlockSpec((BLK, GRP_W), lambda g, c: (rc(c), g)),
                   pl.BlockSpec((BLK, SSM_STATE), lambda g, c: (rc(c), g)),
                   pl.BlockSpec((BLK, SSM_STATE), lambda g, c: (rc(c), g)),
                   pl.BlockSpec((BLK, 128), lambda g, c: (rc(c), g)),
                   pl.BlockSpec((BLK, GRP_W), lambda g, c: (rc(c), g)),
                   small, small, small,
                   pl.BlockSpec((8, GRP_W), lambda g, c: (0, g))],
        out_shape=[SDS((T, SSM_INNER), F32), SDS((T, GRP_W), F32), SDS((T, GRP_W), F32), SDS((T, GRP_W), F32),
                   SDS((T, SSM_INNER), BF16), SDS((8, 128), F32), SDS((8, 128), F32), SDS((8, 128), F32),
                   SDS((8, SSM_INNER), F32)],
        scratch_shapes=[pltpu.VMEM((SSM_STATE, GRP_W), F32), pltpu.VMEM((BLK, BLK), F32)],
        compiler_params=_cparams(),
        name="ssd_bwd")(xbc_act, xbc_act, xbc_act, proj, proj, dt_bias, a_log, d_skip, g_norm, states, dy)


def _post_a(o, proj, sn, w_att, w_ssm, w_o):
    def body(o_ref, za_ref, ga_ref, gs_ref, sn_ref, wa_ref, ws_ref, wo_ref, a_ref, mg_ref, ya_ref, ys_ref, out_ref):
        a = (o_ref[...] * _silu(za_ref[...])).astype(BF16)
        a_ref[...] = a
        ya = jnp.dot(a, wa_ref[...], preferred_element_type=F32)
        ys = jnp.dot(sn_ref[...], ws_ref[...], preferred_element_type=F32)
        ya_ref[...] = ya
        ys_ref[...] = ys
        mg = (jax.nn.sigmoid(ga_ref[...]) * ya + jax.nn.sigmoid(gs_ref[...]) * ys).astype(BF16)
        mg_ref[...] = mg
        out_ref[...] = jnp.dot(mg, wo_ref[...], preferred_element_type=F32)

    row = pl.BlockSpec((BLK, D_MODEL), lambda i: (i, 0))
    pcol = lambda c0: pl.BlockSpec((BLK, D_MODEL), lambda i: (i, c0 // D_MODEL))
    full = lambda r: pl.BlockSpec((r, D_MODEL), lambda i: (0, 0))
    return pl.pallas_call(
        body, grid=(NB,),
        in_specs=[row, pcol(C_ZA), pcol(C_GA), pcol(C_GS), pl.BlockSpec((BLK, SSM_INNER), lambda i: (i, 0)),
                  full(D_MODEL), full(SSM_INNER), full(D_MODEL)],
        out_specs=[row, row, row, row, row],
        out_shape=[SDS((T, D_MODEL), BF16), SDS((T, D_MODEL), BF16), SDS((T, D_MODEL), F32), SDS((T, D_MODEL), F32),
                   SDS((T, D_MODEL), F32)],
        compiler_params=_cparams(), name="post_a")(o, proj, proj, proj, sn, w_att, w_ssm, w_o)


def _post_b(out, h, tgt, proj, ya, ys, o, g_post, w_att, w_ssm, w_o):
    def body(out_ref, h_ref, t_ref, za_ref, ga_ref, gs_ref, ya_ref, ys_ref, o_ref, gp_ref, wa_ref, ws_ref, wo_ref,
             loss_ref, dres_ref, dout_ref, dya_ref, dys_ref, dga_ref, dgs_ref, do_ref, dza_ref, dsn_ref, dgp_ref):
        i = pl.program_id(0)
        x = out_ref[...]
        gp = gp_ref[...]
        r = lax.rsqrt(jnp.mean(x * x, axis=-1, keepdims=True) + EPS)
        row = i * BLK + lax.broadcasted_iota(jnp.int32, (BLK, 1), 0)
        res = h_ref[...] + jnp.where(row >= PAD, x * r * gp, 0.0)
        live = row >= PAD + N_META
        err = jnp.where(live, res - t_ref[...], 0.0)
        lpart = 0.5 * jnp.sum(jnp.sum(err * err, axis=1, keepdims=True) / D_MODEL, axis=0, keepdims=True)
        dres = err / D_MODEL
        dres_ref[...] = dres
        gpart = jnp.sum(dres * x * r, axis=0, keepdims=True)

        @pl.when(i == 0)
        def _():
            loss_ref[...] = jnp.zeros_like(loss_ref)
            dgp_ref[...] = jnp.zeros_like(dgp_ref)

        loss_ref[...] += jnp.broadcast_to(lpart, loss_ref.shape)
        dgp_ref[0:1, :] += gpart
        gd = gp * dres
        dout = (r * gd - x * (r * r * r) * jnp.mean(x * gd, axis=-1, keepdims=True)).astype(BF16)
        dout_ref[...] = dout
        dmg = lax.dot_general(dout, wo_ref[...], (((1,), (1,)), ((), ())), preferred_element_type=F32)
        sga = jax.nn.sigmoid(ga_ref[...])
        sgs = jax.nn.sigmoid(gs_ref[...])
        dya = (dmg * sga).astype(BF16)
        dys = (dmg * sgs).astype(BF16)
        dya_ref[...] = dya
        dys_ref[...] = dys
        dga_ref[...] = (dmg * ya_ref[...] * sga * (1.0 - sga)).astype(BF16)
        dgs_ref[...] = (dmg * ys_ref[...] * sgs * (1.0 - sgs)).astype(BF16)
        da = lax.dot_general(dya, wa_ref[...], (((1,), (1,)), ((), ())), preferred_element_type=F32)
        za = za_ref[...]
        do_ref[...] = da * _silu(za)
        dza_ref[...] = (da * o_ref[...] * _dsilu(za)).astype(BF16)
        dsn_ref[...] = lax.dot_general(dys, ws_ref[...], (((1,), (1,)), ((), ())), preferred_element_type=F32)

    row = pl.BlockSpec((BLK, D_MODEL), lambda i: (i, 0))
    pcol = lambda c0: pl.BlockSpec((BLK, D_MODEL), lambda i: (i, c0 // D_MODEL))
    full = lambda r: pl.BlockSpec((r, D_MODEL), lambda i: (0, 0))
    small = pl.BlockSpec((8, D_MODEL), lambda i: (0, 0))
    return pl.pallas_call(
        body, grid=(NB,),
        in_specs=[row, row, row, pcol(C_ZA), pcol(C_GA), pcol(C_GS), row, row, row,
                  pl.BlockSpec((1, D_MODEL), lambda i: (0, 0)), full(D_MODEL), full(SSM_INNER), full(D_MODEL)],
        out_specs=[pl.BlockSpec((8, 128), lambda i: (0, 0)), row, row, row, row, row, row, row, row,
                   pl.BlockSpec((BLK, SSM_INNER), lambda i: (i, 0)), small],
        out_shape=[SDS((8, 128), F32), SDS((T, D_MODEL), F32), SDS((T, D_MODEL), BF16), SDS((T, D_MODEL), BF16),
                   SDS((T, D_MODEL), BF16), SDS((T, D_MODEL), BF16), SDS((T, D_MODEL), BF16), SDS((T, D_MODEL), F32),
                   SDS((T, D_MODEL), BF16), SDS((T, SSM_INNER), F32), SDS((8, D_MODEL), F32)],
        compiler_params=_cparams(), name="post_b")(out, h, tgt, proj, proj, proj, ya, ys, o, g_post, w_att, w_ssm, w_o)


def _assemble(dq, dza, dga, dgs, dzs, dxx, dxb, dxc, dk, dv, ddt4):
    def body(dq_ref, dza_ref, dga_ref, dgs_ref, dzs_ref, dxx_ref, dxb_ref, dxc_ref, dk_ref, dv_ref, ddt_ref, o_ref):
        o_ref[:, C_Q:C_Q + D_MODEL] = dq_ref[...].astype(BF16)
        o_ref[:, C_ZA:C_ZA + D_MODEL] = dza_ref[...]
        o_ref[:, C_GA:C_GA + D_MODEL] = dga_ref[...]
        o_ref[:, C_GS:C_GS + D_MODEL] = dgs_ref[...]
        o_ref[:, C_ZS:C_ZS + SSM_INNER] = dzs_ref[...]
        o_ref[:, C_XBC:C_XBC + SSM_INNER] = dxx_ref[...]
        o_ref[:, C_XBC + SSM_INNER:C_XBC + SSM_INNER + GRP_W] = dxb_ref[...]
        o_ref[:, C_XBC + SSM_INNER + GRP_W:C_XBC + CONV_DIM] = dxc_ref[...]
        o_ref[:, C_K:C_K + KV_W] = dk_ref[...].astype(BF16)
        o_ref[:, C_V:C_V + KV_W] = dv_ref[...].astype(BF16)
        d4 = ddt_ref[...]
        o_ref[:, C_DT:C_DT + 128] = (d4[:, 0:128] + d4[:, 128:256] + d4[:, 256:384] + d4[:, 384:512]).astype(BF16)

    spec = lambda w: pl.BlockSpec((BLK, w), lambda i: (i, 0))
    ins = [dq, dza, dga, dgs, dzs, dxx, dxb, dxc, dk, dv, ddt4]
    return pl.pallas_call(
        body, grid=(NB,), in_specs=[spec(a.shape[1]) for a in ins], out_specs=spec(PW),
        out_shape=SDS((T, PW), BF16), name="assemble")(*ins)


def _adamw_math(w, g, m, v):
    m = ADAM_B1 * m + (1.0 - ADAM_B1) * g
    v = ADAM_B2 * v + (1.0 - ADAM_B2) * (g * g)
    m_hat = m / (1.0 - ADAM_B1 ** ADAM_STEP)
    v_hat = v / (1.0 - ADAM_B2 ** ADAM_STEP)
    delta = -ADAM_LR * (m_hat / (jnp.sqrt(v_hat) + ADAM_EPS) + ADAM_WD * w)
    return delta, m, v


def _sum_adamw(recv, w, m, v, tc, name):
    rows, cols = w.shape
    assert cols % tc == 0

    def body(r_ref, w_ref, m_ref, v_ref, g_ref, d_ref, nm_ref, nv_ref):
        g = r_ref[0].astype(F32)
        for d in range(1, N_DEV):
            g = g + r_ref[d].astype(F32)
        g_ref[...] = g
        delta, nm, nv = _adamw_math(w_ref[...], g, m_ref[...], v_ref[...])
        d_ref[...] = delta
        nm_ref[...] = nm
        nv_ref[...] = nv

    blk = pl.BlockSpec((rows, tc), lambda i: (0, i))
    return pl.pallas_call(
        body, grid=(cols // tc,),
        in_specs=[pl.BlockSpec((N_DEV, rows, tc), lambda i: (0, 0, i)), blk, blk, blk],
        out_specs=[blk, blk, blk, blk], out_shape=[SDS((rows, cols), F32)] * 4,
        compiler_params=_cparams(), name=name)(recv, w, m, v)


ROW_GPRE, ROW_CONVB, ROW_DTB, ROW_ALOG, ROW_DSKIP, ROW_SINK, ROW_GSSM, ROW_GPOST = 0, 1, 4, 5, 6, 7, 8, 10
REP_ROWS, ROW_CONVW, ROW_META, SM_ROWS = 16, 16, 24, 40
CW_SHARD = CONV_DIM // N_DEV
META_SHARD = D_MODEL // N_DEV


def _small_pack(dgpre, dbx, dbb, dbc, ddtb, dal, ddsk, dsink, dgn, dgp, dwx, dwb, dwc, dh):
    def body(dgpre_ref, dbx_ref, dbb_ref, dbc_ref, ddtb_ref, dal_ref, ddsk_ref, dsink_ref, dgn_ref, dgp_ref,
             dwx_ref, dwb_ref, dwc_ref, dh_ref, o_ref, rep):
        rep[...] = jnp.zeros_like(rep)
        rep[ROW_GPRE:ROW_GPRE + 1, :] = dgpre_ref[0:1, :]
        rep[ROW_CONVB:ROW_CONVB + 1, :] = dbx_ref[0:1, 0:1024]
        rep[ROW_CONVB + 1:ROW_CONVB + 2, :] = dbx_ref[0:1, 1024:2048]
        rep[ROW_CONVB + 2:ROW_CONVB + 3, 0:512] = dbb_ref[0:1, :]
        rep[ROW_CONVB + 2:ROW_CONVB + 3, 512:1024] = dbc_ref[0:1, :]
        rep[ROW_DTB:ROW_DTB + 1, 0:128] = ddtb_ref[0:1, :]
        rep[ROW_ALOG:ROW_ALOG + 1, 0:128] = dal_ref[0:1, :]
        rep[ROW_DSKIP:ROW_DSKIP + 1, 0:128] = ddsk_ref[0:1, :]
        rep[ROW_SINK:ROW_SINK + 1, 0:128] = dsink_ref[0:1, :]
        rep[ROW_GSSM:ROW_GSSM + 1, :] = dgn_ref[0:1, 0:1024]
        rep[ROW_GSSM + 1:ROW_GSSM + 2, :] = dgn_ref[0:1, 1024:2048]
        rep[ROW_GPOST:ROW_GPOST + 1, :] = dgp_ref[0:1, :]
        cw = jnp.concatenate([dwx_ref[...], dwb_ref[...], dwc_ref[...]], axis=1)
        mh = dh_ref[...]
        o_ref[...] = jnp.zeros_like(o_ref)
        for p in range(N_DEV):
            o_ref[p, 0:REP_ROWS, :] = rep[...]
            o_ref[p, ROW_CONVW:ROW_CONVW + 8, 0:CW_SHARD] = cw[:, p * CW_SHARD:(p + 1) * CW_SHARD]
            o_ref[p, ROW_META:ROW_META + N_META, 0:META_SHARD] = mh[:, p * META_SHARD:(p + 1) * META_SHARD]

    ins = [dgpre, dbx, dbb, dbc, ddtb, dal, ddsk, dsink, dgn, dgp, dwx, dwb, dwc]
    return pl.pallas_call(
        body, grid=(1,),
        in_specs=[pl.BlockSpec(a.shape, lambda i: (0, 0)) for a in ins]
        + [pl.BlockSpec((N_META, D_MODEL), lambda i: (PAD // N_META, 0))],
        out_specs=pl.BlockSpec((N_DEV, SM_ROWS, 1024), lambda i: (0, 0, 0)),
        out_shape=SDS((N_DEV, SM_ROWS, 1024), F32), scratch_shapes=[pltpu.VMEM((REP_ROWS, 1024), F32)],
        name="small_pack")(*ins, dh)


def _small_finish(recv, params):
    npar = len(params)

    def body(*refs):
        r_ref = refs[0]
        wmv = refs[1:1 + 3 * npar]
        outs = refs[1 + 3 * npar:1 + 7 * npar]
        gs = refs[-1]
        g = r_ref[0]
        for d in range(1, N_DEV):
            g = g + r_ref[d]
        gs[...] = g
        grads = [
            gs[ROW_GPRE:ROW_GPRE + 1, :],
            jnp.concatenate([gs[ROW_CONVB + k:ROW_CONVB + k + 1, :] for k in range(3)], axis=1),
            gs[ROW_DTB:ROW_DTB + 1, 0:SSM_HEADS], gs[ROW_ALOG:ROW_ALOG + 1, 0:SSM_HEADS],
            gs[ROW_DSKIP:ROW_DSKIP + 1, 0:SSM_HEADS], gs[ROW_SINK:ROW_SINK + 1, 0:Q_HEADS],
            jnp.concatenate([gs[ROW_GSSM:ROW_GSSM + 1, :], gs[ROW_GSSM + 1:ROW_GSSM + 2, :]], axis=1),
            gs[ROW_GPOST:ROW_GPOST + 1, :],
            gs[ROW_CONVW:ROW_CONVW + 4, 0:CW_SHARD],
            gs[ROW_META:ROW_META + N_META, 0:META_SHARD]]
        for i in range(npar):
            w_ref, m_ref, v_ref = wmv[3 * i:3 * i + 3]
            delta, nm, nv = _adamw_math(w_ref[...], grads[i], m_ref[...], v_ref[...])
            outs[4 * i][...] = grads[i]
            outs[4 * i + 1][...] = delta
            outs[4 * i + 2][...] = nm
            outs[4 * i + 3][...] = nv

    flat = [a for wmv in params for a in wmv]
    res = pl.pallas_call(
        body, out_shape=[SDS(wmv[0].shape, F32) for wmv in params for _ in range(4)],
        scratch_shapes=[pltpu.VMEM((SM_ROWS, 1024), F32)], name="small_finish")(recv, *flat)
    return [tuple(res[4 * i:4 * i + 4]) for i in range(npar)]


def _slab(ref, px, py, pc):
    return ref.at[4 * px + 2 * py + pc]


def _all_gather(shards):
    na = len(shards)

    def body(*refs):
        ins, outs = refs[:na], refs[na:2 * na]
        send_sems, recv_sems, local_sems = refs[2 * na:]
        x, y, c = lax.axis_index("x"), lax.axis_index("y"), lax.axis_index("c")
        me, sibling = (x, y, c), (x, y, 1 - c)
        chips = [(1 - x, y), (x, 1 - y), (1 - x, 1 - y)]

        def copy(a, k, block, to, src=None):
            dst = _slab(outs[a], *block)
            return pltpu.make_async_remote_copy(
                src_ref=dst if src is None else src, dst_ref=dst, send_sem=send_sems.at[a, k],
                recv_sem=recv_sems.at[a, k], device_id=to, device_id_type=MESH)

        mine = [pltpu.make_async_copy(ins[a], _slab(outs[a], *me), local_sems.at[a]) for a in range(na)]
        for cp in mine:
            cp.start()
        first = []
        for a in range(na):
            first.append(copy(a, 0, me, sibling, src=ins[a]))
            first += [copy(a, 1 + j, me, (*chip, c), src=ins[a]) for j, chip in enumerate(chips)]
        for cp in first:
            cp.start()
        passed = []
        for j, chip in enumerate(chips):
            for a in range(na):
                copy(a, 1 + j, (*chip, c), me).wait_recv()
                cp = copy(a, 4 + j, (*chip, c), sibling)
                cp.start()
                passed.append(cp)
        for a in range(na):
            copy(a, 0, sibling, me).wait_recv()
            for j, chip in enumerate(chips):
                copy(a, 4 + j, (*chip, 1 - c), me).wait_recv()
        for cp in first + passed:
            cp.wait_send()
        for cp in mine:
            cp.wait()

    return pl.pallas_call(
        body, in_specs=[ANY] * na, out_specs=[ANY] * na,
        out_shape=[SDS((N_DEV,) + s.shape, s.dtype) for s in shards],
        scratch_shapes=[pltpu.SemaphoreType.DMA((na, 7)), pltpu.SemaphoreType.DMA((na, 7)),
                        pltpu.SemaphoreType.DMA((na,))],
        name="all_gather")(*shards)


def _exchange(parts):
    na = len(parts)

    def body(*refs):
        ins, outs = refs[:na], refs[na:2 * na]
        send_sems, recv_sems, local_sems = refs[2 * na:]
        x, y, c = lax.axis_index("x"), lax.axis_index("y"), lax.axis_index("c")
        me = (x, y, c)
        mine = [pltpu.make_async_copy(_slab(ins[a], *me), _slab(outs[a], *me), local_sems.at[a]) for a in range(na)]
        for cp in mine:
            cp.start()
        peers = []
        for k in range(1, N_DEV):
            dx, dy, dc = (k >> 2) & 1, (k >> 1) & 1, k & 1
            peers.append(((1 - x) if dx else x, (1 - y) if dy else y, (1 - c) if dc else c))
        sent = []
        for a in range(na):
            for k, peer in enumerate(peers):
                cp = pltpu.make_async_remote_copy(
                    src_ref=_slab(ins[a], *peer), dst_ref=_slab(outs[a], *me), send_sem=send_sems.at[a, k],
                    recv_sem=recv_sems.at[a, k], device_id=peer, device_id_type=MESH)
                cp.start()
                sent.append(cp)
        for a in range(na):
            for k, peer in enumerate(peers):
                pltpu.make_async_remote_copy(
                    src_ref=_slab(ins[a], *peer), dst_ref=_slab(outs[a], *peer), send_sem=send_sems.at[a, k],
                    recv_sem=recv_sems.at[a, k], device_id=peer, device_id_type=MESH).wait_recv()
        for cp in sent:
            cp.wait_send()
        for cp in mine:
            cp.wait()

    return pl.pallas_call(
        body, in_specs=[ANY] * na, out_specs=[ANY] * na,
        out_shape=[SDS(p.shape, p.dtype) for p in parts],
        scratch_shapes=[pltpu.SemaphoreType.DMA((na, 7)), pltpu.SemaphoreType.DMA((na, 7)),
                        pltpu.SemaphoreType.DMA((na,))],
        name="exchange")(*parts)


def _cast_shards(w_in_t, w_att, w_ssm, w_o):
    def body(wi_ref, wa_ref, ws_ref, wo_ref, a_ref, b_ref, c_ref, d_ref):
        a_ref[...] = wi_ref[...].astype(BF16)
        b_ref[...] = wa_ref[...].astype(BF16)
        c_ref[...] = ws_ref[...].astype(BF16)
        d_ref[...] = wo_ref[...].astype(BF16)

    return pl.pallas_call(
        body, out_shape=[SDS(w_in_t.shape, BF16), SDS(w_att.shape, BF16), SDS(w_ssm.shape, BF16), SDS(w_o.shape, BF16)],
        compiler_params=_cparams(), name="cast_shards")(w_in_t, w_att, w_ssm, w_o)


def _pieces():
    out = []
    for r0, c0, w in _SEGS:
        r = r0
        while r < r0 + w:
            d = r // SHARD_IN
            n = min(r0 + w, (d + 1) * SHARD_IN) - r
            out.append((c0 + (r - r0), d, r - d * SHARD_IN, n))
            r += n
    return out


def _to_aligned_t(slabs):
    def body(a_ref, o_ref):
        for (t, d, s, n) in _pieces():
            o_ref[t:t + n, :] = a_ref[d, s:s + n, :]
        o_ref[C_DT + 32:C_DT + 128, :] = jnp.zeros((96, D_MODEL), slabs.dtype)

    return pl.pallas_call(body, out_shape=SDS((PW, D_MODEL), slabs.dtype), compiler_params=_cparams(),
                          name="to_aligned")(slabs)


def _from_aligned_t(g):
    def body(g_ref, o_ref):
        for (t, d, s, n) in _pieces():
            o_ref[d, s:s + n, :] = g_ref[t:t + n, :]

    return pl.pallas_call(body, out_shape=SDS((N_DEV, SHARD_IN, D_MODEL), g.dtype), compiler_params=_cparams(),
                          name="from_aligned")(g)


_SEGS = [
    (R_Q, C_Q, 1024), (R_K, C_K, 256), (R_V, C_V, 256), (R_ZA, C_ZA, 1024), (R_ZS, C_ZS, 2048),
    (R_XBC, C_XBC, 3072), (R_DT, C_DT, 32), (R_GA, C_GA, 1024), (R_GS, C_GS, 1024)]


def _pad_lanes(v, n=128):
    return jnp.pad(v, ((0, 0), (0, n - v.shape[1])))


def _local_step(h, tgt, w_alt, w_att, w_ssm, w_o, g_pre, conv_w8, conv_b, dt_bias, a_log, d_skip, sinks,
                g_ssm, g_post):
    dtb, al, dsk, snk = _pad_lanes(dt_bias), _pad_lanes(a_log), _pad_lanes(d_skip), _pad_lanes(sinks)
    u = _norm_u(h, g_pre)
    proj = _matmul(u, w_alt, "nt", F32, 1088, 896, D_MODEL, "in_proj")
    o = _attn_fwd(proj, snk)
    xbc_act = _conv_fwd(proj, conv_w8, conv_b)
    sn, states = _ssd_fwd(xbc_act, proj, dtb, al, dsk, g_ssm)
    a_in, mg, ya, ys, out = _post_a(o, proj, sn, w_att, w_ssm, w_o)
    (loss, dres, dout, dya, dys, dga, dgs, do, dza, dsn, dgp) = _post_b(
        out, h, tgt, proj, ya, ys, o, g_post, w_att, w_ssm, w_o)
    dxs, dbm, dcm, ddt4, dzs, ddtb, dal, ddsk, dgn = _ssd_bwd(xbc_act, proj, dtb, al, dsk, g_ssm, states, dsn)
    dxx, dwx, dbx = _conv_bwd(proj, conv_w8, conv_b, dxs, 0, "conv_bwd_x")
    dxb, dwb, dbb = _conv_bwd(proj, conv_w8, conv_b, dbm, SSM_INNER, "conv_bwd_b")
    dxc, dwc, dbc = _conv_bwd(proj, conv_w8, conv_b, dcm, SSM_INNER + GRP_W, "conv_bwd_c")
    dq, dk, dv, dsink = _attn_bwd(proj, snk, do)
    dproj = _assemble(dq, dza, dga, dgs, dzs, dxx, dxb, dxc, dk, dv, ddt4)
    du = _matmul(dproj, w_alt, "nn", F32, 1088, D_MODEL, 896, "d_u")
    dw_alt = _matmul(dproj, u, "tn", BF16, 896, D_MODEL, T, "d_w_in")
    dh, dgpre = _norm_bwd(h, g_pre, du, dres)
    dw_att = _matmul(a_in, dya, "tn", BF16, D_MODEL, D_MODEL, T, "d_w_att")
    dw_ssm = _matmul(sn, dys, "tn", BF16, D_MODEL, D_MODEL, T, "d_w_ssm")
    dw_o = _matmul(mg, dout, "tn", BF16, D_MODEL, D_MODEL, T, "d_w_o")
    return dict(
        loss=loss[0, 0], dh=dh, dw_alt=dw_alt, dw_att=dw_att, dw_ssm=dw_ssm, dw_o=dw_o,
        small=(dgpre, dbx, dbb, dbc, ddtb, dal, ddsk, dsink, dgn, dgp, dwx, dwb, dwc))


def kernel(x, meta_tokens, g_pre, w_in, conv_w, conv_b, dt_bias, a_log, d_skip, attn_sinks, g_ssm_norm, w_out_att, w_out_ssm, w_out, g_post, loss_target, m_meta_tokens, m_g_pre, m_w_in, m_conv_w, m_conv_b, m_dt_bias, m_a_log, m_d_skip, m_attn_sinks, m_g_ssm_norm, m_w_out_att, m_w_out_ssm, m_w_out, m_g_post, v_meta_tokens, v_g_pre, v_w_in, v_conv_w, v_conv_b, v_dt_bias, v_a_log, v_d_skip, v_attn_sinks, v_g_ssm_norm, v_w_out_att, v_w_out_ssm, v_w_out, v_g_post):
    w_in_t, m_in_t, v_in_t = jnp.transpose(w_in[0]), jnp.transpose(m_w_in[0]), jnp.transpose(v_w_in[0])
    a_sh, att_sh, ssm_sh, o_sh = _cast_shards(w_in_t, w_out_att[0], w_out_ssm[0], w_out[0])
    cw_sh = jnp.pad(conv_w[0], ((0, 4), (0, 0)))
    a_all, att_all, ssm_all, o_all, meta_all, cw_all = _all_gather([a_sh, att_sh, ssm_sh, o_sh, meta_tokens, cw_sh])
    w_alt = _to_aligned_t(a_all)
    w_att = att_all.reshape(D_MODEL, D_MODEL)
    w_ssm = ssm_all.reshape(SSM_INNER, D_MODEL)
    w_o = o_all.reshape(D_MODEL, D_MODEL)
    meta_full = meta_all.transpose(1, 0, 2).reshape(N_META, D_MODEL)
    conv_w8 = cw_all.transpose(1, 0, 2).reshape(8, CONV_DIM)

    h = jnp.concatenate([jnp.zeros((PAD, D_MODEL), F32), meta_full, x[0]], axis=0)
    tgt = jnp.concatenate([jnp.zeros((PAD + N_META, D_MODEL), F32), loss_target[0]], axis=0)
    r = _local_step(h, tgt, w_alt, w_att, w_ssm, w_o, g_pre, conv_w8, conv_b, dt_bias, a_log, d_skip, attn_sinks,
                    g_ssm_norm, g_post)
    loss = lax.psum(r["loss"], ("x", "y", "c"))
    grad_x = r["dh"][PAD + N_META:][None]

    small8 = _small_pack(*r["small"], r["dh"])
    ra, r_att, r_ssm, r_o, rs = _exchange([
        _from_aligned_t(r["dw_alt"]), r["dw_att"].reshape(N_DEV, 128, D_MODEL),
        r["dw_ssm"].reshape(N_DEV, 256, D_MODEL), r["dw_o"].reshape(N_DEV, 128, D_MODEL), small8])

    res_in = [jnp.transpose(t)[None] for t in _sum_adamw(ra, w_in_t, m_in_t, v_in_t, 128, "adamw_w_in")]
    res_att = [t[None] for t in _sum_adamw(r_att, w_out_att[0], m_w_out_att[0], v_w_out_att[0], 512, "adamw_w_att")]
    res_ssm = [t[None] for t in _sum_adamw(r_ssm, w_out_ssm[0], m_w_out_ssm[0], v_w_out_ssm[0], 512, "adamw_w_ssm")]
    res_o = [t[None] for t in _sum_adamw(r_o, w_out[0], m_w_out[0], v_w_out[0], 512, "adamw_w_o")]
    (res_gpre, res_convb, res_dtb, res_alog, res_dskip, res_sink, res_gssm, res_gpost, res_cw, res_meta) = _small_finish(
        rs, [(g_pre, m_g_pre, v_g_pre), (conv_b, m_conv_b, v_conv_b), (dt_bias, m_dt_bias, v_dt_bias),
             (a_log, m_a_log, v_a_log), (d_skip, m_d_skip, v_d_skip), (attn_sinks, m_attn_sinks, v_attn_sinks),
             (g_ssm_norm, m_g_ssm_norm, v_g_ssm_norm), (g_post, m_g_post, v_g_post),
             (conv_w[0], m_conv_w[0], v_conv_w[0]), (meta_tokens, m_meta_tokens, v_meta_tokens)])
    res_cw = [t[None] for t in res_cw]
    per_weight = [res_meta, res_gpre, res_in, res_cw, res_convb, res_dtb, res_alog, res_dskip, res_sink, res_gssm,
                  res_att, res_ssm, res_o, res_gpost]
    return (loss, grad_x, *[p[0] for p in per_weight], *[p[1] for p in per_weight], *[p[2] for p in per_weight],
            *[p[3] for p in per_weight])
```

```python
import functools
import math

import jax
import jax.numpy as jnp
from jax import lax
from jax.experimental import pallas as pl
from jax.experimental.pallas import tpu as pltpu

F32 = jnp.float32
BF16 = jnp.bfloat16
SDS = jax.ShapeDtypeStruct
HI = lax.Precision.HIGHEST
MESH = pl.DeviceIdType.MESH
ANY = pl.BlockSpec(memory_space=pl.ANY)

N_DEV = 8
D_MODEL = 1024
SEQ = 2048
N_META = 16
BLK = 128
PAD = 112
T = PAD + N_META + SEQ
NB = T // BLK
EPS = 1e-6
HEAD = 64
Q_HEADS = 16
KV_HEADS = 4
GROUP = 4
KV_W = 256
SSM_INNER = 2048
SSM_HEADS = 32
SSM_GROUPS = 4
GRP_W = 512
SSM_STATE = 128
CONV_DIM = 3072
IN_PROJ = 9760
SHARD_IN = IN_PROJ // N_DEV
NEG = -1e30

C_Q, C_ZA, C_GA, C_GS, C_ZS, C_XBC, C_K, C_V, C_DT = 0, 1024, 2048, 3072, 4096, 6144, 9216, 9472, 9728
PW = 9856
R_Q, R_K, R_V, R_ZA, R_ZS, R_XBC, R_DT, R_GA, R_GS = 0, 1024, 1280, 1536, 2560, 4608, 7680, 7712, 8736

ADAM_LR, ADAM_B1, ADAM_B2, ADAM_EPS, ADAM_WD, ADAM_STEP = 0.001, 0.9, 0.999, 1e-08, 0.01, 10

VMEM_LIMIT = 56 * 1024 * 1024


def _cparams():
    return pltpu.CompilerParams(vmem_limit_bytes=VMEM_LIMIT)


def _silu(x):
    return x * jax.nn.sigmoid(x)


def _dsilu(x):
    s = jax.nn.sigmoid(x)
    return s * (1.0 + x * (1.0 - s))


def _matmul(a, b, mode, out_dtype, tm, tn, tk, name):
    if mode == "nn":
        (m, k), n = a.shape, b.shape[1]
        a_spec = pl.BlockSpec((tm, tk), lambda i, j, kk: (i, kk))
        b_spec = pl.BlockSpec((tk, tn), lambda i, j, kk: (kk, j))
        dims = (((1,), (0,)), ((), ()))
    elif mode == "nt":
        (m, k), n = a.shape, b.shape[0]
        a_spec = pl.BlockSpec((tm, tk), lambda i, j, kk: (i, kk))
        b_spec = pl.BlockSpec((tn, tk), lambda i, j, kk: (j, kk))
        dims = (((1,), (1,)), ((), ()))
    else:
        (k, m), n = a.shape, b.shape[1]
        a_spec = pl.BlockSpec((tk, tm), lambda i, j, kk: (kk, i))
        b_spec = pl.BlockSpec((tk, tn), lambda i, j, kk: (kk, j))
        dims = (((0,), (0,)), ((), ()))
    assert m % tm == 0 and n % tn == 0 and k % tk == 0, (a.shape, b.shape, tm, tn, tk)
    nk = k // tk

    def body(a_ref, b_ref, o_ref, acc_ref):
        kk = pl.program_id(2)
        part = lax.dot_general(a_ref[...], b_ref[...], dims, preferred_element_type=F32)

        @pl.when(kk == 0)
        def _():
            acc_ref[...] = part

        @pl.when(kk > 0)
        def _():
            acc_ref[...] += part

        @pl.when(kk == nk - 1)
        def _():
            o_ref[...] = acc_ref[...].astype(out_dtype)

    return pl.pallas_call(
        body, grid=(m // tm, n // tn, nk), in_specs=[a_spec, b_spec],
        out_specs=pl.BlockSpec((tm, tn), lambda i, j, kk: (i, j)),
        out_shape=SDS((m, n), out_dtype), scratch_shapes=[pltpu.VMEM((tm, tn), F32)],
        compiler_params=_cparams(), name=name)(a, b)


def _norm_u(h, g_pre):
    def body(h_ref, g_ref, u_ref):
        x = h_ref[...]
        r = lax.rsqrt(jnp.mean(x * x, axis=-1, keepdims=True) + EPS)
        u_ref[...] = (x * r * g_ref[...]).astype(BF16)

    return pl.pallas_call(
        body, grid=(NB,),
        in_specs=[pl.BlockSpec((BLK, D_MODEL), lambda i: (i, 0)), pl.BlockSpec((1, D_MODEL), lambda i: (0, 0))],
        out_specs=pl.BlockSpec((BLK, D_MODEL), lambda i: (i, 0)),
        out_shape=SDS((T, D_MODEL), BF16), name="norm_u")(h, g_pre)


def _norm_bwd(h, g_pre, du, dres):
    def body(h_ref, g_ref, du_ref, dres_ref, dh_ref, dg_ref):
        i = pl.program_id(0)
        x = h_ref[...]
        g = g_ref[...]
        du_ = du_ref[...]
        r = lax.rsqrt(jnp.mean(x * x, axis=-1, keepdims=True) + EPS)
        gd = g * du_
        dx = r * gd - x * (r * r * r) * jnp.mean(x * gd, axis=-1, keepdims=True)
        dh_ref[...] = dx + dres_ref[...]
        part = jnp.sum(du_ * x * r, axis=0, keepdims=True)

        @pl.when(i == 0)
        def _():
            dg_ref[...] = jnp.zeros_like(dg_ref)

        dg_ref[0:1, :] += part

    row = pl.BlockSpec((BLK, D_MODEL), lambda i: (i, 0))
    return pl.pallas_call(
        body, grid=(NB,),
        in_specs=[row, pl.BlockSpec((1, D_MODEL), lambda i: (0, 0)), row, row],
        out_specs=[row, pl.BlockSpec((8, D_MODEL), lambda i: (0, 0))],
        out_shape=[SDS((T, D_MODEL), F32), SDS((8, D_MODEL), F32)], name="norm_bwd")(h, g_pre, du, dres)


def _lane_pick(row, h):
    lane = lax.broadcasted_iota(jnp.int32, row.shape, 1)
    return jnp.sum(jnp.where(lane == h, row, 0.0), axis=1, keepdims=True)


def _attn_fn(q4s, kcats, vcats, kms, vms, sinks, n):
    r = lax.broadcasted_iota(jnp.int32, (GROUP * BLK, 2 * BLK), 0)
    s = lax.broadcasted_iota(jnp.int32, (GROUP * BLK, 2 * BLK), 1)
    i = jnp.bitwise_and(r, BLK - 1)
    gi = jnp.right_shift(r, 7)
    rel = i - s + BLK
    k_pos = n * BLK - BLK + s
    band_ok = (rel >= 0) & (rel < BLK) & (k_pos >= PAD + N_META)
    relf = rel.astype(F32)
    rm = lax.broadcasted_iota(jnp.int32, (GROUP * BLK, N_META), 0)
    mm = lax.broadcasted_iota(jnp.int32, (GROUP * BLK, N_META), 1)
    meta_ok = (PAD + mm) <= (n * BLK + jnp.bitwise_and(rm, BLK - 1))
    gcol = jnp.right_shift(lax.broadcasted_iota(jnp.int32, (GROUP * BLK, 1), 0), 7)
    outs = []
    for kh in range(KV_HEADS):
        slopes = [2.0 ** (-8.0 * (kh * GROUP + g + 1) / Q_HEADS) for g in range(GROUP)]
        slope = jnp.where(gi == 0, slopes[0], jnp.where(gi == 1, slopes[1], jnp.where(gi == 2, slopes[2], slopes[3])))
        sk = [_lane_pick(sinks, kh * GROUP + g) for g in range(GROUP)]
        sink = jnp.where(gcol == 0, sk[0], jnp.where(gcol == 1, sk[1], jnp.where(gcol == 2, sk[2], sk[3])))
        qb = (q4s[kh] * (HEAD ** -0.5)).astype(BF16)
        sb = lax.dot_general(qb, kcats[kh].astype(BF16), (((1,), (1,)), ((), ())), preferred_element_type=F32)
        sb = jnp.where(band_ok, sb - slope * relf, NEG)
        sm = lax.dot_general(qb, kms[kh].astype(BF16), (((1,), (1,)), ((), ())), preferred_element_type=F32)
        sm = jnp.where(meta_ok, sm, NEG)
        mx = jnp.maximum(jnp.maximum(jnp.max(sb, axis=1, keepdims=True), jnp.max(sm, axis=1, keepdims=True)), sink)
        mx = lax.stop_gradient(mx)
        eb = jnp.exp(sb - mx)
        em = jnp.exp(sm - mx)
        es = jnp.exp(sink - mx)
        inv = 1.0 / (jnp.sum(eb, axis=1, keepdims=True) + jnp.sum(em, axis=1, keepdims=True) + es)
        pb = (eb * inv).astype(BF16)
        pm = (em * inv).astype(BF16)
        o4 = (jnp.dot(pm, vms[kh].astype(BF16), preferred_element_type=F32)
              + jnp.dot(pb, vcats[kh].astype(BF16), preferred_element_type=F32))
        outs.append(o4)
    return outs


def _attn_specs():
    prev = lambda n: jnp.maximum(n - 1, 0)
    return [
        pl.BlockSpec((BLK, D_MODEL), lambda n: (n, C_Q // D_MODEL)),
        pl.BlockSpec((BLK, KV_W), lambda n: (prev(n), C_K // KV_W)),
        pl.BlockSpec((BLK, KV_W), lambda n: (n, C_K // KV_W)),
        pl.BlockSpec((BLK, KV_W), lambda n: (prev(n), C_V // KV_W)),
        pl.BlockSpec((BLK, KV_W), lambda n: (n, C_V // KV_W)),
        pl.BlockSpec((N_META, KV_W), lambda n: (PAD // N_META, C_K // KV_W)),
        pl.BlockSpec((N_META, KV_W), lambda n: (PAD // N_META, C_V // KV_W)),
        pl.BlockSpec((1, 128), lambda n: (0, 0)),
    ]


def _attn_load(q_ref, kp_ref, kc_ref, vp_ref, vc_ref, km_ref, vm_ref):
    q4s, kcats, vcats, kms, vms = [], [], [], [], []
    for kh in range(KV_HEADS):
        q4s.append(jnp.concatenate(
            [q_ref[:, (kh * GROUP + g) * HEAD:(kh * GROUP + g + 1) * HEAD] for g in range(GROUP)], axis=0))
        cs = slice(kh * HEAD, (kh + 1) * HEAD)
        kcats.append(jnp.concatenate([kp_ref[:, cs], kc_ref[:, cs]], axis=0))
        vcats.append(jnp.concatenate([vp_ref[:, cs], vc_ref[:, cs]], axis=0))
        kms.append(km_ref[:, cs])
        vms.append(vm_ref[:, cs])
    return q4s, kcats, vcats, kms, vms


def _attn_fwd(proj, sinks):
    def body(q_ref, kp_ref, kc_ref, vp_ref, vc_ref, km_ref, vm_ref, s_ref, o_ref):
        n = pl.program_id(0)
        args = _attn_load(q_ref, kp_ref, kc_ref, vp_ref, vc_ref, km_ref, vm_ref)
        outs = _attn_fn(*args, s_ref[...], n)
        for kh in range(KV_HEADS):
            for g in range(GROUP):
                hh = kh * GROUP + g
                o_ref[:, hh * HEAD:(hh + 1) * HEAD] = outs[kh][g * BLK:(g + 1) * BLK]

    return pl.pallas_call(
        body, grid=(NB,), in_specs=_attn_specs(),
        out_specs=pl.BlockSpec((BLK, D_MODEL), lambda n: (n, 0)),
        out_shape=SDS((T, D_MODEL), F32), name="attn_fwd")(proj, proj, proj, proj, proj, proj, proj, sinks)


def _attn_bwd(proj, sinks, do):
    def body(q_ref, kp_ref, kc_ref, vp_ref, vc_ref, km_ref, vm_ref, s_ref, do_ref, dq_ref, dk_ref, dv_ref, ds_ref):
        n = pl.program_id(0)

        @pl.when(n == 0)
        def _():
            dk_ref[...] = jnp.zeros_like(dk_ref)
            dv_ref[...] = jnp.zeros_like(dv_ref)
            ds_ref[...] = jnp.zeros_like(ds_ref)

        args = _attn_load(q_ref, kp_ref, kc_ref, vp_ref, vc_ref, km_ref, vm_ref)
        _, vjp = jax.vjp(lambda a, b, c, d, e, f: _attn_fn(a, b, c, d, e, f, n), *args, s_ref[...])
        cot = [jnp.concatenate([do_ref[:, (kh * GROUP + g) * HEAD:(kh * GROUP + g + 1) * HEAD] for g in range(GROUP)],
                               axis=0) for kh in range(KV_HEADS)]
        dq4s, dkcats, dvcats, dkms, dvms, dsk = vjp(cot)
        ds_ref[0:1, :] += dsk
        cur = pl.ds(pl.multiple_of(n * BLK, BLK), BLK)
        meta = slice(PAD, PAD + N_META)
        for kh in range(KV_HEADS):
            cs = slice(kh * HEAD, (kh + 1) * HEAD)
            for g in range(GROUP):
                hh = kh * GROUP + g
                dq_ref[:, hh * HEAD:(hh + 1) * HEAD] = dq4s[kh][g * BLK:(g + 1) * BLK]
            dk_ref[cur, cs] += dkcats[kh][BLK:]
            dv_ref[cur, cs] += dvcats[kh][BLK:]
            dk_ref[meta, cs] += dkms[kh]
            dv_ref[meta, cs] += dvms[kh]

        @pl.when(n > 0)
        def _():
            prv = pl.ds(pl.multiple_of((n - 1) * BLK, BLK), BLK)
            for kh in range(KV_HEADS):
                cs = slice(kh * HEAD, (kh + 1) * HEAD)
                dk_ref[prv, cs] += dkcats[kh][:BLK]
                dv_ref[prv, cs] += dvcats[kh][:BLK]

    full_kv = pl.BlockSpec((T, KV_W), lambda n: (0, 0))
    return pl.pallas_call(
        body, grid=(NB,),
        in_specs=_attn_specs() + [pl.BlockSpec((BLK, D_MODEL), lambda n: (n, 0))],
        out_specs=[pl.BlockSpec((BLK, D_MODEL), lambda n: (n, 0)), full_kv, full_kv,
                   pl.BlockSpec((8, 128), lambda n: (0, 0))],
        out_shape=[SDS((T, D_MODEL), F32), SDS((T, KV_W), F32), SDS((T, KV_W), F32), SDS((8, 128), F32)],
        name="attn_bwd")(proj, proj, proj, proj, proj, proj, proj, sinks, do)


def _conv_taps(xp, w, rows):
    return (w[0:1] * xp[5:5 + rows] + w[1:2] * xp[6:6 + rows] + w[2:3] * xp[7:7 + rows] + w[3:4] * xp[8:8 + rows])


def _conv_fwd(proj, conv_w, conv_b):
    CONV_CB = CONV_DIM
    ncb = CONV_DIM // CONV_CB
    cb0 = C_XBC // CONV_CB

    def body(tail_ref, cur_ref, w_ref, b_ref, o_ref):
        n = pl.program_id(1)
        tail = jnp.where(n > 0, tail_ref[...], 0.0)
        xp = jnp.concatenate([tail, cur_ref[...]], axis=0)
        conv = _conv_taps(xp, w_ref[...], BLK) + b_ref[...]
        row = n * BLK + lax.broadcasted_iota(jnp.int32, (BLK, 1), 0)
        o_ref[...] = jnp.where(row >= PAD, _silu(conv), 0.0)

    return pl.pallas_call(
        body, grid=(ncb, NB),
        in_specs=[pl.BlockSpec((8, CONV_CB), lambda j, n: (jnp.maximum(n * (BLK // 8) - 1, 0), cb0 + j)),
                  pl.BlockSpec((BLK, CONV_CB), lambda j, n: (n, cb0 + j)),
                  pl.BlockSpec((8, CONV_CB), lambda j, n: (0, j)),
                  pl.BlockSpec((1, CONV_CB), lambda j, n: (0, j))],
        out_specs=pl.BlockSpec((BLK, CONV_CB), lambda j, n: (n, j)),
        out_shape=SDS((T, CONV_DIM), F32), name="conv_fwd")(proj, proj, conv_w, conv_b)


def _conv_bwd(proj, conv_w, conv_b, dact, ch0, name):
    width = dact.shape[1]
    CONV_CB = width
    ncb = width // CONV_CB
    cb0 = (C_XBC + ch0) // CONV_CB
    wb0 = ch0 // CONV_CB
    last8 = T // 8 - 1

    def body(tail_ref, cur_ref, nxt_ref, w_ref, b_ref, dcur_ref, dnxt_ref, dx_ref, dw_ref, db_ref):
        n = pl.program_id(1)
        w = w_ref[...]
        tail = jnp.where(n > 0, tail_ref[...], 0.0)
        xp = jnp.concatenate([tail, cur_ref[...], nxt_ref[...]], axis=0)
        conv = _conv_taps(xp, w, BLK + 8) + b_ref[...]
        dext = jnp.concatenate([dcur_ref[...], jnp.where(n < NB - 1, dnxt_ref[...], 0.0)], axis=0)
        row = n * BLK + lax.broadcasted_iota(jnp.int32, (BLK + 8, 1), 0)
        dconv = jnp.where(row >= PAD, dext * _dsilu(conv), 0.0)
        dx = (w[0:1] * dconv[3:3 + BLK] + w[1:2] * dconv[2:2 + BLK] + w[2:3] * dconv[1:1 + BLK]
              + w[3:4] * dconv[0:BLK])
        dx_ref[...] = dx.astype(BF16)
        dc = dconv[0:BLK]
        dws = [jnp.sum(dc * xp[5 + k:5 + k + BLK], axis=0, keepdims=True) for k in range(4)]
        dwp = jnp.concatenate(dws + [jnp.zeros((4, CONV_CB), F32)], axis=0)
        dbp = jnp.sum(dc, axis=0, keepdims=True)

        @pl.when(n == 0)
        def _():
            dw_ref[...] = dwp
            db_ref[...] = jnp.concatenate([dbp, jnp.zeros((7, CONV_CB), F32)], axis=0)

        @pl.when(n > 0)
        def _():
            dw_ref[...] += dwp
            db_ref[0:1, :] += dbp

    return pl.pallas_call(
        body, grid=(ncb, NB),
        in_specs=[pl.BlockSpec((8, CONV_CB), lambda j, n: (jnp.maximum(n * (BLK // 8) - 1, 0), cb0 + j)),
                  pl.BlockSpec((BLK, CONV_CB), lambda j, n: (n, cb0 + j)),
                  pl.BlockSpec((8, CONV_CB), lambda j, n: (jnp.minimum((n + 1) * (BLK // 8), last8), cb0 + j)),
                  pl.BlockSpec((8, CONV_CB), lambda j, n: (0, wb0 + j)),
                  pl.BlockSpec((1, CONV_CB), lambda j, n: (0, wb0 + j)),
                  pl.BlockSpec((BLK, CONV_CB), lambda j, n: (n, j)),
                  pl.BlockSpec((8, CONV_CB), lambda j, n: (jnp.minimum((n + 1) * (BLK // 8), last8), j))],
        out_specs=[pl.BlockSpec((BLK, CONV_CB), lambda j, n: (n, j)),
                   pl.BlockSpec((8, CONV_CB), lambda j, n: (0, j)),
                   pl.BlockSpec((8, CONV_CB), lambda j, n: (0, j))],
        out_shape=[SDS((T, width), BF16), SDS((8, width), F32), SDS((8, width), F32)],
        name=name)(proj, proj, proj, conv_w, conv_b, dact, dact)


HPG = SSM_HEADS // SSM_GROUPS


def _iota(shape, dim):
    return lax.broadcasted_iota(jnp.int32, shape, dim)


def _mm(a, b, ca=1, cb=0):
    return lax.dot_general(a.astype(BF16), b.astype(BF16), (((ca,), (cb,)), ((), ())), preferred_element_type=F32)


def _split3(v):
    hi = v.astype(BF16)
    r1 = v - hi.astype(F32)
    mid = r1.astype(BF16)
    lo = (r1 - mid.astype(F32)).astype(BF16)
    return hi, mid, lo


def _sel_r(parts, onehot, ca=1, cb=0):
    out = lax.dot_general(parts[0], onehot, (((ca,), (cb,)), ((), ())), preferred_element_type=F32)
    for p in parts[1:]:
        out = out + lax.dot_general(p, onehot, (((ca,), (cb,)), ((), ())), preferred_element_type=F32)
    return out


def _sel_l(onehot, parts):
    out = jnp.dot(onehot, parts[0], preferred_element_type=F32)
    for p in parts[1:]:
        out = out + jnp.dot(onehot, p, preferred_element_type=F32)
    return out


def _rows8(*rows):
    r = _iota((8, rows[0].shape[1]), 0)
    out = jnp.zeros((8, rows[0].shape[1]), F32)
    for k, v in enumerate(rows):
        out = jnp.where(r == k, v, out)
    return out


def _ssd_forward(x, z, bm, cm, dt_raw, st_prev, dtb, alog, dskip, gn, g, cst_scr):
    li, si = _iota((BLK, BLK), 0), _iota((BLK, BLK), 1)
    dt_all = jax.nn.softplus(dt_raw + dtb)
    a_row = -jnp.exp(alog)
    a_all = dt_all * a_row
    cs_all = _sel_l((li >= si).astype(BF16), _split3(a_all))
    cs_parts = _split3(cs_all)
    spread = (_iota((BLK, GRP_W), 0) == g * HPG + jnp.right_shift(_iota((BLK, GRP_W), 1), 6)).astype(BF16)
    dt_e = _sel_r(_split3(dt_all), spread)
    cs_e = _sel_r(cs_parts, spread)
    d_e = _sel_r(_split3(_rows8(dskip)), spread)[0:1]
    cs_last_e = jnp.sum(jnp.where(_iota((BLK, GRP_W), 0) == BLK - 1, cs_e, 0.0), axis=0, keepdims=True)
    p_e = jnp.exp(cs_e)
    w_e = jnp.exp(cs_last_e - cs_e)
    cd_e = jnp.exp(cs_last_e)
    xr = x * dt_e
    cst_scr[...] = cs_all.T
    cst_g = cst_scr[pl.ds(pl.multiple_of(g * HPG, HPG), HPG), :]
    own = jnp.right_shift(_iota((HPG, HPG * BLK), 1), 7) == _iota((HPG, HPG * BLK), 0)
    ownf = own.astype(F32)
    q_rows = [ownf, ownf, ownf] + [jnp.where(own, jnp.concatenate([p.astype(F32)] * HPG, axis=1), 0.0)
                                   for p in _split3(cst_g)]
    q2 = jnp.concatenate(q_rows + [jnp.zeros((BLK - 6 * HPG, HPG * BLK), F32)], axis=0).astype(BF16)
    lane1 = _iota((1, BLK), 1)
    p2 = jnp.where((lane1 >= 3 * HPG) & (lane1 < 6 * HPG), -1.0, 0.0)
    for k, part in enumerate(cs_parts):
        pick = ((li == g * HPG + si - k * HPG) & (si >= k * HPG) & (si < (k + 1) * HPG)).astype(BF16)
        p2 = p2 + jnp.dot(part, pick, preferred_element_type=F32)
    dmat = jnp.dot(p2.astype(BF16), q2, preferred_element_type=F32)
    causal = _iota((BLK, HPG * BLK), 0) >= jnp.bitwise_and(_iota((BLK, HPG * BLK), 1), BLK - 1)
    lam = jnp.exp(jnp.where(causal, dmat, NEG))
    gmat = _mm(cm, bm, 1, 1)
    m_all = lam * jnp.concatenate([gmat] * HPG, axis=1)
    mb = m_all.astype(BF16)
    lo = _iota((BLK, BLK), 1) < HEAD
    xrb = xr.astype(BF16)
    zero = jnp.zeros((BLK, BLK), BF16)
    bds, yd = [], []
    for i in range(HPG // 2):
        t = xrb[:, BLK * i:BLK * (i + 1)]
        bd = jnp.concatenate([jnp.where(lo, t, zero), jnp.where(lo, zero, t)], axis=0)
        bds.append(bd)
        yd.append(jnp.dot(mb[:, 2 * BLK * i:2 * BLK * (i + 1)], bd, preferred_element_type=F32))
    cs_st = _mm(cm, st_prev)
    y = jnp.concatenate(yd, axis=1) + cs_st * p_e + d_e * x
    xrw = xr * w_e
    st_new = cd_e * st_prev + _mm(bm, xrw, 0, 0)
    yz = y * _silu(z)
    rn = lax.rsqrt(jnp.sum(yz * yz, axis=1, keepdims=True) / GRP_W + EPS)
    return dict(out=yz * rn * gn, st_new=st_new, dt_all=dt_all, a_row=a_row, dt_e=dt_e, d_e=d_e, p_e=p_e, w_e=w_e,
                cd_e=cd_e, xr=xr, xrw=xrw, lam=lam, m_all=m_all, mb=mb, bds=bds, cs_st=cs_st, y=y, yz=yz, rn=rn, lo=lo)


def _ssd_backward(f, x, z, bm, cm, dt_raw, st_prev, dtb, gn, g, dout, dst_next, cst_scr):
    li, si = _iota((BLK, BLK), 0), _iota((BLK, BLK), 1)
    yz, rn, y, p_e, w_e, cd_e, xr = f["yz"], f["rn"], f["y"], f["p_e"], f["w_e"], f["cd_e"], f["xr"]
    dgn = jnp.sum(dout * yz * rn, axis=0, keepdims=True)
    t = dout * gn
    dyz = rn * t - yz * (rn * rn * rn) * (jnp.sum(yz * t, axis=1, keepdims=True) / GRP_W)
    dy = dyz * _silu(z)
    dz = dyz * y * _dsilu(z)
    dx = f["d_e"] * dy
    dd_e = jnp.sum(dy * x, axis=0, keepdims=True)
    dcsst = dy * p_e
    dp_e = dy * f["cs_st"]
    dcm = _mm(dcsst, st_prev, 1, 1)
    dst_prev = _mm(cm, dcsst, 0, 0) + cd_e * dst_next
    dcd_e = jnp.sum(dst_next * st_prev, axis=0, keepdims=True)
    dbm = _mm(f["xrw"], dst_next, 1, 1)
    dxrw = _mm(bm, dst_next)
    dxr = dxrw * w_e
    dw_e = dxrw * xr
    dyb = dy.astype(BF16)
    dms, dxr_d = [], []
    for i in range(HPG // 2):
        dyp = dyb[:, BLK * i:BLK * (i + 1)]
        dms.append(lax.dot_general(dyp, f["bds"][i], (((1,), (1,)), ((), ())), preferred_element_type=F32))
        r = lax.dot_general(f["mb"][:, 2 * BLK * i:2 * BLK * (i + 1)], dyp, (((0,), (0,)), ((), ())),
                            preferred_element_type=F32)
        dxr_d.append(jnp.where(f["lo"], r[0:BLK], r[BLK:2 * BLK]))
    dm_all = jnp.concatenate(dms, axis=1)
    dxr = dxr + jnp.concatenate(dxr_d, axis=1)
    dlg = dm_all * f["lam"]
    dg = dlg[:, 0:BLK]
    for j in range(1, HPG):
        dg = dg + dlg[:, BLK * j:BLK * (j + 1)]
    dcm = dcm + _mm(dg, bm)
    dbm = dbm + _mm(dg, cm, 0, 0)
    q_all = dm_all * f["m_all"]
    col_sums = jnp.sum(q_all, axis=0, keepdims=True)
    cst_scr[...] = jnp.zeros_like(cst_scr)
    cst_scr[pl.ds(pl.multiple_of(g * HPG, HPG), HPG), :] = _rows8(
        *[col_sums[:, BLK * j:BLK * (j + 1)] for j in range(HPG)])
    dcs = -cst_scr[...].T
    for j in range(HPG):
        dcs = dcs + jnp.where(si == g * HPG + j,
                              jnp.sum(q_all[:, BLK * j:BLK * (j + 1)], axis=1, keepdims=True), 0.0)
    unspread = (_iota((GRP_W, BLK), 1) == g * HPG + jnp.right_shift(_iota((GRP_W, BLK), 0), 6)).astype(BF16)
    dww = dw_e * w_e
    per_head = _sel_r(_split3(jnp.concatenate([dp_e * p_e - dww, dxr * x], axis=0)), unspread)
    last = _sel_r(_split3(_rows8(jnp.sum(dww, axis=0, keepdims=True) + dcd_e * cd_e, dd_e)), unspread)
    dcs = dcs + per_head[0:BLK] + jnp.where(li == BLK - 1, last[0:1], 0.0)
    da = _sel_l((si >= li).astype(BF16), _split3(dcs))
    ddt_all = da * f["a_row"] + per_head[BLK:2 * BLK]
    dalog = jnp.sum(da * f["dt_all"], axis=0, keepdims=True) * f["a_row"]
    dx = dx + dxr * f["dt_e"]
    ddt_raw = ddt_all * jax.nn.sigmoid(dt_raw + dtb)
    ddtb = jnp.sum(ddt_raw, axis=0, keepdims=True)
    ddskip = last[1:2]
    return dict(dx=dx, dz=dz, dbm=dbm, dcm=dcm, ddt_raw=ddt_raw, dst_prev=dst_prev, ddtb=ddtb, dalog=dalog,
                ddskip=ddskip, dgn=dgn)


def _ssd_in_specs(rev):
    cidx = (lambda c: NB - 1 - c) if rev else (lambda c: c)
    return [
        pl.BlockSpec((BLK, GRP_W), lambda g, c: (cidx(c), g)),
        pl.BlockSpec((BLK, SSM_STATE), lambda g, c: (cidx(c), SSM_INNER // SSM_STATE + g)),
        pl.BlockSpec((BLK, SSM_STATE), lambda g, c: (cidx(c), SSM_INNER // SSM_STATE + SSM_GROUPS + g)),
        pl.BlockSpec((BLK, 128), lambda g, c: (cidx(c), C_DT // 128)),
        pl.BlockSpec((BLK, GRP_W), lambda g, c: (cidx(c), C_ZS // GRP_W + g)),
        pl.BlockSpec((1, 128), lambda g, c: (0, 0)),
        pl.BlockSpec((1, 128), lambda g, c: (0, 0)),
        pl.BlockSpec((1, 128), lambda g, c: (0, 0)),
        pl.BlockSpec((1, GRP_W), lambda g, c: (0, g)),
    ]


def _ssd_fwd(xbc_act, proj, dt_bias, a_log, d_skip, g_norm):
    def body(xs_ref, b_ref, c_ref, dt_ref, z_ref, dtb_ref, al_ref, dsk_ref, gn_ref, y_ref, st_ref, s_scr, cst_scr):
        g = pl.program_id(0)
        c = pl.program_id(1)

        @pl.when(c == 0)
        def _():
            s_scr[...] = jnp.zeros_like(s_scr)

        st_prev = s_scr[...]
        st_ref[0, 0] = st_prev
        f = _ssd_forward(xs_ref[...], z_ref[...], b_ref[...], c_ref[...], dt_ref[...], st_prev, dtb_ref[...],
                         al_ref[...], dsk_ref[...], gn_ref[...], g, cst_scr)
        y_ref[...] = f["out"].astype(BF16)
        s_scr[...] = f["st_new"]

    return pl.pallas_call(
        body, grid=(SSM_GROUPS, NB), in_specs=_ssd_in_specs(False),
        out_specs=[pl.BlockSpec((BLK, GRP_W), lambda g, c: (c, g)),
                   pl.BlockSpec((1, 1, SSM_STATE, GRP_W), lambda g, c: (g, c, 0, 0))],
        out_shape=[SDS((T, SSM_INNER), BF16), SDS((SSM_GROUPS, NB, SSM_STATE, GRP_W), F32)],
        scratch_shapes=[pltpu.VMEM((SSM_STATE, GRP_W), F32), pltpu.VMEM((BLK, BLK), F32)],
        compiler_params=_cparams(),
        name="ssd_fwd")(xbc_act, xbc_act, xbc_act, proj, proj, dt_bias, a_log, d_skip, g_norm)


def _ssd_bwd(xbc_act, proj, dt_bias, a_log, d_skip, g_norm, states, dy):
    def body(xs_ref, b_ref, c_ref, dt_ref, z_ref, dtb_ref, al_ref, dsk_ref, gn_ref, st_ref, dy_ref,
             dxs_ref, db_ref, dc_ref, ddt_ref, dz_ref, ddtb_ref, dal_ref, ddsk_ref, dgn_ref, ds_scr, cst_scr):
        g = pl.program_id(0)
        c = pl.program_id(1)

        @pl.when(c == 0)
        def _():
            ds_scr[...] = jnp.zeros_like(ds_scr)
            dgn_ref[...] = jnp.zeros_like(dgn_ref)

        @pl.when((c == 0) & (g == 0))
        def _():
            ddtb_ref[...] = jnp.zeros_like(ddtb_ref)
            dal_ref[...] = jnp.zeros_like(dal_ref)
            ddsk_ref[...] = jnp.zeros_like(ddsk_ref)

        x, z, bm, cm, dt_raw, st_prev = xs_ref[...], z_ref[...], b_ref[...], c_ref[...], dt_ref[...], st_ref[0, 0]
        f = _ssd_forward(x, z, bm, cm, dt_raw, st_prev, dtb_ref[...], al_ref[...], dsk_ref[...], gn_ref[...], g,
                         cst_scr)
        d = _ssd_backward(f, x, z, bm, cm, dt_raw, st_prev, dtb_ref[...], gn_ref[...], g, dy_ref[...], ds_scr[...],
                          cst_scr)
        dxs_ref[...] = d["dx"]
        dz_ref[...] = d["dz"].astype(BF16)
        ds_scr[...] = d["dst_prev"]
        db_ref[...] = d["dbm"]
        dc_ref[...] = d["dcm"]
        ddt_ref[...] = d["ddt_raw"]
        dgn_ref[0:1, :] += d["dgn"]
        ddtb_ref[0:1, :] += d["ddtb"]
        dal_ref[0:1, :] += d["dalog"]
        ddsk_ref[0:1, :] += d["ddskip"]

    rc = lambda c: NB - 1 - c
    small = pl.BlockSpec((8, 128), lambda g, c: (0, 0))
    return pl.pallas_call(
        body, grid=(SSM_GROUPS, NB),
        in_specs=_ssd_in_specs(True) + [
            pl.BlockSpec((1, 1, SSM_STATE, GRP_W), lambda g, c: (g, rc(c), 0, 0)),
            pl.BlockSpec((BLK, GRP_W), lambda g, c: (rc(c), g))],
        out_specs=[pl.BlockSpec((BLK, GRP_W), lambda g, c: (rc(c), g)),
                   pl.BlockSpec((BLK, SSM_STATE), lambda g, c: (rc(c), g)),
                   pl.BlockSpec((BLK, SSM_STATE), lambda g, c: (rc(c), g)),
                   pl.BlockSpec((BLK, 128), lambda g, c: (rc(c), g)),
                   pl.BlockSpec((BLK, GRP_W), lambda g, c: (rc(c), g)),
                   small, small, small,
                   pl.BlockSpec((8, GRP_W), lambda g, c: (0, g))],
        out_shape=[SDS((T, SSM_INNER), F32), SDS((T, GRP_W), F32), SDS((T, GRP_W), F32), SDS((T, GRP_W), F32),
                   SDS((T, SSM_INNER), BF16), SDS((8, 128), F32), SDS((8, 128), F32), SDS((8, 128), F32),
                   SDS((8, SSM_INNER), F32)],
        scratch_shapes=[pltpu.VMEM((SSM_STATE, GRP_W), F32), pltpu.VMEM((BLK, BLK), F32)],
        compiler_params=_cparams(),
        name="ssd_bwd")(xbc_act, xbc_act, xbc_act, proj, proj, dt_bias, a_log, d_skip, g_norm, states, dy)


def _post_a(o, proj, sn, w_att, w_ssm, w_o):
    def body(o_ref, za_ref, ga_ref, gs_ref, sn_ref, wa_ref, ws_ref, wo_ref, a_ref, mg_ref, ya_ref, ys_ref, out_ref):
        a = (o_ref[...] * _silu(za_ref[...])).astype(BF16)
        a_ref[...] = a
        ya = jnp.dot(a, wa_ref[...], preferred_element_type=F32)
        ys = jnp.dot(sn_ref[...], ws_ref[...], preferred_element_type=F32)
        ya_ref[...] = ya
        ys_ref[...] = ys
        mg = (jax.nn.sigmoid(ga_ref[...]) * ya + jax.nn.sigmoid(gs_ref[...]) * ys).astype(BF16)
        mg_ref[...] = mg
        out_ref[...] = jnp.dot(mg, wo_ref[...], preferred_element_type=F32)

    row = pl.BlockSpec((BLK, D_MODEL), lambda i: (i, 0))
    pcol = lambda c0: pl.BlockSpec((BLK, D_MODEL), lambda i: (i, c0 // D_MODEL))
    full = lambda r: pl.BlockSpec((r, D_MODEL), lambda i: (0, 0))
    return pl.pallas_call(
        body, grid=(NB,),
        in_specs=[row, pcol(C_ZA), pcol(C_GA), pcol(C_GS), pl.BlockSpec((BLK, SSM_INNER), lambda i: (i, 0)),
                  full(D_MODEL), full(SSM_INNER), full(D_MODEL)],
        out_specs=[row, row, row, row, row],
        out_shape=[SDS((T, D_MODEL), BF16), SDS((T, D_MODEL), BF16), SDS((T, D_MODEL), F32), SDS((T, D_MODEL), F32),
                   SDS((T, D_MODEL), F32)],
        compiler_params=_cparams(), name="post_a")(o, proj, proj, proj, sn, w_att, w_ssm, w_o)


def _post_b(out, h, tgt, proj, ya, ys, o, g_post, w_att, w_ssm, w_o):
    def body(out_ref, h_ref, t_ref, za_ref, ga_ref, gs_ref, ya_ref, ys_ref, o_ref, gp_ref, wa_ref, ws_ref, wo_ref,
             loss_ref, dres_ref, dout_ref, dya_ref, dys_ref, dga_ref, dgs_ref, do_ref, dza_ref, dsn_ref, dgp_ref):
        i = pl.program_id(0)
        x = out_ref[...]
        gp = gp_ref[...]
        r = lax.rsqrt(jnp.mean(x * x, axis=-1, keepdims=True) + EPS)
        row = i * BLK + lax.broadcasted_iota(jnp.int32, (BLK, 1), 0)
        res = h_ref[...] + jnp.where(row >= PAD, x * r * gp, 0.0)
        live = row >= PAD + N_META
        err = jnp.where(live, res - t_ref[...], 0.0)
        lpart = 0.5 * jnp.sum(jnp.sum(err * err, axis=1, keepdims=True) / D_MODEL, axis=0, keepdims=True)
        dres = err / D_MODEL
        dres_ref[...] = dres
        gpart = jnp.sum(dres * x * r, axis=0, keepdims=True)

        @pl.when(i == 0)
        def _():
            loss_ref[...] = jnp.zeros_like(loss_ref)
            dgp_ref[...] = jnp.zeros_like(dgp_ref)

        loss_ref[...] += jnp.broadcast_to(lpart, loss_ref.shape)
        dgp_ref[0:1, :] += gpart
        gd = gp * dres
        dout = (r * gd - x * (r * r * r) * jnp.mean(x * gd, axis=-1, keepdims=True)).astype(BF16)
        dout_ref[...] = dout
        dmg = lax.dot_general(dout, wo_ref[...], (((1,), (1,)), ((), ())), preferred_element_type=F32)
        sga = jax.nn.sigmoid(ga_ref[...])
        sgs = jax.nn.sigmoid(gs_ref[...])
        dya = (dmg * sga).astype(BF16)
        dys = (dmg * sgs).astype(BF16)
        dya_ref[...] = dya
        dys_ref[...] = dys
        dga_ref[...] = (dmg * ya_ref[...] * sga * (1.0 - sga)).astype(BF16)
        dgs_ref[...] = (dmg * ys_ref[...] * sgs * (1.0 - sgs)).astype(BF16)
        da = lax.dot_general(dya, wa_ref[...], (((1,), (1,)), ((), ())), preferred_element_type=F32)
        za = za_ref[...]
        do_ref[...] = da * _silu(za)
        dza_ref[...] = (da * o_ref[...] * _dsilu(za)).astype(BF16)
        dsn_ref[...] = lax.dot_general(dys, ws_ref[...], (((1,), (1,)), ((), ())), preferred_element_type=F32)

    row = pl.BlockSpec((BLK, D_MODEL), lambda i: (i, 0))
    pcol = lambda c0: pl.BlockSpec((BLK, D_MODEL), lambda i: (i, c0 // D_MODEL))
    full = lambda r: pl.BlockSpec((r, D_MODEL), lambda i: (0, 0))
    small = pl.BlockSpec((8, D_MODEL), lambda i: (0, 0))
    return pl.pallas_call(
        body, grid=(NB,),
        in_specs=[row, row, row, pcol(C_ZA), pcol(C_GA), pcol(C_GS), row, row, row,
                  pl.BlockSpec((1, D_MODEL), lambda i: (0, 0)), full(D_MODEL), full(SSM_INNER), full(D_MODEL)],
        out_specs=[pl.BlockSpec((8, 128), lambda i: (0, 0)), row, row, row, row, row, row, row, row,
                   pl.BlockSpec((BLK, SSM_INNER), lambda i: (i, 0)), small],
        out_shape=[SDS((8, 128), F32), SDS((T, D_MODEL), F32), SDS((T, D_MODEL), BF16), SDS((T, D_MODEL), BF16),
                   SDS((T, D_MODEL), BF16), SDS((T, D_MODEL), BF16), SDS((T, D_MODEL), BF16), SDS((T, D_MODEL), F32),
                   SDS((T, D_MODEL), BF16), SDS((T, SSM_INNER), F32), SDS((8, D_MODEL), F32)],
        compiler_params=_cparams(), name="post_b")(out, h, tgt, proj, proj, proj, ya, ys, o, g_post, w_att, w_ssm, w_o)


def _assemble(dq, dza, dga, dgs, dzs, dxx, dxb, dxc, dk, dv, ddt4):
    def body(dq_ref, dza_ref, dga_ref, dgs_ref, dzs_ref, dxx_ref, dxb_ref, dxc_ref, dk_ref, dv_ref, ddt_ref, o_ref):
        o_ref[:, C_Q:C_Q + D_MODEL] = dq_ref[...].astype(BF16)
        o_ref[:, C_ZA:C_ZA + D_MODEL] = dza_ref[...]
        o_ref[:, C_GA:C_GA + D_MODEL] = dga_ref[...]
        o_ref[:, C_GS:C_GS + D_MODEL] = dgs_ref[...]
        o_ref[:, C_ZS:C_ZS + SSM_INNER] = dzs_ref[...]
        o_ref[:, C_XBC:C_XBC + SSM_INNER] = dxx_ref[...]
        o_ref[:, C_XBC + SSM_INNER:C_XBC + SSM_INNER + GRP_W] = dxb_ref[...]
        o_ref[:, C_XBC + SSM_INNER + GRP_W:C_XBC + CONV_DIM] = dxc_ref[...]
        o_ref[:, C_K:C_K + KV_W] = dk_ref[...].astype(BF16)
        o_ref[:, C_V:C_V + KV_W] = dv_ref[...].astype(BF16)
        d4 = ddt_ref[...]
        o_ref[:, C_DT:C_DT + 128] = (d4[:, 0:128] + d4[:, 128:256] + d4[:, 256:384] + d4[:, 384:512]).astype(BF16)

    spec = lambda w: pl.BlockSpec((BLK, w), lambda i: (i, 0))
    ins = [dq, dza, dga, dgs, dzs, dxx, dxb, dxc, dk, dv, ddt4]
    return pl.pallas_call(
        body, grid=(NB,), in_specs=[spec(a.shape[1]) for a in ins], out_specs=spec(PW),
        out_shape=SDS((T, PW), BF16), name="assemble")(*ins)


def _adamw_math(w, g, m, v):
    m = ADAM_B1 * m + (1.0 - ADAM_B1) * g
    v = ADAM_B2 * v + (1.0 - ADAM_B2) * (g * g)
    m_hat = m / (1.0 - ADAM_B1 ** ADAM_STEP)
    v_hat = v / (1.0 - ADAM_B2 ** ADAM_STEP)
    delta = -ADAM_LR * (m_hat / (jnp.sqrt(v_hat) + ADAM_EPS) + ADAM_WD * w)
    return delta, m, v


def _sum_adamw(recv, w, m, v, tc, name):
    rows, cols = w.shape
    assert cols % tc == 0

    def body(r_ref, w_ref, m_ref, v_ref, g_ref, d_ref, nm_ref, nv_ref):
        g = r_ref[0].astype(F32)
        for d in range(1, N_DEV):
            g = g + r_ref[d].astype(F32)
        g_ref[...] = g
        delta, nm, nv = _adamw_math(w_ref[...], g, m_ref[...], v_ref[...])
        d_ref[...] = delta
        nm_ref[...] = nm
        nv_ref[...] = nv

    blk = pl.BlockSpec((rows, tc), lambda i: (0, i))
    return pl.pallas_call(
        body, grid=(cols // tc,),
        in_specs=[pl.BlockSpec((N_DEV, rows, tc), lambda i: (0, 0, i)), blk, blk, blk],
        out_specs=[blk, blk, blk, blk], out_shape=[SDS((rows, cols), F32)] * 4,
        compiler_params=_cparams(), name=name)(recv, w, m, v)


def _sum_adamw_rows3(recv, w3, m3, v3, name):
    pairs = 61
    assert (SHARD_IN // 2) % pairs == 0

    def body(r_ref, w_ref, m_ref, v_ref, g_ref, d_ref, nm_ref, nv_ref):
        g = r_ref[0].astype(F32)
        for d in range(1, N_DEV):
            g = g + r_ref[d].astype(F32)
        g = g.reshape(2 * pairs, ROW_TILES, 128)
        g_ref[...] = g
        delta, nm, nv = _adamw_math(w_ref[...], g, m_ref[...], v_ref[...])
        d_ref[...] = delta
        nm_ref[...] = nm
        nv_ref[...] = nv

    blk = pl.BlockSpec((2 * pairs, ROW_TILES, 128), lambda i: (i, 0, 0))
    return pl.pallas_call(
        body, grid=(SHARD_IN // 2 // pairs,),
        in_specs=[pl.BlockSpec((N_DEV, pairs, 2 * ROW_TILES, 128), lambda i: (0, i, 0, 0)), blk, blk, blk],
        out_specs=[blk, blk, blk, blk], out_shape=[SDS(w3.shape, F32)] * 4,
        compiler_params=_cparams(), name=name)(recv, w3, m3, v3)


ROW_GPRE, ROW_CONVB, ROW_DTB, ROW_ALOG, ROW_DSKIP, ROW_SINK, ROW_GSSM, ROW_GPOST = 0, 1, 4, 5, 6, 7, 8, 10
REP_ROWS, ROW_CONVW, ROW_META, SM_ROWS = 16, 16, 24, 40
CW_SHARD = CONV_DIM // N_DEV
META_SHARD = D_MODEL // N_DEV


def _small_pack(dgpre, dbx, dbb, dbc, ddtb, dal, ddsk, dsink, dgn, dgp, dwx, dwb, dwc, dh):
    def body(dgpre_ref, dbx_ref, dbb_ref, dbc_ref, ddtb_ref, dal_ref, ddsk_ref, dsink_ref, dgn_ref, dgp_ref,
             dwx_ref, dwb_ref, dwc_ref, dh_ref, o_ref, rep):
        rep[...] = jnp.zeros_like(rep)
        rep[ROW_GPRE:ROW_GPRE + 1, :] = dgpre_ref[0:1, :]
        rep[ROW_CONVB:ROW_CONVB + 1, :] = dbx_ref[0:1, 0:1024]
        rep[ROW_CONVB + 1:ROW_CONVB + 2, :] = dbx_ref[0:1, 1024:2048]
        rep[ROW_CONVB + 2:ROW_CONVB + 3, 0:512] = dbb_ref[0:1, :]
        rep[ROW_CONVB + 2:ROW_CONVB + 3, 512:1024] = dbc_ref[0:1, :]
        rep[ROW_DTB:ROW_DTB + 1, 0:128] = ddtb_ref[0:1, :]
        rep[ROW_ALOG:ROW_ALOG + 1, 0:128] = dal_ref[0:1, :]
        rep[ROW_DSKIP:ROW_DSKIP + 1, 0:128] = ddsk_ref[0:1, :]
        rep[ROW_SINK:ROW_SINK + 1, 0:128] = dsink_ref[0:1, :]
        rep[ROW_GSSM:ROW_GSSM + 1, :] = dgn_ref[0:1, 0:1024]
        rep[ROW_GSSM + 1:ROW_GSSM + 2, :] = dgn_ref[0:1, 1024:2048]
        rep[ROW_GPOST:ROW_GPOST + 1, :] = dgp_ref[0:1, :]
        cw = jnp.concatenate([dwx_ref[...], dwb_ref[...], dwc_ref[...]], axis=1)
        mh = dh_ref[...]
        o_ref[...] = jnp.zeros_like(o_ref)
        for p in range(N_DEV):
            o_ref[p, 0:REP_ROWS, :] = rep[...]
            o_ref[p, ROW_CONVW:ROW_CONVW + 8, 0:CW_SHARD] = cw[:, p * CW_SHARD:(p + 1) * CW_SHARD]
            o_ref[p, ROW_META:ROW_META + N_META, 0:META_SHARD] = mh[:, p * META_SHARD:(p + 1) * META_SHARD]

    ins = [dgpre, dbx, dbb, dbc, ddtb, dal, ddsk, dsink, dgn, dgp, dwx, dwb, dwc]
    return pl.pallas_call(
        body, grid=(1,),
        in_specs=[pl.BlockSpec(a.shape, lambda i: (0, 0)) for a in ins]
        + [pl.BlockSpec((N_META, D_MODEL), lambda i: (PAD // N_META, 0))],
        out_specs=pl.BlockSpec((N_DEV, SM_ROWS, 1024), lambda i: (0, 0, 0)),
        out_shape=SDS((N_DEV, SM_ROWS, 1024), F32), scratch_shapes=[pltpu.VMEM((REP_ROWS, 1024), F32)],
        name="small_pack")(*ins, dh)


def _small_finish(recv, params):
    npar = len(params)

    def body(*refs):
        r_ref = refs[0]
        wmv = refs[1:1 + 3 * npar]
        outs = refs[1 + 3 * npar:1 + 7 * npar]
        gs = refs[-1]
        g = r_ref[0]
        for d in range(1, N_DEV):
            g = g + r_ref[d]
        gs[...] = g
        grads = [
            gs[ROW_GPRE:ROW_GPRE + 1, :],
            jnp.concatenate([gs[ROW_CONVB + k:ROW_CONVB + k + 1, :] for k in range(3)], axis=1),
            gs[ROW_DTB:ROW_DTB + 1, 0:SSM_HEADS], gs[ROW_ALOG:ROW_ALOG + 1, 0:SSM_HEADS],
            gs[ROW_DSKIP:ROW_DSKIP + 1, 0:SSM_HEADS], gs[ROW_SINK:ROW_SINK + 1, 0:Q_HEADS],
            jnp.concatenate([gs[ROW_GSSM:ROW_GSSM + 1, :], gs[ROW_GSSM + 1:ROW_GSSM + 2, :]], axis=1),
            gs[ROW_GPOST:ROW_GPOST + 1, :],
            gs[ROW_CONVW:ROW_CONVW + 4, 0:CW_SHARD],
            gs[ROW_META:ROW_META + N_META, 0:META_SHARD]]
        for i in range(npar):
            w_ref, m_ref, v_ref = wmv[3 * i:3 * i + 3]
            delta, nm, nv = _adamw_math(w_ref[...], grads[i], m_ref[...], v_ref[...])
            outs[4 * i][...] = grads[i]
            outs[4 * i + 1][...] = delta
            outs[4 * i + 2][...] = nm
            outs[4 * i + 3][...] = nv

    flat = [a for wmv in params for a in wmv]
    res = pl.pallas_call(
        body, out_shape=[SDS(wmv[0].shape, F32) for wmv in params for _ in range(4)],
        scratch_shapes=[pltpu.VMEM((SM_ROWS, 1024), F32)], name="small_finish")(recv, *flat)
    return [tuple(res[4 * i:4 * i + 4]) for i in range(npar)]


def _slab(ref, px, py, pc):
    return ref.at[4 * px + 2 * py + pc]


def _all_gather(shards):
    na = len(shards)

    def body(*refs):
        ins, outs = refs[:na], refs[na:2 * na]
        send_sems, recv_sems, local_sems = refs[2 * na:]
        x, y, c = lax.axis_index("x"), lax.axis_index("y"), lax.axis_index("c")
        me, sibling = (x, y, c), (x, y, 1 - c)
        chips = [(1 - x, y), (x, 1 - y), (1 - x, 1 - y)]

        def copy(a, k, block, to, src=None):
            dst = _slab(outs[a], *block)
            return pltpu.make_async_remote_copy(
                src_ref=dst if src is None else src, dst_ref=dst, send_sem=send_sems.at[a, k],
                recv_sem=recv_sems.at[a, k], device_id=to, device_id_type=MESH)

        mine = [pltpu.make_async_copy(ins[a], _slab(outs[a], *me), local_sems.at[a]) for a in range(na)]
        for cp in mine:
            cp.start()
        first = []
        for a in range(na):
            first.append(copy(a, 0, me, sibling, src=ins[a]))
            first += [copy(a, 1 + j, me, (*chip, c), src=ins[a]) for j, chip in enumerate(chips)]
        for cp in first:
            cp.start()
        passed = []
        for j, chip in enumerate(chips):
            for a in range(na):
                copy(a, 1 + j, (*chip, c), me).wait_recv()
                cp = copy(a, 4 + j, (*chip, c), sibling)
                cp.start()
                passed.append(cp)
        for a in range(na):
            copy(a, 0, sibling, me).wait_recv()
            for j, chip in enumerate(chips):
                copy(a, 4 + j, (*chip, 1 - c), me).wait_recv()
        for cp in first + passed:
            cp.wait_send()
        for cp in mine:
            cp.wait()

    return pl.pallas_call(
        body, in_specs=[ANY] * na, out_specs=[ANY] * na,
        out_shape=[SDS((N_DEV,) + s.shape, s.dtype) for s in shards],
        scratch_shapes=[pltpu.SemaphoreType.DMA((na, 7)), pltpu.SemaphoreType.DMA((na, 7)),
                        pltpu.SemaphoreType.DMA((na,))],
        name="all_gather")(*shards)


def _exchange(parts):
    na = len(parts)

    def body(*refs):
        ins, outs = refs[:na], refs[na:2 * na]
        send_sems, recv_sems, local_sems = refs[2 * na:]
        x, y, c = lax.axis_index("x"), lax.axis_index("y"), lax.axis_index("c")
        me = (x, y, c)
        mine = [pltpu.make_async_copy(_slab(ins[a], *me), _slab(outs[a], *me), local_sems.at[a]) for a in range(na)]
        for cp in mine:
            cp.start()
        peers = []
        for k in range(1, N_DEV):
            dx, dy, dc = (k >> 2) & 1, (k >> 1) & 1, k & 1
            peers.append(((1 - x) if dx else x, (1 - y) if dy else y, (1 - c) if dc else c))
        sent = []
        for a in range(na):
            for k, peer in enumerate(peers):
                cp = pltpu.make_async_remote_copy(
                    src_ref=_slab(ins[a], *peer), dst_ref=_slab(outs[a], *me), send_sem=send_sems.at[a, k],
                    recv_sem=recv_sems.at[a, k], device_id=peer, device_id_type=MESH)
                cp.start()
                sent.append(cp)
        for a in range(na):
            for k, peer in enumerate(peers):
                pltpu.make_async_remote_copy(
                    src_ref=_slab(ins[a], *peer), dst_ref=_slab(outs[a], *peer), send_sem=send_sems.at[a, k],
                    recv_sem=recv_sems.at[a, k], device_id=peer, device_id_type=MESH).wait_recv()
        for cp in sent:
            cp.wait_send()
        for cp in mine:
            cp.wait()

    return pl.pallas_call(
        body, in_specs=[ANY] * na, out_specs=[ANY] * na,
        out_shape=[SDS(p.shape, p.dtype) for p in parts],
        scratch_shapes=[pltpu.SemaphoreType.DMA((na, 7)), pltpu.SemaphoreType.DMA((na, 7)),
                        pltpu.SemaphoreType.DMA((na,))],
        name="exchange")(*parts)


ROW_TILES = D_MODEL // 128


def _rows3(t):
    return jnp.transpose(t[0]).reshape(t.shape[2], ROW_TILES, 128)


def _unrows3(t):
    return jnp.transpose(t.reshape(t.shape[0], D_MODEL))[None]


def _cast_shards(w_in3, w_att, w_ssm, w_o):
    def body(wi_ref, wa_ref, ws_ref, wo_ref, a_ref, b_ref, c_ref, d_ref):
        a_ref[...] = wi_ref[...].reshape(SHARD_IN // 2, 2 * ROW_TILES, 128).astype(BF16)
        b_ref[...] = wa_ref[...].astype(BF16)
        c_ref[...] = ws_ref[...].astype(BF16)
        d_ref[...] = wo_ref[...].astype(BF16)

    return pl.pallas_call(
        body, out_shape=[SDS((SHARD_IN // 2, 2 * ROW_TILES, 128), BF16), SDS(w_att.shape, BF16),
                         SDS(w_ssm.shape, BF16), SDS(w_o.shape, BF16)],
        compiler_params=_cparams(), name="cast_shards")(w_in3, w_att, w_ssm, w_o)


def _pieces():
    out = []
    for r0, c0, w in _SEGS:
        r = r0
        while r < r0 + w:
            d = r // SHARD_IN
            n = min(r0 + w, (d + 1) * SHARD_IN) - r
            out.append((c0 + (r - r0), d, r - d * SHARD_IN, n))
            r += n
    return out


def _to_aligned_t(slabs):
    def body(a_ref, o_ref):
        for (t, d, s, n) in _pieces():
            o_ref[t:t + n, :] = a_ref[d, s // 2:(s + n) // 2].reshape(n, D_MODEL)
        o_ref[C_DT + 32:C_DT + 128, :] = jnp.zeros((96, D_MODEL), slabs.dtype)

    return pl.pallas_call(body, out_shape=SDS((PW, D_MODEL), slabs.dtype), compiler_params=_cparams(),
                          name="to_aligned")(slabs)


def _from_aligned_t(g):
    def body(g_ref, o_ref):
        for (t, d, s, n) in _pieces():
            o_ref[d, s // 2:(s + n) // 2] = g_ref[t:t + n, :].reshape(n // 2, 2 * ROW_TILES, 128)

    return pl.pallas_call(body, out_shape=SDS((N_DEV, SHARD_IN // 2, 2 * ROW_TILES, 128), g.dtype),
                          compiler_params=_cparams(), name="from_aligned")(g)


_SEGS = [
    (R_Q, C_Q, 1024), (R_K, C_K, 256), (R_V, C_V, 256), (R_ZA, C_ZA, 1024), (R_ZS, C_ZS, 2048),
    (R_XBC, C_XBC, 3072), (R_DT, C_DT, 32), (R_GA, C_GA, 1024), (R_GS, C_GS, 1024)]


def _pad_lanes(v, n=128):
    return jnp.pad(v, ((0, 0), (0, n - v.shape[1])))


def _local_step(h, tgt, w_alt, w_att, w_ssm, w_o, g_pre, conv_w8, conv_b, dt_bias, a_log, d_skip, sinks,
                g_ssm, g_post):
    dtb, al, dsk, snk = _pad_lanes(dt_bias), _pad_lanes(a_log), _pad_lanes(d_skip), _pad_lanes(sinks)
    u = _norm_u(h, g_pre)
    proj = _matmul(u, w_alt, "nt", F32, 1088, 896, D_MODEL, "in_proj")
    o = _attn_fwd(proj, snk)
    xbc_act = _conv_fwd(proj, conv_w8, conv_b)
    sn, states = _ssd_fwd(xbc_act, proj, dtb, al, dsk, g_ssm)
    a_in, mg, ya, ys, out = _post_a(o, proj, sn, w_att, w_ssm, w_o)
    (loss, dres, dout, dya, dys, dga, dgs, do, dza, dsn, dgp) = _post_b(
        out, h, tgt, proj, ya, ys, o, g_post, w_att, w_ssm, w_o)
    dxs, dbm, dcm, ddt4, dzs, ddtb, dal, ddsk, dgn = _ssd_bwd(xbc_act, proj, dtb, al, dsk, g_ssm, states, dsn)
    dxx, dwx, dbx = _conv_bwd(proj, conv_w8, conv_b, dxs, 0, "conv_bwd_x")
    dxb, dwb, dbb = _conv_bwd(proj, conv_w8, conv_b, dbm, SSM_INNER, "conv_bwd_b")
    dxc, dwc, dbc = _conv_bwd(proj, conv_w8, conv_b, dcm, SSM_INNER + GRP_W, "conv_bwd_c")
    dq, dk, dv, dsink = _attn_bwd(proj, snk, do)
    dproj = _assemble(dq, dza, dga, dgs, dzs, dxx, dxb, dxc, dk, dv, ddt4)
    du = _matmul(dproj, w_alt, "nn", F32, 1088, D_MODEL, 896, "d_u")
    dw_alt = _matmul(dproj, u, "tn", BF16, 896, D_MODEL, T, "d_w_in")
    dh, dgpre = _norm_bwd(h, g_pre, du, dres)
    dw_att = _matmul(a_in, dya, "tn", BF16, D_MODEL, D_MODEL, T, "d_w_att")
    dw_ssm = _matmul(sn, dys, "tn", BF16, D_MODEL, D_MODEL, T, "d_w_ssm")
    dw_o = _matmul(mg, dout, "tn", BF16, D_MODEL, D_MODEL, T, "d_w_o")
    return dict(
        loss=loss[0, 0], dh=dh, dw_alt=dw_alt, dw_att=dw_att, dw_ssm=dw_ssm, dw_o=dw_o,
        small=(dgpre, dbx, dbb, dbc, ddtb, dal, ddsk, dsink, dgn, dgp, dwx, dwb, dwc))


def kernel(x, meta_tokens, g_pre, w_in, conv_w, conv_b, dt_bias, a_log, d_skip, attn_sinks, g_ssm_norm, w_out_att, w_out_ssm, w_out, g_post, loss_target, m_meta_tokens, m_g_pre, m_w_in, m_conv_w, m_conv_b, m_dt_bias, m_a_log, m_d_skip, m_attn_sinks, m_g_ssm_norm, m_w_out_att, m_w_out_ssm, m_w_out, m_g_post, v_meta_tokens, v_g_pre, v_w_in, v_conv_w, v_conv_b, v_dt_bias, v_a_log, v_d_skip, v_attn_sinks, v_g_ssm_norm, v_w_out_att, v_w_out_ssm, v_w_out, v_g_post):
    w_in3, m_in3, v_in3 = _rows3(w_in), _rows3(m_w_in), _rows3(v_w_in)
    a_sh, att_sh, ssm_sh, o_sh = _cast_shards(w_in3, w_out_att[0], w_out_ssm[0], w_out[0])
    cw_sh = jnp.pad(conv_w[0], ((0, 4), (0, 0)))
    a_all, att_all, ssm_all, o_all, meta_all, cw_all = _all_gather([a_sh, att_sh, ssm_sh, o_sh, meta_tokens, cw_sh])
    w_alt = _to_aligned_t(a_all)
    w_att = att_all.reshape(D_MODEL, D_MODEL)
    w_ssm = ssm_all.reshape(SSM_INNER, D_MODEL)
    w_o = o_all.reshape(D_MODEL, D_MODEL)
    meta_full = meta_all.transpose(1, 0, 2).reshape(N_META, D_MODEL)
    conv_w8 = cw_all.transpose(1, 0, 2).reshape(8, CONV_DIM)

    h = jnp.concatenate([jnp.zeros((PAD, D_MODEL), F32), meta_full, x[0]], axis=0)
    tgt = jnp.concatenate([jnp.zeros((PAD + N_META, D_MODEL), F32), loss_target[0]], axis=0)
    r = _local_step(h, tgt, w_alt, w_att, w_ssm, w_o, g_pre, conv_w8, conv_b, dt_bias, a_log, d_skip, attn_sinks,
                    g_ssm_norm, g_post)
    loss = lax.psum(r["loss"], ("x", "y", "c"))
    grad_x = r["dh"][PAD + N_META:][None]

    small8 = _small_pack(*r["small"], r["dh"])
    ra, r_att, r_ssm, r_o, rs = _exchange([
        _from_aligned_t(r["dw_alt"]), r["dw_att"].reshape(N_DEV, 128, D_MODEL),
        r["dw_ssm"].reshape(N_DEV, 256, D_MODEL), r["dw_o"].reshape(N_DEV, 128, D_MODEL), small8])

    res_in = [_unrows3(t) for t in _sum_adamw_rows3(ra, w_in3, m_in3, v_in3, "adamw_w_in")]
    res_att = [t[None] for t in _sum_adamw(r_att, w_out_att[0], m_w_out_att[0], v_w_out_att[0], 512, "adamw_w_att")]
    res_ssm = [t[None] for t in _sum_adamw(r_ssm, w_out_ssm[0], m_w_out_ssm[0], v_w_out_ssm[0], 512, "adamw_w_ssm")]
    res_o = [t[None] for t in _sum_adamw(r_o, w_out[0], m_w_out[0], v_w_out[0], 512, "adamw_w_o")]
    (res_gpre, res_convb, res_dtb, res_alog, res_dskip, res_sink, res_gssm, res_gpost, res_cw, res_meta) = _small_finish(
        rs, [(g_pre, m_g_pre, v_g_pre), (conv_b, m_conv_b, v_conv_b), (dt_bias, m_dt_bias, v_dt_bias),
             (a_log, m_a_log, v_a_log), (d_skip, m_d_skip, v_d_skip), (attn_sinks, m_attn_sinks, v_attn_sinks),
             (g_ssm_norm, m_g_ssm_norm, v_g_ssm_norm), (g_post, m_g_post, v_g_post),
             (conv_w[0], m_conv_w[0], v_conv_w[0]), (meta_tokens, m_meta_tokens, v_meta_tokens)])
    res_cw = [t[None] for t in res_cw]
    per_weight = [res_meta, res_gpre, res_in, res_cw, res_convb, res_dtb, res_alog, res_dskip, res_sink, res_gssm,
                  res_att, res_ssm, res_o, res_gpost]
    return (loss, grad_x, *[p[0] for p in per_weight], *[p[1] for p in per_weight], *[p[2] for p in per_weight],
            *[p[3] for p in per_weight])
```

```python
import functools
import math

import jax
import jax.numpy as jnp
from jax import lax
from jax.experimental import pallas as pl
from jax.experimental.pallas import tpu as pltpu

F32 = jnp.float32
BF16 = jnp.bfloat16
SDS = jax.ShapeDtypeStruct
HI = lax.Precision.HIGHEST
MESH = pl.DeviceIdType.MESH
ANY = pl.BlockSpec(memory_space=pl.ANY)

N_DEV = 8
D_MODEL = 1024
SEQ = 2048
N_META = 16
BLK = 128
PAD = 112
T = PAD + N_META + SEQ
NB = T // BLK
EPS = 1e-6
HEAD = 64
Q_HEADS = 16
KV_HEADS = 4
GROUP = 4
KV_W = 256
SSM_INNER = 2048
SSM_HEADS = 32
SSM_GROUPS = 4
GRP_W = 512
SSM_STATE = 128
CONV_DIM = 3072
IN_PROJ = 9760
SHARD_IN = IN_PROJ // N_DEV
NEG = -1e30

C_Q, C_ZA, C_GA, C_GS, C_ZS, C_XBC, C_K, C_V, C_DT = 0, 1024, 2048, 3072, 4096, 6144, 9216, 9472, 9728
PW = 9856
R_Q, R_K, R_V, R_ZA, R_ZS, R_XBC, R_DT, R_GA, R_GS = 0, 1024, 1280, 1536, 2560, 4608, 7680, 7712, 8736

ADAM_LR, ADAM_B1, ADAM_B2, ADAM_EPS, ADAM_WD, ADAM_STEP = 0.001, 0.9, 0.999, 1e-08, 0.01, 10

VMEM_LIMIT = 56 * 1024 * 1024


def _cparams():
    return pltpu.CompilerParams(vmem_limit_bytes=VMEM_LIMIT)


def _silu(x):
    return x * jax.nn.sigmoid(x)


def _dsilu(x):
    s = jax.nn.sigmoid(x)
    return s * (1.0 + x * (1.0 - s))


def _matmul(a, b, mode, out_dtype, tm, tn, tk, name):
    if mode == "nn":
        (m, k), n = a.shape, b.shape[1]
        a_spec = pl.BlockSpec((tm, tk), lambda i, j, kk: (i, kk))
        b_spec = pl.BlockSpec((tk, tn), lambda i, j, kk: (kk, j))
        dims = (((1,), (0,)), ((), ()))
    elif mode == "nt":
        (m, k), n = a.shape, b.shape[0]
        a_spec = pl.BlockSpec((tm, tk), lambda i, j, kk: (i, kk))
        b_spec = pl.BlockSpec((tn, tk), lambda i, j, kk: (j, kk))
        dims = (((1,), (1,)), ((), ()))
    else:
        (k, m), n = a.shape, b.shape[1]
        a_spec = pl.BlockSpec((tk, tm), lambda i, j, kk: (kk, i))
        b_spec = pl.BlockSpec((tk, tn), lambda i, j, kk: (kk, j))
        dims = (((0,), (0,)), ((), ()))
    assert m % tm == 0 and n % tn == 0 and k % tk == 0, (a.shape, b.shape, tm, tn, tk)
    nk = k // tk

    def body(a_ref, b_ref, o_ref, acc_ref):
        kk = pl.program_id(2)
        part = lax.dot_general(a_ref[...], b_ref[...], dims, preferred_element_type=F32)

        @pl.when(kk == 0)
        def _():
            acc_ref[...] = part

        @pl.when(kk > 0)
        def _():
            acc_ref[...] += part

        @pl.when(kk == nk - 1)
        def _():
            o_ref[...] = acc_ref[...].astype(out_dtype)

    return pl.pallas_call(
        body, grid=(m // tm, n // tn, nk), in_specs=[a_spec, b_spec],
        out_specs=pl.BlockSpec((tm, tn), lambda i, j, kk: (i, j)),
        out_shape=SDS((m, n), out_dtype), scratch_shapes=[pltpu.VMEM((tm, tn), F32)],
        compiler_params=_cparams(), name=name)(a, b)


def _norm_u(h, g_pre):
    def body(h_ref, g_ref, u_ref):
        x = h_ref[...]
        r = lax.rsqrt(jnp.mean(x * x, axis=-1, keepdims=True) + EPS)
        u_ref[...] = (x * r * g_ref[...]).astype(BF16)

    return pl.pallas_call(
        body, grid=(NB,),
        in_specs=[pl.BlockSpec((BLK, D_MODEL), lambda i: (i, 0)), pl.BlockSpec((1, D_MODEL), lambda i: (0, 0))],
        out_specs=pl.BlockSpec((BLK, D_MODEL), lambda i: (i, 0)),
        out_shape=SDS((T, D_MODEL), BF16), name="norm_u")(h, g_pre)


def _norm_bwd(h, g_pre, du, dres):
    def body(h_ref, g_ref, du_ref, dres_ref, dh_ref, dg_ref):
        i = pl.program_id(0)
        x = h_ref[...]
        g = g_ref[...]
        du_ = du_ref[...]
        r = lax.rsqrt(jnp.mean(x * x, axis=-1, keepdims=True) + EPS)
        gd = g * du_
        dx = r * gd - x * (r * r * r) * jnp.mean(x * gd, axis=-1, keepdims=True)
        dh_ref[...] = dx + dres_ref[...]
        part = jnp.sum(du_ * x * r, axis=0, keepdims=True)

        @pl.when(i == 0)
        def _():
            dg_ref[...] = jnp.zeros_like(dg_ref)

        dg_ref[0:1, :] += part

    row = pl.BlockSpec((BLK, D_MODEL), lambda i: (i, 0))
    return pl.pallas_call(
        body, grid=(NB,),
        in_specs=[row, pl.BlockSpec((1, D_MODEL), lambda i: (0, 0)), row, row],
        out_specs=[row, pl.BlockSpec((8, D_MODEL), lambda i: (0, 0))],
        out_shape=[SDS((T, D_MODEL), F32), SDS((8, D_MODEL), F32)], name="norm_bwd")(h, g_pre, du, dres)


def _lane_pick(row, h):
    lane = lax.broadcasted_iota(jnp.int32, row.shape, 1)
    return jnp.sum(jnp.where(lane == h, row, 0.0), axis=1, keepdims=True)


def _attn_fn(q4s, kcats, vcats, kms, vms, sinks, n):
    r = lax.broadcasted_iota(jnp.int32, (GROUP * BLK, 2 * BLK), 0)
    s = lax.broadcasted_iota(jnp.int32, (GROUP * BLK, 2 * BLK), 1)
    i = jnp.bitwise_and(r, BLK - 1)
    gi = jnp.right_shift(r, 7)
    rel = i - s + BLK
    k_pos = n * BLK - BLK + s
    band_ok = (rel >= 0) & (rel < BLK) & (k_pos >= PAD + N_META)
    relf = rel.astype(F32)
    rm = lax.broadcasted_iota(jnp.int32, (GROUP * BLK, N_META), 0)
    mm = lax.broadcasted_iota(jnp.int32, (GROUP * BLK, N_META), 1)
    meta_ok = (PAD + mm) <= (n * BLK + jnp.bitwise_and(rm, BLK - 1))
    gcol = jnp.right_shift(lax.broadcasted_iota(jnp.int32, (GROUP * BLK, 1), 0), 7)
    outs = []
    for kh in range(KV_HEADS):
        slopes = [2.0 ** (-8.0 * (kh * GROUP + g + 1) / Q_HEADS) for g in range(GROUP)]
        slope = jnp.where(gi == 0, slopes[0], jnp.where(gi == 1, slopes[1], jnp.where(gi == 2, slopes[2], slopes[3])))
        sk = [_lane_pick(sinks, kh * GROUP + g) for g in range(GROUP)]
        sink = jnp.where(gcol == 0, sk[0], jnp.where(gcol == 1, sk[1], jnp.where(gcol == 2, sk[2], sk[3])))
        qb = (q4s[kh] * (HEAD ** -0.5)).astype(BF16)
        sb = lax.dot_general(qb, kcats[kh].astype(BF16), (((1,), (1,)), ((), ())), preferred_element_type=F32)
        sb = jnp.where(band_ok, sb - slope * relf, NEG)
        sm = lax.dot_general(qb, kms[kh].astype(BF16), (((1,), (1,)), ((), ())), preferred_element_type=F32)
        sm = jnp.where(meta_ok, sm, NEG)
        mx = jnp.maximum(jnp.maximum(jnp.max(sb, axis=1, keepdims=True), jnp.max(sm, axis=1, keepdims=True)), sink)
        mx = lax.stop_gradient(mx)
        eb = jnp.exp(sb - mx)
        em = jnp.exp(sm - mx)
        es = jnp.exp(sink - mx)
        inv = 1.0 / (jnp.sum(eb, axis=1, keepdims=True) + jnp.sum(em, axis=1, keepdims=True) + es)
        pb = (eb * inv).astype(BF16)
        pm = (em * inv).astype(BF16)
        o4 = (jnp.dot(pm, vms[kh].astype(BF16), preferred_element_type=F32)
              + jnp.dot(pb, vcats[kh].astype(BF16), preferred_element_type=F32))
        outs.append(o4)
    return outs


def _attn_specs():
    prev = lambda n: jnp.maximum(n - 1, 0)
    return [
        pl.BlockSpec((BLK, D_MODEL), lambda n: (n, C_Q // D_MODEL)),
        pl.BlockSpec((BLK, KV_W), lambda n: (prev(n), C_K // KV_W)),
        pl.BlockSpec((BLK, KV_W), lambda n: (n, C_K // KV_W)),
        pl.BlockSpec((BLK, KV_W), lambda n: (prev(n), C_V // KV_W)),
        pl.BlockSpec((BLK, KV_W), lambda n: (n, C_V // KV_W)),
        pl.BlockSpec((N_META, KV_W), lambda n: (PAD // N_META, C_K // KV_W)),
        pl.BlockSpec((N_META, KV_W), lambda n: (PAD // N_META, C_V // KV_W)),
        pl.BlockSpec((1, 128), lambda n: (0, 0)),
    ]


def _attn_load(q_ref, kp_ref, kc_ref, vp_ref, vc_ref, km_ref, vm_ref):
    q4s, kcats, vcats, kms, vms = [], [], [], [], []
    for kh in range(KV_HEADS):
        q4s.append(jnp.concatenate(
            [q_ref[:, (kh * GROUP + g) * HEAD:(kh * GROUP + g + 1) * HEAD] for g in range(GROUP)], axis=0))
        cs = slice(kh * HEAD, (kh + 1) * HEAD)
        kcats.append(jnp.concatenate([kp_ref[:, cs], kc_ref[:, cs]], axis=0))
        vcats.append(jnp.concatenate([vp_ref[:, cs], vc_ref[:, cs]], axis=0))
        kms.append(km_ref[:, cs])
        vms.append(vm_ref[:, cs])
    return q4s, kcats, vcats, kms, vms


def _attn_fwd(proj, sinks):
    def body(q_ref, kp_ref, kc_ref, vp_ref, vc_ref, km_ref, vm_ref, s_ref, o_ref):
        n = pl.program_id(0)
        args = _attn_load(q_ref, kp_ref, kc_ref, vp_ref, vc_ref, km_ref, vm_ref)
        outs = _attn_fn(*args, s_ref[...], n)
        for kh in range(KV_HEADS):
            for g in range(GROUP):
                hh = kh * GROUP + g
                o_ref[:, hh * HEAD:(hh + 1) * HEAD] = outs[kh][g * BLK:(g + 1) * BLK]

    return pl.pallas_call(
        body, grid=(NB,), in_specs=_attn_specs(),
        out_specs=pl.BlockSpec((BLK, D_MODEL), lambda n: (n, 0)),
        out_shape=SDS((T, D_MODEL), F32), name="attn_fwd")(proj, proj, proj, proj, proj, proj, proj, sinks)


def _attn_bwd(proj, sinks, do):
    def body(q_ref, kp_ref, kc_ref, vp_ref, vc_ref, km_ref, vm_ref, s_ref, do_ref, dq_ref, dk_ref, dv_ref, ds_ref):
        n = pl.program_id(0)

        @pl.when(n == 0)
        def _():
            dk_ref[...] = jnp.zeros_like(dk_ref)
            dv_ref[...] = jnp.zeros_like(dv_ref)
            ds_ref[...] = jnp.zeros_like(ds_ref)

        args = _attn_load(q_ref, kp_ref, kc_ref, vp_ref, vc_ref, km_ref, vm_ref)
        _, vjp = jax.vjp(lambda a, b, c, d, e, f: _attn_fn(a, b, c, d, e, f, n), *args, s_ref[...])
        cot = [jnp.concatenate([do_ref[:, (kh * GROUP + g) * HEAD:(kh * GROUP + g + 1) * HEAD] for g in range(GROUP)],
                               axis=0) for kh in range(KV_HEADS)]
        dq4s, dkcats, dvcats, dkms, dvms, dsk = vjp(cot)
        ds_ref[0:1, :] += dsk
        cur = pl.ds(pl.multiple_of(n * BLK, BLK), BLK)
        meta = slice(PAD, PAD + N_META)
        for kh in range(KV_HEADS):
            cs = slice(kh * HEAD, (kh + 1) * HEAD)
            for g in range(GROUP):
                hh = kh * GROUP + g
                dq_ref[:, hh * HEAD:(hh + 1) * HEAD] = dq4s[kh][g * BLK:(g + 1) * BLK]
            dk_ref[cur, cs] += dkcats[kh][BLK:]
            dv_ref[cur, cs] += dvcats[kh][BLK:]
            dk_ref[meta, cs] += dkms[kh]
            dv_ref[meta, cs] += dvms[kh]

        @pl.when(n > 0)
        def _():
            prv = pl.ds(pl.multiple_of((n - 1) * BLK, BLK), BLK)
            for kh in range(KV_HEADS):
                cs = slice(kh * HEAD, (kh + 1) * HEAD)
                dk_ref[prv, cs] += dkcats[kh][:BLK]
                dv_ref[prv, cs] += dvcats[kh][:BLK]

    full_kv = pl.BlockSpec((T, KV_W), lambda n: (0, 0))
    return pl.pallas_call(
        body, grid=(NB,),
        in_specs=_attn_specs() + [pl.BlockSpec((BLK, D_MODEL), lambda n: (n, 0))],
        out_specs=[pl.BlockSpec((BLK, D_MODEL), lambda n: (n, 0)), full_kv, full_kv,
                   pl.BlockSpec((8, 128), lambda n: (0, 0))],
        out_shape=[SDS((T, D_MODEL), F32), SDS((T, KV_W), F32), SDS((T, KV_W), F32), SDS((8, 128), F32)],
        name="attn_bwd")(proj, proj, proj, proj, proj, proj, proj, sinks, do)


def _conv_taps(xp, w, rows):
    return (w[0:1] * xp[5:5 + rows] + w[1:2] * xp[6:6 + rows] + w[2:3] * xp[7:7 + rows] + w[3:4] * xp[8:8 + rows])


def _conv_fwd(proj, conv_w, conv_b):
    CONV_CB = CONV_DIM
    ncb = CONV_DIM // CONV_CB
    cb0 = C_XBC // CONV_CB

    def body(tail_ref, cur_ref, w_ref, b_ref, o_ref):
        n = pl.program_id(1)
        tail = jnp.where(n > 0, tail_ref[...], 0.0)
        xp = jnp.concatenate([tail, cur_ref[...]], axis=0)
        conv = _conv_taps(xp, w_ref[...], BLK) + b_ref[...]
        row = n * BLK + lax.broadcasted_iota(jnp.int32, (BLK, 1), 0)
        o_ref[...] = jnp.where(row >= PAD, _silu(conv), 0.0)

    return pl.pallas_call(
        body, grid=(ncb, NB),
        in_specs=[pl.BlockSpec((8, CONV_CB), lambda j, n: (jnp.maximum(n * (BLK // 8) - 1, 0), cb0 + j)),
                  pl.BlockSpec((BLK, CONV_CB), lambda j, n: (n, cb0 + j)),
                  pl.BlockSpec((8, CONV_CB), lambda j, n: (0, j)),
                  pl.BlockSpec((1, CONV_CB), lambda j, n: (0, j))],
        out_specs=pl.BlockSpec((BLK, CONV_CB), lambda j, n: (n, j)),
        out_shape=SDS((T, CONV_DIM), F32), name="conv_fwd")(proj, proj, conv_w, conv_b)


def _conv_bwd(proj, conv_w, conv_b, dact, ch0, name):
    width = dact.shape[1]
    CONV_CB = width
    ncb = width // CONV_CB
    cb0 = (C_XBC + ch0) // CONV_CB
    wb0 = ch0 // CONV_CB
    last8 = T // 8 - 1

    def body(tail_ref, cur_ref, nxt_ref, w_ref, b_ref, dcur_ref, dnxt_ref, dx_ref, dw_ref, db_ref):
        n = pl.program_id(1)
        w = w_ref[...]
        tail = jnp.where(n > 0, tail_ref[...], 0.0)
        xp = jnp.concatenate([tail, cur_ref[...], nxt_ref[...]], axis=0)
        conv = _conv_taps(xp, w, BLK + 8) + b_ref[...]
        dext = jnp.concatenate([dcur_ref[...], jnp.where(n < NB - 1, dnxt_ref[...], 0.0)], axis=0)
        row = n * BLK + lax.broadcasted_iota(jnp.int32, (BLK + 8, 1), 0)
        dconv = jnp.where(row >= PAD, dext * _dsilu(conv), 0.0)
        dx = (w[0:1] * dconv[3:3 + BLK] + w[1:2] * dconv[2:2 + BLK] + w[2:3] * dconv[1:1 + BLK]
              + w[3:4] * dconv[0:BLK])
        dx_ref[...] = dx.astype(BF16)
        dc = dconv[0:BLK]
        dws = [jnp.sum(dc * xp[5 + k:5 + k + BLK], axis=0, keepdims=True) for k in range(4)]
        dwp = jnp.concatenate(dws + [jnp.zeros((4, CONV_CB), F32)], axis=0)
        dbp = jnp.sum(dc, axis=0, keepdims=True)

        @pl.when(n == 0)
        def _():
            dw_ref[...] = dwp
            db_ref[...] = jnp.concatenate([dbp, jnp.zeros((7, CONV_CB), F32)], axis=0)

        @pl.when(n > 0)
        def _():
            dw_ref[...] += dwp
            db_ref[0:1, :] += dbp

    return pl.pallas_call(
        body, grid=(ncb, NB),
        in_specs=[pl.BlockSpec((8, CONV_CB), lambda j, n: (jnp.maximum(n * (BLK // 8) - 1, 0), cb0 + j)),
                  pl.BlockSpec((BLK, CONV_CB), lambda j, n: (n, cb0 + j)),
                  pl.BlockSpec((8, CONV_CB), lambda j, n: (jnp.minimum((n + 1) * (BLK // 8), last8), cb0 + j)),
                  pl.BlockSpec((8, CONV_CB), lambda j, n: (0, wb0 + j)),
                  pl.BlockSpec((1, CONV_CB), lambda j, n: (0, wb0 + j)),
                  pl.BlockSpec((BLK, CONV_CB), lambda j, n: (n, j)),
                  pl.BlockSpec((8, CONV_CB), lambda j, n: (jnp.minimum((n + 1) * (BLK // 8), last8), j))],
        out_specs=[pl.BlockSpec((BLK, CONV_CB), lambda j, n: (n, j)),
                   pl.BlockSpec((8, CONV_CB), lambda j, n: (0, j)),
                   pl.BlockSpec((8, CONV_CB), lambda j, n: (0, j))],
        out_shape=[SDS((T, width), BF16), SDS((8, width), F32), SDS((8, width), F32)],
        name=name)(proj, proj, proj, conv_w, conv_b, dact, dact)


HPG = SSM_HEADS // SSM_GROUPS


def _iota(shape, dim):
    return lax.broadcasted_iota(jnp.int32, shape, dim)


def _mm(a, b, ca=1, cb=0):
    return lax.dot_general(a.astype(BF16), b.astype(BF16), (((ca,), (cb,)), ((), ())), preferred_element_type=F32)


def _split3(v):
    hi = v.astype(BF16)
    r1 = v - hi.astype(F32)
    mid = r1.astype(BF16)
    lo = (r1 - mid.astype(F32)).astype(BF16)
    return hi, mid, lo


def _sel_r(parts, onehot, ca=1, cb=0):
    out = lax.dot_general(parts[0], onehot, (((ca,), (cb,)), ((), ())), preferred_element_type=F32)
    for p in parts[1:]:
        out = out + lax.dot_general(p, onehot, (((ca,), (cb,)), ((), ())), preferred_element_type=F32)
    return out


def _sel_l(onehot, parts):
    out = jnp.dot(onehot, parts[0], preferred_element_type=F32)
    for p in parts[1:]:
        out = out + jnp.dot(onehot, p, preferred_element_type=F32)
    return out


def _rows8(*rows):
    r = _iota((8, rows[0].shape[1]), 0)
    out = jnp.zeros((8, rows[0].shape[1]), F32)
    for k, v in enumerate(rows):
        out = jnp.where(r == k, v, out)
    return out


def _ssd_forward(x, z, bm, cm, dt_raw, st_prev, dtb, alog, dskip, gn, g, cst_scr):
    li, si = _iota((BLK, BLK), 0), _iota((BLK, BLK), 1)
    dt_all = jax.nn.softplus(dt_raw + dtb)
    a_row = -jnp.exp(alog)
    a_all = dt_all * a_row
    cs_all = _sel_l((li >= si).astype(BF16), _split3(a_all))
    cs_parts = _split3(cs_all)
    spread = (_iota((BLK, GRP_W), 0) == g * HPG + jnp.right_shift(_iota((BLK, GRP_W), 1), 6)).astype(BF16)
    dt_e = _sel_r(_split3(dt_all), spread)
    cs_e = _sel_r(cs_parts, spread)
    d_e = _sel_r(_split3(_rows8(dskip)), spread)[0:1]
    cs_last_e = jnp.sum(jnp.where(_iota((BLK, GRP_W), 0) == BLK - 1, cs_e, 0.0), axis=0, keepdims=True)
    p_e = jnp.exp(cs_e)
    w_e = jnp.exp(cs_last_e - cs_e)
    cd_e = jnp.exp(cs_last_e)
    xr = x * dt_e
    cst_scr[...] = cs_all.T
    cst_g = cst_scr[pl.ds(pl.multiple_of(g * HPG, HPG), HPG), :]
    own = jnp.right_shift(_iota((HPG, HPG * BLK), 1), 7) == _iota((HPG, HPG * BLK), 0)
    ownf = own.astype(F32)
    q_rows = [ownf, ownf, ownf] + [jnp.where(own, jnp.concatenate([p.astype(F32)] * HPG, axis=1), 0.0)
                                   for p in _split3(cst_g)]
    q2 = jnp.concatenate(q_rows + [jnp.zeros((BLK - 6 * HPG, HPG * BLK), F32)], axis=0).astype(BF16)
    lane1 = _iota((1, BLK), 1)
    p2 = jnp.where((lane1 >= 3 * HPG) & (lane1 < 6 * HPG), -1.0, 0.0)
    for k, part in enumerate(cs_parts):
        pick = ((li == g * HPG + si - k * HPG) & (si >= k * HPG) & (si < (k + 1) * HPG)).astype(BF16)
        p2 = p2 + jnp.dot(part, pick, preferred_element_type=F32)
    dmat = jnp.dot(p2.astype(BF16), q2, preferred_element_type=F32)
    causal = _iota((BLK, HPG * BLK), 0) >= jnp.bitwise_and(_iota((BLK, HPG * BLK), 1), BLK - 1)
    lam = jnp.exp(jnp.where(causal, dmat, NEG))
    gmat = _mm(cm, bm, 1, 1)
    m_all = lam * jnp.concatenate([gmat] * HPG, axis=1)
    mb = m_all.astype(BF16)
    lo = _iota((BLK, BLK), 1) < HEAD
    xrb = xr.astype(BF16)
    zero = jnp.zeros((BLK, BLK), BF16)
    bds, yd = [], []
    for i in range(HPG // 2):
        t = xrb[:, BLK * i:BLK * (i + 1)]
        bd = jnp.concatenate([jnp.where(lo, t, zero), jnp.where(lo, zero, t)], axis=0)
        bds.append(bd)
        yd.append(jnp.dot(mb[:, 2 * BLK * i:2 * BLK * (i + 1)], bd, preferred_element_type=F32))
    cs_st = _mm(cm, st_prev)
    y = jnp.concatenate(yd, axis=1) + cs_st * p_e + d_e * x
    xrw = xr * w_e
    st_new = cd_e * st_prev + _mm(bm, xrw, 0, 0)
    yz = y * _silu(z)
    rn = lax.rsqrt(jnp.sum(yz * yz, axis=1, keepdims=True) / GRP_W + EPS)
    return dict(out=yz * rn * gn, st_new=st_new, dt_all=dt_all, a_row=a_row, dt_e=dt_e, d_e=d_e, p_e=p_e, w_e=w_e,
                cd_e=cd_e, xr=xr, xrw=xrw, lam=lam, m_all=m_all, mb=mb, bds=bds, cs_st=cs_st, y=y, yz=yz, rn=rn, lo=lo)


def _ssd_backward(f, x, z, bm, cm, dt_raw, st_prev, dtb, gn, g, dout, dst_next, cst_scr):
    li, si = _iota((BLK, BLK), 0), _iota((BLK, BLK), 1)
    yz, rn, y, p_e, w_e, cd_e, xr = f["yz"], f["rn"], f["y"], f["p_e"], f["w_e"], f["cd_e"], f["xr"]
    dgn = jnp.sum(dout * yz * rn, axis=0, keepdims=True)
    t = dout * gn
    dyz = rn * t - yz * (rn * rn * rn) * (jnp.sum(yz * t, axis=1, keepdims=True) / GRP_W)
    dy = dyz * _silu(z)
    dz = dyz * y * _dsilu(z)
    dx = f["d_e"] * dy
    dd_e = jnp.sum(dy * x, axis=0, keepdims=True)
    dcsst = dy * p_e
    dp_e = dy * f["cs_st"]
    dcm = _mm(dcsst, st_prev, 1, 1)
    dst_prev = _mm(cm, dcsst, 0, 0) + cd_e * dst_next
    dcd_e = jnp.sum(dst_next * st_prev, axis=0, keepdims=True)
    dbm = _mm(f["xrw"], dst_next, 1, 1)
    dxrw = _mm(bm, dst_next)
    dxr = dxrw * w_e
    dw_e = dxrw * xr
    dyb = dy.astype(BF16)
    dms, dxr_d = [], []
    for i in range(HPG // 2):
        dyp = dyb[:, BLK * i:BLK * (i + 1)]
        dms.append(lax.dot_general(dyp, f["bds"][i], (((1,), (1,)), ((), ())), preferred_element_type=F32))
        r = lax.dot_general(f["mb"][:, 2 * BLK * i:2 * BLK * (i + 1)], dyp, (((0,), (0,)), ((), ())),
                            preferred_element_type=F32)
        dxr_d.append(jnp.where(f["lo"], r[0:BLK], r[BLK:2 * BLK]))
    dm_all = jnp.concatenate(dms, axis=1)
    dxr = dxr + jnp.concatenate(dxr_d, axis=1)
    dlg = dm_all * f["lam"]
    dg = dlg[:, 0:BLK]
    for j in range(1, HPG):
        dg = dg + dlg[:, BLK * j:BLK * (j + 1)]
    dcm = dcm + _mm(dg, bm)
    dbm = dbm + _mm(dg, cm, 0, 0)
    q_all = dm_all * f["m_all"]
    col_sums = jnp.sum(q_all, axis=0, keepdims=True)
    cst_scr[...] = jnp.zeros_like(cst_scr)
    cst_scr[pl.ds(pl.multiple_of(g * HPG, HPG), HPG), :] = _rows8(
        *[col_sums[:, BLK * j:BLK * (j + 1)] for j in range(HPG)])
    dcs = -cst_scr[...].T
    for j in range(HPG):
        dcs = dcs + jnp.where(si == g * HPG + j,
                              jnp.sum(q_all[:, BLK * j:BLK * (j + 1)], axis=1, keepdims=True), 0.0)
    unspread = (_iota((GRP_W, BLK), 1) == g * HPG + jnp.right_shift(_iota((GRP_W, BLK), 0), 6)).astype(BF16)
    dww = dw_e * w_e
    per_head = _sel_r(_split3(jnp.concatenate([dp_e * p_e - dww, dxr * x], axis=0)), unspread)
    last = _sel_r(_split3(_rows8(jnp.sum(dww, axis=0, keepdims=True) + dcd_e * cd_e, dd_e)), unspread)
    dcs = dcs + per_head[0:BLK] + jnp.where(li == BLK - 1, last[0:1], 0.0)
    da = _sel_l((si >= li).astype(BF16), _split3(dcs))
    ddt_all = da * f["a_row"] + per_head[BLK:2 * BLK]
    dalog = jnp.sum(da * f["dt_all"], axis=0, keepdims=True) * f["a_row"]
    dx = dx + dxr * f["dt_e"]
    ddt_raw = ddt_all * jax.nn.sigmoid(dt_raw + dtb)
    ddtb = jnp.sum(ddt_raw, axis=0, keepdims=True)
    ddskip = last[1:2]
    return dict(dx=dx, dz=dz, dbm=dbm, dcm=dcm, ddt_raw=ddt_raw, dst_prev=dst_prev, ddtb=ddtb, dalog=dalog,
                ddskip=ddskip, dgn=dgn)


def _ssd_in_specs(rev):
    cidx = (lambda c: NB - 1 - c) if rev else (lambda c: c)
    return [
        pl.BlockSpec((BLK, GRP_W), lambda g, c: (cidx(c), g)),
        pl.BlockSpec((BLK, SSM_STATE), lambda g, c: (cidx(c), SSM_INNER // SSM_STATE + g)),
        pl.BlockSpec((BLK, SSM_STATE), lambda g, c: (cidx(c), SSM_INNER // SSM_STATE + SSM_GROUPS + g)),
        pl.BlockSpec((BLK, 128), lambda g, c: (cidx(c), C_DT // 128)),
        pl.BlockSpec((BLK, GRP_W), lambda g, c: (cidx(c), C_ZS // GRP_W + g)),
        pl.BlockSpec((1, 128), lambda g, c: (0, 0)),
        pl.BlockSpec((1, 128), lambda g, c: (0, 0)),
        pl.BlockSpec((1, 128), lambda g, c: (0, 0)),
        pl.BlockSpec((1, GRP_W), lambda g, c: (0, g)),
    ]


def _ssd_fwd(xbc_act, proj, dt_bias, a_log, d_skip, g_norm):
    def body(xs_ref, b_ref, c_ref, dt_ref, z_ref, dtb_ref, al_ref, dsk_ref, gn_ref, y_ref, st_ref, s_scr, cst_scr):
        g = pl.program_id(0)
        c = pl.program_id(1)

        @pl.when(c == 0)
        def _():
            s_scr[...] = jnp.zeros_like(s_scr)

        st_prev = s_scr[...]
        st_ref[0, 0] = st_prev
        f = _ssd_forward(xs_ref[...], z_ref[...], b_ref[...], c_ref[...], dt_ref[...], st_prev, dtb_ref[...],
                         al_ref[...], dsk_ref[...], gn_ref[...], g, cst_scr)
        y_ref[...] = f["out"].astype(BF16)
        s_scr[...] = f["st_new"]

    return pl.pallas_call(
        body, grid=(SSM_GROUPS, NB), in_specs=_ssd_in_specs(False),
        out_specs=[pl.BlockSpec((BLK, GRP_W), lambda g, c: (c, g)),
                   pl.BlockSpec((1, 1, SSM_STATE, GRP_W), lambda g, c: (g, c, 0, 0))],
        out_shape=[SDS((T, SSM_INNER), BF16), SDS((SSM_GROUPS, NB, SSM_STATE, GRP_W), F32)],
        scratch_shapes=[pltpu.VMEM((SSM_STATE, GRP_W), F32), pltpu.VMEM((BLK, BLK), F32)],
        compiler_params=_cparams(),
        name="ssd_fwd")(xbc_act, xbc_act, xbc_act, proj, proj, dt_bias, a_log, d_skip, g_norm)


def _ssd_bwd(xbc_act, proj, dt_bias, a_log, d_skip, g_norm, states, dy):
    def body(xs_ref, b_ref, c_ref, dt_ref, z_ref, dtb_ref, al_ref, dsk_ref, gn_ref, st_ref, dy_ref,
             dxs_ref, db_ref, dc_ref, ddt_ref, dz_ref, ddtb_ref, dal_ref, ddsk_ref, dgn_ref, ds_scr, cst_scr):
        g = pl.program_id(0)
        c = pl.program_id(1)

        @pl.when(c == 0)
        def _():
            ds_scr[...] = jnp.zeros_like(ds_scr)
            dgn_ref[...] = jnp.zeros_like(dgn_ref)

        @pl.when((c == 0) & (g == 0))
        def _():
            ddtb_ref[...] = jnp.zeros_like(ddtb_ref)
            dal_ref[...] = jnp.zeros_like(dal_ref)
            ddsk_ref[...] = jnp.zeros_like(ddsk_ref)

        x, z, bm, cm, dt_raw, st_prev = xs_ref[...], z_ref[...], b_ref[...], c_ref[...], dt_ref[...], st_ref[0, 0]
        f = _ssd_forward(x, z, bm, cm, dt_raw, st_prev, dtb_ref[...], al_ref[...], dsk_ref[...], gn_ref[...], g,
                         cst_scr)
        d = _ssd_backward(f, x, z, bm, cm, dt_raw, st_prev, dtb_ref[...], gn_ref[...], g, dy_ref[...], ds_scr[...],
                          cst_scr)
        dxs_ref[...] = d["dx"]
        dz_ref[...] = d["dz"].astype(BF16)
        ds_scr[...] = d["dst_prev"]
        db_ref[...] = d["dbm"]
        dc_ref[...] = d["dcm"]
        ddt_ref[...] = d["ddt_raw"]
        dgn_ref[0:1, :] += d["dgn"]
        ddtb_ref[0:1, :] += d["ddtb"]
        dal_ref[0:1, :] += d["dalog"]
        ddsk_ref[0:1, :] += d["ddskip"]

    rc = lambda c: NB - 1 - c
    small = pl.BlockSpec((8, 128), lambda g, c: (0, 0))
    return pl.pallas_call(
        body, grid=(SSM_GROUPS, NB),
        in_specs=_ssd_in_specs(True) + [
            pl.BlockSpec((1, 1, SSM_STATE, GRP_W), lambda g, c: (g, rc(c), 0, 0)),
            pl.BlockSpec((BLK, GRP_W), lambda g, c: (rc(c), g))],
        out_specs=[pl.BlockSpec((BLK, GRP_W), lambda g, c: (rc(c), g)),
                   pl.BlockSpec((BLK, SSM_STATE), lambda g, c: (rc(c), g)),
                   pl.BlockSpec((BLK, SSM_STATE), lambda g, c: (rc(c), g)),
                   pl.BlockSpec((BLK, 128), lambda g, c: (rc(c), g)),
                   pl.BlockSpec((BLK, GRP_W), lambda g, c: (rc(c), g)),
                   small, small, small,
                   pl.BlockSpec((8, GRP_W), lambda g, c: (0, g))],
        out_shape=[SDS((T, SSM_INNER), F32), SDS((T, GRP_W), F32), SDS((T, GRP_W), F32), SDS((T, GRP_W), F32),
                   SDS((T, SSM_INNER), BF16), SDS((8, 128), F32), SDS((8, 128), F32), SDS((8, 128), F32),
                   SDS((8, SSM_INNER), F32)],
        scratch_shapes=[pltpu.VMEM((SSM_STATE, GRP_W), F32), pltpu.VMEM((BLK, BLK), F32)],
        compiler_params=_cparams(),
        name="ssd_bwd")(xbc_act, xbc_act, xbc_act, proj, proj, dt_bias, a_log, d_skip, g_norm, states, dy)


def _post_a(o, proj, sn, w_att, w_ssm, w_o):
    def body(o_ref, za_ref, ga_ref, gs_ref, sn_ref, wa_ref, ws_ref, wo_ref, a_ref, mg_ref, ya_ref, ys_ref, out_ref):
        a = (o_ref[...] * _silu(za_ref[...])).astype(BF16)
        a_ref[...] = a
        ya = jnp.dot(a, wa_ref[...], preferred_element_type=F32)
        ys = jnp.dot(sn_ref[...], ws_ref[...], preferred_element_type=F32)
        ya_ref[...] = ya
        ys_ref[...] = ys
        mg = (jax.nn.sigmoid(ga_ref[...]) * ya + jax.nn.sigmoid(gs_ref[...]) * ys).astype(BF16)
        mg_ref[...] = mg
        out_ref[...] = jnp.dot(mg, wo_ref[...], preferred_element_type=F32)

    row = pl.BlockSpec((BLK, D_MODEL), lambda i: (i, 0))
    pcol = lambda c0: pl.BlockSpec((BLK, D_MODEL), lambda i: (i, c0 // D_MODEL))
    full = lambda r: pl.BlockSpec((r, D_MODEL), lambda i: (0, 0))
    return pl.pallas_call(
        body, grid=(NB,),
        in_specs=[row, pcol(C_ZA), pcol(C_GA), pcol(C_GS), pl.BlockSpec((BLK, SSM_INNER), lambda i: (i, 0)),
                  full(D_MODEL), full(SSM_INNER), full(D_MODEL)],
        out_specs=[row, row, row, row, row],
        out_shape=[SDS((T, D_MODEL), BF16), SDS((T, D_MODEL), BF16), SDS((T, D_MODEL), F32), SDS((T, D_MODEL), F32),
                   SDS((T, D_MODEL), F32)],
        compiler_params=_cparams(), name="post_a")(o, proj, proj, proj, sn, w_att, w_ssm, w_o)


def _post_b(out, h, tgt, proj, ya, ys, o, g_post, w_att, w_ssm, w_o):
    def body(out_ref, h_ref, t_ref, za_ref, ga_ref, gs_ref, ya_ref, ys_ref, o_ref, gp_ref, wa_ref, ws_ref, wo_ref,
             loss_ref, dres_ref, dout_ref, dya_ref, dys_ref, dga_ref, dgs_ref, do_ref, dza_ref, dsn_ref, dgp_ref):
        i = pl.program_id(0)
        x = out_ref[...]
        gp = gp_ref[...]
        r = lax.rsqrt(jnp.mean(x * x, axis=-1, keepdims=True) + EPS)
        row = i * BLK + lax.broadcasted_iota(jnp.int32, (BLK, 1), 0)
        res = h_ref[...] + jnp.where(row >= PAD, x * r * gp, 0.0)
        live = row >= PAD + N_META
        err = jnp.where(live, res - t_ref[...], 0.0)
        lpart = 0.5 * jnp.sum(jnp.sum(err * err, axis=1, keepdims=True) / D_MODEL, axis=0, keepdims=True)
        dres = err / D_MODEL
        dres_ref[...] = dres
        gpart = jnp.sum(dres * x * r, axis=0, keepdims=True)

        @pl.when(i == 0)
        def _():
            loss_ref[...] = jnp.zeros_like(loss_ref)
            dgp_ref[...] = jnp.zeros_like(dgp_ref)

        loss_ref[...] += jnp.broadcast_to(lpart, loss_ref.shape)
        dgp_ref[0:1, :] += gpart
        gd = gp * dres
        dout = (r * gd - x * (r * r * r) * jnp.mean(x * gd, axis=-1, keepdims=True)).astype(BF16)
        dout_ref[...] = dout
        dmg = lax.dot_general(dout, wo_ref[...], (((1,), (1,)), ((), ())), preferred_element_type=F32)
        sga = jax.nn.sigmoid(ga_ref[...])
        sgs = jax.nn.sigmoid(gs_ref[...])
        dya = (dmg * sga).astype(BF16)
        dys = (dmg * sgs).astype(BF16)
        dya_ref[...] = dya
        dys_ref[...] = dys
        dga_ref[...] = (dmg * ya_ref[...] * sga * (1.0 - sga)).astype(BF16)
        dgs_ref[...] = (dmg * ys_ref[...] * sgs * (1.0 - sgs)).astype(BF16)
        da = lax.dot_general(dya, wa_ref[...], (((1,), (1,)), ((), ())), preferred_element_type=F32)
        za = za_ref[...]
        do_ref[...] = da * _silu(za)
        dza_ref[...] = (da * o_ref[...] * _dsilu(za)).astype(BF16)
        dsn_ref[...] = lax.dot_general(dys, ws_ref[...], (((1,), (1,)), ((), ())), preferred_element_type=F32)

    row = pl.BlockSpec((BLK, D_MODEL), lambda i: (i, 0))
    pcol = lambda c0: pl.BlockSpec((BLK, D_MODEL), lambda i: (i, c0 // D_MODEL))
    full = lambda r: pl.BlockSpec((r, D_MODEL), lambda i: (0, 0))
    small = pl.BlockSpec((8, D_MODEL), lambda i: (0, 0))
    return pl.pallas_call(
        body, grid=(NB,),
        in_specs=[row, row, row, pcol(C_ZA), pcol(C_GA), pcol(C_GS), row, row, row,
                  pl.BlockSpec((1, D_MODEL), lambda i: (0, 0)), full(D_MODEL), full(SSM_INNER), full(D_MODEL)],
        out_specs=[pl.BlockSpec((8, 128), lambda i: (0, 0)), row, row, row, row, row, row, row, row,
                   pl.BlockSpec((BLK, SSM_INNER), lambda i: (i, 0)), small],
        out_shape=[SDS((8, 128), F32), SDS((T, D_MODEL), F32), SDS((T, D_MODEL), BF16), SDS((T, D_MODEL), BF16),
                   SDS((T, D_MODEL), BF16), SDS((T, D_MODEL), BF16), SDS((T, D_MODEL), BF16), SDS((T, D_MODEL), F32),
                   SDS((T, D_MODEL), BF16), SDS((T, SSM_INNER), F32), SDS((8, D_MODEL), F32)],
        compiler_params=_cparams(), name="post_b")(out, h, tgt, proj, proj, proj, ya, ys, o, g_post, w_att, w_ssm, w_o)


def _assemble(dq, dza, dga, dgs, dzs, dxx, dxb, dxc, dk, dv, ddt4):
    def body(dq_ref, dza_ref, dga_ref, dgs_ref, dzs_ref, dxx_ref, dxb_ref, dxc_ref, dk_ref, dv_ref, ddt_ref, o_ref):
        o_ref[:, C_Q:C_Q + D_MODEL] = dq_ref[...].astype(BF16)
        o_ref[:, C_ZA:C_ZA + D_MODEL] = dza_ref[...]
        o_ref[:, C_GA:C_GA + D_MODEL] = dga_ref[...]
        o_ref[:, C_GS:C_GS + D_MODEL] = dgs_ref[...]
        o_ref[:, C_ZS:C_ZS + SSM_INNER] = dzs_ref[...]
        o_ref[:, C_XBC:C_XBC + SSM_INNER] = dxx_ref[...]
        o_ref[:, C_XBC + SSM_INNER:C_XBC + SSM_INNER + GRP_W] = dxb_ref[...]
        o_ref[:, C_XBC + SSM_INNER + GRP_W:C_XBC + CONV_DIM] = dxc_ref[...]
        o_ref[:, C_K:C_K + KV_W] = dk_ref[...].astype(BF16)
        o_ref[:, C_V:C_V + KV_W] = dv_ref[...].astype(BF16)
        d4 = ddt_ref[...]
        o_ref[:, C_DT:C_DT + 128] = (d4[:, 0:128] + d4[:, 128:256] + d4[:, 256:384] + d4[:, 384:512]).astype(BF16)

    spec = lambda w: pl.BlockSpec((BLK, w), lambda i: (i, 0))
    ins = [dq, dza, dga, dgs, dzs, dxx, dxb, dxc, dk, dv, ddt4]
    return pl.pallas_call(
        body, grid=(NB,), in_specs=[spec(a.shape[1]) for a in ins], out_specs=spec(PW),
        out_shape=SDS((T, PW), BF16), name="assemble")(*ins)


def _adamw_math(w, g, m, v):
    m = ADAM_B1 * m + (1.0 - ADAM_B1) * g
    v = ADAM_B2 * v + (1.0 - ADAM_B2) * (g * g)
    m_hat = m / (1.0 - ADAM_B1 ** ADAM_STEP)
    v_hat = v / (1.0 - ADAM_B2 ** ADAM_STEP)
    delta = -ADAM_LR * (m_hat / (jnp.sqrt(v_hat) + ADAM_EPS) + ADAM_WD * w)
    return delta, m, v


def _sum_adamw(recv, w, m, v, tc, name):
    rows, cols = w.shape
    assert cols % tc == 0

    def body(r_ref, w_ref, m_ref, v_ref, g_ref, d_ref, nm_ref, nv_ref):
        g = r_ref[0].astype(F32)
        for d in range(1, N_CHIP):
            g = g + r_ref[d].astype(F32)
        g_ref[...] = g
        delta, nm, nv = _adamw_math(w_ref[...], g, m_ref[...], v_ref[...])
        d_ref[...] = delta
        nm_ref[...] = nm
        nv_ref[...] = nv

    blk = pl.BlockSpec((rows, tc), lambda i: (0, i))
    return pl.pallas_call(
        body, grid=(cols // tc,),
        in_specs=[pl.BlockSpec((N_CHIP, rows, tc), lambda i: (0, 0, i)), blk, blk, blk],
        out_specs=[blk, blk, blk, blk], out_shape=[SDS((rows, cols), F32)] * 4,
        compiler_params=_cparams(), name=name)(recv, w, m, v)


def _sum_adamw_rows3(recv, w3, m3, v3, name):
    pairs = 61
    assert (SHARD_IN // 2) % pairs == 0

    def body(r_ref, w_ref, m_ref, v_ref, g_ref, d_ref, nm_ref, nv_ref):
        g = r_ref[0].astype(F32)
        for d in range(1, N_CHIP):
            g = g + r_ref[d].astype(F32)
        g = g.reshape(2 * pairs, ROW_TILES, 128)
        g_ref[...] = g
        delta, nm, nv = _adamw_math(w_ref[...], g, m_ref[...], v_ref[...])
        d_ref[...] = delta
        nm_ref[...] = nm
        nv_ref[...] = nv

    blk = pl.BlockSpec((2 * pairs, ROW_TILES, 128), lambda i: (i, 0, 0))
    return pl.pallas_call(
        body, grid=(SHARD_IN // 2 // pairs,),
        in_specs=[pl.BlockSpec((N_CHIP, pairs, 2 * ROW_TILES, 128), lambda i: (0, i, 0, 0)), blk, blk, blk],
        out_specs=[blk, blk, blk, blk], out_shape=[SDS(w3.shape, F32)] * 4,
        compiler_params=_cparams(), name=name)(recv, w3, m3, v3)


ROW_GPRE, ROW_CONVB, ROW_DTB, ROW_ALOG, ROW_DSKIP, ROW_SINK, ROW_GSSM, ROW_GPOST = 0, 1, 4, 5, 6, 7, 8, 10
REP_ROWS, ROW_CONVW, ROW_META, SM_ROWS = 16, 16, 24, 40
CW_SHARD = CONV_DIM // N_DEV
META_SHARD = D_MODEL // N_DEV


def _small_pack(dgpre, dbx, dbb, dbc, ddtb, dal, ddsk, dsink, dgn, dgp, dwx, dwb, dwc, dh):
    def body(dgpre_ref, dbx_ref, dbb_ref, dbc_ref, ddtb_ref, dal_ref, ddsk_ref, dsink_ref, dgn_ref, dgp_ref,
             dwx_ref, dwb_ref, dwc_ref, dh_ref, o_ref, rep):
        rep[...] = jnp.zeros_like(rep)
        rep[ROW_GPRE:ROW_GPRE + 1, :] = dgpre_ref[0:1, :]
        rep[ROW_CONVB:ROW_CONVB + 1, :] = dbx_ref[0:1, 0:1024]
        rep[ROW_CONVB + 1:ROW_CONVB + 2, :] = dbx_ref[0:1, 1024:2048]
        rep[ROW_CONVB + 2:ROW_CONVB + 3, 0:512] = dbb_ref[0:1, :]
        rep[ROW_CONVB + 2:ROW_CONVB + 3, 512:1024] = dbc_ref[0:1, :]
        rep[ROW_DTB:ROW_DTB + 1, 0:128] = ddtb_ref[0:1, :]
        rep[ROW_ALOG:ROW_ALOG + 1, 0:128] = dal_ref[0:1, :]
        rep[ROW_DSKIP:ROW_DSKIP + 1, 0:128] = ddsk_ref[0:1, :]
        rep[ROW_SINK:ROW_SINK + 1, 0:128] = dsink_ref[0:1, :]
        rep[ROW_GSSM:ROW_GSSM + 1, :] = dgn_ref[0:1, 0:1024]
        rep[ROW_GSSM + 1:ROW_GSSM + 2, :] = dgn_ref[0:1, 1024:2048]
        rep[ROW_GPOST:ROW_GPOST + 1, :] = dgp_ref[0:1, :]
        cw = jnp.concatenate([dwx_ref[...], dwb_ref[...], dwc_ref[...]], axis=1)
        mh = dh_ref[...]
        o_ref[...] = jnp.zeros_like(o_ref)
        for p in range(N_DEV):
            o_ref[p, 0:REP_ROWS, :] = rep[...]
            o_ref[p, ROW_CONVW:ROW_CONVW + 8, 0:CW_SHARD] = cw[:, p * CW_SHARD:(p + 1) * CW_SHARD]
            o_ref[p, ROW_META:ROW_META + N_META, 0:META_SHARD] = mh[:, p * META_SHARD:(p + 1) * META_SHARD]

    ins = [dgpre, dbx, dbb, dbc, ddtb, dal, ddsk, dsink, dgn, dgp, dwx, dwb, dwc]
    return pl.pallas_call(
        body, grid=(1,),
        in_specs=[pl.BlockSpec(a.shape, lambda i: (0, 0)) for a in ins]
        + [pl.BlockSpec((N_META, D_MODEL), lambda i: (PAD // N_META, 0))],
        out_specs=pl.BlockSpec((N_DEV, SM_ROWS, 1024), lambda i: (0, 0, 0)),
        out_shape=SDS((N_DEV, SM_ROWS, 1024), F32), scratch_shapes=[pltpu.VMEM((REP_ROWS, 1024), F32)],
        name="small_pack")(*ins, dh)


def _small_finish(recv, params):
    npar = len(params)

    def body(*refs):
        r_ref = refs[0]
        wmv = refs[1:1 + 3 * npar]
        outs = refs[1 + 3 * npar:1 + 7 * npar]
        gs = refs[-1]
        g = r_ref[0]
        for d in range(1, N_CHIP):
            g = g + r_ref[d]
        gs[...] = g
        grads = [
            gs[ROW_GPRE:ROW_GPRE + 1, :],
            jnp.concatenate([gs[ROW_CONVB + k:ROW_CONVB + k + 1, :] for k in range(3)], axis=1),
            gs[ROW_DTB:ROW_DTB + 1, 0:SSM_HEADS], gs[ROW_ALOG:ROW_ALOG + 1, 0:SSM_HEADS],
            gs[ROW_DSKIP:ROW_DSKIP + 1, 0:SSM_HEADS], gs[ROW_SINK:ROW_SINK + 1, 0:Q_HEADS],
            jnp.concatenate([gs[ROW_GSSM:ROW_GSSM + 1, :], gs[ROW_GSSM + 1:ROW_GSSM + 2, :]], axis=1),
            gs[ROW_GPOST:ROW_GPOST + 1, :],
            gs[ROW_CONVW:ROW_CONVW + 4, 0:CW_SHARD],
            gs[ROW_META:ROW_META + N_META, 0:META_SHARD]]
        for i in range(npar):
            w_ref, m_ref, v_ref = wmv[3 * i:3 * i + 3]
            delta, nm, nv = _adamw_math(w_ref[...], grads[i], m_ref[...], v_ref[...])
            outs[4 * i][...] = grads[i]
            outs[4 * i + 1][...] = delta
            outs[4 * i + 2][...] = nm
            outs[4 * i + 3][...] = nv

    flat = [a for wmv in params for a in wmv]
    res = pl.pallas_call(
        body, out_shape=[SDS(wmv[0].shape, F32) for wmv in params for _ in range(4)],
        scratch_shapes=[pltpu.VMEM((SM_ROWS, 1024), F32)], name="small_finish")(recv, *flat)
    return [tuple(res[4 * i:4 * i + 4]) for i in range(npar)]


def _slab(ref, px, py, pc):
    return ref.at[4 * px + 2 * py + pc]


def _all_gather(shards):
    na = len(shards)

    def body(*refs):
        ins, outs = refs[:na], refs[na:2 * na]
        send_sems, recv_sems, local_sems = refs[2 * na:]
        x, y, c = lax.axis_index("x"), lax.axis_index("y"), lax.axis_index("c")
        me, sibling = (x, y, c), (x, y, 1 - c)
        chips = [(1 - x, y), (x, 1 - y), (1 - x, 1 - y)]

        def copy(a, k, block, to, src=None):
            dst = _slab(outs[a], *block)
            return pltpu.make_async_remote_copy(
                src_ref=dst if src is None else src, dst_ref=dst, send_sem=send_sems.at[a, k],
                recv_sem=recv_sems.at[a, k], device_id=to, device_id_type=MESH)

        mine = [pltpu.make_async_copy(ins[a], _slab(outs[a], *me), local_sems.at[a]) for a in range(na)]
        for cp in mine:
            cp.start()
        first = []
        for a in range(na):
            first.append(copy(a, 0, me, sibling, src=ins[a]))
            first += [copy(a, 1 + j, me, (*chip, c), src=ins[a]) for j, chip in enumerate(chips)]
        for cp in first:
            cp.start()
        passed = []
        for j, chip in enumerate(chips):
            for a in range(na):
                copy(a, 1 + j, (*chip, c), me).wait_recv()
                cp = copy(a, 4 + j, (*chip, c), sibling)
                cp.start()
                passed.append(cp)
        for a in range(na):
            copy(a, 0, sibling, me).wait_recv()
            for j, chip in enumerate(chips):
                copy(a, 4 + j, (*chip, 1 - c), me).wait_recv()
        for cp in first + passed:
            cp.wait_send()
        for cp in mine:
            cp.wait()

    return pl.pallas_call(
        body, in_specs=[ANY] * na, out_specs=[ANY] * na,
        out_shape=[SDS((N_DEV,) + s.shape, s.dtype) for s in shards],
        scratch_shapes=[pltpu.SemaphoreType.DMA((na, 7)), pltpu.SemaphoreType.DMA((na, 7)),
                        pltpu.SemaphoreType.DMA((na,))],
        name="all_gather")(*shards)


N_CHIP = 4


def _exchange_pair(parts):
    na = len(parts)

    def body(*refs):
        ins, own, got = refs[:na], refs[na:2 * na], refs[2 * na:3 * na]
        send_sems, recv_sems, local_sems = refs[3 * na:]
        x, y, c = lax.axis_index("x"), lax.axis_index("y"), lax.axis_index("c")
        sibling = (x, y, 1 - c)
        local, sent = [], []
        for a in range(na):
            for k in range(N_CHIP):
                cp = pltpu.make_async_copy(ins[a].at[2 * k + c], own[a].at[k], local_sems.at[a, k])
                cp.start()
                local.append(cp)
                cp = pltpu.make_async_remote_copy(
                    src_ref=ins[a].at[2 * k + 1 - c], dst_ref=got[a].at[k], send_sem=send_sems.at[a, k],
                    recv_sem=recv_sems.at[a, k], device_id=sibling, device_id_type=MESH)
                cp.start()
                sent.append(cp)
        for cp in sent:
            cp.wait()
        for cp in local:
            cp.wait()

    half = [SDS((N_CHIP,) + p.shape[1:], p.dtype) for p in parts]
    res = pl.pallas_call(
        body, in_specs=[ANY] * na, out_specs=[ANY] * (2 * na), out_shape=half + half,
        scratch_shapes=[pltpu.SemaphoreType.DMA((na, N_CHIP)), pltpu.SemaphoreType.DMA((na, N_CHIP)),
                        pltpu.SemaphoreType.DMA((na, N_CHIP))],
        name="exchange_pair")(*parts)
    return res[:na], res[na:]


def _pair_sum(own, got):
    na = len(own)

    def body(*refs):
        for a in range(na):
            o_ref, g_ref, s_ref = refs[a], refs[na + a], refs[2 * na + a]
            s_ref[...] = (o_ref[...].astype(F32) + g_ref[...].astype(F32)).astype(s_ref.dtype)

    def spec(p):
        nd = len(p.shape) - 1
        return pl.BlockSpec((1,) + p.shape[1:], lambda k, nd=nd: (k,) + (0,) * nd)

    return pl.pallas_call(
        body, grid=(N_CHIP,), in_specs=[spec(p) for p in own] + [spec(p) for p in got],
        out_specs=[spec(p) for p in own], out_shape=[SDS(p.shape, p.dtype) for p in own],
        compiler_params=_cparams(), name="pair_sum")(*own, *got)


def _exchange_chips(parts):
    na = len(parts)

    def body(*refs):
        ins, outs = refs[:na], refs[na:2 * na]
        send_sems, recv_sems, local_sems = refs[2 * na:]
        x, y, c = lax.axis_index("x"), lax.axis_index("y"), lax.axis_index("c")
        mine = 2 * x + y
        local = [pltpu.make_async_copy(ins[a].at[mine], outs[a].at[mine], local_sems.at[a]) for a in range(na)]
        for cp in local:
            cp.start()
        chips = [(1 - x, y), (x, 1 - y), (1 - x, 1 - y)]
        sent = []
        for a in range(na):
            for j, (px, py) in enumerate(chips):
                cp = pltpu.make_async_remote_copy(
                    src_ref=ins[a].at[2 * px + py], dst_ref=outs[a].at[mine], send_sem=send_sems.at[a, j],
                    recv_sem=recv_sems.at[a, j], device_id=(px, py, c), device_id_type=MESH)
                cp.start()
                sent.append(cp)
        for a in range(na):
            for j, (px, py) in enumerate(chips):
                pltpu.make_async_remote_copy(
                    src_ref=ins[a].at[2 * px + py], dst_ref=outs[a].at[2 * px + py], send_sem=send_sems.at[a, j],
                    recv_sem=recv_sems.at[a, j], device_id=(px, py, c), device_id_type=MESH).wait_recv()
        for cp in sent:
            cp.wait_send()
        for cp in local:
            cp.wait()

    return pl.pallas_call(
        body, in_specs=[ANY] * na, out_specs=[ANY] * na, out_shape=[SDS(p.shape, p.dtype) for p in parts],
        scratch_shapes=[pltpu.SemaphoreType.DMA((na, 3)), pltpu.SemaphoreType.DMA((na, 3)),
                        pltpu.SemaphoreType.DMA((na,))],
        name="exchange_chips")(*parts)


ROW_TILES = D_MODEL // 128


def _rows3(t):
    return jnp.transpose(t[0]).reshape(t.shape[2], ROW_TILES, 128)


def _unrows3(t):
    return jnp.transpose(t.reshape(t.shape[0], D_MODEL))[None]


def _cast_shards(w_in3, w_att, w_ssm, w_o):
    def body(wi_ref, wa_ref, ws_ref, wo_ref, a_ref, b_ref, c_ref, d_ref):
        a_ref[...] = wi_ref[...].reshape(SHARD_IN // 2, 2 * ROW_TILES, 128).astype(BF16)
        b_ref[...] = wa_ref[...].astype(BF16)
        c_ref[...] = ws_ref[...].astype(BF16)
        d_ref[...] = wo_ref[...].astype(BF16)

    return pl.pallas_call(
        body, out_shape=[SDS((SHARD_IN // 2, 2 * ROW_TILES, 128), BF16), SDS(w_att.shape, BF16),
                         SDS(w_ssm.shape, BF16), SDS(w_o.shape, BF16)],
        compiler_params=_cparams(), name="cast_shards")(w_in3, w_att, w_ssm, w_o)


def _pieces():
    out = []
    for r0, c0, w in _SEGS:
        r = r0
        while r < r0 + w:
            d = r // SHARD_IN
            n = min(r0 + w, (d + 1) * SHARD_IN) - r
            out.append((c0 + (r - r0), d, r - d * SHARD_IN, n))
            r += n
    return out


def _to_aligned_t(slabs):
    def body(a_ref, o_ref):
        for (t, d, s, n) in _pieces():
            o_ref[t:t + n, :] = a_ref[d, s // 2:(s + n) // 2].reshape(n, D_MODEL)
        o_ref[C_DT + 32:C_DT + 128, :] = jnp.zeros((96, D_MODEL), slabs.dtype)

    return pl.pallas_call(body, out_shape=SDS((PW, D_MODEL), slabs.dtype), compiler_params=_cparams(),
                          name="to_aligned")(slabs)


def _from_aligned_t(g):
    def body(g_ref, o_ref):
        for (t, d, s, n) in _pieces():
            o_ref[d, s // 2:(s + n) // 2] = g_ref[t:t + n, :].reshape(n // 2, 2 * ROW_TILES, 128)

    return pl.pallas_call(body, out_shape=SDS((N_DEV, SHARD_IN // 2, 2 * ROW_TILES, 128), g.dtype),
                          compiler_params=_cparams(), name="from_aligned")(g)


_SEGS = [
    (R_Q, C_Q, 1024), (R_K, C_K, 256), (R_V, C_V, 256), (R_ZA, C_ZA, 1024), (R_ZS, C_ZS, 2048),
    (R_XBC, C_XBC, 3072), (R_DT, C_DT, 32), (R_GA, C_GA, 1024), (R_GS, C_GS, 1024)]


def _pad_lanes(v, n=128):
    return jnp.pad(v, ((0, 0), (0, n - v.shape[1])))


def _local_step(h, tgt, w_alt, w_att, w_ssm, w_o, g_pre, conv_w8, conv_b, dt_bias, a_log, d_skip, sinks,
                g_ssm, g_post):
    dtb, al, dsk, snk = _pad_lanes(dt_bias), _pad_lanes(a_log), _pad_lanes(d_skip), _pad_lanes(sinks)
    u = _norm_u(h, g_pre)
    proj = _matmul(u, w_alt, "nt", F32, 1088, 896, D_MODEL, "in_proj")
    o = _attn_fwd(proj, snk)
    xbc_act = _conv_fwd(proj, conv_w8, conv_b)
    sn, states = _ssd_fwd(xbc_act, proj, dtb, al, dsk, g_ssm)
    a_in, mg, ya, ys, out = _post_a(o, proj, sn, w_att, w_ssm, w_o)
    (loss, dres, dout, dya, dys, dga, dgs, do, dza, dsn, dgp) = _post_b(
        out, h, tgt, proj, ya, ys, o, g_post, w_att, w_ssm, w_o)
    dxs, dbm, dcm, ddt4, dzs, ddtb, dal, ddsk, dgn = _ssd_bwd(xbc_act, proj, dtb, al, dsk, g_ssm, states, dsn)
    dxx, dwx, dbx = _conv_bwd(proj, conv_w8, conv_b, dxs, 0, "conv_bwd_x")
    dxb, dwb, dbb = _conv_bwd(proj, conv_w8, conv_b, dbm, SSM_INNER, "conv_bwd_b")
    dxc, dwc, dbc = _conv_bwd(proj, conv_w8, conv_b, dcm, SSM_INNER + GRP_W, "conv_bwd_c")
    dq, dk, dv, dsink = _attn_bwd(proj, snk, do)
    dproj = _assemble(dq, dza, dga, dgs, dzs, dxx, dxb, dxc, dk, dv, ddt4)
    du = _matmul(dproj, w_alt, "nn", F32, 1088, D_MODEL, 896, "d_u")
    dw_alt = _matmul(dproj, u, "tn", BF16, 896, D_MODEL, T, "d_w_in")
    dh, dgpre = _norm_bwd(h, g_pre, du, dres)
    dw_att = _matmul(a_in, dya, "tn", BF16, D_MODEL, D_MODEL, T, "d_w_att")
    dw_ssm = _matmul(sn, dys, "tn", BF16, D_MODEL, D_MODEL, T, "d_w_ssm")
    dw_o = _matmul(mg, dout, "tn", BF16, D_MODEL, D_MODEL, T, "d_w_o")
    return dict(
        loss=loss[0, 0], dh=dh, dw_alt=dw_alt, dw_att=dw_att, dw_ssm=dw_ssm, dw_o=dw_o,
        small=(dgpre, dbx, dbb, dbc, ddtb, dal, ddsk, dsink, dgn, dgp, dwx, dwb, dwc))


def kernel(x, meta_tokens, g_pre, w_in, conv_w, conv_b, dt_bias, a_log, d_skip, attn_sinks, g_ssm_norm, w_out_att, w_out_ssm, w_out, g_post, loss_target, m_meta_tokens, m_g_pre, m_w_in, m_conv_w, m_conv_b, m_dt_bias, m_a_log, m_d_skip, m_attn_sinks, m_g_ssm_norm, m_w_out_att, m_w_out_ssm, m_w_out, m_g_post, v_meta_tokens, v_g_pre, v_w_in, v_conv_w, v_conv_b, v_dt_bias, v_a_log, v_d_skip, v_attn_sinks, v_g_ssm_norm, v_w_out_att, v_w_out_ssm, v_w_out, v_g_post):
    w_in3, m_in3, v_in3 = _rows3(w_in), _rows3(m_w_in), _rows3(v_w_in)
    a_sh, att_sh, ssm_sh, o_sh = _cast_shards(w_in3, w_out_att[0], w_out_ssm[0], w_out[0])
    cw_sh = jnp.pad(conv_w[0], ((0, 4), (0, 0)))
    a_all, att_all, ssm_all, o_all, meta_all, cw_all = _all_gather([a_sh, att_sh, ssm_sh, o_sh, meta_tokens, cw_sh])
    w_alt = _to_aligned_t(a_all)
    w_att = att_all.reshape(D_MODEL, D_MODEL)
    w_ssm = ssm_all.reshape(SSM_INNER, D_MODEL)
    w_o = o_all.reshape(D_MODEL, D_MODEL)
    meta_full = meta_all.transpose(1, 0, 2).reshape(N_META, D_MODEL)
    conv_w8 = cw_all.transpose(1, 0, 2).reshape(8, CONV_DIM)

    h = jnp.concatenate([jnp.zeros((PAD, D_MODEL), F32), meta_full, x[0]], axis=0)
    tgt = jnp.concatenate([jnp.zeros((PAD + N_META, D_MODEL), F32), loss_target[0]], axis=0)
    r = _local_step(h, tgt, w_alt, w_att, w_ssm, w_o, g_pre, conv_w8, conv_b, dt_bias, a_log, d_skip, attn_sinks,
                    g_ssm_norm, g_post)
    loss = lax.psum(r["loss"], ("x", "y", "c"))
    grad_x = r["dh"][PAD + N_META:][None]

    small8 = _small_pack(*r["small"], r["dh"])
    own, got = _exchange_pair([
        _from_aligned_t(r["dw_alt"]), r["dw_att"].reshape(N_DEV, 128, D_MODEL),
        r["dw_ssm"].reshape(N_DEV, 256, D_MODEL), r["dw_o"].reshape(N_DEV, 128, D_MODEL), small8])
    ra, r_att, r_ssm, r_o, rs = _exchange_chips(_pair_sum(own, got))

    res_in = [_unrows3(t) for t in _sum_adamw_rows3(ra, w_in3, m_in3, v_in3, "adamw_w_in")]
    res_att = [t[None] for t in _sum_adamw(r_att, w_out_att[0], m_w_out_att[0], v_w_out_att[0], 512, "adamw_w_att")]
    res_ssm = [t[None] for t in _sum_adamw(r_ssm, w_out_ssm[0], m_w_out_ssm[0], v_w_out_ssm[0], 512, "adamw_w_ssm")]
    res_o = [t[None] for t in _sum_adamw(r_o, w_out[0], m_w_out[0], v_w_out[0], 512, "adamw_w_o")]
    (res_gpre, res_convb, res_dtb, res_alog, res_dskip, res_sink, res_gssm, res_gpost, res_cw, res_meta) = _small_finish(
        rs, [(g_pre, m_g_pre, v_g_pre), (conv_b, m_conv_b, v_conv_b), (dt_bias, m_dt_bias, v_dt_bias),
             (a_log, m_a_log, v_a_log), (d_skip, m_d_skip, v_d_skip), (attn_sinks, m_attn_sinks, v_attn_sinks),
             (g_ssm_norm, m_g_ssm_norm, v_g_ssm_norm), (g_post, m_g_post, v_g_post),
             (conv_w[0], m_conv_w[0], v_conv_w[0]), (meta_tokens, m_meta_tokens, v_meta_tokens)])
    res_cw = [t[None] for t in res_cw]
    per_weight = [res_meta, res_gpre, res_in, res_cw, res_convb, res_dtb, res_alog, res_dskip, res_sink, res_gssm,
                  res_att, res_ssm, res_o, res_gpost]
    return (loss, grad_x, *[p[0] for p in per_weight], *[p[1] for p in per_weight], *[p[2] for p in per_weight],
            *[p[3] for p in per_weight])
```

```python
import functools
import math

import jax
import jax.numpy as jnp
from jax import lax
from jax.experimental import pallas as pl
from jax.experimental.pallas import tpu as pltpu

F32 = jnp.float32
BF16 = jnp.bfloat16
SDS = jax.ShapeDtypeStruct
HI = lax.Precision.HIGHEST
MESH = pl.DeviceIdType.MESH
ANY = pl.BlockSpec(memory_space=pl.ANY)

N_DEV = 8
D_MODEL = 1024
SEQ = 2048
N_META = 16
BLK = 128
PAD = 112
T = PAD + N_META + SEQ
NB = T // BLK
EPS = 1e-6
HEAD = 64
Q_HEADS = 16
KV_HEADS = 4
GROUP = 4
KV_W = 256
SSM_INNER = 2048
SSM_HEADS = 32
SSM_GROUPS = 4
GRP_W = 512
SSM_STATE = 128
CONV_DIM = 3072
IN_PROJ = 9760
SHARD_IN = IN_PROJ // N_DEV
NEG = -1e30

C_Q, C_ZA, C_GA, C_GS, C_ZS, C_XBC, C_K, C_V, C_DT = 0, 1024, 2048, 3072, 4096, 6144, 9216, 9472, 9728
PW = 9856
R_Q, R_K, R_V, R_ZA, R_ZS, R_XBC, R_DT, R_GA, R_GS = 0, 1024, 1280, 1536, 2560, 4608, 7680, 7712, 8736

ADAM_LR, ADAM_B1, ADAM_B2, ADAM_EPS, ADAM_WD, ADAM_STEP = 0.001, 0.9, 0.999, 1e-08, 0.01, 10

VMEM_LIMIT = 56 * 1024 * 1024


def _cparams():
    return pltpu.CompilerParams(vmem_limit_bytes=VMEM_LIMIT)


def _silu(x):
    return x * jax.nn.sigmoid(x)


def _dsilu(x):
    s = jax.nn.sigmoid(x)
    return s * (1.0 + x * (1.0 - s))


def _matmul(a, b, mode, out_dtype, tm, tn, tk, name):
    if mode == "nn":
        (m, k), n = a.shape, b.shape[1]
        a_spec = pl.BlockSpec((tm, tk), lambda i, j, kk: (i, kk))
        b_spec = pl.BlockSpec((tk, tn), lambda i, j, kk: (kk, j))
        dims = (((1,), (0,)), ((), ()))
    elif mode == "nt":
        (m, k), n = a.shape, b.shape[0]
        a_spec = pl.BlockSpec((tm, tk), lambda i, j, kk: (i, kk))
        b_spec = pl.BlockSpec((tn, tk), lambda i, j, kk: (j, kk))
        dims = (((1,), (1,)), ((), ()))
    else:
        (k, m), n = a.shape, b.shape[1]
        a_spec = pl.BlockSpec((tk, tm), lambda i, j, kk: (kk, i))
        b_spec = pl.BlockSpec((tk, tn), lambda i, j, kk: (kk, j))
        dims = (((0,), (0,)), ((), ()))
    assert m % tm == 0 and n % tn == 0 and k % tk == 0, (a.shape, b.shape, tm, tn, tk)
    nk = k // tk

    def body(a_ref, b_ref, o_ref, acc_ref):
        kk = pl.program_id(2)
        part = lax.dot_general(a_ref[...], b_ref[...], dims, preferred_element_type=F32)

        @pl.when(kk == 0)
        def _():
            acc_ref[...] = part

        @pl.when(kk > 0)
        def _():
            acc_ref[...] += part

        @pl.when(kk == nk - 1)
        def _():
            o_ref[...] = acc_ref[...].astype(out_dtype)

    return pl.pallas_call(
        body, grid=(m // tm, n // tn, nk), in_specs=[a_spec, b_spec],
        out_specs=pl.BlockSpec((tm, tn), lambda i, j, kk: (i, j)),
        out_shape=SDS((m, n), out_dtype), scratch_shapes=[pltpu.VMEM((tm, tn), F32)],
        compiler_params=_cparams(), name=name)(a, b)


def _norm_u(h, g_pre):
    def body(h_ref, g_ref, u_ref):
        x = h_ref[...]
        r = lax.rsqrt(jnp.mean(x * x, axis=-1, keepdims=True) + EPS)
        u_ref[...] = (x * r * g_ref[...]).astype(BF16)

    return pl.pallas_call(
        body, grid=(NB,),
        in_specs=[pl.BlockSpec((BLK, D_MODEL), lambda i: (i, 0)), pl.BlockSpec((1, D_MODEL), lambda i: (0, 0))],
        out_specs=pl.BlockSpec((BLK, D_MODEL), lambda i: (i, 0)),
        out_shape=SDS((T, D_MODEL), BF16), name="norm_u")(h, g_pre)


def _norm_bwd(h, g_pre, du, dres):
    def body(h_ref, g_ref, du_ref, dres_ref, dh_ref, dg_ref):
        i = pl.program_id(0)
        x = h_ref[...]
        g = g_ref[...]
        du_ = du_ref[...]
        r = lax.rsqrt(jnp.mean(x * x, axis=-1, keepdims=True) + EPS)
        gd = g * du_
        dx = r * gd - x * (r * r * r) * jnp.mean(x * gd, axis=-1, keepdims=True)
        dh_ref[...] = dx + dres_ref[...]
        part = jnp.sum(du_ * x * r, axis=0, keepdims=True)

        @pl.when(i == 0)
        def _():
            dg_ref[...] = jnp.zeros_like(dg_ref)

        dg_ref[0:1, :] += part

    row = pl.BlockSpec((BLK, D_MODEL), lambda i: (i, 0))
    return pl.pallas_call(
        body, grid=(NB,),
        in_specs=[row, pl.BlockSpec((1, D_MODEL), lambda i: (0, 0)), row, row],
        out_specs=[row, pl.BlockSpec((8, D_MODEL), lambda i: (0, 0))],
        out_shape=[SDS((T, D_MODEL), F32), SDS((8, D_MODEL), F32)], name="norm_bwd")(h, g_pre, du, dres)


def _lane_pick(row, h):
    lane = lax.broadcasted_iota(jnp.int32, row.shape, 1)
    return jnp.sum(jnp.where(lane == h, row, 0.0), axis=1, keepdims=True)


def _attn_fn(q4s, kcats, vcats, kms, vms, sinks, n):
    r = lax.broadcasted_iota(jnp.int32, (GROUP * BLK, 2 * BLK), 0)
    s = lax.broadcasted_iota(jnp.int32, (GROUP * BLK, 2 * BLK), 1)
    i = jnp.bitwise_and(r, BLK - 1)
    gi = jnp.right_shift(r, 7)
    rel = i - s + BLK
    k_pos = n * BLK - BLK + s
    band_ok = (rel >= 0) & (rel < BLK) & (k_pos >= PAD + N_META)
    relf = rel.astype(F32)
    rm = lax.broadcasted_iota(jnp.int32, (GROUP * BLK, N_META), 0)
    mm = lax.broadcasted_iota(jnp.int32, (GROUP * BLK, N_META), 1)
    meta_ok = (PAD + mm) <= (n * BLK + jnp.bitwise_and(rm, BLK - 1))
    gcol = jnp.right_shift(lax.broadcasted_iota(jnp.int32, (GROUP * BLK, 1), 0), 7)
    outs = []
    for kh in range(KV_HEADS):
        slopes = [2.0 ** (-8.0 * (kh * GROUP + g + 1) / Q_HEADS) for g in range(GROUP)]
        slope = jnp.where(gi == 0, slopes[0], jnp.where(gi == 1, slopes[1], jnp.where(gi == 2, slopes[2], slopes[3])))
        sk = [_lane_pick(sinks, kh * GROUP + g) for g in range(GROUP)]
        sink = jnp.where(gcol == 0, sk[0], jnp.where(gcol == 1, sk[1], jnp.where(gcol == 2, sk[2], sk[3])))
        qb = (q4s[kh] * (HEAD ** -0.5)).astype(BF16)
        sb = lax.dot_general(qb, kcats[kh].astype(BF16), (((1,), (1,)), ((), ())), preferred_element_type=F32)
        sb = jnp.where(band_ok, sb - slope * relf, NEG)
        sm = lax.dot_general(qb, kms[kh].astype(BF16), (((1,), (1,)), ((), ())), preferred_element_type=F32)
        sm = jnp.where(meta_ok, sm, NEG)
        mx = jnp.maximum(jnp.maximum(jnp.max(sb, axis=1, keepdims=True), jnp.max(sm, axis=1, keepdims=True)), sink)
        mx = lax.stop_gradient(mx)
        eb = jnp.exp(sb - mx)
        em = jnp.exp(sm - mx)
        es = jnp.exp(sink - mx)
        inv = 1.0 / (jnp.sum(eb, axis=1, keepdims=True) + jnp.sum(em, axis=1, keepdims=True) + es)
        pb = (eb * inv).astype(BF16)
        pm = (em * inv).astype(BF16)
        o4 = (jnp.dot(pm, vms[kh].astype(BF16), preferred_element_type=F32)
              + jnp.dot(pb, vcats[kh].astype(BF16), preferred_element_type=F32))
        outs.append(o4)
    return outs


def _attn_specs():
    prev = lambda n: jnp.maximum(n - 1, 0)
    return [
        pl.BlockSpec((BLK, D_MODEL), lambda n: (n, C_Q // D_MODEL)),
        pl.BlockSpec((BLK, KV_W), lambda n: (prev(n), C_K // KV_W)),
        pl.BlockSpec((BLK, KV_W), lambda n: (n, C_K // KV_W)),
        pl.BlockSpec((BLK, KV_W), lambda n: (prev(n), C_V // KV_W)),
        pl.BlockSpec((BLK, KV_W), lambda n: (n, C_V // KV_W)),
        pl.BlockSpec((N_META, KV_W), lambda n: (PAD // N_META, C_K // KV_W)),
        pl.BlockSpec((N_META, KV_W), lambda n: (PAD // N_META, C_V // KV_W)),
        pl.BlockSpec((1, 128), lambda n: (0, 0)),
    ]


def _attn_load(q_ref, kp_ref, kc_ref, vp_ref, vc_ref, km_ref, vm_ref):
    q4s, kcats, vcats, kms, vms = [], [], [], [], []
    for kh in range(KV_HEADS):
        q4s.append(jnp.concatenate(
            [q_ref[:, (kh * GROUP + g) * HEAD:(kh * GROUP + g + 1) * HEAD] for g in range(GROUP)], axis=0))
        cs = slice(kh * HEAD, (kh + 1) * HEAD)
        kcats.append(jnp.concatenate([kp_ref[:, cs], kc_ref[:, cs]], axis=0))
        vcats.append(jnp.concatenate([vp_ref[:, cs], vc_ref[:, cs]], axis=0))
        kms.append(km_ref[:, cs])
        vms.append(vm_ref[:, cs])
    return q4s, kcats, vcats, kms, vms


def _attn_fwd(proj, sinks):
    def body(q_ref, kp_ref, kc_ref, vp_ref, vc_ref, km_ref, vm_ref, s_ref, o_ref):
        n = pl.program_id(0)
        args = _attn_load(q_ref, kp_ref, kc_ref, vp_ref, vc_ref, km_ref, vm_ref)
        outs = _attn_fn(*args, s_ref[...], n)
        for kh in range(KV_HEADS):
            for g in range(GROUP):
                hh = kh * GROUP + g
                o_ref[:, hh * HEAD:(hh + 1) * HEAD] = outs[kh][g * BLK:(g + 1) * BLK]

    return pl.pallas_call(
        body, grid=(NB,), in_specs=_attn_specs(),
        out_specs=pl.BlockSpec((BLK, D_MODEL), lambda n: (n, 0)),
        out_shape=SDS((T, D_MODEL), F32), name="attn_fwd")(proj, proj, proj, proj, proj, proj, proj, sinks)


def _attn_bwd(proj, sinks, do):
    def body(q_ref, kp_ref, kc_ref, vp_ref, vc_ref, km_ref, vm_ref, s_ref, do_ref, dq_ref, dk_ref, dv_ref, ds_ref):
        n = pl.program_id(0)

        @pl.when(n == 0)
        def _():
            dk_ref[...] = jnp.zeros_like(dk_ref)
            dv_ref[...] = jnp.zeros_like(dv_ref)
            ds_ref[...] = jnp.zeros_like(ds_ref)

        args = _attn_load(q_ref, kp_ref, kc_ref, vp_ref, vc_ref, km_ref, vm_ref)
        _, vjp = jax.vjp(lambda a, b, c, d, e, f: _attn_fn(a, b, c, d, e, f, n), *args, s_ref[...])
        cot = [jnp.concatenate([do_ref[:, (kh * GROUP + g) * HEAD:(kh * GROUP + g + 1) * HEAD] for g in range(GROUP)],
                               axis=0) for kh in range(KV_HEADS)]
        dq4s, dkcats, dvcats, dkms, dvms, dsk = vjp(cot)
        ds_ref[0:1, :] += dsk
        cur = pl.ds(pl.multiple_of(n * BLK, BLK), BLK)
        meta = slice(PAD, PAD + N_META)
        for kh in range(KV_HEADS):
            cs = slice(kh * HEAD, (kh + 1) * HEAD)
            for g in range(GROUP):
                hh = kh * GROUP + g
                dq_ref[:, hh * HEAD:(hh + 1) * HEAD] = dq4s[kh][g * BLK:(g + 1) * BLK]
            dk_ref[cur, cs] += dkcats[kh][BLK:]
            dv_ref[cur, cs] += dvcats[kh][BLK:]
            dk_ref[meta, cs] += dkms[kh]
            dv_ref[meta, cs] += dvms[kh]

        @pl.when(n > 0)
        def _():
            prv = pl.ds(pl.multiple_of((n - 1) * BLK, BLK), BLK)
            for kh in range(KV_HEADS):
                cs = slice(kh * HEAD, (kh + 1) * HEAD)
                dk_ref[prv, cs] += dkcats[kh][:BLK]
                dv_ref[prv, cs] += dvcats[kh][:BLK]

    full_kv = pl.BlockSpec((T, KV_W), lambda n: (0, 0))
    return pl.pallas_call(
        body, grid=(NB,),
        in_specs=_attn_specs() + [pl.BlockSpec((BLK, D_MODEL), lambda n: (n, 0))],
        out_specs=[pl.BlockSpec((BLK, D_MODEL), lambda n: (n, 0)), full_kv, full_kv,
                   pl.BlockSpec((8, 128), lambda n: (0, 0))],
        out_shape=[SDS((T, D_MODEL), F32), SDS((T, KV_W), F32), SDS((T, KV_W), F32), SDS((8, 128), F32)],
        name="attn_bwd")(proj, proj, proj, proj, proj, proj, proj, sinks, do)


def _conv_taps(xp, w, rows):
    return (w[0:1] * xp[5:5 + rows] + w[1:2] * xp[6:6 + rows] + w[2:3] * xp[7:7 + rows] + w[3:4] * xp[8:8 + rows])


def _conv_fwd(proj, conv_w, conv_b):
    CONV_CB = CONV_DIM
    ncb = CONV_DIM // CONV_CB
    cb0 = C_XBC // CONV_CB

    def body(tail_ref, cur_ref, w_ref, b_ref, o_ref):
        n = pl.program_id(1)
        tail = jnp.where(n > 0, tail_ref[...], 0.0)
        xp = jnp.concatenate([tail, cur_ref[...]], axis=0)
        conv = _conv_taps(xp, w_ref[...], BLK) + b_ref[...]
        row = n * BLK + lax.broadcasted_iota(jnp.int32, (BLK, 1), 0)
        o_ref[...] = jnp.where(row >= PAD, _silu(conv), 0.0)

    return pl.pallas_call(
        body, grid=(ncb, NB),
        in_specs=[pl.BlockSpec((8, CONV_CB), lambda j, n: (jnp.maximum(n * (BLK // 8) - 1, 0), cb0 + j)),
                  pl.BlockSpec((BLK, CONV_CB), lambda j, n: (n, cb0 + j)),
                  pl.BlockSpec((8, CONV_CB), lambda j, n: (0, j)),
                  pl.BlockSpec((1, CONV_CB), lambda j, n: (0, j))],
        out_specs=pl.BlockSpec((BLK, CONV_CB), lambda j, n: (n, j)),
        out_shape=SDS((T, CONV_DIM), F32), name="conv_fwd")(proj, proj, conv_w, conv_b)


def _conv_bwd(proj, conv_w, conv_b, dact, ch0, name):
    width = dact.shape[1]
    CONV_CB = width
    ncb = width // CONV_CB
    cb0 = (C_XBC + ch0) // CONV_CB
    wb0 = ch0 // CONV_CB
    last8 = T // 8 - 1

    def body(tail_ref, cur_ref, nxt_ref, w_ref, b_ref, dcur_ref, dnxt_ref, dx_ref, dw_ref, db_ref):
        n = pl.program_id(1)
        w = w_ref[...]
        tail = jnp.where(n > 0, tail_ref[...], 0.0)
        xp = jnp.concatenate([tail, cur_ref[...], nxt_ref[...]], axis=0)
        conv = _conv_taps(xp, w, BLK + 8) + b_ref[...]
        dext = jnp.concatenate([dcur_ref[...], jnp.where(n < NB - 1, dnxt_ref[...], 0.0)], axis=0)
        row = n * BLK + lax.broadcasted_iota(jnp.int32, (BLK + 8, 1), 0)
        dconv = jnp.where(row >= PAD, dext * _dsilu(conv), 0.0)
        dx = (w[0:1] * dconv[3:3 + BLK] + w[1:2] * dconv[2:2 + BLK] + w[2:3] * dconv[1:1 + BLK]
              + w[3:4] * dconv[0:BLK])
        dx_ref[...] = dx.astype(BF16)
        dc = dconv[0:BLK]
        dws = [jnp.sum(dc * xp[5 + k:5 + k + BLK], axis=0, keepdims=True) for k in range(4)]
        dwp = jnp.concatenate(dws + [jnp.zeros((4, CONV_CB), F32)], axis=0)
        dbp = jnp.sum(dc, axis=0, keepdims=True)

        @pl.when(n == 0)
        def _():
            dw_ref[...] = dwp
            db_ref[...] = jnp.concatenate([dbp, jnp.zeros((7, CONV_CB), F32)], axis=0)

        @pl.when(n > 0)
        def _():
            dw_ref[...] += dwp
            db_ref[0:1, :] += dbp

    return pl.pallas_call(
        body, grid=(ncb, NB),
        in_specs=[pl.BlockSpec((8, CONV_CB), lambda j, n: (jnp.maximum(n * (BLK // 8) - 1, 0), cb0 + j)),
                  pl.BlockSpec((BLK, CONV_CB), lambda j, n: (n, cb0 + j)),
                  pl.BlockSpec((8, CONV_CB), lambda j, n: (jnp.minimum((n + 1) * (BLK // 8), last8), cb0 + j)),
                  pl.BlockSpec((8, CONV_CB), lambda j, n: (0, wb0 + j)),
                  pl.BlockSpec((1, CONV_CB), lambda j, n: (0, wb0 + j)),
                  pl.BlockSpec((BLK, CONV_CB), lambda j, n: (n, j)),
                  pl.BlockSpec((8, CONV_CB), lambda j, n: (jnp.minimum((n + 1) * (BLK // 8), last8), j))],
        out_specs=[pl.BlockSpec((BLK, CONV_CB), lambda j, n: (n, j)),
                   pl.BlockSpec((8, CONV_CB), lambda j, n: (0, j)),
                   pl.BlockSpec((8, CONV_CB), lambda j, n: (0, j))],
        out_shape=[SDS((T, width), BF16), SDS((8, width), F32), SDS((8, width), F32)],
        name=name)(proj, proj, proj, conv_w, conv_b, dact, dact)


HPG = SSM_HEADS // SSM_GROUPS


def _iota(shape, dim):
    return lax.broadcasted_iota(jnp.int32, shape, dim)


def _mm(a, b, ca=1, cb=0):
    return lax.dot_general(a.astype(BF16), b.astype(BF16), (((ca,), (cb,)), ((), ())), preferred_element_type=F32)


def _split3(v):
    hi = v.astype(BF16)
    r1 = v - hi.astype(F32)
    mid = r1.astype(BF16)
    lo = (r1 - mid.astype(F32)).astype(BF16)
    return hi, mid, lo


def _sel_r(parts, onehot, ca=1, cb=0):
    out = lax.dot_general(parts[0], onehot, (((ca,), (cb,)), ((), ())), preferred_element_type=F32)
    for p in parts[1:]:
        out = out + lax.dot_general(p, onehot, (((ca,), (cb,)), ((), ())), preferred_element_type=F32)
    return out


def _sel_l(onehot, parts):
    out = jnp.dot(onehot, parts[0], preferred_element_type=F32)
    for p in parts[1:]:
        out = out + jnp.dot(onehot, p, preferred_element_type=F32)
    return out


def _rows8(*rows):
    r = _iota((8, rows[0].shape[1]), 0)
    out = jnp.zeros((8, rows[0].shape[1]), F32)
    for k, v in enumerate(rows):
        out = jnp.where(r == k, v, out)
    return out


def _ssd_forward(x, z, bm, cm, dt_raw, st_prev, dtb, alog, dskip, gn, g, cst_scr):
    li, si = _iota((BLK, BLK), 0), _iota((BLK, BLK), 1)
    dt_all = jax.nn.softplus(dt_raw + dtb)
    a_row = -jnp.exp(alog)
    a_all = dt_all * a_row
    cs_all = _sel_l((li >= si).astype(BF16), _split3(a_all))
    cs_parts = _split3(cs_all)
    spread = (_iota((BLK, GRP_W), 0) == g * HPG + jnp.right_shift(_iota((BLK, GRP_W), 1), 6)).astype(BF16)
    dt_e = _sel_r(_split3(dt_all), spread)
    cs_e = _sel_r(cs_parts, spread)
    d_e = _sel_r(_split3(_rows8(dskip)), spread)[0:1]
    cs_last_e = jnp.sum(jnp.where(_iota((BLK, GRP_W), 0) == BLK - 1, cs_e, 0.0), axis=0, keepdims=True)
    p_e = jnp.exp(cs_e)
    w_e = jnp.exp(cs_last_e - cs_e)
    cd_e = jnp.exp(cs_last_e)
    xr = x * dt_e
    cst_scr[...] = cs_all.T
    cst_g = cst_scr[pl.ds(pl.multiple_of(g * HPG, HPG), HPG), :]
    own = jnp.right_shift(_iota((HPG, HPG * BLK), 1), 7) == _iota((HPG, HPG * BLK), 0)
    ownf = own.astype(F32)
    q_rows = [ownf, ownf, ownf] + [jnp.where(own, jnp.concatenate([p.astype(F32)] * HPG, axis=1), 0.0)
                                   for p in _split3(cst_g)]
    q2 = jnp.concatenate(q_rows + [jnp.zeros((BLK - 6 * HPG, HPG * BLK), F32)], axis=0).astype(BF16)
    lane1 = _iota((1, BLK), 1)
    p2 = jnp.where((lane1 >= 3 * HPG) & (lane1 < 6 * HPG), -1.0, 0.0)
    for k, part in enumerate(cs_parts):
        pick = ((li == g * HPG + si - k * HPG) & (si >= k * HPG) & (si < (k + 1) * HPG)).astype(BF16)
        p2 = p2 + jnp.dot(part, pick, preferred_element_type=F32)
    dmat = jnp.dot(p2.astype(BF16), q2, preferred_element_type=F32)
    causal = _iota((BLK, HPG * BLK), 0) >= jnp.bitwise_and(_iota((BLK, HPG * BLK), 1), BLK - 1)
    lam = jnp.exp(jnp.where(causal, dmat, NEG))
    gmat = _mm(cm, bm, 1, 1)
    m_all = lam * jnp.concatenate([gmat] * HPG, axis=1)
    mb = m_all.astype(BF16)
    lo = _iota((BLK, BLK), 1) < HEAD
    xrb = xr.astype(BF16)
    zero = jnp.zeros((BLK, BLK), BF16)
    bds, yd = [], []
    for i in range(HPG // 2):
        t = xrb[:, BLK * i:BLK * (i + 1)]
        bd = jnp.concatenate([jnp.where(lo, t, zero), jnp.where(lo, zero, t)], axis=0)
        bds.append(bd)
        yd.append(jnp.dot(mb[:, 2 * BLK * i:2 * BLK * (i + 1)], bd, preferred_element_type=F32))
    cs_st = _mm(cm, st_prev)
    y = jnp.concatenate(yd, axis=1) + cs_st * p_e + d_e * x
    xrw = xr * w_e
    st_new = cd_e * st_prev + _mm(bm, xrw, 0, 0)
    yz = y * _silu(z)
    rn = lax.rsqrt(jnp.sum(yz * yz, axis=1, keepdims=True) / GRP_W + EPS)
    return dict(out=yz * rn * gn, st_new=st_new, dt_all=dt_all, a_row=a_row, dt_e=dt_e, d_e=d_e, p_e=p_e, w_e=w_e,
                cd_e=cd_e, xr=xr, xrw=xrw, lam=lam, m_all=m_all, mb=mb, bds=bds, cs_st=cs_st, y=y, yz=yz, rn=rn, lo=lo)


def _ssd_backward(f, x, z, bm, cm, dt_raw, st_prev, dtb, gn, g, dout, dst_next, cst_scr):
    li, si = _iota((BLK, BLK), 0), _iota((BLK, BLK), 1)
    yz, rn, y, p_e, w_e, cd_e, xr = f["yz"], f["rn"], f["y"], f["p_e"], f["w_e"], f["cd_e"], f["xr"]
    dgn = jnp.sum(dout * yz * rn, axis=0, keepdims=True)
    t = dout * gn
    dyz = rn * t - yz * (rn * rn * rn) * (jnp.sum(yz * t, axis=1, keepdims=True) / GRP_W)
    dy = dyz * _silu(z)
    dz = dyz * y * _dsilu(z)
    dx = f["d_e"] * dy
    dd_e = jnp.sum(dy * x, axis=0, keepdims=True)
    dcsst = dy * p_e
    dp_e = dy * f["cs_st"]
    dcm = _mm(dcsst, st_prev, 1, 1)
    dst_prev = _mm(cm, dcsst, 0, 0) + cd_e * dst_next
    dcd_e = jnp.sum(dst_next * st_prev, axis=0, keepdims=True)
    dbm = _mm(f["xrw"], dst_next, 1, 1)
    dxrw = _mm(bm, dst_next)
    dxr = dxrw * w_e
    dw_e = dxrw * xr
    dyb = dy.astype(BF16)
    dms, dxr_d = [], []
    for i in range(HPG // 2):
        dyp = dyb[:, BLK * i:BLK * (i + 1)]
        dms.append(lax.dot_general(dyp, f["bds"][i], (((1,), (1,)), ((), ())), preferred_element_type=F32))
        r = lax.dot_general(f["mb"][:, 2 * BLK * i:2 * BLK * (i + 1)], dyp, (((0,), (0,)), ((), ())),
                            preferred_element_type=F32)
        dxr_d.append(jnp.where(f["lo"], r[0:BLK], r[BLK:2 * BLK]))
    dm_all = jnp.concatenate(dms, axis=1)
    dxr = dxr + jnp.concatenate(dxr_d, axis=1)
    dlg = dm_all * f["lam"]
    dg = dlg[:, 0:BLK]
    for j in range(1, HPG):
        dg = dg + dlg[:, BLK * j:BLK * (j + 1)]
    dcm = dcm + _mm(dg, bm)
    dbm = dbm + _mm(dg, cm, 0, 0)
    q_all = dm_all * f["m_all"]
    col_sums = jnp.sum(q_all, axis=0, keepdims=True)
    cst_scr[...] = jnp.zeros_like(cst_scr)
    cst_scr[pl.ds(pl.multiple_of(g * HPG, HPG), HPG), :] = _rows8(
        *[col_sums[:, BLK * j:BLK * (j + 1)] for j in range(HPG)])
    dcs = -cst_scr[...].T
    for j in range(HPG):
        dcs = dcs + jnp.where(si == g * HPG + j,
                              jnp.sum(q_all[:, BLK * j:BLK * (j + 1)], axis=1, keepdims=True), 0.0)
    unspread = (_iota((GRP_W, BLK), 1) == g * HPG + jnp.right_shift(_iota((GRP_W, BLK), 0), 6)).astype(BF16)
    dww = dw_e * w_e
    per_head = _sel_r(_split3(jnp.concatenate([dp_e * p_e - dww, dxr * x], axis=0)), unspread)
    last = _sel_r(_split3(_rows8(jnp.sum(dww, axis=0, keepdims=True) + dcd_e * cd_e, dd_e)), unspread)
    dcs = dcs + per_head[0:BLK] + jnp.where(li == BLK - 1, last[0:1], 0.0)
    da = _sel_l((si >= li).astype(BF16), _split3(dcs))
    ddt_all = da * f["a_row"] + per_head[BLK:2 * BLK]
    dalog = jnp.sum(da * f["dt_all"], axis=0, keepdims=True) * f["a_row"]
    dx = dx + dxr * f["dt_e"]
    ddt_raw = ddt_all * jax.nn.sigmoid(dt_raw + dtb)
    ddtb = jnp.sum(ddt_raw, axis=0, keepdims=True)
    ddskip = last[1:2]
    return dict(dx=dx, dz=dz, dbm=dbm, dcm=dcm, ddt_raw=ddt_raw, dst_prev=dst_prev, ddtb=ddtb, dalog=dalog,
                ddskip=ddskip, dgn=dgn)


def _ssd_in_specs(rev):
    cidx = (lambda c: NB - 1 - c) if rev else (lambda c: c)
    return [
        pl.BlockSpec((BLK, GRP_W), lambda g, c: (cidx(c), g)),
        pl.BlockSpec((BLK, SSM_STATE), lambda g, c: (cidx(c), SSM_INNER // SSM_STATE + g)),
        pl.BlockSpec((BLK, SSM_STATE), lambda g, c: (cidx(c), SSM_INNER // SSM_STATE + SSM_GROUPS + g)),
        pl.BlockSpec((BLK, 128), lambda g, c: (cidx(c), C_DT // 128)),
        pl.BlockSpec((BLK, GRP_W), lambda g, c: (cidx(c), C_ZS // GRP_W + g)),
        pl.BlockSpec((1, 128), lambda g, c: (0, 0)),
        pl.BlockSpec((1, 128), lambda g, c: (0, 0)),
        pl.BlockSpec((1, 128), lambda g, c: (0, 0)),
        pl.BlockSpec((1, GRP_W), lambda g, c: (0, g)),
    ]


def _ssd_fwd(xbc_act, proj, dt_bias, a_log, d_skip, g_norm):
    def body(xs_ref, b_ref, c_ref, dt_ref, z_ref, dtb_ref, al_ref, dsk_ref, gn_ref, y_ref, st_ref, s_scr, cst_scr):
        g = pl.program_id(0)
        c = pl.program_id(1)

        @pl.when(c == 0)
        def _():
            s_scr[...] = jnp.zeros_like(s_scr)

        st_prev = s_scr[...]
        st_ref[0, 0] = st_prev
        f = _ssd_forward(xs_ref[...], z_ref[...], b_ref[...], c_ref[...], dt_ref[...], st_prev, dtb_ref[...],
                         al_ref[...], dsk_ref[...], gn_ref[...], g, cst_scr)
        y_ref[...] = f["out"].astype(BF16)
        s_scr[...] = f["st_new"]

    return pl.pallas_call(
        body, grid=(SSM_GROUPS, NB), in_specs=_ssd_in_specs(False),
        out_specs=[pl.BlockSpec((BLK, GRP_W), lambda g, c: (c, g)),
                   pl.BlockSpec((1, 1, SSM_STATE, GRP_W), lambda g, c: (g, c, 0, 0))],
        out_shape=[SDS((T, SSM_INNER), BF16), SDS((SSM_GROUPS, NB, SSM_STATE, GRP_W), F32)],
        scratch_shapes=[pltpu.VMEM((SSM_STATE, GRP_W), F32), pltpu.VMEM((BLK, BLK), F32)],
        compiler_params=_cparams(),
        name="ssd_fwd")(xbc_act, xbc_act, xbc_act, proj, proj, dt_bias, a_log, d_skip, g_norm)


def _ssd_bwd(xbc_act, proj, dt_bias, a_log, d_skip, g_norm, states, dy):
    def body(xs_ref, b_ref, c_ref, dt_ref, z_ref, dtb_ref, al_ref, dsk_ref, gn_ref, st_ref, dy_ref,
             dxs_ref, db_ref, dc_ref, ddt_ref, dz_ref, ddtb_ref, dal_ref, ddsk_ref, dgn_ref, ds_scr, cst_scr):
        g = pl.program_id(0)
        c = pl.program_id(1)

        @pl.when(c == 0)
        def _():
            ds_scr[...] = jnp.zeros_like(ds_scr)
            dgn_ref[...] = jnp.zeros_like(dgn_ref)

        @pl.when((c == 0) & (g == 0))
        def _():
            ddtb_ref[...] = jnp.zeros_like(ddtb_ref)
            dal_ref[...] = jnp.zeros_like(dal_ref)
            ddsk_ref[...] = jnp.zeros_like(ddsk_ref)

        x, z, bm, cm, dt_raw, st_prev = xs_ref[...], z_ref[...], b_ref[...], c_ref[...], dt_ref[...], st_ref[0, 0]
        f = _ssd_forward(x, z, bm, cm, dt_raw, st_prev, dtb_ref[...], al_ref[...], dsk_ref[...], gn_ref[...], g,
                         cst_scr)
        d = _ssd_backward(f, x, z, bm, cm, dt_raw, st_prev, dtb_ref[...], gn_ref[...], g, dy_ref[...], ds_scr[...],
                          cst_scr)
        dxs_ref[...] = d["dx"]
        dz_ref[...] = d["dz"].astype(BF16)
        ds_scr[...] = d["dst_prev"]
        db_ref[...] = d["dbm"]
        dc_ref[...] = d["dcm"]
        ddt_ref[...] = d["ddt_raw"]
        dgn_ref[0:1, :] += d["dgn"]
        ddtb_ref[0:1, :] += d["ddtb"]
        dal_ref[0:1, :] += d["dalog"]
        ddsk_ref[0:1, :] += d["ddskip"]

    rc = lambda c: NB - 1 - c
    small = pl.BlockSpec((8, 128), lambda g, c: (0, 0))
    return pl.pallas_call(
        body, grid=(SSM_GROUPS, NB),
        in_specs=_ssd_in_specs(True) + [
            pl.BlockSpec((1, 1, SSM_STATE, GRP_W), lambda g, c: (g, rc(c), 0, 0)),
            pl.BlockSpec((BLK, GRP_W), lambda g, c: (rc(c), g))],
        out_specs=[pl.BlockSpec((BLK, GRP_W), lambda g, c: (rc(c), g)),
                   pl.BlockSpec((BLK, SSM_STATE), lambda g, c: (rc(c), g)),
                   pl.BlockSpec((BLK, SSM_STATE), lambda g, c: (rc(c), g)),
                   pl.BlockSpec((BLK, 128), lambda g, c: (rc(c), g)),
                   pl.BlockSpec((BLK, GRP_W), lambda g, c: (rc(c), g)),
                   small, small, small,
                   pl.BlockSpec((8, GRP_W), lambda g, c: (0, g))],
        out_shape=[SDS((T, SSM_INNER), F32), SDS((T, GRP_W), F32), SDS((T, GRP_W), F32), SDS((T, GRP_W), F32),
                   SDS((T, SSM_INNER), BF16), SDS((8, 128), F32), SDS((8, 128), F32), SDS((8, 128), F32),
                   SDS((8, SSM_INNER), F32)],
        scratch_shapes=[pltpu.VMEM((SSM_STATE, GRP_W), F32), pltpu.VMEM((BLK, BLK), F32)],
        compiler_params=_cparams(),
        name="ssd_bwd")(xbc_act, xbc_act, xbc_act, proj, proj, dt_bias, a_log, d_skip, g_norm, states, dy)


def _post_a(o, proj, sn, w_att, w_ssm, w_o):
    def body(o_ref, za_ref, ga_ref, gs_ref, sn_ref, wa_ref, ws_ref, wo_ref, a_ref, mg_ref, ya_ref, ys_ref, out_ref):
        a = (o_ref[...] * _silu(za_ref[...])).astype(BF16)
        a_ref[...] = a
        ya = jnp.dot(a, wa_ref[...], preferred_element_type=F32)
        ys = jnp.dot(sn_ref[...], ws_ref[...], preferred_element_type=F32)
        ya_ref[...] = ya
        ys_ref[...] = ys
        mg = (jax.nn.sigmoid(ga_ref[...]) * ya + jax.nn.sigmoid(gs_ref[...]) * ys).astype(BF16)
        mg_ref[...] = mg
        out_ref[...] = jnp.dot(mg, wo_ref[...], preferred_element_type=F32)

    row = pl.BlockSpec((BLK, D_MODEL), lambda i: (i, 0))
    pcol = lambda c0: pl.BlockSpec((BLK, D_MODEL), lambda i: (i, c0 // D_MODEL))
    full = lambda r: pl.BlockSpec((r, D_MODEL), lambda i: (0, 0))
    return pl.pallas_call(
        body, grid=(NB,),
        in_specs=[row, pcol(C_ZA), pcol(C_GA), pcol(C_GS), pl.BlockSpec((BLK, SSM_INNER), lambda i: (i, 0)),
                  full(D_MODEL), full(SSM_INNER), full(D_MODEL)],
        out_specs=[row, row, row, row, row],
        out_shape=[SDS((T, D_MODEL), BF16), SDS((T, D_MODEL), BF16), SDS((T, D_MODEL), F32), SDS((T, D_MODEL), F32),
                   SDS((T, D_MODEL), F32)],
        compiler_params=_cparams(), name="post_a")(o, proj, proj, proj, sn, w_att, w_ssm, w_o)


def _post_b(out, h, tgt, proj, ya, ys, o, g_post, w_att, w_ssm, w_o):
    def body(out_ref, h_ref, t_ref, za_ref, ga_ref, gs_ref, ya_ref, ys_ref, o_ref, gp_ref, wa_ref, ws_ref, wo_ref,
             loss_ref, dres_ref, dout_ref, dya_ref, dys_ref, dga_ref, dgs_ref, do_ref, dza_ref, dsn_ref, dgp_ref):
        i = pl.program_id(0)
        x = out_ref[...]
        gp = gp_ref[...]
        r = lax.rsqrt(jnp.mean(x * x, axis=-1, keepdims=True) + EPS)
        row = i * BLK + lax.broadcasted_iota(jnp.int32, (BLK, 1), 0)
        res = h_ref[...] + jnp.where(row >= PAD, x * r * gp, 0.0)
        live = row >= PAD + N_META
        err = jnp.where(live, res - t_ref[...], 0.0)
        lpart = 0.5 * jnp.sum(jnp.sum(err * err, axis=1, keepdims=True) / D_MODEL, axis=0, keepdims=True)
        dres = err / D_MODEL
        dres_ref[...] = dres
        gpart = jnp.sum(dres * x * r, axis=0, keepdims=True)

        @pl.when(i == 0)
        def _():
            loss_ref[...] = jnp.zeros_like(loss_ref)
            dgp_ref[...] = jnp.zeros_like(dgp_ref)

        loss_ref[...] += jnp.broadcast_to(lpart, loss_ref.shape)
        dgp_ref[0:1, :] += gpart
        gd = gp * dres
        dout = (r * gd - x * (r * r * r) * jnp.mean(x * gd, axis=-1, keepdims=True)).astype(BF16)
        dout_ref[...] = dout
        dmg = lax.dot_general(dout, wo_ref[...], (((1,), (1,)), ((), ())), preferred_element_type=F32)
        sga = jax.nn.sigmoid(ga_ref[...])
        sgs = jax.nn.sigmoid(gs_ref[...])
        dya = (dmg * sga).astype(BF16)
        dys = (dmg * sgs).astype(BF16)
        dya_ref[...] = dya
        dys_ref[...] = dys
        dga_ref[...] = (dmg * ya_ref[...] * sga * (1.0 - sga)).astype(BF16)
        dgs_ref[...] = (dmg * ys_ref[...] * sgs * (1.0 - sgs)).astype(BF16)
        da = lax.dot_general(dya, wa_ref[...], (((1,), (1,)), ((), ())), preferred_element_type=F32)
        za = za_ref[...]
        do_ref[...] = da * _silu(za)
        dza_ref[...] = (da * o_ref[...] * _dsilu(za)).astype(BF16)
        dsn_ref[...] = lax.dot_general(dys, ws_ref[...], (((1,), (1,)), ((), ())), preferred_element_type=F32)

    row = pl.BlockSpec((BLK, D_MODEL), lambda i: (i, 0))
    pcol = lambda c0: pl.BlockSpec((BLK, D_MODEL), lambda i: (i, c0 // D_MODEL))
    full = lambda r: pl.BlockSpec((r, D_MODEL), lambda i: (0, 0))
    small = pl.BlockSpec((8, D_MODEL), lambda i: (0, 0))
    return pl.pallas_call(
        body, grid=(NB,),
        in_specs=[row, row, row, pcol(C_ZA), pcol(C_GA), pcol(C_GS), row, row, row,
                  pl.BlockSpec((1, D_MODEL), lambda i: (0, 0)), full(D_MODEL), full(SSM_INNER), full(D_MODEL)],
        out_specs=[pl.BlockSpec((8, 128), lambda i: (0, 0)), row, row, row, row, row, row, row, row,
                   pl.BlockSpec((BLK, SSM_INNER), lambda i: (i, 0)), small],
        out_shape=[SDS((8, 128), F32), SDS((T, D_MODEL), F32), SDS((T, D_MODEL), BF16), SDS((T, D_MODEL), BF16),
                   SDS((T, D_MODEL), BF16), SDS((T, D_MODEL), BF16), SDS((T, D_MODEL), BF16), SDS((T, D_MODEL), F32),
                   SDS((T, D_MODEL), BF16), SDS((T, SSM_INNER), F32), SDS((8, D_MODEL), F32)],
        compiler_params=_cparams(), name="post_b")(out, h, tgt, proj, proj, proj, ya, ys, o, g_post, w_att, w_ssm, w_o)


def _assemble(dq, dza, dga, dgs, dzs, dxx, dxb, dxc, dk, dv, ddt4):
    def body(dq_ref, dza_ref, dga_ref, dgs_ref, dzs_ref, dxx_ref, dxb_ref, dxc_ref, dk_ref, dv_ref, ddt_ref, o_ref):
        o_ref[:, C_Q:C_Q + D_MODEL] = dq_ref[...].astype(BF16)
        o_ref[:, C_ZA:C_ZA + D_MODEL] = dza_ref[...]
        o_ref[:, C_GA:C_GA + D_MODEL] = dga_ref[...]
        o_ref[:, C_GS:C_GS + D_MODEL] = dgs_ref[...]
        o_ref[:, C_ZS:C_ZS + SSM_INNER] = dzs_ref[...]
        o_ref[:, C_XBC:C_XBC + SSM_INNER] = dxx_ref[...]
        o_ref[:, C_XBC + SSM_INNER:C_XBC + SSM_INNER + GRP_W] = dxb_ref[...]
        o_ref[:, C_XBC + SSM_INNER + GRP_W:C_XBC + CONV_DIM] = dxc_ref[...]
        o_ref[:, C_K:C_K + KV_W] = dk_ref[...].astype(BF16)
        o_ref[:, C_V:C_V + KV_W] = dv_ref[...].astype(BF16)
        d4 = ddt_ref[...]
        o_ref[:, C_DT:C_DT + 128] = (d4[:, 0:128] + d4[:, 128:256] + d4[:, 256:384] + d4[:, 384:512]).astype(BF16)

    spec = lambda w: pl.BlockSpec((BLK, w), lambda i: (i, 0))
    ins = [dq, dza, dga, dgs, dzs, dxx, dxb, dxc, dk, dv, ddt4]
    return pl.pallas_call(
        body, grid=(NB,), in_specs=[spec(a.shape[1]) for a in ins], out_specs=spec(PW),
        out_shape=SDS((T, PW), BF16), name="assemble")(*ins)


def _adamw_math(w, g, m, v):
    m = ADAM_B1 * m + (1.0 - ADAM_B1) * g
    v = ADAM_B2 * v + (1.0 - ADAM_B2) * (g * g)
    m_hat = m / (1.0 - ADAM_B1 ** ADAM_STEP)
    v_hat = v / (1.0 - ADAM_B2 ** ADAM_STEP)
    delta = -ADAM_LR * (m_hat / (jnp.sqrt(v_hat) + ADAM_EPS) + ADAM_WD * w)
    return delta, m, v


def _sum_adamw(recv, w, m, v, tc, name):
    rows, cols = w.shape
    assert cols % tc == 0

    def body(r_ref, w_ref, m_ref, v_ref, g_ref, d_ref, nm_ref, nv_ref):
        g = r_ref[0].astype(F32)
        for d in range(1, N_CHIP):
            g = g + r_ref[d].astype(F32)
        g_ref[...] = g
        delta, nm, nv = _adamw_math(w_ref[...], g, m_ref[...], v_ref[...])
        d_ref[...] = delta
        nm_ref[...] = nm
        nv_ref[...] = nv

    blk = pl.BlockSpec((rows, tc), lambda i: (0, i))
    return pl.pallas_call(
        body, grid=(cols // tc,),
        in_specs=[pl.BlockSpec((N_CHIP, rows, tc), lambda i: (0, 0, i)), blk, blk, blk],
        out_specs=[blk, blk, blk, blk], out_shape=[SDS((rows, cols), F32)] * 4,
        compiler_params=_cparams(), name=name)(recv, w, m, v)


def _sum_adamw_rows3(recv, w3, m3, v3, name):
    pairs = 61
    assert (SHARD_IN // 2) % pairs == 0

    def body(r_ref, w_ref, m_ref, v_ref, g_ref, d_ref, nm_ref, nv_ref):
        g = r_ref[0].astype(F32)
        for d in range(1, N_CHIP):
            g = g + r_ref[d].astype(F32)
        g = g.reshape(2 * pairs, ROW_TILES, 128)
        g_ref[...] = g
        delta, nm, nv = _adamw_math(w_ref[...], g, m_ref[...], v_ref[...])
        d_ref[...] = delta
        nm_ref[...] = nm
        nv_ref[...] = nv

    blk = pl.BlockSpec((2 * pairs, ROW_TILES, 128), lambda i: (i, 0, 0))
    return pl.pallas_call(
        body, grid=(SHARD_IN // 2 // pairs,),
        in_specs=[pl.BlockSpec((N_CHIP, pairs, 2 * ROW_TILES, 128), lambda i: (0, i, 0, 0)), blk, blk, blk],
        out_specs=[blk, blk, blk, blk], out_shape=[SDS(w3.shape, F32)] * 4,
        compiler_params=_cparams(), name=name)(recv, w3, m3, v3)


ROW_GPRE, ROW_CONVB, ROW_DTB, ROW_ALOG, ROW_DSKIP, ROW_SINK, ROW_GSSM, ROW_GPOST = 0, 1, 4, 5, 6, 7, 8, 10
REP_ROWS, ROW_CONVW, ROW_META, SM_ROWS = 16, 16, 24, 40
CW_SHARD = CONV_DIM // N_DEV
META_SHARD = D_MODEL // N_DEV


def _small_pack(dgpre, dbx, dbb, dbc, ddtb, dal, ddsk, dsink, dgn, dgp, dwx, dwb, dwc, dh):
    def body(dgpre_ref, dbx_ref, dbb_ref, dbc_ref, ddtb_ref, dal_ref, ddsk_ref, dsink_ref, dgn_ref, dgp_ref,
             dwx_ref, dwb_ref, dwc_ref, dh_ref, o_ref, rep):
        rep[...] = jnp.zeros_like(rep)
        rep[ROW_GPRE:ROW_GPRE + 1, :] = dgpre_ref[0:1, :]
        rep[ROW_CONVB:ROW_CONVB + 1, :] = dbx_ref[0:1, 0:1024]
        rep[ROW_CONVB + 1:ROW_CONVB + 2, :] = dbx_ref[0:1, 1024:2048]
        rep[ROW_CONVB + 2:ROW_CONVB + 3, 0:512] = dbb_ref[0:1, :]
        rep[ROW_CONVB + 2:ROW_CONVB + 3, 512:1024] = dbc_ref[0:1, :]
        rep[ROW_DTB:ROW_DTB + 1, 0:128] = ddtb_ref[0:1, :]
        rep[ROW_ALOG:ROW_ALOG + 1, 0:128] = dal_ref[0:1, :]
        rep[ROW_DSKIP:ROW_DSKIP + 1, 0:128] = ddsk_ref[0:1, :]
        rep[ROW_SINK:ROW_SINK + 1, 0:128] = dsink_ref[0:1, :]
        rep[ROW_GSSM:ROW_GSSM + 1, :] = dgn_ref[0:1, 0:1024]
        rep[ROW_GSSM + 1:ROW_GSSM + 2, :] = dgn_ref[0:1, 1024:2048]
        rep[ROW_GPOST:ROW_GPOST + 1, :] = dgp_ref[0:1, :]
        cw = jnp.concatenate([dwx_ref[...], dwb_ref[...], dwc_ref[...]], axis=1)
        mh = dh_ref[...]
        o_ref[...] = jnp.zeros_like(o_ref)
        for p in range(N_DEV):
            o_ref[p, 0:REP_ROWS, :] = rep[...]
            o_ref[p, ROW_CONVW:ROW_CONVW + 8, 0:CW_SHARD] = cw[:, p * CW_SHARD:(p + 1) * CW_SHARD]
            o_ref[p, ROW_META:ROW_META + N_META, 0:META_SHARD] = mh[:, p * META_SHARD:(p + 1) * META_SHARD]

    ins = [dgpre, dbx, dbb, dbc, ddtb, dal, ddsk, dsink, dgn, dgp, dwx, dwb, dwc]
    return pl.pallas_call(
        body, grid=(1,),
        in_specs=[pl.BlockSpec(a.shape, lambda i: (0, 0)) for a in ins]
        + [pl.BlockSpec((N_META, D_MODEL), lambda i: (PAD // N_META, 0))],
        out_specs=pl.BlockSpec((N_DEV, SM_ROWS, 1024), lambda i: (0, 0, 0)),
        out_shape=SDS((N_DEV, SM_ROWS, 1024), F32), scratch_shapes=[pltpu.VMEM((REP_ROWS, 1024), F32)],
        name="small_pack")(*ins, dh)


def _small_finish(recv, params):
    npar = len(params)

    def body(*refs):
        r_ref = refs[0]
        wmv = refs[1:1 + 3 * npar]
        outs = refs[1 + 3 * npar:1 + 7 * npar]
        gs = refs[-1]
        g = r_ref[0]
        for d in range(1, N_CHIP):
            g = g + r_ref[d]
        gs[...] = g
        grads = [
            gs[ROW_GPRE:ROW_GPRE + 1, :],
            jnp.concatenate([gs[ROW_CONVB + k:ROW_CONVB + k + 1, :] for k in range(3)], axis=1),
            gs[ROW_DTB:ROW_DTB + 1, 0:SSM_HEADS], gs[ROW_ALOG:ROW_ALOG + 1, 0:SSM_HEADS],
            gs[ROW_DSKIP:ROW_DSKIP + 1, 0:SSM_HEADS], gs[ROW_SINK:ROW_SINK + 1, 0:Q_HEADS],
            jnp.concatenate([gs[ROW_GSSM:ROW_GSSM + 1, :], gs[ROW_GSSM + 1:ROW_GSSM + 2, :]], axis=1),
            gs[ROW_GPOST:ROW_GPOST + 1, :],
            gs[ROW_CONVW:ROW_CONVW + 4, 0:CW_SHARD],
            gs[ROW_META:ROW_META + N_META, 0:META_SHARD]]
        for i in range(npar):
            w_ref, m_ref, v_ref = wmv[3 * i:3 * i + 3]
            delta, nm, nv = _adamw_math(w_ref[...], grads[i], m_ref[...], v_ref[...])
            outs[4 * i][...] = grads[i]
            outs[4 * i + 1][...] = delta
            outs[4 * i + 2][...] = nm
            outs[4 * i + 3][...] = nv

    flat = [a for wmv in params for a in wmv]
    res = pl.pallas_call(
        body, out_shape=[SDS(wmv[0].shape, F32) for wmv in params for _ in range(4)],
        scratch_shapes=[pltpu.VMEM((SM_ROWS, 1024), F32)], name="small_finish")(recv, *flat)
    return [tuple(res[4 * i:4 * i + 4]) for i in range(npar)]


def _slab(ref, px, py, pc):
    return ref.at[4 * px + 2 * py + pc]


def _bounce(src, dst, buf, sem):
    cp = pltpu.make_async_copy(src, buf, sem)
    cp.start()
    cp.wait()
    cp = pltpu.make_async_copy(buf, dst, sem)
    cp.start()
    cp.wait()


def _all_gather(shards):
    na = len(shards)

    def body(*refs):
        ins, outs = refs[:na], refs[na:2 * na]
        send_sems, recv_sems, local_sems = refs[2 * na:2 * na + 3]
        bufs = refs[2 * na + 3:]
        x, y, c = lax.axis_index("x"), lax.axis_index("y"), lax.axis_index("c")
        me, sibling = (x, y, c), (x, y, 1 - c)
        chips = [(1 - x, y), (x, 1 - y), (1 - x, 1 - y)]

        def copy(a, k, block, to, src=None):
            dst = _slab(outs[a], *block)
            return pltpu.make_async_remote_copy(
                src_ref=dst if src is None else src, dst_ref=dst, send_sem=send_sems.at[a, k],
                recv_sem=recv_sems.at[a, k], device_id=to, device_id_type=MESH)

        first = []
        for a in range(na):
            first.append(copy(a, 0, me, sibling, src=ins[a]))
            first += [copy(a, 1 + j, me, (*chip, c), src=ins[a]) for j, chip in enumerate(chips)]
        for cp in first:
            cp.start()
        for a in range(na):
            _bounce(ins[a], _slab(outs[a], *me), bufs[a], local_sems.at[a])
        passed = []
        for j, chip in enumerate(chips):
            for a in range(na):
                copy(a, 1 + j, (*chip, c), me).wait_recv()
                cp = copy(a, 4 + j, (*chip, c), sibling)
                cp.start()
                passed.append(cp)
        for a in range(na):
            copy(a, 0, sibling, me).wait_recv()
            for j, chip in enumerate(chips):
                copy(a, 4 + j, (*chip, 1 - c), me).wait_recv()
        for cp in first + passed:
            cp.wait_send()

    return pl.pallas_call(
        body, in_specs=[ANY] * na, out_specs=[ANY] * na,
        out_shape=[SDS((N_DEV,) + s.shape, s.dtype) for s in shards],
        scratch_shapes=[pltpu.SemaphoreType.DMA((na, 7)), pltpu.SemaphoreType.DMA((na, 7)),
                        pltpu.SemaphoreType.DMA((na,))] + [pltpu.VMEM(s.shape, s.dtype) for s in shards],
        name="all_gather")(*shards)


N_CHIP = 4


def _exchange_pair(parts):
    na = len(parts)

    def body(*refs):
        ins, own, got = refs[:na], refs[na:2 * na], refs[2 * na:3 * na]
        send_sems, recv_sems, local_sems = refs[3 * na:3 * na + 3]
        bufs = refs[3 * na + 3:]
        x, y, c = lax.axis_index("x"), lax.axis_index("y"), lax.axis_index("c")
        sibling = (x, y, 1 - c)
        sent = []
        for a in range(na):
            for k in range(N_CHIP):
                cp = pltpu.make_async_remote_copy(
                    src_ref=ins[a].at[2 * k + 1 - c], dst_ref=got[a].at[k], send_sem=send_sems.at[a, k],
                    recv_sem=recv_sems.at[a, k], device_id=sibling, device_id_type=MESH)
                cp.start()
                sent.append(cp)
        for a in range(na):
            for k in range(N_CHIP):
                _bounce(ins[a].at[2 * k + c], own[a].at[k], bufs[a], local_sems.at[a])
        for cp in sent:
            cp.wait()

    half = [SDS((N_CHIP,) + p.shape[1:], p.dtype) for p in parts]
    res = pl.pallas_call(
        body, in_specs=[ANY] * na, out_specs=[ANY] * (2 * na), out_shape=half + half,
        scratch_shapes=[pltpu.SemaphoreType.DMA((na, N_CHIP)), pltpu.SemaphoreType.DMA((na, N_CHIP)),
                        pltpu.SemaphoreType.DMA((na,))] + [pltpu.VMEM(p.shape[1:], p.dtype) for p in parts],
        name="exchange_pair")(*parts)
    return res[:na], res[na:]


def _pair_sum(own, got):
    na = len(own)

    def body(*refs):
        for a in range(na):
            o_ref, g_ref, s_ref = refs[a], refs[na + a], refs[2 * na + a]
            s_ref[...] = (o_ref[...].astype(F32) + g_ref[...].astype(F32)).astype(s_ref.dtype)

    def spec(p):
        nd = len(p.shape) - 1
        return pl.BlockSpec((1,) + p.shape[1:], lambda k, nd=nd: (k,) + (0,) * nd)

    return pl.pallas_call(
        body, grid=(N_CHIP,), in_specs=[spec(p) for p in own] + [spec(p) for p in got],
        out_specs=[spec(p) for p in own], out_shape=[SDS(p.shape, p.dtype) for p in own],
        compiler_params=_cparams(), name="pair_sum")(*own, *got)


def _exchange_chips(parts):
    na = len(parts)

    def body(*refs):
        ins, outs = refs[:na], refs[na:2 * na]
        send_sems, recv_sems, local_sems = refs[2 * na:2 * na + 3]
        bufs = refs[2 * na + 3:]
        x, y, c = lax.axis_index("x"), lax.axis_index("y"), lax.axis_index("c")
        mine = 2 * x + y
        chips = [(1 - x, y), (x, 1 - y), (1 - x, 1 - y)]
        sent = []
        for a in range(na):
            for j, (px, py) in enumerate(chips):
                cp = pltpu.make_async_remote_copy(
                    src_ref=ins[a].at[2 * px + py], dst_ref=outs[a].at[mine], send_sem=send_sems.at[a, j],
                    recv_sem=recv_sems.at[a, j], device_id=(px, py, c), device_id_type=MESH)
                cp.start()
                sent.append(cp)
        for a in range(na):
            _bounce(ins[a].at[mine], outs[a].at[mine], bufs[a], local_sems.at[a])
        for a in range(na):
            for j, (px, py) in enumerate(chips):
                pltpu.make_async_remote_copy(
                    src_ref=ins[a].at[2 * px + py], dst_ref=outs[a].at[2 * px + py], send_sem=send_sems.at[a, j],
                    recv_sem=recv_sems.at[a, j], device_id=(px, py, c), device_id_type=MESH).wait_recv()
        for cp in sent:
            cp.wait_send()

    return pl.pallas_call(
        body, in_specs=[ANY] * na, out_specs=[ANY] * na, out_shape=[SDS(p.shape, p.dtype) for p in parts],
        scratch_shapes=[pltpu.SemaphoreType.DMA((na, 3)), pltpu.SemaphoreType.DMA((na, 3)),
                        pltpu.SemaphoreType.DMA((na,))] + [pltpu.VMEM(p.shape[1:], p.dtype) for p in parts],
        name="exchange_chips")(*parts)


ROW_TILES = D_MODEL // 128


def _rows3(t):
    return jnp.transpose(t[0]).reshape(t.shape[2], ROW_TILES, 128)


def _unrows3(t):
    return jnp.transpose(t.reshape(t.shape[0], D_MODEL))[None]


def _cast_shards(w_in3, w_att, w_ssm, w_o):
    def body(wi_ref, wa_ref, ws_ref, wo_ref, a_ref, b_ref, c_ref, d_ref):
        a_ref[...] = wi_ref[...].reshape(SHARD_IN // 2, 2 * ROW_TILES, 128).astype(BF16)
        b_ref[...] = wa_ref[...].astype(BF16)
        c_ref[...] = ws_ref[...].astype(BF16)
        d_ref[...] = wo_ref[...].astype(BF16)

    return pl.pallas_call(
        body, out_shape=[SDS((SHARD_IN // 2, 2 * ROW_TILES, 128), BF16), SDS(w_att.shape, BF16),
                         SDS(w_ssm.shape, BF16), SDS(w_o.shape, BF16)],
        compiler_params=_cparams(), name="cast_shards")(w_in3, w_att, w_ssm, w_o)


def _pieces():
    out = []
    for r0, c0, w in _SEGS:
        r = r0
        while r < r0 + w:
            d = r // SHARD_IN
            n = min(r0 + w, (d + 1) * SHARD_IN) - r
            out.append((c0 + (r - r0), d, r - d * SHARD_IN, n))
            r += n
    return out


def _to_aligned_t(slabs):
    def body(a_ref, o_ref):
        for (t, d, s, n) in _pieces():
            o_ref[t:t + n, :] = a_ref[d, s // 2:(s + n) // 2].reshape(n, D_MODEL)
        o_ref[C_DT + 32:C_DT + 128, :] = jnp.zeros((96, D_MODEL), slabs.dtype)

    return pl.pallas_call(body, out_shape=SDS((PW, D_MODEL), slabs.dtype), compiler_params=_cparams(),
                          name="to_aligned")(slabs)


def _from_aligned_t(g):
    def body(g_ref, o_ref):
        for (t, d, s, n) in _pieces():
            o_ref[d, s // 2:(s + n) // 2] = g_ref[t:t + n, :].reshape(n // 2, 2 * ROW_TILES, 128)

    return pl.pallas_call(body, out_shape=SDS((N_DEV, SHARD_IN // 2, 2 * ROW_TILES, 128), g.dtype),
                          compiler_params=_cparams(), name="from_aligned")(g)


_SEGS = [
    (R_Q, C_Q, 1024), (R_K, C_K, 256), (R_V, C_V, 256), (R_ZA, C_ZA, 1024), (R_ZS, C_ZS, 2048),
    (R_XBC, C_XBC, 3072), (R_DT, C_DT, 32), (R_GA, C_GA, 1024), (R_GS, C_GS, 1024)]


def _pad_lanes(v, n=128):
    return jnp.pad(v, ((0, 0), (0, n - v.shape[1])))


def _local_step(h, tgt, w_alt, w_att, w_ssm, w_o, g_pre, conv_w8, conv_b, dt_bias, a_log, d_skip, sinks,
                g_ssm, g_post):
    dtb, al, dsk, snk = _pad_lanes(dt_bias), _pad_lanes(a_log), _pad_lanes(d_skip), _pad_lanes(sinks)
    u = _norm_u(h, g_pre)
    proj = _matmul(u, w_alt, "nt", F32, 1088, 896, D_MODEL, "in_proj")
    o = _attn_fwd(proj, snk)
    xbc_act = _conv_fwd(proj, conv_w8, conv_b)
    sn, states = _ssd_fwd(xbc_act, proj, dtb, al, dsk, g_ssm)
    a_in, mg, ya, ys, out = _post_a(o, proj, sn, w_att, w_ssm, w_o)
    (loss, dres, dout, dya, dys, dga, dgs, do, dza, dsn, dgp) = _post_b(
        out, h, tgt, proj, ya, ys, o, g_post, w_att, w_ssm, w_o)
    dxs, dbm, dcm, ddt4, dzs, ddtb, dal, ddsk, dgn = _ssd_bwd(xbc_act, proj, dtb, al, dsk, g_ssm, states, dsn)
    dxx, dwx, dbx = _conv_bwd(proj, conv_w8, conv_b, dxs, 0, "conv_bwd_x")
    dxb, dwb, dbb = _conv_bwd(proj, conv_w8, conv_b, dbm, SSM_INNER, "conv_bwd_b")
    dxc, dwc, dbc = _conv_bwd(proj, conv_w8, conv_b, dcm, SSM_INNER + GRP_W, "conv_bwd_c")
    dq, dk, dv, dsink = _attn_bwd(proj, snk, do)
    dproj = _assemble(dq, dza, dga, dgs, dzs, dxx, dxb, dxc, dk, dv, ddt4)
    du = _matmul(dproj, w_alt, "nn", F32, 1088, D_MODEL, 896, "d_u")
    dw_alt = _matmul(dproj, u, "tn", BF16, 896, D_MODEL, T, "d_w_in")
    dh, dgpre = _norm_bwd(h, g_pre, du, dres)
    dw_att = _matmul(a_in, dya, "tn", BF16, D_MODEL, D_MODEL, T, "d_w_att")
    dw_ssm = _matmul(sn, dys, "tn", BF16, D_MODEL, D_MODEL, T, "d_w_ssm")
    dw_o = _matmul(mg, dout, "tn", BF16, D_MODEL, D_MODEL, T, "d_w_o")
    return dict(
        loss=loss[0, 0], dh=dh, dw_alt=dw_alt, dw_att=dw_att, dw_ssm=dw_ssm, dw_o=dw_o,
        small=(dgpre, dbx, dbb, dbc, ddtb, dal, ddsk, dsink, dgn, dgp, dwx, dwb, dwc))


def kernel(x, meta_tokens, g_pre, w_in, conv_w, conv_b, dt_bias, a_log, d_skip, attn_sinks, g_ssm_norm, w_out_att, w_out_ssm, w_out, g_post, loss_target, m_meta_tokens, m_g_pre, m_w_in, m_conv_w, m_conv_b, m_dt_bias, m_a_log, m_d_skip, m_attn_sinks, m_g_ssm_norm, m_w_out_att, m_w_out_ssm, m_w_out, m_g_post, v_meta_tokens, v_g_pre, v_w_in, v_conv_w, v_conv_b, v_dt_bias, v_a_log, v_d_skip, v_attn_sinks, v_g_ssm_norm, v_w_out_att, v_w_out_ssm, v_w_out, v_g_post):
    w_in3, m_in3, v_in3 = _rows3(w_in), _rows3(m_w_in), _rows3(v_w_in)
    a_sh, att_sh, ssm_sh, o_sh = _cast_shards(w_in3, w_out_att[0], w_out_ssm[0], w_out[0])
    cw_sh = jnp.pad(conv_w[0], ((0, 4), (0, 0)))
    a_all, att_all, ssm_all, o_all, meta_all, cw_all = _all_gather([a_sh, att_sh, ssm_sh, o_sh, meta_tokens, cw_sh])
    w_alt = _to_aligned_t(a_all)
    w_att = att_all.reshape(D_MODEL, D_MODEL)
    w_ssm = ssm_all.reshape(SSM_INNER, D_MODEL)
    w_o = o_all.reshape(D_MODEL, D_MODEL)
    meta_full = meta_all.transpose(1, 0, 2).reshape(N_META, D_MODEL)
    conv_w8 = cw_all.transpose(1, 0, 2).reshape(8, CONV_DIM)

    h = jnp.concatenate([jnp.zeros((PAD, D_MODEL), F32), meta_full, x[0]], axis=0)
    tgt = jnp.concatenate([jnp.zeros((PAD + N_META, D_MODEL), F32), loss_target[0]], axis=0)
    r = _local_step(h, tgt, w_alt, w_att, w_ssm, w_o, g_pre, conv_w8, conv_b, dt_bias, a_log, d_skip, attn_sinks,
                    g_ssm_norm, g_post)
    loss = lax.psum(r["loss"], ("x", "y", "c"))
    grad_x = r["dh"][PAD + N_META:][None]

    small8 = _small_pack(*r["small"], r["dh"])
    own, got = _exchange_pair([
        _from_aligned_t(r["dw_alt"]), r["dw_att"].reshape(N_DEV, 128, D_MODEL),
        r["dw_ssm"].reshape(N_DEV, 256, D_MODEL), r["dw_o"].reshape(N_DEV, 128, D_MODEL), small8])
    ra, r_att, r_ssm, r_o, rs = _exchange_chips(_pair_sum(own, got))

    res_in = [_unrows3(t) for t in _sum_adamw_rows3(ra, w_in3, m_in3, v_in3, "adamw_w_in")]
    res_att = [t[None] for t in _sum_adamw(r_att, w_out_att[0], m_w_out_att[0], v_w_out_att[0], 512, "adamw_w_att")]
    res_ssm = [t[None] for t in _sum_adamw(r_ssm, w_out_ssm[0], m_w_out_ssm[0], v_w_out_ssm[0], 512, "adamw_w_ssm")]
    res_o = [t[None] for t in _sum_adamw(r_o, w_out[0], m_w_out[0], v_w_out[0], 512, "adamw_w_o")]
    (res_gpre, res_convb, res_dtb, res_alog, res_dskip, res_sink, res_gssm, res_gpost, res_cw, res_meta) = _small_finish(
        rs, [(g_pre, m_g_pre, v_g_pre), (conv_b, m_conv_b, v_conv_b), (dt_bias, m_dt_bias, v_dt_bias),
             (a_log, m_a_log, v_a_log), (d_skip, m_d_skip, v_d_skip), (attn_sinks, m_attn_sinks, v_attn_sinks),
             (g_ssm_norm, m_g_ssm_norm, v_g_ssm_norm), (g_post, m_g_post, v_g_post),
             (conv_w[0], m_conv_w[0], v_conv_w[0]), (meta_tokens, m_meta_tokens, v_meta_tokens)])
    res_cw = [t[None] for t in res_cw]
    per_weight = [res_meta, res_gpre, res_in, res_cw, res_convb, res_dtb, res_alog, res_dskip, res_sink, res_gssm,
                  res_att, res_ssm, res_o, res_gpost]
    return (loss, grad_x, *[p[0] for p in per_weight], *[p[1] for p in per_weight], *[p[2] for p in per_weight],
            *[p[3] for p in per_weight])
```

```python
import functools
import math

import jax
import jax.numpy as jnp
from jax import lax
from jax.experimental import pallas as pl
from jax.experimental.pallas import tpu as pltpu

F32 = jnp.float32
BF16 = jnp.bfloat16
SDS = jax.ShapeDtypeStruct
HI = lax.Precision.HIGHEST
MESH = pl.DeviceIdType.MESH
ANY = pl.BlockSpec(memory_space=pl.ANY)

N_DEV = 8
D_MODEL = 1024
SEQ = 2048
N_META = 16
BLK = 128
PAD = 112
T = PAD + N_META + SEQ
NB = T // BLK
EPS = 1e-6
HEAD = 64
Q_HEADS = 16
KV_HEADS = 4
GROUP = 4
KV_W = 256
SSM_INNER = 2048
SSM_HEADS = 32
SSM_GROUPS = 4
GRP_W = 512
SSM_STATE = 128
CONV_DIM = 3072
IN_PROJ = 9760
SHARD_IN = IN_PROJ // N_DEV
NEG = -1e30

C_Q, C_ZA, C_GA, C_GS, C_ZS, C_XBC, C_K, C_V, C_DT = 0, 1024, 2048, 3072, 4096, 6144, 9216, 9472, 9728
PW = 9856
R_Q, R_K, R_V, R_ZA, R_ZS, R_XBC, R_DT, R_GA, R_GS = 0, 1024, 1280, 1536, 2560, 4608, 7680, 7712, 8736

ADAM_LR, ADAM_B1, ADAM_B2, ADAM_EPS, ADAM_WD, ADAM_STEP = 0.001, 0.9, 0.999, 1e-08, 0.01, 10

VMEM_LIMIT = 56 * 1024 * 1024


def _cparams():
    return pltpu.CompilerParams(vmem_limit_bytes=VMEM_LIMIT)


def _silu(x):
    return x * jax.nn.sigmoid(x)


def _dsilu(x):
    s = jax.nn.sigmoid(x)
    return s * (1.0 + x * (1.0 - s))


def _matmul(a, b, mode, out_dtype, tm, tn, tk, name):
    if mode == "nn":
        (m, k), n = a.shape, b.shape[1]
        a_spec = pl.BlockSpec((tm, tk), lambda i, j, kk: (i, kk))
        b_spec = pl.BlockSpec((tk, tn), lambda i, j, kk: (kk, j))
        dims = (((1,), (0,)), ((), ()))
    elif mode == "nt":
        (m, k), n = a.shape, b.shape[0]
        a_spec = pl.BlockSpec((tm, tk), lambda i, j, kk: (i, kk))
        b_spec = pl.BlockSpec((tn, tk), lambda i, j, kk: (j, kk))
        dims = (((1,), (1,)), ((), ()))
    else:
        (k, m), n = a.shape, b.shape[1]
        a_spec = pl.BlockSpec((tk, tm), lambda i, j, kk: (kk, i))
        b_spec = pl.BlockSpec((tk, tn), lambda i, j, kk: (kk, j))
        dims = (((0,), (0,)), ((), ()))
    assert m % tm == 0 and n % tn == 0 and k % tk == 0, (a.shape, b.shape, tm, tn, tk)
    nk = k // tk

    def body_one(a_ref, b_ref, o_ref):
        o_ref[...] = lax.dot_general(a_ref[...], b_ref[...], dims, preferred_element_type=F32).astype(out_dtype)

    def body_acc(a_ref, b_ref, o_ref, acc_ref):
        kk = pl.program_id(2)
        part = lax.dot_general(a_ref[...], b_ref[...], dims, preferred_element_type=F32)

        @pl.when(kk == 0)
        def _():
            acc_ref[...] = part

        @pl.when((kk > 0) & (kk < nk - 1))
        def _():
            acc_ref[...] += part

        @pl.when(kk == nk - 1)
        def _():
            o_ref[...] = (acc_ref[...] + part).astype(out_dtype)

    return pl.pallas_call(
        body_one if nk == 1 else body_acc, grid=(m // tm, n // tn, nk), in_specs=[a_spec, b_spec],
        out_specs=pl.BlockSpec((tm, tn), lambda i, j, kk: (i, j)),
        out_shape=SDS((m, n), out_dtype), scratch_shapes=[] if nk == 1 else [pltpu.VMEM((tm, tn), F32)],
        compiler_params=_cparams(), name=name)(a, b)


def _norm_u(h, g_pre):
    def body(h_ref, g_ref, u_ref):
        x = h_ref[...]
        r = lax.rsqrt(jnp.mean(x * x, axis=-1, keepdims=True) + EPS)
        u_ref[...] = (x * r * g_ref[...]).astype(BF16)

    return pl.pallas_call(
        body, grid=(NB,),
        in_specs=[pl.BlockSpec((BLK, D_MODEL), lambda i: (i, 0)), pl.BlockSpec((1, D_MODEL), lambda i: (0, 0))],
        out_specs=pl.BlockSpec((BLK, D_MODEL), lambda i: (i, 0)),
        out_shape=SDS((T, D_MODEL), BF16), name="norm_u")(h, g_pre)


def _norm_bwd(h, g_pre, du, dres):
    def body(h_ref, g_ref, du_ref, dres_ref, dh_ref, dg_ref):
        i = pl.program_id(0)
        x = h_ref[...]
        g = g_ref[...]
        du_ = du_ref[...]
        r = lax.rsqrt(jnp.mean(x * x, axis=-1, keepdims=True) + EPS)
        gd = g * du_
        dx = r * gd - x * (r * r * r) * jnp.mean(x * gd, axis=-1, keepdims=True)
        dh_ref[...] = dx + dres_ref[...]
        part = jnp.sum(du_ * x * r, axis=0, keepdims=True)

        @pl.when(i == 0)
        def _():
            dg_ref[...] = jnp.zeros_like(dg_ref)

        dg_ref[0:1, :] += part

    row = pl.BlockSpec((BLK, D_MODEL), lambda i: (i, 0))
    return pl.pallas_call(
        body, grid=(NB,),
        in_specs=[row, pl.BlockSpec((1, D_MODEL), lambda i: (0, 0)), row, row],
        out_specs=[row, pl.BlockSpec((8, D_MODEL), lambda i: (0, 0))],
        out_shape=[SDS((T, D_MODEL), F32), SDS((8, D_MODEL), F32)], name="norm_bwd")(h, g_pre, du, dres)


def _lane_pick(row, h):
    lane = lax.broadcasted_iota(jnp.int32, row.shape, 1)
    return jnp.sum(jnp.where(lane == h, row, 0.0), axis=1, keepdims=True)


def _attn_fn(q4s, kcats, vcats, kms, vms, sinks, n):
    r = lax.broadcasted_iota(jnp.int32, (GROUP * BLK, 2 * BLK), 0)
    s = lax.broadcasted_iota(jnp.int32, (GROUP * BLK, 2 * BLK), 1)
    i = jnp.bitwise_and(r, BLK - 1)
    gi = jnp.right_shift(r, 7)
    rel = i - s + BLK
    k_pos = n * BLK - BLK + s
    band_ok = (rel >= 0) & (rel < BLK) & (k_pos >= PAD + N_META)
    relf = rel.astype(F32)
    rm = lax.broadcasted_iota(jnp.int32, (GROUP * BLK, N_META), 0)
    mm = lax.broadcasted_iota(jnp.int32, (GROUP * BLK, N_META), 1)
    meta_ok = (PAD + mm) <= (n * BLK + jnp.bitwise_and(rm, BLK - 1))
    gcol = jnp.right_shift(lax.broadcasted_iota(jnp.int32, (GROUP * BLK, 1), 0), 7)
    outs = []
    for kh in range(KV_HEADS):
        slopes = [2.0 ** (-8.0 * (kh * GROUP + g + 1) / Q_HEADS) for g in range(GROUP)]
        slope = jnp.where(gi == 0, slopes[0], jnp.where(gi == 1, slopes[1], jnp.where(gi == 2, slopes[2], slopes[3])))
        sk = [_lane_pick(sinks, kh * GROUP + g) for g in range(GROUP)]
        sink = jnp.where(gcol == 0, sk[0], jnp.where(gcol == 1, sk[1], jnp.where(gcol == 2, sk[2], sk[3])))
        qb = (q4s[kh] * (HEAD ** -0.5)).astype(BF16)
        sb = lax.dot_general(qb, kcats[kh].astype(BF16), (((1,), (1,)), ((), ())), preferred_element_type=F32)
        sb = jnp.where(band_ok, sb - slope * relf, NEG)
        sm = lax.dot_general(qb, kms[kh].astype(BF16), (((1,), (1,)), ((), ())), preferred_element_type=F32)
        sm = jnp.where(meta_ok, sm, NEG)
        mx = jnp.maximum(jnp.maximum(jnp.max(sb, axis=1, keepdims=True), jnp.max(sm, axis=1, keepdims=True)), sink)
        mx = lax.stop_gradient(mx)
        eb = jnp.exp(sb - mx)
        em = jnp.exp(sm - mx)
        es = jnp.exp(sink - mx)
        inv = 1.0 / (jnp.sum(eb, axis=1, keepdims=True) + jnp.sum(em, axis=1, keepdims=True) + es)
        pb = (eb * inv).astype(BF16)
        pm = (em * inv).astype(BF16)
        o4 = (jnp.dot(pm, vms[kh].astype(BF16), preferred_element_type=F32)
              + jnp.dot(pb, vcats[kh].astype(BF16), preferred_element_type=F32))
        outs.append(o4)
    return outs


def _attn_specs():
    prev = lambda n: jnp.maximum(n - 1, 0)
    return [
        pl.BlockSpec((BLK, D_MODEL), lambda n: (n, C_Q // D_MODEL)),
        pl.BlockSpec((BLK, KV_W), lambda n: (prev(n), C_K // KV_W)),
        pl.BlockSpec((BLK, KV_W), lambda n: (n, C_K // KV_W)),
        pl.BlockSpec((BLK, KV_W), lambda n: (prev(n), C_V // KV_W)),
        pl.BlockSpec((BLK, KV_W), lambda n: (n, C_V // KV_W)),
        pl.BlockSpec((N_META, KV_W), lambda n: (PAD // N_META, C_K // KV_W)),
        pl.BlockSpec((N_META, KV_W), lambda n: (PAD // N_META, C_V // KV_W)),
        pl.BlockSpec((1, 128), lambda n: (0, 0)),
    ]


def _attn_load(q_ref, kp_ref, kc_ref, vp_ref, vc_ref, km_ref, vm_ref):
    q4s, kcats, vcats, kms, vms = [], [], [], [], []
    for kh in range(KV_HEADS):
        q4s.append(jnp.concatenate(
            [q_ref[:, (kh * GROUP + g) * HEAD:(kh * GROUP + g + 1) * HEAD] for g in range(GROUP)], axis=0))
        cs = slice(kh * HEAD, (kh + 1) * HEAD)
        kcats.append(jnp.concatenate([kp_ref[:, cs], kc_ref[:, cs]], axis=0))
        vcats.append(jnp.concatenate([vp_ref[:, cs], vc_ref[:, cs]], axis=0))
        kms.append(km_ref[:, cs])
        vms.append(vm_ref[:, cs])
    return q4s, kcats, vcats, kms, vms


def _attn_fwd(proj, sinks):
    def body(q_ref, kp_ref, kc_ref, vp_ref, vc_ref, km_ref, vm_ref, s_ref, o_ref):
        n = pl.program_id(0)
        args = _attn_load(q_ref, kp_ref, kc_ref, vp_ref, vc_ref, km_ref, vm_ref)
        outs = _attn_fn(*args, s_ref[...], n)
        for kh in range(KV_HEADS):
            for g in range(GROUP):
                hh = kh * GROUP + g
                o_ref[:, hh * HEAD:(hh + 1) * HEAD] = outs[kh][g * BLK:(g + 1) * BLK]

    return pl.pallas_call(
        body, grid=(NB,), in_specs=_attn_specs(),
        out_specs=pl.BlockSpec((BLK, D_MODEL), lambda n: (n, 0)),
        out_shape=SDS((T, D_MODEL), F32), name="attn_fwd")(proj, proj, proj, proj, proj, proj, proj, sinks)


def _attn_bwd(proj, sinks, do):
    def body(q_ref, kp_ref, kc_ref, vp_ref, vc_ref, km_ref, vm_ref, s_ref, do_ref, dq_ref, dk_ref, dv_ref, ds_ref):
        n = pl.program_id(0)

        @pl.when(n == 0)
        def _():
            dk_ref[...] = jnp.zeros_like(dk_ref)
            dv_ref[...] = jnp.zeros_like(dv_ref)
            ds_ref[...] = jnp.zeros_like(ds_ref)

        args = _attn_load(q_ref, kp_ref, kc_ref, vp_ref, vc_ref, km_ref, vm_ref)
        _, vjp = jax.vjp(lambda a, b, c, d, e, f: _attn_fn(a, b, c, d, e, f, n), *args, s_ref[...])
        do_f = do_ref[...].astype(F32)
        cot = [jnp.concatenate([do_f[:, (kh * GROUP + g) * HEAD:(kh * GROUP + g + 1) * HEAD] for g in range(GROUP)],
                               axis=0) for kh in range(KV_HEADS)]
        dq4s, dkcats, dvcats, dkms, dvms, dsk = vjp(cot)
        ds_ref[0:1, :] += dsk
        cur = pl.ds(pl.multiple_of(n * BLK, BLK), BLK)
        meta = slice(PAD, PAD + N_META)
        for kh in range(KV_HEADS):
            cs = slice(kh * HEAD, (kh + 1) * HEAD)
            for g in range(GROUP):
                hh = kh * GROUP + g
                dq_ref[:, hh * HEAD:(hh + 1) * HEAD] = dq4s[kh][g * BLK:(g + 1) * BLK]
            dk_ref[cur, cs] += dkcats[kh][BLK:]
            dv_ref[cur, cs] += dvcats[kh][BLK:]
            dk_ref[meta, cs] += dkms[kh]
            dv_ref[meta, cs] += dvms[kh]

        @pl.when(n > 0)
        def _():
            prv = pl.ds(pl.multiple_of((n - 1) * BLK, BLK), BLK)
            for kh in range(KV_HEADS):
                cs = slice(kh * HEAD, (kh + 1) * HEAD)
                dk_ref[prv, cs] += dkcats[kh][:BLK]
                dv_ref[prv, cs] += dvcats[kh][:BLK]

    full_kv = pl.BlockSpec((T, KV_W), lambda n: (0, 0))
    return pl.pallas_call(
        body, grid=(NB,),
        in_specs=_attn_specs() + [pl.BlockSpec((BLK, D_MODEL), lambda n: (n, 0))],
        out_specs=[pl.BlockSpec((BLK, D_MODEL), lambda n: (n, 0)), full_kv, full_kv,
                   pl.BlockSpec((8, 128), lambda n: (0, 0))],
        out_shape=[SDS((T, D_MODEL), F32), SDS((T, KV_W), F32), SDS((T, KV_W), F32), SDS((8, 128), F32)],
        name="attn_bwd")(proj, proj, proj, proj, proj, proj, proj, sinks, do)


def _conv_taps(xp, w, rows):
    return (w[0:1] * xp[5:5 + rows] + w[1:2] * xp[6:6 + rows] + w[2:3] * xp[7:7 + rows] + w[3:4] * xp[8:8 + rows])


def _conv_fwd(proj, conv_w, conv_b):
    CONV_CB = CONV_DIM
    ncb = CONV_DIM // CONV_CB
    cb0 = C_XBC // CONV_CB

    def body(tail_ref, cur_ref, w_ref, b_ref, o_ref):
        n = pl.program_id(1)
        tail = jnp.where(n > 0, tail_ref[...], 0.0)
        xp = jnp.concatenate([tail, cur_ref[...]], axis=0)
        conv = _conv_taps(xp, w_ref[...], BLK) + b_ref[...]
        row = n * BLK + lax.broadcasted_iota(jnp.int32, (BLK, 1), 0)
        o_ref[...] = jnp.where(row >= PAD, _silu(conv), 0.0)

    return pl.pallas_call(
        body, grid=(ncb, NB),
        in_specs=[pl.BlockSpec((8, CONV_CB), lambda j, n: (jnp.maximum(n * (BLK // 8) - 1, 0), cb0 + j)),
                  pl.BlockSpec((BLK, CONV_CB), lambda j, n: (n, cb0 + j)),
                  pl.BlockSpec((8, CONV_CB), lambda j, n: (0, j)),
                  pl.BlockSpec((1, CONV_CB), lambda j, n: (0, j))],
        out_specs=pl.BlockSpec((BLK, CONV_CB), lambda j, n: (n, j)),
        out_shape=SDS((T, CONV_DIM), F32), name="conv_fwd")(proj, proj, conv_w, conv_b)


def _conv_bwd(proj, conv_w, conv_b, dact, ch0, name):
    width = dact.shape[1]
    CONV_CB = width
    ncb = width // CONV_CB
    cb0 = (C_XBC + ch0) // CONV_CB
    wb0 = ch0 // CONV_CB
    last8 = T // 8 - 1

    def body(tail_ref, cur_ref, nxt_ref, w_ref, b_ref, dcur_ref, dnxt_ref, dx_ref, dw_ref, db_ref):
        n = pl.program_id(1)
        w = w_ref[...]
        tail = jnp.where(n > 0, tail_ref[...], 0.0)
        xp = jnp.concatenate([tail, cur_ref[...], nxt_ref[...]], axis=0)
        conv = _conv_taps(xp, w, BLK + 8) + b_ref[...]
        dext = jnp.concatenate([dcur_ref[...], jnp.where(n < NB - 1, dnxt_ref[...], 0.0)], axis=0)
        row = n * BLK + lax.broadcasted_iota(jnp.int32, (BLK + 8, 1), 0)
        dconv = jnp.where(row >= PAD, dext * _dsilu(conv), 0.0)
        dx = (w[0:1] * dconv[3:3 + BLK] + w[1:2] * dconv[2:2 + BLK] + w[2:3] * dconv[1:1 + BLK]
              + w[3:4] * dconv[0:BLK])
        dx_ref[...] = dx.astype(BF16)
        dc = dconv[0:BLK]
        dws = [jnp.sum(dc * xp[5 + k:5 + k + BLK], axis=0, keepdims=True) for k in range(4)]
        dwp = jnp.concatenate(dws + [jnp.zeros((4, CONV_CB), F32)], axis=0)
        dbp = jnp.sum(dc, axis=0, keepdims=True)

        @pl.when(n == 0)
        def _():
            dw_ref[...] = dwp
            db_ref[...] = jnp.concatenate([dbp, jnp.zeros((7, CONV_CB), F32)], axis=0)

        @pl.when(n > 0)
        def _():
            dw_ref[...] += dwp
            db_ref[0:1, :] += dbp

    return pl.pallas_call(
        body, grid=(ncb, NB),
        in_specs=[pl.BlockSpec((8, CONV_CB), lambda j, n: (jnp.maximum(n * (BLK // 8) - 1, 0), cb0 + j)),
                  pl.BlockSpec((BLK, CONV_CB), lambda j, n: (n, cb0 + j)),
                  pl.BlockSpec((8, CONV_CB), lambda j, n: (jnp.minimum((n + 1) * (BLK // 8), last8), cb0 + j)),
                  pl.BlockSpec((8, CONV_CB), lambda j, n: (0, wb0 + j)),
                  pl.BlockSpec((1, CONV_CB), lambda j, n: (0, wb0 + j)),
                  pl.BlockSpec((BLK, CONV_CB), lambda j, n: (n, j)),
                  pl.BlockSpec((8, CONV_CB), lambda j, n: (jnp.minimum((n + 1) * (BLK // 8), last8), j))],
        out_specs=[pl.BlockSpec((BLK, CONV_CB), lambda j, n: (n, j)),
                   pl.BlockSpec((8, CONV_CB), lambda j, n: (0, j)),
                   pl.BlockSpec((8, CONV_CB), lambda j, n: (0, j))],
        out_shape=[SDS((T, width), BF16), SDS((8, width), F32), SDS((8, width), F32)],
        name=name)(proj, proj, proj, conv_w, conv_b, dact, dact)


HPG = SSM_HEADS // SSM_GROUPS


def _iota(shape, dim):
    return lax.broadcasted_iota(jnp.int32, shape, dim)


def _mm(a, b, ca=1, cb=0):
    return lax.dot_general(a.astype(BF16), b.astype(BF16), (((ca,), (cb,)), ((), ())), preferred_element_type=F32)


def _split3(v):
    hi = v.astype(BF16)
    r1 = v - hi.astype(F32)
    mid = r1.astype(BF16)
    lo = (r1 - mid.astype(F32)).astype(BF16)
    return hi, mid, lo


def _sel_r(parts, onehot, ca=1, cb=0):
    out = lax.dot_general(parts[0], onehot, (((ca,), (cb,)), ((), ())), preferred_element_type=F32)
    for p in parts[1:]:
        out = out + lax.dot_general(p, onehot, (((ca,), (cb,)), ((), ())), preferred_element_type=F32)
    return out


def _sel_l(onehot, parts):
    out = jnp.dot(onehot, parts[0], preferred_element_type=F32)
    for p in parts[1:]:
        out = out + jnp.dot(onehot, p, preferred_element_type=F32)
    return out


def _rows8(*rows):
    r = _iota((8, rows[0].shape[1]), 0)
    out = jnp.zeros((8, rows[0].shape[1]), F32)
    for k, v in enumerate(rows):
        out = jnp.where(r == k, v, out)
    return out


def _ssd_forward(x, z, bm, cm, dt_raw, st_prev, dtb, alog, dskip, gn, g, cst_scr):
    li, si = _iota((BLK, BLK), 0), _iota((BLK, BLK), 1)
    dt_all = jax.nn.softplus(dt_raw + dtb)
    a_row = -jnp.exp(alog)
    a_all = dt_all * a_row
    cs_all = _sel_l((li >= si).astype(BF16), _split3(a_all))
    cs_parts = _split3(cs_all)
    spread = (_iota((BLK, GRP_W), 0) == g * HPG + jnp.right_shift(_iota((BLK, GRP_W), 1), 6)).astype(BF16)
    dt_e = _sel_r(_split3(dt_all), spread)
    cs_e = _sel_r(cs_parts, spread)
    d_e = _sel_r(_split3(_rows8(dskip)), spread)[0:1]
    cs_last_e = jnp.sum(jnp.where(_iota((BLK, GRP_W), 0) == BLK - 1, cs_e, 0.0), axis=0, keepdims=True)
    p_e = jnp.exp(cs_e)
    w_e = jnp.exp(cs_last_e - cs_e)
    cd_e = jnp.exp(cs_last_e)
    xr = x * dt_e
    cst_scr[...] = cs_all.T
    cst_g = cst_scr[pl.ds(pl.multiple_of(g * HPG, HPG), HPG), :]
    own = jnp.right_shift(_iota((HPG, HPG * BLK), 1), 7) == _iota((HPG, HPG * BLK), 0)
    ownf = own.astype(F32)
    q_rows = [ownf, ownf, ownf] + [jnp.where(own, jnp.concatenate([p.astype(F32)] * HPG, axis=1), 0.0)
                                   for p in _split3(cst_g)]
    q2 = jnp.concatenate(q_rows + [jnp.zeros((BLK - 6 * HPG, HPG * BLK), F32)], axis=0).astype(BF16)
    lane1 = _iota((1, BLK), 1)
    p2 = jnp.where((lane1 >= 3 * HPG) & (lane1 < 6 * HPG), -1.0, 0.0)
    for k, part in enumerate(cs_parts):
        pick = ((li == g * HPG + si - k * HPG) & (si >= k * HPG) & (si < (k + 1) * HPG)).astype(BF16)
        p2 = p2 + jnp.dot(part, pick, preferred_element_type=F32)
    dmat = jnp.dot(p2.astype(BF16), q2, preferred_element_type=F32)
    causal = _iota((BLK, HPG * BLK), 0) >= jnp.bitwise_and(_iota((BLK, HPG * BLK), 1), BLK - 1)
    lam = jnp.exp(jnp.where(causal, dmat, NEG))
    gmat = _mm(cm, bm, 1, 1)
    m_all = lam * jnp.concatenate([gmat] * HPG, axis=1)
    mb = m_all.astype(BF16)
    lo = _iota((BLK, BLK), 1) < HEAD
    xrb = xr.astype(BF16)
    zero = jnp.zeros((BLK, BLK), BF16)
    bds, yd = [], []
    for i in range(HPG // 2):
        t = xrb[:, BLK * i:BLK * (i + 1)]
        bd = jnp.concatenate([jnp.where(lo, t, zero), jnp.where(lo, zero, t)], axis=0)
        bds.append(bd)
        yd.append(jnp.dot(mb[:, 2 * BLK * i:2 * BLK * (i + 1)], bd, preferred_element_type=F32))
    cs_st = _mm(cm, st_prev)
    y = jnp.concatenate(yd, axis=1) + cs_st * p_e + d_e * x
    xrw = xr * w_e
    st_new = cd_e * st_prev + _mm(bm, xrw, 0, 0)
    yz = y * _silu(z)
    rn = lax.rsqrt(jnp.sum(yz * yz, axis=1, keepdims=True) / GRP_W + EPS)
    return dict(out=yz * rn * gn, st_new=st_new, dt_all=dt_all, a_row=a_row, dt_e=dt_e, d_e=d_e, p_e=p_e, w_e=w_e,
                cd_e=cd_e, xr=xr, xrw=xrw, lam=lam, m_all=m_all, mb=mb, bds=bds, cs_st=cs_st, y=y, yz=yz, rn=rn, lo=lo)


def _ssd_backward(f, x, z, bm, cm, dt_raw, st_prev, dtb, gn, g, dout, dst_next, cst_scr):
    li, si = _iota((BLK, BLK), 0), _iota((BLK, BLK), 1)
    yz, rn, y, p_e, w_e, cd_e, xr = f["yz"], f["rn"], f["y"], f["p_e"], f["w_e"], f["cd_e"], f["xr"]
    dgn = jnp.sum(dout * yz * rn, axis=0, keepdims=True)
    t = dout * gn
    dyz = rn * t - yz * (rn * rn * rn) * (jnp.sum(yz * t, axis=1, keepdims=True) / GRP_W)
    dy = dyz * _silu(z)
    dz = dyz * y * _dsilu(z)
    dx = f["d_e"] * dy
    dd_e = jnp.sum(dy * x, axis=0, keepdims=True)
    dcsst = dy * p_e
    dp_e = dy * f["cs_st"]
    dcm = _mm(dcsst, st_prev, 1, 1)
    dst_prev = _mm(cm, dcsst, 0, 0) + cd_e * dst_next
    dcd_e = jnp.sum(dst_next * st_prev, axis=0, keepdims=True)
    dbm = _mm(f["xrw"], dst_next, 1, 1)
    dxrw = _mm(bm, dst_next)
    dxr = dxrw * w_e
    dw_e = dxrw * xr
    dyb = dy.astype(BF16)
    dms, dxr_d = [], []
    for i in range(HPG // 2):
        dyp = dyb[:, BLK * i:BLK * (i + 1)]
        dms.append(lax.dot_general(dyp, f["bds"][i], (((1,), (1,)), ((), ())), preferred_element_type=F32))
        r = lax.dot_general(f["mb"][:, 2 * BLK * i:2 * BLK * (i + 1)], dyp, (((0,), (0,)), ((), ())),
                            preferred_element_type=F32)
        dxr_d.append(jnp.where(f["lo"], r[0:BLK], r[BLK:2 * BLK]))
    dm_all = jnp.concatenate(dms, axis=1)
    dxr = dxr + jnp.concatenate(dxr_d, axis=1)
    dlg = dm_all * f["lam"]
    dg = dlg[:, 0:BLK]
    for j in range(1, HPG):
        dg = dg + dlg[:, BLK * j:BLK * (j + 1)]
    dcm = dcm + _mm(dg, bm)
    dbm = dbm + _mm(dg, cm, 0, 0)
    q_all = dm_all * f["m_all"]
    col_sums = jnp.sum(q_all, axis=0, keepdims=True)
    cst_scr[...] = jnp.zeros_like(cst_scr)
    cst_scr[pl.ds(pl.multiple_of(g * HPG, HPG), HPG), :] = _rows8(
        *[col_sums[:, BLK * j:BLK * (j + 1)] for j in range(HPG)])
    dcs = -cst_scr[...].T
    for j in range(HPG):
        dcs = dcs + jnp.where(si == g * HPG + j,
                              jnp.sum(q_all[:, BLK * j:BLK * (j + 1)], axis=1, keepdims=True), 0.0)
    unspread = (_iota((GRP_W, BLK), 1) == g * HPG + jnp.right_shift(_iota((GRP_W, BLK), 0), 6)).astype(BF16)
    dww = dw_e * w_e
    per_head = _sel_r(_split3(jnp.concatenate([dp_e * p_e - dww, dxr * x], axis=0)), unspread)
    last = _sel_r(_split3(_rows8(jnp.sum(dww, axis=0, keepdims=True) + dcd_e * cd_e, dd_e)), unspread)
    dcs = dcs + per_head[0:BLK] + jnp.where(li == BLK - 1, last[0:1], 0.0)
    da = _sel_l((si >= li).astype(BF16), _split3(dcs))
    ddt_all = da * f["a_row"] + per_head[BLK:2 * BLK]
    dalog = jnp.sum(da * f["dt_all"], axis=0, keepdims=True) * f["a_row"]
    dx = dx + dxr * f["dt_e"]
    ddt_raw = ddt_all * jax.nn.sigmoid(dt_raw + dtb)
    ddtb = jnp.sum(ddt_raw, axis=0, keepdims=True)
    ddskip = last[1:2]
    return dict(dx=dx, dz=dz, dbm=dbm, dcm=dcm, ddt_raw=ddt_raw, dst_prev=dst_prev, ddtb=ddtb, dalog=dalog,
                ddskip=ddskip, dgn=dgn)


def _ssd_in_specs(rev):
    cidx = (lambda c: NB - 1 - c) if rev else (lambda c: c)
    return [
        pl.BlockSpec((BLK, GRP_W), lambda g, c: (cidx(c), g)),
        pl.BlockSpec((BLK, SSM_STATE), lambda g, c: (cidx(c), SSM_INNER // SSM_STATE + g)),
        pl.BlockSpec((BLK, SSM_STATE), lambda g, c: (cidx(c), SSM_INNER // SSM_STATE + SSM_GROUPS + g)),
        pl.BlockSpec((BLK, 128), lambda g, c: (cidx(c), C_DT // 128)),
        pl.BlockSpec((BLK, GRP_W), lambda g, c: (cidx(c), C_ZS // GRP_W + g)),
        pl.BlockSpec((1, 128), lambda g, c: (0, 0)),
        pl.BlockSpec((1, 128), lambda g, c: (0, 0)),
        pl.BlockSpec((1, 128), lambda g, c: (0, 0)),
        pl.BlockSpec((1, GRP_W), lambda g, c: (0, g)),
    ]


def _ssd_fwd(xbc_act, proj, dt_bias, a_log, d_skip, g_norm):
    def body(xs_ref, b_ref, c_ref, dt_ref, z_ref, dtb_ref, al_ref, dsk_ref, gn_ref, y_ref, st_ref, s_scr, cst_scr):
        g = pl.program_id(0)
        c = pl.program_id(1)

        @pl.when(c == 0)
        def _():
            s_scr[...] = jnp.zeros_like(s_scr)

        st_prev = s_scr[...]
        st_ref[0, 0] = st_prev
        f = _ssd_forward(xs_ref[...], z_ref[...], b_ref[...], c_ref[...], dt_ref[...], st_prev, dtb_ref[...],
                         al_ref[...], dsk_ref[...], gn_ref[...], g, cst_scr)
        y_ref[...] = f["out"].astype(BF16)
        s_scr[...] = f["st_new"]

    return pl.pallas_call(
        body, grid=(SSM_GROUPS, NB), in_specs=_ssd_in_specs(False),
        out_specs=[pl.BlockSpec((BLK, GRP_W), lambda g, c: (c, g)),
                   pl.BlockSpec((1, 1, SSM_STATE, GRP_W), lambda g, c: (g, c, 0, 0))],
        out_shape=[SDS((T, SSM_INNER), BF16), SDS((SSM_GROUPS, NB, SSM_STATE, GRP_W), F32)],
        scratch_shapes=[pltpu.VMEM((SSM_STATE, GRP_W), F32), pltpu.VMEM((BLK, BLK), F32)],
        compiler_params=_cparams(),
        name="ssd_fwd")(xbc_act, xbc_act, xbc_act, proj, proj, dt_bias, a_log, d_skip, g_norm)


def _ssd_bwd(xbc_act, proj, dt_bias, a_log, d_skip, g_norm, states, dy):
    def body(xs_ref, b_ref, c_ref, dt_ref, z_ref, dtb_ref, al_ref, dsk_ref, gn_ref, st_ref, dy_ref,
             dxs_ref, db_ref, dc_ref, ddt_ref, dz_ref, ddtb_ref, dal_ref, ddsk_ref, dgn_ref, ds_scr, cst_scr):
        g = pl.program_id(0)
        c = pl.program_id(1)

        @pl.when(c == 0)
        def _():
            ds_scr[...] = jnp.zeros_like(ds_scr)
            dgn_ref[...] = jnp.zeros_like(dgn_ref)

        @pl.when((c == 0) & (g == 0))
        def _():
            ddtb_ref[...] = jnp.zeros_like(ddtb_ref)
            dal_ref[...] = jnp.zeros_like(dal_ref)
            ddsk_ref[...] = jnp.zeros_like(ddsk_ref)

        x, z, bm, cm, dt_raw, st_prev = xs_ref[...], z_ref[...], b_ref[...], c_ref[...], dt_ref[...], st_ref[0, 0]
        f = _ssd_forward(x, z, bm, cm, dt_raw, st_prev, dtb_ref[...], al_ref[...], dsk_ref[...], gn_ref[...], g,
                         cst_scr)
        d = _ssd_backward(f, x, z, bm, cm, dt_raw, st_prev, dtb_ref[...], gn_ref[...], g, dy_ref[...].astype(F32),
                          ds_scr[...], cst_scr)
        dxs_ref[...] = d["dx"]
        dz_ref[...] = d["dz"].astype(BF16)
        ds_scr[...] = d["dst_prev"]
        db_ref[...] = d["dbm"]
        dc_ref[...] = d["dcm"]
        ddt_ref[...] = d["ddt_raw"]
        dgn_ref[0:1, :] += d["dgn"]
        ddtb_ref[0:1, :] += d["ddtb"]
        dal_ref[0:1, :] += d["dalog"]
        ddsk_ref[0:1, :] += d["ddskip"]

    rc = lambda c: NB - 1 - c
    small = pl.BlockSpec((8, 128), lambda g, c: (0, 0))
    return pl.pallas_call(
        body, grid=(SSM_GROUPS, NB),
        in_specs=_ssd_in_specs(True) + [
            pl.BlockSpec((1, 1, SSM_STATE, GRP_W), lambda g, c: (g, rc(c), 0, 0)),
            pl.BlockSpec((BLK, GRP_W), lambda g, c: (rc(c), g))],
        out_specs=[pl.BlockSpec((BLK, GRP_W), lambda g, c: (rc(c), g)),
                   pl.BlockSpec((BLK, SSM_STATE), lambda g, c: (rc(c), g)),
                   pl.BlockSpec((BLK, SSM_STATE), lambda g, c: (rc(c), g)),
                   pl.BlockSpec((BLK, 128), lambda g, c: (rc(c), g)),
                   pl.BlockSpec((BLK, GRP_W), lambda g, c: (rc(c), g)),
                   small, small, small,
                   pl.BlockSpec((8, GRP_W), lambda g, c: (0, g))],
        out_shape=[SDS((T, SSM_INNER), F32), SDS((T, GRP_W), F32), SDS((T, GRP_W), F32), SDS((T, GRP_W), F32),
                   SDS((T, SSM_INNER), BF16), SDS((8, 128), F32), SDS((8, 128), F32), SDS((8, 128), F32),
                   SDS((8, SSM_INNER), F32)],
        scratch_shapes=[pltpu.VMEM((SSM_STATE, GRP_W), F32), pltpu.VMEM((BLK, BLK), F32)],
        compiler_params=_cparams(),
        name="ssd_bwd")(xbc_act, xbc_act, xbc_act, proj, proj, dt_bias, a_log, d_skip, g_norm, states, dy)


POST_R = 272


def _post_a(o, proj, sn, w_att, w_ssm, w_o):
    def body(o_ref, za_ref, ga_ref, gs_ref, sn_ref, wa_ref, ws_ref, wo_ref, a_ref, mg_ref, ya_ref, ys_ref, out_ref):
        a = (o_ref[...] * _silu(za_ref[...])).astype(BF16)
        a_ref[...] = a
        ya = jnp.dot(a, wa_ref[...], preferred_element_type=F32)
        ys = jnp.dot(sn_ref[...], ws_ref[...], preferred_element_type=F32)
        ya_ref[...] = ya.astype(BF16)
        ys_ref[...] = ys.astype(BF16)
        mg = (jax.nn.sigmoid(ga_ref[...]) * ya + jax.nn.sigmoid(gs_ref[...]) * ys).astype(BF16)
        mg_ref[...] = mg
        out_ref[...] = jnp.dot(mg, wo_ref[...], preferred_element_type=F32)

    row = pl.BlockSpec((POST_R, D_MODEL), lambda i: (i, 0))
    pcol = lambda c0: pl.BlockSpec((POST_R, D_MODEL), lambda i: (i, c0 // D_MODEL))
    full = lambda r: pl.BlockSpec((r, D_MODEL), lambda i: (0, 0))
    return pl.pallas_call(
        body, grid=(T // POST_R,),
        in_specs=[row, pcol(C_ZA), pcol(C_GA), pcol(C_GS), pl.BlockSpec((POST_R, SSM_INNER), lambda i: (i, 0)),
                  full(D_MODEL), full(SSM_INNER), full(D_MODEL)],
        out_specs=[row, row, row, row, row],
        out_shape=[SDS((T, D_MODEL), BF16), SDS((T, D_MODEL), BF16), SDS((T, D_MODEL), BF16), SDS((T, D_MODEL), BF16),
                   SDS((T, D_MODEL), F32)],
        compiler_params=_cparams(), name="post_a")(o, proj, proj, proj, sn, w_att, w_ssm, w_o)


def _post_b(out, h, tgt, proj, ya, ys, o, g_post, w_att, w_ssm, w_o):
    def body(out_ref, h_ref, t_ref, za_ref, ga_ref, gs_ref, ya_ref, ys_ref, o_ref, gp_ref, wa_ref, ws_ref, wo_ref,
             loss_ref, dres_ref, dout_ref, dya_ref, dys_ref, dga_ref, dgs_ref, do_ref, dza_ref, dsn_ref, dgp_ref):
        i = pl.program_id(0)
        x = out_ref[...]
        gp = gp_ref[...]
        r = lax.rsqrt(jnp.mean(x * x, axis=-1, keepdims=True) + EPS)
        row = i * POST_R + lax.broadcasted_iota(jnp.int32, (POST_R, 1), 0)
        res = h_ref[...] + jnp.where(row >= PAD, x * r * gp, 0.0)
        live = row >= PAD + N_META
        err = jnp.where(live, res - t_ref[...], 0.0)
        lpart = 0.5 * jnp.sum(jnp.sum(err * err, axis=1, keepdims=True) / D_MODEL, axis=0, keepdims=True)
        dres = err / D_MODEL
        dres_ref[...] = dres
        gpart = jnp.sum(dres * x * r, axis=0, keepdims=True)

        @pl.when(i == 0)
        def _():
            loss_ref[...] = jnp.zeros_like(loss_ref)
            dgp_ref[...] = jnp.zeros_like(dgp_ref)

        loss_ref[...] += jnp.broadcast_to(lpart, loss_ref.shape)
        dgp_ref[0:1, :] += gpart
        gd = gp * dres
        dout = (r * gd - x * (r * r * r) * jnp.mean(x * gd, axis=-1, keepdims=True)).astype(BF16)
        dout_ref[...] = dout
        dmg = lax.dot_general(dout, wo_ref[...], (((1,), (1,)), ((), ())), preferred_element_type=F32)
        sga = jax.nn.sigmoid(ga_ref[...])
        sgs = jax.nn.sigmoid(gs_ref[...])
        dya = (dmg * sga).astype(BF16)
        dys = (dmg * sgs).astype(BF16)
        dya_ref[...] = dya
        dys_ref[...] = dys
        dga_ref[...] = (dmg * ya_ref[...].astype(F32) * sga * (1.0 - sga)).astype(BF16)
        dgs_ref[...] = (dmg * ys_ref[...].astype(F32) * sgs * (1.0 - sgs)).astype(BF16)
        da = lax.dot_general(dya, wa_ref[...], (((1,), (1,)), ((), ())), preferred_element_type=F32)
        za = za_ref[...]
        do_ref[...] = (da * _silu(za)).astype(BF16)
        dza_ref[...] = (da * o_ref[...] * _dsilu(za)).astype(BF16)
        dsn_ref[...] = lax.dot_general(dys, ws_ref[...], (((1,), (1,)), ((), ())),
                                       preferred_element_type=F32).astype(BF16)

    row = pl.BlockSpec((POST_R, D_MODEL), lambda i: (i, 0))
    pcol = lambda c0: pl.BlockSpec((POST_R, D_MODEL), lambda i: (i, c0 // D_MODEL))
    full = lambda r: pl.BlockSpec((r, D_MODEL), lambda i: (0, 0))
    small = pl.BlockSpec((8, D_MODEL), lambda i: (0, 0))
    return pl.pallas_call(
        body, grid=(T // POST_R,),
        in_specs=[row, row, row, pcol(C_ZA), pcol(C_GA), pcol(C_GS), row, row, row,
                  pl.BlockSpec((1, D_MODEL), lambda i: (0, 0)), full(D_MODEL), full(SSM_INNER), full(D_MODEL)],
        out_specs=[pl.BlockSpec((8, 128), lambda i: (0, 0)), row, row, row, row, row, row, row, row,
                   pl.BlockSpec((POST_R, SSM_INNER), lambda i: (i, 0)), small],
        out_shape=[SDS((8, 128), F32), SDS((T, D_MODEL), F32), SDS((T, D_MODEL), BF16), SDS((T, D_MODEL), BF16),
                   SDS((T, D_MODEL), BF16), SDS((T, D_MODEL), BF16), SDS((T, D_MODEL), BF16), SDS((T, D_MODEL), BF16),
                   SDS((T, D_MODEL), BF16), SDS((T, SSM_INNER), BF16), SDS((8, D_MODEL), F32)],
        compiler_params=_cparams(), name="post_b")(out, h, tgt, proj, proj, proj, ya, ys, o, g_post, w_att, w_ssm, w_o)


def _assemble(dq, dza, dga, dgs, dzs, dxx, dxb, dxc, dk, dv, ddt4):
    def body(dq_ref, dza_ref, dga_ref, dgs_ref, dzs_ref, dxx_ref, dxb_ref, dxc_ref, dk_ref, dv_ref, ddt_ref, o_ref):
        o_ref[:, C_Q:C_Q + D_MODEL] = dq_ref[...].astype(BF16)
        o_ref[:, C_ZA:C_ZA + D_MODEL] = dza_ref[...]
        o_ref[:, C_GA:C_GA + D_MODEL] = dga_ref[...]
        o_ref[:, C_GS:C_GS + D_MODEL] = dgs_ref[...]
        o_ref[:, C_ZS:C_ZS + SSM_INNER] = dzs_ref[...]
        o_ref[:, C_XBC:C_XBC + SSM_INNER] = dxx_ref[...]
        o_ref[:, C_XBC + SSM_INNER:C_XBC + SSM_INNER + GRP_W] = dxb_ref[...]
        o_ref[:, C_XBC + SSM_INNER + GRP_W:C_XBC + CONV_DIM] = dxc_ref[...]
        o_ref[:, C_K:C_K + KV_W] = dk_ref[...].astype(BF16)
        o_ref[:, C_V:C_V + KV_W] = dv_ref[...].astype(BF16)
        d4 = ddt_ref[...]
        o_ref[:, C_DT:C_DT + 128] = (d4[:, 0:128] + d4[:, 128:256] + d4[:, 256:384] + d4[:, 384:512]).astype(BF16)

    spec = lambda w: pl.BlockSpec((BLK, w), lambda i: (i, 0))
    ins = [dq, dza, dga, dgs, dzs, dxx, dxb, dxc, dk, dv, ddt4]
    return pl.pallas_call(
        body, grid=(NB,), in_specs=[spec(a.shape[1]) for a in ins], out_specs=spec(PW),
        out_shape=SDS((T, PW), BF16), name="assemble")(*ins)


def _adamw_math(w, g, m, v):
    m = ADAM_B1 * m + (1.0 - ADAM_B1) * g
    v = ADAM_B2 * v + (1.0 - ADAM_B2) * (g * g)
    m_hat = m / (1.0 - ADAM_B1 ** ADAM_STEP)
    v_hat = v / (1.0 - ADAM_B2 ** ADAM_STEP)
    delta = -ADAM_LR * (m_hat / (jnp.sqrt(v_hat) + ADAM_EPS) + ADAM_WD * w)
    return delta, m, v


def _sum_adamw(recv, w, m, v, tc, name):
    rows, cols = w.shape
    assert cols % tc == 0

    def body(r_ref, w_ref, m_ref, v_ref, g_ref, d_ref, nm_ref, nv_ref):
        g = r_ref[0].astype(F32)
        for d in range(1, N_CHIP):
            g = g + r_ref[d].astype(F32)
        g_ref[...] = g
        delta, nm, nv = _adamw_math(w_ref[...], g, m_ref[...], v_ref[...])
        d_ref[...] = delta
        nm_ref[...] = nm
        nv_ref[...] = nv

    blk = pl.BlockSpec((rows, tc), lambda i: (0, i))
    return pl.pallas_call(
        body, grid=(cols // tc,),
        in_specs=[pl.BlockSpec((N_CHIP, rows, tc), lambda i: (0, 0, i)), blk, blk, blk],
        out_specs=[blk, blk, blk, blk], out_shape=[SDS((rows, cols), F32)] * 4,
        compiler_params=_cparams(), name=name)(recv, w, m, v)


def _sum_adamw_rows3(recv, w3, m3, v3, name):
    pairs = 61
    assert (SHARD_IN // 2) % pairs == 0

    def body(r_ref, w_ref, m_ref, v_ref, g_ref, d_ref, nm_ref, nv_ref):
        g = r_ref[0].astype(F32)
        for d in range(1, N_CHIP):
            g = g + r_ref[d].astype(F32)
        g = g.reshape(2 * pairs, ROW_TILES, 128)
        g_ref[...] = g
        delta, nm, nv = _adamw_math(w_ref[...], g, m_ref[...], v_ref[...])
        d_ref[...] = delta
        nm_ref[...] = nm
        nv_ref[...] = nv

    blk = pl.BlockSpec((2 * pairs, ROW_TILES, 128), lambda i: (i, 0, 0))
    return pl.pallas_call(
        body, grid=(SHARD_IN // 2 // pairs,),
        in_specs=[pl.BlockSpec((N_CHIP, pairs, 2 * ROW_TILES, 128), lambda i: (0, i, 0, 0)), blk, blk, blk],
        out_specs=[blk, blk, blk, blk], out_shape=[SDS(w3.shape, F32)] * 4,
        compiler_params=_cparams(), name=name)(recv, w3, m3, v3)


ROW_GPRE, ROW_CONVB, ROW_DTB, ROW_ALOG, ROW_DSKIP, ROW_SINK, ROW_GSSM, ROW_GPOST = 0, 1, 4, 5, 6, 7, 8, 10
REP_ROWS, ROW_CONVW, ROW_META, SM_ROWS = 16, 16, 24, 40
CW_SHARD = CONV_DIM // N_DEV
META_SHARD = D_MODEL // N_DEV


def _small_pack(dgpre, dbx, dbb, dbc, ddtb, dal, ddsk, dsink, dgn, dgp, dwx, dwb, dwc, dh):
    def body(dgpre_ref, dbx_ref, dbb_ref, dbc_ref, ddtb_ref, dal_ref, ddsk_ref, dsink_ref, dgn_ref, dgp_ref,
             dwx_ref, dwb_ref, dwc_ref, dh_ref, o_ref, rep):
        rep[...] = jnp.zeros_like(rep)
        rep[ROW_GPRE:ROW_GPRE + 1, :] = dgpre_ref[0:1, :]
        rep[ROW_CONVB:ROW_CONVB + 1, :] = dbx_ref[0:1, 0:1024]
        rep[ROW_CONVB + 1:ROW_CONVB + 2, :] = dbx_ref[0:1, 1024:2048]
        rep[ROW_CONVB + 2:ROW_CONVB + 3, 0:512] = dbb_ref[0:1, :]
        rep[ROW_CONVB + 2:ROW_CONVB + 3, 512:1024] = dbc_ref[0:1, :]
        rep[ROW_DTB:ROW_DTB + 1, 0:128] = ddtb_ref[0:1, :]
        rep[ROW_ALOG:ROW_ALOG + 1, 0:128] = dal_ref[0:1, :]
        rep[ROW_DSKIP:ROW_DSKIP + 1, 0:128] = ddsk_ref[0:1, :]
        rep[ROW_SINK:ROW_SINK + 1, 0:128] = dsink_ref[0:1, :]
        rep[ROW_GSSM:ROW_GSSM + 1, :] = dgn_ref[0:1, 0:1024]
        rep[ROW_GSSM + 1:ROW_GSSM + 2, :] = dgn_ref[0:1, 1024:2048]
        rep[ROW_GPOST:ROW_GPOST + 1, :] = dgp_ref[0:1, :]
        cw = jnp.concatenate([dwx_ref[...], dwb_ref[...], dwc_ref[...]], axis=1)
        mh = dh_ref[...]
        o_ref[...] = jnp.zeros_like(o_ref)
        for p in range(N_DEV):
            o_ref[p, 0:REP_ROWS, :] = rep[...]
            o_ref[p, ROW_CONVW:ROW_CONVW + 8, 0:CW_SHARD] = cw[:, p * CW_SHARD:(p + 1) * CW_SHARD]
            o_ref[p, ROW_META:ROW_META + N_META, 0:META_SHARD] = mh[:, p * META_SHARD:(p + 1) * META_SHARD]

    ins = [dgpre, dbx, dbb, dbc, ddtb, dal, ddsk, dsink, dgn, dgp, dwx, dwb, dwc]
    return pl.pallas_call(
        body, grid=(1,),
        in_specs=[pl.BlockSpec(a.shape, lambda i: (0, 0)) for a in ins]
        + [pl.BlockSpec((N_META, D_MODEL), lambda i: (PAD // N_META, 0))],
        out_specs=pl.BlockSpec((N_DEV, SM_ROWS, 1024), lambda i: (0, 0, 0)),
        out_shape=SDS((N_DEV, SM_ROWS, 1024), F32), scratch_shapes=[pltpu.VMEM((REP_ROWS, 1024), F32)],
        name="small_pack")(*ins, dh)


def _small_finish(recv, params):
    npar = len(params)

    def body(*refs):
        r_ref = refs[0]
        wmv = refs[1:1 + 3 * npar]
        outs = refs[1 + 3 * npar:1 + 7 * npar]
        gs = refs[-1]
        g = r_ref[0]
        for d in range(1, N_CHIP):
            g = g + r_ref[d]
        gs[...] = g
        grads = [
            gs[ROW_GPRE:ROW_GPRE + 1, :],
            jnp.concatenate([gs[ROW_CONVB + k:ROW_CONVB + k + 1, :] for k in range(3)], axis=1),
            gs[ROW_DTB:ROW_DTB + 1, 0:SSM_HEADS], gs[ROW_ALOG:ROW_ALOG + 1, 0:SSM_HEADS],
            gs[ROW_DSKIP:ROW_DSKIP + 1, 0:SSM_HEADS], gs[ROW_SINK:ROW_SINK + 1, 0:Q_HEADS],
            jnp.concatenate([gs[ROW_GSSM:ROW_GSSM + 1, :], gs[ROW_GSSM + 1:ROW_GSSM + 2, :]], axis=1),
            gs[ROW_GPOST:ROW_GPOST + 1, :],
            gs[ROW_CONVW:ROW_CONVW + 4, 0:CW_SHARD],
            gs[ROW_META:ROW_META + N_META, 0:META_SHARD]]
        for i in range(npar):
            w_ref, m_ref, v_ref = wmv[3 * i:3 * i + 3]
            delta, nm, nv = _adamw_math(w_ref[...], grads[i], m_ref[...], v_ref[...])
            outs[4 * i][...] = grads[i]
            outs[4 * i + 1][...] = delta
            outs[4 * i + 2][...] = nm
            outs[4 * i + 3][...] = nv

    flat = [a for wmv in params for a in wmv]
    res = pl.pallas_call(
        body, out_shape=[SDS(wmv[0].shape, F32) for wmv in params for _ in range(4)],
        scratch_shapes=[pltpu.VMEM((SM_ROWS, 1024), F32)], name="small_finish")(recv, *flat)
    return [tuple(res[4 * i:4 * i + 4]) for i in range(npar)]


def _slab(ref, px, py, pc):
    return ref.at[4 * px + 2 * py + pc]


def _bounce(src, dst, buf, sem):
    cp = pltpu.make_async_copy(src, buf, sem)
    cp.start()
    cp.wait()
    cp = pltpu.make_async_copy(buf, dst, sem)
    cp.start()
    cp.wait()


def _all_gather(shards):
    na = len(shards)

    def body(*refs):
        ins, outs = refs[:na], refs[na:2 * na]
        send_sems, recv_sems, local_sems = refs[2 * na:2 * na + 3]
        bufs = refs[2 * na + 3:]
        x, y, c = lax.axis_index("x"), lax.axis_index("y"), lax.axis_index("c")
        me, sibling = (x, y, c), (x, y, 1 - c)
        chips = [(1 - x, y), (x, 1 - y), (1 - x, 1 - y)]

        def copy(a, k, block, to, src=None):
            dst = _slab(outs[a], *block)
            return pltpu.make_async_remote_copy(
                src_ref=dst if src is None else src, dst_ref=dst, send_sem=send_sems.at[a, k],
                recv_sem=recv_sems.at[a, k], device_id=to, device_id_type=MESH)

        first = []
        for a in range(na):
            first.append(copy(a, 0, me, sibling, src=ins[a]))
            first += [copy(a, 1 + j, me, (*chip, c), src=ins[a]) for j, chip in enumerate(chips)]
        for cp in first:
            cp.start()
        for a in range(na):
            _bounce(ins[a], _slab(outs[a], *me), bufs[a], local_sems.at[a])
        passed = []
        for j, chip in enumerate(chips):
            for a in range(na):
                copy(a, 1 + j, (*chip, c), me).wait_recv()
                cp = copy(a, 4 + j, (*chip, c), sibling)
                cp.start()
                passed.append(cp)
        for a in range(na):
            copy(a, 0, sibling, me).wait_recv()
            for j, chip in enumerate(chips):
                copy(a, 4 + j, (*chip, 1 - c), me).wait_recv()
        for cp in first + passed:
            cp.wait_send()

    return pl.pallas_call(
        body, in_specs=[ANY] * na, out_specs=[ANY] * na,
        out_shape=[SDS((N_DEV,) + s.shape, s.dtype) for s in shards],
        scratch_shapes=[pltpu.SemaphoreType.DMA((na, 7)), pltpu.SemaphoreType.DMA((na, 7)),
                        pltpu.SemaphoreType.DMA((na,))] + [pltpu.VMEM(s.shape, s.dtype) for s in shards],
        name="all_gather")(*shards)


N_CHIP = 4


def _exchange_pair(parts):
    na = len(parts)

    def body(*refs):
        ins, own, got = refs[:na], refs[na:2 * na], refs[2 * na:3 * na]
        send_sems, recv_sems, local_sems = refs[3 * na:3 * na + 3]
        bufs = refs[3 * na + 3:]
        x, y, c = lax.axis_index("x"), lax.axis_index("y"), lax.axis_index("c")
        sibling = (x, y, 1 - c)
        sent = []
        for a in range(na):
            for k in range(N_CHIP):
                cp = pltpu.make_async_remote_copy(
                    src_ref=ins[a].at[2 * k + 1 - c], dst_ref=got[a].at[k], send_sem=send_sems.at[a, k],
                    recv_sem=recv_sems.at[a, k], device_id=sibling, device_id_type=MESH)
                cp.start()
                sent.append(cp)
        for a in range(na):
            for k in range(N_CHIP):
                _bounce(ins[a].at[2 * k + c], own[a].at[k], bufs[a], local_sems.at[a])
        for cp in sent:
            cp.wait()

    half = [SDS((N_CHIP,) + p.shape[1:], p.dtype) for p in parts]
    res = pl.pallas_call(
        body, in_specs=[ANY] * na, out_specs=[ANY] * (2 * na), out_shape=half + half,
        scratch_shapes=[pltpu.SemaphoreType.DMA((na, N_CHIP)), pltpu.SemaphoreType.DMA((na, N_CHIP)),
                        pltpu.SemaphoreType.DMA((na,))] + [pltpu.VMEM(p.shape[1:], p.dtype) for p in parts],
        name="exchange_pair")(*parts)
    return res[:na], res[na:]


def _pair_sum(own, got):
    na = len(own)

    def body(*refs):
        for a in range(na):
            o_ref, g_ref, s_ref = refs[a], refs[na + a], refs[2 * na + a]
            s_ref[...] = (o_ref[...].astype(F32) + g_ref[...].astype(F32)).astype(s_ref.dtype)

    def spec(p):
        nd = len(p.shape) - 1
        return pl.BlockSpec((1,) + p.shape[1:], lambda k, nd=nd: (k,) + (0,) * nd)

    return pl.pallas_call(
        body, grid=(N_CHIP,), in_specs=[spec(p) for p in own] + [spec(p) for p in got],
        out_specs=[spec(p) for p in own], out_shape=[SDS(p.shape, p.dtype) for p in own],
        compiler_params=_cparams(), name="pair_sum")(*own, *got)


def _exchange_chips(parts):
    na = len(parts)

    def body(*refs):
        ins, outs = refs[:na], refs[na:2 * na]
        send_sems, recv_sems, local_sems = refs[2 * na:2 * na + 3]
        bufs = refs[2 * na + 3:]
        x, y, c = lax.axis_index("x"), lax.axis_index("y"), lax.axis_index("c")
        mine = 2 * x + y
        chips = [(1 - x, y), (x, 1 - y), (1 - x, 1 - y)]
        sent = []
        for a in range(na):
            for j, (px, py) in enumerate(chips):
                cp = pltpu.make_async_remote_copy(
                    src_ref=ins[a].at[2 * px + py], dst_ref=outs[a].at[mine], send_sem=send_sems.at[a, j],
                    recv_sem=recv_sems.at[a, j], device_id=(px, py, c), device_id_type=MESH)
                cp.start()
                sent.append(cp)
        for a in range(na):
            _bounce(ins[a].at[mine], outs[a].at[mine], bufs[a], local_sems.at[a])
        for a in range(na):
            for j, (px, py) in enumerate(chips):
                pltpu.make_async_remote_copy(
                    src_ref=ins[a].at[2 * px + py], dst_ref=outs[a].at[2 * px + py], send_sem=send_sems.at[a, j],
                    recv_sem=recv_sems.at[a, j], device_id=(px, py, c), device_id_type=MESH).wait_recv()
        for cp in sent:
            cp.wait_send()

    return pl.pallas_call(
        body, in_specs=[ANY] * na, out_specs=[ANY] * na, out_shape=[SDS(p.shape, p.dtype) for p in parts],
        scratch_shapes=[pltpu.SemaphoreType.DMA((na, 3)), pltpu.SemaphoreType.DMA((na, 3)),
                        pltpu.SemaphoreType.DMA((na,))] + [pltpu.VMEM(p.shape[1:], p.dtype) for p in parts],
        name="exchange_chips")(*parts)


ROW_TILES = D_MODEL // 128


def _rows3(t):
    return jnp.transpose(t[0]).reshape(t.shape[2], ROW_TILES, 128)


def _unrows3(t):
    return jnp.transpose(t.reshape(t.shape[0], D_MODEL))[None]


def _cast_shards(w_in3, w_att, w_ssm, w_o):
    def body(wi_ref, wa_ref, ws_ref, wo_ref, a_ref, b_ref, c_ref, d_ref):
        a_ref[...] = wi_ref[...].reshape(SHARD_IN // 2, 2 * ROW_TILES, 128).astype(BF16)
        b_ref[...] = wa_ref[...].astype(BF16)
        c_ref[...] = ws_ref[...].astype(BF16)
        d_ref[...] = wo_ref[...].astype(BF16)

    return pl.pallas_call(
        body, out_shape=[SDS((SHARD_IN // 2, 2 * ROW_TILES, 128), BF16), SDS(w_att.shape, BF16),
                         SDS(w_ssm.shape, BF16), SDS(w_o.shape, BF16)],
        compiler_params=_cparams(), name="cast_shards")(w_in3, w_att, w_ssm, w_o)


def _pieces():
    out = []
    for r0, c0, w in _SEGS:
        r = r0
        while r < r0 + w:
            d = r // SHARD_IN
            n = min(r0 + w, (d + 1) * SHARD_IN) - r
            out.append((c0 + (r - r0), d, r - d * SHARD_IN, n))
            r += n
    return out


def _to_aligned_t(slabs):
    def body(a_ref, o_ref):
        for (t, d, s, n) in _pieces():
            o_ref[t:t + n, :] = a_ref[d, s // 2:(s + n) // 2].reshape(n, D_MODEL)
        o_ref[C_DT + 32:C_DT + 128, :] = jnp.zeros((96, D_MODEL), slabs.dtype)

    return pl.pallas_call(body, out_shape=SDS((PW, D_MODEL), slabs.dtype), compiler_params=_cparams(),
                          name="to_aligned")(slabs)


def _from_aligned_t(g):
    def body(g_ref, o_ref):
        for (t, d, s, n) in _pieces():
            o_ref[d, s // 2:(s + n) // 2] = g_ref[t:t + n, :].reshape(n // 2, 2 * ROW_TILES, 128)

    return pl.pallas_call(body, out_shape=SDS((N_DEV, SHARD_IN // 2, 2 * ROW_TILES, 128), g.dtype),
                          compiler_params=_cparams(), name="from_aligned")(g)


_SEGS = [
    (R_Q, C_Q, 1024), (R_K, C_K, 256), (R_V, C_V, 256), (R_ZA, C_ZA, 1024), (R_ZS, C_ZS, 2048),
    (R_XBC, C_XBC, 3072), (R_DT, C_DT, 32), (R_GA, C_GA, 1024), (R_GS, C_GS, 1024)]


def _pad_lanes(v, n=128):
    return jnp.pad(v, ((0, 0), (0, n - v.shape[1])))


def _local_step(h, tgt, w_alt, w_att, w_ssm, w_o, g_pre, conv_w8, conv_b, dt_bias, a_log, d_skip, sinks,
                g_ssm, g_post):
    dtb, al, dsk, snk = _pad_lanes(dt_bias), _pad_lanes(a_log), _pad_lanes(d_skip), _pad_lanes(sinks)
    u = _norm_u(h, g_pre)
    proj = _matmul(u, w_alt, "nt", F32, T, 896, D_MODEL, "in_proj")
    o = _attn_fwd(proj, snk)
    xbc_act = _conv_fwd(proj, conv_w8, conv_b)
    sn, states = _ssd_fwd(xbc_act, proj, dtb, al, dsk, g_ssm)
    a_in, mg, ya, ys, out = _post_a(o, proj, sn, w_att, w_ssm, w_o)
    (loss, dres, dout, dya, dys, dga, dgs, do, dza, dsn, dgp) = _post_b(
        out, h, tgt, proj, ya, ys, o, g_post, w_att, w_ssm, w_o)
    dxs, dbm, dcm, ddt4, dzs, ddtb, dal, ddsk, dgn = _ssd_bwd(xbc_act, proj, dtb, al, dsk, g_ssm, states, dsn)
    dxx, dwx, dbx = _conv_bwd(proj, conv_w8, conv_b, dxs, 0, "conv_bwd_x")
    dxb, dwb, dbb = _conv_bwd(proj, conv_w8, conv_b, dbm, SSM_INNER, "conv_bwd_b")
    dxc, dwc, dbc = _conv_bwd(proj, conv_w8, conv_b, dcm, SSM_INNER + GRP_W, "conv_bwd_c")
    dq, dk, dv, dsink = _attn_bwd(proj, snk, do)
    dproj = _assemble(dq, dza, dga, dgs, dzs, dxx, dxb, dxc, dk, dv, ddt4)
    du = _matmul(dproj, w_alt, "nn", F32, T, D_MODEL, 1408, "d_u")
    dw_alt = _matmul(dproj, u, "tn", BF16, 896, D_MODEL, T, "d_w_in")
    dh, dgpre = _norm_bwd(h, g_pre, du, dres)
    dw_att = _matmul(a_in, dya, "tn", BF16, D_MODEL, D_MODEL, T, "d_w_att")
    dw_ssm = _matmul(sn, dys, "tn", BF16, D_MODEL, D_MODEL, T, "d_w_ssm")
    dw_o = _matmul(mg, dout, "tn", BF16, D_MODEL, D_MODEL, T, "d_w_o")
    return dict(
        loss=loss[0, 0], dh=dh, dw_alt=dw_alt, dw_att=dw_att, dw_ssm=dw_ssm, dw_o=dw_o,
        small=(dgpre, dbx, dbb, dbc, ddtb, dal, ddsk, dsink, dgn, dgp, dwx, dwb, dwc))


def kernel(x, meta_tokens, g_pre, w_in, conv_w, conv_b, dt_bias, a_log, d_skip, attn_sinks, g_ssm_norm, w_out_att, w_out_ssm, w_out, g_post, loss_target, m_meta_tokens, m_g_pre, m_w_in, m_conv_w, m_conv_b, m_dt_bias, m_a_log, m_d_skip, m_attn_sinks, m_g_ssm_norm, m_w_out_att, m_w_out_ssm, m_w_out, m_g_post, v_meta_tokens, v_g_pre, v_w_in, v_conv_w, v_conv_b, v_dt_bias, v_a_log, v_d_skip, v_attn_sinks, v_g_ssm_norm, v_w_out_att, v_w_out_ssm, v_w_out, v_g_post):
    w_in3, m_in3, v_in3 = _rows3(w_in), _rows3(m_w_in), _rows3(v_w_in)
    a_sh, att_sh, ssm_sh, o_sh = _cast_shards(w_in3, w_out_att[0], w_out_ssm[0], w_out[0])
    cw_sh = jnp.pad(conv_w[0], ((0, 4), (0, 0)))
    a_all, att_all, ssm_all, o_all, meta_all, cw_all = _all_gather([a_sh, att_sh, ssm_sh, o_sh, meta_tokens, cw_sh])
    w_alt = _to_aligned_t(a_all)
    w_att = att_all.reshape(D_MODEL, D_MODEL)
    w_ssm = ssm_all.reshape(SSM_INNER, D_MODEL)
    w_o = o_all.reshape(D_MODEL, D_MODEL)
    meta_full = meta_all.transpose(1, 0, 2).reshape(N_META, D_MODEL)
    conv_w8 = cw_all.transpose(1, 0, 2).reshape(8, CONV_DIM)

    h = jnp.concatenate([jnp.zeros((PAD, D_MODEL), F32), meta_full, x[0]], axis=0)
    tgt = jnp.concatenate([jnp.zeros((PAD + N_META, D_MODEL), F32), loss_target[0]], axis=0)
    r = _local_step(h, tgt, w_alt, w_att, w_ssm, w_o, g_pre, conv_w8, conv_b, dt_bias, a_log, d_skip, attn_sinks,
                    g_ssm_norm, g_post)
    loss = lax.psum(r["loss"], ("x", "y", "c"))
    grad_x = r["dh"][PAD + N_META:][None]

    small8 = _small_pack(*r["small"], r["dh"])
    own, got = _exchange_pair([
        _from_aligned_t(r["dw_alt"]), r["dw_att"].reshape(N_DEV, 128, D_MODEL),
        r["dw_ssm"].reshape(N_DEV, 256, D_MODEL), r["dw_o"].reshape(N_DEV, 128, D_MODEL), small8])
    ra, r_att, r_ssm, r_o, rs = _exchange_chips(_pair_sum(own, got))

    res_in = [_unrows3(t) for t in _sum_adamw_rows3(ra, w_in3, m_in3, v_in3, "adamw_w_in")]
    res_att = [t[None] for t in _sum_adamw(r_att, w_out_att[0], m_w_out_att[0], v_w_out_att[0], 512, "adamw_w_att")]
    res_ssm = [t[None] for t in _sum_adamw(r_ssm, w_out_ssm[0], m_w_out_ssm[0], v_w_out_ssm[0], 512, "adamw_w_ssm")]
    res_o = [t[None] for t in _sum_adamw(r_o, w_out[0], m_w_out[0], v_w_out[0], 512, "adamw_w_o")]
    (res_gpre, res_convb, res_dtb, res_alog, res_dskip, res_sink, res_gssm, res_gpost, res_cw, res_meta) = _small_finish(
        rs, [(g_pre, m_g_pre, v_g_pre), (conv_b, m_conv_b, v_conv_b), (dt_bias, m_dt_bias, v_dt_bias),
             (a_log, m_a_log, v_a_log), (d_skip, m_d_skip, v_d_skip), (attn_sinks, m_attn_sinks, v_attn_sinks),
             (g_ssm_norm, m_g_ssm_norm, v_g_ssm_norm), (g_post, m_g_post, v_g_post),
             (conv_w[0], m_conv_w[0], v_conv_w[0]), (meta_tokens, m_meta_tokens, v_meta_tokens)])
    res_cw = [t[None] for t in res_cw]
    per_weight = [res_meta, res_gpre, res_in, res_cw, res_convb, res_dtb, res_alog, res_dskip, res_sink, res_gssm,
                  res_att, res_ssm, res_o, res_gpost]
    return (loss, grad_x, *[p[0] for p in per_weight], *[p[1] for p in per_weight], *[p[2] for p in per_weight],
            *[p[3] for p in per_weight])
```

```python
import functools
import math

import jax
import jax.numpy as jnp
from jax import lax
from jax.experimental import pallas as pl
from jax.experimental.pallas import tpu as pltpu

F32 = jnp.float32
BF16 = jnp.bfloat16
SDS = jax.ShapeDtypeStruct
HI = lax.Precision.HIGHEST
MESH = pl.DeviceIdType.MESH
ANY = pl.BlockSpec(memory_space=pl.ANY)

N_DEV = 8
D_MODEL = 1024
SEQ = 2048
N_META = 16
BLK = 128
PAD = 112
T = PAD + N_META + SEQ
NB = T // BLK
EPS = 1e-6
HEAD = 64
Q_HEADS = 16
KV_HEADS = 4
GROUP = 4
KV_W = 256
SSM_INNER = 2048
SSM_HEADS = 32
SSM_GROUPS = 4
GRP_W = 512
SSM_STATE = 128
CONV_DIM = 3072
IN_PROJ = 9760
SHARD_IN = IN_PROJ // N_DEV
NEG = -1e30

C_Q, C_ZA, C_GA, C_GS, C_ZS, C_XBC, C_K, C_V, C_DT = 0, 1024, 2048, 3072, 4096, 6144, 9216, 9472, 9728
PW = 9856
R_Q, R_K, R_V, R_ZA, R_ZS, R_XBC, R_DT, R_GA, R_GS = 0, 1024, 1280, 1536, 2560, 4608, 7680, 7712, 8736

ADAM_LR, ADAM_B1, ADAM_B2, ADAM_EPS, ADAM_WD, ADAM_STEP = 0.001, 0.9, 0.999, 1e-08, 0.01, 10

VMEM_LIMIT = 56 * 1024 * 1024


def _cparams():
    return pltpu.CompilerParams(vmem_limit_bytes=VMEM_LIMIT)


def _silu(x):
    return x * jax.nn.sigmoid(x)


def _dsilu(x):
    s = jax.nn.sigmoid(x)
    return s * (1.0 + x * (1.0 - s))


def _matmul(a, b, mode, out_dtype, tm, tn, tk, name, chips=()):
    if mode == "nn":
        (m, k), n = a.shape, b.shape[1]
        a_spec = pl.BlockSpec((tm, tk), lambda i, j, kk: (i, kk))
        b_spec = pl.BlockSpec((tk, tn), lambda i, j, kk: (kk, j))
        dims = (((1,), (0,)), ((), ()))
    elif mode == "nt":
        (m, k), n = a.shape, b.shape[0]
        a_spec = pl.BlockSpec((tm, tk), lambda i, j, kk: (i, kk))
        b_spec = pl.BlockSpec((tn, tk), lambda i, j, kk: (j, kk))
        dims = (((1,), (1,)), ((), ()))
    else:
        (k, m), n = a.shape, b.shape[1]
        a_spec = pl.BlockSpec((tk, tm), lambda i, j, kk: (kk, i))
        b_spec = pl.BlockSpec((tk, tn), lambda i, j, kk: (kk, j))
        dims = (((0,), (0,)), ((), ()))
    assert m % tm == 0 and n % tn == 0 and k % tk == 0, (a.shape, b.shape, tm, tn, tk)
    nk = k // tk
    nc = len(chips)
    grid = (m // tm, n // tn, nk)

    def body(*refs):
        a_ref, b_ref = refs[:2]
        o_ref = refs[2 + nc]
        scratch = refs[3 + 2 * nc:]
        i, j, kk = pl.program_id(0), pl.program_id(1), pl.program_id(2)
        if nc:
            ch_start, ch_finish = _chips_program(refs[2:2 + nc], refs[3 + nc:3 + 2 * nc], scratch[0 if nk == 1 else 1:])
            pl.when((i == 0) & (j == 0) & (kk == 0))(ch_start)
        part = lax.dot_general(a_ref[...], b_ref[...], dims, preferred_element_type=F32)
        if nk == 1:
            o_ref[...] = part.astype(out_dtype)
        else:
            acc_ref = scratch[0]

            @pl.when(kk == 0)
            def _():
                acc_ref[...] = part

            @pl.when((kk > 0) & (kk < nk - 1))
            def _():
                acc_ref[...] += part

            @pl.when(kk == nk - 1)
            def _():
                o_ref[...] = (acc_ref[...] + part).astype(out_dtype)
        if nc:
            pl.when((i == grid[0] - 1) & (j == grid[1] - 1) & (kk == nk - 1))(ch_finish)

    res = pl.pallas_call(
        body, grid=grid, in_specs=[a_spec, b_spec] + [ANY] * nc,
        out_specs=[pl.BlockSpec((tm, tn), lambda i, j, kk: (i, j))] + [ANY] * nc,
        out_shape=[SDS((m, n), out_dtype)] + [SDS(p.shape, p.dtype) for p in chips],
        scratch_shapes=([] if nk == 1 else [pltpu.VMEM((tm, tn), F32)]) + (_chips_scratch(chips) if nc else []),
        compiler_params=_cparams(), name=name)(a, b, *chips)
    return res if nc else res[0]


def _norm_u(h, g_pre):
    def body(h_ref, g_ref, u_ref):
        x = h_ref[...]
        r = lax.rsqrt(jnp.mean(x * x, axis=-1, keepdims=True) + EPS)
        u_ref[...] = (x * r * g_ref[...]).astype(BF16)

    return pl.pallas_call(
        body, grid=(NB,),
        in_specs=[pl.BlockSpec((BLK, D_MODEL), lambda i: (i, 0)), pl.BlockSpec((1, D_MODEL), lambda i: (0, 0))],
        out_specs=pl.BlockSpec((BLK, D_MODEL), lambda i: (i, 0)),
        out_shape=SDS((T, D_MODEL), BF16), name="norm_u")(h, g_pre)


def _norm_bwd(h, g_pre, du, dres):
    def body(h_ref, g_ref, du_ref, dres_ref, dh_ref, dg_ref):
        i = pl.program_id(0)
        x = h_ref[...]
        g = g_ref[...]
        du_ = du_ref[...]
        r = lax.rsqrt(jnp.mean(x * x, axis=-1, keepdims=True) + EPS)
        gd = g * du_
        dx = r * gd - x * (r * r * r) * jnp.mean(x * gd, axis=-1, keepdims=True)
        dh_ref[...] = dx + dres_ref[...]
        part = jnp.sum(du_ * x * r, axis=0, keepdims=True)

        @pl.when(i == 0)
        def _():
            dg_ref[...] = jnp.zeros_like(dg_ref)

        dg_ref[0:1, :] += part

    row = pl.BlockSpec((BLK, D_MODEL), lambda i: (i, 0))
    return pl.pallas_call(
        body, grid=(NB,),
        in_specs=[row, pl.BlockSpec((1, D_MODEL), lambda i: (0, 0)), row, row],
        out_specs=[row, pl.BlockSpec((8, D_MODEL), lambda i: (0, 0))],
        out_shape=[SDS((T, D_MODEL), F32), SDS((8, D_MODEL), F32)], name="norm_bwd")(h, g_pre, du, dres)


def _lane_pick(row, h):
    lane = lax.broadcasted_iota(jnp.int32, row.shape, 1)
    return jnp.sum(jnp.where(lane == h, row, 0.0), axis=1, keepdims=True)


def _attn_fn(q4s, kcats, vcats, kms, vms, sinks, n):
    r = lax.broadcasted_iota(jnp.int32, (GROUP * BLK, 2 * BLK), 0)
    s = lax.broadcasted_iota(jnp.int32, (GROUP * BLK, 2 * BLK), 1)
    i = jnp.bitwise_and(r, BLK - 1)
    gi = jnp.right_shift(r, 7)
    rel = i - s + BLK
    k_pos = n * BLK - BLK + s
    band_ok = (rel >= 0) & (rel < BLK) & (k_pos >= PAD + N_META)
    relf = rel.astype(F32)
    rm = lax.broadcasted_iota(jnp.int32, (GROUP * BLK, N_META), 0)
    mm = lax.broadcasted_iota(jnp.int32, (GROUP * BLK, N_META), 1)
    meta_ok = (PAD + mm) <= (n * BLK + jnp.bitwise_and(rm, BLK - 1))
    gcol = jnp.right_shift(lax.broadcasted_iota(jnp.int32, (GROUP * BLK, 1), 0), 7)
    outs = []
    for kh in range(KV_HEADS):
        slopes = [2.0 ** (-8.0 * (kh * GROUP + g + 1) / Q_HEADS) for g in range(GROUP)]
        slope = jnp.where(gi == 0, slopes[0], jnp.where(gi == 1, slopes[1], jnp.where(gi == 2, slopes[2], slopes[3])))
        sk = [_lane_pick(sinks, kh * GROUP + g) for g in range(GROUP)]
        sink = jnp.where(gcol == 0, sk[0], jnp.where(gcol == 1, sk[1], jnp.where(gcol == 2, sk[2], sk[3])))
        qb = (q4s[kh] * (HEAD ** -0.5)).astype(BF16)
        sb = lax.dot_general(qb, kcats[kh].astype(BF16), (((1,), (1,)), ((), ())), preferred_element_type=F32)
        sb = jnp.where(band_ok, sb - slope * relf, NEG)
        sm = lax.dot_general(qb, kms[kh].astype(BF16), (((1,), (1,)), ((), ())), preferred_element_type=F32)
        sm = jnp.where(meta_ok, sm, NEG)
        mx = jnp.maximum(jnp.maximum(jnp.max(sb, axis=1, keepdims=True), jnp.max(sm, axis=1, keepdims=True)), sink)
        mx = lax.stop_gradient(mx)
        eb = jnp.exp(sb - mx)
        em = jnp.exp(sm - mx)
        es = jnp.exp(sink - mx)
        inv = 1.0 / (jnp.sum(eb, axis=1, keepdims=True) + jnp.sum(em, axis=1, keepdims=True) + es)
        pb = (eb * inv).astype(BF16)
        pm = (em * inv).astype(BF16)
        o4 = (jnp.dot(pm, vms[kh].astype(BF16), preferred_element_type=F32)
              + jnp.dot(pb, vcats[kh].astype(BF16), preferred_element_type=F32))
        outs.append(o4)
    return outs


def _attn_specs():
    prev = lambda n: jnp.maximum(n - 1, 0)
    return [
        pl.BlockSpec((BLK, D_MODEL), lambda n: (n, C_Q // D_MODEL)),
        pl.BlockSpec((BLK, KV_W), lambda n: (prev(n), C_K // KV_W)),
        pl.BlockSpec((BLK, KV_W), lambda n: (n, C_K // KV_W)),
        pl.BlockSpec((BLK, KV_W), lambda n: (prev(n), C_V // KV_W)),
        pl.BlockSpec((BLK, KV_W), lambda n: (n, C_V // KV_W)),
        pl.BlockSpec((N_META, KV_W), lambda n: (PAD // N_META, C_K // KV_W)),
        pl.BlockSpec((N_META, KV_W), lambda n: (PAD // N_META, C_V // KV_W)),
        pl.BlockSpec((1, 128), lambda n: (0, 0)),
    ]


def _attn_load(q_ref, kp_ref, kc_ref, vp_ref, vc_ref, km_ref, vm_ref):
    q4s, kcats, vcats, kms, vms = [], [], [], [], []
    for kh in range(KV_HEADS):
        q4s.append(jnp.concatenate(
            [q_ref[:, (kh * GROUP + g) * HEAD:(kh * GROUP + g + 1) * HEAD] for g in range(GROUP)], axis=0))
        cs = slice(kh * HEAD, (kh + 1) * HEAD)
        kcats.append(jnp.concatenate([kp_ref[:, cs], kc_ref[:, cs]], axis=0))
        vcats.append(jnp.concatenate([vp_ref[:, cs], vc_ref[:, cs]], axis=0))
        kms.append(km_ref[:, cs])
        vms.append(vm_ref[:, cs])
    return q4s, kcats, vcats, kms, vms


def _attn_fwd(proj, sinks):
    def body(q_ref, kp_ref, kc_ref, vp_ref, vc_ref, km_ref, vm_ref, s_ref, o_ref):
        n = pl.program_id(0)
        args = _attn_load(q_ref, kp_ref, kc_ref, vp_ref, vc_ref, km_ref, vm_ref)
        outs = _attn_fn(*args, s_ref[...], n)
        for kh in range(KV_HEADS):
            for g in range(GROUP):
                hh = kh * GROUP + g
                o_ref[:, hh * HEAD:(hh + 1) * HEAD] = outs[kh][g * BLK:(g + 1) * BLK]

    return pl.pallas_call(
        body, grid=(NB,), in_specs=_attn_specs(),
        out_specs=pl.BlockSpec((BLK, D_MODEL), lambda n: (n, 0)),
        out_shape=SDS((T, D_MODEL), F32), name="attn_fwd")(proj, proj, proj, proj, proj, proj, proj, sinks)


def _attn_bwd(proj, sinks, do):
    def body(q_ref, kp_ref, kc_ref, vp_ref, vc_ref, km_ref, vm_ref, s_ref, do_ref, dq_ref, dk_ref, dv_ref, ds_ref):
        n = pl.program_id(0)

        @pl.when(n == 0)
        def _():
            dk_ref[...] = jnp.zeros_like(dk_ref)
            dv_ref[...] = jnp.zeros_like(dv_ref)
            ds_ref[...] = jnp.zeros_like(ds_ref)

        args = _attn_load(q_ref, kp_ref, kc_ref, vp_ref, vc_ref, km_ref, vm_ref)
        _, vjp = jax.vjp(lambda a, b, c, d, e, f: _attn_fn(a, b, c, d, e, f, n), *args, s_ref[...])
        do_f = do_ref[...].astype(F32)
        cot = [jnp.concatenate([do_f[:, (kh * GROUP + g) * HEAD:(kh * GROUP + g + 1) * HEAD] for g in range(GROUP)],
                               axis=0) for kh in range(KV_HEADS)]
        dq4s, dkcats, dvcats, dkms, dvms, dsk = vjp(cot)
        ds_ref[0:1, :] += dsk
        cur = pl.ds(pl.multiple_of(n * BLK, BLK), BLK)
        meta = slice(PAD, PAD + N_META)
        for kh in range(KV_HEADS):
            cs = slice(kh * HEAD, (kh + 1) * HEAD)
            for g in range(GROUP):
                hh = kh * GROUP + g
                dq_ref[:, hh * HEAD:(hh + 1) * HEAD] = dq4s[kh][g * BLK:(g + 1) * BLK]
            dk_ref[cur, cs] += dkcats[kh][BLK:]
            dv_ref[cur, cs] += dvcats[kh][BLK:]
            dk_ref[meta, cs] += dkms[kh]
            dv_ref[meta, cs] += dvms[kh]

        @pl.when(n > 0)
        def _():
            prv = pl.ds(pl.multiple_of((n - 1) * BLK, BLK), BLK)
            for kh in range(KV_HEADS):
                cs = slice(kh * HEAD, (kh + 1) * HEAD)
                dk_ref[prv, cs] += dkcats[kh][:BLK]
                dv_ref[prv, cs] += dvcats[kh][:BLK]

    full_kv = pl.BlockSpec((T, KV_W), lambda n: (0, 0))
    return pl.pallas_call(
        body, grid=(NB,),
        in_specs=_attn_specs() + [pl.BlockSpec((BLK, D_MODEL), lambda n: (n, 0))],
        out_specs=[pl.BlockSpec((BLK, D_MODEL), lambda n: (n, 0)), full_kv, full_kv,
                   pl.BlockSpec((8, 128), lambda n: (0, 0))],
        out_shape=[SDS((T, D_MODEL), F32), SDS((T, KV_W), F32), SDS((T, KV_W), F32), SDS((8, 128), F32)],
        name="attn_bwd")(proj, proj, proj, proj, proj, proj, proj, sinks, do)


def _conv_taps(xp, w, rows):
    return (w[0:1] * xp[5:5 + rows] + w[1:2] * xp[6:6 + rows] + w[2:3] * xp[7:7 + rows] + w[3:4] * xp[8:8 + rows])


def _conv_fwd(proj, conv_w, conv_b):
    CONV_CB = CONV_DIM
    ncb = CONV_DIM // CONV_CB
    cb0 = C_XBC // CONV_CB

    def body(tail_ref, cur_ref, w_ref, b_ref, o_ref):
        n = pl.program_id(1)
        tail = jnp.where(n > 0, tail_ref[...], 0.0)
        xp = jnp.concatenate([tail, cur_ref[...]], axis=0)
        conv = _conv_taps(xp, w_ref[...], BLK) + b_ref[...]
        row = n * BLK + lax.broadcasted_iota(jnp.int32, (BLK, 1), 0)
        o_ref[...] = jnp.where(row >= PAD, _silu(conv), 0.0)

    return pl.pallas_call(
        body, grid=(ncb, NB),
        in_specs=[pl.BlockSpec((8, CONV_CB), lambda j, n: (jnp.maximum(n * (BLK // 8) - 1, 0), cb0 + j)),
                  pl.BlockSpec((BLK, CONV_CB), lambda j, n: (n, cb0 + j)),
                  pl.BlockSpec((8, CONV_CB), lambda j, n: (0, j)),
                  pl.BlockSpec((1, CONV_CB), lambda j, n: (0, j))],
        out_specs=pl.BlockSpec((BLK, CONV_CB), lambda j, n: (n, j)),
        out_shape=SDS((T, CONV_DIM), F32), name="conv_fwd")(proj, proj, conv_w, conv_b)


def _conv_bwd(proj, conv_w, conv_b, dact, ch0, name):
    width = dact.shape[1]
    CONV_CB = width
    ncb = width // CONV_CB
    cb0 = (C_XBC + ch0) // CONV_CB
    wb0 = ch0 // CONV_CB
    last8 = T // 8 - 1

    def body(tail_ref, cur_ref, nxt_ref, w_ref, b_ref, dcur_ref, dnxt_ref, dx_ref, dw_ref, db_ref):
        n = pl.program_id(1)
        w = w_ref[...]
        tail = jnp.where(n > 0, tail_ref[...], 0.0)
        xp = jnp.concatenate([tail, cur_ref[...], nxt_ref[...]], axis=0)
        conv = _conv_taps(xp, w, BLK + 8) + b_ref[...]
        dext = jnp.concatenate([dcur_ref[...], jnp.where(n < NB - 1, dnxt_ref[...], 0.0)], axis=0)
        row = n * BLK + lax.broadcasted_iota(jnp.int32, (BLK + 8, 1), 0)
        dconv = jnp.where(row >= PAD, dext * _dsilu(conv), 0.0)
        dx = (w[0:1] * dconv[3:3 + BLK] + w[1:2] * dconv[2:2 + BLK] + w[2:3] * dconv[1:1 + BLK]
              + w[3:4] * dconv[0:BLK])
        dx_ref[...] = dx.astype(BF16)
        dc = dconv[0:BLK]
        dws = [jnp.sum(dc * xp[5 + k:5 + k + BLK], axis=0, keepdims=True) for k in range(4)]
        dwp = jnp.concatenate(dws + [jnp.zeros((4, CONV_CB), F32)], axis=0)
        dbp = jnp.sum(dc, axis=0, keepdims=True)

        @pl.when(n == 0)
        def _():
            dw_ref[...] = dwp
            db_ref[...] = jnp.concatenate([dbp, jnp.zeros((7, CONV_CB), F32)], axis=0)

        @pl.when(n > 0)
        def _():
            dw_ref[...] += dwp
            db_ref[0:1, :] += dbp

    return pl.pallas_call(
        body, grid=(ncb, NB),
        in_specs=[pl.BlockSpec((8, CONV_CB), lambda j, n: (jnp.maximum(n * (BLK // 8) - 1, 0), cb0 + j)),
                  pl.BlockSpec((BLK, CONV_CB), lambda j, n: (n, cb0 + j)),
                  pl.BlockSpec((8, CONV_CB), lambda j, n: (jnp.minimum((n + 1) * (BLK // 8), last8), cb0 + j)),
                  pl.BlockSpec((8, CONV_CB), lambda j, n: (0, wb0 + j)),
                  pl.BlockSpec((1, CONV_CB), lambda j, n: (0, wb0 + j)),
                  pl.BlockSpec((BLK, CONV_CB), lambda j, n: (n, j)),
                  pl.BlockSpec((8, CONV_CB), lambda j, n: (jnp.minimum((n + 1) * (BLK // 8), last8), j))],
        out_specs=[pl.BlockSpec((BLK, CONV_CB), lambda j, n: (n, j)),
                   pl.BlockSpec((8, CONV_CB), lambda j, n: (0, j)),
                   pl.BlockSpec((8, CONV_CB), lambda j, n: (0, j))],
        out_shape=[SDS((T, width), BF16), SDS((8, width), F32), SDS((8, width), F32)],
        name=name)(proj, proj, proj, conv_w, conv_b, dact, dact)


HPG = SSM_HEADS // SSM_GROUPS


def _iota(shape, dim):
    return lax.broadcasted_iota(jnp.int32, shape, dim)


def _mm(a, b, ca=1, cb=0):
    return lax.dot_general(a.astype(BF16), b.astype(BF16), (((ca,), (cb,)), ((), ())), preferred_element_type=F32)


def _split3(v):
    hi = v.astype(BF16)
    r1 = v - hi.astype(F32)
    mid = r1.astype(BF16)
    lo = (r1 - mid.astype(F32)).astype(BF16)
    return hi, mid, lo


def _sel_r(parts, onehot, ca=1, cb=0):
    out = lax.dot_general(parts[0], onehot, (((ca,), (cb,)), ((), ())), preferred_element_type=F32)
    for p in parts[1:]:
        out = out + lax.dot_general(p, onehot, (((ca,), (cb,)), ((), ())), preferred_element_type=F32)
    return out


def _sel_l(onehot, parts):
    out = jnp.dot(onehot, parts[0], preferred_element_type=F32)
    for p in parts[1:]:
        out = out + jnp.dot(onehot, p, preferred_element_type=F32)
    return out


def _rows8(*rows):
    r = _iota((8, rows[0].shape[1]), 0)
    out = jnp.zeros((8, rows[0].shape[1]), F32)
    for k, v in enumerate(rows):
        out = jnp.where(r == k, v, out)
    return out


def _ssd_forward(x, z, bm, cm, dt_raw, st_prev, dtb, alog, dskip, gn, g, cst_scr):
    li, si = _iota((BLK, BLK), 0), _iota((BLK, BLK), 1)
    dt_all = jax.nn.softplus(dt_raw + dtb)
    a_row = -jnp.exp(alog)
    a_all = dt_all * a_row
    cs_all = _sel_l((li >= si).astype(BF16), _split3(a_all))
    cs_parts = _split3(cs_all)
    spread = (_iota((BLK, GRP_W), 0) == g * HPG + jnp.right_shift(_iota((BLK, GRP_W), 1), 6)).astype(BF16)
    dt_e = _sel_r(_split3(dt_all), spread)
    cs_e = _sel_r(cs_parts, spread)
    d_e = _sel_r(_split3(_rows8(dskip)), spread)[0:1]
    cs_last_e = jnp.sum(jnp.where(_iota((BLK, GRP_W), 0) == BLK - 1, cs_e, 0.0), axis=0, keepdims=True)
    p_e = jnp.exp(cs_e)
    w_e = jnp.exp(cs_last_e - cs_e)
    cd_e = jnp.exp(cs_last_e)
    xr = x * dt_e
    cst_scr[...] = cs_all.T
    cst_g = cst_scr[pl.ds(pl.multiple_of(g * HPG, HPG), HPG), :]
    own = jnp.right_shift(_iota((HPG, HPG * BLK), 1), 7) == _iota((HPG, HPG * BLK), 0)
    ownf = own.astype(F32)
    q_rows = [ownf, ownf, ownf] + [jnp.where(own, jnp.concatenate([p.astype(F32)] * HPG, axis=1), 0.0)
                                   for p in _split3(cst_g)]
    q2 = jnp.concatenate(q_rows + [jnp.zeros((BLK - 6 * HPG, HPG * BLK), F32)], axis=0).astype(BF16)
    lane1 = _iota((1, BLK), 1)
    p2 = jnp.where((lane1 >= 3 * HPG) & (lane1 < 6 * HPG), -1.0, 0.0)
    for k, part in enumerate(cs_parts):
        pick = ((li == g * HPG + si - k * HPG) & (si >= k * HPG) & (si < (k + 1) * HPG)).astype(BF16)
        p2 = p2 + jnp.dot(part, pick, preferred_element_type=F32)
    dmat = jnp.dot(p2.astype(BF16), q2, preferred_element_type=F32)
    causal = _iota((BLK, HPG * BLK), 0) >= jnp.bitwise_and(_iota((BLK, HPG * BLK), 1), BLK - 1)
    lam = jnp.exp(jnp.where(causal, dmat, NEG))
    gmat = _mm(cm, bm, 1, 1)
    m_all = lam * jnp.concatenate([gmat] * HPG, axis=1)
    mb = m_all.astype(BF16)
    lo = _iota((BLK, BLK), 1) < HEAD
    xrb = xr.astype(BF16)
    zero = jnp.zeros((BLK, BLK), BF16)
    bds, yd = [], []
    for i in range(HPG // 2):
        t = xrb[:, BLK * i:BLK * (i + 1)]
        bd = jnp.concatenate([jnp.where(lo, t, zero), jnp.where(lo, zero, t)], axis=0)
        bds.append(bd)
        yd.append(jnp.dot(mb[:, 2 * BLK * i:2 * BLK * (i + 1)], bd, preferred_element_type=F32))
    cs_st = _mm(cm, st_prev)
    y = jnp.concatenate(yd, axis=1) + cs_st * p_e + d_e * x
    xrw = xr * w_e
    st_new = cd_e * st_prev + _mm(bm, xrw, 0, 0)
    yz = y * _silu(z)
    rn = lax.rsqrt(jnp.sum(yz * yz, axis=1, keepdims=True) / GRP_W + EPS)
    return dict(out=yz * rn * gn, st_new=st_new, dt_all=dt_all, a_row=a_row, dt_e=dt_e, d_e=d_e, p_e=p_e, w_e=w_e,
                cd_e=cd_e, xr=xr, xrw=xrw, lam=lam, m_all=m_all, mb=mb, bds=bds, cs_st=cs_st, y=y, yz=yz, rn=rn, lo=lo)


def _ssd_backward(f, x, z, bm, cm, dt_raw, st_prev, dtb, gn, g, dout, dst_next, cst_scr):
    li, si = _iota((BLK, BLK), 0), _iota((BLK, BLK), 1)
    yz, rn, y, p_e, w_e, cd_e, xr = f["yz"], f["rn"], f["y"], f["p_e"], f["w_e"], f["cd_e"], f["xr"]
    dgn = jnp.sum(dout * yz * rn, axis=0, keepdims=True)
    t = dout * gn
    dyz = rn * t - yz * (rn * rn * rn) * (jnp.sum(yz * t, axis=1, keepdims=True) / GRP_W)
    dy = dyz * _silu(z)
    dz = dyz * y * _dsilu(z)
    dx = f["d_e"] * dy
    dd_e = jnp.sum(dy * x, axis=0, keepdims=True)
    dcsst = dy * p_e
    dp_e = dy * f["cs_st"]
    dcm = _mm(dcsst, st_prev, 1, 1)
    dst_prev = _mm(cm, dcsst, 0, 0) + cd_e * dst_next
    dcd_e = jnp.sum(dst_next * st_prev, axis=0, keepdims=True)
    dbm = _mm(f["xrw"], dst_next, 1, 1)
    dxrw = _mm(bm, dst_next)
    dxr = dxrw * w_e
    dw_e = dxrw * xr
    dyb = dy.astype(BF16)
    dms, dxr_d = [], []
    for i in range(HPG // 2):
        dyp = dyb[:, BLK * i:BLK * (i + 1)]
        dms.append(lax.dot_general(dyp, f["bds"][i], (((1,), (1,)), ((), ())), preferred_element_type=F32))
        r = lax.dot_general(f["mb"][:, 2 * BLK * i:2 * BLK * (i + 1)], dyp, (((0,), (0,)), ((), ())),
                            preferred_element_type=F32)
        dxr_d.append(jnp.where(f["lo"], r[0:BLK], r[BLK:2 * BLK]))
    dm_all = jnp.concatenate(dms, axis=1)
    dxr = dxr + jnp.concatenate(dxr_d, axis=1)
    dlg = dm_all * f["lam"]
    dg = dlg[:, 0:BLK]
    for j in range(1, HPG):
        dg = dg + dlg[:, BLK * j:BLK * (j + 1)]
    dcm = dcm + _mm(dg, bm)
    dbm = dbm + _mm(dg, cm, 0, 0)
    q_all = dm_all * f["m_all"]
    col_sums = jnp.sum(q_all, axis=0, keepdims=True)
    cst_scr[...] = jnp.zeros_like(cst_scr)
    cst_scr[pl.ds(pl.multiple_of(g * HPG, HPG), HPG), :] = _rows8(
        *[col_sums[:, BLK * j:BLK * (j + 1)] for j in range(HPG)])
    dcs = -cst_scr[...].T
    for j in range(HPG):
        dcs = dcs + jnp.where(si == g * HPG + j,
                              jnp.sum(q_all[:, BLK * j:BLK * (j + 1)], axis=1, keepdims=True), 0.0)
    unspread = (_iota((GRP_W, BLK), 1) == g * HPG + jnp.right_shift(_iota((GRP_W, BLK), 0), 6)).astype(BF16)
    dww = dw_e * w_e
    per_head = _sel_r(_split3(jnp.concatenate([dp_e * p_e - dww, dxr * x], axis=0)), unspread)
    last = _sel_r(_split3(_rows8(jnp.sum(dww, axis=0, keepdims=True) + dcd_e * cd_e, dd_e)), unspread)
    dcs = dcs + per_head[0:BLK] + jnp.where(li == BLK - 1, last[0:1], 0.0)
    da = _sel_l((si >= li).astype(BF16), _split3(dcs))
    ddt_all = da * f["a_row"] + per_head[BLK:2 * BLK]
    dalog = jnp.sum(da * f["dt_all"], axis=0, keepdims=True) * f["a_row"]
    dx = dx + dxr * f["dt_e"]
    ddt_raw = ddt_all * jax.nn.sigmoid(dt_raw + dtb)
    ddtb = jnp.sum(ddt_raw, axis=0, keepdims=True)
    ddskip = last[1:2]
    return dict(dx=dx, dz=dz, dbm=dbm, dcm=dcm, ddt_raw=ddt_raw, dst_prev=dst_prev, ddtb=ddtb, dalog=dalog,
                ddskip=ddskip, dgn=dgn)


def _ssd_in_specs(rev):
    cidx = (lambda c: NB - 1 - c) if rev else (lambda c: c)
    return [
        pl.BlockSpec((BLK, GRP_W), lambda g, c: (cidx(c), g)),
        pl.BlockSpec((BLK, SSM_STATE), lambda g, c: (cidx(c), SSM_INNER // SSM_STATE + g)),
        pl.BlockSpec((BLK, SSM_STATE), lambda g, c: (cidx(c), SSM_INNER // SSM_STATE + SSM_GROUPS + g)),
        pl.BlockSpec((BLK, 128), lambda g, c: (cidx(c), C_DT // 128)),
        pl.BlockSpec((BLK, GRP_W), lambda g, c: (cidx(c), C_ZS // GRP_W + g)),
        pl.BlockSpec((1, 128), lambda g, c: (0, 0)),
        pl.BlockSpec((1, 128), lambda g, c: (0, 0)),
        pl.BlockSpec((1, 128), lambda g, c: (0, 0)),
        pl.BlockSpec((1, GRP_W), lambda g, c: (0, g)),
    ]


def _ssd_fwd(xbc_act, proj, dt_bias, a_log, d_skip, g_norm, gather=()):
    ng = len(gather)

    def body(*refs):
        xs_ref, b_ref, c_ref, dt_ref, z_ref, dtb_ref, al_ref, dsk_ref, gn_ref = refs[:9]
        y_ref, st_ref = refs[9 + ng:11 + ng]
        s_scr, cst_scr = refs[11 + 2 * ng:13 + 2 * ng]
        g = pl.program_id(0)
        c = pl.program_id(1)
        if ng:
            ag_start, ag_forward, ag_finish = _ag_program(refs[9:9 + ng], refs[11 + ng:11 + 2 * ng],
                                                          refs[13 + 2 * ng:])
            pl.when((g == 0) & (c == 0))(ag_start)
            pl.when((g == SSM_GROUPS // 2) & (c == 0))(ag_forward)

        @pl.when(c == 0)
        def _():
            s_scr[...] = jnp.zeros_like(s_scr)

        st_prev = s_scr[...]
        st_ref[0, 0] = st_prev
        f = _ssd_forward(xs_ref[...], z_ref[...], b_ref[...], c_ref[...], dt_ref[...], st_prev, dtb_ref[...],
                         al_ref[...], dsk_ref[...], gn_ref[...], g, cst_scr)
        y_ref[...] = f["out"].astype(BF16)
        s_scr[...] = f["st_new"]
        if ng:
            pl.when((g == SSM_GROUPS - 1) & (c == NB - 1))(ag_finish)

    return pl.pallas_call(
        body, grid=(SSM_GROUPS, NB), in_specs=_ssd_in_specs(False) + [ANY] * ng,
        out_specs=[pl.BlockSpec((BLK, GRP_W), lambda g, c: (c, g)),
                   pl.BlockSpec((1, 1, SSM_STATE, GRP_W), lambda g, c: (g, c, 0, 0))] + [ANY] * ng,
        out_shape=[SDS((T, SSM_INNER), BF16), SDS((SSM_GROUPS, NB, SSM_STATE, GRP_W), F32)]
        + [SDS((N_DEV,) + s.shape, s.dtype) for s in gather],
        scratch_shapes=[pltpu.VMEM((SSM_STATE, GRP_W), F32), pltpu.VMEM((BLK, BLK), F32)]
        + (_ag_scratch(gather) if ng else []),
        compiler_params=_cparams(),
        name="ssd_fwd")(xbc_act, xbc_act, xbc_act, proj, proj, dt_bias, a_log, d_skip, g_norm, *gather)


def _ssd_bwd(xbc_act, proj, dt_bias, a_log, d_skip, g_norm, states, dy, chips=()):
    nc = len(chips)

    def body(*refs):
        xs_ref, b_ref, c_ref, dt_ref, z_ref, dtb_ref, al_ref, dsk_ref, gn_ref, st_ref, dy_ref = refs[:11]
        (dxs_ref, db_ref, dc_ref, ddt_ref, dz_ref, ddtb_ref, dal_ref, ddsk_ref, dgn_ref) = refs[11 + nc:20 + nc]
        ds_scr, cst_scr = refs[20 + 2 * nc:22 + 2 * nc]
        g = pl.program_id(0)
        c = pl.program_id(1)
        if nc:
            ch_start, ch_finish = _chips_program(refs[11:11 + nc], refs[20 + nc:20 + 2 * nc], refs[22 + 2 * nc:])
            pl.when((g == 0) & (c == 0))(ch_start)

        @pl.when(c == 0)
        def _():
            ds_scr[...] = jnp.zeros_like(ds_scr)
            dgn_ref[...] = jnp.zeros_like(dgn_ref)

        @pl.when((c == 0) & (g == 0))
        def _():
            ddtb_ref[...] = jnp.zeros_like(ddtb_ref)
            dal_ref[...] = jnp.zeros_like(dal_ref)
            ddsk_ref[...] = jnp.zeros_like(ddsk_ref)

        x, z, bm, cm, dt_raw, st_prev = xs_ref[...], z_ref[...], b_ref[...], c_ref[...], dt_ref[...], st_ref[0, 0]
        f = _ssd_forward(x, z, bm, cm, dt_raw, st_prev, dtb_ref[...], al_ref[...], dsk_ref[...], gn_ref[...], g,
                         cst_scr)
        d = _ssd_backward(f, x, z, bm, cm, dt_raw, st_prev, dtb_ref[...], gn_ref[...], g, dy_ref[...].astype(F32),
                          ds_scr[...], cst_scr)
        dxs_ref[...] = d["dx"]
        dz_ref[...] = d["dz"].astype(BF16)
        ds_scr[...] = d["dst_prev"]
        db_ref[...] = d["dbm"]
        dc_ref[...] = d["dcm"]
        ddt_ref[...] = d["ddt_raw"]
        dgn_ref[0:1, :] += d["dgn"]
        ddtb_ref[0:1, :] += d["ddtb"]
        dal_ref[0:1, :] += d["dalog"]
        ddsk_ref[0:1, :] += d["ddskip"]
        if nc:
            pl.when((g == SSM_GROUPS - 1) & (c == NB - 1))(ch_finish)

    rc = lambda c: NB - 1 - c
    small = pl.BlockSpec((8, 128), lambda g, c: (0, 0))
    return pl.pallas_call(
        body, grid=(SSM_GROUPS, NB),
        in_specs=_ssd_in_specs(True) + [
            pl.BlockSpec((1, 1, SSM_STATE, GRP_W), lambda g, c: (g, rc(c), 0, 0)),
            pl.BlockSpec((BLK, GRP_W), lambda g, c: (rc(c), g))] + [ANY] * nc,
        out_specs=[pl.BlockSpec((BLK, GRP_W), lambda g, c: (rc(c), g)),
                   pl.BlockSpec((BLK, SSM_STATE), lambda g, c: (rc(c), g)),
                   pl.BlockSpec((BLK, SSM_STATE), lambda g, c: (rc(c), g)),
                   pl.BlockSpec((BLK, 128), lambda g, c: (rc(c), g)),
                   pl.BlockSpec((BLK, GRP_W), lambda g, c: (rc(c), g)),
                   small, small, small,
                   pl.BlockSpec((8, GRP_W), lambda g, c: (0, g))] + [ANY] * nc,
        out_shape=[SDS((T, SSM_INNER), F32), SDS((T, GRP_W), F32), SDS((T, GRP_W), F32), SDS((T, GRP_W), F32),
                   SDS((T, SSM_INNER), BF16), SDS((8, 128), F32), SDS((8, 128), F32), SDS((8, 128), F32),
                   SDS((8, SSM_INNER), F32)] + [SDS(p.shape, p.dtype) for p in chips],
        scratch_shapes=[pltpu.VMEM((SSM_STATE, GRP_W), F32), pltpu.VMEM((BLK, BLK), F32)]
        + (_chips_scratch(chips) if nc else []),
        compiler_params=_cparams(),
        name="ssd_bwd")(xbc_act, xbc_act, xbc_act, proj, proj, dt_bias, a_log, d_skip, g_norm, states, dy, *chips)


POST_R = 272


def _post_a(o, proj, sn, w_att, w_ssm, w_o):
    def body(o_ref, za_ref, ga_ref, gs_ref, sn_ref, wa_ref, ws_ref, wo_ref, a_ref, mg_ref, ya_ref, ys_ref, out_ref):
        a = (o_ref[...] * _silu(za_ref[...])).astype(BF16)
        a_ref[...] = a
        ya = jnp.dot(a, wa_ref[...], preferred_element_type=F32)
        ys = jnp.dot(sn_ref[...], ws_ref[...], preferred_element_type=F32)
        ya_ref[...] = ya.astype(BF16)
        ys_ref[...] = ys.astype(BF16)
        mg = (jax.nn.sigmoid(ga_ref[...]) * ya + jax.nn.sigmoid(gs_ref[...]) * ys).astype(BF16)
        mg_ref[...] = mg
        out_ref[...] = jnp.dot(mg, wo_ref[...], preferred_element_type=F32)

    row = pl.BlockSpec((POST_R, D_MODEL), lambda i: (i, 0))
    pcol = lambda c0: pl.BlockSpec((POST_R, D_MODEL), lambda i: (i, c0 // D_MODEL))
    full = lambda r: pl.BlockSpec((r, D_MODEL), lambda i: (0, 0))
    return pl.pallas_call(
        body, grid=(T // POST_R,),
        in_specs=[row, pcol(C_ZA), pcol(C_GA), pcol(C_GS), pl.BlockSpec((POST_R, SSM_INNER), lambda i: (i, 0)),
                  full(D_MODEL), full(SSM_INNER), full(D_MODEL)],
        out_specs=[row, row, row, row, row],
        out_shape=[SDS((T, D_MODEL), BF16), SDS((T, D_MODEL), BF16), SDS((T, D_MODEL), BF16), SDS((T, D_MODEL), BF16),
                   SDS((T, D_MODEL), F32)],
        compiler_params=_cparams(), name="post_a")(o, proj, proj, proj, sn, w_att, w_ssm, w_o)


def _post_b(out, h, tgt, proj, ya, ys, o, g_post, w_att, w_ssm, w_o):
    def body(out_ref, h_ref, t_ref, za_ref, ga_ref, gs_ref, ya_ref, ys_ref, o_ref, gp_ref, wa_ref, ws_ref, wo_ref,
             loss_ref, dres_ref, dout_ref, dya_ref, dys_ref, dga_ref, dgs_ref, do_ref, dza_ref, dsn_ref, dgp_ref):
        i = pl.program_id(0)
        x = out_ref[...]
        gp = gp_ref[...]
        r = lax.rsqrt(jnp.mean(x * x, axis=-1, keepdims=True) + EPS)
        row = i * POST_R + lax.broadcasted_iota(jnp.int32, (POST_R, 1), 0)
        res = h_ref[...] + jnp.where(row >= PAD, x * r * gp, 0.0)
        live = row >= PAD + N_META
        err = jnp.where(live, res - t_ref[...], 0.0)
        lpart = 0.5 * jnp.sum(jnp.sum(err * err, axis=1, keepdims=True) / D_MODEL, axis=0, keepdims=True)
        dres = err / D_MODEL
        dres_ref[...] = dres
        gpart = jnp.sum(dres * x * r, axis=0, keepdims=True)

        @pl.when(i == 0)
        def _():
            loss_ref[...] = jnp.zeros_like(loss_ref)
            dgp_ref[...] = jnp.zeros_like(dgp_ref)

        loss_ref[...] += jnp.broadcast_to(lpart, loss_ref.shape)
        dgp_ref[0:1, :] += gpart
        gd = gp * dres
        dout = (r * gd - x * (r * r * r) * jnp.mean(x * gd, axis=-1, keepdims=True)).astype(BF16)
        dout_ref[...] = dout
        dmg = lax.dot_general(dout, wo_ref[...], (((1,), (1,)), ((), ())), preferred_element_type=F32)
        sga = jax.nn.sigmoid(ga_ref[...])
        sgs = jax.nn.sigmoid(gs_ref[...])
        dya = (dmg * sga).astype(BF16)
        dys = (dmg * sgs).astype(BF16)
        dya_ref[...] = dya
        dys_ref[...] = dys
        dga_ref[...] = (dmg * ya_ref[...].astype(F32) * sga * (1.0 - sga)).astype(BF16)
        dgs_ref[...] = (dmg * ys_ref[...].astype(F32) * sgs * (1.0 - sgs)).astype(BF16)
        da = lax.dot_general(dya, wa_ref[...], (((1,), (1,)), ((), ())), preferred_element_type=F32)
        za = za_ref[...]
        do_ref[...] = (da * _silu(za)).astype(BF16)
        dza_ref[...] = (da * o_ref[...] * _dsilu(za)).astype(BF16)
        dsn_ref[...] = lax.dot_general(dys, ws_ref[...], (((1,), (1,)), ((), ())),
                                       preferred_element_type=F32).astype(BF16)

    row = pl.BlockSpec((POST_R, D_MODEL), lambda i: (i, 0))
    pcol = lambda c0: pl.BlockSpec((POST_R, D_MODEL), lambda i: (i, c0 // D_MODEL))
    full = lambda r: pl.BlockSpec((r, D_MODEL), lambda i: (0, 0))
    small = pl.BlockSpec((8, D_MODEL), lambda i: (0, 0))
    return pl.pallas_call(
        body, grid=(T // POST_R,),
        in_specs=[row, row, row, pcol(C_ZA), pcol(C_GA), pcol(C_GS), row, row, row,
                  pl.BlockSpec((1, D_MODEL), lambda i: (0, 0)), full(D_MODEL), full(SSM_INNER), full(D_MODEL)],
        out_specs=[pl.BlockSpec((8, 128), lambda i: (0, 0)), row, row, row, row, row, row, row, row,
                   pl.BlockSpec((POST_R, SSM_INNER), lambda i: (i, 0)), small],
        out_shape=[SDS((8, 128), F32), SDS((T, D_MODEL), F32), SDS((T, D_MODEL), BF16), SDS((T, D_MODEL), BF16),
                   SDS((T, D_MODEL), BF16), SDS((T, D_MODEL), BF16), SDS((T, D_MODEL), BF16), SDS((T, D_MODEL), BF16),
                   SDS((T, D_MODEL), BF16), SDS((T, SSM_INNER), BF16), SDS((8, D_MODEL), F32)],
        compiler_params=_cparams(), name="post_b")(out, h, tgt, proj, proj, proj, ya, ys, o, g_post, w_att, w_ssm, w_o)


def _assemble(dq, dza, dga, dgs, dzs, dxx, dxb, dxc, dk, dv, ddt4):
    def body(dq_ref, dza_ref, dga_ref, dgs_ref, dzs_ref, dxx_ref, dxb_ref, dxc_ref, dk_ref, dv_ref, ddt_ref, o_ref):
        o_ref[:, C_Q:C_Q + D_MODEL] = dq_ref[...].astype(BF16)
        o_ref[:, C_ZA:C_ZA + D_MODEL] = dza_ref[...]
        o_ref[:, C_GA:C_GA + D_MODEL] = dga_ref[...]
        o_ref[:, C_GS:C_GS + D_MODEL] = dgs_ref[...]
        o_ref[:, C_ZS:C_ZS + SSM_INNER] = dzs_ref[...]
        o_ref[:, C_XBC:C_XBC + SSM_INNER] = dxx_ref[...]
        o_ref[:, C_XBC + SSM_INNER:C_XBC + SSM_INNER + GRP_W] = dxb_ref[...]
        o_ref[:, C_XBC + SSM_INNER + GRP_W:C_XBC + CONV_DIM] = dxc_ref[...]
        o_ref[:, C_K:C_K + KV_W] = dk_ref[...].astype(BF16)
        o_ref[:, C_V:C_V + KV_W] = dv_ref[...].astype(BF16)
        d4 = ddt_ref[...]
        o_ref[:, C_DT:C_DT + 128] = (d4[:, 0:128] + d4[:, 128:256] + d4[:, 256:384] + d4[:, 384:512]).astype(BF16)

    spec = lambda w: pl.BlockSpec((BLK, w), lambda i: (i, 0))
    ins = [dq, dza, dga, dgs, dzs, dxx, dxb, dxc, dk, dv, ddt4]
    return pl.pallas_call(
        body, grid=(NB,), in_specs=[spec(a.shape[1]) for a in ins], out_specs=spec(PW),
        out_shape=SDS((T, PW), BF16), name="assemble")(*ins)


def _adamw_math(w, g, m, v):
    m = ADAM_B1 * m + (1.0 - ADAM_B1) * g
    v = ADAM_B2 * v + (1.0 - ADAM_B2) * (g * g)
    m_hat = m / (1.0 - ADAM_B1 ** ADAM_STEP)
    v_hat = v / (1.0 - ADAM_B2 ** ADAM_STEP)
    delta = -ADAM_LR * (m_hat / (jnp.sqrt(v_hat) + ADAM_EPS) + ADAM_WD * w)
    return delta, m, v


def _sum_adamw(recv, w, m, v, tc, name):
    rows, cols = w.shape
    assert cols % tc == 0

    def body(r_ref, w_ref, m_ref, v_ref, g_ref, d_ref, nm_ref, nv_ref):
        g = r_ref[0].astype(F32)
        for d in range(1, N_CHIP):
            g = g + r_ref[d].astype(F32)
        g_ref[...] = g
        delta, nm, nv = _adamw_math(w_ref[...], g, m_ref[...], v_ref[...])
        d_ref[...] = delta
        nm_ref[...] = nm
        nv_ref[...] = nv

    blk = pl.BlockSpec((rows, tc), lambda i: (0, i))
    return pl.pallas_call(
        body, grid=(cols // tc,),
        in_specs=[pl.BlockSpec((N_CHIP, rows, tc), lambda i: (0, 0, i)), blk, blk, blk],
        out_specs=[blk, blk, blk, blk], out_shape=[SDS((rows, cols), F32)] * 4,
        compiler_params=_cparams(), name=name)(recv, w, m, v)


def _sum_adamw_rows3(recv, w3, m3, v3, name):
    pairs = 61
    assert (SHARD_IN // 2) % pairs == 0

    def body(r_ref, w_ref, m_ref, v_ref, g_ref, d_ref, nm_ref, nv_ref):
        g = r_ref[0].astype(F32)
        for d in range(1, N_CHIP):
            g = g + r_ref[d].astype(F32)
        g = g.reshape(2 * pairs, ROW_TILES, 128)
        g_ref[...] = g
        delta, nm, nv = _adamw_math(w_ref[...], g, m_ref[...], v_ref[...])
        d_ref[...] = delta
        nm_ref[...] = nm
        nv_ref[...] = nv

    blk = pl.BlockSpec((2 * pairs, ROW_TILES, 128), lambda i: (i, 0, 0))
    return pl.pallas_call(
        body, grid=(SHARD_IN // 2 // pairs,),
        in_specs=[pl.BlockSpec((N_CHIP, pairs, 2 * ROW_TILES, 128), lambda i: (0, i, 0, 0)), blk, blk, blk],
        out_specs=[blk, blk, blk, blk], out_shape=[SDS(w3.shape, F32)] * 4,
        compiler_params=_cparams(), name=name)(recv, w3, m3, v3)


ROW_GPRE, ROW_CONVB, ROW_DTB, ROW_ALOG, ROW_DSKIP, ROW_SINK, ROW_GSSM, ROW_GPOST = 0, 1, 4, 5, 6, 7, 8, 10
REP_ROWS, ROW_CONVW, ROW_META, SM_ROWS = 16, 16, 24, 40
CW_SHARD = CONV_DIM // N_DEV
META_SHARD = D_MODEL // N_DEV


def _small_pack(dgpre, dbx, dbb, dbc, ddtb, dal, ddsk, dsink, dgn, dgp, dwx, dwb, dwc, dh):
    def body(dgpre_ref, dbx_ref, dbb_ref, dbc_ref, ddtb_ref, dal_ref, ddsk_ref, dsink_ref, dgn_ref, dgp_ref,
             dwx_ref, dwb_ref, dwc_ref, dh_ref, o_ref, rep):
        rep[...] = jnp.zeros_like(rep)
        rep[ROW_GPRE:ROW_GPRE + 1, :] = dgpre_ref[0:1, :]
        rep[ROW_CONVB:ROW_CONVB + 1, :] = dbx_ref[0:1, 0:1024]
        rep[ROW_CONVB + 1:ROW_CONVB + 2, :] = dbx_ref[0:1, 1024:2048]
        rep[ROW_CONVB + 2:ROW_CONVB + 3, 0:512] = dbb_ref[0:1, :]
        rep[ROW_CONVB + 2:ROW_CONVB + 3, 512:1024] = dbc_ref[0:1, :]
        rep[ROW_DTB:ROW_DTB + 1, 0:128] = ddtb_ref[0:1, :]
        rep[ROW_ALOG:ROW_ALOG + 1, 0:128] = dal_ref[0:1, :]
        rep[ROW_DSKIP:ROW_DSKIP + 1, 0:128] = ddsk_ref[0:1, :]
        rep[ROW_SINK:ROW_SINK + 1, 0:128] = dsink_ref[0:1, :]
        rep[ROW_GSSM:ROW_GSSM + 1, :] = dgn_ref[0:1, 0:1024]
        rep[ROW_GSSM + 1:ROW_GSSM + 2, :] = dgn_ref[0:1, 1024:2048]
        rep[ROW_GPOST:ROW_GPOST + 1, :] = dgp_ref[0:1, :]
        cw = jnp.concatenate([dwx_ref[...], dwb_ref[...], dwc_ref[...]], axis=1)
        mh = dh_ref[...]
        o_ref[...] = jnp.zeros_like(o_ref)
        for p in range(N_DEV):
            o_ref[p, 0:REP_ROWS, :] = rep[...]
            o_ref[p, ROW_CONVW:ROW_CONVW + 8, 0:CW_SHARD] = cw[:, p * CW_SHARD:(p + 1) * CW_SHARD]
            o_ref[p, ROW_META:ROW_META + N_META, 0:META_SHARD] = mh[:, p * META_SHARD:(p + 1) * META_SHARD]

    ins = [dgpre, dbx, dbb, dbc, ddtb, dal, ddsk, dsink, dgn, dgp, dwx, dwb, dwc]
    return pl.pallas_call(
        body, grid=(1,),
        in_specs=[pl.BlockSpec(a.shape, lambda i: (0, 0)) for a in ins]
        + [pl.BlockSpec((N_META, D_MODEL), lambda i: (PAD // N_META, 0))],
        out_specs=pl.BlockSpec((N_DEV, SM_ROWS, 1024), lambda i: (0, 0, 0)),
        out_shape=SDS((N_DEV, SM_ROWS, 1024), F32), scratch_shapes=[pltpu.VMEM((REP_ROWS, 1024), F32)],
        name="small_pack")(*ins, dh)


def _small_finish(recv, params):
    npar = len(params)

    def body(*refs):
        r_ref = refs[0]
        wmv = refs[1:1 + 3 * npar]
        outs = refs[1 + 3 * npar:1 + 7 * npar]
        gs = refs[-1]
        g = r_ref[0]
        for d in range(1, N_CHIP):
            g = g + r_ref[d]
        gs[...] = g
        grads = [
            gs[ROW_GPRE:ROW_GPRE + 1, :],
            jnp.concatenate([gs[ROW_CONVB + k:ROW_CONVB + k + 1, :] for k in range(3)], axis=1),
            gs[ROW_DTB:ROW_DTB + 1, 0:SSM_HEADS], gs[ROW_ALOG:ROW_ALOG + 1, 0:SSM_HEADS],
            gs[ROW_DSKIP:ROW_DSKIP + 1, 0:SSM_HEADS], gs[ROW_SINK:ROW_SINK + 1, 0:Q_HEADS],
            jnp.concatenate([gs[ROW_GSSM:ROW_GSSM + 1, :], gs[ROW_GSSM + 1:ROW_GSSM + 2, :]], axis=1),
            gs[ROW_GPOST:ROW_GPOST + 1, :],
            gs[ROW_CONVW:ROW_CONVW + 4, 0:CW_SHARD],
            gs[ROW_META:ROW_META + N_META, 0:META_SHARD]]
        for i in range(npar):
            w_ref, m_ref, v_ref = wmv[3 * i:3 * i + 3]
            delta, nm, nv = _adamw_math(w_ref[...], grads[i], m_ref[...], v_ref[...])
            outs[4 * i][...] = grads[i]
            outs[4 * i + 1][...] = delta
            outs[4 * i + 2][...] = nm
            outs[4 * i + 3][...] = nv

    flat = [a for wmv in params for a in wmv]
    res = pl.pallas_call(
        body, out_shape=[SDS(wmv[0].shape, F32) for wmv in params for _ in range(4)],
        scratch_shapes=[pltpu.VMEM((SM_ROWS, 1024), F32)], name="small_finish")(recv, *flat)
    return [tuple(res[4 * i:4 * i + 4]) for i in range(npar)]


def _slab(ref, px, py, pc):
    return ref.at[4 * px + 2 * py + pc]


def _bounce(src, dst, buf, sem):
    cp = pltpu.make_async_copy(src, buf, sem)
    cp.start()
    cp.wait()
    cp = pltpu.make_async_copy(buf, dst, sem)
    cp.start()
    cp.wait()


def _ag_program(ins, outs, scratch):
    na = len(ins)
    send_sems, recv_sems, local_sems = scratch[:3]
    bufs = scratch[3:]
    x, y, c = lax.axis_index("x"), lax.axis_index("y"), lax.axis_index("c")
    me, sibling = (x, y, c), (x, y, 1 - c)
    chips = [(1 - x, y), (x, 1 - y), (1 - x, 1 - y)]

    def copy(a, k, block, to, src=None):
        dst = _slab(outs[a], *block)
        return pltpu.make_async_remote_copy(
            src_ref=dst if src is None else src, dst_ref=dst, send_sem=send_sems.at[a, k],
            recv_sem=recv_sems.at[a, k], device_id=to, device_id_type=MESH)

    def own_sends():
        out = []
        for a in range(na):
            out.append(copy(a, 0, me, sibling, src=ins[a]))
            out += [copy(a, 1 + j, me, (*chip, c), src=ins[a]) for j, chip in enumerate(chips)]
        return out

    def start():
        for cp in own_sends():
            cp.start()
        for a in range(na):
            _bounce(ins[a], _slab(outs[a], *me), bufs[a], local_sems.at[a])

    def forward():
        for j, chip in enumerate(chips):
            for a in range(na):
                copy(a, 1 + j, (*chip, c), me).wait_recv()
                copy(a, 4 + j, (*chip, c), sibling).start()

    def finish():
        for a in range(na):
            copy(a, 0, sibling, me).wait_recv()
            for j, chip in enumerate(chips):
                copy(a, 4 + j, (*chip, 1 - c), me).wait_recv()
        for cp in own_sends():
            cp.wait_send()
        for j, chip in enumerate(chips):
            for a in range(na):
                copy(a, 4 + j, (*chip, c), sibling).wait_send()

    return start, forward, finish


def _ag_scratch(shards):
    na = len(shards)
    return [pltpu.SemaphoreType.DMA((na, 7)), pltpu.SemaphoreType.DMA((na, 7)),
            pltpu.SemaphoreType.DMA((na,))] + [pltpu.VMEM(s.shape, s.dtype) for s in shards]


def _all_gather(shards):
    na = len(shards)

    def body(*refs):
        start, forward, finish = _ag_program(refs[:na], refs[na:2 * na], refs[2 * na:])
        start()
        forward()
        finish()

    return pl.pallas_call(
        body, in_specs=[ANY] * na, out_specs=[ANY] * na,
        out_shape=[SDS((N_DEV,) + s.shape, s.dtype) for s in shards],
        scratch_shapes=_ag_scratch(shards), name="all_gather")(*shards)


N_CHIP = 4


def _exchange_pair(parts, name):
    na = len(parts)

    def body(*refs):
        ins, own, got = refs[:na], refs[na:2 * na], refs[2 * na:3 * na]
        send_sems, recv_sems, local_sems = refs[3 * na:3 * na + 3]
        bufs = refs[3 * na + 3:]
        x, y, c = lax.axis_index("x"), lax.axis_index("y"), lax.axis_index("c")
        sibling = (x, y, 1 - c)
        sent = []
        for a in range(na):
            for k in range(N_CHIP):
                cp = pltpu.make_async_remote_copy(
                    src_ref=ins[a].at[2 * k + 1 - c], dst_ref=got[a].at[k], send_sem=send_sems.at[a, k],
                    recv_sem=recv_sems.at[a, k], device_id=sibling, device_id_type=MESH)
                cp.start()
                sent.append(cp)
        for a in range(na):
            for k in range(N_CHIP):
                _bounce(ins[a].at[2 * k + c], own[a].at[k], bufs[a], local_sems.at[a])
        for cp in sent:
            cp.wait()

    half = [SDS((N_CHIP,) + p.shape[1:], p.dtype) for p in parts]
    res = pl.pallas_call(
        body, in_specs=[ANY] * na, out_specs=[ANY] * (2 * na), out_shape=half + half,
        scratch_shapes=[pltpu.SemaphoreType.DMA((na, N_CHIP)), pltpu.SemaphoreType.DMA((na, N_CHIP)),
                        pltpu.SemaphoreType.DMA((na,))] + [pltpu.VMEM(p.shape[1:], p.dtype) for p in parts],
        name=name)(*parts)
    return res[:na], res[na:]


def _pair_sum(own, got, name):
    na = len(own)

    def body(*refs):
        for a in range(na):
            o_ref, g_ref, s_ref = refs[a], refs[na + a], refs[2 * na + a]
            s_ref[...] = (o_ref[...].astype(F32) + g_ref[...].astype(F32)).astype(s_ref.dtype)

    def spec(p):
        nd = len(p.shape) - 1
        return pl.BlockSpec((1,) + p.shape[1:], lambda k, nd=nd: (k,) + (0,) * nd)

    return pl.pallas_call(
        body, grid=(N_CHIP,), in_specs=[spec(p) for p in own] + [spec(p) for p in got],
        out_specs=[spec(p) for p in own], out_shape=[SDS(p.shape, p.dtype) for p in own],
        compiler_params=_cparams(), name=name)(*own, *got)


def _chips_program(ins, outs, scratch):
    na = len(ins)
    send_sems, recv_sems, local_sems = scratch[:3]
    bufs = scratch[3:]
    x, y, c = lax.axis_index("x"), lax.axis_index("y"), lax.axis_index("c")
    mine = 2 * x + y
    chips = [(1 - x, y), (x, 1 - y), (1 - x, 1 - y)]

    def send(a, j):
        px, py = chips[j]
        return pltpu.make_async_remote_copy(
            src_ref=ins[a].at[2 * px + py], dst_ref=outs[a].at[mine], send_sem=send_sems.at[a, j],
            recv_sem=recv_sems.at[a, j], device_id=(px, py, c), device_id_type=MESH)

    def arrival(a, j):
        px, py = chips[j]
        return pltpu.make_async_remote_copy(
            src_ref=ins[a].at[2 * px + py], dst_ref=outs[a].at[2 * px + py], send_sem=send_sems.at[a, j],
            recv_sem=recv_sems.at[a, j], device_id=(px, py, c), device_id_type=MESH)

    def start():
        for a in range(na):
            for j in range(3):
                send(a, j).start()
        for a in range(na):
            _bounce(ins[a].at[mine], outs[a].at[mine], bufs[a], local_sems.at[a])

    def finish():
        for a in range(na):
            for j in range(3):
                arrival(a, j).wait_recv()
        for a in range(na):
            for j in range(3):
                send(a, j).wait_send()

    return start, finish


def _chips_scratch(parts):
    na = len(parts)
    return [pltpu.SemaphoreType.DMA((na, 3)), pltpu.SemaphoreType.DMA((na, 3)),
            pltpu.SemaphoreType.DMA((na,))] + [pltpu.VMEM(p.shape[1:], p.dtype) for p in parts]


def _exchange_chips(parts, name):
    na = len(parts)

    def body(*refs):
        start, finish = _chips_program(refs[:na], refs[na:2 * na], refs[2 * na:])
        start()
        finish()

    return pl.pallas_call(
        body, in_specs=[ANY] * na, out_specs=[ANY] * na, out_shape=[SDS(p.shape, p.dtype) for p in parts],
        scratch_shapes=_chips_scratch(parts), name=name)(*parts)


ROW_TILES = D_MODEL // 128


def _rows3(t):
    return jnp.transpose(t[0]).reshape(t.shape[2], ROW_TILES, 128)


def _unrows3(t):
    return jnp.transpose(t.reshape(t.shape[0], D_MODEL))[None]


def _cast_shards(w_in3, w_att, w_ssm, w_o):
    def body(wi_ref, wa_ref, ws_ref, wo_ref, a_ref, b_ref, c_ref, d_ref):
        a_ref[...] = wi_ref[...].reshape(SHARD_IN // 2, 2 * ROW_TILES, 128).astype(BF16)
        b_ref[...] = wa_ref[...].astype(BF16)
        c_ref[...] = ws_ref[...].astype(BF16)
        d_ref[...] = wo_ref[...].astype(BF16)

    return pl.pallas_call(
        body, out_shape=[SDS((SHARD_IN // 2, 2 * ROW_TILES, 128), BF16), SDS(w_att.shape, BF16),
                         SDS(w_ssm.shape, BF16), SDS(w_o.shape, BF16)],
        compiler_params=_cparams(), name="cast_shards")(w_in3, w_att, w_ssm, w_o)


def _pieces():
    out = []
    for r0, c0, w in _SEGS:
        r = r0
        while r < r0 + w:
            d = r // SHARD_IN
            n = min(r0 + w, (d + 1) * SHARD_IN) - r
            out.append((c0 + (r - r0), d, r - d * SHARD_IN, n))
            r += n
    return out


def _to_aligned_t(slabs):
    def body(a_ref, o_ref):
        for (t, d, s, n) in _pieces():
            o_ref[t:t + n, :] = a_ref[d, s // 2:(s + n) // 2].reshape(n, D_MODEL)
        o_ref[C_DT + 32:C_DT + 128, :] = jnp.zeros((96, D_MODEL), slabs.dtype)

    return pl.pallas_call(body, out_shape=SDS((PW, D_MODEL), slabs.dtype), compiler_params=_cparams(),
                          name="to_aligned")(slabs)


def _from_aligned_t(g):
    def body(g_ref, o_ref):
        for (t, d, s, n) in _pieces():
            o_ref[d, s // 2:(s + n) // 2] = g_ref[t:t + n, :].reshape(n // 2, 2 * ROW_TILES, 128)

    return pl.pallas_call(body, out_shape=SDS((N_DEV, SHARD_IN // 2, 2 * ROW_TILES, 128), g.dtype),
                          compiler_params=_cparams(), name="from_aligned")(g)


_SEGS = [
    (R_Q, C_Q, 1024), (R_K, C_K, 256), (R_V, C_V, 256), (R_ZA, C_ZA, 1024), (R_ZS, C_ZS, 2048),
    (R_XBC, C_XBC, 3072), (R_DT, C_DT, 32), (R_GA, C_GA, 1024), (R_GS, C_GS, 1024)]


def _pad_lanes(v, n=128):
    return jnp.pad(v, ((0, 0), (0, n - v.shape[1])))


def _reduce_pair(parts, tag):
    own, got = _exchange_pair(parts, "exchange_pair_" + tag)
    return _pair_sum(own, got, "pair_sum_" + tag)


def _device_step(h, tgt, w_alt, w_out, g_pre, conv_w8, conv_b, dt_bias, a_log, d_skip, sinks, g_ssm, g_post, on_mesh):
    dtb, al, dsk, snk = _pad_lanes(dt_bias), _pad_lanes(a_log), _pad_lanes(d_skip), _pad_lanes(sinks)
    u = _norm_u(h, g_pre)
    proj = _matmul(u, w_alt, "nt", F32, T, 896, D_MODEL, "in_proj")
    o = _attn_fwd(proj, snk)
    xbc_act = _conv_fwd(proj, conv_w8, conv_b)
    if on_mesh:
        sn, states, att_all, ssm_all, o_all = _ssd_fwd(xbc_act, proj, dtb, al, dsk, g_ssm, gather=w_out)
        w_att = att_all.reshape(D_MODEL, D_MODEL)
        w_ssm = ssm_all.reshape(SSM_INNER, D_MODEL)
        w_o = o_all.reshape(D_MODEL, D_MODEL)
    else:
        sn, states = _ssd_fwd(xbc_act, proj, dtb, al, dsk, g_ssm)
        w_att, w_ssm, w_o = w_out
    a_in, mg, ya, ys, out = _post_a(o, proj, sn, w_att, w_ssm, w_o)
    (loss, dres, dout, dya, dys, dga, dgs, do, dza, dsn, dgp) = _post_b(
        out, h, tgt, proj, ya, ys, o, g_post, w_att, w_ssm, w_o)
    dw_att = _matmul(a_in, dya, "tn", BF16, D_MODEL, D_MODEL, T, "d_w_att")
    dw_ssm = _matmul(sn, dys, "tn", BF16, D_MODEL, D_MODEL, T, "d_w_ssm")
    dw_o = _matmul(mg, dout, "tn", BF16, D_MODEL, D_MODEL, T, "d_w_o")
    res = {}
    if on_mesh:
        sums = _reduce_pair([dw_att.reshape(N_DEV, 128, D_MODEL), dw_ssm.reshape(N_DEV, 256, D_MODEL),
                             dw_o.reshape(N_DEV, 128, D_MODEL)], "a")
        (dxs, dbm, dcm, ddt4, dzs, ddtb, dal, ddsk, dgn, res["r_att"], res["r_ssm"], res["r_o"]) = _ssd_bwd(
            xbc_act, proj, dtb, al, dsk, g_ssm, states, dsn, chips=sums)
    else:
        dxs, dbm, dcm, ddt4, dzs, ddtb, dal, ddsk, dgn = _ssd_bwd(xbc_act, proj, dtb, al, dsk, g_ssm, states, dsn)
        res.update(dw_att=dw_att, dw_ssm=dw_ssm, dw_o=dw_o)
    dxx, dwx, dbx = _conv_bwd(proj, conv_w8, conv_b, dxs, 0, "conv_bwd_x")
    dxb, dwb, dbb = _conv_bwd(proj, conv_w8, conv_b, dbm, SSM_INNER, "conv_bwd_b")
    dxc, dwc, dbc = _conv_bwd(proj, conv_w8, conv_b, dcm, SSM_INNER + GRP_W, "conv_bwd_c")
    dq, dk, dv, dsink = _attn_bwd(proj, snk, do)
    dproj = _assemble(dq, dza, dga, dgs, dzs, dxx, dxb, dxc, dk, dv, ddt4)
    dw_alt = _matmul(dproj, u, "tn", BF16, 896, D_MODEL, T, "d_w_in")
    if on_mesh:
        sums = _reduce_pair([_from_aligned_t(dw_alt)], "b")
        du, res["r_in"] = _matmul(dproj, w_alt, "nn", F32, T, D_MODEL, 1408, "d_u", chips=sums)
    else:
        du = _matmul(dproj, w_alt, "nn", F32, T, D_MODEL, 1408, "d_u")
        res["dw_alt"] = dw_alt
    dh, dgpre = _norm_bwd(h, g_pre, du, dres)
    small = (dgpre, dbx, dbb, dbc, ddtb, dal, ddsk, dsink, dgn, dgp, dwx, dwb, dwc)
    if on_mesh:
        (res["r_small"],) = _exchange_chips(_reduce_pair([_small_pack(*small, dh)], "c"), "exchange_chips_c")
    else:
        res["small"] = small
    res.update(loss=loss[0, 0], dh=dh)
    return res


def kernel(x, meta_tokens, g_pre, w_in, conv_w, conv_b, dt_bias, a_log, d_skip, attn_sinks, g_ssm_norm, w_out_att, w_out_ssm, w_out, g_post, loss_target, m_meta_tokens, m_g_pre, m_w_in, m_conv_w, m_conv_b, m_dt_bias, m_a_log, m_d_skip, m_attn_sinks, m_g_ssm_norm, m_w_out_att, m_w_out_ssm, m_w_out, m_g_post, v_meta_tokens, v_g_pre, v_w_in, v_conv_w, v_conv_b, v_dt_bias, v_a_log, v_d_skip, v_attn_sinks, v_g_ssm_norm, v_w_out_att, v_w_out_ssm, v_w_out, v_g_post):
    w_in3, m_in3, v_in3 = _rows3(w_in), _rows3(m_w_in), _rows3(v_w_in)
    a_sh, att_sh, ssm_sh, o_sh = _cast_shards(w_in3, w_out_att[0], w_out_ssm[0], w_out[0])
    cw_sh = jnp.pad(conv_w[0], ((0, 4), (0, 0)))
    a_all, meta_all, cw_all = _all_gather([a_sh, meta_tokens, cw_sh])
    w_alt = _to_aligned_t(a_all)
    meta_full = meta_all.transpose(1, 0, 2).reshape(N_META, D_MODEL)
    conv_w8 = cw_all.transpose(1, 0, 2).reshape(8, CONV_DIM)

    h = jnp.concatenate([jnp.zeros((PAD, D_MODEL), F32), meta_full, x[0]], axis=0)
    tgt = jnp.concatenate([jnp.zeros((PAD + N_META, D_MODEL), F32), loss_target[0]], axis=0)
    r = _device_step(h, tgt, w_alt, (att_sh, ssm_sh, o_sh), g_pre, conv_w8, conv_b, dt_bias, a_log, d_skip,
                     attn_sinks, g_ssm_norm, g_post, True)
    loss = lax.psum(r["loss"], ("x", "y", "c"))
    grad_x = r["dh"][PAD + N_META:][None]

    res_in = [_unrows3(t) for t in _sum_adamw_rows3(r["r_in"], w_in3, m_in3, v_in3, "adamw_w_in")]
    res_att = [t[None] for t in _sum_adamw(r["r_att"], w_out_att[0], m_w_out_att[0], v_w_out_att[0], 512,
                                           "adamw_w_att")]
    res_ssm = [t[None] for t in _sum_adamw(r["r_ssm"], w_out_ssm[0], m_w_out_ssm[0], v_w_out_ssm[0], 512,
                                           "adamw_w_ssm")]
    res_o = [t[None] for t in _sum_adamw(r["r_o"], w_out[0], m_w_out[0], v_w_out[0], 512, "adamw_w_o")]
    (res_gpre, res_convb, res_dtb, res_alog, res_dskip, res_sink, res_gssm, res_gpost, res_cw, res_meta) = _small_finish(
        r["r_small"], [(g_pre, m_g_pre, v_g_pre), (conv_b, m_conv_b, v_conv_b), (dt_bias, m_dt_bias, v_dt_bias),
                       (a_log, m_a_log, v_a_log), (d_skip, m_d_skip, v_d_skip),
                       (attn_sinks, m_attn_sinks, v_attn_sinks), (g_ssm_norm, m_g_ssm_norm, v_g_ssm_norm),
                       (g_post, m_g_post, v_g_post), (conv_w[0], m_conv_w[0], v_conv_w[0]),
                       (meta_tokens, m_meta_tokens, v_meta_tokens)])
    res_cw = [t[None] for t in res_cw]
    per_weight = [res_meta, res_gpre, res_in, res_cw, res_convb, res_dtb, res_alog, res_dskip, res_sink, res_gssm,
                  res_att, res_ssm, res_o, res_gpost]
    return (loss, grad_x, *[p[0] for p in per_weight], *[p[1] for p in per_weight], *[p[2] for p in per_weight],
            *[p[3] for p in per_weight])
```

```python
import functools
import math

import jax
import jax.numpy as jnp
from jax import lax
from jax.experimental import pallas as pl
from jax.experimental.pallas import tpu as pltpu

F32 = jnp.float32
BF16 = jnp.bfloat16
SDS = jax.ShapeDtypeStruct
HI = lax.Precision.HIGHEST
MESH = pl.DeviceIdType.MESH
ANY = pl.BlockSpec(memory_space=pl.ANY)

N_DEV = 8
D_MODEL = 1024
SEQ = 2048
N_META = 16
BLK = 128
PAD = 112
T = PAD + N_META + SEQ
NB = T // BLK
EPS = 1e-6
HEAD = 64
Q_HEADS = 16
KV_HEADS = 4
GROUP = 4
KV_W = 256
SSM_INNER = 2048
SSM_HEADS = 32
SSM_GROUPS = 4
GRP_W = 512
SSM_STATE = 128
CONV_DIM = 3072
IN_PROJ = 9760
SHARD_IN = IN_PROJ // N_DEV
NEG = -1e30

C_Q, C_ZA, C_GA, C_GS, C_ZS, C_XBC, C_K, C_V, C_DT = 0, 1024, 2048, 3072, 4096, 6144, 9216, 9472, 9728
PW = 9856
R_Q, R_K, R_V, R_ZA, R_ZS, R_XBC, R_DT, R_GA, R_GS = 0, 1024, 1280, 1536, 2560, 4608, 7680, 7712, 8736

ADAM_LR, ADAM_B1, ADAM_B2, ADAM_EPS, ADAM_WD, ADAM_STEP = 0.001, 0.9, 0.999, 1e-08, 0.01, 10

VMEM_LIMIT = 56 * 1024 * 1024


def _cparams():
    return pltpu.CompilerParams(vmem_limit_bytes=VMEM_LIMIT)


def _silu(x):
    return x * jax.nn.sigmoid(x)


def _dsilu(x):
    s = jax.nn.sigmoid(x)
    return s * (1.0 + x * (1.0 - s))


def _matmul(a, b, mode, out_dtype, tm, tn, tk, name, chips=()):
    if mode == "nn":
        (m, k), n = a.shape, b.shape[1]
        a_spec = pl.BlockSpec((tm, tk), lambda i, j, kk: (i, kk))
        b_spec = pl.BlockSpec((tk, tn), lambda i, j, kk: (kk, j))
        dims = (((1,), (0,)), ((), ()))
    elif mode == "nt":
        (m, k), n = a.shape, b.shape[0]
        a_spec = pl.BlockSpec((tm, tk), lambda i, j, kk: (i, kk))
        b_spec = pl.BlockSpec((tn, tk), lambda i, j, kk: (j, kk))
        dims = (((1,), (1,)), ((), ()))
    else:
        (k, m), n = a.shape, b.shape[1]
        a_spec = pl.BlockSpec((tk, tm), lambda i, j, kk: (kk, i))
        b_spec = pl.BlockSpec((tk, tn), lambda i, j, kk: (kk, j))
        dims = (((0,), (0,)), ((), ()))
    assert m % tm == 0 and n % tn == 0 and k % tk == 0, (a.shape, b.shape, tm, tn, tk)
    nk = k // tk
    nc = len(chips)
    grid = (m // tm, n // tn, nk)

    def body(*refs):
        a_ref, b_ref = refs[:2]
        o_ref = refs[2 + nc]
        scratch = refs[3 + 2 * nc:]
        i, j, kk = pl.program_id(0), pl.program_id(1), pl.program_id(2)
        if nc:
            ch_start, ch_finish = _chips_program(refs[2:2 + nc], refs[3 + nc:3 + 2 * nc], scratch[0 if nk == 1 else 1:])
            pl.when((i == 0) & (j == 0) & (kk == 0))(ch_start)
        part = lax.dot_general(a_ref[...], b_ref[...], dims, preferred_element_type=F32)
        if nk == 1:
            o_ref[...] = part.astype(out_dtype)
        else:
            acc_ref = scratch[0]

            @pl.when(kk == 0)
            def _():
                acc_ref[...] = part

            @pl.when((kk > 0) & (kk < nk - 1))
            def _():
                acc_ref[...] += part

            @pl.when(kk == nk - 1)
            def _():
                o_ref[...] = (acc_ref[...] + part).astype(out_dtype)
        if nc:
            pl.when((i == grid[0] - 1) & (j == grid[1] - 1) & (kk == nk - 1))(ch_finish)

    res = pl.pallas_call(
        body, grid=grid, in_specs=[a_spec, b_spec] + [ANY] * nc,
        out_specs=[pl.BlockSpec((tm, tn), lambda i, j, kk: (i, j))] + [ANY] * nc,
        out_shape=[SDS((m, n), out_dtype)] + [SDS(p.shape, p.dtype) for p in chips],
        scratch_shapes=([] if nk == 1 else [pltpu.VMEM((tm, tn), F32)]) + (_chips_scratch(chips) if nc else []),
        compiler_params=_cparams(), name=name)(a, b, *chips)
    return res if nc else res[0]


def _norm_u(h, g_pre):
    def body(h_ref, g_ref, u_ref):
        x = h_ref[...]
        r = lax.rsqrt(jnp.mean(x * x, axis=-1, keepdims=True) + EPS)
        u_ref[...] = (x * r * g_ref[...]).astype(BF16)

    return pl.pallas_call(
        body, grid=(NB,),
        in_specs=[pl.BlockSpec((BLK, D_MODEL), lambda i: (i, 0)), pl.BlockSpec((1, D_MODEL), lambda i: (0, 0))],
        out_specs=pl.BlockSpec((BLK, D_MODEL), lambda i: (i, 0)),
        out_shape=SDS((T, D_MODEL), BF16), name="norm_u")(h, g_pre)


def _norm_bwd(h, g_pre, du, dres):
    def body(h_ref, g_ref, du_ref, dres_ref, dh_ref, dg_ref):
        i = pl.program_id(0)
        x = h_ref[...]
        g = g_ref[...]
        du_ = du_ref[...]
        r = lax.rsqrt(jnp.mean(x * x, axis=-1, keepdims=True) + EPS)
        gd = g * du_
        dx = r * gd - x * (r * r * r) * jnp.mean(x * gd, axis=-1, keepdims=True)
        dh_ref[...] = dx + dres_ref[...]
        part = jnp.sum(du_ * x * r, axis=0, keepdims=True)

        @pl.when(i == 0)
        def _():
            dg_ref[...] = jnp.zeros_like(dg_ref)

        dg_ref[0:1, :] += part

    row = pl.BlockSpec((BLK, D_MODEL), lambda i: (i, 0))
    return pl.pallas_call(
        body, grid=(NB,),
        in_specs=[row, pl.BlockSpec((1, D_MODEL), lambda i: (0, 0)), row, row],
        out_specs=[row, pl.BlockSpec((8, D_MODEL), lambda i: (0, 0))],
        out_shape=[SDS((T, D_MODEL), F32), SDS((8, D_MODEL), F32)], name="norm_bwd")(h, g_pre, du, dres)


def _lane_pick(row, h):
    lane = lax.broadcasted_iota(jnp.int32, row.shape, 1)
    return jnp.sum(jnp.where(lane == h, row, 0.0), axis=1, keepdims=True)


def _attn_fn(q4s, kcats, vcats, kms, vms, sinks, n):
    r = lax.broadcasted_iota(jnp.int32, (GROUP * BLK, 2 * BLK), 0)
    s = lax.broadcasted_iota(jnp.int32, (GROUP * BLK, 2 * BLK), 1)
    i = jnp.bitwise_and(r, BLK - 1)
    gi = jnp.right_shift(r, 7)
    rel = i - s + BLK
    k_pos = n * BLK - BLK + s
    band_ok = (rel >= 0) & (rel < BLK) & (k_pos >= PAD + N_META)
    relf = rel.astype(F32)
    rm = lax.broadcasted_iota(jnp.int32, (GROUP * BLK, N_META), 0)
    mm = lax.broadcasted_iota(jnp.int32, (GROUP * BLK, N_META), 1)
    meta_ok = (PAD + mm) <= (n * BLK + jnp.bitwise_and(rm, BLK - 1))
    gcol = jnp.right_shift(lax.broadcasted_iota(jnp.int32, (GROUP * BLK, 1), 0), 7)
    outs = []
    for kh in range(KV_HEADS):
        slopes = [2.0 ** (-8.0 * (kh * GROUP + g + 1) / Q_HEADS) for g in range(GROUP)]
        slope = jnp.where(gi == 0, slopes[0], jnp.where(gi == 1, slopes[1], jnp.where(gi == 2, slopes[2], slopes[3])))
        sk = [_lane_pick(sinks, kh * GROUP + g) for g in range(GROUP)]
        sink = jnp.where(gcol == 0, sk[0], jnp.where(gcol == 1, sk[1], jnp.where(gcol == 2, sk[2], sk[3])))
        qb = (q4s[kh] * (HEAD ** -0.5)).astype(BF16)
        sb = lax.dot_general(qb, kcats[kh].astype(BF16), (((1,), (1,)), ((), ())), preferred_element_type=F32)
        sb = jnp.where(band_ok, sb - slope * relf, NEG)
        sm = lax.dot_general(qb, kms[kh].astype(BF16), (((1,), (1,)), ((), ())), preferred_element_type=F32)
        sm = jnp.where(meta_ok, sm, NEG)
        mx = jnp.maximum(jnp.maximum(jnp.max(sb, axis=1, keepdims=True), jnp.max(sm, axis=1, keepdims=True)), sink)
        mx = lax.stop_gradient(mx)
        eb = jnp.exp(sb - mx)
        em = jnp.exp(sm - mx)
        es = jnp.exp(sink - mx)
        inv = 1.0 / (jnp.sum(eb, axis=1, keepdims=True) + jnp.sum(em, axis=1, keepdims=True) + es)
        pb = (eb * inv).astype(BF16)
        pm = (em * inv).astype(BF16)
        o4 = (jnp.dot(pm, vms[kh].astype(BF16), preferred_element_type=F32)
              + jnp.dot(pb, vcats[kh].astype(BF16), preferred_element_type=F32))
        outs.append(o4)
    return outs


def _attn_specs():
    prev = lambda n: jnp.maximum(n - 1, 0)
    return [
        pl.BlockSpec((BLK, D_MODEL), lambda n: (n, C_Q // D_MODEL)),
        pl.BlockSpec((BLK, KV_W), lambda n: (prev(n), C_K // KV_W)),
        pl.BlockSpec((BLK, KV_W), lambda n: (n, C_K // KV_W)),
        pl.BlockSpec((BLK, KV_W), lambda n: (prev(n), C_V // KV_W)),
        pl.BlockSpec((BLK, KV_W), lambda n: (n, C_V // KV_W)),
        pl.BlockSpec((N_META, KV_W), lambda n: (PAD // N_META, C_K // KV_W)),
        pl.BlockSpec((N_META, KV_W), lambda n: (PAD // N_META, C_V // KV_W)),
        pl.BlockSpec((1, 128), lambda n: (0, 0)),
    ]


def _attn_load(q_ref, kp_ref, kc_ref, vp_ref, vc_ref, km_ref, vm_ref):
    q4s, kcats, vcats, kms, vms = [], [], [], [], []
    for kh in range(KV_HEADS):
        q4s.append(jnp.concatenate(
            [q_ref[:, (kh * GROUP + g) * HEAD:(kh * GROUP + g + 1) * HEAD] for g in range(GROUP)], axis=0))
        cs = slice(kh * HEAD, (kh + 1) * HEAD)
        kcats.append(jnp.concatenate([kp_ref[:, cs], kc_ref[:, cs]], axis=0))
        vcats.append(jnp.concatenate([vp_ref[:, cs], vc_ref[:, cs]], axis=0))
        kms.append(km_ref[:, cs])
        vms.append(vm_ref[:, cs])
    return q4s, kcats, vcats, kms, vms


def _attn_fwd(proj, sinks):
    def body(q_ref, kp_ref, kc_ref, vp_ref, vc_ref, km_ref, vm_ref, s_ref, o_ref):
        n = pl.program_id(0)
        args = _attn_load(q_ref, kp_ref, kc_ref, vp_ref, vc_ref, km_ref, vm_ref)
        outs = _attn_fn(*args, s_ref[...], n)
        for kh in range(KV_HEADS):
            for g in range(GROUP):
                hh = kh * GROUP + g
                o_ref[:, hh * HEAD:(hh + 1) * HEAD] = outs[kh][g * BLK:(g + 1) * BLK]

    return pl.pallas_call(
        body, grid=(NB,), in_specs=_attn_specs(),
        out_specs=pl.BlockSpec((BLK, D_MODEL), lambda n: (n, 0)),
        out_shape=SDS((T, D_MODEL), F32), name="attn_fwd")(proj, proj, proj, proj, proj, proj, proj, sinks)


def _attn_bwd(proj, sinks, do):
    def body(q_ref, kp_ref, kc_ref, vp_ref, vc_ref, km_ref, vm_ref, s_ref, do_ref, dq_ref, dk_ref, dv_ref, ds_ref):
        n = pl.program_id(0)

        @pl.when(n == 0)
        def _():
            dk_ref[...] = jnp.zeros_like(dk_ref)
            dv_ref[...] = jnp.zeros_like(dv_ref)
            ds_ref[...] = jnp.zeros_like(ds_ref)

        args = _attn_load(q_ref, kp_ref, kc_ref, vp_ref, vc_ref, km_ref, vm_ref)
        _, vjp = jax.vjp(lambda a, b, c, d, e, f: _attn_fn(a, b, c, d, e, f, n), *args, s_ref[...])
        do_f = do_ref[...].astype(F32)
        cot = [jnp.concatenate([do_f[:, (kh * GROUP + g) * HEAD:(kh * GROUP + g + 1) * HEAD] for g in range(GROUP)],
                               axis=0) for kh in range(KV_HEADS)]
        dq4s, dkcats, dvcats, dkms, dvms, dsk = vjp(cot)
        ds_ref[0:1, :] += dsk
        cur = pl.ds(pl.multiple_of(n * BLK, BLK), BLK)
        meta = slice(PAD, PAD + N_META)
        for kh in range(KV_HEADS):
            cs = slice(kh * HEAD, (kh + 1) * HEAD)
            for g in range(GROUP):
                hh = kh * GROUP + g
                dq_ref[:, hh * HEAD:(hh + 1) * HEAD] = dq4s[kh][g * BLK:(g + 1) * BLK]
            dk_ref[cur, cs] += dkcats[kh][BLK:]
            dv_ref[cur, cs] += dvcats[kh][BLK:]
            dk_ref[meta, cs] += dkms[kh]
            dv_ref[meta, cs] += dvms[kh]

        @pl.when(n > 0)
        def _():
            prv = pl.ds(pl.multiple_of((n - 1) * BLK, BLK), BLK)
            for kh in range(KV_HEADS):
                cs = slice(kh * HEAD, (kh + 1) * HEAD)
                dk_ref[prv, cs] += dkcats[kh][:BLK]
                dv_ref[prv, cs] += dvcats[kh][:BLK]

    full_kv = pl.BlockSpec((T, KV_W), lambda n: (0, 0))
    return pl.pallas_call(
        body, grid=(NB,),
        in_specs=_attn_specs() + [pl.BlockSpec((BLK, D_MODEL), lambda n: (n, 0))],
        out_specs=[pl.BlockSpec((BLK, D_MODEL), lambda n: (n, 0)), full_kv, full_kv,
                   pl.BlockSpec((8, 128), lambda n: (0, 0))],
        out_shape=[SDS((T, D_MODEL), F32), SDS((T, KV_W), F32), SDS((T, KV_W), F32), SDS((8, 128), F32)],
        name="attn_bwd")(proj, proj, proj, proj, proj, proj, proj, sinks, do)


def _conv_taps(xp, w, rows):
    return (w[0:1] * xp[5:5 + rows] + w[1:2] * xp[6:6 + rows] + w[2:3] * xp[7:7 + rows] + w[3:4] * xp[8:8 + rows])


def _conv_fwd(proj, conv_w, conv_b):
    CONV_CB = CONV_DIM
    ncb = CONV_DIM // CONV_CB
    cb0 = C_XBC // CONV_CB

    def body(tail_ref, cur_ref, w_ref, b_ref, o_ref):
        n = pl.program_id(1)
        tail = jnp.where(n > 0, tail_ref[...], 0.0)
        xp = jnp.concatenate([tail, cur_ref[...]], axis=0)
        conv = _conv_taps(xp, w_ref[...], BLK) + b_ref[...]
        row = n * BLK + lax.broadcasted_iota(jnp.int32, (BLK, 1), 0)
        o_ref[...] = jnp.where(row >= PAD, _silu(conv), 0.0)

    return pl.pallas_call(
        body, grid=(ncb, NB),
        in_specs=[pl.BlockSpec((8, CONV_CB), lambda j, n: (jnp.maximum(n * (BLK // 8) - 1, 0), cb0 + j)),
                  pl.BlockSpec((BLK, CONV_CB), lambda j, n: (n, cb0 + j)),
                  pl.BlockSpec((8, CONV_CB), lambda j, n: (0, j)),
                  pl.BlockSpec((1, CONV_CB), lambda j, n: (0, j))],
        out_specs=pl.BlockSpec((BLK, CONV_CB), lambda j, n: (n, j)),
        out_shape=SDS((T, CONV_DIM), F32), name="conv_fwd")(proj, proj, conv_w, conv_b)


def _conv_bwd(proj, conv_w, conv_b, dact, ch0, name):
    width = dact.shape[1]
    CONV_CB = width
    ncb = width // CONV_CB
    cb0 = (C_XBC + ch0) // CONV_CB
    wb0 = ch0 // CONV_CB
    last8 = T // 8 - 1

    def body(tail_ref, cur_ref, nxt_ref, w_ref, b_ref, dcur_ref, dnxt_ref, dx_ref, dw_ref, db_ref):
        n = pl.program_id(1)
        w = w_ref[...]
        tail = jnp.where(n > 0, tail_ref[...], 0.0)
        xp = jnp.concatenate([tail, cur_ref[...], nxt_ref[...]], axis=0)
        conv = _conv_taps(xp, w, BLK + 8) + b_ref[...]
        dext = jnp.concatenate([dcur_ref[...], jnp.where(n < NB - 1, dnxt_ref[...], 0.0)], axis=0)
        row = n * BLK + lax.broadcasted_iota(jnp.int32, (BLK + 8, 1), 0)
        dconv = jnp.where(row >= PAD, dext * _dsilu(conv), 0.0)
        dx = (w[0:1] * dconv[3:3 + BLK] + w[1:2] * dconv[2:2 + BLK] + w[2:3] * dconv[1:1 + BLK]
              + w[3:4] * dconv[0:BLK])
        dx_ref[...] = dx.astype(BF16)
        dc = dconv[0:BLK]
        dws = [jnp.sum(dc * xp[5 + k:5 + k + BLK], axis=0, keepdims=True) for k in range(4)]
        dwp = jnp.concatenate(dws + [jnp.zeros((4, CONV_CB), F32)], axis=0)
        dbp = jnp.sum(dc, axis=0, keepdims=True)

        @pl.when(n == 0)
        def _():
            dw_ref[...] = dwp
            db_ref[...] = jnp.concatenate([dbp, jnp.zeros((7, CONV_CB), F32)], axis=0)

        @pl.when(n > 0)
        def _():
            dw_ref[...] += dwp
            db_ref[0:1, :] += dbp

    return pl.pallas_call(
        body, grid=(ncb, NB),
        in_specs=[pl.BlockSpec((8, CONV_CB), lambda j, n: (jnp.maximum(n * (BLK // 8) - 1, 0), cb0 + j)),
                  pl.BlockSpec((BLK, CONV_CB), lambda j, n: (n, cb0 + j)),
                  pl.BlockSpec((8, CONV_CB), lambda j, n: (jnp.minimum((n + 1) * (BLK // 8), last8), cb0 + j)),
                  pl.BlockSpec((8, CONV_CB), lambda j, n: (0, wb0 + j)),
                  pl.BlockSpec((1, CONV_CB), lambda j, n: (0, wb0 + j)),
                  pl.BlockSpec((BLK, CONV_CB), lambda j, n: (n, j)),
                  pl.BlockSpec((8, CONV_CB), lambda j, n: (jnp.minimum((n + 1) * (BLK // 8), last8), j))],
        out_specs=[pl.BlockSpec((BLK, CONV_CB), lambda j, n: (n, j)),
                   pl.BlockSpec((8, CONV_CB), lambda j, n: (0, j)),
                   pl.BlockSpec((8, CONV_CB), lambda j, n: (0, j))],
        out_shape=[SDS((T, width), BF16), SDS((8, width), F32), SDS((8, width), F32)],
        name=name)(proj, proj, proj, conv_w, conv_b, dact, dact)


HPG = SSM_HEADS // SSM_GROUPS


def _iota(shape, dim):
    return lax.broadcasted_iota(jnp.int32, shape, dim)


def _mm(a, b, ca=1, cb=0):
    return lax.dot_general(a.astype(BF16), b.astype(BF16), (((ca,), (cb,)), ((), ())), preferred_element_type=F32)


def _split3(v):
    hi = v.astype(BF16)
    r1 = v - hi.astype(F32)
    mid = r1.astype(BF16)
    lo = (r1 - mid.astype(F32)).astype(BF16)
    return hi, mid, lo


def _sel_r(parts, onehot, ca=1, cb=0):
    out = lax.dot_general(parts[0], onehot, (((ca,), (cb,)), ((), ())), preferred_element_type=F32)
    for p in parts[1:]:
        out = out + lax.dot_general(p, onehot, (((ca,), (cb,)), ((), ())), preferred_element_type=F32)
    return out


def _sel_l(onehot, parts):
    out = jnp.dot(onehot, parts[0], preferred_element_type=F32)
    for p in parts[1:]:
        out = out + jnp.dot(onehot, p, preferred_element_type=F32)
    return out


def _rows8(*rows):
    r = _iota((8, rows[0].shape[1]), 0)
    out = jnp.zeros((8, rows[0].shape[1]), F32)
    for k, v in enumerate(rows):
        out = jnp.where(r == k, v, out)
    return out


def _ssd_forward(x, z, bm, cm, dt_raw, st_prev, dtb, alog, dskip, gn, g, cst_scr):
    li, si = _iota((BLK, BLK), 0), _iota((BLK, BLK), 1)
    dt_all = jax.nn.softplus(dt_raw + dtb)
    a_row = -jnp.exp(alog)
    a_all = dt_all * a_row
    cs_all = _sel_l((li >= si).astype(BF16), _split3(a_all))
    cs_parts = _split3(cs_all)
    spread = (_iota((BLK, GRP_W), 0) == g * HPG + jnp.right_shift(_iota((BLK, GRP_W), 1), 6)).astype(BF16)
    dt_e = _sel_r(_split3(dt_all), spread)
    cs_e = _sel_r(cs_parts, spread)
    d_e = _sel_r(_split3(_rows8(dskip)), spread)[0:1]
    cs_last_e = jnp.sum(jnp.where(_iota((BLK, GRP_W), 0) == BLK - 1, cs_e, 0.0), axis=0, keepdims=True)
    p_e = jnp.exp(cs_e)
    w_e = jnp.exp(cs_last_e - cs_e)
    cd_e = jnp.exp(cs_last_e)
    xr = x * dt_e
    cst_scr[...] = cs_all.T
    cst_g = cst_scr[pl.ds(pl.multiple_of(g * HPG, HPG), HPG), :]
    own = jnp.right_shift(_iota((HPG, HPG * BLK), 1), 7) == _iota((HPG, HPG * BLK), 0)
    ownf = own.astype(F32)
    q_rows = [ownf, ownf, ownf] + [jnp.where(own, jnp.concatenate([p.astype(F32)] * HPG, axis=1), 0.0)
                                   for p in _split3(cst_g)]
    q2 = jnp.concatenate(q_rows + [jnp.zeros((BLK - 6 * HPG, HPG * BLK), F32)], axis=0).astype(BF16)
    lane1 = _iota((1, BLK), 1)
    p2 = jnp.where((lane1 >= 3 * HPG) & (lane1 < 6 * HPG), -1.0, 0.0)
    for k, part in enumerate(cs_parts):
        pick = ((li == g * HPG + si - k * HPG) & (si >= k * HPG) & (si < (k + 1) * HPG)).astype(BF16)
        p2 = p2 + jnp.dot(part, pick, preferred_element_type=F32)
    dmat = jnp.dot(p2.astype(BF16), q2, preferred_element_type=F32)
    causal = _iota((BLK, HPG * BLK), 0) >= jnp.bitwise_and(_iota((BLK, HPG * BLK), 1), BLK - 1)
    lam = jnp.exp(jnp.where(causal, dmat, NEG))
    gmat = _mm(cm, bm, 1, 1)
    m_all = lam * jnp.concatenate([gmat] * HPG, axis=1)
    mb = m_all.astype(BF16)
    lo = _iota((BLK, BLK), 1) < HEAD
    xrb = xr.astype(BF16)
    zero = jnp.zeros((BLK, BLK), BF16)
    bds, yd = [], []
    for i in range(HPG // 2):
        t = xrb[:, BLK * i:BLK * (i + 1)]
        bd = jnp.concatenate([jnp.where(lo, t, zero), jnp.where(lo, zero, t)], axis=0)
        bds.append(bd)
        yd.append(jnp.dot(mb[:, 2 * BLK * i:2 * BLK * (i + 1)], bd, preferred_element_type=F32))
    cs_st = _mm(cm, st_prev)
    y = jnp.concatenate(yd, axis=1) + cs_st * p_e + d_e * x
    xrw = xr * w_e
    st_new = cd_e * st_prev + _mm(bm, xrw, 0, 0)
    yz = y * _silu(z)
    rn = lax.rsqrt(jnp.sum(yz * yz, axis=1, keepdims=True) / GRP_W + EPS)
    return dict(out=yz * rn * gn, st_new=st_new, dt_all=dt_all, a_row=a_row, dt_e=dt_e, d_e=d_e, p_e=p_e, w_e=w_e,
                cd_e=cd_e, xr=xr, xrw=xrw, lam=lam, m_all=m_all, mb=mb, bds=bds, cs_st=cs_st, y=y, yz=yz, rn=rn, lo=lo)


def _ssd_backward(f, x, z, bm, cm, dt_raw, st_prev, dtb, gn, g, dout, dst_next, cst_scr):
    li, si = _iota((BLK, BLK), 0), _iota((BLK, BLK), 1)
    yz, rn, y, p_e, w_e, cd_e, xr = f["yz"], f["rn"], f["y"], f["p_e"], f["w_e"], f["cd_e"], f["xr"]
    dgn = jnp.sum(dout * yz * rn, axis=0, keepdims=True)
    t = dout * gn
    dyz = rn * t - yz * (rn * rn * rn) * (jnp.sum(yz * t, axis=1, keepdims=True) / GRP_W)
    dy = dyz * _silu(z)
    dz = dyz * y * _dsilu(z)
    dx = f["d_e"] * dy
    dd_e = jnp.sum(dy * x, axis=0, keepdims=True)
    dcsst = dy * p_e
    dp_e = dy * f["cs_st"]
    dcm = _mm(dcsst, st_prev, 1, 1)
    dst_prev = _mm(cm, dcsst, 0, 0) + cd_e * dst_next
    dcd_e = jnp.sum(dst_next * st_prev, axis=0, keepdims=True)
    dbm = _mm(f["xrw"], dst_next, 1, 1)
    dxrw = _mm(bm, dst_next)
    dxr = dxrw * w_e
    dw_e = dxrw * xr
    dyb = dy.astype(BF16)
    dms, dxr_d = [], []
    for i in range(HPG // 2):
        dyp = dyb[:, BLK * i:BLK * (i + 1)]
        dms.append(lax.dot_general(dyp, f["bds"][i], (((1,), (1,)), ((), ())), preferred_element_type=F32))
        r = lax.dot_general(f["mb"][:, 2 * BLK * i:2 * BLK * (i + 1)], dyp, (((0,), (0,)), ((), ())),
                            preferred_element_type=F32)
        dxr_d.append(jnp.where(f["lo"], r[0:BLK], r[BLK:2 * BLK]))
    dm_all = jnp.concatenate(dms, axis=1)
    dxr = dxr + jnp.concatenate(dxr_d, axis=1)
    dlg = dm_all * f["lam"]
    dg = dlg[:, 0:BLK]
    for j in range(1, HPG):
        dg = dg + dlg[:, BLK * j:BLK * (j + 1)]
    dcm = dcm + _mm(dg, bm)
    dbm = dbm + _mm(dg, cm, 0, 0)
    q_all = dm_all * f["m_all"]
    col_sums = jnp.sum(q_all, axis=0, keepdims=True)
    cst_scr[...] = jnp.zeros_like(cst_scr)
    cst_scr[pl.ds(pl.multiple_of(g * HPG, HPG), HPG), :] = _rows8(
        *[col_sums[:, BLK * j:BLK * (j + 1)] for j in range(HPG)])
    dcs = -cst_scr[...].T
    for j in range(HPG):
        dcs = dcs + jnp.where(si == g * HPG + j,
                              jnp.sum(q_all[:, BLK * j:BLK * (j + 1)], axis=1, keepdims=True), 0.0)
    unspread = (_iota((GRP_W, BLK), 1) == g * HPG + jnp.right_shift(_iota((GRP_W, BLK), 0), 6)).astype(BF16)
    dww = dw_e * w_e
    per_head = _sel_r(_split3(jnp.concatenate([dp_e * p_e - dww, dxr * x], axis=0)), unspread)
    last = _sel_r(_split3(_rows8(jnp.sum(dww, axis=0, keepdims=True) + dcd_e * cd_e, dd_e)), unspread)
    dcs = dcs + per_head[0:BLK] + jnp.where(li == BLK - 1, last[0:1], 0.0)
    da = _sel_l((si >= li).astype(BF16), _split3(dcs))
    ddt_all = da * f["a_row"] + per_head[BLK:2 * BLK]
    dalog = jnp.sum(da * f["dt_all"], axis=0, keepdims=True) * f["a_row"]
    dx = dx + dxr * f["dt_e"]
    ddt_raw = ddt_all * jax.nn.sigmoid(dt_raw + dtb)
    ddtb = jnp.sum(ddt_raw, axis=0, keepdims=True)
    ddskip = last[1:2]
    return dict(dx=dx, dz=dz, dbm=dbm, dcm=dcm, ddt_raw=ddt_raw, dst_prev=dst_prev, ddtb=ddtb, dalog=dalog,
                ddskip=ddskip, dgn=dgn)


def _ssd_in_specs(rev):
    cidx = (lambda c: NB - 1 - c) if rev else (lambda c: c)
    return [
        pl.BlockSpec((BLK, GRP_W), lambda g, c: (cidx(c), g)),
        pl.BlockSpec((BLK, SSM_STATE), lambda g, c: (cidx(c), SSM_INNER // SSM_STATE + g)),
        pl.BlockSpec((BLK, SSM_STATE), lambda g, c: (cidx(c), SSM_INNER // SSM_STATE + SSM_GROUPS + g)),
        pl.BlockSpec((BLK, 128), lambda g, c: (cidx(c), C_DT // 128)),
        pl.BlockSpec((BLK, GRP_W), lambda g, c: (cidx(c), C_ZS // GRP_W + g)),
        pl.BlockSpec((1, 128), lambda g, c: (0, 0)),
        pl.BlockSpec((1, 128), lambda g, c: (0, 0)),
        pl.BlockSpec((1, 128), lambda g, c: (0, 0)),
        pl.BlockSpec((1, GRP_W), lambda g, c: (0, g)),
    ]


def _ssd_fwd(xbc_act, proj, dt_bias, a_log, d_skip, g_norm, gather=()):
    ng = len(gather)

    def body(*refs):
        xs_ref, b_ref, c_ref, dt_ref, z_ref, dtb_ref, al_ref, dsk_ref, gn_ref = refs[:9]
        y_ref, st_ref = refs[9 + ng:11 + ng]
        s_scr, cst_scr = refs[11 + 2 * ng:13 + 2 * ng]
        g = pl.program_id(0)
        c = pl.program_id(1)
        if ng:
            ag_start, ag_forward, ag_finish = _ag_program(refs[9:9 + ng], refs[11 + ng:11 + 2 * ng],
                                                          refs[13 + 2 * ng:])
            pl.when((g == 0) & (c == 0))(ag_start)
            pl.when((g == SSM_GROUPS // 2) & (c == 0))(ag_forward)

        @pl.when(c == 0)
        def _():
            s_scr[...] = jnp.zeros_like(s_scr)

        st_prev = s_scr[...]
        st_ref[0, 0] = st_prev
        f = _ssd_forward(xs_ref[...], z_ref[...], b_ref[...], c_ref[...], dt_ref[...], st_prev, dtb_ref[...],
                         al_ref[...], dsk_ref[...], gn_ref[...], g, cst_scr)
        y_ref[...] = f["out"].astype(BF16)
        s_scr[...] = f["st_new"]
        if ng:
            pl.when((g == SSM_GROUPS - 1) & (c == NB - 1))(ag_finish)

    return pl.pallas_call(
        body, grid=(SSM_GROUPS, NB), in_specs=_ssd_in_specs(False) + [ANY] * ng,
        out_specs=[pl.BlockSpec((BLK, GRP_W), lambda g, c: (c, g)),
                   pl.BlockSpec((1, 1, SSM_STATE, GRP_W), lambda g, c: (g, c, 0, 0))] + [ANY] * ng,
        out_shape=[SDS((T, SSM_INNER), BF16), SDS((SSM_GROUPS, NB, SSM_STATE, GRP_W), F32)]
        + [SDS((N_DEV,) + s.shape, s.dtype) for s in gather],
        scratch_shapes=[pltpu.VMEM((SSM_STATE, GRP_W), F32), pltpu.VMEM((BLK, BLK), F32)]
        + (_ag_scratch(gather) if ng else []),
        compiler_params=_cparams(),
        name="ssd_fwd")(xbc_act, xbc_act, xbc_act, proj, proj, dt_bias, a_log, d_skip, g_norm, *gather)


def _ssd_bwd(xbc_act, proj, dt_bias, a_log, d_skip, g_norm, states, dy, exchange=()):
    chips = exchange
    nc = len(chips)

    def body(*refs):
        xs_ref, b_ref, c_ref, dt_ref, z_ref, dtb_ref, al_ref, dsk_ref, gn_ref, st_ref, dy_ref = refs[:11]
        (dxs_ref, db_ref, dc_ref, ddt_ref, dz_ref, ddtb_ref, dal_ref, ddsk_ref, dgn_ref) = refs[11 + nc:20 + nc]
        ds_scr, cst_scr = refs[20 + 2 * nc:22 + 2 * nc]
        g = pl.program_id(0)
        c = pl.program_id(1)
        if nc:
            ch_start, ch_finish = _direct_program(refs[11:11 + nc], refs[20 + nc:20 + 2 * nc], refs[22 + 2 * nc:])
            pl.when((g == 0) & (c == 0))(ch_start)

        @pl.when(c == 0)
        def _():
            ds_scr[...] = jnp.zeros_like(ds_scr)
            dgn_ref[...] = jnp.zeros_like(dgn_ref)

        @pl.when((c == 0) & (g == 0))
        def _():
            ddtb_ref[...] = jnp.zeros_like(ddtb_ref)
            dal_ref[...] = jnp.zeros_like(dal_ref)
            ddsk_ref[...] = jnp.zeros_like(ddsk_ref)

        x, z, bm, cm, dt_raw, st_prev = xs_ref[...], z_ref[...], b_ref[...], c_ref[...], dt_ref[...], st_ref[0, 0]
        f = _ssd_forward(x, z, bm, cm, dt_raw, st_prev, dtb_ref[...], al_ref[...], dsk_ref[...], gn_ref[...], g,
                         cst_scr)
        d = _ssd_backward(f, x, z, bm, cm, dt_raw, st_prev, dtb_ref[...], gn_ref[...], g, dy_ref[...].astype(F32),
                          ds_scr[...], cst_scr)
        dxs_ref[...] = d["dx"]
        dz_ref[...] = d["dz"].astype(BF16)
        ds_scr[...] = d["dst_prev"]
        db_ref[...] = d["dbm"]
        dc_ref[...] = d["dcm"]
        ddt_ref[...] = d["ddt_raw"]
        dgn_ref[0:1, :] += d["dgn"]
        ddtb_ref[0:1, :] += d["ddtb"]
        dal_ref[0:1, :] += d["dalog"]
        ddsk_ref[0:1, :] += d["ddskip"]
        if nc:
            pl.when((g == SSM_GROUPS - 1) & (c == NB - 1))(ch_finish)

    rc = lambda c: NB - 1 - c
    small = pl.BlockSpec((8, 128), lambda g, c: (0, 0))
    return pl.pallas_call(
        body, grid=(SSM_GROUPS, NB),
        in_specs=_ssd_in_specs(True) + [
            pl.BlockSpec((1, 1, SSM_STATE, GRP_W), lambda g, c: (g, rc(c), 0, 0)),
            pl.BlockSpec((BLK, GRP_W), lambda g, c: (rc(c), g))] + [ANY] * nc,
        out_specs=[pl.BlockSpec((BLK, GRP_W), lambda g, c: (rc(c), g)),
                   pl.BlockSpec((BLK, SSM_STATE), lambda g, c: (rc(c), g)),
                   pl.BlockSpec((BLK, SSM_STATE), lambda g, c: (rc(c), g)),
                   pl.BlockSpec((BLK, 128), lambda g, c: (rc(c), g)),
                   pl.BlockSpec((BLK, GRP_W), lambda g, c: (rc(c), g)),
                   small, small, small,
                   pl.BlockSpec((8, GRP_W), lambda g, c: (0, g))] + [ANY] * nc,
        out_shape=[SDS((T, SSM_INNER), F32), SDS((T, GRP_W), F32), SDS((T, GRP_W), F32), SDS((T, GRP_W), F32),
                   SDS((T, SSM_INNER), BF16), SDS((8, 128), F32), SDS((8, 128), F32), SDS((8, 128), F32),
                   SDS((8, SSM_INNER), F32)] + [SDS(p.shape, p.dtype) for p in chips],
        scratch_shapes=[pltpu.VMEM((SSM_STATE, GRP_W), F32), pltpu.VMEM((BLK, BLK), F32)]
        + (_direct_scratch(chips) if nc else []),
        compiler_params=_cparams(),
        name="ssd_bwd")(xbc_act, xbc_act, xbc_act, proj, proj, dt_bias, a_log, d_skip, g_norm, states, dy, *chips)


POST_R = 272


def _post_a(o, proj, sn, w_att, w_ssm, w_o):
    def body(o_ref, za_ref, ga_ref, gs_ref, sn_ref, wa_ref, ws_ref, wo_ref, a_ref, mg_ref, ya_ref, ys_ref, out_ref):
        a = (o_ref[...] * _silu(za_ref[...])).astype(BF16)
        a_ref[...] = a
        ya = jnp.dot(a, wa_ref[...], preferred_element_type=F32)
        ys = jnp.dot(sn_ref[...], ws_ref[...], preferred_element_type=F32)
        ya_ref[...] = ya.astype(BF16)
        ys_ref[...] = ys.astype(BF16)
        mg = (jax.nn.sigmoid(ga_ref[...]) * ya + jax.nn.sigmoid(gs_ref[...]) * ys).astype(BF16)
        mg_ref[...] = mg
        out_ref[...] = jnp.dot(mg, wo_ref[...], preferred_element_type=F32)

    row = pl.BlockSpec((POST_R, D_MODEL), lambda i: (i, 0))
    pcol = lambda c0: pl.BlockSpec((POST_R, D_MODEL), lambda i: (i, c0 // D_MODEL))
    full = lambda r: pl.BlockSpec((r, D_MODEL), lambda i: (0, 0))
    return pl.pallas_call(
        body, grid=(T // POST_R,),
        in_specs=[row, pcol(C_ZA), pcol(C_GA), pcol(C_GS), pl.BlockSpec((POST_R, SSM_INNER), lambda i: (i, 0)),
                  full(D_MODEL), full(SSM_INNER), full(D_MODEL)],
        out_specs=[row, row, row, row, row],
        out_shape=[SDS((T, D_MODEL), BF16), SDS((T, D_MODEL), BF16), SDS((T, D_MODEL), BF16), SDS((T, D_MODEL), BF16),
                   SDS((T, D_MODEL), F32)],
        compiler_params=_cparams(), name="post_a")(o, proj, proj, proj, sn, w_att, w_ssm, w_o)


def _post_b(out, h, tgt, proj, ya, ys, o, g_post, w_att, w_ssm, w_o):
    def body(out_ref, h_ref, t_ref, za_ref, ga_ref, gs_ref, ya_ref, ys_ref, o_ref, gp_ref, wa_ref, ws_ref, wo_ref,
             loss_ref, dres_ref, dout_ref, dya_ref, dys_ref, dga_ref, dgs_ref, do_ref, dza_ref, dsn_ref, dgp_ref):
        i = pl.program_id(0)
        x = out_ref[...]
        gp = gp_ref[...]
        r = lax.rsqrt(jnp.mean(x * x, axis=-1, keepdims=True) + EPS)
        row = i * POST_R + lax.broadcasted_iota(jnp.int32, (POST_R, 1), 0)
        res = h_ref[...] + jnp.where(row >= PAD, x * r * gp, 0.0)
        live = row >= PAD + N_META
        err = jnp.where(live, res - t_ref[...], 0.0)
        lpart = 0.5 * jnp.sum(jnp.sum(err * err, axis=1, keepdims=True) / D_MODEL, axis=0, keepdims=True)
        dres = err / D_MODEL
        dres_ref[...] = dres
        gpart = jnp.sum(dres * x * r, axis=0, keepdims=True)

        @pl.when(i == 0)
        def _():
            loss_ref[...] = jnp.zeros_like(loss_ref)
            dgp_ref[...] = jnp.zeros_like(dgp_ref)

        loss_ref[...] += jnp.broadcast_to(lpart, loss_ref.shape)
        dgp_ref[0:1, :] += gpart
        gd = gp * dres
        dout = (r * gd - x * (r * r * r) * jnp.mean(x * gd, axis=-1, keepdims=True)).astype(BF16)
        dout_ref[...] = dout
        dmg = lax.dot_general(dout, wo_ref[...], (((1,), (1,)), ((), ())), preferred_element_type=F32)
        sga = jax.nn.sigmoid(ga_ref[...])
        sgs = jax.nn.sigmoid(gs_ref[...])
        dya = (dmg * sga).astype(BF16)
        dys = (dmg * sgs).astype(BF16)
        dya_ref[...] = dya
        dys_ref[...] = dys
        dga_ref[...] = (dmg * ya_ref[...].astype(F32) * sga * (1.0 - sga)).astype(BF16)
        dgs_ref[...] = (dmg * ys_ref[...].astype(F32) * sgs * (1.0 - sgs)).astype(BF16)
        da = lax.dot_general(dya, wa_ref[...], (((1,), (1,)), ((), ())), preferred_element_type=F32)
        za = za_ref[...]
        do_ref[...] = (da * _silu(za)).astype(BF16)
        dza_ref[...] = (da * o_ref[...] * _dsilu(za)).astype(BF16)
        dsn_ref[...] = lax.dot_general(dys, ws_ref[...], (((1,), (1,)), ((), ())),
                                       preferred_element_type=F32).astype(BF16)

    row = pl.BlockSpec((POST_R, D_MODEL), lambda i: (i, 0))
    pcol = lambda c0: pl.BlockSpec((POST_R, D_MODEL), lambda i: (i, c0 // D_MODEL))
    full = lambda r: pl.BlockSpec((r, D_MODEL), lambda i: (0, 0))
    small = pl.BlockSpec((8, D_MODEL), lambda i: (0, 0))
    return pl.pallas_call(
        body, grid=(T // POST_R,),
        in_specs=[row, row, row, pcol(C_ZA), pcol(C_GA), pcol(C_GS), row, row, row,
                  pl.BlockSpec((1, D_MODEL), lambda i: (0, 0)), full(D_MODEL), full(SSM_INNER), full(D_MODEL)],
        out_specs=[pl.BlockSpec((8, 128), lambda i: (0, 0)), row, row, row, row, row, row, row, row,
                   pl.BlockSpec((POST_R, SSM_INNER), lambda i: (i, 0)), small],
        out_shape=[SDS((8, 128), F32), SDS((T, D_MODEL), F32), SDS((T, D_MODEL), BF16), SDS((T, D_MODEL), BF16),
                   SDS((T, D_MODEL), BF16), SDS((T, D_MODEL), BF16), SDS((T, D_MODEL), BF16), SDS((T, D_MODEL), BF16),
                   SDS((T, D_MODEL), BF16), SDS((T, SSM_INNER), BF16), SDS((8, D_MODEL), F32)],
        compiler_params=_cparams(), name="post_b")(out, h, tgt, proj, proj, proj, ya, ys, o, g_post, w_att, w_ssm, w_o)


def _assemble(dq, dza, dga, dgs, dzs, dxx, dxb, dxc, dk, dv, ddt4):
    def body(dq_ref, dza_ref, dga_ref, dgs_ref, dzs_ref, dxx_ref, dxb_ref, dxc_ref, dk_ref, dv_ref, ddt_ref, o_ref):
        o_ref[:, C_Q:C_Q + D_MODEL] = dq_ref[...].astype(BF16)
        o_ref[:, C_ZA:C_ZA + D_MODEL] = dza_ref[...]
        o_ref[:, C_GA:C_GA + D_MODEL] = dga_ref[...]
        o_ref[:, C_GS:C_GS + D_MODEL] = dgs_ref[...]
        o_ref[:, C_ZS:C_ZS + SSM_INNER] = dzs_ref[...]
        o_ref[:, C_XBC:C_XBC + SSM_INNER] = dxx_ref[...]
        o_ref[:, C_XBC + SSM_INNER:C_XBC + SSM_INNER + GRP_W] = dxb_ref[...]
        o_ref[:, C_XBC + SSM_INNER + GRP_W:C_XBC + CONV_DIM] = dxc_ref[...]
        o_ref[:, C_K:C_K + KV_W] = dk_ref[...].astype(BF16)
        o_ref[:, C_V:C_V + KV_W] = dv_ref[...].astype(BF16)
        d4 = ddt_ref[...]
        o_ref[:, C_DT:C_DT + 128] = (d4[:, 0:128] + d4[:, 128:256] + d4[:, 256:384] + d4[:, 384:512]).astype(BF16)

    spec = lambda w: pl.BlockSpec((BLK, w), lambda i: (i, 0))
    ins = [dq, dza, dga, dgs, dzs, dxx, dxb, dxc, dk, dv, ddt4]
    return pl.pallas_call(
        body, grid=(NB,), in_specs=[spec(a.shape[1]) for a in ins], out_specs=spec(PW),
        out_shape=SDS((T, PW), BF16), name="assemble")(*ins)


def _adamw_math(w, g, m, v):
    m = ADAM_B1 * m + (1.0 - ADAM_B1) * g
    v = ADAM_B2 * v + (1.0 - ADAM_B2) * (g * g)
    m_hat = m / (1.0 - ADAM_B1 ** ADAM_STEP)
    v_hat = v / (1.0 - ADAM_B2 ** ADAM_STEP)
    delta = -ADAM_LR * (m_hat / (jnp.sqrt(v_hat) + ADAM_EPS) + ADAM_WD * w)
    return delta, m, v


def _sum_adamw(recv, w, m, v, tc, name):
    rows, cols = w.shape
    nslab = recv.shape[0]
    assert cols % tc == 0

    def body(r_ref, w_ref, m_ref, v_ref, g_ref, d_ref, nm_ref, nv_ref):
        g = r_ref[0].astype(F32)
        for d in range(1, nslab):
            g = g + r_ref[d].astype(F32)
        g_ref[...] = g
        delta, nm, nv = _adamw_math(w_ref[...], g, m_ref[...], v_ref[...])
        d_ref[...] = delta
        nm_ref[...] = nm
        nv_ref[...] = nv

    blk = pl.BlockSpec((rows, tc), lambda i: (0, i))
    return pl.pallas_call(
        body, grid=(cols // tc,),
        in_specs=[pl.BlockSpec((nslab, rows, tc), lambda i: (0, 0, i)), blk, blk, blk],
        out_specs=[blk, blk, blk, blk], out_shape=[SDS((rows, cols), F32)] * 4,
        compiler_params=_cparams(), name=name)(recv, w, m, v)


def _sum_adamw_rows3(recv, w3, m3, v3, name):
    pairs = 61
    assert (SHARD_IN // 2) % pairs == 0

    def body(r_ref, w_ref, m_ref, v_ref, g_ref, d_ref, nm_ref, nv_ref):
        g = r_ref[0].astype(F32)
        for d in range(1, N_CHIP):
            g = g + r_ref[d].astype(F32)
        g = g.reshape(2 * pairs, ROW_TILES, 128)
        g_ref[...] = g
        delta, nm, nv = _adamw_math(w_ref[...], g, m_ref[...], v_ref[...])
        d_ref[...] = delta
        nm_ref[...] = nm
        nv_ref[...] = nv

    blk = pl.BlockSpec((2 * pairs, ROW_TILES, 128), lambda i: (i, 0, 0))
    return pl.pallas_call(
        body, grid=(SHARD_IN // 2 // pairs,),
        in_specs=[pl.BlockSpec((N_CHIP, pairs, 2 * ROW_TILES, 128), lambda i: (0, i, 0, 0)), blk, blk, blk],
        out_specs=[blk, blk, blk, blk], out_shape=[SDS(w3.shape, F32)] * 4,
        compiler_params=_cparams(), name=name)(recv, w3, m3, v3)


ROW_GPRE, ROW_CONVB, ROW_DTB, ROW_ALOG, ROW_DSKIP, ROW_SINK, ROW_GSSM, ROW_GPOST = 0, 1, 4, 5, 6, 7, 8, 10
REP_ROWS, ROW_CONVW, ROW_META, SM_ROWS = 16, 16, 24, 40
CW_SHARD = CONV_DIM // N_DEV
META_SHARD = D_MODEL // N_DEV


def _small_pack(dgpre, dbx, dbb, dbc, ddtb, dal, ddsk, dsink, dgn, dgp, dwx, dwb, dwc, dh):
    def body(dgpre_ref, dbx_ref, dbb_ref, dbc_ref, ddtb_ref, dal_ref, ddsk_ref, dsink_ref, dgn_ref, dgp_ref,
             dwx_ref, dwb_ref, dwc_ref, dh_ref, o_ref, rep):
        rep[...] = jnp.zeros_like(rep)
        rep[ROW_GPRE:ROW_GPRE + 1, :] = dgpre_ref[0:1, :]
        rep[ROW_CONVB:ROW_CONVB + 1, :] = dbx_ref[0:1, 0:1024]
        rep[ROW_CONVB + 1:ROW_CONVB + 2, :] = dbx_ref[0:1, 1024:2048]
        rep[ROW_CONVB + 2:ROW_CONVB + 3, 0:512] = dbb_ref[0:1, :]
        rep[ROW_CONVB + 2:ROW_CONVB + 3, 512:1024] = dbc_ref[0:1, :]
        rep[ROW_DTB:ROW_DTB + 1, 0:128] = ddtb_ref[0:1, :]
        rep[ROW_ALOG:ROW_ALOG + 1, 0:128] = dal_ref[0:1, :]
        rep[ROW_DSKIP:ROW_DSKIP + 1, 0:128] = ddsk_ref[0:1, :]
        rep[ROW_SINK:ROW_SINK + 1, 0:128] = dsink_ref[0:1, :]
        rep[ROW_GSSM:ROW_GSSM + 1, :] = dgn_ref[0:1, 0:1024]
        rep[ROW_GSSM + 1:ROW_GSSM + 2, :] = dgn_ref[0:1, 1024:2048]
        rep[ROW_GPOST:ROW_GPOST + 1, :] = dgp_ref[0:1, :]
        cw = jnp.concatenate([dwx_ref[...], dwb_ref[...], dwc_ref[...]], axis=1)
        mh = dh_ref[...]
        o_ref[...] = jnp.zeros_like(o_ref)
        for p in range(N_DEV):
            o_ref[p, 0:REP_ROWS, :] = rep[...]
            o_ref[p, ROW_CONVW:ROW_CONVW + 8, 0:CW_SHARD] = cw[:, p * CW_SHARD:(p + 1) * CW_SHARD]
            o_ref[p, ROW_META:ROW_META + N_META, 0:META_SHARD] = mh[:, p * META_SHARD:(p + 1) * META_SHARD]

    ins = [dgpre, dbx, dbb, dbc, ddtb, dal, ddsk, dsink, dgn, dgp, dwx, dwb, dwc]
    return pl.pallas_call(
        body, grid=(1,),
        in_specs=[pl.BlockSpec(a.shape, lambda i: (0, 0)) for a in ins]
        + [pl.BlockSpec((N_META, D_MODEL), lambda i: (PAD // N_META, 0))],
        out_specs=pl.BlockSpec((N_DEV, SM_ROWS, 1024), lambda i: (0, 0, 0)),
        out_shape=SDS((N_DEV, SM_ROWS, 1024), F32), scratch_shapes=[pltpu.VMEM((REP_ROWS, 1024), F32)],
        name="small_pack")(*ins, dh)


def _small_finish(recv, params):
    npar = len(params)

    def body(*refs):
        r_ref = refs[0]
        wmv = refs[1:1 + 3 * npar]
        outs = refs[1 + 3 * npar:1 + 7 * npar]
        gs = refs[-1]
        g = r_ref[0]
        for d in range(1, recv.shape[0]):
            g = g + r_ref[d]
        gs[...] = g
        grads = [
            gs[ROW_GPRE:ROW_GPRE + 1, :],
            jnp.concatenate([gs[ROW_CONVB + k:ROW_CONVB + k + 1, :] for k in range(3)], axis=1),
            gs[ROW_DTB:ROW_DTB + 1, 0:SSM_HEADS], gs[ROW_ALOG:ROW_ALOG + 1, 0:SSM_HEADS],
            gs[ROW_DSKIP:ROW_DSKIP + 1, 0:SSM_HEADS], gs[ROW_SINK:ROW_SINK + 1, 0:Q_HEADS],
            jnp.concatenate([gs[ROW_GSSM:ROW_GSSM + 1, :], gs[ROW_GSSM + 1:ROW_GSSM + 2, :]], axis=1),
            gs[ROW_GPOST:ROW_GPOST + 1, :],
            gs[ROW_CONVW:ROW_CONVW + 4, 0:CW_SHARD],
            gs[ROW_META:ROW_META + N_META, 0:META_SHARD]]
        for i in range(npar):
            w_ref, m_ref, v_ref = wmv[3 * i:3 * i + 3]
            delta, nm, nv = _adamw_math(w_ref[...], grads[i], m_ref[...], v_ref[...])
            outs[4 * i][...] = grads[i]
            outs[4 * i + 1][...] = delta
            outs[4 * i + 2][...] = nm
            outs[4 * i + 3][...] = nv

    flat = [a for wmv in params for a in wmv]
    res = pl.pallas_call(
        body, out_shape=[SDS(wmv[0].shape, F32) for wmv in params for _ in range(4)],
        scratch_shapes=[pltpu.VMEM((SM_ROWS, 1024), F32)], name="small_finish")(recv, *flat)
    return [tuple(res[4 * i:4 * i + 4]) for i in range(npar)]


def _slab(ref, px, py, pc):
    return ref.at[4 * px + 2 * py + pc]


def _bounce(src, dst, buf, sem):
    cp = pltpu.make_async_copy(src, buf, sem)
    cp.start()
    cp.wait()
    cp = pltpu.make_async_copy(buf, dst, sem)
    cp.start()
    cp.wait()


def _ag_program(ins, outs, scratch):
    na = len(ins)
    send_sems, recv_sems, local_sems = scratch[:3]
    bufs = scratch[3:]
    x, y, c = lax.axis_index("x"), lax.axis_index("y"), lax.axis_index("c")
    me, sibling = (x, y, c), (x, y, 1 - c)
    chips = [(1 - x, y), (x, 1 - y), (1 - x, 1 - y)]

    def copy(a, k, block, to, src=None):
        dst = _slab(outs[a], *block)
        return pltpu.make_async_remote_copy(
            src_ref=dst if src is None else src, dst_ref=dst, send_sem=send_sems.at[a, k],
            recv_sem=recv_sems.at[a, k], device_id=to, device_id_type=MESH)

    def own_sends():
        out = []
        for a in range(na):
            out.append(copy(a, 0, me, sibling, src=ins[a]))
            out += [copy(a, 1 + j, me, (*chip, c), src=ins[a]) for j, chip in enumerate(chips)]
        return out

    def start():
        for cp in own_sends():
            cp.start()
        for a in range(na):
            _bounce(ins[a], _slab(outs[a], *me), bufs[a], local_sems.at[a])

    def forward():
        for j, chip in enumerate(chips):
            for a in range(na):
                copy(a, 1 + j, (*chip, c), me).wait_recv()
                copy(a, 4 + j, (*chip, c), sibling).start()

    def finish():
        for a in range(na):
            copy(a, 0, sibling, me).wait_recv()
            for j, chip in enumerate(chips):
                copy(a, 4 + j, (*chip, 1 - c), me).wait_recv()
        for cp in own_sends():
            cp.wait_send()
        for j, chip in enumerate(chips):
            for a in range(na):
                copy(a, 4 + j, (*chip, c), sibling).wait_send()

    return start, forward, finish


def _ag_scratch(shards):
    na = len(shards)
    return [pltpu.SemaphoreType.DMA((na, 7)), pltpu.SemaphoreType.DMA((na, 7)),
            pltpu.SemaphoreType.DMA((na,))] + [pltpu.VMEM(s.shape, s.dtype) for s in shards]


def _all_gather(shards):
    na = len(shards)

    def body(*refs):
        start, forward, finish = _ag_program(refs[:na], refs[na:2 * na], refs[2 * na:])
        start()
        forward()
        finish()

    return pl.pallas_call(
        body, in_specs=[ANY] * na, out_specs=[ANY] * na,
        out_shape=[SDS((N_DEV,) + s.shape, s.dtype) for s in shards],
        scratch_shapes=_ag_scratch(shards), name="all_gather")(*shards)


N_CHIP = 4


def _exchange_pair(parts, name):
    na = len(parts)

    def body(*refs):
        ins, own, got = refs[:na], refs[na:2 * na], refs[2 * na:3 * na]
        send_sems, recv_sems, local_sems = refs[3 * na:3 * na + 3]
        bufs = refs[3 * na + 3:]
        x, y, c = lax.axis_index("x"), lax.axis_index("y"), lax.axis_index("c")
        sibling = (x, y, 1 - c)
        sent = []
        for a in range(na):
            for k in range(N_CHIP):
                cp = pltpu.make_async_remote_copy(
                    src_ref=ins[a].at[2 * k + 1 - c], dst_ref=got[a].at[k], send_sem=send_sems.at[a, k],
                    recv_sem=recv_sems.at[a, k], device_id=sibling, device_id_type=MESH)
                cp.start()
                sent.append(cp)
        for a in range(na):
            for k in range(N_CHIP):
                _bounce(ins[a].at[2 * k + c], own[a].at[k], bufs[a], local_sems.at[a])
        for cp in sent:
            cp.wait()

    half = [SDS((N_CHIP,) + p.shape[1:], p.dtype) for p in parts]
    res = pl.pallas_call(
        body, in_specs=[ANY] * na, out_specs=[ANY] * (2 * na), out_shape=half + half,
        scratch_shapes=[pltpu.SemaphoreType.DMA((na, N_CHIP)), pltpu.SemaphoreType.DMA((na, N_CHIP)),
                        pltpu.SemaphoreType.DMA((na,))] + [pltpu.VMEM(p.shape[1:], p.dtype) for p in parts],
        name=name)(*parts)
    return res[:na], res[na:]


def _pair_sum(own, got, name):
    na = len(own)

    def body(*refs):
        for a in range(na):
            o_ref, g_ref, s_ref = refs[a], refs[na + a], refs[2 * na + a]
            s_ref[...] = (o_ref[...].astype(F32) + g_ref[...].astype(F32)).astype(s_ref.dtype)

    def spec(p):
        nd = len(p.shape) - 1
        return pl.BlockSpec((1,) + p.shape[1:], lambda k, nd=nd: (k,) + (0,) * nd)

    return pl.pallas_call(
        body, grid=(N_CHIP,), in_specs=[spec(p) for p in own] + [spec(p) for p in got],
        out_specs=[spec(p) for p in own], out_shape=[SDS(p.shape, p.dtype) for p in own],
        compiler_params=_cparams(), name=name)(*own, *got)


def _chips_program(ins, outs, scratch):
    na = len(ins)
    send_sems, recv_sems, local_sems = scratch[:3]
    bufs = scratch[3:]
    x, y, c = lax.axis_index("x"), lax.axis_index("y"), lax.axis_index("c")
    mine = 2 * x + y
    chips = [(1 - x, y), (x, 1 - y), (1 - x, 1 - y)]

    def send(a, j):
        px, py = chips[j]
        return pltpu.make_async_remote_copy(
            src_ref=ins[a].at[2 * px + py], dst_ref=outs[a].at[mine], send_sem=send_sems.at[a, j],
            recv_sem=recv_sems.at[a, j], device_id=(px, py, c), device_id_type=MESH)

    def arrival(a, j):
        px, py = chips[j]
        return pltpu.make_async_remote_copy(
            src_ref=ins[a].at[2 * px + py], dst_ref=outs[a].at[2 * px + py], send_sem=send_sems.at[a, j],
            recv_sem=recv_sems.at[a, j], device_id=(px, py, c), device_id_type=MESH)

    def start():
        for a in range(na):
            for j in range(3):
                send(a, j).start()
        for a in range(na):
            _bounce(ins[a].at[mine], outs[a].at[mine], bufs[a], local_sems.at[a])

    def finish():
        for a in range(na):
            for j in range(3):
                arrival(a, j).wait_recv()
        for a in range(na):
            for j in range(3):
                send(a, j).wait_send()

    return start, finish


def _chips_scratch(parts):
    na = len(parts)
    return [pltpu.SemaphoreType.DMA((na, 3)), pltpu.SemaphoreType.DMA((na, 3)),
            pltpu.SemaphoreType.DMA((na,))] + [pltpu.VMEM(p.shape[1:], p.dtype) for p in parts]


def _direct_program(ins, outs, scratch):
    na = len(ins)
    send_sems, recv_sems, local_sems = scratch[:3]
    bufs = scratch[3:]
    x, y, c = lax.axis_index("x"), lax.axis_index("y"), lax.axis_index("c")
    me = (x, y, c)
    peers = []
    for k in range(1, N_DEV):
        dx, dy, dc = (k >> 2) & 1, (k >> 1) & 1, k & 1
        peers.append(((1 - x) if dx else x, (1 - y) if dy else y, (1 - c) if dc else c))

    def send(a, k):
        return pltpu.make_async_remote_copy(
            src_ref=_slab(ins[a], *peers[k]), dst_ref=_slab(outs[a], *me), send_sem=send_sems.at[a, k],
            recv_sem=recv_sems.at[a, k], device_id=peers[k], device_id_type=MESH)

    def arrival(a, k):
        return pltpu.make_async_remote_copy(
            src_ref=_slab(ins[a], *peers[k]), dst_ref=_slab(outs[a], *peers[k]), send_sem=send_sems.at[a, k],
            recv_sem=recv_sems.at[a, k], device_id=peers[k], device_id_type=MESH)

    def start():
        for a in range(na):
            for k in range(N_DEV - 1):
                send(a, k).start()
        for a in range(na):
            _bounce(_slab(ins[a], *me), _slab(outs[a], *me), bufs[a], local_sems.at[a])

    def finish():
        for a in range(na):
            for k in range(N_DEV - 1):
                arrival(a, k).wait_recv()
        for a in range(na):
            for k in range(N_DEV - 1):
                send(a, k).wait_send()

    return start, finish


def _direct_scratch(parts):
    na = len(parts)
    return [pltpu.SemaphoreType.DMA((na, N_DEV - 1)), pltpu.SemaphoreType.DMA((na, N_DEV - 1)),
            pltpu.SemaphoreType.DMA((na,))] + [pltpu.VMEM(p.shape[1:], p.dtype) for p in parts]


def _exchange_direct(parts, name):
    na = len(parts)

    def body(*refs):
        start, finish = _direct_program(refs[:na], refs[na:2 * na], refs[2 * na:])
        start()
        finish()

    return pl.pallas_call(
        body, in_specs=[ANY] * na, out_specs=[ANY] * na, out_shape=[SDS(p.shape, p.dtype) for p in parts],
        scratch_shapes=_direct_scratch(parts), name=name)(*parts)


def _exchange_chips(parts, name):
    na = len(parts)

    def body(*refs):
        start, finish = _chips_program(refs[:na], refs[na:2 * na], refs[2 * na:])
        start()
        finish()

    return pl.pallas_call(
        body, in_specs=[ANY] * na, out_specs=[ANY] * na, out_shape=[SDS(p.shape, p.dtype) for p in parts],
        scratch_shapes=_chips_scratch(parts), name=name)(*parts)


ROW_TILES = D_MODEL // 128


def _rows3(t):
    return jnp.transpose(t[0]).reshape(t.shape[2], ROW_TILES, 128)


def _unrows3(t):
    return jnp.transpose(t.reshape(t.shape[0], D_MODEL))[None]


def _cast_shards(w_in3, w_att, w_ssm, w_o):
    def body(wi_ref, wa_ref, ws_ref, wo_ref, a_ref, b_ref, c_ref, d_ref):
        a_ref[...] = wi_ref[...].reshape(SHARD_IN // 2, 2 * ROW_TILES, 128).astype(BF16)
        b_ref[...] = wa_ref[...].astype(BF16)
        c_ref[...] = ws_ref[...].astype(BF16)
        d_ref[...] = wo_ref[...].astype(BF16)

    return pl.pallas_call(
        body, out_shape=[SDS((SHARD_IN // 2, 2 * ROW_TILES, 128), BF16), SDS(w_att.shape, BF16),
                         SDS(w_ssm.shape, BF16), SDS(w_o.shape, BF16)],
        compiler_params=_cparams(), name="cast_shards")(w_in3, w_att, w_ssm, w_o)


def _pieces():
    out = []
    for r0, c0, w in _SEGS:
        r = r0
        while r < r0 + w:
            d = r // SHARD_IN
            n = min(r0 + w, (d + 1) * SHARD_IN) - r
            out.append((c0 + (r - r0), d, r - d * SHARD_IN, n))
            r += n
    return out


def _to_aligned_t(slabs):
    def body(a_ref, o_ref):
        for (t, d, s, n) in _pieces():
            o_ref[t:t + n, :] = a_ref[d, s // 2:(s + n) // 2].reshape(n, D_MODEL)
        o_ref[C_DT + 32:C_DT + 128, :] = jnp.zeros((96, D_MODEL), slabs.dtype)

    return pl.pallas_call(body, out_shape=SDS((PW, D_MODEL), slabs.dtype), compiler_params=_cparams(),
                          name="to_aligned")(slabs)


def _from_aligned_t(g):
    def body(g_ref, o_ref):
        for (t, d, s, n) in _pieces():
            o_ref[d, s // 2:(s + n) // 2] = g_ref[t:t + n, :].reshape(n // 2, 2 * ROW_TILES, 128)

    return pl.pallas_call(body, out_shape=SDS((N_DEV, SHARD_IN // 2, 2 * ROW_TILES, 128), g.dtype),
                          compiler_params=_cparams(), name="from_aligned")(g)


_SEGS = [
    (R_Q, C_Q, 1024), (R_K, C_K, 256), (R_V, C_V, 256), (R_ZA, C_ZA, 1024), (R_ZS, C_ZS, 2048),
    (R_XBC, C_XBC, 3072), (R_DT, C_DT, 32), (R_GA, C_GA, 1024), (R_GS, C_GS, 1024)]


def _pad_lanes(v, n=128):
    return jnp.pad(v, ((0, 0), (0, n - v.shape[1])))


def _reduce_pair(parts, tag):
    own, got = _exchange_pair(parts, "exchange_pair_" + tag)
    return _pair_sum(own, got, "pair_sum_" + tag)


def _device_step(h, tgt, w_alt, w_out, g_pre, conv_w8, conv_b, dt_bias, a_log, d_skip, sinks, g_ssm, g_post, on_mesh):
    dtb, al, dsk, snk = _pad_lanes(dt_bias), _pad_lanes(a_log), _pad_lanes(d_skip), _pad_lanes(sinks)
    u = _norm_u(h, g_pre)
    proj = _matmul(u, w_alt, "nt", F32, T, 896, D_MODEL, "in_proj")
    o = _attn_fwd(proj, snk)
    xbc_act = _conv_fwd(proj, conv_w8, conv_b)
    if on_mesh:
        sn, states, att_all, ssm_all, o_all = _ssd_fwd(xbc_act, proj, dtb, al, dsk, g_ssm, gather=w_out)
        w_att = att_all.reshape(D_MODEL, D_MODEL)
        w_ssm = ssm_all.reshape(SSM_INNER, D_MODEL)
        w_o = o_all.reshape(D_MODEL, D_MODEL)
    else:
        sn, states = _ssd_fwd(xbc_act, proj, dtb, al, dsk, g_ssm)
        w_att, w_ssm, w_o = w_out
    a_in, mg, ya, ys, out = _post_a(o, proj, sn, w_att, w_ssm, w_o)
    (loss, dres, dout, dya, dys, dga, dgs, do, dza, dsn, dgp) = _post_b(
        out, h, tgt, proj, ya, ys, o, g_post, w_att, w_ssm, w_o)
    dw_att = _matmul(a_in, dya, "tn", BF16, D_MODEL, D_MODEL, T, "d_w_att")
    dw_ssm = _matmul(sn, dys, "tn", BF16, D_MODEL, D_MODEL, T, "d_w_ssm")
    dw_o = _matmul(mg, dout, "tn", BF16, D_MODEL, D_MODEL, T, "d_w_o")
    res = {}
    if on_mesh:
        parts = [dw_att.reshape(N_DEV, 128, D_MODEL), dw_ssm.reshape(N_DEV, 256, D_MODEL),
                 dw_o.reshape(N_DEV, 128, D_MODEL)]
        (dxs, dbm, dcm, ddt4, dzs, ddtb, dal, ddsk, dgn, res["r_att"], res["r_ssm"], res["r_o"]) = _ssd_bwd(
            xbc_act, proj, dtb, al, dsk, g_ssm, states, dsn, exchange=parts)
    else:
        dxs, dbm, dcm, ddt4, dzs, ddtb, dal, ddsk, dgn = _ssd_bwd(xbc_act, proj, dtb, al, dsk, g_ssm, states, dsn)
        res.update(dw_att=dw_att, dw_ssm=dw_ssm, dw_o=dw_o)
    dxx, dwx, dbx = _conv_bwd(proj, conv_w8, conv_b, dxs, 0, "conv_bwd_x")
    dxb, dwb, dbb = _conv_bwd(proj, conv_w8, conv_b, dbm, SSM_INNER, "conv_bwd_b")
    dxc, dwc, dbc = _conv_bwd(proj, conv_w8, conv_b, dcm, SSM_INNER + GRP_W, "conv_bwd_c")
    dq, dk, dv, dsink = _attn_bwd(proj, snk, do)
    dproj = _assemble(dq, dza, dga, dgs, dzs, dxx, dxb, dxc, dk, dv, ddt4)
    dw_alt = _matmul(dproj, u, "tn", BF16, 896, D_MODEL, T, "d_w_in")
    if on_mesh:
        sums = _reduce_pair([_from_aligned_t(dw_alt)], "b")
        du, res["r_in"] = _matmul(dproj, w_alt, "nn", F32, T, D_MODEL, 1408, "d_u", chips=sums)
    else:
        du = _matmul(dproj, w_alt, "nn", F32, T, D_MODEL, 1408, "d_u")
        res["dw_alt"] = dw_alt
    dh, dgpre = _norm_bwd(h, g_pre, du, dres)
    small = (dgpre, dbx, dbb, dbc, ddtb, dal, ddsk, dsink, dgn, dgp, dwx, dwb, dwc)
    if on_mesh:
        (res["r_small"],) = _exchange_direct([_small_pack(*small, dh)], "exchange_small")
    else:
        res["small"] = small
    res.update(loss=loss[0, 0], dh=dh)
    return res


def kernel(x, meta_tokens, g_pre, w_in, conv_w, conv_b, dt_bias, a_log, d_skip, attn_sinks, g_ssm_norm, w_out_att, w_out_ssm, w_out, g_post, loss_target, m_meta_tokens, m_g_pre, m_w_in, m_conv_w, m_conv_b, m_dt_bias, m_a_log, m_d_skip, m_attn_sinks, m_g_ssm_norm, m_w_out_att, m_w_out_ssm, m_w_out, m_g_post, v_meta_tokens, v_g_pre, v_w_in, v_conv_w, v_conv_b, v_dt_bias, v_a_log, v_d_skip, v_attn_sinks, v_g_ssm_norm, v_w_out_att, v_w_out_ssm, v_w_out, v_g_post):
    w_in3, m_in3, v_in3 = _rows3(w_in), _rows3(m_w_in), _rows3(v_w_in)
    a_sh, att_sh, ssm_sh, o_sh = _cast_shards(w_in3, w_out_att[0], w_out_ssm[0], w_out[0])
    cw_sh = jnp.pad(conv_w[0], ((0, 4), (0, 0)))
    a_all, meta_all, cw_all = _all_gather([a_sh, meta_tokens, cw_sh])
    w_alt = _to_aligned_t(a_all)
    meta_full = meta_all.transpose(1, 0, 2).reshape(N_META, D_MODEL)
    conv_w8 = cw_all.transpose(1, 0, 2).reshape(8, CONV_DIM)

    h = jnp.concatenate([jnp.zeros((PAD, D_MODEL), F32), meta_full, x[0]], axis=0)
    tgt = jnp.concatenate([jnp.zeros((PAD + N_META, D_MODEL), F32), loss_target[0]], axis=0)
    r = _device_step(h, tgt, w_alt, (att_sh, ssm_sh, o_sh), g_pre, conv_w8, conv_b, dt_bias, a_log, d_skip,
                     attn_sinks, g_ssm_norm, g_post, True)
    loss = lax.psum(r["loss"], ("x", "y", "c"))
    grad_x = r["dh"][PAD + N_META:][None]

    res_in = [_unrows3(t) for t in _sum_adamw_rows3(r["r_in"], w_in3, m_in3, v_in3, "adamw_w_in")]
    res_att = [t[None] for t in _sum_adamw(r["r_att"], w_out_att[0], m_w_out_att[0], v_w_out_att[0], 512,
                                           "adamw_w_att")]
    res_ssm = [t[None] for t in _sum_adamw(r["r_ssm"], w_out_ssm[0], m_w_out_ssm[0], v_w_out_ssm[0], 512,
                                           "adamw_w_ssm")]
    res_o = [t[None] for t in _sum_adamw(r["r_o"], w_out[0], m_w_out[0], v_w_out[0], 512, "adamw_w_o")]
    (res_gpre, res_convb, res_dtb, res_alog, res_dskip, res_sink, res_gssm, res_gpost, res_cw, res_meta) = _small_finish(
        r["r_small"], [(g_pre, m_g_pre, v_g_pre), (conv_b, m_conv_b, v_conv_b), (dt_bias, m_dt_bias, v_dt_bias),
                       (a_log, m_a_log, v_a_log), (d_skip, m_d_skip, v_d_skip),
                       (attn_sinks, m_attn_sinks, v_attn_sinks), (g_ssm_norm, m_g_ssm_norm, v_g_ssm_norm),
                       (g_post, m_g_post, v_g_post), (conv_w[0], m_conv_w[0], v_conv_w[0]),
                       (meta_tokens, m_meta_tokens, v_meta_tokens)])
    res_cw = [t[None] for t in res_cw]
    per_weight = [res_meta, res_gpre, res_in, res_cw, res_convb, res_dtb, res_alog, res_dskip, res_sink, res_gssm,
                  res_att, res_ssm, res_o, res_gpost]
    return (loss, grad_x, *[p[0] for p in per_weight], *[p[1] for p in per_weight], *[p[2] for p in per_weight],
            *[p[3] for p in per_weight])
```

```python
import functools
import math

import jax
import jax.numpy as jnp
from jax import lax
from jax.experimental import pallas as pl
from jax.experimental.pallas import tpu as pltpu

F32 = jnp.float32
BF16 = jnp.bfloat16
SDS = jax.ShapeDtypeStruct
HI = lax.Precision.HIGHEST
MESH = pl.DeviceIdType.MESH
ANY = pl.BlockSpec(memory_space=pl.ANY)

N_DEV = 8
D_MODEL = 1024
SEQ = 2048
N_META = 16
BLK = 128
PAD = 112
T = PAD + N_META + SEQ
NB = T // BLK
EPS = 1e-6
HEAD = 64
Q_HEADS = 16
KV_HEADS = 4
GROUP = 4
KV_W = 256
SSM_INNER = 2048
SSM_HEADS = 32
SSM_GROUPS = 4
GRP_W = 512
SSM_STATE = 128
CONV_DIM = 3072
IN_PROJ = 9760
SHARD_IN = IN_PROJ // N_DEV
NEG = -1e30

C_Q, C_ZA, C_GA, C_GS, C_ZS, C_XBC, C_K, C_V, C_DT = 0, 1024, 2048, 3072, 4096, 6144, 9216, 9472, 9728
PW = 9856
R_Q, R_K, R_V, R_ZA, R_ZS, R_XBC, R_DT, R_GA, R_GS = 0, 1024, 1280, 1536, 2560, 4608, 7680, 7712, 8736

ADAM_LR, ADAM_B1, ADAM_B2, ADAM_EPS, ADAM_WD, ADAM_STEP = 0.001, 0.9, 0.999, 1e-08, 0.01, 10

VMEM_LIMIT = 56 * 1024 * 1024


def _cparams():
    return pltpu.CompilerParams(vmem_limit_bytes=VMEM_LIMIT)


def _silu(x):
    return x * jax.nn.sigmoid(x)


def _dsilu(x):
    s = jax.nn.sigmoid(x)
    return s * (1.0 + x * (1.0 - s))


def _matmul(a, b, mode, out_dtype, tm, tn, tk, name, chips=()):
    if mode == "nn":
        (m, k), n = a.shape, b.shape[1]
        a_spec = pl.BlockSpec((tm, tk), lambda i, j, kk: (i, kk))
        b_spec = pl.BlockSpec((tk, tn), lambda i, j, kk: (kk, j))
        dims = (((1,), (0,)), ((), ()))
    elif mode == "nt":
        (m, k), n = a.shape, b.shape[0]
        a_spec = pl.BlockSpec((tm, tk), lambda i, j, kk: (i, kk))
        b_spec = pl.BlockSpec((tn, tk), lambda i, j, kk: (j, kk))
        dims = (((1,), (1,)), ((), ()))
    else:
        (k, m), n = a.shape, b.shape[1]
        a_spec = pl.BlockSpec((tk, tm), lambda i, j, kk: (kk, i))
        b_spec = pl.BlockSpec((tk, tn), lambda i, j, kk: (kk, j))
        dims = (((0,), (0,)), ((), ()))
    assert m % tm == 0 and n % tn == 0 and k % tk == 0, (a.shape, b.shape, tm, tn, tk)
    nk = k // tk
    nc = len(chips)
    grid = (m // tm, n // tn, nk)

    def body(*refs):
        a_ref, b_ref = refs[:2]
        o_ref = refs[2 + nc]
        scratch = refs[3 + 2 * nc:]
        i, j, kk = pl.program_id(0), pl.program_id(1), pl.program_id(2)
        if nc:
            ch_start, ch_finish = _chips_program(refs[2:2 + nc], refs[3 + nc:3 + 2 * nc], scratch[0 if nk == 1 else 1:])
            pl.when((i == 0) & (j == 0) & (kk == 0))(ch_start)
        part = lax.dot_general(a_ref[...], b_ref[...], dims, preferred_element_type=F32)
        if nk == 1:
            o_ref[...] = part.astype(out_dtype)
        else:
            acc_ref = scratch[0]

            @pl.when(kk == 0)
            def _():
                acc_ref[...] = part

            @pl.when((kk > 0) & (kk < nk - 1))
            def _():
                acc_ref[...] += part

            @pl.when(kk == nk - 1)
            def _():
                o_ref[...] = (acc_ref[...] + part).astype(out_dtype)
        if nc:
            pl.when((i == grid[0] - 1) & (j == grid[1] - 1) & (kk == nk - 1))(ch_finish)

    res = pl.pallas_call(
        body, grid=grid, in_specs=[a_spec, b_spec] + [ANY] * nc,
        out_specs=[pl.BlockSpec((tm, tn), lambda i, j, kk: (i, j))] + [ANY] * nc,
        out_shape=[SDS((m, n), out_dtype)] + [SDS(p.shape, p.dtype) for p in chips],
        scratch_shapes=([] if nk == 1 else [pltpu.VMEM((tm, tn), F32)]) + (_chips_scratch(chips) if nc else []),
        compiler_params=_cparams(), name=name)(a, b, *chips)
    return res if nc else res[0]


def _norm_u(h, g_pre):
    def body(h_ref, g_ref, u_ref):
        x = h_ref[...]
        r = lax.rsqrt(jnp.mean(x * x, axis=-1, keepdims=True) + EPS)
        u_ref[...] = (x * r * g_ref[...]).astype(BF16)

    return pl.pallas_call(
        body, grid=(NB,),
        in_specs=[pl.BlockSpec((BLK, D_MODEL), lambda i: (i, 0)), pl.BlockSpec((1, D_MODEL), lambda i: (0, 0))],
        out_specs=pl.BlockSpec((BLK, D_MODEL), lambda i: (i, 0)),
        out_shape=SDS((T, D_MODEL), BF16), name="norm_u")(h, g_pre)


DU_TM, DU_TK = T // 2, 1408


def _d_u_norm(dproj, w_alt, h, g_pre, dres, chips=()):
    nk = PW // DU_TK
    ni = T // DU_TM
    nc = len(chips)

    def body(*refs):
        a_ref, b_ref, h_ref, g_ref, dres_ref = refs[:5]
        dh_ref, dg_ref = refs[5 + nc:7 + nc]
        acc_ref = refs[7 + 2 * nc]
        i, kk = pl.program_id(0), pl.program_id(1)
        if nc:
            ch_start, ch_finish = _chips_program(refs[5:5 + nc], refs[7 + nc:7 + 2 * nc], refs[8 + 2 * nc:])
            pl.when((i == 0) & (kk == 0))(ch_start)
        part = jnp.dot(a_ref[...], b_ref[...], preferred_element_type=F32)

        @pl.when(kk == 0)
        def _():
            acc_ref[...] = part

        @pl.when((kk > 0) & (kk < nk - 1))
        def _():
            acc_ref[...] += part

        @pl.when(kk == nk - 1)
        def _():
            du_ = acc_ref[...] + part
            x = h_ref[...]
            r = lax.rsqrt(jnp.mean(x * x, axis=-1, keepdims=True) + EPS)
            gd = g_ref[...] * du_
            dx = r * gd - x * (r * r * r) * jnp.mean(x * gd, axis=-1, keepdims=True)
            dh_ref[...] = dx + dres_ref[...]
            gpart = jnp.concatenate([jnp.sum(du_ * x * r, axis=0, keepdims=True), jnp.zeros((7, D_MODEL), F32)],
                                    axis=0)

            @pl.when(i == 0)
            def _():
                dg_ref[...] = gpart

            @pl.when(i > 0)
            def _():
                dg_ref[...] += gpart

        if nc:
            pl.when((i == ni - 1) & (kk == nk - 1))(ch_finish)

    row = pl.BlockSpec((DU_TM, D_MODEL), lambda i, kk: (i, 0))
    return pl.pallas_call(
        body, grid=(ni, nk),
        in_specs=[pl.BlockSpec((DU_TM, DU_TK), lambda i, kk: (i, kk)),
                  pl.BlockSpec((DU_TK, D_MODEL), lambda i, kk: (kk, 0)),
                  row, pl.BlockSpec((1, D_MODEL), lambda i, kk: (0, 0)), row] + [ANY] * nc,
        out_specs=[row, pl.BlockSpec((8, D_MODEL), lambda i, kk: (0, 0))] + [ANY] * nc,
        out_shape=[SDS((T, D_MODEL), F32), SDS((8, D_MODEL), F32)] + [SDS(p.shape, p.dtype) for p in chips],
        scratch_shapes=[pltpu.VMEM((DU_TM, D_MODEL), F32)] + (_chips_scratch(chips) if nc else []),
        compiler_params=_cparams(), name="d_u_norm")(dproj, w_alt, h, g_pre, dres, *chips)


def _lane_pick(row, h):
    lane = lax.broadcasted_iota(jnp.int32, row.shape, 1)
    return jnp.sum(jnp.where(lane == h, row, 0.0), axis=1, keepdims=True)


def _attn_fn(q4s, kcats, vcats, kms, vms, sinks, n):
    r = lax.broadcasted_iota(jnp.int32, (GROUP * BLK, 2 * BLK), 0)
    s = lax.broadcasted_iota(jnp.int32, (GROUP * BLK, 2 * BLK), 1)
    i = jnp.bitwise_and(r, BLK - 1)
    gi = jnp.right_shift(r, 7)
    rel = i - s + BLK
    k_pos = n * BLK - BLK + s
    band_ok = (rel >= 0) & (rel < BLK) & (k_pos >= PAD + N_META)
    relf = rel.astype(F32)
    rm = lax.broadcasted_iota(jnp.int32, (GROUP * BLK, N_META), 0)
    mm = lax.broadcasted_iota(jnp.int32, (GROUP * BLK, N_META), 1)
    meta_ok = (PAD + mm) <= (n * BLK + jnp.bitwise_and(rm, BLK - 1))
    gcol = jnp.right_shift(lax.broadcasted_iota(jnp.int32, (GROUP * BLK, 1), 0), 7)
    outs = []
    for kh in range(KV_HEADS):
        slopes = [2.0 ** (-8.0 * (kh * GROUP + g + 1) / Q_HEADS) for g in range(GROUP)]
        slope = jnp.where(gi == 0, slopes[0], jnp.where(gi == 1, slopes[1], jnp.where(gi == 2, slopes[2], slopes[3])))
        sk = [_lane_pick(sinks, kh * GROUP + g) for g in range(GROUP)]
        sink = jnp.where(gcol == 0, sk[0], jnp.where(gcol == 1, sk[1], jnp.where(gcol == 2, sk[2], sk[3])))
        qb = (q4s[kh] * (HEAD ** -0.5)).astype(BF16)
        sb = lax.dot_general(qb, kcats[kh].astype(BF16), (((1,), (1,)), ((), ())), preferred_element_type=F32)
        sb = jnp.where(band_ok, sb - slope * relf, NEG)
        sm = lax.dot_general(qb, kms[kh].astype(BF16), (((1,), (1,)), ((), ())), preferred_element_type=F32)
        sm = jnp.where(meta_ok, sm, NEG)
        mx = jnp.maximum(jnp.maximum(jnp.max(sb, axis=1, keepdims=True), jnp.max(sm, axis=1, keepdims=True)), sink)
        mx = lax.stop_gradient(mx)
        eb = jnp.exp(sb - mx)
        em = jnp.exp(sm - mx)
        es = jnp.exp(sink - mx)
        inv = 1.0 / (jnp.sum(eb, axis=1, keepdims=True) + jnp.sum(em, axis=1, keepdims=True) + es)
        pb = (eb * inv).astype(BF16)
        pm = (em * inv).astype(BF16)
        o4 = (jnp.dot(pm, vms[kh].astype(BF16), preferred_element_type=F32)
              + jnp.dot(pb, vcats[kh].astype(BF16), preferred_element_type=F32))
        outs.append(o4)
    return outs


def _attn_specs():
    prev = lambda n: jnp.maximum(n - 1, 0)
    return [
        pl.BlockSpec((BLK, D_MODEL), lambda n: (n, C_Q // D_MODEL)),
        pl.BlockSpec((BLK, KV_W), lambda n: (prev(n), C_K // KV_W)),
        pl.BlockSpec((BLK, KV_W), lambda n: (n, C_K // KV_W)),
        pl.BlockSpec((BLK, KV_W), lambda n: (prev(n), C_V // KV_W)),
        pl.BlockSpec((BLK, KV_W), lambda n: (n, C_V // KV_W)),
        pl.BlockSpec((N_META, KV_W), lambda n: (PAD // N_META, C_K // KV_W)),
        pl.BlockSpec((N_META, KV_W), lambda n: (PAD // N_META, C_V // KV_W)),
        pl.BlockSpec((1, 128), lambda n: (0, 0)),
    ]


def _attn_load(q_ref, kp_ref, kc_ref, vp_ref, vc_ref, km_ref, vm_ref):
    q4s, kcats, vcats, kms, vms = [], [], [], [], []
    for kh in range(KV_HEADS):
        q4s.append(jnp.concatenate(
            [q_ref[:, (kh * GROUP + g) * HEAD:(kh * GROUP + g + 1) * HEAD] for g in range(GROUP)], axis=0))
        cs = slice(kh * HEAD, (kh + 1) * HEAD)
        kcats.append(jnp.concatenate([kp_ref[:, cs], kc_ref[:, cs]], axis=0))
        vcats.append(jnp.concatenate([vp_ref[:, cs], vc_ref[:, cs]], axis=0))
        kms.append(km_ref[:, cs])
        vms.append(vm_ref[:, cs])
    return q4s, kcats, vcats, kms, vms


def _attn_fwd(proj, sinks):
    def body(q_ref, kp_ref, kc_ref, vp_ref, vc_ref, km_ref, vm_ref, s_ref, o_ref):
        n = pl.program_id(0)
        args = _attn_load(q_ref, kp_ref, kc_ref, vp_ref, vc_ref, km_ref, vm_ref)
        outs = _attn_fn(*args, s_ref[...], n)
        for kh in range(KV_HEADS):
            for g in range(GROUP):
                hh = kh * GROUP + g
                o_ref[:, hh * HEAD:(hh + 1) * HEAD] = outs[kh][g * BLK:(g + 1) * BLK]

    return pl.pallas_call(
        body, grid=(NB,), in_specs=_attn_specs(),
        out_specs=pl.BlockSpec((BLK, D_MODEL), lambda n: (n, 0)),
        out_shape=SDS((T, D_MODEL), F32), name="attn_fwd")(proj, proj, proj, proj, proj, proj, proj, sinks)


def _attn_bwd(proj, sinks, do):
    def body(q_ref, kp_ref, kc_ref, vp_ref, vc_ref, km_ref, vm_ref, s_ref, do_ref, dq_ref, dk_ref, dv_ref, ds_ref):
        n = pl.program_id(0)

        @pl.when(n == 0)
        def _():
            dk_ref[...] = jnp.zeros_like(dk_ref)
            dv_ref[...] = jnp.zeros_like(dv_ref)
            ds_ref[...] = jnp.zeros_like(ds_ref)

        args = _attn_load(q_ref, kp_ref, kc_ref, vp_ref, vc_ref, km_ref, vm_ref)
        _, vjp = jax.vjp(lambda a, b, c, d, e, f: _attn_fn(a, b, c, d, e, f, n), *args, s_ref[...])
        do_f = do_ref[...].astype(F32)
        cot = [jnp.concatenate([do_f[:, (kh * GROUP + g) * HEAD:(kh * GROUP + g + 1) * HEAD] for g in range(GROUP)],
                               axis=0) for kh in range(KV_HEADS)]
        dq4s, dkcats, dvcats, dkms, dvms, dsk = vjp(cot)
        ds_ref[0:1, :] += dsk
        cur = pl.ds(pl.multiple_of(n * BLK, BLK), BLK)
        meta = slice(PAD, PAD + N_META)
        for kh in range(KV_HEADS):
            cs = slice(kh * HEAD, (kh + 1) * HEAD)
            for g in range(GROUP):
                hh = kh * GROUP + g
                dq_ref[:, hh * HEAD:(hh + 1) * HEAD] = dq4s[kh][g * BLK:(g + 1) * BLK]
            dk_ref[cur, cs] += dkcats[kh][BLK:]
            dv_ref[cur, cs] += dvcats[kh][BLK:]
            dk_ref[meta, cs] += dkms[kh]
            dv_ref[meta, cs] += dvms[kh]

        @pl.when(n > 0)
        def _():
            prv = pl.ds(pl.multiple_of((n - 1) * BLK, BLK), BLK)
            for kh in range(KV_HEADS):
                cs = slice(kh * HEAD, (kh + 1) * HEAD)
                dk_ref[prv, cs] += dkcats[kh][:BLK]
                dv_ref[prv, cs] += dvcats[kh][:BLK]

    full_kv = pl.BlockSpec((T, KV_W), lambda n: (0, 0))
    return pl.pallas_call(
        body, grid=(NB,),
        in_specs=_attn_specs() + [pl.BlockSpec((BLK, D_MODEL), lambda n: (n, 0))],
        out_specs=[pl.BlockSpec((BLK, D_MODEL), lambda n: (n, 0)), full_kv, full_kv,
                   pl.BlockSpec((8, 128), lambda n: (0, 0))],
        out_shape=[SDS((T, D_MODEL), F32), SDS((T, KV_W), F32), SDS((T, KV_W), F32), SDS((8, 128), F32)],
        name="attn_bwd")(proj, proj, proj, proj, proj, proj, proj, sinks, do)


def _conv_taps(xp, w, rows):
    return (w[0:1] * xp[5:5 + rows] + w[1:2] * xp[6:6 + rows] + w[2:3] * xp[7:7 + rows] + w[3:4] * xp[8:8 + rows])


def _conv_fwd(proj, conv_w, conv_b):
    CONV_CB = CONV_DIM
    ncb = CONV_DIM // CONV_CB
    cb0 = C_XBC // CONV_CB

    def body(tail_ref, cur_ref, w_ref, b_ref, o_ref):
        n = pl.program_id(1)
        tail = jnp.where(n > 0, tail_ref[...], 0.0)
        xp = jnp.concatenate([tail, cur_ref[...]], axis=0)
        conv = _conv_taps(xp, w_ref[...], BLK) + b_ref[...]
        row = n * BLK + lax.broadcasted_iota(jnp.int32, (BLK, 1), 0)
        o_ref[...] = jnp.where(row >= PAD, _silu(conv), 0.0)

    return pl.pallas_call(
        body, grid=(ncb, NB),
        in_specs=[pl.BlockSpec((8, CONV_CB), lambda j, n: (jnp.maximum(n * (BLK // 8) - 1, 0), cb0 + j)),
                  pl.BlockSpec((BLK, CONV_CB), lambda j, n: (n, cb0 + j)),
                  pl.BlockSpec((8, CONV_CB), lambda j, n: (0, j)),
                  pl.BlockSpec((1, CONV_CB), lambda j, n: (0, j))],
        out_specs=pl.BlockSpec((BLK, CONV_CB), lambda j, n: (n, j)),
        out_shape=SDS((T, CONV_DIM), F32), name="conv_fwd")(proj, proj, conv_w, conv_b)


def _conv_bwd(proj, conv_w, conv_b, dact, ch0, name):
    width = dact.shape[1]
    CONV_CB = width
    ncb = width // CONV_CB
    cb0 = (C_XBC + ch0) // CONV_CB
    wb0 = ch0 // CONV_CB
    last8 = T // 8 - 1

    def body(tail_ref, cur_ref, nxt_ref, w_ref, b_ref, dcur_ref, dnxt_ref, dx_ref, dw_ref, db_ref):
        n = pl.program_id(1)
        w = w_ref[...]
        tail = jnp.where(n > 0, tail_ref[...], 0.0)
        xp = jnp.concatenate([tail, cur_ref[...], nxt_ref[...]], axis=0)
        conv = _conv_taps(xp, w, BLK + 8) + b_ref[...]
        dext = jnp.concatenate([dcur_ref[...], jnp.where(n < NB - 1, dnxt_ref[...], 0.0)], axis=0)
        row = n * BLK + lax.broadcasted_iota(jnp.int32, (BLK + 8, 1), 0)
        dconv = jnp.where(row >= PAD, dext * _dsilu(conv), 0.0)
        dx = (w[0:1] * dconv[3:3 + BLK] + w[1:2] * dconv[2:2 + BLK] + w[2:3] * dconv[1:1 + BLK]
              + w[3:4] * dconv[0:BLK])
        dx_ref[...] = dx.astype(BF16)
        dc = dconv[0:BLK]
        dws = [jnp.sum(dc * xp[5 + k:5 + k + BLK], axis=0, keepdims=True) for k in range(4)]
        dwp = jnp.concatenate(dws + [jnp.zeros((4, CONV_CB), F32)], axis=0)
        dbp = jnp.sum(dc, axis=0, keepdims=True)

        @pl.when(n == 0)
        def _():
            dw_ref[...] = dwp
            db_ref[...] = jnp.concatenate([dbp, jnp.zeros((7, CONV_CB), F32)], axis=0)

        @pl.when(n > 0)
        def _():
            dw_ref[...] += dwp
            db_ref[0:1, :] += dbp

    return pl.pallas_call(
        body, grid=(ncb, NB),
        in_specs=[pl.BlockSpec((8, CONV_CB), lambda j, n: (jnp.maximum(n * (BLK // 8) - 1, 0), cb0 + j)),
                  pl.BlockSpec((BLK, CONV_CB), lambda j, n: (n, cb0 + j)),
                  pl.BlockSpec((8, CONV_CB), lambda j, n: (jnp.minimum((n + 1) * (BLK // 8), last8), cb0 + j)),
                  pl.BlockSpec((8, CONV_CB), lambda j, n: (0, wb0 + j)),
                  pl.BlockSpec((1, CONV_CB), lambda j, n: (0, wb0 + j)),
                  pl.BlockSpec((BLK, CONV_CB), lambda j, n: (n, j)),
                  pl.BlockSpec((8, CONV_CB), lambda j, n: (jnp.minimum((n + 1) * (BLK // 8), last8), j))],
        out_specs=[pl.BlockSpec((BLK, CONV_CB), lambda j, n: (n, j)),
                   pl.BlockSpec((8, CONV_CB), lambda j, n: (0, j)),
                   pl.BlockSpec((8, CONV_CB), lambda j, n: (0, j))],
        out_shape=[SDS((T, width), BF16), SDS((8, width), F32), SDS((8, width), F32)],
        name=name)(proj, proj, proj, conv_w, conv_b, dact, dact)


HPG = SSM_HEADS // SSM_GROUPS


def _iota(shape, dim):
    return lax.broadcasted_iota(jnp.int32, shape, dim)


def _mm(a, b, ca=1, cb=0):
    return lax.dot_general(a.astype(BF16), b.astype(BF16), (((ca,), (cb,)), ((), ())), preferred_element_type=F32)


def _split3(v):
    hi = v.astype(BF16)
    r1 = v - hi.astype(F32)
    mid = r1.astype(BF16)
    lo = (r1 - mid.astype(F32)).astype(BF16)
    return hi, mid, lo


def _sel_r(parts, onehot, ca=1, cb=0):
    out = lax.dot_general(parts[0], onehot, (((ca,), (cb,)), ((), ())), preferred_element_type=F32)
    for p in parts[1:]:
        out = out + lax.dot_general(p, onehot, (((ca,), (cb,)), ((), ())), preferred_element_type=F32)
    return out


def _sel_l(onehot, parts):
    out = jnp.dot(onehot, parts[0], preferred_element_type=F32)
    for p in parts[1:]:
        out = out + jnp.dot(onehot, p, preferred_element_type=F32)
    return out


def _rows8(*rows):
    r = _iota((8, rows[0].shape[1]), 0)
    out = jnp.zeros((8, rows[0].shape[1]), F32)
    for k, v in enumerate(rows):
        out = jnp.where(r == k, v, out)
    return out


def _ssd_forward(x, z, bm, cm, dt_raw, st_prev, dtb, alog, dskip, gn, g, cst_scr):
    li, si = _iota((BLK, BLK), 0), _iota((BLK, BLK), 1)
    dt_all = jax.nn.softplus(dt_raw + dtb)
    a_row = -jnp.exp(alog)
    a_all = dt_all * a_row
    cs_all = _sel_l((li >= si).astype(BF16), _split3(a_all))
    cs_parts = _split3(cs_all)
    spread = (_iota((BLK, GRP_W), 0) == g * HPG + jnp.right_shift(_iota((BLK, GRP_W), 1), 6)).astype(BF16)
    dt_e = _sel_r(_split3(dt_all), spread)
    cs_e = _sel_r(cs_parts, spread)
    d_e = _sel_r(_split3(_rows8(dskip)), spread)[0:1]
    cs_last_e = jnp.sum(jnp.where(_iota((BLK, GRP_W), 0) == BLK - 1, cs_e, 0.0), axis=0, keepdims=True)
    p_e = jnp.exp(cs_e)
    w_e = jnp.exp(cs_last_e - cs_e)
    cd_e = jnp.exp(cs_last_e)
    xr = x * dt_e
    cst_scr[...] = cs_all.T
    cst_g = cst_scr[pl.ds(pl.multiple_of(g * HPG, HPG), HPG), :]
    own = jnp.right_shift(_iota((HPG, HPG * BLK), 1), 7) == _iota((HPG, HPG * BLK), 0)
    ownf = own.astype(F32)
    q_rows = [ownf, ownf, ownf] + [jnp.where(own, jnp.concatenate([p.astype(F32)] * HPG, axis=1), 0.0)
                                   for p in _split3(cst_g)]
    q2 = jnp.concatenate(q_rows + [jnp.zeros((BLK - 6 * HPG, HPG * BLK), F32)], axis=0).astype(BF16)
    lane1 = _iota((1, BLK), 1)
    p2 = jnp.where((lane1 >= 3 * HPG) & (lane1 < 6 * HPG), -1.0, 0.0)
    for k, part in enumerate(cs_parts):
        pick = ((li == g * HPG + si - k * HPG) & (si >= k * HPG) & (si < (k + 1) * HPG)).astype(BF16)
        p2 = p2 + jnp.dot(part, pick, preferred_element_type=F32)
    dmat = jnp.dot(p2.astype(BF16), q2, preferred_element_type=F32)
    causal = _iota((BLK, HPG * BLK), 0) >= jnp.bitwise_and(_iota((BLK, HPG * BLK), 1), BLK - 1)
    lam = jnp.exp(jnp.where(causal, dmat, NEG))
    gmat = _mm(cm, bm, 1, 1)
    m_all = lam * jnp.concatenate([gmat] * HPG, axis=1)
    mb = m_all.astype(BF16)
    lo = _iota((BLK, BLK), 1) < HEAD
    xrb = xr.astype(BF16)
    zero = jnp.zeros((BLK, BLK), BF16)
    bds, yd = [], []
    for i in range(HPG // 2):
        t = xrb[:, BLK * i:BLK * (i + 1)]
        bd = jnp.concatenate([jnp.where(lo, t, zero), jnp.where(lo, zero, t)], axis=0)
        bds.append(bd)
        yd.append(jnp.dot(mb[:, 2 * BLK * i:2 * BLK * (i + 1)], bd, preferred_element_type=F32))
    cs_st = _mm(cm, st_prev)
    y = jnp.concatenate(yd, axis=1) + cs_st * p_e + d_e * x
    xrw = xr * w_e
    st_new = cd_e * st_prev + _mm(bm, xrw, 0, 0)
    yz = y * _silu(z)
    rn = lax.rsqrt(jnp.sum(yz * yz, axis=1, keepdims=True) / GRP_W + EPS)
    return dict(out=yz * rn * gn, st_new=st_new, dt_all=dt_all, a_row=a_row, dt_e=dt_e, d_e=d_e, p_e=p_e, w_e=w_e,
                cd_e=cd_e, xr=xr, xrw=xrw, lam=lam, m_all=m_all, mb=mb, bds=bds, cs_st=cs_st, y=y, yz=yz, rn=rn, lo=lo)


def _ssd_backward(f, x, z, bm, cm, dt_raw, st_prev, dtb, gn, g, dout, dst_next, cst_scr):
    li, si = _iota((BLK, BLK), 0), _iota((BLK, BLK), 1)
    yz, rn, y, p_e, w_e, cd_e, xr = f["yz"], f["rn"], f["y"], f["p_e"], f["w_e"], f["cd_e"], f["xr"]
    dgn = jnp.sum(dout * yz * rn, axis=0, keepdims=True)
    t = dout * gn
    dyz = rn * t - yz * (rn * rn * rn) * (jnp.sum(yz * t, axis=1, keepdims=True) / GRP_W)
    dy = dyz * _silu(z)
    dz = dyz * y * _dsilu(z)
    dx = f["d_e"] * dy
    dd_e = jnp.sum(dy * x, axis=0, keepdims=True)
    dcsst = dy * p_e
    dp_e = dy * f["cs_st"]
    dcm = _mm(dcsst, st_prev, 1, 1)
    dst_prev = _mm(cm, dcsst, 0, 0) + cd_e * dst_next
    dcd_e = jnp.sum(dst_next * st_prev, axis=0, keepdims=True)
    dbm = _mm(f["xrw"], dst_next, 1, 1)
    dxrw = _mm(bm, dst_next)
    dxr = dxrw * w_e
    dw_e = dxrw * xr
    dyb = dy.astype(BF16)
    dms, dxr_d = [], []
    for i in range(HPG // 2):
        dyp = dyb[:, BLK * i:BLK * (i + 1)]
        dms.append(lax.dot_general(dyp, f["bds"][i], (((1,), (1,)), ((), ())), preferred_element_type=F32))
        r = lax.dot_general(f["mb"][:, 2 * BLK * i:2 * BLK * (i + 1)], dyp, (((0,), (0,)), ((), ())),
                            preferred_element_type=F32)
        dxr_d.append(jnp.where(f["lo"], r[0:BLK], r[BLK:2 * BLK]))
    dm_all = jnp.concatenate(dms, axis=1)
    dxr = dxr + jnp.concatenate(dxr_d, axis=1)
    dlg = dm_all * f["lam"]
    dg = dlg[:, 0:BLK]
    for j in range(1, HPG):
        dg = dg + dlg[:, BLK * j:BLK * (j + 1)]
    dcm = dcm + _mm(dg, bm)
    dbm = dbm + _mm(dg, cm, 0, 0)
    q_all = dm_all * f["m_all"]
    col_sums = jnp.sum(q_all, axis=0, keepdims=True)
    cst_scr[...] = jnp.zeros_like(cst_scr)
    cst_scr[pl.ds(pl.multiple_of(g * HPG, HPG), HPG), :] = _rows8(
        *[col_sums[:, BLK * j:BLK * (j + 1)] for j in range(HPG)])
    dcs = -cst_scr[...].T
    for j in range(HPG):
        dcs = dcs + jnp.where(si == g * HPG + j,
                              jnp.sum(q_all[:, BLK * j:BLK * (j + 1)], axis=1, keepdims=True), 0.0)
    unspread = (_iota((GRP_W, BLK), 1) == g * HPG + jnp.right_shift(_iota((GRP_W, BLK), 0), 6)).astype(BF16)
    dww = dw_e * w_e
    per_head = _sel_r(_split3(jnp.concatenate([dp_e * p_e - dww, dxr * x], axis=0)), unspread)
    last = _sel_r(_split3(_rows8(jnp.sum(dww, axis=0, keepdims=True) + dcd_e * cd_e, dd_e)), unspread)
    dcs = dcs + per_head[0:BLK] + jnp.where(li == BLK - 1, last[0:1], 0.0)
    da = _sel_l((si >= li).astype(BF16), _split3(dcs))
    ddt_all = da * f["a_row"] + per_head[BLK:2 * BLK]
    dalog = jnp.sum(da * f["dt_all"], axis=0, keepdims=True) * f["a_row"]
    dx = dx + dxr * f["dt_e"]
    ddt_raw = ddt_all * jax.nn.sigmoid(dt_raw + dtb)
    ddtb = jnp.sum(ddt_raw, axis=0, keepdims=True)
    ddskip = last[1:2]
    return dict(dx=dx, dz=dz, dbm=dbm, dcm=dcm, ddt_raw=ddt_raw, dst_prev=dst_prev, ddtb=ddtb, dalog=dalog,
                ddskip=ddskip, dgn=dgn)


def _ssd_in_specs(rev):
    cidx = (lambda c: NB - 1 - c) if rev else (lambda c: c)
    return [
        pl.BlockSpec((BLK, GRP_W), lambda g, c: (cidx(c), g)),
        pl.BlockSpec((BLK, SSM_STATE), lambda g, c: (cidx(c), SSM_INNER // SSM_STATE + g)),
        pl.BlockSpec((BLK, SSM_STATE), lambda g, c: (cidx(c), SSM_INNER // SSM_STATE + SSM_GROUPS + g)),
        pl.BlockSpec((BLK, 128), lambda g, c: (cidx(c), C_DT // 128)),
        pl.BlockSpec((BLK, GRP_W), lambda g, c: (cidx(c), C_ZS // GRP_W + g)),
        pl.BlockSpec((1, 128), lambda g, c: (0, 0)),
        pl.BlockSpec((1, 128), lambda g, c: (0, 0)),
        pl.BlockSpec((1, 128), lambda g, c: (0, 0)),
        pl.BlockSpec((1, GRP_W), lambda g, c: (0, g)),
    ]


def _ssd_fwd(xbc_act, proj, dt_bias, a_log, d_skip, g_norm, gather=()):
    ng = len(gather)

    def body(*refs):
        xs_ref, b_ref, c_ref, dt_ref, z_ref, dtb_ref, al_ref, dsk_ref, gn_ref = refs[:9]
        y_ref, st_ref = refs[9 + ng:11 + ng]
        s_scr, cst_scr = refs[11 + 2 * ng:13 + 2 * ng]
        g = pl.program_id(0)
        c = pl.program_id(1)
        if ng:
            ag_start, ag_forward, ag_finish = _ag_program(refs[9:9 + ng], refs[11 + ng:11 + 2 * ng],
                                                          refs[13 + 2 * ng:])
            pl.when((g == 0) & (c == 0))(ag_start)
            pl.when((g == SSM_GROUPS // 2) & (c == 0))(ag_forward)

        @pl.when(c == 0)
        def _():
            s_scr[...] = jnp.zeros_like(s_scr)

        st_prev = s_scr[...]
        st_ref[0, 0] = st_prev
        f = _ssd_forward(xs_ref[...], z_ref[...], b_ref[...], c_ref[...], dt_ref[...], st_prev, dtb_ref[...],
                         al_ref[...], dsk_ref[...], gn_ref[...], g, cst_scr)
        y_ref[...] = f["out"].astype(BF16)
        s_scr[...] = f["st_new"]
        if ng:
            pl.when((g == SSM_GROUPS - 1) & (c == NB - 1))(ag_finish)

    return pl.pallas_call(
        body, grid=(SSM_GROUPS, NB), in_specs=_ssd_in_specs(False) + [ANY] * ng,
        out_specs=[pl.BlockSpec((BLK, GRP_W), lambda g, c: (c, g)),
                   pl.BlockSpec((1, 1, SSM_STATE, GRP_W), lambda g, c: (g, c, 0, 0))] + [ANY] * ng,
        out_shape=[SDS((T, SSM_INNER), BF16), SDS((SSM_GROUPS, NB, SSM_STATE, GRP_W), F32)]
        + [SDS((N_DEV,) + s.shape, s.dtype) for s in gather],
        scratch_shapes=[pltpu.VMEM((SSM_STATE, GRP_W), F32), pltpu.VMEM((BLK, BLK), F32)]
        + (_ag_scratch(gather) if ng else []),
        compiler_params=_cparams(),
        name="ssd_fwd")(xbc_act, xbc_act, xbc_act, proj, proj, dt_bias, a_log, d_skip, g_norm, *gather)


def _ssd_bwd(xbc_act, proj, dt_bias, a_log, d_skip, g_norm, states, dy, exchange=()):
    chips = exchange
    nc = len(chips)

    def body(*refs):
        xs_ref, b_ref, c_ref, dt_ref, z_ref, dtb_ref, al_ref, dsk_ref, gn_ref, st_ref, dy_ref = refs[:11]
        (dxs_ref, db_ref, dc_ref, ddt_ref, dz_ref, ddtb_ref, dal_ref, ddsk_ref, dgn_ref) = refs[11 + nc:20 + nc]
        ds_scr, cst_scr = refs[20 + 2 * nc:22 + 2 * nc]
        g = pl.program_id(0)
        c = pl.program_id(1)
        if nc:
            ch_start, ch_finish = _direct_program(refs[11:11 + nc], refs[20 + nc:20 + 2 * nc], refs[22 + 2 * nc:])
            pl.when((g == 0) & (c == 0))(ch_start)

        @pl.when(c == 0)
        def _():
            ds_scr[...] = jnp.zeros_like(ds_scr)
            dgn_ref[...] = jnp.zeros_like(dgn_ref)

        @pl.when((c == 0) & (g == 0))
        def _():
            ddtb_ref[...] = jnp.zeros_like(ddtb_ref)
            dal_ref[...] = jnp.zeros_like(dal_ref)
            ddsk_ref[...] = jnp.zeros_like(ddsk_ref)

        x, z, bm, cm, dt_raw, st_prev = xs_ref[...], z_ref[...], b_ref[...], c_ref[...], dt_ref[...], st_ref[0, 0]
        f = _ssd_forward(x, z, bm, cm, dt_raw, st_prev, dtb_ref[...], al_ref[...], dsk_ref[...], gn_ref[...], g,
                         cst_scr)
        d = _ssd_backward(f, x, z, bm, cm, dt_raw, st_prev, dtb_ref[...], gn_ref[...], g, dy_ref[...].astype(F32),
                          ds_scr[...], cst_scr)
        dxs_ref[...] = d["dx"]
        dz_ref[...] = d["dz"].astype(BF16)
        ds_scr[...] = d["dst_prev"]
        db_ref[...] = d["dbm"]
        dc_ref[...] = d["dcm"]
        ddt_ref[...] = d["ddt_raw"]
        dgn_ref[0:1, :] += d["dgn"]
        ddtb_ref[0:1, :] += d["ddtb"]
        dal_ref[0:1, :] += d["dalog"]
        ddsk_ref[0:1, :] += d["ddskip"]
        if nc:
            pl.when((g == SSM_GROUPS - 1) & (c == NB - 1))(ch_finish)

    rc = lambda c: NB - 1 - c
    small = pl.BlockSpec((8, 128), lambda g, c: (0, 0))
    return pl.pallas_call(
        body, grid=(SSM_GROUPS, NB),
        in_specs=_ssd_in_specs(True) + [
            pl.BlockSpec((1, 1, SSM_STATE, GRP_W), lambda g, c: (g, rc(c), 0, 0)),
            pl.BlockSpec((BLK, GRP_W), lambda g, c: (rc(c), g))] + [ANY] * nc,
        out_specs=[pl.BlockSpec((BLK, GRP_W), lambda g, c: (rc(c), g)),
                   pl.BlockSpec((BLK, SSM_STATE), lambda g, c: (rc(c), g)),
                   pl.BlockSpec((BLK, SSM_STATE), lambda g, c: (rc(c), g)),
                   pl.BlockSpec((BLK, 128), lambda g, c: (rc(c), g)),
                   pl.BlockSpec((BLK, GRP_W), lambda g, c: (rc(c), g)),
                   small, small, small,
                   pl.BlockSpec((8, GRP_W), lambda g, c: (0, g))] + [ANY] * nc,
        out_shape=[SDS((T, SSM_INNER), F32), SDS((T, GRP_W), F32), SDS((T, GRP_W), F32), SDS((T, GRP_W), F32),
                   SDS((T, SSM_INNER), BF16), SDS((8, 128), F32), SDS((8, 128), F32), SDS((8, 128), F32),
                   SDS((8, SSM_INNER), F32)] + [SDS(p.shape, p.dtype) for p in chips],
        scratch_shapes=[pltpu.VMEM((SSM_STATE, GRP_W), F32), pltpu.VMEM((BLK, BLK), F32)]
        + (_direct_scratch(chips) if nc else []),
        compiler_params=_cparams(),
        name="ssd_bwd")(xbc_act, xbc_act, xbc_act, proj, proj, dt_bias, a_log, d_skip, g_norm, states, dy, *chips)


POST_R = 272


def _post_a(o, proj, sn, w_att, w_ssm, w_o):
    def body(o_ref, za_ref, ga_ref, gs_ref, sn_ref, wa_ref, ws_ref, wo_ref, a_ref, mg_ref, ya_ref, ys_ref, out_ref):
        a = (o_ref[...] * _silu(za_ref[...])).astype(BF16)
        a_ref[...] = a
        ya = jnp.dot(a, wa_ref[...], preferred_element_type=F32)
        ys = jnp.dot(sn_ref[...], ws_ref[...], preferred_element_type=F32)
        ya_ref[...] = ya.astype(BF16)
        ys_ref[...] = ys.astype(BF16)
        mg = (jax.nn.sigmoid(ga_ref[...]) * ya + jax.nn.sigmoid(gs_ref[...]) * ys).astype(BF16)
        mg_ref[...] = mg
        out_ref[...] = jnp.dot(mg, wo_ref[...], preferred_element_type=F32)

    row = pl.BlockSpec((POST_R, D_MODEL), lambda i: (i, 0))
    pcol = lambda c0: pl.BlockSpec((POST_R, D_MODEL), lambda i: (i, c0 // D_MODEL))
    full = lambda r: pl.BlockSpec((r, D_MODEL), lambda i: (0, 0))
    return pl.pallas_call(
        body, grid=(T // POST_R,),
        in_specs=[row, pcol(C_ZA), pcol(C_GA), pcol(C_GS), pl.BlockSpec((POST_R, SSM_INNER), lambda i: (i, 0)),
                  full(D_MODEL), full(SSM_INNER), full(D_MODEL)],
        out_specs=[row, row, row, row, row],
        out_shape=[SDS((T, D_MODEL), BF16), SDS((T, D_MODEL), BF16), SDS((T, D_MODEL), BF16), SDS((T, D_MODEL), BF16),
                   SDS((T, D_MODEL), F32)],
        compiler_params=_cparams(), name="post_a")(o, proj, proj, proj, sn, w_att, w_ssm, w_o)


def _post_b(out, h, tgt, proj, ya, ys, o, g_post, w_att, w_ssm, w_o):
    def body(out_ref, h_ref, t_ref, za_ref, ga_ref, gs_ref, ya_ref, ys_ref, o_ref, gp_ref, wa_ref, ws_ref, wo_ref,
             loss_ref, dres_ref, dout_ref, dya_ref, dys_ref, dga_ref, dgs_ref, do_ref, dza_ref, dsn_ref, dgp_ref):
        i = pl.program_id(0)
        x = out_ref[...]
        gp = gp_ref[...]
        r = lax.rsqrt(jnp.mean(x * x, axis=-1, keepdims=True) + EPS)
        row = i * POST_R + lax.broadcasted_iota(jnp.int32, (POST_R, 1), 0)
        res = h_ref[...] + jnp.where(row >= PAD, x * r * gp, 0.0)
        live = row >= PAD + N_META
        err = jnp.where(live, res - t_ref[...], 0.0)
        lpart = 0.5 * jnp.sum(jnp.sum(err * err, axis=1, keepdims=True) / D_MODEL, axis=0, keepdims=True)
        dres = err / D_MODEL
        dres_ref[...] = dres
        gpart = jnp.sum(dres * x * r, axis=0, keepdims=True)

        @pl.when(i == 0)
        def _():
            loss_ref[...] = jnp.zeros_like(loss_ref)
            dgp_ref[...] = jnp.zeros_like(dgp_ref)

        loss_ref[...] += jnp.broadcast_to(lpart, loss_ref.shape)
        dgp_ref[0:1, :] += gpart
        gd = gp * dres
        dout = (r * gd - x * (r * r * r) * jnp.mean(x * gd, axis=-1, keepdims=True)).astype(BF16)
        dout_ref[...] = dout
        dmg = lax.dot_general(dout, wo_ref[...], (((1,), (1,)), ((), ())), preferred_element_type=F32)
        sga = jax.nn.sigmoid(ga_ref[...])
        sgs = jax.nn.sigmoid(gs_ref[...])
        dya = (dmg * sga).astype(BF16)
        dys = (dmg * sgs).astype(BF16)
        dya_ref[...] = dya
        dys_ref[...] = dys
        dga_ref[...] = (dmg * ya_ref[...].astype(F32) * sga * (1.0 - sga)).astype(BF16)
        dgs_ref[...] = (dmg * ys_ref[...].astype(F32) * sgs * (1.0 - sgs)).astype(BF16)
        da = lax.dot_general(dya, wa_ref[...], (((1,), (1,)), ((), ())), preferred_element_type=F32)
        za = za_ref[...]
        do_ref[...] = (da * _silu(za)).astype(BF16)
        dza_ref[...] = (da * o_ref[...] * _dsilu(za)).astype(BF16)
        dsn_ref[...] = lax.dot_general(dys, ws_ref[...], (((1,), (1,)), ((), ())),
                                       preferred_element_type=F32).astype(BF16)

    row = pl.BlockSpec((POST_R, D_MODEL), lambda i: (i, 0))
    pcol = lambda c0: pl.BlockSpec((POST_R, D_MODEL), lambda i: (i, c0 // D_MODEL))
    full = lambda r: pl.BlockSpec((r, D_MODEL), lambda i: (0, 0))
    small = pl.BlockSpec((8, D_MODEL), lambda i: (0, 0))
    return pl.pallas_call(
        body, grid=(T // POST_R,),
        in_specs=[row, row, row, pcol(C_ZA), pcol(C_GA), pcol(C_GS), row, row, row,
                  pl.BlockSpec((1, D_MODEL), lambda i: (0, 0)), full(D_MODEL), full(SSM_INNER), full(D_MODEL)],
        out_specs=[pl.BlockSpec((8, 128), lambda i: (0, 0)), row, row, row, row, row, row, row, row,
                   pl.BlockSpec((POST_R, SSM_INNER), lambda i: (i, 0)), small],
        out_shape=[SDS((8, 128), F32), SDS((T, D_MODEL), F32), SDS((T, D_MODEL), BF16), SDS((T, D_MODEL), BF16),
                   SDS((T, D_MODEL), BF16), SDS((T, D_MODEL), BF16), SDS((T, D_MODEL), BF16), SDS((T, D_MODEL), BF16),
                   SDS((T, D_MODEL), BF16), SDS((T, SSM_INNER), BF16), SDS((8, D_MODEL), F32)],
        compiler_params=_cparams(), name="post_b")(out, h, tgt, proj, proj, proj, ya, ys, o, g_post, w_att, w_ssm, w_o)


def _assemble(dq, dza, dga, dgs, dzs, dxx, dxb, dxc, dk, dv, ddt4):
    def body(dq_ref, dza_ref, dga_ref, dgs_ref, dzs_ref, dxx_ref, dxb_ref, dxc_ref, dk_ref, dv_ref, ddt_ref, o_ref):
        o_ref[:, C_Q:C_Q + D_MODEL] = dq_ref[...].astype(BF16)
        o_ref[:, C_ZA:C_ZA + D_MODEL] = dza_ref[...]
        o_ref[:, C_GA:C_GA + D_MODEL] = dga_ref[...]
        o_ref[:, C_GS:C_GS + D_MODEL] = dgs_ref[...]
        o_ref[:, C_ZS:C_ZS + SSM_INNER] = dzs_ref[...]
        o_ref[:, C_XBC:C_XBC + SSM_INNER] = dxx_ref[...]
        o_ref[:, C_XBC + SSM_INNER:C_XBC + SSM_INNER + GRP_W] = dxb_ref[...]
        o_ref[:, C_XBC + SSM_INNER + GRP_W:C_XBC + CONV_DIM] = dxc_ref[...]
        o_ref[:, C_K:C_K + KV_W] = dk_ref[...].astype(BF16)
        o_ref[:, C_V:C_V + KV_W] = dv_ref[...].astype(BF16)
        d4 = ddt_ref[...]
        o_ref[:, C_DT:C_DT + 128] = (d4[:, 0:128] + d4[:, 128:256] + d4[:, 256:384] + d4[:, 384:512]).astype(BF16)

    spec = lambda w: pl.BlockSpec((BLK, w), lambda i: (i, 0))
    ins = [dq, dza, dga, dgs, dzs, dxx, dxb, dxc, dk, dv, ddt4]
    return pl.pallas_call(
        body, grid=(NB,), in_specs=[spec(a.shape[1]) for a in ins], out_specs=spec(PW),
        out_shape=SDS((T, PW), BF16), name="assemble")(*ins)


def _adamw_math(w, g, m, v):
    m = ADAM_B1 * m + (1.0 - ADAM_B1) * g
    v = ADAM_B2 * v + (1.0 - ADAM_B2) * (g * g)
    m_hat = m / (1.0 - ADAM_B1 ** ADAM_STEP)
    v_hat = v / (1.0 - ADAM_B2 ** ADAM_STEP)
    delta = -ADAM_LR * (m_hat / (jnp.sqrt(v_hat) + ADAM_EPS) + ADAM_WD * w)
    return delta, m, v


def _sum_adamw(recv, w, m, v, tc, name):
    rows, cols = w.shape
    nslab = recv.shape[0]
    assert cols % tc == 0

    def body(r_ref, w_ref, m_ref, v_ref, g_ref, d_ref, nm_ref, nv_ref):
        g = r_ref[0].astype(F32)
        for d in range(1, nslab):
            g = g + r_ref[d].astype(F32)
        g_ref[...] = g
        delta, nm, nv = _adamw_math(w_ref[...], g, m_ref[...], v_ref[...])
        d_ref[...] = delta
        nm_ref[...] = nm
        nv_ref[...] = nv

    blk = pl.BlockSpec((rows, tc), lambda i: (0, i))
    return pl.pallas_call(
        body, grid=(cols // tc,),
        in_specs=[pl.BlockSpec((nslab, rows, tc), lambda i: (0, 0, i)), blk, blk, blk],
        out_specs=[blk, blk, blk, blk], out_shape=[SDS((rows, cols), F32)] * 4,
        compiler_params=_cparams(), name=name)(recv, w, m, v)


def _sum_adamw_rows3(recv, w3, m3, v3, name):
    pairs = 61
    assert (SHARD_IN // 2) % pairs == 0

    def body(r_ref, w_ref, m_ref, v_ref, g_ref, d_ref, nm_ref, nv_ref):
        g = r_ref[0].astype(F32)
        for d in range(1, N_CHIP):
            g = g + r_ref[d].astype(F32)
        g = g.reshape(2 * pairs, ROW_TILES, 128)
        g_ref[...] = g
        delta, nm, nv = _adamw_math(w_ref[...], g, m_ref[...], v_ref[...])
        d_ref[...] = delta
        nm_ref[...] = nm
        nv_ref[...] = nv

    blk = pl.BlockSpec((2 * pairs, ROW_TILES, 128), lambda i: (i, 0, 0))
    return pl.pallas_call(
        body, grid=(SHARD_IN // 2 // pairs,),
        in_specs=[pl.BlockSpec((N_CHIP, pairs, 2 * ROW_TILES, 128), lambda i: (0, i, 0, 0)), blk, blk, blk],
        out_specs=[blk, blk, blk, blk], out_shape=[SDS(w3.shape, F32)] * 4,
        compiler_params=_cparams(), name=name)(recv, w3, m3, v3)


ROW_GPRE, ROW_CONVB, ROW_DTB, ROW_ALOG, ROW_DSKIP, ROW_SINK, ROW_GSSM, ROW_GPOST = 0, 1, 4, 5, 6, 7, 8, 10
REP_ROWS, ROW_CONVW, ROW_META, SM_ROWS = 16, 16, 24, 40
CW_SHARD = CONV_DIM // N_DEV
META_SHARD = D_MODEL // N_DEV


def _small_pack(dgpre, dbx, dbb, dbc, ddtb, dal, ddsk, dsink, dgn, dgp, dwx, dwb, dwc, dh):
    def body(dgpre_ref, dbx_ref, dbb_ref, dbc_ref, ddtb_ref, dal_ref, ddsk_ref, dsink_ref, dgn_ref, dgp_ref,
             dwx_ref, dwb_ref, dwc_ref, dh_ref, o_ref, rep):
        rep[...] = jnp.zeros_like(rep)
        rep[ROW_GPRE:ROW_GPRE + 1, :] = dgpre_ref[0:1, :]
        rep[ROW_CONVB:ROW_CONVB + 1, :] = dbx_ref[0:1, 0:1024]
        rep[ROW_CONVB + 1:ROW_CONVB + 2, :] = dbx_ref[0:1, 1024:2048]
        rep[ROW_CONVB + 2:ROW_CONVB + 3, 0:512] = dbb_ref[0:1, :]
        rep[ROW_CONVB + 2:ROW_CONVB + 3, 512:1024] = dbc_ref[0:1, :]
        rep[ROW_DTB:ROW_DTB + 1, 0:128] = ddtb_ref[0:1, :]
        rep[ROW_ALOG:ROW_ALOG + 1, 0:128] = dal_ref[0:1, :]
        rep[ROW_DSKIP:ROW_DSKIP + 1, 0:128] = ddsk_ref[0:1, :]
        rep[ROW_SINK:ROW_SINK + 1, 0:128] = dsink_ref[0:1, :]
        rep[ROW_GSSM:ROW_GSSM + 1, :] = dgn_ref[0:1, 0:1024]
        rep[ROW_GSSM + 1:ROW_GSSM + 2, :] = dgn_ref[0:1, 1024:2048]
        rep[ROW_GPOST:ROW_GPOST + 1, :] = dgp_ref[0:1, :]
        cw = jnp.concatenate([dwx_ref[...], dwb_ref[...], dwc_ref[...]], axis=1)
        mh = dh_ref[...]
        o_ref[...] = jnp.zeros_like(o_ref)
        for p in range(N_DEV):
            o_ref[p, 0:REP_ROWS, :] = rep[...]
            o_ref[p, ROW_CONVW:ROW_CONVW + 8, 0:CW_SHARD] = cw[:, p * CW_SHARD:(p + 1) * CW_SHARD]
            o_ref[p, ROW_META:ROW_META + N_META, 0:META_SHARD] = mh[:, p * META_SHARD:(p + 1) * META_SHARD]

    ins = [dgpre, dbx, dbb, dbc, ddtb, dal, ddsk, dsink, dgn, dgp, dwx, dwb, dwc]
    return pl.pallas_call(
        body, grid=(1,),
        in_specs=[pl.BlockSpec(a.shape, lambda i: (0, 0)) for a in ins]
        + [pl.BlockSpec((N_META, D_MODEL), lambda i: (PAD // N_META, 0))],
        out_specs=pl.BlockSpec((N_DEV, SM_ROWS, 1024), lambda i: (0, 0, 0)),
        out_shape=SDS((N_DEV, SM_ROWS, 1024), F32), scratch_shapes=[pltpu.VMEM((REP_ROWS, 1024), F32)],
        name="small_pack")(*ins, dh)


def _small_finish(recv, params):
    npar = len(params)

    def body(*refs):
        r_ref = refs[0]
        wmv = refs[1:1 + 3 * npar]
        outs = refs[1 + 3 * npar:1 + 7 * npar]
        gs = refs[-1]
        g = r_ref[0]
        for d in range(1, recv.shape[0]):
            g = g + r_ref[d]
        gs[...] = g
        grads = [
            gs[ROW_GPRE:ROW_GPRE + 1, :],
            jnp.concatenate([gs[ROW_CONVB + k:ROW_CONVB + k + 1, :] for k in range(3)], axis=1),
            gs[ROW_DTB:ROW_DTB + 1, 0:SSM_HEADS], gs[ROW_ALOG:ROW_ALOG + 1, 0:SSM_HEADS],
            gs[ROW_DSKIP:ROW_DSKIP + 1, 0:SSM_HEADS], gs[ROW_SINK:ROW_SINK + 1, 0:Q_HEADS],
            jnp.concatenate([gs[ROW_GSSM:ROW_GSSM + 1, :], gs[ROW_GSSM + 1:ROW_GSSM + 2, :]], axis=1),
            gs[ROW_GPOST:ROW_GPOST + 1, :],
            gs[ROW_CONVW:ROW_CONVW + 4, 0:CW_SHARD],
            gs[ROW_META:ROW_META + N_META, 0:META_SHARD]]
        for i in range(npar):
            w_ref, m_ref, v_ref = wmv[3 * i:3 * i + 3]
            delta, nm, nv = _adamw_math(w_ref[...], grads[i], m_ref[...], v_ref[...])
            outs[4 * i][...] = grads[i]
            outs[4 * i + 1][...] = delta
            outs[4 * i + 2][...] = nm
            outs[4 * i + 3][...] = nv

    flat = [a for wmv in params for a in wmv]
    res = pl.pallas_call(
        body, out_shape=[SDS(wmv[0].shape, F32) for wmv in params for _ in range(4)],
        scratch_shapes=[pltpu.VMEM((SM_ROWS, 1024), F32)], name="small_finish")(recv, *flat)
    return [tuple(res[4 * i:4 * i + 4]) for i in range(npar)]


def _slab(ref, px, py, pc):
    return ref.at[4 * px + 2 * py + pc]


def _bounce(src, dst, buf, sem):
    cp = pltpu.make_async_copy(src, buf, sem)
    cp.start()
    cp.wait()
    cp = pltpu.make_async_copy(buf, dst, sem)
    cp.start()
    cp.wait()


def _ag_program(ins, outs, scratch):
    na = len(ins)
    send_sems, recv_sems, local_sems = scratch[:3]
    bufs = scratch[3:]
    x, y, c = lax.axis_index("x"), lax.axis_index("y"), lax.axis_index("c")
    me, sibling = (x, y, c), (x, y, 1 - c)
    chips = [(1 - x, y), (x, 1 - y), (1 - x, 1 - y)]

    def copy(a, k, block, to, src=None):
        dst = _slab(outs[a], *block)
        return pltpu.make_async_remote_copy(
            src_ref=dst if src is None else src, dst_ref=dst, send_sem=send_sems.at[a, k],
            recv_sem=recv_sems.at[a, k], device_id=to, device_id_type=MESH)

    def own_sends():
        out = []
        for a in range(na):
            out.append(copy(a, 0, me, sibling, src=ins[a]))
            out += [copy(a, 1 + j, me, (*chip, c), src=ins[a]) for j, chip in enumerate(chips)]
        return out

    def start():
        for cp in own_sends():
            cp.start()
        for a in range(na):
            _bounce(ins[a], _slab(outs[a], *me), bufs[a], local_sems.at[a])

    def forward():
        for j, chip in enumerate(chips):
            for a in range(na):
                copy(a, 1 + j, (*chip, c), me).wait_recv()
                copy(a, 4 + j, (*chip, c), sibling).start()

    def finish():
        for a in range(na):
            copy(a, 0, sibling, me).wait_recv()
            for j, chip in enumerate(chips):
                copy(a, 4 + j, (*chip, 1 - c), me).wait_recv()
        for cp in own_sends():
            cp.wait_send()
        for j, chip in enumerate(chips):
            for a in range(na):
                copy(a, 4 + j, (*chip, c), sibling).wait_send()

    return start, forward, finish


def _ag_scratch(shards):
    na = len(shards)
    return [pltpu.SemaphoreType.DMA((na, 7)), pltpu.SemaphoreType.DMA((na, 7)),
            pltpu.SemaphoreType.DMA((na,))] + [pltpu.VMEM(s.shape, s.dtype) for s in shards]


def _all_gather(shards):
    na = len(shards)

    def body(*refs):
        start, forward, finish = _ag_program(refs[:na], refs[na:2 * na], refs[2 * na:])
        start()
        forward()
        finish()

    return pl.pallas_call(
        body, in_specs=[ANY] * na, out_specs=[ANY] * na,
        out_shape=[SDS((N_DEV,) + s.shape, s.dtype) for s in shards],
        scratch_shapes=_ag_scratch(shards), name="all_gather")(*shards)


N_CHIP = 4


def _exchange_pair(parts, name):
    na = len(parts)

    def body(*refs):
        ins, own, got = refs[:na], refs[na:2 * na], refs[2 * na:3 * na]
        send_sems, recv_sems, local_sems = refs[3 * na:3 * na + 3]
        bufs = refs[3 * na + 3:]
        x, y, c = lax.axis_index("x"), lax.axis_index("y"), lax.axis_index("c")
        sibling = (x, y, 1 - c)
        sent = []
        for a in range(na):
            for k in range(N_CHIP):
                cp = pltpu.make_async_remote_copy(
                    src_ref=ins[a].at[2 * k + 1 - c], dst_ref=got[a].at[k], send_sem=send_sems.at[a, k],
                    recv_sem=recv_sems.at[a, k], device_id=sibling, device_id_type=MESH)
                cp.start()
                sent.append(cp)
        for a in range(na):
            for k in range(N_CHIP):
                _bounce(ins[a].at[2 * k + c], own[a].at[k], bufs[a], local_sems.at[a])
        for cp in sent:
            cp.wait()

    half = [SDS((N_CHIP,) + p.shape[1:], p.dtype) for p in parts]
    res = pl.pallas_call(
        body, in_specs=[ANY] * na, out_specs=[ANY] * (2 * na), out_shape=half + half,
        scratch_shapes=[pltpu.SemaphoreType.DMA((na, N_CHIP)), pltpu.SemaphoreType.DMA((na, N_CHIP)),
                        pltpu.SemaphoreType.DMA((na,))] + [pltpu.VMEM(p.shape[1:], p.dtype) for p in parts],
        name=name)(*parts)
    return res[:na], res[na:]


def _pair_sum(own, got, name):
    na = len(own)

    def body(*refs):
        for a in range(na):
            o_ref, g_ref, s_ref = refs[a], refs[na + a], refs[2 * na + a]
            s_ref[...] = (o_ref[...].astype(F32) + g_ref[...].astype(F32)).astype(s_ref.dtype)

    def spec(p):
        nd = len(p.shape) - 1
        return pl.BlockSpec((1,) + p.shape[1:], lambda k, nd=nd: (k,) + (0,) * nd)

    return pl.pallas_call(
        body, grid=(N_CHIP,), in_specs=[spec(p) for p in own] + [spec(p) for p in got],
        out_specs=[spec(p) for p in own], out_shape=[SDS(p.shape, p.dtype) for p in own],
        compiler_params=_cparams(), name=name)(*own, *got)


def _chips_program(ins, outs, scratch):
    na = len(ins)
    send_sems, recv_sems, local_sems = scratch[:3]
    bufs = scratch[3:]
    x, y, c = lax.axis_index("x"), lax.axis_index("y"), lax.axis_index("c")
    mine = 2 * x + y
    chips = [(1 - x, y), (x, 1 - y), (1 - x, 1 - y)]

    def send(a, j):
        px, py = chips[j]
        return pltpu.make_async_remote_copy(
            src_ref=ins[a].at[2 * px + py], dst_ref=outs[a].at[mine], send_sem=send_sems.at[a, j],
            recv_sem=recv_sems.at[a, j], device_id=(px, py, c), device_id_type=MESH)

    def arrival(a, j):
        px, py = chips[j]
        return pltpu.make_async_remote_copy(
            src_ref=ins[a].at[2 * px + py], dst_ref=outs[a].at[2 * px + py], send_sem=send_sems.at[a, j],
            recv_sem=recv_sems.at[a, j], device_id=(px, py, c), device_id_type=MESH)

    def start():
        for a in range(na):
            for j in range(3):
                send(a, j).start()
        for a in range(na):
            _bounce(ins[a].at[mine], outs[a].at[mine], bufs[a], local_sems.at[a])

    def finish():
        for a in range(na):
            for j in range(3):
                arrival(a, j).wait_recv()
        for a in range(na):
            for j in range(3):
                send(a, j).wait_send()

    return start, finish


def _chips_scratch(parts):
    na = len(parts)
    return [pltpu.SemaphoreType.DMA((na, 3)), pltpu.SemaphoreType.DMA((na, 3)),
            pltpu.SemaphoreType.DMA((na,))] + [pltpu.VMEM(p.shape[1:], p.dtype) for p in parts]


def _direct_program(ins, outs, scratch):
    na = len(ins)
    send_sems, recv_sems, local_sems = scratch[:3]
    bufs = scratch[3:]
    x, y, c = lax.axis_index("x"), lax.axis_index("y"), lax.axis_index("c")
    me = (x, y, c)
    peers = []
    for k in range(1, N_DEV):
        dx, dy, dc = (k >> 2) & 1, (k >> 1) & 1, k & 1
        peers.append(((1 - x) if dx else x, (1 - y) if dy else y, (1 - c) if dc else c))

    def send(a, k):
        return pltpu.make_async_remote_copy(
            src_ref=_slab(ins[a], *peers[k]), dst_ref=_slab(outs[a], *me), send_sem=send_sems.at[a, k],
            recv_sem=recv_sems.at[a, k], device_id=peers[k], device_id_type=MESH)

    def arrival(a, k):
        return pltpu.make_async_remote_copy(
            src_ref=_slab(ins[a], *peers[k]), dst_ref=_slab(outs[a], *peers[k]), send_sem=send_sems.at[a, k],
            recv_sem=recv_sems.at[a, k], device_id=peers[k], device_id_type=MESH)

    def start():
        for a in range(na):
            for k in range(N_DEV - 1):
                send(a, k).start()
        for a in range(na):
            _bounce(_slab(ins[a], *me), _slab(outs[a], *me), bufs[a], local_sems.at[a])

    def finish():
        for a in range(na):
            for k in range(N_DEV - 1):
                arrival(a, k).wait_recv()
        for a in range(na):
            for k in range(N_DEV - 1):
                send(a, k).wait_send()

    return start, finish


def _direct_scratch(parts):
    na = len(parts)
    return [pltpu.SemaphoreType.DMA((na, N_DEV - 1)), pltpu.SemaphoreType.DMA((na, N_DEV - 1)),
            pltpu.SemaphoreType.DMA((na,))] + [pltpu.VMEM(p.shape[1:], p.dtype) for p in parts]


def _exchange_direct(parts, name):
    na = len(parts)

    def body(*refs):
        start, finish = _direct_program(refs[:na], refs[na:2 * na], refs[2 * na:])
        start()
        finish()

    return pl.pallas_call(
        body, in_specs=[ANY] * na, out_specs=[ANY] * na, out_shape=[SDS(p.shape, p.dtype) for p in parts],
        scratch_shapes=_direct_scratch(parts), name=name)(*parts)


def _exchange_chips(parts, name):
    na = len(parts)

    def body(*refs):
        start, finish = _chips_program(refs[:na], refs[na:2 * na], refs[2 * na:])
        start()
        finish()

    return pl.pallas_call(
        body, in_specs=[ANY] * na, out_specs=[ANY] * na, out_shape=[SDS(p.shape, p.dtype) for p in parts],
        scratch_shapes=_chips_scratch(parts), name=name)(*parts)


ROW_TILES = D_MODEL // 128


def _rows3(t):
    return jnp.transpose(t[0]).reshape(t.shape[2], ROW_TILES, 128)


def _unrows3(t):
    return jnp.transpose(t.reshape(t.shape[0], D_MODEL))[None]


def _cast_shards(w_in3, w_att, w_ssm, w_o):
    def body(wi_ref, wa_ref, ws_ref, wo_ref, a_ref, b_ref, c_ref, d_ref):
        a_ref[...] = wi_ref[...].reshape(SHARD_IN // 2, 2 * ROW_TILES, 128).astype(BF16)
        b_ref[...] = wa_ref[...].astype(BF16)
        c_ref[...] = ws_ref[...].astype(BF16)
        d_ref[...] = wo_ref[...].astype(BF16)

    return pl.pallas_call(
        body, out_shape=[SDS((SHARD_IN // 2, 2 * ROW_TILES, 128), BF16), SDS(w_att.shape, BF16),
                         SDS(w_ssm.shape, BF16), SDS(w_o.shape, BF16)],
        compiler_params=_cparams(), name="cast_shards")(w_in3, w_att, w_ssm, w_o)


def _pieces():
    out = []
    for r0, c0, w in _SEGS:
        r = r0
        while r < r0 + w:
            d = r // SHARD_IN
            n = min(r0 + w, (d + 1) * SHARD_IN) - r
            out.append((c0 + (r - r0), d, r - d * SHARD_IN, n))
            r += n
    return out


def _to_aligned_t(slabs):
    def body(a_ref, o_ref):
        for (t, d, s, n) in _pieces():
            o_ref[t:t + n, :] = a_ref[d, s // 2:(s + n) // 2].reshape(n, D_MODEL)
        o_ref[C_DT + 32:C_DT + 128, :] = jnp.zeros((96, D_MODEL), slabs.dtype)

    return pl.pallas_call(body, out_shape=SDS((PW, D_MODEL), slabs.dtype), compiler_params=_cparams(),
                          name="to_aligned")(slabs)


def _from_aligned_t(g):
    def body(g_ref, o_ref):
        for (t, d, s, n) in _pieces():
            o_ref[d, s // 2:(s + n) // 2] = g_ref[t:t + n, :].reshape(n // 2, 2 * ROW_TILES, 128)

    return pl.pallas_call(body, out_shape=SDS((N_DEV, SHARD_IN // 2, 2 * ROW_TILES, 128), g.dtype),
                          compiler_params=_cparams(), name="from_aligned")(g)


_SEGS = [
    (R_Q, C_Q, 1024), (R_K, C_K, 256), (R_V, C_V, 256), (R_ZA, C_ZA, 1024), (R_ZS, C_ZS, 2048),
    (R_XBC, C_XBC, 3072), (R_DT, C_DT, 32), (R_GA, C_GA, 1024), (R_GS, C_GS, 1024)]


def _pad_lanes(v, n=128):
    return jnp.pad(v, ((0, 0), (0, n - v.shape[1])))


def _reduce_pair(parts, tag):
    own, got = _exchange_pair(parts, "exchange_pair_" + tag)
    return _pair_sum(own, got, "pair_sum_" + tag)


def _device_step(h, tgt, w_alt, w_out, g_pre, conv_w8, conv_b, dt_bias, a_log, d_skip, sinks, g_ssm, g_post, on_mesh):
    dtb, al, dsk, snk = _pad_lanes(dt_bias), _pad_lanes(a_log), _pad_lanes(d_skip), _pad_lanes(sinks)
    u = _norm_u(h, g_pre)
    proj = _matmul(u, w_alt, "nt", F32, T, 896, D_MODEL, "in_proj")
    o = _attn_fwd(proj, snk)
    xbc_act = _conv_fwd(proj, conv_w8, conv_b)
    if on_mesh:
        sn, states, att_all, ssm_all, o_all = _ssd_fwd(xbc_act, proj, dtb, al, dsk, g_ssm, gather=w_out)
        w_att = att_all.reshape(D_MODEL, D_MODEL)
        w_ssm = ssm_all.reshape(SSM_INNER, D_MODEL)
        w_o = o_all.reshape(D_MODEL, D_MODEL)
    else:
        sn, states = _ssd_fwd(xbc_act, proj, dtb, al, dsk, g_ssm)
        w_att, w_ssm, w_o = w_out
    a_in, mg, ya, ys, out = _post_a(o, proj, sn, w_att, w_ssm, w_o)
    (loss, dres, dout, dya, dys, dga, dgs, do, dza, dsn, dgp) = _post_b(
        out, h, tgt, proj, ya, ys, o, g_post, w_att, w_ssm, w_o)
    dw_att = _matmul(a_in, dya, "tn", BF16, D_MODEL, D_MODEL, T, "d_w_att")
    dw_ssm = _matmul(sn, dys, "tn", BF16, D_MODEL, D_MODEL, T, "d_w_ssm")
    dw_o = _matmul(mg, dout, "tn", BF16, D_MODEL, D_MODEL, T, "d_w_o")
    res = {}
    if on_mesh:
        parts = [dw_att.reshape(N_DEV, 128, D_MODEL), dw_ssm.reshape(N_DEV, 256, D_MODEL),
                 dw_o.reshape(N_DEV, 128, D_MODEL)]
        (dxs, dbm, dcm, ddt4, dzs, ddtb, dal, ddsk, dgn, res["r_att"], res["r_ssm"], res["r_o"]) = _ssd_bwd(
            xbc_act, proj, dtb, al, dsk, g_ssm, states, dsn, exchange=parts)
    else:
        dxs, dbm, dcm, ddt4, dzs, ddtb, dal, ddsk, dgn = _ssd_bwd(xbc_act, proj, dtb, al, dsk, g_ssm, states, dsn)
        res.update(dw_att=dw_att, dw_ssm=dw_ssm, dw_o=dw_o)
    dxx, dwx, dbx = _conv_bwd(proj, conv_w8, conv_b, dxs, 0, "conv_bwd_x")
    dxb, dwb, dbb = _conv_bwd(proj, conv_w8, conv_b, dbm, SSM_INNER, "conv_bwd_b")
    dxc, dwc, dbc = _conv_bwd(proj, conv_w8, conv_b, dcm, SSM_INNER + GRP_W, "conv_bwd_c")
    dq, dk, dv, dsink = _attn_bwd(proj, snk, do)
    dproj = _assemble(dq, dza, dga, dgs, dzs, dxx, dxb, dxc, dk, dv, ddt4)
    dw_alt = _matmul(dproj, u, "tn", BF16, 896, D_MODEL, T, "d_w_in")
    if on_mesh:
        sums = _reduce_pair([_from_aligned_t(dw_alt)], "b")
        dh, dgpre, res["r_in"] = _d_u_norm(dproj, w_alt, h, g_pre, dres, chips=sums)
    else:
        dh, dgpre = _d_u_norm(dproj, w_alt, h, g_pre, dres)
        res["dw_alt"] = dw_alt
    small = (dgpre, dbx, dbb, dbc, ddtb, dal, ddsk, dsink, dgn, dgp, dwx, dwb, dwc)
    if on_mesh:
        (res["r_small"],) = _exchange_direct([_small_pack(*small, dh)], "exchange_small")
    else:
        res["small"] = small
    res.update(loss=loss[0, 0], dh=dh)
    return res


def kernel(x, meta_tokens, g_pre, w_in, conv_w, conv_b, dt_bias, a_log, d_skip, attn_sinks, g_ssm_norm, w_out_att, w_out_ssm, w_out, g_post, loss_target, m_meta_tokens, m_g_pre, m_w_in, m_conv_w, m_conv_b, m_dt_bias, m_a_log, m_d_skip, m_attn_sinks, m_g_ssm_norm, m_w_out_att, m_w_out_ssm, m_w_out, m_g_post, v_meta_tokens, v_g_pre, v_w_in, v_conv_w, v_conv_b, v_dt_bias, v_a_log, v_d_skip, v_attn_sinks, v_g_ssm_norm, v_w_out_att, v_w_out_ssm, v_w_out, v_g_post):
    w_in3, m_in3, v_in3 = _rows3(w_in), _rows3(m_w_in), _rows3(v_w_in)
    a_sh, att_sh, ssm_sh, o_sh = _cast_shards(w_in3, w_out_att[0], w_out_ssm[0], w_out[0])
    cw_sh = jnp.pad(conv_w[0], ((0, 4), (0, 0)))
    a_all, meta_all, cw_all = _all_gather([a_sh, meta_tokens, cw_sh])
    w_alt = _to_aligned_t(a_all)
    meta_full = meta_all.transpose(1, 0, 2).reshape(N_META, D_MODEL)
    conv_w8 = cw_all.transpose(1, 0, 2).reshape(8, CONV_DIM)

    h = jnp.concatenate([jnp.zeros((PAD, D_MODEL), F32), meta_full, x[0]], axis=0)
    tgt = jnp.concatenate([jnp.zeros((PAD + N_META, D_MODEL), F32), loss_target[0]], axis=0)
    r = _device_step(h, tgt, w_alt, (att_sh, ssm_sh, o_sh), g_pre, conv_w8, conv_b, dt_bias, a_log, d_skip,
                     attn_sinks, g_ssm_norm, g_post, True)
    loss = lax.psum(r["loss"], ("x", "y", "c"))
    grad_x = r["dh"][PAD + N_META:][None]

    res_in = [_unrows3(t) for t in _sum_adamw_rows3(r["r_in"], w_in3, m_in3, v_in3, "adamw_w_in")]
    res_att = [t[None] for t in _sum_adamw(r["r_att"], w_out_att[0], m_w_out_att[0], v_w_out_att[0], 512,
                                           "adamw_w_att")]
    res_ssm = [t[None] for t in _sum_adamw(r["r_ssm"], w_out_ssm[0], m_w_out_ssm[0], v_w_out_ssm[0], 512,
                                           "adamw_w_ssm")]
    res_o = [t[None] for t in _sum_adamw(r["r_o"], w_out[0], m_w_out[0], v_w_out[0], 512, "adamw_w_o")]
    (res_gpre, res_convb, res_dtb, res_alog, res_dskip, res_sink, res_gssm, res_gpost, res_cw, res_meta) = _small_finish(
        r["r_small"], [(g_pre, m_g_pre, v_g_pre), (conv_b, m_conv_b, v_conv_b), (dt_bias, m_dt_bias, v_dt_bias),
                       (a_log, m_a_log, v_a_log), (d_skip, m_d_skip, v_d_skip),
                       (attn_sinks, m_attn_sinks, v_attn_sinks), (g_ssm_norm, m_g_ssm_norm, v_g_ssm_norm),
                       (g_post, m_g_post, v_g_post), (conv_w[0], m_conv_w[0], v_conv_w[0]),
                       (meta_tokens, m_meta_tokens, v_meta_tokens)])
    res_cw = [t[None] for t in res_cw]
    per_weight = [res_meta, res_gpre, res_in, res_cw, res_convb, res_dtb, res_alog, res_dskip, res_sink, res_gssm,
                  res_att, res_ssm, res_o, res_gpost]
    return (loss, grad_x, *[p[0] for p in per_weight], *[p[1] for p in per_weight], *[p[2] for p in per_weight],
            *[p[3] for p in per_weight])
```

```python
import functools
import math

import jax
import jax.numpy as jnp
from jax import lax
from jax.experimental import pallas as pl
from jax.experimental.pallas import tpu as pltpu

F32 = jnp.float32
BF16 = jnp.bfloat16
SDS = jax.ShapeDtypeStruct
MESH = pl.DeviceIdType.MESH
ANY = pl.BlockSpec(memory_space=pl.ANY)

N_DEV = 8
D_MODEL = 1024
SEQ = 2048
N_META = 16
BLK = 128
PAD = 112
T = PAD + N_META + SEQ
NB = T // BLK
EPS = 1e-6
HEAD = 64
Q_HEADS = 16
KV_HEADS = 4
GROUP = 4
KV_W = 256
SSM_INNER = 2048
SSM_HEADS = 32
SSM_GROUPS = 4
GRP_W = 512
SSM_STATE = 128
CONV_DIM = 3072
IN_PROJ = 9760
SHARD_IN = IN_PROJ // N_DEV
NEG = -1e30

C_Q, C_ZA, C_GA, C_GS, C_ZS, C_XBC, C_K, C_V, C_DT = 0, 1024, 2048, 3072, 4096, 6144, 9216, 9472, 9728
PW = 9856
R_Q, R_K, R_V, R_ZA, R_ZS, R_XBC, R_DT, R_GA, R_GS = 0, 1024, 1280, 1536, 2560, 4608, 7680, 7712, 8736

ADAM_LR, ADAM_B1, ADAM_B2, ADAM_EPS, ADAM_WD, ADAM_STEP = 0.001, 0.9, 0.999, 1e-08, 0.01, 10

VMEM_LIMIT = 56 * 1024 * 1024


def _cparams():
    return pltpu.CompilerParams(vmem_limit_bytes=VMEM_LIMIT)


def _silu(x):
    return x * jax.nn.sigmoid(x)


def _dsilu(x):
    s = jax.nn.sigmoid(x)
    return s * (1.0 + x * (1.0 - s))


def _matmul(a, b, mode, out_dtype, tm, tn, name):
    if mode == "nt":
        (m, k), n = a.shape, b.shape[0]
        a_spec = pl.BlockSpec((tm, k), lambda i, j: (i, 0))
        b_spec = pl.BlockSpec((tn, k), lambda i, j: (j, 0))
        dims = (((1,), (1,)), ((), ()))
    else:
        assert mode == "tn"
        (k, m), n = a.shape, b.shape[1]
        a_spec = pl.BlockSpec((k, tm), lambda i, j: (0, i))
        b_spec = pl.BlockSpec((k, tn), lambda i, j: (0, j))
        dims = (((0,), (0,)), ((), ()))
    assert m % tm == 0 and n % tn == 0, (a.shape, b.shape, tm, tn)

    def body(a_ref, b_ref, o_ref):
        o_ref[...] = lax.dot_general(a_ref[...], b_ref[...], dims, preferred_element_type=F32).astype(out_dtype)

    return pl.pallas_call(
        body, grid=(m // tm, n // tn), in_specs=[a_spec, b_spec],
        out_specs=pl.BlockSpec((tm, tn), lambda i, j: (i, j)), out_shape=SDS((m, n), out_dtype),
        compiler_params=_cparams(), name=name)(a, b)


def _norm_u(h, g_pre):
    def body(h_ref, g_ref, u_ref):
        x = h_ref[...]
        r = lax.rsqrt(jnp.mean(x * x, axis=-1, keepdims=True) + EPS)
        u_ref[...] = (x * r * g_ref[...]).astype(BF16)

    return pl.pallas_call(
        body, grid=(NB,),
        in_specs=[pl.BlockSpec((BLK, D_MODEL), lambda i: (i, 0)), pl.BlockSpec((1, D_MODEL), lambda i: (0, 0))],
        out_specs=pl.BlockSpec((BLK, D_MODEL), lambda i: (i, 0)),
        out_shape=SDS((T, D_MODEL), BF16), name="norm_u")(h, g_pre)


DU_TM, DU_TK = T // 2, 1408


def _d_u_norm(dproj, w_alt, h, g_pre, dres, chips=()):
    nk = PW // DU_TK
    ni = T // DU_TM
    nc = len(chips)

    def body(*refs):
        a_ref, b_ref, h_ref, g_ref, dres_ref = refs[:5]
        dh_ref, dg_ref = refs[5 + nc:7 + nc]
        acc_ref = refs[7 + 2 * nc]
        i, kk = pl.program_id(0), pl.program_id(1)
        if nc:
            ch_start, ch_finish = _chips_program(refs[5:5 + nc], refs[7 + nc:7 + 2 * nc], refs[8 + 2 * nc:])
            pl.when((i == 0) & (kk == 0))(ch_start)
        part = jnp.dot(a_ref[...], b_ref[...], preferred_element_type=F32)

        @pl.when(kk == 0)
        def _():
            acc_ref[...] = part

        @pl.when((kk > 0) & (kk < nk - 1))
        def _():
            acc_ref[...] += part

        @pl.when(kk == nk - 1)
        def _():
            du_ = acc_ref[...] + part
            x = h_ref[...]
            r = lax.rsqrt(jnp.mean(x * x, axis=-1, keepdims=True) + EPS)
            gd = g_ref[...] * du_
            dx = r * gd - x * (r * r * r) * jnp.mean(x * gd, axis=-1, keepdims=True)
            dh_ref[...] = dx + dres_ref[...]
            gpart = jnp.concatenate([jnp.sum(du_ * x * r, axis=0, keepdims=True), jnp.zeros((7, D_MODEL), F32)],
                                    axis=0)

            @pl.when(i == 0)
            def _():
                dg_ref[...] = gpart

            @pl.when(i > 0)
            def _():
                dg_ref[...] += gpart

        if nc:
            pl.when((i == ni - 1) & (kk == nk - 1))(ch_finish)

    row = pl.BlockSpec((DU_TM, D_MODEL), lambda i, kk: (i, 0))
    return pl.pallas_call(
        body, grid=(ni, nk),
        in_specs=[pl.BlockSpec((DU_TM, DU_TK), lambda i, kk: (i, kk)),
                  pl.BlockSpec((DU_TK, D_MODEL), lambda i, kk: (kk, 0)),
                  row, pl.BlockSpec((1, D_MODEL), lambda i, kk: (0, 0)), row] + [ANY] * nc,
        out_specs=[row, pl.BlockSpec((8, D_MODEL), lambda i, kk: (0, 0))] + [ANY] * nc,
        out_shape=[SDS((T, D_MODEL), F32), SDS((8, D_MODEL), F32)] + [SDS(p.shape, p.dtype) for p in chips],
        scratch_shapes=[pltpu.VMEM((DU_TM, D_MODEL), F32)] + (_chips_scratch(chips) if nc else []),
        compiler_params=_cparams(), name="d_u_norm")(dproj, w_alt, h, g_pre, dres, *chips)


def _lane_pick(row, h):
    lane = lax.broadcasted_iota(jnp.int32, row.shape, 1)
    return jnp.sum(jnp.where(lane == h, row, 0.0), axis=1, keepdims=True)


def _attn_fn(q4s, kcats, vcats, kms, vms, sinks, n):
    r = lax.broadcasted_iota(jnp.int32, (GROUP * BLK, 2 * BLK), 0)
    s = lax.broadcasted_iota(jnp.int32, (GROUP * BLK, 2 * BLK), 1)
    i = jnp.bitwise_and(r, BLK - 1)
    gi = jnp.right_shift(r, 7)
    rel = i - s + BLK
    k_pos = n * BLK - BLK + s
    band_ok = (rel >= 0) & (rel < BLK) & (k_pos >= PAD + N_META)
    relf = rel.astype(F32)
    rm = lax.broadcasted_iota(jnp.int32, (GROUP * BLK, N_META), 0)
    mm = lax.broadcasted_iota(jnp.int32, (GROUP * BLK, N_META), 1)
    meta_ok = (PAD + mm) <= (n * BLK + jnp.bitwise_and(rm, BLK - 1))
    gcol = jnp.right_shift(lax.broadcasted_iota(jnp.int32, (GROUP * BLK, 1), 0), 7)
    outs = []
    for kh in range(KV_HEADS):
        slopes = [2.0 ** (-8.0 * (kh * GROUP + g + 1) / Q_HEADS) for g in range(GROUP)]
        slope = jnp.where(gi == 0, slopes[0], jnp.where(gi == 1, slopes[1], jnp.where(gi == 2, slopes[2], slopes[3])))
        sk = [_lane_pick(sinks, kh * GROUP + g) for g in range(GROUP)]
        sink = jnp.where(gcol == 0, sk[0], jnp.where(gcol == 1, sk[1], jnp.where(gcol == 2, sk[2], sk[3])))
        qb = (q4s[kh] * (HEAD ** -0.5)).astype(BF16)
        sb = lax.dot_general(qb, kcats[kh].astype(BF16), (((1,), (1,)), ((), ())), preferred_element_type=F32)
        sb = jnp.where(band_ok, sb - slope * relf, NEG)
        sm = lax.dot_general(qb, kms[kh].astype(BF16), (((1,), (1,)), ((), ())), preferred_element_type=F32)
        sm = jnp.where(meta_ok, sm, NEG)
        mx = jnp.maximum(jnp.maximum(jnp.max(sb, axis=1, keepdims=True), jnp.max(sm, axis=1, keepdims=True)), sink)
        mx = lax.stop_gradient(mx)
        eb = jnp.exp(sb - mx)
        em = jnp.exp(sm - mx)
        es = jnp.exp(sink - mx)
        inv = 1.0 / (jnp.sum(eb, axis=1, keepdims=True) + jnp.sum(em, axis=1, keepdims=True) + es)
        pb = (eb * inv).astype(BF16)
        pm = (em * inv).astype(BF16)
        o4 = (jnp.dot(pm, vms[kh].astype(BF16), preferred_element_type=F32)
              + jnp.dot(pb, vcats[kh].astype(BF16), preferred_element_type=F32))
        outs.append(o4)
    return outs


def _attn_specs():
    prev = lambda n: jnp.maximum(n - 1, 0)
    return [
        pl.BlockSpec((BLK, D_MODEL), lambda n: (n, C_Q // D_MODEL)),
        pl.BlockSpec((BLK, KV_W), lambda n: (prev(n), C_K // KV_W)),
        pl.BlockSpec((BLK, KV_W), lambda n: (n, C_K // KV_W)),
        pl.BlockSpec((BLK, KV_W), lambda n: (prev(n), C_V // KV_W)),
        pl.BlockSpec((BLK, KV_W), lambda n: (n, C_V // KV_W)),
        pl.BlockSpec((N_META, KV_W), lambda n: (PAD // N_META, C_K // KV_W)),
        pl.BlockSpec((N_META, KV_W), lambda n: (PAD // N_META, C_V // KV_W)),
        pl.BlockSpec((1, 128), lambda n: (0, 0)),
    ]


def _attn_load(q_ref, kp_ref, kc_ref, vp_ref, vc_ref, km_ref, vm_ref):
    q4s, kcats, vcats, kms, vms = [], [], [], [], []
    for kh in range(KV_HEADS):
        q4s.append(jnp.concatenate(
            [q_ref[:, (kh * GROUP + g) * HEAD:(kh * GROUP + g + 1) * HEAD] for g in range(GROUP)], axis=0))
        cs = slice(kh * HEAD, (kh + 1) * HEAD)
        kcats.append(jnp.concatenate([kp_ref[:, cs], kc_ref[:, cs]], axis=0))
        vcats.append(jnp.concatenate([vp_ref[:, cs], vc_ref[:, cs]], axis=0))
        kms.append(km_ref[:, cs])
        vms.append(vm_ref[:, cs])
    return q4s, kcats, vcats, kms, vms


def _attn_fwd(proj, sinks):
    def body(q_ref, kp_ref, kc_ref, vp_ref, vc_ref, km_ref, vm_ref, s_ref, o_ref):
        n = pl.program_id(0)
        args = _attn_load(q_ref, kp_ref, kc_ref, vp_ref, vc_ref, km_ref, vm_ref)
        outs = _attn_fn(*args, s_ref[...], n)
        for kh in range(KV_HEADS):
            for g in range(GROUP):
                hh = kh * GROUP + g
                o_ref[:, hh * HEAD:(hh + 1) * HEAD] = outs[kh][g * BLK:(g + 1) * BLK]

    return pl.pallas_call(
        body, grid=(NB,), in_specs=_attn_specs(),
        out_specs=pl.BlockSpec((BLK, D_MODEL), lambda n: (n, 0)),
        out_shape=SDS((T, D_MODEL), F32), name="attn_fwd")(proj, proj, proj, proj, proj, proj, proj, sinks)


def _attn_bwd(proj, sinks, do):
    def body(q_ref, kp_ref, kc_ref, vp_ref, vc_ref, km_ref, vm_ref, s_ref, do_ref, dq_ref, dk_ref, dv_ref, ds_ref):
        n = pl.program_id(0)

        @pl.when(n == 0)
        def _():
            dk_ref[...] = jnp.zeros_like(dk_ref)
            dv_ref[...] = jnp.zeros_like(dv_ref)
            ds_ref[...] = jnp.zeros_like(ds_ref)

        args = _attn_load(q_ref, kp_ref, kc_ref, vp_ref, vc_ref, km_ref, vm_ref)
        _, vjp = jax.vjp(lambda a, b, c, d, e, f: _attn_fn(a, b, c, d, e, f, n), *args, s_ref[...])
        do_f = do_ref[...].astype(F32)
        cot = [jnp.concatenate([do_f[:, (kh * GROUP + g) * HEAD:(kh * GROUP + g + 1) * HEAD] for g in range(GROUP)],
                               axis=0) for kh in range(KV_HEADS)]
        dq4s, dkcats, dvcats, dkms, dvms, dsk = vjp(cot)
        ds_ref[0:1, :] += dsk
        cur = pl.ds(pl.multiple_of(n * BLK, BLK), BLK)
        meta = slice(PAD, PAD + N_META)
        for kh in range(KV_HEADS):
            cs = slice(kh * HEAD, (kh + 1) * HEAD)
            for g in range(GROUP):
                hh = kh * GROUP + g
                dq_ref[:, hh * HEAD:(hh + 1) * HEAD] = dq4s[kh][g * BLK:(g + 1) * BLK]
            dk_ref[cur, cs] += dkcats[kh][BLK:]
            dv_ref[cur, cs] += dvcats[kh][BLK:]
            dk_ref[meta, cs] += dkms[kh]
            dv_ref[meta, cs] += dvms[kh]

        @pl.when(n > 0)
        def _():
            prv = pl.ds(pl.multiple_of((n - 1) * BLK, BLK), BLK)
            for kh in range(KV_HEADS):
                cs = slice(kh * HEAD, (kh + 1) * HEAD)
                dk_ref[prv, cs] += dkcats[kh][:BLK]
                dv_ref[prv, cs] += dvcats[kh][:BLK]

    full_kv = pl.BlockSpec((T, KV_W), lambda n: (0, 0))
    return pl.pallas_call(
        body, grid=(NB,),
        in_specs=_attn_specs() + [pl.BlockSpec((BLK, D_MODEL), lambda n: (n, 0))],
        out_specs=[pl.BlockSpec((BLK, D_MODEL), lambda n: (n, 0)), full_kv, full_kv,
                   pl.BlockSpec((8, 128), lambda n: (0, 0))],
        out_shape=[SDS((T, D_MODEL), F32), SDS((T, KV_W), F32), SDS((T, KV_W), F32), SDS((8, 128), F32)],
        name="attn_bwd")(proj, proj, proj, proj, proj, proj, proj, sinks, do)


def _conv_taps(xp, w, rows):
    return (w[0:1] * xp[5:5 + rows] + w[1:2] * xp[6:6 + rows] + w[2:3] * xp[7:7 + rows] + w[3:4] * xp[8:8 + rows])


def _conv_fwd(proj, conv_w, conv_b):
    CONV_CB = CONV_DIM
    ncb = CONV_DIM // CONV_CB
    cb0 = C_XBC // CONV_CB

    def body(tail_ref, cur_ref, w_ref, b_ref, o_ref):
        n = pl.program_id(1)
        tail = jnp.where(n > 0, tail_ref[...], 0.0)
        xp = jnp.concatenate([tail, cur_ref[...]], axis=0)
        conv = _conv_taps(xp, w_ref[...], BLK) + b_ref[...]
        row = n * BLK + lax.broadcasted_iota(jnp.int32, (BLK, 1), 0)
        o_ref[...] = jnp.where(row >= PAD, _silu(conv), 0.0)

    return pl.pallas_call(
        body, grid=(ncb, NB),
        in_specs=[pl.BlockSpec((8, CONV_CB), lambda j, n: (jnp.maximum(n * (BLK // 8) - 1, 0), cb0 + j)),
                  pl.BlockSpec((BLK, CONV_CB), lambda j, n: (n, cb0 + j)),
                  pl.BlockSpec((8, CONV_CB), lambda j, n: (0, j)),
                  pl.BlockSpec((1, CONV_CB), lambda j, n: (0, j))],
        out_specs=pl.BlockSpec((BLK, CONV_CB), lambda j, n: (n, j)),
        out_shape=SDS((T, CONV_DIM), F32), name="conv_fwd")(proj, proj, conv_w, conv_b)


def _conv_bwd(proj, conv_w, conv_b, dact, ch0, name):
    width = dact.shape[1]
    CONV_CB = width
    ncb = width // CONV_CB
    cb0 = (C_XBC + ch0) // CONV_CB
    wb0 = ch0 // CONV_CB
    last8 = T // 8 - 1

    def body(tail_ref, cur_ref, nxt_ref, w_ref, b_ref, dcur_ref, dnxt_ref, dx_ref, dw_ref, db_ref):
        n = pl.program_id(1)
        w = w_ref[...]
        tail = jnp.where(n > 0, tail_ref[...], 0.0)
        xp = jnp.concatenate([tail, cur_ref[...], nxt_ref[...]], axis=0)
        conv = _conv_taps(xp, w, BLK + 8) + b_ref[...]
        dext = jnp.concatenate([dcur_ref[...], jnp.where(n < NB - 1, dnxt_ref[...], 0.0)], axis=0)
        row = n * BLK + lax.broadcasted_iota(jnp.int32, (BLK + 8, 1), 0)
        dconv = jnp.where(row >= PAD, dext * _dsilu(conv), 0.0)
        dx = (w[0:1] * dconv[3:3 + BLK] + w[1:2] * dconv[2:2 + BLK] + w[2:3] * dconv[1:1 + BLK]
              + w[3:4] * dconv[0:BLK])
        dx_ref[...] = dx.astype(BF16)
        dc = dconv[0:BLK]
        dws = [jnp.sum(dc * xp[5 + k:5 + k + BLK], axis=0, keepdims=True) for k in range(4)]
        dwp = jnp.concatenate(dws + [jnp.zeros((4, CONV_CB), F32)], axis=0)
        dbp = jnp.sum(dc, axis=0, keepdims=True)

        @pl.when(n == 0)
        def _():
            dw_ref[...] = dwp
            db_ref[...] = jnp.concatenate([dbp, jnp.zeros((7, CONV_CB), F32)], axis=0)

        @pl.when(n > 0)
        def _():
            dw_ref[...] += dwp
            db_ref[0:1, :] += dbp

    return pl.pallas_call(
        body, grid=(ncb, NB),
        in_specs=[pl.BlockSpec((8, CONV_CB), lambda j, n: (jnp.maximum(n * (BLK // 8) - 1, 0), cb0 + j)),
                  pl.BlockSpec((BLK, CONV_CB), lambda j, n: (n, cb0 + j)),
                  pl.BlockSpec((8, CONV_CB), lambda j, n: (jnp.minimum((n + 1) * (BLK // 8), last8), cb0 + j)),
                  pl.BlockSpec((8, CONV_CB), lambda j, n: (0, wb0 + j)),
                  pl.BlockSpec((1, CONV_CB), lambda j, n: (0, wb0 + j)),
                  pl.BlockSpec((BLK, CONV_CB), lambda j, n: (n, j)),
                  pl.BlockSpec((8, CONV_CB), lambda j, n: (jnp.minimum((n + 1) * (BLK // 8), last8), j))],
        out_specs=[pl.BlockSpec((BLK, CONV_CB), lambda j, n: (n, j)),
                   pl.BlockSpec((8, CONV_CB), lambda j, n: (0, j)),
                   pl.BlockSpec((8, CONV_CB), lambda j, n: (0, j))],
        out_shape=[SDS((T, width), BF16), SDS((8, width), F32), SDS((8, width), F32)],
        name=name)(proj, proj, proj, conv_w, conv_b, dact, dact)


HPG = SSM_HEADS // SSM_GROUPS


def _iota(shape, dim):
    return lax.broadcasted_iota(jnp.int32, shape, dim)


def _mm(a, b, ca=1, cb=0):
    return lax.dot_general(a.astype(BF16), b.astype(BF16), (((ca,), (cb,)), ((), ())), preferred_element_type=F32)


def _split3(v):
    hi = v.astype(BF16)
    r1 = v - hi.astype(F32)
    mid = r1.astype(BF16)
    lo = (r1 - mid.astype(F32)).astype(BF16)
    return hi, mid, lo


def _sel_r(parts, onehot, ca=1, cb=0):
    out = lax.dot_general(parts[0], onehot, (((ca,), (cb,)), ((), ())), preferred_element_type=F32)
    for p in parts[1:]:
        out = out + lax.dot_general(p, onehot, (((ca,), (cb,)), ((), ())), preferred_element_type=F32)
    return out


def _sel_l(onehot, parts):
    out = jnp.dot(onehot, parts[0], preferred_element_type=F32)
    for p in parts[1:]:
        out = out + jnp.dot(onehot, p, preferred_element_type=F32)
    return out


def _rows8(*rows):
    r = _iota((8, rows[0].shape[1]), 0)
    out = jnp.zeros((8, rows[0].shape[1]), F32)
    for k, v in enumerate(rows):
        out = jnp.where(r == k, v, out)
    return out


def _ssd_forward(x, z, bm, cm, dt_raw, st_prev, dtb, alog, dskip, gn, g, cst_scr):
    li, si = _iota((BLK, BLK), 0), _iota((BLK, BLK), 1)
    dt_all = jax.nn.softplus(dt_raw + dtb)
    a_row = -jnp.exp(alog)
    a_all = dt_all * a_row
    cs_all = _sel_l((li >= si).astype(BF16), _split3(a_all))
    cs_parts = _split3(cs_all)
    spread = (_iota((BLK, GRP_W), 0) == g * HPG + jnp.right_shift(_iota((BLK, GRP_W), 1), 6)).astype(BF16)
    dt_e = _sel_r(_split3(dt_all), spread)
    cs_e = _sel_r(cs_parts, spread)
    d_e = _sel_r(_split3(_rows8(dskip)), spread)[0:1]
    cs_last_e = jnp.sum(jnp.where(_iota((BLK, GRP_W), 0) == BLK - 1, cs_e, 0.0), axis=0, keepdims=True)
    p_e = jnp.exp(cs_e)
    w_e = jnp.exp(cs_last_e - cs_e)
    cd_e = jnp.exp(cs_last_e)
    xr = x * dt_e
    cst_scr[...] = cs_all.T
    cst_g = cst_scr[pl.ds(pl.multiple_of(g * HPG, HPG), HPG), :]
    own = jnp.right_shift(_iota((HPG, HPG * BLK), 1), 7) == _iota((HPG, HPG * BLK), 0)
    ownf = own.astype(F32)
    q_rows = [ownf, ownf, ownf] + [jnp.where(own, jnp.concatenate([p.astype(F32)] * HPG, axis=1), 0.0)
                                   for p in _split3(cst_g)]
    q2 = jnp.concatenate(q_rows + [jnp.zeros((BLK - 6 * HPG, HPG * BLK), F32)], axis=0).astype(BF16)
    lane1 = _iota((1, BLK), 1)
    p2 = jnp.where((lane1 >= 3 * HPG) & (lane1 < 6 * HPG), -1.0, 0.0)
    for k, part in enumerate(cs_parts):
        pick = ((li == g * HPG + si - k * HPG) & (si >= k * HPG) & (si < (k + 1) * HPG)).astype(BF16)
        p2 = p2 + jnp.dot(part, pick, preferred_element_type=F32)
    dmat = jnp.dot(p2.astype(BF16), q2, preferred_element_type=F32)
    causal = _iota((BLK, HPG * BLK), 0) >= jnp.bitwise_and(_iota((BLK, HPG * BLK), 1), BLK - 1)
    lam = jnp.exp(jnp.where(causal, dmat, NEG))
    gmat = _mm(cm, bm, 1, 1)
    m_all = lam * jnp.concatenate([gmat] * HPG, axis=1)
    mb = m_all.astype(BF16)
    lo = _iota((BLK, BLK), 1) < HEAD
    xrb = xr.astype(BF16)
    zero = jnp.zeros((BLK, BLK), BF16)
    bds, yd = [], []
    for i in range(HPG // 2):
        t = xrb[:, BLK * i:BLK * (i + 1)]
        bd = jnp.concatenate([jnp.where(lo, t, zero), jnp.where(lo, zero, t)], axis=0)
        bds.append(bd)
        yd.append(jnp.dot(mb[:, 2 * BLK * i:2 * BLK * (i + 1)], bd, preferred_element_type=F32))
    cs_st = _mm(cm, st_prev)
    y = jnp.concatenate(yd, axis=1) + cs_st * p_e + d_e * x
    xrw = xr * w_e
    st_new = cd_e * st_prev + _mm(bm, xrw, 0, 0)
    yz = y * _silu(z)
    rn = lax.rsqrt(jnp.sum(yz * yz, axis=1, keepdims=True) / GRP_W + EPS)
    return dict(out=yz * rn * gn, st_new=st_new, dt_all=dt_all, a_row=a_row, dt_e=dt_e, d_e=d_e, p_e=p_e, w_e=w_e,
                cd_e=cd_e, xr=xr, xrw=xrw, lam=lam, m_all=m_all, mb=mb, bds=bds, cs_st=cs_st, y=y, yz=yz, rn=rn, lo=lo)


def _ssd_backward(f, x, z, bm, cm, dt_raw, st_prev, dtb, gn, g, dout, dst_next, cst_scr):
    li, si = _iota((BLK, BLK), 0), _iota((BLK, BLK), 1)
    yz, rn, y, p_e, w_e, cd_e, xr = f["yz"], f["rn"], f["y"], f["p_e"], f["w_e"], f["cd_e"], f["xr"]
    dgn = jnp.sum(dout * yz * rn, axis=0, keepdims=True)
    t = dout * gn
    dyz = rn * t - yz * (rn * rn * rn) * (jnp.sum(yz * t, axis=1, keepdims=True) / GRP_W)
    dy = dyz * _silu(z)
    dz = dyz * y * _dsilu(z)
    dx = f["d_e"] * dy
    dd_e = jnp.sum(dy * x, axis=0, keepdims=True)
    dcsst = dy * p_e
    dp_e = dy * f["cs_st"]
    dcm = _mm(dcsst, st_prev, 1, 1)
    dst_prev = _mm(cm, dcsst, 0, 0) + cd_e * dst_next
    dcd_e = jnp.sum(dst_next * st_prev, axis=0, keepdims=True)
    dbm = _mm(f["xrw"], dst_next, 1, 1)
    dxrw = _mm(bm, dst_next)
    dxr = dxrw * w_e
    dw_e = dxrw * xr
    dyb = dy.astype(BF16)
    dms, dxr_d = [], []
    for i in range(HPG // 2):
        dyp = dyb[:, BLK * i:BLK * (i + 1)]
        dms.append(lax.dot_general(dyp, f["bds"][i], (((1,), (1,)), ((), ())), preferred_element_type=F32))
        r = lax.dot_general(f["mb"][:, 2 * BLK * i:2 * BLK * (i + 1)], dyp, (((0,), (0,)), ((), ())),
                            preferred_element_type=F32)
        dxr_d.append(jnp.where(f["lo"], r[0:BLK], r[BLK:2 * BLK]))
    dm_all = jnp.concatenate(dms, axis=1)
    dxr = dxr + jnp.concatenate(dxr_d, axis=1)
    dlg = dm_all * f["lam"]
    dg = dlg[:, 0:BLK]
    for j in range(1, HPG):
        dg = dg + dlg[:, BLK * j:BLK * (j + 1)]
    dcm = dcm + _mm(dg, bm)
    dbm = dbm + _mm(dg, cm, 0, 0)
    q_all = dm_all * f["m_all"]
    col_sums = jnp.sum(q_all, axis=0, keepdims=True)
    cst_scr[...] = jnp.zeros_like(cst_scr)
    cst_scr[pl.ds(pl.multiple_of(g * HPG, HPG), HPG), :] = _rows8(
        *[col_sums[:, BLK * j:BLK * (j + 1)] for j in range(HPG)])
    dcs = -cst_scr[...].T
    for j in range(HPG):
        dcs = dcs + jnp.where(si == g * HPG + j,
                              jnp.sum(q_all[:, BLK * j:BLK * (j + 1)], axis=1, keepdims=True), 0.0)
    unspread = (_iota((GRP_W, BLK), 1) == g * HPG + jnp.right_shift(_iota((GRP_W, BLK), 0), 6)).astype(BF16)
    dww = dw_e * w_e
    per_head = _sel_r(_split3(jnp.concatenate([dp_e * p_e - dww, dxr * x], axis=0)), unspread)
    last = _sel_r(_split3(_rows8(jnp.sum(dww, axis=0, keepdims=True) + dcd_e * cd_e, dd_e)), unspread)
    dcs = dcs + per_head[0:BLK] + jnp.where(li == BLK - 1, last[0:1], 0.0)
    da = _sel_l((si >= li).astype(BF16), _split3(dcs))
    ddt_all = da * f["a_row"] + per_head[BLK:2 * BLK]
    dalog = jnp.sum(da * f["dt_all"], axis=0, keepdims=True) * f["a_row"]
    dx = dx + dxr * f["dt_e"]
    ddt_raw = ddt_all * jax.nn.sigmoid(dt_raw + dtb)
    ddtb = jnp.sum(ddt_raw, axis=0, keepdims=True)
    ddskip = last[1:2]
    return dict(dx=dx, dz=dz, dbm=dbm, dcm=dcm, ddt_raw=ddt_raw, dst_prev=dst_prev, ddtb=ddtb, dalog=dalog,
                ddskip=ddskip, dgn=dgn)


def _ssd_in_specs(rev):
    cidx = (lambda c: NB - 1 - c) if rev else (lambda c: c)
    return [
        pl.BlockSpec((BLK, GRP_W), lambda g, c: (cidx(c), g)),
        pl.BlockSpec((BLK, SSM_STATE), lambda g, c: (cidx(c), SSM_INNER // SSM_STATE + g)),
        pl.BlockSpec((BLK, SSM_STATE), lambda g, c: (cidx(c), SSM_INNER // SSM_STATE + SSM_GROUPS + g)),
        pl.BlockSpec((BLK, 128), lambda g, c: (cidx(c), C_DT // 128)),
        pl.BlockSpec((BLK, GRP_W), lambda g, c: (cidx(c), C_ZS // GRP_W + g)),
        pl.BlockSpec((1, 128), lambda g, c: (0, 0)),
        pl.BlockSpec((1, 128), lambda g, c: (0, 0)),
        pl.BlockSpec((1, 128), lambda g, c: (0, 0)),
        pl.BlockSpec((1, GRP_W), lambda g, c: (0, g)),
    ]


def _ssd_fwd(xbc_act, proj, dt_bias, a_log, d_skip, g_norm, gather=()):
    ng = len(gather)

    def body(*refs):
        xs_ref, b_ref, c_ref, dt_ref, z_ref, dtb_ref, al_ref, dsk_ref, gn_ref = refs[:9]
        y_ref, st_ref = refs[9 + ng:11 + ng]
        s_scr, cst_scr = refs[11 + 2 * ng:13 + 2 * ng]
        g = pl.program_id(0)
        c = pl.program_id(1)
        if ng:
            ag_start, ag_forward, ag_finish = _ag_program(refs[9:9 + ng], refs[11 + ng:11 + 2 * ng],
                                                          refs[13 + 2 * ng:])
            pl.when((g == 0) & (c == 0))(ag_start)
            pl.when((g == SSM_GROUPS // 2) & (c == 0))(ag_forward)

        @pl.when(c == 0)
        def _():
            s_scr[...] = jnp.zeros_like(s_scr)

        st_prev = s_scr[...]
        st_ref[0, 0] = st_prev
        f = _ssd_forward(xs_ref[...], z_ref[...], b_ref[...], c_ref[...], dt_ref[...], st_prev, dtb_ref[...],
                         al_ref[...], dsk_ref[...], gn_ref[...], g, cst_scr)
        y_ref[...] = f["out"].astype(BF16)
        s_scr[...] = f["st_new"]
        if ng:
            pl.when((g == SSM_GROUPS - 1) & (c == NB - 1))(ag_finish)

    return pl.pallas_call(
        body, grid=(SSM_GROUPS, NB), in_specs=_ssd_in_specs(False) + [ANY] * ng,
        out_specs=[pl.BlockSpec((BLK, GRP_W), lambda g, c: (c, g)),
                   pl.BlockSpec((1, 1, SSM_STATE, GRP_W), lambda g, c: (g, c, 0, 0))] + [ANY] * ng,
        out_shape=[SDS((T, SSM_INNER), BF16), SDS((SSM_GROUPS, NB, SSM_STATE, GRP_W), F32)]
        + [SDS((N_DEV,) + s.shape, s.dtype) for s in gather],
        scratch_shapes=[pltpu.VMEM((SSM_STATE, GRP_W), F32), pltpu.VMEM((BLK, BLK), F32)]
        + (_ag_scratch(gather) if ng else []),
        compiler_params=_cparams(),
        name="ssd_fwd")(xbc_act, xbc_act, xbc_act, proj, proj, dt_bias, a_log, d_skip, g_norm, *gather)


def _ssd_bwd(xbc_act, proj, dt_bias, a_log, d_skip, g_norm, states, dy, exchange=()):
    chips = exchange
    nc = len(chips)

    def body(*refs):
        xs_ref, b_ref, c_ref, dt_ref, z_ref, dtb_ref, al_ref, dsk_ref, gn_ref, st_ref, dy_ref = refs[:11]
        (dxs_ref, db_ref, dc_ref, ddt_ref, dz_ref, ddtb_ref, dal_ref, ddsk_ref, dgn_ref) = refs[11 + nc:20 + nc]
        ds_scr, cst_scr = refs[20 + 2 * nc:22 + 2 * nc]
        g = pl.program_id(0)
        c = pl.program_id(1)
        if nc:
            ch_start, ch_finish = _direct_program(refs[11:11 + nc], refs[20 + nc:20 + 2 * nc], refs[22 + 2 * nc:])
            pl.when((g == 0) & (c == 0))(ch_start)

        @pl.when(c == 0)
        def _():
            ds_scr[...] = jnp.zeros_like(ds_scr)
            dgn_ref[...] = jnp.zeros_like(dgn_ref)

        @pl.when((c == 0) & (g == 0))
        def _():
            ddtb_ref[...] = jnp.zeros_like(ddtb_ref)
            dal_ref[...] = jnp.zeros_like(dal_ref)
            ddsk_ref[...] = jnp.zeros_like(ddsk_ref)

        x, z, bm, cm, dt_raw, st_prev = xs_ref[...], z_ref[...], b_ref[...], c_ref[...], dt_ref[...], st_ref[0, 0]
        f = _ssd_forward(x, z, bm, cm, dt_raw, st_prev, dtb_ref[...], al_ref[...], dsk_ref[...], gn_ref[...], g,
                         cst_scr)
        d = _ssd_backward(f, x, z, bm, cm, dt_raw, st_prev, dtb_ref[...], gn_ref[...], g, dy_ref[...].astype(F32),
                          ds_scr[...], cst_scr)
        dxs_ref[...] = d["dx"]
        dz_ref[...] = d["dz"].astype(BF16)
        ds_scr[...] = d["dst_prev"]
        db_ref[...] = d["dbm"]
        dc_ref[...] = d["dcm"]
        ddt_ref[...] = d["ddt_raw"]
        dgn_ref[0:1, :] += d["dgn"]
        ddtb_ref[0:1, :] += d["ddtb"]
        dal_ref[0:1, :] += d["dalog"]
        ddsk_ref[0:1, :] += d["ddskip"]
        if nc:
            pl.when((g == SSM_GROUPS - 1) & (c == NB - 1))(ch_finish)

    rc = lambda c: NB - 1 - c
    small = pl.BlockSpec((8, 128), lambda g, c: (0, 0))
    return pl.pallas_call(
        body, grid=(SSM_GROUPS, NB),
        in_specs=_ssd_in_specs(True) + [
            pl.BlockSpec((1, 1, SSM_STATE, GRP_W), lambda g, c: (g, rc(c), 0, 0)),
            pl.BlockSpec((BLK, GRP_W), lambda g, c: (rc(c), g))] + [ANY] * nc,
        out_specs=[pl.BlockSpec((BLK, GRP_W), lambda g, c: (rc(c), g)),
                   pl.BlockSpec((BLK, SSM_STATE), lambda g, c: (rc(c), g)),
                   pl.BlockSpec((BLK, SSM_STATE), lambda g, c: (rc(c), g)),
                   pl.BlockSpec((BLK, 128), lambda g, c: (rc(c), g)),
                   pl.BlockSpec((BLK, GRP_W), lambda g, c: (rc(c), g)),
                   small, small, small,
                   pl.BlockSpec((8, GRP_W), lambda g, c: (0, g))] + [ANY] * nc,
        out_shape=[SDS((T, SSM_INNER), F32), SDS((T, GRP_W), F32), SDS((T, GRP_W), F32), SDS((T, GRP_W), F32),
                   SDS((T, SSM_INNER), BF16), SDS((8, 128), F32), SDS((8, 128), F32), SDS((8, 128), F32),
                   SDS((8, SSM_INNER), F32)] + [SDS(p.shape, p.dtype) for p in chips],
        scratch_shapes=[pltpu.VMEM((SSM_STATE, GRP_W), F32), pltpu.VMEM((BLK, BLK), F32)]
        + (_direct_scratch(chips) if nc else []),
        compiler_params=_cparams(),
        name="ssd_bwd")(xbc_act, xbc_act, xbc_act, proj, proj, dt_bias, a_log, d_skip, g_norm, states, dy, *chips)


POST_R = 272


def _post_a(o, proj, sn, w_att, w_ssm, w_o):
    def body(o_ref, za_ref, ga_ref, gs_ref, sn_ref, wa_ref, ws_ref, wo_ref, a_ref, mg_ref, ya_ref, ys_ref, out_ref):
        a = (o_ref[...] * _silu(za_ref[...])).astype(BF16)
        a_ref[...] = a
        ya = jnp.dot(a, wa_ref[...], preferred_element_type=F32)
        ys = jnp.dot(sn_ref[...], ws_ref[...], preferred_element_type=F32)
        ya_ref[...] = ya.astype(BF16)
        ys_ref[...] = ys.astype(BF16)
        mg = (jax.nn.sigmoid(ga_ref[...]) * ya + jax.nn.sigmoid(gs_ref[...]) * ys).astype(BF16)
        mg_ref[...] = mg
        out_ref[...] = jnp.dot(mg, wo_ref[...], preferred_element_type=F32)

    row = pl.BlockSpec((POST_R, D_MODEL), lambda i: (i, 0))
    pcol = lambda c0: pl.BlockSpec((POST_R, D_MODEL), lambda i: (i, c0 // D_MODEL))
    full = lambda r: pl.BlockSpec((r, D_MODEL), lambda i: (0, 0))
    return pl.pallas_call(
        body, grid=(T // POST_R,),
        in_specs=[row, pcol(C_ZA), pcol(C_GA), pcol(C_GS), pl.BlockSpec((POST_R, SSM_INNER), lambda i: (i, 0)),
                  full(D_MODEL), full(SSM_INNER), full(D_MODEL)],
        out_specs=[row, row, row, row, row],
        out_shape=[SDS((T, D_MODEL), BF16), SDS((T, D_MODEL), BF16), SDS((T, D_MODEL), BF16), SDS((T, D_MODEL), BF16),
                   SDS((T, D_MODEL), F32)],
        compiler_params=_cparams(), name="post_a")(o, proj, proj, proj, sn, w_att, w_ssm, w_o)


def _post_b(out, h, tgt, proj, ya, ys, o, g_post, w_att, w_ssm, w_o):
    def body(out_ref, h_ref, t_ref, za_ref, ga_ref, gs_ref, ya_ref, ys_ref, o_ref, gp_ref, wa_ref, ws_ref, wo_ref,
             loss_ref, dres_ref, dout_ref, dya_ref, dys_ref, dga_ref, dgs_ref, do_ref, dza_ref, dsn_ref, dgp_ref):
        i = pl.program_id(0)
        x = out_ref[...]
        gp = gp_ref[...]
        r = lax.rsqrt(jnp.mean(x * x, axis=-1, keepdims=True) + EPS)
        row = i * POST_R + lax.broadcasted_iota(jnp.int32, (POST_R, 1), 0)
        res = h_ref[...] + jnp.where(row >= PAD, x * r * gp, 0.0)
        live = row >= PAD + N_META
        err = jnp.where(live, res - t_ref[...], 0.0)
        lpart = 0.5 * jnp.sum(jnp.sum(err * err, axis=1, keepdims=True) / D_MODEL, axis=0, keepdims=True)
        dres = err / D_MODEL
        dres_ref[...] = dres
        gpart = jnp.sum(dres * x * r, axis=0, keepdims=True)

        @pl.when(i == 0)
        def _():
            loss_ref[...] = jnp.zeros_like(loss_ref)
            dgp_ref[...] = jnp.zeros_like(dgp_ref)

        loss_ref[...] += jnp.broadcast_to(lpart, loss_ref.shape)
        dgp_ref[0:1, :] += gpart
        gd = gp * dres
        dout = (r * gd - x * (r * r * r) * jnp.mean(x * gd, axis=-1, keepdims=True)).astype(BF16)
        dout_ref[...] = dout
        dmg = lax.dot_general(dout, wo_ref[...], (((1,), (1,)), ((), ())), preferred_element_type=F32)
        sga = jax.nn.sigmoid(ga_ref[...])
        sgs = jax.nn.sigmoid(gs_ref[...])
        dya = (dmg * sga).astype(BF16)
        dys = (dmg * sgs).astype(BF16)
        dya_ref[...] = dya
        dys_ref[...] = dys
        dga_ref[...] = (dmg * ya_ref[...].astype(F32) * sga * (1.0 - sga)).astype(BF16)
        dgs_ref[...] = (dmg * ys_ref[...].astype(F32) * sgs * (1.0 - sgs)).astype(BF16)
        da = lax.dot_general(dya, wa_ref[...], (((1,), (1,)), ((), ())), preferred_element_type=F32)
        za = za_ref[...]
        do_ref[...] = (da * _silu(za)).astype(BF16)
        dza_ref[...] = (da * o_ref[...] * _dsilu(za)).astype(BF16)
        dsn_ref[...] = lax.dot_general(dys, ws_ref[...], (((1,), (1,)), ((), ())),
                                       preferred_element_type=F32).astype(BF16)

    row = pl.BlockSpec((POST_R, D_MODEL), lambda i: (i, 0))
    pcol = lambda c0: pl.BlockSpec((POST_R, D_MODEL), lambda i: (i, c0 // D_MODEL))
    full = lambda r: pl.BlockSpec((r, D_MODEL), lambda i: (0, 0))
    small = pl.BlockSpec((8, D_MODEL), lambda i: (0, 0))
    return pl.pallas_call(
        body, grid=(T // POST_R,),
        in_specs=[row, row, row, pcol(C_ZA), pcol(C_GA), pcol(C_GS), row, row, row,
                  pl.BlockSpec((1, D_MODEL), lambda i: (0, 0)), full(D_MODEL), full(SSM_INNER), full(D_MODEL)],
        out_specs=[pl.BlockSpec((8, 128), lambda i: (0, 0)), row, row, row, row, row, row, row, row,
                   pl.BlockSpec((POST_R, SSM_INNER), lambda i: (i, 0)), small],
        out_shape=[SDS((8, 128), F32), SDS((T, D_MODEL), F32), SDS((T, D_MODEL), BF16), SDS((T, D_MODEL), BF16),
                   SDS((T, D_MODEL), BF16), SDS((T, D_MODEL), BF16), SDS((T, D_MODEL), BF16), SDS((T, D_MODEL), BF16),
                   SDS((T, D_MODEL), BF16), SDS((T, SSM_INNER), BF16), SDS((8, D_MODEL), F32)],
        compiler_params=_cparams(), name="post_b")(out, h, tgt, proj, proj, proj, ya, ys, o, g_post, w_att, w_ssm, w_o)


def _assemble(dq, dza, dga, dgs, dzs, dxx, dxb, dxc, dk, dv, ddt4):
    def body(dq_ref, dza_ref, dga_ref, dgs_ref, dzs_ref, dxx_ref, dxb_ref, dxc_ref, dk_ref, dv_ref, ddt_ref, o_ref):
        o_ref[:, C_Q:C_Q + D_MODEL] = dq_ref[...].astype(BF16)
        o_ref[:, C_ZA:C_ZA + D_MODEL] = dza_ref[...]
        o_ref[:, C_GA:C_GA + D_MODEL] = dga_ref[...]
        o_ref[:, C_GS:C_GS + D_MODEL] = dgs_ref[...]
        o_ref[:, C_ZS:C_ZS + SSM_INNER] = dzs_ref[...]
        o_ref[:, C_XBC:C_XBC + SSM_INNER] = dxx_ref[...]
        o_ref[:, C_XBC + SSM_INNER:C_XBC + SSM_INNER + GRP_W] = dxb_ref[...]
        o_ref[:, C_XBC + SSM_INNER + GRP_W:C_XBC + CONV_DIM] = dxc_ref[...]
        o_ref[:, C_K:C_K + KV_W] = dk_ref[...].astype(BF16)
        o_ref[:, C_V:C_V + KV_W] = dv_ref[...].astype(BF16)
        d4 = ddt_ref[...]
        o_ref[:, C_DT:C_DT + 128] = (d4[:, 0:128] + d4[:, 128:256] + d4[:, 256:384] + d4[:, 384:512]).astype(BF16)

    spec = lambda w: pl.BlockSpec((BLK, w), lambda i: (i, 0))
    ins = [dq, dza, dga, dgs, dzs, dxx, dxb, dxc, dk, dv, ddt4]
    return pl.pallas_call(
        body, grid=(NB,), in_specs=[spec(a.shape[1]) for a in ins], out_specs=spec(PW),
        out_shape=SDS((T, PW), BF16), name="assemble")(*ins)


def _adamw_math(w, g, m, v):
    m = ADAM_B1 * m + (1.0 - ADAM_B1) * g
    v = ADAM_B2 * v + (1.0 - ADAM_B2) * (g * g)
    m_hat = m / (1.0 - ADAM_B1 ** ADAM_STEP)
    v_hat = v / (1.0 - ADAM_B2 ** ADAM_STEP)
    delta = -ADAM_LR * (m_hat / (jnp.sqrt(v_hat) + ADAM_EPS) + ADAM_WD * w)
    return delta, m, v


def _sum_adamw(recv, w, m, v, tc, name):
    rows, cols = w.shape
    nslab = recv.shape[0]
    assert cols % tc == 0

    def body(r_ref, w_ref, m_ref, v_ref, g_ref, d_ref, nm_ref, nv_ref):
        g = r_ref[0].astype(F32)
        for d in range(1, nslab):
            g = g + r_ref[d].astype(F32)
        g_ref[...] = g
        delta, nm, nv = _adamw_math(w_ref[...], g, m_ref[...], v_ref[...])
        d_ref[...] = delta
        nm_ref[...] = nm
        nv_ref[...] = nv

    blk = pl.BlockSpec((rows, tc), lambda i: (0, i))
    return pl.pallas_call(
        body, grid=(cols // tc,),
        in_specs=[pl.BlockSpec((nslab, rows, tc), lambda i: (0, 0, i)), blk, blk, blk],
        out_specs=[blk, blk, blk, blk], out_shape=[SDS((rows, cols), F32)] * 4,
        compiler_params=_cparams(), name=name)(recv, w, m, v)


def _sum_adamw_rows3(recv, w3, m3, v3, name, exchange=()):
    pairs = 61
    assert (SHARD_IN // 2) % pairs == 0
    nsteps = SHARD_IN // 2 // pairs
    ne = len(exchange)

    def body(*refs):
        r_ref, w_ref, m_ref, v_ref = refs[:4]
        g_ref, d_ref, nm_ref, nv_ref = refs[4 + ne:8 + ne]
        if ne:
            ex_start, ex_finish = _direct_program(refs[4:4 + ne], refs[8 + ne:8 + 2 * ne], refs[8 + 2 * ne:])
            pl.when(pl.program_id(0) == 0)(ex_start)
        g = r_ref[0].astype(F32)
        for d in range(1, N_CHIP):
            g = g + r_ref[d].astype(F32)
        g = g.reshape(2 * pairs, ROW_TILES, 128)
        g_ref[...] = g
        delta, nm, nv = _adamw_math(w_ref[...], g, m_ref[...], v_ref[...])
        d_ref[...] = delta
        nm_ref[...] = nm
        nv_ref[...] = nv
        if ne:
            pl.when(pl.program_id(0) == nsteps - 1)(ex_finish)

    blk = pl.BlockSpec((2 * pairs, ROW_TILES, 128), lambda i: (i, 0, 0))
    return pl.pallas_call(
        body, grid=(nsteps,),
        in_specs=[pl.BlockSpec((N_CHIP, pairs, 2 * ROW_TILES, 128), lambda i: (0, i, 0, 0)), blk, blk, blk]
        + [ANY] * ne,
        out_specs=[blk, blk, blk, blk] + [ANY] * ne,
        out_shape=[SDS(w3.shape, F32)] * 4 + [SDS(p.shape, p.dtype) for p in exchange],
        scratch_shapes=_direct_scratch(exchange) if ne else [],
        compiler_params=_cparams(), name=name)(recv, w3, m3, v3, *exchange)


ROW_GPRE, ROW_CONVB, ROW_DTB, ROW_ALOG, ROW_DSKIP, ROW_SINK, ROW_GSSM, ROW_GPOST = 0, 1, 4, 5, 6, 7, 8, 10
REP_ROWS, ROW_CONVW, ROW_META, SM_ROWS = 16, 16, 24, 40
CW_SHARD = CONV_DIM // N_DEV
META_SHARD = D_MODEL // N_DEV


def _small_pack(dgpre, dbx, dbb, dbc, ddtb, dal, ddsk, dsink, dgn, dgp, dwx, dwb, dwc, dh):
    def body(dgpre_ref, dbx_ref, dbb_ref, dbc_ref, ddtb_ref, dal_ref, ddsk_ref, dsink_ref, dgn_ref, dgp_ref,
             dwx_ref, dwb_ref, dwc_ref, dh_ref, o_ref, rep):
        rep[...] = jnp.zeros_like(rep)
        rep[ROW_GPRE:ROW_GPRE + 1, :] = dgpre_ref[0:1, :]
        rep[ROW_CONVB:ROW_CONVB + 1, :] = dbx_ref[0:1, 0:1024]
        rep[ROW_CONVB + 1:ROW_CONVB + 2, :] = dbx_ref[0:1, 1024:2048]
        rep[ROW_CONVB + 2:ROW_CONVB + 3, 0:512] = dbb_ref[0:1, :]
        rep[ROW_CONVB + 2:ROW_CONVB + 3, 512:1024] = dbc_ref[0:1, :]
        rep[ROW_DTB:ROW_DTB + 1, 0:128] = ddtb_ref[0:1, :]
        rep[ROW_ALOG:ROW_ALOG + 1, 0:128] = dal_ref[0:1, :]
        rep[ROW_DSKIP:ROW_DSKIP + 1, 0:128] = ddsk_ref[0:1, :]
        rep[ROW_SINK:ROW_SINK + 1, 0:128] = dsink_ref[0:1, :]
        rep[ROW_GSSM:ROW_GSSM + 1, :] = dgn_ref[0:1, 0:1024]
        rep[ROW_GSSM + 1:ROW_GSSM + 2, :] = dgn_ref[0:1, 1024:2048]
        rep[ROW_GPOST:ROW_GPOST + 1, :] = dgp_ref[0:1, :]
        cw = jnp.concatenate([dwx_ref[...], dwb_ref[...], dwc_ref[...]], axis=1)
        mh = dh_ref[...]
        o_ref[...] = jnp.zeros_like(o_ref)
        for p in range(N_DEV):
            o_ref[p, 0:REP_ROWS, :] = rep[...]
            o_ref[p, ROW_CONVW:ROW_CONVW + 8, 0:CW_SHARD] = cw[:, p * CW_SHARD:(p + 1) * CW_SHARD]
            o_ref[p, ROW_META:ROW_META + N_META, 0:META_SHARD] = mh[:, p * META_SHARD:(p + 1) * META_SHARD]

    ins = [dgpre, dbx, dbb, dbc, ddtb, dal, ddsk, dsink, dgn, dgp, dwx, dwb, dwc]
    return pl.pallas_call(
        body, grid=(1,),
        in_specs=[pl.BlockSpec(a.shape, lambda i: (0, 0)) for a in ins]
        + [pl.BlockSpec((N_META, D_MODEL), lambda i: (PAD // N_META, 0))],
        out_specs=pl.BlockSpec((N_DEV, SM_ROWS, 1024), lambda i: (0, 0, 0)),
        out_shape=SDS((N_DEV, SM_ROWS, 1024), F32), scratch_shapes=[pltpu.VMEM((REP_ROWS, 1024), F32)],
        name="small_pack")(*ins, dh)


def _small_finish(recv, params):
    npar = len(params)

    def body(*refs):
        r_ref = refs[0]
        wmv = refs[1:1 + 3 * npar]
        outs = refs[1 + 3 * npar:1 + 7 * npar]
        gs = refs[-1]
        g = r_ref[0]
        for d in range(1, recv.shape[0]):
            g = g + r_ref[d]
        gs[...] = g
        grads = [
            gs[ROW_GPRE:ROW_GPRE + 1, :],
            jnp.concatenate([gs[ROW_CONVB + k:ROW_CONVB + k + 1, :] for k in range(3)], axis=1),
            gs[ROW_DTB:ROW_DTB + 1, 0:SSM_HEADS], gs[ROW_ALOG:ROW_ALOG + 1, 0:SSM_HEADS],
            gs[ROW_DSKIP:ROW_DSKIP + 1, 0:SSM_HEADS], gs[ROW_SINK:ROW_SINK + 1, 0:Q_HEADS],
            jnp.concatenate([gs[ROW_GSSM:ROW_GSSM + 1, :], gs[ROW_GSSM + 1:ROW_GSSM + 2, :]], axis=1),
            gs[ROW_GPOST:ROW_GPOST + 1, :],
            gs[ROW_CONVW:ROW_CONVW + 4, 0:CW_SHARD],
            gs[ROW_META:ROW_META + N_META, 0:META_SHARD]]
        for i in range(npar):
            w_ref, m_ref, v_ref = wmv[3 * i:3 * i + 3]
            delta, nm, nv = _adamw_math(w_ref[...], grads[i], m_ref[...], v_ref[...])
            outs[4 * i][...] = grads[i]
            outs[4 * i + 1][...] = delta
            outs[4 * i + 2][...] = nm
            outs[4 * i + 3][...] = nv

    flat = [a for wmv in params for a in wmv]
    res = pl.pallas_call(
        body, out_shape=[SDS(wmv[0].shape, F32) for wmv in params for _ in range(4)],
        scratch_shapes=[pltpu.VMEM((SM_ROWS, 1024), F32)], name="small_finish")(recv, *flat)
    return [tuple(res[4 * i:4 * i + 4]) for i in range(npar)]


def _slab(ref, px, py, pc):
    return ref.at[4 * px + 2 * py + pc]


def _bounce(src, dst, buf, sem):
    cp = pltpu.make_async_copy(src, buf, sem)
    cp.start()
    cp.wait()
    cp = pltpu.make_async_copy(buf, dst, sem)
    cp.start()
    cp.wait()


def _ag_program(ins, outs, scratch):
    na = len(ins)
    send_sems, recv_sems, local_sems = scratch[:3]
    bufs = scratch[3:]
    x, y, c = lax.axis_index("x"), lax.axis_index("y"), lax.axis_index("c")
    me, sibling = (x, y, c), (x, y, 1 - c)
    chips = [(1 - x, y), (x, 1 - y), (1 - x, 1 - y)]

    def copy(a, k, block, to, src=None):
        dst = _slab(outs[a], *block)
        return pltpu.make_async_remote_copy(
            src_ref=dst if src is None else src, dst_ref=dst, send_sem=send_sems.at[a, k],
            recv_sem=recv_sems.at[a, k], device_id=to, device_id_type=MESH)

    def own_sends():
        out = []
        for a in range(na):
            out.append(copy(a, 0, me, sibling, src=ins[a]))
            out += [copy(a, 1 + j, me, (*chip, c), src=ins[a]) for j, chip in enumerate(chips)]
        return out

    def start():
        for cp in own_sends():
            cp.start()
        for a in range(na):
            _bounce(ins[a], _slab(outs[a], *me), bufs[a], local_sems.at[a])

    def forward():
        for j, chip in enumerate(chips):
            for a in range(na):
                copy(a, 1 + j, (*chip, c), me).wait_recv()
                copy(a, 4 + j, (*chip, c), sibling).start()

    def finish():
        for a in range(na):
            copy(a, 0, sibling, me).wait_recv()
            for j, chip in enumerate(chips):
                copy(a, 4 + j, (*chip, 1 - c), me).wait_recv()
        for cp in own_sends():
            cp.wait_send()
        for j, chip in enumerate(chips):
            for a in range(na):
                copy(a, 4 + j, (*chip, c), sibling).wait_send()

    return start, forward, finish


def _ag_scratch(shards):
    na = len(shards)
    return [pltpu.SemaphoreType.DMA((na, 7)), pltpu.SemaphoreType.DMA((na, 7)),
            pltpu.SemaphoreType.DMA((na,))] + [pltpu.VMEM(s.shape, s.dtype) for s in shards]


def _all_gather(shards):
    na = len(shards)

    def body(*refs):
        start, forward, finish = _ag_program(refs[:na], refs[na:2 * na], refs[2 * na:])
        start()
        forward()
        finish()

    return pl.pallas_call(
        body, in_specs=[ANY] * na, out_specs=[ANY] * na,
        out_shape=[SDS((N_DEV,) + s.shape, s.dtype) for s in shards],
        scratch_shapes=_ag_scratch(shards), name="all_gather")(*shards)


N_CHIP = 4


def _exchange_pair(parts, name):
    na = len(parts)

    def body(*refs):
        ins, own, got = refs[:na], refs[na:2 * na], refs[2 * na:3 * na]
        send_sems, recv_sems, local_sems = refs[3 * na:3 * na + 3]
        bufs = refs[3 * na + 3:]
        x, y, c = lax.axis_index("x"), lax.axis_index("y"), lax.axis_index("c")
        sibling = (x, y, 1 - c)
        sent = []
        for a in range(na):
            for k in range(N_CHIP):
                cp = pltpu.make_async_remote_copy(
                    src_ref=ins[a].at[2 * k + 1 - c], dst_ref=got[a].at[k], send_sem=send_sems.at[a, k],
                    recv_sem=recv_sems.at[a, k], device_id=sibling, device_id_type=MESH)
                cp.start()
                sent.append(cp)
        for a in range(na):
            for k in range(N_CHIP):
                _bounce(ins[a].at[2 * k + c], own[a].at[k], bufs[a], local_sems.at[a])
        for cp in sent:
            cp.wait()

    half = [SDS((N_CHIP,) + p.shape[1:], p.dtype) for p in parts]
    res = pl.pallas_call(
        body, in_specs=[ANY] * na, out_specs=[ANY] * (2 * na), out_shape=half + half,
        scratch_shapes=[pltpu.SemaphoreType.DMA((na, N_CHIP)), pltpu.SemaphoreType.DMA((na, N_CHIP)),
                        pltpu.SemaphoreType.DMA((na,))] + [pltpu.VMEM(p.shape[1:], p.dtype) for p in parts],
        name=name)(*parts)
    return res[:na], res[na:]


def _pair_sum(own, got, name):
    na = len(own)

    def body(*refs):
        for a in range(na):
            o_ref, g_ref, s_ref = refs[a], refs[na + a], refs[2 * na + a]
            s_ref[...] = (o_ref[...].astype(F32) + g_ref[...].astype(F32)).astype(s_ref.dtype)

    def spec(p):
        nd = len(p.shape) - 1
        return pl.BlockSpec((1,) + p.shape[1:], lambda k, nd=nd: (k,) + (0,) * nd)

    return pl.pallas_call(
        body, grid=(N_CHIP,), in_specs=[spec(p) for p in own] + [spec(p) for p in got],
        out_specs=[spec(p) for p in own], out_shape=[SDS(p.shape, p.dtype) for p in own],
        compiler_params=_cparams(), name=name)(*own, *got)


def _chips_program(ins, outs, scratch):
    na = len(ins)
    send_sems, recv_sems, local_sems = scratch[:3]
    bufs = scratch[3:]
    x, y, c = lax.axis_index("x"), lax.axis_index("y"), lax.axis_index("c")
    mine = 2 * x + y
    chips = [(1 - x, y), (x, 1 - y), (1 - x, 1 - y)]

    def send(a, j):
        px, py = chips[j]
        return pltpu.make_async_remote_copy(
            src_ref=ins[a].at[2 * px + py], dst_ref=outs[a].at[mine], send_sem=send_sems.at[a, j],
            recv_sem=recv_sems.at[a, j], device_id=(px, py, c), device_id_type=MESH)

    def arrival(a, j):
        px, py = chips[j]
        return pltpu.make_async_remote_copy(
            src_ref=ins[a].at[2 * px + py], dst_ref=outs[a].at[2 * px + py], send_sem=send_sems.at[a, j],
            recv_sem=recv_sems.at[a, j], device_id=(px, py, c), device_id_type=MESH)

    def start():
        for a in range(na):
            for j in range(3):
                send(a, j).start()
        for a in range(na):
            _bounce(ins[a].at[mine], outs[a].at[mine], bufs[a], local_sems.at[a])

    def finish():
        for a in range(na):
            for j in range(3):
                arrival(a, j).wait_recv()
        for a in range(na):
            for j in range(3):
                send(a, j).wait_send()

    return start, finish


def _chips_scratch(parts):
    na = len(parts)
    return [pltpu.SemaphoreType.DMA((na, 3)), pltpu.SemaphoreType.DMA((na, 3)),
            pltpu.SemaphoreType.DMA((na,))] + [pltpu.VMEM(p.shape[1:], p.dtype) for p in parts]


def _direct_program(ins, outs, scratch):
    na = len(ins)
    send_sems, recv_sems, local_sems = scratch[:3]
    bufs = scratch[3:]
    x, y, c = lax.axis_index("x"), lax.axis_index("y"), lax.axis_index("c")
    me = (x, y, c)
    peers = []
    for k in range(1, N_DEV):
        dx, dy, dc = (k >> 2) & 1, (k >> 1) & 1, k & 1
        peers.append(((1 - x) if dx else x, (1 - y) if dy else y, (1 - c) if dc else c))

    def send(a, k):
        return pltpu.make_async_remote_copy(
            src_ref=_slab(ins[a], *peers[k]), dst_ref=_slab(outs[a], *me), send_sem=send_sems.at[a, k],
            recv_sem=recv_sems.at[a, k], device_id=peers[k], device_id_type=MESH)

    def arrival(a, k):
        return pltpu.make_async_remote_copy(
            src_ref=_slab(ins[a], *peers[k]), dst_ref=_slab(outs[a], *peers[k]), send_sem=send_sems.at[a, k],
            recv_sem=recv_sems.at[a, k], device_id=peers[k], device_id_type=MESH)

    def start():
        for a in range(na):
            for k in range(N_DEV - 1):
                send(a, k).start()
        for a in range(na):
            _bounce(_slab(ins[a], *me), _slab(outs[a], *me), bufs[a], local_sems.at[a])

    def finish():
        for a in range(na):
            for k in range(N_DEV - 1):
                arrival(a, k).wait_recv()
        for a in range(na):
            for k in range(N_DEV - 1):
                send(a, k).wait_send()

    return start, finish


def _direct_scratch(parts):
    na = len(parts)
    return [pltpu.SemaphoreType.DMA((na, N_DEV - 1)), pltpu.SemaphoreType.DMA((na, N_DEV - 1)),
            pltpu.SemaphoreType.DMA((na,))] + [pltpu.VMEM(p.shape[1:], p.dtype) for p in parts]


ROW_TILES = D_MODEL // 128


def _rows3(t):
    return jnp.transpose(t[0]).reshape(t.shape[2], ROW_TILES, 128)


def _unrows3(t):
    return jnp.transpose(t.reshape(t.shape[0], D_MODEL))[None]


def _cast_shards(w_in3, w_att, w_ssm, w_o):
    def body(wi_ref, wa_ref, ws_ref, wo_ref, a_ref, b_ref, c_ref, d_ref):
        a_ref[...] = wi_ref[...].reshape(SHARD_IN // 2, 2 * ROW_TILES, 128).astype(BF16)
        b_ref[...] = wa_ref[...].astype(BF16)
        c_ref[...] = ws_ref[...].astype(BF16)
        d_ref[...] = wo_ref[...].astype(BF16)

    return pl.pallas_call(
        body, out_shape=[SDS((SHARD_IN // 2, 2 * ROW_TILES, 128), BF16), SDS(w_att.shape, BF16),
                         SDS(w_ssm.shape, BF16), SDS(w_o.shape, BF16)],
        compiler_params=_cparams(), name="cast_shards")(w_in3, w_att, w_ssm, w_o)


def _pieces():
    out = []
    for r0, c0, w in _SEGS:
        r = r0
        while r < r0 + w:
            d = r // SHARD_IN
            n = min(r0 + w, (d + 1) * SHARD_IN) - r
            out.append((c0 + (r - r0), d, r - d * SHARD_IN, n))
            r += n
    return out


def _to_aligned_t(slabs):
    def body(a_ref, o_ref):
        for (t, d, s, n) in _pieces():
            o_ref[t:t + n, :] = a_ref[d, s // 2:(s + n) // 2].reshape(n, D_MODEL)
        o_ref[C_DT + 32:C_DT + 128, :] = jnp.zeros((96, D_MODEL), slabs.dtype)

    return pl.pallas_call(body, out_shape=SDS((PW, D_MODEL), slabs.dtype), compiler_params=_cparams(),
                          name="to_aligned")(slabs)


def _from_aligned_t(g):
    def body(g_ref, o_ref):
        for (t, d, s, n) in _pieces():
            o_ref[d, s // 2:(s + n) // 2] = g_ref[t:t + n, :].reshape(n // 2, 2 * ROW_TILES, 128)

    return pl.pallas_call(body, out_shape=SDS((N_DEV, SHARD_IN // 2, 2 * ROW_TILES, 128), g.dtype),
                          compiler_params=_cparams(), name="from_aligned")(g)


_SEGS = [
    (R_Q, C_Q, 1024), (R_K, C_K, 256), (R_V, C_V, 256), (R_ZA, C_ZA, 1024), (R_ZS, C_ZS, 2048),
    (R_XBC, C_XBC, 3072), (R_DT, C_DT, 32), (R_GA, C_GA, 1024), (R_GS, C_GS, 1024)]


def _pad_lanes(v, n=128):
    return jnp.pad(v, ((0, 0), (0, n - v.shape[1])))


def _reduce_pair(parts, tag):
    own, got = _exchange_pair(parts, "exchange_pair_" + tag)
    return _pair_sum(own, got, "pair_sum_" + tag)


def _device_step(h, tgt, w_alt, w_out, g_pre, conv_w8, conv_b, dt_bias, a_log, d_skip, sinks, g_ssm, g_post, on_mesh):
    dtb, al, dsk, snk = _pad_lanes(dt_bias), _pad_lanes(a_log), _pad_lanes(d_skip), _pad_lanes(sinks)
    u = _norm_u(h, g_pre)
    proj = _matmul(u, w_alt, "nt", F32, T, 896, "in_proj")
    o = _attn_fwd(proj, snk)
    xbc_act = _conv_fwd(proj, conv_w8, conv_b)
    if on_mesh:
        sn, states, att_all, ssm_all, o_all = _ssd_fwd(xbc_act, proj, dtb, al, dsk, g_ssm, gather=w_out)
        w_att = att_all.reshape(D_MODEL, D_MODEL)
        w_ssm = ssm_all.reshape(SSM_INNER, D_MODEL)
        w_o = o_all.reshape(D_MODEL, D_MODEL)
    else:
        sn, states = _ssd_fwd(xbc_act, proj, dtb, al, dsk, g_ssm)
        w_att, w_ssm, w_o = w_out
    a_in, mg, ya, ys, out = _post_a(o, proj, sn, w_att, w_ssm, w_o)
    (loss, dres, dout, dya, dys, dga, dgs, do, dza, dsn, dgp) = _post_b(
        out, h, tgt, proj, ya, ys, o, g_post, w_att, w_ssm, w_o)
    dw_att = _matmul(a_in, dya, "tn", BF16, D_MODEL, D_MODEL, "d_w_att")
    dw_ssm = _matmul(sn, dys, "tn", BF16, D_MODEL, D_MODEL, "d_w_ssm")
    dw_o = _matmul(mg, dout, "tn", BF16, D_MODEL, D_MODEL, "d_w_o")
    res = {}
    if on_mesh:
        parts = [dw_att.reshape(N_DEV, 128, D_MODEL), dw_ssm.reshape(N_DEV, 256, D_MODEL),
                 dw_o.reshape(N_DEV, 128, D_MODEL)]
        (dxs, dbm, dcm, ddt4, dzs, ddtb, dal, ddsk, dgn, res["r_att"], res["r_ssm"], res["r_o"]) = _ssd_bwd(
            xbc_act, proj, dtb, al, dsk, g_ssm, states, dsn, exchange=parts)
    else:
        dxs, dbm, dcm, ddt4, dzs, ddtb, dal, ddsk, dgn = _ssd_bwd(xbc_act, proj, dtb, al, dsk, g_ssm, states, dsn)
        res.update(dw_att=dw_att, dw_ssm=dw_ssm, dw_o=dw_o)
    dxx, dwx, dbx = _conv_bwd(proj, conv_w8, conv_b, dxs, 0, "conv_bwd_x")
    dxb, dwb, dbb = _conv_bwd(proj, conv_w8, conv_b, dbm, SSM_INNER, "conv_bwd_b")
    dxc, dwc, dbc = _conv_bwd(proj, conv_w8, conv_b, dcm, SSM_INNER + GRP_W, "conv_bwd_c")
    dq, dk, dv, dsink = _attn_bwd(proj, snk, do)
    dproj = _assemble(dq, dza, dga, dgs, dzs, dxx, dxb, dxc, dk, dv, ddt4)
    dw_alt = _matmul(dproj, u, "tn", BF16, 896, D_MODEL, "d_w_in")
    if on_mesh:
        sums = _reduce_pair([_from_aligned_t(dw_alt)], "b")
        dh, dgpre, res["r_in"] = _d_u_norm(dproj, w_alt, h, g_pre, dres, chips=sums)
    else:
        dh, dgpre = _d_u_norm(dproj, w_alt, h, g_pre, dres)
        res["dw_alt"] = dw_alt
    small = (dgpre, dbx, dbb, dbc, ddtb, dal, ddsk, dsink, dgn, dgp, dwx, dwb, dwc)
    if on_mesh:
        res["small_pack"] = _small_pack(*small, dh)
    else:
        res["small"] = small
    res.update(loss=loss[0, 0], dh=dh)
    return res


def kernel(x, meta_tokens, g_pre, w_in, conv_w, conv_b, dt_bias, a_log, d_skip, attn_sinks, g_ssm_norm, w_out_att, w_out_ssm, w_out, g_post, loss_target, m_meta_tokens, m_g_pre, m_w_in, m_conv_w, m_conv_b, m_dt_bias, m_a_log, m_d_skip, m_attn_sinks, m_g_ssm_norm, m_w_out_att, m_w_out_ssm, m_w_out, m_g_post, v_meta_tokens, v_g_pre, v_w_in, v_conv_w, v_conv_b, v_dt_bias, v_a_log, v_d_skip, v_attn_sinks, v_g_ssm_norm, v_w_out_att, v_w_out_ssm, v_w_out, v_g_post):
    w_in3, m_in3, v_in3 = _rows3(w_in), _rows3(m_w_in), _rows3(v_w_in)
    a_sh, att_sh, ssm_sh, o_sh = _cast_shards(w_in3, w_out_att[0], w_out_ssm[0], w_out[0])
    cw_sh = jnp.pad(conv_w[0], ((0, 4), (0, 0)))
    a_all, meta_all, cw_all = _all_gather([a_sh, meta_tokens, cw_sh])
    w_alt = _to_aligned_t(a_all)
    meta_full = meta_all.transpose(1, 0, 2).reshape(N_META, D_MODEL)
    conv_w8 = cw_all.transpose(1, 0, 2).reshape(8, CONV_DIM)

    h = jnp.concatenate([jnp.zeros((PAD, D_MODEL), F32), meta_full, x[0]], axis=0)
    tgt = jnp.concatenate([jnp.zeros((PAD + N_META, D_MODEL), F32), loss_target[0]], axis=0)
    r = _device_step(h, tgt, w_alt, (att_sh, ssm_sh, o_sh), g_pre, conv_w8, conv_b, dt_bias, a_log, d_skip,
                     attn_sinks, g_ssm_norm, g_post, True)
    loss = lax.psum(r["loss"], ("x", "y", "c"))
    grad_x = r["dh"][PAD + N_META:][None]

    *res_in, r_small = _sum_adamw_rows3(r["r_in"], w_in3, m_in3, v_in3, "adamw_w_in", exchange=[r["small_pack"]])
    res_in = [_unrows3(t) for t in res_in]
    res_att = [t[None] for t in _sum_adamw(r["r_att"], w_out_att[0], m_w_out_att[0], v_w_out_att[0], 512,
                                           "adamw_w_att")]
    res_ssm = [t[None] for t in _sum_adamw(r["r_ssm"], w_out_ssm[0], m_w_out_ssm[0], v_w_out_ssm[0], 512,
                                           "adamw_w_ssm")]
    res_o = [t[None] for t in _sum_adamw(r["r_o"], w_out[0], m_w_out[0], v_w_out[0], 512, "adamw_w_o")]
    (res_gpre, res_convb, res_dtb, res_alog, res_dskip, res_sink, res_gssm, res_gpost, res_cw, res_meta) = _small_finish(
        r_small, [(g_pre, m_g_pre, v_g_pre), (conv_b, m_conv_b, v_conv_b), (dt_bias, m_dt_bias, v_dt_bias),
                       (a_log, m_a_log, v_a_log), (d_skip, m_d_skip, v_d_skip),
                       (attn_sinks, m_attn_sinks, v_attn_sinks), (g_ssm_norm, m_g_ssm_norm, v_g_ssm_norm),
                       (g_post, m_g_post, v_g_post), (conv_w[0], m_conv_w[0], v_conv_w[0]),
                       (meta_tokens, m_meta_tokens, v_meta_tokens)])
    res_cw = [t[None] for t in res_cw]
    per_weight = [res_meta, res_gpre, res_in, res_cw, res_convb, res_dtb, res_alog, res_dskip, res_sink, res_gssm,
                  res_att, res_ssm, res_o, res_gpost]
    return (loss, grad_x, *[p[0] for p in per_weight], *[p[1] for p in per_weight], *[p[2] for p in per_weight],
            *[p[3] for p in per_weight])
```

```python
import functools
import math

import jax
import jax.numpy as jnp
from jax import lax
from jax.experimental import pallas as pl
from jax.experimental.pallas import tpu as pltpu

F32 = jnp.float32
BF16 = jnp.bfloat16
SDS = jax.ShapeDtypeStruct
MESH = pl.DeviceIdType.MESH
ANY = pl.BlockSpec(memory_space=pl.ANY)

N_DEV = 8
D_MODEL = 1024
SEQ = 2048
N_META = 16
BLK = 128
PAD = 112
T = PAD + N_META + SEQ
NB = T // BLK
EPS = 1e-6
HEAD = 64
Q_HEADS = 16
KV_HEADS = 4
GROUP = 4
KV_W = 256
SSM_INNER = 2048
SSM_HEADS = 32
SSM_GROUPS = 4
GRP_W = 512
SSM_STATE = 128
CONV_DIM = 3072
IN_PROJ = 9760
SHARD_IN = IN_PROJ // N_DEV
NEG = -1e30

C_Q, C_ZA, C_GA, C_GS, C_ZS, C_XBC, C_K, C_V, C_DT = 0, 1024, 2048, 3072, 4096, 6144, 9216, 9472, 9728
PW = 9856
R_Q, R_K, R_V, R_ZA, R_ZS, R_XBC, R_DT, R_GA, R_GS = 0, 1024, 1280, 1536, 2560, 4608, 7680, 7712, 8736

ADAM_LR, ADAM_B1, ADAM_B2, ADAM_EPS, ADAM_WD, ADAM_STEP = 0.001, 0.9, 0.999, 1e-08, 0.01, 10

VMEM_LIMIT = 56 * 1024 * 1024


def _cparams():
    return pltpu.CompilerParams(vmem_limit_bytes=VMEM_LIMIT)


def _silu(x):
    return x * jax.nn.sigmoid(x)


def _dsilu(x):
    s = jax.nn.sigmoid(x)
    return s * (1.0 + x * (1.0 - s))


def _matmul(a, b, mode, out_dtype, tm, tn, name):
    if mode == "nt":
        (m, k), n = a.shape, b.shape[0]
        a_spec = pl.BlockSpec((tm, k), lambda i, j: (i, 0))
        b_spec = pl.BlockSpec((tn, k), lambda i, j: (j, 0))
        dims = (((1,), (1,)), ((), ()))
    else:
        assert mode == "tn"
        (k, m), n = a.shape, b.shape[1]
        a_spec = pl.BlockSpec((k, tm), lambda i, j: (0, i))
        b_spec = pl.BlockSpec((k, tn), lambda i, j: (0, j))
        dims = (((0,), (0,)), ((), ()))
    assert m % tm == 0 and n % tn == 0, (a.shape, b.shape, tm, tn)

    def body(a_ref, b_ref, o_ref):
        o_ref[...] = lax.dot_general(a_ref[...], b_ref[...], dims, preferred_element_type=F32).astype(out_dtype)

    return pl.pallas_call(
        body, grid=(m // tm, n // tn), in_specs=[a_spec, b_spec],
        out_specs=pl.BlockSpec((tm, tn), lambda i, j: (i, j)), out_shape=SDS((m, n), out_dtype),
        compiler_params=_cparams(), name=name)(a, b)


def _norm_u(h, g_pre):
    def body(h_ref, g_ref, u_ref):
        x = h_ref[...]
        r = lax.rsqrt(jnp.mean(x * x, axis=-1, keepdims=True) + EPS)
        u_ref[...] = (x * r * g_ref[...]).astype(BF16)

    return pl.pallas_call(
        body, grid=(NB,),
        in_specs=[pl.BlockSpec((BLK, D_MODEL), lambda i: (i, 0)), pl.BlockSpec((1, D_MODEL), lambda i: (0, 0))],
        out_specs=pl.BlockSpec((BLK, D_MODEL), lambda i: (i, 0)),
        out_shape=SDS((T, D_MODEL), BF16), name="norm_u")(h, g_pre)


DU_TM, DU_TK = T // 2, 1408


def _d_u_norm(dproj, w_alt, h, g_pre, dres, chips=()):
    nk = PW // DU_TK
    ni = T // DU_TM
    nc = len(chips)

    def body(*refs):
        a_ref, b_ref, h_ref, g_ref, dres_ref = refs[:5]
        dh_ref, dg_ref = refs[5 + nc:7 + nc]
        acc_ref = refs[7 + 2 * nc]
        i, kk = pl.program_id(0), pl.program_id(1)
        if nc:
            ch_start, ch_finish = _chips_program(refs[5:5 + nc], refs[7 + nc:7 + 2 * nc], refs[8 + 2 * nc:])
            pl.when((i == 0) & (kk == 0))(ch_start)
        part = jnp.dot(a_ref[...], b_ref[...], preferred_element_type=F32)

        @pl.when(kk == 0)
        def _():
            acc_ref[...] = part

        @pl.when((kk > 0) & (kk < nk - 1))
        def _():
            acc_ref[...] += part

        @pl.when(kk == nk - 1)
        def _():
            du_ = acc_ref[...] + part
            x = h_ref[...]
            r = lax.rsqrt(jnp.mean(x * x, axis=-1, keepdims=True) + EPS)
            gd = g_ref[...] * du_
            dx = r * gd - x * (r * r * r) * jnp.mean(x * gd, axis=-1, keepdims=True)
            dh_ref[...] = dx + dres_ref[...]
            gpart = jnp.concatenate([jnp.sum(du_ * x * r, axis=0, keepdims=True), jnp.zeros((7, D_MODEL), F32)],
                                    axis=0)

            @pl.when(i == 0)
            def _():
                dg_ref[...] = gpart

            @pl.when(i > 0)
            def _():
                dg_ref[...] += gpart

        if nc:
            pl.when((i == ni - 1) & (kk == nk - 1))(ch_finish)

    row = pl.BlockSpec((DU_TM, D_MODEL), lambda i, kk: (i, 0))
    return pl.pallas_call(
        body, grid=(ni, nk),
        in_specs=[pl.BlockSpec((DU_TM, DU_TK), lambda i, kk: (i, kk)),
                  pl.BlockSpec((DU_TK, D_MODEL), lambda i, kk: (kk, 0)),
                  row, pl.BlockSpec((1, D_MODEL), lambda i, kk: (0, 0)), row] + [ANY] * nc,
        out_specs=[row, pl.BlockSpec((8, D_MODEL), lambda i, kk: (0, 0))] + [ANY] * nc,
        out_shape=[SDS((T, D_MODEL), F32), SDS((8, D_MODEL), F32)] + [SDS(p.shape, p.dtype) for p in chips],
        scratch_shapes=[pltpu.VMEM((DU_TM, D_MODEL), F32)] + (_chips_scratch(chips) if nc else []),
        compiler_params=_cparams(), name="d_u_norm")(dproj, w_alt, h, g_pre, dres, *chips)


def _lane_pick(row, h):
    lane = lax.broadcasted_iota(jnp.int32, row.shape, 1)
    return jnp.sum(jnp.where(lane == h, row, 0.0), axis=1, keepdims=True)


def _attn_fn(q4s, kcats, vcats, kms, vms, sinks, n):
    r = lax.broadcasted_iota(jnp.int32, (GROUP * BLK, 2 * BLK), 0)
    s = lax.broadcasted_iota(jnp.int32, (GROUP * BLK, 2 * BLK), 1)
    i = jnp.bitwise_and(r, BLK - 1)
    gi = jnp.right_shift(r, 7)
    rel = i - s + BLK
    k_pos = n * BLK - BLK + s
    band_ok = (rel >= 0) & (rel < BLK) & (k_pos >= PAD + N_META)
    relf = rel.astype(F32)
    rm = lax.broadcasted_iota(jnp.int32, (GROUP * BLK, N_META), 0)
    mm = lax.broadcasted_iota(jnp.int32, (GROUP * BLK, N_META), 1)
    meta_ok = (PAD + mm) <= (n * BLK + jnp.bitwise_and(rm, BLK - 1))
    gcol = jnp.right_shift(lax.broadcasted_iota(jnp.int32, (GROUP * BLK, 1), 0), 7)
    outs = []
    for kh in range(KV_HEADS):
        slopes = [2.0 ** (-8.0 * (kh * GROUP + g + 1) / Q_HEADS) for g in range(GROUP)]
        slope = jnp.where(gi == 0, slopes[0], jnp.where(gi == 1, slopes[1], jnp.where(gi == 2, slopes[2], slopes[3])))
        sk = [_lane_pick(sinks, kh * GROUP + g) for g in range(GROUP)]
        sink = jnp.where(gcol == 0, sk[0], jnp.where(gcol == 1, sk[1], jnp.where(gcol == 2, sk[2], sk[3])))
        qb = (q4s[kh] * (HEAD ** -0.5)).astype(BF16)
        sb = lax.dot_general(qb, kcats[kh].astype(BF16), (((1,), (1,)), ((), ())), preferred_element_type=F32)
        sb = jnp.where(band_ok, sb - slope * relf, NEG)
        sm = lax.dot_general(qb, kms[kh].astype(BF16), (((1,), (1,)), ((), ())), preferred_element_type=F32)
        sm = jnp.where(meta_ok, sm, NEG)
        mx = jnp.maximum(jnp.maximum(jnp.max(sb, axis=1, keepdims=True), jnp.max(sm, axis=1, keepdims=True)), sink)
        mx = lax.stop_gradient(mx)
        eb = jnp.exp(sb - mx)
        em = jnp.exp(sm - mx)
        es = jnp.exp(sink - mx)
        inv = 1.0 / (jnp.sum(eb, axis=1, keepdims=True) + jnp.sum(em, axis=1, keepdims=True) + es)
        pb = (eb * inv).astype(BF16)
        pm = (em * inv).astype(BF16)
        o4 = (jnp.dot(pm, vms[kh].astype(BF16), preferred_element_type=F32)
              + jnp.dot(pb, vcats[kh].astype(BF16), preferred_element_type=F32))
        outs.append(o4)
    return outs


def _attn_specs():
    prev = lambda n: jnp.maximum(n - 1, 0)
    return [
        pl.BlockSpec((BLK, D_MODEL), lambda n: (n, C_Q // D_MODEL)),
        pl.BlockSpec((BLK, KV_W), lambda n: (prev(n), C_K // KV_W)),
        pl.BlockSpec((BLK, KV_W), lambda n: (n, C_K // KV_W)),
        pl.BlockSpec((BLK, KV_W), lambda n: (prev(n), C_V // KV_W)),
        pl.BlockSpec((BLK, KV_W), lambda n: (n, C_V // KV_W)),
        pl.BlockSpec((N_META, KV_W), lambda n: (PAD // N_META, C_K // KV_W)),
        pl.BlockSpec((N_META, KV_W), lambda n: (PAD // N_META, C_V // KV_W)),
        pl.BlockSpec((1, 128), lambda n: (0, 0)),
    ]


def _attn_load(q_ref, kp_ref, kc_ref, vp_ref, vc_ref, km_ref, vm_ref):
    q4s, kcats, vcats, kms, vms = [], [], [], [], []
    for kh in range(KV_HEADS):
        q4s.append(jnp.concatenate(
            [q_ref[:, (kh * GROUP + g) * HEAD:(kh * GROUP + g + 1) * HEAD] for g in range(GROUP)], axis=0))
        cs = slice(kh * HEAD, (kh + 1) * HEAD)
        kcats.append(jnp.concatenate([kp_ref[:, cs], kc_ref[:, cs]], axis=0))
        vcats.append(jnp.concatenate([vp_ref[:, cs], vc_ref[:, cs]], axis=0))
        kms.append(km_ref[:, cs])
        vms.append(vm_ref[:, cs])
    return q4s, kcats, vcats, kms, vms


def _attn_fwd(proj, sinks):
    def body(q_ref, kp_ref, kc_ref, vp_ref, vc_ref, km_ref, vm_ref, s_ref, o_ref):
        n = pl.program_id(0)
        args = _attn_load(q_ref, kp_ref, kc_ref, vp_ref, vc_ref, km_ref, vm_ref)
        outs = _attn_fn(*args, s_ref[...], n)
        for kh in range(KV_HEADS):
            for g in range(GROUP):
                hh = kh * GROUP + g
                o_ref[:, hh * HEAD:(hh + 1) * HEAD] = outs[kh][g * BLK:(g + 1) * BLK]

    return pl.pallas_call(
        body, grid=(NB,), in_specs=_attn_specs(),
        out_specs=pl.BlockSpec((BLK, D_MODEL), lambda n: (n, 0)),
        out_shape=SDS((T, D_MODEL), F32), name="attn_fwd")(proj, proj, proj, proj, proj, proj, proj, sinks)


def _attn_bwd(proj, sinks, do):
    def body(q_ref, kp_ref, kc_ref, vp_ref, vc_ref, km_ref, vm_ref, s_ref, do_ref, dq_ref, dk_ref, dv_ref, ds_ref):
        n = pl.program_id(0)

        @pl.when(n == 0)
        def _():
            dk_ref[...] = jnp.zeros_like(dk_ref)
            dv_ref[...] = jnp.zeros_like(dv_ref)
            ds_ref[...] = jnp.zeros_like(ds_ref)

        args = _attn_load(q_ref, kp_ref, kc_ref, vp_ref, vc_ref, km_ref, vm_ref)
        _, vjp = jax.vjp(lambda a, b, c, d, e, f: _attn_fn(a, b, c, d, e, f, n), *args, s_ref[...])
        do_f = do_ref[...].astype(F32)
        cot = [jnp.concatenate([do_f[:, (kh * GROUP + g) * HEAD:(kh * GROUP + g + 1) * HEAD] for g in range(GROUP)],
                               axis=0) for kh in range(KV_HEADS)]
        dq4s, dkcats, dvcats, dkms, dvms, dsk = vjp(cot)
        ds_ref[0:1, :] += dsk
        cur = pl.ds(pl.multiple_of(n * BLK, BLK), BLK)
        meta = slice(PAD, PAD + N_META)
        for kh in range(KV_HEADS):
            cs = slice(kh * HEAD, (kh + 1) * HEAD)
            for g in range(GROUP):
                hh = kh * GROUP + g
                dq_ref[:, hh * HEAD:(hh + 1) * HEAD] = dq4s[kh][g * BLK:(g + 1) * BLK]
            dk_ref[cur, cs] += dkcats[kh][BLK:]
            dv_ref[cur, cs] += dvcats[kh][BLK:]
            dk_ref[meta, cs] += dkms[kh]
            dv_ref[meta, cs] += dvms[kh]

        @pl.when(n > 0)
        def _():
            prv = pl.ds(pl.multiple_of((n - 1) * BLK, BLK), BLK)
            for kh in range(KV_HEADS):
                cs = slice(kh * HEAD, (kh + 1) * HEAD)
                dk_ref[prv, cs] += dkcats[kh][:BLK]
                dv_ref[prv, cs] += dvcats[kh][:BLK]

    full_kv = pl.BlockSpec((T, KV_W), lambda n: (0, 0))
    return pl.pallas_call(
        body, grid=(NB,),
        in_specs=_attn_specs() + [pl.BlockSpec((BLK, D_MODEL), lambda n: (n, 0))],
        out_specs=[pl.BlockSpec((BLK, D_MODEL), lambda n: (n, 0)), full_kv, full_kv,
                   pl.BlockSpec((8, 128), lambda n: (0, 0))],
        out_shape=[SDS((T, D_MODEL), F32), SDS((T, KV_W), F32), SDS((T, KV_W), F32), SDS((8, 128), F32)],
        name="attn_bwd")(proj, proj, proj, proj, proj, proj, proj, sinks, do)


def _conv_taps(xp, w, rows):
    return (w[0:1] * xp[5:5 + rows] + w[1:2] * xp[6:6 + rows] + w[2:3] * xp[7:7 + rows] + w[3:4] * xp[8:8 + rows])


def _conv_fwd(proj, conv_w, conv_b):
    CONV_CB = CONV_DIM
    ncb = CONV_DIM // CONV_CB
    cb0 = C_XBC // CONV_CB

    def body(tail_ref, cur_ref, w_ref, b_ref, o_ref):
        n = pl.program_id(1)
        tail = jnp.where(n > 0, tail_ref[...], 0.0)
        xp = jnp.concatenate([tail, cur_ref[...]], axis=0)
        conv = _conv_taps(xp, w_ref[...], BLK) + b_ref[...]
        row = n * BLK + lax.broadcasted_iota(jnp.int32, (BLK, 1), 0)
        o_ref[...] = jnp.where(row >= PAD, _silu(conv), 0.0)

    return pl.pallas_call(
        body, grid=(ncb, NB),
        in_specs=[pl.BlockSpec((8, CONV_CB), lambda j, n: (jnp.maximum(n * (BLK // 8) - 1, 0), cb0 + j)),
                  pl.BlockSpec((BLK, CONV_CB), lambda j, n: (n, cb0 + j)),
                  pl.BlockSpec((8, CONV_CB), lambda j, n: (0, j)),
                  pl.BlockSpec((1, CONV_CB), lambda j, n: (0, j))],
        out_specs=pl.BlockSpec((BLK, CONV_CB), lambda j, n: (n, j)),
        out_shape=SDS((T, CONV_DIM), F32), name="conv_fwd")(proj, proj, conv_w, conv_b)


def _conv_bwd(proj, conv_w, conv_b, dact, ch0, name):
    width = dact.shape[1]
    CONV_CB = width
    ncb = width // CONV_CB
    cb0 = (C_XBC + ch0) // CONV_CB
    wb0 = ch0 // CONV_CB
    last8 = T // 8 - 1

    def body(tail_ref, cur_ref, nxt_ref, w_ref, b_ref, dcur_ref, dnxt_ref, dx_ref, dw_ref, db_ref):
        n = pl.program_id(1)
        w = w_ref[...]
        tail = jnp.where(n > 0, tail_ref[...], 0.0)
        xp = jnp.concatenate([tail, cur_ref[...], nxt_ref[...]], axis=0)
        conv = _conv_taps(xp, w, BLK + 8) + b_ref[...]
        dext = jnp.concatenate([dcur_ref[...], jnp.where(n < NB - 1, dnxt_ref[...], 0.0)], axis=0)
        row = n * BLK + lax.broadcasted_iota(jnp.int32, (BLK + 8, 1), 0)
        dconv = jnp.where(row >= PAD, dext * _dsilu(conv), 0.0)
        dx = (w[0:1] * dconv[3:3 + BLK] + w[1:2] * dconv[2:2 + BLK] + w[2:3] * dconv[1:1 + BLK]
              + w[3:4] * dconv[0:BLK])
        dx_ref[...] = dx.astype(BF16)
        dc = dconv[0:BLK]
        dws = [jnp.sum(dc * xp[5 + k:5 + k + BLK], axis=0, keepdims=True) for k in range(4)]
        dwp = jnp.concatenate(dws + [jnp.zeros((4, CONV_CB), F32)], axis=0)
        dbp = jnp.sum(dc, axis=0, keepdims=True)

        @pl.when(n == 0)
        def _():
            dw_ref[...] = dwp
            db_ref[...] = jnp.concatenate([dbp, jnp.zeros((7, CONV_CB), F32)], axis=0)

        @pl.when(n > 0)
        def _():
            dw_ref[...] += dwp
            db_ref[0:1, :] += dbp

    return pl.pallas_call(
        body, grid=(ncb, NB),
        in_specs=[pl.BlockSpec((8, CONV_CB), lambda j, n: (jnp.maximum(n * (BLK // 8) - 1, 0), cb0 + j)),
                  pl.BlockSpec((BLK, CONV_CB), lambda j, n: (n, cb0 + j)),
                  pl.BlockSpec((8, CONV_CB), lambda j, n: (jnp.minimum((n + 1) * (BLK // 8), last8), cb0 + j)),
                  pl.BlockSpec((8, CONV_CB), lambda j, n: (0, wb0 + j)),
                  pl.BlockSpec((1, CONV_CB), lambda j, n: (0, wb0 + j)),
                  pl.BlockSpec((BLK, CONV_CB), lambda j, n: (n, j)),
                  pl.BlockSpec((8, CONV_CB), lambda j, n: (jnp.minimum((n + 1) * (BLK // 8), last8), j))],
        out_specs=[pl.BlockSpec((BLK, CONV_CB), lambda j, n: (n, j)),
                   pl.BlockSpec((8, CONV_CB), lambda j, n: (0, j)),
                   pl.BlockSpec((8, CONV_CB), lambda j, n: (0, j))],
        out_shape=[SDS((T, width), BF16), SDS((8, width), F32), SDS((8, width), F32)],
        name=name)(proj, proj, proj, conv_w, conv_b, dact, dact)


HPG = SSM_HEADS // SSM_GROUPS


def _iota(shape, dim):
    return lax.broadcasted_iota(jnp.int32, shape, dim)


def _mm(a, b, ca=1, cb=0):
    return lax.dot_general(a.astype(BF16), b.astype(BF16), (((ca,), (cb,)), ((), ())), preferred_element_type=F32)


def _split3(v):
    hi = v.astype(BF16)
    r1 = v - hi.astype(F32)
    mid = r1.astype(BF16)
    lo = (r1 - mid.astype(F32)).astype(BF16)
    return hi, mid, lo


def _sel_r(parts, onehot, ca=1, cb=0):
    out = lax.dot_general(parts[0], onehot, (((ca,), (cb,)), ((), ())), preferred_element_type=F32)
    for p in parts[1:]:
        out = out + lax.dot_general(p, onehot, (((ca,), (cb,)), ((), ())), preferred_element_type=F32)
    return out


def _sel_l(onehot, parts):
    out = jnp.dot(onehot, parts[0], preferred_element_type=F32)
    for p in parts[1:]:
        out = out + jnp.dot(onehot, p, preferred_element_type=F32)
    return out


def _rows8(*rows):
    r = _iota((8, rows[0].shape[1]), 0)
    out = jnp.zeros((8, rows[0].shape[1]), F32)
    for k, v in enumerate(rows):
        out = jnp.where(r == k, v, out)
    return out


def _ssd_forward(x, z, bm, cm, dt_raw, st_prev, dtb, alog, dskip, gn, g, cst_scr):
    li, si = _iota((BLK, BLK), 0), _iota((BLK, BLK), 1)
    dt_all = jax.nn.softplus(dt_raw + dtb)
    a_row = -jnp.exp(alog)
    a_all = dt_all * a_row
    cs_all = _sel_l((li >= si).astype(BF16), _split3(a_all))
    cs_parts = _split3(cs_all)
    spread = (_iota((BLK, GRP_W), 0) == g * HPG + jnp.right_shift(_iota((BLK, GRP_W), 1), 6)).astype(BF16)
    dt_e = _sel_r(_split3(dt_all), spread)
    cs_e = _sel_r(cs_parts, spread)
    d_e = _sel_r(_split3(_rows8(dskip)), spread)[0:1]
    cs_last_e = jnp.sum(jnp.where(_iota((BLK, GRP_W), 0) == BLK - 1, cs_e, 0.0), axis=0, keepdims=True)
    p_e = jnp.exp(cs_e)
    w_e = jnp.exp(cs_last_e - cs_e)
    cd_e = jnp.exp(cs_last_e)
    xr = x * dt_e
    cst_scr[...] = cs_all.T
    cst_g = cst_scr[pl.ds(pl.multiple_of(g * HPG, HPG), HPG), :]
    own = jnp.right_shift(_iota((HPG, HPG * BLK), 1), 7) == _iota((HPG, HPG * BLK), 0)
    ownf = own.astype(F32)
    q_rows = [ownf, ownf, ownf] + [jnp.where(own, jnp.concatenate([p.astype(F32)] * HPG, axis=1), 0.0)
                                   for p in _split3(cst_g)]
    q2 = jnp.concatenate(q_rows + [jnp.zeros((BLK - 6 * HPG, HPG * BLK), F32)], axis=0).astype(BF16)
    lane1 = _iota((1, BLK), 1)
    p2 = jnp.where((lane1 >= 3 * HPG) & (lane1 < 6 * HPG), -1.0, 0.0)
    for k, part in enumerate(cs_parts):
        pick = ((li == g * HPG + si - k * HPG) & (si >= k * HPG) & (si < (k + 1) * HPG)).astype(BF16)
        p2 = p2 + jnp.dot(part, pick, preferred_element_type=F32)
    dmat = jnp.dot(p2.astype(BF16), q2, preferred_element_type=F32)
    causal = _iota((BLK, HPG * BLK), 0) >= jnp.bitwise_and(_iota((BLK, HPG * BLK), 1), BLK - 1)
    lam = jnp.exp(jnp.where(causal, dmat, NEG))
    gmat = _mm(cm, bm, 1, 1)
    m_all = lam * jnp.concatenate([gmat] * HPG, axis=1)
    mb = m_all.astype(BF16)
    lo = _iota((BLK, BLK), 1) < HEAD
    xrb = xr.astype(BF16)
    zero = jnp.zeros((BLK, BLK), BF16)
    bds, yd = [], []
    for i in range(HPG // 2):
        t = xrb[:, BLK * i:BLK * (i + 1)]
        bd = jnp.concatenate([jnp.where(lo, t, zero), jnp.where(lo, zero, t)], axis=0)
        bds.append(bd)
        yd.append(jnp.dot(mb[:, 2 * BLK * i:2 * BLK * (i + 1)], bd, preferred_element_type=F32))
    cs_st = _mm(cm, st_prev)
    y = jnp.concatenate(yd, axis=1) + cs_st * p_e + d_e * x
    xrw = xr * w_e
    st_new = cd_e * st_prev + _mm(bm, xrw, 0, 0)
    yz = y * _silu(z)
    rn = lax.rsqrt(jnp.sum(yz * yz, axis=1, keepdims=True) / GRP_W + EPS)
    return dict(out=yz * rn * gn, st_new=st_new, dt_all=dt_all, a_row=a_row, dt_e=dt_e, d_e=d_e, p_e=p_e, w_e=w_e,
                cd_e=cd_e, xr=xr, xrw=xrw, lam=lam, m_all=m_all, mb=mb, bds=bds, cs_st=cs_st, y=y, yz=yz, rn=rn, lo=lo)


def _ssd_backward(f, x, z, bm, cm, dt_raw, st_prev, dtb, gn, g, dout, dst_next, cst_scr):
    li, si = _iota((BLK, BLK), 0), _iota((BLK, BLK), 1)
    yz, rn, y, p_e, w_e, cd_e, xr = f["yz"], f["rn"], f["y"], f["p_e"], f["w_e"], f["cd_e"], f["xr"]
    dgn = jnp.sum(dout * yz * rn, axis=0, keepdims=True)
    t = dout * gn
    dyz = rn * t - yz * (rn * rn * rn) * (jnp.sum(yz * t, axis=1, keepdims=True) / GRP_W)
    dy = dyz * _silu(z)
    dz = dyz * y * _dsilu(z)
    dx = f["d_e"] * dy
    dd_e = jnp.sum(dy * x, axis=0, keepdims=True)
    dcsst = dy * p_e
    dp_e = dy * f["cs_st"]
    dcm = _mm(dcsst, st_prev, 1, 1)
    dst_prev = _mm(cm, dcsst, 0, 0) + cd_e * dst_next
    dcd_e = jnp.sum(dst_next * st_prev, axis=0, keepdims=True)
    dbm = _mm(f["xrw"], dst_next, 1, 1)
    dxrw = _mm(bm, dst_next)
    dxr = dxrw * w_e
    dw_e = dxrw * xr
    dyb = dy.astype(BF16)
    dms, dxr_d = [], []
    for i in range(HPG // 2):
        dyp = dyb[:, BLK * i:BLK * (i + 1)]
        dms.append(lax.dot_general(dyp, f["bds"][i], (((1,), (1,)), ((), ())), preferred_element_type=F32))
        r = lax.dot_general(f["mb"][:, 2 * BLK * i:2 * BLK * (i + 1)], dyp, (((0,), (0,)), ((), ())),
                            preferred_element_type=F32)
        dxr_d.append(jnp.where(f["lo"], r[0:BLK], r[BLK:2 * BLK]))
    dm_all = jnp.concatenate(dms, axis=1)
    dxr = dxr + jnp.concatenate(dxr_d, axis=1)
    dlg = dm_all * f["lam"]
    dg = dlg[:, 0:BLK]
    for j in range(1, HPG):
        dg = dg + dlg[:, BLK * j:BLK * (j + 1)]
    dcm = dcm + _mm(dg, bm)
    dbm = dbm + _mm(dg, cm, 0, 0)
    q_all = dm_all * f["m_all"]
    col_sums = jnp.sum(q_all, axis=0, keepdims=True)
    cst_scr[...] = jnp.zeros_like(cst_scr)
    cst_scr[pl.ds(pl.multiple_of(g * HPG, HPG), HPG), :] = _rows8(
        *[col_sums[:, BLK * j:BLK * (j + 1)] for j in range(HPG)])
    dcs = -cst_scr[...].T
    for j in range(HPG):
        dcs = dcs + jnp.where(si == g * HPG + j,
                              jnp.sum(q_all[:, BLK * j:BLK * (j + 1)], axis=1, keepdims=True), 0.0)
    unspread = (_iota((GRP_W, BLK), 1) == g * HPG + jnp.right_shift(_iota((GRP_W, BLK), 0), 6)).astype(BF16)
    dww = dw_e * w_e
    per_head = _sel_r(_split3(jnp.concatenate([dp_e * p_e - dww, dxr * x], axis=0)), unspread)
    last = _sel_r(_split3(_rows8(jnp.sum(dww, axis=0, keepdims=True) + dcd_e * cd_e, dd_e)), unspread)
    dcs = dcs + per_head[0:BLK] + jnp.where(li == BLK - 1, last[0:1], 0.0)
    da = _sel_l((si >= li).astype(BF16), _split3(dcs))
    ddt_all = da * f["a_row"] + per_head[BLK:2 * BLK]
    dalog = jnp.sum(da * f["dt_all"], axis=0, keepdims=True) * f["a_row"]
    dx = dx + dxr * f["dt_e"]
    ddt_raw = ddt_all * jax.nn.sigmoid(dt_raw + dtb)
    ddtb = jnp.sum(ddt_raw, axis=0, keepdims=True)
    ddskip = last[1:2]
    return dict(dx=dx, dz=dz, dbm=dbm, dcm=dcm, ddt_raw=ddt_raw, dst_prev=dst_prev, ddtb=ddtb, dalog=dalog,
                ddskip=ddskip, dgn=dgn)


def _ssd_in_specs(rev):
    cidx = (lambda c: NB - 1 - c) if rev else (lambda c: c)
    return [
        pl.BlockSpec((BLK, GRP_W), lambda g, c: (cidx(c), g)),
        pl.BlockSpec((BLK, SSM_STATE), lambda g, c: (cidx(c), SSM_INNER // SSM_STATE + g)),
        pl.BlockSpec((BLK, SSM_STATE), lambda g, c: (cidx(c), SSM_INNER // SSM_STATE + SSM_GROUPS + g)),
        pl.BlockSpec((BLK, 128), lambda g, c: (cidx(c), C_DT // 128)),
        pl.BlockSpec((BLK, GRP_W), lambda g, c: (cidx(c), C_ZS // GRP_W + g)),
        pl.BlockSpec((1, 128), lambda g, c: (0, 0)),
        pl.BlockSpec((1, 128), lambda g, c: (0, 0)),
        pl.BlockSpec((1, 128), lambda g, c: (0, 0)),
        pl.BlockSpec((1, GRP_W), lambda g, c: (0, g)),
    ]


def _ssd_fwd(xbc_act, proj, dt_bias, a_log, d_skip, g_norm, gather=()):
    ng = len(gather)

    def body(*refs):
        xs_ref, b_ref, c_ref, dt_ref, z_ref, dtb_ref, al_ref, dsk_ref, gn_ref = refs[:9]
        y_ref, st_ref = refs[9 + ng:11 + ng]
        s_scr, cst_scr = refs[11 + 2 * ng:13 + 2 * ng]
        g = pl.program_id(0)
        c = pl.program_id(1)
        if ng:
            ag_start, ag_forward, ag_finish = _ag_program(refs[9:9 + ng], refs[11 + ng:11 + 2 * ng],
                                                          refs[13 + 2 * ng:])
            pl.when((g == 0) & (c == 0))(ag_start)
            pl.when((g == SSM_GROUPS - 1) & (c == 0))(ag_forward)

        @pl.when(c == 0)
        def _():
            s_scr[...] = jnp.zeros_like(s_scr)

        st_prev = s_scr[...]
        st_ref[0, 0] = st_prev
        f = _ssd_forward(xs_ref[...], z_ref[...], b_ref[...], c_ref[...], dt_ref[...], st_prev, dtb_ref[...],
                         al_ref[...], dsk_ref[...], gn_ref[...], g, cst_scr)
        y_ref[...] = f["out"].astype(BF16)
        s_scr[...] = f["st_new"]
        if ng:
            pl.when((g == SSM_GROUPS - 1) & (c == NB - 1))(ag_finish)

    return pl.pallas_call(
        body, grid=(SSM_GROUPS, NB), in_specs=_ssd_in_specs(False) + [ANY] * ng,
        out_specs=[pl.BlockSpec((BLK, GRP_W), lambda g, c: (c, g)),
                   pl.BlockSpec((1, 1, SSM_STATE, GRP_W), lambda g, c: (g, c, 0, 0))] + [ANY] * ng,
        out_shape=[SDS((T, SSM_INNER), BF16), SDS((SSM_GROUPS, NB, SSM_STATE, GRP_W), F32)]
        + [SDS((N_DEV,) + s.shape, s.dtype) for s in gather],
        scratch_shapes=[pltpu.VMEM((SSM_STATE, GRP_W), F32), pltpu.VMEM((BLK, BLK), F32)]
        + (_ag_scratch(gather) if ng else []),
        compiler_params=_cparams(),
        name="ssd_fwd")(xbc_act, xbc_act, xbc_act, proj, proj, dt_bias, a_log, d_skip, g_norm, *gather)


def _ssd_bwd(xbc_act, proj, dt_bias, a_log, d_skip, g_norm, states, dy, exchange=()):
    chips = exchange
    nc = len(chips)

    def body(*refs):
        xs_ref, b_ref, c_ref, dt_ref, z_ref, dtb_ref, al_ref, dsk_ref, gn_ref, st_ref, dy_ref = refs[:11]
        (dxs_ref, db_ref, dc_ref, ddt_ref, dz_ref, ddtb_ref, dal_ref, ddsk_ref, dgn_ref) = refs[11 + nc:20 + nc]
        ds_scr, cst_scr = refs[20 + 2 * nc:22 + 2 * nc]
        g = pl.program_id(0)
        c = pl.program_id(1)
        if nc:
            ch_start, ch_finish = _direct_program(refs[11:11 + nc], refs[20 + nc:20 + 2 * nc], refs[22 + 2 * nc:])
            pl.when((g == 0) & (c == 0))(ch_start)

        @pl.when(c == 0)
        def _():
            ds_scr[...] = jnp.zeros_like(ds_scr)
            dgn_ref[...] = jnp.zeros_like(dgn_ref)

        @pl.when((c == 0) & (g == 0))
        def _():
            ddtb_ref[...] = jnp.zeros_like(ddtb_ref)
            dal_ref[...] = jnp.zeros_like(dal_ref)
            ddsk_ref[...] = jnp.zeros_like(ddsk_ref)

        x, z, bm, cm, dt_raw, st_prev = xs_ref[...], z_ref[...], b_ref[...], c_ref[...], dt_ref[...], st_ref[0, 0]
        f = _ssd_forward(x, z, bm, cm, dt_raw, st_prev, dtb_ref[...], al_ref[...], dsk_ref[...], gn_ref[...], g,
                         cst_scr)
        d = _ssd_backward(f, x, z, bm, cm, dt_raw, st_prev, dtb_ref[...], gn_ref[...], g, dy_ref[...].astype(F32),
                          ds_scr[...], cst_scr)
        dxs_ref[...] = d["dx"]
        dz_ref[...] = d["dz"].astype(BF16)
        ds_scr[...] = d["dst_prev"]
        db_ref[...] = d["dbm"]
        dc_ref[...] = d["dcm"]
        ddt_ref[...] = d["ddt_raw"]
        dgn_ref[0:1, :] += d["dgn"]
        ddtb_ref[0:1, :] += d["ddtb"]
        dal_ref[0:1, :] += d["dalog"]
        ddsk_ref[0:1, :] += d["ddskip"]
        if nc:
            pl.when((g == SSM_GROUPS - 1) & (c == NB - 1))(ch_finish)

    rc = lambda c: NB - 1 - c
    small = pl.BlockSpec((8, 128), lambda g, c: (0, 0))
    return pl.pallas_call(
        body, grid=(SSM_GROUPS, NB),
        in_specs=_ssd_in_specs(True) + [
            pl.BlockSpec((1, 1, SSM_STATE, GRP_W), lambda g, c: (g, rc(c), 0, 0)),
            pl.BlockSpec((BLK, GRP_W), lambda g, c: (rc(c), g))] + [ANY] * nc,
        out_specs=[pl.BlockSpec((BLK, GRP_W), lambda g, c: (rc(c), g)),
                   pl.BlockSpec((BLK, SSM_STATE), lambda g, c: (rc(c), g)),
                   pl.BlockSpec((BLK, SSM_STATE), lambda g, c: (rc(c), g)),
                   pl.BlockSpec((BLK, 128), lambda g, c: (rc(c), g)),
                   pl.BlockSpec((BLK, GRP_W), lambda g, c: (rc(c), g)),
                   small, small, small,
                   pl.BlockSpec((8, GRP_W), lambda g, c: (0, g))] + [ANY] * nc,
        out_shape=[SDS((T, SSM_INNER), F32), SDS((T, GRP_W), F32), SDS((T, GRP_W), F32), SDS((T, GRP_W), F32),
                   SDS((T, SSM_INNER), BF16), SDS((8, 128), F32), SDS((8, 128), F32), SDS((8, 128), F32),
                   SDS((8, SSM_INNER), F32)] + [SDS(p.shape, p.dtype) for p in chips],
        scratch_shapes=[pltpu.VMEM((SSM_STATE, GRP_W), F32), pltpu.VMEM((BLK, BLK), F32)]
        + (_direct_scratch(chips) if nc else []),
        compiler_params=_cparams(),
        name="ssd_bwd")(xbc_act, xbc_act, xbc_act, proj, proj, dt_bias, a_log, d_skip, g_norm, states, dy, *chips)


POST_R = 272


def _post_a(o, proj, sn, w_att, w_ssm, w_o):
    def body(o_ref, za_ref, ga_ref, gs_ref, sn_ref, wa_ref, ws_ref, wo_ref, a_ref, mg_ref, ya_ref, ys_ref, out_ref):
        a = (o_ref[...] * _silu(za_ref[...])).astype(BF16)
        a_ref[...] = a
        ya = jnp.dot(a, wa_ref[...], preferred_element_type=F32)
        ys = jnp.dot(sn_ref[...], ws_ref[...], preferred_element_type=F32)
        ya_ref[...] = ya.astype(BF16)
        ys_ref[...] = ys.astype(BF16)
        mg = (jax.nn.sigmoid(ga_ref[...]) * ya + jax.nn.sigmoid(gs_ref[...]) * ys).astype(BF16)
        mg_ref[...] = mg
        out_ref[...] = jnp.dot(mg, wo_ref[...], preferred_element_type=F32)

    row = pl.BlockSpec((POST_R, D_MODEL), lambda i: (i, 0))
    pcol = lambda c0: pl.BlockSpec((POST_R, D_MODEL), lambda i: (i, c0 // D_MODEL))
    full = lambda r: pl.BlockSpec((r, D_MODEL), lambda i: (0, 0))
    return pl.pallas_call(
        body, grid=(T // POST_R,),
        in_specs=[row, pcol(C_ZA), pcol(C_GA), pcol(C_GS), pl.BlockSpec((POST_R, SSM_INNER), lambda i: (i, 0)),
                  full(D_MODEL), full(SSM_INNER), full(D_MODEL)],
        out_specs=[row, row, row, row, row],
        out_shape=[SDS((T, D_MODEL), BF16), SDS((T, D_MODEL), BF16), SDS((T, D_MODEL), BF16), SDS((T, D_MODEL), BF16),
                   SDS((T, D_MODEL), F32)],
        compiler_params=_cparams(), name="post_a")(o, proj, proj, proj, sn, w_att, w_ssm, w_o)


def _post_b(out, h, tgt, proj, ya, ys, o, g_post, w_att, w_ssm, w_o):
    def body(out_ref, h_ref, t_ref, za_ref, ga_ref, gs_ref, ya_ref, ys_ref, o_ref, gp_ref, wa_ref, ws_ref, wo_ref,
             loss_ref, dres_ref, dout_ref, dya_ref, dys_ref, dga_ref, dgs_ref, do_ref, dza_ref, dsn_ref, dgp_ref):
        i = pl.program_id(0)
        x = out_ref[...]
        gp = gp_ref[...]
        r = lax.rsqrt(jnp.mean(x * x, axis=-1, keepdims=True) + EPS)
        row = i * POST_R + lax.broadcasted_iota(jnp.int32, (POST_R, 1), 0)
        res = h_ref[...] + jnp.where(row >= PAD, x * r * gp, 0.0)
        live = row >= PAD + N_META
        err = jnp.where(live, res - t_ref[...], 0.0)
        lpart = 0.5 * jnp.sum(jnp.sum(err * err, axis=1, keepdims=True) / D_MODEL, axis=0, keepdims=True)
        dres = err / D_MODEL
        dres_ref[...] = dres
        gpart = jnp.sum(dres * x * r, axis=0, keepdims=True)

        @pl.when(i == 0)
        def _():
            loss_ref[...] = jnp.zeros_like(loss_ref)
            dgp_ref[...] = jnp.zeros_like(dgp_ref)

        loss_ref[...] += jnp.broadcast_to(lpart, loss_ref.shape)
        dgp_ref[0:1, :] += gpart
        gd = gp * dres
        dout = (r * gd - x * (r * r * r) * jnp.mean(x * gd, axis=-1, keepdims=True)).astype(BF16)
        dout_ref[...] = dout
        dmg = lax.dot_general(dout, wo_ref[...], (((1,), (1,)), ((), ())), preferred_element_type=F32)
        sga = jax.nn.sigmoid(ga_ref[...])
        sgs = jax.nn.sigmoid(gs_ref[...])
        dya = (dmg * sga).astype(BF16)
        dys = (dmg * sgs).astype(BF16)
        dya_ref[...] = dya
        dys_ref[...] = dys
        dga_ref[...] = (dmg * ya_ref[...].astype(F32) * sga * (1.0 - sga)).astype(BF16)
        dgs_ref[...] = (dmg * ys_ref[...].astype(F32) * sgs * (1.0 - sgs)).astype(BF16)
        da = lax.dot_general(dya, wa_ref[...], (((1,), (1,)), ((), ())), preferred_element_type=F32)
        za = za_ref[...]
        do_ref[...] = (da * _silu(za)).astype(BF16)
        dza_ref[...] = (da * o_ref[...] * _dsilu(za)).astype(BF16)
        dsn_ref[...] = lax.dot_general(dys, ws_ref[...], (((1,), (1,)), ((), ())),
                                       preferred_element_type=F32).astype(BF16)

    row = pl.BlockSpec((POST_R, D_MODEL), lambda i: (i, 0))
    pcol = lambda c0: pl.BlockSpec((POST_R, D_MODEL), lambda i: (i, c0 // D_MODEL))
    full = lambda r: pl.BlockSpec((r, D_MODEL), lambda i: (0, 0))
    small = pl.BlockSpec((8, D_MODEL), lambda i: (0, 0))
    return pl.pallas_call(
        body, grid=(T // POST_R,),
        in_specs=[row, row, row, pcol(C_ZA), pcol(C_GA), pcol(C_GS), row, row, row,
                  pl.BlockSpec((1, D_MODEL), lambda i: (0, 0)), full(D_MODEL), full(SSM_INNER), full(D_MODEL)],
        out_specs=[pl.BlockSpec((8, 128), lambda i: (0, 0)), row, row, row, row, row, row, row, row,
                   pl.BlockSpec((POST_R, SSM_INNER), lambda i: (i, 0)), small],
        out_shape=[SDS((8, 128), F32), SDS((T, D_MODEL), F32), SDS((T, D_MODEL), BF16), SDS((T, D_MODEL), BF16),
                   SDS((T, D_MODEL), BF16), SDS((T, D_MODEL), BF16), SDS((T, D_MODEL), BF16), SDS((T, D_MODEL), BF16),
                   SDS((T, D_MODEL), BF16), SDS((T, SSM_INNER), BF16), SDS((8, D_MODEL), F32)],
        compiler_params=_cparams(), name="post_b")(out, h, tgt, proj, proj, proj, ya, ys, o, g_post, w_att, w_ssm, w_o)


def _assemble(dq, dza, dga, dgs, dzs, dxx, dxb, dxc, dk, dv, ddt4):
    def body(dq_ref, dza_ref, dga_ref, dgs_ref, dzs_ref, dxx_ref, dxb_ref, dxc_ref, dk_ref, dv_ref, ddt_ref, o_ref):
        o_ref[:, C_Q:C_Q + D_MODEL] = dq_ref[...].astype(BF16)
        o_ref[:, C_ZA:C_ZA + D_MODEL] = dza_ref[...]
        o_ref[:, C_GA:C_GA + D_MODEL] = dga_ref[...]
        o_ref[:, C_GS:C_GS + D_MODEL] = dgs_ref[...]
        o_ref[:, C_ZS:C_ZS + SSM_INNER] = dzs_ref[...]
        o_ref[:, C_XBC:C_XBC + SSM_INNER] = dxx_ref[...]
        o_ref[:, C_XBC + SSM_INNER:C_XBC + SSM_INNER + GRP_W] = dxb_ref[...]
        o_ref[:, C_XBC + SSM_INNER + GRP_W:C_XBC + CONV_DIM] = dxc_ref[...]
        o_ref[:, C_K:C_K + KV_W] = dk_ref[...].astype(BF16)
        o_ref[:, C_V:C_V + KV_W] = dv_ref[...].astype(BF16)
        d4 = ddt_ref[...]
        o_ref[:, C_DT:C_DT + 128] = (d4[:, 0:128] + d4[:, 128:256] + d4[:, 256:384] + d4[:, 384:512]).astype(BF16)

    spec = lambda w: pl.BlockSpec((BLK, w), lambda i: (i, 0))
    ins = [dq, dza, dga, dgs, dzs, dxx, dxb, dxc, dk, dv, ddt4]
    return pl.pallas_call(
        body, grid=(NB,), in_specs=[spec(a.shape[1]) for a in ins], out_specs=spec(PW),
        out_shape=SDS((T, PW), BF16), name="assemble")(*ins)


def _adamw_math(w, g, m, v):
    m = ADAM_B1 * m + (1.0 - ADAM_B1) * g
    v = ADAM_B2 * v + (1.0 - ADAM_B2) * (g * g)
    m_hat = m / (1.0 - ADAM_B1 ** ADAM_STEP)
    v_hat = v / (1.0 - ADAM_B2 ** ADAM_STEP)
    delta = -ADAM_LR * (m_hat / (jnp.sqrt(v_hat) + ADAM_EPS) + ADAM_WD * w)
    return delta, m, v


def _sum_adamw(recv, w, m, v, tc, name):
    rows, cols = w.shape
    nslab = recv.shape[0]
    assert cols % tc == 0

    def body(r_ref, w_ref, m_ref, v_ref, g_ref, d_ref, nm_ref, nv_ref):
        g = r_ref[0].astype(F32)
        for d in range(1, nslab):
            g = g + r_ref[d].astype(F32)
        g_ref[...] = g
        delta, nm, nv = _adamw_math(w_ref[...], g, m_ref[...], v_ref[...])
        d_ref[...] = delta
        nm_ref[...] = nm
        nv_ref[...] = nv

    blk = pl.BlockSpec((rows, tc), lambda i: (0, i))
    return pl.pallas_call(
        body, grid=(cols // tc,),
        in_specs=[pl.BlockSpec((nslab, rows, tc), lambda i: (0, 0, i)), blk, blk, blk],
        out_specs=[blk, blk, blk, blk], out_shape=[SDS((rows, cols), F32)] * 4,
        compiler_params=_cparams(), name=name)(recv, w, m, v)


def _sum_adamw_rows3(recv, w3, m3, v3, name, exchange=()):
    pairs = 61
    assert (SHARD_IN // 2) % pairs == 0
    nsteps = SHARD_IN // 2 // pairs
    ne = len(exchange)

    def body(*refs):
        r_ref, w_ref, m_ref, v_ref = refs[:4]
        g_ref, d_ref, nm_ref, nv_ref = refs[4 + ne:8 + ne]
        if ne:
            ex_start, ex_finish = _direct_program(refs[4:4 + ne], refs[8 + ne:8 + 2 * ne], refs[8 + 2 * ne:])
            pl.when(pl.program_id(0) == 0)(ex_start)
        g = r_ref[0].astype(F32)
        for d in range(1, N_CHIP):
            g = g + r_ref[d].astype(F32)
        g = g.reshape(2 * pairs, ROW_TILES, 128)
        g_ref[...] = g
        delta, nm, nv = _adamw_math(w_ref[...], g, m_ref[...], v_ref[...])
        d_ref[...] = delta
        nm_ref[...] = nm
        nv_ref[...] = nv
        if ne:
            pl.when(pl.program_id(0) == nsteps - 1)(ex_finish)

    blk = pl.BlockSpec((2 * pairs, ROW_TILES, 128), lambda i: (i, 0, 0))
    return pl.pallas_call(
        body, grid=(nsteps,),
        in_specs=[pl.BlockSpec((N_CHIP, pairs, 2 * ROW_TILES, 128), lambda i: (0, i, 0, 0)), blk, blk, blk]
        + [ANY] * ne,
        out_specs=[blk, blk, blk, blk] + [ANY] * ne,
        out_shape=[SDS(w3.shape, F32)] * 4 + [SDS(p.shape, p.dtype) for p in exchange],
        scratch_shapes=_direct_scratch(exchange) if ne else [],
        compiler_params=_cparams(), name=name)(recv, w3, m3, v3, *exchange)


ROW_GPRE, ROW_CONVB, ROW_DTB, ROW_ALOG, ROW_DSKIP, ROW_SINK, ROW_GSSM, ROW_GPOST = 0, 1, 4, 5, 6, 7, 8, 10
REP_ROWS, ROW_CONVW, ROW_META, SM_ROWS = 16, 16, 24, 40
CW_SHARD = CONV_DIM // N_DEV
META_SHARD = D_MODEL // N_DEV


def _small_pack(dgpre, dbx, dbb, dbc, ddtb, dal, ddsk, dsink, dgn, dgp, dwx, dwb, dwc, dh):
    def body(dgpre_ref, dbx_ref, dbb_ref, dbc_ref, ddtb_ref, dal_ref, ddsk_ref, dsink_ref, dgn_ref, dgp_ref,
             dwx_ref, dwb_ref, dwc_ref, dh_ref, o_ref, rep):
        rep[...] = jnp.zeros_like(rep)
        rep[ROW_GPRE:ROW_GPRE + 1, :] = dgpre_ref[0:1, :]
        rep[ROW_CONVB:ROW_CONVB + 1, :] = dbx_ref[0:1, 0:1024]
        rep[ROW_CONVB + 1:ROW_CONVB + 2, :] = dbx_ref[0:1, 1024:2048]
        rep[ROW_CONVB + 2:ROW_CONVB + 3, 0:512] = dbb_ref[0:1, :]
        rep[ROW_CONVB + 2:ROW_CONVB + 3, 512:1024] = dbc_ref[0:1, :]
        rep[ROW_DTB:ROW_DTB + 1, 0:128] = ddtb_ref[0:1, :]
        rep[ROW_ALOG:ROW_ALOG + 1, 0:128] = dal_ref[0:1, :]
        rep[ROW_DSKIP:ROW_DSKIP + 1, 0:128] = ddsk_ref[0:1, :]
        rep[ROW_SINK:ROW_SINK + 1, 0:128] = dsink_ref[0:1, :]
        rep[ROW_GSSM:ROW_GSSM + 1, :] = dgn_ref[0:1, 0:1024]
        rep[ROW_GSSM + 1:ROW_GSSM + 2, :] = dgn_ref[0:1, 1024:2048]
        rep[ROW_GPOST:ROW_GPOST + 1, :] = dgp_ref[0:1, :]
        cw = jnp.concatenate([dwx_ref[...], dwb_ref[...], dwc_ref[...]], axis=1)
        mh = dh_ref[...]
        o_ref[...] = jnp.zeros_like(o_ref)
        for p in range(N_DEV):
            o_ref[p, 0:REP_ROWS, :] = rep[...]
            o_ref[p, ROW_CONVW:ROW_CONVW + 8, 0:CW_SHARD] = cw[:, p * CW_SHARD:(p + 1) * CW_SHARD]
            o_ref[p, ROW_META:ROW_META + N_META, 0:META_SHARD] = mh[:, p * META_SHARD:(p + 1) * META_SHARD]

    ins = [dgpre, dbx, dbb, dbc, ddtb, dal, ddsk, dsink, dgn, dgp, dwx, dwb, dwc]
    return pl.pallas_call(
        body, grid=(1,),
        in_specs=[pl.BlockSpec(a.shape, lambda i: (0, 0)) for a in ins]
        + [pl.BlockSpec((N_META, D_MODEL), lambda i: (PAD // N_META, 0))],
        out_specs=pl.BlockSpec((N_DEV, SM_ROWS, 1024), lambda i: (0, 0, 0)),
        out_shape=SDS((N_DEV, SM_ROWS, 1024), F32), scratch_shapes=[pltpu.VMEM((REP_ROWS, 1024), F32)],
        name="small_pack")(*ins, dh)


def _small_finish(recv, params):
    npar = len(params)

    def body(*refs):
        r_ref = refs[0]
        wmv = refs[1:1 + 3 * npar]
        outs = refs[1 + 3 * npar:1 + 7 * npar]
        gs = refs[-1]
        g = r_ref[0]
        for d in range(1, recv.shape[0]):
            g = g + r_ref[d]
        gs[...] = g
        grads = [
            gs[ROW_GPRE:ROW_GPRE + 1, :],
            jnp.concatenate([gs[ROW_CONVB + k:ROW_CONVB + k + 1, :] for k in range(3)], axis=1),
            gs[ROW_DTB:ROW_DTB + 1, 0:SSM_HEADS], gs[ROW_ALOG:ROW_ALOG + 1, 0:SSM_HEADS],
            gs[ROW_DSKIP:ROW_DSKIP + 1, 0:SSM_HEADS], gs[ROW_SINK:ROW_SINK + 1, 0:Q_HEADS],
            jnp.concatenate([gs[ROW_GSSM:ROW_GSSM + 1, :], gs[ROW_GSSM + 1:ROW_GSSM + 2, :]], axis=1),
            gs[ROW_GPOST:ROW_GPOST + 1, :],
            gs[ROW_CONVW:ROW_CONVW + 4, 0:CW_SHARD],
            gs[ROW_META:ROW_META + N_META, 0:META_SHARD]]
        for i in range(npar):
            w_ref, m_ref, v_ref = wmv[3 * i:3 * i + 3]
            delta, nm, nv = _adamw_math(w_ref[...], grads[i], m_ref[...], v_ref[...])
            outs[4 * i][...] = grads[i]
            outs[4 * i + 1][...] = delta
            outs[4 * i + 2][...] = nm
            outs[4 * i + 3][...] = nv

    flat = [a for wmv in params for a in wmv]
    res = pl.pallas_call(
        body, out_shape=[SDS(wmv[0].shape, F32) for wmv in params for _ in range(4)],
        scratch_shapes=[pltpu.VMEM((SM_ROWS, 1024), F32)], name="small_finish")(recv, *flat)
    return [tuple(res[4 * i:4 * i + 4]) for i in range(npar)]


def _slab(ref, px, py, pc):
    return ref.at[4 * px + 2 * py + pc]


def _bounce(src, dst, buf, sem):
    cp = pltpu.make_async_copy(src, buf, sem)
    cp.start()
    cp.wait()
    cp = pltpu.make_async_copy(buf, dst, sem)
    cp.start()
    cp.wait()


def _ag_program(ins, outs, scratch):
    na = len(ins)
    send_sems, recv_sems, local_sems = scratch[:3]
    bufs = scratch[3:]
    x, y, c = lax.axis_index("x"), lax.axis_index("y"), lax.axis_index("c")
    me, sibling = (x, y, c), (x, y, 1 - c)
    chips = [(1 - x, y), (x, 1 - y), (1 - x, 1 - y)]

    def copy(a, k, block, to, src=None):
        dst = _slab(outs[a], *block)
        return pltpu.make_async_remote_copy(
            src_ref=dst if src is None else src, dst_ref=dst, send_sem=send_sems.at[a, k],
            recv_sem=recv_sems.at[a, k], device_id=to, device_id_type=MESH)

    def own_sends():
        out = []
        for a in range(na):
            out.append(copy(a, 0, me, sibling, src=ins[a]))
            out += [copy(a, 1 + j, me, (*chip, c), src=ins[a]) for j, chip in enumerate(chips)]
        return out

    def start():
        for cp in own_sends():
            cp.start()
        for a in range(na):
            _bounce(ins[a], _slab(outs[a], *me), bufs[a], local_sems.at[a])

    def forward():
        for j, chip in enumerate(chips):
            for a in range(na):
                copy(a, 1 + j, (*chip, c), me).wait_recv()
                copy(a, 4 + j, (*chip, c), sibling).start()

    def finish():
        for a in range(na):
            copy(a, 0, sibling, me).wait_recv()
            for j, chip in enumerate(chips):
                copy(a, 4 + j, (*chip, 1 - c), me).wait_recv()
        for cp in own_sends():
            cp.wait_send()
        for j, chip in enumerate(chips):
            for a in range(na):
                copy(a, 4 + j, (*chip, c), sibling).wait_send()

    return start, forward, finish


def _ag_scratch(shards):
    na = len(shards)
    return [pltpu.SemaphoreType.DMA((na, 7)), pltpu.SemaphoreType.DMA((na, 7)),
            pltpu.SemaphoreType.DMA((na,))] + [pltpu.VMEM(s.shape, s.dtype) for s in shards]


def _all_gather(shards):
    na = len(shards)

    def body(*refs):
        start, forward, finish = _ag_program(refs[:na], refs[na:2 * na], refs[2 * na:])
        start()
        forward()
        finish()

    return pl.pallas_call(
        body, in_specs=[ANY] * na, out_specs=[ANY] * na,
        out_shape=[SDS((N_DEV,) + s.shape, s.dtype) for s in shards],
        scratch_shapes=_ag_scratch(shards), name="all_gather")(*shards)


N_CHIP = 4


def _pair_sum(own, got, name):
    na = len(own)

    def body(*refs):
        for a in range(na):
            o_ref, g_ref, s_ref = refs[a], refs[na + a], refs[2 * na + a]
            s_ref[...] = (o_ref[...].astype(F32) + g_ref[...].astype(F32)).astype(s_ref.dtype)

    def spec(p):
        nd = len(p.shape) - 1
        return pl.BlockSpec((1,) + p.shape[1:], lambda k, nd=nd: (k,) + (0,) * nd)

    return pl.pallas_call(
        body, grid=(N_CHIP,), in_specs=[spec(p) for p in own] + [spec(p) for p in got],
        out_specs=[spec(p) for p in own], out_shape=[SDS(p.shape, p.dtype) for p in own],
        compiler_params=_cparams(), name=name)(*own, *got)


def _chips_program(ins, outs, scratch):
    na = len(ins)
    send_sems, recv_sems, local_sems = scratch[:3]
    bufs = scratch[3:]
    x, y, c = lax.axis_index("x"), lax.axis_index("y"), lax.axis_index("c")
    mine = 2 * x + y
    chips = [(1 - x, y), (x, 1 - y), (1 - x, 1 - y)]

    def send(a, j):
        px, py = chips[j]
        return pltpu.make_async_remote_copy(
            src_ref=ins[a].at[2 * px + py], dst_ref=outs[a].at[mine], send_sem=send_sems.at[a, j],
            recv_sem=recv_sems.at[a, j], device_id=(px, py, c), device_id_type=MESH)

    def arrival(a, j):
        px, py = chips[j]
        return pltpu.make_async_remote_copy(
            src_ref=ins[a].at[2 * px + py], dst_ref=outs[a].at[2 * px + py], send_sem=send_sems.at[a, j],
            recv_sem=recv_sems.at[a, j], device_id=(px, py, c), device_id_type=MESH)

    def start():
        for a in range(na):
            for j in range(3):
                send(a, j).start()
        for a in range(na):
            _bounce(ins[a].at[mine], outs[a].at[mine], bufs[a], local_sems.at[a])

    def finish():
        for a in range(na):
            for j in range(3):
                arrival(a, j).wait_recv()
        for a in range(na):
            for j in range(3):
                send(a, j).wait_send()

    return start, finish


def _chips_scratch(parts):
    na = len(parts)
    return [pltpu.SemaphoreType.DMA((na, 3)), pltpu.SemaphoreType.DMA((na, 3)),
            pltpu.SemaphoreType.DMA((na,))] + [pltpu.VMEM(p.shape[1:], p.dtype) for p in parts]


def _direct_program(ins, outs, scratch):
    na = len(ins)
    send_sems, recv_sems, local_sems = scratch[:3]
    bufs = scratch[3:]
    x, y, c = lax.axis_index("x"), lax.axis_index("y"), lax.axis_index("c")
    me = (x, y, c)
    peers = []
    for k in range(1, N_DEV):
        dx, dy, dc = (k >> 2) & 1, (k >> 1) & 1, k & 1
        peers.append(((1 - x) if dx else x, (1 - y) if dy else y, (1 - c) if dc else c))

    def send(a, k):
        return pltpu.make_async_remote_copy(
            src_ref=_slab(ins[a], *peers[k]), dst_ref=_slab(outs[a], *me), send_sem=send_sems.at[a, k],
            recv_sem=recv_sems.at[a, k], device_id=peers[k], device_id_type=MESH)

    def arrival(a, k):
        return pltpu.make_async_remote_copy(
            src_ref=_slab(ins[a], *peers[k]), dst_ref=_slab(outs[a], *peers[k]), send_sem=send_sems.at[a, k],
            recv_sem=recv_sems.at[a, k], device_id=peers[k], device_id_type=MESH)

    def start():
        for a in range(na):
            for k in range(N_DEV - 1):
                send(a, k).start()
        for a in range(na):
            _bounce(_slab(ins[a], *me), _slab(outs[a], *me), bufs[a], local_sems.at[a])

    def finish():
        for a in range(na):
            for k in range(N_DEV - 1):
                arrival(a, k).wait_recv()
        for a in range(na):
            for k in range(N_DEV - 1):
                send(a, k).wait_send()

    return start, finish


def _direct_scratch(parts):
    na = len(parts)
    return [pltpu.SemaphoreType.DMA((na, N_DEV - 1)), pltpu.SemaphoreType.DMA((na, N_DEV - 1)),
            pltpu.SemaphoreType.DMA((na,))] + [pltpu.VMEM(p.shape[1:], p.dtype) for p in parts]


ROW_TILES = D_MODEL // 128


def _rows3(t):
    return jnp.transpose(t[0]).reshape(t.shape[2], ROW_TILES, 128)


def _unrows3(t):
    return jnp.transpose(t.reshape(t.shape[0], D_MODEL))[None]


def _cast_shards(w_in3, w_att, w_ssm, w_o):
    def body(wi_ref, wa_ref, ws_ref, wo_ref, a_ref, b_ref, c_ref, d_ref):
        a_ref[...] = wi_ref[...].reshape(SHARD_IN // 2, 2 * ROW_TILES, 128).astype(BF16)
        b_ref[...] = wa_ref[...].astype(BF16)
        c_ref[...] = ws_ref[...].astype(BF16)
        d_ref[...] = wo_ref[...].astype(BF16)

    return pl.pallas_call(
        body, out_shape=[SDS((SHARD_IN // 2, 2 * ROW_TILES, 128), BF16), SDS(w_att.shape, BF16),
                         SDS(w_ssm.shape, BF16), SDS(w_o.shape, BF16)],
        compiler_params=_cparams(), name="cast_shards")(w_in3, w_att, w_ssm, w_o)


def _pieces():
    out = []
    for r0, c0, w in _SEGS:
        r = r0
        while r < r0 + w:
            d = r // SHARD_IN
            n = min(r0 + w, (d + 1) * SHARD_IN) - r
            out.append((c0 + (r - r0), d, r - d * SHARD_IN, n))
            r += n
    return out


def _to_aligned_t(slabs):
    def body(a_ref, o_ref):
        for (t, d, s, n) in _pieces():
            o_ref[t:t + n, :] = a_ref[d, s // 2:(s + n) // 2].reshape(n, D_MODEL)
        o_ref[C_DT + 32:C_DT + 128, :] = jnp.zeros((96, D_MODEL), slabs.dtype)

    return pl.pallas_call(body, out_shape=SDS((PW, D_MODEL), slabs.dtype), compiler_params=_cparams(),
                          name="to_aligned")(slabs)


def _from_aligned_pair(g):
    slab = (SHARD_IN // 2, 2 * ROW_TILES, 128)
    by_slab = [[p for p in _pieces() if p[1] == d] for d in range(N_DEV)]

    def body(g_ref, own_ref, got_ref, slabs, send_sems, recv_sems, local_sems):
        x, y, c = lax.axis_index("x"), lax.axis_index("y"), lax.axis_index("c")
        sibling = (x, y, 1 - c)

        def to_own(d, k):
            return pltpu.make_async_copy(slabs.at[d], own_ref.at[k], local_sems.at[k])

        def to_sibling(d, k):
            return pltpu.make_async_remote_copy(
                src_ref=slabs.at[d], dst_ref=got_ref.at[k], send_sem=send_sems.at[k], recv_sem=recv_sems.at[k],
                device_id=sibling, device_id_type=MESH)

        for d in range(N_DEV):
            for (t, _, s, n) in by_slab[d]:
                slabs[d, s // 2:(s + n) // 2] = g_ref[t:t + n, :].reshape(n // 2, 2 * ROW_TILES, 128)
            k, side = d // 2, d % 2
            pl.when(c == side)(to_own(d, k).start)
            pl.when(c != side)(to_sibling(d, k).start)
        for k in range(N_CHIP):
            to_own(0, k).wait()
            to_sibling(0, k).wait()

    half = SDS((N_CHIP,) + slab, g.dtype)
    return pl.pallas_call(
        body, in_specs=[pl.BlockSpec(memory_space=pltpu.VMEM)], out_specs=[ANY, ANY], out_shape=[half, half],
        scratch_shapes=[pltpu.VMEM((N_DEV,) + slab, g.dtype), pltpu.SemaphoreType.DMA((N_CHIP,)),
                        pltpu.SemaphoreType.DMA((N_CHIP,)), pltpu.SemaphoreType.DMA((N_CHIP,))],
        compiler_params=_cparams(), name="from_aligned_pair")(g)


_SEGS = [
    (R_Q, C_Q, 1024), (R_K, C_K, 256), (R_V, C_V, 256), (R_ZA, C_ZA, 1024), (R_ZS, C_ZS, 2048),
    (R_XBC, C_XBC, 3072), (R_DT, C_DT, 32), (R_GA, C_GA, 1024), (R_GS, C_GS, 1024)]


def _pad_lanes(v, n=128):
    return jnp.pad(v, ((0, 0), (0, n - v.shape[1])))


def _device_step(h, tgt, w_alt, w_out, g_pre, conv_w8, conv_b, dt_bias, a_log, d_skip, sinks, g_ssm, g_post, on_mesh):
    dtb, al, dsk, snk = _pad_lanes(dt_bias), _pad_lanes(a_log), _pad_lanes(d_skip), _pad_lanes(sinks)
    u = _norm_u(h, g_pre)
    proj = _matmul(u, w_alt, "nt", F32, T, 896, "in_proj")
    o = _attn_fwd(proj, snk)
    xbc_act = _conv_fwd(proj, conv_w8, conv_b)
    if on_mesh:
        sn, states, att_all, ssm_all, o_all = _ssd_fwd(xbc_act, proj, dtb, al, dsk, g_ssm, gather=w_out)
        w_att = att_all.reshape(D_MODEL, D_MODEL)
        w_ssm = ssm_all.reshape(SSM_INNER, D_MODEL)
        w_o = o_all.reshape(D_MODEL, D_MODEL)
    else:
        sn, states = _ssd_fwd(xbc_act, proj, dtb, al, dsk, g_ssm)
        w_att, w_ssm, w_o = w_out
    a_in, mg, ya, ys, out = _post_a(o, proj, sn, w_att, w_ssm, w_o)
    (loss, dres, dout, dya, dys, dga, dgs, do, dza, dsn, dgp) = _post_b(
        out, h, tgt, proj, ya, ys, o, g_post, w_att, w_ssm, w_o)
    dw_att = _matmul(a_in, dya, "tn", BF16, D_MODEL, D_MODEL, "d_w_att")
    dw_ssm = _matmul(sn, dys, "tn", BF16, D_MODEL, D_MODEL, "d_w_ssm")
    dw_o = _matmul(mg, dout, "tn", BF16, D_MODEL, D_MODEL, "d_w_o")
    res = {}
    if on_mesh:
        parts = [dw_att.reshape(N_DEV, 128, D_MODEL), dw_ssm.reshape(N_DEV, 256, D_MODEL),
                 dw_o.reshape(N_DEV, 128, D_MODEL)]
        (dxs, dbm, dcm, ddt4, dzs, ddtb, dal, ddsk, dgn, res["r_att"], res["r_ssm"], res["r_o"]) = _ssd_bwd(
            xbc_act, proj, dtb, al, dsk, g_ssm, states, dsn, exchange=parts)
    else:
        dxs, dbm, dcm, ddt4, dzs, ddtb, dal, ddsk, dgn = _ssd_bwd(xbc_act, proj, dtb, al, dsk, g_ssm, states, dsn)
        res.update(dw_att=dw_att, dw_ssm=dw_ssm, dw_o=dw_o)
    dxx, dwx, dbx = _conv_bwd(proj, conv_w8, conv_b, dxs, 0, "conv_bwd_x")
    dxb, dwb, dbb = _conv_bwd(proj, conv_w8, conv_b, dbm, SSM_INNER, "conv_bwd_b")
    dxc, dwc, dbc = _conv_bwd(proj, conv_w8, conv_b, dcm, SSM_INNER + GRP_W, "conv_bwd_c")
    dq, dk, dv, dsink = _attn_bwd(proj, snk, do)
    dproj = _assemble(dq, dza, dga, dgs, dzs, dxx, dxb, dxc, dk, dv, ddt4)
    dw_alt = _matmul(dproj, u, "tn", BF16, 896, D_MODEL, "d_w_in")
    if on_mesh:
        own, got = _from_aligned_pair(dw_alt)
        dh, dgpre, res["r_in"] = _d_u_norm(dproj, w_alt, h, g_pre, dres,
                                           chips=_pair_sum([own], [got], "pair_sum_w_in"))
    else:
        dh, dgpre = _d_u_norm(dproj, w_alt, h, g_pre, dres)
        res["dw_alt"] = dw_alt
    small = (dgpre, dbx, dbb, dbc, ddtb, dal, ddsk, dsink, dgn, dgp, dwx, dwb, dwc)
    if on_mesh:
        res["small_pack"] = _small_pack(*small, dh)
    else:
        res["small"] = small
    res.update(loss=loss[0, 0], dh=dh)
    return res


def kernel(x, meta_tokens, g_pre, w_in, conv_w, conv_b, dt_bias, a_log, d_skip, attn_sinks, g_ssm_norm, w_out_att, w_out_ssm, w_out, g_post, loss_target, m_meta_tokens, m_g_pre, m_w_in, m_conv_w, m_conv_b, m_dt_bias, m_a_log, m_d_skip, m_attn_sinks, m_g_ssm_norm, m_w_out_att, m_w_out_ssm, m_w_out, m_g_post, v_meta_tokens, v_g_pre, v_w_in, v_conv_w, v_conv_b, v_dt_bias, v_a_log, v_d_skip, v_attn_sinks, v_g_ssm_norm, v_w_out_att, v_w_out_ssm, v_w_out, v_g_post):
    w_in3, m_in3, v_in3 = _rows3(w_in), _rows3(m_w_in), _rows3(v_w_in)
    a_sh, att_sh, ssm_sh, o_sh = _cast_shards(w_in3, w_out_att[0], w_out_ssm[0], w_out[0])
    cw_sh = jnp.pad(conv_w[0], ((0, 4), (0, 0)))
    a_all, meta_all, cw_all = _all_gather([a_sh, meta_tokens, cw_sh])
    w_alt = _to_aligned_t(a_all)
    meta_full = meta_all.transpose(1, 0, 2).reshape(N_META, D_MODEL)
    conv_w8 = cw_all.transpose(1, 0, 2).reshape(8, CONV_DIM)

    h = jnp.concatenate([jnp.zeros((PAD, D_MODEL), F32), meta_full, x[0]], axis=0)
    tgt = jnp.concatenate([jnp.zeros((PAD + N_META, D_MODEL), F32), loss_target[0]], axis=0)
    r = _device_step(h, tgt, w_alt, (att_sh, ssm_sh, o_sh), g_pre, conv_w8, conv_b, dt_bias, a_log, d_skip,
                     attn_sinks, g_ssm_norm, g_post, True)
    loss = lax.psum(r["loss"], ("x", "y", "c"))
    grad_x = r["dh"][PAD + N_META:][None]

    *res_in, r_small = _sum_adamw_rows3(r["r_in"], w_in3, m_in3, v_in3, "adamw_w_in", exchange=[r["small_pack"]])
    res_in = [_unrows3(t) for t in res_in]
    res_att = [t[None] for t in _sum_adamw(r["r_att"], w_out_att[0], m_w_out_att[0], v_w_out_att[0], 512,
                                           "adamw_w_att")]
    res_ssm = [t[None] for t in _sum_adamw(r["r_ssm"], w_out_ssm[0], m_w_out_ssm[0], v_w_out_ssm[0], 512,
                                           "adamw_w_ssm")]
    res_o = [t[None] for t in _sum_adamw(r["r_o"], w_out[0], m_w_out[0], v_w_out[0], 512, "adamw_w_o")]
    (res_gpre, res_convb, res_dtb, res_alog, res_dskip, res_sink, res_gssm, res_gpost, res_cw, res_meta) = _small_finish(
        r_small, [(g_pre, m_g_pre, v_g_pre), (conv_b, m_conv_b, v_conv_b), (dt_bias, m_dt_bias, v_dt_bias),
                       (a_log, m_a_log, v_a_log), (d_skip, m_d_skip, v_d_skip),
                       (attn_sinks, m_attn_sinks, v_attn_sinks), (g_ssm_norm, m_g_ssm_norm, v_g_ssm_norm),
                       (g_post, m_g_post, v_g_post), (conv_w[0], m_conv_w[0], v_conv_w[0]),
                       (meta_tokens, m_meta_tokens, v_meta_tokens)])
    res_cw = [t[None] for t in res_cw]
    per_weight = [res_meta, res_gpre, res_in, res_cw, res_convb, res_dtb, res_alog, res_dskip, res_sink, res_gssm,
                  res_att, res_ssm, res_o, res_gpost]
    return (loss, grad_x, *[p[0] for p in per_weight], *[p[1] for p in per_weight], *[p[2] for p in per_weight],
            *[p[3] for p in per_weight])
```

```python
import functools
import math

import jax
import jax.numpy as jnp
from jax import lax
from jax.experimental import pallas as pl
from jax.experimental.pallas import tpu as pltpu

F32 = jnp.float32
BF16 = jnp.bfloat16
SDS = jax.ShapeDtypeStruct
MESH = pl.DeviceIdType.MESH
ANY = pl.BlockSpec(memory_space=pl.ANY)

N_DEV = 8
D_MODEL = 1024
SEQ = 2048
N_META = 16
BLK = 128
PAD = 112
T = PAD + N_META + SEQ
NB = T // BLK
EPS = 1e-6
HEAD = 64
Q_HEADS = 16
KV_HEADS = 4
GROUP = 4
KV_W = 256
SSM_INNER = 2048
SSM_HEADS = 32
SSM_GROUPS = 4
GRP_W = 512
SSM_STATE = 128
CONV_DIM = 3072
IN_PROJ = 9760
SHARD_IN = IN_PROJ // N_DEV
NEG = -1e30

C_Q, C_ZA, C_GA, C_GS, C_ZS, C_XBC, C_K, C_V, C_DT = 0, 1024, 2048, 3072, 4096, 6144, 9216, 9472, 9728
PW = 9856
R_Q, R_K, R_V, R_ZA, R_ZS, R_XBC, R_DT, R_GA, R_GS = 0, 1024, 1280, 1536, 2560, 4608, 7680, 7712, 8736

ADAM_LR, ADAM_B1, ADAM_B2, ADAM_EPS, ADAM_WD, ADAM_STEP = 0.001, 0.9, 0.999, 1e-08, 0.01, 10

VMEM_LIMIT = 56 * 1024 * 1024


def _cparams():
    return pltpu.CompilerParams(vmem_limit_bytes=VMEM_LIMIT)


def _silu(x):
    return x * jax.nn.sigmoid(x)


def _dsilu(x):
    s = jax.nn.sigmoid(x)
    return s * (1.0 + x * (1.0 - s))


def _matmul(a, b, mode, out_dtype, tm, tn, name):
    if mode == "nt":
        (m, k), n = a.shape, b.shape[0]
        a_spec = pl.BlockSpec((tm, k), lambda i, j: (i, 0))
        b_spec = pl.BlockSpec((tn, k), lambda i, j: (j, 0))
        dims = (((1,), (1,)), ((), ()))
    else:
        assert mode == "tn"
        (k, m), n = a.shape, b.shape[1]
        a_spec = pl.BlockSpec((k, tm), lambda i, j: (0, i))
        b_spec = pl.BlockSpec((k, tn), lambda i, j: (0, j))
        dims = (((0,), (0,)), ((), ()))
    assert m % tm == 0 and n % tn == 0, (a.shape, b.shape, tm, tn)

    def body(a_ref, b_ref, o_ref):
        o_ref[...] = lax.dot_general(a_ref[...], b_ref[...], dims, preferred_element_type=F32).astype(out_dtype)

    return pl.pallas_call(
        body, grid=(m // tm, n // tn), in_specs=[a_spec, b_spec],
        out_specs=pl.BlockSpec((tm, tn), lambda i, j: (i, j)), out_shape=SDS((m, n), out_dtype),
        compiler_params=_cparams(), name=name)(a, b)


def _norm_u(h, g_pre):
    def body(h_ref, g_ref, u_ref):
        x = h_ref[...]
        r = lax.rsqrt(jnp.mean(x * x, axis=-1, keepdims=True) + EPS)
        u_ref[...] = (x * r * g_ref[...]).astype(BF16)

    return pl.pallas_call(
        body, grid=(NB,),
        in_specs=[pl.BlockSpec((BLK, D_MODEL), lambda i: (i, 0)), pl.BlockSpec((1, D_MODEL), lambda i: (0, 0))],
        out_specs=pl.BlockSpec((BLK, D_MODEL), lambda i: (i, 0)),
        out_shape=SDS((T, D_MODEL), BF16), name="norm_u")(h, g_pre)


DU_TM, DU_TK = T // 2, 1408


def _d_u_norm(dproj, w_alt, h, g_pre, dres, chips=()):
    nk = PW // DU_TK
    ni = T // DU_TM
    nc = len(chips)

    def body(*refs):
        a_ref, b_ref, h_ref, g_ref, dres_ref = refs[:5]
        dh_ref, dg_ref = refs[5 + nc:7 + nc]
        acc_ref = refs[7 + 2 * nc]
        i, kk = pl.program_id(0), pl.program_id(1)
        if nc:
            ch_start, ch_finish = _chips_program(refs[5:5 + nc], refs[7 + nc:7 + 2 * nc], refs[8 + 2 * nc:])
            pl.when((i == 0) & (kk == 0))(ch_start)
        part = jnp.dot(a_ref[...], b_ref[...], preferred_element_type=F32)

        @pl.when(kk == 0)
        def _():
            acc_ref[...] = part

        @pl.when((kk > 0) & (kk < nk - 1))
        def _():
            acc_ref[...] += part

        @pl.when(kk == nk - 1)
        def _():
            du_ = acc_ref[...] + part
            x = h_ref[...]
            r = lax.rsqrt(jnp.mean(x * x, axis=-1, keepdims=True) + EPS)
            gd = g_ref[...] * du_
            dx = r * gd - x * (r * r * r) * jnp.mean(x * gd, axis=-1, keepdims=True)
            dh_ref[...] = dx + dres_ref[...]
            gpart = jnp.concatenate([jnp.sum(du_ * x * r, axis=0, keepdims=True), jnp.zeros((7, D_MODEL), F32)],
                                    axis=0)

            @pl.when(i == 0)
            def _():
                dg_ref[...] = gpart

            @pl.when(i > 0)
            def _():
                dg_ref[...] += gpart

        if nc:
            pl.when((i == ni - 1) & (kk == nk - 1))(ch_finish)

    row = pl.BlockSpec((DU_TM, D_MODEL), lambda i, kk: (i, 0))
    return pl.pallas_call(
        body, grid=(ni, nk),
        in_specs=[pl.BlockSpec((DU_TM, DU_TK), lambda i, kk: (i, kk)),
                  pl.BlockSpec((DU_TK, D_MODEL), lambda i, kk: (kk, 0)),
                  row, pl.BlockSpec((1, D_MODEL), lambda i, kk: (0, 0)), row] + [ANY] * nc,
        out_specs=[row, pl.BlockSpec((8, D_MODEL), lambda i, kk: (0, 0))] + [ANY] * nc,
        out_shape=[SDS((T, D_MODEL), F32), SDS((8, D_MODEL), F32)] + [SDS(p.shape, p.dtype) for p in chips],
        scratch_shapes=[pltpu.VMEM((DU_TM, D_MODEL), F32)] + (_chips_scratch(chips) if nc else []),
        compiler_params=_cparams(), name="d_u_norm")(dproj, w_alt, h, g_pre, dres, *chips)


def _lane_pick(row, h):
    lane = lax.broadcasted_iota(jnp.int32, row.shape, 1)
    return jnp.sum(jnp.where(lane == h, row, 0.0), axis=1, keepdims=True)


def _attn_fn(q4s, kcats, vcats, kms, vms, sinks, n):
    r = lax.broadcasted_iota(jnp.int32, (GROUP * BLK, 2 * BLK), 0)
    s = lax.broadcasted_iota(jnp.int32, (GROUP * BLK, 2 * BLK), 1)
    i = jnp.bitwise_and(r, BLK - 1)
    gi = jnp.right_shift(r, 7)
    rel = i - s + BLK
    k_pos = n * BLK - BLK + s
    band_ok = (rel >= 0) & (rel < BLK) & (k_pos >= PAD + N_META)
    relf = rel.astype(F32)
    rm = lax.broadcasted_iota(jnp.int32, (GROUP * BLK, N_META), 0)
    mm = lax.broadcasted_iota(jnp.int32, (GROUP * BLK, N_META), 1)
    meta_ok = (PAD + mm) <= (n * BLK + jnp.bitwise_and(rm, BLK - 1))
    gcol = jnp.right_shift(lax.broadcasted_iota(jnp.int32, (GROUP * BLK, 1), 0), 7)
    outs = []
    for kh in range(KV_HEADS):
        slopes = [2.0 ** (-8.0 * (kh * GROUP + g + 1) / Q_HEADS) for g in range(GROUP)]
        slope = jnp.where(gi == 0, slopes[0], jnp.where(gi == 1, slopes[1], jnp.where(gi == 2, slopes[2], slopes[3])))
        sk = [_lane_pick(sinks, kh * GROUP + g) for g in range(GROUP)]
        sink = jnp.where(gcol == 0, sk[0], jnp.where(gcol == 1, sk[1], jnp.where(gcol == 2, sk[2], sk[3])))
        qb = (q4s[kh] * (HEAD ** -0.5)).astype(BF16)
        sb = lax.dot_general(qb, kcats[kh].astype(BF16), (((1,), (1,)), ((), ())), preferred_element_type=F32)
        sb = jnp.where(band_ok, sb - slope * relf, NEG)
        sm = lax.dot_general(qb, kms[kh].astype(BF16), (((1,), (1,)), ((), ())), preferred_element_type=F32)
        sm = jnp.where(meta_ok, sm, NEG)
        mx = jnp.maximum(jnp.maximum(jnp.max(sb, axis=1, keepdims=True), jnp.max(sm, axis=1, keepdims=True)), sink)
        mx = lax.stop_gradient(mx)
        eb = jnp.exp(sb - mx)
        em = jnp.exp(sm - mx)
        es = jnp.exp(sink - mx)
        inv = 1.0 / (jnp.sum(eb, axis=1, keepdims=True) + jnp.sum(em, axis=1, keepdims=True) + es)
        pb = (eb * inv).astype(BF16)
        pm = (em * inv).astype(BF16)
        o4 = (jnp.dot(pm, vms[kh].astype(BF16), preferred_element_type=F32)
              + jnp.dot(pb, vcats[kh].astype(BF16), preferred_element_type=F32))
        outs.append(o4)
    return outs


def _attn_specs():
    prev = lambda n: jnp.maximum(n - 1, 0)
    return [
        pl.BlockSpec((BLK, D_MODEL), lambda n: (n, C_Q // D_MODEL)),
        pl.BlockSpec((BLK, KV_W), lambda n: (prev(n), C_K // KV_W)),
        pl.BlockSpec((BLK, KV_W), lambda n: (n, C_K // KV_W)),
        pl.BlockSpec((BLK, KV_W), lambda n: (prev(n), C_V // KV_W)),
        pl.BlockSpec((BLK, KV_W), lambda n: (n, C_V // KV_W)),
        pl.BlockSpec((N_META, KV_W), lambda n: (PAD // N_META, C_K // KV_W)),
        pl.BlockSpec((N_META, KV_W), lambda n: (PAD // N_META, C_V // KV_W)),
        pl.BlockSpec((1, 128), lambda n: (0, 0)),
    ]


def _attn_load(q_ref, kp_ref, kc_ref, vp_ref, vc_ref, km_ref, vm_ref):
    q4s, kcats, vcats, kms, vms = [], [], [], [], []
    for kh in range(KV_HEADS):
        q4s.append(jnp.concatenate(
            [q_ref[:, (kh * GROUP + g) * HEAD:(kh * GROUP + g + 1) * HEAD] for g in range(GROUP)], axis=0))
        cs = slice(kh * HEAD, (kh + 1) * HEAD)
        kcats.append(jnp.concatenate([kp_ref[:, cs], kc_ref[:, cs]], axis=0))
        vcats.append(jnp.concatenate([vp_ref[:, cs], vc_ref[:, cs]], axis=0))
        kms.append(km_ref[:, cs])
        vms.append(vm_ref[:, cs])
    return q4s, kcats, vcats, kms, vms


def _attn_fwd(proj, sinks):
    def body(q_ref, kp_ref, kc_ref, vp_ref, vc_ref, km_ref, vm_ref, s_ref, o_ref):
        n = pl.program_id(0)
        args = _attn_load(q_ref, kp_ref, kc_ref, vp_ref, vc_ref, km_ref, vm_ref)
        outs = _attn_fn(*args, s_ref[...], n)
        for kh in range(KV_HEADS):
            for g in range(GROUP):
                hh = kh * GROUP + g
                o_ref[:, hh * HEAD:(hh + 1) * HEAD] = outs[kh][g * BLK:(g + 1) * BLK]

    return pl.pallas_call(
        body, grid=(NB,), in_specs=_attn_specs(),
        out_specs=pl.BlockSpec((BLK, D_MODEL), lambda n: (n, 0)),
        out_shape=SDS((T, D_MODEL), F32), name="attn_fwd")(proj, proj, proj, proj, proj, proj, proj, sinks)


def _attn_bwd(proj, sinks, do):
    def body(q_ref, kp_ref, kc_ref, vp_ref, vc_ref, km_ref, vm_ref, s_ref, do_ref, dq_ref, dk_ref, dv_ref, ds_ref):
        n = pl.program_id(0)

        @pl.when(n == 0)
        def _():
            dk_ref[...] = jnp.zeros_like(dk_ref)
            dv_ref[...] = jnp.zeros_like(dv_ref)
            ds_ref[...] = jnp.zeros_like(ds_ref)

        args = _attn_load(q_ref, kp_ref, kc_ref, vp_ref, vc_ref, km_ref, vm_ref)
        _, vjp = jax.vjp(lambda a, b, c, d, e, f: _attn_fn(a, b, c, d, e, f, n), *args, s_ref[...])
        do_f = do_ref[...].astype(F32)
        cot = [jnp.concatenate([do_f[:, (kh * GROUP + g) * HEAD:(kh * GROUP + g + 1) * HEAD] for g in range(GROUP)],
                               axis=0) for kh in range(KV_HEADS)]
        dq4s, dkcats, dvcats, dkms, dvms, dsk = vjp(cot)
        ds_ref[0:1, :] += dsk
        cur = pl.ds(pl.multiple_of(n * BLK, BLK), BLK)
        meta = slice(PAD, PAD + N_META)
        for kh in range(KV_HEADS):
            cs = slice(kh * HEAD, (kh + 1) * HEAD)
            for g in range(GROUP):
                hh = kh * GROUP + g
                dq_ref[:, hh * HEAD:(hh + 1) * HEAD] = dq4s[kh][g * BLK:(g + 1) * BLK]
            dk_ref[cur, cs] += dkcats[kh][BLK:]
            dv_ref[cur, cs] += dvcats[kh][BLK:]
            dk_ref[meta, cs] += dkms[kh]
            dv_ref[meta, cs] += dvms[kh]

        @pl.when(n > 0)
        def _():
            prv = pl.ds(pl.multiple_of((n - 1) * BLK, BLK), BLK)
            for kh in range(KV_HEADS):
                cs = slice(kh * HEAD, (kh + 1) * HEAD)
                dk_ref[prv, cs] += dkcats[kh][:BLK]
                dv_ref[prv, cs] += dvcats[kh][:BLK]

    full_kv = pl.BlockSpec((T, KV_W), lambda n: (0, 0))
    return pl.pallas_call(
        body, grid=(NB,),
        in_specs=_attn_specs() + [pl.BlockSpec((BLK, D_MODEL), lambda n: (n, 0))],
        out_specs=[pl.BlockSpec((BLK, D_MODEL), lambda n: (n, 0)), full_kv, full_kv,
                   pl.BlockSpec((8, 128), lambda n: (0, 0))],
        out_shape=[SDS((T, D_MODEL), F32), SDS((T, KV_W), F32), SDS((T, KV_W), F32), SDS((8, 128), F32)],
        name="attn_bwd")(proj, proj, proj, proj, proj, proj, proj, sinks, do)


def _conv_taps(xp, w, rows):
    return (w[0:1] * xp[5:5 + rows] + w[1:2] * xp[6:6 + rows] + w[2:3] * xp[7:7 + rows] + w[3:4] * xp[8:8 + rows])


def _conv_fwd(proj, conv_w, conv_b):
    CONV_CB = CONV_DIM
    ncb = CONV_DIM // CONV_CB
    cb0 = C_XBC // CONV_CB

    def body(tail_ref, cur_ref, w_ref, b_ref, o_ref):
        n = pl.program_id(1)
        tail = jnp.where(n > 0, tail_ref[...], 0.0)
        xp = jnp.concatenate([tail, cur_ref[...]], axis=0)
        conv = _conv_taps(xp, w_ref[...], BLK) + b_ref[...]
        row = n * BLK + lax.broadcasted_iota(jnp.int32, (BLK, 1), 0)
        o_ref[...] = jnp.where(row >= PAD, _silu(conv), 0.0)

    return pl.pallas_call(
        body, grid=(ncb, NB),
        in_specs=[pl.BlockSpec((8, CONV_CB), lambda j, n: (jnp.maximum(n * (BLK // 8) - 1, 0), cb0 + j)),
                  pl.BlockSpec((BLK, CONV_CB), lambda j, n: (n, cb0 + j)),
                  pl.BlockSpec((8, CONV_CB), lambda j, n: (0, j)),
                  pl.BlockSpec((1, CONV_CB), lambda j, n: (0, j))],
        out_specs=pl.BlockSpec((BLK, CONV_CB), lambda j, n: (n, j)),
        out_shape=SDS((T, CONV_DIM), F32), name="conv_fwd")(proj, proj, conv_w, conv_b)


def _conv_bwd(proj, conv_w, conv_b, dact, ch0, name):
    width = dact.shape[1]
    CONV_CB = width
    ncb = width // CONV_CB
    cb0 = (C_XBC + ch0) // CONV_CB
    wb0 = ch0 // CONV_CB
    last8 = T // 8 - 1

    def body(tail_ref, cur_ref, nxt_ref, w_ref, b_ref, dcur_ref, dnxt_ref, dx_ref, dw_ref, db_ref):
        n = pl.program_id(1)
        w = w_ref[...]
        tail = jnp.where(n > 0, tail_ref[...], 0.0)
        xp = jnp.concatenate([tail, cur_ref[...], nxt_ref[...]], axis=0)
        conv = _conv_taps(xp, w, BLK + 8) + b_ref[...]
        dext = jnp.concatenate([dcur_ref[...], jnp.where(n < NB - 1, dnxt_ref[...], 0.0)], axis=0)
        row = n * BLK + lax.broadcasted_iota(jnp.int32, (BLK + 8, 1), 0)
        dconv = jnp.where(row >= PAD, dext * _dsilu(conv), 0.0)
        dx = (w[0:1] * dconv[3:3 + BLK] + w[1:2] * dconv[2:2 + BLK] + w[2:3] * dconv[1:1 + BLK]
              + w[3:4] * dconv[0:BLK])
        dx_ref[...] = dx.astype(BF16)
        dc = dconv[0:BLK]
        dws = [jnp.sum(dc * xp[5 + k:5 + k + BLK], axis=0, keepdims=True) for k in range(4)]
        dwp = jnp.concatenate(dws + [jnp.zeros((4, CONV_CB), F32)], axis=0)
        dbp = jnp.sum(dc, axis=0, keepdims=True)

        @pl.when(n == 0)
        def _():
            dw_ref[...] = dwp
            db_ref[...] = jnp.concatenate([dbp, jnp.zeros((7, CONV_CB), F32)], axis=0)

        @pl.when(n > 0)
        def _():
            dw_ref[...] += dwp
            db_ref[0:1, :] += dbp

    return pl.pallas_call(
        body, grid=(ncb, NB),
        in_specs=[pl.BlockSpec((8, CONV_CB), lambda j, n: (jnp.maximum(n * (BLK // 8) - 1, 0), cb0 + j)),
                  pl.BlockSpec((BLK, CONV_CB), lambda j, n: (n, cb0 + j)),
                  pl.BlockSpec((8, CONV_CB), lambda j, n: (jnp.minimum((n + 1) * (BLK // 8), last8), cb0 + j)),
                  pl.BlockSpec((8, CONV_CB), lambda j, n: (0, wb0 + j)),
                  pl.BlockSpec((1, CONV_CB), lambda j, n: (0, wb0 + j)),
                  pl.BlockSpec((BLK, CONV_CB), lambda j, n: (n, j)),
                  pl.BlockSpec((8, CONV_CB), lambda j, n: (jnp.minimum((n + 1) * (BLK // 8), last8), j))],
        out_specs=[pl.BlockSpec((BLK, CONV_CB), lambda j, n: (n, j)),
                   pl.BlockSpec((8, CONV_CB), lambda j, n: (0, j)),
                   pl.BlockSpec((8, CONV_CB), lambda j, n: (0, j))],
        out_shape=[SDS((T, width), BF16), SDS((8, width), F32), SDS((8, width), F32)],
        name=name)(proj, proj, proj, conv_w, conv_b, dact, dact)


HPG = SSM_HEADS // SSM_GROUPS


def _iota(shape, dim):
    return lax.broadcasted_iota(jnp.int32, shape, dim)


def _mm(a, b, ca=1, cb=0):
    return lax.dot_general(a.astype(BF16), b.astype(BF16), (((ca,), (cb,)), ((), ())), preferred_element_type=F32)


def _split3(v):
    hi = v.astype(BF16)
    r1 = v - hi.astype(F32)
    mid = r1.astype(BF16)
    lo = (r1 - mid.astype(F32)).astype(BF16)
    return hi, mid, lo


def _sel_r(parts, onehot, ca=1, cb=0):
    out = lax.dot_general(parts[0], onehot, (((ca,), (cb,)), ((), ())), preferred_element_type=F32)
    for p in parts[1:]:
        out = out + lax.dot_general(p, onehot, (((ca,), (cb,)), ((), ())), preferred_element_type=F32)
    return out


def _sel_l(onehot, parts):
    out = jnp.dot(onehot, parts[0], preferred_element_type=F32)
    for p in parts[1:]:
        out = out + jnp.dot(onehot, p, preferred_element_type=F32)
    return out


def _rows8(*rows):
    r = _iota((8, rows[0].shape[1]), 0)
    out = jnp.zeros((8, rows[0].shape[1]), F32)
    for k, v in enumerate(rows):
        out = jnp.where(r == k, v, out)
    return out


def _ssd_forward(x, z, bm, cm, dt_raw, st_prev, dtb, alog, dskip, gn, g, cst_scr):
    li, si = _iota((BLK, BLK), 0), _iota((BLK, BLK), 1)
    dt_all = jax.nn.softplus(dt_raw + dtb)
    a_row = -jnp.exp(alog)
    a_all = dt_all * a_row
    cs_all = _sel_l((li >= si).astype(BF16), _split3(a_all))
    cs_parts = _split3(cs_all)
    spread = (_iota((BLK, GRP_W), 0) == g * HPG + jnp.right_shift(_iota((BLK, GRP_W), 1), 6)).astype(BF16)
    dt_e = _sel_r(_split3(dt_all), spread)
    cs_e = _sel_r(cs_parts, spread)
    d_e = _sel_r(_split3(_rows8(dskip)), spread)[0:1]
    cs_last_e = jnp.sum(jnp.where(_iota((BLK, GRP_W), 0) == BLK - 1, cs_e, 0.0), axis=0, keepdims=True)
    p_e = jnp.exp(cs_e)
    w_e = jnp.exp(cs_last_e - cs_e)
    cd_e = jnp.exp(cs_last_e)
    xr = x * dt_e
    cst_scr[...] = cs_all.T
    cst_g = cst_scr[pl.ds(pl.multiple_of(g * HPG, HPG), HPG), :]
    own = jnp.right_shift(_iota((HPG, HPG * BLK), 1), 7) == _iota((HPG, HPG * BLK), 0)
    ownf = own.astype(F32)
    q_rows = [ownf, ownf, ownf] + [jnp.where(own, jnp.concatenate([p.astype(F32)] * HPG, axis=1), 0.0)
                                   for p in _split3(cst_g)]
    q2 = jnp.concatenate(q_rows + [jnp.zeros((BLK - 6 * HPG, HPG * BLK), F32)], axis=0).astype(BF16)
    lane1 = _iota((1, BLK), 1)
    p2 = jnp.where((lane1 >= 3 * HPG) & (lane1 < 6 * HPG), -1.0, 0.0)
    for k, part in enumerate(cs_parts):
        pick = ((li == g * HPG + si - k * HPG) & (si >= k * HPG) & (si < (k + 1) * HPG)).astype(BF16)
        p2 = p2 + jnp.dot(part, pick, preferred_element_type=F32)
    dmat = jnp.dot(p2.astype(BF16), q2, preferred_element_type=F32)
    causal = _iota((BLK, HPG * BLK), 0) >= jnp.bitwise_and(_iota((BLK, HPG * BLK), 1), BLK - 1)
    lam = jnp.exp(jnp.where(causal, dmat, NEG))
    gmat = _mm(cm, bm, 1, 1)
    m_all = lam * jnp.concatenate([gmat] * HPG, axis=1)
    mb = m_all.astype(BF16)
    lo = _iota((BLK, BLK), 1) < HEAD
    xrb = xr.astype(BF16)
    zero = jnp.zeros((BLK, BLK), BF16)
    bds, yd = [], []
    for i in range(HPG // 2):
        t = xrb[:, BLK * i:BLK * (i + 1)]
        bd = jnp.concatenate([jnp.where(lo, t, zero), jnp.where(lo, zero, t)], axis=0)
        bds.append(bd)
        yd.append(jnp.dot(mb[:, 2 * BLK * i:2 * BLK * (i + 1)], bd, preferred_element_type=F32))
    cs_st = _mm(cm, st_prev)
    y = jnp.concatenate(yd, axis=1) + cs_st * p_e + d_e * x
    xrw = xr * w_e
    st_new = cd_e * st_prev + _mm(bm, xrw, 0, 0)
    yz = y * _silu(z)
    rn = lax.rsqrt(jnp.sum(yz * yz, axis=1, keepdims=True) / GRP_W + EPS)
    return dict(out=yz * rn * gn, st_new=st_new, dt_all=dt_all, a_row=a_row, dt_e=dt_e, d_e=d_e, p_e=p_e, w_e=w_e,
                cd_e=cd_e, xr=xr, xrw=xrw, lam=lam, m_all=m_all, mb=mb, bds=bds, cs_st=cs_st, y=y, yz=yz, rn=rn, lo=lo)


def _ssd_backward(f, x, z, bm, cm, dt_raw, st_prev, dtb, gn, g, dout, dst_next, cst_scr):
    li, si = _iota((BLK, BLK), 0), _iota((BLK, BLK), 1)
    yz, rn, y, p_e, w_e, cd_e, xr = f["yz"], f["rn"], f["y"], f["p_e"], f["w_e"], f["cd_e"], f["xr"]
    dgn = jnp.sum(dout * yz * rn, axis=0, keepdims=True)
    t = dout * gn
    dyz = rn * t - yz * (rn * rn * rn) * (jnp.sum(yz * t, axis=1, keepdims=True) / GRP_W)
    dy = dyz * _silu(z)
    dz = dyz * y * _dsilu(z)
    dx = f["d_e"] * dy
    dd_e = jnp.sum(dy * x, axis=0, keepdims=True)
    dcsst = dy * p_e
    dp_e = dy * f["cs_st"]
    dcm = _mm(dcsst, st_prev, 1, 1)
    dst_prev = _mm(cm, dcsst, 0, 0) + cd_e * dst_next
    dcd_e = jnp.sum(dst_next * st_prev, axis=0, keepdims=True)
    dbm = _mm(f["xrw"], dst_next, 1, 1)
    dxrw = _mm(bm, dst_next)
    dxr = dxrw * w_e
    dw_e = dxrw * xr
    dyb = dy.astype(BF16)
    dms, dxr_d = [], []
    for i in range(HPG // 2):
        dyp = dyb[:, BLK * i:BLK * (i + 1)]
        dms.append(lax.dot_general(dyp, f["bds"][i], (((1,), (1,)), ((), ())), preferred_element_type=F32))
        r = lax.dot_general(f["mb"][:, 2 * BLK * i:2 * BLK * (i + 1)], dyp, (((0,), (0,)), ((), ())),
                            preferred_element_type=F32)
        dxr_d.append(jnp.where(f["lo"], r[0:BLK], r[BLK:2 * BLK]))
    dm_all = jnp.concatenate(dms, axis=1)
    dxr = dxr + jnp.concatenate(dxr_d, axis=1)
    dlg = dm_all * f["lam"]
    dg = dlg[:, 0:BLK]
    for j in range(1, HPG):
        dg = dg + dlg[:, BLK * j:BLK * (j + 1)]
    dcm = dcm + _mm(dg, bm)
    dbm = dbm + _mm(dg, cm, 0, 0)
    q_all = dm_all * f["m_all"]
    col_sums = jnp.sum(q_all, axis=0, keepdims=True)
    cst_scr[...] = jnp.zeros_like(cst_scr)
    cst_scr[pl.ds(pl.multiple_of(g * HPG, HPG), HPG), :] = _rows8(
        *[col_sums[:, BLK * j:BLK * (j + 1)] for j in range(HPG)])
    dcs = -cst_scr[...].T
    for j in range(HPG):
        dcs = dcs + jnp.where(si == g * HPG + j,
                              jnp.sum(q_all[:, BLK * j:BLK * (j + 1)], axis=1, keepdims=True), 0.0)
    unspread = (_iota((GRP_W, BLK), 1) == g * HPG + jnp.right_shift(_iota((GRP_W, BLK), 0), 6)).astype(BF16)
    dww = dw_e * w_e
    per_head = _sel_r(_split3(jnp.concatenate([dp_e * p_e - dww, dxr * x], axis=0)), unspread)
    last = _sel_r(_split3(_rows8(jnp.sum(dww, axis=0, keepdims=True) + dcd_e * cd_e, dd_e)), unspread)
    dcs = dcs + per_head[0:BLK] + jnp.where(li == BLK - 1, last[0:1], 0.0)
    da = _sel_l((si >= li).astype(BF16), _split3(dcs))
    ddt_all = da * f["a_row"] + per_head[BLK:2 * BLK]
    dalog = jnp.sum(da * f["dt_all"], axis=0, keepdims=True) * f["a_row"]
    dx = dx + dxr * f["dt_e"]
    ddt_raw = ddt_all * jax.nn.sigmoid(dt_raw + dtb)
    ddtb = jnp.sum(ddt_raw, axis=0, keepdims=True)
    ddskip = last[1:2]
    return dict(dx=dx, dz=dz, dbm=dbm, dcm=dcm, ddt_raw=ddt_raw, dst_prev=dst_prev, ddtb=ddtb, dalog=dalog,
                ddskip=ddskip, dgn=dgn)


GPS = 2
NPG = SSM_GROUPS // GPS


def _ssd_in_specs(rev):
    cidx = (lambda c: NB - 1 - c) if rev else (lambda c: c)
    wx, wb = GPS * GRP_W, GPS * SSM_STATE
    return [
        pl.BlockSpec((BLK, wx), lambda p, c: (cidx(c), p)),
        pl.BlockSpec((BLK, wb), lambda p, c: (cidx(c), SSM_INNER // wb + p)),
        pl.BlockSpec((BLK, wb), lambda p, c: (cidx(c), (SSM_INNER + SSM_GROUPS * SSM_STATE) // wb + p)),
        pl.BlockSpec((BLK, 128), lambda p, c: (cidx(c), C_DT // 128)),
        pl.BlockSpec((BLK, wx), lambda p, c: (cidx(c), C_ZS // wx + p)),
        pl.BlockSpec((1, 128), lambda p, c: (0, 0)),
        pl.BlockSpec((1, 128), lambda p, c: (0, 0)),
        pl.BlockSpec((1, 128), lambda p, c: (0, 0)),
        pl.BlockSpec((1, wx), lambda p, c: (0, p)),
    ]


def _grp(ref, i, w):
    return ref[:, i * w:(i + 1) * w]


def _ssd_fwd(xbc_act, proj, dt_bias, a_log, d_skip, g_norm, gather=()):
    ng = len(gather)

    def body(*refs):
        xs_ref, b_ref, c_ref, dt_ref, z_ref, dtb_ref, al_ref, dsk_ref, gn_ref = refs[:9]
        y_ref, st_ref = refs[9 + ng:11 + ng]
        s_scr, cst_scr = refs[11 + 2 * ng:13 + 2 * ng]
        p = pl.program_id(0)
        c = pl.program_id(1)
        if ng:
            ag_start, ag_forward, ag_finish = _ag_program(refs[9:9 + ng], refs[11 + ng:11 + 2 * ng],
                                                          refs[13 + 2 * ng:])
            pl.when((p == 0) & (c == 0))(ag_start)
            pl.when((p == NPG - 1) & (c == NB // 2))(ag_forward)

        @pl.when(c == 0)
        def _():
            s_scr[...] = jnp.zeros_like(s_scr)

        for i in range(GPS):
            st_prev = s_scr[i]
            st_ref[i, 0] = st_prev
            f = _ssd_forward(_grp(xs_ref, i, GRP_W), _grp(z_ref, i, GRP_W), _grp(b_ref, i, SSM_STATE),
                             _grp(c_ref, i, SSM_STATE), dt_ref[...], st_prev, dtb_ref[...], al_ref[...],
                             dsk_ref[...], _grp(gn_ref, i, GRP_W), p * GPS + i, cst_scr.at[i])
            y_ref[:, i * GRP_W:(i + 1) * GRP_W] = f["out"].astype(BF16)
            s_scr[i] = f["st_new"]
        if ng:
            pl.when((p == NPG - 1) & (c == NB - 1))(ag_finish)

    return pl.pallas_call(
        body, grid=(NPG, NB), in_specs=_ssd_in_specs(False) + [ANY] * ng,
        out_specs=[pl.BlockSpec((BLK, GPS * GRP_W), lambda p, c: (c, p)),
                   pl.BlockSpec((GPS, 1, SSM_STATE, GRP_W), lambda p, c: (p, c, 0, 0))] + [ANY] * ng,
        out_shape=[SDS((T, SSM_INNER), BF16), SDS((SSM_GROUPS, NB, SSM_STATE, GRP_W), F32)]
        + [SDS((N_DEV,) + s.shape, s.dtype) for s in gather],
        scratch_shapes=[pltpu.VMEM((GPS, SSM_STATE, GRP_W), F32), pltpu.VMEM((GPS, BLK, BLK), F32)]
        + (_ag_scratch(gather) if ng else []),
        compiler_params=_cparams(),
        name="ssd_fwd")(xbc_act, xbc_act, xbc_act, proj, proj, dt_bias, a_log, d_skip, g_norm, *gather)


def _ssd_bwd(xbc_act, proj, dt_bias, a_log, d_skip, g_norm, states, dy, exchange=()):
    chips = exchange
    nc = len(chips)

    def body(*refs):
        xs_ref, b_ref, c_ref, dt_ref, z_ref, dtb_ref, al_ref, dsk_ref, gn_ref, st_ref, dy_ref = refs[:11]
        (dxs_ref, db_ref, dc_ref, ddt_ref, dz_ref, ddtb_ref, dal_ref, ddsk_ref, dgn_ref) = refs[11 + nc:20 + nc]
        ds_scr, cst_scr = refs[20 + 2 * nc:22 + 2 * nc]
        p = pl.program_id(0)
        c = pl.program_id(1)
        if nc:
            ch_start, ch_finish = _direct_program(refs[11:11 + nc], refs[20 + nc:20 + 2 * nc], refs[22 + 2 * nc:])
            pl.when((p == 0) & (c == 0))(ch_start)

        @pl.when(c == 0)
        def _():
            ds_scr[...] = jnp.zeros_like(ds_scr)
            dgn_ref[...] = jnp.zeros_like(dgn_ref)

        @pl.when((c == 0) & (p == 0))
        def _():
            ddtb_ref[...] = jnp.zeros_like(ddtb_ref)
            dal_ref[...] = jnp.zeros_like(dal_ref)
            ddsk_ref[...] = jnp.zeros_like(ddsk_ref)

        dt_raw = dt_ref[...]
        for i in range(GPS):
            g = p * GPS + i
            x, z, gn = _grp(xs_ref, i, GRP_W), _grp(z_ref, i, GRP_W), _grp(gn_ref, i, GRP_W)
            bm, cm, st_prev = _grp(b_ref, i, SSM_STATE), _grp(c_ref, i, SSM_STATE), st_ref[i, 0]
            f = _ssd_forward(x, z, bm, cm, dt_raw, st_prev, dtb_ref[...], al_ref[...], dsk_ref[...], gn, g,
                             cst_scr.at[i])
            d = _ssd_backward(f, x, z, bm, cm, dt_raw, st_prev, dtb_ref[...], gn, g,
                              _grp(dy_ref, i, GRP_W).astype(F32), ds_scr[i], cst_scr.at[i])
            dxs_ref[:, i * GRP_W:(i + 1) * GRP_W] = d["dx"]
            dz_ref[:, i * GRP_W:(i + 1) * GRP_W] = d["dz"].astype(BF16)
            ds_scr[i] = d["dst_prev"]
            db_ref[:, i * SSM_STATE:(i + 1) * SSM_STATE] = d["dbm"]
            dc_ref[:, i * SSM_STATE:(i + 1) * SSM_STATE] = d["dcm"]
            ddt_ref[:, i * 128:(i + 1) * 128] = d["ddt_raw"]
            dgn_ref[0:1, i * GRP_W:(i + 1) * GRP_W] += d["dgn"]
            ddtb_ref[0:1, :] += d["ddtb"]
            dal_ref[0:1, :] += d["dalog"]
            ddsk_ref[0:1, :] += d["ddskip"]
        if nc:
            pl.when((p == NPG - 1) & (c == NB - 1))(ch_finish)

    rc = lambda c: NB - 1 - c
    small = pl.BlockSpec((8, 128), lambda p, c: (0, 0))
    wx, wb = GPS * GRP_W, GPS * SSM_STATE
    return pl.pallas_call(
        body, grid=(NPG, NB),
        in_specs=_ssd_in_specs(True) + [
            pl.BlockSpec((GPS, 1, SSM_STATE, GRP_W), lambda p, c: (p, rc(c), 0, 0)),
            pl.BlockSpec((BLK, wx), lambda p, c: (rc(c), p))] + [ANY] * nc,
        out_specs=[pl.BlockSpec((BLK, wx), lambda p, c: (rc(c), p)),
                   pl.BlockSpec((BLK, wb), lambda p, c: (rc(c), p)),
                   pl.BlockSpec((BLK, wb), lambda p, c: (rc(c), p)),
                   pl.BlockSpec((BLK, GPS * 128), lambda p, c: (rc(c), p)),
                   pl.BlockSpec((BLK, wx), lambda p, c: (rc(c), p)),
                   small, small, small,
                   pl.BlockSpec((8, wx), lambda p, c: (0, p))] + [ANY] * nc,
        out_shape=[SDS((T, SSM_INNER), F32), SDS((T, GRP_W), F32), SDS((T, GRP_W), F32), SDS((T, GRP_W), F32),
                   SDS((T, SSM_INNER), BF16), SDS((8, 128), F32), SDS((8, 128), F32), SDS((8, 128), F32),
                   SDS((8, SSM_INNER), F32)] + [SDS(p.shape, p.dtype) for p in chips],
        scratch_shapes=[pltpu.VMEM((GPS, SSM_STATE, GRP_W), F32), pltpu.VMEM((GPS, BLK, BLK), F32)]
        + (_direct_scratch(chips) if nc else []),
        compiler_params=_cparams(),
        name="ssd_bwd")(xbc_act, xbc_act, xbc_act, proj, proj, dt_bias, a_log, d_skip, g_norm, states, dy, *chips)


POST_R = 272


def _post_a(o, proj, sn, w_att, w_ssm, w_o):
    def body(o_ref, za_ref, ga_ref, gs_ref, sn_ref, wa_ref, ws_ref, wo_ref, a_ref, mg_ref, ya_ref, ys_ref, out_ref):
        a = (o_ref[...] * _silu(za_ref[...])).astype(BF16)
        a_ref[...] = a
        ya = jnp.dot(a, wa_ref[...], preferred_element_type=F32)
        ys = jnp.dot(sn_ref[...], ws_ref[...], preferred_element_type=F32)
        ya_ref[...] = ya.astype(BF16)
        ys_ref[...] = ys.astype(BF16)
        mg = (jax.nn.sigmoid(ga_ref[...]) * ya + jax.nn.sigmoid(gs_ref[...]) * ys).astype(BF16)
        mg_ref[...] = mg
        out_ref[...] = jnp.dot(mg, wo_ref[...], preferred_element_type=F32)

    row = pl.BlockSpec((POST_R, D_MODEL), lambda i: (i, 0))
    pcol = lambda c0: pl.BlockSpec((POST_R, D_MODEL), lambda i: (i, c0 // D_MODEL))
    full = lambda r: pl.BlockSpec((r, D_MODEL), lambda i: (0, 0))
    return pl.pallas_call(
        body, grid=(T // POST_R,),
        in_specs=[row, pcol(C_ZA), pcol(C_GA), pcol(C_GS), pl.BlockSpec((POST_R, SSM_INNER), lambda i: (i, 0)),
                  full(D_MODEL), full(SSM_INNER), full(D_MODEL)],
        out_specs=[row, row, row, row, row],
        out_shape=[SDS((T, D_MODEL), BF16), SDS((T, D_MODEL), BF16), SDS((T, D_MODEL), BF16), SDS((T, D_MODEL), BF16),
                   SDS((T, D_MODEL), F32)],
        compiler_params=_cparams(), name="post_a")(o, proj, proj, proj, sn, w_att, w_ssm, w_o)


def _post_b(out, h, tgt, proj, ya, ys, o, g_post, w_att, w_ssm, w_o):
    def body(out_ref, h_ref, t_ref, za_ref, ga_ref, gs_ref, ya_ref, ys_ref, o_ref, gp_ref, wa_ref, ws_ref, wo_ref,
             loss_ref, dres_ref, dout_ref, dya_ref, dys_ref, dga_ref, dgs_ref, do_ref, dza_ref, dsn_ref, dgp_ref):
        i = pl.program_id(0)
        x = out_ref[...]
        gp = gp_ref[...]
        r = lax.rsqrt(jnp.mean(x * x, axis=-1, keepdims=True) + EPS)
        row = i * POST_R + lax.broadcasted_iota(jnp.int32, (POST_R, 1), 0)
        res = h_ref[...] + jnp.where(row >= PAD, x * r * gp, 0.0)
        live = row >= PAD + N_META
        err = jnp.where(live, res - t_ref[...], 0.0)
        lpart = 0.5 * jnp.sum(jnp.sum(err * err, axis=1, keepdims=True) / D_MODEL, axis=0, keepdims=True)
        dres = err / D_MODEL
        dres_ref[...] = dres
        gpart = jnp.sum(dres * x * r, axis=0, keepdims=True)

        @pl.when(i == 0)
        def _():
            loss_ref[...] = jnp.zeros_like(loss_ref)
            dgp_ref[...] = jnp.zeros_like(dgp_ref)

        loss_ref[...] += jnp.broadcast_to(lpart, loss_ref.shape)
        dgp_ref[0:1, :] += gpart
        gd = gp * dres
        dout = (r * gd - x * (r * r * r) * jnp.mean(x * gd, axis=-1, keepdims=True)).astype(BF16)
        dout_ref[...] = dout
        dmg = lax.dot_general(dout, wo_ref[...], (((1,), (1,)), ((), ())), preferred_element_type=F32)
        sga = jax.nn.sigmoid(ga_ref[...])
        sgs = jax.nn.sigmoid(gs_ref[...])
        dya = (dmg * sga).astype(BF16)
        dys = (dmg * sgs).astype(BF16)
        dya_ref[...] = dya
        dys_ref[...] = dys
        dga_ref[...] = (dmg * ya_ref[...].astype(F32) * sga * (1.0 - sga)).astype(BF16)
        dgs_ref[...] = (dmg * ys_ref[...].astype(F32) * sgs * (1.0 - sgs)).astype(BF16)
        da = lax.dot_general(dya, wa_ref[...], (((1,), (1,)), ((), ())), preferred_element_type=F32)
        za = za_ref[...]
        do_ref[...] = (da * _silu(za)).astype(BF16)
        dza_ref[...] = (da * o_ref[...] * _dsilu(za)).astype(BF16)
        dsn_ref[...] = lax.dot_general(dys, ws_ref[...], (((1,), (1,)), ((), ())),
                                       preferred_element_type=F32).astype(BF16)

    row = pl.BlockSpec((POST_R, D_MODEL), lambda i: (i, 0))
    pcol = lambda c0: pl.BlockSpec((POST_R, D_MODEL), lambda i: (i, c0 // D_MODEL))
    full = lambda r: pl.BlockSpec((r, D_MODEL), lambda i: (0, 0))
    small = pl.BlockSpec((8, D_MODEL), lambda i: (0, 0))
    return pl.pallas_call(
        body, grid=(T // POST_R,),
        in_specs=[row, row, row, pcol(C_ZA), pcol(C_GA), pcol(C_GS), row, row, row,
                  pl.BlockSpec((1, D_MODEL), lambda i: (0, 0)), full(D_MODEL), full(SSM_INNER), full(D_MODEL)],
        out_specs=[pl.BlockSpec((8, 128), lambda i: (0, 0)), row, row, row, row, row, row, row, row,
                   pl.BlockSpec((POST_R, SSM_INNER), lambda i: (i, 0)), small],
        out_shape=[SDS((8, 128), F32), SDS((T, D_MODEL), F32), SDS((T, D_MODEL), BF16), SDS((T, D_MODEL), BF16),
                   SDS((T, D_MODEL), BF16), SDS((T, D_MODEL), BF16), SDS((T, D_MODEL), BF16), SDS((T, D_MODEL), BF16),
                   SDS((T, D_MODEL), BF16), SDS((T, SSM_INNER), BF16), SDS((8, D_MODEL), F32)],
        compiler_params=_cparams(), name="post_b")(out, h, tgt, proj, proj, proj, ya, ys, o, g_post, w_att, w_ssm, w_o)


def _assemble(dq, dza, dga, dgs, dzs, dxx, dxb, dxc, dk, dv, ddt4):
    def body(dq_ref, dza_ref, dga_ref, dgs_ref, dzs_ref, dxx_ref, dxb_ref, dxc_ref, dk_ref, dv_ref, ddt_ref, o_ref):
        o_ref[:, C_Q:C_Q + D_MODEL] = dq_ref[...].astype(BF16)
        o_ref[:, C_ZA:C_ZA + D_MODEL] = dza_ref[...]
        o_ref[:, C_GA:C_GA + D_MODEL] = dga_ref[...]
        o_ref[:, C_GS:C_GS + D_MODEL] = dgs_ref[...]
        o_ref[:, C_ZS:C_ZS + SSM_INNER] = dzs_ref[...]
        o_ref[:, C_XBC:C_XBC + SSM_INNER] = dxx_ref[...]
        o_ref[:, C_XBC + SSM_INNER:C_XBC + SSM_INNER + GRP_W] = dxb_ref[...]
        o_ref[:, C_XBC + SSM_INNER + GRP_W:C_XBC + CONV_DIM] = dxc_ref[...]
        o_ref[:, C_K:C_K + KV_W] = dk_ref[...].astype(BF16)
        o_ref[:, C_V:C_V + KV_W] = dv_ref[...].astype(BF16)
        d4 = ddt_ref[...]
        o_ref[:, C_DT:C_DT + 128] = (d4[:, 0:128] + d4[:, 128:256] + d4[:, 256:384] + d4[:, 384:512]).astype(BF16)

    spec = lambda w: pl.BlockSpec((BLK, w), lambda i: (i, 0))
    ins = [dq, dza, dga, dgs, dzs, dxx, dxb, dxc, dk, dv, ddt4]
    return pl.pallas_call(
        body, grid=(NB,), in_specs=[spec(a.shape[1]) for a in ins], out_specs=spec(PW),
        out_shape=SDS((T, PW), BF16), name="assemble")(*ins)


def _adamw_math(w, g, m, v):
    m = ADAM_B1 * m + (1.0 - ADAM_B1) * g
    v = ADAM_B2 * v + (1.0 - ADAM_B2) * (g * g)
    m_hat = m / (1.0 - ADAM_B1 ** ADAM_STEP)
    v_hat = v / (1.0 - ADAM_B2 ** ADAM_STEP)
    delta = -ADAM_LR * (m_hat / (jnp.sqrt(v_hat) + ADAM_EPS) + ADAM_WD * w)
    return delta, m, v


def _sum_adamw(recv, w, m, v, tc, name):
    rows, cols = w.shape
    nslab = recv.shape[0]
    assert cols % tc == 0

    def body(r_ref, w_ref, m_ref, v_ref, g_ref, d_ref, nm_ref, nv_ref):
        g = r_ref[0].astype(F32)
        for d in range(1, nslab):
            g = g + r_ref[d].astype(F32)
        g_ref[...] = g
        delta, nm, nv = _adamw_math(w_ref[...], g, m_ref[...], v_ref[...])
        d_ref[...] = delta
        nm_ref[...] = nm
        nv_ref[...] = nv

    blk = pl.BlockSpec((rows, tc), lambda i: (0, i))
    return pl.pallas_call(
        body, grid=(cols // tc,),
        in_specs=[pl.BlockSpec((nslab, rows, tc), lambda i: (0, 0, i)), blk, blk, blk],
        out_specs=[blk, blk, blk, blk], out_shape=[SDS((rows, cols), F32)] * 4,
        compiler_params=_cparams(), name=name)(recv, w, m, v)


def _sum_adamw_rows3(recv, w3, m3, v3, name, exchange=()):
    pairs = 61
    assert (SHARD_IN // 2) % pairs == 0
    nsteps = SHARD_IN // 2 // pairs
    ne = len(exchange)

    def body(*refs):
        r_ref, w_ref, m_ref, v_ref = refs[:4]
        g_ref, d_ref, nm_ref, nv_ref = refs[4 + ne:8 + ne]
        if ne:
            ex_start, ex_finish = _direct_program(refs[4:4 + ne], refs[8 + ne:8 + 2 * ne], refs[8 + 2 * ne:])
            pl.when(pl.program_id(0) == 0)(ex_start)
        g = r_ref[0].astype(F32)
        for d in range(1, N_CHIP):
            g = g + r_ref[d].astype(F32)
        g = g.reshape(2 * pairs, ROW_TILES, 128)
        g_ref[...] = g
        delta, nm, nv = _adamw_math(w_ref[...], g, m_ref[...], v_ref[...])
        d_ref[...] = delta
        nm_ref[...] = nm
        nv_ref[...] = nv
        if ne:
            pl.when(pl.program_id(0) == nsteps - 1)(ex_finish)

    blk = pl.BlockSpec((2 * pairs, ROW_TILES, 128), lambda i: (i, 0, 0))
    return pl.pallas_call(
        body, grid=(nsteps,),
        in_specs=[pl.BlockSpec((N_CHIP, pairs, 2 * ROW_TILES, 128), lambda i: (0, i, 0, 0)), blk, blk, blk]
        + [ANY] * ne,
        out_specs=[blk, blk, blk, blk] + [ANY] * ne,
        out_shape=[SDS(w3.shape, F32)] * 4 + [SDS(p.shape, p.dtype) for p in exchange],
        scratch_shapes=_direct_scratch(exchange) if ne else [],
        compiler_params=_cparams(), name=name)(recv, w3, m3, v3, *exchange)


ROW_GPRE, ROW_CONVB, ROW_DTB, ROW_ALOG, ROW_DSKIP, ROW_SINK, ROW_GSSM, ROW_GPOST = 0, 1, 4, 5, 6, 7, 8, 10
REP_ROWS, ROW_CONVW, ROW_META, SM_ROWS = 16, 16, 24, 40
CW_SHARD = CONV_DIM // N_DEV
META_SHARD = D_MODEL // N_DEV


def _small_pack(dgpre, dbx, dbb, dbc, ddtb, dal, ddsk, dsink, dgn, dgp, dwx, dwb, dwc, dh):
    def body(dgpre_ref, dbx_ref, dbb_ref, dbc_ref, ddtb_ref, dal_ref, ddsk_ref, dsink_ref, dgn_ref, dgp_ref,
             dwx_ref, dwb_ref, dwc_ref, dh_ref, o_ref, rep):
        rep[...] = jnp.zeros_like(rep)
        rep[ROW_GPRE:ROW_GPRE + 1, :] = dgpre_ref[0:1, :]
        rep[ROW_CONVB:ROW_CONVB + 1, :] = dbx_ref[0:1, 0:1024]
        rep[ROW_CONVB + 1:ROW_CONVB + 2, :] = dbx_ref[0:1, 1024:2048]
        rep[ROW_CONVB + 2:ROW_CONVB + 3, 0:512] = dbb_ref[0:1, :]
        rep[ROW_CONVB + 2:ROW_CONVB + 3, 512:1024] = dbc_ref[0:1, :]
        rep[ROW_DTB:ROW_DTB + 1, 0:128] = ddtb_ref[0:1, :]
        rep[ROW_ALOG:ROW_ALOG + 1, 0:128] = dal_ref[0:1, :]
        rep[ROW_DSKIP:ROW_DSKIP + 1, 0:128] = ddsk_ref[0:1, :]
        rep[ROW_SINK:ROW_SINK + 1, 0:128] = dsink_ref[0:1, :]
        rep[ROW_GSSM:ROW_GSSM + 1, :] = dgn_ref[0:1, 0:1024]
        rep[ROW_GSSM + 1:ROW_GSSM + 2, :] = dgn_ref[0:1, 1024:2048]
        rep[ROW_GPOST:ROW_GPOST + 1, :] = dgp_ref[0:1, :]
        cw = jnp.concatenate([dwx_ref[...], dwb_ref[...], dwc_ref[...]], axis=1)
        mh = dh_ref[...]
        o_ref[...] = jnp.zeros_like(o_ref)
        for p in range(N_DEV):
            o_ref[p, 0:REP_ROWS, :] = rep[...]
            o_ref[p, ROW_CONVW:ROW_CONVW + 8, 0:CW_SHARD] = cw[:, p * CW_SHARD:(p + 1) * CW_SHARD]
            o_ref[p, ROW_META:ROW_META + N_META, 0:META_SHARD] = mh[:, p * META_SHARD:(p + 1) * META_SHARD]

    ins = [dgpre, dbx, dbb, dbc, ddtb, dal, ddsk, dsink, dgn, dgp, dwx, dwb, dwc]
    return pl.pallas_call(
        body, grid=(1,),
        in_specs=[pl.BlockSpec(a.shape, lambda i: (0, 0)) for a in ins]
        + [pl.BlockSpec((N_META, D_MODEL), lambda i: (PAD // N_META, 0))],
        out_specs=pl.BlockSpec((N_DEV, SM_ROWS, 1024), lambda i: (0, 0, 0)),
        out_shape=SDS((N_DEV, SM_ROWS, 1024), F32), scratch_shapes=[pltpu.VMEM((REP_ROWS, 1024), F32)],
        name="small_pack")(*ins, dh)


def _small_finish(recv, params):
    npar = len(params)

    def body(*refs):
        r_ref = refs[0]
        wmv = refs[1:1 + 3 * npar]
        outs = refs[1 + 3 * npar:1 + 7 * npar]
        gs = refs[-1]
        g = r_ref[0]
        for d in range(1, recv.shape[0]):
            g = g + r_ref[d]
        gs[...] = g
        grads = [
            gs[ROW_GPRE:ROW_GPRE + 1, :],
            jnp.concatenate([gs[ROW_CONVB + k:ROW_CONVB + k + 1, :] for k in range(3)], axis=1),
            gs[ROW_DTB:ROW_DTB + 1, 0:SSM_HEADS], gs[ROW_ALOG:ROW_ALOG + 1, 0:SSM_HEADS],
            gs[ROW_DSKIP:ROW_DSKIP + 1, 0:SSM_HEADS], gs[ROW_SINK:ROW_SINK + 1, 0:Q_HEADS],
            jnp.concatenate([gs[ROW_GSSM:ROW_GSSM + 1, :], gs[ROW_GSSM + 1:ROW_GSSM + 2, :]], axis=1),
            gs[ROW_GPOST:ROW_GPOST + 1, :],
            gs[ROW_CONVW:ROW_CONVW + 4, 0:CW_SHARD],
            gs[ROW_META:ROW_META + N_META, 0:META_SHARD]]
        for i in range(npar):
            w_ref, m_ref, v_ref = wmv[3 * i:3 * i + 3]
            delta, nm, nv = _adamw_math(w_ref[...], grads[i], m_ref[...], v_ref[...])
            outs[4 * i][...] = grads[i]
            outs[4 * i + 1][...] = delta
            outs[4 * i + 2][...] = nm
            outs[4 * i + 3][...] = nv

    flat = [a for wmv in params for a in wmv]
    res = pl.pallas_call(
        body, out_shape=[SDS(wmv[0].shape, F32) for wmv in params for _ in range(4)],
        scratch_shapes=[pltpu.VMEM((SM_ROWS, 1024), F32)], name="small_finish")(recv, *flat)
    return [tuple(res[4 * i:4 * i + 4]) for i in range(npar)]


def _slab(ref, px, py, pc):
    return ref.at[4 * px + 2 * py + pc]


def _bounce(src, dst, buf, sem):
    cp = pltpu.make_async_copy(src, buf, sem)
    cp.start()
    cp.wait()
    cp = pltpu.make_async_copy(buf, dst, sem)
    cp.start()
    cp.wait()


def _ag_program(ins, outs, scratch):
    na = len(ins)
    send_sems, recv_sems, local_sems = scratch[:3]
    bufs = scratch[3:]
    x, y, c = lax.axis_index("x"), lax.axis_index("y"), lax.axis_index("c")
    me, sibling = (x, y, c), (x, y, 1 - c)
    chips = [(1 - x, y), (x, 1 - y), (1 - x, 1 - y)]

    def copy(a, k, block, to, src=None):
        dst = _slab(outs[a], *block)
        return pltpu.make_async_remote_copy(
            src_ref=dst if src is None else src, dst_ref=dst, send_sem=send_sems.at[a, k],
            recv_sem=recv_sems.at[a, k], device_id=to, device_id_type=MESH)

    def own_sends():
        out = []
        for a in range(na):
            out.append(copy(a, 0, me, sibling, src=ins[a]))
            out += [copy(a, 1 + j, me, (*chip, c), src=ins[a]) for j, chip in enumerate(chips)]
        return out

    def start():
        for cp in own_sends():
            cp.start()
        for a in range(na):
            _bounce(ins[a], _slab(outs[a], *me), bufs[a], local_sems.at[a])

    def forward():
        for j, chip in enumerate(chips):
            for a in range(na):
                copy(a, 1 + j, (*chip, c), me).wait_recv()
                copy(a, 4 + j, (*chip, c), sibling).start()

    def finish():
        for a in range(na):
            copy(a, 0, sibling, me).wait_recv()
            for j, chip in enumerate(chips):
                copy(a, 4 + j, (*chip, 1 - c), me).wait_recv()
        for cp in own_sends():
            cp.wait_send()
        for j, chip in enumerate(chips):
            for a in range(na):
                copy(a, 4 + j, (*chip, c), sibling).wait_send()

    return start, forward, finish


def _ag_scratch(shards):
    na = len(shards)
    return [pltpu.SemaphoreType.DMA((na, 7)), pltpu.SemaphoreType.DMA((na, 7)),
            pltpu.SemaphoreType.DMA((na,))] + [pltpu.VMEM(s.shape, s.dtype) for s in shards]


def _all_gather(shards):
    na = len(shards)

    def body(*refs):
        start, forward, finish = _ag_program(refs[:na], refs[na:2 * na], refs[2 * na:])
        start()
        forward()
        finish()

    return pl.pallas_call(
        body, in_specs=[ANY] * na, out_specs=[ANY] * na,
        out_shape=[SDS((N_DEV,) + s.shape, s.dtype) for s in shards],
        scratch_shapes=_ag_scratch(shards), name="all_gather")(*shards)


N_CHIP = 4


def _pair_sum(own, got, name):
    na = len(own)

    def body(*refs):
        for a in range(na):
            o_ref, g_ref, s_ref = refs[a], refs[na + a], refs[2 * na + a]
            s_ref[...] = (o_ref[...].astype(F32) + g_ref[...].astype(F32)).astype(s_ref.dtype)

    def spec(p):
        nd = len(p.shape) - 1
        return pl.BlockSpec((1,) + p.shape[1:], lambda k, nd=nd: (k,) + (0,) * nd)

    return pl.pallas_call(
        body, grid=(N_CHIP,), in_specs=[spec(p) for p in own] + [spec(p) for p in got],
        out_specs=[spec(p) for p in own], out_shape=[SDS(p.shape, p.dtype) for p in own],
        compiler_params=_cparams(), name=name)(*own, *got)


def _chips_program(ins, outs, scratch):
    na = len(ins)
    send_sems, recv_sems, local_sems = scratch[:3]
    bufs = scratch[3:]
    x, y, c = lax.axis_index("x"), lax.axis_index("y"), lax.axis_index("c")
    mine = 2 * x + y
    chips = [(1 - x, y), (x, 1 - y), (1 - x, 1 - y)]

    def send(a, j):
        px, py = chips[j]
        return pltpu.make_async_remote_copy(
            src_ref=ins[a].at[2 * px + py], dst_ref=outs[a].at[mine], send_sem=send_sems.at[a, j],
            recv_sem=recv_sems.at[a, j], device_id=(px, py, c), device_id_type=MESH)

    def arrival(a, j):
        px, py = chips[j]
        return pltpu.make_async_remote_copy(
            src_ref=ins[a].at[2 * px + py], dst_ref=outs[a].at[2 * px + py], send_sem=send_sems.at[a, j],
            recv_sem=recv_sems.at[a, j], device_id=(px, py, c), device_id_type=MESH)

    def start():
        for a in range(na):
            for j in range(3):
                send(a, j).start()
        for a in range(na):
            _bounce(ins[a].at[mine], outs[a].at[mine], bufs[a], local_sems.at[a])

    def finish():
        for a in range(na):
            for j in range(3):
                arrival(a, j).wait_recv()
        for a in range(na):
            for j in range(3):
                send(a, j).wait_send()

    return start, finish


def _chips_scratch(parts):
    na = len(parts)
    return [pltpu.SemaphoreType.DMA((na, 3)), pltpu.SemaphoreType.DMA((na, 3)),
            pltpu.SemaphoreType.DMA((na,))] + [pltpu.VMEM(p.shape[1:], p.dtype) for p in parts]


def _direct_program(ins, outs, scratch):
    na = len(ins)
    send_sems, recv_sems, local_sems = scratch[:3]
    bufs = scratch[3:]
    x, y, c = lax.axis_index("x"), lax.axis_index("y"), lax.axis_index("c")
    me = (x, y, c)
    peers = []
    for k in range(1, N_DEV):
        dx, dy, dc = (k >> 2) & 1, (k >> 1) & 1, k & 1
        peers.append(((1 - x) if dx else x, (1 - y) if dy else y, (1 - c) if dc else c))

    def send(a, k):
        return pltpu.make_async_remote_copy(
            src_ref=_slab(ins[a], *peers[k]), dst_ref=_slab(outs[a], *me), send_sem=send_sems.at[a, k],
            recv_sem=recv_sems.at[a, k], device_id=peers[k], device_id_type=MESH)

    def arrival(a, k):
        return pltpu.make_async_remote_copy(
            src_ref=_slab(ins[a], *peers[k]), dst_ref=_slab(outs[a], *peers[k]), send_sem=send_sems.at[a, k],
            recv_sem=recv_sems.at[a, k], device_id=peers[k], device_id_type=MESH)

    def start():
        for a in range(na):
            for k in range(N_DEV - 1):
                send(a, k).start()
        for a in range(na):
            _bounce(_slab(ins[a], *me), _slab(outs[a], *me), bufs[a], local_sems.at[a])

    def finish():
        for a in range(na):
            for k in range(N_DEV - 1):
                arrival(a, k).wait_recv()
        for a in range(na):
            for k in range(N_DEV - 1):
                send(a, k).wait_send()

    return start, finish


def _direct_scratch(parts):
    na = len(parts)
    return [pltpu.SemaphoreType.DMA((na, N_DEV - 1)), pltpu.SemaphoreType.DMA((na, N_DEV - 1)),
            pltpu.SemaphoreType.DMA((na,))] + [pltpu.VMEM(p.shape[1:], p.dtype) for p in parts]


ROW_TILES = D_MODEL // 128


def _rows3(t):
    return jnp.transpose(t[0]).reshape(t.shape[2], ROW_TILES, 128)


def _unrows3(t):
    return jnp.transpose(t.reshape(t.shape[0], D_MODEL))[None]


def _cast_shards(w_in3, w_att, w_ssm, w_o):
    def body(wi_ref, wa_ref, ws_ref, wo_ref, a_ref, b_ref, c_ref, d_ref):
        a_ref[...] = wi_ref[...].reshape(SHARD_IN // 2, 2 * ROW_TILES, 128).astype(BF16)
        b_ref[...] = wa_ref[...].astype(BF16)
        c_ref[...] = ws_ref[...].astype(BF16)
        d_ref[...] = wo_ref[...].astype(BF16)

    return pl.pallas_call(
        body, out_shape=[SDS((SHARD_IN // 2, 2 * ROW_TILES, 128), BF16), SDS(w_att.shape, BF16),
                         SDS(w_ssm.shape, BF16), SDS(w_o.shape, BF16)],
        compiler_params=_cparams(), name="cast_shards")(w_in3, w_att, w_ssm, w_o)


def _pieces():
    out = []
    for r0, c0, w in _SEGS:
        r = r0
        while r < r0 + w:
            d = r // SHARD_IN
            n = min(r0 + w, (d + 1) * SHARD_IN) - r
            out.append((c0 + (r - r0), d, r - d * SHARD_IN, n))
            r += n
    return out


def _to_aligned_t(slabs):
    def body(a_ref, o_ref):
        for (t, d, s, n) in _pieces():
            o_ref[t:t + n, :] = a_ref[d, s // 2:(s + n) // 2].reshape(n, D_MODEL)
        o_ref[C_DT + 32:C_DT + 128, :] = jnp.zeros((96, D_MODEL), slabs.dtype)

    return pl.pallas_call(body, out_shape=SDS((PW, D_MODEL), slabs.dtype), compiler_params=_cparams(),
                          name="to_aligned")(slabs)


def _from_aligned_pair(g):
    slab = (SHARD_IN // 2, 2 * ROW_TILES, 128)
    by_slab = [[p for p in _pieces() if p[1] == d] for d in range(N_DEV)]

    def body(g_ref, own_ref, got_ref, slabs, send_sems, recv_sems, local_sems):
        x, y, c = lax.axis_index("x"), lax.axis_index("y"), lax.axis_index("c")
        sibling = (x, y, 1 - c)

        def to_own(d, k):
            return pltpu.make_async_copy(slabs.at[d], own_ref.at[k], local_sems.at[k])

        def to_sibling(d, k):
            return pltpu.make_async_remote_copy(
                src_ref=slabs.at[d], dst_ref=got_ref.at[k], send_sem=send_sems.at[k], recv_sem=recv_sems.at[k],
                device_id=sibling, device_id_type=MESH)

        for d in range(N_DEV):
            for (t, _, s, n) in by_slab[d]:
                slabs[d, s // 2:(s + n) // 2] = g_ref[t:t + n, :].reshape(n // 2, 2 * ROW_TILES, 128)
            k, side = d // 2, d % 2
            pl.when(c == side)(to_own(d, k).start)
            pl.when(c != side)(to_sibling(d, k).start)
        for k in range(N_CHIP):
            to_own(0, k).wait()
            to_sibling(0, k).wait()

    half = SDS((N_CHIP,) + slab, g.dtype)
    return pl.pallas_call(
        body, in_specs=[pl.BlockSpec(memory_space=pltpu.VMEM)], out_specs=[ANY, ANY], out_shape=[half, half],
        scratch_shapes=[pltpu.VMEM((N_DEV,) + slab, g.dtype), pltpu.SemaphoreType.DMA((N_CHIP,)),
                        pltpu.SemaphoreType.DMA((N_CHIP,)), pltpu.SemaphoreType.DMA((N_CHIP,))],
        compiler_params=_cparams(), name="from_aligned_pair")(g)


_SEGS = [
    (R_Q, C_Q, 1024), (R_K, C_K, 256), (R_V, C_V, 256), (R_ZA, C_ZA, 1024), (R_ZS, C_ZS, 2048),
    (R_XBC, C_XBC, 3072), (R_DT, C_DT, 32), (R_GA, C_GA, 1024), (R_GS, C_GS, 1024)]


def _pad_lanes(v, n=128):
    return jnp.pad(v, ((0, 0), (0, n - v.shape[1])))


def _device_step(h, tgt, w_alt, w_out, g_pre, conv_w8, conv_b, dt_bias, a_log, d_skip, sinks, g_ssm, g_post, on_mesh):
    dtb, al, dsk, snk = _pad_lanes(dt_bias), _pad_lanes(a_log), _pad_lanes(d_skip), _pad_lanes(sinks)
    u = _norm_u(h, g_pre)
    proj = _matmul(u, w_alt, "nt", F32, T, 896, "in_proj")
    o = _attn_fwd(proj, snk)
    xbc_act = _conv_fwd(proj, conv_w8, conv_b)
    if on_mesh:
        sn, states, att_all, ssm_all, o_all = _ssd_fwd(xbc_act, proj, dtb, al, dsk, g_ssm, gather=w_out)
        w_att = att_all.reshape(D_MODEL, D_MODEL)
        w_ssm = ssm_all.reshape(SSM_INNER, D_MODEL)
        w_o = o_all.reshape(D_MODEL, D_MODEL)
    else:
        sn, states = _ssd_fwd(xbc_act, proj, dtb, al, dsk, g_ssm)
        w_att, w_ssm, w_o = w_out
    a_in, mg, ya, ys, out = _post_a(o, proj, sn, w_att, w_ssm, w_o)
    (loss, dres, dout, dya, dys, dga, dgs, do, dza, dsn, dgp) = _post_b(
        out, h, tgt, proj, ya, ys, o, g_post, w_att, w_ssm, w_o)
    dw_att = _matmul(a_in, dya, "tn", BF16, D_MODEL, D_MODEL, "d_w_att")
    dw_ssm = _matmul(sn, dys, "tn", BF16, D_MODEL, D_MODEL, "d_w_ssm")
    dw_o = _matmul(mg, dout, "tn", BF16, D_MODEL, D_MODEL, "d_w_o")
    res = {}
    if on_mesh:
        parts = [dw_att.reshape(N_DEV, 128, D_MODEL), dw_ssm.reshape(N_DEV, 256, D_MODEL),
                 dw_o.reshape(N_DEV, 128, D_MODEL)]
        (dxs, dbm, dcm, ddt4, dzs, ddtb, dal, ddsk, dgn, res["r_att"], res["r_ssm"], res["r_o"]) = _ssd_bwd(
            xbc_act, proj, dtb, al, dsk, g_ssm, states, dsn, exchange=parts)
    else:
        dxs, dbm, dcm, ddt4, dzs, ddtb, dal, ddsk, dgn = _ssd_bwd(xbc_act, proj, dtb, al, dsk, g_ssm, states, dsn)
        res.update(dw_att=dw_att, dw_ssm=dw_ssm, dw_o=dw_o)
    dxx, dwx, dbx = _conv_bwd(proj, conv_w8, conv_b, dxs, 0, "conv_bwd_x")
    dxb, dwb, dbb = _conv_bwd(proj, conv_w8, conv_b, dbm, SSM_INNER, "conv_bwd_b")
    dxc, dwc, dbc = _conv_bwd(proj, conv_w8, conv_b, dcm, SSM_INNER + GRP_W, "conv_bwd_c")
    dq, dk, dv, dsink = _attn_bwd(proj, snk, do)
    dproj = _assemble(dq, dza, dga, dgs, dzs, dxx, dxb, dxc, dk, dv, ddt4)
    dw_alt = _matmul(dproj, u, "tn", BF16, 896, D_MODEL, "d_w_in")
    if on_mesh:
        own, got = _from_aligned_pair(dw_alt)
        dh, dgpre, res["r_in"] = _d_u_norm(dproj, w_alt, h, g_pre, dres,
                                           chips=_pair_sum([own], [got], "pair_sum_w_in"))
    else:
        dh, dgpre = _d_u_norm(dproj, w_alt, h, g_pre, dres)
        res["dw_alt"] = dw_alt
    small = (dgpre, dbx, dbb, dbc, ddtb, dal, ddsk, dsink, dgn, dgp, dwx, dwb, dwc)
    if on_mesh:
        res["small_pack"] = _small_pack(*small, dh)
    else:
        res["small"] = small
    res.update(loss=loss[0, 0], dh=dh)
    return res


def kernel(x, meta_tokens, g_pre, w_in, conv_w, conv_b, dt_bias, a_log, d_skip, attn_sinks, g_ssm_norm, w_out_att, w_out_ssm, w_out, g_post, loss_target, m_meta_tokens, m_g_pre, m_w_in, m_conv_w, m_conv_b, m_dt_bias, m_a_log, m_d_skip, m_attn_sinks, m_g_ssm_norm, m_w_out_att, m_w_out_ssm, m_w_out, m_g_post, v_meta_tokens, v_g_pre, v_w_in, v_conv_w, v_conv_b, v_dt_bias, v_a_log, v_d_skip, v_attn_sinks, v_g_ssm_norm, v_w_out_att, v_w_out_ssm, v_w_out, v_g_post):
    w_in3, m_in3, v_in3 = _rows3(w_in), _rows3(m_w_in), _rows3(v_w_in)
    a_sh, att_sh, ssm_sh, o_sh = _cast_shards(w_in3, w_out_att[0], w_out_ssm[0], w_out[0])
    cw_sh = jnp.pad(conv_w[0], ((0, 4), (0, 0)))
    a_all, meta_all, cw_all = _all_gather([a_sh, meta_tokens, cw_sh])
    w_alt = _to_aligned_t(a_all)
    meta_full = meta_all.transpose(1, 0, 2).reshape(N_META, D_MODEL)
    conv_w8 = cw_all.transpose(1, 0, 2).reshape(8, CONV_DIM)

    h = jnp.concatenate([jnp.zeros((PAD, D_MODEL), F32), meta_full, x[0]], axis=0)
    tgt = jnp.concatenate([jnp.zeros((PAD + N_META, D_MODEL), F32), loss_target[0]], axis=0)
    r = _device_step(h, tgt, w_alt, (att_sh, ssm_sh, o_sh), g_pre, conv_w8, conv_b, dt_bias, a_log, d_skip,
                     attn_sinks, g_ssm_norm, g_post, True)
    loss = lax.psum(r["loss"], ("x", "y", "c"))
    grad_x = r["dh"][PAD + N_META:][None]

    *res_in, r_small = _sum_adamw_rows3(r["r_in"], w_in3, m_in3, v_in3, "adamw_w_in", exchange=[r["small_pack"]])
    res_in = [_unrows3(t) for t in res_in]
    res_att = [t[None] for t in _sum_adamw(r["r_att"], w_out_att[0], m_w_out_att[0], v_w_out_att[0], 512,
                                           "adamw_w_att")]
    res_ssm = [t[None] for t in _sum_adamw(r["r_ssm"], w_out_ssm[0], m_w_out_ssm[0], v_w_out_ssm[0], 512,
                                           "adamw_w_ssm")]
    res_o = [t[None] for t in _sum_adamw(r["r_o"], w_out[0], m_w_out[0], v_w_out[0], 512, "adamw_w_o")]
    (res_gpre, res_convb, res_dtb, res_alog, res_dskip, res_sink, res_gssm, res_gpost, res_cw, res_meta) = _small_finish(
        r_small, [(g_pre, m_g_pre, v_g_pre), (conv_b, m_conv_b, v_conv_b), (dt_bias, m_dt_bias, v_dt_bias),
                       (a_log, m_a_log, v_a_log), (d_skip, m_d_skip, v_d_skip),
                       (attn_sinks, m_attn_sinks, v_attn_sinks), (g_ssm_norm, m_g_ssm_norm, v_g_ssm_norm),
                       (g_post, m_g_post, v_g_post), (conv_w[0], m_conv_w[0], v_conv_w[0]),
                       (meta_tokens, m_meta_tokens, v_meta_tokens)])
    res_cw = [t[None] for t in res_cw]
    per_weight = [res_meta, res_gpre, res_in, res_cw, res_convb, res_dtb, res_alog, res_dskip, res_sink, res_gssm,
                  res_att, res_ssm, res_o, res_gpost]
    return (loss, grad_x, *[p[0] for p in per_weight], *[p[1] for p in per_weight], *[p[2] for p in per_weight],
            *[p[3] for p in per_weight])
```

```python
import functools
import math

import jax
import jax.numpy as jnp
from jax import lax
from jax.experimental import pallas as pl
from jax.experimental.pallas import tpu as pltpu

F32 = jnp.float32
BF16 = jnp.bfloat16
SDS = jax.ShapeDtypeStruct
MESH = pl.DeviceIdType.MESH
ANY = pl.BlockSpec(memory_space=pl.ANY)

N_DEV = 8
D_MODEL = 1024
SEQ = 2048
N_META = 16
BLK = 128
PAD = 112
T = PAD + N_META + SEQ
NB = T // BLK
EPS = 1e-6
HEAD = 64
Q_HEADS = 16
KV_HEADS = 4
GROUP = 4
KV_W = 256
SSM_INNER = 2048
SSM_HEADS = 32
SSM_GROUPS = 4
GRP_W = 512
SSM_STATE = 128
CONV_DIM = 3072
IN_PROJ = 9760
SHARD_IN = IN_PROJ // N_DEV
NEG = -1e30

C_Q, C_ZA, C_GA, C_GS, C_ZS, C_XBC, C_K, C_V, C_DT = 0, 1024, 2048, 3072, 4096, 6144, 9216, 9472, 9728
PW = 9856
R_Q, R_K, R_V, R_ZA, R_ZS, R_XBC, R_DT, R_GA, R_GS = 0, 1024, 1280, 1536, 2560, 4608, 7680, 7712, 8736

ADAM_LR, ADAM_B1, ADAM_B2, ADAM_EPS, ADAM_WD, ADAM_STEP = 0.001, 0.9, 0.999, 1e-08, 0.01, 10

VMEM_LIMIT = 56 * 1024 * 1024


def _cparams():
    return pltpu.CompilerParams(vmem_limit_bytes=VMEM_LIMIT)


def _silu(x):
    return x * jax.nn.sigmoid(x)


def _dsilu(x):
    s = jax.nn.sigmoid(x)
    return s * (1.0 + x * (1.0 - s))


def _matmul(a, b, mode, out_dtype, tm, tn, name):
    if mode == "nt":
        (m, k), n = a.shape, b.shape[0]
        a_spec = pl.BlockSpec((tm, k), lambda i, j: (i, 0))
        b_spec = pl.BlockSpec((tn, k), lambda i, j: (j, 0))
        dims = (((1,), (1,)), ((), ()))
    else:
        assert mode == "tn"
        (k, m), n = a.shape, b.shape[1]
        a_spec = pl.BlockSpec((k, tm), lambda i, j: (0, i))
        b_spec = pl.BlockSpec((k, tn), lambda i, j: (0, j))
        dims = (((0,), (0,)), ((), ()))
    assert m % tm == 0 and n % tn == 0, (a.shape, b.shape, tm, tn)

    def body(a_ref, b_ref, o_ref):
        o_ref[...] = lax.dot_general(a_ref[...], b_ref[...], dims, preferred_element_type=F32).astype(out_dtype)

    return pl.pallas_call(
        body, grid=(m // tm, n // tn), in_specs=[a_spec, b_spec],
        out_specs=pl.BlockSpec((tm, tn), lambda i, j: (i, j)), out_shape=SDS((m, n), out_dtype),
        compiler_params=_cparams(), name=name)(a, b)


def _norm_u(h, g_pre):
    def body(h_ref, g_ref, u_ref):
        x = h_ref[...]
        r = lax.rsqrt(jnp.mean(x * x, axis=-1, keepdims=True) + EPS)
        u_ref[...] = (x * r * g_ref[...]).astype(BF16)

    return pl.pallas_call(
        body, grid=(NB,),
        in_specs=[pl.BlockSpec((BLK, D_MODEL), lambda i: (i, 0)), pl.BlockSpec((1, D_MODEL), lambda i: (0, 0))],
        out_specs=pl.BlockSpec((BLK, D_MODEL), lambda i: (i, 0)),
        out_shape=SDS((T, D_MODEL), BF16), name="norm_u")(h, g_pre)


DU_TM, DU_TK = T // 2, 1408


def _d_u_norm(dproj, w_alt, h, g_pre, dres, chips=()):
    nk = PW // DU_TK
    ni = T // DU_TM
    nc = len(chips)

    def body(*refs):
        a_ref, b_ref, h_ref, g_ref, dres_ref = refs[:5]
        dh_ref, dg_ref = refs[5 + nc:7 + nc]
        acc_ref = refs[7 + 2 * nc]
        i, kk = pl.program_id(0), pl.program_id(1)
        if nc:
            ch_start, ch_finish = _chips_program(refs[5:5 + nc], refs[7 + nc:7 + 2 * nc], refs[8 + 2 * nc:])
            pl.when((i == 0) & (kk == 0))(ch_start)
        part = jnp.dot(a_ref[...], b_ref[...], preferred_element_type=F32)

        @pl.when(kk == 0)
        def _():
            acc_ref[...] = part

        @pl.when((kk > 0) & (kk < nk - 1))
        def _():
            acc_ref[...] += part

        @pl.when(kk == nk - 1)
        def _():
            du_ = acc_ref[...] + part
            x = h_ref[...]
            r = lax.rsqrt(jnp.mean(x * x, axis=-1, keepdims=True) + EPS)
            gd = g_ref[...] * du_
            dx = r * gd - x * (r * r * r) * jnp.mean(x * gd, axis=-1, keepdims=True)
            dh_ref[...] = dx + dres_ref[...]
            gpart = jnp.concatenate([jnp.sum(du_ * x * r, axis=0, keepdims=True), jnp.zeros((7, D_MODEL), F32)],
                                    axis=0)

            @pl.when(i == 0)
            def _():
                dg_ref[...] = gpart

            @pl.when(i > 0)
            def _():
                dg_ref[...] += gpart

        if nc:
            pl.when((i == ni - 1) & (kk == nk - 1))(ch_finish)

    row = pl.BlockSpec((DU_TM, D_MODEL), lambda i, kk: (i, 0))
    return pl.pallas_call(
        body, grid=(ni, nk),
        in_specs=[pl.BlockSpec((DU_TM, DU_TK), lambda i, kk: (i, kk)),
                  pl.BlockSpec((DU_TK, D_MODEL), lambda i, kk: (kk, 0)),
                  row, pl.BlockSpec((1, D_MODEL), lambda i, kk: (0, 0)), row] + [ANY] * nc,
        out_specs=[row, pl.BlockSpec((8, D_MODEL), lambda i, kk: (0, 0))] + [ANY] * nc,
        out_shape=[SDS((T, D_MODEL), F32), SDS((8, D_MODEL), F32)] + [SDS(p.shape, p.dtype) for p in chips],
        scratch_shapes=[pltpu.VMEM((DU_TM, D_MODEL), F32)] + (_chips_scratch(chips) if nc else []),
        compiler_params=_cparams(), name="d_u_norm")(dproj, w_alt, h, g_pre, dres, *chips)


def _lane_pick(row, h):
    lane = lax.broadcasted_iota(jnp.int32, row.shape, 1)
    return jnp.sum(jnp.where(lane == h, row, 0.0), axis=1, keepdims=True)


def _attn_fn(q4s, kcats, vcats, kms, vms, sinks, n):
    r = lax.broadcasted_iota(jnp.int32, (GROUP * BLK, 2 * BLK), 0)
    s = lax.broadcasted_iota(jnp.int32, (GROUP * BLK, 2 * BLK), 1)
    i = jnp.bitwise_and(r, BLK - 1)
    gi = jnp.right_shift(r, 7)
    rel = i - s + BLK
    k_pos = n * BLK - BLK + s
    band_ok = (rel >= 0) & (rel < BLK) & (k_pos >= PAD + N_META)
    relf = rel.astype(F32)
    rm = lax.broadcasted_iota(jnp.int32, (GROUP * BLK, N_META), 0)
    mm = lax.broadcasted_iota(jnp.int32, (GROUP * BLK, N_META), 1)
    meta_ok = (PAD + mm) <= (n * BLK + jnp.bitwise_and(rm, BLK - 1))
    gcol = jnp.right_shift(lax.broadcasted_iota(jnp.int32, (GROUP * BLK, 1), 0), 7)
    outs = []
    for kh in range(KV_HEADS):
        slopes = [2.0 ** (-8.0 * (kh * GROUP + g + 1) / Q_HEADS) for g in range(GROUP)]
        slope = jnp.where(gi == 0, slopes[0], jnp.where(gi == 1, slopes[1], jnp.where(gi == 2, slopes[2], slopes[3])))
        sk = [_lane_pick(sinks, kh * GROUP + g) for g in range(GROUP)]
        sink = jnp.where(gcol == 0, sk[0], jnp.where(gcol == 1, sk[1], jnp.where(gcol == 2, sk[2], sk[3])))
        qb = (q4s[kh] * (HEAD ** -0.5)).astype(BF16)
        sb = lax.dot_general(qb, kcats[kh].astype(BF16), (((1,), (1,)), ((), ())), preferred_element_type=F32)
        sb = jnp.where(band_ok, sb - slope * relf, NEG)
        sm = lax.dot_general(qb, kms[kh].astype(BF16), (((1,), (1,)), ((), ())), preferred_element_type=F32)
        sm = jnp.where(meta_ok, sm, NEG)
        mx = jnp.maximum(jnp.maximum(jnp.max(sb, axis=1, keepdims=True), jnp.max(sm, axis=1, keepdims=True)), sink)
        mx = lax.stop_gradient(mx)
        eb = jnp.exp(sb - mx)
        em = jnp.exp(sm - mx)
        es = jnp.exp(sink - mx)
        inv = 1.0 / (jnp.sum(eb, axis=1, keepdims=True) + jnp.sum(em, axis=1, keepdims=True) + es)
        pb = (eb * inv).astype(BF16)
        pm = (em * inv).astype(BF16)
        o4 = (jnp.dot(pm, vms[kh].astype(BF16), preferred_element_type=F32)
              + jnp.dot(pb, vcats[kh].astype(BF16), preferred_element_type=F32))
        outs.append(o4)
    return outs


def _attn_specs():
    prev = lambda n: jnp.maximum(n - 1, 0)
    return [
        pl.BlockSpec((BLK, D_MODEL), lambda n: (n, C_Q // D_MODEL)),
        pl.BlockSpec((BLK, KV_W), lambda n: (prev(n), C_K // KV_W)),
        pl.BlockSpec((BLK, KV_W), lambda n: (n, C_K // KV_W)),
        pl.BlockSpec((BLK, KV_W), lambda n: (prev(n), C_V // KV_W)),
        pl.BlockSpec((BLK, KV_W), lambda n: (n, C_V // KV_W)),
        pl.BlockSpec((N_META, KV_W), lambda n: (PAD // N_META, C_K // KV_W)),
        pl.BlockSpec((N_META, KV_W), lambda n: (PAD // N_META, C_V // KV_W)),
        pl.BlockSpec((1, 128), lambda n: (0, 0)),
    ]


def _attn_load(q_ref, kp_ref, kc_ref, vp_ref, vc_ref, km_ref, vm_ref):
    q4s, kcats, vcats, kms, vms = [], [], [], [], []
    for kh in range(KV_HEADS):
        q4s.append(jnp.concatenate(
            [q_ref[:, (kh * GROUP + g) * HEAD:(kh * GROUP + g + 1) * HEAD] for g in range(GROUP)], axis=0))
        cs = slice(kh * HEAD, (kh + 1) * HEAD)
        kcats.append(jnp.concatenate([kp_ref[:, cs], kc_ref[:, cs]], axis=0))
        vcats.append(jnp.concatenate([vp_ref[:, cs], vc_ref[:, cs]], axis=0))
        kms.append(km_ref[:, cs])
        vms.append(vm_ref[:, cs])
    return q4s, kcats, vcats, kms, vms


def _attn_fwd(proj, sinks):
    def body(q_ref, kp_ref, kc_ref, vp_ref, vc_ref, km_ref, vm_ref, s_ref, o_ref):
        n = pl.program_id(0)
        args = _attn_load(q_ref, kp_ref, kc_ref, vp_ref, vc_ref, km_ref, vm_ref)
        outs = _attn_fn(*args, s_ref[...], n)
        for kh in range(KV_HEADS):
            for g in range(GROUP):
                hh = kh * GROUP + g
                o_ref[:, hh * HEAD:(hh + 1) * HEAD] = outs[kh][g * BLK:(g + 1) * BLK]

    return pl.pallas_call(
        body, grid=(NB,), in_specs=_attn_specs(),
        out_specs=pl.BlockSpec((BLK, D_MODEL), lambda n: (n, 0)),
        out_shape=SDS((T, D_MODEL), F32), name="attn_fwd")(proj, proj, proj, proj, proj, proj, proj, sinks)


def _attn_bwd(proj, sinks, do):
    def body(q_ref, kp_ref, kc_ref, vp_ref, vc_ref, km_ref, vm_ref, s_ref, do_ref, dq_ref, dk_ref, dv_ref, ds_ref):
        n = pl.program_id(0)

        @pl.when(n == 0)
        def _():
            dk_ref[...] = jnp.zeros_like(dk_ref)
            dv_ref[...] = jnp.zeros_like(dv_ref)
            ds_ref[...] = jnp.zeros_like(ds_ref)

        args = _attn_load(q_ref, kp_ref, kc_ref, vp_ref, vc_ref, km_ref, vm_ref)
        _, vjp = jax.vjp(lambda a, b, c, d, e, f: _attn_fn(a, b, c, d, e, f, n), *args, s_ref[...])
        do_f = do_ref[...].astype(F32)
        cot = [jnp.concatenate([do_f[:, (kh * GROUP + g) * HEAD:(kh * GROUP + g + 1) * HEAD] for g in range(GROUP)],
                               axis=0) for kh in range(KV_HEADS)]
        dq4s, dkcats, dvcats, dkms, dvms, dsk = vjp(cot)
        ds_ref[0:1, :] += dsk
        cur = pl.ds(pl.multiple_of(n * BLK, BLK), BLK)
        meta = slice(PAD, PAD + N_META)
        for kh in range(KV_HEADS):
            cs = slice(kh * HEAD, (kh + 1) * HEAD)
            for g in range(GROUP):
                hh = kh * GROUP + g
                dq_ref[:, hh * HEAD:(hh + 1) * HEAD] = dq4s[kh][g * BLK:(g + 1) * BLK]
            dk_ref[cur, cs] += dkcats[kh][BLK:]
            dv_ref[cur, cs] += dvcats[kh][BLK:]
            dk_ref[meta, cs] += dkms[kh]
            dv_ref[meta, cs] += dvms[kh]

        @pl.when(n > 0)
        def _():
            prv = pl.ds(pl.multiple_of((n - 1) * BLK, BLK), BLK)
            for kh in range(KV_HEADS):
                cs = slice(kh * HEAD, (kh + 1) * HEAD)
                dk_ref[prv, cs] += dkcats[kh][:BLK]
                dv_ref[prv, cs] += dvcats[kh][:BLK]

    full_kv = pl.BlockSpec((T, KV_W), lambda n: (0, 0))
    return pl.pallas_call(
        body, grid=(NB,),
        in_specs=_attn_specs() + [pl.BlockSpec((BLK, D_MODEL), lambda n: (n, 0))],
        out_specs=[pl.BlockSpec((BLK, D_MODEL), lambda n: (n, 0)), full_kv, full_kv,
                   pl.BlockSpec((8, 128), lambda n: (0, 0))],
        out_shape=[SDS((T, D_MODEL), F32), SDS((T, KV_W), F32), SDS((T, KV_W), F32), SDS((8, 128), F32)],
        name="attn_bwd")(proj, proj, proj, proj, proj, proj, proj, sinks, do)


def _conv_taps(xp, w, rows):
    return (w[0:1] * xp[5:5 + rows] + w[1:2] * xp[6:6 + rows] + w[2:3] * xp[7:7 + rows] + w[3:4] * xp[8:8 + rows])


def _conv_fwd(proj, conv_w, conv_b):
    CONV_CB = CONV_DIM
    ncb = CONV_DIM // CONV_CB
    cb0 = C_XBC // CONV_CB

    def body(tail_ref, cur_ref, w_ref, b_ref, o_ref):
        n = pl.program_id(1)
        tail = jnp.where(n > 0, tail_ref[...], 0.0)
        xp = jnp.concatenate([tail, cur_ref[...]], axis=0)
        conv = _conv_taps(xp, w_ref[...], BLK) + b_ref[...]
        row = n * BLK + lax.broadcasted_iota(jnp.int32, (BLK, 1), 0)
        o_ref[...] = jnp.where(row >= PAD, _silu(conv), 0.0)

    return pl.pallas_call(
        body, grid=(ncb, NB),
        in_specs=[pl.BlockSpec((8, CONV_CB), lambda j, n: (jnp.maximum(n * (BLK // 8) - 1, 0), cb0 + j)),
                  pl.BlockSpec((BLK, CONV_CB), lambda j, n: (n, cb0 + j)),
                  pl.BlockSpec((8, CONV_CB), lambda j, n: (0, j)),
                  pl.BlockSpec((1, CONV_CB), lambda j, n: (0, j))],
        out_specs=pl.BlockSpec((BLK, CONV_CB), lambda j, n: (n, j)),
        out_shape=SDS((T, CONV_DIM), F32), name="conv_fwd")(proj, proj, conv_w, conv_b)


def _conv_bwd(proj, conv_w, conv_b, dact, ch0, name):
    width = dact.shape[1]
    CONV_CB = width
    ncb = width // CONV_CB
    cb0 = (C_XBC + ch0) // CONV_CB
    wb0 = ch0 // CONV_CB
    last8 = T // 8 - 1

    def body(tail_ref, cur_ref, nxt_ref, w_ref, b_ref, dcur_ref, dnxt_ref, dx_ref, dw_ref, db_ref):
        n = pl.program_id(1)
        w = w_ref[...]
        tail = jnp.where(n > 0, tail_ref[...], 0.0)
        xp = jnp.concatenate([tail, cur_ref[...], nxt_ref[...]], axis=0)
        conv = _conv_taps(xp, w, BLK + 8) + b_ref[...]
        dext = jnp.concatenate([dcur_ref[...], jnp.where(n < NB - 1, dnxt_ref[...], 0.0)], axis=0)
        row = n * BLK + lax.broadcasted_iota(jnp.int32, (BLK + 8, 1), 0)
        dconv = jnp.where(row >= PAD, dext * _dsilu(conv), 0.0)
        dx = (w[0:1] * dconv[3:3 + BLK] + w[1:2] * dconv[2:2 + BLK] + w[2:3] * dconv[1:1 + BLK]
              + w[3:4] * dconv[0:BLK])
        dx_ref[...] = dx.astype(BF16)
        dc = dconv[0:BLK]
        dws = [jnp.sum(dc * xp[5 + k:5 + k + BLK], axis=0, keepdims=True) for k in range(4)]
        dwp = jnp.concatenate(dws + [jnp.zeros((4, CONV_CB), F32)], axis=0)
        dbp = jnp.sum(dc, axis=0, keepdims=True)

        @pl.when(n == 0)
        def _():
            dw_ref[...] = dwp
            db_ref[...] = jnp.concatenate([dbp, jnp.zeros((7, CONV_CB), F32)], axis=0)

        @pl.when(n > 0)
        def _():
            dw_ref[...] += dwp
            db_ref[0:1, :] += dbp

    return pl.pallas_call(
        body, grid=(ncb, NB),
        in_specs=[pl.BlockSpec((8, CONV_CB), lambda j, n: (jnp.maximum(n * (BLK // 8) - 1, 0), cb0 + j)),
                  pl.BlockSpec((BLK, CONV_CB), lambda j, n: (n, cb0 + j)),
                  pl.BlockSpec((8, CONV_CB), lambda j, n: (jnp.minimum((n + 1) * (BLK // 8), last8), cb0 + j)),
                  pl.BlockSpec((8, CONV_CB), lambda j, n: (0, wb0 + j)),
                  pl.BlockSpec((1, CONV_CB), lambda j, n: (0, wb0 + j)),
                  pl.BlockSpec((BLK, CONV_CB), lambda j, n: (n, j)),
                  pl.BlockSpec((8, CONV_CB), lambda j, n: (jnp.minimum((n + 1) * (BLK // 8), last8), j))],
        out_specs=[pl.BlockSpec((BLK, CONV_CB), lambda j, n: (n, j)),
                   pl.BlockSpec((8, CONV_CB), lambda j, n: (0, j)),
                   pl.BlockSpec((8, CONV_CB), lambda j, n: (0, j))],
        out_shape=[SDS((T, width), BF16), SDS((8, width), F32), SDS((8, width), F32)],
        name=name)(proj, proj, proj, conv_w, conv_b, dact, dact)


HPG = SSM_HEADS // SSM_GROUPS


def _iota(shape, dim):
    return lax.broadcasted_iota(jnp.int32, shape, dim)


def _mm(a, b, ca=1, cb=0):
    return lax.dot_general(a.astype(BF16), b.astype(BF16), (((ca,), (cb,)), ((), ())), preferred_element_type=F32)


def _split3(v):
    hi = v.astype(BF16)
    r1 = v - hi.astype(F32)
    mid = r1.astype(BF16)
    lo = (r1 - mid.astype(F32)).astype(BF16)
    return hi, mid, lo


def _sel_r(parts, onehot, ca=1, cb=0):
    out = lax.dot_general(parts[0], onehot, (((ca,), (cb,)), ((), ())), preferred_element_type=F32)
    for p in parts[1:]:
        out = out + lax.dot_general(p, onehot, (((ca,), (cb,)), ((), ())), preferred_element_type=F32)
    return out


def _sel_l(onehot, parts):
    out = jnp.dot(onehot, parts[0], preferred_element_type=F32)
    for p in parts[1:]:
        out = out + jnp.dot(onehot, p, preferred_element_type=F32)
    return out


def _rows8(*rows):
    r = _iota((8, rows[0].shape[1]), 0)
    out = jnp.zeros((8, rows[0].shape[1]), F32)
    for k, v in enumerate(rows):
        out = jnp.where(r == k, v, out)
    return out


def _ssd_forward(x, z, bm, cm, dt_raw, st_prev, dtb, alog, dskip, gn, g, cst_scr):
    li, si = _iota((BLK, BLK), 0), _iota((BLK, BLK), 1)
    dt_all = jax.nn.softplus(dt_raw + dtb)
    a_row = -jnp.exp(alog)
    a_all = dt_all * a_row
    cs_all = _sel_l((li >= si).astype(BF16), _split3(a_all))
    cs_parts = _split3(cs_all)
    spread = (_iota((BLK, GRP_W), 0) == g * HPG + jnp.right_shift(_iota((BLK, GRP_W), 1), 6)).astype(BF16)
    dt_e = _sel_r(_split3(dt_all), spread)
    cs_e = _sel_r(cs_parts, spread)
    d_e = _sel_r(_split3(_rows8(dskip)), spread)[0:1]
    cs_last_e = jnp.sum(jnp.where(_iota((BLK, GRP_W), 0) == BLK - 1, cs_e, 0.0), axis=0, keepdims=True)
    p_e = jnp.exp(cs_e)
    w_e = jnp.exp(cs_last_e - cs_e)
    cd_e = jnp.exp(cs_last_e)
    xr = x * dt_e
    cst_scr[...] = cs_all.T
    cst_g = cst_scr[pl.ds(pl.multiple_of(g * HPG, HPG), HPG), :]
    own = jnp.right_shift(_iota((HPG, HPG * BLK), 1), 7) == _iota((HPG, HPG * BLK), 0)
    ownf = own.astype(F32)
    q_rows = [ownf, ownf, ownf] + [jnp.where(own, jnp.concatenate([p.astype(F32)] * HPG, axis=1), 0.0)
                                   for p in _split3(cst_g)]
    q2 = jnp.concatenate(q_rows + [jnp.zeros((BLK - 6 * HPG, HPG * BLK), F32)], axis=0).astype(BF16)
    lane1 = _iota((1, BLK), 1)
    p2 = jnp.where((lane1 >= 3 * HPG) & (lane1 < 6 * HPG), -1.0, 0.0)
    for k, part in enumerate(cs_parts):
        pick = ((li == g * HPG + si - k * HPG) & (si >= k * HPG) & (si < (k + 1) * HPG)).astype(BF16)
        p2 = p2 + jnp.dot(part, pick, preferred_element_type=F32)
    dmat = jnp.dot(p2.astype(BF16), q2, preferred_element_type=F32)
    causal = _iota((BLK, HPG * BLK), 0) >= jnp.bitwise_and(_iota((BLK, HPG * BLK), 1), BLK - 1)
    lam = jnp.exp(jnp.where(causal, dmat, NEG))
    gmat = _mm(cm, bm, 1, 1)
    m_all = lam * jnp.concatenate([gmat] * HPG, axis=1)
    mb = m_all.astype(BF16)
    lo = _iota((BLK, BLK), 1) < HEAD
    xrb = xr.astype(BF16)
    zero = jnp.zeros((BLK, BLK), BF16)
    bds, yd = [], []
    for i in range(HPG // 2):
        t = xrb[:, BLK * i:BLK * (i + 1)]
        bd = jnp.concatenate([jnp.where(lo, t, zero), jnp.where(lo, zero, t)], axis=0)
        bds.append(bd)
        yd.append(jnp.dot(mb[:, 2 * BLK * i:2 * BLK * (i + 1)], bd, preferred_element_type=F32))
    cs_st = _mm(cm, st_prev)
    y = jnp.concatenate(yd, axis=1) + cs_st * p_e + d_e * x
    xrw = xr * w_e
    st_new = cd_e * st_prev + _mm(bm, xrw, 0, 0)
    yz = y * _silu(z)
    rn = lax.rsqrt(jnp.sum(yz * yz, axis=1, keepdims=True) / GRP_W + EPS)
    return dict(out=yz * rn * gn, st_new=st_new, dt_all=dt_all, a_row=a_row, dt_e=dt_e, d_e=d_e, p_e=p_e, w_e=w_e,
                cd_e=cd_e, xr=xr, xrw=xrw, lam=lam, m_all=m_all, mb=mb, bds=bds, cs_st=cs_st, y=y, yz=yz, rn=rn, lo=lo)


def _ssd_backward(f, x, z, bm, cm, dt_raw, st_prev, dtb, gn, g, dout, dst_next, cst_scr):
    li, si = _iota((BLK, BLK), 0), _iota((BLK, BLK), 1)
    yz, rn, y, p_e, w_e, cd_e, xr = f["yz"], f["rn"], f["y"], f["p_e"], f["w_e"], f["cd_e"], f["xr"]
    dgn = jnp.sum(dout * yz * rn, axis=0, keepdims=True)
    t = dout * gn
    dyz = rn * t - yz * (rn * rn * rn) * (jnp.sum(yz * t, axis=1, keepdims=True) / GRP_W)
    dy = dyz * _silu(z)
    dz = dyz * y * _dsilu(z)
    dx = f["d_e"] * dy
    dd_e = jnp.sum(dy * x, axis=0, keepdims=True)
    dcsst = dy * p_e
    dp_e = dy * f["cs_st"]
    dcm = _mm(dcsst, st_prev, 1, 1)
    dst_prev = _mm(cm, dcsst, 0, 0) + cd_e * dst_next
    dcd_e = jnp.sum(dst_next * st_prev, axis=0, keepdims=True)
    dbm = _mm(f["xrw"], dst_next, 1, 1)
    dxrw = _mm(bm, dst_next)
    dxr = dxrw * w_e
    dw_e = dxrw * xr
    dyb = dy.astype(BF16)
    dms, dxr_d = [], []
    for i in range(HPG // 2):
        dyp = dyb[:, BLK * i:BLK * (i + 1)]
        dms.append(lax.dot_general(dyp, f["bds"][i], (((1,), (1,)), ((), ())), preferred_element_type=F32))
        r = lax.dot_general(f["mb"][:, 2 * BLK * i:2 * BLK * (i + 1)], dyp, (((0,), (0,)), ((), ())),
                            preferred_element_type=F32)
        dxr_d.append(jnp.where(f["lo"], r[0:BLK], r[BLK:2 * BLK]))
    dm_all = jnp.concatenate(dms, axis=1)
    dxr = dxr + jnp.concatenate(dxr_d, axis=1)
    dlg = dm_all * f["lam"]
    dg = dlg[:, 0:BLK]
    for j in range(1, HPG):
        dg = dg + dlg[:, BLK * j:BLK * (j + 1)]
    dcm = dcm + _mm(dg, bm)
    dbm = dbm + _mm(dg, cm, 0, 0)
    q_all = dm_all * f["m_all"]
    col_sums = jnp.sum(q_all, axis=0, keepdims=True)
    cst_scr[...] = jnp.zeros_like(cst_scr)
    cst_scr[pl.ds(pl.multiple_of(g * HPG, HPG), HPG), :] = _rows8(
        *[col_sums[:, BLK * j:BLK * (j + 1)] for j in range(HPG)])
    dcs = -cst_scr[...].T
    for j in range(HPG):
        dcs = dcs + jnp.where(si == g * HPG + j,
                              jnp.sum(q_all[:, BLK * j:BLK * (j + 1)], axis=1, keepdims=True), 0.0)
    unspread = (_iota((GRP_W, BLK), 1) == g * HPG + jnp.right_shift(_iota((GRP_W, BLK), 0), 6)).astype(BF16)
    dww = dw_e * w_e
    per_head = _sel_r(_split3(jnp.concatenate([dp_e * p_e - dww, dxr * x], axis=0)), unspread)
    last = _sel_r(_split3(_rows8(jnp.sum(dww, axis=0, keepdims=True) + dcd_e * cd_e, dd_e)), unspread)
    dcs = dcs + per_head[0:BLK] + jnp.where(li == BLK - 1, last[0:1], 0.0)
    da = _sel_l((si >= li).astype(BF16), _split3(dcs))
    ddt_all = da * f["a_row"] + per_head[BLK:2 * BLK]
    dalog = jnp.sum(da * f["dt_all"], axis=0, keepdims=True) * f["a_row"]
    dx = dx + dxr * f["dt_e"]
    ddt_raw = ddt_all * jax.nn.sigmoid(dt_raw + dtb)
    ddtb = jnp.sum(ddt_raw, axis=0, keepdims=True)
    ddskip = last[1:2]
    return dict(dx=dx, dz=dz, dbm=dbm, dcm=dcm, ddt_raw=ddt_raw, dst_prev=dst_prev, ddtb=ddtb, dalog=dalog,
                ddskip=ddskip, dgn=dgn)


GPS = 4
NPG = SSM_GROUPS // GPS


def _ssd_in_specs(rev):
    cidx = (lambda c: NB - 1 - c) if rev else (lambda c: c)
    wx, wb = GPS * GRP_W, GPS * SSM_STATE
    return [
        pl.BlockSpec((BLK, wx), lambda p, c: (cidx(c), p)),
        pl.BlockSpec((BLK, wb), lambda p, c: (cidx(c), SSM_INNER // wb + p)),
        pl.BlockSpec((BLK, wb), lambda p, c: (cidx(c), (SSM_INNER + SSM_GROUPS * SSM_STATE) // wb + p)),
        pl.BlockSpec((BLK, 128), lambda p, c: (cidx(c), C_DT // 128)),
        pl.BlockSpec((BLK, wx), lambda p, c: (cidx(c), C_ZS // wx + p)),
        pl.BlockSpec((1, 128), lambda p, c: (0, 0)),
        pl.BlockSpec((1, 128), lambda p, c: (0, 0)),
        pl.BlockSpec((1, 128), lambda p, c: (0, 0)),
        pl.BlockSpec((1, wx), lambda p, c: (0, p)),
    ]


def _grp(ref, i, w):
    return ref[:, i * w:(i + 1) * w]


def _ssd_fwd(xbc_act, proj, dt_bias, a_log, d_skip, g_norm, gather=()):
    ng = len(gather)

    def body(*refs):
        xs_ref, b_ref, c_ref, dt_ref, z_ref, dtb_ref, al_ref, dsk_ref, gn_ref = refs[:9]
        y_ref, st_ref = refs[9 + ng:11 + ng]
        s_scr, cst_scr = refs[11 + 2 * ng:13 + 2 * ng]
        p = pl.program_id(0)
        c = pl.program_id(1)
        if ng:
            ag_start, ag_forward, ag_finish = _ag_program(refs[9:9 + ng], refs[11 + ng:11 + 2 * ng],
                                                          refs[13 + 2 * ng:])
            pl.when((p == 0) & (c == 0))(ag_start)
            pl.when((p == NPG - 1) & (c == NB // 2))(ag_forward)

        @pl.when(c == 0)
        def _():
            s_scr[...] = jnp.zeros_like(s_scr)

        for i in range(GPS):
            st_prev = s_scr[i]
            st_ref[i, 0] = st_prev
            f = _ssd_forward(_grp(xs_ref, i, GRP_W), _grp(z_ref, i, GRP_W), _grp(b_ref, i, SSM_STATE),
                             _grp(c_ref, i, SSM_STATE), dt_ref[...], st_prev, dtb_ref[...], al_ref[...],
                             dsk_ref[...], _grp(gn_ref, i, GRP_W), p * GPS + i, cst_scr.at[i])
            y_ref[:, i * GRP_W:(i + 1) * GRP_W] = f["out"].astype(BF16)
            s_scr[i] = f["st_new"]
        if ng:
            pl.when((p == NPG - 1) & (c == NB - 1))(ag_finish)

    return pl.pallas_call(
        body, grid=(NPG, NB), in_specs=_ssd_in_specs(False) + [ANY] * ng,
        out_specs=[pl.BlockSpec((BLK, GPS * GRP_W), lambda p, c: (c, p)),
                   pl.BlockSpec((GPS, 1, SSM_STATE, GRP_W), lambda p, c: (p, c, 0, 0))] + [ANY] * ng,
        out_shape=[SDS((T, SSM_INNER), BF16), SDS((SSM_GROUPS, NB, SSM_STATE, GRP_W), F32)]
        + [SDS((N_DEV,) + s.shape, s.dtype) for s in gather],
        scratch_shapes=[pltpu.VMEM((GPS, SSM_STATE, GRP_W), F32), pltpu.VMEM((GPS, BLK, BLK), F32)]
        + (_ag_scratch(gather) if ng else []),
        compiler_params=_cparams(),
        name="ssd_fwd")(xbc_act, xbc_act, xbc_act, proj, proj, dt_bias, a_log, d_skip, g_norm, *gather)


def _ssd_bwd(xbc_act, proj, dt_bias, a_log, d_skip, g_norm, states, dy, exchange=()):
    chips = exchange
    nc = len(chips)

    def body(*refs):
        xs_ref, b_ref, c_ref, dt_ref, z_ref, dtb_ref, al_ref, dsk_ref, gn_ref, st_ref, dy_ref = refs[:11]
        (dxs_ref, db_ref, dc_ref, ddt_ref, dz_ref, ddtb_ref, dal_ref, ddsk_ref, dgn_ref) = refs[11 + nc:20 + nc]
        ds_scr, cst_scr = refs[20 + 2 * nc:22 + 2 * nc]
        p = pl.program_id(0)
        c = pl.program_id(1)
        if nc:
            ch_start, ch_finish = _direct_program(refs[11:11 + nc], refs[20 + nc:20 + 2 * nc], refs[22 + 2 * nc:])
            pl.when((p == 0) & (c == 0))(ch_start)

        @pl.when(c == 0)
        def _():
            ds_scr[...] = jnp.zeros_like(ds_scr)
            dgn_ref[...] = jnp.zeros_like(dgn_ref)

        @pl.when((c == 0) & (p == 0))
        def _():
            ddtb_ref[...] = jnp.zeros_like(ddtb_ref)
            dal_ref[...] = jnp.zeros_like(dal_ref)
            ddsk_ref[...] = jnp.zeros_like(ddsk_ref)

        dt_raw = dt_ref[...]
        for i in range(GPS):
            g = p * GPS + i
            x, z, gn = _grp(xs_ref, i, GRP_W), _grp(z_ref, i, GRP_W), _grp(gn_ref, i, GRP_W)
            bm, cm, st_prev = _grp(b_ref, i, SSM_STATE), _grp(c_ref, i, SSM_STATE), st_ref[i, 0]
            f = _ssd_forward(x, z, bm, cm, dt_raw, st_prev, dtb_ref[...], al_ref[...], dsk_ref[...], gn, g,
                             cst_scr.at[i])
            d = _ssd_backward(f, x, z, bm, cm, dt_raw, st_prev, dtb_ref[...], gn, g,
                              _grp(dy_ref, i, GRP_W).astype(F32), ds_scr[i], cst_scr.at[i])
            dxs_ref[:, i * GRP_W:(i + 1) * GRP_W] = d["dx"]
            dz_ref[:, i * GRP_W:(i + 1) * GRP_W] = d["dz"].astype(BF16)
            ds_scr[i] = d["dst_prev"]
            db_ref[:, i * SSM_STATE:(i + 1) * SSM_STATE] = d["dbm"]
            dc_ref[:, i * SSM_STATE:(i + 1) * SSM_STATE] = d["dcm"]
            ddt_ref[:, i * 128:(i + 1) * 128] = d["ddt_raw"]
            dgn_ref[0:1, i * GRP_W:(i + 1) * GRP_W] += d["dgn"]
            ddtb_ref[0:1, :] += d["ddtb"]
            dal_ref[0:1, :] += d["dalog"]
            ddsk_ref[0:1, :] += d["ddskip"]
        if nc:
            pl.when((p == NPG - 1) & (c == NB - 1))(ch_finish)

    rc = lambda c: NB - 1 - c
    small = pl.BlockSpec((8, 128), lambda p, c: (0, 0))
    wx, wb = GPS * GRP_W, GPS * SSM_STATE
    return pl.pallas_call(
        body, grid=(NPG, NB),
        in_specs=_ssd_in_specs(True) + [
            pl.BlockSpec((GPS, 1, SSM_STATE, GRP_W), lambda p, c: (p, rc(c), 0, 0)),
            pl.BlockSpec((BLK, wx), lambda p, c: (rc(c), p))] + [ANY] * nc,
        out_specs=[pl.BlockSpec((BLK, wx), lambda p, c: (rc(c), p)),
                   pl.BlockSpec((BLK, wb), lambda p, c: (rc(c), p)),
                   pl.BlockSpec((BLK, wb), lambda p, c: (rc(c), p)),
                   pl.BlockSpec((BLK, GPS * 128), lambda p, c: (rc(c), p)),
                   pl.BlockSpec((BLK, wx), lambda p, c: (rc(c), p)),
                   small, small, small,
                   pl.BlockSpec((8, wx), lambda p, c: (0, p))] + [ANY] * nc,
        out_shape=[SDS((T, SSM_INNER), F32), SDS((T, GRP_W), F32), SDS((T, GRP_W), F32), SDS((T, GRP_W), F32),
                   SDS((T, SSM_INNER), BF16), SDS((8, 128), F32), SDS((8, 128), F32), SDS((8, 128), F32),
                   SDS((8, SSM_INNER), F32)] + [SDS(p.shape, p.dtype) for p in chips],
        scratch_shapes=[pltpu.VMEM((GPS, SSM_STATE, GRP_W), F32), pltpu.VMEM((GPS, BLK, BLK), F32)]
        + (_direct_scratch(chips) if nc else []),
        compiler_params=_cparams(),
        name="ssd_bwd")(xbc_act, xbc_act, xbc_act, proj, proj, dt_bias, a_log, d_skip, g_norm, states, dy, *chips)


POST_R = 272


def _post_a(o, proj, sn, w_att, w_ssm, w_o):
    def body(o_ref, za_ref, ga_ref, gs_ref, sn_ref, wa_ref, ws_ref, wo_ref, a_ref, mg_ref, ya_ref, ys_ref, out_ref):
        a = (o_ref[...] * _silu(za_ref[...])).astype(BF16)
        a_ref[...] = a
        ya = jnp.dot(a, wa_ref[...], preferred_element_type=F32)
        ys = jnp.dot(sn_ref[...], ws_ref[...], preferred_element_type=F32)
        ya_ref[...] = ya.astype(BF16)
        ys_ref[...] = ys.astype(BF16)
        mg = (jax.nn.sigmoid(ga_ref[...]) * ya + jax.nn.sigmoid(gs_ref[...]) * ys).astype(BF16)
        mg_ref[...] = mg
        out_ref[...] = jnp.dot(mg, wo_ref[...], preferred_element_type=F32)

    row = pl.BlockSpec((POST_R, D_MODEL), lambda i: (i, 0))
    pcol = lambda c0: pl.BlockSpec((POST_R, D_MODEL), lambda i: (i, c0 // D_MODEL))
    full = lambda r: pl.BlockSpec((r, D_MODEL), lambda i: (0, 0))
    return pl.pallas_call(
        body, grid=(T // POST_R,),
        in_specs=[row, pcol(C_ZA), pcol(C_GA), pcol(C_GS), pl.BlockSpec((POST_R, SSM_INNER), lambda i: (i, 0)),
                  full(D_MODEL), full(SSM_INNER), full(D_MODEL)],
        out_specs=[row, row, row, row, row],
        out_shape=[SDS((T, D_MODEL), BF16), SDS((T, D_MODEL), BF16), SDS((T, D_MODEL), BF16), SDS((T, D_MODEL), BF16),
                   SDS((T, D_MODEL), F32)],
        compiler_params=_cparams(), name="post_a")(o, proj, proj, proj, sn, w_att, w_ssm, w_o)


def _post_b(out, h, tgt, proj, ya, ys, o, g_post, w_att, w_ssm, w_o):
    def body(out_ref, h_ref, t_ref, za_ref, ga_ref, gs_ref, ya_ref, ys_ref, o_ref, gp_ref, wa_ref, ws_ref, wo_ref,
             loss_ref, dres_ref, dout_ref, dya_ref, dys_ref, dga_ref, dgs_ref, do_ref, dza_ref, dsn_ref, dgp_ref):
        i = pl.program_id(0)
        x = out_ref[...]
        gp = gp_ref[...]
        r = lax.rsqrt(jnp.mean(x * x, axis=-1, keepdims=True) + EPS)
        row = i * POST_R + lax.broadcasted_iota(jnp.int32, (POST_R, 1), 0)
        res = h_ref[...] + jnp.where(row >= PAD, x * r * gp, 0.0)
        live = row >= PAD + N_META
        err = jnp.where(live, res - t_ref[...], 0.0)
        lpart = 0.5 * jnp.sum(jnp.sum(err * err, axis=1, keepdims=True) / D_MODEL, axis=0, keepdims=True)
        dres = err / D_MODEL
        dres_ref[...] = dres
        gpart = jnp.sum(dres * x * r, axis=0, keepdims=True)

        @pl.when(i == 0)
        def _():
            loss_ref[...] = jnp.zeros_like(loss_ref)
            dgp_ref[...] = jnp.zeros_like(dgp_ref)

        loss_ref[...] += jnp.broadcast_to(lpart, loss_ref.shape)
        dgp_ref[0:1, :] += gpart
        gd = gp * dres
        dout = (r * gd - x * (r * r * r) * jnp.mean(x * gd, axis=-1, keepdims=True)).astype(BF16)
        dout_ref[...] = dout
        dmg = lax.dot_general(dout, wo_ref[...], (((1,), (1,)), ((), ())), preferred_element_type=F32)
        sga = jax.nn.sigmoid(ga_ref[...])
        sgs = jax.nn.sigmoid(gs_ref[...])
        dya = (dmg * sga).astype(BF16)
        dys = (dmg * sgs).astype(BF16)
        dya_ref[...] = dya
        dys_ref[...] = dys
        dga_ref[...] = (dmg * ya_ref[...].astype(F32) * sga * (1.0 - sga)).astype(BF16)
        dgs_ref[...] = (dmg * ys_ref[...].astype(F32) * sgs * (1.0 - sgs)).astype(BF16)
        da = lax.dot_general(dya, wa_ref[...], (((1,), (1,)), ((), ())), preferred_element_type=F32)
        za = za_ref[...]
        do_ref[...] = (da * _silu(za)).astype(BF16)
        dza_ref[...] = (da * o_ref[...] * _dsilu(za)).astype(BF16)
        dsn_ref[...] = lax.dot_general(dys, ws_ref[...], (((1,), (1,)), ((), ())),
                                       preferred_element_type=F32).astype(BF16)

    row = pl.BlockSpec((POST_R, D_MODEL), lambda i: (i, 0))
    pcol = lambda c0: pl.BlockSpec((POST_R, D_MODEL), lambda i: (i, c0 // D_MODEL))
    full = lambda r: pl.BlockSpec((r, D_MODEL), lambda i: (0, 0))
    small = pl.BlockSpec((8, D_MODEL), lambda i: (0, 0))
    return pl.pallas_call(
        body, grid=(T // POST_R,),
        in_specs=[row, row, row, pcol(C_ZA), pcol(C_GA), pcol(C_GS), row, row, row,
                  pl.BlockSpec((1, D_MODEL), lambda i: (0, 0)), full(D_MODEL), full(SSM_INNER), full(D_MODEL)],
        out_specs=[pl.BlockSpec((8, 128), lambda i: (0, 0)), row, row, row, row, row, row, row, row,
                   pl.BlockSpec((POST_R, SSM_INNER), lambda i: (i, 0)), small],
        out_shape=[SDS((8, 128), F32), SDS((T, D_MODEL), F32), SDS((T, D_MODEL), BF16), SDS((T, D_MODEL), BF16),
                   SDS((T, D_MODEL), BF16), SDS((T, D_MODEL), BF16), SDS((T, D_MODEL), BF16), SDS((T, D_MODEL), BF16),
                   SDS((T, D_MODEL), BF16), SDS((T, SSM_INNER), BF16), SDS((8, D_MODEL), F32)],
        compiler_params=_cparams(), name="post_b")(out, h, tgt, proj, proj, proj, ya, ys, o, g_post, w_att, w_ssm, w_o)


def _assemble(dq, dza, dga, dgs, dzs, dxx, dxb, dxc, dk, dv, ddt4):
    def body(dq_ref, dza_ref, dga_ref, dgs_ref, dzs_ref, dxx_ref, dxb_ref, dxc_ref, dk_ref, dv_ref, ddt_ref, o_ref):
        o_ref[:, C_Q:C_Q + D_MODEL] = dq_ref[...].astype(BF16)
        o_ref[:, C_ZA:C_ZA + D_MODEL] = dza_ref[...]
        o_ref[:, C_GA:C_GA + D_MODEL] = dga_ref[...]
        o_ref[:, C_GS:C_GS + D_MODEL] = dgs_ref[...]
        o_ref[:, C_ZS:C_ZS + SSM_INNER] = dzs_ref[...]
        o_ref[:, C_XBC:C_XBC + SSM_INNER] = dxx_ref[...]
        o_ref[:, C_XBC + SSM_INNER:C_XBC + SSM_INNER + GRP_W] = dxb_ref[...]
        o_ref[:, C_XBC + SSM_INNER + GRP_W:C_XBC + CONV_DIM] = dxc_ref[...]
        o_ref[:, C_K:C_K + KV_W] = dk_ref[...].astype(BF16)
        o_ref[:, C_V:C_V + KV_W] = dv_ref[...].astype(BF16)
        d4 = ddt_ref[...]
        o_ref[:, C_DT:C_DT + 128] = (d4[:, 0:128] + d4[:, 128:256] + d4[:, 256:384] + d4[:, 384:512]).astype(BF16)

    spec = lambda w: pl.BlockSpec((BLK, w), lambda i: (i, 0))
    ins = [dq, dza, dga, dgs, dzs, dxx, dxb, dxc, dk, dv, ddt4]
    return pl.pallas_call(
        body, grid=(NB,), in_specs=[spec(a.shape[1]) for a in ins], out_specs=spec(PW),
        out_shape=SDS((T, PW), BF16), name="assemble")(*ins)


def _adamw_math(w, g, m, v):
    m = ADAM_B1 * m + (1.0 - ADAM_B1) * g
    v = ADAM_B2 * v + (1.0 - ADAM_B2) * (g * g)
    m_hat = m / (1.0 - ADAM_B1 ** ADAM_STEP)
    v_hat = v / (1.0 - ADAM_B2 ** ADAM_STEP)
    delta = -ADAM_LR * (m_hat / (jnp.sqrt(v_hat) + ADAM_EPS) + ADAM_WD * w)
    return delta, m, v


def _sum_adamw(recv, w, m, v, tc, name):
    rows, cols = w.shape
    nslab = recv.shape[0]
    assert cols % tc == 0

    def body(r_ref, w_ref, m_ref, v_ref, g_ref, d_ref, nm_ref, nv_ref):
        g = r_ref[0].astype(F32)
        for d in range(1, nslab):
            g = g + r_ref[d].astype(F32)
        g_ref[...] = g
        delta, nm, nv = _adamw_math(w_ref[...], g, m_ref[...], v_ref[...])
        d_ref[...] = delta
        nm_ref[...] = nm
        nv_ref[...] = nv

    blk = pl.BlockSpec((rows, tc), lambda i: (0, i))
    return pl.pallas_call(
        body, grid=(cols // tc,),
        in_specs=[pl.BlockSpec((nslab, rows, tc), lambda i: (0, 0, i)), blk, blk, blk],
        out_specs=[blk, blk, blk, blk], out_shape=[SDS((rows, cols), F32)] * 4,
        compiler_params=_cparams(), name=name)(recv, w, m, v)


def _sum_adamw_rows3(recv, w3, m3, v3, name, exchange=()):
    pairs = 61
    assert (SHARD_IN // 2) % pairs == 0
    nsteps = SHARD_IN // 2 // pairs
    ne = len(exchange)

    def body(*refs):
        r_ref, w_ref, m_ref, v_ref = refs[:4]
        g_ref, d_ref, nm_ref, nv_ref = refs[4 + ne:8 + ne]
        if ne:
            ex_start, ex_finish = _direct_program(refs[4:4 + ne], refs[8 + ne:8 + 2 * ne], refs[8 + 2 * ne:])
            pl.when(pl.program_id(0) == 0)(ex_start)
        g = r_ref[0].astype(F32)
        for d in range(1, N_CHIP):
            g = g + r_ref[d].astype(F32)
        g = g.reshape(2 * pairs, ROW_TILES, 128)
        g_ref[...] = g
        delta, nm, nv = _adamw_math(w_ref[...], g, m_ref[...], v_ref[...])
        d_ref[...] = delta
        nm_ref[...] = nm
        nv_ref[...] = nv
        if ne:
            pl.when(pl.program_id(0) == nsteps - 1)(ex_finish)

    blk = pl.BlockSpec((2 * pairs, ROW_TILES, 128), lambda i: (i, 0, 0))
    return pl.pallas_call(
        body, grid=(nsteps,),
        in_specs=[pl.BlockSpec((N_CHIP, pairs, 2 * ROW_TILES, 128), lambda i: (0, i, 0, 0)), blk, blk, blk]
        + [ANY] * ne,
        out_specs=[blk, blk, blk, blk] + [ANY] * ne,
        out_shape=[SDS(w3.shape, F32)] * 4 + [SDS(p.shape, p.dtype) for p in exchange],
        scratch_shapes=_direct_scratch(exchange) if ne else [],
        compiler_params=_cparams(), name=name)(recv, w3, m3, v3, *exchange)


ROW_GPRE, ROW_CONVB, ROW_DTB, ROW_ALOG, ROW_DSKIP, ROW_SINK, ROW_GSSM, ROW_GPOST = 0, 1, 4, 5, 6, 7, 8, 10
REP_ROWS, ROW_CONVW, ROW_META, SM_ROWS = 16, 16, 24, 40
CW_SHARD = CONV_DIM // N_DEV
META_SHARD = D_MODEL // N_DEV


def _small_pack(dgpre, dbx, dbb, dbc, ddtb, dal, ddsk, dsink, dgn, dgp, dwx, dwb, dwc, dh):
    def body(dgpre_ref, dbx_ref, dbb_ref, dbc_ref, ddtb_ref, dal_ref, ddsk_ref, dsink_ref, dgn_ref, dgp_ref,
             dwx_ref, dwb_ref, dwc_ref, dh_ref, o_ref, rep):
        rep[...] = jnp.zeros_like(rep)
        rep[ROW_GPRE:ROW_GPRE + 1, :] = dgpre_ref[0:1, :]
        rep[ROW_CONVB:ROW_CONVB + 1, :] = dbx_ref[0:1, 0:1024]
        rep[ROW_CONVB + 1:ROW_CONVB + 2, :] = dbx_ref[0:1, 1024:2048]
        rep[ROW_CONVB + 2:ROW_CONVB + 3, 0:512] = dbb_ref[0:1, :]
        rep[ROW_CONVB + 2:ROW_CONVB + 3, 512:1024] = dbc_ref[0:1, :]
        rep[ROW_DTB:ROW_DTB + 1, 0:128] = ddtb_ref[0:1, :]
        rep[ROW_ALOG:ROW_ALOG + 1, 0:128] = dal_ref[0:1, :]
        rep[ROW_DSKIP:ROW_DSKIP + 1, 0:128] = ddsk_ref[0:1, :]
        rep[ROW_SINK:ROW_SINK + 1, 0:128] = dsink_ref[0:1, :]
        rep[ROW_GSSM:ROW_GSSM + 1, :] = dgn_ref[0:1, 0:1024]
        rep[ROW_GSSM + 1:ROW_GSSM + 2, :] = dgn_ref[0:1, 1024:2048]
        rep[ROW_GPOST:ROW_GPOST + 1, :] = dgp_ref[0:1, :]
        cw = jnp.concatenate([dwx_ref[...], dwb_ref[...], dwc_ref[...]], axis=1)
        mh = dh_ref[...]
        o_ref[...] = jnp.zeros_like(o_ref)
        for p in range(N_DEV):
            o_ref[p, 0:REP_ROWS, :] = rep[...]
            o_ref[p, ROW_CONVW:ROW_CONVW + 8, 0:CW_SHARD] = cw[:, p * CW_SHARD:(p + 1) * CW_SHARD]
            o_ref[p, ROW_META:ROW_META + N_META, 0:META_SHARD] = mh[:, p * META_SHARD:(p + 1) * META_SHARD]

    ins = [dgpre, dbx, dbb, dbc, ddtb, dal, ddsk, dsink, dgn, dgp, dwx, dwb, dwc]
    return pl.pallas_call(
        body, grid=(1,),
        in_specs=[pl.BlockSpec(a.shape, lambda i: (0, 0)) for a in ins]
        + [pl.BlockSpec((N_META, D_MODEL), lambda i: (PAD // N_META, 0))],
        out_specs=pl.BlockSpec((N_DEV, SM_ROWS, 1024), lambda i: (0, 0, 0)),
        out_shape=SDS((N_DEV, SM_ROWS, 1024), F32), scratch_shapes=[pltpu.VMEM((REP_ROWS, 1024), F32)],
        name="small_pack")(*ins, dh)


def _small_finish(recv, params):
    npar = len(params)

    def body(*refs):
        r_ref = refs[0]
        wmv = refs[1:1 + 3 * npar]
        outs = refs[1 + 3 * npar:1 + 7 * npar]
        gs = refs[-1]
        g = r_ref[0]
        for d in range(1, recv.shape[0]):
            g = g + r_ref[d]
        gs[...] = g
        grads = [
            gs[ROW_GPRE:ROW_GPRE + 1, :],
            jnp.concatenate([gs[ROW_CONVB + k:ROW_CONVB + k + 1, :] for k in range(3)], axis=1),
            gs[ROW_DTB:ROW_DTB + 1, 0:SSM_HEADS], gs[ROW_ALOG:ROW_ALOG + 1, 0:SSM_HEADS],
            gs[ROW_DSKIP:ROW_DSKIP + 1, 0:SSM_HEADS], gs[ROW_SINK:ROW_SINK + 1, 0:Q_HEADS],
            jnp.concatenate([gs[ROW_GSSM:ROW_GSSM + 1, :], gs[ROW_GSSM + 1:ROW_GSSM + 2, :]], axis=1),
            gs[ROW_GPOST:ROW_GPOST + 1, :],
            gs[ROW_CONVW:ROW_CONVW + 4, 0:CW_SHARD],
            gs[ROW_META:ROW_META + N_META, 0:META_SHARD]]
        for i in range(npar):
            w_ref, m_ref, v_ref = wmv[3 * i:3 * i + 3]
            delta, nm, nv = _adamw_math(w_ref[...], grads[i], m_ref[...], v_ref[...])
            outs[4 * i][...] = grads[i]
            outs[4 * i + 1][...] = delta
            outs[4 * i + 2][...] = nm
            outs[4 * i + 3][...] = nv

    flat = [a for wmv in params for a in wmv]
    res = pl.pallas_call(
        body, out_shape=[SDS(wmv[0].shape, F32) for wmv in params for _ in range(4)],
        scratch_shapes=[pltpu.VMEM((SM_ROWS, 1024), F32)], name="small_finish")(recv, *flat)
    return [tuple(res[4 * i:4 * i + 4]) for i in range(npar)]


def _slab(ref, px, py, pc):
    return ref.at[4 * px + 2 * py + pc]


def _bounce(src, dst, buf, sem):
    cp = pltpu.make_async_copy(src, buf, sem)
    cp.start()
    cp.wait()
    cp = pltpu.make_async_copy(buf, dst, sem)
    cp.start()
    cp.wait()


def _ag_program(ins, outs, scratch):
    na = len(ins)
    send_sems, recv_sems, local_sems = scratch[:3]
    bufs = scratch[3:]
    x, y, c = lax.axis_index("x"), lax.axis_index("y"), lax.axis_index("c")
    me, sibling = (x, y, c), (x, y, 1 - c)
    chips = [(1 - x, y), (x, 1 - y), (1 - x, 1 - y)]

    def copy(a, k, block, to, src=None):
        dst = _slab(outs[a], *block)
        return pltpu.make_async_remote_copy(
            src_ref=dst if src is None else src, dst_ref=dst, send_sem=send_sems.at[a, k],
            recv_sem=recv_sems.at[a, k], device_id=to, device_id_type=MESH)

    def own_sends():
        out = []
        for a in range(na):
            out.append(copy(a, 0, me, sibling, src=ins[a]))
            out += [copy(a, 1 + j, me, (*chip, c), src=ins[a]) for j, chip in enumerate(chips)]
        return out

    def start():
        for cp in own_sends():
            cp.start()
        for a in range(na):
            _bounce(ins[a], _slab(outs[a], *me), bufs[a], local_sems.at[a])

    def forward():
        for j, chip in enumerate(chips):
            for a in range(na):
                copy(a, 1 + j, (*chip, c), me).wait_recv()
                copy(a, 4 + j, (*chip, c), sibling).start()

    def finish():
        for a in range(na):
            copy(a, 0, sibling, me).wait_recv()
            for j, chip in enumerate(chips):
                copy(a, 4 + j, (*chip, 1 - c), me).wait_recv()
        for cp in own_sends():
            cp.wait_send()
        for j, chip in enumerate(chips):
            for a in range(na):
                copy(a, 4 + j, (*chip, c), sibling).wait_send()

    return start, forward, finish


def _ag_scratch(shards):
    na = len(shards)
    return [pltpu.SemaphoreType.DMA((na, 7)), pltpu.SemaphoreType.DMA((na, 7)),
            pltpu.SemaphoreType.DMA((na,))] + [pltpu.VMEM(s.shape, s.dtype) for s in shards]


def _all_gather(shards):
    na = len(shards)

    def body(*refs):
        start, forward, finish = _ag_program(refs[:na], refs[na:2 * na], refs[2 * na:])
        start()
        forward()
        finish()

    return pl.pallas_call(
        body, in_specs=[ANY] * na, out_specs=[ANY] * na,
        out_shape=[SDS((N_DEV,) + s.shape, s.dtype) for s in shards],
        scratch_shapes=_ag_scratch(shards), name="all_gather")(*shards)


N_CHIP = 4


def _pair_sum(own, got, name):
    na = len(own)

    def body(*refs):
        for a in range(na):
            o_ref, g_ref, s_ref = refs[a], refs[na + a], refs[2 * na + a]
            s_ref[...] = (o_ref[...].astype(F32) + g_ref[...].astype(F32)).astype(s_ref.dtype)

    def spec(p):
        nd = len(p.shape) - 1
        return pl.BlockSpec((1,) + p.shape[1:], lambda k, nd=nd: (k,) + (0,) * nd)

    return pl.pallas_call(
        body, grid=(N_CHIP,), in_specs=[spec(p) for p in own] + [spec(p) for p in got],
        out_specs=[spec(p) for p in own], out_shape=[SDS(p.shape, p.dtype) for p in own],
        compiler_params=_cparams(), name=name)(*own, *got)


def _chips_program(ins, outs, scratch):
    na = len(ins)
    send_sems, recv_sems, local_sems = scratch[:3]
    bufs = scratch[3:]
    x, y, c = lax.axis_index("x"), lax.axis_index("y"), lax.axis_index("c")
    mine = 2 * x + y
    chips = [(1 - x, y), (x, 1 - y), (1 - x, 1 - y)]

    def send(a, j):
        px, py = chips[j]
        return pltpu.make_async_remote_copy(
            src_ref=ins[a].at[2 * px + py], dst_ref=outs[a].at[mine], send_sem=send_sems.at[a, j],
            recv_sem=recv_sems.at[a, j], device_id=(px, py, c), device_id_type=MESH)

    def arrival(a, j):
        px, py = chips[j]
        return pltpu.make_async_remote_copy(
            src_ref=ins[a].at[2 * px + py], dst_ref=outs[a].at[2 * px + py], send_sem=send_sems.at[a, j],
            recv_sem=recv_sems.at[a, j], device_id=(px, py, c), device_id_type=MESH)

    def start():
        for a in range(na):
            for j in range(3):
                send(a, j).start()
        for a in range(na):
            _bounce(ins[a].at[mine], outs[a].at[mine], bufs[a], local_sems.at[a])

    def finish():
        for a in range(na):
            for j in range(3):
                arrival(a, j).wait_recv()
        for a in range(na):
            for j in range(3):
                send(a, j).wait_send()

    return start, finish


def _chips_scratch(parts):
    na = len(parts)
    return [pltpu.SemaphoreType.DMA((na, 3)), pltpu.SemaphoreType.DMA((na, 3)),
            pltpu.SemaphoreType.DMA((na,))] + [pltpu.VMEM(p.shape[1:], p.dtype) for p in parts]


def _direct_program(ins, outs, scratch):
    na = len(ins)
    send_sems, recv_sems, local_sems = scratch[:3]
    bufs = scratch[3:]
    x, y, c = lax.axis_index("x"), lax.axis_index("y"), lax.axis_index("c")
    me = (x, y, c)
    peers = []
    for k in range(1, N_DEV):
        dx, dy, dc = (k >> 2) & 1, (k >> 1) & 1, k & 1
        peers.append(((1 - x) if dx else x, (1 - y) if dy else y, (1 - c) if dc else c))

    def send(a, k):
        return pltpu.make_async_remote_copy(
            src_ref=_slab(ins[a], *peers[k]), dst_ref=_slab(outs[a], *me), send_sem=send_sems.at[a, k],
            recv_sem=recv_sems.at[a, k], device_id=peers[k], device_id_type=MESH)

    def arrival(a, k):
        return pltpu.make_async_remote_copy(
            src_ref=_slab(ins[a], *peers[k]), dst_ref=_slab(outs[a], *peers[k]), send_sem=send_sems.at[a, k],
            recv_sem=recv_sems.at[a, k], device_id=peers[k], device_id_type=MESH)

    def start():
        for a in range(na):
            for k in range(N_DEV - 1):
                send(a, k).start()
        for a in range(na):
            _bounce(_slab(ins[a], *me), _slab(outs[a], *me), bufs[a], local_sems.at[a])

    def finish():
        for a in range(na):
            for k in range(N_DEV - 1):
                arrival(a, k).wait_recv()
        for a in range(na):
            for k in range(N_DEV - 1):
                send(a, k).wait_send()

    return start, finish


def _direct_scratch(parts):
    na = len(parts)
    return [pltpu.SemaphoreType.DMA((na, N_DEV - 1)), pltpu.SemaphoreType.DMA((na, N_DEV - 1)),
            pltpu.SemaphoreType.DMA((na,))] + [pltpu.VMEM(p.shape[1:], p.dtype) for p in parts]


ROW_TILES = D_MODEL // 128


def _rows3(t):
    return jnp.transpose(t[0]).reshape(t.shape[2], ROW_TILES, 128)


def _unrows3(t):
    return jnp.transpose(t.reshape(t.shape[0], D_MODEL))[None]


def _cast_shards(w_in3, w_att, w_ssm, w_o):
    def body(wi_ref, wa_ref, ws_ref, wo_ref, a_ref, b_ref, c_ref, d_ref):
        a_ref[...] = wi_ref[...].reshape(SHARD_IN // 2, 2 * ROW_TILES, 128).astype(BF16)
        b_ref[...] = wa_ref[...].astype(BF16)
        c_ref[...] = ws_ref[...].astype(BF16)
        d_ref[...] = wo_ref[...].astype(BF16)

    return pl.pallas_call(
        body, out_shape=[SDS((SHARD_IN // 2, 2 * ROW_TILES, 128), BF16), SDS(w_att.shape, BF16),
                         SDS(w_ssm.shape, BF16), SDS(w_o.shape, BF16)],
        compiler_params=_cparams(), name="cast_shards")(w_in3, w_att, w_ssm, w_o)


def _pieces():
    out = []
    for r0, c0, w in _SEGS:
        r = r0
        while r < r0 + w:
            d = r // SHARD_IN
            n = min(r0 + w, (d + 1) * SHARD_IN) - r
            out.append((c0 + (r - r0), d, r - d * SHARD_IN, n))
            r += n
    return out


def _to_aligned_t(slabs):
    def body(a_ref, o_ref):
        for (t, d, s, n) in _pieces():
            o_ref[t:t + n, :] = a_ref[d, s // 2:(s + n) // 2].reshape(n, D_MODEL)
        o_ref[C_DT + 32:C_DT + 128, :] = jnp.zeros((96, D_MODEL), slabs.dtype)

    return pl.pallas_call(body, out_shape=SDS((PW, D_MODEL), slabs.dtype), compiler_params=_cparams(),
                          name="to_aligned")(slabs)


def _from_aligned_pair(g):
    slab = (SHARD_IN // 2, 2 * ROW_TILES, 128)
    by_slab = [[p for p in _pieces() if p[1] == d] for d in range(N_DEV)]

    def body(g_ref, own_ref, got_ref, slabs, send_sems, recv_sems, local_sems):
        x, y, c = lax.axis_index("x"), lax.axis_index("y"), lax.axis_index("c")
        sibling = (x, y, 1 - c)

        def to_own(d, k):
            return pltpu.make_async_copy(slabs.at[d], own_ref.at[k], local_sems.at[k])

        def to_sibling(d, k):
            return pltpu.make_async_remote_copy(
                src_ref=slabs.at[d], dst_ref=got_ref.at[k], send_sem=send_sems.at[k], recv_sem=recv_sems.at[k],
                device_id=sibling, device_id_type=MESH)

        for d in range(N_DEV):
            for (t, _, s, n) in by_slab[d]:
                slabs[d, s // 2:(s + n) // 2] = g_ref[t:t + n, :].reshape(n // 2, 2 * ROW_TILES, 128)
            k, side = d // 2, d % 2
            pl.when(c == side)(to_own(d, k).start)
            pl.when(c != side)(to_sibling(d, k).start)
        for k in range(N_CHIP):
            to_own(0, k).wait()
            to_sibling(0, k).wait()

    half = SDS((N_CHIP,) + slab, g.dtype)
    return pl.pallas_call(
        body, in_specs=[pl.BlockSpec(memory_space=pltpu.VMEM)], out_specs=[ANY, ANY], out_shape=[half, half],
        scratch_shapes=[pltpu.VMEM((N_DEV,) + slab, g.dtype), pltpu.SemaphoreType.DMA((N_CHIP,)),
                        pltpu.SemaphoreType.DMA((N_CHIP,)), pltpu.SemaphoreType.DMA((N_CHIP,))],
        compiler_params=_cparams(), name="from_aligned_pair")(g)


_SEGS = [
    (R_Q, C_Q, 1024), (R_K, C_K, 256), (R_V, C_V, 256), (R_ZA, C_ZA, 1024), (R_ZS, C_ZS, 2048),
    (R_XBC, C_XBC, 3072), (R_DT, C_DT, 32), (R_GA, C_GA, 1024), (R_GS, C_GS, 1024)]


def _pad_lanes(v, n=128):
    return jnp.pad(v, ((0, 0), (0, n - v.shape[1])))


def _device_step(h, tgt, w_alt, w_out, g_pre, conv_w8, conv_b, dt_bias, a_log, d_skip, sinks, g_ssm, g_post, on_mesh):
    dtb, al, dsk, snk = _pad_lanes(dt_bias), _pad_lanes(a_log), _pad_lanes(d_skip), _pad_lanes(sinks)
    u = _norm_u(h, g_pre)
    proj = _matmul(u, w_alt, "nt", F32, T, 896, "in_proj")
    o = _attn_fwd(proj, snk)
    xbc_act = _conv_fwd(proj, conv_w8, conv_b)
    if on_mesh:
        sn, states, att_all, ssm_all, o_all = _ssd_fwd(xbc_act, proj, dtb, al, dsk, g_ssm, gather=w_out)
        w_att = att_all.reshape(D_MODEL, D_MODEL)
        w_ssm = ssm_all.reshape(SSM_INNER, D_MODEL)
        w_o = o_all.reshape(D_MODEL, D_MODEL)
    else:
        sn, states = _ssd_fwd(xbc_act, proj, dtb, al, dsk, g_ssm)
        w_att, w_ssm, w_o = w_out
    a_in, mg, ya, ys, out = _post_a(o, proj, sn, w_att, w_ssm, w_o)
    (loss, dres, dout, dya, dys, dga, dgs, do, dza, dsn, dgp) = _post_b(
        out, h, tgt, proj, ya, ys, o, g_post, w_att, w_ssm, w_o)
    dw_att = _matmul(a_in, dya, "tn", BF16, D_MODEL, D_MODEL, "d_w_att")
    dw_ssm = _matmul(sn, dys, "tn", BF16, D_MODEL, D_MODEL, "d_w_ssm")
    dw_o = _matmul(mg, dout, "tn", BF16, D_MODEL, D_MODEL, "d_w_o")
    res = {}
    if on_mesh:
        parts = [dw_att.reshape(N_DEV, 128, D_MODEL), dw_ssm.reshape(N_DEV, 256, D_MODEL),
                 dw_o.reshape(N_DEV, 128, D_MODEL)]
        (dxs, dbm, dcm, ddt4, dzs, ddtb, dal, ddsk, dgn, res["r_att"], res["r_ssm"], res["r_o"]) = _ssd_bwd(
            xbc_act, proj, dtb, al, dsk, g_ssm, states, dsn, exchange=parts)
    else:
        dxs, dbm, dcm, ddt4, dzs, ddtb, dal, ddsk, dgn = _ssd_bwd(xbc_act, proj, dtb, al, dsk, g_ssm, states, dsn)
        res.update(dw_att=dw_att, dw_ssm=dw_ssm, dw_o=dw_o)
    dxx, dwx, dbx = _conv_bwd(proj, conv_w8, conv_b, dxs, 0, "conv_bwd_x")
    dxb, dwb, dbb = _conv_bwd(proj, conv_w8, conv_b, dbm, SSM_INNER, "conv_bwd_b")
    dxc, dwc, dbc = _conv_bwd(proj, conv_w8, conv_b, dcm, SSM_INNER + GRP_W, "conv_bwd_c")
    dq, dk, dv, dsink = _attn_bwd(proj, snk, do)
    dproj = _assemble(dq, dza, dga, dgs, dzs, dxx, dxb, dxc, dk, dv, ddt4)
    dw_alt = _matmul(dproj, u, "tn", BF16, 896, D_MODEL, "d_w_in")
    if on_mesh:
        own, got = _from_aligned_pair(dw_alt)
        dh, dgpre, res["r_in"] = _d_u_norm(dproj, w_alt, h, g_pre, dres,
                                           chips=_pair_sum([own], [got], "pair_sum_w_in"))
    else:
        dh, dgpre = _d_u_norm(dproj, w_alt, h, g_pre, dres)
        res["dw_alt"] = dw_alt
    small = (dgpre, dbx, dbb, dbc, ddtb, dal, ddsk, dsink, dgn, dgp, dwx, dwb, dwc)
    if on_mesh:
        res["small_pack"] = _small_pack(*small, dh)
    else:
        res["small"] = small
    res.update(loss=loss[0, 0], dh=dh)
    return res


def kernel(x, meta_tokens, g_pre, w_in, conv_w, conv_b, dt_bias, a_log, d_skip, attn_sinks, g_ssm_norm, w_out_att, w_out_ssm, w_out, g_post, loss_target, m_meta_tokens, m_g_pre, m_w_in, m_conv_w, m_conv_b, m_dt_bias, m_a_log, m_d_skip, m_attn_sinks, m_g_ssm_norm, m_w_out_att, m_w_out_ssm, m_w_out, m_g_post, v_meta_tokens, v_g_pre, v_w_in, v_conv_w, v_conv_b, v_dt_bias, v_a_log, v_d_skip, v_attn_sinks, v_g_ssm_norm, v_w_out_att, v_w_out_ssm, v_w_out, v_g_post):
    w_in3, m_in3, v_in3 = _rows3(w_in), _rows3(m_w_in), _rows3(v_w_in)
    a_sh, att_sh, ssm_sh, o_sh = _cast_shards(w_in3, w_out_att[0], w_out_ssm[0], w_out[0])
    cw_sh = jnp.pad(conv_w[0], ((0, 4), (0, 0)))
    a_all, meta_all, cw_all = _all_gather([a_sh, meta_tokens, cw_sh])
    w_alt = _to_aligned_t(a_all)
    meta_full = meta_all.transpose(1, 0, 2).reshape(N_META, D_MODEL)
    conv_w8 = cw_all.transpose(1, 0, 2).reshape(8, CONV_DIM)

    h = jnp.concatenate([jnp.zeros((PAD, D_MODEL), F32), meta_full, x[0]], axis=0)
    tgt = jnp.concatenate([jnp.zeros((PAD + N_META, D_MODEL), F32), loss_target[0]], axis=0)
    r = _device_step(h, tgt, w_alt, (att_sh, ssm_sh, o_sh), g_pre, conv_w8, conv_b, dt_bias, a_log, d_skip,
                     attn_sinks, g_ssm_norm, g_post, True)
    loss = lax.psum(r["loss"], ("x", "y", "c"))
    grad_x = r["dh"][PAD + N_META:][None]

    *res_in, r_small = _sum_adamw_rows3(r["r_in"], w_in3, m_in3, v_in3, "adamw_w_in", exchange=[r["small_pack"]])
    res_in = [_unrows3(t) for t in res_in]
    res_att = [t[None] for t in _sum_adamw(r["r_att"], w_out_att[0], m_w_out_att[0], v_w_out_att[0], 512,
                                           "adamw_w_att")]
    res_ssm = [t[None] for t in _sum_adamw(r["r_ssm"], w_out_ssm[0], m_w_out_ssm[0], v_w_out_ssm[0], 512,
                                           "adamw_w_ssm")]
    res_o = [t[None] for t in _sum_adamw(r["r_o"], w_out[0], m_w_out[0], v_w_out[0], 512, "adamw_w_o")]
    (res_gpre, res_convb, res_dtb, res_alog, res_dskip, res_sink, res_gssm, res_gpost, res_cw, res_meta) = _small_finish(
        r_small, [(g_pre, m_g_pre, v_g_pre), (conv_b, m_conv_b, v_conv_b), (dt_bias, m_dt_bias, v_dt_bias),
                       (a_log, m_a_log, v_a_log), (d_skip, m_d_skip, v_d_skip),
                       (attn_sinks, m_attn_sinks, v_attn_sinks), (g_ssm_norm, m_g_ssm_norm, v_g_ssm_norm),
                       (g_post, m_g_post, v_g_post), (conv_w[0], m_conv_w[0], v_conv_w[0]),
                       (meta_tokens, m_meta_tokens, v_meta_tokens)])
    res_cw = [t[None] for t in res_cw]
    per_weight = [res_meta, res_gpre, res_in, res_cw, res_convb, res_dtb, res_alog, res_dskip, res_sink, res_gssm,
                  res_att, res_ssm, res_o, res_gpost]
    return (loss, grad_x, *[p[0] for p in per_weight], *[p[1] for p in per_weight], *[p[2] for p in per_weight],
            *[p[3] for p in per_weight])
```

```python
import functools
import math

import jax
import jax.numpy as jnp
from jax import lax
from jax.experimental import pallas as pl
from jax.experimental.pallas import tpu as pltpu

F32 = jnp.float32
BF16 = jnp.bfloat16
SDS = jax.ShapeDtypeStruct
MESH = pl.DeviceIdType.MESH
ANY = pl.BlockSpec(memory_space=pl.ANY)

N_DEV = 8
D_MODEL = 1024
SEQ = 2048
N_META = 16
BLK = 128
PAD = 112
T = PAD + N_META + SEQ
NB = T // BLK
EPS = 1e-6
HEAD = 64
Q_HEADS = 16
KV_HEADS = 4
GROUP = 4
KV_W = 256
SSM_INNER = 2048
SSM_HEADS = 32
SSM_GROUPS = 4
GRP_W = 512
SSM_STATE = 128
CONV_DIM = 3072
IN_PROJ = 9760
SHARD_IN = IN_PROJ // N_DEV
NEG = -1e30

C_Q, C_ZA, C_GA, C_GS, C_ZS, C_XBC, C_K, C_V, C_DT = 0, 1024, 2048, 3072, 4096, 6144, 9216, 9472, 9728
PW = 9856
R_Q, R_K, R_V, R_ZA, R_ZS, R_XBC, R_DT, R_GA, R_GS = 0, 1024, 1280, 1536, 2560, 4608, 7680, 7712, 8736

ADAM_LR, ADAM_B1, ADAM_B2, ADAM_EPS, ADAM_WD, ADAM_STEP = 0.001, 0.9, 0.999, 1e-08, 0.01, 10

VMEM_LIMIT = 56 * 1024 * 1024


def _cparams():
    return pltpu.CompilerParams(vmem_limit_bytes=VMEM_LIMIT)


def _silu(x):
    return x * jax.nn.sigmoid(x)


def _dsilu(x):
    s = jax.nn.sigmoid(x)
    return s * (1.0 + x * (1.0 - s))


def _matmul(a, b, mode, out_dtype, tm, tn, name):
    if mode == "nt":
        (m, k), n = a.shape, b.shape[0]
        a_spec = pl.BlockSpec((tm, k), lambda i, j: (i, 0))
        b_spec = pl.BlockSpec((tn, k), lambda i, j: (j, 0))
        dims = (((1,), (1,)), ((), ()))
    else:
        assert mode == "tn"
        (k, m), n = a.shape, b.shape[1]
        a_spec = pl.BlockSpec((k, tm), lambda i, j: (0, i))
        b_spec = pl.BlockSpec((k, tn), lambda i, j: (0, j))
        dims = (((0,), (0,)), ((), ()))
    assert m % tm == 0 and n % tn == 0, (a.shape, b.shape, tm, tn)

    def body(a_ref, b_ref, o_ref):
        o_ref[...] = lax.dot_general(a_ref[...], b_ref[...], dims, preferred_element_type=F32).astype(out_dtype)

    return pl.pallas_call(
        body, grid=(m // tm, n // tn), in_specs=[a_spec, b_spec],
        out_specs=pl.BlockSpec((tm, tn), lambda i, j: (i, j)), out_shape=SDS((m, n), out_dtype),
        compiler_params=_cparams(), name=name)(a, b)


def _norm_u(h, g_pre):
    def body(h_ref, g_ref, u_ref):
        x = h_ref[...]
        r = lax.rsqrt(jnp.mean(x * x, axis=-1, keepdims=True) + EPS)
        u_ref[...] = (x * r * g_ref[...]).astype(BF16)

    return pl.pallas_call(
        body, grid=(NB,),
        in_specs=[pl.BlockSpec((BLK, D_MODEL), lambda i: (i, 0)), pl.BlockSpec((1, D_MODEL), lambda i: (0, 0))],
        out_specs=pl.BlockSpec((BLK, D_MODEL), lambda i: (i, 0)),
        out_shape=SDS((T, D_MODEL), BF16), name="norm_u")(h, g_pre)


DU_TM, DU_TK = T // 2, 1408


def _d_u_norm(dproj, w_alt, h, g_pre, dres, chips=()):
    nk = PW // DU_TK
    ni = T // DU_TM
    nc = len(chips)

    def body(*refs):
        a_ref, b_ref, h_ref, g_ref, dres_ref = refs[:5]
        dh_ref, dg_ref = refs[5 + nc:7 + nc]
        acc_ref = refs[7 + 2 * nc]
        i, kk = pl.program_id(0), pl.program_id(1)
        if nc:
            ch_start, ch_finish = _chips_program(refs[5:5 + nc], refs[7 + nc:7 + 2 * nc], refs[8 + 2 * nc:])
            pl.when((i == 0) & (kk == 0))(ch_start)
        part = jnp.dot(a_ref[...], b_ref[...], preferred_element_type=F32)

        @pl.when(kk == 0)
        def _():
            acc_ref[...] = part

        @pl.when((kk > 0) & (kk < nk - 1))
        def _():
            acc_ref[...] += part

        @pl.when(kk == nk - 1)
        def _():
            du_ = acc_ref[...] + part
            x = h_ref[...]
            r = lax.rsqrt(jnp.mean(x * x, axis=-1, keepdims=True) + EPS)
            gd = g_ref[...] * du_
            dx = r * gd - x * (r * r * r) * jnp.mean(x * gd, axis=-1, keepdims=True)
            dh_ref[...] = dx + dres_ref[...]
            gpart = jnp.concatenate([jnp.sum(du_ * x * r, axis=0, keepdims=True), jnp.zeros((7, D_MODEL), F32)],
                                    axis=0)

            @pl.when(i == 0)
            def _():
                dg_ref[...] = gpart

            @pl.when(i > 0)
            def _():
                dg_ref[...] += gpart

        if nc:
            pl.when((i == ni - 1) & (kk == nk - 1))(ch_finish)

    row = pl.BlockSpec((DU_TM, D_MODEL), lambda i, kk: (i, 0))
    return pl.pallas_call(
        body, grid=(ni, nk),
        in_specs=[pl.BlockSpec((DU_TM, DU_TK), lambda i, kk: (i, kk)),
                  pl.BlockSpec((DU_TK, D_MODEL), lambda i, kk: (kk, 0)),
                  row, pl.BlockSpec((1, D_MODEL), lambda i, kk: (0, 0)), row] + [ANY] * nc,
        out_specs=[row, pl.BlockSpec((8, D_MODEL), lambda i, kk: (0, 0))] + [ANY] * nc,
        out_shape=[SDS((T, D_MODEL), F32), SDS((8, D_MODEL), F32)] + [SDS(p.shape, p.dtype) for p in chips],
        scratch_shapes=[pltpu.VMEM((DU_TM, D_MODEL), F32)] + (_chips_scratch(chips) if nc else []),
        compiler_params=_cparams(), name="d_u_norm")(dproj, w_alt, h, g_pre, dres, *chips)


def _lane_pick(row, h):
    lane = lax.broadcasted_iota(jnp.int32, row.shape, 1)
    return jnp.sum(jnp.where(lane == h, row, 0.0), axis=1, keepdims=True)


def _attn_fn(q4s, kcats, vcats, kms, vms, sinks, n):
    r = lax.broadcasted_iota(jnp.int32, (GROUP * BLK, 2 * BLK), 0)
    s = lax.broadcasted_iota(jnp.int32, (GROUP * BLK, 2 * BLK), 1)
    i = jnp.bitwise_and(r, BLK - 1)
    gi = jnp.right_shift(r, 7)
    rel = i - s + BLK
    k_pos = n * BLK - BLK + s
    band_ok = (rel >= 0) & (rel < BLK) & (k_pos >= PAD + N_META)
    relf = rel.astype(F32)
    rm = lax.broadcasted_iota(jnp.int32, (GROUP * BLK, N_META), 0)
    mm = lax.broadcasted_iota(jnp.int32, (GROUP * BLK, N_META), 1)
    meta_ok = (PAD + mm) <= (n * BLK + jnp.bitwise_and(rm, BLK - 1))
    gcol = jnp.right_shift(lax.broadcasted_iota(jnp.int32, (GROUP * BLK, 1), 0), 7)
    outs = []
    for kh in range(KV_HEADS):
        slopes = [2.0 ** (-8.0 * (kh * GROUP + g + 1) / Q_HEADS) for g in range(GROUP)]
        slope = jnp.where(gi == 0, slopes[0], jnp.where(gi == 1, slopes[1], jnp.where(gi == 2, slopes[2], slopes[3])))
        sk = [_lane_pick(sinks, kh * GROUP + g) for g in range(GROUP)]
        sink = jnp.where(gcol == 0, sk[0], jnp.where(gcol == 1, sk[1], jnp.where(gcol == 2, sk[2], sk[3])))
        qb = (q4s[kh] * (HEAD ** -0.5)).astype(BF16)
        sb = lax.dot_general(qb, kcats[kh].astype(BF16), (((1,), (1,)), ((), ())), preferred_element_type=F32)
        sb = jnp.where(band_ok, sb - slope * relf, NEG)
        sm = lax.dot_general(qb, kms[kh].astype(BF16), (((1,), (1,)), ((), ())), preferred_element_type=F32)
        sm = jnp.where(meta_ok, sm, NEG)
        mx = jnp.maximum(jnp.maximum(jnp.max(sb, axis=1, keepdims=True), jnp.max(sm, axis=1, keepdims=True)), sink)
        mx = lax.stop_gradient(mx)
        eb = jnp.exp(sb - mx)
        em = jnp.exp(sm - mx)
        es = jnp.exp(sink - mx)
        inv = 1.0 / (jnp.sum(eb, axis=1, keepdims=True) + jnp.sum(em, axis=1, keepdims=True) + es)
        pb = (eb * inv).astype(BF16)
        pm = (em * inv).astype(BF16)
        o4 = (jnp.dot(pm, vms[kh].astype(BF16), preferred_element_type=F32)
              + jnp.dot(pb, vcats[kh].astype(BF16), preferred_element_type=F32))
        outs.append(o4)
    return outs


def _attn_specs():
    prev = lambda n: jnp.maximum(n - 1, 0)
    return [
        pl.BlockSpec((BLK, D_MODEL), lambda n: (n, C_Q // D_MODEL)),
        pl.BlockSpec((BLK, KV_W), lambda n: (prev(n), C_K // KV_W)),
        pl.BlockSpec((BLK, KV_W), lambda n: (n, C_K // KV_W)),
        pl.BlockSpec((BLK, KV_W), lambda n: (prev(n), C_V // KV_W)),
        pl.BlockSpec((BLK, KV_W), lambda n: (n, C_V // KV_W)),
        pl.BlockSpec((N_META, KV_W), lambda n: (PAD // N_META, C_K // KV_W)),
        pl.BlockSpec((N_META, KV_W), lambda n: (PAD // N_META, C_V // KV_W)),
        pl.BlockSpec((1, 128), lambda n: (0, 0)),
    ]


def _attn_load(q_ref, kp_ref, kc_ref, vp_ref, vc_ref, km_ref, vm_ref):
    q4s, kcats, vcats, kms, vms = [], [], [], [], []
    for kh in range(KV_HEADS):
        q4s.append(jnp.concatenate(
            [q_ref[:, (kh * GROUP + g) * HEAD:(kh * GROUP + g + 1) * HEAD] for g in range(GROUP)], axis=0))
        cs = slice(kh * HEAD, (kh + 1) * HEAD)
        kcats.append(jnp.concatenate([kp_ref[:, cs], kc_ref[:, cs]], axis=0))
        vcats.append(jnp.concatenate([vp_ref[:, cs], vc_ref[:, cs]], axis=0))
        kms.append(km_ref[:, cs])
        vms.append(vm_ref[:, cs])
    return q4s, kcats, vcats, kms, vms


def _attn_fwd(proj, sinks):
    def body(q_ref, kp_ref, kc_ref, vp_ref, vc_ref, km_ref, vm_ref, s_ref, o_ref):
        n = pl.program_id(0)
        args = _attn_load(q_ref, kp_ref, kc_ref, vp_ref, vc_ref, km_ref, vm_ref)
        outs = _attn_fn(*args, s_ref[...], n)
        for kh in range(KV_HEADS):
            for g in range(GROUP):
                hh = kh * GROUP + g
                o_ref[:, hh * HEAD:(hh + 1) * HEAD] = outs[kh][g * BLK:(g + 1) * BLK]

    return pl.pallas_call(
        body, grid=(NB,), in_specs=_attn_specs(),
        out_specs=pl.BlockSpec((BLK, D_MODEL), lambda n: (n, 0)),
        out_shape=SDS((T, D_MODEL), F32), name="attn_fwd")(proj, proj, proj, proj, proj, proj, proj, sinks)


def _attn_bwd(proj, sinks, do):
    def body(q_ref, kp_ref, kc_ref, vp_ref, vc_ref, km_ref, vm_ref, s_ref, do_ref, dq_ref, dk_ref, dv_ref, ds_ref):
        n = pl.program_id(0)

        @pl.when(n == 0)
        def _():
            dk_ref[...] = jnp.zeros_like(dk_ref)
            dv_ref[...] = jnp.zeros_like(dv_ref)
            ds_ref[...] = jnp.zeros_like(ds_ref)

        args = _attn_load(q_ref, kp_ref, kc_ref, vp_ref, vc_ref, km_ref, vm_ref)
        _, vjp = jax.vjp(lambda a, b, c, d, e, f: _attn_fn(a, b, c, d, e, f, n), *args, s_ref[...])
        do_f = do_ref[...].astype(F32)
        cot = [jnp.concatenate([do_f[:, (kh * GROUP + g) * HEAD:(kh * GROUP + g + 1) * HEAD] for g in range(GROUP)],
                               axis=0) for kh in range(KV_HEADS)]
        dq4s, dkcats, dvcats, dkms, dvms, dsk = vjp(cot)
        ds_ref[0:1, :] += dsk
        cur = pl.ds(pl.multiple_of(n * BLK, BLK), BLK)
        meta = slice(PAD, PAD + N_META)
        for kh in range(KV_HEADS):
            cs = slice(kh * HEAD, (kh + 1) * HEAD)
            for g in range(GROUP):
                hh = kh * GROUP + g
                dq_ref[:, hh * HEAD:(hh + 1) * HEAD] = dq4s[kh][g * BLK:(g + 1) * BLK]
            dk_ref[cur, cs] += dkcats[kh][BLK:]
            dv_ref[cur, cs] += dvcats[kh][BLK:]
            dk_ref[meta, cs] += dkms[kh]
            dv_ref[meta, cs] += dvms[kh]

        @pl.when(n > 0)
        def _():
            prv = pl.ds(pl.multiple_of((n - 1) * BLK, BLK), BLK)
            for kh in range(KV_HEADS):
                cs = slice(kh * HEAD, (kh + 1) * HEAD)
                dk_ref[prv, cs] += dkcats[kh][:BLK]
                dv_ref[prv, cs] += dvcats[kh][:BLK]

    full_kv = pl.BlockSpec((T, KV_W), lambda n: (0, 0))
    return pl.pallas_call(
        body, grid=(NB,),
        in_specs=_attn_specs() + [pl.BlockSpec((BLK, D_MODEL), lambda n: (n, 0))],
        out_specs=[pl.BlockSpec((BLK, D_MODEL), lambda n: (n, 0)), full_kv, full_kv,
                   pl.BlockSpec((8, 128), lambda n: (0, 0))],
        out_shape=[SDS((T, D_MODEL), F32), SDS((T, KV_W), F32), SDS((T, KV_W), F32), SDS((8, 128), F32)],
        name="attn_bwd")(proj, proj, proj, proj, proj, proj, proj, sinks, do)


def _conv_taps(xp, w, rows):
    return (w[0:1] * xp[5:5 + rows] + w[1:2] * xp[6:6 + rows] + w[2:3] * xp[7:7 + rows] + w[3:4] * xp[8:8 + rows])


def _conv_fwd(proj, conv_w, conv_b):
    CONV_CB = CONV_DIM
    ncb = CONV_DIM // CONV_CB
    cb0 = C_XBC // CONV_CB

    def body(tail_ref, cur_ref, w_ref, b_ref, o_ref):
        n = pl.program_id(1)
        tail = jnp.where(n > 0, tail_ref[...], 0.0)
        xp = jnp.concatenate([tail, cur_ref[...]], axis=0)
        conv = _conv_taps(xp, w_ref[...], BLK) + b_ref[...]
        row = n * BLK + lax.broadcasted_iota(jnp.int32, (BLK, 1), 0)
        o_ref[...] = jnp.where(row >= PAD, _silu(conv), 0.0)

    return pl.pallas_call(
        body, grid=(ncb, NB),
        in_specs=[pl.BlockSpec((8, CONV_CB), lambda j, n: (jnp.maximum(n * (BLK // 8) - 1, 0), cb0 + j)),
                  pl.BlockSpec((BLK, CONV_CB), lambda j, n: (n, cb0 + j)),
                  pl.BlockSpec((8, CONV_CB), lambda j, n: (0, j)),
                  pl.BlockSpec((1, CONV_CB), lambda j, n: (0, j))],
        out_specs=pl.BlockSpec((BLK, CONV_CB), lambda j, n: (n, j)),
        out_shape=SDS((T, CONV_DIM), F32), name="conv_fwd")(proj, proj, conv_w, conv_b)


def _conv_bwd(proj, conv_w, conv_b, dact, ch0, name):
    width = dact.shape[1]
    CONV_CB = width
    ncb = width // CONV_CB
    cb0 = (C_XBC + ch0) // CONV_CB
    wb0 = ch0 // CONV_CB
    last8 = T // 8 - 1

    def body(tail_ref, cur_ref, nxt_ref, w_ref, b_ref, dcur_ref, dnxt_ref, dx_ref, dw_ref, db_ref):
        n = pl.program_id(1)
        w = w_ref[...]
        tail = jnp.where(n > 0, tail_ref[...], 0.0)
        xp = jnp.concatenate([tail, cur_ref[...], nxt_ref[...]], axis=0)
        conv = _conv_taps(xp, w, BLK + 8) + b_ref[...]
        dext = jnp.concatenate([dcur_ref[...], jnp.where(n < NB - 1, dnxt_ref[...], 0.0)], axis=0)
        row = n * BLK + lax.broadcasted_iota(jnp.int32, (BLK + 8, 1), 0)
        dconv = jnp.where(row >= PAD, dext * _dsilu(conv), 0.0)
        dx = (w[0:1] * dconv[3:3 + BLK] + w[1:2] * dconv[2:2 + BLK] + w[2:3] * dconv[1:1 + BLK]
              + w[3:4] * dconv[0:BLK])
        dx_ref[...] = dx.astype(BF16)
        dc = dconv[0:BLK]
        dws = [jnp.sum(dc * xp[5 + k:5 + k + BLK], axis=0, keepdims=True) for k in range(4)]
        dwp = jnp.concatenate(dws + [jnp.zeros((4, CONV_CB), F32)], axis=0)
        dbp = jnp.sum(dc, axis=0, keepdims=True)

        @pl.when(n == 0)
        def _():
            dw_ref[...] = dwp
            db_ref[...] = jnp.concatenate([dbp, jnp.zeros((7, CONV_CB), F32)], axis=0)

        @pl.when(n > 0)
        def _():
            dw_ref[...] += dwp
            db_ref[0:1, :] += dbp

    return pl.pallas_call(
        body, grid=(ncb, NB),
        in_specs=[pl.BlockSpec((8, CONV_CB), lambda j, n: (jnp.maximum(n * (BLK // 8) - 1, 0), cb0 + j)),
                  pl.BlockSpec((BLK, CONV_CB), lambda j, n: (n, cb0 + j)),
                  pl.BlockSpec((8, CONV_CB), lambda j, n: (jnp.minimum((n + 1) * (BLK // 8), last8), cb0 + j)),
                  pl.BlockSpec((8, CONV_CB), lambda j, n: (0, wb0 + j)),
                  pl.BlockSpec((1, CONV_CB), lambda j, n: (0, wb0 + j)),
                  pl.BlockSpec((BLK, CONV_CB), lambda j, n: (n, j)),
                  pl.BlockSpec((8, CONV_CB), lambda j, n: (jnp.minimum((n + 1) * (BLK // 8), last8), j))],
        out_specs=[pl.BlockSpec((BLK, CONV_CB), lambda j, n: (n, j)),
                   pl.BlockSpec((8, CONV_CB), lambda j, n: (0, j)),
                   pl.BlockSpec((8, CONV_CB), lambda j, n: (0, j))],
        out_shape=[SDS((T, width), BF16), SDS((8, width), F32), SDS((8, width), F32)],
        name=name)(proj, proj, proj, conv_w, conv_b, dact, dact)


HPG = SSM_HEADS // SSM_GROUPS


def _iota(shape, dim):
    return lax.broadcasted_iota(jnp.int32, shape, dim)


def _mm(a, b, ca=1, cb=0):
    return lax.dot_general(a.astype(BF16), b.astype(BF16), (((ca,), (cb,)), ((), ())), preferred_element_type=F32)


def _split3(v):
    hi = v.astype(BF16)
    r1 = v - hi.astype(F32)
    mid = r1.astype(BF16)
    lo = (r1 - mid.astype(F32)).astype(BF16)
    return hi, mid, lo


def _sel_r(parts, onehot, ca=1, cb=0):
    out = lax.dot_general(parts[0], onehot, (((ca,), (cb,)), ((), ())), preferred_element_type=F32)
    for p in parts[1:]:
        out = out + lax.dot_general(p, onehot, (((ca,), (cb,)), ((), ())), preferred_element_type=F32)
    return out


def _sel_l(onehot, parts):
    out = jnp.dot(onehot, parts[0], preferred_element_type=F32)
    for p in parts[1:]:
        out = out + jnp.dot(onehot, p, preferred_element_type=F32)
    return out


def _rows8(*rows):
    r = _iota((8, rows[0].shape[1]), 0)
    out = jnp.zeros((8, rows[0].shape[1]), F32)
    for k, v in enumerate(rows):
        out = jnp.where(r == k, v, out)
    return out


def _ssd_forward(x, z, bm, cm, dt_raw, st_prev, dtb, alog, dskip, gn, g, cst_scr):
    li, si = _iota((BLK, BLK), 0), _iota((BLK, BLK), 1)
    dt_all = jax.nn.softplus(dt_raw + dtb)
    a_row = -jnp.exp(alog)
    a_all = dt_all * a_row
    cs_all = _sel_l((li >= si).astype(BF16), _split3(a_all))
    cs_parts = _split3(cs_all)
    spread = (_iota((BLK, GRP_W), 0) == g * HPG + jnp.right_shift(_iota((BLK, GRP_W), 1), 6)).astype(BF16)
    dt_e = _sel_r(_split3(dt_all), spread)
    cs_e = _sel_r(cs_parts, spread)
    d_e = _sel_r(_split3(_rows8(dskip)), spread)[0:1]
    cs_last_e = jnp.sum(jnp.where(_iota((BLK, GRP_W), 0) == BLK - 1, cs_e, 0.0), axis=0, keepdims=True)
    p_e = jnp.exp(cs_e)
    w_e = jnp.exp(cs_last_e - cs_e)
    cd_e = jnp.exp(cs_last_e)
    xr = x * dt_e
    cst_scr[...] = cs_all.T
    cst_g = cst_scr[pl.ds(pl.multiple_of(g * HPG, HPG), HPG), :]
    own = jnp.right_shift(_iota((HPG, HPG * BLK), 1), 7) == _iota((HPG, HPG * BLK), 0)
    ownf = own.astype(F32)
    q_rows = [ownf, ownf, ownf] + [jnp.where(own, jnp.concatenate([p.astype(F32)] * HPG, axis=1), 0.0)
                                   for p in _split3(cst_g)]
    q2 = jnp.concatenate(q_rows + [jnp.zeros((BLK - 6 * HPG, HPG * BLK), F32)], axis=0).astype(BF16)
    lane1 = _iota((1, BLK), 1)
    p2 = jnp.where((lane1 >= 3 * HPG) & (lane1 < 6 * HPG), -1.0, 0.0)
    for k, part in enumerate(cs_parts):
        pick = ((li == g * HPG + si - k * HPG) & (si >= k * HPG) & (si < (k + 1) * HPG)).astype(BF16)
        p2 = p2 + jnp.dot(part, pick, preferred_element_type=F32)
    dmat = jnp.dot(p2.astype(BF16), q2, preferred_element_type=F32)
    causal = _iota((BLK, HPG * BLK), 0) >= jnp.bitwise_and(_iota((BLK, HPG * BLK), 1), BLK - 1)
    lam = jnp.exp(jnp.where(causal, dmat, NEG))
    gmat = _mm(cm, bm, 1, 1)
    m_all = lam * jnp.concatenate([gmat] * HPG, axis=1)
    mb = m_all.astype(BF16)
    lo = _iota((BLK, BLK), 1) < HEAD
    xrb = xr.astype(BF16)
    zero = jnp.zeros((BLK, BLK), BF16)
    bds, yd = [], []
    for i in range(HPG // 2):
        t = xrb[:, BLK * i:BLK * (i + 1)]
        bd = jnp.concatenate([jnp.where(lo, t, zero), jnp.where(lo, zero, t)], axis=0)
        bds.append(bd)
        yd.append(jnp.dot(mb[:, 2 * BLK * i:2 * BLK * (i + 1)], bd, preferred_element_type=F32))
    cs_st = _mm(cm, st_prev)
    y = jnp.concatenate(yd, axis=1) + cs_st * p_e + d_e * x
    xrw = xr * w_e
    st_new = cd_e * st_prev + _mm(bm, xrw, 0, 0)
    yz = y * _silu(z)
    rn = lax.rsqrt(jnp.sum(yz * yz, axis=1, keepdims=True) / GRP_W + EPS)
    return dict(out=yz * rn * gn, st_new=st_new, dt_all=dt_all, a_row=a_row, dt_e=dt_e, d_e=d_e, p_e=p_e, w_e=w_e,
                cd_e=cd_e, xr=xr, xrw=xrw, lam=lam, m_all=m_all, mb=mb, bds=bds, cs_st=cs_st, y=y, yz=yz, rn=rn, lo=lo)


def _ssd_backward(f, x, z, bm, cm, dt_raw, st_prev, dtb, gn, g, dout, dst_next, cst_scr):
    li, si = _iota((BLK, BLK), 0), _iota((BLK, BLK), 1)
    yz, rn, y, p_e, w_e, cd_e, xr = f["yz"], f["rn"], f["y"], f["p_e"], f["w_e"], f["cd_e"], f["xr"]
    dgn = jnp.sum(dout * yz * rn, axis=0, keepdims=True)
    t = dout * gn
    dyz = rn * t - yz * (rn * rn * rn) * (jnp.sum(yz * t, axis=1, keepdims=True) / GRP_W)
    dy = dyz * _silu(z)
    dz = dyz * y * _dsilu(z)
    dx = f["d_e"] * dy
    dd_e = jnp.sum(dy * x, axis=0, keepdims=True)
    dcsst = dy * p_e
    dp_e = dy * f["cs_st"]
    dcm = _mm(dcsst, st_prev, 1, 1)
    dst_prev = _mm(cm, dcsst, 0, 0) + cd_e * dst_next
    dcd_e = jnp.sum(dst_next * st_prev, axis=0, keepdims=True)
    dbm = _mm(f["xrw"], dst_next, 1, 1)
    dxrw = _mm(bm, dst_next)
    dxr = dxrw * w_e
    dw_e = dxrw * xr
    dyb = dy.astype(BF16)
    dms, dxr_d = [], []
    for i in range(HPG // 2):
        dyp = dyb[:, BLK * i:BLK * (i + 1)]
        dms.append(lax.dot_general(dyp, f["bds"][i], (((1,), (1,)), ((), ())), preferred_element_type=F32))
        r = lax.dot_general(f["mb"][:, 2 * BLK * i:2 * BLK * (i + 1)], dyp, (((0,), (0,)), ((), ())),
                            preferred_element_type=F32)
        dxr_d.append(jnp.where(f["lo"], r[0:BLK], r[BLK:2 * BLK]))
    dm_all = jnp.concatenate(dms, axis=1)
    dxr = dxr + jnp.concatenate(dxr_d, axis=1)
    dlg = dm_all * f["lam"]
    dg = dlg[:, 0:BLK]
    for j in range(1, HPG):
        dg = dg + dlg[:, BLK * j:BLK * (j + 1)]
    dcm = dcm + _mm(dg, bm)
    dbm = dbm + _mm(dg, cm, 0, 0)
    q_all = dm_all * f["m_all"]
    col_sums = jnp.sum(q_all, axis=0, keepdims=True)
    cst_scr[...] = jnp.zeros_like(cst_scr)
    cst_scr[pl.ds(pl.multiple_of(g * HPG, HPG), HPG), :] = _rows8(
        *[col_sums[:, BLK * j:BLK * (j + 1)] for j in range(HPG)])
    dcs = -cst_scr[...].T
    for j in range(HPG):
        dcs = dcs + jnp.where(si == g * HPG + j,
                              jnp.sum(q_all[:, BLK * j:BLK * (j + 1)], axis=1, keepdims=True), 0.0)
    unspread = (_iota((GRP_W, BLK), 1) == g * HPG + jnp.right_shift(_iota((GRP_W, BLK), 0), 6)).astype(BF16)
    dww = dw_e * w_e
    per_head = _sel_r(_split3(jnp.concatenate([dp_e * p_e - dww, dxr * x], axis=0)), unspread)
    last = _sel_r(_split3(_rows8(jnp.sum(dww, axis=0, keepdims=True) + dcd_e * cd_e, dd_e)), unspread)
    dcs = dcs + per_head[0:BLK] + jnp.where(li == BLK - 1, last[0:1], 0.0)
    da = _sel_l((si >= li).astype(BF16), _split3(dcs))
    ddt_all = da * f["a_row"] + per_head[BLK:2 * BLK]
    dalog = jnp.sum(da * f["dt_all"], axis=0, keepdims=True) * f["a_row"]
    dx = dx + dxr * f["dt_e"]
    ddt_raw = ddt_all * jax.nn.sigmoid(dt_raw + dtb)
    ddtb = jnp.sum(ddt_raw, axis=0, keepdims=True)
    ddskip = last[1:2]
    return dict(dx=dx, dz=dz, dbm=dbm, dcm=dcm, ddt_raw=ddt_raw, dst_prev=dst_prev, ddtb=ddtb, dalog=dalog,
                ddskip=ddskip, dgn=dgn)


GPS = 4
NPG = SSM_GROUPS // GPS


def _ssd_in_specs(rev):
    cidx = (lambda c: NB - 1 - c) if rev else (lambda c: c)
    wx, wb = GPS * GRP_W, GPS * SSM_STATE
    return [
        pl.BlockSpec((BLK, wx), lambda p, c: (cidx(c), p)),
        pl.BlockSpec((BLK, wb), lambda p, c: (cidx(c), SSM_INNER // wb + p)),
        pl.BlockSpec((BLK, wb), lambda p, c: (cidx(c), (SSM_INNER + SSM_GROUPS * SSM_STATE) // wb + p)),
        pl.BlockSpec((BLK, 128), lambda p, c: (cidx(c), C_DT // 128)),
        pl.BlockSpec((BLK, wx), lambda p, c: (cidx(c), C_ZS // wx + p)),
        pl.BlockSpec((1, 128), lambda p, c: (0, 0)),
        pl.BlockSpec((1, 128), lambda p, c: (0, 0)),
        pl.BlockSpec((1, 128), lambda p, c: (0, 0)),
        pl.BlockSpec((1, wx), lambda p, c: (0, p)),
    ]


def _grp(ref, i, w):
    return ref[:, i * w:(i + 1) * w]


def _ssd_fwd(xbc_act, proj, dt_bias, a_log, d_skip, g_norm, gather=()):
    ng = len(gather)

    def body(*refs):
        xs_ref, b_ref, c_ref, dt_ref, z_ref, dtb_ref, al_ref, dsk_ref, gn_ref = refs[:9]
        y_ref, st_ref = refs[9 + ng:11 + ng]
        s_scr, cst_scr = refs[11 + 2 * ng:13 + 2 * ng]
        p = pl.program_id(0)
        c = pl.program_id(1)
        if ng:
            ag_start, ag_forward, ag_finish = _ag_program(refs[9:9 + ng], refs[11 + ng:11 + 2 * ng],
                                                          refs[13 + 2 * ng:])
            pl.when((p == 0) & (c == 0))(ag_start)
            pl.when((p == NPG - 1) & (c == (3 * NB) // 4))(ag_forward)

        @pl.when(c == 0)
        def _():
            s_scr[...] = jnp.zeros_like(s_scr)

        for i in range(GPS):
            st_prev = s_scr[i]
            st_ref[i, 0] = st_prev
            f = _ssd_forward(_grp(xs_ref, i, GRP_W), _grp(z_ref, i, GRP_W), _grp(b_ref, i, SSM_STATE),
                             _grp(c_ref, i, SSM_STATE), dt_ref[...], st_prev, dtb_ref[...], al_ref[...],
                             dsk_ref[...], _grp(gn_ref, i, GRP_W), p * GPS + i, cst_scr.at[i])
            y_ref[:, i * GRP_W:(i + 1) * GRP_W] = f["out"].astype(BF16)
            s_scr[i] = f["st_new"]
        if ng:
            pl.when((p == NPG - 1) & (c == NB - 1))(ag_finish)

    return pl.pallas_call(
        body, grid=(NPG, NB), in_specs=_ssd_in_specs(False) + [ANY] * ng,
        out_specs=[pl.BlockSpec((BLK, GPS * GRP_W), lambda p, c: (c, p)),
                   pl.BlockSpec((GPS, 1, SSM_STATE, GRP_W), lambda p, c: (p, c, 0, 0))] + [ANY] * ng,
        out_shape=[SDS((T, SSM_INNER), BF16), SDS((SSM_GROUPS, NB, SSM_STATE, GRP_W), F32)]
        + [SDS((N_DEV,) + s.shape, s.dtype) for s in gather],
        scratch_shapes=[pltpu.VMEM((GPS, SSM_STATE, GRP_W), F32), pltpu.VMEM((GPS, BLK, BLK), F32)]
        + (_ag_scratch(gather) if ng else []),
        compiler_params=_cparams(),
        name="ssd_fwd")(xbc_act, xbc_act, xbc_act, proj, proj, dt_bias, a_log, d_skip, g_norm, *gather)


def _ssd_bwd(xbc_act, proj, dt_bias, a_log, d_skip, g_norm, states, dy, exchange=()):
    chips = exchange
    nc = len(chips)

    def body(*refs):
        xs_ref, b_ref, c_ref, dt_ref, z_ref, dtb_ref, al_ref, dsk_ref, gn_ref, st_ref, dy_ref = refs[:11]
        (dxs_ref, db_ref, dc_ref, ddt_ref, dz_ref, ddtb_ref, dal_ref, ddsk_ref, dgn_ref) = refs[11 + nc:20 + nc]
        ds_scr, cst_scr = refs[20 + 2 * nc:22 + 2 * nc]
        p = pl.program_id(0)
        c = pl.program_id(1)
        if nc:
            ch_start, ch_finish = _direct_program(refs[11:11 + nc], refs[20 + nc:20 + 2 * nc], refs[22 + 2 * nc:])
            pl.when((p == 0) & (c == 0))(ch_start)

        @pl.when(c == 0)
        def _():
            ds_scr[...] = jnp.zeros_like(ds_scr)
            dgn_ref[...] = jnp.zeros_like(dgn_ref)

        @pl.when((c == 0) & (p == 0))
        def _():
            ddtb_ref[...] = jnp.zeros_like(ddtb_ref)
            dal_ref[...] = jnp.zeros_like(dal_ref)
            ddsk_ref[...] = jnp.zeros_like(ddsk_ref)

        dt_raw = dt_ref[...]
        for i in range(GPS):
            g = p * GPS + i
            x, z, gn = _grp(xs_ref, i, GRP_W), _grp(z_ref, i, GRP_W), _grp(gn_ref, i, GRP_W)
            bm, cm, st_prev = _grp(b_ref, i, SSM_STATE), _grp(c_ref, i, SSM_STATE), st_ref[i, 0]
            f = _ssd_forward(x, z, bm, cm, dt_raw, st_prev, dtb_ref[...], al_ref[...], dsk_ref[...], gn, g,
                             cst_scr.at[i])
            d = _ssd_backward(f, x, z, bm, cm, dt_raw, st_prev, dtb_ref[...], gn, g,
                              _grp(dy_ref, i, GRP_W).astype(F32), ds_scr[i], cst_scr.at[i])
            dxs_ref[:, i * GRP_W:(i + 1) * GRP_W] = d["dx"]
            dz_ref[:, i * GRP_W:(i + 1) * GRP_W] = d["dz"].astype(BF16)
            ds_scr[i] = d["dst_prev"]
            db_ref[:, i * SSM_STATE:(i + 1) * SSM_STATE] = d["dbm"]
            dc_ref[:, i * SSM_STATE:(i + 1) * SSM_STATE] = d["dcm"]
            ddt_ref[:, i * 128:(i + 1) * 128] = d["ddt_raw"]
            dgn_ref[0:1, i * GRP_W:(i + 1) * GRP_W] += d["dgn"]
            ddtb_ref[0:1, :] += d["ddtb"]
            dal_ref[0:1, :] += d["dalog"]
            ddsk_ref[0:1, :] += d["ddskip"]
        if nc:
            pl.when((p == NPG - 1) & (c == NB - 1))(ch_finish)

    rc = lambda c: NB - 1 - c
    small = pl.BlockSpec((8, 128), lambda p, c: (0, 0))
    wx, wb = GPS * GRP_W, GPS * SSM_STATE
    return pl.pallas_call(
        body, grid=(NPG, NB),
        in_specs=_ssd_in_specs(True) + [
            pl.BlockSpec((GPS, 1, SSM_STATE, GRP_W), lambda p, c: (p, rc(c), 0, 0)),
            pl.BlockSpec((BLK, wx), lambda p, c: (rc(c), p))] + [ANY] * nc,
        out_specs=[pl.BlockSpec((BLK, wx), lambda p, c: (rc(c), p)),
                   pl.BlockSpec((BLK, wb), lambda p, c: (rc(c), p)),
                   pl.BlockSpec((BLK, wb), lambda p, c: (rc(c), p)),
                   pl.BlockSpec((BLK, GPS * 128), lambda p, c: (rc(c), p)),
                   pl.BlockSpec((BLK, wx), lambda p, c: (rc(c), p)),
                   small, small, small,
                   pl.BlockSpec((8, wx), lambda p, c: (0, p))] + [ANY] * nc,
        out_shape=[SDS((T, SSM_INNER), F32), SDS((T, GRP_W), F32), SDS((T, GRP_W), F32), SDS((T, GRP_W), F32),
                   SDS((T, SSM_INNER), BF16), SDS((8, 128), F32), SDS((8, 128), F32), SDS((8, 128), F32),
                   SDS((8, SSM_INNER), F32)] + [SDS(p.shape, p.dtype) for p in chips],
        scratch_shapes=[pltpu.VMEM((GPS, SSM_STATE, GRP_W), F32), pltpu.VMEM((GPS, BLK, BLK), F32)]
        + (_direct_scratch(chips) if nc else []),
        compiler_params=_cparams(),
        name="ssd_bwd")(xbc_act, xbc_act, xbc_act, proj, proj, dt_bias, a_log, d_skip, g_norm, states, dy, *chips)


POST_R = 272


def _post_a(o, proj, sn, w_att, w_ssm, w_o):
    def body(o_ref, za_ref, ga_ref, gs_ref, sn_ref, wa_ref, ws_ref, wo_ref, a_ref, mg_ref, ya_ref, ys_ref, out_ref):
        a = (o_ref[...] * _silu(za_ref[...])).astype(BF16)
        a_ref[...] = a
        ya = jnp.dot(a, wa_ref[...], preferred_element_type=F32)
        ys = jnp.dot(sn_ref[...], ws_ref[...], preferred_element_type=F32)
        ya_ref[...] = ya.astype(BF16)
        ys_ref[...] = ys.astype(BF16)
        mg = (jax.nn.sigmoid(ga_ref[...]) * ya + jax.nn.sigmoid(gs_ref[...]) * ys).astype(BF16)
        mg_ref[...] = mg
        out_ref[...] = jnp.dot(mg, wo_ref[...], preferred_element_type=F32)

    row = pl.BlockSpec((POST_R, D_MODEL), lambda i: (i, 0))
    pcol = lambda c0: pl.BlockSpec((POST_R, D_MODEL), lambda i: (i, c0 // D_MODEL))
    full = lambda r: pl.BlockSpec((r, D_MODEL), lambda i: (0, 0))
    return pl.pallas_call(
        body, grid=(T // POST_R,),
        in_specs=[row, pcol(C_ZA), pcol(C_GA), pcol(C_GS), pl.BlockSpec((POST_R, SSM_INNER), lambda i: (i, 0)),
                  full(D_MODEL), full(SSM_INNER), full(D_MODEL)],
        out_specs=[row, row, row, row, row],
        out_shape=[SDS((T, D_MODEL), BF16), SDS((T, D_MODEL), BF16), SDS((T, D_MODEL), BF16), SDS((T, D_MODEL), BF16),
                   SDS((T, D_MODEL), F32)],
        compiler_params=_cparams(), name="post_a")(o, proj, proj, proj, sn, w_att, w_ssm, w_o)


def _post_b(out, h, tgt, proj, ya, ys, o, g_post, w_att, w_ssm, w_o):
    def body(out_ref, h_ref, t_ref, za_ref, ga_ref, gs_ref, ya_ref, ys_ref, o_ref, gp_ref, wa_ref, ws_ref, wo_ref,
             loss_ref, dres_ref, dout_ref, dya_ref, dys_ref, dga_ref, dgs_ref, do_ref, dza_ref, dsn_ref, dgp_ref):
        i = pl.program_id(0)
        x = out_ref[...]
        gp = gp_ref[...]
        r = lax.rsqrt(jnp.mean(x * x, axis=-1, keepdims=True) + EPS)
        row = i * POST_R + lax.broadcasted_iota(jnp.int32, (POST_R, 1), 0)
        res = h_ref[...] + jnp.where(row >= PAD, x * r * gp, 0.0)
        live = row >= PAD + N_META
        err = jnp.where(live, res - t_ref[...], 0.0)
        lpart = 0.5 * jnp.sum(jnp.sum(err * err, axis=1, keepdims=True) / D_MODEL, axis=0, keepdims=True)
        dres = err / D_MODEL
        dres_ref[...] = dres
        gpart = jnp.sum(dres * x * r, axis=0, keepdims=True)

        @pl.when(i == 0)
        def _():
            loss_ref[...] = jnp.zeros_like(loss_ref)
            dgp_ref[...] = jnp.zeros_like(dgp_ref)

        loss_ref[...] += jnp.broadcast_to(lpart, loss_ref.shape)
        dgp_ref[0:1, :] += gpart
        gd = gp * dres
        dout = (r * gd - x * (r * r * r) * jnp.mean(x * gd, axis=-1, keepdims=True)).astype(BF16)
        dout_ref[...] = dout
        dmg = lax.dot_general(dout, wo_ref[...], (((1,), (1,)), ((), ())), preferred_element_type=F32)
        sga = jax.nn.sigmoid(ga_ref[...])
        sgs = jax.nn.sigmoid(gs_ref[...])
        dya = (dmg * sga).astype(BF16)
        dys = (dmg * sgs).astype(BF16)
        dya_ref[...] = dya
        dys_ref[...] = dys
        dga_ref[...] = (dmg * ya_ref[...].astype(F32) * sga * (1.0 - sga)).astype(BF16)
        dgs_ref[...] = (dmg * ys_ref[...].astype(F32) * sgs * (1.0 - sgs)).astype(BF16)
        da = lax.dot_general(dya, wa_ref[...], (((1,), (1,)), ((), ())), preferred_element_type=F32)
        za = za_ref[...]
        do_ref[...] = (da * _silu(za)).astype(BF16)
        dza_ref[...] = (da * o_ref[...] * _dsilu(za)).astype(BF16)
        dsn_ref[...] = lax.dot_general(dys, ws_ref[...], (((1,), (1,)), ((), ())),
                                       preferred_element_type=F32).astype(BF16)

    row = pl.BlockSpec((POST_R, D_MODEL), lambda i: (i, 0))
    pcol = lambda c0: pl.BlockSpec((POST_R, D_MODEL), lambda i: (i, c0 // D_MODEL))
    full = lambda r: pl.BlockSpec((r, D_MODEL), lambda i: (0, 0))
    small = pl.BlockSpec((8, D_MODEL), lambda i: (0, 0))
    return pl.pallas_call(
        body, grid=(T // POST_R,),
        in_specs=[row, row, row, pcol(C_ZA), pcol(C_GA), pcol(C_GS), row, row, row,
                  pl.BlockSpec((1, D_MODEL), lambda i: (0, 0)), full(D_MODEL), full(SSM_INNER), full(D_MODEL)],
        out_specs=[pl.BlockSpec((8, 128), lambda i: (0, 0)), row, row, row, row, row, row, row, row,
                   pl.BlockSpec((POST_R, SSM_INNER), lambda i: (i, 0)), small],
        out_shape=[SDS((8, 128), F32), SDS((T, D_MODEL), F32), SDS((T, D_MODEL), BF16), SDS((T, D_MODEL), BF16),
                   SDS((T, D_MODEL), BF16), SDS((T, D_MODEL), BF16), SDS((T, D_MODEL), BF16), SDS((T, D_MODEL), BF16),
                   SDS((T, D_MODEL), BF16), SDS((T, SSM_INNER), BF16), SDS((8, D_MODEL), F32)],
        compiler_params=_cparams(), name="post_b")(out, h, tgt, proj, proj, proj, ya, ys, o, g_post, w_att, w_ssm, w_o)


def _assemble(dq, dza, dga, dgs, dzs, dxx, dxb, dxc, dk, dv, ddt4):
    def body(dq_ref, dza_ref, dga_ref, dgs_ref, dzs_ref, dxx_ref, dxb_ref, dxc_ref, dk_ref, dv_ref, ddt_ref, o_ref):
        o_ref[:, C_Q:C_Q + D_MODEL] = dq_ref[...].astype(BF16)
        o_ref[:, C_ZA:C_ZA + D_MODEL] = dza_ref[...]
        o_ref[:, C_GA:C_GA + D_MODEL] = dga_ref[...]
        o_ref[:, C_GS:C_GS + D_MODEL] = dgs_ref[...]
        o_ref[:, C_ZS:C_ZS + SSM_INNER] = dzs_ref[...]
        o_ref[:, C_XBC:C_XBC + SSM_INNER] = dxx_ref[...]
        o_ref[:, C_XBC + SSM_INNER:C_XBC + SSM_INNER + GRP_W] = dxb_ref[...]
        o_ref[:, C_XBC + SSM_INNER + GRP_W:C_XBC + CONV_DIM] = dxc_ref[...]
        o_ref[:, C_K:C_K + KV_W] = dk_ref[...].astype(BF16)
        o_ref[:, C_V:C_V + KV_W] = dv_ref[...].astype(BF16)
        d4 = ddt_ref[...]
        o_ref[:, C_DT:C_DT + 128] = (d4[:, 0:128] + d4[:, 128:256] + d4[:, 256:384] + d4[:, 384:512]).astype(BF16)

    spec = lambda w: pl.BlockSpec((BLK, w), lambda i: (i, 0))
    ins = [dq, dza, dga, dgs, dzs, dxx, dxb, dxc, dk, dv, ddt4]
    return pl.pallas_call(
        body, grid=(NB,), in_specs=[spec(a.shape[1]) for a in ins], out_specs=spec(PW),
        out_shape=SDS((T, PW), BF16), name="assemble")(*ins)


def _adamw_math(w, g, m, v):
    m = ADAM_B1 * m + (1.0 - ADAM_B1) * g
    v = ADAM_B2 * v + (1.0 - ADAM_B2) * (g * g)
    m_hat = m / (1.0 - ADAM_B1 ** ADAM_STEP)
    v_hat = v / (1.0 - ADAM_B2 ** ADAM_STEP)
    delta = -ADAM_LR * (m_hat / (jnp.sqrt(v_hat) + ADAM_EPS) + ADAM_WD * w)
    return delta, m, v


def _sum_adamw(recv, w, m, v, tc, name):
    rows, cols = w.shape
    nslab = recv.shape[0]
    assert cols % tc == 0

    def body(r_ref, w_ref, m_ref, v_ref, g_ref, d_ref, nm_ref, nv_ref):
        g = r_ref[0].astype(F32)
        for d in range(1, nslab):
            g = g + r_ref[d].astype(F32)
        g_ref[...] = g
        delta, nm, nv = _adamw_math(w_ref[...], g, m_ref[...], v_ref[...])
        d_ref[...] = delta
        nm_ref[...] = nm
        nv_ref[...] = nv

    blk = pl.BlockSpec((rows, tc), lambda i: (0, i))
    return pl.pallas_call(
        body, grid=(cols // tc,),
        in_specs=[pl.BlockSpec((nslab, rows, tc), lambda i: (0, 0, i)), blk, blk, blk],
        out_specs=[blk, blk, blk, blk], out_shape=[SDS((rows, cols), F32)] * 4,
        compiler_params=_cparams(), name=name)(recv, w, m, v)


def _sum_adamw_rows3(recv, w3, m3, v3, name, exchange=()):
    pairs = 61
    assert (SHARD_IN // 2) % pairs == 0
    nsteps = SHARD_IN // 2 // pairs
    ne = len(exchange)

    def body(*refs):
        r_ref, w_ref, m_ref, v_ref = refs[:4]
        g_ref, d_ref, nm_ref, nv_ref = refs[4 + ne:8 + ne]
        if ne:
            ex_start, ex_finish = _direct_program(refs[4:4 + ne], refs[8 + ne:8 + 2 * ne], refs[8 + 2 * ne:])
            pl.when(pl.program_id(0) == 0)(ex_start)
        g = r_ref[0].astype(F32)
        for d in range(1, N_CHIP):
            g = g + r_ref[d].astype(F32)
        g = g.reshape(2 * pairs, ROW_TILES, 128)
        g_ref[...] = g
        delta, nm, nv = _adamw_math(w_ref[...], g, m_ref[...], v_ref[...])
        d_ref[...] = delta
        nm_ref[...] = nm
        nv_ref[...] = nv
        if ne:
            pl.when(pl.program_id(0) == nsteps - 1)(ex_finish)

    blk = pl.BlockSpec((2 * pairs, ROW_TILES, 128), lambda i: (i, 0, 0))
    return pl.pallas_call(
        body, grid=(nsteps,),
        in_specs=[pl.BlockSpec((N_CHIP, pairs, 2 * ROW_TILES, 128), lambda i: (0, i, 0, 0)), blk, blk, blk]
        + [ANY] * ne,
        out_specs=[blk, blk, blk, blk] + [ANY] * ne,
        out_shape=[SDS(w3.shape, F32)] * 4 + [SDS(p.shape, p.dtype) for p in exchange],
        scratch_shapes=_direct_scratch(exchange) if ne else [],
        compiler_params=_cparams(), name=name)(recv, w3, m3, v3, *exchange)


ROW_GPRE, ROW_CONVB, ROW_DTB, ROW_ALOG, ROW_DSKIP, ROW_SINK, ROW_GSSM, ROW_GPOST = 0, 1, 4, 5, 6, 7, 8, 10
ROW_LOSS = 11
REP_ROWS, ROW_CONVW, ROW_META, SM_ROWS = 16, 16, 24, 40
CW_SHARD = CONV_DIM // N_DEV
META_SHARD = D_MODEL // N_DEV


def _small_pack(dgpre, dbx, dbb, dbc, ddtb, dal, ddsk, dsink, dgn, dgp, dwx, dwb, dwc, loss, dh):
    def body(dgpre_ref, dbx_ref, dbb_ref, dbc_ref, ddtb_ref, dal_ref, ddsk_ref, dsink_ref, dgn_ref, dgp_ref,
             dwx_ref, dwb_ref, dwc_ref, loss_ref, dh_ref, o_ref, rep):
        rep[...] = jnp.zeros_like(rep)
        rep[ROW_LOSS:ROW_LOSS + 1, 0:128] = loss_ref[0:1, :]
        rep[ROW_GPRE:ROW_GPRE + 1, :] = dgpre_ref[0:1, :]
        rep[ROW_CONVB:ROW_CONVB + 1, :] = dbx_ref[0:1, 0:1024]
        rep[ROW_CONVB + 1:ROW_CONVB + 2, :] = dbx_ref[0:1, 1024:2048]
        rep[ROW_CONVB + 2:ROW_CONVB + 3, 0:512] = dbb_ref[0:1, :]
        rep[ROW_CONVB + 2:ROW_CONVB + 3, 512:1024] = dbc_ref[0:1, :]
        rep[ROW_DTB:ROW_DTB + 1, 0:128] = ddtb_ref[0:1, :]
        rep[ROW_ALOG:ROW_ALOG + 1, 0:128] = dal_ref[0:1, :]
        rep[ROW_DSKIP:ROW_DSKIP + 1, 0:128] = ddsk_ref[0:1, :]
        rep[ROW_SINK:ROW_SINK + 1, 0:128] = dsink_ref[0:1, :]
        rep[ROW_GSSM:ROW_GSSM + 1, :] = dgn_ref[0:1, 0:1024]
        rep[ROW_GSSM + 1:ROW_GSSM + 2, :] = dgn_ref[0:1, 1024:2048]
        rep[ROW_GPOST:ROW_GPOST + 1, :] = dgp_ref[0:1, :]
        cw = jnp.concatenate([dwx_ref[...], dwb_ref[...], dwc_ref[...]], axis=1)
        mh = dh_ref[...]
        o_ref[...] = jnp.zeros_like(o_ref)
        for p in range(N_DEV):
            o_ref[p, 0:REP_ROWS, :] = rep[...]
            o_ref[p, ROW_CONVW:ROW_CONVW + 8, 0:CW_SHARD] = cw[:, p * CW_SHARD:(p + 1) * CW_SHARD]
            o_ref[p, ROW_META:ROW_META + N_META, 0:META_SHARD] = mh[:, p * META_SHARD:(p + 1) * META_SHARD]

    ins = [dgpre, dbx, dbb, dbc, ddtb, dal, ddsk, dsink, dgn, dgp, dwx, dwb, dwc, loss]
    return pl.pallas_call(
        body, grid=(1,),
        in_specs=[pl.BlockSpec(a.shape, lambda i: (0, 0)) for a in ins]
        + [pl.BlockSpec((N_META, D_MODEL), lambda i: (PAD // N_META, 0))],
        out_specs=pl.BlockSpec((N_DEV, SM_ROWS, 1024), lambda i: (0, 0, 0)),
        out_shape=SDS((N_DEV, SM_ROWS, 1024), F32), scratch_shapes=[pltpu.VMEM((REP_ROWS, 1024), F32)],
        name="small_pack")(*ins, dh)


def _small_finish(recv, params):
    npar = len(params)

    def body(*refs):
        r_ref = refs[0]
        wmv = refs[1:1 + 3 * npar]
        outs = refs[1 + 3 * npar:1 + 7 * npar]
        loss_ref = refs[1 + 7 * npar]
        gs = refs[-1]
        g = r_ref[0]
        for d in range(1, recv.shape[0]):
            g = g + r_ref[d]
        gs[...] = g
        loss_ref[...] = gs[ROW_LOSS:ROW_LOSS + 1, 0:128]
        grads = [
            gs[ROW_GPRE:ROW_GPRE + 1, :],
            jnp.concatenate([gs[ROW_CONVB + k:ROW_CONVB + k + 1, :] for k in range(3)], axis=1),
            gs[ROW_DTB:ROW_DTB + 1, 0:SSM_HEADS], gs[ROW_ALOG:ROW_ALOG + 1, 0:SSM_HEADS],
            gs[ROW_DSKIP:ROW_DSKIP + 1, 0:SSM_HEADS], gs[ROW_SINK:ROW_SINK + 1, 0:Q_HEADS],
            jnp.concatenate([gs[ROW_GSSM:ROW_GSSM + 1, :], gs[ROW_GSSM + 1:ROW_GSSM + 2, :]], axis=1),
            gs[ROW_GPOST:ROW_GPOST + 1, :],
            gs[ROW_CONVW:ROW_CONVW + 4, 0:CW_SHARD],
            gs[ROW_META:ROW_META + N_META, 0:META_SHARD]]
        for i in range(npar):
            w_ref, m_ref, v_ref = wmv[3 * i:3 * i + 3]
            delta, nm, nv = _adamw_math(w_ref[...], grads[i], m_ref[...], v_ref[...])
            outs[4 * i][...] = grads[i]
            outs[4 * i + 1][...] = delta
            outs[4 * i + 2][...] = nm
            outs[4 * i + 3][...] = nv

    flat = [a for wmv in params for a in wmv]
    res = pl.pallas_call(
        body, out_shape=[SDS(wmv[0].shape, F32) for wmv in params for _ in range(4)] + [SDS((1, 128), F32)],
        scratch_shapes=[pltpu.VMEM((SM_ROWS, 1024), F32)], name="small_finish")(recv, *flat)
    return [tuple(res[4 * i:4 * i + 4]) for i in range(npar)], res[4 * npar]


def _slab(ref, px, py, pc):
    return ref.at[4 * px + 2 * py + pc]


def _bounce(src, dst, buf, sem):
    cp = pltpu.make_async_copy(src, buf, sem)
    cp.start()
    cp.wait()
    cp = pltpu.make_async_copy(buf, dst, sem)
    cp.start()
    cp.wait()


def _ag_program(ins, outs, scratch):
    na = len(ins)
    send_sems, recv_sems, local_sems = scratch[:3]
    bufs = scratch[3:]
    x, y, c = lax.axis_index("x"), lax.axis_index("y"), lax.axis_index("c")
    me, sibling = (x, y, c), (x, y, 1 - c)
    chips = [(1 - x, y), (x, 1 - y), (1 - x, 1 - y)]

    def copy(a, k, block, to, src=None):
        dst = _slab(outs[a], *block)
        return pltpu.make_async_remote_copy(
            src_ref=dst if src is None else src, dst_ref=dst, send_sem=send_sems.at[a, k],
            recv_sem=recv_sems.at[a, k], device_id=to, device_id_type=MESH)

    def own_sends():
        out = []
        for a in range(na):
            out.append(copy(a, 0, me, sibling, src=ins[a]))
            out += [copy(a, 1 + j, me, (*chip, c), src=ins[a]) for j, chip in enumerate(chips)]
        return out

    def start():
        for cp in own_sends():
            cp.start()
        for a in range(na):
            _bounce(ins[a], _slab(outs[a], *me), bufs[a], local_sems.at[a])

    def forward():
        for j, chip in enumerate(chips):
            for a in range(na):
                copy(a, 1 + j, (*chip, c), me).wait_recv()
                copy(a, 4 + j, (*chip, c), sibling).start()

    def finish():
        for a in range(na):
            copy(a, 0, sibling, me).wait_recv()
            for j, chip in enumerate(chips):
                copy(a, 4 + j, (*chip, 1 - c), me).wait_recv()
        for cp in own_sends():
            cp.wait_send()
        for j, chip in enumerate(chips):
            for a in range(na):
                copy(a, 4 + j, (*chip, c), sibling).wait_send()

    return start, forward, finish


def _ag_scratch(shards):
    na = len(shards)
    return [pltpu.SemaphoreType.DMA((na, 7)), pltpu.SemaphoreType.DMA((na, 7)),
            pltpu.SemaphoreType.DMA((na,))] + [pltpu.VMEM(s.shape, s.dtype) for s in shards]


def _all_gather(shards):
    na = len(shards)

    def body(*refs):
        start, forward, finish = _ag_program(refs[:na], refs[na:2 * na], refs[2 * na:])
        start()
        forward()
        finish()

    return pl.pallas_call(
        body, in_specs=[ANY] * na, out_specs=[ANY] * na,
        out_shape=[SDS((N_DEV,) + s.shape, s.dtype) for s in shards],
        scratch_shapes=_ag_scratch(shards), name="all_gather")(*shards)


N_CHIP = 4


def _pair_sum(own, got, name):
    na = len(own)

    def body(*refs):
        for a in range(na):
            o_ref, g_ref, s_ref = refs[a], refs[na + a], refs[2 * na + a]
            s_ref[...] = (o_ref[...].astype(F32) + g_ref[...].astype(F32)).astype(s_ref.dtype)

    def spec(p):
        nd = len(p.shape) - 1
        return pl.BlockSpec((1,) + p.shape[1:], lambda k, nd=nd: (k,) + (0,) * nd)

    return pl.pallas_call(
        body, grid=(N_CHIP,), in_specs=[spec(p) for p in own] + [spec(p) for p in got],
        out_specs=[spec(p) for p in own], out_shape=[SDS(p.shape, p.dtype) for p in own],
        compiler_params=_cparams(), name=name)(*own, *got)


def _chips_program(ins, outs, scratch):
    na = len(ins)
    send_sems, recv_sems, local_sems = scratch[:3]
    bufs = scratch[3:]
    x, y, c = lax.axis_index("x"), lax.axis_index("y"), lax.axis_index("c")
    mine = 2 * x + y
    chips = [(1 - x, y), (x, 1 - y), (1 - x, 1 - y)]

    def send(a, j):
        px, py = chips[j]
        return pltpu.make_async_remote_copy(
            src_ref=ins[a].at[2 * px + py], dst_ref=outs[a].at[mine], send_sem=send_sems.at[a, j],
            recv_sem=recv_sems.at[a, j], device_id=(px, py, c), device_id_type=MESH)

    def arrival(a, j):
        px, py = chips[j]
        return pltpu.make_async_remote_copy(
            src_ref=ins[a].at[2 * px + py], dst_ref=outs[a].at[2 * px + py], send_sem=send_sems.at[a, j],
            recv_sem=recv_sems.at[a, j], device_id=(px, py, c), device_id_type=MESH)

    def start():
        for a in range(na):
            for j in range(3):
                send(a, j).start()
        for a in range(na):
            _bounce(ins[a].at[mine], outs[a].at[mine], bufs[a], local_sems.at[a])

    def finish():
        for a in range(na):
            for j in range(3):
                arrival(a, j).wait_recv()
        for a in range(na):
            for j in range(3):
                send(a, j).wait_send()

    return start, finish


def _chips_scratch(parts):
    na = len(parts)
    return [pltpu.SemaphoreType.DMA((na, 3)), pltpu.SemaphoreType.DMA((na, 3)),
            pltpu.SemaphoreType.DMA((na,))] + [pltpu.VMEM(p.shape[1:], p.dtype) for p in parts]


def _direct_program(ins, outs, scratch):
    na = len(ins)
    send_sems, recv_sems, local_sems = scratch[:3]
    bufs = scratch[3:]
    x, y, c = lax.axis_index("x"), lax.axis_index("y"), lax.axis_index("c")
    me = (x, y, c)
    peers = []
    for k in range(1, N_DEV):
        dx, dy, dc = (k >> 2) & 1, (k >> 1) & 1, k & 1
        peers.append(((1 - x) if dx else x, (1 - y) if dy else y, (1 - c) if dc else c))

    def send(a, k):
        return pltpu.make_async_remote_copy(
            src_ref=_slab(ins[a], *peers[k]), dst_ref=_slab(outs[a], *me), send_sem=send_sems.at[a, k],
            recv_sem=recv_sems.at[a, k], device_id=peers[k], device_id_type=MESH)

    def arrival(a, k):
        return pltpu.make_async_remote_copy(
            src_ref=_slab(ins[a], *peers[k]), dst_ref=_slab(outs[a], *peers[k]), send_sem=send_sems.at[a, k],
            recv_sem=recv_sems.at[a, k], device_id=peers[k], device_id_type=MESH)

    def start():
        for a in range(na):
            for k in range(N_DEV - 1):
                send(a, k).start()
        for a in range(na):
            _bounce(_slab(ins[a], *me), _slab(outs[a], *me), bufs[a], local_sems.at[a])

    def finish():
        for a in range(na):
            for k in range(N_DEV - 1):
                arrival(a, k).wait_recv()
        for a in range(na):
            for k in range(N_DEV - 1):
                send(a, k).wait_send()

    return start, finish


def _direct_scratch(parts):
    na = len(parts)
    return [pltpu.SemaphoreType.DMA((na, N_DEV - 1)), pltpu.SemaphoreType.DMA((na, N_DEV - 1)),
            pltpu.SemaphoreType.DMA((na,))] + [pltpu.VMEM(p.shape[1:], p.dtype) for p in parts]


ROW_TILES = D_MODEL // 128


def _rows3(t):
    return jnp.transpose(t[0]).reshape(t.shape[2], ROW_TILES, 128)


def _unrows3(t):
    return jnp.transpose(t.reshape(t.shape[0], D_MODEL))[None]


def _cast_shards(w_in3, w_att, w_ssm, w_o):
    def body(wi_ref, wa_ref, ws_ref, wo_ref, a_ref, b_ref, c_ref, d_ref):
        a_ref[...] = wi_ref[...].reshape(SHARD_IN // 2, 2 * ROW_TILES, 128).astype(BF16)
        b_ref[...] = wa_ref[...].astype(BF16)
        c_ref[...] = ws_ref[...].astype(BF16)
        d_ref[...] = wo_ref[...].astype(BF16)

    return pl.pallas_call(
        body, out_shape=[SDS((SHARD_IN // 2, 2 * ROW_TILES, 128), BF16), SDS(w_att.shape, BF16),
                         SDS(w_ssm.shape, BF16), SDS(w_o.shape, BF16)],
        compiler_params=_cparams(), name="cast_shards")(w_in3, w_att, w_ssm, w_o)


def _pieces():
    out = []
    for r0, c0, w in _SEGS:
        r = r0
        while r < r0 + w:
            d = r // SHARD_IN
            n = min(r0 + w, (d + 1) * SHARD_IN) - r
            out.append((c0 + (r - r0), d, r - d * SHARD_IN, n))
            r += n
    return out


def _to_aligned_t(slabs):
    def body(a_ref, o_ref):
        for (t, d, s, n) in _pieces():
            o_ref[t:t + n, :] = a_ref[d, s // 2:(s + n) // 2].reshape(n, D_MODEL)
        o_ref[C_DT + 32:C_DT + 128, :] = jnp.zeros((96, D_MODEL), slabs.dtype)

    return pl.pallas_call(body, out_shape=SDS((PW, D_MODEL), slabs.dtype), compiler_params=_cparams(),
                          name="to_aligned")(slabs)


def _from_aligned_pair(g):
    slab = (SHARD_IN // 2, 2 * ROW_TILES, 128)
    by_slab = [[p for p in _pieces() if p[1] == d] for d in range(N_DEV)]

    def body(g_ref, own_ref, got_ref, slabs, send_sems, recv_sems, local_sems):
        x, y, c = lax.axis_index("x"), lax.axis_index("y"), lax.axis_index("c")
        sibling = (x, y, 1 - c)

        def to_own(d, k):
            return pltpu.make_async_copy(slabs.at[d], own_ref.at[k], local_sems.at[k])

        def to_sibling(d, k):
            return pltpu.make_async_remote_copy(
                src_ref=slabs.at[d], dst_ref=got_ref.at[k], send_sem=send_sems.at[k], recv_sem=recv_sems.at[k],
                device_id=sibling, device_id_type=MESH)

        for d in range(N_DEV):
            for (t, _, s, n) in by_slab[d]:
                slabs[d, s // 2:(s + n) // 2] = g_ref[t:t + n, :].reshape(n // 2, 2 * ROW_TILES, 128)
            k, side = d // 2, d % 2
            pl.when(c == side)(to_own(d, k).start)
            pl.when(c != side)(to_sibling(d, k).start)
        for k in range(N_CHIP):
            to_own(0, k).wait()
            to_sibling(0, k).wait()

    half = SDS((N_CHIP,) + slab, g.dtype)
    return pl.pallas_call(
        body, in_specs=[pl.BlockSpec(memory_space=pltpu.VMEM)], out_specs=[ANY, ANY], out_shape=[half, half],
        scratch_shapes=[pltpu.VMEM((N_DEV,) + slab, g.dtype), pltpu.SemaphoreType.DMA((N_CHIP,)),
                        pltpu.SemaphoreType.DMA((N_CHIP,)), pltpu.SemaphoreType.DMA((N_CHIP,))],
        compiler_params=_cparams(), name="from_aligned_pair")(g)


_SEGS = [
    (R_Q, C_Q, 1024), (R_K, C_K, 256), (R_V, C_V, 256), (R_ZA, C_ZA, 1024), (R_ZS, C_ZS, 2048),
    (R_XBC, C_XBC, 3072), (R_DT, C_DT, 32), (R_GA, C_GA, 1024), (R_GS, C_GS, 1024)]


def _pad_lanes(v, n=128):
    return jnp.pad(v, ((0, 0), (0, n - v.shape[1])))


def _device_step(h, tgt, w_alt, w_out, g_pre, conv_w8, conv_b, dt_bias, a_log, d_skip, sinks, g_ssm, g_post, on_mesh):
    dtb, al, dsk, snk = _pad_lanes(dt_bias), _pad_lanes(a_log), _pad_lanes(d_skip), _pad_lanes(sinks)
    u = _norm_u(h, g_pre)
    proj = _matmul(u, w_alt, "nt", F32, T, 896, "in_proj")
    o = _attn_fwd(proj, snk)
    xbc_act = _conv_fwd(proj, conv_w8, conv_b)
    if on_mesh:
        sn, states, att_all, ssm_all, o_all = _ssd_fwd(xbc_act, proj, dtb, al, dsk, g_ssm, gather=w_out)
        w_att = att_all.reshape(D_MODEL, D_MODEL)
        w_ssm = ssm_all.reshape(SSM_INNER, D_MODEL)
        w_o = o_all.reshape(D_MODEL, D_MODEL)
    else:
        sn, states = _ssd_fwd(xbc_act, proj, dtb, al, dsk, g_ssm)
        w_att, w_ssm, w_o = w_out
    a_in, mg, ya, ys, out = _post_a(o, proj, sn, w_att, w_ssm, w_o)
    (loss, dres, dout, dya, dys, dga, dgs, do, dza, dsn, dgp) = _post_b(
        out, h, tgt, proj, ya, ys, o, g_post, w_att, w_ssm, w_o)
    dw_att = _matmul(a_in, dya, "tn", BF16, D_MODEL, D_MODEL, "d_w_att")
    dw_ssm = _matmul(sn, dys, "tn", BF16, D_MODEL, D_MODEL, "d_w_ssm")
    dw_o = _matmul(mg, dout, "tn", BF16, D_MODEL, D_MODEL, "d_w_o")
    res = {}
    if on_mesh:
        parts = [dw_att.reshape(N_DEV, 128, D_MODEL), dw_ssm.reshape(N_DEV, 256, D_MODEL),
                 dw_o.reshape(N_DEV, 128, D_MODEL)]
        (dxs, dbm, dcm, ddt4, dzs, ddtb, dal, ddsk, dgn, res["r_att"], res["r_ssm"], res["r_o"]) = _ssd_bwd(
            xbc_act, proj, dtb, al, dsk, g_ssm, states, dsn, exchange=parts)
    else:
        dxs, dbm, dcm, ddt4, dzs, ddtb, dal, ddsk, dgn = _ssd_bwd(xbc_act, proj, dtb, al, dsk, g_ssm, states, dsn)
        res.update(dw_att=dw_att, dw_ssm=dw_ssm, dw_o=dw_o)
    dxx, dwx, dbx = _conv_bwd(proj, conv_w8, conv_b, dxs, 0, "conv_bwd_x")
    dxb, dwb, dbb = _conv_bwd(proj, conv_w8, conv_b, dbm, SSM_INNER, "conv_bwd_b")
    dxc, dwc, dbc = _conv_bwd(proj, conv_w8, conv_b, dcm, SSM_INNER + GRP_W, "conv_bwd_c")
    dq, dk, dv, dsink = _attn_bwd(proj, snk, do)
    dproj = _assemble(dq, dza, dga, dgs, dzs, dxx, dxb, dxc, dk, dv, ddt4)
    dw_alt = _matmul(dproj, u, "tn", BF16, 896, D_MODEL, "d_w_in")
    if on_mesh:
        own, got = _from_aligned_pair(dw_alt)
        dh, dgpre, res["r_in"] = _d_u_norm(dproj, w_alt, h, g_pre, dres,
                                           chips=_pair_sum([own], [got], "pair_sum_w_in"))
    else:
        dh, dgpre = _d_u_norm(dproj, w_alt, h, g_pre, dres)
        res["dw_alt"] = dw_alt
    small = (dgpre, dbx, dbb, dbc, ddtb, dal, ddsk, dsink, dgn, dgp, dwx, dwb, dwc)
    if on_mesh:
        res["small_pack"] = _small_pack(*small, loss, dh)
    else:
        res["small"] = small
    res.update(loss=loss[0, 0], dh=dh)
    return res


def kernel(x, meta_tokens, g_pre, w_in, conv_w, conv_b, dt_bias, a_log, d_skip, attn_sinks, g_ssm_norm, w_out_att, w_out_ssm, w_out, g_post, loss_target, m_meta_tokens, m_g_pre, m_w_in, m_conv_w, m_conv_b, m_dt_bias, m_a_log, m_d_skip, m_attn_sinks, m_g_ssm_norm, m_w_out_att, m_w_out_ssm, m_w_out, m_g_post, v_meta_tokens, v_g_pre, v_w_in, v_conv_w, v_conv_b, v_dt_bias, v_a_log, v_d_skip, v_attn_sinks, v_g_ssm_norm, v_w_out_att, v_w_out_ssm, v_w_out, v_g_post):
    w_in3, m_in3, v_in3 = _rows3(w_in), _rows3(m_w_in), _rows3(v_w_in)
    a_sh, att_sh, ssm_sh, o_sh = _cast_shards(w_in3, w_out_att[0], w_out_ssm[0], w_out[0])
    cw_sh = jnp.pad(conv_w[0], ((0, 4), (0, 0)))
    a_all, meta_all, cw_all = _all_gather([a_sh, meta_tokens, cw_sh])
    w_alt = _to_aligned_t(a_all)
    meta_full = meta_all.transpose(1, 0, 2).reshape(N_META, D_MODEL)
    conv_w8 = cw_all.transpose(1, 0, 2).reshape(8, CONV_DIM)

    h = jnp.concatenate([jnp.zeros((PAD, D_MODEL), F32), meta_full, x[0]], axis=0)
    tgt = jnp.concatenate([jnp.zeros((PAD + N_META, D_MODEL), F32), loss_target[0]], axis=0)
    r = _device_step(h, tgt, w_alt, (att_sh, ssm_sh, o_sh), g_pre, conv_w8, conv_b, dt_bias, a_log, d_skip,
                     attn_sinks, g_ssm_norm, g_post, True)
    grad_x = r["dh"][PAD + N_META:][None]

    *res_in, r_small = _sum_adamw_rows3(r["r_in"], w_in3, m_in3, v_in3, "adamw_w_in", exchange=[r["small_pack"]])
    res_in = [_unrows3(t) for t in res_in]
    res_att = [t[None] for t in _sum_adamw(r["r_att"], w_out_att[0], m_w_out_att[0], v_w_out_att[0], 512,
                                           "adamw_w_att")]
    res_ssm = [t[None] for t in _sum_adamw(r["r_ssm"], w_out_ssm[0], m_w_out_ssm[0], v_w_out_ssm[0], 512,
                                           "adamw_w_ssm")]
    res_o = [t[None] for t in _sum_adamw(r["r_o"], w_out[0], m_w_out[0], v_w_out[0], 512, "adamw_w_o")]
    (res_gpre, res_convb, res_dtb, res_alog, res_dskip, res_sink, res_gssm, res_gpost, res_cw, res_meta), loss = _small_finish(
        r_small, [(g_pre, m_g_pre, v_g_pre), (conv_b, m_conv_b, v_conv_b), (dt_bias, m_dt_bias, v_dt_bias),
                       (a_log, m_a_log, v_a_log), (d_skip, m_d_skip, v_d_skip),
                       (attn_sinks, m_attn_sinks, v_attn_sinks), (g_ssm_norm, m_g_ssm_norm, v_g_ssm_norm),
                       (g_post, m_g_post, v_g_post), (conv_w[0], m_conv_w[0], v_conv_w[0]),
                       (meta_tokens, m_meta_tokens, v_meta_tokens)])
    res_cw = [t[None] for t in res_cw]
    per_weight = [res_meta, res_gpre, res_in, res_cw, res_convb, res_dtb, res_alog, res_dskip, res_sink, res_gssm,
                  res_att, res_ssm, res_o, res_gpost]
    return (loss[0, 0], grad_x, *[p[0] for p in per_weight], *[p[1] for p in per_weight], *[p[2] for p in per_weight],
            *[p[3] for p in per_weight])
```

```python
import functools
import math

import jax
import jax.numpy as jnp
from jax import lax
from jax.experimental import pallas as pl
from jax.experimental.pallas import tpu as pltpu

F32 = jnp.float32
BF16 = jnp.bfloat16
SDS = jax.ShapeDtypeStruct
MESH = pl.DeviceIdType.MESH
ANY = pl.BlockSpec(memory_space=pl.ANY)

N_DEV = 8
D_MODEL = 1024
SEQ = 2048
N_META = 16
BLK = 128
PAD = 112
T = PAD + N_META + SEQ
NB = T // BLK
EPS = 1e-6
HEAD = 64
Q_HEADS = 16
KV_HEADS = 4
GROUP = 4
KV_W = 256
SSM_INNER = 2048
SSM_HEADS = 32
SSM_GROUPS = 4
GRP_W = 512
SSM_STATE = 128
CONV_DIM = 3072
IN_PROJ = 9760
SHARD_IN = IN_PROJ // N_DEV
NEG = -1e30

C_ZA, C_GA, C_GS, C_Q, C_ZS, C_XBC, C_K, C_V, C_DT = 0, 1024, 2048, 3072, 4096, 6144, 9216, 9472, 9728
PW = 9856
R_Q, R_K, R_V, R_ZA, R_ZS, R_XBC, R_DT, R_GA, R_GS = 0, 1024, 1280, 1536, 2560, 4608, 7680, 7712, 8736

ADAM_LR, ADAM_B1, ADAM_B2, ADAM_EPS, ADAM_WD, ADAM_STEP = 0.001, 0.9, 0.999, 1e-08, 0.01, 10

VMEM_LIMIT = 56 * 1024 * 1024


def _cparams():
    return pltpu.CompilerParams(vmem_limit_bytes=VMEM_LIMIT)


def _silu(x):
    return x * jax.nn.sigmoid(x)


def _dsilu(x):
    s = jax.nn.sigmoid(x)
    return s * (1.0 + x * (1.0 - s))


def _matmul(a, b, mode, out_dtype, tm, tn, name):
    if mode == "nt":
        (m, k), n = a.shape, b.shape[0]
        a_spec = pl.BlockSpec((tm, k), lambda i, j: (i, 0))
        b_spec = pl.BlockSpec((tn, k), lambda i, j: (j, 0))
        dims = (((1,), (1,)), ((), ()))
    else:
        assert mode == "tn"
        (k, m), n = a.shape, b.shape[1]
        a_spec = pl.BlockSpec((k, tm), lambda i, j: (0, i))
        b_spec = pl.BlockSpec((k, tn), lambda i, j: (0, j))
        dims = (((0,), (0,)), ((), ()))
    assert m % tm == 0 and n % tn == 0, (a.shape, b.shape, tm, tn)

    def body(a_ref, b_ref, o_ref):
        o_ref[...] = lax.dot_general(a_ref[...], b_ref[...], dims, preferred_element_type=F32).astype(out_dtype)

    return pl.pallas_call(
        body, grid=(m // tm, n // tn), in_specs=[a_spec, b_spec],
        out_specs=pl.BlockSpec((tm, tn), lambda i, j: (i, j)), out_shape=SDS((m, n), out_dtype),
        compiler_params=_cparams(), name=name)(a, b)


def _norm_u(h, g_pre):
    def body(h_ref, g_ref, u_ref):
        x = h_ref[...]
        r = lax.rsqrt(jnp.mean(x * x, axis=-1, keepdims=True) + EPS)
        u_ref[...] = (x * r * g_ref[...]).astype(BF16)

    return pl.pallas_call(
        body, grid=(NB,),
        in_specs=[pl.BlockSpec((BLK, D_MODEL), lambda i: (i, 0)), pl.BlockSpec((1, D_MODEL), lambda i: (0, 0))],
        out_specs=pl.BlockSpec((BLK, D_MODEL), lambda i: (i, 0)),
        out_shape=SDS((T, D_MODEL), BF16), name="norm_u")(h, g_pre)


DU_TM, DU_TK = T // 2, 1408


def _d_u_norm(dproj, w_alt, h, g_pre, dres, chips=()):
    nk = PW // DU_TK
    ni = T // DU_TM
    nc = len(chips)

    def body(*refs):
        a_ref, b_ref, h_ref, g_ref, dres_ref = refs[:5]
        dh_ref, dg_ref = refs[5 + nc:7 + nc]
        acc_ref = refs[7 + 2 * nc]
        i, kk = pl.program_id(0), pl.program_id(1)
        if nc:
            ch_start, ch_finish = _chips_program(refs[5:5 + nc], refs[7 + nc:7 + 2 * nc], refs[8 + 2 * nc:])
            pl.when((i == 0) & (kk == 0))(ch_start)
        part = jnp.dot(a_ref[...], b_ref[...], preferred_element_type=F32)

        @pl.when(kk == 0)
        def _():
            acc_ref[...] = part

        @pl.when((kk > 0) & (kk < nk - 1))
        def _():
            acc_ref[...] += part

        @pl.when(kk == nk - 1)
        def _():
            du_ = acc_ref[...] + part
            x = h_ref[...]
            r = lax.rsqrt(jnp.mean(x * x, axis=-1, keepdims=True) + EPS)
            gd = g_ref[...] * du_
            dx = r * gd - x * (r * r * r) * jnp.mean(x * gd, axis=-1, keepdims=True)
            dh_ref[...] = dx + dres_ref[...]
            gpart = jnp.concatenate([jnp.sum(du_ * x * r, axis=0, keepdims=True), jnp.zeros((7, D_MODEL), F32)],
                                    axis=0)

            @pl.when(i == 0)
            def _():
                dg_ref[...] = gpart

            @pl.when(i > 0)
            def _():
                dg_ref[...] += gpart

        if nc:
            pl.when((i == ni - 1) & (kk == nk - 1))(ch_finish)

    row = pl.BlockSpec((DU_TM, D_MODEL), lambda i, kk: (i, 0))
    return pl.pallas_call(
        body, grid=(ni, nk),
        in_specs=[pl.BlockSpec((DU_TM, DU_TK), lambda i, kk: (i, kk)),
                  pl.BlockSpec((DU_TK, D_MODEL), lambda i, kk: (kk, 0)),
                  row, pl.BlockSpec((1, D_MODEL), lambda i, kk: (0, 0)), row] + [ANY] * nc,
        out_specs=[row, pl.BlockSpec((8, D_MODEL), lambda i, kk: (0, 0))] + [ANY] * nc,
        out_shape=[SDS((T, D_MODEL), F32), SDS((8, D_MODEL), F32)] + [SDS(p.shape, p.dtype) for p in chips],
        scratch_shapes=[pltpu.VMEM((DU_TM, D_MODEL), F32)] + (_chips_scratch(chips) if nc else []),
        compiler_params=_cparams(), name="d_u_norm")(dproj, w_alt, h, g_pre, dres, *chips)


def _lane_pick(row, h):
    lane = lax.broadcasted_iota(jnp.int32, row.shape, 1)
    return jnp.sum(jnp.where(lane == h, row, 0.0), axis=1, keepdims=True)


def _attn_fn(q4s, kcats, vcats, kms, vms, sinks, n):
    r = lax.broadcasted_iota(jnp.int32, (GROUP * BLK, 2 * BLK), 0)
    s = lax.broadcasted_iota(jnp.int32, (GROUP * BLK, 2 * BLK), 1)
    i = jnp.bitwise_and(r, BLK - 1)
    gi = jnp.right_shift(r, 7)
    rel = i - s + BLK
    k_pos = n * BLK - BLK + s
    band_ok = (rel >= 0) & (rel < BLK) & (k_pos >= PAD + N_META)
    relf = rel.astype(F32)
    rm = lax.broadcasted_iota(jnp.int32, (GROUP * BLK, N_META), 0)
    mm = lax.broadcasted_iota(jnp.int32, (GROUP * BLK, N_META), 1)
    meta_ok = (PAD + mm) <= (n * BLK + jnp.bitwise_and(rm, BLK - 1))
    gcol = jnp.right_shift(lax.broadcasted_iota(jnp.int32, (GROUP * BLK, 1), 0), 7)
    outs = []
    for kh in range(KV_HEADS):
        slopes = [2.0 ** (-8.0 * (kh * GROUP + g + 1) / Q_HEADS) for g in range(GROUP)]
        slope = jnp.where(gi == 0, slopes[0], jnp.where(gi == 1, slopes[1], jnp.where(gi == 2, slopes[2], slopes[3])))
        sk = [_lane_pick(sinks, kh * GROUP + g) for g in range(GROUP)]
        sink = jnp.where(gcol == 0, sk[0], jnp.where(gcol == 1, sk[1], jnp.where(gcol == 2, sk[2], sk[3])))
        qb = (q4s[kh] * (HEAD ** -0.5)).astype(BF16)
        sb = lax.dot_general(qb, kcats[kh].astype(BF16), (((1,), (1,)), ((), ())), preferred_element_type=F32)
        sb = jnp.where(band_ok, sb - slope * relf, NEG)
        sm = lax.dot_general(qb, kms[kh].astype(BF16), (((1,), (1,)), ((), ())), preferred_element_type=F32)
        sm = jnp.where(meta_ok, sm, NEG)
        mx = jnp.maximum(jnp.maximum(jnp.max(sb, axis=1, keepdims=True), jnp.max(sm, axis=1, keepdims=True)), sink)
        mx = lax.stop_gradient(mx)
        eb = jnp.exp(sb - mx)
        em = jnp.exp(sm - mx)
        es = jnp.exp(sink - mx)
        inv = 1.0 / (jnp.sum(eb, axis=1, keepdims=True) + jnp.sum(em, axis=1, keepdims=True) + es)
        pb = (eb * inv).astype(BF16)
        pm = (em * inv).astype(BF16)
        o4 = (jnp.dot(pm, vms[kh].astype(BF16), preferred_element_type=F32)
              + jnp.dot(pb, vcats[kh].astype(BF16), preferred_element_type=F32))
        outs.append(o4)
    return outs


def _attn_specs():
    prev = lambda n: jnp.maximum(n - 1, 0)
    return [
        pl.BlockSpec((BLK, D_MODEL), lambda n: (n, C_Q // D_MODEL)),
        pl.BlockSpec((BLK, KV_W), lambda n: (prev(n), C_K // KV_W)),
        pl.BlockSpec((BLK, KV_W), lambda n: (n, C_K // KV_W)),
        pl.BlockSpec((BLK, KV_W), lambda n: (prev(n), C_V // KV_W)),
        pl.BlockSpec((BLK, KV_W), lambda n: (n, C_V // KV_W)),
        pl.BlockSpec((N_META, KV_W), lambda n: (PAD // N_META, C_K // KV_W)),
        pl.BlockSpec((N_META, KV_W), lambda n: (PAD // N_META, C_V // KV_W)),
        pl.BlockSpec((1, 128), lambda n: (0, 0)),
    ]


def _attn_load(q_ref, kp_ref, kc_ref, vp_ref, vc_ref, km_ref, vm_ref):
    q4s, kcats, vcats, kms, vms = [], [], [], [], []
    for kh in range(KV_HEADS):
        q4s.append(jnp.concatenate(
            [q_ref[:, (kh * GROUP + g) * HEAD:(kh * GROUP + g + 1) * HEAD] for g in range(GROUP)], axis=0))
        cs = slice(kh * HEAD, (kh + 1) * HEAD)
        kcats.append(jnp.concatenate([kp_ref[:, cs], kc_ref[:, cs]], axis=0))
        vcats.append(jnp.concatenate([vp_ref[:, cs], vc_ref[:, cs]], axis=0))
        kms.append(km_ref[:, cs])
        vms.append(vm_ref[:, cs])
    return q4s, kcats, vcats, kms, vms


def _attn_fwd(proj, sinks):
    def body(q_ref, kp_ref, kc_ref, vp_ref, vc_ref, km_ref, vm_ref, s_ref, o_ref):
        n = pl.program_id(0)
        args = _attn_load(q_ref, kp_ref, kc_ref, vp_ref, vc_ref, km_ref, vm_ref)
        outs = _attn_fn(*args, s_ref[...], n)
        for kh in range(KV_HEADS):
            for g in range(GROUP):
                hh = kh * GROUP + g
                o_ref[:, hh * HEAD:(hh + 1) * HEAD] = outs[kh][g * BLK:(g + 1) * BLK]

    return pl.pallas_call(
        body, grid=(NB,), in_specs=_attn_specs(),
        out_specs=pl.BlockSpec((BLK, D_MODEL), lambda n: (n, 0)),
        out_shape=SDS((T, D_MODEL), F32), name="attn_fwd")(proj, proj, proj, proj, proj, proj, proj, sinks)


def _attn_bwd(proj, sinks, do, dproj):
    def body(q_ref, kp_ref, kc_ref, vp_ref, vc_ref, km_ref, vm_ref, s_ref, do_ref, _, dq_ref, dk_ref, dv_ref, ds_ref):
        n = pl.program_id(0)

        @pl.when(n == 0)
        def _():
            dk_ref[...] = jnp.zeros_like(dk_ref)
            dv_ref[...] = jnp.zeros_like(dv_ref)
            ds_ref[...] = jnp.zeros_like(ds_ref)

        args = _attn_load(q_ref, kp_ref, kc_ref, vp_ref, vc_ref, km_ref, vm_ref)
        _, vjp = jax.vjp(lambda a, b, c, d, e, f: _attn_fn(a, b, c, d, e, f, n), *args, s_ref[...])
        do_f = do_ref[...].astype(F32)
        cot = [jnp.concatenate([do_f[:, (kh * GROUP + g) * HEAD:(kh * GROUP + g + 1) * HEAD] for g in range(GROUP)],
                               axis=0) for kh in range(KV_HEADS)]
        dq4s, dkcats, dvcats, dkms, dvms, dsk = vjp(cot)
        ds_ref[0:1, :] += dsk
        cur = pl.ds(pl.multiple_of(n * BLK, BLK), BLK)
        meta = slice(PAD, PAD + N_META)
        for kh in range(KV_HEADS):
            cs = slice(kh * HEAD, (kh + 1) * HEAD)
            for g in range(GROUP):
                hh = kh * GROUP + g
                dq_ref[:, hh * HEAD:(hh + 1) * HEAD] = dq4s[kh][g * BLK:(g + 1) * BLK].astype(BF16)
            dk_ref[cur, cs] += dkcats[kh][BLK:]
            dv_ref[cur, cs] += dvcats[kh][BLK:]
            dk_ref[meta, cs] += dkms[kh]
            dv_ref[meta, cs] += dvms[kh]

        @pl.when(n > 0)
        def _():
            prv = pl.ds(pl.multiple_of((n - 1) * BLK, BLK), BLK)
            for kh in range(KV_HEADS):
                cs = slice(kh * HEAD, (kh + 1) * HEAD)
                dk_ref[prv, cs] += dkcats[kh][:BLK]
                dv_ref[prv, cs] += dvcats[kh][:BLK]

    full_kv = pl.BlockSpec((T, KV_W), lambda n: (0, 0))
    return pl.pallas_call(
        body, grid=(NB,),
        in_specs=_attn_specs() + [pl.BlockSpec((BLK, D_MODEL), lambda n: (n, 0)), ANY],
        out_specs=[pl.BlockSpec((BLK, D_MODEL), lambda n: (n, C_Q // D_MODEL)), full_kv, full_kv,
                   pl.BlockSpec((8, 128), lambda n: (0, 0))],
        out_shape=[SDS((T, PW), BF16), SDS((T, KV_W), F32), SDS((T, KV_W), F32), SDS((8, 128), F32)],
        input_output_aliases={9: 0},
        name="attn_bwd")(proj, proj, proj, proj, proj, proj, proj, sinks, do, dproj)


def _conv_taps(xp, w, rows):
    return (w[0:1] * xp[5:5 + rows] + w[1:2] * xp[6:6 + rows] + w[2:3] * xp[7:7 + rows] + w[3:4] * xp[8:8 + rows])


def _conv_fwd(proj, conv_w, conv_b):
    CONV_CB = CONV_DIM
    ncb = CONV_DIM // CONV_CB
    cb0 = C_XBC // CONV_CB

    def body(tail_ref, cur_ref, w_ref, b_ref, o_ref):
        n = pl.program_id(1)
        tail = jnp.where(n > 0, tail_ref[...], 0.0)
        xp = jnp.concatenate([tail, cur_ref[...]], axis=0)
        conv = _conv_taps(xp, w_ref[...], BLK) + b_ref[...]
        row = n * BLK + lax.broadcasted_iota(jnp.int32, (BLK, 1), 0)
        o_ref[...] = jnp.where(row >= PAD, _silu(conv), 0.0)

    return pl.pallas_call(
        body, grid=(ncb, NB),
        in_specs=[pl.BlockSpec((8, CONV_CB), lambda j, n: (jnp.maximum(n * (BLK // 8) - 1, 0), cb0 + j)),
                  pl.BlockSpec((BLK, CONV_CB), lambda j, n: (n, cb0 + j)),
                  pl.BlockSpec((8, CONV_CB), lambda j, n: (0, j)),
                  pl.BlockSpec((1, CONV_CB), lambda j, n: (0, j))],
        out_specs=pl.BlockSpec((BLK, CONV_CB), lambda j, n: (n, j)),
        out_shape=SDS((T, CONV_DIM), F32), name="conv_fwd")(proj, proj, conv_w, conv_b)


def _conv_bwd(proj, conv_w, conv_b, dact, ch0, dproj, name):
    width = dact.shape[1]
    CONV_CB = width
    ncb = width // CONV_CB
    cb0 = (C_XBC + ch0) // CONV_CB
    wb0 = ch0 // CONV_CB
    last8 = T // 8 - 1

    def body(tail_ref, cur_ref, nxt_ref, w_ref, b_ref, dcur_ref, dnxt_ref, _, dx_ref, dw_ref, db_ref):
        n = pl.program_id(1)
        w = w_ref[...]
        tail = jnp.where(n > 0, tail_ref[...], 0.0)
        xp = jnp.concatenate([tail, cur_ref[...], nxt_ref[...]], axis=0)
        conv = _conv_taps(xp, w, BLK + 8) + b_ref[...]
        dext = jnp.concatenate([dcur_ref[...], jnp.where(n < NB - 1, dnxt_ref[...], 0.0)], axis=0)
        row = n * BLK + lax.broadcasted_iota(jnp.int32, (BLK + 8, 1), 0)
        dconv = jnp.where(row >= PAD, dext * _dsilu(conv), 0.0)
        dx = (w[0:1] * dconv[3:3 + BLK] + w[1:2] * dconv[2:2 + BLK] + w[2:3] * dconv[1:1 + BLK]
              + w[3:4] * dconv[0:BLK])
        dx_ref[...] = dx.astype(BF16)
        dc = dconv[0:BLK]
        dws = [jnp.sum(dc * xp[5 + k:5 + k + BLK], axis=0, keepdims=True) for k in range(4)]
        dwp = jnp.concatenate(dws + [jnp.zeros((4, CONV_CB), F32)], axis=0)
        dbp = jnp.sum(dc, axis=0, keepdims=True)

        @pl.when(n == 0)
        def _():
            dw_ref[...] = dwp
            db_ref[...] = jnp.concatenate([dbp, jnp.zeros((7, CONV_CB), F32)], axis=0)

        @pl.when(n > 0)
        def _():
            dw_ref[...] += dwp
            db_ref[0:1, :] += dbp

    return pl.pallas_call(
        body, grid=(ncb, NB),
        in_specs=[pl.BlockSpec((8, CONV_CB), lambda j, n: (jnp.maximum(n * (BLK // 8) - 1, 0), cb0 + j)),
                  pl.BlockSpec((BLK, CONV_CB), lambda j, n: (n, cb0 + j)),
                  pl.BlockSpec((8, CONV_CB), lambda j, n: (jnp.minimum((n + 1) * (BLK // 8), last8), cb0 + j)),
                  pl.BlockSpec((8, CONV_CB), lambda j, n: (0, wb0 + j)),
                  pl.BlockSpec((1, CONV_CB), lambda j, n: (0, wb0 + j)),
                  pl.BlockSpec((BLK, CONV_CB), lambda j, n: (n, j)),
                  pl.BlockSpec((8, CONV_CB), lambda j, n: (jnp.minimum((n + 1) * (BLK // 8), last8), j)), ANY],
        out_specs=[pl.BlockSpec((BLK, CONV_CB), lambda j, n: (n, cb0 + j)),
                   pl.BlockSpec((8, CONV_CB), lambda j, n: (0, j)),
                   pl.BlockSpec((8, CONV_CB), lambda j, n: (0, j))],
        out_shape=[SDS((T, PW), BF16), SDS((8, width), F32), SDS((8, width), F32)],
        input_output_aliases={7: 0},
        name=name)(proj, proj, proj, conv_w, conv_b, dact, dact, dproj)


HPG = SSM_HEADS // SSM_GROUPS


def _iota(shape, dim):
    return lax.broadcasted_iota(jnp.int32, shape, dim)


def _mm(a, b, ca=1, cb=0):
    return lax.dot_general(a.astype(BF16), b.astype(BF16), (((ca,), (cb,)), ((), ())), preferred_element_type=F32)


def _split3(v):
    hi = v.astype(BF16)
    r1 = v - hi.astype(F32)
    mid = r1.astype(BF16)
    lo = (r1 - mid.astype(F32)).astype(BF16)
    return hi, mid, lo


def _sel_r(parts, onehot, ca=1, cb=0):
    out = lax.dot_general(parts[0], onehot, (((ca,), (cb,)), ((), ())), preferred_element_type=F32)
    for p in parts[1:]:
        out = out + lax.dot_general(p, onehot, (((ca,), (cb,)), ((), ())), preferred_element_type=F32)
    return out


def _sel_l(onehot, parts):
    out = jnp.dot(onehot, parts[0], preferred_element_type=F32)
    for p in parts[1:]:
        out = out + jnp.dot(onehot, p, preferred_element_type=F32)
    return out


def _rows8(*rows):
    r = _iota((8, rows[0].shape[1]), 0)
    out = jnp.zeros((8, rows[0].shape[1]), F32)
    for k, v in enumerate(rows):
        out = jnp.where(r == k, v, out)
    return out


def _ssd_forward(x, z, bm, cm, dt_raw, st_prev, dtb, alog, dskip, gn, g, cst_scr):
    li, si = _iota((BLK, BLK), 0), _iota((BLK, BLK), 1)
    dt_all = jax.nn.softplus(dt_raw + dtb)
    a_row = -jnp.exp(alog)
    a_all = dt_all * a_row
    cs_all = _sel_l((li >= si).astype(BF16), _split3(a_all))
    cs_parts = _split3(cs_all)
    spread = (_iota((BLK, GRP_W), 0) == g * HPG + jnp.right_shift(_iota((BLK, GRP_W), 1), 6)).astype(BF16)
    dt_e = _sel_r(_split3(dt_all), spread)
    cs_e = _sel_r(cs_parts, spread)
    d_e = _sel_r(_split3(_rows8(dskip)), spread)[0:1]
    cs_last_e = jnp.sum(jnp.where(_iota((BLK, GRP_W), 0) == BLK - 1, cs_e, 0.0), axis=0, keepdims=True)
    p_e = jnp.exp(cs_e)
    w_e = jnp.exp(cs_last_e - cs_e)
    cd_e = jnp.exp(cs_last_e)
    xr = x * dt_e
    cst_scr[...] = cs_all.T
    cst_g = cst_scr[pl.ds(pl.multiple_of(g * HPG, HPG), HPG), :]
    own = jnp.right_shift(_iota((HPG, HPG * BLK), 1), 7) == _iota((HPG, HPG * BLK), 0)
    ownf = own.astype(F32)
    q_rows = [ownf, ownf, ownf] + [jnp.where(own, jnp.concatenate([p.astype(F32)] * HPG, axis=1), 0.0)
                                   for p in _split3(cst_g)]
    q2 = jnp.concatenate(q_rows + [jnp.zeros((BLK - 6 * HPG, HPG * BLK), F32)], axis=0).astype(BF16)
    lane1 = _iota((1, BLK), 1)
    p2 = jnp.where((lane1 >= 3 * HPG) & (lane1 < 6 * HPG), -1.0, 0.0)
    for k, part in enumerate(cs_parts):
        pick = ((li == g * HPG + si - k * HPG) & (si >= k * HPG) & (si < (k + 1) * HPG)).astype(BF16)
        p2 = p2 + jnp.dot(part, pick, preferred_element_type=F32)
    dmat = jnp.dot(p2.astype(BF16), q2, preferred_element_type=F32)
    causal = _iota((BLK, HPG * BLK), 0) >= jnp.bitwise_and(_iota((BLK, HPG * BLK), 1), BLK - 1)
    lam = jnp.exp(jnp.where(causal, dmat, NEG))
    gmat = _mm(cm, bm, 1, 1)
    m_all = lam * jnp.concatenate([gmat] * HPG, axis=1)
    mb = m_all.astype(BF16)
    lo = _iota((BLK, BLK), 1) < HEAD
    xrb = xr.astype(BF16)
    zero = jnp.zeros((BLK, BLK), BF16)
    bds, yd = [], []
    for i in range(HPG // 2):
        t = xrb[:, BLK * i:BLK * (i + 1)]
        bd = jnp.concatenate([jnp.where(lo, t, zero), jnp.where(lo, zero, t)], axis=0)
        bds.append(bd)
        yd.append(jnp.dot(mb[:, 2 * BLK * i:2 * BLK * (i + 1)], bd, preferred_element_type=F32))
    cs_st = _mm(cm, st_prev)
    y = jnp.concatenate(yd, axis=1) + cs_st * p_e + d_e * x
    xrw = xr * w_e
    st_new = cd_e * st_prev + _mm(bm, xrw, 0, 0)
    yz = y * _silu(z)
    rn = lax.rsqrt(jnp.sum(yz * yz, axis=1, keepdims=True) / GRP_W + EPS)
    return dict(out=yz * rn * gn, st_new=st_new, dt_all=dt_all, a_row=a_row, dt_e=dt_e, d_e=d_e, p_e=p_e, w_e=w_e,
                cd_e=cd_e, xr=xr, xrw=xrw, lam=lam, m_all=m_all, mb=mb, bds=bds, cs_st=cs_st, y=y, yz=yz, rn=rn, lo=lo)


def _ssd_backward(f, x, z, bm, cm, dt_raw, st_prev, dtb, gn, g, dout, dst_next, cst_scr):
    li, si = _iota((BLK, BLK), 0), _iota((BLK, BLK), 1)
    yz, rn, y, p_e, w_e, cd_e, xr = f["yz"], f["rn"], f["y"], f["p_e"], f["w_e"], f["cd_e"], f["xr"]
    dgn = jnp.sum(dout * yz * rn, axis=0, keepdims=True)
    t = dout * gn
    dyz = rn * t - yz * (rn * rn * rn) * (jnp.sum(yz * t, axis=1, keepdims=True) / GRP_W)
    dy = dyz * _silu(z)
    dz = dyz * y * _dsilu(z)
    dx = f["d_e"] * dy
    dd_e = jnp.sum(dy * x, axis=0, keepdims=True)
    dcsst = dy * p_e
    dp_e = dy * f["cs_st"]
    dcm = _mm(dcsst, st_prev, 1, 1)
    dst_prev = _mm(cm, dcsst, 0, 0) + cd_e * dst_next
    dcd_e = jnp.sum(dst_next * st_prev, axis=0, keepdims=True)
    dbm = _mm(f["xrw"], dst_next, 1, 1)
    dxrw = _mm(bm, dst_next)
    dxr = dxrw * w_e
    dw_e = dxrw * xr
    dyb = dy.astype(BF16)
    dms, dxr_d = [], []
    for i in range(HPG // 2):
        dyp = dyb[:, BLK * i:BLK * (i + 1)]
        dms.append(lax.dot_general(dyp, f["bds"][i], (((1,), (1,)), ((), ())), preferred_element_type=F32))
        r = lax.dot_general(f["mb"][:, 2 * BLK * i:2 * BLK * (i + 1)], dyp, (((0,), (0,)), ((), ())),
                            preferred_element_type=F32)
        dxr_d.append(jnp.where(f["lo"], r[0:BLK], r[BLK:2 * BLK]))
    dm_all = jnp.concatenate(dms, axis=1)
    dxr = dxr + jnp.concatenate(dxr_d, axis=1)
    dlg = dm_all * f["lam"]
    dg = dlg[:, 0:BLK]
    for j in range(1, HPG):
        dg = dg + dlg[:, BLK * j:BLK * (j + 1)]
    dcm = dcm + _mm(dg, bm)
    dbm = dbm + _mm(dg, cm, 0, 0)
    q_all = dm_all * f["m_all"]
    col_sums = jnp.sum(q_all, axis=0, keepdims=True)
    cst_scr[...] = jnp.zeros_like(cst_scr)
    cst_scr[pl.ds(pl.multiple_of(g * HPG, HPG), HPG), :] = _rows8(
        *[col_sums[:, BLK * j:BLK * (j + 1)] for j in range(HPG)])
    dcs = -cst_scr[...].T
    for j in range(HPG):
        dcs = dcs + jnp.where(si == g * HPG + j,
                              jnp.sum(q_all[:, BLK * j:BLK * (j + 1)], axis=1, keepdims=True), 0.0)
    unspread = (_iota((GRP_W, BLK), 1) == g * HPG + jnp.right_shift(_iota((GRP_W, BLK), 0), 6)).astype(BF16)
    dww = dw_e * w_e
    per_head = _sel_r(_split3(jnp.concatenate([dp_e * p_e - dww, dxr * x], axis=0)), unspread)
    last = _sel_r(_split3(_rows8(jnp.sum(dww, axis=0, keepdims=True) + dcd_e * cd_e, dd_e)), unspread)
    dcs = dcs + per_head[0:BLK] + jnp.where(li == BLK - 1, last[0:1], 0.0)
    da = _sel_l((si >= li).astype(BF16), _split3(dcs))
    ddt_all = da * f["a_row"] + per_head[BLK:2 * BLK]
    dalog = jnp.sum(da * f["dt_all"], axis=0, keepdims=True) * f["a_row"]
    dx = dx + dxr * f["dt_e"]
    ddt_raw = ddt_all * jax.nn.sigmoid(dt_raw + dtb)
    ddtb = jnp.sum(ddt_raw, axis=0, keepdims=True)
    ddskip = last[1:2]
    return dict(dx=dx, dz=dz, dbm=dbm, dcm=dcm, ddt_raw=ddt_raw, dst_prev=dst_prev, ddtb=ddtb, dalog=dalog,
                ddskip=ddskip, dgn=dgn)


GPS = 4
NPG = SSM_GROUPS // GPS


def _ssd_in_specs(rev):
    cidx = (lambda c: NB - 1 - c) if rev else (lambda c: c)
    wx, wb = GPS * GRP_W, GPS * SSM_STATE
    return [
        pl.BlockSpec((BLK, wx), lambda p, c: (cidx(c), p)),
        pl.BlockSpec((BLK, wb), lambda p, c: (cidx(c), SSM_INNER // wb + p)),
        pl.BlockSpec((BLK, wb), lambda p, c: (cidx(c), (SSM_INNER + SSM_GROUPS * SSM_STATE) // wb + p)),
        pl.BlockSpec((BLK, 128), lambda p, c: (cidx(c), C_DT // 128)),
        pl.BlockSpec((BLK, wx), lambda p, c: (cidx(c), C_ZS // wx + p)),
        pl.BlockSpec((1, 128), lambda p, c: (0, 0)),
        pl.BlockSpec((1, 128), lambda p, c: (0, 0)),
        pl.BlockSpec((1, 128), lambda p, c: (0, 0)),
        pl.BlockSpec((1, wx), lambda p, c: (0, p)),
    ]


def _grp(ref, i, w):
    return ref[:, i * w:(i + 1) * w]


def _ssd_fwd(xbc_act, proj, dt_bias, a_log, d_skip, g_norm, gather=()):
    ng = len(gather)

    def body(*refs):
        xs_ref, b_ref, c_ref, dt_ref, z_ref, dtb_ref, al_ref, dsk_ref, gn_ref = refs[:9]
        y_ref, st_ref = refs[9 + ng:11 + ng]
        s_scr, cst_scr = refs[11 + 2 * ng:13 + 2 * ng]
        p = pl.program_id(0)
        c = pl.program_id(1)
        if ng:
            ag_start, ag_forward, ag_finish = _ag_program(refs[9:9 + ng], refs[11 + ng:11 + 2 * ng],
                                                          refs[13 + 2 * ng:])
            pl.when((p == 0) & (c == 0))(ag_start)
            pl.when((p == NPG - 1) & (c == (3 * NB) // 4))(ag_forward)

        @pl.when(c == 0)
        def _():
            s_scr[...] = jnp.zeros_like(s_scr)

        for i in range(GPS):
            st_prev = s_scr[i]
            st_ref[i, 0] = st_prev
            f = _ssd_forward(_grp(xs_ref, i, GRP_W), _grp(z_ref, i, GRP_W), _grp(b_ref, i, SSM_STATE),
                             _grp(c_ref, i, SSM_STATE), dt_ref[...], st_prev, dtb_ref[...], al_ref[...],
                             dsk_ref[...], _grp(gn_ref, i, GRP_W), p * GPS + i, cst_scr.at[i])
            y_ref[:, i * GRP_W:(i + 1) * GRP_W] = f["out"].astype(BF16)
            s_scr[i] = f["st_new"]
        if ng:
            pl.when((p == NPG - 1) & (c == NB - 1))(ag_finish)

    return pl.pallas_call(
        body, grid=(NPG, NB), in_specs=_ssd_in_specs(False) + [ANY] * ng,
        out_specs=[pl.BlockSpec((BLK, GPS * GRP_W), lambda p, c: (c, p)),
                   pl.BlockSpec((GPS, 1, SSM_STATE, GRP_W), lambda p, c: (p, c, 0, 0))] + [ANY] * ng,
        out_shape=[SDS((T, SSM_INNER), BF16), SDS((SSM_GROUPS, NB, SSM_STATE, GRP_W), F32)]
        + [SDS((N_DEV,) + s.shape, s.dtype) for s in gather],
        scratch_shapes=[pltpu.VMEM((GPS, SSM_STATE, GRP_W), F32), pltpu.VMEM((GPS, BLK, BLK), F32)]
        + (_ag_scratch(gather) if ng else []),
        compiler_params=_cparams(),
        name="ssd_fwd")(xbc_act, xbc_act, xbc_act, proj, proj, dt_bias, a_log, d_skip, g_norm, *gather)


def _ssd_bwd(xbc_act, proj, dt_bias, a_log, d_skip, g_norm, states, dy, dproj, exchange=()):
    chips = exchange
    nc = len(chips)

    def body(*refs):
        xs_ref, b_ref, c_ref, dt_ref, z_ref, dtb_ref, al_ref, dsk_ref, gn_ref, st_ref, dy_ref = refs[:11]
        (dxs_ref, db_ref, dc_ref, ddt_ref, dz_ref, ddtb_ref, dal_ref, ddsk_ref, dgn_ref) = refs[12 + nc:21 + nc]
        ds_scr, cst_scr = refs[21 + 2 * nc:23 + 2 * nc]
        p = pl.program_id(0)
        c = pl.program_id(1)
        if nc:
            ch_start, ch_finish = _direct_program(refs[12:12 + nc], refs[21 + nc:21 + 2 * nc], refs[23 + 2 * nc:])
            pl.when((p == 0) & (c == 0))(ch_start)

        @pl.when(c == 0)
        def _():
            ds_scr[...] = jnp.zeros_like(ds_scr)
            dgn_ref[...] = jnp.zeros_like(dgn_ref)

        @pl.when((c == 0) & (p == 0))
        def _():
            ddtb_ref[...] = jnp.zeros_like(ddtb_ref)
            dal_ref[...] = jnp.zeros_like(dal_ref)
            ddsk_ref[...] = jnp.zeros_like(ddsk_ref)

        dt_raw = dt_ref[...]
        for i in range(GPS):
            g = p * GPS + i
            x, z, gn = _grp(xs_ref, i, GRP_W), _grp(z_ref, i, GRP_W), _grp(gn_ref, i, GRP_W)
            bm, cm, st_prev = _grp(b_ref, i, SSM_STATE), _grp(c_ref, i, SSM_STATE), st_ref[i, 0]
            f = _ssd_forward(x, z, bm, cm, dt_raw, st_prev, dtb_ref[...], al_ref[...], dsk_ref[...], gn, g,
                             cst_scr.at[i])
            d = _ssd_backward(f, x, z, bm, cm, dt_raw, st_prev, dtb_ref[...], gn, g,
                              _grp(dy_ref, i, GRP_W).astype(F32), ds_scr[i], cst_scr.at[i])
            dxs_ref[:, i * GRP_W:(i + 1) * GRP_W] = d["dx"]
            dz_ref[:, i * GRP_W:(i + 1) * GRP_W] = d["dz"].astype(BF16)
            ds_scr[i] = d["dst_prev"]
            db_ref[:, i * SSM_STATE:(i + 1) * SSM_STATE] = d["dbm"]
            dc_ref[:, i * SSM_STATE:(i + 1) * SSM_STATE] = d["dcm"]
            ddt_ref[:, i * 128:(i + 1) * 128] = d["ddt_raw"]
            dgn_ref[0:1, i * GRP_W:(i + 1) * GRP_W] += d["dgn"]
            ddtb_ref[0:1, :] += d["ddtb"]
            dal_ref[0:1, :] += d["dalog"]
            ddsk_ref[0:1, :] += d["ddskip"]
        if nc:
            pl.when((p == NPG - 1) & (c == NB - 1))(ch_finish)

    rc = lambda c: NB - 1 - c
    small = pl.BlockSpec((8, 128), lambda p, c: (0, 0))
    wx, wb = GPS * GRP_W, GPS * SSM_STATE
    return pl.pallas_call(
        body, grid=(NPG, NB),
        in_specs=_ssd_in_specs(True) + [
            pl.BlockSpec((GPS, 1, SSM_STATE, GRP_W), lambda p, c: (p, rc(c), 0, 0)),
            pl.BlockSpec((BLK, wx), lambda p, c: (rc(c), p)), ANY] + [ANY] * nc,
        out_specs=[pl.BlockSpec((BLK, wx), lambda p, c: (rc(c), p)),
                   pl.BlockSpec((BLK, wb), lambda p, c: (rc(c), p)),
                   pl.BlockSpec((BLK, wb), lambda p, c: (rc(c), p)),
                   pl.BlockSpec((BLK, GPS * 128), lambda p, c: (rc(c), p)),
                   pl.BlockSpec((BLK, wx), lambda p, c: (rc(c), C_ZS // wx + p)),
                   small, small, small,
                   pl.BlockSpec((8, wx), lambda p, c: (0, p))] + [ANY] * nc,
        out_shape=[SDS((T, SSM_INNER), F32), SDS((T, GRP_W), F32), SDS((T, GRP_W), F32), SDS((T, GRP_W), F32),
                   SDS((T, PW), BF16), SDS((8, 128), F32), SDS((8, 128), F32), SDS((8, 128), F32),
                   SDS((8, SSM_INNER), F32)] + [SDS(p.shape, p.dtype) for p in chips],
        scratch_shapes=[pltpu.VMEM((GPS, SSM_STATE, GRP_W), F32), pltpu.VMEM((GPS, BLK, BLK), F32)]
        + (_direct_scratch(chips) if nc else []),
        input_output_aliases={11: 4},
        compiler_params=_cparams(),
        name="ssd_bwd")(xbc_act, xbc_act, xbc_act, proj, proj, dt_bias, a_log, d_skip, g_norm, states, dy, dproj,
                        *chips)


POST_R = 272


def _post_a(o, proj, sn, w_att, w_ssm, w_o):
    def body(o_ref, za_ref, ga_ref, gs_ref, sn_ref, wa_ref, ws_ref, wo_ref, a_ref, mg_ref, ya_ref, ys_ref, out_ref):
        a = (o_ref[...] * _silu(za_ref[...])).astype(BF16)
        a_ref[...] = a
        ya = jnp.dot(a, wa_ref[...], preferred_element_type=F32)
        ys = jnp.dot(sn_ref[...], ws_ref[...], preferred_element_type=F32)
        ya_ref[...] = ya.astype(BF16)
        ys_ref[...] = ys.astype(BF16)
        mg = (jax.nn.sigmoid(ga_ref[...]) * ya + jax.nn.sigmoid(gs_ref[...]) * ys).astype(BF16)
        mg_ref[...] = mg
        out_ref[...] = jnp.dot(mg, wo_ref[...], preferred_element_type=F32)

    row = pl.BlockSpec((POST_R, D_MODEL), lambda i: (i, 0))
    pcol = lambda c0: pl.BlockSpec((POST_R, D_MODEL), lambda i: (i, c0 // D_MODEL))
    full = lambda r: pl.BlockSpec((r, D_MODEL), lambda i: (0, 0))
    return pl.pallas_call(
        body, grid=(T // POST_R,),
        in_specs=[row, pcol(C_ZA), pcol(C_GA), pcol(C_GS), pl.BlockSpec((POST_R, SSM_INNER), lambda i: (i, 0)),
                  full(D_MODEL), full(SSM_INNER), full(D_MODEL)],
        out_specs=[row, row, row, row, row],
        out_shape=[SDS((T, D_MODEL), BF16), SDS((T, D_MODEL), BF16), SDS((T, D_MODEL), BF16), SDS((T, D_MODEL), BF16),
                   SDS((T, D_MODEL), F32)],
        compiler_params=_cparams(), name="post_a")(o, proj, proj, proj, sn, w_att, w_ssm, w_o)


def _post_b(out, h, tgt, proj, ya, ys, o, g_post, w_att, w_ssm, w_o):
    def body(out_ref, h_ref, t_ref, za_ref, ga_ref, gs_ref, ya_ref, ys_ref, o_ref, gp_ref, wa_ref, ws_ref, wo_ref,
             loss_ref, dres_ref, dout_ref, dya_ref, dys_ref, do_ref, dp_ref, dsn_ref, dgp_ref):
        i = pl.program_id(0)
        x = out_ref[...]
        gp = gp_ref[...]
        r = lax.rsqrt(jnp.mean(x * x, axis=-1, keepdims=True) + EPS)
        row = i * POST_R + lax.broadcasted_iota(jnp.int32, (POST_R, 1), 0)
        res = h_ref[...] + jnp.where(row >= PAD, x * r * gp, 0.0)
        live = row >= PAD + N_META
        err = jnp.where(live, res - t_ref[...], 0.0)
        lpart = 0.5 * jnp.sum(jnp.sum(err * err, axis=1, keepdims=True) / D_MODEL, axis=0, keepdims=True)
        dres = err / D_MODEL
        dres_ref[...] = dres
        gpart = jnp.sum(dres * x * r, axis=0, keepdims=True)

        @pl.when(i == 0)
        def _():
            loss_ref[...] = jnp.zeros_like(loss_ref)
            dgp_ref[...] = jnp.zeros_like(dgp_ref)

        loss_ref[...] += jnp.broadcast_to(lpart, loss_ref.shape)
        dgp_ref[0:1, :] += gpart
        gd = gp * dres
        dout = (r * gd - x * (r * r * r) * jnp.mean(x * gd, axis=-1, keepdims=True)).astype(BF16)
        dout_ref[...] = dout
        dmg = lax.dot_general(dout, wo_ref[...], (((1,), (1,)), ((), ())), preferred_element_type=F32)
        sga = jax.nn.sigmoid(ga_ref[...])
        sgs = jax.nn.sigmoid(gs_ref[...])
        dya = (dmg * sga).astype(BF16)
        dys = (dmg * sgs).astype(BF16)
        dya_ref[...] = dya
        dys_ref[...] = dys
        dp_ref[:, C_GA:C_GA + D_MODEL] = (dmg * ya_ref[...].astype(F32) * sga * (1.0 - sga)).astype(BF16)
        dp_ref[:, C_GS:C_GS + D_MODEL] = (dmg * ys_ref[...].astype(F32) * sgs * (1.0 - sgs)).astype(BF16)
        da = lax.dot_general(dya, wa_ref[...], (((1,), (1,)), ((), ())), preferred_element_type=F32)
        za = za_ref[...]
        do_ref[...] = (da * _silu(za)).astype(BF16)
        dp_ref[:, C_ZA:C_ZA + D_MODEL] = (da * o_ref[...] * _dsilu(za)).astype(BF16)
        dsn_ref[...] = lax.dot_general(dys, ws_ref[...], (((1,), (1,)), ((), ())),
                                       preferred_element_type=F32).astype(BF16)

    row = pl.BlockSpec((POST_R, D_MODEL), lambda i: (i, 0))
    pcol = lambda c0: pl.BlockSpec((POST_R, D_MODEL), lambda i: (i, c0 // D_MODEL))
    full = lambda r: pl.BlockSpec((r, D_MODEL), lambda i: (0, 0))
    small = pl.BlockSpec((8, D_MODEL), lambda i: (0, 0))
    return pl.pallas_call(
        body, grid=(T // POST_R,),
        in_specs=[row, row, row, pcol(C_ZA), pcol(C_GA), pcol(C_GS), row, row, row,
                  pl.BlockSpec((1, D_MODEL), lambda i: (0, 0)), full(D_MODEL), full(SSM_INNER), full(D_MODEL)],
        out_specs=[pl.BlockSpec((8, 128), lambda i: (0, 0)), row, row, row, row, row,
                   pl.BlockSpec((POST_R, C_Q), lambda i: (i, 0)),
                   pl.BlockSpec((POST_R, SSM_INNER), lambda i: (i, 0)), small],
        out_shape=[SDS((8, 128), F32), SDS((T, D_MODEL), F32), SDS((T, D_MODEL), BF16), SDS((T, D_MODEL), BF16),
                   SDS((T, D_MODEL), BF16), SDS((T, D_MODEL), BF16), SDS((T, PW), BF16),
                   SDS((T, SSM_INNER), BF16), SDS((8, D_MODEL), F32)],
        compiler_params=_cparams(), name="post_b")(out, h, tgt, proj, proj, proj, ya, ys, o, g_post, w_att, w_ssm, w_o)


TAIL_W = PW - C_K


def _dproj_tail(dproj, dk, dv, ddt4):
    def body(_, dk_ref, dv_ref, ddt_ref, o_ref, buf, sem):
        n = pl.program_id(0)
        d4 = ddt_ref[...]
        buf[:, 0:KV_W] = dk_ref[...].astype(BF16)
        buf[:, KV_W:2 * KV_W] = dv_ref[...].astype(BF16)
        buf[:, 2 * KV_W:TAIL_W] = (d4[:, 0:128] + d4[:, 128:256] + d4[:, 256:384] + d4[:, 384:512]).astype(BF16)
        cp = pltpu.make_async_copy(buf, o_ref.at[pl.ds(pl.multiple_of(n * BLK, BLK), BLK), pl.ds(C_K, TAIL_W)], sem)
        cp.start()
        cp.wait()

    spec = lambda w: pl.BlockSpec((BLK, w), lambda i: (i, 0))
    return pl.pallas_call(
        body, grid=(NB,), in_specs=[ANY, spec(KV_W), spec(KV_W), spec(GRP_W)], out_specs=ANY,
        out_shape=SDS((T, PW), BF16), input_output_aliases={0: 0},
        scratch_shapes=[pltpu.VMEM((BLK, TAIL_W), BF16), pltpu.SemaphoreType.DMA],
        name="dproj_tail")(dproj, dk, dv, ddt4)


def _adamw_math(w, g, m, v):
    m = ADAM_B1 * m + (1.0 - ADAM_B1) * g
    v = ADAM_B2 * v + (1.0 - ADAM_B2) * (g * g)
    m_hat = m / (1.0 - ADAM_B1 ** ADAM_STEP)
    v_hat = v / (1.0 - ADAM_B2 ** ADAM_STEP)
    delta = -ADAM_LR * (m_hat / (jnp.sqrt(v_hat) + ADAM_EPS) + ADAM_WD * w)
    return delta, m, v


def _sum_adamw(recv, w, m, v, tc, name):
    rows, cols = w.shape
    nslab = recv.shape[0]
    assert cols % tc == 0

    def body(r_ref, w_ref, m_ref, v_ref, g_ref, d_ref, nm_ref, nv_ref):
        g = r_ref[0].astype(F32)
        for d in range(1, nslab):
            g = g + r_ref[d].astype(F32)
        g_ref[...] = g
        delta, nm, nv = _adamw_math(w_ref[...], g, m_ref[...], v_ref[...])
        d_ref[...] = delta
        nm_ref[...] = nm
        nv_ref[...] = nv

    blk = pl.BlockSpec((rows, tc), lambda i: (0, i))
    return pl.pallas_call(
        body, grid=(cols // tc,),
        in_specs=[pl.BlockSpec((nslab, rows, tc), lambda i: (0, 0, i)), blk, blk, blk],
        out_specs=[blk, blk, blk, blk], out_shape=[SDS((rows, cols), F32)] * 4,
        compiler_params=_cparams(), name=name)(recv, w, m, v)


def _sum_adamw_rows3(recv, w3, m3, v3, name, exchange=()):
    pairs = 61
    assert (SHARD_IN // 2) % pairs == 0
    nsteps = SHARD_IN // 2 // pairs
    ne = len(exchange)

    def body(*refs):
        r_ref, w_ref, m_ref, v_ref = refs[:4]
        g_ref, d_ref, nm_ref, nv_ref = refs[4 + ne:8 + ne]
        if ne:
            ex_start, ex_finish = _direct_program(refs[4:4 + ne], refs[8 + ne:8 + 2 * ne], refs[8 + 2 * ne:])
            pl.when(pl.program_id(0) == 0)(ex_start)
        g = r_ref[0].astype(F32)
        for d in range(1, N_CHIP):
            g = g + r_ref[d].astype(F32)
        g = g.reshape(2 * pairs, ROW_TILES, 128)
        g_ref[...] = g
        delta, nm, nv = _adamw_math(w_ref[...], g, m_ref[...], v_ref[...])
        d_ref[...] = delta
        nm_ref[...] = nm
        nv_ref[...] = nv
        if ne:
            pl.when(pl.program_id(0) == nsteps - 1)(ex_finish)

    blk = pl.BlockSpec((2 * pairs, ROW_TILES, 128), lambda i: (i, 0, 0))
    return pl.pallas_call(
        body, grid=(nsteps,),
        in_specs=[pl.BlockSpec((N_CHIP, pairs, 2 * ROW_TILES, 128), lambda i: (0, i, 0, 0)), blk, blk, blk]
        + [ANY] * ne,
        out_specs=[blk, blk, blk, blk] + [ANY] * ne,
        out_shape=[SDS(w3.shape, F32)] * 4 + [SDS(p.shape, p.dtype) for p in exchange],
        scratch_shapes=_direct_scratch(exchange) if ne else [],
        compiler_params=_cparams(), name=name)(recv, w3, m3, v3, *exchange)


ROW_GPRE, ROW_CONVB, ROW_DTB, ROW_ALOG, ROW_DSKIP, ROW_SINK, ROW_GSSM, ROW_GPOST = 0, 1, 4, 5, 6, 7, 8, 10
ROW_LOSS = 11
REP_ROWS, ROW_CONVW, ROW_META, SM_ROWS = 16, 16, 24, 40
CW_SHARD = CONV_DIM // N_DEV
META_SHARD = D_MODEL // N_DEV


def _small_pack(dgpre, dbx, dbb, dbc, ddtb, dal, ddsk, dsink, dgn, dgp, dwx, dwb, dwc, loss, dh):
    def body(dgpre_ref, dbx_ref, dbb_ref, dbc_ref, ddtb_ref, dal_ref, ddsk_ref, dsink_ref, dgn_ref, dgp_ref,
             dwx_ref, dwb_ref, dwc_ref, loss_ref, dh_ref, o_ref, rep):
        rep[...] = jnp.zeros_like(rep)
        rep[ROW_LOSS:ROW_LOSS + 1, 0:128] = loss_ref[0:1, :]
        rep[ROW_GPRE:ROW_GPRE + 1, :] = dgpre_ref[0:1, :]
        rep[ROW_CONVB:ROW_CONVB + 1, :] = dbx_ref[0:1, 0:1024]
        rep[ROW_CONVB + 1:ROW_CONVB + 2, :] = dbx_ref[0:1, 1024:2048]
        rep[ROW_CONVB + 2:ROW_CONVB + 3, 0:512] = dbb_ref[0:1, :]
        rep[ROW_CONVB + 2:ROW_CONVB + 3, 512:1024] = dbc_ref[0:1, :]
        rep[ROW_DTB:ROW_DTB + 1, 0:128] = ddtb_ref[0:1, :]
        rep[ROW_ALOG:ROW_ALOG + 1, 0:128] = dal_ref[0:1, :]
        rep[ROW_DSKIP:ROW_DSKIP + 1, 0:128] = ddsk_ref[0:1, :]
        rep[ROW_SINK:ROW_SINK + 1, 0:128] = dsink_ref[0:1, :]
        rep[ROW_GSSM:ROW_GSSM + 1, :] = dgn_ref[0:1, 0:1024]
        rep[ROW_GSSM + 1:ROW_GSSM + 2, :] = dgn_ref[0:1, 1024:2048]
        rep[ROW_GPOST:ROW_GPOST + 1, :] = dgp_ref[0:1, :]
        cw = jnp.concatenate([dwx_ref[...], dwb_ref[...], dwc_ref[...]], axis=1)
        mh = dh_ref[...]
        o_ref[...] = jnp.zeros_like(o_ref)
        for p in range(N_DEV):
            o_ref[p, 0:REP_ROWS, :] = rep[...]
            o_ref[p, ROW_CONVW:ROW_CONVW + 8, 0:CW_SHARD] = cw[:, p * CW_SHARD:(p + 1) * CW_SHARD]
            o_ref[p, ROW_META:ROW_META + N_META, 0:META_SHARD] = mh[:, p * META_SHARD:(p + 1) * META_SHARD]

    ins = [dgpre, dbx, dbb, dbc, ddtb, dal, ddsk, dsink, dgn, dgp, dwx, dwb, dwc, loss]
    return pl.pallas_call(
        body, grid=(1,),
        in_specs=[pl.BlockSpec(a.shape, lambda i: (0, 0)) for a in ins]
        + [pl.BlockSpec((N_META, D_MODEL), lambda i: (PAD // N_META, 0))],
        out_specs=pl.BlockSpec((N_DEV, SM_ROWS, 1024), lambda i: (0, 0, 0)),
        out_shape=SDS((N_DEV, SM_ROWS, 1024), F32), scratch_shapes=[pltpu.VMEM((REP_ROWS, 1024), F32)],
        name="small_pack")(*ins, dh)


def _small_finish(recv, params):
    npar = len(params)

    def body(*refs):
        r_ref = refs[0]
        wmv = refs[1:1 + 3 * npar]
        outs = refs[1 + 3 * npar:1 + 7 * npar]
        loss_ref = refs[1 + 7 * npar]
        gs = refs[-1]
        g = r_ref[0]
        for d in range(1, recv.shape[0]):
            g = g + r_ref[d]
        gs[...] = g
        loss_ref[...] = gs[ROW_LOSS:ROW_LOSS + 1, 0:128]
        grads = [
            gs[ROW_GPRE:ROW_GPRE + 1, :],
            jnp.concatenate([gs[ROW_CONVB + k:ROW_CONVB + k + 1, :] for k in range(3)], axis=1),
            gs[ROW_DTB:ROW_DTB + 1, 0:SSM_HEADS], gs[ROW_ALOG:ROW_ALOG + 1, 0:SSM_HEADS],
            gs[ROW_DSKIP:ROW_DSKIP + 1, 0:SSM_HEADS], gs[ROW_SINK:ROW_SINK + 1, 0:Q_HEADS],
            jnp.concatenate([gs[ROW_GSSM:ROW_GSSM + 1, :], gs[ROW_GSSM + 1:ROW_GSSM + 2, :]], axis=1),
            gs[ROW_GPOST:ROW_GPOST + 1, :],
            gs[ROW_CONVW:ROW_CONVW + 4, 0:CW_SHARD],
            gs[ROW_META:ROW_META + N_META, 0:META_SHARD]]
        for i in range(npar):
            w_ref, m_ref, v_ref = wmv[3 * i:3 * i + 3]
            delta, nm, nv = _adamw_math(w_ref[...], grads[i], m_ref[...], v_ref[...])
            outs[4 * i][...] = grads[i]
            outs[4 * i + 1][...] = delta
            outs[4 * i + 2][...] = nm
            outs[4 * i + 3][...] = nv

    flat = [a for wmv in params for a in wmv]
    res = pl.pallas_call(
        body, out_shape=[SDS(wmv[0].shape, F32) for wmv in params for _ in range(4)] + [SDS((1, 128), F32)],
        scratch_shapes=[pltpu.VMEM((SM_ROWS, 1024), F32)], name="small_finish")(recv, *flat)
    return [tuple(res[4 * i:4 * i + 4]) for i in range(npar)], res[4 * npar]


def _slab(ref, px, py, pc):
    return ref.at[4 * px + 2 * py + pc]


def _bounce(src, dst, buf, sem):
    cp = pltpu.make_async_copy(src, buf, sem)
    cp.start()
    cp.wait()
    cp = pltpu.make_async_copy(buf, dst, sem)
    cp.start()
    cp.wait()


def _ag_program(ins, outs, scratch):
    na = len(ins)
    send_sems, recv_sems, local_sems = scratch[:3]
    bufs = scratch[3:]
    x, y, c = lax.axis_index("x"), lax.axis_index("y"), lax.axis_index("c")
    me, sibling = (x, y, c), (x, y, 1 - c)
    chips = [(1 - x, y), (x, 1 - y), (1 - x, 1 - y)]

    def copy(a, k, block, to, src=None):
        dst = _slab(outs[a], *block)
        return pltpu.make_async_remote_copy(
            src_ref=dst if src is None else src, dst_ref=dst, send_sem=send_sems.at[a, k],
            recv_sem=recv_sems.at[a, k], device_id=to, device_id_type=MESH)

    def own_sends():
        out = []
        for a in range(na):
            out.append(copy(a, 0, me, sibling, src=ins[a]))
            out += [copy(a, 1 + j, me, (*chip, c), src=ins[a]) for j, chip in enumerate(chips)]
        return out

    def start():
        for cp in own_sends():
            cp.start()
        for a in range(na):
            _bounce(ins[a], _slab(outs[a], *me), bufs[a], local_sems.at[a])

    def forward():
        for j, chip in enumerate(chips):
            for a in range(na):
                copy(a, 1 + j, (*chip, c), me).wait_recv()
                copy(a, 4 + j, (*chip, c), sibling).start()

    def finish():
        for a in range(na):
            copy(a, 0, sibling, me).wait_recv()
            for j, chip in enumerate(chips):
                copy(a, 4 + j, (*chip, 1 - c), me).wait_recv()
        for cp in own_sends():
            cp.wait_send()
        for j, chip in enumerate(chips):
            for a in range(na):
                copy(a, 4 + j, (*chip, c), sibling).wait_send()

    return start, forward, finish


def _ag_scratch(shards):
    na = len(shards)
    return [pltpu.SemaphoreType.DMA((na, 7)), pltpu.SemaphoreType.DMA((na, 7)),
            pltpu.SemaphoreType.DMA((na,))] + [pltpu.VMEM(s.shape, s.dtype) for s in shards]


def _all_gather(shards):
    na = len(shards)

    def body(*refs):
        start, forward, finish = _ag_program(refs[:na], refs[na:2 * na], refs[2 * na:])
        start()
        forward()
        finish()

    return pl.pallas_call(
        body, in_specs=[ANY] * na, out_specs=[ANY] * na,
        out_shape=[SDS((N_DEV,) + s.shape, s.dtype) for s in shards],
        scratch_shapes=_ag_scratch(shards), name="all_gather")(*shards)


N_CHIP = 4


def _pair_sum(own, got, name):
    na = len(own)

    def body(*refs):
        for a in range(na):
            o_ref, g_ref, s_ref = refs[a], refs[na + a], refs[2 * na + a]
            s_ref[...] = (o_ref[...].astype(F32) + g_ref[...].astype(F32)).astype(s_ref.dtype)

    def spec(p):
        nd = len(p.shape) - 1
        return pl.BlockSpec((1,) + p.shape[1:], lambda k, nd=nd: (k,) + (0,) * nd)

    return pl.pallas_call(
        body, grid=(N_CHIP,), in_specs=[spec(p) for p in own] + [spec(p) for p in got],
        out_specs=[spec(p) for p in own], out_shape=[SDS(p.shape, p.dtype) for p in own],
        compiler_params=_cparams(), name=name)(*own, *got)


def _chips_program(ins, outs, scratch):
    na = len(ins)
    send_sems, recv_sems, local_sems = scratch[:3]
    bufs = scratch[3:]
    x, y, c = lax.axis_index("x"), lax.axis_index("y"), lax.axis_index("c")
    mine = 2 * x + y
    chips = [(1 - x, y), (x, 1 - y), (1 - x, 1 - y)]

    def send(a, j):
        px, py = chips[j]
        return pltpu.make_async_remote_copy(
            src_ref=ins[a].at[2 * px + py], dst_ref=outs[a].at[mine], send_sem=send_sems.at[a, j],
            recv_sem=recv_sems.at[a, j], device_id=(px, py, c), device_id_type=MESH)

    def arrival(a, j):
        px, py = chips[j]
        return pltpu.make_async_remote_copy(
            src_ref=ins[a].at[2 * px + py], dst_ref=outs[a].at[2 * px + py], send_sem=send_sems.at[a, j],
            recv_sem=recv_sems.at[a, j], device_id=(px, py, c), device_id_type=MESH)

    def start():
        for a in range(na):
            for j in range(3):
                send(a, j).start()
        for a in range(na):
            _bounce(ins[a].at[mine], outs[a].at[mine], bufs[a], local_sems.at[a])

    def finish():
        for a in range(na):
            for j in range(3):
                arrival(a, j).wait_recv()
        for a in range(na):
            for j in range(3):
                send(a, j).wait_send()

    return start, finish


def _chips_scratch(parts):
    na = len(parts)
    return [pltpu.SemaphoreType.DMA((na, 3)), pltpu.SemaphoreType.DMA((na, 3)),
            pltpu.SemaphoreType.DMA((na,))] + [pltpu.VMEM(p.shape[1:], p.dtype) for p in parts]


def _direct_program(ins, outs, scratch):
    na = len(ins)
    send_sems, recv_sems, local_sems = scratch[:3]
    bufs = scratch[3:]
    x, y, c = lax.axis_index("x"), lax.axis_index("y"), lax.axis_index("c")
    me = (x, y, c)
    peers = []
    for k in range(1, N_DEV):
        dx, dy, dc = (k >> 2) & 1, (k >> 1) & 1, k & 1
        peers.append(((1 - x) if dx else x, (1 - y) if dy else y, (1 - c) if dc else c))

    def send(a, k):
        return pltpu.make_async_remote_copy(
            src_ref=_slab(ins[a], *peers[k]), dst_ref=_slab(outs[a], *me), send_sem=send_sems.at[a, k],
            recv_sem=recv_sems.at[a, k], device_id=peers[k], device_id_type=MESH)

    def arrival(a, k):
        return pltpu.make_async_remote_copy(
            src_ref=_slab(ins[a], *peers[k]), dst_ref=_slab(outs[a], *peers[k]), send_sem=send_sems.at[a, k],
            recv_sem=recv_sems.at[a, k], device_id=peers[k], device_id_type=MESH)

    def start():
        for a in range(na):
            for k in range(N_DEV - 1):
                send(a, k).start()
        for a in range(na):
            _bounce(_slab(ins[a], *me), _slab(outs[a], *me), bufs[a], local_sems.at[a])

    def finish():
        for a in range(na):
            for k in range(N_DEV - 1):
                arrival(a, k).wait_recv()
        for a in range(na):
            for k in range(N_DEV - 1):
                send(a, k).wait_send()

    return start, finish


def _direct_scratch(parts):
    na = len(parts)
    return [pltpu.SemaphoreType.DMA((na, N_DEV - 1)), pltpu.SemaphoreType.DMA((na, N_DEV - 1)),
            pltpu.SemaphoreType.DMA((na,))] + [pltpu.VMEM(p.shape[1:], p.dtype) for p in parts]


ROW_TILES = D_MODEL // 128


def _rows3(t):
    return jnp.transpose(t[0]).reshape(t.shape[2], ROW_TILES, 128)


def _unrows3(t):
    return jnp.transpose(t.reshape(t.shape[0], D_MODEL))[None]


def _cast_shards(w_in3, w_att, w_ssm, w_o):
    def body(wi_ref, wa_ref, ws_ref, wo_ref, a_ref, b_ref, c_ref, d_ref):
        a_ref[...] = wi_ref[...].reshape(SHARD_IN // 2, 2 * ROW_TILES, 128).astype(BF16)
        b_ref[...] = wa_ref[...].astype(BF16)
        c_ref[...] = ws_ref[...].astype(BF16)
        d_ref[...] = wo_ref[...].astype(BF16)

    return pl.pallas_call(
        body, out_shape=[SDS((SHARD_IN // 2, 2 * ROW_TILES, 128), BF16), SDS(w_att.shape, BF16),
                         SDS(w_ssm.shape, BF16), SDS(w_o.shape, BF16)],
        compiler_params=_cparams(), name="cast_shards")(w_in3, w_att, w_ssm, w_o)


def _pieces():
    out = []
    for r0, c0, w in _SEGS:
        r = r0
        while r < r0 + w:
            d = r // SHARD_IN
            n = min(r0 + w, (d + 1) * SHARD_IN) - r
            out.append((c0 + (r - r0), d, r - d * SHARD_IN, n))
            r += n
    return out


def _to_aligned_t(slabs):
    def body(a_ref, o_ref):
        for (t, d, s, n) in _pieces():
            o_ref[t:t + n, :] = a_ref[d, s // 2:(s + n) // 2].reshape(n, D_MODEL)
        o_ref[C_DT + 32:C_DT + 128, :] = jnp.zeros((96, D_MODEL), slabs.dtype)

    return pl.pallas_call(body, out_shape=SDS((PW, D_MODEL), slabs.dtype), compiler_params=_cparams(),
                          name="to_aligned")(slabs)


def _from_aligned_pair(g):
    slab = (SHARD_IN // 2, 2 * ROW_TILES, 128)
    by_slab = [[p for p in _pieces() if p[1] == d] for d in range(N_DEV)]

    def body(g_ref, own_ref, got_ref, slabs, send_sems, recv_sems, local_sems):
        x, y, c = lax.axis_index("x"), lax.axis_index("y"), lax.axis_index("c")
        sibling = (x, y, 1 - c)

        def to_own(d, k):
            return pltpu.make_async_copy(slabs.at[d], own_ref.at[k], local_sems.at[k])

        def to_sibling(d, k):
            return pltpu.make_async_remote_copy(
                src_ref=slabs.at[d], dst_ref=got_ref.at[k], send_sem=send_sems.at[k], recv_sem=recv_sems.at[k],
                device_id=sibling, device_id_type=MESH)

        for d in range(N_DEV):
            for (t, _, s, n) in by_slab[d]:
                slabs[d, s // 2:(s + n) // 2] = g_ref[t:t + n, :].reshape(n // 2, 2 * ROW_TILES, 128)
            k, side = d // 2, d % 2
            pl.when(c == side)(to_own(d, k).start)
            pl.when(c != side)(to_sibling(d, k).start)
        for k in range(N_CHIP):
            to_own(0, k).wait()
            to_sibling(0, k).wait()

    half = SDS((N_CHIP,) + slab, g.dtype)
    return pl.pallas_call(
        body, in_specs=[pl.BlockSpec(memory_space=pltpu.VMEM)], out_specs=[ANY, ANY], out_shape=[half, half],
        scratch_shapes=[pltpu.VMEM((N_DEV,) + slab, g.dtype), pltpu.SemaphoreType.DMA((N_CHIP,)),
                        pltpu.SemaphoreType.DMA((N_CHIP,)), pltpu.SemaphoreType.DMA((N_CHIP,))],
        compiler_params=_cparams(), name="from_aligned_pair")(g)


_SEGS = [
    (R_Q, C_Q, 1024), (R_K, C_K, 256), (R_V, C_V, 256), (R_ZA, C_ZA, 1024), (R_ZS, C_ZS, 2048),
    (R_XBC, C_XBC, 3072), (R_DT, C_DT, 32), (R_GA, C_GA, 1024), (R_GS, C_GS, 1024)]


def _pad_lanes(v, n=128):
    return jnp.pad(v, ((0, 0), (0, n - v.shape[1])))


def _device_step(h, tgt, w_alt, w_out, g_pre, conv_w8, conv_b, dt_bias, a_log, d_skip, sinks, g_ssm, g_post, on_mesh):
    dtb, al, dsk, snk = _pad_lanes(dt_bias), _pad_lanes(a_log), _pad_lanes(d_skip), _pad_lanes(sinks)
    u = _norm_u(h, g_pre)
    proj = _matmul(u, w_alt, "nt", F32, T, 896, "in_proj")
    o = _attn_fwd(proj, snk)
    xbc_act = _conv_fwd(proj, conv_w8, conv_b)
    if on_mesh:
        sn, states, att_all, ssm_all, o_all = _ssd_fwd(xbc_act, proj, dtb, al, dsk, g_ssm, gather=w_out)
        w_att = att_all.reshape(D_MODEL, D_MODEL)
        w_ssm = ssm_all.reshape(SSM_INNER, D_MODEL)
        w_o = o_all.reshape(D_MODEL, D_MODEL)
    else:
        sn, states = _ssd_fwd(xbc_act, proj, dtb, al, dsk, g_ssm)
        w_att, w_ssm, w_o = w_out
    a_in, mg, ya, ys, out = _post_a(o, proj, sn, w_att, w_ssm, w_o)
    (loss, dres, dout, dya, dys, do, dproj, dsn, dgp) = _post_b(
        out, h, tgt, proj, ya, ys, o, g_post, w_att, w_ssm, w_o)
    dw_att = _matmul(a_in, dya, "tn", BF16, D_MODEL, D_MODEL, "d_w_att")
    dw_ssm = _matmul(sn, dys, "tn", BF16, D_MODEL, D_MODEL, "d_w_ssm")
    dw_o = _matmul(mg, dout, "tn", BF16, D_MODEL, D_MODEL, "d_w_o")
    res = {}
    if on_mesh:
        parts = [dw_att.reshape(N_DEV, 128, D_MODEL), dw_ssm.reshape(N_DEV, 256, D_MODEL),
                 dw_o.reshape(N_DEV, 128, D_MODEL)]
        (dxs, dbm, dcm, ddt4, dproj, ddtb, dal, ddsk, dgn, res["r_att"], res["r_ssm"], res["r_o"]) = _ssd_bwd(
            xbc_act, proj, dtb, al, dsk, g_ssm, states, dsn, dproj, exchange=parts)
    else:
        dxs, dbm, dcm, ddt4, dproj, ddtb, dal, ddsk, dgn = _ssd_bwd(xbc_act, proj, dtb, al, dsk, g_ssm, states, dsn,
                                                                    dproj)
        res.update(dw_att=dw_att, dw_ssm=dw_ssm, dw_o=dw_o)
    dproj, dwx, dbx = _conv_bwd(proj, conv_w8, conv_b, dxs, 0, dproj, "conv_bwd_x")
    dproj, dwb, dbb = _conv_bwd(proj, conv_w8, conv_b, dbm, SSM_INNER, dproj, "conv_bwd_b")
    dproj, dwc, dbc = _conv_bwd(proj, conv_w8, conv_b, dcm, SSM_INNER + GRP_W, dproj, "conv_bwd_c")
    dproj, dk, dv, dsink = _attn_bwd(proj, snk, do, dproj)
    dproj = _dproj_tail(dproj, dk, dv, ddt4)
    dw_alt = _matmul(dproj, u, "tn", BF16, 896, D_MODEL, "d_w_in")
    if on_mesh:
        own, got = _from_aligned_pair(dw_alt)
        dh, dgpre, res["r_in"] = _d_u_norm(dproj, w_alt, h, g_pre, dres,
                                           chips=_pair_sum([own], [got], "pair_sum_w_in"))
    else:
        dh, dgpre = _d_u_norm(dproj, w_alt, h, g_pre, dres)
        res["dw_alt"] = dw_alt
    small = (dgpre, dbx, dbb, dbc, ddtb, dal, ddsk, dsink, dgn, dgp, dwx, dwb, dwc)
    if on_mesh:
        res["small_pack"] = _small_pack(*small, loss, dh)
    else:
        res["small"] = small
    res.update(loss=loss[0, 0], dh=dh)
    return res


def kernel(x, meta_tokens, g_pre, w_in, conv_w, conv_b, dt_bias, a_log, d_skip, attn_sinks, g_ssm_norm, w_out_att, w_out_ssm, w_out, g_post, loss_target, m_meta_tokens, m_g_pre, m_w_in, m_conv_w, m_conv_b, m_dt_bias, m_a_log, m_d_skip, m_attn_sinks, m_g_ssm_norm, m_w_out_att, m_w_out_ssm, m_w_out, m_g_post, v_meta_tokens, v_g_pre, v_w_in, v_conv_w, v_conv_b, v_dt_bias, v_a_log, v_d_skip, v_attn_sinks, v_g_ssm_norm, v_w_out_att, v_w_out_ssm, v_w_out, v_g_post):
    w_in3, m_in3, v_in3 = _rows3(w_in), _rows3(m_w_in), _rows3(v_w_in)
    a_sh, att_sh, ssm_sh, o_sh = _cast_shards(w_in3, w_out_att[0], w_out_ssm[0], w_out[0])
    cw_sh = jnp.pad(conv_w[0], ((0, 4), (0, 0)))
    a_all, meta_all, cw_all = _all_gather([a_sh, meta_tokens, cw_sh])
    w_alt = _to_aligned_t(a_all)
    meta_full = meta_all.transpose(1, 0, 2).reshape(N_META, D_MODEL)
    conv_w8 = cw_all.transpose(1, 0, 2).reshape(8, CONV_DIM)

    h = jnp.concatenate([jnp.zeros((PAD, D_MODEL), F32), meta_full, x[0]], axis=0)
    tgt = jnp.concatenate([jnp.zeros((PAD + N_META, D_MODEL), F32), loss_target[0]], axis=0)
    r = _device_step(h, tgt, w_alt, (att_sh, ssm_sh, o_sh), g_pre, conv_w8, conv_b, dt_bias, a_log, d_skip,
                     attn_sinks, g_ssm_norm, g_post, True)
    grad_x = r["dh"][PAD + N_META:][None]

    *res_in, r_small = _sum_adamw_rows3(r["r_in"], w_in3, m_in3, v_in3, "adamw_w_in", exchange=[r["small_pack"]])
    res_in = [_unrows3(t) for t in res_in]
    res_att = [t[None] for t in _sum_adamw(r["r_att"], w_out_att[0], m_w_out_att[0], v_w_out_att[0], 512,
                                           "adamw_w_att")]
    res_ssm = [t[None] for t in _sum_adamw(r["r_ssm"], w_out_ssm[0], m_w_out_ssm[0], v_w_out_ssm[0], 512,
                                           "adamw_w_ssm")]
    res_o = [t[None] for t in _sum_adamw(r["r_o"], w_out[0], m_w_out[0], v_w_out[0], 512, "adamw_w_o")]
    (res_gpre, res_convb, res_dtb, res_alog, res_dskip, res_sink, res_gssm, res_gpost, res_cw, res_meta), loss = _small_finish(
        r_small, [(g_pre, m_g_pre, v_g_pre), (conv_b, m_conv_b, v_conv_b), (dt_bias, m_dt_bias, v_dt_bias),
                       (a_log, m_a_log, v_a_log), (d_skip, m_d_skip, v_d_skip),
                       (attn_sinks, m_attn_sinks, v_attn_sinks), (g_ssm_norm, m_g_ssm_norm, v_g_ssm_norm),
                       (g_post, m_g_post, v_g_post), (conv_w[0], m_conv_w[0], v_conv_w[0]),
                       (meta_tokens, m_meta_tokens, v_meta_tokens)])
    res_cw = [t[None] for t in res_cw]
    per_weight = [res_meta, res_gpre, res_in, res_cw, res_convb, res_dtb, res_alog, res_dskip, res_sink, res_gssm,
                  res_att, res_ssm, res_o, res_gpost]
    return (loss[0, 0], grad_x, *[p[0] for p in per_weight], *[p[1] for p in per_weight], *[p[2] for p in per_weight],
            *[p[3] for p in per_weight])
```

```python
import functools
import math

import jax
import jax.numpy as jnp
from jax import lax
from jax.experimental import pallas as pl
from jax.experimental.pallas import tpu as pltpu

F32 = jnp.float32
BF16 = jnp.bfloat16
SDS = jax.ShapeDtypeStruct
MESH = pl.DeviceIdType.MESH
ANY = pl.BlockSpec(memory_space=pl.ANY)

N_DEV = 8
D_MODEL = 1024
SEQ = 2048
N_META = 16
BLK = 128
PAD = 112
T = PAD + N_META + SEQ
NB = T // BLK
EPS = 1e-6
HEAD = 64
Q_HEADS = 16
KV_HEADS = 4
GROUP = 4
KV_W = 256
SSM_INNER = 2048
SSM_HEADS = 32
SSM_GROUPS = 4
GRP_W = 512
SSM_STATE = 128
CONV_DIM = 3072
IN_PROJ = 9760
SHARD_IN = IN_PROJ // N_DEV
NEG = -1e30

C_ZA, C_GA, C_GS, C_Q, C_ZS, C_XBC, C_K, C_V, C_DT = 0, 1024, 2048, 3072, 4096, 6144, 9216, 9472, 9728
PW = 9856
R_Q, R_K, R_V, R_ZA, R_ZS, R_XBC, R_DT, R_GA, R_GS = 0, 1024, 1280, 1536, 2560, 4608, 7680, 7712, 8736

ADAM_LR, ADAM_B1, ADAM_B2, ADAM_EPS, ADAM_WD, ADAM_STEP = 0.001, 0.9, 0.999, 1e-08, 0.01, 10

VMEM_LIMIT = 56 * 1024 * 1024


def _cparams():
    return pltpu.CompilerParams(vmem_limit_bytes=VMEM_LIMIT)


def _silu(x):
    return x * jax.nn.sigmoid(x)


def _dsilu(x):
    s = jax.nn.sigmoid(x)
    return s * (1.0 + x * (1.0 - s))


def _matmul(a, b, mode, out_dtype, tm, tn, name):
    if mode == "nt":
        (m, k), n = a.shape, b.shape[0]
        a_spec = pl.BlockSpec((tm, k), lambda i, j: (i, 0))
        b_spec = pl.BlockSpec((tn, k), lambda i, j: (j, 0))
        dims = (((1,), (1,)), ((), ()))
    else:
        assert mode == "tn"
        (k, m), n = a.shape, b.shape[1]
        a_spec = pl.BlockSpec((k, tm), lambda i, j: (0, i))
        b_spec = pl.BlockSpec((k, tn), lambda i, j: (0, j))
        dims = (((0,), (0,)), ((), ()))
    assert m % tm == 0 and n % tn == 0, (a.shape, b.shape, tm, tn)

    def body(a_ref, b_ref, o_ref):
        o_ref[...] = lax.dot_general(a_ref[...], b_ref[...], dims, preferred_element_type=F32).astype(out_dtype)

    return pl.pallas_call(
        body, grid=(m // tm, n // tn), in_specs=[a_spec, b_spec],
        out_specs=pl.BlockSpec((tm, tn), lambda i, j: (i, j)), out_shape=SDS((m, n), out_dtype),
        compiler_params=_cparams(), name=name)(a, b)


def _norm_u(h, g_pre):
    def body(h_ref, g_ref, u_ref):
        x = h_ref[...]
        r = lax.rsqrt(jnp.mean(x * x, axis=-1, keepdims=True) + EPS)
        u_ref[...] = (x * r * g_ref[...]).astype(BF16)

    return pl.pallas_call(
        body, grid=(NB,),
        in_specs=[pl.BlockSpec((BLK, D_MODEL), lambda i: (i, 0)), pl.BlockSpec((1, D_MODEL), lambda i: (0, 0))],
        out_specs=pl.BlockSpec((BLK, D_MODEL), lambda i: (i, 0)),
        out_shape=SDS((T, D_MODEL), BF16), name="norm_u")(h, g_pre)


DU_TM, DU_TK = T // 2, 1408


def _d_u_norm(dproj, w_alt, h, g_pre, dres, chips=()):
    nk = PW // DU_TK
    ni = T // DU_TM
    nc = len(chips)

    def body(*refs):
        a_ref, b_ref, h_ref, g_ref, dres_ref = refs[:5]
        dh_ref, dg_ref = refs[5 + nc:7 + nc]
        acc_ref = refs[7 + 2 * nc]
        i, kk = pl.program_id(0), pl.program_id(1)
        if nc:
            ch_start, ch_finish = _chips_program(refs[5:5 + nc], refs[7 + nc:7 + 2 * nc], refs[8 + 2 * nc:])
            pl.when((i == 0) & (kk == 0))(ch_start)
        part = jnp.dot(a_ref[...], b_ref[...], preferred_element_type=F32)

        @pl.when(kk == 0)
        def _():
            acc_ref[...] = part

        @pl.when((kk > 0) & (kk < nk - 1))
        def _():
            acc_ref[...] += part

        @pl.when(kk == nk - 1)
        def _():
            du_ = acc_ref[...] + part
            x = h_ref[...]
            r = lax.rsqrt(jnp.mean(x * x, axis=-1, keepdims=True) + EPS)
            gd = g_ref[...] * du_
            dx = r * gd - x * (r * r * r) * jnp.mean(x * gd, axis=-1, keepdims=True)
            dh_ref[...] = dx + dres_ref[...]
            gpart = jnp.concatenate([jnp.sum(du_ * x * r, axis=0, keepdims=True), jnp.zeros((7, D_MODEL), F32)],
                                    axis=0)

            @pl.when(i == 0)
            def _():
                dg_ref[...] = gpart

            @pl.when(i > 0)
            def _():
                dg_ref[...] += gpart

        if nc:
            pl.when((i == ni - 1) & (kk == nk - 1))(ch_finish)

    row = pl.BlockSpec((DU_TM, D_MODEL), lambda i, kk: (i, 0))
    return pl.pallas_call(
        body, grid=(ni, nk),
        in_specs=[pl.BlockSpec((DU_TM, DU_TK), lambda i, kk: (i, kk)),
                  pl.BlockSpec((DU_TK, D_MODEL), lambda i, kk: (kk, 0)),
                  row, pl.BlockSpec((1, D_MODEL), lambda i, kk: (0, 0)), row] + [ANY] * nc,
        out_specs=[row, pl.BlockSpec((8, D_MODEL), lambda i, kk: (0, 0))] + [ANY] * nc,
        out_shape=[SDS((T, D_MODEL), F32), SDS((8, D_MODEL), F32)] + [SDS(p.shape, p.dtype) for p in chips],
        scratch_shapes=[pltpu.VMEM((DU_TM, D_MODEL), F32)] + (_chips_scratch(chips) if nc else []),
        compiler_params=_cparams(), name="d_u_norm")(dproj, w_alt, h, g_pre, dres, *chips)


def _lane_pick(row, h):
    lane = lax.broadcasted_iota(jnp.int32, row.shape, 1)
    return jnp.sum(jnp.where(lane == h, row, 0.0), axis=1, keepdims=True)


def _attn_fn(q4s, kcats, vcats, kms, vms, sinks, n):
    r = lax.broadcasted_iota(jnp.int32, (GROUP * BLK, 2 * BLK), 0)
    s = lax.broadcasted_iota(jnp.int32, (GROUP * BLK, 2 * BLK), 1)
    i = jnp.bitwise_and(r, BLK - 1)
    gi = jnp.right_shift(r, 7)
    rel = i - s + BLK
    k_pos = n * BLK - BLK + s
    band_ok = (rel >= 0) & (rel < BLK) & (k_pos >= PAD + N_META)
    relf = rel.astype(F32)
    rm = lax.broadcasted_iota(jnp.int32, (GROUP * BLK, N_META), 0)
    mm = lax.broadcasted_iota(jnp.int32, (GROUP * BLK, N_META), 1)
    meta_ok = (PAD + mm) <= (n * BLK + jnp.bitwise_and(rm, BLK - 1))
    gcol = jnp.right_shift(lax.broadcasted_iota(jnp.int32, (GROUP * BLK, 1), 0), 7)
    outs = []
    for kh in range(KV_HEADS):
        slopes = [2.0 ** (-8.0 * (kh * GROUP + g + 1) / Q_HEADS) for g in range(GROUP)]
        slope = jnp.where(gi == 0, slopes[0], jnp.where(gi == 1, slopes[1], jnp.where(gi == 2, slopes[2], slopes[3])))
        sk = [_lane_pick(sinks, kh * GROUP + g) for g in range(GROUP)]
        sink = jnp.where(gcol == 0, sk[0], jnp.where(gcol == 1, sk[1], jnp.where(gcol == 2, sk[2], sk[3])))
        qb = (q4s[kh] * (HEAD ** -0.5)).astype(BF16)
        sb = lax.dot_general(qb, kcats[kh].astype(BF16), (((1,), (1,)), ((), ())), preferred_element_type=F32)
        sb = jnp.where(band_ok, sb - slope * relf, NEG)
        sm = lax.dot_general(qb, kms[kh].astype(BF16), (((1,), (1,)), ((), ())), preferred_element_type=F32)
        sm = jnp.where(meta_ok, sm, NEG)
        mx = jnp.maximum(jnp.maximum(jnp.max(sb, axis=1, keepdims=True), jnp.max(sm, axis=1, keepdims=True)), sink)
        mx = lax.stop_gradient(mx)
        eb = jnp.exp(sb - mx)
        em = jnp.exp(sm - mx)
        es = jnp.exp(sink - mx)
        inv = 1.0 / (jnp.sum(eb, axis=1, keepdims=True) + jnp.sum(em, axis=1, keepdims=True) + es)
        pb = (eb * inv).astype(BF16)
        pm = (em * inv).astype(BF16)
        o4 = (jnp.dot(pm, vms[kh].astype(BF16), preferred_element_type=F32)
              + jnp.dot(pb, vcats[kh].astype(BF16), preferred_element_type=F32))
        outs.append(o4)
    return outs


def _attn_specs():
    prev = lambda n: jnp.maximum(n - 1, 0)
    return [
        pl.BlockSpec((BLK, D_MODEL), lambda n: (n, C_Q // D_MODEL)),
        pl.BlockSpec((BLK, KV_W), lambda n: (prev(n), C_K // KV_W)),
        pl.BlockSpec((BLK, KV_W), lambda n: (n, C_K // KV_W)),
        pl.BlockSpec((BLK, KV_W), lambda n: (prev(n), C_V // KV_W)),
        pl.BlockSpec((BLK, KV_W), lambda n: (n, C_V // KV_W)),
        pl.BlockSpec((N_META, KV_W), lambda n: (PAD // N_META, C_K // KV_W)),
        pl.BlockSpec((N_META, KV_W), lambda n: (PAD // N_META, C_V // KV_W)),
        pl.BlockSpec((1, 128), lambda n: (0, 0)),
    ]


def _attn_load(q_ref, kp_ref, kc_ref, vp_ref, vc_ref, km_ref, vm_ref):
    q4s, kcats, vcats, kms, vms = [], [], [], [], []
    for kh in range(KV_HEADS):
        q4s.append(jnp.concatenate(
            [q_ref[:, (kh * GROUP + g) * HEAD:(kh * GROUP + g + 1) * HEAD] for g in range(GROUP)], axis=0))
        cs = slice(kh * HEAD, (kh + 1) * HEAD)
        kcats.append(jnp.concatenate([kp_ref[:, cs], kc_ref[:, cs]], axis=0))
        vcats.append(jnp.concatenate([vp_ref[:, cs], vc_ref[:, cs]], axis=0))
        kms.append(km_ref[:, cs])
        vms.append(vm_ref[:, cs])
    return q4s, kcats, vcats, kms, vms


def _attn_fwd(proj, sinks):
    def body(q_ref, kp_ref, kc_ref, vp_ref, vc_ref, km_ref, vm_ref, s_ref, o_ref):
        n = pl.program_id(0)
        args = _attn_load(q_ref, kp_ref, kc_ref, vp_ref, vc_ref, km_ref, vm_ref)
        outs = _attn_fn(*args, s_ref[...], n)
        for kh in range(KV_HEADS):
            for g in range(GROUP):
                hh = kh * GROUP + g
                o_ref[:, hh * HEAD:(hh + 1) * HEAD] = outs[kh][g * BLK:(g + 1) * BLK]

    return pl.pallas_call(
        body, grid=(NB,), in_specs=_attn_specs(),
        out_specs=pl.BlockSpec((BLK, D_MODEL), lambda n: (n, 0)),
        out_shape=SDS((T, D_MODEL), F32), name="attn_fwd")(proj, proj, proj, proj, proj, proj, proj, sinks)


def _attn_bwd(proj, sinks, do, dproj):
    def body(q_ref, kp_ref, kc_ref, vp_ref, vc_ref, km_ref, vm_ref, s_ref, do_ref, _, dq_ref, dk_ref, dv_ref, ds_ref):
        n = pl.program_id(0)

        @pl.when(n == 0)
        def _():
            dk_ref[...] = jnp.zeros_like(dk_ref)
            dv_ref[...] = jnp.zeros_like(dv_ref)
            ds_ref[...] = jnp.zeros_like(ds_ref)

        args = _attn_load(q_ref, kp_ref, kc_ref, vp_ref, vc_ref, km_ref, vm_ref)
        _, vjp = jax.vjp(lambda a, b, c, d, e, f: _attn_fn(a, b, c, d, e, f, n), *args, s_ref[...])
        do_f = do_ref[...].astype(F32)
        cot = [jnp.concatenate([do_f[:, (kh * GROUP + g) * HEAD:(kh * GROUP + g + 1) * HEAD] for g in range(GROUP)],
                               axis=0) for kh in range(KV_HEADS)]
        dq4s, dkcats, dvcats, dkms, dvms, dsk = vjp(cot)
        ds_ref[0:1, :] += dsk
        cur = pl.ds(pl.multiple_of(n * BLK, BLK), BLK)
        meta = slice(PAD, PAD + N_META)
        for kh in range(KV_HEADS):
            cs = slice(kh * HEAD, (kh + 1) * HEAD)
            for g in range(GROUP):
                hh = kh * GROUP + g
                dq_ref[:, hh * HEAD:(hh + 1) * HEAD] = dq4s[kh][g * BLK:(g + 1) * BLK].astype(BF16)
            dk_ref[cur, cs] += dkcats[kh][BLK:]
            dv_ref[cur, cs] += dvcats[kh][BLK:]
            dk_ref[meta, cs] += dkms[kh]
            dv_ref[meta, cs] += dvms[kh]

        @pl.when(n > 0)
        def _():
            prv = pl.ds(pl.multiple_of((n - 1) * BLK, BLK), BLK)
            for kh in range(KV_HEADS):
                cs = slice(kh * HEAD, (kh + 1) * HEAD)
                dk_ref[prv, cs] += dkcats[kh][:BLK]
                dv_ref[prv, cs] += dvcats[kh][:BLK]

    full_kv = pl.BlockSpec((T, KV_W), lambda n: (0, 0))
    return pl.pallas_call(
        body, grid=(NB,),
        in_specs=_attn_specs() + [pl.BlockSpec((BLK, D_MODEL), lambda n: (n, 0)), ANY],
        out_specs=[pl.BlockSpec((BLK, D_MODEL), lambda n: (n, C_Q // D_MODEL)), full_kv, full_kv,
                   pl.BlockSpec((8, 128), lambda n: (0, 0))],
        out_shape=[SDS((T, PW), BF16), SDS((T, KV_W), F32), SDS((T, KV_W), F32), SDS((8, 128), F32)],
        input_output_aliases={9: 0},
        name="attn_bwd")(proj, proj, proj, proj, proj, proj, proj, sinks, do, dproj)


def _conv_taps(xp, w, rows):
    return (w[0:1] * xp[5:5 + rows] + w[1:2] * xp[6:6 + rows] + w[2:3] * xp[7:7 + rows] + w[3:4] * xp[8:8 + rows])


def _conv_fwd(proj, conv_w, conv_b):
    CONV_CB = CONV_DIM
    ncb = CONV_DIM // CONV_CB
    cb0 = C_XBC // CONV_CB

    def body(tail_ref, cur_ref, w_ref, b_ref, o_ref):
        n = pl.program_id(1)
        tail = jnp.where(n > 0, tail_ref[...], 0.0)
        xp = jnp.concatenate([tail, cur_ref[...]], axis=0)
        conv = _conv_taps(xp, w_ref[...], BLK) + b_ref[...]
        row = n * BLK + lax.broadcasted_iota(jnp.int32, (BLK, 1), 0)
        o_ref[...] = jnp.where(row >= PAD, _silu(conv), 0.0)

    return pl.pallas_call(
        body, grid=(ncb, NB),
        in_specs=[pl.BlockSpec((8, CONV_CB), lambda j, n: (jnp.maximum(n * (BLK // 8) - 1, 0), cb0 + j)),
                  pl.BlockSpec((BLK, CONV_CB), lambda j, n: (n, cb0 + j)),
                  pl.BlockSpec((8, CONV_CB), lambda j, n: (0, j)),
                  pl.BlockSpec((1, CONV_CB), lambda j, n: (0, j))],
        out_specs=pl.BlockSpec((BLK, CONV_CB), lambda j, n: (n, j)),
        out_shape=SDS((T, CONV_DIM), F32), name="conv_fwd")(proj, proj, conv_w, conv_b)


def _conv_bwd(proj, conv_w, conv_b, dact, ch0, dproj, name):
    width = dact.shape[1]
    CONV_CB = width
    ncb = width // CONV_CB
    cb0 = (C_XBC + ch0) // CONV_CB
    wb0 = ch0 // CONV_CB
    last8 = T // 8 - 1

    def body(tail_ref, cur_ref, nxt_ref, w_ref, b_ref, dcur_ref, dnxt_ref, _, dx_ref, dw_ref, db_ref):
        n = pl.program_id(1)
        w = w_ref[...]
        tail = jnp.where(n > 0, tail_ref[...], 0.0)
        xp = jnp.concatenate([tail, cur_ref[...], nxt_ref[...]], axis=0)
        conv = _conv_taps(xp, w, BLK + 8) + b_ref[...]
        dext = jnp.concatenate([dcur_ref[...], jnp.where(n < NB - 1, dnxt_ref[...], 0.0)], axis=0)
        row = n * BLK + lax.broadcasted_iota(jnp.int32, (BLK + 8, 1), 0)
        dconv = jnp.where(row >= PAD, dext * _dsilu(conv), 0.0)
        dx = (w[0:1] * dconv[3:3 + BLK] + w[1:2] * dconv[2:2 + BLK] + w[2:3] * dconv[1:1 + BLK]
              + w[3:4] * dconv[0:BLK])
        dx_ref[...] = dx.astype(BF16)
        dc = dconv[0:BLK]
        dws = [jnp.sum(dc * xp[5 + k:5 + k + BLK], axis=0, keepdims=True) for k in range(4)]
        dwp = jnp.concatenate(dws + [jnp.zeros((4, CONV_CB), F32)], axis=0)
        dbp = jnp.sum(dc, axis=0, keepdims=True)

        @pl.when(n == 0)
        def _():
            dw_ref[...] = dwp
            db_ref[...] = jnp.concatenate([dbp, jnp.zeros((7, CONV_CB), F32)], axis=0)

        @pl.when(n > 0)
        def _():
            dw_ref[...] += dwp
            db_ref[0:1, :] += dbp

    return pl.pallas_call(
        body, grid=(ncb, NB),
        in_specs=[pl.BlockSpec((8, CONV_CB), lambda j, n: (jnp.maximum(n * (BLK // 8) - 1, 0), cb0 + j)),
                  pl.BlockSpec((BLK, CONV_CB), lambda j, n: (n, cb0 + j)),
                  pl.BlockSpec((8, CONV_CB), lambda j, n: (jnp.minimum((n + 1) * (BLK // 8), last8), cb0 + j)),
                  pl.BlockSpec((8, CONV_CB), lambda j, n: (0, wb0 + j)),
                  pl.BlockSpec((1, CONV_CB), lambda j, n: (0, wb0 + j)),
                  pl.BlockSpec((BLK, CONV_CB), lambda j, n: (n, j)),
                  pl.BlockSpec((8, CONV_CB), lambda j, n: (jnp.minimum((n + 1) * (BLK // 8), last8), j)), ANY],
        out_specs=[pl.BlockSpec((BLK, CONV_CB), lambda j, n: (n, cb0 + j)),
                   pl.BlockSpec((8, CONV_CB), lambda j, n: (0, j)),
                   pl.BlockSpec((8, CONV_CB), lambda j, n: (0, j))],
        out_shape=[SDS((T, PW), BF16), SDS((8, width), F32), SDS((8, width), F32)],
        input_output_aliases={7: 0},
        name=name)(proj, proj, proj, conv_w, conv_b, dact, dact, dproj)


HPG = SSM_HEADS // SSM_GROUPS


def _iota(shape, dim):
    return lax.broadcasted_iota(jnp.int32, shape, dim)


def _mm(a, b, ca=1, cb=0):
    return lax.dot_general(a.astype(BF16), b.astype(BF16), (((ca,), (cb,)), ((), ())), preferred_element_type=F32)


def _split3(v):
    hi = v.astype(BF16)
    r1 = v - hi.astype(F32)
    mid = r1.astype(BF16)
    lo = (r1 - mid.astype(F32)).astype(BF16)
    return hi, mid, lo


def _sel_r(parts, onehot, ca=1, cb=0):
    out = lax.dot_general(parts[0], onehot, (((ca,), (cb,)), ((), ())), preferred_element_type=F32)
    for p in parts[1:]:
        out = out + lax.dot_general(p, onehot, (((ca,), (cb,)), ((), ())), preferred_element_type=F32)
    return out


def _sel_l(onehot, parts):
    out = jnp.dot(onehot, parts[0], preferred_element_type=F32)
    for p in parts[1:]:
        out = out + jnp.dot(onehot, p, preferred_element_type=F32)
    return out


def _rows8(*rows):
    r = _iota((8, rows[0].shape[1]), 0)
    out = jnp.zeros((8, rows[0].shape[1]), F32)
    for k, v in enumerate(rows):
        out = jnp.where(r == k, v, out)
    return out


def _ssd_forward(x, z, bm, cm, dt_raw, st_prev, dtb, alog, dskip, gn, g, cst_scr):
    li, si = _iota((BLK, BLK), 0), _iota((BLK, BLK), 1)
    dt_all = jax.nn.softplus(dt_raw + dtb)
    a_row = -jnp.exp(alog)
    a_all = dt_all * a_row
    cs_all = _sel_l((li >= si).astype(BF16), _split3(a_all))
    cs_parts = _split3(cs_all)
    spread = (_iota((BLK, GRP_W), 0) == g * HPG + jnp.right_shift(_iota((BLK, GRP_W), 1), 6)).astype(BF16)
    dt_e = _sel_r(_split3(dt_all), spread)
    cs_e = _sel_r(cs_parts, spread)
    d_e = _sel_r(_split3(_rows8(dskip)), spread)[0:1]
    cs_last_e = jnp.sum(jnp.where(_iota((BLK, GRP_W), 0) == BLK - 1, cs_e, 0.0), axis=0, keepdims=True)
    p_e = jnp.exp(cs_e)
    w_e = jnp.exp(cs_last_e - cs_e)
    cd_e = jnp.exp(cs_last_e)
    xr = x * dt_e
    cst_scr[...] = cs_all.T
    cst_g = cst_scr[pl.ds(pl.multiple_of(g * HPG, HPG), HPG), :]
    own = jnp.right_shift(_iota((HPG, HPG * BLK), 1), 7) == _iota((HPG, HPG * BLK), 0)
    ownf = own.astype(F32)
    q_rows = [ownf, ownf, ownf] + [jnp.where(own, jnp.concatenate([p.astype(F32)] * HPG, axis=1), 0.0)
                                   for p in _split3(cst_g)]
    q2 = jnp.concatenate(q_rows + [jnp.zeros((BLK - 6 * HPG, HPG * BLK), F32)], axis=0).astype(BF16)
    lane1 = _iota((1, BLK), 1)
    p2 = jnp.where((lane1 >= 3 * HPG) & (lane1 < 6 * HPG), -1.0, 0.0)
    for k, part in enumerate(cs_parts):
        pick = ((li == g * HPG + si - k * HPG) & (si >= k * HPG) & (si < (k + 1) * HPG)).astype(BF16)
        p2 = p2 + jnp.dot(part, pick, preferred_element_type=F32)
    dmat = jnp.dot(p2.astype(BF16), q2, preferred_element_type=F32)
    causal = _iota((BLK, HPG * BLK), 0) >= jnp.bitwise_and(_iota((BLK, HPG * BLK), 1), BLK - 1)
    lam = jnp.exp(jnp.where(causal, dmat, NEG))
    gmat = _mm(cm, bm, 1, 1)
    m_all = lam * jnp.concatenate([gmat] * HPG, axis=1)
    mb = m_all.astype(BF16)
    lo = _iota((BLK, BLK), 1) < HEAD
    xrb = xr.astype(BF16)
    zero = jnp.zeros((BLK, BLK), BF16)
    bds, yd = [], []
    for i in range(HPG // 2):
        t = xrb[:, BLK * i:BLK * (i + 1)]
        bd = jnp.concatenate([jnp.where(lo, t, zero), jnp.where(lo, zero, t)], axis=0)
        bds.append(bd)
        yd.append(jnp.dot(mb[:, 2 * BLK * i:2 * BLK * (i + 1)], bd, preferred_element_type=F32))
    cs_st = _mm(cm, st_prev)
    y = jnp.concatenate(yd, axis=1) + cs_st * p_e + d_e * x
    xrw = xr * w_e
    st_new = cd_e * st_prev + _mm(bm, xrw, 0, 0)
    yz = y * _silu(z)
    rn = lax.rsqrt(jnp.sum(yz * yz, axis=1, keepdims=True) / GRP_W + EPS)
    return dict(out=yz * rn * gn, st_new=st_new, dt_all=dt_all, a_row=a_row, dt_e=dt_e, d_e=d_e, p_e=p_e, w_e=w_e,
                cd_e=cd_e, xr=xr, xrw=xrw, lam=lam, m_all=m_all, mb=mb, bds=bds, cs_st=cs_st, y=y, yz=yz, rn=rn, lo=lo)


def _ssd_backward(f, x, z, bm, cm, dt_raw, st_prev, dtb, gn, g, dout, dst_next, cst_scr):
    li, si = _iota((BLK, BLK), 0), _iota((BLK, BLK), 1)
    yz, rn, y, p_e, w_e, cd_e, xr = f["yz"], f["rn"], f["y"], f["p_e"], f["w_e"], f["cd_e"], f["xr"]
    dgn = jnp.sum(dout * yz * rn, axis=0, keepdims=True)
    t = dout * gn
    dyz = rn * t - yz * (rn * rn * rn) * (jnp.sum(yz * t, axis=1, keepdims=True) / GRP_W)
    dy = dyz * _silu(z)
    dz = dyz * y * _dsilu(z)
    dx = f["d_e"] * dy
    dd_e = jnp.sum(dy * x, axis=0, keepdims=True)
    dcsst = dy * p_e
    dp_e = dy * f["cs_st"]
    dcm = _mm(dcsst, st_prev, 1, 1)
    dst_prev = _mm(cm, dcsst, 0, 0) + cd_e * dst_next
    dcd_e = jnp.sum(dst_next * st_prev, axis=0, keepdims=True)
    dbm = _mm(f["xrw"], dst_next, 1, 1)
    dxrw = _mm(bm, dst_next)
    dxr = dxrw * w_e
    dw_e = dxrw * xr
    dyb = dy.astype(BF16)
    dms, dxr_d = [], []
    for i in range(HPG // 2):
        dyp = dyb[:, BLK * i:BLK * (i + 1)]
        dms.append(lax.dot_general(dyp, f["bds"][i], (((1,), (1,)), ((), ())), preferred_element_type=F32))
        r = lax.dot_general(f["mb"][:, 2 * BLK * i:2 * BLK * (i + 1)], dyp, (((0,), (0,)), ((), ())),
                            preferred_element_type=F32)
        dxr_d.append(jnp.where(f["lo"], r[0:BLK], r[BLK:2 * BLK]))
    dm_all = jnp.concatenate(dms, axis=1)
    dxr = dxr + jnp.concatenate(dxr_d, axis=1)
    dlg = dm_all * f["lam"]
    dg = dlg[:, 0:BLK]
    for j in range(1, HPG):
        dg = dg + dlg[:, BLK * j:BLK * (j + 1)]
    dcm = dcm + _mm(dg, bm)
    dbm = dbm + _mm(dg, cm, 0, 0)
    q_all = dm_all * f["m_all"]
    col_sums = jnp.sum(q_all, axis=0, keepdims=True)
    cst_scr[...] = jnp.zeros_like(cst_scr)
    cst_scr[pl.ds(pl.multiple_of(g * HPG, HPG), HPG), :] = _rows8(
        *[col_sums[:, BLK * j:BLK * (j + 1)] for j in range(HPG)])
    dcs = -cst_scr[...].T
    for j in range(HPG):
        dcs = dcs + jnp.where(si == g * HPG + j,
                              jnp.sum(q_all[:, BLK * j:BLK * (j + 1)], axis=1, keepdims=True), 0.0)
    unspread = (_iota((GRP_W, BLK), 1) == g * HPG + jnp.right_shift(_iota((GRP_W, BLK), 0), 6)).astype(BF16)
    dww = dw_e * w_e
    per_head = _sel_r(_split3(jnp.concatenate([dp_e * p_e - dww, dxr * x], axis=0)), unspread)
    last = _sel_r(_split3(_rows8(jnp.sum(dww, axis=0, keepdims=True) + dcd_e * cd_e, dd_e)), unspread)
    dcs = dcs + per_head[0:BLK] + jnp.where(li == BLK - 1, last[0:1], 0.0)
    da = _sel_l((si >= li).astype(BF16), _split3(dcs))
    ddt_all = da * f["a_row"] + per_head[BLK:2 * BLK]
    dalog = jnp.sum(da * f["dt_all"], axis=0, keepdims=True) * f["a_row"]
    dx = dx + dxr * f["dt_e"]
    ddt_raw = ddt_all * jax.nn.sigmoid(dt_raw + dtb)
    ddtb = jnp.sum(ddt_raw, axis=0, keepdims=True)
    ddskip = last[1:2]
    return dict(dx=dx, dz=dz, dbm=dbm, dcm=dcm, ddt_raw=ddt_raw, dst_prev=dst_prev, ddtb=ddtb, dalog=dalog,
                ddskip=ddskip, dgn=dgn)


GPS = 4
NPG = SSM_GROUPS // GPS


def _ssd_in_specs(rev):
    cidx = (lambda c: NB - 1 - c) if rev else (lambda c: c)
    wx, wb = GPS * GRP_W, GPS * SSM_STATE
    return [
        pl.BlockSpec((BLK, wx), lambda p, c: (cidx(c), p)),
        pl.BlockSpec((BLK, wb), lambda p, c: (cidx(c), SSM_INNER // wb + p)),
        pl.BlockSpec((BLK, wb), lambda p, c: (cidx(c), (SSM_INNER + SSM_GROUPS * SSM_STATE) // wb + p)),
        pl.BlockSpec((BLK, 128), lambda p, c: (cidx(c), C_DT // 128)),
        pl.BlockSpec((BLK, wx), lambda p, c: (cidx(c), C_ZS // wx + p)),
        pl.BlockSpec((1, 128), lambda p, c: (0, 0)),
        pl.BlockSpec((1, 128), lambda p, c: (0, 0)),
        pl.BlockSpec((1, 128), lambda p, c: (0, 0)),
        pl.BlockSpec((1, wx), lambda p, c: (0, p)),
    ]


def _grp(ref, i, w):
    return ref[:, i * w:(i + 1) * w]


def _ssd_fwd(xbc_act, proj, dt_bias, a_log, d_skip, g_norm, gather=()):
    ng = len(gather)

    def body(*refs):
        xs_ref, b_ref, c_ref, dt_ref, z_ref, dtb_ref, al_ref, dsk_ref, gn_ref = refs[:9]
        y_ref, st_ref = refs[9 + ng:11 + ng]
        s_scr, cst_scr = refs[11 + 2 * ng:13 + 2 * ng]
        p = pl.program_id(0)
        c = pl.program_id(1)
        if ng:
            ag_start, ag_forward, ag_finish = _ag_program(refs[9:9 + ng], refs[11 + ng:11 + 2 * ng],
                                                          refs[13 + 2 * ng:])
            pl.when((p == 0) & (c == 0))(ag_start)
            pl.when((p == NPG - 1) & (c == (3 * NB) // 4))(ag_forward)

        @pl.when(c == 0)
        def _():
            s_scr[...] = jnp.zeros_like(s_scr)

        for i in range(GPS):
            st_prev = s_scr[i]
            st_ref[i, 0] = st_prev
            f = _ssd_forward(_grp(xs_ref, i, GRP_W), _grp(z_ref, i, GRP_W), _grp(b_ref, i, SSM_STATE),
                             _grp(c_ref, i, SSM_STATE), dt_ref[...], st_prev, dtb_ref[...], al_ref[...],
                             dsk_ref[...], _grp(gn_ref, i, GRP_W), p * GPS + i, cst_scr.at[i])
            y_ref[:, i * GRP_W:(i + 1) * GRP_W] = f["out"].astype(BF16)
            s_scr[i] = f["st_new"]
        if ng:
            pl.when((p == NPG - 1) & (c == NB - 1))(ag_finish)

    return pl.pallas_call(
        body, grid=(NPG, NB), in_specs=_ssd_in_specs(False) + [ANY] * ng,
        out_specs=[pl.BlockSpec((BLK, GPS * GRP_W), lambda p, c: (c, p)),
                   pl.BlockSpec((GPS, 1, SSM_STATE, GRP_W), lambda p, c: (p, c, 0, 0))] + [ANY] * ng,
        out_shape=[SDS((T, SSM_INNER), BF16), SDS((SSM_GROUPS, NB, SSM_STATE, GRP_W), F32)]
        + [SDS((N_DEV,) + s.shape, s.dtype) for s in gather],
        scratch_shapes=[pltpu.VMEM((GPS, SSM_STATE, GRP_W), F32), pltpu.VMEM((GPS, BLK, BLK), F32)]
        + (_ag_scratch(gather) if ng else []),
        compiler_params=_cparams(),
        name="ssd_fwd")(xbc_act, xbc_act, xbc_act, proj, proj, dt_bias, a_log, d_skip, g_norm, *gather)


def _ssd_bwd(xbc_act, proj, dt_bias, a_log, d_skip, g_norm, states, dy, dproj, exchange=()):
    chips = exchange
    nc = len(chips)
    assert GPS == SSM_GROUPS

    def body(*refs):
        xs_ref, b_ref, c_ref, dt_ref, z_ref, dtb_ref, al_ref, dsk_ref, gn_ref, st_ref, dy_ref = refs[:11]
        (dxbc_ref, ddt_ref, dz_ref, ddtb_ref, dal_ref, ddsk_ref, dgn_ref) = refs[12 + nc:19 + nc]
        ds_scr, cst_scr = refs[19 + 2 * nc:21 + 2 * nc]
        p = pl.program_id(0)
        c = pl.program_id(1)
        if nc:
            ch_start, ch_finish = _direct_program(refs[12:12 + nc], refs[19 + nc:19 + 2 * nc], refs[21 + 2 * nc:])
            pl.when((p == 0) & (c == 0))(ch_start)

        @pl.when(c == 0)
        def _():
            ds_scr[...] = jnp.zeros_like(ds_scr)
            dgn_ref[...] = jnp.zeros_like(dgn_ref)

        @pl.when((c == 0) & (p == 0))
        def _():
            ddtb_ref[...] = jnp.zeros_like(ddtb_ref)
            dal_ref[...] = jnp.zeros_like(dal_ref)
            ddsk_ref[...] = jnp.zeros_like(ddsk_ref)

        dt_raw = dt_ref[...]
        for i in range(GPS):
            g = p * GPS + i
            x, z, gn = _grp(xs_ref, i, GRP_W), _grp(z_ref, i, GRP_W), _grp(gn_ref, i, GRP_W)
            bm, cm, st_prev = _grp(b_ref, i, SSM_STATE), _grp(c_ref, i, SSM_STATE), st_ref[i, 0]
            f = _ssd_forward(x, z, bm, cm, dt_raw, st_prev, dtb_ref[...], al_ref[...], dsk_ref[...], gn, g,
                             cst_scr.at[i])
            d = _ssd_backward(f, x, z, bm, cm, dt_raw, st_prev, dtb_ref[...], gn, g,
                              _grp(dy_ref, i, GRP_W).astype(F32), ds_scr[i], cst_scr.at[i])
            b0, c0 = SSM_INNER + i * SSM_STATE, SSM_INNER + (SSM_GROUPS + i) * SSM_STATE
            dxbc_ref[:, i * GRP_W:(i + 1) * GRP_W] = d["dx"]
            dxbc_ref[:, b0:b0 + SSM_STATE] = d["dbm"]
            dxbc_ref[:, c0:c0 + SSM_STATE] = d["dcm"]
            dz_ref[:, i * GRP_W:(i + 1) * GRP_W] = d["dz"].astype(BF16)
            ds_scr[i] = d["dst_prev"]
            ddt_ref[:, i * 128:(i + 1) * 128] = d["ddt_raw"]
            dgn_ref[0:1, i * GRP_W:(i + 1) * GRP_W] += d["dgn"]
            ddtb_ref[0:1, :] += d["ddtb"]
            dal_ref[0:1, :] += d["dalog"]
            ddsk_ref[0:1, :] += d["ddskip"]
        if nc:
            pl.when((p == NPG - 1) & (c == NB - 1))(ch_finish)

    rc = lambda c: NB - 1 - c
    small = pl.BlockSpec((8, 128), lambda p, c: (0, 0))
    wx, wb = GPS * GRP_W, GPS * SSM_STATE
    return pl.pallas_call(
        body, grid=(NPG, NB),
        in_specs=_ssd_in_specs(True) + [
            pl.BlockSpec((GPS, 1, SSM_STATE, GRP_W), lambda p, c: (p, rc(c), 0, 0)),
            pl.BlockSpec((BLK, wx), lambda p, c: (rc(c), p)), ANY] + [ANY] * nc,
        out_specs=[pl.BlockSpec((BLK, CONV_DIM), lambda p, c: (rc(c), 0)),
                   pl.BlockSpec((BLK, GPS * 128), lambda p, c: (rc(c), p)),
                   pl.BlockSpec((BLK, wx), lambda p, c: (rc(c), C_ZS // wx + p)),
                   small, small, small,
                   pl.BlockSpec((8, wx), lambda p, c: (0, p))] + [ANY] * nc,
        out_shape=[SDS((T, CONV_DIM), F32), SDS((T, GRP_W), F32), SDS((T, PW), BF16), SDS((8, 128), F32),
                   SDS((8, 128), F32), SDS((8, 128), F32), SDS((8, SSM_INNER), F32)]
        + [SDS(p.shape, p.dtype) for p in chips],
        scratch_shapes=[pltpu.VMEM((GPS, SSM_STATE, GRP_W), F32), pltpu.VMEM((GPS, BLK, BLK), F32)]
        + (_direct_scratch(chips) if nc else []),
        input_output_aliases={11: 2},
        compiler_params=_cparams(),
        name="ssd_bwd")(xbc_act, xbc_act, xbc_act, proj, proj, dt_bias, a_log, d_skip, g_norm, states, dy, dproj,
                        *chips)


POST_R = 272


def _post_a(o, proj, sn, w_att, w_ssm, w_o):
    def body(o_ref, za_ref, ga_ref, gs_ref, sn_ref, wa_ref, ws_ref, wo_ref, a_ref, mg_ref, ya_ref, ys_ref, out_ref):
        a = (o_ref[...] * _silu(za_ref[...])).astype(BF16)
        a_ref[...] = a
        ya = jnp.dot(a, wa_ref[...], preferred_element_type=F32)
        ys = jnp.dot(sn_ref[...], ws_ref[...], preferred_element_type=F32)
        ya_ref[...] = ya.astype(BF16)
        ys_ref[...] = ys.astype(BF16)
        mg = (jax.nn.sigmoid(ga_ref[...]) * ya + jax.nn.sigmoid(gs_ref[...]) * ys).astype(BF16)
        mg_ref[...] = mg
        out_ref[...] = jnp.dot(mg, wo_ref[...], preferred_element_type=F32)

    row = pl.BlockSpec((POST_R, D_MODEL), lambda i: (i, 0))
    pcol = lambda c0: pl.BlockSpec((POST_R, D_MODEL), lambda i: (i, c0 // D_MODEL))
    full = lambda r: pl.BlockSpec((r, D_MODEL), lambda i: (0, 0))
    return pl.pallas_call(
        body, grid=(T // POST_R,),
        in_specs=[row, pcol(C_ZA), pcol(C_GA), pcol(C_GS), pl.BlockSpec((POST_R, SSM_INNER), lambda i: (i, 0)),
                  full(D_MODEL), full(SSM_INNER), full(D_MODEL)],
        out_specs=[row, row, row, row, row],
        out_shape=[SDS((T, D_MODEL), BF16), SDS((T, D_MODEL), BF16), SDS((T, D_MODEL), BF16), SDS((T, D_MODEL), BF16),
                   SDS((T, D_MODEL), F32)],
        compiler_params=_cparams(), name="post_a")(o, proj, proj, proj, sn, w_att, w_ssm, w_o)


def _post_b(out, h, tgt, proj, ya, ys, o, g_post, w_att, w_ssm, w_o):
    def body(out_ref, h_ref, t_ref, za_ref, ga_ref, gs_ref, ya_ref, ys_ref, o_ref, gp_ref, wa_ref, ws_ref, wo_ref,
             loss_ref, dres_ref, dout_ref, dya_ref, dys_ref, do_ref, dp_ref, dsn_ref, dgp_ref):
        i = pl.program_id(0)
        x = out_ref[...]
        gp = gp_ref[...]
        r = lax.rsqrt(jnp.mean(x * x, axis=-1, keepdims=True) + EPS)
        row = i * POST_R + lax.broadcasted_iota(jnp.int32, (POST_R, 1), 0)
        res = h_ref[...] + jnp.where(row >= PAD, x * r * gp, 0.0)
        live = row >= PAD + N_META
        err = jnp.where(live, res - t_ref[...], 0.0)
        lpart = 0.5 * jnp.sum(jnp.sum(err * err, axis=1, keepdims=True) / D_MODEL, axis=0, keepdims=True)
        dres = err / D_MODEL
        dres_ref[...] = dres
        gpart = jnp.sum(dres * x * r, axis=0, keepdims=True)

        @pl.when(i == 0)
        def _():
            loss_ref[...] = jnp.zeros_like(loss_ref)
            dgp_ref[...] = jnp.zeros_like(dgp_ref)

        loss_ref[...] += jnp.broadcast_to(lpart, loss_ref.shape)
        dgp_ref[0:1, :] += gpart
        gd = gp * dres
        dout = (r * gd - x * (r * r * r) * jnp.mean(x * gd, axis=-1, keepdims=True)).astype(BF16)
        dout_ref[...] = dout
        dmg = lax.dot_general(dout, wo_ref[...], (((1,), (1,)), ((), ())), preferred_element_type=F32)
        sga = jax.nn.sigmoid(ga_ref[...])
        sgs = jax.nn.sigmoid(gs_ref[...])
        dya = (dmg * sga).astype(BF16)
        dys = (dmg * sgs).astype(BF16)
        dya_ref[...] = dya
        dys_ref[...] = dys
        dp_ref[:, C_GA:C_GA + D_MODEL] = (dmg * ya_ref[...].astype(F32) * sga * (1.0 - sga)).astype(BF16)
        dp_ref[:, C_GS:C_GS + D_MODEL] = (dmg * ys_ref[...].astype(F32) * sgs * (1.0 - sgs)).astype(BF16)
        da = lax.dot_general(dya, wa_ref[...], (((1,), (1,)), ((), ())), preferred_element_type=F32)
        za = za_ref[...]
        do_ref[...] = (da * _silu(za)).astype(BF16)
        dp_ref[:, C_ZA:C_ZA + D_MODEL] = (da * o_ref[...] * _dsilu(za)).astype(BF16)
        dsn_ref[...] = lax.dot_general(dys, ws_ref[...], (((1,), (1,)), ((), ())),
                                       preferred_element_type=F32).astype(BF16)

    row = pl.BlockSpec((POST_R, D_MODEL), lambda i: (i, 0))
    pcol = lambda c0: pl.BlockSpec((POST_R, D_MODEL), lambda i: (i, c0 // D_MODEL))
    full = lambda r: pl.BlockSpec((r, D_MODEL), lambda i: (0, 0))
    small = pl.BlockSpec((8, D_MODEL), lambda i: (0, 0))
    return pl.pallas_call(
        body, grid=(T // POST_R,),
        in_specs=[row, row, row, pcol(C_ZA), pcol(C_GA), pcol(C_GS), row, row, row,
                  pl.BlockSpec((1, D_MODEL), lambda i: (0, 0)), full(D_MODEL), full(SSM_INNER), full(D_MODEL)],
        out_specs=[pl.BlockSpec((8, 128), lambda i: (0, 0)), row, row, row, row, row,
                   pl.BlockSpec((POST_R, C_Q), lambda i: (i, 0)),
                   pl.BlockSpec((POST_R, SSM_INNER), lambda i: (i, 0)), small],
        out_shape=[SDS((8, 128), F32), SDS((T, D_MODEL), F32), SDS((T, D_MODEL), BF16), SDS((T, D_MODEL), BF16),
                   SDS((T, D_MODEL), BF16), SDS((T, D_MODEL), BF16), SDS((T, PW), BF16),
                   SDS((T, SSM_INNER), BF16), SDS((8, D_MODEL), F32)],
        compiler_params=_cparams(), name="post_b")(out, h, tgt, proj, proj, proj, ya, ys, o, g_post, w_att, w_ssm, w_o)


TAIL_W = PW - C_K


def _dproj_tail(dproj, dk, dv, ddt4):
    rows = T // 4

    def body(_, dk_ref, dv_ref, ddt_ref, o_ref, buf, sem):
        n = pl.program_id(0)
        d4 = ddt_ref[...]
        buf[:, 0:KV_W] = dk_ref[...].astype(BF16)
        buf[:, KV_W:2 * KV_W] = dv_ref[...].astype(BF16)
        buf[:, 2 * KV_W:TAIL_W] = (d4[:, 0:128] + d4[:, 128:256] + d4[:, 256:384] + d4[:, 384:512]).astype(BF16)
        cp = pltpu.make_async_copy(buf, o_ref.at[pl.ds(pl.multiple_of(n * rows, 16), rows), pl.ds(C_K, TAIL_W)], sem)
        cp.start()
        cp.wait()

    spec = lambda w: pl.BlockSpec((rows, w), lambda i: (i, 0))
    return pl.pallas_call(
        body, grid=(T // rows,), in_specs=[ANY, spec(KV_W), spec(KV_W), spec(GRP_W)], out_specs=ANY,
        out_shape=SDS((T, PW), BF16), input_output_aliases={0: 0},
        scratch_shapes=[pltpu.VMEM((rows, TAIL_W), BF16), pltpu.SemaphoreType.DMA],
        name="dproj_tail")(dproj, dk, dv, ddt4)


def _adamw_math(w, g, m, v):
    m = ADAM_B1 * m + (1.0 - ADAM_B1) * g
    v = ADAM_B2 * v + (1.0 - ADAM_B2) * (g * g)
    m_hat = m / (1.0 - ADAM_B1 ** ADAM_STEP)
    v_hat = v / (1.0 - ADAM_B2 ** ADAM_STEP)
    delta = -ADAM_LR * (m_hat / (jnp.sqrt(v_hat) + ADAM_EPS) + ADAM_WD * w)
    return delta, m, v


def _sum_adamw(recv, w, m, v, tc, name):
    rows, cols = w.shape
    nslab = recv.shape[0]
    assert cols % tc == 0

    def body(r_ref, w_ref, m_ref, v_ref, g_ref, d_ref, nm_ref, nv_ref):
        g = r_ref[0].astype(F32)
        for d in range(1, nslab):
            g = g + r_ref[d].astype(F32)
        g_ref[...] = g
        delta, nm, nv = _adamw_math(w_ref[...], g, m_ref[...], v_ref[...])
        d_ref[...] = delta
        nm_ref[...] = nm
        nv_ref[...] = nv

    blk = pl.BlockSpec((rows, tc), lambda i: (0, i))
    return pl.pallas_call(
        body, grid=(cols // tc,),
        in_specs=[pl.BlockSpec((nslab, rows, tc), lambda i: (0, 0, i)), blk, blk, blk],
        out_specs=[blk, blk, blk, blk], out_shape=[SDS((rows, cols), F32)] * 4,
        compiler_params=_cparams(), name=name)(recv, w, m, v)


def _sum_adamw_rows3(recv, w3, m3, v3, name, exchange=()):
    pairs = 61
    assert (SHARD_IN // 2) % pairs == 0
    nsteps = SHARD_IN // 2 // pairs
    ne = len(exchange)

    def body(*refs):
        r_ref, w_ref, m_ref, v_ref = refs[:4]
        g_ref, d_ref, nm_ref, nv_ref = refs[4 + ne:8 + ne]
        if ne:
            ex_start, ex_finish = _direct_program(refs[4:4 + ne], refs[8 + ne:8 + 2 * ne], refs[8 + 2 * ne:])
            pl.when(pl.program_id(0) == 0)(ex_start)
        g = r_ref[0].astype(F32)
        for d in range(1, N_CHIP):
            g = g + r_ref[d].astype(F32)
        g = g.reshape(2 * pairs, ROW_TILES, 128)
        g_ref[...] = g
        delta, nm, nv = _adamw_math(w_ref[...], g, m_ref[...], v_ref[...])
        d_ref[...] = delta
        nm_ref[...] = nm
        nv_ref[...] = nv
        if ne:
            pl.when(pl.program_id(0) == nsteps - 1)(ex_finish)

    blk = pl.BlockSpec((2 * pairs, ROW_TILES, 128), lambda i: (i, 0, 0))
    return pl.pallas_call(
        body, grid=(nsteps,),
        in_specs=[pl.BlockSpec((N_CHIP, pairs, 2 * ROW_TILES, 128), lambda i: (0, i, 0, 0)), blk, blk, blk]
        + [ANY] * ne,
        out_specs=[blk, blk, blk, blk] + [ANY] * ne,
        out_shape=[SDS(w3.shape, F32)] * 4 + [SDS(p.shape, p.dtype) for p in exchange],
        scratch_shapes=_direct_scratch(exchange) if ne else [],
        compiler_params=_cparams(), name=name)(recv, w3, m3, v3, *exchange)


ROW_GPRE, ROW_CONVB, ROW_DTB, ROW_ALOG, ROW_DSKIP, ROW_SINK, ROW_GSSM, ROW_GPOST = 0, 1, 4, 5, 6, 7, 8, 10
ROW_LOSS = 11
REP_ROWS, ROW_CONVW, ROW_META, SM_ROWS = 16, 16, 24, 40
CW_SHARD = CONV_DIM // N_DEV
META_SHARD = D_MODEL // N_DEV


def _small_pack(dgpre, db, ddtb, dal, ddsk, dsink, dgn, dgp, dw, loss, dh):
    def body(dgpre_ref, db_ref, ddtb_ref, dal_ref, ddsk_ref, dsink_ref, dgn_ref, dgp_ref, dw_ref, loss_ref, dh_ref,
             o_ref, rep):
        rep[...] = jnp.zeros_like(rep)
        rep[ROW_LOSS:ROW_LOSS + 1, 0:128] = loss_ref[0:1, :]
        rep[ROW_GPRE:ROW_GPRE + 1, :] = dgpre_ref[0:1, :]
        for k in range(3):
            rep[ROW_CONVB + k:ROW_CONVB + k + 1, :] = db_ref[0:1, 1024 * k:1024 * (k + 1)]
        rep[ROW_DTB:ROW_DTB + 1, 0:128] = ddtb_ref[0:1, :]
        rep[ROW_ALOG:ROW_ALOG + 1, 0:128] = dal_ref[0:1, :]
        rep[ROW_DSKIP:ROW_DSKIP + 1, 0:128] = ddsk_ref[0:1, :]
        rep[ROW_SINK:ROW_SINK + 1, 0:128] = dsink_ref[0:1, :]
        rep[ROW_GSSM:ROW_GSSM + 1, :] = dgn_ref[0:1, 0:1024]
        rep[ROW_GSSM + 1:ROW_GSSM + 2, :] = dgn_ref[0:1, 1024:2048]
        rep[ROW_GPOST:ROW_GPOST + 1, :] = dgp_ref[0:1, :]
        cw = dw_ref[...]
        mh = dh_ref[...]
        o_ref[...] = jnp.zeros_like(o_ref)
        for p in range(N_DEV):
            o_ref[p, 0:REP_ROWS, :] = rep[...]
            o_ref[p, ROW_CONVW:ROW_CONVW + 8, 0:CW_SHARD] = cw[:, p * CW_SHARD:(p + 1) * CW_SHARD]
            o_ref[p, ROW_META:ROW_META + N_META, 0:META_SHARD] = mh[:, p * META_SHARD:(p + 1) * META_SHARD]

    ins = [dgpre, db, ddtb, dal, ddsk, dsink, dgn, dgp, dw, loss]
    return pl.pallas_call(
        body, grid=(1,),
        in_specs=[pl.BlockSpec(a.shape, lambda i: (0, 0)) for a in ins]
        + [pl.BlockSpec((N_META, D_MODEL), lambda i: (PAD // N_META, 0))],
        out_specs=pl.BlockSpec((N_DEV, SM_ROWS, 1024), lambda i: (0, 0, 0)),
        out_shape=SDS((N_DEV, SM_ROWS, 1024), F32), scratch_shapes=[pltpu.VMEM((REP_ROWS, 1024), F32)],
        name="small_pack")(*ins, dh)


def _small_finish(recv, params):
    npar = len(params)

    def body(*refs):
        r_ref = refs[0]
        wmv = refs[1:1 + 3 * npar]
        outs = refs[1 + 3 * npar:1 + 7 * npar]
        loss_ref = refs[1 + 7 * npar]
        gs = refs[-1]
        g = r_ref[0]
        for d in range(1, recv.shape[0]):
            g = g + r_ref[d]
        gs[...] = g
        loss_ref[...] = gs[ROW_LOSS:ROW_LOSS + 1, 0:128]
        grads = [
            gs[ROW_GPRE:ROW_GPRE + 1, :],
            jnp.concatenate([gs[ROW_CONVB + k:ROW_CONVB + k + 1, :] for k in range(3)], axis=1),
            gs[ROW_DTB:ROW_DTB + 1, 0:SSM_HEADS], gs[ROW_ALOG:ROW_ALOG + 1, 0:SSM_HEADS],
            gs[ROW_DSKIP:ROW_DSKIP + 1, 0:SSM_HEADS], gs[ROW_SINK:ROW_SINK + 1, 0:Q_HEADS],
            jnp.concatenate([gs[ROW_GSSM:ROW_GSSM + 1, :], gs[ROW_GSSM + 1:ROW_GSSM + 2, :]], axis=1),
            gs[ROW_GPOST:ROW_GPOST + 1, :],
            gs[ROW_CONVW:ROW_CONVW + 4, 0:CW_SHARD],
            gs[ROW_META:ROW_META + N_META, 0:META_SHARD]]
        for i in range(npar):
            w_ref, m_ref, v_ref = wmv[3 * i:3 * i + 3]
            delta, nm, nv = _adamw_math(w_ref[...], grads[i], m_ref[...], v_ref[...])
            outs[4 * i][...] = grads[i]
            outs[4 * i + 1][...] = delta
            outs[4 * i + 2][...] = nm
            outs[4 * i + 3][...] = nv

    flat = [a for wmv in params for a in wmv]
    res = pl.pallas_call(
        body, out_shape=[SDS(wmv[0].shape, F32) for wmv in params for _ in range(4)] + [SDS((1, 128), F32)],
        scratch_shapes=[pltpu.VMEM((SM_ROWS, 1024), F32)], name="small_finish")(recv, *flat)
    return [tuple(res[4 * i:4 * i + 4]) for i in range(npar)], res[4 * npar]


def _slab(ref, px, py, pc):
    return ref.at[4 * px + 2 * py + pc]


def _bounce(src, dst, buf, sem):
    cp = pltpu.make_async_copy(src, buf, sem)
    cp.start()
    cp.wait()
    cp = pltpu.make_async_copy(buf, dst, sem)
    cp.start()
    cp.wait()


def _ag_program(ins, outs, scratch):
    na = len(ins)
    send_sems, recv_sems, local_sems = scratch[:3]
    bufs = scratch[3:]
    x, y, c = lax.axis_index("x"), lax.axis_index("y"), lax.axis_index("c")
    me, sibling = (x, y, c), (x, y, 1 - c)
    chips = [(1 - x, y), (x, 1 - y), (1 - x, 1 - y)]

    def copy(a, k, block, to, src=None):
        dst = _slab(outs[a], *block)
        return pltpu.make_async_remote_copy(
            src_ref=dst if src is None else src, dst_ref=dst, send_sem=send_sems.at[a, k],
            recv_sem=recv_sems.at[a, k], device_id=to, device_id_type=MESH)

    def own_sends():
        out = []
        for a in range(na):
            out.append(copy(a, 0, me, sibling, src=ins[a]))
            out += [copy(a, 1 + j, me, (*chip, c), src=ins[a]) for j, chip in enumerate(chips)]
        return out

    def start():
        for cp in own_sends():
            cp.start()
        for a in range(na):
            _bounce(ins[a], _slab(outs[a], *me), bufs[a], local_sems.at[a])

    def forward():
        for j, chip in enumerate(chips):
            for a in range(na):
                copy(a, 1 + j, (*chip, c), me).wait_recv()
                copy(a, 4 + j, (*chip, c), sibling).start()

    def finish():
        for a in range(na):
            copy(a, 0, sibling, me).wait_recv()
            for j, chip in enumerate(chips):
                copy(a, 4 + j, (*chip, 1 - c), me).wait_recv()
        for cp in own_sends():
            cp.wait_send()
        for j, chip in enumerate(chips):
            for a in range(na):
                copy(a, 4 + j, (*chip, c), sibling).wait_send()

    return start, forward, finish


def _ag_scratch(shards):
    na = len(shards)
    return [pltpu.SemaphoreType.DMA((na, 7)), pltpu.SemaphoreType.DMA((na, 7)),
            pltpu.SemaphoreType.DMA((na,))] + [pltpu.VMEM(s.shape, s.dtype) for s in shards]


def _all_gather(shards):
    na = len(shards)

    def body(*refs):
        start, forward, finish = _ag_program(refs[:na], refs[na:2 * na], refs[2 * na:])
        start()
        forward()
        finish()

    return pl.pallas_call(
        body, in_specs=[ANY] * na, out_specs=[ANY] * na,
        out_shape=[SDS((N_DEV,) + s.shape, s.dtype) for s in shards],
        scratch_shapes=_ag_scratch(shards), name="all_gather")(*shards)


N_CHIP = 4


def _pair_sum(own, got, name):
    na = len(own)

    def body(*refs):
        for a in range(na):
            o_ref, g_ref, s_ref = refs[a], refs[na + a], refs[2 * na + a]
            s_ref[...] = (o_ref[...].astype(F32) + g_ref[...].astype(F32)).astype(s_ref.dtype)

    def spec(p):
        nd = len(p.shape) - 1
        return pl.BlockSpec((1,) + p.shape[1:], lambda k, nd=nd: (k,) + (0,) * nd)

    return pl.pallas_call(
        body, grid=(N_CHIP,), in_specs=[spec(p) for p in own] + [spec(p) for p in got],
        out_specs=[spec(p) for p in own], out_shape=[SDS(p.shape, p.dtype) for p in own],
        compiler_params=_cparams(), name=name)(*own, *got)


def _chips_program(ins, outs, scratch):
    na = len(ins)
    send_sems, recv_sems, local_sems = scratch[:3]
    bufs = scratch[3:]
    x, y, c = lax.axis_index("x"), lax.axis_index("y"), lax.axis_index("c")
    mine = 2 * x + y
    chips = [(1 - x, y), (x, 1 - y), (1 - x, 1 - y)]

    def send(a, j):
        px, py = chips[j]
        return pltpu.make_async_remote_copy(
            src_ref=ins[a].at[2 * px + py], dst_ref=outs[a].at[mine], send_sem=send_sems.at[a, j],
            recv_sem=recv_sems.at[a, j], device_id=(px, py, c), device_id_type=MESH)

    def arrival(a, j):
        px, py = chips[j]
        return pltpu.make_async_remote_copy(
            src_ref=ins[a].at[2 * px + py], dst_ref=outs[a].at[2 * px + py], send_sem=send_sems.at[a, j],
            recv_sem=recv_sems.at[a, j], device_id=(px, py, c), device_id_type=MESH)

    def start():
        for a in range(na):
            for j in range(3):
                send(a, j).start()
        for a in range(na):
            _bounce(ins[a].at[mine], outs[a].at[mine], bufs[a], local_sems.at[a])

    def finish():
        for a in range(na):
            for j in range(3):
                arrival(a, j).wait_recv()
        for a in range(na):
            for j in range(3):
                send(a, j).wait_send()

    return start, finish


def _chips_scratch(parts):
    na = len(parts)
    return [pltpu.SemaphoreType.DMA((na, 3)), pltpu.SemaphoreType.DMA((na, 3)),
            pltpu.SemaphoreType.DMA((na,))] + [pltpu.VMEM(p.shape[1:], p.dtype) for p in parts]


def _direct_program(ins, outs, scratch):
    na = len(ins)
    send_sems, recv_sems, local_sems = scratch[:3]
    bufs = scratch[3:]
    x, y, c = lax.axis_index("x"), lax.axis_index("y"), lax.axis_index("c")
    me = (x, y, c)
    peers = []
    for k in range(1, N_DEV):
        dx, dy, dc = (k >> 2) & 1, (k >> 1) & 1, k & 1
        peers.append(((1 - x) if dx else x, (1 - y) if dy else y, (1 - c) if dc else c))

    def send(a, k):
        return pltpu.make_async_remote_copy(
            src_ref=_slab(ins[a], *peers[k]), dst_ref=_slab(outs[a], *me), send_sem=send_sems.at[a, k],
            recv_sem=recv_sems.at[a, k], device_id=peers[k], device_id_type=MESH)

    def arrival(a, k):
        return pltpu.make_async_remote_copy(
            src_ref=_slab(ins[a], *peers[k]), dst_ref=_slab(outs[a], *peers[k]), send_sem=send_sems.at[a, k],
            recv_sem=recv_sems.at[a, k], device_id=peers[k], device_id_type=MESH)

    def start():
        for a in range(na):
            for k in range(N_DEV - 1):
                send(a, k).start()
        for a in range(na):
            _bounce(_slab(ins[a], *me), _slab(outs[a], *me), bufs[a], local_sems.at[a])

    def finish():
        for a in range(na):
            for k in range(N_DEV - 1):
                arrival(a, k).wait_recv()
        for a in range(na):
            for k in range(N_DEV - 1):
                send(a, k).wait_send()

    return start, finish


def _direct_scratch(parts):
    na = len(parts)
    return [pltpu.SemaphoreType.DMA((na, N_DEV - 1)), pltpu.SemaphoreType.DMA((na, N_DEV - 1)),
            pltpu.SemaphoreType.DMA((na,))] + [pltpu.VMEM(p.shape[1:], p.dtype) for p in parts]


ROW_TILES = D_MODEL // 128


def _rows3(t):
    return jnp.transpose(t[0]).reshape(t.shape[2], ROW_TILES, 128)


def _unrows3(t):
    return jnp.transpose(t.reshape(t.shape[0], D_MODEL))[None]


def _cast_shards(w_in3, w_att, w_ssm, w_o):
    def body(wi_ref, wa_ref, ws_ref, wo_ref, a_ref, b_ref, c_ref, d_ref):
        a_ref[...] = wi_ref[...].reshape(SHARD_IN // 2, 2 * ROW_TILES, 128).astype(BF16)
        b_ref[...] = wa_ref[...].astype(BF16)
        c_ref[...] = ws_ref[...].astype(BF16)
        d_ref[...] = wo_ref[...].astype(BF16)

    return pl.pallas_call(
        body, out_shape=[SDS((SHARD_IN // 2, 2 * ROW_TILES, 128), BF16), SDS(w_att.shape, BF16),
                         SDS(w_ssm.shape, BF16), SDS(w_o.shape, BF16)],
        compiler_params=_cparams(), name="cast_shards")(w_in3, w_att, w_ssm, w_o)


def _pieces():
    out = []
    for r0, c0, w in _SEGS:
        r = r0
        while r < r0 + w:
            d = r // SHARD_IN
            n = min(r0 + w, (d + 1) * SHARD_IN) - r
            out.append((c0 + (r - r0), d, r - d * SHARD_IN, n))
            r += n
    return out


def _to_aligned_t(slabs):
    def body(a_ref, o_ref):
        for (t, d, s, n) in _pieces():
            o_ref[t:t + n, :] = a_ref[d, s // 2:(s + n) // 2].reshape(n, D_MODEL)
        o_ref[C_DT + 32:C_DT + 128, :] = jnp.zeros((96, D_MODEL), slabs.dtype)

    return pl.pallas_call(body, out_shape=SDS((PW, D_MODEL), slabs.dtype), compiler_params=_cparams(),
                          name="to_aligned")(slabs)


def _from_aligned_pair(g):
    slab = (SHARD_IN // 2, 2 * ROW_TILES, 128)
    by_slab = [[p for p in _pieces() if p[1] == d] for d in range(N_DEV)]

    def body(g_ref, own_ref, got_ref, slabs, send_sems, recv_sems, local_sems):
        x, y, c = lax.axis_index("x"), lax.axis_index("y"), lax.axis_index("c")
        sibling = (x, y, 1 - c)

        def to_own(d, k):
            return pltpu.make_async_copy(slabs.at[d], own_ref.at[k], local_sems.at[k])

        def to_sibling(d, k):
            return pltpu.make_async_remote_copy(
                src_ref=slabs.at[d], dst_ref=got_ref.at[k], send_sem=send_sems.at[k], recv_sem=recv_sems.at[k],
                device_id=sibling, device_id_type=MESH)

        for d in range(N_DEV):
            for (t, _, s, n) in by_slab[d]:
                slabs[d, s // 2:(s + n) // 2] = g_ref[t:t + n, :].reshape(n // 2, 2 * ROW_TILES, 128)
            k, side = d // 2, d % 2
            pl.when(c == side)(to_own(d, k).start)
            pl.when(c != side)(to_sibling(d, k).start)
        for k in range(N_CHIP):
            to_own(0, k).wait()
            to_sibling(0, k).wait()

    half = SDS((N_CHIP,) + slab, g.dtype)
    return pl.pallas_call(
        body, in_specs=[pl.BlockSpec(memory_space=pltpu.VMEM)], out_specs=[ANY, ANY], out_shape=[half, half],
        scratch_shapes=[pltpu.VMEM((N_DEV,) + slab, g.dtype), pltpu.SemaphoreType.DMA((N_CHIP,)),
                        pltpu.SemaphoreType.DMA((N_CHIP,)), pltpu.SemaphoreType.DMA((N_CHIP,))],
        compiler_params=_cparams(), name="from_aligned_pair")(g)


_SEGS = [
    (R_Q, C_Q, 1024), (R_K, C_K, 256), (R_V, C_V, 256), (R_ZA, C_ZA, 1024), (R_ZS, C_ZS, 2048),
    (R_XBC, C_XBC, 3072), (R_DT, C_DT, 32), (R_GA, C_GA, 1024), (R_GS, C_GS, 1024)]


def _pad_lanes(v, n=128):
    return jnp.pad(v, ((0, 0), (0, n - v.shape[1])))


def _device_step(h, tgt, w_alt, w_out, g_pre, conv_w8, conv_b, dt_bias, a_log, d_skip, sinks, g_ssm, g_post, on_mesh):
    dtb, al, dsk, snk = _pad_lanes(dt_bias), _pad_lanes(a_log), _pad_lanes(d_skip), _pad_lanes(sinks)
    u = _norm_u(h, g_pre)
    proj = _matmul(u, w_alt, "nt", F32, T, 896, "in_proj")
    o = _attn_fwd(proj, snk)
    xbc_act = _conv_fwd(proj, conv_w8, conv_b)
    if on_mesh:
        sn, states, att_all, ssm_all, o_all = _ssd_fwd(xbc_act, proj, dtb, al, dsk, g_ssm, gather=w_out)
        w_att = att_all.reshape(D_MODEL, D_MODEL)
        w_ssm = ssm_all.reshape(SSM_INNER, D_MODEL)
        w_o = o_all.reshape(D_MODEL, D_MODEL)
    else:
        sn, states = _ssd_fwd(xbc_act, proj, dtb, al, dsk, g_ssm)
        w_att, w_ssm, w_o = w_out
    a_in, mg, ya, ys, out = _post_a(o, proj, sn, w_att, w_ssm, w_o)
    (loss, dres, dout, dya, dys, do, dproj, dsn, dgp) = _post_b(
        out, h, tgt, proj, ya, ys, o, g_post, w_att, w_ssm, w_o)
    dw_att = _matmul(a_in, dya, "tn", BF16, D_MODEL, D_MODEL, "d_w_att")
    dw_ssm = _matmul(sn, dys, "tn", BF16, D_MODEL, D_MODEL, "d_w_ssm")
    dw_o = _matmul(mg, dout, "tn", BF16, D_MODEL, D_MODEL, "d_w_o")
    res = {}
    if on_mesh:
        parts = [dw_att.reshape(N_DEV, 128, D_MODEL), dw_ssm.reshape(N_DEV, 256, D_MODEL),
                 dw_o.reshape(N_DEV, 128, D_MODEL)]
        (dxbc, ddt4, dproj, ddtb, dal, ddsk, dgn, res["r_att"], res["r_ssm"], res["r_o"]) = _ssd_bwd(
            xbc_act, proj, dtb, al, dsk, g_ssm, states, dsn, dproj, exchange=parts)
    else:
        dxbc, ddt4, dproj, ddtb, dal, ddsk, dgn = _ssd_bwd(xbc_act, proj, dtb, al, dsk, g_ssm, states, dsn, dproj)
        res.update(dw_att=dw_att, dw_ssm=dw_ssm, dw_o=dw_o)
    dproj, dcw, dcb = _conv_bwd(proj, conv_w8, conv_b, dxbc, 0, dproj, "conv_bwd")
    dproj, dk, dv, dsink = _attn_bwd(proj, snk, do, dproj)
    dproj = _dproj_tail(dproj, dk, dv, ddt4)
    dw_alt = _matmul(dproj, u, "tn", BF16, 896, D_MODEL, "d_w_in")
    if on_mesh:
        own, got = _from_aligned_pair(dw_alt)
        dh, dgpre, res["r_in"] = _d_u_norm(dproj, w_alt, h, g_pre, dres,
                                           chips=_pair_sum([own], [got], "pair_sum_w_in"))
    else:
        dh, dgpre = _d_u_norm(dproj, w_alt, h, g_pre, dres)
        res["dw_alt"] = dw_alt
    small = (dgpre, dcb, ddtb, dal, ddsk, dsink, dgn, dgp, dcw)
    if on_mesh:
        res["small_pack"] = _small_pack(*small, loss, dh)
    else:
        res["small"] = small
    res.update(loss=loss[0, 0], dh=dh)
    return res


def kernel(x, meta_tokens, g_pre, w_in, conv_w, conv_b, dt_bias, a_log, d_skip, attn_sinks, g_ssm_norm, w_out_att, w_out_ssm, w_out, g_post, loss_target, m_meta_tokens, m_g_pre, m_w_in, m_conv_w, m_conv_b, m_dt_bias, m_a_log, m_d_skip, m_attn_sinks, m_g_ssm_norm, m_w_out_att, m_w_out_ssm, m_w_out, m_g_post, v_meta_tokens, v_g_pre, v_w_in, v_conv_w, v_conv_b, v_dt_bias, v_a_log, v_d_skip, v_attn_sinks, v_g_ssm_norm, v_w_out_att, v_w_out_ssm, v_w_out, v_g_post):
    w_in3, m_in3, v_in3 = _rows3(w_in), _rows3(m_w_in), _rows3(v_w_in)
    a_sh, att_sh, ssm_sh, o_sh = _cast_shards(w_in3, w_out_att[0], w_out_ssm[0], w_out[0])
    cw_sh = jnp.pad(conv_w[0], ((0, 4), (0, 0)))
    a_all, meta_all, cw_all = _all_gather([a_sh, meta_tokens, cw_sh])
    w_alt = _to_aligned_t(a_all)
    meta_full = meta_all.transpose(1, 0, 2).reshape(N_META, D_MODEL)
    conv_w8 = cw_all.transpose(1, 0, 2).reshape(8, CONV_DIM)

    h = jnp.concatenate([jnp.zeros((PAD, D_MODEL), F32), meta_full, x[0]], axis=0)
    tgt = jnp.concatenate([jnp.zeros((PAD + N_META, D_MODEL), F32), loss_target[0]], axis=0)
    r = _device_step(h, tgt, w_alt, (att_sh, ssm_sh, o_sh), g_pre, conv_w8, conv_b, dt_bias, a_log, d_skip,
                     attn_sinks, g_ssm_norm, g_post, True)
    grad_x = r["dh"][PAD + N_META:][None]

    *res_in, r_small = _sum_adamw_rows3(r["r_in"], w_in3, m_in3, v_in3, "adamw_w_in", exchange=[r["small_pack"]])
    res_in = [_unrows3(t) for t in res_in]
    res_att = [t[None] for t in _sum_adamw(r["r_att"], w_out_att[0], m_w_out_att[0], v_w_out_att[0], 512,
                                           "adamw_w_att")]
    res_ssm = [t[None] for t in _sum_adamw(r["r_ssm"], w_out_ssm[0], m_w_out_ssm[0], v_w_out_ssm[0], 512,
                                           "adamw_w_ssm")]
    res_o = [t[None] for t in _sum_adamw(r["r_o"], w_out[0], m_w_out[0], v_w_out[0], 512, "adamw_w_o")]
    (res_gpre, res_convb, res_dtb, res_alog, res_dskip, res_sink, res_gssm, res_gpost, res_cw, res_meta), loss = _small_finish(
        r_small, [(g_pre, m_g_pre, v_g_pre), (conv_b, m_conv_b, v_conv_b), (dt_bias, m_dt_bias, v_dt_bias),
                       (a_log, m_a_log, v_a_log), (d_skip, m_d_skip, v_d_skip),
                       (attn_sinks, m_attn_sinks, v_attn_sinks), (g_ssm_norm, m_g_ssm_norm, v_g_ssm_norm),
                       (g_post, m_g_post, v_g_post), (conv_w[0], m_conv_w[0], v_conv_w[0]),
                       (meta_tokens, m_meta_tokens, v_meta_tokens)])
    res_cw = [t[None] for t in res_cw]
    per_weight = [res_meta, res_gpre, res_in, res_cw, res_convb, res_dtb, res_alog, res_dskip, res_sink, res_gssm,
                  res_att, res_ssm, res_o, res_gpost]
    return (loss[0, 0], grad_x, *[p[0] for p in per_weight], *[p[1] for p in per_weight], *[p[2] for p in per_weight],
            *[p[3] for p in per_weight])
```

```python
import functools
import math

import jax
import jax.numpy as jnp
from jax import lax
from jax.experimental import pallas as pl
from jax.experimental.pallas import tpu as pltpu

F32 = jnp.float32
BF16 = jnp.bfloat16
SDS = jax.ShapeDtypeStruct
MESH = pl.DeviceIdType.MESH
ANY = pl.BlockSpec(memory_space=pl.ANY)

N_DEV = 8
D_MODEL = 1024
SEQ = 2048
N_META = 16
BLK = 128
PAD = 112
T = PAD + N_META + SEQ
NB = T // BLK
EPS = 1e-6
HEAD = 64
Q_HEADS = 16
KV_HEADS = 4
GROUP = 4
KV_W = 256
SSM_INNER = 2048
SSM_HEADS = 32
SSM_GROUPS = 4
GRP_W = 512
SSM_STATE = 128
CONV_DIM = 3072
IN_PROJ = 9760
SHARD_IN = IN_PROJ // N_DEV
NEG = -1e30

C_ZS, C_XBC, C_Q, C_ZA, C_GA, C_GS, C_K, C_V, C_DT = 0, 2048, 5120, 6144, 7168, 8192, 9216, 9472, 9728
PW = 9856
GATES_W = 3 * D_MODEL
R_Q, R_K, R_V, R_ZA, R_ZS, R_XBC, R_DT, R_GA, R_GS = 0, 1024, 1280, 1536, 2560, 4608, 7680, 7712, 8736

ADAM_LR, ADAM_B1, ADAM_B2, ADAM_EPS, ADAM_WD, ADAM_STEP = 0.001, 0.9, 0.999, 1e-08, 0.01, 10

VMEM_LIMIT = 56 * 1024 * 1024


def _cparams():
    return pltpu.CompilerParams(vmem_limit_bytes=VMEM_LIMIT)


def _silu(x):
    return x * jax.nn.sigmoid(x)


def _dsilu(x):
    s = jax.nn.sigmoid(x)
    return s * (1.0 + x * (1.0 - s))


def _matmul(a, b, mode, out_dtype, tm, tn, name):
    if mode == "nt":
        (m, k), n = a.shape, b.shape[0]
        a_spec = pl.BlockSpec((tm, k), lambda i, j: (i, 0))
        b_spec = pl.BlockSpec((tn, k), lambda i, j: (j, 0))
        dims = (((1,), (1,)), ((), ()))
    else:
        assert mode == "tn"
        (k, m), n = a.shape, b.shape[1]
        a_spec = pl.BlockSpec((k, tm), lambda i, j: (0, i))
        b_spec = pl.BlockSpec((k, tn), lambda i, j: (0, j))
        dims = (((0,), (0,)), ((), ()))
    assert m % tm == 0 and n % tn == 0, (a.shape, b.shape, tm, tn)

    def body(a_ref, b_ref, o_ref):
        o_ref[...] = lax.dot_general(a_ref[...], b_ref[...], dims, preferred_element_type=F32).astype(out_dtype)

    return pl.pallas_call(
        body, grid=(m // tm, n // tn), in_specs=[a_spec, b_spec],
        out_specs=pl.BlockSpec((tm, tn), lambda i, j: (i, j)), out_shape=SDS((m, n), out_dtype),
        compiler_params=_cparams(), name=name)(a, b)


def _norm_u(h, g_pre):
    def body(h_ref, g_ref, u_ref):
        x = h_ref[...]
        r = lax.rsqrt(jnp.mean(x * x, axis=-1, keepdims=True) + EPS)
        u_ref[...] = (x * r * g_ref[...]).astype(BF16)

    return pl.pallas_call(
        body, grid=(NB,),
        in_specs=[pl.BlockSpec((BLK, D_MODEL), lambda i: (i, 0)), pl.BlockSpec((1, D_MODEL), lambda i: (0, 0))],
        out_specs=pl.BlockSpec((BLK, D_MODEL), lambda i: (i, 0)),
        out_shape=SDS((T, D_MODEL), BF16), name="norm_u")(h, g_pre)


DU_TM, DU_TK = T // 2, 1408


def _d_u_norm(dproj, w_alt, h, g_pre, dres, chips=()):
    nk = PW // DU_TK
    ni = T // DU_TM
    nc = len(chips)

    def body(*refs):
        a_ref, b_ref, h_ref, g_ref, dres_ref = refs[:5]
        dh_ref, dg_ref = refs[5 + nc:7 + nc]
        acc_ref = refs[7 + 2 * nc]
        i, kk = pl.program_id(0), pl.program_id(1)
        if nc:
            ch_start, ch_finish = _chips_program(refs[5:5 + nc], refs[7 + nc:7 + 2 * nc], refs[8 + 2 * nc:])
            pl.when((i == 0) & (kk == 0))(ch_start)
        part = jnp.dot(a_ref[...], b_ref[...], preferred_element_type=F32)

        @pl.when(kk == 0)
        def _():
            acc_ref[...] = part

        @pl.when((kk > 0) & (kk < nk - 1))
        def _():
            acc_ref[...] += part

        @pl.when(kk == nk - 1)
        def _():
            du_ = acc_ref[...] + part
            x = h_ref[...]
            r = lax.rsqrt(jnp.mean(x * x, axis=-1, keepdims=True) + EPS)
            gd = g_ref[...] * du_
            dx = r * gd - x * (r * r * r) * jnp.mean(x * gd, axis=-1, keepdims=True)
            dh_ref[...] = dx + dres_ref[...]
            gpart = jnp.concatenate([jnp.sum(du_ * x * r, axis=0, keepdims=True), jnp.zeros((7, D_MODEL), F32)],
                                    axis=0)

            @pl.when(i == 0)
            def _():
                dg_ref[...] = gpart

            @pl.when(i > 0)
            def _():
                dg_ref[...] += gpart

        if nc:
            pl.when((i == ni - 1) & (kk == nk - 1))(ch_finish)

    row = pl.BlockSpec((DU_TM, D_MODEL), lambda i, kk: (i, 0))
    return pl.pallas_call(
        body, grid=(ni, nk),
        in_specs=[pl.BlockSpec((DU_TM, DU_TK), lambda i, kk: (i, kk)),
                  pl.BlockSpec((DU_TK, D_MODEL), lambda i, kk: (kk, 0)),
                  row, pl.BlockSpec((1, D_MODEL), lambda i, kk: (0, 0)), row] + [ANY] * nc,
        out_specs=[row, pl.BlockSpec((8, D_MODEL), lambda i, kk: (0, 0))] + [ANY] * nc,
        out_shape=[SDS((T, D_MODEL), F32), SDS((8, D_MODEL), F32)] + [SDS(p.shape, p.dtype) for p in chips],
        scratch_shapes=[pltpu.VMEM((DU_TM, D_MODEL), F32)] + (_chips_scratch(chips) if nc else []),
        compiler_params=_cparams(), name="d_u_norm")(dproj, w_alt, h, g_pre, dres, *chips)


def _lane_pick(row, h):
    lane = lax.broadcasted_iota(jnp.int32, row.shape, 1)
    return jnp.sum(jnp.where(lane == h, row, 0.0), axis=1, keepdims=True)


def _attn_fn(q4s, kcats, vcats, kms, vms, sinks, n):
    r = lax.broadcasted_iota(jnp.int32, (GROUP * BLK, 2 * BLK), 0)
    s = lax.broadcasted_iota(jnp.int32, (GROUP * BLK, 2 * BLK), 1)
    i = jnp.bitwise_and(r, BLK - 1)
    gi = jnp.right_shift(r, 7)
    rel = i - s + BLK
    k_pos = n * BLK - BLK + s
    band_ok = (rel >= 0) & (rel < BLK) & (k_pos >= PAD + N_META)
    relf = rel.astype(F32)
    rm = lax.broadcasted_iota(jnp.int32, (GROUP * BLK, N_META), 0)
    mm = lax.broadcasted_iota(jnp.int32, (GROUP * BLK, N_META), 1)
    meta_ok = (PAD + mm) <= (n * BLK + jnp.bitwise_and(rm, BLK - 1))
    gcol = jnp.right_shift(lax.broadcasted_iota(jnp.int32, (GROUP * BLK, 1), 0), 7)
    outs = []
    for kh in range(KV_HEADS):
        slopes = [2.0 ** (-8.0 * (kh * GROUP + g + 1) / Q_HEADS) for g in range(GROUP)]
        slope = jnp.where(gi == 0, slopes[0], jnp.where(gi == 1, slopes[1], jnp.where(gi == 2, slopes[2], slopes[3])))
        sk = [_lane_pick(sinks, kh * GROUP + g) for g in range(GROUP)]
        sink = jnp.where(gcol == 0, sk[0], jnp.where(gcol == 1, sk[1], jnp.where(gcol == 2, sk[2], sk[3])))
        qb = (q4s[kh] * (HEAD ** -0.5)).astype(BF16)
        sb = lax.dot_general(qb, kcats[kh].astype(BF16), (((1,), (1,)), ((), ())), preferred_element_type=F32)
        sb = jnp.where(band_ok, sb - slope * relf, NEG)
        sm = lax.dot_general(qb, kms[kh].astype(BF16), (((1,), (1,)), ((), ())), preferred_element_type=F32)
        sm = jnp.where(meta_ok, sm, NEG)
        mx = jnp.maximum(jnp.maximum(jnp.max(sb, axis=1, keepdims=True), jnp.max(sm, axis=1, keepdims=True)), sink)
        mx = lax.stop_gradient(mx)
        eb = jnp.exp(sb - mx)
        em = jnp.exp(sm - mx)
        es = jnp.exp(sink - mx)
        inv = 1.0 / (jnp.sum(eb, axis=1, keepdims=True) + jnp.sum(em, axis=1, keepdims=True) + es)
        pb = (eb * inv).astype(BF16)
        pm = (em * inv).astype(BF16)
        o4 = (jnp.dot(pm, vms[kh].astype(BF16), preferred_element_type=F32)
              + jnp.dot(pb, vcats[kh].astype(BF16), preferred_element_type=F32))
        outs.append(o4)
    return outs


def _attn_specs():
    prev = lambda n: jnp.maximum(n - 1, 0)
    return [
        pl.BlockSpec((BLK, D_MODEL), lambda n: (n, C_Q // D_MODEL)),
        pl.BlockSpec((BLK, KV_W), lambda n: (prev(n), C_K // KV_W)),
        pl.BlockSpec((BLK, KV_W), lambda n: (n, C_K // KV_W)),
        pl.BlockSpec((BLK, KV_W), lambda n: (prev(n), C_V // KV_W)),
        pl.BlockSpec((BLK, KV_W), lambda n: (n, C_V // KV_W)),
        pl.BlockSpec((N_META, KV_W), lambda n: (PAD // N_META, C_K // KV_W)),
        pl.BlockSpec((N_META, KV_W), lambda n: (PAD // N_META, C_V // KV_W)),
        pl.BlockSpec((1, 128), lambda n: (0, 0)),
    ]


def _attn_load(q_ref, kp_ref, kc_ref, vp_ref, vc_ref, km_ref, vm_ref):
    q4s, kcats, vcats, kms, vms = [], [], [], [], []
    for kh in range(KV_HEADS):
        q4s.append(jnp.concatenate(
            [q_ref[:, (kh * GROUP + g) * HEAD:(kh * GROUP + g + 1) * HEAD] for g in range(GROUP)], axis=0))
        cs = slice(kh * HEAD, (kh + 1) * HEAD)
        kcats.append(jnp.concatenate([kp_ref[:, cs], kc_ref[:, cs]], axis=0))
        vcats.append(jnp.concatenate([vp_ref[:, cs], vc_ref[:, cs]], axis=0))
        kms.append(km_ref[:, cs])
        vms.append(vm_ref[:, cs])
    return q4s, kcats, vcats, kms, vms


def _attn_fwd(proj, sinks):
    def body(q_ref, kp_ref, kc_ref, vp_ref, vc_ref, km_ref, vm_ref, s_ref, o_ref):
        n = pl.program_id(0)
        args = _attn_load(q_ref, kp_ref, kc_ref, vp_ref, vc_ref, km_ref, vm_ref)
        outs = _attn_fn(*args, s_ref[...], n)
        for kh in range(KV_HEADS):
            for g in range(GROUP):
                hh = kh * GROUP + g
                o_ref[:, hh * HEAD:(hh + 1) * HEAD] = outs[kh][g * BLK:(g + 1) * BLK]

    return pl.pallas_call(
        body, grid=(NB,), in_specs=_attn_specs(),
        out_specs=pl.BlockSpec((BLK, D_MODEL), lambda n: (n, 0)),
        out_shape=SDS((T, D_MODEL), F32), name="attn_fwd")(proj, proj, proj, proj, proj, proj, proj, sinks)


def _attn_bwd(proj, sinks, do, dproj):
    def body(q_ref, kp_ref, kc_ref, vp_ref, vc_ref, km_ref, vm_ref, s_ref, do_ref, _, dq_ref, dk_ref, dv_ref, ds_ref):
        n = pl.program_id(0)

        @pl.when(n == 0)
        def _():
            dk_ref[...] = jnp.zeros_like(dk_ref)
            dv_ref[...] = jnp.zeros_like(dv_ref)
            ds_ref[...] = jnp.zeros_like(ds_ref)

        args = _attn_load(q_ref, kp_ref, kc_ref, vp_ref, vc_ref, km_ref, vm_ref)
        _, vjp = jax.vjp(lambda a, b, c, d, e, f: _attn_fn(a, b, c, d, e, f, n), *args, s_ref[...])
        do_f = do_ref[...].astype(F32)
        cot = [jnp.concatenate([do_f[:, (kh * GROUP + g) * HEAD:(kh * GROUP + g + 1) * HEAD] for g in range(GROUP)],
                               axis=0) for kh in range(KV_HEADS)]
        dq4s, dkcats, dvcats, dkms, dvms, dsk = vjp(cot)
        ds_ref[0:1, :] += dsk
        cur = pl.ds(pl.multiple_of(n * BLK, BLK), BLK)
        meta = slice(PAD, PAD + N_META)
        for kh in range(KV_HEADS):
            cs = slice(kh * HEAD, (kh + 1) * HEAD)
            for g in range(GROUP):
                hh = kh * GROUP + g
                dq_ref[:, hh * HEAD:(hh + 1) * HEAD] = dq4s[kh][g * BLK:(g + 1) * BLK].astype(BF16)
            dk_ref[cur, cs] += dkcats[kh][BLK:]
            dv_ref[cur, cs] += dvcats[kh][BLK:]
            dk_ref[meta, cs] += dkms[kh]
            dv_ref[meta, cs] += dvms[kh]

        @pl.when(n > 0)
        def _():
            prv = pl.ds(pl.multiple_of((n - 1) * BLK, BLK), BLK)
            for kh in range(KV_HEADS):
                cs = slice(kh * HEAD, (kh + 1) * HEAD)
                dk_ref[prv, cs] += dkcats[kh][:BLK]
                dv_ref[prv, cs] += dvcats[kh][:BLK]

    full_kv = pl.BlockSpec((T, KV_W), lambda n: (0, 0))
    return pl.pallas_call(
        body, grid=(NB,),
        in_specs=_attn_specs() + [pl.BlockSpec((BLK, D_MODEL), lambda n: (n, 0)), ANY],
        out_specs=[pl.BlockSpec((BLK, D_MODEL), lambda n: (n, C_Q // D_MODEL)), full_kv, full_kv,
                   pl.BlockSpec((8, 128), lambda n: (0, 0))],
        out_shape=[SDS((T, PW), BF16), SDS((T, KV_W), F32), SDS((T, KV_W), F32), SDS((8, 128), F32)],
        input_output_aliases={9: 0},
        name="attn_bwd")(proj, proj, proj, proj, proj, proj, proj, sinks, do, dproj)


def _conv_taps(xp, w, rows):
    return (w[0:1] * xp[5:5 + rows] + w[1:2] * xp[6:6 + rows] + w[2:3] * xp[7:7 + rows] + w[3:4] * xp[8:8 + rows])


HPG = SSM_HEADS // SSM_GROUPS


def _iota(shape, dim):
    return lax.broadcasted_iota(jnp.int32, shape, dim)


def _mm(a, b, ca=1, cb=0):
    return lax.dot_general(a.astype(BF16), b.astype(BF16), (((ca,), (cb,)), ((), ())), preferred_element_type=F32)


def _split3(v):
    hi = v.astype(BF16)
    r1 = v - hi.astype(F32)
    mid = r1.astype(BF16)
    lo = (r1 - mid.astype(F32)).astype(BF16)
    return hi, mid, lo


def _sel_r(parts, onehot, ca=1, cb=0):
    out = lax.dot_general(parts[0], onehot, (((ca,), (cb,)), ((), ())), preferred_element_type=F32)
    for p in parts[1:]:
        out = out + lax.dot_general(p, onehot, (((ca,), (cb,)), ((), ())), preferred_element_type=F32)
    return out


def _sel_l(onehot, parts):
    out = jnp.dot(onehot, parts[0], preferred_element_type=F32)
    for p in parts[1:]:
        out = out + jnp.dot(onehot, p, preferred_element_type=F32)
    return out


def _rows8(*rows):
    r = _iota((8, rows[0].shape[1]), 0)
    out = jnp.zeros((8, rows[0].shape[1]), F32)
    for k, v in enumerate(rows):
        out = jnp.where(r == k, v, out)
    return out


def _ssd_forward(x, z, bm, cm, dt_raw, st_prev, dtb, alog, dskip, gn, g, cst_scr):
    li, si = _iota((BLK, BLK), 0), _iota((BLK, BLK), 1)
    dt_all = jax.nn.softplus(dt_raw + dtb)
    a_row = -jnp.exp(alog)
    a_all = dt_all * a_row
    cs_all = _sel_l((li >= si).astype(BF16), _split3(a_all))
    cs_parts = _split3(cs_all)
    spread = (_iota((BLK, GRP_W), 0) == g * HPG + jnp.right_shift(_iota((BLK, GRP_W), 1), 6)).astype(BF16)
    dt_e = _sel_r(_split3(dt_all), spread)
    cs_e = _sel_r(cs_parts, spread)
    d_e = _sel_r(_split3(_rows8(dskip)), spread)[0:1]
    cs_last_e = jnp.sum(jnp.where(_iota((BLK, GRP_W), 0) == BLK - 1, cs_e, 0.0), axis=0, keepdims=True)
    p_e = jnp.exp(cs_e)
    w_e = jnp.exp(cs_last_e - cs_e)
    cd_e = jnp.exp(cs_last_e)
    xr = x * dt_e
    cst_scr[...] = cs_all.T
    cst_g = cst_scr[g * HPG:(g + 1) * HPG, :]
    own = jnp.right_shift(_iota((HPG, HPG * BLK), 1), 7) == _iota((HPG, HPG * BLK), 0)
    ownf = own.astype(F32)
    q_rows = [ownf, ownf, ownf] + [jnp.where(own, jnp.concatenate([p.astype(F32)] * HPG, axis=1), 0.0)
                                   for p in _split3(cst_g)]
    q2 = jnp.concatenate(q_rows + [jnp.zeros((BLK - 6 * HPG, HPG * BLK), F32)], axis=0).astype(BF16)
    lane1 = _iota((1, BLK), 1)
    p2 = jnp.where((lane1 >= 3 * HPG) & (lane1 < 6 * HPG), -1.0, 0.0)
    for k, part in enumerate(cs_parts):
        pick = ((li == g * HPG + si - k * HPG) & (si >= k * HPG) & (si < (k + 1) * HPG)).astype(BF16)
        p2 = p2 + jnp.dot(part, pick, preferred_element_type=F32)
    dmat = jnp.dot(p2.astype(BF16), q2, preferred_element_type=F32)
    causal = _iota((BLK, HPG * BLK), 0) >= jnp.bitwise_and(_iota((BLK, HPG * BLK), 1), BLK - 1)
    lam = jnp.exp(jnp.where(causal, dmat, NEG))
    gmat = _mm(cm, bm, 1, 1)
    m_all = lam * jnp.concatenate([gmat] * HPG, axis=1)
    mb = m_all.astype(BF16)
    lo = _iota((BLK, BLK), 1) < HEAD
    xrb = xr.astype(BF16)
    zero = jnp.zeros((BLK, BLK), BF16)
    bds, yd = [], []
    for i in range(HPG // 2):
        t = xrb[:, BLK * i:BLK * (i + 1)]
        bd = jnp.concatenate([jnp.where(lo, t, zero), jnp.where(lo, zero, t)], axis=0)
        bds.append(bd)
        yd.append(jnp.dot(mb[:, 2 * BLK * i:2 * BLK * (i + 1)], bd, preferred_element_type=F32))
    cs_st = _mm(cm, st_prev)
    y = jnp.concatenate(yd, axis=1) + cs_st * p_e + d_e * x
    xrw = xr * w_e
    st_new = cd_e * st_prev + _mm(bm, xrw, 0, 0)
    yz = y * _silu(z)
    rn = lax.rsqrt(jnp.sum(yz * yz, axis=1, keepdims=True) / GRP_W + EPS)
    return dict(out=yz * rn * gn, st_new=st_new, dt_all=dt_all, a_row=a_row, dt_e=dt_e, d_e=d_e, p_e=p_e, w_e=w_e,
                cd_e=cd_e, xr=xr, xrw=xrw, lam=lam, m_all=m_all, mb=mb, bds=bds, cs_st=cs_st, y=y, yz=yz, rn=rn, lo=lo)


def _ssd_backward(f, x, z, bm, cm, dt_raw, st_prev, dtb, gn, g, dout, dst_next, cst_scr):
    li, si = _iota((BLK, BLK), 0), _iota((BLK, BLK), 1)
    yz, rn, y, p_e, w_e, cd_e, xr = f["yz"], f["rn"], f["y"], f["p_e"], f["w_e"], f["cd_e"], f["xr"]
    dgn = jnp.sum(dout * yz * rn, axis=0, keepdims=True)
    t = dout * gn
    dyz = rn * t - yz * (rn * rn * rn) * (jnp.sum(yz * t, axis=1, keepdims=True) / GRP_W)
    dy = dyz * _silu(z)
    dz = dyz * y * _dsilu(z)
    dx = f["d_e"] * dy
    dd_e = jnp.sum(dy * x, axis=0, keepdims=True)
    dcsst = dy * p_e
    dp_e = dy * f["cs_st"]
    dcm = _mm(dcsst, st_prev, 1, 1)
    dst_prev = _mm(cm, dcsst, 0, 0) + cd_e * dst_next
    dcd_e = jnp.sum(dst_next * st_prev, axis=0, keepdims=True)
    dbm = _mm(f["xrw"], dst_next, 1, 1)
    dxrw = _mm(bm, dst_next)
    dxr = dxrw * w_e
    dw_e = dxrw * xr
    dyb = dy.astype(BF16)
    dms, dxr_d = [], []
    for i in range(HPG // 2):
        dyp = dyb[:, BLK * i:BLK * (i + 1)]
        dms.append(lax.dot_general(dyp, f["bds"][i], (((1,), (1,)), ((), ())), preferred_element_type=F32))
        r = lax.dot_general(f["mb"][:, 2 * BLK * i:2 * BLK * (i + 1)], dyp, (((0,), (0,)), ((), ())),
                            preferred_element_type=F32)
        dxr_d.append(jnp.where(f["lo"], r[0:BLK], r[BLK:2 * BLK]))
    dm_all = jnp.concatenate(dms, axis=1)
    dxr = dxr + jnp.concatenate(dxr_d, axis=1)
    dlg = dm_all * f["lam"]
    dg = dlg[:, 0:BLK]
    for j in range(1, HPG):
        dg = dg + dlg[:, BLK * j:BLK * (j + 1)]
    dcm = dcm + _mm(dg, bm)
    dbm = dbm + _mm(dg, cm, 0, 0)
    q_all = dm_all * f["m_all"]
    col_sums = jnp.sum(q_all, axis=0, keepdims=True)
    cst_scr[...] = jnp.zeros_like(cst_scr)
    cst_scr[g * HPG:(g + 1) * HPG, :] = _rows8(
        *[col_sums[:, BLK * j:BLK * (j + 1)] for j in range(HPG)])
    dcs = -cst_scr[...].T
    for j in range(HPG):
        dcs = dcs + jnp.where(si == g * HPG + j,
                              jnp.sum(q_all[:, BLK * j:BLK * (j + 1)], axis=1, keepdims=True), 0.0)
    unspread = (_iota((GRP_W, BLK), 1) == g * HPG + jnp.right_shift(_iota((GRP_W, BLK), 0), 6)).astype(BF16)
    dww = dw_e * w_e
    per_head = _sel_r(_split3(jnp.concatenate([dp_e * p_e - dww, dxr * x], axis=0)), unspread)
    last = _sel_r(_split3(_rows8(jnp.sum(dww, axis=0, keepdims=True) + dcd_e * cd_e, dd_e)), unspread)
    dcs = dcs + per_head[0:BLK] + jnp.where(li == BLK - 1, last[0:1], 0.0)
    da = _sel_l((si >= li).astype(BF16), _split3(dcs))
    ddt_all = da * f["a_row"] + per_head[BLK:2 * BLK]
    dalog = jnp.sum(da * f["dt_all"], axis=0, keepdims=True) * f["a_row"]
    dx = dx + dxr * f["dt_e"]
    ddt_raw = ddt_all * jax.nn.sigmoid(dt_raw + dtb)
    ddtb = jnp.sum(ddt_raw, axis=0, keepdims=True)
    ddskip = last[1:2]
    return dict(dx=dx, dz=dz, dbm=dbm, dcm=dcm, ddt_raw=ddt_raw, dst_prev=dst_prev, ddtb=ddtb, dalog=dalog,
                ddskip=ddskip, dgn=dgn)


ZX_W = SSM_INNER + CONV_DIM
assert C_ZS == 0 and C_XBC == SSM_INNER


def _ssd_in_specs(rev):
    cidx = (lambda c: NB - 1 - c) if rev else (lambda c: c)
    return [
        pl.BlockSpec((BLK, ZX_W), lambda c: (cidx(c), 0)),
        pl.BlockSpec((8, ZX_W), lambda c: (jnp.maximum(cidx(c) * (BLK // 8) - 1, 0), 0)),
        pl.BlockSpec((BLK, 128), lambda c: (cidx(c), C_DT // 128)),
        pl.BlockSpec((8, CONV_DIM), lambda c: (0, 0)),
        pl.BlockSpec((1, CONV_DIM), lambda c: (0, 0)),
        pl.BlockSpec((1, 128), lambda c: (0, 0)),
        pl.BlockSpec((1, 128), lambda c: (0, 0)),
        pl.BlockSpec((1, 128), lambda c: (0, 0)),
        pl.BlockSpec((1, SSM_INNER), lambda c: (0, 0)),
    ]


def _xbc_act(zx_ref, tail_ref, w_ref, b_ref, n):
    tail = jnp.where(n > 0, tail_ref[:, SSM_INNER:], 0.0)
    xp = jnp.concatenate([tail, zx_ref[:, SSM_INNER:]], axis=0)
    conv = _conv_taps(xp, w_ref[...], BLK) + b_ref[...]
    valid = n * BLK + _iota((BLK, 1), 0) >= PAD
    return xp, conv, valid, jnp.where(valid, _silu(conv), 0.0)


def _grp_cols(act, i):
    b0, c0 = SSM_INNER + i * SSM_STATE, SSM_INNER + (SSM_GROUPS + i) * SSM_STATE
    return act[:, i * GRP_W:(i + 1) * GRP_W], act[:, b0:b0 + SSM_STATE], act[:, c0:c0 + SSM_STATE]


def _ssd_fwd(proj, conv_w, conv_b, dt_bias, a_log, d_skip, g_norm, gather=()):
    ng = len(gather)

    def body(*refs):
        zx_ref, tail_ref, dt_ref, w_ref, b_ref, dtb_ref, al_ref, dsk_ref, gn_ref = refs[:9]
        y_ref, st_ref = refs[9 + ng:11 + ng]
        s_scr, cst_scr = refs[11 + 2 * ng:13 + 2 * ng]
        c = pl.program_id(0)
        if ng:
            ag_start, ag_forward, ag_finish = _ag_program(refs[9:9 + ng], refs[11 + ng:11 + 2 * ng],
                                                          refs[13 + 2 * ng:])
            pl.when(c == 0)(ag_start)
            pl.when(c == (3 * NB) // 4)(ag_forward)

        @pl.when(c == 0)
        def _():
            s_scr[...] = jnp.zeros_like(s_scr)

        _, _, _, act = _xbc_act(zx_ref, tail_ref, w_ref, b_ref, c)
        for i in range(SSM_GROUPS):
            st_prev = s_scr[i]
            st_ref[i, 0] = st_prev
            x, bm, cm = _grp_cols(act, i)
            f = _ssd_forward(x, zx_ref[:, i * GRP_W:(i + 1) * GRP_W], bm, cm, dt_ref[...], st_prev, dtb_ref[...],
                             al_ref[...], dsk_ref[...], gn_ref[:, i * GRP_W:(i + 1) * GRP_W], i, cst_scr.at[i])
            y_ref[:, i * GRP_W:(i + 1) * GRP_W] = f["out"].astype(BF16)
            s_scr[i] = f["st_new"]
        if ng:
            pl.when(c == NB - 1)(ag_finish)

    return pl.pallas_call(
        body, grid=(NB,), in_specs=_ssd_in_specs(False) + [ANY] * ng,
        out_specs=[pl.BlockSpec((BLK, SSM_INNER), lambda c: (c, 0)),
                   pl.BlockSpec((SSM_GROUPS, 1, SSM_STATE, GRP_W), lambda c: (0, c, 0, 0))] + [ANY] * ng,
        out_shape=[SDS((T, SSM_INNER), BF16), SDS((SSM_GROUPS, NB, SSM_STATE, GRP_W), F32)]
        + [SDS((N_DEV,) + s.shape, s.dtype) for s in gather],
        scratch_shapes=[pltpu.VMEM((SSM_GROUPS, SSM_STATE, GRP_W), F32), pltpu.VMEM((SSM_GROUPS, BLK, BLK), F32)]
        + (_ag_scratch(gather) if ng else []),
        compiler_params=_cparams(),
        name="ssd_fwd")(proj, proj, proj, conv_w, conv_b, dt_bias, a_log, d_skip, g_norm, *gather)


def _ssd_bwd(proj, conv_w, conv_b, dt_bias, a_log, d_skip, g_norm, states, dy, dproj, exchange=()):
    ne = len(exchange)

    def body(*refs):
        zx_ref, tail_ref, dt_ref, w_ref, b_ref, dtb_ref, al_ref, dsk_ref, gn_ref, st_ref, dy_ref = refs[:11]
        (ddt_ref, dp_ref, ddtb_ref, dal_ref, ddsk_ref, dgn_ref, dcw_ref, dcb_ref) = refs[12 + ne:20 + ne]
        ds_scr, cst_scr, carry = refs[20 + 2 * ne:23 + 2 * ne]
        c = pl.program_id(0)
        n = NB - 1 - c
        if ne:
            ex_start, ex_finish = _direct_program(refs[12:12 + ne], refs[20 + ne:20 + 2 * ne], refs[23 + 2 * ne:])
            pl.when(c == 0)(ex_start)

        @pl.when(c == 0)
        def _():
            for ref in (ds_scr, carry, dgn_ref, ddtb_ref, dal_ref, ddsk_ref, dcw_ref, dcb_ref):
                ref[...] = jnp.zeros_like(ref)

        xp, conv, valid, act = _xbc_act(zx_ref, tail_ref, w_ref, b_ref, n)
        dt_raw = dt_ref[...]
        dxs, dbs, dcs = [], [], []
        for i in range(SSM_GROUPS):
            x, bm, cm = _grp_cols(act, i)
            z, gn, st_prev = zx_ref[:, i * GRP_W:(i + 1) * GRP_W], gn_ref[:, i * GRP_W:(i + 1) * GRP_W], st_ref[i, 0]
            f = _ssd_forward(x, z, bm, cm, dt_raw, st_prev, dtb_ref[...], al_ref[...], dsk_ref[...], gn, i,
                             cst_scr.at[i])
            d = _ssd_backward(f, x, z, bm, cm, dt_raw, st_prev, dtb_ref[...], gn, i,
                              dy_ref[:, i * GRP_W:(i + 1) * GRP_W].astype(F32), ds_scr[i], cst_scr.at[i])
            dxs.append(d["dx"])
            dbs.append(d["dbm"])
            dcs.append(d["dcm"])
            dp_ref[:, i * GRP_W:(i + 1) * GRP_W] = d["dz"].astype(BF16)
            ds_scr[i] = d["dst_prev"]
            ddt_ref[:, i * 128:(i + 1) * 128] = d["ddt_raw"]
            dgn_ref[0:1, i * GRP_W:(i + 1) * GRP_W] += d["dgn"]
            ddtb_ref[0:1, :] += d["ddtb"]
            dal_ref[0:1, :] += d["dalog"]
            ddsk_ref[0:1, :] += d["ddskip"]
        dconv = jnp.where(valid, jnp.concatenate(dxs + dbs + dcs, axis=1) * _dsilu(conv), 0.0)
        dext = jnp.concatenate([dconv, carry[...]], axis=0)
        w = w_ref[...]
        dp_ref[:, SSM_INNER:] = (w[0:1] * dext[3:3 + BLK] + w[1:2] * dext[2:2 + BLK] + w[2:3] * dext[1:1 + BLK]
                                 + w[3:4] * dext[0:BLK]).astype(BF16)
        carry[...] = dconv[0:8]
        dcw_ref[...] += jnp.concatenate(
            [jnp.sum(dconv * xp[5 + k:5 + k + BLK], axis=0, keepdims=True) for k in range(4)]
            + [jnp.zeros((4, CONV_DIM), F32)], axis=0)
        dcb_ref[0:1, :] += jnp.sum(dconv, axis=0, keepdims=True)
        if ne:
            pl.when(c == NB - 1)(ex_finish)

    rc = lambda c: NB - 1 - c
    small = pl.BlockSpec((8, 128), lambda c: (0, 0))
    wide = lambda w: pl.BlockSpec((8, w), lambda c: (0, 0))
    return pl.pallas_call(
        body, grid=(NB,),
        in_specs=_ssd_in_specs(True) + [
            pl.BlockSpec((SSM_GROUPS, 1, SSM_STATE, GRP_W), lambda c: (0, rc(c), 0, 0)),
            pl.BlockSpec((BLK, SSM_INNER), lambda c: (rc(c), 0)), ANY] + [ANY] * ne,
        out_specs=[pl.BlockSpec((BLK, SSM_GROUPS * 128), lambda c: (rc(c), 0)),
                   pl.BlockSpec((BLK, ZX_W), lambda c: (rc(c), 0)),
                   small, small, small, wide(SSM_INNER), wide(CONV_DIM), wide(CONV_DIM)] + [ANY] * ne,
        out_shape=[SDS((T, GRP_W), F32), SDS((T, PW), BF16), SDS((8, 128), F32), SDS((8, 128), F32),
                   SDS((8, 128), F32), SDS((8, SSM_INNER), F32), SDS((8, CONV_DIM), F32), SDS((8, CONV_DIM), F32)]
        + [SDS(p.shape, p.dtype) for p in exchange],
        scratch_shapes=[pltpu.VMEM((SSM_GROUPS, SSM_STATE, GRP_W), F32), pltpu.VMEM((SSM_GROUPS, BLK, BLK), F32),
                        pltpu.VMEM((8, CONV_DIM), F32)] + (_direct_scratch(exchange) if ne else []),
        input_output_aliases={11: 1},
        compiler_params=_cparams(),
        name="ssd_bwd")(proj, proj, proj, conv_w, conv_b, dt_bias, a_log, d_skip, g_norm, states, dy, dproj,
                        *exchange)


POST_R = 272


def _post_a(o, proj, sn, w_att, w_ssm, w_o):
    def body(o_ref, za_ref, ga_ref, gs_ref, sn_ref, wa_ref, ws_ref, wo_ref, a_ref, mg_ref, ya_ref, ys_ref, out_ref):
        a = (o_ref[...] * _silu(za_ref[...])).astype(BF16)
        a_ref[...] = a
        ya = jnp.dot(a, wa_ref[...], preferred_element_type=F32)
        ys = jnp.dot(sn_ref[...], ws_ref[...], preferred_element_type=F32)
        ya_ref[...] = ya.astype(BF16)
        ys_ref[...] = ys.astype(BF16)
        mg = (jax.nn.sigmoid(ga_ref[...]) * ya + jax.nn.sigmoid(gs_ref[...]) * ys).astype(BF16)
        mg_ref[...] = mg
        out_ref[...] = jnp.dot(mg, wo_ref[...], preferred_element_type=F32)

    row = pl.BlockSpec((POST_R, D_MODEL), lambda i: (i, 0))
    pcol = lambda c0: pl.BlockSpec((POST_R, D_MODEL), lambda i: (i, c0 // D_MODEL))
    full = lambda r: pl.BlockSpec((r, D_MODEL), lambda i: (0, 0))
    return pl.pallas_call(
        body, grid=(T // POST_R,),
        in_specs=[row, pcol(C_ZA), pcol(C_GA), pcol(C_GS), pl.BlockSpec((POST_R, SSM_INNER), lambda i: (i, 0)),
                  full(D_MODEL), full(SSM_INNER), full(D_MODEL)],
        out_specs=[row, row, row, row, row],
        out_shape=[SDS((T, D_MODEL), BF16), SDS((T, D_MODEL), BF16), SDS((T, D_MODEL), BF16), SDS((T, D_MODEL), BF16),
                   SDS((T, D_MODEL), F32)],
        compiler_params=_cparams(), name="post_a")(o, proj, proj, proj, sn, w_att, w_ssm, w_o)


def _post_b(out, h, tgt, proj, ya, ys, o, g_post, w_att, w_ssm, w_o):
    def body(out_ref, h_ref, t_ref, za_ref, ga_ref, gs_ref, ya_ref, ys_ref, o_ref, gp_ref, wa_ref, ws_ref, wo_ref,
             loss_ref, dres_ref, dout_ref, dya_ref, dys_ref, do_ref, dp_ref, dsn_ref, dgp_ref):
        i = pl.program_id(0)
        x = out_ref[...]
        gp = gp_ref[...]
        r = lax.rsqrt(jnp.mean(x * x, axis=-1, keepdims=True) + EPS)
        row = i * POST_R + lax.broadcasted_iota(jnp.int32, (POST_R, 1), 0)
        res = h_ref[...] + jnp.where(row >= PAD, x * r * gp, 0.0)
        live = row >= PAD + N_META
        err = jnp.where(live, res - t_ref[...], 0.0)
        lpart = 0.5 * jnp.sum(jnp.sum(err * err, axis=1, keepdims=True) / D_MODEL, axis=0, keepdims=True)
        dres = err / D_MODEL
        dres_ref[...] = dres
        gpart = jnp.sum(dres * x * r, axis=0, keepdims=True)

        @pl.when(i == 0)
        def _():
            loss_ref[...] = jnp.zeros_like(loss_ref)
            dgp_ref[...] = jnp.zeros_like(dgp_ref)

        loss_ref[...] += jnp.broadcast_to(lpart, loss_ref.shape)
        dgp_ref[0:1, :] += gpart
        gd = gp * dres
        dout = (r * gd - x * (r * r * r) * jnp.mean(x * gd, axis=-1, keepdims=True)).astype(BF16)
        dout_ref[...] = dout
        dmg = lax.dot_general(dout, wo_ref[...], (((1,), (1,)), ((), ())), preferred_element_type=F32)
        sga = jax.nn.sigmoid(ga_ref[...])
        sgs = jax.nn.sigmoid(gs_ref[...])
        dya = (dmg * sga).astype(BF16)
        dys = (dmg * sgs).astype(BF16)
        dya_ref[...] = dya
        dys_ref[...] = dys
        dp_ref[:, C_GA - C_ZA:C_GA - C_ZA + D_MODEL] = (dmg * ya_ref[...].astype(F32) * sga * (1.0 - sga)).astype(BF16)
        dp_ref[:, C_GS - C_ZA:C_GS - C_ZA + D_MODEL] = (dmg * ys_ref[...].astype(F32) * sgs * (1.0 - sgs)).astype(BF16)
        da = lax.dot_general(dya, wa_ref[...], (((1,), (1,)), ((), ())), preferred_element_type=F32)
        za = za_ref[...]
        do_ref[...] = (da * _silu(za)).astype(BF16)
        dp_ref[:, 0:D_MODEL] = (da * o_ref[...] * _dsilu(za)).astype(BF16)
        dsn_ref[...] = lax.dot_general(dys, ws_ref[...], (((1,), (1,)), ((), ())),
                                       preferred_element_type=F32).astype(BF16)

    row = pl.BlockSpec((POST_R, D_MODEL), lambda i: (i, 0))
    pcol = lambda c0: pl.BlockSpec((POST_R, D_MODEL), lambda i: (i, c0 // D_MODEL))
    full = lambda r: pl.BlockSpec((r, D_MODEL), lambda i: (0, 0))
    small = pl.BlockSpec((8, D_MODEL), lambda i: (0, 0))
    return pl.pallas_call(
        body, grid=(T // POST_R,),
        in_specs=[row, row, row, pcol(C_ZA), pcol(C_GA), pcol(C_GS), row, row, row,
                  pl.BlockSpec((1, D_MODEL), lambda i: (0, 0)), full(D_MODEL), full(SSM_INNER), full(D_MODEL)],
        out_specs=[pl.BlockSpec((8, 128), lambda i: (0, 0)), row, row, row, row, row,
                   pl.BlockSpec((POST_R, GATES_W), lambda i: (i, C_ZA // GATES_W)),
                   pl.BlockSpec((POST_R, SSM_INNER), lambda i: (i, 0)), small],
        out_shape=[SDS((8, 128), F32), SDS((T, D_MODEL), F32), SDS((T, D_MODEL), BF16), SDS((T, D_MODEL), BF16),
                   SDS((T, D_MODEL), BF16), SDS((T, D_MODEL), BF16), SDS((T, PW), BF16),
                   SDS((T, SSM_INNER), BF16), SDS((8, D_MODEL), F32)],
        compiler_params=_cparams(), name="post_b")(out, h, tgt, proj, proj, proj, ya, ys, o, g_post, w_att, w_ssm, w_o)


TAIL_W = PW - C_K


def _dproj_tail(dproj, dk, dv, ddt4):
    rows = T // 4

    def body(_, dk_ref, dv_ref, ddt_ref, o_ref, buf, sem):
        n = pl.program_id(0)
        d4 = ddt_ref[...]
        buf[:, 0:KV_W] = dk_ref[...].astype(BF16)
        buf[:, KV_W:2 * KV_W] = dv_ref[...].astype(BF16)
        buf[:, 2 * KV_W:TAIL_W] = (d4[:, 0:128] + d4[:, 128:256] + d4[:, 256:384] + d4[:, 384:512]).astype(BF16)
        cp = pltpu.make_async_copy(buf, o_ref.at[pl.ds(pl.multiple_of(n * rows, 16), rows), pl.ds(C_K, TAIL_W)], sem)
        cp.start()
        cp.wait()

    spec = lambda w: pl.BlockSpec((rows, w), lambda i: (i, 0))
    return pl.pallas_call(
        body, grid=(T // rows,), in_specs=[ANY, spec(KV_W), spec(KV_W), spec(GRP_W)], out_specs=ANY,
        out_shape=SDS((T, PW), BF16), input_output_aliases={0: 0},
        scratch_shapes=[pltpu.VMEM((rows, TAIL_W), BF16), pltpu.SemaphoreType.DMA],
        name="dproj_tail")(dproj, dk, dv, ddt4)


def _adamw_math(w, g, m, v):
    m = ADAM_B1 * m + (1.0 - ADAM_B1) * g
    v = ADAM_B2 * v + (1.0 - ADAM_B2) * (g * g)
    m_hat = m / (1.0 - ADAM_B1 ** ADAM_STEP)
    v_hat = v / (1.0 - ADAM_B2 ** ADAM_STEP)
    delta = -ADAM_LR * (m_hat / (jnp.sqrt(v_hat) + ADAM_EPS) + ADAM_WD * w)
    return delta, m, v


def _sum_adamw(recv, w, m, v, tc, name):
    rows, cols = w.shape
    nslab = recv.shape[0]
    assert cols % tc == 0

    def body(r_ref, w_ref, m_ref, v_ref, g_ref, d_ref, nm_ref, nv_ref):
        g = r_ref[0].astype(F32)
        for d in range(1, nslab):
            g = g + r_ref[d].astype(F32)
        g_ref[...] = g
        delta, nm, nv = _adamw_math(w_ref[...], g, m_ref[...], v_ref[...])
        d_ref[...] = delta
        nm_ref[...] = nm
        nv_ref[...] = nv

    blk = pl.BlockSpec((rows, tc), lambda i: (0, i))
    return pl.pallas_call(
        body, grid=(cols // tc,),
        in_specs=[pl.BlockSpec((nslab, rows, tc), lambda i: (0, 0, i)), blk, blk, blk],
        out_specs=[blk, blk, blk, blk], out_shape=[SDS((rows, cols), F32)] * 4,
        compiler_params=_cparams(), name=name)(recv, w, m, v)


def _sum_adamw_rows3(recv, w3, m3, v3, name, exchange=()):
    pairs = 61
    assert (SHARD_IN // 2) % pairs == 0
    nsteps = SHARD_IN // 2 // pairs
    ne = len(exchange)

    def body(*refs):
        r_ref, w_ref, m_ref, v_ref = refs[:4]
        g_ref, d_ref, nm_ref, nv_ref = refs[4 + ne:8 + ne]
        if ne:
            ex_start, ex_finish = _direct_program(refs[4:4 + ne], refs[8 + ne:8 + 2 * ne], refs[8 + 2 * ne:])
            pl.when(pl.program_id(0) == 0)(ex_start)
        g = r_ref[0].astype(F32)
        for d in range(1, N_CHIP):
            g = g + r_ref[d].astype(F32)
        g = g.reshape(2 * pairs, ROW_TILES, 128)
        g_ref[...] = g
        delta, nm, nv = _adamw_math(w_ref[...], g, m_ref[...], v_ref[...])
        d_ref[...] = delta
        nm_ref[...] = nm
        nv_ref[...] = nv
        if ne:
            pl.when(pl.program_id(0) == nsteps - 1)(ex_finish)

    blk = pl.BlockSpec((2 * pairs, ROW_TILES, 128), lambda i: (i, 0, 0))
    return pl.pallas_call(
        body, grid=(nsteps,),
        in_specs=[pl.BlockSpec((N_CHIP, pairs, 2 * ROW_TILES, 128), lambda i: (0, i, 0, 0)), blk, blk, blk]
        + [ANY] * ne,
        out_specs=[blk, blk, blk, blk] + [ANY] * ne,
        out_shape=[SDS(w3.shape, F32)] * 4 + [SDS(p.shape, p.dtype) for p in exchange],
        scratch_shapes=_direct_scratch(exchange) if ne else [],
        compiler_params=_cparams(), name=name)(recv, w3, m3, v3, *exchange)


ROW_GPRE, ROW_CONVB, ROW_DTB, ROW_ALOG, ROW_DSKIP, ROW_SINK, ROW_GSSM, ROW_GPOST = 0, 1, 4, 5, 6, 7, 8, 10
ROW_LOSS = 11
REP_ROWS, ROW_CONVW, ROW_META, SM_ROWS = 16, 16, 24, 40
CW_SHARD = CONV_DIM // N_DEV
META_SHARD = D_MODEL // N_DEV


def _small_pack(dgpre, db, ddtb, dal, ddsk, dsink, dgn, dgp, dw, loss, dh):
    def body(dgpre_ref, db_ref, ddtb_ref, dal_ref, ddsk_ref, dsink_ref, dgn_ref, dgp_ref, dw_ref, loss_ref, dh_ref,
             o_ref, rep):
        rep[...] = jnp.zeros_like(rep)
        rep[ROW_LOSS:ROW_LOSS + 1, 0:128] = loss_ref[0:1, :]
        rep[ROW_GPRE:ROW_GPRE + 1, :] = dgpre_ref[0:1, :]
        for k in range(3):
            rep[ROW_CONVB + k:ROW_CONVB + k + 1, :] = db_ref[0:1, 1024 * k:1024 * (k + 1)]
        rep[ROW_DTB:ROW_DTB + 1, 0:128] = ddtb_ref[0:1, :]
        rep[ROW_ALOG:ROW_ALOG + 1, 0:128] = dal_ref[0:1, :]
        rep[ROW_DSKIP:ROW_DSKIP + 1, 0:128] = ddsk_ref[0:1, :]
        rep[ROW_SINK:ROW_SINK + 1, 0:128] = dsink_ref[0:1, :]
        rep[ROW_GSSM:ROW_GSSM + 1, :] = dgn_ref[0:1, 0:1024]
        rep[ROW_GSSM + 1:ROW_GSSM + 2, :] = dgn_ref[0:1, 1024:2048]
        rep[ROW_GPOST:ROW_GPOST + 1, :] = dgp_ref[0:1, :]
        cw = dw_ref[...]
        mh = dh_ref[...]
        o_ref[...] = jnp.zeros_like(o_ref)
        for p in range(N_DEV):
            o_ref[p, 0:REP_ROWS, :] = rep[...]
            o_ref[p, ROW_CONVW:ROW_CONVW + 8, 0:CW_SHARD] = cw[:, p * CW_SHARD:(p + 1) * CW_SHARD]
            o_ref[p, ROW_META:ROW_META + N_META, 0:META_SHARD] = mh[:, p * META_SHARD:(p + 1) * META_SHARD]

    ins = [dgpre, db, ddtb, dal, ddsk, dsink, dgn, dgp, dw, loss]
    return pl.pallas_call(
        body, grid=(1,),
        in_specs=[pl.BlockSpec(a.shape, lambda i: (0, 0)) for a in ins]
        + [pl.BlockSpec((N_META, D_MODEL), lambda i: (PAD // N_META, 0))],
        out_specs=pl.BlockSpec((N_DEV, SM_ROWS, 1024), lambda i: (0, 0, 0)),
        out_shape=SDS((N_DEV, SM_ROWS, 1024), F32), scratch_shapes=[pltpu.VMEM((REP_ROWS, 1024), F32)],
        name="small_pack")(*ins, dh)


def _small_finish(recv, params):
    npar = len(params)

    def body(*refs):
        r_ref = refs[0]
        wmv = refs[1:1 + 3 * npar]
        outs = refs[1 + 3 * npar:1 + 7 * npar]
        loss_ref = refs[1 + 7 * npar]
        gs = refs[-1]
        g = r_ref[0]
        for d in range(1, recv.shape[0]):
            g = g + r_ref[d]
        gs[...] = g
        loss_ref[...] = gs[ROW_LOSS:ROW_LOSS + 1, 0:128]
        grads = [
            gs[ROW_GPRE:ROW_GPRE + 1, :],
            jnp.concatenate([gs[ROW_CONVB + k:ROW_CONVB + k + 1, :] for k in range(3)], axis=1),
            gs[ROW_DTB:ROW_DTB + 1, 0:SSM_HEADS], gs[ROW_ALOG:ROW_ALOG + 1, 0:SSM_HEADS],
            gs[ROW_DSKIP:ROW_DSKIP + 1, 0:SSM_HEADS], gs[ROW_SINK:ROW_SINK + 1, 0:Q_HEADS],
            jnp.concatenate([gs[ROW_GSSM:ROW_GSSM + 1, :], gs[ROW_GSSM + 1:ROW_GSSM + 2, :]], axis=1),
            gs[ROW_GPOST:ROW_GPOST + 1, :],
            gs[ROW_CONVW:ROW_CONVW + 4, 0:CW_SHARD],
            gs[ROW_META:ROW_META + N_META, 0:META_SHARD]]
        for i in range(npar):
            w_ref, m_ref, v_ref = wmv[3 * i:3 * i + 3]
            delta, nm, nv = _adamw_math(w_ref[...], grads[i], m_ref[...], v_ref[...])
            outs[4 * i][...] = grads[i]
            outs[4 * i + 1][...] = delta
            outs[4 * i + 2][...] = nm
            outs[4 * i + 3][...] = nv

    flat = [a for wmv in params for a in wmv]
    res = pl.pallas_call(
        body, out_shape=[SDS(wmv[0].shape, F32) for wmv in params for _ in range(4)] + [SDS((1, 128), F32)],
        scratch_shapes=[pltpu.VMEM((SM_ROWS, 1024), F32)], name="small_finish")(recv, *flat)
    return [tuple(res[4 * i:4 * i + 4]) for i in range(npar)], res[4 * npar]


def _slab(ref, px, py, pc):
    return ref.at[4 * px + 2 * py + pc]


def _bounce(src, dst, buf, sem):
    cp = pltpu.make_async_copy(src, buf, sem)
    cp.start()
    cp.wait()
    cp = pltpu.make_async_copy(buf, dst, sem)
    cp.start()
    cp.wait()


def _ag_program(ins, outs, scratch):
    na = len(ins)
    send_sems, recv_sems, local_sems = scratch[:3]
    bufs = scratch[3:]
    x, y, c = lax.axis_index("x"), lax.axis_index("y"), lax.axis_index("c")
    me, sibling = (x, y, c), (x, y, 1 - c)
    chips = [(1 - x, y), (x, 1 - y), (1 - x, 1 - y)]

    def copy(a, k, block, to, src=None):
        dst = _slab(outs[a], *block)
        return pltpu.make_async_remote_copy(
            src_ref=dst if src is None else src, dst_ref=dst, send_sem=send_sems.at[a, k],
            recv_sem=recv_sems.at[a, k], device_id=to, device_id_type=MESH)

    def own_sends():
        out = []
        for a in range(na):
            out.append(copy(a, 0, me, sibling, src=ins[a]))
            out += [copy(a, 1 + j, me, (*chip, c), src=ins[a]) for j, chip in enumerate(chips)]
        return out

    def start():
        for cp in own_sends():
            cp.start()
        for a in range(na):
            _bounce(ins[a], _slab(outs[a], *me), bufs[a], local_sems.at[a])

    def forward():
        for j, chip in enumerate(chips):
            for a in range(na):
                copy(a, 1 + j, (*chip, c), me).wait_recv()
                copy(a, 4 + j, (*chip, c), sibling).start()

    def finish():
        for a in range(na):
            copy(a, 0, sibling, me).wait_recv()
            for j, chip in enumerate(chips):
                copy(a, 4 + j, (*chip, 1 - c), me).wait_recv()
        for cp in own_sends():
            cp.wait_send()
        for j, chip in enumerate(chips):
            for a in range(na):
                copy(a, 4 + j, (*chip, c), sibling).wait_send()

    return start, forward, finish


def _ag_scratch(shards):
    na = len(shards)
    return [pltpu.SemaphoreType.DMA((na, 7)), pltpu.SemaphoreType.DMA((na, 7)),
            pltpu.SemaphoreType.DMA((na,))] + [pltpu.VMEM(s.shape, s.dtype) for s in shards]


def _all_gather(shards):
    na = len(shards)

    def body(*refs):
        start, forward, finish = _ag_program(refs[:na], refs[na:2 * na], refs[2 * na:])
        start()
        forward()
        finish()

    return pl.pallas_call(
        body, in_specs=[ANY] * na, out_specs=[ANY] * na,
        out_shape=[SDS((N_DEV,) + s.shape, s.dtype) for s in shards],
        scratch_shapes=_ag_scratch(shards), name="all_gather")(*shards)


N_CHIP = 4


def _pair_sum(own, got, name):
    na = len(own)

    def body(*refs):
        for a in range(na):
            o_ref, g_ref, s_ref = refs[a], refs[na + a], refs[2 * na + a]
            s_ref[...] = (o_ref[...].astype(F32) + g_ref[...].astype(F32)).astype(s_ref.dtype)

    def spec(p):
        nd = len(p.shape) - 1
        return pl.BlockSpec((1,) + p.shape[1:], lambda k, nd=nd: (k,) + (0,) * nd)

    return pl.pallas_call(
        body, grid=(N_CHIP,), in_specs=[spec(p) for p in own] + [spec(p) for p in got],
        out_specs=[spec(p) for p in own], out_shape=[SDS(p.shape, p.dtype) for p in own],
        compiler_params=_cparams(), name=name)(*own, *got)


def _chips_program(ins, outs, scratch):
    na = len(ins)
    send_sems, recv_sems, local_sems = scratch[:3]
    bufs = scratch[3:]
    x, y, c = lax.axis_index("x"), lax.axis_index("y"), lax.axis_index("c")
    mine = 2 * x + y
    chips = [(1 - x, y), (x, 1 - y), (1 - x, 1 - y)]

    def send(a, j):
        px, py = chips[j]
        return pltpu.make_async_remote_copy(
            src_ref=ins[a].at[2 * px + py], dst_ref=outs[a].at[mine], send_sem=send_sems.at[a, j],
            recv_sem=recv_sems.at[a, j], device_id=(px, py, c), device_id_type=MESH)

    def arrival(a, j):
        px, py = chips[j]
        return pltpu.make_async_remote_copy(
            src_ref=ins[a].at[2 * px + py], dst_ref=outs[a].at[2 * px + py], send_sem=send_sems.at[a, j],
            recv_sem=recv_sems.at[a, j], device_id=(px, py, c), device_id_type=MESH)

    def start():
        for a in range(na):
            for j in range(3):
                send(a, j).start()
        for a in range(na):
            _bounce(ins[a].at[mine], outs[a].at[mine], bufs[a], local_sems.at[a])

    def finish():
        for a in range(na):
            for j in range(3):
                arrival(a, j).wait_recv()
        for a in range(na):
            for j in range(3):
                send(a, j).wait_send()

    return start, finish


def _chips_scratch(parts):
    na = len(parts)
    return [pltpu.SemaphoreType.DMA((na, 3)), pltpu.SemaphoreType.DMA((na, 3)),
            pltpu.SemaphoreType.DMA((na,))] + [pltpu.VMEM(p.shape[1:], p.dtype) for p in parts]


def _direct_program(ins, outs, scratch):
    na = len(ins)
    send_sems, recv_sems, local_sems = scratch[:3]
    bufs = scratch[3:]
    x, y, c = lax.axis_index("x"), lax.axis_index("y"), lax.axis_index("c")
    me = (x, y, c)
    peers = []
    for k in range(1, N_DEV):
        dx, dy, dc = (k >> 2) & 1, (k >> 1) & 1, k & 1
        peers.append(((1 - x) if dx else x, (1 - y) if dy else y, (1 - c) if dc else c))

    def send(a, k):
        return pltpu.make_async_remote_copy(
            src_ref=_slab(ins[a], *peers[k]), dst_ref=_slab(outs[a], *me), send_sem=send_sems.at[a, k],
            recv_sem=recv_sems.at[a, k], device_id=peers[k], device_id_type=MESH)

    def arrival(a, k):
        return pltpu.make_async_remote_copy(
            src_ref=_slab(ins[a], *peers[k]), dst_ref=_slab(outs[a], *peers[k]), send_sem=send_sems.at[a, k],
            recv_sem=recv_sems.at[a, k], device_id=peers[k], device_id_type=MESH)

    def start():
        for a in range(na):
            for k in range(N_DEV - 1):
                send(a, k).start()
        for a in range(na):
            _bounce(_slab(ins[a], *me), _slab(outs[a], *me), bufs[a], local_sems.at[a])

    def finish():
        for a in range(na):
            for k in range(N_DEV - 1):
                arrival(a, k).wait_recv()
        for a in range(na):
            for k in range(N_DEV - 1):
                send(a, k).wait_send()

    return start, finish


def _direct_scratch(parts):
    na = len(parts)
    return [pltpu.SemaphoreType.DMA((na, N_DEV - 1)), pltpu.SemaphoreType.DMA((na, N_DEV - 1)),
            pltpu.SemaphoreType.DMA((na,))] + [pltpu.VMEM(p.shape[1:], p.dtype) for p in parts]


ROW_TILES = D_MODEL // 128


def _rows3(t):
    return jnp.transpose(t[0]).reshape(t.shape[2], ROW_TILES, 128)


def _unrows3(t):
    return jnp.transpose(t.reshape(t.shape[0], D_MODEL))[None]


def _cast_shards(w_in3, w_att, w_ssm, w_o):
    def body(wi_ref, wa_ref, ws_ref, wo_ref, a_ref, b_ref, c_ref, d_ref):
        a_ref[...] = wi_ref[...].reshape(SHARD_IN // 2, 2 * ROW_TILES, 128).astype(BF16)
        b_ref[...] = wa_ref[...].astype(BF16)
        c_ref[...] = ws_ref[...].astype(BF16)
        d_ref[...] = wo_ref[...].astype(BF16)

    return pl.pallas_call(
        body, out_shape=[SDS((SHARD_IN // 2, 2 * ROW_TILES, 128), BF16), SDS(w_att.shape, BF16),
                         SDS(w_ssm.shape, BF16), SDS(w_o.shape, BF16)],
        compiler_params=_cparams(), name="cast_shards")(w_in3, w_att, w_ssm, w_o)


def _pieces():
    out = []
    for r0, c0, w in _SEGS:
        r = r0
        while r < r0 + w:
            d = r // SHARD_IN
            n = min(r0 + w, (d + 1) * SHARD_IN) - r
            out.append((c0 + (r - r0), d, r - d * SHARD_IN, n))
            r += n
    return out


def _to_aligned_t(slabs):
    def body(a_ref, o_ref):
        for (t, d, s, n) in _pieces():
            o_ref[t:t + n, :] = a_ref[d, s // 2:(s + n) // 2].reshape(n, D_MODEL)
        o_ref[C_DT + 32:C_DT + 128, :] = jnp.zeros((96, D_MODEL), slabs.dtype)

    return pl.pallas_call(body, out_shape=SDS((PW, D_MODEL), slabs.dtype), compiler_params=_cparams(),
                          name="to_aligned")(slabs)


def _from_aligned_pair(g):
    slab = (SHARD_IN // 2, 2 * ROW_TILES, 128)
    by_slab = [[p for p in _pieces() if p[1] == d] for d in range(N_DEV)]

    def body(g_ref, own_ref, got_ref, slabs, send_sems, recv_sems, local_sems):
        x, y, c = lax.axis_index("x"), lax.axis_index("y"), lax.axis_index("c")
        sibling = (x, y, 1 - c)

        def to_own(d, k):
            return pltpu.make_async_copy(slabs.at[d], own_ref.at[k], local_sems.at[k])

        def to_sibling(d, k):
            return pltpu.make_async_remote_copy(
                src_ref=slabs.at[d], dst_ref=got_ref.at[k], send_sem=send_sems.at[k], recv_sem=recv_sems.at[k],
                device_id=sibling, device_id_type=MESH)

        for d in range(N_DEV):
            for (t, _, s, n) in by_slab[d]:
                slabs[d, s // 2:(s + n) // 2] = g_ref[t:t + n, :].reshape(n // 2, 2 * ROW_TILES, 128)
            k, side = d // 2, d % 2
            pl.when(c == side)(to_own(d, k).start)
            pl.when(c != side)(to_sibling(d, k).start)
        for k in range(N_CHIP):
            to_own(0, k).wait()
            to_sibling(0, k).wait()

    half = SDS((N_CHIP,) + slab, g.dtype)
    return pl.pallas_call(
        body, in_specs=[pl.BlockSpec(memory_space=pltpu.VMEM)], out_specs=[ANY, ANY], out_shape=[half, half],
        scratch_shapes=[pltpu.VMEM((N_DEV,) + slab, g.dtype), pltpu.SemaphoreType.DMA((N_CHIP,)),
                        pltpu.SemaphoreType.DMA((N_CHIP,)), pltpu.SemaphoreType.DMA((N_CHIP,))],
        compiler_params=_cparams(), name="from_aligned_pair")(g)


_SEGS = [
    (R_Q, C_Q, 1024), (R_K, C_K, 256), (R_V, C_V, 256), (R_ZA, C_ZA, 1024), (R_ZS, C_ZS, 2048),
    (R_XBC, C_XBC, 3072), (R_DT, C_DT, 32), (R_GA, C_GA, 1024), (R_GS, C_GS, 1024)]


def _pad_lanes(v, n=128):
    return jnp.pad(v, ((0, 0), (0, n - v.shape[1])))


def _device_step(h, tgt, w_alt, w_out, g_pre, conv_w8, conv_b, dt_bias, a_log, d_skip, sinks, g_ssm, g_post, on_mesh):
    dtb, al, dsk, snk = _pad_lanes(dt_bias), _pad_lanes(a_log), _pad_lanes(d_skip), _pad_lanes(sinks)
    u = _norm_u(h, g_pre)
    proj = _matmul(u, w_alt, "nt", F32, T, 896, "in_proj")
    o = _attn_fwd(proj, snk)
    if on_mesh:
        sn, states, att_all, ssm_all, o_all = _ssd_fwd(proj, conv_w8, conv_b, dtb, al, dsk, g_ssm, gather=w_out)
        w_att = att_all.reshape(D_MODEL, D_MODEL)
        w_ssm = ssm_all.reshape(SSM_INNER, D_MODEL)
        w_o = o_all.reshape(D_MODEL, D_MODEL)
    else:
        sn, states = _ssd_fwd(proj, conv_w8, conv_b, dtb, al, dsk, g_ssm)
        w_att, w_ssm, w_o = w_out
    a_in, mg, ya, ys, out = _post_a(o, proj, sn, w_att, w_ssm, w_o)
    (loss, dres, dout, dya, dys, do, dproj, dsn, dgp) = _post_b(
        out, h, tgt, proj, ya, ys, o, g_post, w_att, w_ssm, w_o)
    dw_att = _matmul(a_in, dya, "tn", BF16, D_MODEL, D_MODEL, "d_w_att")
    dw_ssm = _matmul(sn, dys, "tn", BF16, D_MODEL, D_MODEL, "d_w_ssm")
    dw_o = _matmul(mg, dout, "tn", BF16, D_MODEL, D_MODEL, "d_w_o")
    res = {}
    if on_mesh:
        parts = [dw_att.reshape(N_DEV, 128, D_MODEL), dw_ssm.reshape(N_DEV, 256, D_MODEL),
                 dw_o.reshape(N_DEV, 128, D_MODEL)]
        (ddt4, dproj, ddtb, dal, ddsk, dgn, dcw, dcb, res["r_att"], res["r_ssm"], res["r_o"]) = _ssd_bwd(
            proj, conv_w8, conv_b, dtb, al, dsk, g_ssm, states, dsn, dproj, exchange=parts)
    else:
        ddt4, dproj, ddtb, dal, ddsk, dgn, dcw, dcb = _ssd_bwd(proj, conv_w8, conv_b, dtb, al, dsk, g_ssm, states,
                                                               dsn, dproj)
        res.update(dw_att=dw_att, dw_ssm=dw_ssm, dw_o=dw_o)
    dproj, dk, dv, dsink = _attn_bwd(proj, snk, do, dproj)
    dproj = _dproj_tail(dproj, dk, dv, ddt4)
    dw_alt = _matmul(dproj, u, "tn", BF16, 896, D_MODEL, "d_w_in")
    if on_mesh:
        own, got = _from_aligned_pair(dw_alt)
        dh, dgpre, res["r_in"] = _d_u_norm(dproj, w_alt, h, g_pre, dres,
                                           chips=_pair_sum([own], [got], "pair_sum_w_in"))
    else:
        dh, dgpre = _d_u_norm(dproj, w_alt, h, g_pre, dres)
        res["dw_alt"] = dw_alt
    small = (dgpre, dcb, ddtb, dal, ddsk, dsink, dgn, dgp, dcw)
    if on_mesh:
        res["small_pack"] = _small_pack(*small, loss, dh)
    else:
        res["small"] = small
    res.update(loss=loss[0, 0], dh=dh)
    return res


def kernel(x, meta_tokens, g_pre, w_in, conv_w, conv_b, dt_bias, a_log, d_skip, attn_sinks, g_ssm_norm, w_out_att, w_out_ssm, w_out, g_post, loss_target, m_meta_tokens, m_g_pre, m_w_in, m_conv_w, m_conv_b, m_dt_bias, m_a_log, m_d_skip, m_attn_sinks, m_g_ssm_norm, m_w_out_att, m_w_out_ssm, m_w_out, m_g_post, v_meta_tokens, v_g_pre, v_w_in, v_conv_w, v_conv_b, v_dt_bias, v_a_log, v_d_skip, v_attn_sinks, v_g_ssm_norm, v_w_out_att, v_w_out_ssm, v_w_out, v_g_post):
    w_in3, m_in3, v_in3 = _rows3(w_in), _rows3(m_w_in), _rows3(v_w_in)
    a_sh, att_sh, ssm_sh, o_sh = _cast_shards(w_in3, w_out_att[0], w_out_ssm[0], w_out[0])
    cw_sh = jnp.pad(conv_w[0], ((0, 4), (0, 0)))
    a_all, meta_all, cw_all = _all_gather([a_sh, meta_tokens, cw_sh])
    w_alt = _to_aligned_t(a_all)
    meta_full = meta_all.transpose(1, 0, 2).reshape(N_META, D_MODEL)
    conv_w8 = cw_all.transpose(1, 0, 2).reshape(8, CONV_DIM)

    h = jnp.concatenate([jnp.zeros((PAD, D_MODEL), F32), meta_full, x[0]], axis=0)
    tgt = jnp.concatenate([jnp.zeros((PAD + N_META, D_MODEL), F32), loss_target[0]], axis=0)
    r = _device_step(h, tgt, w_alt, (att_sh, ssm_sh, o_sh), g_pre, conv_w8, conv_b, dt_bias, a_log, d_skip,
                     attn_sinks, g_ssm_norm, g_post, True)
    grad_x = r["dh"][PAD + N_META:][None]

    *res_in, r_small = _sum_adamw_rows3(r["r_in"], w_in3, m_in3, v_in3, "adamw_w_in", exchange=[r["small_pack"]])
    res_in = [_unrows3(t) for t in res_in]
    res_att = [t[None] for t in _sum_adamw(r["r_att"], w_out_att[0], m_w_out_att[0], v_w_out_att[0], 512,
                                           "adamw_w_att")]
    res_ssm = [t[None] for t in _sum_adamw(r["r_ssm"], w_out_ssm[0], m_w_out_ssm[0], v_w_out_ssm[0], 512,
                                           "adamw_w_ssm")]
    res_o = [t[None] for t in _sum_adamw(r["r_o"], w_out[0], m_w_out[0], v_w_out[0], 512, "adamw_w_o")]
    (res_gpre, res_convb, res_dtb, res_alog, res_dskip, res_sink, res_gssm, res_gpost, res_cw, res_meta), loss = _small_finish(
        r_small, [(g_pre, m_g_pre, v_g_pre), (conv_b, m_conv_b, v_conv_b), (dt_bias, m_dt_bias, v_dt_bias),
                       (a_log, m_a_log, v_a_log), (d_skip, m_d_skip, v_d_skip),
                       (attn_sinks, m_attn_sinks, v_attn_sinks), (g_ssm_norm, m_g_ssm_norm, v_g_ssm_norm),
                       (g_post, m_g_post, v_g_post), (conv_w[0], m_conv_w[0], v_conv_w[0]),
                       (meta_tokens, m_meta_tokens, v_meta_tokens)])
    res_cw = [t[None] for t in res_cw]
    per_weight = [res_meta, res_gpre, res_in, res_cw, res_convb, res_dtb, res_alog, res_dskip, res_sink, res_gssm,
                  res_att, res_ssm, res_o, res_gpost]
    return (loss[0, 0], grad_x, *[p[0] for p in per_weight], *[p[1] for p in per_weight], *[p[2] for p in per_weight],
            *[p[3] for p in per_weight])
```

```python
import functools
import math

import jax
import jax.numpy as jnp
from jax import lax
from jax.experimental import pallas as pl
from jax.experimental.pallas import tpu as pltpu

F32 = jnp.float32
BF16 = jnp.bfloat16
SDS = jax.ShapeDtypeStruct
MESH = pl.DeviceIdType.MESH
ANY = pl.BlockSpec(memory_space=pl.ANY)

N_DEV = 8
D_MODEL = 1024
SEQ = 2048
N_META = 16
BLK = 128
PAD = 112
T = PAD + N_META + SEQ
NB = T // BLK
EPS = 1e-6
HEAD = 64
Q_HEADS = 16
KV_HEADS = 4
GROUP = 4
KV_W = 256
SSM_INNER = 2048
SSM_HEADS = 32
SSM_GROUPS = 4
GRP_W = 512
SSM_STATE = 128
CONV_DIM = 3072
IN_PROJ = 9760
SHARD_IN = IN_PROJ // N_DEV
NEG = -1e30

C_ZS, C_XBC, C_Q, C_ZA, C_GA, C_GS, C_K, C_V, C_DT = 0, 2048, 5120, 6144, 7168, 8192, 9216, 9472, 9728
PW = 9856
GATES_W = 3 * D_MODEL
R_Q, R_K, R_V, R_ZA, R_ZS, R_XBC, R_DT, R_GA, R_GS = 0, 1024, 1280, 1536, 2560, 4608, 7680, 7712, 8736

ADAM_LR, ADAM_B1, ADAM_B2, ADAM_EPS, ADAM_WD, ADAM_STEP = 0.001, 0.9, 0.999, 1e-08, 0.01, 10

VMEM_LIMIT = 56 * 1024 * 1024


def _cparams():
    return pltpu.CompilerParams(vmem_limit_bytes=VMEM_LIMIT)


def _silu(x):
    return x * jax.nn.sigmoid(x)


def _dsilu(x):
    s = jax.nn.sigmoid(x)
    return s * (1.0 + x * (1.0 - s))


def _matmul(a, b, mode, out_dtype, tm, tn, name):
    if mode == "nt":
        (m, k), n = a.shape, b.shape[0]
        a_spec = pl.BlockSpec((tm, k), lambda i, j: (i, 0))
        b_spec = pl.BlockSpec((tn, k), lambda i, j: (j, 0))
        dims = (((1,), (1,)), ((), ()))
    else:
        assert mode == "tn"
        (k, m), n = a.shape, b.shape[1]
        a_spec = pl.BlockSpec((k, tm), lambda i, j: (0, i))
        b_spec = pl.BlockSpec((k, tn), lambda i, j: (0, j))
        dims = (((0,), (0,)), ((), ()))
    assert m % tm == 0 and n % tn == 0, (a.shape, b.shape, tm, tn)

    def body(a_ref, b_ref, o_ref):
        o_ref[...] = lax.dot_general(a_ref[...], b_ref[...], dims, preferred_element_type=F32).astype(out_dtype)

    return pl.pallas_call(
        body, grid=(m // tm, n // tn), in_specs=[a_spec, b_spec],
        out_specs=pl.BlockSpec((tm, tn), lambda i, j: (i, j)), out_shape=SDS((m, n), out_dtype),
        compiler_params=_cparams(), name=name)(a, b)


def _norm_u(h, g_pre):
    def body(h_ref, g_ref, u_ref):
        x = h_ref[...]
        r = lax.rsqrt(jnp.mean(x * x, axis=-1, keepdims=True) + EPS)
        u_ref[...] = (x * r * g_ref[...]).astype(BF16)

    return pl.pallas_call(
        body, grid=(NB,),
        in_specs=[pl.BlockSpec((BLK, D_MODEL), lambda i: (i, 0)), pl.BlockSpec((1, D_MODEL), lambda i: (0, 0))],
        out_specs=pl.BlockSpec((BLK, D_MODEL), lambda i: (i, 0)),
        out_shape=SDS((T, D_MODEL), BF16), name="norm_u")(h, g_pre)


DU_TM, DU_TK = T // 2, 1408


def _d_u_norm(dproj, w_alt, h, g_pre, dres, chips=()):
    nk = PW // DU_TK
    ni = T // DU_TM
    nc = len(chips)

    def body(*refs):
        a_ref, b_ref, h_ref, g_ref, dres_ref = refs[:5]
        dh_ref, dg_ref = refs[5 + nc:7 + nc]
        acc_ref = refs[7 + 2 * nc]
        i, kk = pl.program_id(0), pl.program_id(1)
        if nc:
            ch_start, ch_finish = _chips_program(refs[5:5 + nc], refs[7 + nc:7 + 2 * nc], refs[8 + 2 * nc:])
            pl.when((i == 0) & (kk == 0))(ch_start)
        part = jnp.dot(a_ref[...], b_ref[...], preferred_element_type=F32)

        @pl.when(kk == 0)
        def _():
            acc_ref[...] = part

        @pl.when((kk > 0) & (kk < nk - 1))
        def _():
            acc_ref[...] += part

        @pl.when(kk == nk - 1)
        def _():
            du_ = acc_ref[...] + part
            x = h_ref[...]
            r = lax.rsqrt(jnp.mean(x * x, axis=-1, keepdims=True) + EPS)
            gd = g_ref[...] * du_
            dx = r * gd - x * (r * r * r) * jnp.mean(x * gd, axis=-1, keepdims=True)
            dh_ref[...] = dx + dres_ref[...]
            gpart = jnp.concatenate([jnp.sum(du_ * x * r, axis=0, keepdims=True), jnp.zeros((7, D_MODEL), F32)],
                                    axis=0)

            @pl.when(i == 0)
            def _():
                dg_ref[...] = gpart

            @pl.when(i > 0)
            def _():
                dg_ref[...] += gpart

        if nc:
            pl.when((i == ni - 1) & (kk == nk - 1))(ch_finish)

    row = pl.BlockSpec((DU_TM, D_MODEL), lambda i, kk: (i, 0))
    return pl.pallas_call(
        body, grid=(ni, nk),
        in_specs=[pl.BlockSpec((DU_TM, DU_TK), lambda i, kk: (i, kk)),
                  pl.BlockSpec((DU_TK, D_MODEL), lambda i, kk: (kk, 0)),
                  row, pl.BlockSpec((1, D_MODEL), lambda i, kk: (0, 0)), row] + [ANY] * nc,
        out_specs=[row, pl.BlockSpec((8, D_MODEL), lambda i, kk: (0, 0))] + [ANY] * nc,
        out_shape=[SDS((T, D_MODEL), F32), SDS((8, D_MODEL), F32)] + [SDS(p.shape, p.dtype) for p in chips],
        scratch_shapes=[pltpu.VMEM((DU_TM, D_MODEL), F32)] + (_chips_scratch(chips) if nc else []),
        compiler_params=_cparams(), name="d_u_norm")(dproj, w_alt, h, g_pre, dres, *chips)


def _lane_pick(row, h):
    lane = lax.broadcasted_iota(jnp.int32, row.shape, 1)
    return jnp.sum(jnp.where(lane == h, row, 0.0), axis=1, keepdims=True)


def _attn_fn(q4s, kcats, vcats, kms, vms, sinks, n):
    r = lax.broadcasted_iota(jnp.int32, (GROUP * BLK, 2 * BLK), 0)
    s = lax.broadcasted_iota(jnp.int32, (GROUP * BLK, 2 * BLK), 1)
    i = jnp.bitwise_and(r, BLK - 1)
    gi = jnp.right_shift(r, 7)
    rel = i - s + BLK
    k_pos = n * BLK - BLK + s
    band_ok = (rel >= 0) & (rel < BLK) & (k_pos >= PAD + N_META)
    relf = rel.astype(F32)
    rm = lax.broadcasted_iota(jnp.int32, (GROUP * BLK, N_META), 0)
    mm = lax.broadcasted_iota(jnp.int32, (GROUP * BLK, N_META), 1)
    meta_ok = (PAD + mm) <= (n * BLK + jnp.bitwise_and(rm, BLK - 1))
    gcol = jnp.right_shift(lax.broadcasted_iota(jnp.int32, (GROUP * BLK, 1), 0), 7)
    outs = []
    for kh in range(KV_HEADS):
        slopes = [2.0 ** (-8.0 * (kh * GROUP + g + 1) / Q_HEADS) for g in range(GROUP)]
        slope = jnp.where(gi == 0, slopes[0], jnp.where(gi == 1, slopes[1], jnp.where(gi == 2, slopes[2], slopes[3])))
        sk = [_lane_pick(sinks, kh * GROUP + g) for g in range(GROUP)]
        sink = jnp.where(gcol == 0, sk[0], jnp.where(gcol == 1, sk[1], jnp.where(gcol == 2, sk[2], sk[3])))
        qb = (q4s[kh] * (HEAD ** -0.5)).astype(BF16)
        sb = lax.dot_general(qb, kcats[kh].astype(BF16), (((1,), (1,)), ((), ())), preferred_element_type=F32)
        sb = jnp.where(band_ok, sb - slope * relf, NEG)
        sm = lax.dot_general(qb, kms[kh].astype(BF16), (((1,), (1,)), ((), ())), preferred_element_type=F32)
        sm = jnp.where(meta_ok, sm, NEG)
        mx = jnp.maximum(jnp.maximum(jnp.max(sb, axis=1, keepdims=True), jnp.max(sm, axis=1, keepdims=True)), sink)
        mx = lax.stop_gradient(mx)
        eb = jnp.exp(sb - mx)
        em = jnp.exp(sm - mx)
        es = jnp.exp(sink - mx)
        inv = 1.0 / (jnp.sum(eb, axis=1, keepdims=True) + jnp.sum(em, axis=1, keepdims=True) + es)
        pb = (eb * inv).astype(BF16)
        pm = (em * inv).astype(BF16)
        o4 = (jnp.dot(pm, vms[kh].astype(BF16), preferred_element_type=F32)
              + jnp.dot(pb, vcats[kh].astype(BF16), preferred_element_type=F32))
        outs.append(o4)
    return outs


def _attn_specs():
    prev = lambda n: jnp.maximum(n - 1, 0)
    return [
        pl.BlockSpec((BLK, D_MODEL), lambda n: (n, C_Q // D_MODEL)),
        pl.BlockSpec((BLK, KV_W), lambda n: (prev(n), C_K // KV_W)),
        pl.BlockSpec((BLK, KV_W), lambda n: (n, C_K // KV_W)),
        pl.BlockSpec((BLK, KV_W), lambda n: (prev(n), C_V // KV_W)),
        pl.BlockSpec((BLK, KV_W), lambda n: (n, C_V // KV_W)),
        pl.BlockSpec((N_META, KV_W), lambda n: (PAD // N_META, C_K // KV_W)),
        pl.BlockSpec((N_META, KV_W), lambda n: (PAD // N_META, C_V // KV_W)),
        pl.BlockSpec((1, 128), lambda n: (0, 0)),
    ]


def _attn_load(q_ref, kp_ref, kc_ref, vp_ref, vc_ref, km_ref, vm_ref):
    q4s, kcats, vcats, kms, vms = [], [], [], [], []
    for kh in range(KV_HEADS):
        q4s.append(jnp.concatenate(
            [q_ref[:, (kh * GROUP + g) * HEAD:(kh * GROUP + g + 1) * HEAD] for g in range(GROUP)], axis=0))
        cs = slice(kh * HEAD, (kh + 1) * HEAD)
        kcats.append(jnp.concatenate([kp_ref[:, cs], kc_ref[:, cs]], axis=0))
        vcats.append(jnp.concatenate([vp_ref[:, cs], vc_ref[:, cs]], axis=0))
        kms.append(km_ref[:, cs])
        vms.append(vm_ref[:, cs])
    return q4s, kcats, vcats, kms, vms


def _attn_fwd(proj, sinks):
    def body(q_ref, kp_ref, kc_ref, vp_ref, vc_ref, km_ref, vm_ref, s_ref, o_ref):
        n = pl.program_id(0)
        args = _attn_load(q_ref, kp_ref, kc_ref, vp_ref, vc_ref, km_ref, vm_ref)
        outs = _attn_fn(*args, s_ref[...], n)
        for kh in range(KV_HEADS):
            for g in range(GROUP):
                hh = kh * GROUP + g
                o_ref[:, hh * HEAD:(hh + 1) * HEAD] = outs[kh][g * BLK:(g + 1) * BLK]

    return pl.pallas_call(
        body, grid=(NB,), in_specs=_attn_specs(),
        out_specs=pl.BlockSpec((BLK, D_MODEL), lambda n: (n, 0)),
        out_shape=SDS((T, D_MODEL), F32), name="attn_fwd")(proj, proj, proj, proj, proj, proj, proj, sinks)


def _attn_bwd(proj, sinks, do, dproj):
    def body(q_ref, kp_ref, kc_ref, vp_ref, vc_ref, km_ref, vm_ref, s_ref, do_ref, _, dq_ref, dk_ref, dv_ref, ds_ref):
        n = pl.program_id(0)

        @pl.when(n == 0)
        def _():
            dk_ref[...] = jnp.zeros_like(dk_ref)
            dv_ref[...] = jnp.zeros_like(dv_ref)
            ds_ref[...] = jnp.zeros_like(ds_ref)

        args = _attn_load(q_ref, kp_ref, kc_ref, vp_ref, vc_ref, km_ref, vm_ref)
        _, vjp = jax.vjp(lambda a, b, c, d, e, f: _attn_fn(a, b, c, d, e, f, n), *args, s_ref[...])
        do_f = do_ref[...].astype(F32)
        cot = [jnp.concatenate([do_f[:, (kh * GROUP + g) * HEAD:(kh * GROUP + g + 1) * HEAD] for g in range(GROUP)],
                               axis=0) for kh in range(KV_HEADS)]
        dq4s, dkcats, dvcats, dkms, dvms, dsk = vjp(cot)
        ds_ref[0:1, :] += dsk
        cur = pl.ds(pl.multiple_of(n * BLK, BLK), BLK)
        meta = slice(PAD, PAD + N_META)
        for kh in range(KV_HEADS):
            cs = slice(kh * HEAD, (kh + 1) * HEAD)
            for g in range(GROUP):
                hh = kh * GROUP + g
                dq_ref[:, hh * HEAD:(hh + 1) * HEAD] = dq4s[kh][g * BLK:(g + 1) * BLK].astype(BF16)
            dk_ref[cur, cs] += dkcats[kh][BLK:]
            dv_ref[cur, cs] += dvcats[kh][BLK:]
            dk_ref[meta, cs] += dkms[kh]
            dv_ref[meta, cs] += dvms[kh]

        @pl.when(n > 0)
        def _():
            prv = pl.ds(pl.multiple_of((n - 1) * BLK, BLK), BLK)
            for kh in range(KV_HEADS):
                cs = slice(kh * HEAD, (kh + 1) * HEAD)
                dk_ref[prv, cs] += dkcats[kh][:BLK]
                dv_ref[prv, cs] += dvcats[kh][:BLK]

    full_kv = pl.BlockSpec((T, KV_W), lambda n: (0, 0))
    return pl.pallas_call(
        body, grid=(NB,),
        in_specs=_attn_specs() + [pl.BlockSpec((BLK, D_MODEL), lambda n: (n, 0)), ANY],
        out_specs=[pl.BlockSpec((BLK, D_MODEL), lambda n: (n, C_Q // D_MODEL)), full_kv, full_kv,
                   pl.BlockSpec((8, 128), lambda n: (0, 0))],
        out_shape=[SDS((T, PW), BF16), SDS((T, KV_W), F32), SDS((T, KV_W), F32), SDS((8, 128), F32)],
        input_output_aliases={9: 0},
        name="attn_bwd")(proj, proj, proj, proj, proj, proj, proj, sinks, do, dproj)


def _conv_taps(xp, w, rows):
    return (w[0:1] * xp[5:5 + rows] + w[1:2] * xp[6:6 + rows] + w[2:3] * xp[7:7 + rows] + w[3:4] * xp[8:8 + rows])


HPG = SSM_HEADS // SSM_GROUPS


def _iota(shape, dim):
    return lax.broadcasted_iota(jnp.int32, shape, dim)


def _mm(a, b, ca=1, cb=0):
    return lax.dot_general(a.astype(BF16), b.astype(BF16), (((ca,), (cb,)), ((), ())), preferred_element_type=F32)


def _split3(v):
    hi = v.astype(BF16)
    r1 = v - hi.astype(F32)
    mid = r1.astype(BF16)
    lo = (r1 - mid.astype(F32)).astype(BF16)
    return hi, mid, lo


def _split2(v):
    hi = v.astype(BF16)
    return hi, (v - hi.astype(F32)).astype(BF16)


def _sel_r(parts, onehot, ca=1, cb=0):
    out = lax.dot_general(parts[0], onehot, (((ca,), (cb,)), ((), ())), preferred_element_type=F32)
    for p in parts[1:]:
        out = out + lax.dot_general(p, onehot, (((ca,), (cb,)), ((), ())), preferred_element_type=F32)
    return out


def _sel_l(onehot, parts):
    out = jnp.dot(onehot, parts[0], preferred_element_type=F32)
    for p in parts[1:]:
        out = out + jnp.dot(onehot, p, preferred_element_type=F32)
    return out


def _rows8(*rows):
    r = _iota((8, rows[0].shape[1]), 0)
    out = jnp.zeros((8, rows[0].shape[1]), F32)
    for k, v in enumerate(rows):
        out = jnp.where(r == k, v, out)
    return out


def _ssd_forward(x, z, bm, cm, dt_raw, st_prev, dtb, alog, dskip, gn, g, cst_scr):
    li, si = _iota((BLK, BLK), 0), _iota((BLK, BLK), 1)
    dt_all = jax.nn.softplus(dt_raw + dtb)
    a_row = -jnp.exp(alog)
    a_all = dt_all * a_row
    cs_all = _sel_l((li >= si).astype(BF16), _split3(a_all))
    cs_parts = _split3(cs_all)
    spread = (_iota((BLK, GRP_W), 0) == g * HPG + jnp.right_shift(_iota((BLK, GRP_W), 1), 6)).astype(BF16)
    dt_e = _sel_r(_split2(dt_all), spread)
    cs_e = _sel_r(cs_parts, spread)
    d_e = _sel_r(_split2(_rows8(dskip)), spread)[0:1]
    cs_last_e = jnp.sum(jnp.where(_iota((BLK, GRP_W), 0) == BLK - 1, cs_e, 0.0), axis=0, keepdims=True)
    p_e = jnp.exp(cs_e)
    w_e = jnp.exp(cs_last_e - cs_e)
    cd_e = jnp.exp(cs_last_e)
    xr = x * dt_e
    cst_scr[...] = cs_all.T
    cst_g = cst_scr[g * HPG:(g + 1) * HPG, :]
    own = jnp.right_shift(_iota((HPG, HPG * BLK), 1), 7) == _iota((HPG, HPG * BLK), 0)
    ownf = own.astype(F32)
    q_rows = [ownf, ownf, ownf] + [jnp.where(own, jnp.concatenate([p.astype(F32)] * HPG, axis=1), 0.0)
                                   for p in _split3(cst_g)]
    q2 = jnp.concatenate(q_rows + [jnp.zeros((BLK - 6 * HPG, HPG * BLK), F32)], axis=0).astype(BF16)
    lane1 = _iota((1, BLK), 1)
    p2 = jnp.where((lane1 >= 3 * HPG) & (lane1 < 6 * HPG), -1.0, 0.0)
    for k, part in enumerate(cs_parts):
        pick = ((li == g * HPG + si - k * HPG) & (si >= k * HPG) & (si < (k + 1) * HPG)).astype(BF16)
        p2 = p2 + jnp.dot(part, pick, preferred_element_type=F32)
    dmat = jnp.dot(p2.astype(BF16), q2, preferred_element_type=F32)
    causal = _iota((BLK, HPG * BLK), 0) >= jnp.bitwise_and(_iota((BLK, HPG * BLK), 1), BLK - 1)
    lam = jnp.exp(jnp.where(causal, dmat, NEG))
    gmat = _mm(cm, bm, 1, 1)
    m_all = lam * jnp.concatenate([gmat] * HPG, axis=1)
    mb = m_all.astype(BF16)
    lo = _iota((BLK, BLK), 1) < HEAD
    xrb = xr.astype(BF16)
    zero = jnp.zeros((BLK, BLK), BF16)
    bds, yd = [], []
    for i in range(HPG // 2):
        t = xrb[:, BLK * i:BLK * (i + 1)]
        bd = jnp.concatenate([jnp.where(lo, t, zero), jnp.where(lo, zero, t)], axis=0)
        bds.append(bd)
        yd.append(jnp.dot(mb[:, 2 * BLK * i:2 * BLK * (i + 1)], bd, preferred_element_type=F32))
    cs_st = _mm(cm, st_prev)
    y = jnp.concatenate(yd, axis=1) + cs_st * p_e + d_e * x
    xrw = xr * w_e
    st_new = cd_e * st_prev + _mm(bm, xrw, 0, 0)
    yz = y * _silu(z)
    rn = lax.rsqrt(jnp.sum(yz * yz, axis=1, keepdims=True) / GRP_W + EPS)
    return dict(out=yz * rn * gn, st_new=st_new, dt_all=dt_all, a_row=a_row, dt_e=dt_e, d_e=d_e, p_e=p_e, w_e=w_e,
                cd_e=cd_e, xr=xr, xrw=xrw, lam=lam, m_all=m_all, mb=mb, bds=bds, cs_st=cs_st, y=y, yz=yz, rn=rn, lo=lo)


def _ssd_backward(f, x, z, bm, cm, dt_raw, st_prev, dtb, gn, g, dout, dst_next, cst_scr):
    li, si = _iota((BLK, BLK), 0), _iota((BLK, BLK), 1)
    yz, rn, y, p_e, w_e, cd_e, xr = f["yz"], f["rn"], f["y"], f["p_e"], f["w_e"], f["cd_e"], f["xr"]
    dgn = jnp.sum(dout * yz * rn, axis=0, keepdims=True)
    t = dout * gn
    dyz = rn * t - yz * (rn * rn * rn) * (jnp.sum(yz * t, axis=1, keepdims=True) / GRP_W)
    dy = dyz * _silu(z)
    dz = dyz * y * _dsilu(z)
    dx = f["d_e"] * dy
    dd_e = jnp.sum(dy * x, axis=0, keepdims=True)
    dcsst = dy * p_e
    dp_e = dy * f["cs_st"]
    dcm = _mm(dcsst, st_prev, 1, 1)
    dst_prev = _mm(cm, dcsst, 0, 0) + cd_e * dst_next
    dcd_e = jnp.sum(dst_next * st_prev, axis=0, keepdims=True)
    dbm = _mm(f["xrw"], dst_next, 1, 1)
    dxrw = _mm(bm, dst_next)
    dxr = dxrw * w_e
    dw_e = dxrw * xr
    dyb = dy.astype(BF16)
    dms, dxr_d = [], []
    for i in range(HPG // 2):
        dyp = dyb[:, BLK * i:BLK * (i + 1)]
        dms.append(lax.dot_general(dyp, f["bds"][i], (((1,), (1,)), ((), ())), preferred_element_type=F32))
        r = lax.dot_general(f["mb"][:, 2 * BLK * i:2 * BLK * (i + 1)], dyp, (((0,), (0,)), ((), ())),
                            preferred_element_type=F32)
        dxr_d.append(jnp.where(f["lo"], r[0:BLK], r[BLK:2 * BLK]))
    dm_all = jnp.concatenate(dms, axis=1)
    dxr = dxr + jnp.concatenate(dxr_d, axis=1)
    dlg = dm_all * f["lam"]
    dg = dlg[:, 0:BLK]
    for j in range(1, HPG):
        dg = dg + dlg[:, BLK * j:BLK * (j + 1)]
    dcm = dcm + _mm(dg, bm)
    dbm = dbm + _mm(dg, cm, 0, 0)
    q_all = dm_all * f["m_all"]
    col_sums = jnp.sum(q_all, axis=0, keepdims=True)
    cst_scr[...] = jnp.zeros_like(cst_scr)
    cst_scr[g * HPG:(g + 1) * HPG, :] = _rows8(
        *[col_sums[:, BLK * j:BLK * (j + 1)] for j in range(HPG)])
    dcs = -cst_scr[...].T
    for j in range(HPG):
        dcs = dcs + jnp.where(si == g * HPG + j,
                              jnp.sum(q_all[:, BLK * j:BLK * (j + 1)], axis=1, keepdims=True), 0.0)
    unspread = (_iota((GRP_W, BLK), 1) == g * HPG + jnp.right_shift(_iota((GRP_W, BLK), 0), 6)).astype(BF16)
    dww = dw_e * w_e
    per_head = _sel_r(_split2(jnp.concatenate([dp_e * p_e - dww, dxr * x], axis=0)), unspread)
    last = _sel_r(_split2(_rows8(jnp.sum(dww, axis=0, keepdims=True) + dcd_e * cd_e, dd_e)), unspread)
    dcs = dcs + per_head[0:BLK] + jnp.where(li == BLK - 1, last[0:1], 0.0)
    da = _sel_l((si >= li).astype(BF16), _split2(dcs))
    ddt_all = da * f["a_row"] + per_head[BLK:2 * BLK]
    dalog = jnp.sum(da * f["dt_all"], axis=0, keepdims=True) * f["a_row"]
    dx = dx + dxr * f["dt_e"]
    ddt_raw = ddt_all * jax.nn.sigmoid(dt_raw + dtb)
    ddtb = jnp.sum(ddt_raw, axis=0, keepdims=True)
    ddskip = last[1:2]
    return dict(dx=dx, dz=dz, dbm=dbm, dcm=dcm, ddt_raw=ddt_raw, dst_prev=dst_prev, ddtb=ddtb, dalog=dalog,
                ddskip=ddskip, dgn=dgn)


ZX_W = SSM_INNER + CONV_DIM
assert C_ZS == 0 and C_XBC == SSM_INNER


def _ssd_in_specs(rev):
    cidx = (lambda c: NB - 1 - c) if rev else (lambda c: c)
    return [
        pl.BlockSpec((BLK, ZX_W), lambda c: (cidx(c), 0)),
        pl.BlockSpec((8, ZX_W), lambda c: (jnp.maximum(cidx(c) * (BLK // 8) - 1, 0), 0)),
        pl.BlockSpec((BLK, 128), lambda c: (cidx(c), C_DT // 128)),
        pl.BlockSpec((8, CONV_DIM), lambda c: (0, 0)),
        pl.BlockSpec((1, CONV_DIM), lambda c: (0, 0)),
        pl.BlockSpec((1, 128), lambda c: (0, 0)),
        pl.BlockSpec((1, 128), lambda c: (0, 0)),
        pl.BlockSpec((1, 128), lambda c: (0, 0)),
        pl.BlockSpec((1, SSM_INNER), lambda c: (0, 0)),
    ]


def _xbc_act(zx_ref, tail_ref, w_ref, b_ref, n):
    tail = jnp.where(n > 0, tail_ref[:, SSM_INNER:], 0.0)
    xp = jnp.concatenate([tail, zx_ref[:, SSM_INNER:]], axis=0)
    conv = _conv_taps(xp, w_ref[...], BLK) + b_ref[...]
    valid = n * BLK + _iota((BLK, 1), 0) >= PAD
    return xp, conv, valid, jnp.where(valid, _silu(conv), 0.0)


def _grp_cols(act, i):
    b0, c0 = SSM_INNER + i * SSM_STATE, SSM_INNER + (SSM_GROUPS + i) * SSM_STATE
    return act[:, i * GRP_W:(i + 1) * GRP_W], act[:, b0:b0 + SSM_STATE], act[:, c0:c0 + SSM_STATE]


def _ssd_fwd(proj, conv_w, conv_b, dt_bias, a_log, d_skip, g_norm, gather=()):
    ng = len(gather)

    def body(*refs):
        zx_ref, tail_ref, dt_ref, w_ref, b_ref, dtb_ref, al_ref, dsk_ref, gn_ref = refs[:9]
        y_ref, st_ref = refs[9 + ng:11 + ng]
        s_scr, cst_scr = refs[11 + 2 * ng:13 + 2 * ng]
        c = pl.program_id(0)
        if ng:
            ag_start, ag_forward, ag_finish = _ag_program(refs[9:9 + ng], refs[11 + ng:11 + 2 * ng],
                                                          refs[13 + 2 * ng:])
            pl.when(c == 0)(ag_start)
            pl.when(c == (3 * NB) // 4)(ag_forward)

        @pl.when(c == 0)
        def _():
            s_scr[...] = jnp.zeros_like(s_scr)

        _, _, _, act = _xbc_act(zx_ref, tail_ref, w_ref, b_ref, c)
        for i in range(SSM_GROUPS):
            st_prev = s_scr[i]
            st_ref[i, 0] = st_prev
            x, bm, cm = _grp_cols(act, i)
            f = _ssd_forward(x, zx_ref[:, i * GRP_W:(i + 1) * GRP_W], bm, cm, dt_ref[...], st_prev, dtb_ref[...],
                             al_ref[...], dsk_ref[...], gn_ref[:, i * GRP_W:(i + 1) * GRP_W], i, cst_scr.at[i])
            y_ref[:, i * GRP_W:(i + 1) * GRP_W] = f["out"].astype(BF16)
            s_scr[i] = f["st_new"]
        if ng:
            pl.when(c == NB - 1)(ag_finish)

    return pl.pallas_call(
        body, grid=(NB,), in_specs=_ssd_in_specs(False) + [ANY] * ng,
        out_specs=[pl.BlockSpec((BLK, SSM_INNER), lambda c: (c, 0)),
                   pl.BlockSpec((SSM_GROUPS, 1, SSM_STATE, GRP_W), lambda c: (0, c, 0, 0))] + [ANY] * ng,
        out_shape=[SDS((T, SSM_INNER), BF16), SDS((SSM_GROUPS, NB, SSM_STATE, GRP_W), F32)]
        + [SDS((N_DEV,) + s.shape, s.dtype) for s in gather],
        scratch_shapes=[pltpu.VMEM((SSM_GROUPS, SSM_STATE, GRP_W), F32), pltpu.VMEM((SSM_GROUPS, BLK, BLK), F32)]
        + (_ag_scratch(gather) if ng else []),
        compiler_params=_cparams(),
        name="ssd_fwd")(proj, proj, proj, conv_w, conv_b, dt_bias, a_log, d_skip, g_norm, *gather)


def _ssd_bwd(proj, conv_w, conv_b, dt_bias, a_log, d_skip, g_norm, states, dy, dproj, exchange=()):
    ne = len(exchange)

    def body(*refs):
        zx_ref, tail_ref, dt_ref, w_ref, b_ref, dtb_ref, al_ref, dsk_ref, gn_ref, st_ref, dy_ref = refs[:11]
        (ddt_ref, dp_ref, ddtb_ref, dal_ref, ddsk_ref, dgn_ref, dcw_ref, dcb_ref) = refs[12 + ne:20 + ne]
        ds_scr, cst_scr, carry = refs[20 + 2 * ne:23 + 2 * ne]
        c = pl.program_id(0)
        n = NB - 1 - c
        if ne:
            ex_start, ex_finish = _direct_program(refs[12:12 + ne], refs[20 + ne:20 + 2 * ne], refs[23 + 2 * ne:])
            pl.when(c == 0)(ex_start)

        @pl.when(c == 0)
        def _():
            for ref in (ds_scr, carry, dgn_ref, ddtb_ref, dal_ref, ddsk_ref, dcw_ref, dcb_ref):
                ref[...] = jnp.zeros_like(ref)

        xp, conv, valid, act = _xbc_act(zx_ref, tail_ref, w_ref, b_ref, n)
        dt_raw = dt_ref[...]
        dxs, dbs, dcs = [], [], []
        for i in range(SSM_GROUPS):
            x, bm, cm = _grp_cols(act, i)
            z, gn, st_prev = zx_ref[:, i * GRP_W:(i + 1) * GRP_W], gn_ref[:, i * GRP_W:(i + 1) * GRP_W], st_ref[i, 0]
            f = _ssd_forward(x, z, bm, cm, dt_raw, st_prev, dtb_ref[...], al_ref[...], dsk_ref[...], gn, i,
                             cst_scr.at[i])
            d = _ssd_backward(f, x, z, bm, cm, dt_raw, st_prev, dtb_ref[...], gn, i,
                              dy_ref[:, i * GRP_W:(i + 1) * GRP_W].astype(F32), ds_scr[i], cst_scr.at[i])
            dxs.append(d["dx"])
            dbs.append(d["dbm"])
            dcs.append(d["dcm"])
            dp_ref[:, i * GRP_W:(i + 1) * GRP_W] = d["dz"].astype(BF16)
            ds_scr[i] = d["dst_prev"]
            ddt_ref[:, i * 128:(i + 1) * 128] = d["ddt_raw"]
            dgn_ref[0:1, i * GRP_W:(i + 1) * GRP_W] += d["dgn"]
            ddtb_ref[0:1, :] += d["ddtb"]
            dal_ref[0:1, :] += d["dalog"]
            ddsk_ref[0:1, :] += d["ddskip"]
        dconv = jnp.where(valid, jnp.concatenate(dxs + dbs + dcs, axis=1) * _dsilu(conv), 0.0)
        dext = jnp.concatenate([dconv, carry[...]], axis=0)
        w = w_ref[...]
        dp_ref[:, SSM_INNER:] = (w[0:1] * dext[3:3 + BLK] + w[1:2] * dext[2:2 + BLK] + w[2:3] * dext[1:1 + BLK]
                                 + w[3:4] * dext[0:BLK]).astype(BF16)
        carry[...] = dconv[0:8]
        dcw_ref[...] += jnp.concatenate(
            [jnp.sum(dconv * xp[5 + k:5 + k + BLK], axis=0, keepdims=True) for k in range(4)]
            + [jnp.zeros((4, CONV_DIM), F32)], axis=0)
        dcb_ref[0:1, :] += jnp.sum(dconv, axis=0, keepdims=True)
        if ne:
            pl.when(c == NB - 1)(ex_finish)

    rc = lambda c: NB - 1 - c
    small = pl.BlockSpec((8, 128), lambda c: (0, 0))
    wide = lambda w: pl.BlockSpec((8, w), lambda c: (0, 0))
    return pl.pallas_call(
        body, grid=(NB,),
        in_specs=_ssd_in_specs(True) + [
            pl.BlockSpec((SSM_GROUPS, 1, SSM_STATE, GRP_W), lambda c: (0, rc(c), 0, 0)),
            pl.BlockSpec((BLK, SSM_INNER), lambda c: (rc(c), 0)), ANY] + [ANY] * ne,
        out_specs=[pl.BlockSpec((BLK, SSM_GROUPS * 128), lambda c: (rc(c), 0)),
                   pl.BlockSpec((BLK, ZX_W), lambda c: (rc(c), 0)),
                   small, small, small, wide(SSM_INNER), wide(CONV_DIM), wide(CONV_DIM)] + [ANY] * ne,
        out_shape=[SDS((T, GRP_W), F32), SDS((T, PW), BF16), SDS((8, 128), F32), SDS((8, 128), F32),
                   SDS((8, 128), F32), SDS((8, SSM_INNER), F32), SDS((8, CONV_DIM), F32), SDS((8, CONV_DIM), F32)]
        + [SDS(p.shape, p.dtype) for p in exchange],
        scratch_shapes=[pltpu.VMEM((SSM_GROUPS, SSM_STATE, GRP_W), F32), pltpu.VMEM((SSM_GROUPS, BLK, BLK), F32),
                        pltpu.VMEM((8, CONV_DIM), F32)] + (_direct_scratch(exchange) if ne else []),
        input_output_aliases={11: 1},
        compiler_params=_cparams(),
        name="ssd_bwd")(proj, proj, proj, conv_w, conv_b, dt_bias, a_log, d_skip, g_norm, states, dy, dproj,
                        *exchange)


POST_R = 272


def _post_a(o, proj, sn, w_att, w_ssm, w_o):
    def body(o_ref, za_ref, ga_ref, gs_ref, sn_ref, wa_ref, ws_ref, wo_ref, a_ref, mg_ref, ya_ref, ys_ref, out_ref):
        a = (o_ref[...] * _silu(za_ref[...])).astype(BF16)
        a_ref[...] = a
        ya = jnp.dot(a, wa_ref[...], preferred_element_type=F32)
        ys = jnp.dot(sn_ref[...], ws_ref[...], preferred_element_type=F32)
        ya_ref[...] = ya.astype(BF16)
        ys_ref[...] = ys.astype(BF16)
        mg = (jax.nn.sigmoid(ga_ref[...]) * ya + jax.nn.sigmoid(gs_ref[...]) * ys).astype(BF16)
        mg_ref[...] = mg
        out_ref[...] = jnp.dot(mg, wo_ref[...], preferred_element_type=F32)

    row = pl.BlockSpec((POST_R, D_MODEL), lambda i: (i, 0))
    pcol = lambda c0: pl.BlockSpec((POST_R, D_MODEL), lambda i: (i, c0 // D_MODEL))
    full = lambda r: pl.BlockSpec((r, D_MODEL), lambda i: (0, 0))
    return pl.pallas_call(
        body, grid=(T // POST_R,),
        in_specs=[row, pcol(C_ZA), pcol(C_GA), pcol(C_GS), pl.BlockSpec((POST_R, SSM_INNER), lambda i: (i, 0)),
                  full(D_MODEL), full(SSM_INNER), full(D_MODEL)],
        out_specs=[row, row, row, row, row],
        out_shape=[SDS((T, D_MODEL), BF16), SDS((T, D_MODEL), BF16), SDS((T, D_MODEL), BF16), SDS((T, D_MODEL), BF16),
                   SDS((T, D_MODEL), F32)],
        compiler_params=_cparams(), name="post_a")(o, proj, proj, proj, sn, w_att, w_ssm, w_o)


def _post_b(out, h, tgt, proj, ya, ys, o, g_post, w_att, w_ssm, w_o):
    def body(out_ref, h_ref, t_ref, za_ref, ga_ref, gs_ref, ya_ref, ys_ref, o_ref, gp_ref, wa_ref, ws_ref, wo_ref,
             loss_ref, dres_ref, dout_ref, dya_ref, dys_ref, do_ref, dp_ref, dsn_ref, dgp_ref):
        i = pl.program_id(0)
        x = out_ref[...]
        gp = gp_ref[...]
        r = lax.rsqrt(jnp.mean(x * x, axis=-1, keepdims=True) + EPS)
        row = i * POST_R + lax.broadcasted_iota(jnp.int32, (POST_R, 1), 0)
        res = h_ref[...] + jnp.where(row >= PAD, x * r * gp, 0.0)
        live = row >= PAD + N_META
        err = jnp.where(live, res - t_ref[...], 0.0)
        lpart = 0.5 * jnp.sum(jnp.sum(err * err, axis=1, keepdims=True) / D_MODEL, axis=0, keepdims=True)
        dres = err / D_MODEL
        dres_ref[...] = dres
        gpart = jnp.sum(dres * x * r, axis=0, keepdims=True)

        @pl.when(i == 0)
        def _():
            loss_ref[...] = jnp.zeros_like(loss_ref)
            dgp_ref[...] = jnp.zeros_like(dgp_ref)

        loss_ref[...] += jnp.broadcast_to(lpart, loss_ref.shape)
        dgp_ref[0:1, :] += gpart
        gd = gp * dres
        dout = (r * gd - x * (r * r * r) * jnp.mean(x * gd, axis=-1, keepdims=True)).astype(BF16)
        dout_ref[...] = dout
        dmg = lax.dot_general(dout, wo_ref[...], (((1,), (1,)), ((), ())), preferred_element_type=F32)
        sga = jax.nn.sigmoid(ga_ref[...])
        sgs = jax.nn.sigmoid(gs_ref[...])
        dya = (dmg * sga).astype(BF16)
        dys = (dmg * sgs).astype(BF16)
        dya_ref[...] = dya
        dys_ref[...] = dys
        dp_ref[:, C_GA - C_ZA:C_GA - C_ZA + D_MODEL] = (dmg * ya_ref[...].astype(F32) * sga * (1.0 - sga)).astype(BF16)
        dp_ref[:, C_GS - C_ZA:C_GS - C_ZA + D_MODEL] = (dmg * ys_ref[...].astype(F32) * sgs * (1.0 - sgs)).astype(BF16)
        da = lax.dot_general(dya, wa_ref[...], (((1,), (1,)), ((), ())), preferred_element_type=F32)
        za = za_ref[...]
        do_ref[...] = (da * _silu(za)).astype(BF16)
        dp_ref[:, 0:D_MODEL] = (da * o_ref[...] * _dsilu(za)).astype(BF16)
        dsn_ref[...] = lax.dot_general(dys, ws_ref[...], (((1,), (1,)), ((), ())),
                                       preferred_element_type=F32).astype(BF16)

    row = pl.BlockSpec((POST_R, D_MODEL), lambda i: (i, 0))
    pcol = lambda c0: pl.BlockSpec((POST_R, D_MODEL), lambda i: (i, c0 // D_MODEL))
    full = lambda r: pl.BlockSpec((r, D_MODEL), lambda i: (0, 0))
    small = pl.BlockSpec((8, D_MODEL), lambda i: (0, 0))
    return pl.pallas_call(
        body, grid=(T // POST_R,),
        in_specs=[row, row, row, pcol(C_ZA), pcol(C_GA), pcol(C_GS), row, row, row,
                  pl.BlockSpec((1, D_MODEL), lambda i: (0, 0)), full(D_MODEL), full(SSM_INNER), full(D_MODEL)],
        out_specs=[pl.BlockSpec((8, 128), lambda i: (0, 0)), row, row, row, row, row,
                   pl.BlockSpec((POST_R, GATES_W), lambda i: (i, C_ZA // GATES_W)),
                   pl.BlockSpec((POST_R, SSM_INNER), lambda i: (i, 0)), small],
        out_shape=[SDS((8, 128), F32), SDS((T, D_MODEL), F32), SDS((T, D_MODEL), BF16), SDS((T, D_MODEL), BF16),
                   SDS((T, D_MODEL), BF16), SDS((T, D_MODEL), BF16), SDS((T, PW), BF16),
                   SDS((T, SSM_INNER), BF16), SDS((8, D_MODEL), F32)],
        compiler_params=_cparams(), name="post_b")(out, h, tgt, proj, proj, proj, ya, ys, o, g_post, w_att, w_ssm, w_o)


TAIL_W = PW - C_K


def _dproj_tail(dproj, dk, dv, ddt4):
    rows = T // 4

    def body(_, dk_ref, dv_ref, ddt_ref, o_ref, buf, sem):
        n = pl.program_id(0)
        d4 = ddt_ref[...]
        buf[:, 0:KV_W] = dk_ref[...].astype(BF16)
        buf[:, KV_W:2 * KV_W] = dv_ref[...].astype(BF16)
        buf[:, 2 * KV_W:TAIL_W] = (d4[:, 0:128] + d4[:, 128:256] + d4[:, 256:384] + d4[:, 384:512]).astype(BF16)
        cp = pltpu.make_async_copy(buf, o_ref.at[pl.ds(pl.multiple_of(n * rows, 16), rows), pl.ds(C_K, TAIL_W)], sem)
        cp.start()
        cp.wait()

    spec = lambda w: pl.BlockSpec((rows, w), lambda i: (i, 0))
    return pl.pallas_call(
        body, grid=(T // rows,), in_specs=[ANY, spec(KV_W), spec(KV_W), spec(GRP_W)], out_specs=ANY,
        out_shape=SDS((T, PW), BF16), input_output_aliases={0: 0},
        scratch_shapes=[pltpu.VMEM((rows, TAIL_W), BF16), pltpu.SemaphoreType.DMA],
        name="dproj_tail")(dproj, dk, dv, ddt4)


def _adamw_math(w, g, m, v):
    m = ADAM_B1 * m + (1.0 - ADAM_B1) * g
    v = ADAM_B2 * v + (1.0 - ADAM_B2) * (g * g)
    m_hat = m / (1.0 - ADAM_B1 ** ADAM_STEP)
    v_hat = v / (1.0 - ADAM_B2 ** ADAM_STEP)
    delta = -ADAM_LR * (m_hat / (jnp.sqrt(v_hat) + ADAM_EPS) + ADAM_WD * w)
    return delta, m, v


def _sum_adamw(recv, w, m, v, tc, name):
    rows, cols = w.shape
    nslab = recv.shape[0]
    assert cols % tc == 0

    def body(r_ref, w_ref, m_ref, v_ref, g_ref, d_ref, nm_ref, nv_ref):
        g = r_ref[0].astype(F32)
        for d in range(1, nslab):
            g = g + r_ref[d].astype(F32)
        g_ref[...] = g
        delta, nm, nv = _adamw_math(w_ref[...], g, m_ref[...], v_ref[...])
        d_ref[...] = delta
        nm_ref[...] = nm
        nv_ref[...] = nv

    blk = pl.BlockSpec((rows, tc), lambda i: (0, i))
    return pl.pallas_call(
        body, grid=(cols // tc,),
        in_specs=[pl.BlockSpec((nslab, rows, tc), lambda i: (0, 0, i)), blk, blk, blk],
        out_specs=[blk, blk, blk, blk], out_shape=[SDS((rows, cols), F32)] * 4,
        compiler_params=_cparams(), name=name)(recv, w, m, v)


def _sum_adamw_rows3(recv, w3, m3, v3, name, exchange=()):
    pairs = 61
    assert (SHARD_IN // 2) % pairs == 0
    nsteps = SHARD_IN // 2 // pairs
    ne = len(exchange)

    def body(*refs):
        r_ref, w_ref, m_ref, v_ref = refs[:4]
        g_ref, d_ref, nm_ref, nv_ref = refs[4 + ne:8 + ne]
        if ne:
            ex_start, ex_finish = _direct_program(refs[4:4 + ne], refs[8 + ne:8 + 2 * ne], refs[8 + 2 * ne:])
            pl.when(pl.program_id(0) == 0)(ex_start)
        g = r_ref[0].astype(F32)
        for d in range(1, N_CHIP):
            g = g + r_ref[d].astype(F32)
        g = g.reshape(2 * pairs, ROW_TILES, 128)
        g_ref[...] = g
        delta, nm, nv = _adamw_math(w_ref[...], g, m_ref[...], v_ref[...])
        d_ref[...] = delta
        nm_ref[...] = nm
        nv_ref[...] = nv
        if ne:
            pl.when(pl.program_id(0) == nsteps - 1)(ex_finish)

    blk = pl.BlockSpec((2 * pairs, ROW_TILES, 128), lambda i: (i, 0, 0))
    return pl.pallas_call(
        body, grid=(nsteps,),
        in_specs=[pl.BlockSpec((N_CHIP, pairs, 2 * ROW_TILES, 128), lambda i: (0, i, 0, 0)), blk, blk, blk]
        + [ANY] * ne,
        out_specs=[blk, blk, blk, blk] + [ANY] * ne,
        out_shape=[SDS(w3.shape, F32)] * 4 + [SDS(p.shape, p.dtype) for p in exchange],
        scratch_shapes=_direct_scratch(exchange) if ne else [],
        compiler_params=_cparams(), name=name)(recv, w3, m3, v3, *exchange)


ROW_GPRE, ROW_CONVB, ROW_DTB, ROW_ALOG, ROW_DSKIP, ROW_SINK, ROW_GSSM, ROW_GPOST = 0, 1, 4, 5, 6, 7, 8, 10
ROW_LOSS = 11
REP_ROWS, ROW_CONVW, ROW_META, SM_ROWS = 16, 16, 24, 40
CW_SHARD = CONV_DIM // N_DEV
META_SHARD = D_MODEL // N_DEV


def _small_pack(dgpre, db, ddtb, dal, ddsk, dsink, dgn, dgp, dw, loss, dh):
    def body(dgpre_ref, db_ref, ddtb_ref, dal_ref, ddsk_ref, dsink_ref, dgn_ref, dgp_ref, dw_ref, loss_ref, dh_ref,
             o_ref, rep):
        rep[...] = jnp.zeros_like(rep)
        rep[ROW_LOSS:ROW_LOSS + 1, 0:128] = loss_ref[0:1, :]
        rep[ROW_GPRE:ROW_GPRE + 1, :] = dgpre_ref[0:1, :]
        for k in range(3):
            rep[ROW_CONVB + k:ROW_CONVB + k + 1, :] = db_ref[0:1, 1024 * k:1024 * (k + 1)]
        rep[ROW_DTB:ROW_DTB + 1, 0:128] = ddtb_ref[0:1, :]
        rep[ROW_ALOG:ROW_ALOG + 1, 0:128] = dal_ref[0:1, :]
        rep[ROW_DSKIP:ROW_DSKIP + 1, 0:128] = ddsk_ref[0:1, :]
        rep[ROW_SINK:ROW_SINK + 1, 0:128] = dsink_ref[0:1, :]
        rep[ROW_GSSM:ROW_GSSM + 1, :] = dgn_ref[0:1, 0:1024]
        rep[ROW_GSSM + 1:ROW_GSSM + 2, :] = dgn_ref[0:1, 1024:2048]
        rep[ROW_GPOST:ROW_GPOST + 1, :] = dgp_ref[0:1, :]
        cw = dw_ref[...]
        mh = dh_ref[...]
        o_ref[...] = jnp.zeros_like(o_ref)
        for p in range(N_DEV):
            o_ref[p, 0:REP_ROWS, :] = rep[...]
            o_ref[p, ROW_CONVW:ROW_CONVW + 8, 0:CW_SHARD] = cw[:, p * CW_SHARD:(p + 1) * CW_SHARD]
            o_ref[p, ROW_META:ROW_META + N_META, 0:META_SHARD] = mh[:, p * META_SHARD:(p + 1) * META_SHARD]

    ins = [dgpre, db, ddtb, dal, ddsk, dsink, dgn, dgp, dw, loss]
    return pl.pallas_call(
        body, grid=(1,),
        in_specs=[pl.BlockSpec(a.shape, lambda i: (0, 0)) for a in ins]
        + [pl.BlockSpec((N_META, D_MODEL), lambda i: (PAD // N_META, 0))],
        out_specs=pl.BlockSpec((N_DEV, SM_ROWS, 1024), lambda i: (0, 0, 0)),
        out_shape=SDS((N_DEV, SM_ROWS, 1024), F32), scratch_shapes=[pltpu.VMEM((REP_ROWS, 1024), F32)],
        name="small_pack")(*ins, dh)


def _small_finish(recv, params):
    npar = len(params)

    def body(*refs):
        r_ref = refs[0]
        wmv = refs[1:1 + 3 * npar]
        outs = refs[1 + 3 * npar:1 + 7 * npar]
        loss_ref = refs[1 + 7 * npar]
        gs = refs[-1]
        g = r_ref[0]
        for d in range(1, recv.shape[0]):
            g = g + r_ref[d]
        gs[...] = g
        loss_ref[...] = gs[ROW_LOSS:ROW_LOSS + 1, 0:128]
        grads = [
            gs[ROW_GPRE:ROW_GPRE + 1, :],
            jnp.concatenate([gs[ROW_CONVB + k:ROW_CONVB + k + 1, :] for k in range(3)], axis=1),
            gs[ROW_DTB:ROW_DTB + 1, 0:SSM_HEADS], gs[ROW_ALOG:ROW_ALOG + 1, 0:SSM_HEADS],
            gs[ROW_DSKIP:ROW_DSKIP + 1, 0:SSM_HEADS], gs[ROW_SINK:ROW_SINK + 1, 0:Q_HEADS],
            jnp.concatenate([gs[ROW_GSSM:ROW_GSSM + 1, :], gs[ROW_GSSM + 1:ROW_GSSM + 2, :]], axis=1),
            gs[ROW_GPOST:ROW_GPOST + 1, :],
            gs[ROW_CONVW:ROW_CONVW + 4, 0:CW_SHARD],
            gs[ROW_META:ROW_META + N_META, 0:META_SHARD]]
        for i in range(npar):
            w_ref, m_ref, v_ref = wmv[3 * i:3 * i + 3]
            delta, nm, nv = _adamw_math(w_ref[...], grads[i], m_ref[...], v_ref[...])
            outs[4 * i][...] = grads[i]
            outs[4 * i + 1][...] = delta
            outs[4 * i + 2][...] = nm
            outs[4 * i + 3][...] = nv

    flat = [a for wmv in params for a in wmv]
    res = pl.pallas_call(
        body, out_shape=[SDS(wmv[0].shape, F32) for wmv in params for _ in range(4)] + [SDS((1, 128), F32)],
        scratch_shapes=[pltpu.VMEM((SM_ROWS, 1024), F32)], name="small_finish")(recv, *flat)
    return [tuple(res[4 * i:4 * i + 4]) for i in range(npar)], res[4 * npar]


def _slab(ref, px, py, pc):
    return ref.at[4 * px + 2 * py + pc]


def _bounce(src, dst, buf, sem):
    cp = pltpu.make_async_copy(src, buf, sem)
    cp.start()
    cp.wait()
    cp = pltpu.make_async_copy(buf, dst, sem)
    cp.start()
    cp.wait()


def _ag_program(ins, outs, scratch):
    na = len(ins)
    send_sems, recv_sems, local_sems = scratch[:3]
    bufs = scratch[3:]
    x, y, c = lax.axis_index("x"), lax.axis_index("y"), lax.axis_index("c")
    me, sibling = (x, y, c), (x, y, 1 - c)
    chips = [(1 - x, y), (x, 1 - y), (1 - x, 1 - y)]

    def copy(a, k, block, to, src=None):
        dst = _slab(outs[a], *block)
        return pltpu.make_async_remote_copy(
            src_ref=dst if src is None else src, dst_ref=dst, send_sem=send_sems.at[a, k],
            recv_sem=recv_sems.at[a, k], device_id=to, device_id_type=MESH)

    def own_sends():
        out = []
        for a in range(na):
            out.append(copy(a, 0, me, sibling, src=ins[a]))
            out += [copy(a, 1 + j, me, (*chip, c), src=ins[a]) for j, chip in enumerate(chips)]
        return out

    def start():
        for cp in own_sends():
            cp.start()
        for a in range(na):
            _bounce(ins[a], _slab(outs[a], *me), bufs[a], local_sems.at[a])

    def forward():
        for j, chip in enumerate(chips):
            for a in range(na):
                copy(a, 1 + j, (*chip, c), me).wait_recv()
                copy(a, 4 + j, (*chip, c), sibling).start()

    def finish():
        for a in range(na):
            copy(a, 0, sibling, me).wait_recv()
            for j, chip in enumerate(chips):
                copy(a, 4 + j, (*chip, 1 - c), me).wait_recv()
        for cp in own_sends():
            cp.wait_send()
        for j, chip in enumerate(chips):
            for a in range(na):
                copy(a, 4 + j, (*chip, c), sibling).wait_send()

    return start, forward, finish


def _ag_scratch(shards):
    na = len(shards)
    return [pltpu.SemaphoreType.DMA((na, 7)), pltpu.SemaphoreType.DMA((na, 7)),
            pltpu.SemaphoreType.DMA((na,))] + [pltpu.VMEM(s.shape, s.dtype) for s in shards]


def _all_gather(shards):
    na = len(shards)

    def body(*refs):
        start, forward, finish = _ag_program(refs[:na], refs[na:2 * na], refs[2 * na:])
        start()
        forward()
        finish()

    return pl.pallas_call(
        body, in_specs=[ANY] * na, out_specs=[ANY] * na,
        out_shape=[SDS((N_DEV,) + s.shape, s.dtype) for s in shards],
        scratch_shapes=_ag_scratch(shards), name="all_gather")(*shards)


N_CHIP = 4


def _pair_sum(own, got, name):
    na = len(own)

    def body(*refs):
        for a in range(na):
            o_ref, g_ref, s_ref = refs[a], refs[na + a], refs[2 * na + a]
            s_ref[...] = (o_ref[...].astype(F32) + g_ref[...].astype(F32)).astype(s_ref.dtype)

    def spec(p):
        nd = len(p.shape) - 1
        return pl.BlockSpec((1,) + p.shape[1:], lambda k, nd=nd: (k,) + (0,) * nd)

    return pl.pallas_call(
        body, grid=(N_CHIP,), in_specs=[spec(p) for p in own] + [spec(p) for p in got],
        out_specs=[spec(p) for p in own], out_shape=[SDS(p.shape, p.dtype) for p in own],
        compiler_params=_cparams(), name=name)(*own, *got)


def _chips_program(ins, outs, scratch):
    na = len(ins)
    send_sems, recv_sems, local_sems = scratch[:3]
    bufs = scratch[3:]
    x, y, c = lax.axis_index("x"), lax.axis_index("y"), lax.axis_index("c")
    mine = 2 * x + y
    chips = [(1 - x, y), (x, 1 - y), (1 - x, 1 - y)]

    def send(a, j):
        px, py = chips[j]
        return pltpu.make_async_remote_copy(
            src_ref=ins[a].at[2 * px + py], dst_ref=outs[a].at[mine], send_sem=send_sems.at[a, j],
            recv_sem=recv_sems.at[a, j], device_id=(px, py, c), device_id_type=MESH)

    def arrival(a, j):
        px, py = chips[j]
        return pltpu.make_async_remote_copy(
            src_ref=ins[a].at[2 * px + py], dst_ref=outs[a].at[2 * px + py], send_sem=send_sems.at[a, j],
            recv_sem=recv_sems.at[a, j], device_id=(px, py, c), device_id_type=MESH)

    def start():
        for a in range(na):
            for j in range(3):
                send(a, j).start()
        for a in range(na):
            _bounce(ins[a].at[mine], outs[a].at[mine], bufs[a], local_sems.at[a])

    def finish():
        for a in range(na):
            for j in range(3):
                arrival(a, j).wait_recv()
        for a in range(na):
            for j in range(3):
                send(a, j).wait_send()

    return start, finish


def _chips_scratch(parts):
    na = len(parts)
    return [pltpu.SemaphoreType.DMA((na, 3)), pltpu.SemaphoreType.DMA((na, 3)),
            pltpu.SemaphoreType.DMA((na,))] + [pltpu.VMEM(p.shape[1:], p.dtype) for p in parts]


def _direct_program(ins, outs, scratch):
    na = len(ins)
    send_sems, recv_sems, local_sems = scratch[:3]
    bufs = scratch[3:]
    x, y, c = lax.axis_index("x"), lax.axis_index("y"), lax.axis_index("c")
    me = (x, y, c)
    peers = []
    for k in range(1, N_DEV):
        dx, dy, dc = (k >> 2) & 1, (k >> 1) & 1, k & 1
        peers.append(((1 - x) if dx else x, (1 - y) if dy else y, (1 - c) if dc else c))

    def send(a, k):
        return pltpu.make_async_remote_copy(
            src_ref=_slab(ins[a], *peers[k]), dst_ref=_slab(outs[a], *me), send_sem=send_sems.at[a, k],
            recv_sem=recv_sems.at[a, k], device_id=peers[k], device_id_type=MESH)

    def arrival(a, k):
        return pltpu.make_async_remote_copy(
            src_ref=_slab(ins[a], *peers[k]), dst_ref=_slab(outs[a], *peers[k]), send_sem=send_sems.at[a, k],
            recv_sem=recv_sems.at[a, k], device_id=peers[k], device_id_type=MESH)

    def start():
        for a in range(na):
            for k in range(N_DEV - 1):
                send(a, k).start()
        for a in range(na):
            _bounce(_slab(ins[a], *me), _slab(outs[a], *me), bufs[a], local_sems.at[a])

    def finish():
        for a in range(na):
            for k in range(N_DEV - 1):
                arrival(a, k).wait_recv()
        for a in range(na):
            for k in range(N_DEV - 1):
                send(a, k).wait_send()

    return start, finish


def _direct_scratch(parts):
    na = len(parts)
    return [pltpu.SemaphoreType.DMA((na, N_DEV - 1)), pltpu.SemaphoreType.DMA((na, N_DEV - 1)),
            pltpu.SemaphoreType.DMA((na,))] + [pltpu.VMEM(p.shape[1:], p.dtype) for p in parts]


ROW_TILES = D_MODEL // 128


def _rows3(t):
    return jnp.transpose(t[0]).reshape(t.shape[2], ROW_TILES, 128)


def _unrows3(t):
    return jnp.transpose(t.reshape(t.shape[0], D_MODEL))[None]


def _cast_shards(w_in3, w_att, w_ssm, w_o):
    def body(wi_ref, wa_ref, ws_ref, wo_ref, a_ref, b_ref, c_ref, d_ref):
        a_ref[...] = wi_ref[...].reshape(SHARD_IN // 2, 2 * ROW_TILES, 128).astype(BF16)
        b_ref[...] = wa_ref[...].astype(BF16)
        c_ref[...] = ws_ref[...].astype(BF16)
        d_ref[...] = wo_ref[...].astype(BF16)

    return pl.pallas_call(
        body, out_shape=[SDS((SHARD_IN // 2, 2 * ROW_TILES, 128), BF16), SDS(w_att.shape, BF16),
                         SDS(w_ssm.shape, BF16), SDS(w_o.shape, BF16)],
        compiler_params=_cparams(), name="cast_shards")(w_in3, w_att, w_ssm, w_o)


def _pieces():
    out = []
    for r0, c0, w in _SEGS:
        r = r0
        while r < r0 + w:
            d = r // SHARD_IN
            n = min(r0 + w, (d + 1) * SHARD_IN) - r
            out.append((c0 + (r - r0), d, r - d * SHARD_IN, n))
            r += n
    return out


def _to_aligned_t(slabs):
    def body(a_ref, o_ref):
        for (t, d, s, n) in _pieces():
            o_ref[t:t + n, :] = a_ref[d, s // 2:(s + n) // 2].reshape(n, D_MODEL)
        o_ref[C_DT + 32:C_DT + 128, :] = jnp.zeros((96, D_MODEL), slabs.dtype)

    return pl.pallas_call(body, out_shape=SDS((PW, D_MODEL), slabs.dtype), compiler_params=_cparams(),
                          name="to_aligned")(slabs)


def _from_aligned_pair(g):
    slab = (SHARD_IN // 2, 2 * ROW_TILES, 128)
    by_slab = [[p for p in _pieces() if p[1] == d] for d in range(N_DEV)]

    def body(g_ref, own_ref, got_ref, slabs, send_sems, recv_sems, local_sems):
        x, y, c = lax.axis_index("x"), lax.axis_index("y"), lax.axis_index("c")
        sibling = (x, y, 1 - c)

        def to_own(d, k):
            return pltpu.make_async_copy(slabs.at[d], own_ref.at[k], local_sems.at[k])

        def to_sibling(d, k):
            return pltpu.make_async_remote_copy(
                src_ref=slabs.at[d], dst_ref=got_ref.at[k], send_sem=send_sems.at[k], recv_sem=recv_sems.at[k],
                device_id=sibling, device_id_type=MESH)

        for d in range(N_DEV):
            for (t, _, s, n) in by_slab[d]:
                slabs[d, s // 2:(s + n) // 2] = g_ref[t:t + n, :].reshape(n // 2, 2 * ROW_TILES, 128)
            k, side = d // 2, d % 2
            pl.when(c == side)(to_own(d, k).start)
            pl.when(c != side)(to_sibling(d, k).start)
        for k in range(N_CHIP):
            to_own(0, k).wait()
            to_sibling(0, k).wait()

    half = SDS((N_CHIP,) + slab, g.dtype)
    return pl.pallas_call(
        body, in_specs=[pl.BlockSpec(memory_space=pltpu.VMEM)], out_specs=[ANY, ANY], out_shape=[half, half],
        scratch_shapes=[pltpu.VMEM((N_DEV,) + slab, g.dtype), pltpu.SemaphoreType.DMA((N_CHIP,)),
                        pltpu.SemaphoreType.DMA((N_CHIP,)), pltpu.SemaphoreType.DMA((N_CHIP,))],
        compiler_params=_cparams(), name="from_aligned_pair")(g)


_SEGS = [
    (R_Q, C_Q, 1024), (R_K, C_K, 256), (R_V, C_V, 256), (R_ZA, C_ZA, 1024), (R_ZS, C_ZS, 2048),
    (R_XBC, C_XBC, 3072), (R_DT, C_DT, 32), (R_GA, C_GA, 1024), (R_GS, C_GS, 1024)]


def _pad_lanes(v, n=128):
    return jnp.pad(v, ((0, 0), (0, n - v.shape[1])))


def _device_step(h, tgt, w_alt, w_out, g_pre, conv_w8, conv_b, dt_bias, a_log, d_skip, sinks, g_ssm, g_post, on_mesh):
    dtb, al, dsk, snk = _pad_lanes(dt_bias), _pad_lanes(a_log), _pad_lanes(d_skip), _pad_lanes(sinks)
    u = _norm_u(h, g_pre)
    proj = _matmul(u, w_alt, "nt", F32, T, 896, "in_proj")
    o = _attn_fwd(proj, snk)
    if on_mesh:
        sn, states, att_all, ssm_all, o_all = _ssd_fwd(proj, conv_w8, conv_b, dtb, al, dsk, g_ssm, gather=w_out)
        w_att = att_all.reshape(D_MODEL, D_MODEL)
        w_ssm = ssm_all.reshape(SSM_INNER, D_MODEL)
        w_o = o_all.reshape(D_MODEL, D_MODEL)
    else:
        sn, states = _ssd_fwd(proj, conv_w8, conv_b, dtb, al, dsk, g_ssm)
        w_att, w_ssm, w_o = w_out
    a_in, mg, ya, ys, out = _post_a(o, proj, sn, w_att, w_ssm, w_o)
    (loss, dres, dout, dya, dys, do, dproj, dsn, dgp) = _post_b(
        out, h, tgt, proj, ya, ys, o, g_post, w_att, w_ssm, w_o)
    dw_att = _matmul(a_in, dya, "tn", BF16, D_MODEL, D_MODEL, "d_w_att")
    dw_ssm = _matmul(sn, dys, "tn", BF16, D_MODEL, D_MODEL, "d_w_ssm")
    dw_o = _matmul(mg, dout, "tn", BF16, D_MODEL, D_MODEL, "d_w_o")
    res = {}
    if on_mesh:
        parts = [dw_att.reshape(N_DEV, 128, D_MODEL), dw_ssm.reshape(N_DEV, 256, D_MODEL),
                 dw_o.reshape(N_DEV, 128, D_MODEL)]
        (ddt4, dproj, ddtb, dal, ddsk, dgn, dcw, dcb, res["r_att"], res["r_ssm"], res["r_o"]) = _ssd_bwd(
            proj, conv_w8, conv_b, dtb, al, dsk, g_ssm, states, dsn, dproj, exchange=parts)
    else:
        ddt4, dproj, ddtb, dal, ddsk, dgn, dcw, dcb = _ssd_bwd(proj, conv_w8, conv_b, dtb, al, dsk, g_ssm, states,
                                                               dsn, dproj)
        res.update(dw_att=dw_att, dw_ssm=dw_ssm, dw_o=dw_o)
    dproj, dk, dv, dsink = _attn_bwd(proj, snk, do, dproj)
    dproj = _dproj_tail(dproj, dk, dv, ddt4)
    dw_alt = _matmul(dproj, u, "tn", BF16, 896, D_MODEL, "d_w_in")
    if on_mesh:
        own, got = _from_aligned_pair(dw_alt)
        dh, dgpre, res["r_in"] = _d_u_norm(dproj, w_alt, h, g_pre, dres,
                                           chips=_pair_sum([own], [got], "pair_sum_w_in"))
    else:
        dh, dgpre = _d_u_norm(dproj, w_alt, h, g_pre, dres)
        res["dw_alt"] = dw_alt
    small = (dgpre, dcb, ddtb, dal, ddsk, dsink, dgn, dgp, dcw)
    if on_mesh:
        res["small_pack"] = _small_pack(*small, loss, dh)
    else:
        res["small"] = small
    res.update(loss=loss[0, 0], dh=dh)
    return res


def kernel(x, meta_tokens, g_pre, w_in, conv_w, conv_b, dt_bias, a_log, d_skip, attn_sinks, g_ssm_norm, w_out_att, w_out_ssm, w_out, g_post, loss_target, m_meta_tokens, m_g_pre, m_w_in, m_conv_w, m_conv_b, m_dt_bias, m_a_log, m_d_skip, m_attn_sinks, m_g_ssm_norm, m_w_out_att, m_w_out_ssm, m_w_out, m_g_post, v_meta_tokens, v_g_pre, v_w_in, v_conv_w, v_conv_b, v_dt_bias, v_a_log, v_d_skip, v_attn_sinks, v_g_ssm_norm, v_w_out_att, v_w_out_ssm, v_w_out, v_g_post):
    w_in3, m_in3, v_in3 = _rows3(w_in), _rows3(m_w_in), _rows3(v_w_in)
    a_sh, att_sh, ssm_sh, o_sh = _cast_shards(w_in3, w_out_att[0], w_out_ssm[0], w_out[0])
    cw_sh = jnp.pad(conv_w[0], ((0, 4), (0, 0)))
    a_all, meta_all, cw_all = _all_gather([a_sh, meta_tokens, cw_sh])
    w_alt = _to_aligned_t(a_all)
    meta_full = meta_all.transpose(1, 0, 2).reshape(N_META, D_MODEL)
    conv_w8 = cw_all.transpose(1, 0, 2).reshape(8, CONV_DIM)

    h = jnp.concatenate([jnp.zeros((PAD, D_MODEL), F32), meta_full, x[0]], axis=0)
    tgt = jnp.concatenate([jnp.zeros((PAD + N_META, D_MODEL), F32), loss_target[0]], axis=0)
    r = _device_step(h, tgt, w_alt, (att_sh, ssm_sh, o_sh), g_pre, conv_w8, conv_b, dt_bias, a_log, d_skip,
                     attn_sinks, g_ssm_norm, g_post, True)
    grad_x = r["dh"][PAD + N_META:][None]

    *res_in, r_small = _sum_adamw_rows3(r["r_in"], w_in3, m_in3, v_in3, "adamw_w_in", exchange=[r["small_pack"]])
    res_in = [_unrows3(t) for t in res_in]
    res_att = [t[None] for t in _sum_adamw(r["r_att"], w_out_att[0], m_w_out_att[0], v_w_out_att[0], 512,
                                           "adamw_w_att")]
    res_ssm = [t[None] for t in _sum_adamw(r["r_ssm"], w_out_ssm[0], m_w_out_ssm[0], v_w_out_ssm[0], 512,
                                           "adamw_w_ssm")]
    res_o = [t[None] for t in _sum_adamw(r["r_o"], w_out[0], m_w_out[0], v_w_out[0], 512, "adamw_w_o")]
    (res_gpre, res_convb, res_dtb, res_alog, res_dskip, res_sink, res_gssm, res_gpost, res_cw, res_meta), loss = _small_finish(
        r_small, [(g_pre, m_g_pre, v_g_pre), (conv_b, m_conv_b, v_conv_b), (dt_bias, m_dt_bias, v_dt_bias),
                       (a_log, m_a_log, v_a_log), (d_skip, m_d_skip, v_d_skip),
                       (attn_sinks, m_attn_sinks, v_attn_sinks), (g_ssm_norm, m_g_ssm_norm, v_g_ssm_norm),
                       (g_post, m_g_post, v_g_post), (conv_w[0], m_conv_w[0], v_conv_w[0]),
                       (meta_tokens, m_meta_tokens, v_meta_tokens)])
    res_cw = [t[None] for t in res_cw]
    per_weight = [res_meta, res_gpre, res_in, res_cw, res_convb, res_dtb, res_alog, res_dskip, res_sink, res_gssm,
                  res_att, res_ssm, res_o, res_gpost]
    return (loss[0, 0], grad_x, *[p[0] for p in per_weight], *[p[1] for p in per_weight], *[p[2] for p in per_weight],
            *[p[3] for p in per_weight])
```

```python
import jax
import jax.numpy as jnp
from jax import lax
from jax.experimental import pallas as pl
from jax.experimental.pallas import tpu as pltpu

F32 = jnp.float32
BF16 = jnp.bfloat16
SDS = jax.ShapeDtypeStruct
MESH = pl.DeviceIdType.MESH
ANY = pl.BlockSpec(memory_space=pl.ANY)

N_DEV = 8
D_MODEL = 1024
SEQ = 2048
N_META = 16
BLK = 128
PAD = 112
T = PAD + N_META + SEQ
NB = T // BLK
EPS = 1e-6
HEAD = 64
Q_HEADS = 16
KV_HEADS = 4
GROUP = 4
KV_W = 256
SSM_INNER = 2048
SSM_HEADS = 32
SSM_GROUPS = 4
GRP_W = 512
SSM_STATE = 128
CONV_DIM = 3072
IN_PROJ = 9760
SHARD_IN = IN_PROJ // N_DEV
NEG = -1e30

C_ZS, C_XBC, C_Q, C_ZA, C_GA, C_GS, C_K, C_V, C_DT = 0, 2048, 5120, 6144, 7168, 8192, 9216, 9472, 9728
PW = 9856
GATES_W = 3 * D_MODEL
PROJ_TILE = 1408
R_Q, R_K, R_V, R_ZA, R_ZS, R_XBC, R_DT, R_GA, R_GS = 0, 1024, 1280, 1536, 2560, 4608, 7680, 7712, 8736

ADAM_LR, ADAM_B1, ADAM_B2, ADAM_EPS, ADAM_WD, ADAM_STEP = 0.001, 0.9, 0.999, 1e-08, 0.01, 10

VMEM_LIMIT = 56 * 1024 * 1024


def _cparams():
    return pltpu.CompilerParams(vmem_limit_bytes=VMEM_LIMIT)


def _silu(x):
    return x * jax.nn.sigmoid(x)


def _dsilu(x):
    s = jax.nn.sigmoid(x)
    return s * (1.0 + x * (1.0 - s))


def _matmul(a, b, mode, out_dtype, tm, tn, name):
    if mode == "nt":
        (m, k), n = a.shape, b.shape[0]
        a_spec = pl.BlockSpec((tm, k), lambda i, j: (i, 0))
        b_spec = pl.BlockSpec((tn, k), lambda i, j: (j, 0))
        dims = (((1,), (1,)), ((), ()))
    else:
        assert mode == "tn"
        (k, m), n = a.shape, b.shape[1]
        a_spec = pl.BlockSpec((k, tm), lambda i, j: (0, i))
        b_spec = pl.BlockSpec((k, tn), lambda i, j: (0, j))
        dims = (((0,), (0,)), ((), ()))
    assert m % tm == 0 and n % tn == 0, (a.shape, b.shape, tm, tn)

    def body(a_ref, b_ref, o_ref):
        o_ref[...] = lax.dot_general(a_ref[...], b_ref[...], dims, preferred_element_type=F32).astype(out_dtype)

    return pl.pallas_call(
        body, grid=(m // tm, n // tn), in_specs=[a_spec, b_spec],
        out_specs=pl.BlockSpec((tm, tn), lambda i, j: (i, j)), out_shape=SDS((m, n), out_dtype),
        compiler_params=_cparams(), name=name)(a, b)


def _norm_u(h, g_pre):
    def body(h_ref, g_ref, u_ref):
        x = h_ref[...]
        r = lax.rsqrt(jnp.mean(x * x, axis=-1, keepdims=True) + EPS)
        u_ref[...] = (x * r * g_ref[...]).astype(BF16)

    return pl.pallas_call(
        body, grid=(NB,),
        in_specs=[pl.BlockSpec((BLK, D_MODEL), lambda i: (i, 0)), pl.BlockSpec((1, D_MODEL), lambda i: (0, 0))],
        out_specs=pl.BlockSpec((BLK, D_MODEL), lambda i: (i, 0)),
        out_shape=SDS((T, D_MODEL), BF16), name="norm_u")(h, g_pre)


DU_TM, DU_TK = T // 2, PROJ_TILE


def _d_u_norm(dproj, w_alt, h, g_pre, dres, chips=()):
    nk = PW // DU_TK
    ni = T // DU_TM
    nc = len(chips)

    def body(*refs):
        a_ref, b_ref, h_ref, g_ref, dres_ref = refs[:5]
        dh_ref, dg_ref = refs[5 + nc:7 + nc]
        acc_ref = refs[7 + 2 * nc]
        i, kk = pl.program_id(0), pl.program_id(1)
        if nc:
            ch_start, ch_finish = _chips_program(refs[5:5 + nc], refs[7 + nc:7 + 2 * nc], refs[8 + 2 * nc:])
            pl.when((i == 0) & (kk == 0))(ch_start)
        part = jnp.dot(a_ref[...], b_ref[...], preferred_element_type=F32)

        @pl.when(kk == 0)
        def _():
            acc_ref[...] = part

        @pl.when((kk > 0) & (kk < nk - 1))
        def _():
            acc_ref[...] += part

        @pl.when(kk == nk - 1)
        def _():
            du_ = acc_ref[...] + part
            x = h_ref[...]
            r = lax.rsqrt(jnp.mean(x * x, axis=-1, keepdims=True) + EPS)
            gd = g_ref[...] * du_
            dx = r * gd - x * (r * r * r) * jnp.mean(x * gd, axis=-1, keepdims=True)
            dh_ref[...] = dx + dres_ref[...]
            gpart = jnp.concatenate([jnp.sum(du_ * x * r, axis=0, keepdims=True), jnp.zeros((7, D_MODEL), F32)],
                                    axis=0)

            @pl.when(i == 0)
            def _():
                dg_ref[...] = gpart

            @pl.when(i > 0)
            def _():
                dg_ref[...] += gpart

        if nc:
            pl.when((i == ni - 1) & (kk == nk - 1))(ch_finish)

    row = pl.BlockSpec((DU_TM, D_MODEL), lambda i, kk: (i, 0))
    return pl.pallas_call(
        body, grid=(ni, nk),
        in_specs=[pl.BlockSpec((DU_TM, DU_TK), lambda i, kk: (i, kk)),
                  pl.BlockSpec((DU_TK, D_MODEL), lambda i, kk: (kk, 0)),
                  row, pl.BlockSpec((1, D_MODEL), lambda i, kk: (0, 0)), row] + [ANY] * nc,
        out_specs=[row, pl.BlockSpec((8, D_MODEL), lambda i, kk: (0, 0))] + [ANY] * nc,
        out_shape=[SDS((T, D_MODEL), F32), SDS((8, D_MODEL), F32)] + [SDS(p.shape, p.dtype) for p in chips],
        scratch_shapes=[pltpu.VMEM((DU_TM, D_MODEL), F32)] + (_chips_scratch(chips) if nc else []),
        compiler_params=_cparams(), name="d_u_norm")(dproj, w_alt, h, g_pre, dres, *chips)


def _lane_pick(row, h):
    lane = lax.broadcasted_iota(jnp.int32, row.shape, 1)
    return jnp.sum(jnp.where(lane == h, row, 0.0), axis=1, keepdims=True)


def _attn_fn(q4s, kcats, vcats, kms, vms, sinks, n):
    r = lax.broadcasted_iota(jnp.int32, (GROUP * BLK, 2 * BLK), 0)
    s = lax.broadcasted_iota(jnp.int32, (GROUP * BLK, 2 * BLK), 1)
    i = jnp.bitwise_and(r, BLK - 1)
    gi = jnp.right_shift(r, 7)
    rel = i - s + BLK
    k_pos = n * BLK - BLK + s
    band_ok = (rel >= 0) & (rel < BLK) & (k_pos >= PAD + N_META)
    relf = rel.astype(F32)
    rm = lax.broadcasted_iota(jnp.int32, (GROUP * BLK, N_META), 0)
    mm = lax.broadcasted_iota(jnp.int32, (GROUP * BLK, N_META), 1)
    meta_ok = (PAD + mm) <= (n * BLK + jnp.bitwise_and(rm, BLK - 1))
    gcol = jnp.right_shift(lax.broadcasted_iota(jnp.int32, (GROUP * BLK, 1), 0), 7)
    outs = []
    for kh in range(KV_HEADS):
        slopes = [2.0 ** (-8.0 * (kh * GROUP + g + 1) / Q_HEADS) for g in range(GROUP)]
        slope = jnp.where(gi == 0, slopes[0], jnp.where(gi == 1, slopes[1], jnp.where(gi == 2, slopes[2], slopes[3])))
        sk = [_lane_pick(sinks, kh * GROUP + g) for g in range(GROUP)]
        sink = jnp.where(gcol == 0, sk[0], jnp.where(gcol == 1, sk[1], jnp.where(gcol == 2, sk[2], sk[3])))
        qb = (q4s[kh] * (HEAD ** -0.5)).astype(BF16)
        sb = lax.dot_general(qb, kcats[kh].astype(BF16), (((1,), (1,)), ((), ())), preferred_element_type=F32)
        sb = jnp.where(band_ok, sb - slope * relf, NEG)
        sm = lax.dot_general(qb, kms[kh].astype(BF16), (((1,), (1,)), ((), ())), preferred_element_type=F32)
        sm = jnp.where(meta_ok, sm, NEG)
        mx = jnp.maximum(jnp.maximum(jnp.max(sb, axis=1, keepdims=True), jnp.max(sm, axis=1, keepdims=True)), sink)
        mx = lax.stop_gradient(mx)
        eb = jnp.exp(sb - mx)
        em = jnp.exp(sm - mx)
        es = jnp.exp(sink - mx)
        inv = 1.0 / (jnp.sum(eb, axis=1, keepdims=True) + jnp.sum(em, axis=1, keepdims=True) + es)
        pb = (eb * inv).astype(BF16)
        pm = (em * inv).astype(BF16)
        o4 = (jnp.dot(pm, vms[kh].astype(BF16), preferred_element_type=F32)
              + jnp.dot(pb, vcats[kh].astype(BF16), preferred_element_type=F32))
        outs.append(o4)
    return outs


def _attn_specs():
    prev = lambda n: jnp.maximum(n - 1, 0)
    return [
        pl.BlockSpec((BLK, D_MODEL), lambda n: (n, C_Q // D_MODEL)),
        pl.BlockSpec((BLK, KV_W), lambda n: (prev(n), C_K // KV_W)),
        pl.BlockSpec((BLK, KV_W), lambda n: (n, C_K // KV_W)),
        pl.BlockSpec((BLK, KV_W), lambda n: (prev(n), C_V // KV_W)),
        pl.BlockSpec((BLK, KV_W), lambda n: (n, C_V // KV_W)),
        pl.BlockSpec((N_META, KV_W), lambda n: (PAD // N_META, C_K // KV_W)),
        pl.BlockSpec((N_META, KV_W), lambda n: (PAD // N_META, C_V // KV_W)),
        pl.BlockSpec((1, 128), lambda n: (0, 0)),
    ]


def _attn_load(q_ref, kp_ref, kc_ref, vp_ref, vc_ref, km_ref, vm_ref):
    q4s, kcats, vcats, kms, vms = [], [], [], [], []
    for kh in range(KV_HEADS):
        q4s.append(jnp.concatenate(
            [q_ref[:, (kh * GROUP + g) * HEAD:(kh * GROUP + g + 1) * HEAD] for g in range(GROUP)], axis=0))
        cs = slice(kh * HEAD, (kh + 1) * HEAD)
        kcats.append(jnp.concatenate([kp_ref[:, cs], kc_ref[:, cs]], axis=0))
        vcats.append(jnp.concatenate([vp_ref[:, cs], vc_ref[:, cs]], axis=0))
        kms.append(km_ref[:, cs])
        vms.append(vm_ref[:, cs])
    return q4s, kcats, vcats, kms, vms


def _attn_fwd(proj, sinks):
    def body(q_ref, kp_ref, kc_ref, vp_ref, vc_ref, km_ref, vm_ref, s_ref, o_ref):
        n = pl.program_id(0)
        args = _attn_load(q_ref, kp_ref, kc_ref, vp_ref, vc_ref, km_ref, vm_ref)
        outs = _attn_fn(*args, s_ref[...], n)
        for kh in range(KV_HEADS):
            for g in range(GROUP):
                hh = kh * GROUP + g
                o_ref[:, hh * HEAD:(hh + 1) * HEAD] = outs[kh][g * BLK:(g + 1) * BLK]

    return pl.pallas_call(
        body, grid=(NB,), in_specs=_attn_specs(),
        out_specs=pl.BlockSpec((BLK, D_MODEL), lambda n: (n, 0)),
        out_shape=SDS((T, D_MODEL), F32), name="attn_fwd")(proj, proj, proj, proj, proj, proj, proj, sinks)


def _attn_bwd(proj, sinks, do, dproj):
    def body(q_ref, kp_ref, kc_ref, vp_ref, vc_ref, km_ref, vm_ref, s_ref, do_ref, _, dq_ref, dk_ref, dv_ref, ds_ref):
        n = pl.program_id(0)

        @pl.when(n == 0)
        def _():
            dk_ref[...] = jnp.zeros_like(dk_ref)
            dv_ref[...] = jnp.zeros_like(dv_ref)
            ds_ref[...] = jnp.zeros_like(ds_ref)

        args = _attn_load(q_ref, kp_ref, kc_ref, vp_ref, vc_ref, km_ref, vm_ref)
        _, vjp = jax.vjp(lambda a, b, c, d, e, f: _attn_fn(a, b, c, d, e, f, n), *args, s_ref[...])
        do_f = do_ref[...].astype(F32)
        cot = [jnp.concatenate([do_f[:, (kh * GROUP + g) * HEAD:(kh * GROUP + g + 1) * HEAD] for g in range(GROUP)],
                               axis=0) for kh in range(KV_HEADS)]
        dq4s, dkcats, dvcats, dkms, dvms, dsk = vjp(cot)
        ds_ref[0:1, :] += dsk
        cur = pl.ds(pl.multiple_of(n * BLK, BLK), BLK)
        meta = slice(PAD, PAD + N_META)
        for kh in range(KV_HEADS):
            cs = slice(kh * HEAD, (kh + 1) * HEAD)
            for g in range(GROUP):
                hh = kh * GROUP + g
                dq_ref[:, hh * HEAD:(hh + 1) * HEAD] = dq4s[kh][g * BLK:(g + 1) * BLK].astype(BF16)
            dk_ref[cur, cs] += dkcats[kh][BLK:]
            dv_ref[cur, cs] += dvcats[kh][BLK:]
            dk_ref[meta, cs] += dkms[kh]
            dv_ref[meta, cs] += dvms[kh]

        @pl.when(n > 0)
        def _():
            prv = pl.ds(pl.multiple_of((n - 1) * BLK, BLK), BLK)
            for kh in range(KV_HEADS):
                cs = slice(kh * HEAD, (kh + 1) * HEAD)
                dk_ref[prv, cs] += dkcats[kh][:BLK]
                dv_ref[prv, cs] += dvcats[kh][:BLK]

    full_kv = pl.BlockSpec((T, KV_W), lambda n: (0, 0))
    return pl.pallas_call(
        body, grid=(NB,),
        in_specs=_attn_specs() + [pl.BlockSpec((BLK, D_MODEL), lambda n: (n, 0)), ANY],
        out_specs=[pl.BlockSpec((BLK, D_MODEL), lambda n: (n, C_Q // D_MODEL)), full_kv, full_kv,
                   pl.BlockSpec((8, 128), lambda n: (0, 0))],
        out_shape=[SDS((T, PW), BF16), SDS((T, KV_W), F32), SDS((T, KV_W), F32), SDS((8, 128), F32)],
        input_output_aliases={9: 0},
        name="attn_bwd")(proj, proj, proj, proj, proj, proj, proj, sinks, do, dproj)


def _conv_taps(xp, w, rows):
    return (w[0:1] * xp[5:5 + rows] + w[1:2] * xp[6:6 + rows] + w[2:3] * xp[7:7 + rows] + w[3:4] * xp[8:8 + rows])


HPG = SSM_HEADS // SSM_GROUPS


def _iota(shape, dim):
    return lax.broadcasted_iota(jnp.int32, shape, dim)


def _mm(a, b, ca=1, cb=0):
    return lax.dot_general(a.astype(BF16), b.astype(BF16), (((ca,), (cb,)), ((), ())), preferred_element_type=F32)


def _split3(v):
    hi = v.astype(BF16)
    r1 = v - hi.astype(F32)
    mid = r1.astype(BF16)
    lo = (r1 - mid.astype(F32)).astype(BF16)
    return hi, mid, lo


def _split2(v):
    hi = v.astype(BF16)
    return hi, (v - hi.astype(F32)).astype(BF16)


def _sel_r(parts, onehot, ca=1, cb=0):
    out = lax.dot_general(parts[0], onehot, (((ca,), (cb,)), ((), ())), preferred_element_type=F32)
    for p in parts[1:]:
        out = out + lax.dot_general(p, onehot, (((ca,), (cb,)), ((), ())), preferred_element_type=F32)
    return out


def _sel_l(onehot, parts):
    out = jnp.dot(onehot, parts[0], preferred_element_type=F32)
    for p in parts[1:]:
        out = out + jnp.dot(onehot, p, preferred_element_type=F32)
    return out


def _rows8(*rows):
    r = _iota((8, rows[0].shape[1]), 0)
    out = jnp.zeros((8, rows[0].shape[1]), F32)
    for k, v in enumerate(rows):
        out = jnp.where(r == k, v, out)
    return out


def _ssd_forward(x, z, bm, cm, dt_raw, st_prev, dtb, alog, dskip, gn, g, cst_scr):
    li, si = _iota((BLK, BLK), 0), _iota((BLK, BLK), 1)
    dt_all = jax.nn.softplus(dt_raw + dtb)
    a_row = -jnp.exp(alog)
    a_all = dt_all * a_row
    cs_all = _sel_l((li >= si).astype(BF16), _split3(a_all))
    cs_parts = _split3(cs_all)
    spread = (_iota((BLK, GRP_W), 0) == g * HPG + jnp.right_shift(_iota((BLK, GRP_W), 1), 6)).astype(BF16)
    dt_e = _sel_r(_split2(dt_all), spread)
    cs_e = _sel_r(cs_parts, spread)
    d_e = _sel_r(_split2(_rows8(dskip)), spread)[0:1]
    cs_last_e = jnp.sum(jnp.where(_iota((BLK, GRP_W), 0) == BLK - 1, cs_e, 0.0), axis=0, keepdims=True)
    p_e = jnp.exp(cs_e)
    w_e = jnp.exp(cs_last_e - cs_e)
    cd_e = jnp.exp(cs_last_e)
    xr = x * dt_e
    cst_scr[...] = cs_all.T
    cst_g = cst_scr[g * HPG:(g + 1) * HPG, :]
    own = jnp.right_shift(_iota((HPG, HPG * BLK), 1), 7) == _iota((HPG, HPG * BLK), 0)
    ownf = own.astype(F32)
    q_rows = [ownf, ownf, ownf] + [jnp.where(own, jnp.concatenate([p.astype(F32)] * HPG, axis=1), 0.0)
                                   for p in _split3(cst_g)]
    q2 = jnp.concatenate(q_rows + [jnp.zeros((BLK - 6 * HPG, HPG * BLK), F32)], axis=0).astype(BF16)
    lane1 = _iota((1, BLK), 1)
    p2 = jnp.where((lane1 >= 3 * HPG) & (lane1 < 6 * HPG), -1.0, 0.0)
    for k, part in enumerate(cs_parts):
        pick = ((li == g * HPG + si - k * HPG) & (si >= k * HPG) & (si < (k + 1) * HPG)).astype(BF16)
        p2 = p2 + jnp.dot(part, pick, preferred_element_type=F32)
    dmat = jnp.dot(p2.astype(BF16), q2, preferred_element_type=F32)
    causal = _iota((BLK, HPG * BLK), 0) >= jnp.bitwise_and(_iota((BLK, HPG * BLK), 1), BLK - 1)
    lam = jnp.exp(jnp.where(causal, dmat, NEG))
    gmat = _mm(cm, bm, 1, 1)
    m_all = lam * jnp.concatenate([gmat] * HPG, axis=1)
    mb = m_all.astype(BF16)
    lo = _iota((BLK, BLK), 1) < HEAD
    xrb = xr.astype(BF16)
    zero = jnp.zeros((BLK, BLK), BF16)
    bds, yd = [], []
    for i in range(HPG // 2):
        t = xrb[:, BLK * i:BLK * (i + 1)]
        bd = jnp.concatenate([jnp.where(lo, t, zero), jnp.where(lo, zero, t)], axis=0)
        bds.append(bd)
        yd.append(jnp.dot(mb[:, 2 * BLK * i:2 * BLK * (i + 1)], bd, preferred_element_type=F32))
    cs_st = _mm(cm, st_prev)
    y = jnp.concatenate(yd, axis=1) + cs_st * p_e + d_e * x
    xrw = xr * w_e
    st_new = cd_e * st_prev + _mm(bm, xrw, 0, 0)
    yz = y * _silu(z)
    rn = lax.rsqrt(jnp.sum(yz * yz, axis=1, keepdims=True) / GRP_W + EPS)
    return dict(out=yz * rn * gn, st_new=st_new, dt_all=dt_all, a_row=a_row, dt_e=dt_e, d_e=d_e, p_e=p_e, w_e=w_e,
                cd_e=cd_e, xr=xr, xrw=xrw, lam=lam, m_all=m_all, mb=mb, bds=bds, cs_st=cs_st, y=y, yz=yz, rn=rn, lo=lo)


def _ssd_backward(f, x, z, bm, cm, dt_raw, st_prev, dtb, gn, g, dout, dst_next, cst_scr):
    li, si = _iota((BLK, BLK), 0), _iota((BLK, BLK), 1)
    yz, rn, y, p_e, w_e, cd_e, xr = f["yz"], f["rn"], f["y"], f["p_e"], f["w_e"], f["cd_e"], f["xr"]
    dgn = jnp.sum(dout * yz * rn, axis=0, keepdims=True)
    t = dout * gn
    dyz = rn * t - yz * (rn * rn * rn) * (jnp.sum(yz * t, axis=1, keepdims=True) / GRP_W)
    dy = dyz * _silu(z)
    dz = dyz * y * _dsilu(z)
    dx = f["d_e"] * dy
    dd_e = jnp.sum(dy * x, axis=0, keepdims=True)
    dcsst = dy * p_e
    dp_e = dy * f["cs_st"]
    dcm = _mm(dcsst, st_prev, 1, 1)
    dst_prev = _mm(cm, dcsst, 0, 0) + cd_e * dst_next
    dcd_e = jnp.sum(dst_next * st_prev, axis=0, keepdims=True)
    dbm = _mm(f["xrw"], dst_next, 1, 1)
    dxrw = _mm(bm, dst_next)
    dxr = dxrw * w_e
    dw_e = dxrw * xr
    dyb = dy.astype(BF16)
    dms, dxr_d = [], []
    for i in range(HPG // 2):
        dyp = dyb[:, BLK * i:BLK * (i + 1)]
        dms.append(lax.dot_general(dyp, f["bds"][i], (((1,), (1,)), ((), ())), preferred_element_type=F32))
        r = lax.dot_general(f["mb"][:, 2 * BLK * i:2 * BLK * (i + 1)], dyp, (((0,), (0,)), ((), ())),
                            preferred_element_type=F32)
        dxr_d.append(jnp.where(f["lo"], r[0:BLK], r[BLK:2 * BLK]))
    dm_all = jnp.concatenate(dms, axis=1)
    dxr = dxr + jnp.concatenate(dxr_d, axis=1)
    dlg = dm_all * f["lam"]
    dg = dlg[:, 0:BLK]
    for j in range(1, HPG):
        dg = dg + dlg[:, BLK * j:BLK * (j + 1)]
    dcm = dcm + _mm(dg, bm)
    dbm = dbm + _mm(dg, cm, 0, 0)
    q_all = dm_all * f["m_all"]
    col_sums = jnp.sum(q_all, axis=0, keepdims=True)
    cst_scr[...] = jnp.zeros_like(cst_scr)
    cst_scr[g * HPG:(g + 1) * HPG, :] = _rows8(
        *[col_sums[:, BLK * j:BLK * (j + 1)] for j in range(HPG)])
    dcs = -cst_scr[...].T
    for j in range(HPG):
        dcs = dcs + jnp.where(si == g * HPG + j,
                              jnp.sum(q_all[:, BLK * j:BLK * (j + 1)], axis=1, keepdims=True), 0.0)
    unspread = (_iota((GRP_W, BLK), 1) == g * HPG + jnp.right_shift(_iota((GRP_W, BLK), 0), 6)).astype(BF16)
    dww = dw_e * w_e
    per_head = _sel_r(_split2(jnp.concatenate([dp_e * p_e - dww, dxr * x], axis=0)), unspread)
    last = _sel_r(_split2(_rows8(jnp.sum(dww, axis=0, keepdims=True) + dcd_e * cd_e, dd_e)), unspread)
    dcs = dcs + per_head[0:BLK] + jnp.where(li == BLK - 1, last[0:1], 0.0)
    da = _sel_l((si >= li).astype(BF16), _split2(dcs))
    ddt_all = da * f["a_row"] + per_head[BLK:2 * BLK]
    dalog = jnp.sum(da * f["dt_all"], axis=0, keepdims=True) * f["a_row"]
    dx = dx + dxr * f["dt_e"]
    ddt_raw = ddt_all * jax.nn.sigmoid(dt_raw + dtb)
    ddtb = jnp.sum(ddt_raw, axis=0, keepdims=True)
    ddskip = last[1:2]
    return dict(dx=dx, dz=dz, dbm=dbm, dcm=dcm, ddt_raw=ddt_raw, dst_prev=dst_prev, ddtb=ddtb, dalog=dalog,
                ddskip=ddskip, dgn=dgn)


ZX_W = SSM_INNER + CONV_DIM
assert C_ZS == 0 and C_XBC == SSM_INNER


def _ssd_in_specs(rev):
    cidx = (lambda c: NB - 1 - c) if rev else (lambda c: c)
    return [
        pl.BlockSpec((BLK, ZX_W), lambda c: (cidx(c), 0)),
        pl.BlockSpec((8, ZX_W), lambda c: (jnp.maximum(cidx(c) * (BLK // 8) - 1, 0), 0)),
        pl.BlockSpec((BLK, 128), lambda c: (cidx(c), C_DT // 128)),
        pl.BlockSpec((8, CONV_DIM), lambda c: (0, 0)),
        pl.BlockSpec((1, CONV_DIM), lambda c: (0, 0)),
        pl.BlockSpec((1, 128), lambda c: (0, 0)),
        pl.BlockSpec((1, 128), lambda c: (0, 0)),
        pl.BlockSpec((1, 128), lambda c: (0, 0)),
        pl.BlockSpec((1, SSM_INNER), lambda c: (0, 0)),
    ]


def _xbc_act(zx_ref, tail_ref, w_ref, b_ref, n):
    tail = jnp.where(n > 0, tail_ref[:, SSM_INNER:], 0.0)
    xp = jnp.concatenate([tail, zx_ref[:, SSM_INNER:]], axis=0)
    conv = _conv_taps(xp, w_ref[...], BLK) + b_ref[...]
    valid = n * BLK + _iota((BLK, 1), 0) >= PAD
    return xp, conv, valid, jnp.where(valid, _silu(conv), 0.0)


def _grp_cols(act, i):
    b0, c0 = SSM_INNER + i * SSM_STATE, SSM_INNER + (SSM_GROUPS + i) * SSM_STATE
    return act[:, i * GRP_W:(i + 1) * GRP_W], act[:, b0:b0 + SSM_STATE], act[:, c0:c0 + SSM_STATE]


def _ssd_fwd(proj, conv_w, conv_b, dt_bias, a_log, d_skip, g_norm, gather=()):
    ng = len(gather)

    def body(*refs):
        zx_ref, tail_ref, dt_ref, w_ref, b_ref, dtb_ref, al_ref, dsk_ref, gn_ref = refs[:9]
        y_ref, st_ref = refs[9 + ng:11 + ng]
        s_scr, cst_scr = refs[11 + 2 * ng:13 + 2 * ng]
        c = pl.program_id(0)
        if ng:
            ag_start, ag_forward, ag_finish = _ag_program(refs[9:9 + ng], refs[11 + ng:11 + 2 * ng],
                                                          refs[13 + 2 * ng:])
            pl.when(c == 0)(ag_start)
            pl.when(c == (3 * NB) // 4)(ag_forward)

        @pl.when(c == 0)
        def _():
            s_scr[...] = jnp.zeros_like(s_scr)

        _, _, _, act = _xbc_act(zx_ref, tail_ref, w_ref, b_ref, c)
        for i in range(SSM_GROUPS):
            st_prev = s_scr[i]
            st_ref[i, 0] = st_prev
            x, bm, cm = _grp_cols(act, i)
            f = _ssd_forward(x, zx_ref[:, i * GRP_W:(i + 1) * GRP_W], bm, cm, dt_ref[...], st_prev, dtb_ref[...],
                             al_ref[...], dsk_ref[...], gn_ref[:, i * GRP_W:(i + 1) * GRP_W], i, cst_scr.at[i])
            y_ref[:, i * GRP_W:(i + 1) * GRP_W] = f["out"].astype(BF16)
            s_scr[i] = f["st_new"]
        if ng:
            pl.when(c == NB - 1)(ag_finish)

    return pl.pallas_call(
        body, grid=(NB,), in_specs=_ssd_in_specs(False) + [ANY] * ng,
        out_specs=[pl.BlockSpec((BLK, SSM_INNER), lambda c: (c, 0)),
                   pl.BlockSpec((SSM_GROUPS, 1, SSM_STATE, GRP_W), lambda c: (0, c, 0, 0))] + [ANY] * ng,
        out_shape=[SDS((T, SSM_INNER), BF16), SDS((SSM_GROUPS, NB, SSM_STATE, GRP_W), F32)]
        + [SDS((N_DEV,) + s.shape, s.dtype) for s in gather],
        scratch_shapes=[pltpu.VMEM((SSM_GROUPS, SSM_STATE, GRP_W), F32), pltpu.VMEM((SSM_GROUPS, BLK, BLK), F32)]
        + (_ag_scratch(gather) if ng else []),
        compiler_params=_cparams(),
        name="ssd_fwd")(proj, proj, proj, conv_w, conv_b, dt_bias, a_log, d_skip, g_norm, *gather)


def _ssd_bwd(proj, conv_w, conv_b, dt_bias, a_log, d_skip, g_norm, states, dy, dproj, exchange=()):
    ne = len(exchange)

    def body(*refs):
        zx_ref, tail_ref, dt_ref, w_ref, b_ref, dtb_ref, al_ref, dsk_ref, gn_ref, st_ref, dy_ref = refs[:11]
        (ddt_ref, dp_ref, ddtb_ref, dal_ref, ddsk_ref, dgn_ref, dcw_ref, dcb_ref) = refs[12 + ne:20 + ne]
        ds_scr, cst_scr, carry = refs[20 + 2 * ne:23 + 2 * ne]
        c = pl.program_id(0)
        n = NB - 1 - c
        if ne:
            ex_start, ex_finish = _direct_program(refs[12:12 + ne], refs[20 + ne:20 + 2 * ne], refs[23 + 2 * ne:])
            pl.when(c == 0)(ex_start)

        @pl.when(c == 0)
        def _():
            for ref in (ds_scr, carry, dgn_ref, ddtb_ref, dal_ref, ddsk_ref, dcw_ref, dcb_ref):
                ref[...] = jnp.zeros_like(ref)

        xp, conv, valid, act = _xbc_act(zx_ref, tail_ref, w_ref, b_ref, n)
        dt_raw = dt_ref[...]
        dxs, dbs, dcs = [], [], []
        for i in range(SSM_GROUPS):
            x, bm, cm = _grp_cols(act, i)
            z, gn, st_prev = zx_ref[:, i * GRP_W:(i + 1) * GRP_W], gn_ref[:, i * GRP_W:(i + 1) * GRP_W], st_ref[i, 0]
            f = _ssd_forward(x, z, bm, cm, dt_raw, st_prev, dtb_ref[...], al_ref[...], dsk_ref[...], gn, i,
                             cst_scr.at[i])
            d = _ssd_backward(f, x, z, bm, cm, dt_raw, st_prev, dtb_ref[...], gn, i,
                              dy_ref[:, i * GRP_W:(i + 1) * GRP_W].astype(F32), ds_scr[i], cst_scr.at[i])
            dxs.append(d["dx"])
            dbs.append(d["dbm"])
            dcs.append(d["dcm"])
            dp_ref[:, i * GRP_W:(i + 1) * GRP_W] = d["dz"].astype(BF16)
            ds_scr[i] = d["dst_prev"]
            ddt_ref[:, i * 128:(i + 1) * 128] = d["ddt_raw"]
            dgn_ref[0:1, i * GRP_W:(i + 1) * GRP_W] += d["dgn"]
            ddtb_ref[0:1, :] += d["ddtb"]
            dal_ref[0:1, :] += d["dalog"]
            ddsk_ref[0:1, :] += d["ddskip"]
        dconv = jnp.where(valid, jnp.concatenate(dxs + dbs + dcs, axis=1) * _dsilu(conv), 0.0)
        dext = jnp.concatenate([dconv, carry[...]], axis=0)
        w = w_ref[...]
        dp_ref[:, SSM_INNER:] = (w[0:1] * dext[3:3 + BLK] + w[1:2] * dext[2:2 + BLK] + w[2:3] * dext[1:1 + BLK]
                                 + w[3:4] * dext[0:BLK]).astype(BF16)
        carry[...] = dconv[0:8]
        dcw_ref[...] += jnp.concatenate(
            [jnp.sum(dconv * xp[5 + k:5 + k + BLK], axis=0, keepdims=True) for k in range(4)]
            + [jnp.zeros((4, CONV_DIM), F32)], axis=0)
        dcb_ref[0:1, :] += jnp.sum(dconv, axis=0, keepdims=True)
        if ne:
            pl.when(c == NB - 1)(ex_finish)

    rc = lambda c: NB - 1 - c
    small = pl.BlockSpec((8, 128), lambda c: (0, 0))
    wide = lambda w: pl.BlockSpec((8, w), lambda c: (0, 0))
    return pl.pallas_call(
        body, grid=(NB,),
        in_specs=_ssd_in_specs(True) + [
            pl.BlockSpec((SSM_GROUPS, 1, SSM_STATE, GRP_W), lambda c: (0, rc(c), 0, 0)),
            pl.BlockSpec((BLK, SSM_INNER), lambda c: (rc(c), 0)), ANY] + [ANY] * ne,
        out_specs=[pl.BlockSpec((BLK, SSM_GROUPS * 128), lambda c: (rc(c), 0)),
                   pl.BlockSpec((BLK, ZX_W), lambda c: (rc(c), 0)),
                   small, small, small, wide(SSM_INNER), wide(CONV_DIM), wide(CONV_DIM)] + [ANY] * ne,
        out_shape=[SDS((T, GRP_W), F32), SDS((T, PW), BF16), SDS((8, 128), F32), SDS((8, 128), F32),
                   SDS((8, 128), F32), SDS((8, SSM_INNER), F32), SDS((8, CONV_DIM), F32), SDS((8, CONV_DIM), F32)]
        + [SDS(p.shape, p.dtype) for p in exchange],
        scratch_shapes=[pltpu.VMEM((SSM_GROUPS, SSM_STATE, GRP_W), F32), pltpu.VMEM((SSM_GROUPS, BLK, BLK), F32),
                        pltpu.VMEM((8, CONV_DIM), F32)] + (_direct_scratch(exchange) if ne else []),
        input_output_aliases={11: 1},
        compiler_params=_cparams(),
        name="ssd_bwd")(proj, proj, proj, conv_w, conv_b, dt_bias, a_log, d_skip, g_norm, states, dy, dproj,
                        *exchange)


POST_R = 272


def _post_a(o, proj, sn, w_att, w_ssm, w_o):
    def body(o_ref, za_ref, ga_ref, gs_ref, sn_ref, wa_ref, ws_ref, wo_ref, a_ref, mg_ref, ya_ref, ys_ref, out_ref):
        a = (o_ref[...] * _silu(za_ref[...])).astype(BF16)
        a_ref[...] = a
        ya = jnp.dot(a, wa_ref[...], preferred_element_type=F32)
        ys = jnp.dot(sn_ref[...], ws_ref[...], preferred_element_type=F32)
        ya_ref[...] = ya.astype(BF16)
        ys_ref[...] = ys.astype(BF16)
        mg = (jax.nn.sigmoid(ga_ref[...]) * ya + jax.nn.sigmoid(gs_ref[...]) * ys).astype(BF16)
        mg_ref[...] = mg
        out_ref[...] = jnp.dot(mg, wo_ref[...], preferred_element_type=F32)

    row = pl.BlockSpec((POST_R, D_MODEL), lambda i: (i, 0))
    pcol = lambda c0: pl.BlockSpec((POST_R, D_MODEL), lambda i: (i, c0 // D_MODEL))
    full = lambda r: pl.BlockSpec((r, D_MODEL), lambda i: (0, 0))
    return pl.pallas_call(
        body, grid=(T // POST_R,),
        in_specs=[row, pcol(C_ZA), pcol(C_GA), pcol(C_GS), pl.BlockSpec((POST_R, SSM_INNER), lambda i: (i, 0)),
                  full(D_MODEL), full(SSM_INNER), full(D_MODEL)],
        out_specs=[row, row, row, row, row],
        out_shape=[SDS((T, D_MODEL), BF16), SDS((T, D_MODEL), BF16), SDS((T, D_MODEL), BF16), SDS((T, D_MODEL), BF16),
                   SDS((T, D_MODEL), F32)],
        compiler_params=_cparams(), name="post_a")(o, proj, proj, proj, sn, w_att, w_ssm, w_o)


def _post_b(out, h, tgt, proj, ya, ys, o, g_post, w_att, w_ssm, w_o):
    def body(out_ref, h_ref, t_ref, za_ref, ga_ref, gs_ref, ya_ref, ys_ref, o_ref, gp_ref, wa_ref, ws_ref, wo_ref,
             loss_ref, dres_ref, dout_ref, dya_ref, dys_ref, do_ref, dp_ref, dsn_ref, dgp_ref):
        i = pl.program_id(0)
        x = out_ref[...]
        gp = gp_ref[...]
        r = lax.rsqrt(jnp.mean(x * x, axis=-1, keepdims=True) + EPS)
        row = i * POST_R + lax.broadcasted_iota(jnp.int32, (POST_R, 1), 0)
        res = h_ref[...] + jnp.where(row >= PAD, x * r * gp, 0.0)
        live = row >= PAD + N_META
        err = jnp.where(live, res - t_ref[...], 0.0)
        lpart = 0.5 * jnp.sum(jnp.sum(err * err, axis=1, keepdims=True) / D_MODEL, axis=0, keepdims=True)
        dres = err / D_MODEL
        dres_ref[...] = dres
        gpart = jnp.sum(dres * x * r, axis=0, keepdims=True)

        @pl.when(i == 0)
        def _():
            loss_ref[...] = jnp.zeros_like(loss_ref)
            dgp_ref[...] = jnp.zeros_like(dgp_ref)

        loss_ref[...] += jnp.broadcast_to(lpart, loss_ref.shape)
        dgp_ref[0:1, :] += gpart
        gd = gp * dres
        dout = (r * gd - x * (r * r * r) * jnp.mean(x * gd, axis=-1, keepdims=True)).astype(BF16)
        dout_ref[...] = dout
        dmg = lax.dot_general(dout, wo_ref[...], (((1,), (1,)), ((), ())), preferred_element_type=F32)
        sga = jax.nn.sigmoid(ga_ref[...])
        sgs = jax.nn.sigmoid(gs_ref[...])
        dya = (dmg * sga).astype(BF16)
        dys = (dmg * sgs).astype(BF16)
        dya_ref[...] = dya
        dys_ref[...] = dys
        dp_ref[:, C_GA - C_ZA:C_GA - C_ZA + D_MODEL] = (dmg * ya_ref[...].astype(F32) * sga * (1.0 - sga)).astype(BF16)
        dp_ref[:, C_GS - C_ZA:C_GS - C_ZA + D_MODEL] = (dmg * ys_ref[...].astype(F32) * sgs * (1.0 - sgs)).astype(BF16)
        da = lax.dot_general(dya, wa_ref[...], (((1,), (1,)), ((), ())), preferred_element_type=F32)
        za = za_ref[...]
        do_ref[...] = (da * _silu(za)).astype(BF16)
        dp_ref[:, 0:D_MODEL] = (da * o_ref[...] * _dsilu(za)).astype(BF16)
        dsn_ref[...] = lax.dot_general(dys, ws_ref[...], (((1,), (1,)), ((), ())),
                                       preferred_element_type=F32).astype(BF16)

    row = pl.BlockSpec((POST_R, D_MODEL), lambda i: (i, 0))
    pcol = lambda c0: pl.BlockSpec((POST_R, D_MODEL), lambda i: (i, c0 // D_MODEL))
    full = lambda r: pl.BlockSpec((r, D_MODEL), lambda i: (0, 0))
    small = pl.BlockSpec((8, D_MODEL), lambda i: (0, 0))
    return pl.pallas_call(
        body, grid=(T // POST_R,),
        in_specs=[row, row, row, pcol(C_ZA), pcol(C_GA), pcol(C_GS), row, row, row,
                  pl.BlockSpec((1, D_MODEL), lambda i: (0, 0)), full(D_MODEL), full(SSM_INNER), full(D_MODEL)],
        out_specs=[pl.BlockSpec((8, 128), lambda i: (0, 0)), row, row, row, row, row,
                   pl.BlockSpec((POST_R, GATES_W), lambda i: (i, C_ZA // GATES_W)),
                   pl.BlockSpec((POST_R, SSM_INNER), lambda i: (i, 0)), small],
        out_shape=[SDS((8, 128), F32), SDS((T, D_MODEL), F32), SDS((T, D_MODEL), BF16), SDS((T, D_MODEL), BF16),
                   SDS((T, D_MODEL), BF16), SDS((T, D_MODEL), BF16), SDS((T, PW), BF16),
                   SDS((T, SSM_INNER), BF16), SDS((8, D_MODEL), F32)],
        compiler_params=_cparams(), name="post_b")(out, h, tgt, proj, proj, proj, ya, ys, o, g_post, w_att, w_ssm, w_o)


TAIL_W = PW - C_K


def _dproj_tail(dproj, dk, dv, ddt4):
    rows = T // 4

    def body(_, dk_ref, dv_ref, ddt_ref, o_ref, buf, sem):
        n = pl.program_id(0)
        d4 = ddt_ref[...]
        buf[:, 0:KV_W] = dk_ref[...].astype(BF16)
        buf[:, KV_W:2 * KV_W] = dv_ref[...].astype(BF16)
        buf[:, 2 * KV_W:TAIL_W] = (d4[:, 0:128] + d4[:, 128:256] + d4[:, 256:384] + d4[:, 384:512]).astype(BF16)
        cp = pltpu.make_async_copy(buf, o_ref.at[pl.ds(pl.multiple_of(n * rows, 16), rows), pl.ds(C_K, TAIL_W)], sem)
        cp.start()
        cp.wait()

    spec = lambda w: pl.BlockSpec((rows, w), lambda i: (i, 0))
    return pl.pallas_call(
        body, grid=(T // rows,), in_specs=[ANY, spec(KV_W), spec(KV_W), spec(GRP_W)], out_specs=ANY,
        out_shape=SDS((T, PW), BF16), input_output_aliases={0: 0},
        scratch_shapes=[pltpu.VMEM((rows, TAIL_W), BF16), pltpu.SemaphoreType.DMA],
        name="dproj_tail")(dproj, dk, dv, ddt4)


def _adamw_math(w, g, m, v):
    m = ADAM_B1 * m + (1.0 - ADAM_B1) * g
    v = ADAM_B2 * v + (1.0 - ADAM_B2) * (g * g)
    m_hat = m / (1.0 - ADAM_B1 ** ADAM_STEP)
    v_hat = v / (1.0 - ADAM_B2 ** ADAM_STEP)
    delta = -ADAM_LR * (m_hat / (jnp.sqrt(v_hat) + ADAM_EPS) + ADAM_WD * w)
    return delta, m, v


def _sum_adamw(recv, w, m, v, tc, name):
    rows, cols = w.shape
    nslab = recv.shape[0]
    assert cols % tc == 0

    def body(r_ref, w_ref, m_ref, v_ref, g_ref, d_ref, nm_ref, nv_ref):
        g = r_ref[0].astype(F32)
        for d in range(1, nslab):
            g = g + r_ref[d].astype(F32)
        g_ref[...] = g
        delta, nm, nv = _adamw_math(w_ref[...], g, m_ref[...], v_ref[...])
        d_ref[...] = delta
        nm_ref[...] = nm
        nv_ref[...] = nv

    blk = pl.BlockSpec((rows, tc), lambda i: (0, i))
    return pl.pallas_call(
        body, grid=(cols // tc,),
        in_specs=[pl.BlockSpec((nslab, rows, tc), lambda i: (0, 0, i)), blk, blk, blk],
        out_specs=[blk, blk, blk, blk], out_shape=[SDS((rows, cols), F32)] * 4,
        compiler_params=_cparams(), name=name)(recv, w, m, v)


def _sum_adamw_rows3(recv, w3, m3, v3, name, exchange=()):
    pairs = 61
    assert (SHARD_IN // 2) % pairs == 0
    nsteps = SHARD_IN // 2 // pairs
    ne = len(exchange)

    def body(*refs):
        r_ref, w_ref, m_ref, v_ref = refs[:4]
        g_ref, d_ref, nm_ref, nv_ref = refs[4 + ne:8 + ne]
        if ne:
            ex_start, ex_finish = _direct_program(refs[4:4 + ne], refs[8 + ne:8 + 2 * ne], refs[8 + 2 * ne:])
            pl.when(pl.program_id(0) == 0)(ex_start)
        g = r_ref[0].astype(F32)
        for d in range(1, N_CHIP):
            g = g + r_ref[d].astype(F32)
        g = g.reshape(2 * pairs, ROW_TILES, 128)
        g_ref[...] = g
        delta, nm, nv = _adamw_math(w_ref[...], g, m_ref[...], v_ref[...])
        d_ref[...] = delta
        nm_ref[...] = nm
        nv_ref[...] = nv
        if ne:
            pl.when(pl.program_id(0) == nsteps - 1)(ex_finish)

    blk = pl.BlockSpec((2 * pairs, ROW_TILES, 128), lambda i: (i, 0, 0))
    return pl.pallas_call(
        body, grid=(nsteps,),
        in_specs=[pl.BlockSpec((N_CHIP, pairs, 2 * ROW_TILES, 128), lambda i: (0, i, 0, 0)), blk, blk, blk]
        + [ANY] * ne,
        out_specs=[blk, blk, blk, blk] + [ANY] * ne,
        out_shape=[SDS(w3.shape, F32)] * 4 + [SDS(p.shape, p.dtype) for p in exchange],
        scratch_shapes=_direct_scratch(exchange) if ne else [],
        compiler_params=_cparams(), name=name)(recv, w3, m3, v3, *exchange)


ROW_GPRE, ROW_CONVB, ROW_DTB, ROW_ALOG, ROW_DSKIP, ROW_SINK, ROW_GSSM, ROW_GPOST = 0, 1, 4, 5, 6, 7, 8, 10
ROW_LOSS = 11
REP_ROWS, ROW_CONVW, ROW_META, SM_ROWS = 16, 16, 24, 40
CW_SHARD = CONV_DIM // N_DEV
META_SHARD = D_MODEL // N_DEV


def _small_pack(dgpre, db, ddtb, dal, ddsk, dsink, dgn, dgp, dw, loss, dh):
    def body(dgpre_ref, db_ref, ddtb_ref, dal_ref, ddsk_ref, dsink_ref, dgn_ref, dgp_ref, dw_ref, loss_ref, dh_ref,
             o_ref, rep):
        rep[...] = jnp.zeros_like(rep)
        rep[ROW_LOSS:ROW_LOSS + 1, 0:128] = loss_ref[0:1, :]
        rep[ROW_GPRE:ROW_GPRE + 1, :] = dgpre_ref[0:1, :]
        for k in range(3):
            rep[ROW_CONVB + k:ROW_CONVB + k + 1, :] = db_ref[0:1, 1024 * k:1024 * (k + 1)]
        rep[ROW_DTB:ROW_DTB + 1, 0:128] = ddtb_ref[0:1, :]
        rep[ROW_ALOG:ROW_ALOG + 1, 0:128] = dal_ref[0:1, :]
        rep[ROW_DSKIP:ROW_DSKIP + 1, 0:128] = ddsk_ref[0:1, :]
        rep[ROW_SINK:ROW_SINK + 1, 0:128] = dsink_ref[0:1, :]
        rep[ROW_GSSM:ROW_GSSM + 1, :] = dgn_ref[0:1, 0:1024]
        rep[ROW_GSSM + 1:ROW_GSSM + 2, :] = dgn_ref[0:1, 1024:2048]
        rep[ROW_GPOST:ROW_GPOST + 1, :] = dgp_ref[0:1, :]
        cw = dw_ref[...]
        mh = dh_ref[...]
        o_ref[...] = jnp.zeros_like(o_ref)
        for p in range(N_DEV):
            o_ref[p, 0:REP_ROWS, :] = rep[...]
            o_ref[p, ROW_CONVW:ROW_CONVW + 8, 0:CW_SHARD] = cw[:, p * CW_SHARD:(p + 1) * CW_SHARD]
            o_ref[p, ROW_META:ROW_META + N_META, 0:META_SHARD] = mh[:, p * META_SHARD:(p + 1) * META_SHARD]

    ins = [dgpre, db, ddtb, dal, ddsk, dsink, dgn, dgp, dw, loss]
    return pl.pallas_call(
        body, grid=(1,),
        in_specs=[pl.BlockSpec(a.shape, lambda i: (0, 0)) for a in ins]
        + [pl.BlockSpec((N_META, D_MODEL), lambda i: (PAD // N_META, 0))],
        out_specs=pl.BlockSpec((N_DEV, SM_ROWS, 1024), lambda i: (0, 0, 0)),
        out_shape=SDS((N_DEV, SM_ROWS, 1024), F32), scratch_shapes=[pltpu.VMEM((REP_ROWS, 1024), F32)],
        name="small_pack")(*ins, dh)


def _small_finish(recv, params):
    npar = len(params)

    def body(*refs):
        r_ref = refs[0]
        wmv = refs[1:1 + 3 * npar]
        outs = refs[1 + 3 * npar:1 + 7 * npar]
        loss_ref = refs[1 + 7 * npar]
        gs = refs[-1]
        g = r_ref[0]
        for d in range(1, recv.shape[0]):
            g = g + r_ref[d]
        gs[...] = g
        loss_ref[...] = gs[ROW_LOSS:ROW_LOSS + 1, 0:128]
        grads = [
            gs[ROW_GPRE:ROW_GPRE + 1, :],
            jnp.concatenate([gs[ROW_CONVB + k:ROW_CONVB + k + 1, :] for k in range(3)], axis=1),
            gs[ROW_DTB:ROW_DTB + 1, 0:SSM_HEADS], gs[ROW_ALOG:ROW_ALOG + 1, 0:SSM_HEADS],
            gs[ROW_DSKIP:ROW_DSKIP + 1, 0:SSM_HEADS], gs[ROW_SINK:ROW_SINK + 1, 0:Q_HEADS],
            jnp.concatenate([gs[ROW_GSSM:ROW_GSSM + 1, :], gs[ROW_GSSM + 1:ROW_GSSM + 2, :]], axis=1),
            gs[ROW_GPOST:ROW_GPOST + 1, :],
            gs[ROW_CONVW:ROW_CONVW + 4, 0:CW_SHARD],
            gs[ROW_META:ROW_META + N_META, 0:META_SHARD]]
        for i in range(npar):
            w_ref, m_ref, v_ref = wmv[3 * i:3 * i + 3]
            delta, nm, nv = _adamw_math(w_ref[...], grads[i], m_ref[...], v_ref[...])
            outs[4 * i][...] = grads[i]
            outs[4 * i + 1][...] = delta
            outs[4 * i + 2][...] = nm
            outs[4 * i + 3][...] = nv

    flat = [a for wmv in params for a in wmv]
    res = pl.pallas_call(
        body, out_shape=[SDS(wmv[0].shape, F32) for wmv in params for _ in range(4)] + [SDS((1, 128), F32)],
        scratch_shapes=[pltpu.VMEM((SM_ROWS, 1024), F32)], name="small_finish")(recv, *flat)
    return [tuple(res[4 * i:4 * i + 4]) for i in range(npar)], res[4 * npar]


def _slab(ref, px, py, pc):
    return ref.at[4 * px + 2 * py + pc]


def _bounce(src, dst, buf, sem):
    cp = pltpu.make_async_copy(src, buf, sem)
    cp.start()
    cp.wait()
    cp = pltpu.make_async_copy(buf, dst, sem)
    cp.start()
    cp.wait()


def _ag_program(ins, outs, scratch):
    na = len(ins)
    send_sems, recv_sems, local_sems = scratch[:3]
    bufs = scratch[3:]
    x, y, c = lax.axis_index("x"), lax.axis_index("y"), lax.axis_index("c")
    me, sibling = (x, y, c), (x, y, 1 - c)
    chips = [(1 - x, y), (x, 1 - y), (1 - x, 1 - y)]

    def copy(a, k, block, to, src=None):
        dst = _slab(outs[a], *block)
        return pltpu.make_async_remote_copy(
            src_ref=dst if src is None else src, dst_ref=dst, send_sem=send_sems.at[a, k],
            recv_sem=recv_sems.at[a, k], device_id=to, device_id_type=MESH)

    def own_sends():
        out = []
        for a in range(na):
            out.append(copy(a, 0, me, sibling, src=ins[a]))
            out += [copy(a, 1 + j, me, (*chip, c), src=ins[a]) for j, chip in enumerate(chips)]
        return out

    def start():
        for cp in own_sends():
            cp.start()
        for a in range(na):
            _bounce(ins[a], _slab(outs[a], *me), bufs[a], local_sems.at[a])

    def forward():
        for j, chip in enumerate(chips):
            for a in range(na):
                copy(a, 1 + j, (*chip, c), me).wait_recv()
                copy(a, 4 + j, (*chip, c), sibling).start()

    def finish():
        for a in range(na):
            copy(a, 0, sibling, me).wait_recv()
            for j, chip in enumerate(chips):
                copy(a, 4 + j, (*chip, 1 - c), me).wait_recv()
        for cp in own_sends():
            cp.wait_send()
        for j, chip in enumerate(chips):
            for a in range(na):
                copy(a, 4 + j, (*chip, c), sibling).wait_send()

    return start, forward, finish


def _ag_scratch(shards):
    na = len(shards)
    return [pltpu.SemaphoreType.DMA((na, 7)), pltpu.SemaphoreType.DMA((na, 7)),
            pltpu.SemaphoreType.DMA((na,))] + [pltpu.VMEM(s.shape, s.dtype) for s in shards]


def _all_gather(shards):
    na = len(shards)

    def body(*refs):
        start, forward, finish = _ag_program(refs[:na], refs[na:2 * na], refs[2 * na:])
        start()
        forward()
        finish()

    return pl.pallas_call(
        body, in_specs=[ANY] * na, out_specs=[ANY] * na,
        out_shape=[SDS((N_DEV,) + s.shape, s.dtype) for s in shards],
        scratch_shapes=_ag_scratch(shards), name="all_gather")(*shards)


N_CHIP = 4


def _pair_sum(own, got, name):
    na = len(own)

    def body(*refs):
        for a in range(na):
            o_ref, g_ref, s_ref = refs[a], refs[na + a], refs[2 * na + a]
            s_ref[...] = (o_ref[...].astype(F32) + g_ref[...].astype(F32)).astype(s_ref.dtype)

    def spec(p):
        nd = len(p.shape) - 1
        return pl.BlockSpec((1,) + p.shape[1:], lambda k, nd=nd: (k,) + (0,) * nd)

    return pl.pallas_call(
        body, grid=(N_CHIP,), in_specs=[spec(p) for p in own] + [spec(p) for p in got],
        out_specs=[spec(p) for p in own], out_shape=[SDS(p.shape, p.dtype) for p in own],
        compiler_params=_cparams(), name=name)(*own, *got)


def _chips_program(ins, outs, scratch):
    na = len(ins)
    send_sems, recv_sems, local_sems = scratch[:3]
    bufs = scratch[3:]
    x, y, c = lax.axis_index("x"), lax.axis_index("y"), lax.axis_index("c")
    mine = 2 * x + y
    chips = [(1 - x, y), (x, 1 - y), (1 - x, 1 - y)]

    def send(a, j):
        px, py = chips[j]
        return pltpu.make_async_remote_copy(
            src_ref=ins[a].at[2 * px + py], dst_ref=outs[a].at[mine], send_sem=send_sems.at[a, j],
            recv_sem=recv_sems.at[a, j], device_id=(px, py, c), device_id_type=MESH)

    def arrival(a, j):
        px, py = chips[j]
        return pltpu.make_async_remote_copy(
            src_ref=ins[a].at[2 * px + py], dst_ref=outs[a].at[2 * px + py], send_sem=send_sems.at[a, j],
            recv_sem=recv_sems.at[a, j], device_id=(px, py, c), device_id_type=MESH)

    def start():
        for a in range(na):
            for j in range(3):
                send(a, j).start()
        for a in range(na):
            _bounce(ins[a].at[mine], outs[a].at[mine], bufs[a], local_sems.at[a])

    def finish():
        for a in range(na):
            for j in range(3):
                arrival(a, j).wait_recv()
        for a in range(na):
            for j in range(3):
                send(a, j).wait_send()

    return start, finish


def _chips_scratch(parts):
    na = len(parts)
    return [pltpu.SemaphoreType.DMA((na, 3)), pltpu.SemaphoreType.DMA((na, 3)),
            pltpu.SemaphoreType.DMA((na,))] + [pltpu.VMEM(p.shape[1:], p.dtype) for p in parts]


def _direct_program(ins, outs, scratch):
    na = len(ins)
    send_sems, recv_sems, local_sems = scratch[:3]
    bufs = scratch[3:]
    x, y, c = lax.axis_index("x"), lax.axis_index("y"), lax.axis_index("c")
    me = (x, y, c)
    peers = []
    for k in range(1, N_DEV):
        dx, dy, dc = (k >> 2) & 1, (k >> 1) & 1, k & 1
        peers.append(((1 - x) if dx else x, (1 - y) if dy else y, (1 - c) if dc else c))

    def send(a, k):
        return pltpu.make_async_remote_copy(
            src_ref=_slab(ins[a], *peers[k]), dst_ref=_slab(outs[a], *me), send_sem=send_sems.at[a, k],
            recv_sem=recv_sems.at[a, k], device_id=peers[k], device_id_type=MESH)

    def arrival(a, k):
        return pltpu.make_async_remote_copy(
            src_ref=_slab(ins[a], *peers[k]), dst_ref=_slab(outs[a], *peers[k]), send_sem=send_sems.at[a, k],
            recv_sem=recv_sems.at[a, k], device_id=peers[k], device_id_type=MESH)

    def start():
        for a in range(na):
            for k in range(N_DEV - 1):
                send(a, k).start()
        for a in range(na):
            _bounce(_slab(ins[a], *me), _slab(outs[a], *me), bufs[a], local_sems.at[a])

    def finish():
        for a in range(na):
            for k in range(N_DEV - 1):
                arrival(a, k).wait_recv()
        for a in range(na):
            for k in range(N_DEV - 1):
                send(a, k).wait_send()

    return start, finish


def _direct_scratch(parts):
    na = len(parts)
    return [pltpu.SemaphoreType.DMA((na, N_DEV - 1)), pltpu.SemaphoreType.DMA((na, N_DEV - 1)),
            pltpu.SemaphoreType.DMA((na,))] + [pltpu.VMEM(p.shape[1:], p.dtype) for p in parts]


ROW_TILES = D_MODEL // 128


def _rows3(t):
    return jnp.transpose(t[0]).reshape(t.shape[2], ROW_TILES, 128)


def _unrows3(t):
    return jnp.transpose(t.reshape(t.shape[0], D_MODEL))[None]


def _cast_shards(w_in3, w_att, w_ssm, w_o):
    def body(wi_ref, wa_ref, ws_ref, wo_ref, a_ref, b_ref, c_ref, d_ref):
        a_ref[...] = wi_ref[...].reshape(SHARD_IN // 2, 2 * ROW_TILES, 128).astype(BF16)
        b_ref[...] = wa_ref[...].astype(BF16)
        c_ref[...] = ws_ref[...].astype(BF16)
        d_ref[...] = wo_ref[...].astype(BF16)

    return pl.pallas_call(
        body, out_shape=[SDS((SHARD_IN // 2, 2 * ROW_TILES, 128), BF16), SDS(w_att.shape, BF16),
                         SDS(w_ssm.shape, BF16), SDS(w_o.shape, BF16)],
        compiler_params=_cparams(), name="cast_shards")(w_in3, w_att, w_ssm, w_o)


def _pieces():
    out = []
    for r0, c0, w in _SEGS:
        r = r0
        while r < r0 + w:
            d = r // SHARD_IN
            n = min(r0 + w, (d + 1) * SHARD_IN) - r
            out.append((c0 + (r - r0), d, r - d * SHARD_IN, n))
            r += n
    return out


def _to_aligned_t(slabs):
    def body(a_ref, o_ref):
        for (t, d, s, n) in _pieces():
            o_ref[t:t + n, :] = a_ref[d, s // 2:(s + n) // 2].reshape(n, D_MODEL)
        o_ref[C_DT + 32:C_DT + 128, :] = jnp.zeros((96, D_MODEL), slabs.dtype)

    return pl.pallas_call(body, out_shape=SDS((PW, D_MODEL), slabs.dtype), compiler_params=_cparams(),
                          name="to_aligned")(slabs)


def _from_aligned_pair(g):
    slab = (SHARD_IN // 2, 2 * ROW_TILES, 128)
    by_slab = [[p for p in _pieces() if p[1] == d] for d in range(N_DEV)]

    def body(g_ref, own_ref, got_ref, slabs, send_sems, recv_sems, local_sems):
        x, y, c = lax.axis_index("x"), lax.axis_index("y"), lax.axis_index("c")
        sibling = (x, y, 1 - c)

        def to_own(d, k):
            return pltpu.make_async_copy(slabs.at[d], own_ref.at[k], local_sems.at[k])

        def to_sibling(d, k):
            return pltpu.make_async_remote_copy(
                src_ref=slabs.at[d], dst_ref=got_ref.at[k], send_sem=send_sems.at[k], recv_sem=recv_sems.at[k],
                device_id=sibling, device_id_type=MESH)

        for d in range(N_DEV):
            for (t, _, s, n) in by_slab[d]:
                slabs[d, s // 2:(s + n) // 2] = g_ref[t:t + n, :].reshape(n // 2, 2 * ROW_TILES, 128)
            k, side = d // 2, d % 2
            pl.when(c == side)(to_own(d, k).start)
            pl.when(c != side)(to_sibling(d, k).start)
        for k in range(N_CHIP):
            to_own(0, k).wait()
            to_sibling(0, k).wait()

    half = SDS((N_CHIP,) + slab, g.dtype)
    return pl.pallas_call(
        body, in_specs=[pl.BlockSpec(memory_space=pltpu.VMEM)], out_specs=[ANY, ANY], out_shape=[half, half],
        scratch_shapes=[pltpu.VMEM((N_DEV,) + slab, g.dtype), pltpu.SemaphoreType.DMA((N_CHIP,)),
                        pltpu.SemaphoreType.DMA((N_CHIP,)), pltpu.SemaphoreType.DMA((N_CHIP,))],
        compiler_params=_cparams(), name="from_aligned_pair")(g)


_SEGS = [
    (R_Q, C_Q, 1024), (R_K, C_K, 256), (R_V, C_V, 256), (R_ZA, C_ZA, 1024), (R_ZS, C_ZS, 2048),
    (R_XBC, C_XBC, 3072), (R_DT, C_DT, 32), (R_GA, C_GA, 1024), (R_GS, C_GS, 1024)]


def _pad_lanes(v, n=128):
    return jnp.pad(v, ((0, 0), (0, n - v.shape[1])))


def _device_step(h, tgt, w_alt, w_out, g_pre, conv_w8, conv_b, dt_bias, a_log, d_skip, sinks, g_ssm, g_post, on_mesh):
    dtb, al, dsk, snk = _pad_lanes(dt_bias), _pad_lanes(a_log), _pad_lanes(d_skip), _pad_lanes(sinks)
    u = _norm_u(h, g_pre)
    proj = _matmul(u, w_alt, "nt", F32, T, PROJ_TILE, "in_proj")
    o = _attn_fwd(proj, snk)
    if on_mesh:
        sn, states, att_all, ssm_all, o_all = _ssd_fwd(proj, conv_w8, conv_b, dtb, al, dsk, g_ssm, gather=w_out)
        w_att = att_all.reshape(D_MODEL, D_MODEL)
        w_ssm = ssm_all.reshape(SSM_INNER, D_MODEL)
        w_o = o_all.reshape(D_MODEL, D_MODEL)
    else:
        sn, states = _ssd_fwd(proj, conv_w8, conv_b, dtb, al, dsk, g_ssm)
        w_att, w_ssm, w_o = w_out
    a_in, mg, ya, ys, out = _post_a(o, proj, sn, w_att, w_ssm, w_o)
    (loss, dres, dout, dya, dys, do, dproj, dsn, dgp) = _post_b(
        out, h, tgt, proj, ya, ys, o, g_post, w_att, w_ssm, w_o)
    dw_att = _matmul(a_in, dya, "tn", BF16, D_MODEL, D_MODEL, "d_w_att")
    dw_ssm = _matmul(sn, dys, "tn", BF16, D_MODEL, D_MODEL, "d_w_ssm")
    dw_o = _matmul(mg, dout, "tn", BF16, D_MODEL, D_MODEL, "d_w_o")
    res = {}
    if on_mesh:
        parts = [dw_att.reshape(N_DEV, 128, D_MODEL), dw_ssm.reshape(N_DEV, 256, D_MODEL),
                 dw_o.reshape(N_DEV, 128, D_MODEL)]
        (ddt4, dproj, ddtb, dal, ddsk, dgn, dcw, dcb, res["r_att"], res["r_ssm"], res["r_o"]) = _ssd_bwd(
            proj, conv_w8, conv_b, dtb, al, dsk, g_ssm, states, dsn, dproj, exchange=parts)
    else:
        ddt4, dproj, ddtb, dal, ddsk, dgn, dcw, dcb = _ssd_bwd(proj, conv_w8, conv_b, dtb, al, dsk, g_ssm, states,
                                                               dsn, dproj)
        res.update(dw_att=dw_att, dw_ssm=dw_ssm, dw_o=dw_o)
    dproj, dk, dv, dsink = _attn_bwd(proj, snk, do, dproj)
    dproj = _dproj_tail(dproj, dk, dv, ddt4)
    dw_alt = _matmul(dproj, u, "tn", BF16, PROJ_TILE, D_MODEL, "d_w_in")
    if on_mesh:
        own, got = _from_aligned_pair(dw_alt)
        dh, dgpre, res["r_in"] = _d_u_norm(dproj, w_alt, h, g_pre, dres,
                                           chips=_pair_sum([own], [got], "pair_sum_w_in"))
    else:
        dh, dgpre = _d_u_norm(dproj, w_alt, h, g_pre, dres)
        res["dw_alt"] = dw_alt
    small = (dgpre, dcb, ddtb, dal, ddsk, dsink, dgn, dgp, dcw)
    if on_mesh:
        res["small_pack"] = _small_pack(*small, loss, dh)
    else:
        res["small"] = small
    res.update(loss=loss[0, 0], dh=dh)
    return res


def kernel(x, meta_tokens, g_pre, w_in, conv_w, conv_b, dt_bias, a_log, d_skip, attn_sinks, g_ssm_norm, w_out_att, w_out_ssm, w_out, g_post, loss_target, m_meta_tokens, m_g_pre, m_w_in, m_conv_w, m_conv_b, m_dt_bias, m_a_log, m_d_skip, m_attn_sinks, m_g_ssm_norm, m_w_out_att, m_w_out_ssm, m_w_out, m_g_post, v_meta_tokens, v_g_pre, v_w_in, v_conv_w, v_conv_b, v_dt_bias, v_a_log, v_d_skip, v_attn_sinks, v_g_ssm_norm, v_w_out_att, v_w_out_ssm, v_w_out, v_g_post):
    w_in3, m_in3, v_in3 = _rows3(w_in), _rows3(m_w_in), _rows3(v_w_in)
    a_sh, att_sh, ssm_sh, o_sh = _cast_shards(w_in3, w_out_att[0], w_out_ssm[0], w_out[0])
    cw_sh = jnp.pad(conv_w[0], ((0, 4), (0, 0)))
    a_all, meta_all, cw_all = _all_gather([a_sh, meta_tokens, cw_sh])
    w_alt = _to_aligned_t(a_all)
    meta_full = meta_all.transpose(1, 0, 2).reshape(N_META, D_MODEL)
    conv_w8 = cw_all.transpose(1, 0, 2).reshape(8, CONV_DIM)

    h = jnp.concatenate([jnp.zeros((PAD, D_MODEL), F32), meta_full, x[0]], axis=0)
    tgt = jnp.concatenate([jnp.zeros((PAD + N_META, D_MODEL), F32), loss_target[0]], axis=0)
    r = _device_step(h, tgt, w_alt, (att_sh, ssm_sh, o_sh), g_pre, conv_w8, conv_b, dt_bias, a_log, d_skip,
                     attn_sinks, g_ssm_norm, g_post, True)
    grad_x = r["dh"][PAD + N_META:][None]

    *res_in, r_small = _sum_adamw_rows3(r["r_in"], w_in3, m_in3, v_in3, "adamw_w_in", exchange=[r["small_pack"]])
    res_in = [_unrows3(t) for t in res_in]
    res_att = [t[None] for t in _sum_adamw(r["r_att"], w_out_att[0], m_w_out_att[0], v_w_out_att[0], 512,
                                           "adamw_w_att")]
    res_ssm = [t[None] for t in _sum_adamw(r["r_ssm"], w_out_ssm[0], m_w_out_ssm[0], v_w_out_ssm[0], 512,
                                           "adamw_w_ssm")]
    res_o = [t[None] for t in _sum_adamw(r["r_o"], w_out[0], m_w_out[0], v_w_out[0], 512, "adamw_w_o")]
    (res_gpre, res_convb, res_dtb, res_alog, res_dskip, res_sink, res_gssm, res_gpost, res_cw, res_meta), loss = _small_finish(
        r_small, [(g_pre, m_g_pre, v_g_pre), (conv_b, m_conv_b, v_conv_b), (dt_bias, m_dt_bias, v_dt_bias),
                       (a_log, m_a_log, v_a_log), (d_skip, m_d_skip, v_d_skip),
                       (attn_sinks, m_attn_sinks, v_attn_sinks), (g_ssm_norm, m_g_ssm_norm, v_g_ssm_norm),
                       (g_post, m_g_post, v_g_post), (conv_w[0], m_conv_w[0], v_conv_w[0]),
                       (meta_tokens, m_meta_tokens, v_meta_tokens)])
    res_cw = [t[None] for t in res_cw]
    per_weight = [res_meta, res_gpre, res_in, res_cw, res_convb, res_dtb, res_alog, res_dskip, res_sink, res_gssm,
                  res_att, res_ssm, res_o, res_gpost]
    return (loss[0, 0], grad_x, *[p[0] for p in per_weight], *[p[1] for p in per_weight], *[p[2] for p in per_weight],
            *[p[3] for p in per_weight])
```

```python
import jax
import jax.numpy as jnp
from jax import lax
from jax.experimental import pallas as pl
from jax.experimental.pallas import tpu as pltpu

F32 = jnp.float32
BF16 = jnp.bfloat16
SDS = jax.ShapeDtypeStruct
MESH = pl.DeviceIdType.MESH
ANY = pl.BlockSpec(memory_space=pl.ANY)

N_DEV = 8
D_MODEL = 1024
SEQ = 2048
N_META = 16
BLK = 128
PAD = 112
T = PAD + N_META + SEQ
NB = T // BLK
EPS = 1e-6
HEAD = 64
Q_HEADS = 16
KV_HEADS = 4
GROUP = 4
KV_W = 256
SSM_INNER = 2048
SSM_HEADS = 32
SSM_GROUPS = 4
GRP_W = 512
SSM_STATE = 128
CONV_DIM = 3072
IN_PROJ = 9760
SHARD_IN = IN_PROJ // N_DEV
NEG = -1e30

C_ZS, C_XBC, C_Q, C_ZA, C_GA, C_GS, C_K, C_V, C_DT = 0, 2048, 5120, 6144, 7168, 8192, 9216, 9472, 9728
PW = 9856
GATES_W = 3 * D_MODEL
PROJ_TILE = 1408
R_Q, R_K, R_V, R_ZA, R_ZS, R_XBC, R_DT, R_GA, R_GS = 0, 1024, 1280, 1536, 2560, 4608, 7680, 7712, 8736

ADAM_LR, ADAM_B1, ADAM_B2, ADAM_EPS, ADAM_WD, ADAM_STEP = 0.001, 0.9, 0.999, 1e-08, 0.01, 10

VMEM_LIMIT = 56 * 1024 * 1024


def _cparams():
    return pltpu.CompilerParams(vmem_limit_bytes=VMEM_LIMIT)


def _silu(x):
    return x * jax.nn.sigmoid(x)


def _dsilu(x):
    s = jax.nn.sigmoid(x)
    return s * (1.0 + x * (1.0 - s))


def _matmul(a, b, mode, out_dtype, tm, tn, name):
    if mode == "nt":
        (m, k), n = a.shape, b.shape[0]
        a_spec = pl.BlockSpec((tm, k), lambda i, j: (i, 0))
        b_spec = pl.BlockSpec((tn, k), lambda i, j: (j, 0))
        dims = (((1,), (1,)), ((), ()))
    else:
        assert mode == "tn"
        (k, m), n = a.shape, b.shape[1]
        a_spec = pl.BlockSpec((k, tm), lambda i, j: (0, i))
        b_spec = pl.BlockSpec((k, tn), lambda i, j: (0, j))
        dims = (((0,), (0,)), ((), ()))
    assert m % tm == 0 and n % tn == 0, (a.shape, b.shape, tm, tn)

    def body(a_ref, b_ref, o_ref):
        o_ref[...] = lax.dot_general(a_ref[...], b_ref[...], dims, preferred_element_type=F32).astype(out_dtype)

    return pl.pallas_call(
        body, grid=(m // tm, n // tn), in_specs=[a_spec, b_spec],
        out_specs=pl.BlockSpec((tm, tn), lambda i, j: (i, j)), out_shape=SDS((m, n), out_dtype),
        compiler_params=_cparams(), name=name)(a, b)


def _norm_u(h, g_pre):
    def body(h_ref, g_ref, u_ref):
        x = h_ref[...]
        r = lax.rsqrt(jnp.mean(x * x, axis=-1, keepdims=True) + EPS)
        u_ref[...] = (x * r * g_ref[...]).astype(BF16)

    return pl.pallas_call(
        body, grid=(NB,),
        in_specs=[pl.BlockSpec((BLK, D_MODEL), lambda i: (i, 0)), pl.BlockSpec((1, D_MODEL), lambda i: (0, 0))],
        out_specs=pl.BlockSpec((BLK, D_MODEL), lambda i: (i, 0)),
        out_shape=SDS((T, D_MODEL), BF16), name="norm_u")(h, g_pre)


DU_TM, DU_TK = T // 2, PROJ_TILE


def _d_u_norm(dproj, w_alt, h, g_pre, dres, chips=()):
    nk = PW // DU_TK
    ni = T // DU_TM
    nc = len(chips)

    def body(*refs):
        a_ref, b_ref, h_ref, g_ref, dres_ref = refs[:5]
        dh_ref, dg_ref = refs[5 + nc:7 + nc]
        acc_ref = refs[7 + 2 * nc]
        i, kk = pl.program_id(0), pl.program_id(1)
        if nc:
            ch_start, ch_finish = _chips_program(refs[5:5 + nc], refs[7 + nc:7 + 2 * nc], refs[8 + 2 * nc:])
            pl.when((i == 0) & (kk == 0))(ch_start)
        part = jnp.dot(a_ref[...], b_ref[...], preferred_element_type=F32)

        @pl.when(kk == 0)
        def _():
            acc_ref[...] = part

        @pl.when((kk > 0) & (kk < nk - 1))
        def _():
            acc_ref[...] += part

        @pl.when(kk == nk - 1)
        def _():
            du_ = acc_ref[...] + part
            x = h_ref[...]
            r = lax.rsqrt(jnp.mean(x * x, axis=-1, keepdims=True) + EPS)
            gd = g_ref[...] * du_
            dx = r * gd - x * (r * r * r) * jnp.mean(x * gd, axis=-1, keepdims=True)
            dh_ref[...] = dx + dres_ref[...]
            gpart = jnp.concatenate([jnp.sum(du_ * x * r, axis=0, keepdims=True), jnp.zeros((7, D_MODEL), F32)],
                                    axis=0)

            @pl.when(i == 0)
            def _():
                dg_ref[...] = gpart

            @pl.when(i > 0)
            def _():
                dg_ref[...] += gpart

        if nc:
            pl.when((i == ni - 1) & (kk == nk - 1))(ch_finish)

    row = pl.BlockSpec((DU_TM, D_MODEL), lambda i, kk: (i, 0))
    return pl.pallas_call(
        body, grid=(ni, nk),
        in_specs=[pl.BlockSpec((DU_TM, DU_TK), lambda i, kk: (i, kk)),
                  pl.BlockSpec((DU_TK, D_MODEL), lambda i, kk: (kk, 0)),
                  row, pl.BlockSpec((1, D_MODEL), lambda i, kk: (0, 0)), row] + [ANY] * nc,
        out_specs=[row, pl.BlockSpec((8, D_MODEL), lambda i, kk: (0, 0))] + [ANY] * nc,
        out_shape=[SDS((T, D_MODEL), F32), SDS((8, D_MODEL), F32)] + [SDS(p.shape, p.dtype) for p in chips],
        scratch_shapes=[pltpu.VMEM((DU_TM, D_MODEL), F32)] + (_chips_scratch(chips) if nc else []),
        compiler_params=_cparams(), name="d_u_norm")(dproj, w_alt, h, g_pre, dres, *chips)


def _lane_pick(row, h):
    lane = lax.broadcasted_iota(jnp.int32, row.shape, 1)
    return jnp.sum(jnp.where(lane == h, row, 0.0), axis=1, keepdims=True)


def _attn_fn(q4s, kcats, vcats, kms, vms, sinks, n):
    r = lax.broadcasted_iota(jnp.int32, (GROUP * BLK, 2 * BLK), 0)
    s = lax.broadcasted_iota(jnp.int32, (GROUP * BLK, 2 * BLK), 1)
    i = jnp.bitwise_and(r, BLK - 1)
    gi = jnp.right_shift(r, 7)
    rel = i - s + BLK
    k_pos = n * BLK - BLK + s
    band_ok = (rel >= 0) & (rel < BLK) & (k_pos >= PAD + N_META)
    relf = rel.astype(F32)
    rm = lax.broadcasted_iota(jnp.int32, (GROUP * BLK, N_META), 0)
    mm = lax.broadcasted_iota(jnp.int32, (GROUP * BLK, N_META), 1)
    meta_ok = (PAD + mm) <= (n * BLK + jnp.bitwise_and(rm, BLK - 1))
    gcol = jnp.right_shift(lax.broadcasted_iota(jnp.int32, (GROUP * BLK, 1), 0), 7)
    outs = []
    for kh in range(KV_HEADS):
        slopes = [2.0 ** (-8.0 * (kh * GROUP + g + 1) / Q_HEADS) for g in range(GROUP)]
        slope = jnp.where(gi == 0, slopes[0], jnp.where(gi == 1, slopes[1], jnp.where(gi == 2, slopes[2], slopes[3])))
        sk = [_lane_pick(sinks, kh * GROUP + g) for g in range(GROUP)]
        sink = jnp.where(gcol == 0, sk[0], jnp.where(gcol == 1, sk[1], jnp.where(gcol == 2, sk[2], sk[3])))
        qb = (q4s[kh] * (HEAD ** -0.5)).astype(BF16)
        sb = lax.dot_general(qb, kcats[kh].astype(BF16), (((1,), (1,)), ((), ())), preferred_element_type=F32)
        sb = jnp.where(band_ok, sb - slope * relf, NEG)
        sm = lax.dot_general(qb, kms[kh].astype(BF16), (((1,), (1,)), ((), ())), preferred_element_type=F32)
        sm = jnp.where(meta_ok, sm, NEG)
        mx = jnp.maximum(jnp.maximum(jnp.max(sb, axis=1, keepdims=True), jnp.max(sm, axis=1, keepdims=True)), sink)
        mx = lax.stop_gradient(mx)
        eb = jnp.exp(sb - mx)
        em = jnp.exp(sm - mx)
        es = jnp.exp(sink - mx)
        inv = 1.0 / (jnp.sum(eb, axis=1, keepdims=True) + jnp.sum(em, axis=1, keepdims=True) + es)
        pb = (eb * inv).astype(BF16)
        pm = (em * inv).astype(BF16)
        o4 = (jnp.dot(pm, vms[kh].astype(BF16), preferred_element_type=F32)
              + jnp.dot(pb, vcats[kh].astype(BF16), preferred_element_type=F32))
        outs.append(o4)
    return outs


def _attn_specs():
    prev = lambda n: jnp.maximum(n - 1, 0)
    return [
        pl.BlockSpec((BLK, D_MODEL), lambda n: (n, C_Q // D_MODEL)),
        pl.BlockSpec((BLK, KV_W), lambda n: (prev(n), C_K // KV_W)),
        pl.BlockSpec((BLK, KV_W), lambda n: (n, C_K // KV_W)),
        pl.BlockSpec((BLK, KV_W), lambda n: (prev(n), C_V // KV_W)),
        pl.BlockSpec((BLK, KV_W), lambda n: (n, C_V // KV_W)),
        pl.BlockSpec((N_META, KV_W), lambda n: (PAD // N_META, C_K // KV_W)),
        pl.BlockSpec((N_META, KV_W), lambda n: (PAD // N_META, C_V // KV_W)),
        pl.BlockSpec((1, 128), lambda n: (0, 0)),
    ]


def _attn_load(q_ref, kp_ref, kc_ref, vp_ref, vc_ref, km_ref, vm_ref):
    q4s, kcats, vcats, kms, vms = [], [], [], [], []
    for kh in range(KV_HEADS):
        q4s.append(jnp.concatenate(
            [q_ref[:, (kh * GROUP + g) * HEAD:(kh * GROUP + g + 1) * HEAD] for g in range(GROUP)], axis=0))
        cs = slice(kh * HEAD, (kh + 1) * HEAD)
        kcats.append(jnp.concatenate([kp_ref[:, cs], kc_ref[:, cs]], axis=0))
        vcats.append(jnp.concatenate([vp_ref[:, cs], vc_ref[:, cs]], axis=0))
        kms.append(km_ref[:, cs])
        vms.append(vm_ref[:, cs])
    return q4s, kcats, vcats, kms, vms


def _attn_fwd(proj, sinks):
    def body(q_ref, kp_ref, kc_ref, vp_ref, vc_ref, km_ref, vm_ref, s_ref, o_ref):
        n = pl.program_id(0)
        args = _attn_load(q_ref, kp_ref, kc_ref, vp_ref, vc_ref, km_ref, vm_ref)
        outs = _attn_fn(*args, s_ref[...], n)
        for kh in range(KV_HEADS):
            for g in range(GROUP):
                hh = kh * GROUP + g
                o_ref[:, hh * HEAD:(hh + 1) * HEAD] = outs[kh][g * BLK:(g + 1) * BLK]

    return pl.pallas_call(
        body, grid=(NB,), in_specs=_attn_specs(),
        out_specs=pl.BlockSpec((BLK, D_MODEL), lambda n: (n, 0)),
        out_shape=SDS((T, D_MODEL), F32), name="attn_fwd")(proj, proj, proj, proj, proj, proj, proj, sinks)


def _attn_bwd(proj, sinks, do, dproj):
    def body(q_ref, kp_ref, kc_ref, vp_ref, vc_ref, km_ref, vm_ref, s_ref, do_ref, _, dq_ref, dk_ref, dv_ref, ds_ref):
        n = pl.program_id(0)

        @pl.when(n == 0)
        def _():
            dk_ref[...] = jnp.zeros_like(dk_ref)
            dv_ref[...] = jnp.zeros_like(dv_ref)
            ds_ref[...] = jnp.zeros_like(ds_ref)

        args = _attn_load(q_ref, kp_ref, kc_ref, vp_ref, vc_ref, km_ref, vm_ref)
        _, vjp = jax.vjp(lambda a, b, c, d, e, f: _attn_fn(a, b, c, d, e, f, n), *args, s_ref[...])
        do_f = do_ref[...].astype(F32)
        cot = [jnp.concatenate([do_f[:, (kh * GROUP + g) * HEAD:(kh * GROUP + g + 1) * HEAD] for g in range(GROUP)],
                               axis=0) for kh in range(KV_HEADS)]
        dq4s, dkcats, dvcats, dkms, dvms, dsk = vjp(cot)
        ds_ref[0:1, :] += dsk
        cur = pl.ds(pl.multiple_of(n * BLK, BLK), BLK)
        meta = slice(PAD, PAD + N_META)
        for kh in range(KV_HEADS):
            cs = slice(kh * HEAD, (kh + 1) * HEAD)
            for g in range(GROUP):
                hh = kh * GROUP + g
                dq_ref[:, hh * HEAD:(hh + 1) * HEAD] = dq4s[kh][g * BLK:(g + 1) * BLK].astype(BF16)
            dk_ref[cur, cs] += dkcats[kh][BLK:]
            dv_ref[cur, cs] += dvcats[kh][BLK:]
            dk_ref[meta, cs] += dkms[kh]
            dv_ref[meta, cs] += dvms[kh]

        @pl.when(n > 0)
        def _():
            prv = pl.ds(pl.multiple_of((n - 1) * BLK, BLK), BLK)
            for kh in range(KV_HEADS):
                cs = slice(kh * HEAD, (kh + 1) * HEAD)
                dk_ref[prv, cs] += dkcats[kh][:BLK]
                dv_ref[prv, cs] += dvcats[kh][:BLK]

    full_kv = pl.BlockSpec((T, KV_W), lambda n: (0, 0))
    return pl.pallas_call(
        body, grid=(NB,),
        in_specs=_attn_specs() + [pl.BlockSpec((BLK, D_MODEL), lambda n: (n, 0)), ANY],
        out_specs=[pl.BlockSpec((BLK, D_MODEL), lambda n: (n, C_Q // D_MODEL)), full_kv, full_kv,
                   pl.BlockSpec((8, 128), lambda n: (0, 0))],
        out_shape=[SDS((T, PW), BF16), SDS((T, KV_W), F32), SDS((T, KV_W), F32), SDS((8, 128), F32)],
        input_output_aliases={9: 0},
        name="attn_bwd")(proj, proj, proj, proj, proj, proj, proj, sinks, do, dproj)


def _rows_from(ext, start):
    if start % 8 == 0:
        return ext[start:start + BLK]
    return pltpu.roll(ext, (8 - start) % (BLK + 8), 0)[8:8 + BLK]


def _conv_taps(taps, w):
    return w[0:1] * taps[0] + w[1:2] * taps[1] + w[2:3] * taps[2] + w[3:4] * taps[3]


HPG = SSM_HEADS // SSM_GROUPS


def _iota(shape, dim):
    return lax.broadcasted_iota(jnp.int32, shape, dim)


def _mm(a, b, ca=1, cb=0):
    return lax.dot_general(a.astype(BF16), b.astype(BF16), (((ca,), (cb,)), ((), ())), preferred_element_type=F32)


def _split3(v):
    hi = v.astype(BF16)
    r1 = v - hi.astype(F32)
    mid = r1.astype(BF16)
    lo = (r1 - mid.astype(F32)).astype(BF16)
    return hi, mid, lo


def _split2(v):
    hi = v.astype(BF16)
    return hi, (v - hi.astype(F32)).astype(BF16)


def _sel_r(parts, onehot, ca=1, cb=0):
    out = lax.dot_general(parts[0], onehot, (((ca,), (cb,)), ((), ())), preferred_element_type=F32)
    for p in parts[1:]:
        out = out + lax.dot_general(p, onehot, (((ca,), (cb,)), ((), ())), preferred_element_type=F32)
    return out


def _sel_l(onehot, parts):
    out = jnp.dot(onehot, parts[0], preferred_element_type=F32)
    for p in parts[1:]:
        out = out + jnp.dot(onehot, p, preferred_element_type=F32)
    return out


def _rows8(*rows):
    r = _iota((8, rows[0].shape[1]), 0)
    out = jnp.zeros((8, rows[0].shape[1]), F32)
    for k, v in enumerate(rows):
        out = jnp.where(r == k, v, out)
    return out


def _ssd_forward(x, z, bm, cm, dt_raw, st_prev, dtb, alog, dskip, gn, g, cst_scr):
    li, si = _iota((BLK, BLK), 0), _iota((BLK, BLK), 1)
    dt_all = jax.nn.softplus(dt_raw + dtb)
    a_row = -jnp.exp(alog)
    a_all = dt_all * a_row
    cs_all = _sel_l((li >= si).astype(BF16), _split3(a_all))
    cs_parts = _split3(cs_all)
    spread = (_iota((BLK, GRP_W), 0) == g * HPG + jnp.right_shift(_iota((BLK, GRP_W), 1), 6)).astype(BF16)
    dt_e = _sel_r(_split2(dt_all), spread)
    cs_e = _sel_r(cs_parts, spread)
    d_e = _sel_r(_split2(_rows8(dskip)), spread)[0:1]
    cs_last_e = jnp.sum(jnp.where(_iota((BLK, GRP_W), 0) == BLK - 1, cs_e, 0.0), axis=0, keepdims=True)
    p_e = jnp.exp(cs_e)
    w_e = jnp.exp(cs_last_e - cs_e)
    cd_e = jnp.exp(cs_last_e)
    xr = x * dt_e
    cst_scr[...] = cs_all.T
    cst_g = cst_scr[g * HPG:(g + 1) * HPG, :]
    own = jnp.right_shift(_iota((HPG, HPG * BLK), 1), 7) == _iota((HPG, HPG * BLK), 0)
    ownf = own.astype(F32)
    q_rows = [ownf, ownf, ownf] + [jnp.where(own, jnp.concatenate([p.astype(F32)] * HPG, axis=1), 0.0)
                                   for p in _split3(cst_g)]
    q2 = jnp.concatenate(q_rows + [jnp.zeros((BLK - 6 * HPG, HPG * BLK), F32)], axis=0).astype(BF16)
    lane1 = _iota((1, BLK), 1)
    p2 = jnp.where((lane1 >= 3 * HPG) & (lane1 < 6 * HPG), -1.0, 0.0)
    for k, part in enumerate(cs_parts):
        pick = ((li == g * HPG + si - k * HPG) & (si >= k * HPG) & (si < (k + 1) * HPG)).astype(BF16)
        p2 = p2 + jnp.dot(part, pick, preferred_element_type=F32)
    dmat = jnp.dot(p2.astype(BF16), q2, preferred_element_type=F32)
    causal = _iota((BLK, HPG * BLK), 0) >= jnp.bitwise_and(_iota((BLK, HPG * BLK), 1), BLK - 1)
    lam = jnp.exp(jnp.where(causal, dmat, NEG))
    gmat = _mm(cm, bm, 1, 1)
    m_all = lam * jnp.concatenate([gmat] * HPG, axis=1)
    mb = m_all.astype(BF16)
    lo = _iota((BLK, BLK), 1) < HEAD
    xrb = xr.astype(BF16)
    zero = jnp.zeros((BLK, BLK), BF16)
    bds, yd = [], []
    for i in range(HPG // 2):
        t = xrb[:, BLK * i:BLK * (i + 1)]
        bd = jnp.concatenate([jnp.where(lo, t, zero), jnp.where(lo, zero, t)], axis=0)
        bds.append(bd)
        yd.append(jnp.dot(mb[:, 2 * BLK * i:2 * BLK * (i + 1)], bd, preferred_element_type=F32))
    cs_st = _mm(cm, st_prev)
    y = jnp.concatenate(yd, axis=1) + cs_st * p_e + d_e * x
    xrw = xr * w_e
    st_new = cd_e * st_prev + _mm(bm, xrw, 0, 0)
    yz = y * _silu(z)
    rn = lax.rsqrt(jnp.sum(yz * yz, axis=1, keepdims=True) / GRP_W + EPS)
    return dict(out=yz * rn * gn, st_new=st_new, dt_all=dt_all, a_row=a_row, dt_e=dt_e, d_e=d_e, p_e=p_e, w_e=w_e,
                cd_e=cd_e, xr=xr, xrw=xrw, lam=lam, m_all=m_all, mb=mb, bds=bds, cs_st=cs_st, y=y, yz=yz, rn=rn, lo=lo)


def _ssd_backward(f, x, z, bm, cm, dt_raw, st_prev, dtb, gn, g, dout, dst_next, cst_scr):
    li, si = _iota((BLK, BLK), 0), _iota((BLK, BLK), 1)
    yz, rn, y, p_e, w_e, cd_e, xr = f["yz"], f["rn"], f["y"], f["p_e"], f["w_e"], f["cd_e"], f["xr"]
    dgn = jnp.sum(dout * yz * rn, axis=0, keepdims=True)
    t = dout * gn
    dyz = rn * t - yz * (rn * rn * rn) * (jnp.sum(yz * t, axis=1, keepdims=True) / GRP_W)
    dy = dyz * _silu(z)
    dz = dyz * y * _dsilu(z)
    dx = f["d_e"] * dy
    dd_e = jnp.sum(dy * x, axis=0, keepdims=True)
    dcsst = dy * p_e
    dp_e = dy * f["cs_st"]
    dcm = _mm(dcsst, st_prev, 1, 1)
    dst_prev = _mm(cm, dcsst, 0, 0) + cd_e * dst_next
    dcd_e = jnp.sum(dst_next * st_prev, axis=0, keepdims=True)
    dbm = _mm(f["xrw"], dst_next, 1, 1)
    dxrw = _mm(bm, dst_next)
    dxr = dxrw * w_e
    dw_e = dxrw * xr
    dyb = dy.astype(BF16)
    dms, dxr_d = [], []
    for i in range(HPG // 2):
        dyp = dyb[:, BLK * i:BLK * (i + 1)]
        dms.append(lax.dot_general(dyp, f["bds"][i], (((1,), (1,)), ((), ())), preferred_element_type=F32))
        r = lax.dot_general(f["mb"][:, 2 * BLK * i:2 * BLK * (i + 1)], dyp, (((0,), (0,)), ((), ())),
                            preferred_element_type=F32)
        dxr_d.append(jnp.where(f["lo"], r[0:BLK], r[BLK:2 * BLK]))
    dm_all = jnp.concatenate(dms, axis=1)
    dxr = dxr + jnp.concatenate(dxr_d, axis=1)
    dlg = dm_all * f["lam"]
    dg = dlg[:, 0:BLK]
    for j in range(1, HPG):
        dg = dg + dlg[:, BLK * j:BLK * (j + 1)]
    dcm = dcm + _mm(dg, bm)
    dbm = dbm + _mm(dg, cm, 0, 0)
    q_all = dm_all * f["m_all"]
    col_sums = jnp.sum(q_all, axis=0, keepdims=True)
    cst_scr[...] = jnp.zeros_like(cst_scr)
    cst_scr[g * HPG:(g + 1) * HPG, :] = _rows8(
        *[col_sums[:, BLK * j:BLK * (j + 1)] for j in range(HPG)])
    dcs = -cst_scr[...].T
    for j in range(HPG):
        dcs = dcs + jnp.where(si == g * HPG + j,
                              jnp.sum(q_all[:, BLK * j:BLK * (j + 1)], axis=1, keepdims=True), 0.0)
    unspread = (_iota((GRP_W, BLK), 1) == g * HPG + jnp.right_shift(_iota((GRP_W, BLK), 0), 6)).astype(BF16)
    dww = dw_e * w_e
    per_head = _sel_r(_split2(jnp.concatenate([dp_e * p_e - dww, dxr * x], axis=0)), unspread)
    last = _sel_r(_split2(_rows8(jnp.sum(dww, axis=0, keepdims=True) + dcd_e * cd_e, dd_e)), unspread)
    dcs = dcs + per_head[0:BLK] + jnp.where(li == BLK - 1, last[0:1], 0.0)
    da = _sel_l((si >= li).astype(BF16), _split2(dcs))
    ddt_all = da * f["a_row"] + per_head[BLK:2 * BLK]
    dalog = jnp.sum(da * f["dt_all"], axis=0, keepdims=True) * f["a_row"]
    dx = dx + dxr * f["dt_e"]
    ddt_raw = ddt_all * jax.nn.sigmoid(dt_raw + dtb)
    ddtb = jnp.sum(ddt_raw, axis=0, keepdims=True)
    ddskip = last[1:2]
    return dict(dx=dx, dz=dz, dbm=dbm, dcm=dcm, ddt_raw=ddt_raw, dst_prev=dst_prev, ddtb=ddtb, dalog=dalog,
                ddskip=ddskip, dgn=dgn)


ZX_W = SSM_INNER + CONV_DIM
assert C_ZS == 0 and C_XBC == SSM_INNER


def _ssd_in_specs(rev):
    cidx = (lambda c: NB - 1 - c) if rev else (lambda c: c)
    return [
        pl.BlockSpec((BLK, ZX_W), lambda c: (cidx(c), 0)),
        pl.BlockSpec((8, ZX_W), lambda c: (jnp.maximum(cidx(c) * (BLK // 8) - 1, 0), 0)),
        pl.BlockSpec((BLK, 128), lambda c: (cidx(c), C_DT // 128)),
        pl.BlockSpec((8, CONV_DIM), lambda c: (0, 0)),
        pl.BlockSpec((1, CONV_DIM), lambda c: (0, 0)),
        pl.BlockSpec((1, 128), lambda c: (0, 0)),
        pl.BlockSpec((1, 128), lambda c: (0, 0)),
        pl.BlockSpec((1, 128), lambda c: (0, 0)),
        pl.BlockSpec((1, SSM_INNER), lambda c: (0, 0)),
    ]


def _xbc_act(zx_ref, tail_ref, w_ref, b_ref, n):
    tail = jnp.where(n > 0, tail_ref[:, SSM_INNER:], 0.0)
    xp = jnp.concatenate([tail, zx_ref[:, SSM_INNER:]], axis=0)
    taps = [_rows_from(xp, 5 + k) for k in range(4)]
    conv = _conv_taps(taps, w_ref[...]) + b_ref[...]
    valid = n * BLK + _iota((BLK, 1), 0) >= PAD
    return taps, conv, valid, jnp.where(valid, _silu(conv), 0.0)


def _grp_cols(act, i):
    b0, c0 = SSM_INNER + i * SSM_STATE, SSM_INNER + (SSM_GROUPS + i) * SSM_STATE
    return act[:, i * GRP_W:(i + 1) * GRP_W], act[:, b0:b0 + SSM_STATE], act[:, c0:c0 + SSM_STATE]


def _ssd_fwd(proj, conv_w, conv_b, dt_bias, a_log, d_skip, g_norm, gather=()):
    ng = len(gather)

    def body(*refs):
        zx_ref, tail_ref, dt_ref, w_ref, b_ref, dtb_ref, al_ref, dsk_ref, gn_ref = refs[:9]
        y_ref, st_ref = refs[9 + ng:11 + ng]
        s_scr, cst_scr = refs[11 + 2 * ng:13 + 2 * ng]
        c = pl.program_id(0)
        if ng:
            ag_start, ag_forward, ag_finish = _ag_program(refs[9:9 + ng], refs[11 + ng:11 + 2 * ng],
                                                          refs[13 + 2 * ng:])
            pl.when(c == 0)(ag_start)
            pl.when(c == (3 * NB) // 4)(ag_forward)

        @pl.when(c == 0)
        def _():
            s_scr[...] = jnp.zeros_like(s_scr)

        _, _, _, act = _xbc_act(zx_ref, tail_ref, w_ref, b_ref, c)
        for i in range(SSM_GROUPS):
            st_prev = s_scr[i]
            st_ref[i, 0] = st_prev
            x, bm, cm = _grp_cols(act, i)
            f = _ssd_forward(x, zx_ref[:, i * GRP_W:(i + 1) * GRP_W], bm, cm, dt_ref[...], st_prev, dtb_ref[...],
                             al_ref[...], dsk_ref[...], gn_ref[:, i * GRP_W:(i + 1) * GRP_W], i, cst_scr.at[i])
            y_ref[:, i * GRP_W:(i + 1) * GRP_W] = f["out"].astype(BF16)
            s_scr[i] = f["st_new"]
        if ng:
            pl.when(c == NB - 1)(ag_finish)

    return pl.pallas_call(
        body, grid=(NB,), in_specs=_ssd_in_specs(False) + [ANY] * ng,
        out_specs=[pl.BlockSpec((BLK, SSM_INNER), lambda c: (c, 0)),
                   pl.BlockSpec((SSM_GROUPS, 1, SSM_STATE, GRP_W), lambda c: (0, c, 0, 0))] + [ANY] * ng,
        out_shape=[SDS((T, SSM_INNER), BF16), SDS((SSM_GROUPS, NB, SSM_STATE, GRP_W), F32)]
        + [SDS((N_DEV,) + s.shape, s.dtype) for s in gather],
        scratch_shapes=[pltpu.VMEM((SSM_GROUPS, SSM_STATE, GRP_W), F32), pltpu.VMEM((SSM_GROUPS, BLK, BLK), F32)]
        + (_ag_scratch(gather) if ng else []),
        compiler_params=_cparams(),
        name="ssd_fwd")(proj, proj, proj, conv_w, conv_b, dt_bias, a_log, d_skip, g_norm, *gather)


def _ssd_bwd(proj, conv_w, conv_b, dt_bias, a_log, d_skip, g_norm, states, dy, dproj, exchange=()):
    ne = len(exchange)

    def body(*refs):
        zx_ref, tail_ref, dt_ref, w_ref, b_ref, dtb_ref, al_ref, dsk_ref, gn_ref, st_ref, dy_ref = refs[:11]
        (ddt_ref, dp_ref, ddtb_ref, dal_ref, ddsk_ref, dgn_ref, dcw_ref, dcb_ref) = refs[12 + ne:20 + ne]
        ds_scr, cst_scr, carry = refs[20 + 2 * ne:23 + 2 * ne]
        c = pl.program_id(0)
        n = NB - 1 - c
        if ne:
            ex_start, ex_finish = _direct_program(refs[12:12 + ne], refs[20 + ne:20 + 2 * ne], refs[23 + 2 * ne:])
            pl.when(c == 0)(ex_start)

        @pl.when(c == 0)
        def _():
            for ref in (ds_scr, carry, dgn_ref, ddtb_ref, dal_ref, ddsk_ref, dcw_ref, dcb_ref):
                ref[...] = jnp.zeros_like(ref)

        taps, conv, valid, act = _xbc_act(zx_ref, tail_ref, w_ref, b_ref, n)
        dt_raw = dt_ref[...]
        dxs, dbs, dcs = [], [], []
        for i in range(SSM_GROUPS):
            x, bm, cm = _grp_cols(act, i)
            z, gn, st_prev = zx_ref[:, i * GRP_W:(i + 1) * GRP_W], gn_ref[:, i * GRP_W:(i + 1) * GRP_W], st_ref[i, 0]
            f = _ssd_forward(x, z, bm, cm, dt_raw, st_prev, dtb_ref[...], al_ref[...], dsk_ref[...], gn, i,
                             cst_scr.at[i])
            d = _ssd_backward(f, x, z, bm, cm, dt_raw, st_prev, dtb_ref[...], gn, i,
                              dy_ref[:, i * GRP_W:(i + 1) * GRP_W].astype(F32), ds_scr[i], cst_scr.at[i])
            dxs.append(d["dx"])
            dbs.append(d["dbm"])
            dcs.append(d["dcm"])
            dp_ref[:, i * GRP_W:(i + 1) * GRP_W] = d["dz"].astype(BF16)
            ds_scr[i] = d["dst_prev"]
            ddt_ref[:, i * 128:(i + 1) * 128] = d["ddt_raw"]
            dgn_ref[0:1, i * GRP_W:(i + 1) * GRP_W] += d["dgn"]
            ddtb_ref[0:1, :] += d["ddtb"]
            dal_ref[0:1, :] += d["dalog"]
            ddsk_ref[0:1, :] += d["ddskip"]
        dconv = jnp.where(valid, jnp.concatenate(dxs + dbs + dcs, axis=1) * _dsilu(conv), 0.0)
        dext = jnp.concatenate([dconv, carry[...]], axis=0)
        dp_ref[:, SSM_INNER:] = _conv_taps([_rows_from(dext, 3 - k) for k in range(4)], w_ref[...]).astype(BF16)
        carry[...] = dconv[0:8]
        dcw_ref[...] += jnp.concatenate(
            [jnp.sum(dconv * taps[k], axis=0, keepdims=True) for k in range(4)]
            + [jnp.zeros((4, CONV_DIM), F32)], axis=0)
        dcb_ref[0:1, :] += jnp.sum(dconv, axis=0, keepdims=True)
        if ne:
            pl.when(c == NB - 1)(ex_finish)

    rc = lambda c: NB - 1 - c
    small = pl.BlockSpec((8, 128), lambda c: (0, 0))
    wide = lambda w: pl.BlockSpec((8, w), lambda c: (0, 0))
    return pl.pallas_call(
        body, grid=(NB,),
        in_specs=_ssd_in_specs(True) + [
            pl.BlockSpec((SSM_GROUPS, 1, SSM_STATE, GRP_W), lambda c: (0, rc(c), 0, 0)),
            pl.BlockSpec((BLK, SSM_INNER), lambda c: (rc(c), 0)), ANY] + [ANY] * ne,
        out_specs=[pl.BlockSpec((BLK, SSM_GROUPS * 128), lambda c: (rc(c), 0)),
                   pl.BlockSpec((BLK, ZX_W), lambda c: (rc(c), 0)),
                   small, small, small, wide(SSM_INNER), wide(CONV_DIM), wide(CONV_DIM)] + [ANY] * ne,
        out_shape=[SDS((T, GRP_W), F32), SDS((T, PW), BF16), SDS((8, 128), F32), SDS((8, 128), F32),
                   SDS((8, 128), F32), SDS((8, SSM_INNER), F32), SDS((8, CONV_DIM), F32), SDS((8, CONV_DIM), F32)]
        + [SDS(p.shape, p.dtype) for p in exchange],
        scratch_shapes=[pltpu.VMEM((SSM_GROUPS, SSM_STATE, GRP_W), F32), pltpu.VMEM((SSM_GROUPS, BLK, BLK), F32),
                        pltpu.VMEM((8, CONV_DIM), F32)] + (_direct_scratch(exchange) if ne else []),
        input_output_aliases={11: 1},
        compiler_params=_cparams(),
        name="ssd_bwd")(proj, proj, proj, conv_w, conv_b, dt_bias, a_log, d_skip, g_norm, states, dy, dproj,
                        *exchange)


POST_R = 272


def _post_a(o, proj, sn, w_att, w_ssm, w_o):
    def body(o_ref, za_ref, ga_ref, gs_ref, sn_ref, wa_ref, ws_ref, wo_ref, a_ref, mg_ref, ya_ref, ys_ref, out_ref):
        a = (o_ref[...] * _silu(za_ref[...])).astype(BF16)
        a_ref[...] = a
        ya = jnp.dot(a, wa_ref[...], preferred_element_type=F32)
        ys = jnp.dot(sn_ref[...], ws_ref[...], preferred_element_type=F32)
        ya_ref[...] = ya.astype(BF16)
        ys_ref[...] = ys.astype(BF16)
        mg = (jax.nn.sigmoid(ga_ref[...]) * ya + jax.nn.sigmoid(gs_ref[...]) * ys).astype(BF16)
        mg_ref[...] = mg
        out_ref[...] = jnp.dot(mg, wo_ref[...], preferred_element_type=F32)

    row = pl.BlockSpec((POST_R, D_MODEL), lambda i: (i, 0))
    pcol = lambda c0: pl.BlockSpec((POST_R, D_MODEL), lambda i: (i, c0 // D_MODEL))
    full = lambda r: pl.BlockSpec((r, D_MODEL), lambda i: (0, 0))
    return pl.pallas_call(
        body, grid=(T // POST_R,),
        in_specs=[row, pcol(C_ZA), pcol(C_GA), pcol(C_GS), pl.BlockSpec((POST_R, SSM_INNER), lambda i: (i, 0)),
                  full(D_MODEL), full(SSM_INNER), full(D_MODEL)],
        out_specs=[row, row, row, row, row],
        out_shape=[SDS((T, D_MODEL), BF16), SDS((T, D_MODEL), BF16), SDS((T, D_MODEL), BF16), SDS((T, D_MODEL), BF16),
                   SDS((T, D_MODEL), F32)],
        compiler_params=_cparams(), name="post_a")(o, proj, proj, proj, sn, w_att, w_ssm, w_o)


def _post_b(out, h, tgt, proj, ya, ys, o, g_post, w_att, w_ssm, w_o):
    def body(out_ref, h_ref, t_ref, za_ref, ga_ref, gs_ref, ya_ref, ys_ref, o_ref, gp_ref, wa_ref, ws_ref, wo_ref,
             loss_ref, dres_ref, dout_ref, dya_ref, dys_ref, do_ref, dp_ref, dsn_ref, dgp_ref):
        i = pl.program_id(0)
        x = out_ref[...]
        gp = gp_ref[...]
        r = lax.rsqrt(jnp.mean(x * x, axis=-1, keepdims=True) + EPS)
        row = i * POST_R + lax.broadcasted_iota(jnp.int32, (POST_R, 1), 0)
        res = h_ref[...] + jnp.where(row >= PAD, x * r * gp, 0.0)
        live = row >= PAD + N_META
        err = jnp.where(live, res - t_ref[...], 0.0)
        lpart = 0.5 * jnp.sum(jnp.sum(err * err, axis=1, keepdims=True) / D_MODEL, axis=0, keepdims=True)
        dres = err / D_MODEL
        dres_ref[...] = dres
        gpart = jnp.sum(dres * x * r, axis=0, keepdims=True)

        @pl.when(i == 0)
        def _():
            loss_ref[...] = jnp.zeros_like(loss_ref)
            dgp_ref[...] = jnp.zeros_like(dgp_ref)

        loss_ref[...] += jnp.broadcast_to(lpart, loss_ref.shape)
        dgp_ref[0:1, :] += gpart
        gd = gp * dres
        dout = (r * gd - x * (r * r * r) * jnp.mean(x * gd, axis=-1, keepdims=True)).astype(BF16)
        dout_ref[...] = dout
        dmg = lax.dot_general(dout, wo_ref[...], (((1,), (1,)), ((), ())), preferred_element_type=F32)
        sga = jax.nn.sigmoid(ga_ref[...])
        sgs = jax.nn.sigmoid(gs_ref[...])
        dya = (dmg * sga).astype(BF16)
        dys = (dmg * sgs).astype(BF16)
        dya_ref[...] = dya
        dys_ref[...] = dys
        dp_ref[:, C_GA - C_ZA:C_GA - C_ZA + D_MODEL] = (dmg * ya_ref[...].astype(F32) * sga * (1.0 - sga)).astype(BF16)
        dp_ref[:, C_GS - C_ZA:C_GS - C_ZA + D_MODEL] = (dmg * ys_ref[...].astype(F32) * sgs * (1.0 - sgs)).astype(BF16)
        da = lax.dot_general(dya, wa_ref[...], (((1,), (1,)), ((), ())), preferred_element_type=F32)
        za = za_ref[...]
        do_ref[...] = (da * _silu(za)).astype(BF16)
        dp_ref[:, 0:D_MODEL] = (da * o_ref[...] * _dsilu(za)).astype(BF16)
        dsn_ref[...] = lax.dot_general(dys, ws_ref[...], (((1,), (1,)), ((), ())),
                                       preferred_element_type=F32).astype(BF16)

    row = pl.BlockSpec((POST_R, D_MODEL), lambda i: (i, 0))
    pcol = lambda c0: pl.BlockSpec((POST_R, D_MODEL), lambda i: (i, c0 // D_MODEL))
    full = lambda r: pl.BlockSpec((r, D_MODEL), lambda i: (0, 0))
    small = pl.BlockSpec((8, D_MODEL), lambda i: (0, 0))
    return pl.pallas_call(
        body, grid=(T // POST_R,),
        in_specs=[row, row, row, pcol(C_ZA), pcol(C_GA), pcol(C_GS), row, row, row,
                  pl.BlockSpec((1, D_MODEL), lambda i: (0, 0)), full(D_MODEL), full(SSM_INNER), full(D_MODEL)],
        out_specs=[pl.BlockSpec((8, 128), lambda i: (0, 0)), row, row, row, row, row,
                   pl.BlockSpec((POST_R, GATES_W), lambda i: (i, C_ZA // GATES_W)),
                   pl.BlockSpec((POST_R, SSM_INNER), lambda i: (i, 0)), small],
        out_shape=[SDS((8, 128), F32), SDS((T, D_MODEL), F32), SDS((T, D_MODEL), BF16), SDS((T, D_MODEL), BF16),
                   SDS((T, D_MODEL), BF16), SDS((T, D_MODEL), BF16), SDS((T, PW), BF16),
                   SDS((T, SSM_INNER), BF16), SDS((8, D_MODEL), F32)],
        compiler_params=_cparams(), name="post_b")(out, h, tgt, proj, proj, proj, ya, ys, o, g_post, w_att, w_ssm, w_o)


TAIL_W = PW - C_K


def _dproj_tail(dproj, dk, dv, ddt4):
    rows = T // 4

    def body(_, dk_ref, dv_ref, ddt_ref, o_ref, buf, sem):
        n = pl.program_id(0)
        d4 = ddt_ref[...]
        buf[:, 0:KV_W] = dk_ref[...].astype(BF16)
        buf[:, KV_W:2 * KV_W] = dv_ref[...].astype(BF16)
        buf[:, 2 * KV_W:TAIL_W] = (d4[:, 0:128] + d4[:, 128:256] + d4[:, 256:384] + d4[:, 384:512]).astype(BF16)
        cp = pltpu.make_async_copy(buf, o_ref.at[pl.ds(pl.multiple_of(n * rows, 16), rows), pl.ds(C_K, TAIL_W)], sem)
        cp.start()
        cp.wait()

    spec = lambda w: pl.BlockSpec((rows, w), lambda i: (i, 0))
    return pl.pallas_call(
        body, grid=(T // rows,), in_specs=[ANY, spec(KV_W), spec(KV_W), spec(GRP_W)], out_specs=ANY,
        out_shape=SDS((T, PW), BF16), input_output_aliases={0: 0},
        scratch_shapes=[pltpu.VMEM((rows, TAIL_W), BF16), pltpu.SemaphoreType.DMA],
        name="dproj_tail")(dproj, dk, dv, ddt4)


def _adamw_math(w, g, m, v):
    m = ADAM_B1 * m + (1.0 - ADAM_B1) * g
    v = ADAM_B2 * v + (1.0 - ADAM_B2) * (g * g)
    m_hat = m / (1.0 - ADAM_B1 ** ADAM_STEP)
    v_hat = v / (1.0 - ADAM_B2 ** ADAM_STEP)
    delta = -ADAM_LR * (m_hat / (jnp.sqrt(v_hat) + ADAM_EPS) + ADAM_WD * w)
    return delta, m, v


def _sum_adamw(recv, w, m, v, tc, name):
    rows, cols = w.shape
    nslab = recv.shape[0]
    assert cols % tc == 0

    def body(r_ref, w_ref, m_ref, v_ref, g_ref, d_ref, nm_ref, nv_ref):
        g = r_ref[0].astype(F32)
        for d in range(1, nslab):
            g = g + r_ref[d].astype(F32)
        g_ref[...] = g
        delta, nm, nv = _adamw_math(w_ref[...], g, m_ref[...], v_ref[...])
        d_ref[...] = delta
        nm_ref[...] = nm
        nv_ref[...] = nv

    blk = pl.BlockSpec((rows, tc), lambda i: (0, i))
    return pl.pallas_call(
        body, grid=(cols // tc,),
        in_specs=[pl.BlockSpec((nslab, rows, tc), lambda i: (0, 0, i)), blk, blk, blk],
        out_specs=[blk, blk, blk, blk], out_shape=[SDS((rows, cols), F32)] * 4,
        compiler_params=_cparams(), name=name)(recv, w, m, v)


def _sum_adamw_rows3(recv, w3, m3, v3, name, exchange=()):
    pairs = 61
    assert (SHARD_IN // 2) % pairs == 0
    nsteps = SHARD_IN // 2 // pairs
    ne = len(exchange)

    def body(*refs):
        r_ref, w_ref, m_ref, v_ref = refs[:4]
        g_ref, d_ref, nm_ref, nv_ref = refs[4 + ne:8 + ne]
        if ne:
            ex_start, ex_finish = _direct_program(refs[4:4 + ne], refs[8 + ne:8 + 2 * ne], refs[8 + 2 * ne:])
            pl.when(pl.program_id(0) == 0)(ex_start)
        g = r_ref[0].astype(F32)
        for d in range(1, N_CHIP):
            g = g + r_ref[d].astype(F32)
        g = g.reshape(2 * pairs, ROW_TILES, 128)
        g_ref[...] = g
        delta, nm, nv = _adamw_math(w_ref[...], g, m_ref[...], v_ref[...])
        d_ref[...] = delta
        nm_ref[...] = nm
        nv_ref[...] = nv
        if ne:
            pl.when(pl.program_id(0) == nsteps - 1)(ex_finish)

    blk = pl.BlockSpec((2 * pairs, ROW_TILES, 128), lambda i: (i, 0, 0))
    return pl.pallas_call(
        body, grid=(nsteps,),
        in_specs=[pl.BlockSpec((N_CHIP, pairs, 2 * ROW_TILES, 128), lambda i: (0, i, 0, 0)), blk, blk, blk]
        + [ANY] * ne,
        out_specs=[blk, blk, blk, blk] + [ANY] * ne,
        out_shape=[SDS(w3.shape, F32)] * 4 + [SDS(p.shape, p.dtype) for p in exchange],
        scratch_shapes=_direct_scratch(exchange) if ne else [],
        compiler_params=_cparams(), name=name)(recv, w3, m3, v3, *exchange)


ROW_GPRE, ROW_CONVB, ROW_DTB, ROW_ALOG, ROW_DSKIP, ROW_SINK, ROW_GSSM, ROW_GPOST = 0, 1, 4, 5, 6, 7, 8, 10
ROW_LOSS = 11
REP_ROWS, ROW_CONVW, ROW_META, SM_ROWS = 16, 16, 24, 40
CW_SHARD = CONV_DIM // N_DEV
META_SHARD = D_MODEL // N_DEV


def _small_pack(dgpre, db, ddtb, dal, ddsk, dsink, dgn, dgp, dw, loss, dh):
    def body(dgpre_ref, db_ref, ddtb_ref, dal_ref, ddsk_ref, dsink_ref, dgn_ref, dgp_ref, dw_ref, loss_ref, dh_ref,
             o_ref, rep):
        rep[...] = jnp.zeros_like(rep)
        rep[ROW_LOSS:ROW_LOSS + 1, 0:128] = loss_ref[0:1, :]
        rep[ROW_GPRE:ROW_GPRE + 1, :] = dgpre_ref[0:1, :]
        for k in range(3):
            rep[ROW_CONVB + k:ROW_CONVB + k + 1, :] = db_ref[0:1, 1024 * k:1024 * (k + 1)]
        rep[ROW_DTB:ROW_DTB + 1, 0:128] = ddtb_ref[0:1, :]
        rep[ROW_ALOG:ROW_ALOG + 1, 0:128] = dal_ref[0:1, :]
        rep[ROW_DSKIP:ROW_DSKIP + 1, 0:128] = ddsk_ref[0:1, :]
        rep[ROW_SINK:ROW_SINK + 1, 0:128] = dsink_ref[0:1, :]
        rep[ROW_GSSM:ROW_GSSM + 1, :] = dgn_ref[0:1, 0:1024]
        rep[ROW_GSSM + 1:ROW_GSSM + 2, :] = dgn_ref[0:1, 1024:2048]
        rep[ROW_GPOST:ROW_GPOST + 1, :] = dgp_ref[0:1, :]
        cw = dw_ref[...]
        mh = dh_ref[...]
        o_ref[...] = jnp.zeros_like(o_ref)
        for p in range(N_DEV):
            o_ref[p, 0:REP_ROWS, :] = rep[...]
            o_ref[p, ROW_CONVW:ROW_CONVW + 8, 0:CW_SHARD] = cw[:, p * CW_SHARD:(p + 1) * CW_SHARD]
            o_ref[p, ROW_META:ROW_META + N_META, 0:META_SHARD] = mh[:, p * META_SHARD:(p + 1) * META_SHARD]

    ins = [dgpre, db, ddtb, dal, ddsk, dsink, dgn, dgp, dw, loss]
    return pl.pallas_call(
        body, grid=(1,),
        in_specs=[pl.BlockSpec(a.shape, lambda i: (0, 0)) for a in ins]
        + [pl.BlockSpec((N_META, D_MODEL), lambda i: (PAD // N_META, 0))],
        out_specs=pl.BlockSpec((N_DEV, SM_ROWS, 1024), lambda i: (0, 0, 0)),
        out_shape=SDS((N_DEV, SM_ROWS, 1024), F32), scratch_shapes=[pltpu.VMEM((REP_ROWS, 1024), F32)],
        name="small_pack")(*ins, dh)


def _small_finish(recv, params):
    npar = len(params)

    def body(*refs):
        r_ref = refs[0]
        wmv = refs[1:1 + 3 * npar]
        outs = refs[1 + 3 * npar:1 + 7 * npar]
        loss_ref = refs[1 + 7 * npar]
        gs = refs[-1]
        g = r_ref[0]
        for d in range(1, recv.shape[0]):
            g = g + r_ref[d]
        gs[...] = g
        loss_ref[...] = gs[ROW_LOSS:ROW_LOSS + 1, 0:128]
        grads = [
            gs[ROW_GPRE:ROW_GPRE + 1, :],
            jnp.concatenate([gs[ROW_CONVB + k:ROW_CONVB + k + 1, :] for k in range(3)], axis=1),
            gs[ROW_DTB:ROW_DTB + 1, 0:SSM_HEADS], gs[ROW_ALOG:ROW_ALOG + 1, 0:SSM_HEADS],
            gs[ROW_DSKIP:ROW_DSKIP + 1, 0:SSM_HEADS], gs[ROW_SINK:ROW_SINK + 1, 0:Q_HEADS],
            jnp.concatenate([gs[ROW_GSSM:ROW_GSSM + 1, :], gs[ROW_GSSM + 1:ROW_GSSM + 2, :]], axis=1),
            gs[ROW_GPOST:ROW_GPOST + 1, :],
            gs[ROW_CONVW:ROW_CONVW + 4, 0:CW_SHARD],
            gs[ROW_META:ROW_META + N_META, 0:META_SHARD]]
        for i in range(npar):
            w_ref, m_ref, v_ref = wmv[3 * i:3 * i + 3]
            delta, nm, nv = _adamw_math(w_ref[...], grads[i], m_ref[...], v_ref[...])
            outs[4 * i][...] = grads[i]
            outs[4 * i + 1][...] = delta
            outs[4 * i + 2][...] = nm
            outs[4 * i + 3][...] = nv

    flat = [a for wmv in params for a in wmv]
    res = pl.pallas_call(
        body, out_shape=[SDS(wmv[0].shape, F32) for wmv in params for _ in range(4)] + [SDS((1, 128), F32)],
        scratch_shapes=[pltpu.VMEM((SM_ROWS, 1024), F32)], name="small_finish")(recv, *flat)
    return [tuple(res[4 * i:4 * i + 4]) for i in range(npar)], res[4 * npar]


def _slab(ref, px, py, pc):
    return ref.at[4 * px + 2 * py + pc]


def _bounce(src, dst, buf, sem):
    cp = pltpu.make_async_copy(src, buf, sem)
    cp.start()
    cp.wait()
    cp = pltpu.make_async_copy(buf, dst, sem)
    cp.start()
    cp.wait()


def _ag_program(ins, outs, scratch):
    na = len(ins)
    send_sems, recv_sems, local_sems = scratch[:3]
    bufs = scratch[3:]
    x, y, c = lax.axis_index("x"), lax.axis_index("y"), lax.axis_index("c")
    me, sibling = (x, y, c), (x, y, 1 - c)
    chips = [(1 - x, y), (x, 1 - y), (1 - x, 1 - y)]

    def copy(a, k, block, to, src=None):
        dst = _slab(outs[a], *block)
        return pltpu.make_async_remote_copy(
            src_ref=dst if src is None else src, dst_ref=dst, send_sem=send_sems.at[a, k],
            recv_sem=recv_sems.at[a, k], device_id=to, device_id_type=MESH)

    def own_sends():
        out = []
        for a in range(na):
            out.append(copy(a, 0, me, sibling, src=ins[a]))
            out += [copy(a, 1 + j, me, (*chip, c), src=ins[a]) for j, chip in enumerate(chips)]
        return out

    def start():
        for cp in own_sends():
            cp.start()
        for a in range(na):
            _bounce(ins[a], _slab(outs[a], *me), bufs[a], local_sems.at[a])

    def forward():
        for j, chip in enumerate(chips):
            for a in range(na):
                copy(a, 1 + j, (*chip, c), me).wait_recv()
                copy(a, 4 + j, (*chip, c), sibling).start()

    def finish():
        for a in range(na):
            copy(a, 0, sibling, me).wait_recv()
            for j, chip in enumerate(chips):
                copy(a, 4 + j, (*chip, 1 - c), me).wait_recv()
        for cp in own_sends():
            cp.wait_send()
        for j, chip in enumerate(chips):
            for a in range(na):
                copy(a, 4 + j, (*chip, c), sibling).wait_send()

    return start, forward, finish


def _ag_scratch(shards):
    na = len(shards)
    return [pltpu.SemaphoreType.DMA((na, 7)), pltpu.SemaphoreType.DMA((na, 7)),
            pltpu.SemaphoreType.DMA((na,))] + [pltpu.VMEM(s.shape, s.dtype) for s in shards]


def _all_gather(shards):
    na = len(shards)

    def body(*refs):
        start, forward, finish = _ag_program(refs[:na], refs[na:2 * na], refs[2 * na:])
        start()
        forward()
        finish()

    return pl.pallas_call(
        body, in_specs=[ANY] * na, out_specs=[ANY] * na,
        out_shape=[SDS((N_DEV,) + s.shape, s.dtype) for s in shards],
        scratch_shapes=_ag_scratch(shards), name="all_gather")(*shards)


N_CHIP = 4


def _pair_sum(own, got, name):
    na = len(own)

    def body(*refs):
        for a in range(na):
            o_ref, g_ref, s_ref = refs[a], refs[na + a], refs[2 * na + a]
            s_ref[...] = (o_ref[...].astype(F32) + g_ref[...].astype(F32)).astype(s_ref.dtype)

    def spec(p):
        nd = len(p.shape) - 1
        return pl.BlockSpec((1,) + p.shape[1:], lambda k, nd=nd: (k,) + (0,) * nd)

    return pl.pallas_call(
        body, grid=(N_CHIP,), in_specs=[spec(p) for p in own] + [spec(p) for p in got],
        out_specs=[spec(p) for p in own], out_shape=[SDS(p.shape, p.dtype) for p in own],
        compiler_params=_cparams(), name=name)(*own, *got)


def _chips_program(ins, outs, scratch):
    na = len(ins)
    send_sems, recv_sems, local_sems = scratch[:3]
    bufs = scratch[3:]
    x, y, c = lax.axis_index("x"), lax.axis_index("y"), lax.axis_index("c")
    mine = 2 * x + y
    chips = [(1 - x, y), (x, 1 - y), (1 - x, 1 - y)]

    def send(a, j):
        px, py = chips[j]
        return pltpu.make_async_remote_copy(
            src_ref=ins[a].at[2 * px + py], dst_ref=outs[a].at[mine], send_sem=send_sems.at[a, j],
            recv_sem=recv_sems.at[a, j], device_id=(px, py, c), device_id_type=MESH)

    def arrival(a, j):
        px, py = chips[j]
        return pltpu.make_async_remote_copy(
            src_ref=ins[a].at[2 * px + py], dst_ref=outs[a].at[2 * px + py], send_sem=send_sems.at[a, j],
            recv_sem=recv_sems.at[a, j], device_id=(px, py, c), device_id_type=MESH)

    def start():
        for a in range(na):
            for j in range(3):
                send(a, j).start()
        for a in range(na):
            _bounce(ins[a].at[mine], outs[a].at[mine], bufs[a], local_sems.at[a])

    def finish():
        for a in range(na):
            for j in range(3):
                arrival(a, j).wait_recv()
        for a in range(na):
            for j in range(3):
                send(a, j).wait_send()

    return start, finish


def _chips_scratch(parts):
    na = len(parts)
    return [pltpu.SemaphoreType.DMA((na, 3)), pltpu.SemaphoreType.DMA((na, 3)),
            pltpu.SemaphoreType.DMA((na,))] + [pltpu.VMEM(p.shape[1:], p.dtype) for p in parts]


def _direct_program(ins, outs, scratch):
    na = len(ins)
    send_sems, recv_sems, local_sems = scratch[:3]
    bufs = scratch[3:]
    x, y, c = lax.axis_index("x"), lax.axis_index("y"), lax.axis_index("c")
    me = (x, y, c)
    peers = []
    for k in range(1, N_DEV):
        dx, dy, dc = (k >> 2) & 1, (k >> 1) & 1, k & 1
        peers.append(((1 - x) if dx else x, (1 - y) if dy else y, (1 - c) if dc else c))

    def send(a, k):
        return pltpu.make_async_remote_copy(
            src_ref=_slab(ins[a], *peers[k]), dst_ref=_slab(outs[a], *me), send_sem=send_sems.at[a, k],
            recv_sem=recv_sems.at[a, k], device_id=peers[k], device_id_type=MESH)

    def arrival(a, k):
        return pltpu.make_async_remote_copy(
            src_ref=_slab(ins[a], *peers[k]), dst_ref=_slab(outs[a], *peers[k]), send_sem=send_sems.at[a, k],
            recv_sem=recv_sems.at[a, k], device_id=peers[k], device_id_type=MESH)

    def start():
        for a in range(na):
            for k in range(N_DEV - 1):
                send(a, k).start()
        for a in range(na):
            _bounce(_slab(ins[a], *me), _slab(outs[a], *me), bufs[a], local_sems.at[a])

    def finish():
        for a in range(na):
            for k in range(N_DEV - 1):
                arrival(a, k).wait_recv()
        for a in range(na):
            for k in range(N_DEV - 1):
                send(a, k).wait_send()

    return start, finish


def _direct_scratch(parts):
    na = len(parts)
    return [pltpu.SemaphoreType.DMA((na, N_DEV - 1)), pltpu.SemaphoreType.DMA((na, N_DEV - 1)),
            pltpu.SemaphoreType.DMA((na,))] + [pltpu.VMEM(p.shape[1:], p.dtype) for p in parts]


ROW_TILES = D_MODEL // 128


def _rows3(t):
    return jnp.transpose(t[0]).reshape(t.shape[2], ROW_TILES, 128)


def _unrows3(t):
    return jnp.transpose(t.reshape(t.shape[0], D_MODEL))[None]


def _cast_shards(w_in3, w_att, w_ssm, w_o):
    def body(wi_ref, wa_ref, ws_ref, wo_ref, a_ref, b_ref, c_ref, d_ref):
        a_ref[...] = wi_ref[...].reshape(SHARD_IN // 2, 2 * ROW_TILES, 128).astype(BF16)
        b_ref[...] = wa_ref[...].astype(BF16)
        c_ref[...] = ws_ref[...].astype(BF16)
        d_ref[...] = wo_ref[...].astype(BF16)

    return pl.pallas_call(
        body, out_shape=[SDS((SHARD_IN // 2, 2 * ROW_TILES, 128), BF16), SDS(w_att.shape, BF16),
                         SDS(w_ssm.shape, BF16), SDS(w_o.shape, BF16)],
        compiler_params=_cparams(), name="cast_shards")(w_in3, w_att, w_ssm, w_o)


def _pieces():
    out = []
    for r0, c0, w in _SEGS:
        r = r0
        while r < r0 + w:
            d = r // SHARD_IN
            n = min(r0 + w, (d + 1) * SHARD_IN) - r
            out.append((c0 + (r - r0), d, r - d * SHARD_IN, n))
            r += n
    return out


def _to_aligned_t(slabs):
    def body(a_ref, o_ref):
        for (t, d, s, n) in _pieces():
            o_ref[t:t + n, :] = a_ref[d, s // 2:(s + n) // 2].reshape(n, D_MODEL)
        o_ref[C_DT + 32:C_DT + 128, :] = jnp.zeros((96, D_MODEL), slabs.dtype)

    return pl.pallas_call(body, out_shape=SDS((PW, D_MODEL), slabs.dtype), compiler_params=_cparams(),
                          name="to_aligned")(slabs)


def _from_aligned_pair(g):
    slab = (SHARD_IN // 2, 2 * ROW_TILES, 128)
    by_slab = [[p for p in _pieces() if p[1] == d] for d in range(N_DEV)]

    def body(g_ref, own_ref, got_ref, slabs, send_sems, recv_sems, local_sems):
        x, y, c = lax.axis_index("x"), lax.axis_index("y"), lax.axis_index("c")
        sibling = (x, y, 1 - c)

        def to_own(d, k):
            return pltpu.make_async_copy(slabs.at[d], own_ref.at[k], local_sems.at[k])

        def to_sibling(d, k):
            return pltpu.make_async_remote_copy(
                src_ref=slabs.at[d], dst_ref=got_ref.at[k], send_sem=send_sems.at[k], recv_sem=recv_sems.at[k],
                device_id=sibling, device_id_type=MESH)

        for d in range(N_DEV):
            for (t, _, s, n) in by_slab[d]:
                slabs[d, s // 2:(s + n) // 2] = g_ref[t:t + n, :].reshape(n // 2, 2 * ROW_TILES, 128)
            k, side = d // 2, d % 2
            pl.when(c == side)(to_own(d, k).start)
            pl.when(c != side)(to_sibling(d, k).start)
        for k in range(N_CHIP):
            to_own(0, k).wait()
            to_sibling(0, k).wait()

    half = SDS((N_CHIP,) + slab, g.dtype)
    return pl.pallas_call(
        body, in_specs=[pl.BlockSpec(memory_space=pltpu.VMEM)], out_specs=[ANY, ANY], out_shape=[half, half],
        scratch_shapes=[pltpu.VMEM((N_DEV,) + slab, g.dtype), pltpu.SemaphoreType.DMA((N_CHIP,)),
                        pltpu.SemaphoreType.DMA((N_CHIP,)), pltpu.SemaphoreType.DMA((N_CHIP,))],
        compiler_params=_cparams(), name="from_aligned_pair")(g)


_SEGS = [
    (R_Q, C_Q, 1024), (R_K, C_K, 256), (R_V, C_V, 256), (R_ZA, C_ZA, 1024), (R_ZS, C_ZS, 2048),
    (R_XBC, C_XBC, 3072), (R_DT, C_DT, 32), (R_GA, C_GA, 1024), (R_GS, C_GS, 1024)]


def _pad_lanes(v, n=128):
    return jnp.pad(v, ((0, 0), (0, n - v.shape[1])))


def _device_step(h, tgt, w_alt, w_out, g_pre, conv_w8, conv_b, dt_bias, a_log, d_skip, sinks, g_ssm, g_post, on_mesh):
    dtb, al, dsk, snk = _pad_lanes(dt_bias), _pad_lanes(a_log), _pad_lanes(d_skip), _pad_lanes(sinks)
    u = _norm_u(h, g_pre)
    proj = _matmul(u, w_alt, "nt", F32, T, PROJ_TILE, "in_proj")
    o = _attn_fwd(proj, snk)
    if on_mesh:
        sn, states, att_all, ssm_all, o_all = _ssd_fwd(proj, conv_w8, conv_b, dtb, al, dsk, g_ssm, gather=w_out)
        w_att = att_all.reshape(D_MODEL, D_MODEL)
        w_ssm = ssm_all.reshape(SSM_INNER, D_MODEL)
        w_o = o_all.reshape(D_MODEL, D_MODEL)
    else:
        sn, states = _ssd_fwd(proj, conv_w8, conv_b, dtb, al, dsk, g_ssm)
        w_att, w_ssm, w_o = w_out
    a_in, mg, ya, ys, out = _post_a(o, proj, sn, w_att, w_ssm, w_o)
    (loss, dres, dout, dya, dys, do, dproj, dsn, dgp) = _post_b(
        out, h, tgt, proj, ya, ys, o, g_post, w_att, w_ssm, w_o)
    dw_att = _matmul(a_in, dya, "tn", BF16, D_MODEL, D_MODEL, "d_w_att")
    dw_ssm = _matmul(sn, dys, "tn", BF16, D_MODEL, D_MODEL, "d_w_ssm")
    dw_o = _matmul(mg, dout, "tn", BF16, D_MODEL, D_MODEL, "d_w_o")
    res = {}
    if on_mesh:
        parts = [dw_att.reshape(N_DEV, 128, D_MODEL), dw_ssm.reshape(N_DEV, 256, D_MODEL),
                 dw_o.reshape(N_DEV, 128, D_MODEL)]
        (ddt4, dproj, ddtb, dal, ddsk, dgn, dcw, dcb, res["r_att"], res["r_ssm"], res["r_o"]) = _ssd_bwd(
            proj, conv_w8, conv_b, dtb, al, dsk, g_ssm, states, dsn, dproj, exchange=parts)
    else:
        ddt4, dproj, ddtb, dal, ddsk, dgn, dcw, dcb = _ssd_bwd(proj, conv_w8, conv_b, dtb, al, dsk, g_ssm, states,
                                                               dsn, dproj)
        res.update(dw_att=dw_att, dw_ssm=dw_ssm, dw_o=dw_o)
    dproj, dk, dv, dsink = _attn_bwd(proj, snk, do, dproj)
    dproj = _dproj_tail(dproj, dk, dv, ddt4)
    dw_alt = _matmul(dproj, u, "tn", BF16, PROJ_TILE, D_MODEL, "d_w_in")
    if on_mesh:
        own, got = _from_aligned_pair(dw_alt)
        dh, dgpre, res["r_in"] = _d_u_norm(dproj, w_alt, h, g_pre, dres,
                                           chips=_pair_sum([own], [got], "pair_sum_w_in"))
    else:
        dh, dgpre = _d_u_norm(dproj, w_alt, h, g_pre, dres)
        res["dw_alt"] = dw_alt
    small = (dgpre, dcb, ddtb, dal, ddsk, dsink, dgn, dgp, dcw)
    if on_mesh:
        res["small_pack"] = _small_pack(*small, loss, dh)
    else:
        res["small"] = small
    res.update(loss=loss[0, 0], dh=dh)
    return res


def kernel(x, meta_tokens, g_pre, w_in, conv_w, conv_b, dt_bias, a_log, d_skip, attn_sinks, g_ssm_norm, w_out_att, w_out_ssm, w_out, g_post, loss_target, m_meta_tokens, m_g_pre, m_w_in, m_conv_w, m_conv_b, m_dt_bias, m_a_log, m_d_skip, m_attn_sinks, m_g_ssm_norm, m_w_out_att, m_w_out_ssm, m_w_out, m_g_post, v_meta_tokens, v_g_pre, v_w_in, v_conv_w, v_conv_b, v_dt_bias, v_a_log, v_d_skip, v_attn_sinks, v_g_ssm_norm, v_w_out_att, v_w_out_ssm, v_w_out, v_g_post):
    w_in3, m_in3, v_in3 = _rows3(w_in), _rows3(m_w_in), _rows3(v_w_in)
    a_sh, att_sh, ssm_sh, o_sh = _cast_shards(w_in3, w_out_att[0], w_out_ssm[0], w_out[0])
    cw_sh = jnp.pad(conv_w[0], ((0, 4), (0, 0)))
    a_all, meta_all, cw_all = _all_gather([a_sh, meta_tokens, cw_sh])
    w_alt = _to_aligned_t(a_all)
    meta_full = meta_all.transpose(1, 0, 2).reshape(N_META, D_MODEL)
    conv_w8 = cw_all.transpose(1, 0, 2).reshape(8, CONV_DIM)

    h = jnp.concatenate([jnp.zeros((PAD, D_MODEL), F32), meta_full, x[0]], axis=0)
    tgt = jnp.concatenate([jnp.zeros((PAD + N_META, D_MODEL), F32), loss_target[0]], axis=0)
    r = _device_step(h, tgt, w_alt, (att_sh, ssm_sh, o_sh), g_pre, conv_w8, conv_b, dt_bias, a_log, d_skip,
                     attn_sinks, g_ssm_norm, g_post, True)
    grad_x = r["dh"][PAD + N_META:][None]

    *res_in, r_small = _sum_adamw_rows3(r["r_in"], w_in3, m_in3, v_in3, "adamw_w_in", exchange=[r["small_pack"]])
    res_in = [_unrows3(t) for t in res_in]
    res_att = [t[None] for t in _sum_adamw(r["r_att"], w_out_att[0], m_w_out_att[0], v_w_out_att[0], 512,
                                           "adamw_w_att")]
    res_ssm = [t[None] for t in _sum_adamw(r["r_ssm"], w_out_ssm[0], m_w_out_ssm[0], v_w_out_ssm[0], 512,
                                           "adamw_w_ssm")]
    res_o = [t[None] for t in _sum_adamw(r["r_o"], w_out[0], m_w_out[0], v_w_out[0], 512, "adamw_w_o")]
    (res_gpre, res_convb, res_dtb, res_alog, res_dskip, res_sink, res_gssm, res_gpost, res_cw, res_meta), loss = _small_finish(
        r_small, [(g_pre, m_g_pre, v_g_pre), (conv_b, m_conv_b, v_conv_b), (dt_bias, m_dt_bias, v_dt_bias),
                       (a_log, m_a_log, v_a_log), (d_skip, m_d_skip, v_d_skip),
                       (attn_sinks, m_attn_sinks, v_attn_sinks), (g_ssm_norm, m_g_ssm_norm, v_g_ssm_norm),
                       (g_post, m_g_post, v_g_post), (conv_w[0], m_conv_w[0], v_conv_w[0]),
                       (meta_tokens, m_meta_tokens, v_meta_tokens)])
    res_cw = [t[None] for t in res_cw]
    per_weight = [res_meta, res_gpre, res_in, res_cw, res_convb, res_dtb, res_alog, res_dskip, res_sink, res_gssm,
                  res_att, res_ssm, res_o, res_gpost]
    return (loss[0, 0], grad_x, *[p[0] for p in per_weight], *[p[1] for p in per_weight], *[p[2] for p in per_weight],
            *[p[3] for p in per_weight])
```

```python
import jax
import jax.numpy as jnp
import numpy as np
from jax import lax
from jax.experimental import pallas as pl
from jax.experimental.pallas import tpu as pltpu

F32 = jnp.float32
BF16 = jnp.bfloat16
SDS = jax.ShapeDtypeStruct
MESH = pl.DeviceIdType.MESH
ANY = pl.BlockSpec(memory_space=pl.ANY)

N_DEV = 8
D_MODEL = 1024
SEQ = 2048
N_META = 16
BLK = 128
PAD = 112
T = PAD + N_META + SEQ
NB = T // BLK
EPS = 1e-6
HEAD = 64
Q_HEADS = 16
KV_HEADS = 4
GROUP = 4
KV_W = 256
SSM_INNER = 2048
SSM_HEADS = 32
SSM_GROUPS = 4
GRP_W = 512
SSM_STATE = 128
CONV_DIM = 3072
IN_PROJ = 9760
SHARD_IN = IN_PROJ // N_DEV
NEG = -1e30

C_ZS, C_XBC, C_Q, C_ZA, C_GA, C_GS, C_K, C_V, C_DT = 0, 2048, 5120, 6144, 7168, 8192, 9216, 9472, 9728
PW = 9856
GATES_W = 3 * D_MODEL
PROJ_TILE = 1408
R_Q, R_K, R_V, R_ZA, R_ZS, R_XBC, R_DT, R_GA, R_GS = 0, 1024, 1280, 1536, 2560, 4608, 7680, 7712, 8736

ADAM_LR, ADAM_B1, ADAM_B2, ADAM_EPS, ADAM_WD, ADAM_STEP = 0.001, 0.9, 0.999, 1e-08, 0.01, 10

VMEM_LIMIT = 56 * 1024 * 1024


def _cparams():
    return pltpu.CompilerParams(vmem_limit_bytes=VMEM_LIMIT)


def _silu(x):
    return x * jax.nn.sigmoid(x)


def _dsilu(x):
    s = jax.nn.sigmoid(x)
    return s * (1.0 + x * (1.0 - s))


def _matmul(a, b, mode, out_dtype, tm, tn, name):
    if mode == "nt":
        (m, k), n = a.shape, b.shape[0]
        a_spec = pl.BlockSpec((tm, k), lambda i, j: (i, 0))
        b_spec = pl.BlockSpec((tn, k), lambda i, j: (j, 0))
        dims = (((1,), (1,)), ((), ()))
    else:
        assert mode == "tn"
        (k, m), n = a.shape, b.shape[1]
        a_spec = pl.BlockSpec((k, tm), lambda i, j: (0, i))
        b_spec = pl.BlockSpec((k, tn), lambda i, j: (0, j))
        dims = (((0,), (0,)), ((), ()))
    assert m % tm == 0 and n % tn == 0, (a.shape, b.shape, tm, tn)

    def body(a_ref, b_ref, o_ref):
        o_ref[...] = lax.dot_general(a_ref[...], b_ref[...], dims, preferred_element_type=F32).astype(out_dtype)

    return pl.pallas_call(
        body, grid=(m // tm, n // tn), in_specs=[a_spec, b_spec],
        out_specs=pl.BlockSpec((tm, tn), lambda i, j: (i, j)), out_shape=SDS((m, n), out_dtype),
        compiler_params=_cparams(), name=name)(a, b)


def _norm_u(h, g_pre):
    def body(h_ref, g_ref, u_ref):
        x = h_ref[...]
        r = lax.rsqrt(jnp.mean(x * x, axis=-1, keepdims=True) + EPS)
        u_ref[...] = (x * r * g_ref[...]).astype(BF16)

    return pl.pallas_call(
        body, grid=(NB,),
        in_specs=[pl.BlockSpec((BLK, D_MODEL), lambda i: (i, 0)), pl.BlockSpec((1, D_MODEL), lambda i: (0, 0))],
        out_specs=pl.BlockSpec((BLK, D_MODEL), lambda i: (i, 0)),
        out_shape=SDS((T, D_MODEL), BF16), name="norm_u")(h, g_pre)


DU_TM, DU_TK = T // 2, PROJ_TILE


def _d_u_norm(dproj, w_alt, h, g_pre, dres, chips=()):
    nk = PW // DU_TK
    ni = T // DU_TM
    nc = len(chips)

    def body(*refs):
        a_ref, b_ref, h_ref, g_ref, dres_ref = refs[:5]
        dh_ref, dg_ref = refs[5 + nc:7 + nc]
        acc_ref = refs[7 + 2 * nc]
        i, kk = pl.program_id(0), pl.program_id(1)
        if nc:
            ch_start, ch_finish = _chips_program(refs[5:5 + nc], refs[7 + nc:7 + 2 * nc], refs[8 + 2 * nc:])
            pl.when((i == 0) & (kk == 0))(ch_start)
        part = jnp.dot(a_ref[...], b_ref[...], preferred_element_type=F32)

        @pl.when(kk == 0)
        def _():
            acc_ref[...] = part

        @pl.when((kk > 0) & (kk < nk - 1))
        def _():
            acc_ref[...] += part

        @pl.when(kk == nk - 1)
        def _():
            du_ = acc_ref[...] + part
            x = h_ref[...]
            r = lax.rsqrt(jnp.mean(x * x, axis=-1, keepdims=True) + EPS)
            gd = g_ref[...] * du_
            dx = r * gd - x * (r * r * r) * jnp.mean(x * gd, axis=-1, keepdims=True)
            dh_ref[...] = dx + dres_ref[...]
            gpart = jnp.concatenate([jnp.sum(du_ * x * r, axis=0, keepdims=True), jnp.zeros((7, D_MODEL), F32)],
                                    axis=0)

            @pl.when(i == 0)
            def _():
                dg_ref[...] = gpart

            @pl.when(i > 0)
            def _():
                dg_ref[...] += gpart

        if nc:
            pl.when((i == ni - 1) & (kk == nk - 1))(ch_finish)

    row = pl.BlockSpec((DU_TM, D_MODEL), lambda i, kk: (i, 0))
    return pl.pallas_call(
        body, grid=(ni, nk),
        in_specs=[pl.BlockSpec((DU_TM, DU_TK), lambda i, kk: (i, kk)),
                  pl.BlockSpec((DU_TK, D_MODEL), lambda i, kk: (kk, 0)),
                  row, pl.BlockSpec((1, D_MODEL), lambda i, kk: (0, 0)), row] + [ANY] * nc,
        out_specs=[row, pl.BlockSpec((8, D_MODEL), lambda i, kk: (0, 0))] + [ANY] * nc,
        out_shape=[SDS((T, D_MODEL), F32), SDS((8, D_MODEL), F32)] + [SDS(p.shape, p.dtype) for p in chips],
        scratch_shapes=[pltpu.VMEM((DU_TM, D_MODEL), F32)] + (_chips_scratch(chips) if nc else []),
        compiler_params=_cparams(), name="d_u_norm")(dproj, w_alt, h, g_pre, dres, *chips)


def _lane_pick(row, h):
    lane = lax.broadcasted_iota(jnp.int32, row.shape, 1)
    return jnp.sum(jnp.where(lane == h, row, 0.0), axis=1, keepdims=True)


def _alibi_band():
    r = np.arange(GROUP * BLK)[:, None]
    rel = (r % BLK) - np.arange(2 * BLK)[None, :] + BLK
    out = np.empty((KV_HEADS, GROUP * BLK, 2 * BLK), np.float32)
    for kh in range(KV_HEADS):
        slope = (2.0 ** (-8.0 * (kh * GROUP + r // BLK + 1) / Q_HEADS)).astype(np.float32)
        out[kh] = np.where((rel >= 0) & (rel < BLK), -slope * rel.astype(np.float32), np.float32(NEG))
    return jnp.asarray(out)


def _attn_fn(q4s, kcats, vcats, kms, vms, sinks, n, band):
    s = lax.broadcasted_iota(jnp.int32, (GROUP * BLK, 2 * BLK), 1)
    key_off = jnp.where(n * BLK - BLK + s >= PAD + N_META, 0.0, NEG)
    rm = lax.broadcasted_iota(jnp.int32, (GROUP * BLK, N_META), 0)
    mm = lax.broadcasted_iota(jnp.int32, (GROUP * BLK, N_META), 1)
    meta_ok = (PAD + mm) <= (n * BLK + jnp.bitwise_and(rm, BLK - 1))
    gcol = jnp.right_shift(lax.broadcasted_iota(jnp.int32, (GROUP * BLK, 1), 0), 7)
    outs = []
    for kh in range(KV_HEADS):
        sk = [_lane_pick(sinks, kh * GROUP + g) for g in range(GROUP)]
        sink = jnp.where(gcol == 0, sk[0], jnp.where(gcol == 1, sk[1], jnp.where(gcol == 2, sk[2], sk[3])))
        qb = (q4s[kh] * (HEAD ** -0.5)).astype(BF16)
        sb = lax.dot_general(qb, kcats[kh].astype(BF16), (((1,), (1,)), ((), ())), preferred_element_type=F32)
        sb = sb + (band[kh] + key_off)
        sm = lax.dot_general(qb, kms[kh].astype(BF16), (((1,), (1,)), ((), ())), preferred_element_type=F32)
        sm = jnp.where(meta_ok, sm, NEG)
        mx = jnp.maximum(jnp.maximum(jnp.max(sb, axis=1, keepdims=True), jnp.max(sm, axis=1, keepdims=True)), sink)
        mx = lax.stop_gradient(mx)
        eb = jnp.exp(sb - mx)
        em = jnp.exp(sm - mx)
        es = jnp.exp(sink - mx)
        inv = 1.0 / (jnp.sum(eb, axis=1, keepdims=True) + jnp.sum(em, axis=1, keepdims=True) + es)
        pb = (eb * inv).astype(BF16)
        pm = (em * inv).astype(BF16)
        o4 = (jnp.dot(pm, vms[kh].astype(BF16), preferred_element_type=F32)
              + jnp.dot(pb, vcats[kh].astype(BF16), preferred_element_type=F32))
        outs.append(o4)
    return outs


def _attn_specs():
    prev = lambda n: jnp.maximum(n - 1, 0)
    return [
        pl.BlockSpec((BLK, D_MODEL), lambda n: (n, C_Q // D_MODEL)),
        pl.BlockSpec((BLK, KV_W), lambda n: (prev(n), C_K // KV_W)),
        pl.BlockSpec((BLK, KV_W), lambda n: (n, C_K // KV_W)),
        pl.BlockSpec((BLK, KV_W), lambda n: (prev(n), C_V // KV_W)),
        pl.BlockSpec((BLK, KV_W), lambda n: (n, C_V // KV_W)),
        pl.BlockSpec((N_META, KV_W), lambda n: (PAD // N_META, C_K // KV_W)),
        pl.BlockSpec((N_META, KV_W), lambda n: (PAD // N_META, C_V // KV_W)),
        pl.BlockSpec((1, 128), lambda n: (0, 0)),
        pl.BlockSpec((KV_HEADS, GROUP * BLK, 2 * BLK), lambda n: (0, 0, 0)),
    ]


def _attn_load(q_ref, kp_ref, kc_ref, vp_ref, vc_ref, km_ref, vm_ref):
    q4s, kcats, vcats, kms, vms = [], [], [], [], []
    for kh in range(KV_HEADS):
        q4s.append(jnp.concatenate(
            [q_ref[:, (kh * GROUP + g) * HEAD:(kh * GROUP + g + 1) * HEAD] for g in range(GROUP)], axis=0))
        cs = slice(kh * HEAD, (kh + 1) * HEAD)
        kcats.append(jnp.concatenate([kp_ref[:, cs], kc_ref[:, cs]], axis=0))
        vcats.append(jnp.concatenate([vp_ref[:, cs], vc_ref[:, cs]], axis=0))
        kms.append(km_ref[:, cs])
        vms.append(vm_ref[:, cs])
    return q4s, kcats, vcats, kms, vms


def _attn_fwd(proj, sinks):
    def body(q_ref, kp_ref, kc_ref, vp_ref, vc_ref, km_ref, vm_ref, s_ref, band_ref, o_ref):
        n = pl.program_id(0)
        args = _attn_load(q_ref, kp_ref, kc_ref, vp_ref, vc_ref, km_ref, vm_ref)
        outs = _attn_fn(*args, s_ref[...], n, [band_ref[kh] for kh in range(KV_HEADS)])
        for kh in range(KV_HEADS):
            for g in range(GROUP):
                hh = kh * GROUP + g
                o_ref[:, hh * HEAD:(hh + 1) * HEAD] = outs[kh][g * BLK:(g + 1) * BLK]

    return pl.pallas_call(
        body, grid=(NB,), in_specs=_attn_specs(),
        out_specs=pl.BlockSpec((BLK, D_MODEL), lambda n: (n, 0)),
        out_shape=SDS((T, D_MODEL), F32), name="attn_fwd")(proj, proj, proj, proj, proj, proj, proj, sinks,
                                                            _alibi_band())


def _attn_bwd(proj, sinks, do, dproj):
    def body(q_ref, kp_ref, kc_ref, vp_ref, vc_ref, km_ref, vm_ref, s_ref, band_ref, do_ref, _, dq_ref, dk_ref, dv_ref,
             ds_ref):
        n = pl.program_id(0)
        band = [band_ref[kh] for kh in range(KV_HEADS)]

        @pl.when(n == 0)
        def _():
            dk_ref[...] = jnp.zeros_like(dk_ref)
            dv_ref[...] = jnp.zeros_like(dv_ref)
            ds_ref[...] = jnp.zeros_like(ds_ref)

        args = _attn_load(q_ref, kp_ref, kc_ref, vp_ref, vc_ref, km_ref, vm_ref)
        _, vjp = jax.vjp(lambda a, b, c, d, e, f: _attn_fn(a, b, c, d, e, f, n, band), *args, s_ref[...])
        do_f = do_ref[...].astype(F32)
        cot = [jnp.concatenate([do_f[:, (kh * GROUP + g) * HEAD:(kh * GROUP + g + 1) * HEAD] for g in range(GROUP)],
                               axis=0) for kh in range(KV_HEADS)]
        dq4s, dkcats, dvcats, dkms, dvms, dsk = vjp(cot)
        ds_ref[0:1, :] += dsk
        cur = pl.ds(pl.multiple_of(n * BLK, BLK), BLK)
        meta = slice(PAD, PAD + N_META)
        for kh in range(KV_HEADS):
            cs = slice(kh * HEAD, (kh + 1) * HEAD)
            for g in range(GROUP):
                hh = kh * GROUP + g
                dq_ref[:, hh * HEAD:(hh + 1) * HEAD] = dq4s[kh][g * BLK:(g + 1) * BLK].astype(BF16)
            dk_ref[cur, cs] += dkcats[kh][BLK:]
            dv_ref[cur, cs] += dvcats[kh][BLK:]
            dk_ref[meta, cs] += dkms[kh]
            dv_ref[meta, cs] += dvms[kh]

        @pl.when(n > 0)
        def _():
            prv = pl.ds(pl.multiple_of((n - 1) * BLK, BLK), BLK)
            for kh in range(KV_HEADS):
                cs = slice(kh * HEAD, (kh + 1) * HEAD)
                dk_ref[prv, cs] += dkcats[kh][:BLK]
                dv_ref[prv, cs] += dvcats[kh][:BLK]

    full_kv = pl.BlockSpec((T, KV_W), lambda n: (0, 0))
    return pl.pallas_call(
        body, grid=(NB,),
        in_specs=_attn_specs() + [pl.BlockSpec((BLK, D_MODEL), lambda n: (n, 0)), ANY],
        out_specs=[pl.BlockSpec((BLK, D_MODEL), lambda n: (n, C_Q // D_MODEL)), full_kv, full_kv,
                   pl.BlockSpec((8, 128), lambda n: (0, 0))],
        out_shape=[SDS((T, PW), BF16), SDS((T, KV_W), F32), SDS((T, KV_W), F32), SDS((8, 128), F32)],
        input_output_aliases={10: 0},
        name="attn_bwd")(proj, proj, proj, proj, proj, proj, proj, sinks, _alibi_band(), do, dproj)


def _rows_from(ext, start):
    if start % 8 == 0:
        return ext[start:start + BLK]
    return pltpu.roll(ext, (8 - start) % (BLK + 8), 0)[8:8 + BLK]


def _conv_taps(taps, w):
    return w[0:1] * taps[0] + w[1:2] * taps[1] + w[2:3] * taps[2] + w[3:4] * taps[3]


HPG = SSM_HEADS // SSM_GROUPS


def _iota(shape, dim):
    return lax.broadcasted_iota(jnp.int32, shape, dim)


def _mm(a, b, ca=1, cb=0):
    return lax.dot_general(a.astype(BF16), b.astype(BF16), (((ca,), (cb,)), ((), ())), preferred_element_type=F32)


def _split3(v):
    hi = v.astype(BF16)
    r1 = v - hi.astype(F32)
    mid = r1.astype(BF16)
    lo = (r1 - mid.astype(F32)).astype(BF16)
    return hi, mid, lo


def _split2(v):
    hi = v.astype(BF16)
    return hi, (v - hi.astype(F32)).astype(BF16)


def _sel_r(parts, onehot, ca=1, cb=0):
    out = lax.dot_general(parts[0], onehot, (((ca,), (cb,)), ((), ())), preferred_element_type=F32)
    for p in parts[1:]:
        out = out + lax.dot_general(p, onehot, (((ca,), (cb,)), ((), ())), preferred_element_type=F32)
    return out


def _sel_l(onehot, parts):
    out = jnp.dot(onehot, parts[0], preferred_element_type=F32)
    for p in parts[1:]:
        out = out + jnp.dot(onehot, p, preferred_element_type=F32)
    return out


def _rows8(*rows):
    r = _iota((8, rows[0].shape[1]), 0)
    out = jnp.zeros((8, rows[0].shape[1]), F32)
    for k, v in enumerate(rows):
        out = jnp.where(r == k, v, out)
    return out


def _ssd_forward(x, z, bm, cm, dt_raw, st_prev, dtb, alog, dskip, gn, g, cst_scr):
    li, si = _iota((BLK, BLK), 0), _iota((BLK, BLK), 1)
    dt_all = jax.nn.softplus(dt_raw + dtb)
    a_row = -jnp.exp(alog)
    a_all = dt_all * a_row
    cs_all = _sel_l((li >= si).astype(BF16), _split3(a_all))
    cs_parts = _split3(cs_all)
    spread = (_iota((BLK, GRP_W), 0) == g * HPG + jnp.right_shift(_iota((BLK, GRP_W), 1), 6)).astype(BF16)
    dt_e = _sel_r(_split2(dt_all), spread)
    cs_e = _sel_r(cs_parts, spread)
    d_e = _sel_r(_split2(_rows8(dskip)), spread)[0:1]
    cs_last_e = jnp.sum(jnp.where(_iota((BLK, GRP_W), 0) == BLK - 1, cs_e, 0.0), axis=0, keepdims=True)
    p_e = jnp.exp(cs_e)
    w_e = jnp.exp(cs_last_e - cs_e)
    cd_e = jnp.exp(cs_last_e)
    xr = x * dt_e
    cst_scr[...] = cs_all.T
    cst_g = cst_scr[g * HPG:(g + 1) * HPG, :]
    own = jnp.right_shift(_iota((HPG, HPG * BLK), 1), 7) == _iota((HPG, HPG * BLK), 0)
    ownf = own.astype(F32)
    q_rows = [ownf, ownf, ownf] + [jnp.where(own, jnp.concatenate([p.astype(F32)] * HPG, axis=1), 0.0)
                                   for p in _split3(cst_g)]
    q2 = jnp.concatenate(q_rows + [jnp.zeros((BLK - 6 * HPG, HPG * BLK), F32)], axis=0).astype(BF16)
    lane1 = _iota((1, BLK), 1)
    p2 = jnp.where((lane1 >= 3 * HPG) & (lane1 < 6 * HPG), -1.0, 0.0)
    for k, part in enumerate(cs_parts):
        pick = ((li == g * HPG + si - k * HPG) & (si >= k * HPG) & (si < (k + 1) * HPG)).astype(BF16)
        p2 = p2 + jnp.dot(part, pick, preferred_element_type=F32)
    dmat = jnp.dot(p2.astype(BF16), q2, preferred_element_type=F32)
    causal = _iota((BLK, HPG * BLK), 0) >= jnp.bitwise_and(_iota((BLK, HPG * BLK), 1), BLK - 1)
    lam = jnp.exp(jnp.where(causal, dmat, NEG))
    gmat = _mm(cm, bm, 1, 1)
    m_all = lam * jnp.concatenate([gmat] * HPG, axis=1)
    mb = m_all.astype(BF16)
    lo = _iota((BLK, BLK), 1) < HEAD
    xrb = xr.astype(BF16)
    zero = jnp.zeros((BLK, BLK), BF16)
    bds, yd = [], []
    for i in range(HPG // 2):
        t = xrb[:, BLK * i:BLK * (i + 1)]
        bd = jnp.concatenate([jnp.where(lo, t, zero), jnp.where(lo, zero, t)], axis=0)
        bds.append(bd)
        yd.append(jnp.dot(mb[:, 2 * BLK * i:2 * BLK * (i + 1)], bd, preferred_element_type=F32))
    cs_st = _mm(cm, st_prev)
    y = jnp.concatenate(yd, axis=1) + cs_st * p_e + d_e * x
    xrw = xr * w_e
    st_new = cd_e * st_prev + _mm(bm, xrw, 0, 0)
    yz = y * _silu(z)
    rn = lax.rsqrt(jnp.sum(yz * yz, axis=1, keepdims=True) / GRP_W + EPS)
    return dict(out=yz * rn * gn, st_new=st_new, dt_all=dt_all, a_row=a_row, dt_e=dt_e, d_e=d_e, p_e=p_e, w_e=w_e,
                cd_e=cd_e, xr=xr, xrw=xrw, lam=lam, m_all=m_all, mb=mb, bds=bds, cs_st=cs_st, y=y, yz=yz, rn=rn, lo=lo)


def _ssd_backward(f, x, z, bm, cm, dt_raw, st_prev, dtb, gn, g, dout, dst_next, cst_scr):
    li, si = _iota((BLK, BLK), 0), _iota((BLK, BLK), 1)
    yz, rn, y, p_e, w_e, cd_e, xr = f["yz"], f["rn"], f["y"], f["p_e"], f["w_e"], f["cd_e"], f["xr"]
    dgn = jnp.sum(dout * yz * rn, axis=0, keepdims=True)
    t = dout * gn
    dyz = rn * t - yz * (rn * rn * rn) * (jnp.sum(yz * t, axis=1, keepdims=True) / GRP_W)
    dy = dyz * _silu(z)
    dz = dyz * y * _dsilu(z)
    dx = f["d_e"] * dy
    dd_e = jnp.sum(dy * x, axis=0, keepdims=True)
    dcsst = dy * p_e
    dp_e = dy * f["cs_st"]
    dcm = _mm(dcsst, st_prev, 1, 1)
    dst_prev = _mm(cm, dcsst, 0, 0) + cd_e * dst_next
    dcd_e = jnp.sum(dst_next * st_prev, axis=0, keepdims=True)
    dbm = _mm(f["xrw"], dst_next, 1, 1)
    dxrw = _mm(bm, dst_next)
    dxr = dxrw * w_e
    dw_e = dxrw * xr
    dyb = dy.astype(BF16)
    dms, dxr_d = [], []
    for i in range(HPG // 2):
        dyp = dyb[:, BLK * i:BLK * (i + 1)]
        dms.append(lax.dot_general(dyp, f["bds"][i], (((1,), (1,)), ((), ())), preferred_element_type=F32))
        r = lax.dot_general(f["mb"][:, 2 * BLK * i:2 * BLK * (i + 1)], dyp, (((0,), (0,)), ((), ())),
                            preferred_element_type=F32)
        dxr_d.append(jnp.where(f["lo"], r[0:BLK], r[BLK:2 * BLK]))
    dm_all = jnp.concatenate(dms, axis=1)
    dxr = dxr + jnp.concatenate(dxr_d, axis=1)
    dlg = dm_all * f["lam"]
    dg = dlg[:, 0:BLK]
    for j in range(1, HPG):
        dg = dg + dlg[:, BLK * j:BLK * (j + 1)]
    dcm = dcm + _mm(dg, bm)
    dbm = dbm + _mm(dg, cm, 0, 0)
    q_all = dm_all * f["m_all"]
    col_sums = jnp.sum(q_all, axis=0, keepdims=True)
    cst_scr[...] = jnp.zeros_like(cst_scr)
    cst_scr[g * HPG:(g + 1) * HPG, :] = _rows8(
        *[col_sums[:, BLK * j:BLK * (j + 1)] for j in range(HPG)])
    dcs = -cst_scr[...].T
    for j in range(HPG):
        dcs = dcs + jnp.where(si == g * HPG + j,
                              jnp.sum(q_all[:, BLK * j:BLK * (j + 1)], axis=1, keepdims=True), 0.0)
    unspread = (_iota((GRP_W, BLK), 1) == g * HPG + jnp.right_shift(_iota((GRP_W, BLK), 0), 6)).astype(BF16)
    dww = dw_e * w_e
    per_head = _sel_r(_split2(jnp.concatenate([dp_e * p_e - dww, dxr * x], axis=0)), unspread)
    last = _sel_r(_split2(_rows8(jnp.sum(dww, axis=0, keepdims=True) + dcd_e * cd_e, dd_e)), unspread)
    dcs = dcs + per_head[0:BLK] + jnp.where(li == BLK - 1, last[0:1], 0.0)
    da = _sel_l((si >= li).astype(BF16), _split2(dcs))
    ddt_all = da * f["a_row"] + per_head[BLK:2 * BLK]
    dalog = jnp.sum(da * f["dt_all"], axis=0, keepdims=True) * f["a_row"]
    dx = dx + dxr * f["dt_e"]
    ddt_raw = ddt_all * jax.nn.sigmoid(dt_raw + dtb)
    ddtb = jnp.sum(ddt_raw, axis=0, keepdims=True)
    ddskip = last[1:2]
    return dict(dx=dx, dz=dz, dbm=dbm, dcm=dcm, ddt_raw=ddt_raw, dst_prev=dst_prev, ddtb=ddtb, dalog=dalog,
                ddskip=ddskip, dgn=dgn)


ZX_W = SSM_INNER + CONV_DIM
assert C_ZS == 0 and C_XBC == SSM_INNER


def _ssd_in_specs(rev):
    cidx = (lambda c: NB - 1 - c) if rev else (lambda c: c)
    return [
        pl.BlockSpec((BLK, ZX_W), lambda c: (cidx(c), 0)),
        pl.BlockSpec((8, ZX_W), lambda c: (jnp.maximum(cidx(c) * (BLK // 8) - 1, 0), 0)),
        pl.BlockSpec((BLK, 128), lambda c: (cidx(c), C_DT // 128)),
        pl.BlockSpec((8, CONV_DIM), lambda c: (0, 0)),
        pl.BlockSpec((1, CONV_DIM), lambda c: (0, 0)),
        pl.BlockSpec((1, 128), lambda c: (0, 0)),
        pl.BlockSpec((1, 128), lambda c: (0, 0)),
        pl.BlockSpec((1, 128), lambda c: (0, 0)),
        pl.BlockSpec((1, SSM_INNER), lambda c: (0, 0)),
    ]


def _xbc_act(zx_ref, tail_ref, w_ref, b_ref, n):
    tail = jnp.where(n > 0, tail_ref[:, SSM_INNER:], 0.0)
    xp = jnp.concatenate([tail, zx_ref[:, SSM_INNER:]], axis=0)
    taps = [_rows_from(xp, 5 + k) for k in range(4)]
    conv = _conv_taps(taps, w_ref[...]) + b_ref[...]
    valid = n * BLK + _iota((BLK, 1), 0) >= PAD
    return taps, conv, valid, jnp.where(valid, _silu(conv), 0.0)


def _grp_cols(act, i):
    b0, c0 = SSM_INNER + i * SSM_STATE, SSM_INNER + (SSM_GROUPS + i) * SSM_STATE
    return act[:, i * GRP_W:(i + 1) * GRP_W], act[:, b0:b0 + SSM_STATE], act[:, c0:c0 + SSM_STATE]


def _ssd_fwd(proj, conv_w, conv_b, dt_bias, a_log, d_skip, g_norm, gather=()):
    ng = len(gather)

    def body(*refs):
        zx_ref, tail_ref, dt_ref, w_ref, b_ref, dtb_ref, al_ref, dsk_ref, gn_ref = refs[:9]
        y_ref, st_ref = refs[9 + ng:11 + ng]
        s_scr, cst_scr = refs[11 + 2 * ng:13 + 2 * ng]
        c = pl.program_id(0)
        if ng:
            ag_start, ag_forward, ag_finish = _ag_program(refs[9:9 + ng], refs[11 + ng:11 + 2 * ng],
                                                          refs[13 + 2 * ng:])
            pl.when(c == 0)(ag_start)
            pl.when(c == (3 * NB) // 4)(ag_forward)

        @pl.when(c == 0)
        def _():
            s_scr[...] = jnp.zeros_like(s_scr)

        _, _, _, act = _xbc_act(zx_ref, tail_ref, w_ref, b_ref, c)
        for i in range(SSM_GROUPS):
            st_prev = s_scr[i]
            st_ref[i, 0] = st_prev
            x, bm, cm = _grp_cols(act, i)
            f = _ssd_forward(x, zx_ref[:, i * GRP_W:(i + 1) * GRP_W], bm, cm, dt_ref[...], st_prev, dtb_ref[...],
                             al_ref[...], dsk_ref[...], gn_ref[:, i * GRP_W:(i + 1) * GRP_W], i, cst_scr.at[i])
            y_ref[:, i * GRP_W:(i + 1) * GRP_W] = f["out"].astype(BF16)
            s_scr[i] = f["st_new"]
        if ng:
            pl.when(c == NB - 1)(ag_finish)

    return pl.pallas_call(
        body, grid=(NB,), in_specs=_ssd_in_specs(False) + [ANY] * ng,
        out_specs=[pl.BlockSpec((BLK, SSM_INNER), lambda c: (c, 0)),
                   pl.BlockSpec((SSM_GROUPS, 1, SSM_STATE, GRP_W), lambda c: (0, c, 0, 0))] + [ANY] * ng,
        out_shape=[SDS((T, SSM_INNER), BF16), SDS((SSM_GROUPS, NB, SSM_STATE, GRP_W), F32)]
        + [SDS((N_DEV,) + s.shape, s.dtype) for s in gather],
        scratch_shapes=[pltpu.VMEM((SSM_GROUPS, SSM_STATE, GRP_W), F32), pltpu.VMEM((SSM_GROUPS, BLK, BLK), F32)]
        + (_ag_scratch(gather) if ng else []),
        compiler_params=_cparams(),
        name="ssd_fwd")(proj, proj, proj, conv_w, conv_b, dt_bias, a_log, d_skip, g_norm, *gather)


def _ssd_bwd(proj, conv_w, conv_b, dt_bias, a_log, d_skip, g_norm, states, dy, dproj, exchange=()):
    ne = len(exchange)

    def body(*refs):
        zx_ref, tail_ref, dt_ref, w_ref, b_ref, dtb_ref, al_ref, dsk_ref, gn_ref, st_ref, dy_ref = refs[:11]
        (ddt_ref, dp_ref, ddtb_ref, dal_ref, ddsk_ref, dgn_ref, dcw_ref, dcb_ref) = refs[12 + ne:20 + ne]
        ds_scr, cst_scr, carry = refs[20 + 2 * ne:23 + 2 * ne]
        c = pl.program_id(0)
        n = NB - 1 - c
        if ne:
            ex_start, ex_finish = _direct_program(refs[12:12 + ne], refs[20 + ne:20 + 2 * ne], refs[23 + 2 * ne:])
            pl.when(c == 0)(ex_start)

        @pl.when(c == 0)
        def _():
            for ref in (ds_scr, carry, dgn_ref, ddtb_ref, dal_ref, ddsk_ref, dcw_ref, dcb_ref):
                ref[...] = jnp.zeros_like(ref)

        taps, conv, valid, act = _xbc_act(zx_ref, tail_ref, w_ref, b_ref, n)
        dt_raw = dt_ref[...]
        dxs, dbs, dcs = [], [], []
        for i in range(SSM_GROUPS):
            x, bm, cm = _grp_cols(act, i)
            z, gn, st_prev = zx_ref[:, i * GRP_W:(i + 1) * GRP_W], gn_ref[:, i * GRP_W:(i + 1) * GRP_W], st_ref[i, 0]
            f = _ssd_forward(x, z, bm, cm, dt_raw, st_prev, dtb_ref[...], al_ref[...], dsk_ref[...], gn, i,
                             cst_scr.at[i])
            d = _ssd_backward(f, x, z, bm, cm, dt_raw, st_prev, dtb_ref[...], gn, i,
                              dy_ref[:, i * GRP_W:(i + 1) * GRP_W].astype(F32), ds_scr[i], cst_scr.at[i])
            dxs.append(d["dx"])
            dbs.append(d["dbm"])
            dcs.append(d["dcm"])
            dp_ref[:, i * GRP_W:(i + 1) * GRP_W] = d["dz"].astype(BF16)
            ds_scr[i] = d["dst_prev"]
            ddt_ref[:, i * 128:(i + 1) * 128] = d["ddt_raw"]
            dgn_ref[0:1, i * GRP_W:(i + 1) * GRP_W] += d["dgn"]
            ddtb_ref[0:1, :] += d["ddtb"]
            dal_ref[0:1, :] += d["dalog"]
            ddsk_ref[0:1, :] += d["ddskip"]
        dconv = jnp.where(valid, jnp.concatenate(dxs + dbs + dcs, axis=1) * _dsilu(conv), 0.0)
        dext = jnp.concatenate([dconv, carry[...]], axis=0)
        dp_ref[:, SSM_INNER:] = _conv_taps([_rows_from(dext, 3 - k) for k in range(4)], w_ref[...]).astype(BF16)
        carry[...] = dconv[0:8]
        dcw_ref[...] += jnp.concatenate(
            [jnp.sum(dconv * taps[k], axis=0, keepdims=True) for k in range(4)]
            + [jnp.zeros((4, CONV_DIM), F32)], axis=0)
        dcb_ref[0:1, :] += jnp.sum(dconv, axis=0, keepdims=True)
        if ne:
            pl.when(c == NB - 1)(ex_finish)

    rc = lambda c: NB - 1 - c
    small = pl.BlockSpec((8, 128), lambda c: (0, 0))
    wide = lambda w: pl.BlockSpec((8, w), lambda c: (0, 0))
    return pl.pallas_call(
        body, grid=(NB,),
        in_specs=_ssd_in_specs(True) + [
            pl.BlockSpec((SSM_GROUPS, 1, SSM_STATE, GRP_W), lambda c: (0, rc(c), 0, 0)),
            pl.BlockSpec((BLK, SSM_INNER), lambda c: (rc(c), 0)), ANY] + [ANY] * ne,
        out_specs=[pl.BlockSpec((BLK, SSM_GROUPS * 128), lambda c: (rc(c), 0)),
                   pl.BlockSpec((BLK, ZX_W), lambda c: (rc(c), 0)),
                   small, small, small, wide(SSM_INNER), wide(CONV_DIM), wide(CONV_DIM)] + [ANY] * ne,
        out_shape=[SDS((T, GRP_W), F32), SDS((T, PW), BF16), SDS((8, 128), F32), SDS((8, 128), F32),
                   SDS((8, 128), F32), SDS((8, SSM_INNER), F32), SDS((8, CONV_DIM), F32), SDS((8, CONV_DIM), F32)]
        + [SDS(p.shape, p.dtype) for p in exchange],
        scratch_shapes=[pltpu.VMEM((SSM_GROUPS, SSM_STATE, GRP_W), F32), pltpu.VMEM((SSM_GROUPS, BLK, BLK), F32),
                        pltpu.VMEM((8, CONV_DIM), F32)] + (_direct_scratch(exchange) if ne else []),
        input_output_aliases={11: 1},
        compiler_params=_cparams(),
        name="ssd_bwd")(proj, proj, proj, conv_w, conv_b, dt_bias, a_log, d_skip, g_norm, states, dy, dproj,
                        *exchange)


POST_R = 272


def _post_a(o, proj, sn, w_att, w_ssm, w_o):
    def body(o_ref, za_ref, ga_ref, gs_ref, sn_ref, wa_ref, ws_ref, wo_ref, a_ref, mg_ref, ya_ref, ys_ref, out_ref):
        a = (o_ref[...] * _silu(za_ref[...])).astype(BF16)
        a_ref[...] = a
        ya = jnp.dot(a, wa_ref[...], preferred_element_type=F32)
        ys = jnp.dot(sn_ref[...], ws_ref[...], preferred_element_type=F32)
        ya_ref[...] = ya.astype(BF16)
        ys_ref[...] = ys.astype(BF16)
        mg = (jax.nn.sigmoid(ga_ref[...]) * ya + jax.nn.sigmoid(gs_ref[...]) * ys).astype(BF16)
        mg_ref[...] = mg
        out_ref[...] = jnp.dot(mg, wo_ref[...], preferred_element_type=F32)

    row = pl.BlockSpec((POST_R, D_MODEL), lambda i: (i, 0))
    pcol = lambda c0: pl.BlockSpec((POST_R, D_MODEL), lambda i: (i, c0 // D_MODEL))
    full = lambda r: pl.BlockSpec((r, D_MODEL), lambda i: (0, 0))
    return pl.pallas_call(
        body, grid=(T // POST_R,),
        in_specs=[row, pcol(C_ZA), pcol(C_GA), pcol(C_GS), pl.BlockSpec((POST_R, SSM_INNER), lambda i: (i, 0)),
                  full(D_MODEL), full(SSM_INNER), full(D_MODEL)],
        out_specs=[row, row, row, row, row],
        out_shape=[SDS((T, D_MODEL), BF16), SDS((T, D_MODEL), BF16), SDS((T, D_MODEL), BF16), SDS((T, D_MODEL), BF16),
                   SDS((T, D_MODEL), F32)],
        compiler_params=_cparams(), name="post_a")(o, proj, proj, proj, sn, w_att, w_ssm, w_o)


def _post_b(out, h, tgt, proj, ya, ys, o, g_post, w_att, w_ssm, w_o):
    def body(out_ref, h_ref, t_ref, za_ref, ga_ref, gs_ref, ya_ref, ys_ref, o_ref, gp_ref, wa_ref, ws_ref, wo_ref,
             loss_ref, dres_ref, dout_ref, dya_ref, dys_ref, do_ref, dp_ref, dsn_ref, dgp_ref):
        i = pl.program_id(0)
        x = out_ref[...]
        gp = gp_ref[...]
        r = lax.rsqrt(jnp.mean(x * x, axis=-1, keepdims=True) + EPS)
        row = i * POST_R + lax.broadcasted_iota(jnp.int32, (POST_R, 1), 0)
        res = h_ref[...] + jnp.where(row >= PAD, x * r * gp, 0.0)
        live = row >= PAD + N_META
        err = jnp.where(live, res - t_ref[...], 0.0)
        lpart = 0.5 * jnp.sum(jnp.sum(err * err, axis=1, keepdims=True) / D_MODEL, axis=0, keepdims=True)
        dres = err / D_MODEL
        dres_ref[...] = dres
        gpart = jnp.sum(dres * x * r, axis=0, keepdims=True)

        @pl.when(i == 0)
        def _():
            loss_ref[...] = jnp.zeros_like(loss_ref)
            dgp_ref[...] = jnp.zeros_like(dgp_ref)

        loss_ref[...] += jnp.broadcast_to(lpart, loss_ref.shape)
        dgp_ref[0:1, :] += gpart
        gd = gp * dres
        dout = (r * gd - x * (r * r * r) * jnp.mean(x * gd, axis=-1, keepdims=True)).astype(BF16)
        dout_ref[...] = dout
        dmg = lax.dot_general(dout, wo_ref[...], (((1,), (1,)), ((), ())), preferred_element_type=F32)
        sga = jax.nn.sigmoid(ga_ref[...])
        sgs = jax.nn.sigmoid(gs_ref[...])
        dya = (dmg * sga).astype(BF16)
        dys = (dmg * sgs).astype(BF16)
        dya_ref[...] = dya
        dys_ref[...] = dys
        dp_ref[:, C_GA - C_ZA:C_GA - C_ZA + D_MODEL] = (dmg * ya_ref[...].astype(F32) * sga * (1.0 - sga)).astype(BF16)
        dp_ref[:, C_GS - C_ZA:C_GS - C_ZA + D_MODEL] = (dmg * ys_ref[...].astype(F32) * sgs * (1.0 - sgs)).astype(BF16)
        da = lax.dot_general(dya, wa_ref[...], (((1,), (1,)), ((), ())), preferred_element_type=F32)
        za = za_ref[...]
        do_ref[...] = (da * _silu(za)).astype(BF16)
        dp_ref[:, 0:D_MODEL] = (da * o_ref[...] * _dsilu(za)).astype(BF16)
        dsn_ref[...] = lax.dot_general(dys, ws_ref[...], (((1,), (1,)), ((), ())),
                                       preferred_element_type=F32).astype(BF16)

    row = pl.BlockSpec((POST_R, D_MODEL), lambda i: (i, 0))
    pcol = lambda c0: pl.BlockSpec((POST_R, D_MODEL), lambda i: (i, c0 // D_MODEL))
    full = lambda r: pl.BlockSpec((r, D_MODEL), lambda i: (0, 0))
    small = pl.BlockSpec((8, D_MODEL), lambda i: (0, 0))
    return pl.pallas_call(
        body, grid=(T // POST_R,),
        in_specs=[row, row, row, pcol(C_ZA), pcol(C_GA), pcol(C_GS), row, row, row,
                  pl.BlockSpec((1, D_MODEL), lambda i: (0, 0)), full(D_MODEL), full(SSM_INNER), full(D_MODEL)],
        out_specs=[pl.BlockSpec((8, 128), lambda i: (0, 0)), row, row, row, row, row,
                   pl.BlockSpec((POST_R, GATES_W), lambda i: (i, C_ZA // GATES_W)),
                   pl.BlockSpec((POST_R, SSM_INNER), lambda i: (i, 0)), small],
        out_shape=[SDS((8, 128), F32), SDS((T, D_MODEL), F32), SDS((T, D_MODEL), BF16), SDS((T, D_MODEL), BF16),
                   SDS((T, D_MODEL), BF16), SDS((T, D_MODEL), BF16), SDS((T, PW), BF16),
                   SDS((T, SSM_INNER), BF16), SDS((8, D_MODEL), F32)],
        compiler_params=_cparams(), name="post_b")(out, h, tgt, proj, proj, proj, ya, ys, o, g_post, w_att, w_ssm, w_o)


TAIL_W = PW - C_K


def _dproj_tail(dproj, dk, dv, ddt4):
    rows = T // 4

    def body(_, dk_ref, dv_ref, ddt_ref, o_ref, buf, sem):
        n = pl.program_id(0)
        d4 = ddt_ref[...]
        buf[:, 0:KV_W] = dk_ref[...].astype(BF16)
        buf[:, KV_W:2 * KV_W] = dv_ref[...].astype(BF16)
        buf[:, 2 * KV_W:TAIL_W] = (d4[:, 0:128] + d4[:, 128:256] + d4[:, 256:384] + d4[:, 384:512]).astype(BF16)
        cp = pltpu.make_async_copy(buf, o_ref.at[pl.ds(pl.multiple_of(n * rows, 16), rows), pl.ds(C_K, TAIL_W)], sem)
        cp.start()
        cp.wait()

    spec = lambda w: pl.BlockSpec((rows, w), lambda i: (i, 0))
    return pl.pallas_call(
        body, grid=(T // rows,), in_specs=[ANY, spec(KV_W), spec(KV_W), spec(GRP_W)], out_specs=ANY,
        out_shape=SDS((T, PW), BF16), input_output_aliases={0: 0},
        scratch_shapes=[pltpu.VMEM((rows, TAIL_W), BF16), pltpu.SemaphoreType.DMA],
        name="dproj_tail")(dproj, dk, dv, ddt4)


def _adamw_math(w, g, m, v):
    m = ADAM_B1 * m + (1.0 - ADAM_B1) * g
    v = ADAM_B2 * v + (1.0 - ADAM_B2) * (g * g)
    m_hat = m / (1.0 - ADAM_B1 ** ADAM_STEP)
    v_hat = v / (1.0 - ADAM_B2 ** ADAM_STEP)
    delta = -ADAM_LR * (m_hat / (jnp.sqrt(v_hat) + ADAM_EPS) + ADAM_WD * w)
    return delta, m, v


def _sum_adamw(recv, w, m, v, tc, name):
    rows, cols = w.shape
    nslab = recv.shape[0]
    assert cols % tc == 0

    def body(r_ref, w_ref, m_ref, v_ref, g_ref, d_ref, nm_ref, nv_ref):
        g = r_ref[0].astype(F32)
        for d in range(1, nslab):
            g = g + r_ref[d].astype(F32)
        g_ref[...] = g
        delta, nm, nv = _adamw_math(w_ref[...], g, m_ref[...], v_ref[...])
        d_ref[...] = delta
        nm_ref[...] = nm
        nv_ref[...] = nv

    blk = pl.BlockSpec((rows, tc), lambda i: (0, i))
    return pl.pallas_call(
        body, grid=(cols // tc,),
        in_specs=[pl.BlockSpec((nslab, rows, tc), lambda i: (0, 0, i)), blk, blk, blk],
        out_specs=[blk, blk, blk, blk], out_shape=[SDS((rows, cols), F32)] * 4,
        compiler_params=_cparams(), name=name)(recv, w, m, v)


def _sum_adamw_rows3(recv, w3, m3, v3, name, exchange=()):
    pairs = 61
    assert (SHARD_IN // 2) % pairs == 0
    nsteps = SHARD_IN // 2 // pairs
    ne = len(exchange)

    def body(*refs):
        r_ref, w_ref, m_ref, v_ref = refs[:4]
        g_ref, d_ref, nm_ref, nv_ref = refs[4 + ne:8 + ne]
        if ne:
            ex_start, ex_finish = _direct_program(refs[4:4 + ne], refs[8 + ne:8 + 2 * ne], refs[8 + 2 * ne:])
            pl.when(pl.program_id(0) == 0)(ex_start)
        g = r_ref[0].astype(F32)
        for d in range(1, N_CHIP):
            g = g + r_ref[d].astype(F32)
        g = g.reshape(2 * pairs, ROW_TILES, 128)
        g_ref[...] = g
        delta, nm, nv = _adamw_math(w_ref[...], g, m_ref[...], v_ref[...])
        d_ref[...] = delta
        nm_ref[...] = nm
        nv_ref[...] = nv
        if ne:
            pl.when(pl.program_id(0) == nsteps - 1)(ex_finish)

    blk = pl.BlockSpec((2 * pairs, ROW_TILES, 128), lambda i: (i, 0, 0))
    return pl.pallas_call(
        body, grid=(nsteps,),
        in_specs=[pl.BlockSpec((N_CHIP, pairs, 2 * ROW_TILES, 128), lambda i: (0, i, 0, 0)), blk, blk, blk]
        + [ANY] * ne,
        out_specs=[blk, blk, blk, blk] + [ANY] * ne,
        out_shape=[SDS(w3.shape, F32)] * 4 + [SDS(p.shape, p.dtype) for p in exchange],
        scratch_shapes=_direct_scratch(exchange) if ne else [],
        compiler_params=_cparams(), name=name)(recv, w3, m3, v3, *exchange)


ROW_GPRE, ROW_CONVB, ROW_DTB, ROW_ALOG, ROW_DSKIP, ROW_SINK, ROW_GSSM, ROW_GPOST = 0, 1, 4, 5, 6, 7, 8, 10
ROW_LOSS = 11
REP_ROWS, ROW_CONVW, ROW_META, SM_ROWS = 16, 16, 24, 40
CW_SHARD = CONV_DIM // N_DEV
META_SHARD = D_MODEL // N_DEV


def _small_pack(dgpre, db, ddtb, dal, ddsk, dsink, dgn, dgp, dw, loss, dh):
    def body(dgpre_ref, db_ref, ddtb_ref, dal_ref, ddsk_ref, dsink_ref, dgn_ref, dgp_ref, dw_ref, loss_ref, dh_ref,
             o_ref, rep):
        rep[...] = jnp.zeros_like(rep)
        rep[ROW_LOSS:ROW_LOSS + 1, 0:128] = loss_ref[0:1, :]
        rep[ROW_GPRE:ROW_GPRE + 1, :] = dgpre_ref[0:1, :]
        for k in range(3):
            rep[ROW_CONVB + k:ROW_CONVB + k + 1, :] = db_ref[0:1, 1024 * k:1024 * (k + 1)]
        rep[ROW_DTB:ROW_DTB + 1, 0:128] = ddtb_ref[0:1, :]
        rep[ROW_ALOG:ROW_ALOG + 1, 0:128] = dal_ref[0:1, :]
        rep[ROW_DSKIP:ROW_DSKIP + 1, 0:128] = ddsk_ref[0:1, :]
        rep[ROW_SINK:ROW_SINK + 1, 0:128] = dsink_ref[0:1, :]
        rep[ROW_GSSM:ROW_GSSM + 1, :] = dgn_ref[0:1, 0:1024]
        rep[ROW_GSSM + 1:ROW_GSSM + 2, :] = dgn_ref[0:1, 1024:2048]
        rep[ROW_GPOST:ROW_GPOST + 1, :] = dgp_ref[0:1, :]
        cw = dw_ref[...]
        mh = dh_ref[...]
        o_ref[...] = jnp.zeros_like(o_ref)
        for p in range(N_DEV):
            o_ref[p, 0:REP_ROWS, :] = rep[...]
            o_ref[p, ROW_CONVW:ROW_CONVW + 8, 0:CW_SHARD] = cw[:, p * CW_SHARD:(p + 1) * CW_SHARD]
            o_ref[p, ROW_META:ROW_META + N_META, 0:META_SHARD] = mh[:, p * META_SHARD:(p + 1) * META_SHARD]

    ins = [dgpre, db, ddtb, dal, ddsk, dsink, dgn, dgp, dw, loss]
    return pl.pallas_call(
        body, grid=(1,),
        in_specs=[pl.BlockSpec(a.shape, lambda i: (0, 0)) for a in ins]
        + [pl.BlockSpec((N_META, D_MODEL), lambda i: (PAD // N_META, 0))],
        out_specs=pl.BlockSpec((N_DEV, SM_ROWS, 1024), lambda i: (0, 0, 0)),
        out_shape=SDS((N_DEV, SM_ROWS, 1024), F32), scratch_shapes=[pltpu.VMEM((REP_ROWS, 1024), F32)],
        name="small_pack")(*ins, dh)


def _small_finish(recv, params):
    npar = len(params)

    def body(*refs):
        r_ref = refs[0]
        wmv = refs[1:1 + 3 * npar]
        outs = refs[1 + 3 * npar:1 + 7 * npar]
        loss_ref = refs[1 + 7 * npar]
        gs = refs[-1]
        g = r_ref[0]
        for d in range(1, recv.shape[0]):
            g = g + r_ref[d]
        gs[...] = g
        loss_ref[...] = gs[ROW_LOSS:ROW_LOSS + 1, 0:128]
        grads = [
            gs[ROW_GPRE:ROW_GPRE + 1, :],
            jnp.concatenate([gs[ROW_CONVB + k:ROW_CONVB + k + 1, :] for k in range(3)], axis=1),
            gs[ROW_DTB:ROW_DTB + 1, 0:SSM_HEADS], gs[ROW_ALOG:ROW_ALOG + 1, 0:SSM_HEADS],
            gs[ROW_DSKIP:ROW_DSKIP + 1, 0:SSM_HEADS], gs[ROW_SINK:ROW_SINK + 1, 0:Q_HEADS],
            jnp.concatenate([gs[ROW_GSSM:ROW_GSSM + 1, :], gs[ROW_GSSM + 1:ROW_GSSM + 2, :]], axis=1),
            gs[ROW_GPOST:ROW_GPOST + 1, :],
            gs[ROW_CONVW:ROW_CONVW + 4, 0:CW_SHARD],
            gs[ROW_META:ROW_META + N_META, 0:META_SHARD]]
        for i in range(npar):
            w_ref, m_ref, v_ref = wmv[3 * i:3 * i + 3]
            delta, nm, nv = _adamw_math(w_ref[...], grads[i], m_ref[...], v_ref[...])
            outs[4 * i][...] = grads[i]
            outs[4 * i + 1][...] = delta
            outs[4 * i + 2][...] = nm
            outs[4 * i + 3][...] = nv

    flat = [a for wmv in params for a in wmv]
    res = pl.pallas_call(
        body, out_shape=[SDS(wmv[0].shape, F32) for wmv in params for _ in range(4)] + [SDS((1, 128), F32)],
        scratch_shapes=[pltpu.VMEM((SM_ROWS, 1024), F32)], name="small_finish")(recv, *flat)
    return [tuple(res[4 * i:4 * i + 4]) for i in range(npar)], res[4 * npar]


def _slab(ref, px, py, pc):
    return ref.at[4 * px + 2 * py + pc]


def _bounce(src, dst, buf, sem):
    cp = pltpu.make_async_copy(src, buf, sem)
    cp.start()
    cp.wait()
    cp = pltpu.make_async_copy(buf, dst, sem)
    cp.start()
    cp.wait()


def _ag_program(ins, outs, scratch):
    na = len(ins)
    send_sems, recv_sems, local_sems = scratch[:3]
    bufs = scratch[3:]
    x, y, c = lax.axis_index("x"), lax.axis_index("y"), lax.axis_index("c")
    me, sibling = (x, y, c), (x, y, 1 - c)
    chips = [(1 - x, y), (x, 1 - y), (1 - x, 1 - y)]

    def copy(a, k, block, to, src=None):
        dst = _slab(outs[a], *block)
        return pltpu.make_async_remote_copy(
            src_ref=dst if src is None else src, dst_ref=dst, send_sem=send_sems.at[a, k],
            recv_sem=recv_sems.at[a, k], device_id=to, device_id_type=MESH)

    def own_sends():
        out = []
        for a in range(na):
            out.append(copy(a, 0, me, sibling, src=ins[a]))
            out += [copy(a, 1 + j, me, (*chip, c), src=ins[a]) for j, chip in enumerate(chips)]
        return out

    def start():
        for cp in own_sends():
            cp.start()
        for a in range(na):
            _bounce(ins[a], _slab(outs[a], *me), bufs[a], local_sems.at[a])

    def forward():
        for j, chip in enumerate(chips):
            for a in range(na):
                copy(a, 1 + j, (*chip, c), me).wait_recv()
                copy(a, 4 + j, (*chip, c), sibling).start()

    def finish():
        for a in range(na):
            copy(a, 0, sibling, me).wait_recv()
            for j, chip in enumerate(chips):
                copy(a, 4 + j, (*chip, 1 - c), me).wait_recv()
        for cp in own_sends():
            cp.wait_send()
        for j, chip in enumerate(chips):
            for a in range(na):
                copy(a, 4 + j, (*chip, c), sibling).wait_send()

    return start, forward, finish


def _ag_scratch(shards):
    na = len(shards)
    return [pltpu.SemaphoreType.DMA((na, 7)), pltpu.SemaphoreType.DMA((na, 7)),
            pltpu.SemaphoreType.DMA((na,))] + [pltpu.VMEM(s.shape, s.dtype) for s in shards]


def _all_gather(shards):
    na = len(shards)

    def body(*refs):
        start, forward, finish = _ag_program(refs[:na], refs[na:2 * na], refs[2 * na:])
        start()
        forward()
        finish()

    return pl.pallas_call(
        body, in_specs=[ANY] * na, out_specs=[ANY] * na,
        out_shape=[SDS((N_DEV,) + s.shape, s.dtype) for s in shards],
        scratch_shapes=_ag_scratch(shards), name="all_gather")(*shards)


N_CHIP = 4


def _pair_sum(own, got, name):
    na = len(own)

    def body(*refs):
        for a in range(na):
            o_ref, g_ref, s_ref = refs[a], refs[na + a], refs[2 * na + a]
            s_ref[...] = (o_ref[...].astype(F32) + g_ref[...].astype(F32)).astype(s_ref.dtype)

    def spec(p):
        nd = len(p.shape) - 1
        return pl.BlockSpec((1,) + p.shape[1:], lambda k, nd=nd: (k,) + (0,) * nd)

    return pl.pallas_call(
        body, grid=(N_CHIP,), in_specs=[spec(p) for p in own] + [spec(p) for p in got],
        out_specs=[spec(p) for p in own], out_shape=[SDS(p.shape, p.dtype) for p in own],
        compiler_params=_cparams(), name=name)(*own, *got)


def _chips_program(ins, outs, scratch):
    na = len(ins)
    send_sems, recv_sems, local_sems = scratch[:3]
    bufs = scratch[3:]
    x, y, c = lax.axis_index("x"), lax.axis_index("y"), lax.axis_index("c")
    mine = 2 * x + y
    chips = [(1 - x, y), (x, 1 - y), (1 - x, 1 - y)]

    def send(a, j):
        px, py = chips[j]
        return pltpu.make_async_remote_copy(
            src_ref=ins[a].at[2 * px + py], dst_ref=outs[a].at[mine], send_sem=send_sems.at[a, j],
            recv_sem=recv_sems.at[a, j], device_id=(px, py, c), device_id_type=MESH)

    def arrival(a, j):
        px, py = chips[j]
        return pltpu.make_async_remote_copy(
            src_ref=ins[a].at[2 * px + py], dst_ref=outs[a].at[2 * px + py], send_sem=send_sems.at[a, j],
            recv_sem=recv_sems.at[a, j], device_id=(px, py, c), device_id_type=MESH)

    def start():
        for a in range(na):
            for j in range(3):
                send(a, j).start()
        for a in range(na):
            _bounce(ins[a].at[mine], outs[a].at[mine], bufs[a], local_sems.at[a])

    def finish():
        for a in range(na):
            for j in range(3):
                arrival(a, j).wait_recv()
        for a in range(na):
            for j in range(3):
                send(a, j).wait_send()

    return start, finish


def _chips_scratch(parts):
    na = len(parts)
    return [pltpu.SemaphoreType.DMA((na, 3)), pltpu.SemaphoreType.DMA((na, 3)),
            pltpu.SemaphoreType.DMA((na,))] + [pltpu.VMEM(p.shape[1:], p.dtype) for p in parts]


def _direct_program(ins, outs, scratch):
    na = len(ins)
    send_sems, recv_sems, local_sems = scratch[:3]
    bufs = scratch[3:]
    x, y, c = lax.axis_index("x"), lax.axis_index("y"), lax.axis_index("c")
    me = (x, y, c)
    peers = []
    for k in range(1, N_DEV):
        dx, dy, dc = (k >> 2) & 1, (k >> 1) & 1, k & 1
        peers.append(((1 - x) if dx else x, (1 - y) if dy else y, (1 - c) if dc else c))

    def send(a, k):
        return pltpu.make_async_remote_copy(
            src_ref=_slab(ins[a], *peers[k]), dst_ref=_slab(outs[a], *me), send_sem=send_sems.at[a, k],
            recv_sem=recv_sems.at[a, k], device_id=peers[k], device_id_type=MESH)

    def arrival(a, k):
        return pltpu.make_async_remote_copy(
            src_ref=_slab(ins[a], *peers[k]), dst_ref=_slab(outs[a], *peers[k]), send_sem=send_sems.at[a, k],
            recv_sem=recv_sems.at[a, k], device_id=peers[k], device_id_type=MESH)

    def start():
        for a in range(na):
            for k in range(N_DEV - 1):
                send(a, k).start()
        for a in range(na):
            _bounce(_slab(ins[a], *me), _slab(outs[a], *me), bufs[a], local_sems.at[a])

    def finish():
        for a in range(na):
            for k in range(N_DEV - 1):
                arrival(a, k).wait_recv()
        for a in range(na):
            for k in range(N_DEV - 1):
                send(a, k).wait_send()

    return start, finish


def _direct_scratch(parts):
    na = len(parts)
    return [pltpu.SemaphoreType.DMA((na, N_DEV - 1)), pltpu.SemaphoreType.DMA((na, N_DEV - 1)),
            pltpu.SemaphoreType.DMA((na,))] + [pltpu.VMEM(p.shape[1:], p.dtype) for p in parts]


ROW_TILES = D_MODEL // 128


def _rows3(t):
    return jnp.transpose(t[0]).reshape(t.shape[2], ROW_TILES, 128)


def _unrows3(t):
    return jnp.transpose(t.reshape(t.shape[0], D_MODEL))[None]


def _cast_shards(w_in3, w_att, w_ssm, w_o):
    def body(wi_ref, wa_ref, ws_ref, wo_ref, a_ref, b_ref, c_ref, d_ref):
        a_ref[...] = wi_ref[...].reshape(SHARD_IN // 2, 2 * ROW_TILES, 128).astype(BF16)
        b_ref[...] = wa_ref[...].astype(BF16)
        c_ref[...] = ws_ref[...].astype(BF16)
        d_ref[...] = wo_ref[...].astype(BF16)

    return pl.pallas_call(
        body, out_shape=[SDS((SHARD_IN // 2, 2 * ROW_TILES, 128), BF16), SDS(w_att.shape, BF16),
                         SDS(w_ssm.shape, BF16), SDS(w_o.shape, BF16)],
        compiler_params=_cparams(), name="cast_shards")(w_in3, w_att, w_ssm, w_o)


def _pieces():
    out = []
    for r0, c0, w in _SEGS:
        r = r0
        while r < r0 + w:
            d = r // SHARD_IN
            n = min(r0 + w, (d + 1) * SHARD_IN) - r
            out.append((c0 + (r - r0), d, r - d * SHARD_IN, n))
            r += n
    return out


def _to_aligned_t(slabs):
    def body(a_ref, o_ref):
        for (t, d, s, n) in _pieces():
            o_ref[t:t + n, :] = a_ref[d, s // 2:(s + n) // 2].reshape(n, D_MODEL)
        o_ref[C_DT + 32:C_DT + 128, :] = jnp.zeros((96, D_MODEL), slabs.dtype)

    return pl.pallas_call(body, out_shape=SDS((PW, D_MODEL), slabs.dtype), compiler_params=_cparams(),
                          name="to_aligned")(slabs)


def _from_aligned_pair(g):
    slab = (SHARD_IN // 2, 2 * ROW_TILES, 128)
    by_slab = [[p for p in _pieces() if p[1] == d] for d in range(N_DEV)]

    def body(g_ref, own_ref, got_ref, slabs, send_sems, recv_sems, local_sems):
        x, y, c = lax.axis_index("x"), lax.axis_index("y"), lax.axis_index("c")
        sibling = (x, y, 1 - c)

        def to_own(d, k):
            return pltpu.make_async_copy(slabs.at[d], own_ref.at[k], local_sems.at[k])

        def to_sibling(d, k):
            return pltpu.make_async_remote_copy(
                src_ref=slabs.at[d], dst_ref=got_ref.at[k], send_sem=send_sems.at[k], recv_sem=recv_sems.at[k],
                device_id=sibling, device_id_type=MESH)

        for d in range(N_DEV):
            for (t, _, s, n) in by_slab[d]:
                slabs[d, s // 2:(s + n) // 2] = g_ref[t:t + n, :].reshape(n // 2, 2 * ROW_TILES, 128)
            k, side = d // 2, d % 2
            pl.when(c == side)(to_own(d, k).start)
            pl.when(c != side)(to_sibling(d, k).start)
        for k in range(N_CHIP):
            to_own(0, k).wait()
            to_sibling(0, k).wait()

    half = SDS((N_CHIP,) + slab, g.dtype)
    return pl.pallas_call(
        body, in_specs=[pl.BlockSpec(memory_space=pltpu.VMEM)], out_specs=[ANY, ANY], out_shape=[half, half],
        scratch_shapes=[pltpu.VMEM((N_DEV,) + slab, g.dtype), pltpu.SemaphoreType.DMA((N_CHIP,)),
                        pltpu.SemaphoreType.DMA((N_CHIP,)), pltpu.SemaphoreType.DMA((N_CHIP,))],
        compiler_params=_cparams(), name="from_aligned_pair")(g)


_SEGS = [
    (R_Q, C_Q, 1024), (R_K, C_K, 256), (R_V, C_V, 256), (R_ZA, C_ZA, 1024), (R_ZS, C_ZS, 2048),
    (R_XBC, C_XBC, 3072), (R_DT, C_DT, 32), (R_GA, C_GA, 1024), (R_GS, C_GS, 1024)]


def _pad_lanes(v, n=128):
    return jnp.pad(v, ((0, 0), (0, n - v.shape[1])))


def _device_step(h, tgt, w_alt, w_out, g_pre, conv_w8, conv_b, dt_bias, a_log, d_skip, sinks, g_ssm, g_post, on_mesh):
    dtb, al, dsk, snk = _pad_lanes(dt_bias), _pad_lanes(a_log), _pad_lanes(d_skip), _pad_lanes(sinks)
    u = _norm_u(h, g_pre)
    proj = _matmul(u, w_alt, "nt", F32, T, PROJ_TILE, "in_proj")
    o = _attn_fwd(proj, snk)
    if on_mesh:
        sn, states, att_all, ssm_all, o_all = _ssd_fwd(proj, conv_w8, conv_b, dtb, al, dsk, g_ssm, gather=w_out)
        w_att = att_all.reshape(D_MODEL, D_MODEL)
        w_ssm = ssm_all.reshape(SSM_INNER, D_MODEL)
        w_o = o_all.reshape(D_MODEL, D_MODEL)
    else:
        sn, states = _ssd_fwd(proj, conv_w8, conv_b, dtb, al, dsk, g_ssm)
        w_att, w_ssm, w_o = w_out
    a_in, mg, ya, ys, out = _post_a(o, proj, sn, w_att, w_ssm, w_o)
    (loss, dres, dout, dya, dys, do, dproj, dsn, dgp) = _post_b(
        out, h, tgt, proj, ya, ys, o, g_post, w_att, w_ssm, w_o)
    dw_att = _matmul(a_in, dya, "tn", BF16, D_MODEL, D_MODEL, "d_w_att")
    dw_ssm = _matmul(sn, dys, "tn", BF16, D_MODEL, D_MODEL, "d_w_ssm")
    dw_o = _matmul(mg, dout, "tn", BF16, D_MODEL, D_MODEL, "d_w_o")
    res = {}
    if on_mesh:
        parts = [dw_att.reshape(N_DEV, 128, D_MODEL), dw_ssm.reshape(N_DEV, 256, D_MODEL),
                 dw_o.reshape(N_DEV, 128, D_MODEL)]
        (ddt4, dproj, ddtb, dal, ddsk, dgn, dcw, dcb, res["r_att"], res["r_ssm"], res["r_o"]) = _ssd_bwd(
            proj, conv_w8, conv_b, dtb, al, dsk, g_ssm, states, dsn, dproj, exchange=parts)
    else:
        ddt4, dproj, ddtb, dal, ddsk, dgn, dcw, dcb = _ssd_bwd(proj, conv_w8, conv_b, dtb, al, dsk, g_ssm, states,
                                                               dsn, dproj)
        res.update(dw_att=dw_att, dw_ssm=dw_ssm, dw_o=dw_o)
    dproj, dk, dv, dsink = _attn_bwd(proj, snk, do, dproj)
    dproj = _dproj_tail(dproj, dk, dv, ddt4)
    dw_alt = _matmul(dproj, u, "tn", BF16, PROJ_TILE, D_MODEL, "d_w_in")
    if on_mesh:
        own, got = _from_aligned_pair(dw_alt)
        dh, dgpre, res["r_in"] = _d_u_norm(dproj, w_alt, h, g_pre, dres,
                                           chips=_pair_sum([own], [got], "pair_sum_w_in"))
    else:
        dh, dgpre = _d_u_norm(dproj, w_alt, h, g_pre, dres)
        res["dw_alt"] = dw_alt
    small = (dgpre, dcb, ddtb, dal, ddsk, dsink, dgn, dgp, dcw)
    if on_mesh:
        res["small_pack"] = _small_pack(*small, loss, dh)
    else:
        res["small"] = small
    res.update(loss=loss[0, 0], dh=dh)
    return res


def kernel(x, meta_tokens, g_pre, w_in, conv_w, conv_b, dt_bias, a_log, d_skip, attn_sinks, g_ssm_norm, w_out_att, w_out_ssm, w_out, g_post, loss_target, m_meta_tokens, m_g_pre, m_w_in, m_conv_w, m_conv_b, m_dt_bias, m_a_log, m_d_skip, m_attn_sinks, m_g_ssm_norm, m_w_out_att, m_w_out_ssm, m_w_out, m_g_post, v_meta_tokens, v_g_pre, v_w_in, v_conv_w, v_conv_b, v_dt_bias, v_a_log, v_d_skip, v_attn_sinks, v_g_ssm_norm, v_w_out_att, v_w_out_ssm, v_w_out, v_g_post):
    w_in3, m_in3, v_in3 = _rows3(w_in), _rows3(m_w_in), _rows3(v_w_in)
    a_sh, att_sh, ssm_sh, o_sh = _cast_shards(w_in3, w_out_att[0], w_out_ssm[0], w_out[0])
    cw_sh = jnp.pad(conv_w[0], ((0, 4), (0, 0)))
    a_all, meta_all, cw_all = _all_gather([a_sh, meta_tokens, cw_sh])
    w_alt = _to_aligned_t(a_all)
    meta_full = meta_all.transpose(1, 0, 2).reshape(N_META, D_MODEL)
    conv_w8 = cw_all.transpose(1, 0, 2).reshape(8, CONV_DIM)

    h = jnp.concatenate([jnp.zeros((PAD, D_MODEL), F32), meta_full, x[0]], axis=0)
    tgt = jnp.concatenate([jnp.zeros((PAD + N_META, D_MODEL), F32), loss_target[0]], axis=0)
    r = _device_step(h, tgt, w_alt, (att_sh, ssm_sh, o_sh), g_pre, conv_w8, conv_b, dt_bias, a_log, d_skip,
                     attn_sinks, g_ssm_norm, g_post, True)
    grad_x = r["dh"][PAD + N_META:][None]

    *res_in, r_small = _sum_adamw_rows3(r["r_in"], w_in3, m_in3, v_in3, "adamw_w_in", exchange=[r["small_pack"]])
    res_in = [_unrows3(t) for t in res_in]
    res_att = [t[None] for t in _sum_adamw(r["r_att"], w_out_att[0], m_w_out_att[0], v_w_out_att[0], 512,
                                           "adamw_w_att")]
    res_ssm = [t[None] for t in _sum_adamw(r["r_ssm"], w_out_ssm[0], m_w_out_ssm[0], v_w_out_ssm[0], 512,
                                           "adamw_w_ssm")]
    res_o = [t[None] for t in _sum_adamw(r["r_o"], w_out[0], m_w_out[0], v_w_out[0], 512, "adamw_w_o")]
    (res_gpre, res_convb, res_dtb, res_alog, res_dskip, res_sink, res_gssm, res_gpost, res_cw, res_meta), loss = _small_finish(
        r_small, [(g_pre, m_g_pre, v_g_pre), (conv_b, m_conv_b, v_conv_b), (dt_bias, m_dt_bias, v_dt_bias),
                       (a_log, m_a_log, v_a_log), (d_skip, m_d_skip, v_d_skip),
                       (attn_sinks, m_attn_sinks, v_attn_sinks), (g_ssm_norm, m_g_ssm_norm, v_g_ssm_norm),
                       (g_post, m_g_post, v_g_post), (conv_w[0], m_conv_w[0], v_conv_w[0]),
                       (meta_tokens, m_meta_tokens, v_meta_tokens)])
    res_cw = [t[None] for t in res_cw]
    per_weight = [res_meta, res_gpre, res_in, res_cw, res_convb, res_dtb, res_alog, res_dskip, res_sink, res_gssm,
                  res_att, res_ssm, res_o, res_gpost]
    return (loss[0, 0], grad_x, *[p[0] for p in per_weight], *[p[1] for p in per_weight], *[p[2] for p in per_weight],
            *[p[3] for p in per_weight])
```

```python
import jax
import jax.numpy as jnp
import numpy as np
from jax import lax
from jax.experimental import pallas as pl
from jax.experimental.pallas import tpu as pltpu

F32 = jnp.float32
BF16 = jnp.bfloat16
SDS = jax.ShapeDtypeStruct
MESH = pl.DeviceIdType.MESH
ANY = pl.BlockSpec(memory_space=pl.ANY)

N_DEV = 8
D_MODEL = 1024
SEQ = 2048
N_META = 16
BLK = 128
PAD = 112
T = PAD + N_META + SEQ
NB = T // BLK
EPS = 1e-6
HEAD = 64
Q_HEADS = 16
KV_HEADS = 4
GROUP = 4
KV_W = 256
SSM_INNER = 2048
SSM_HEADS = 32
SSM_GROUPS = 4
GRP_W = 512
SSM_STATE = 128
CONV_DIM = 3072
IN_PROJ = 9760
SHARD_IN = IN_PROJ // N_DEV
NEG = -1e30

C_ZS, C_XBC, C_Q, C_ZA, C_GA, C_GS, C_K, C_V, C_DT = 0, 2048, 5120, 6144, 7168, 8192, 9216, 9472, 9728
PW = 9856
GATES_W = 3 * D_MODEL
PROJ_TILE = 1408
R_Q, R_K, R_V, R_ZA, R_ZS, R_XBC, R_DT, R_GA, R_GS = 0, 1024, 1280, 1536, 2560, 4608, 7680, 7712, 8736

ADAM_LR, ADAM_B1, ADAM_B2, ADAM_EPS, ADAM_WD, ADAM_STEP = 0.001, 0.9, 0.999, 1e-08, 0.01, 10

VMEM_LIMIT = 56 * 1024 * 1024


def _cparams():
    return pltpu.CompilerParams(vmem_limit_bytes=VMEM_LIMIT)


def _silu(x):
    return x * jax.nn.sigmoid(x)


def _dsilu(x):
    s = jax.nn.sigmoid(x)
    return s * (1.0 + x * (1.0 - s))


def _matmul(a, b, mode, out_dtype, tm, tn, name):
    if mode == "nt":
        (m, k), n = a.shape, b.shape[0]
        a_spec = pl.BlockSpec((tm, k), lambda i, j: (i, 0))
        b_spec = pl.BlockSpec((tn, k), lambda i, j: (j, 0))
        dims = (((1,), (1,)), ((), ()))
    else:
        assert mode == "tn"
        (k, m), n = a.shape, b.shape[1]
        a_spec = pl.BlockSpec((k, tm), lambda i, j: (0, i))
        b_spec = pl.BlockSpec((k, tn), lambda i, j: (0, j))
        dims = (((0,), (0,)), ((), ()))
    assert m % tm == 0 and n % tn == 0, (a.shape, b.shape, tm, tn)

    def body(a_ref, b_ref, o_ref):
        o_ref[...] = lax.dot_general(a_ref[...], b_ref[...], dims, preferred_element_type=F32).astype(out_dtype)

    return pl.pallas_call(
        body, grid=(m // tm, n // tn), in_specs=[a_spec, b_spec],
        out_specs=pl.BlockSpec((tm, tn), lambda i, j: (i, j)), out_shape=SDS((m, n), out_dtype),
        compiler_params=_cparams(), name=name)(a, b)


def _norm_u(h, g_pre):
    def body(h_ref, g_ref, u_ref):
        x = h_ref[...]
        r = lax.rsqrt(jnp.mean(x * x, axis=-1, keepdims=True) + EPS)
        u_ref[...] = (x * r * g_ref[...]).astype(BF16)

    return pl.pallas_call(
        body, grid=(NB,),
        in_specs=[pl.BlockSpec((BLK, D_MODEL), lambda i: (i, 0)), pl.BlockSpec((1, D_MODEL), lambda i: (0, 0))],
        out_specs=pl.BlockSpec((BLK, D_MODEL), lambda i: (i, 0)),
        out_shape=SDS((T, D_MODEL), BF16), name="norm_u")(h, g_pre)


DU_TM, DU_TK = T // 2, PROJ_TILE


def _d_u_norm(dproj, w_alt, h, g_pre, dres, chips=()):
    nk = PW // DU_TK
    ni = T // DU_TM
    nc = len(chips)

    def body(*refs):
        a_ref, b_ref, h_ref, g_ref, dres_ref = refs[:5]
        dh_ref, dg_ref = refs[5 + nc:7 + nc]
        acc_ref = refs[7 + 2 * nc]
        i, kk = pl.program_id(0), pl.program_id(1)
        if nc:
            ch_start, ch_finish = _chips_program(refs[5:5 + nc], refs[7 + nc:7 + 2 * nc], refs[8 + 2 * nc:])
            pl.when((i == 0) & (kk == 0))(ch_start)
        part = jnp.dot(a_ref[...], b_ref[...], preferred_element_type=F32)

        @pl.when(kk == 0)
        def _():
            acc_ref[...] = part

        @pl.when((kk > 0) & (kk < nk - 1))
        def _():
            acc_ref[...] += part

        @pl.when(kk == nk - 1)
        def _():
            du_ = acc_ref[...] + part
            x = h_ref[...]
            r = lax.rsqrt(jnp.mean(x * x, axis=-1, keepdims=True) + EPS)
            gd = g_ref[...] * du_
            dx = r * gd - x * (r * r * r) * jnp.mean(x * gd, axis=-1, keepdims=True)
            dh_ref[...] = dx + dres_ref[...]
            gpart = jnp.concatenate([jnp.sum(du_ * x * r, axis=0, keepdims=True), jnp.zeros((7, D_MODEL), F32)],
                                    axis=0)

            @pl.when(i == 0)
            def _():
                dg_ref[...] = gpart

            @pl.when(i > 0)
            def _():
                dg_ref[...] += gpart

        if nc:
            pl.when((i == ni - 1) & (kk == nk - 1))(ch_finish)

    row = pl.BlockSpec((DU_TM, D_MODEL), lambda i, kk: (i, 0))
    return pl.pallas_call(
        body, grid=(ni, nk),
        in_specs=[pl.BlockSpec((DU_TM, DU_TK), lambda i, kk: (i, kk)),
                  pl.BlockSpec((DU_TK, D_MODEL), lambda i, kk: (kk, 0)),
                  row, pl.BlockSpec((1, D_MODEL), lambda i, kk: (0, 0)), row] + [ANY] * nc,
        out_specs=[row, pl.BlockSpec((8, D_MODEL), lambda i, kk: (0, 0))] + [ANY] * nc,
        out_shape=[SDS((T, D_MODEL), F32), SDS((8, D_MODEL), F32)] + [SDS(p.shape, p.dtype) for p in chips],
        scratch_shapes=[pltpu.VMEM((DU_TM, D_MODEL), F32)] + (_chips_scratch(chips) if nc else []),
        compiler_params=_cparams(), name="d_u_norm")(dproj, w_alt, h, g_pre, dres, *chips)


def _lane_pick(row, h):
    lane = lax.broadcasted_iota(jnp.int32, row.shape, 1)
    return jnp.sum(jnp.where(lane == h, row, 0.0), axis=1, keepdims=True)


def _alibi_band():
    r = np.arange(GROUP * BLK)[:, None]
    rel = (r % BLK) - np.arange(2 * BLK)[None, :] + BLK
    out = np.empty((KV_HEADS, GROUP * BLK, 2 * BLK), np.float32)
    for kh in range(KV_HEADS):
        slope = (2.0 ** (-8.0 * (kh * GROUP + r // BLK + 1) / Q_HEADS)).astype(np.float32)
        out[kh] = np.where((rel >= 0) & (rel < BLK), -slope * rel.astype(np.float32), np.float32(NEG))
    return jnp.asarray(out)


def _attn_fn(q4s, kcats, vcats, kms, vms, sinks, n, band):
    s = lax.broadcasted_iota(jnp.int32, (GROUP * BLK, 2 * BLK), 1)
    key_off = jnp.where(n * BLK - BLK + s >= PAD + N_META, 0.0, NEG)
    rm = lax.broadcasted_iota(jnp.int32, (GROUP * BLK, N_META), 0)
    mm = lax.broadcasted_iota(jnp.int32, (GROUP * BLK, N_META), 1)
    meta_ok = (PAD + mm) <= (n * BLK + jnp.bitwise_and(rm, BLK - 1))
    gcol = jnp.right_shift(lax.broadcasted_iota(jnp.int32, (GROUP * BLK, 1), 0), 7)
    outs = []
    for kh in range(KV_HEADS):
        sk = [_lane_pick(sinks, kh * GROUP + g) for g in range(GROUP)]
        sink = jnp.where(gcol == 0, sk[0], jnp.where(gcol == 1, sk[1], jnp.where(gcol == 2, sk[2], sk[3])))
        qb = (q4s[kh] * (HEAD ** -0.5)).astype(BF16)
        sb = lax.dot_general(qb, kcats[kh].astype(BF16), (((1,), (1,)), ((), ())), preferred_element_type=F32)
        sb = sb + (band[kh] + key_off)
        sm = lax.dot_general(qb, kms[kh].astype(BF16), (((1,), (1,)), ((), ())), preferred_element_type=F32)
        sm = jnp.where(meta_ok, sm, NEG)
        mx = jnp.maximum(jnp.maximum(jnp.max(sb, axis=1, keepdims=True), jnp.max(sm, axis=1, keepdims=True)), sink)
        mx = lax.stop_gradient(mx)
        eb = jnp.exp(sb - mx)
        em = jnp.exp(sm - mx)
        es = jnp.exp(sink - mx)
        inv = 1.0 / (jnp.sum(eb, axis=1, keepdims=True) + jnp.sum(em, axis=1, keepdims=True) + es)
        pb = (eb * inv).astype(BF16)
        pm = (em * inv).astype(BF16)
        o4 = (jnp.dot(pm, vms[kh].astype(BF16), preferred_element_type=F32)
              + jnp.dot(pb, vcats[kh].astype(BF16), preferred_element_type=F32))
        outs.append(o4)
    return outs


def _attn_specs():
    prev = lambda n: jnp.maximum(n - 1, 0)
    return [
        pl.BlockSpec((BLK, D_MODEL), lambda n: (n, C_Q // D_MODEL)),
        pl.BlockSpec((BLK, KV_W), lambda n: (prev(n), C_K // KV_W)),
        pl.BlockSpec((BLK, KV_W), lambda n: (n, C_K // KV_W)),
        pl.BlockSpec((BLK, KV_W), lambda n: (prev(n), C_V // KV_W)),
        pl.BlockSpec((BLK, KV_W), lambda n: (n, C_V // KV_W)),
        pl.BlockSpec((N_META, KV_W), lambda n: (PAD // N_META, C_K // KV_W)),
        pl.BlockSpec((N_META, KV_W), lambda n: (PAD // N_META, C_V // KV_W)),
        pl.BlockSpec((1, 128), lambda n: (0, 0)),
        pl.BlockSpec((KV_HEADS, GROUP * BLK, 2 * BLK), lambda n: (0, 0, 0)),
    ]


def _attn_load(q_ref, kp_ref, kc_ref, vp_ref, vc_ref, km_ref, vm_ref):
    q4s, kcats, vcats, kms, vms = [], [], [], [], []
    for kh in range(KV_HEADS):
        q4s.append(jnp.concatenate(
            [q_ref[:, (kh * GROUP + g) * HEAD:(kh * GROUP + g + 1) * HEAD] for g in range(GROUP)], axis=0))
        cs = slice(kh * HEAD, (kh + 1) * HEAD)
        kcats.append(jnp.concatenate([kp_ref[:, cs], kc_ref[:, cs]], axis=0))
        vcats.append(jnp.concatenate([vp_ref[:, cs], vc_ref[:, cs]], axis=0))
        kms.append(km_ref[:, cs])
        vms.append(vm_ref[:, cs])
    return q4s, kcats, vcats, kms, vms


def _attn_fwd(proj, sinks):
    def body(q_ref, kp_ref, kc_ref, vp_ref, vc_ref, km_ref, vm_ref, s_ref, band_ref, o_ref):
        n = pl.program_id(0)
        args = _attn_load(q_ref, kp_ref, kc_ref, vp_ref, vc_ref, km_ref, vm_ref)
        outs = _attn_fn(*args, s_ref[...], n, [band_ref[kh] for kh in range(KV_HEADS)])
        for kh in range(KV_HEADS):
            for g in range(GROUP):
                hh = kh * GROUP + g
                o_ref[:, hh * HEAD:(hh + 1) * HEAD] = outs[kh][g * BLK:(g + 1) * BLK]

    return pl.pallas_call(
        body, grid=(NB,), in_specs=_attn_specs(),
        out_specs=pl.BlockSpec((BLK, D_MODEL), lambda n: (n, 0)),
        out_shape=SDS((T, D_MODEL), F32), name="attn_fwd")(proj, proj, proj, proj, proj, proj, proj, sinks,
                                                            _alibi_band())


def _attn_bwd(proj, sinks, do, dproj):
    def body(q_ref, kp_ref, kc_ref, vp_ref, vc_ref, km_ref, vm_ref, s_ref, band_ref, do_ref, _, dq_ref, dk_ref, dv_ref,
             ds_ref):
        n = pl.program_id(0)
        band = [band_ref[kh] for kh in range(KV_HEADS)]

        @pl.when(n == 0)
        def _():
            dk_ref[...] = jnp.zeros_like(dk_ref)
            dv_ref[...] = jnp.zeros_like(dv_ref)
            ds_ref[...] = jnp.zeros_like(ds_ref)

        args = _attn_load(q_ref, kp_ref, kc_ref, vp_ref, vc_ref, km_ref, vm_ref)
        _, vjp = jax.vjp(lambda a, b, c, d, e, f: _attn_fn(a, b, c, d, e, f, n, band), *args, s_ref[...])
        do_f = do_ref[...].astype(F32)
        cot = [jnp.concatenate([do_f[:, (kh * GROUP + g) * HEAD:(kh * GROUP + g + 1) * HEAD] for g in range(GROUP)],
                               axis=0) for kh in range(KV_HEADS)]
        dq4s, dkcats, dvcats, dkms, dvms, dsk = vjp(cot)
        ds_ref[0:1, :] += dsk
        cur = pl.ds(pl.multiple_of(n * BLK, BLK), BLK)
        meta = slice(PAD, PAD + N_META)
        for kh in range(KV_HEADS):
            cs = slice(kh * HEAD, (kh + 1) * HEAD)
            for g in range(GROUP):
                hh = kh * GROUP + g
                dq_ref[:, hh * HEAD:(hh + 1) * HEAD] = dq4s[kh][g * BLK:(g + 1) * BLK].astype(BF16)
            dk_ref[cur, cs] += dkcats[kh][BLK:]
            dv_ref[cur, cs] += dvcats[kh][BLK:]
            dk_ref[meta, cs] += dkms[kh]
            dv_ref[meta, cs] += dvms[kh]

        @pl.when(n > 0)
        def _():
            prv = pl.ds(pl.multiple_of((n - 1) * BLK, BLK), BLK)
            for kh in range(KV_HEADS):
                cs = slice(kh * HEAD, (kh + 1) * HEAD)
                dk_ref[prv, cs] += dkcats[kh][:BLK]
                dv_ref[prv, cs] += dvcats[kh][:BLK]

    full_kv = pl.BlockSpec((T, KV_W), lambda n: (0, 0))
    return pl.pallas_call(
        body, grid=(NB,),
        in_specs=_attn_specs() + [pl.BlockSpec((BLK, D_MODEL), lambda n: (n, 0)), ANY],
        out_specs=[pl.BlockSpec((BLK, D_MODEL), lambda n: (n, C_Q // D_MODEL)), full_kv, full_kv,
                   pl.BlockSpec((8, 128), lambda n: (0, 0))],
        out_shape=[SDS((T, PW), BF16), SDS((T, KV_W), F32), SDS((T, KV_W), F32), SDS((8, 128), F32)],
        input_output_aliases={10: 0},
        name="attn_bwd")(proj, proj, proj, proj, proj, proj, proj, sinks, _alibi_band(), do, dproj)


def _rows_from(ext, start):
    if start % 8 == 0:
        return ext[start:start + BLK]
    return pltpu.roll(ext, (8 - start) % (BLK + 8), 0)[8:8 + BLK]


def _conv_taps(taps, w):
    return w[0:1] * taps[0] + w[1:2] * taps[1] + w[2:3] * taps[2] + w[3:4] * taps[3]


HPG = SSM_HEADS // SSM_GROUPS


def _iota(shape, dim):
    return lax.broadcasted_iota(jnp.int32, shape, dim)


def _mm(a, b, ca=1, cb=0):
    return lax.dot_general(a.astype(BF16), b.astype(BF16), (((ca,), (cb,)), ((), ())), preferred_element_type=F32)


def _split3(v):
    hi = v.astype(BF16)
    r1 = v - hi.astype(F32)
    mid = r1.astype(BF16)
    lo = (r1 - mid.astype(F32)).astype(BF16)
    return hi, mid, lo


def _split2(v):
    hi = v.astype(BF16)
    return hi, (v - hi.astype(F32)).astype(BF16)


def _sel_r(parts, onehot, ca=1, cb=0):
    out = lax.dot_general(parts[0], onehot, (((ca,), (cb,)), ((), ())), preferred_element_type=F32)
    for p in parts[1:]:
        out = out + lax.dot_general(p, onehot, (((ca,), (cb,)), ((), ())), preferred_element_type=F32)
    return out


def _sel_l(onehot, parts):
    out = jnp.dot(onehot, parts[0], preferred_element_type=F32)
    for p in parts[1:]:
        out = out + jnp.dot(onehot, p, preferred_element_type=F32)
    return out


def _rows8(*rows):
    r = _iota((8, rows[0].shape[1]), 0)
    out = jnp.zeros((8, rows[0].shape[1]), F32)
    for k, v in enumerate(rows):
        out = jnp.where(r == k, v, out)
    return out


def _ssd_consts():
    r, c = np.arange(BLK)[:, None], np.arange(BLK)[None, :]
    tri_l = (c <= r).astype(np.float32)
    q = np.arange(GRP_W)[None, :]
    spread = np.stack([(r == g * HPG + q // HEAD) for g in range(SSM_GROUPS)]).astype(np.float32)
    pick = np.stack([np.stack([(r == g * HPG + c - k * HPG) & (c >= k * HPG) & (c < (k + 1) * HPG) for k in range(3)])
                     for g in range(SSM_GROUPS)]).astype(np.float32)
    causal = np.where(r >= (np.arange(HPG * BLK)[None, :] % BLK), 0.0, NEG).astype(np.float32)
    bf = lambda a: jnp.asarray(a, BF16)
    return dict(tri_l=bf(tri_l), tri_u=bf(tri_l.T), spread=bf(spread), unspread=bf(spread.transpose(0, 2, 1)),
                pick=bf(pick), causal=jnp.asarray(causal))


SSD_CONSTS = ("tri_l", "tri_u", "spread", "unspread", "pick", "causal")


def _ssd_forward(x, z, bm, cm, dt_raw, st_prev, dtb, alog, dskip, gn, g, cst_scr, cn):
    dt_all = jax.nn.softplus(dt_raw + dtb)
    a_row = -jnp.exp(alog)
    a_all = dt_all * a_row
    cs_all = _sel_l(cn["tri_l"][...], _split3(a_all))
    cs_parts = _split3(cs_all)
    spread = cn["spread"][g]
    dt_e = _sel_r(_split2(dt_all), spread)
    cs_e = _sel_r(cs_parts, spread)
    d_e = _sel_r(_split2(_rows8(dskip)), spread)[0:1]
    cs_last_e = jnp.sum(jnp.where(_iota((BLK, GRP_W), 0) == BLK - 1, cs_e, 0.0), axis=0, keepdims=True)
    p_e = jnp.exp(cs_e)
    w_e = jnp.exp(cs_last_e - cs_e)
    cd_e = jnp.exp(cs_last_e)
    xr = x * dt_e
    cst_scr[...] = cs_all.T
    cst_g = cst_scr[g * HPG:(g + 1) * HPG, :]
    own = jnp.right_shift(_iota((HPG, HPG * BLK), 1), 7) == _iota((HPG, HPG * BLK), 0)
    ownf = own.astype(F32)
    q_rows = [ownf, ownf, ownf] + [jnp.where(own, jnp.concatenate([p.astype(F32)] * HPG, axis=1), 0.0)
                                   for p in _split3(cst_g)]
    q2 = jnp.concatenate(q_rows + [jnp.zeros((BLK - 6 * HPG, HPG * BLK), F32)], axis=0).astype(BF16)
    lane1 = _iota((1, BLK), 1)
    p2 = jnp.where((lane1 >= 3 * HPG) & (lane1 < 6 * HPG), -1.0, 0.0)
    for k, part in enumerate(cs_parts):
        p2 = p2 + jnp.dot(part, cn["pick"][g, k], preferred_element_type=F32)
    dmat = jnp.dot(p2.astype(BF16), q2, preferred_element_type=F32)
    lam = jnp.exp(dmat + cn["causal"][...])
    gmat = _mm(cm, bm, 1, 1)
    m_all = lam * jnp.concatenate([gmat] * HPG, axis=1)
    mb = m_all.astype(BF16)
    lo = _iota((BLK, BLK), 1) < HEAD
    xrb = xr.astype(BF16)
    zero = jnp.zeros((BLK, BLK), BF16)
    bds, yd = [], []
    for i in range(HPG // 2):
        t = xrb[:, BLK * i:BLK * (i + 1)]
        bd = jnp.concatenate([jnp.where(lo, t, zero), jnp.where(lo, zero, t)], axis=0)
        bds.append(bd)
        yd.append(jnp.dot(mb[:, 2 * BLK * i:2 * BLK * (i + 1)], bd, preferred_element_type=F32))
    cs_st = _mm(cm, st_prev)
    y = jnp.concatenate(yd, axis=1) + cs_st * p_e + d_e * x
    xrw = xr * w_e
    st_new = cd_e * st_prev + _mm(bm, xrw, 0, 0)
    yz = y * _silu(z)
    rn = lax.rsqrt(jnp.sum(yz * yz, axis=1, keepdims=True) / GRP_W + EPS)
    return dict(out=yz * rn * gn, st_new=st_new, dt_all=dt_all, a_row=a_row, dt_e=dt_e, d_e=d_e, p_e=p_e, w_e=w_e,
                cd_e=cd_e, xr=xr, xrw=xrw, lam=lam, m_all=m_all, mb=mb, bds=bds, cs_st=cs_st, y=y, yz=yz, rn=rn, lo=lo)


def _ssd_backward(f, x, z, bm, cm, dt_raw, st_prev, dtb, gn, g, dout, dst_next, cst_scr, cn):
    li, si = _iota((BLK, BLK), 0), _iota((BLK, BLK), 1)
    yz, rn, y, p_e, w_e, cd_e, xr = f["yz"], f["rn"], f["y"], f["p_e"], f["w_e"], f["cd_e"], f["xr"]
    dgn = jnp.sum(dout * yz * rn, axis=0, keepdims=True)
    t = dout * gn
    dyz = rn * t - yz * (rn * rn * rn) * (jnp.sum(yz * t, axis=1, keepdims=True) / GRP_W)
    dy = dyz * _silu(z)
    dz = dyz * y * _dsilu(z)
    dx = f["d_e"] * dy
    dd_e = jnp.sum(dy * x, axis=0, keepdims=True)
    dcsst = dy * p_e
    dp_e = dy * f["cs_st"]
    dcm = _mm(dcsst, st_prev, 1, 1)
    dst_prev = _mm(cm, dcsst, 0, 0) + cd_e * dst_next
    dcd_e = jnp.sum(dst_next * st_prev, axis=0, keepdims=True)
    dbm = _mm(f["xrw"], dst_next, 1, 1)
    dxrw = _mm(bm, dst_next)
    dxr = dxrw * w_e
    dw_e = dxrw * xr
    dyb = dy.astype(BF16)
    dms, dxr_d = [], []
    for i in range(HPG // 2):
        dyp = dyb[:, BLK * i:BLK * (i + 1)]
        dms.append(lax.dot_general(dyp, f["bds"][i], (((1,), (1,)), ((), ())), preferred_element_type=F32))
        r = lax.dot_general(f["mb"][:, 2 * BLK * i:2 * BLK * (i + 1)], dyp, (((0,), (0,)), ((), ())),
                            preferred_element_type=F32)
        dxr_d.append(jnp.where(f["lo"], r[0:BLK], r[BLK:2 * BLK]))
    dm_all = jnp.concatenate(dms, axis=1)
    dxr = dxr + jnp.concatenate(dxr_d, axis=1)
    dlg = dm_all * f["lam"]
    dg = dlg[:, 0:BLK]
    for j in range(1, HPG):
        dg = dg + dlg[:, BLK * j:BLK * (j + 1)]
    dcm = dcm + _mm(dg, bm)
    dbm = dbm + _mm(dg, cm, 0, 0)
    q_all = dm_all * f["m_all"]
    col_sums = jnp.sum(q_all, axis=0, keepdims=True)
    cst_scr[...] = jnp.zeros_like(cst_scr)
    cst_scr[g * HPG:(g + 1) * HPG, :] = _rows8(
        *[col_sums[:, BLK * j:BLK * (j + 1)] for j in range(HPG)])
    dcs = -cst_scr[...].T
    for j in range(HPG):
        dcs = dcs + jnp.where(si == g * HPG + j,
                              jnp.sum(q_all[:, BLK * j:BLK * (j + 1)], axis=1, keepdims=True), 0.0)
    unspread = cn["unspread"][g]
    dww = dw_e * w_e
    per_head = _sel_r(_split2(jnp.concatenate([dp_e * p_e - dww, dxr * x], axis=0)), unspread)
    last = _sel_r(_split2(_rows8(jnp.sum(dww, axis=0, keepdims=True) + dcd_e * cd_e, dd_e)), unspread)
    dcs = dcs + per_head[0:BLK] + jnp.where(li == BLK - 1, last[0:1], 0.0)
    da = _sel_l(cn["tri_u"][...], _split2(dcs))
    ddt_all = da * f["a_row"] + per_head[BLK:2 * BLK]
    dalog = jnp.sum(da * f["dt_all"], axis=0, keepdims=True) * f["a_row"]
    dx = dx + dxr * f["dt_e"]
    ddt_raw = ddt_all * jax.nn.sigmoid(dt_raw + dtb)
    ddtb = jnp.sum(ddt_raw, axis=0, keepdims=True)
    ddskip = last[1:2]
    return dict(dx=dx, dz=dz, dbm=dbm, dcm=dcm, ddt_raw=ddt_raw, dst_prev=dst_prev, ddtb=ddtb, dalog=dalog,
                ddskip=ddskip, dgn=dgn)


ZX_W = SSM_INNER + CONV_DIM
assert C_ZS == 0 and C_XBC == SSM_INNER


def _ssd_in_specs(rev):
    cidx = (lambda c: NB - 1 - c) if rev else (lambda c: c)
    return [
        pl.BlockSpec((BLK, ZX_W), lambda c: (cidx(c), 0)),
        pl.BlockSpec((8, ZX_W), lambda c: (jnp.maximum(cidx(c) * (BLK // 8) - 1, 0), 0)),
        pl.BlockSpec((BLK, 128), lambda c: (cidx(c), C_DT // 128)),
        pl.BlockSpec((8, CONV_DIM), lambda c: (0, 0)),
        pl.BlockSpec((1, CONV_DIM), lambda c: (0, 0)),
        pl.BlockSpec((1, 128), lambda c: (0, 0)),
        pl.BlockSpec((1, 128), lambda c: (0, 0)),
        pl.BlockSpec((1, 128), lambda c: (0, 0)),
        pl.BlockSpec((1, SSM_INNER), lambda c: (0, 0)),
        pl.BlockSpec((BLK, BLK), lambda c: (0, 0)),
        pl.BlockSpec((BLK, BLK), lambda c: (0, 0)),
        pl.BlockSpec((SSM_GROUPS, BLK, GRP_W), lambda c: (0, 0, 0)),
        pl.BlockSpec((SSM_GROUPS, GRP_W, BLK), lambda c: (0, 0, 0)),
        pl.BlockSpec((SSM_GROUPS, 3, BLK, BLK), lambda c: (0, 0, 0, 0)),
        pl.BlockSpec((BLK, HPG * BLK), lambda c: (0, 0)),
    ]


N_SSD_IN = 15


def _xbc_act(zx_ref, tail_ref, w_ref, b_ref, n):
    tail = jnp.where(n > 0, tail_ref[:, SSM_INNER:], 0.0)
    xp = jnp.concatenate([tail, zx_ref[:, SSM_INNER:]], axis=0)
    taps = [_rows_from(xp, 5 + k) for k in range(4)]
    conv = _conv_taps(taps, w_ref[...]) + b_ref[...]
    valid = n * BLK + _iota((BLK, 1), 0) >= PAD
    return taps, conv, valid, jnp.where(valid, _silu(conv), 0.0)


def _grp_cols(act, i):
    b0, c0 = SSM_INNER + i * SSM_STATE, SSM_INNER + (SSM_GROUPS + i) * SSM_STATE
    return act[:, i * GRP_W:(i + 1) * GRP_W], act[:, b0:b0 + SSM_STATE], act[:, c0:c0 + SSM_STATE]


def _ssd_fwd(proj, conv_w, conv_b, dt_bias, a_log, d_skip, g_norm, gather=()):
    ng = len(gather)

    def body(*refs):
        zx_ref, tail_ref, dt_ref, w_ref, b_ref, dtb_ref, al_ref, dsk_ref, gn_ref = refs[:9]
        cn = dict(zip(SSD_CONSTS, refs[9:N_SSD_IN]))
        k0 = N_SSD_IN
        y_ref, st_ref = refs[k0 + ng:k0 + 2 + ng]
        s_scr, cst_scr = refs[k0 + 2 + 2 * ng:k0 + 4 + 2 * ng]
        c = pl.program_id(0)
        if ng:
            ag_start, ag_forward, ag_finish = _ag_program(refs[k0:k0 + ng], refs[k0 + 2 + ng:k0 + 2 + 2 * ng],
                                                          refs[k0 + 4 + 2 * ng:])
            pl.when(c == 0)(ag_start)
            pl.when(c == (3 * NB) // 4)(ag_forward)

        @pl.when(c == 0)
        def _():
            s_scr[...] = jnp.zeros_like(s_scr)

        _, _, _, act = _xbc_act(zx_ref, tail_ref, w_ref, b_ref, c)
        for i in range(SSM_GROUPS):
            st_prev = s_scr[i]
            st_ref[i, 0] = st_prev
            x, bm, cm = _grp_cols(act, i)
            f = _ssd_forward(x, zx_ref[:, i * GRP_W:(i + 1) * GRP_W], bm, cm, dt_ref[...], st_prev, dtb_ref[...],
                             al_ref[...], dsk_ref[...], gn_ref[:, i * GRP_W:(i + 1) * GRP_W], i, cst_scr.at[i], cn)
            y_ref[:, i * GRP_W:(i + 1) * GRP_W] = f["out"].astype(BF16)
            s_scr[i] = f["st_new"]
        if ng:
            pl.when(c == NB - 1)(ag_finish)

    return pl.pallas_call(
        body, grid=(NB,), in_specs=_ssd_in_specs(False) + [ANY] * ng,
        out_specs=[pl.BlockSpec((BLK, SSM_INNER), lambda c: (c, 0)),
                   pl.BlockSpec((SSM_GROUPS, 1, SSM_STATE, GRP_W), lambda c: (0, c, 0, 0))] + [ANY] * ng,
        out_shape=[SDS((T, SSM_INNER), BF16), SDS((SSM_GROUPS, NB, SSM_STATE, GRP_W), F32)]
        + [SDS((N_DEV,) + s.shape, s.dtype) for s in gather],
        scratch_shapes=[pltpu.VMEM((SSM_GROUPS, SSM_STATE, GRP_W), F32), pltpu.VMEM((SSM_GROUPS, BLK, BLK), F32)]
        + (_ag_scratch(gather) if ng else []),
        compiler_params=_cparams(),
        name="ssd_fwd")(proj, proj, proj, conv_w, conv_b, dt_bias, a_log, d_skip, g_norm,
                        *[_ssd_consts()[k] for k in SSD_CONSTS], *gather)


def _ssd_bwd(proj, conv_w, conv_b, dt_bias, a_log, d_skip, g_norm, states, dy, dproj, exchange=()):
    ne = len(exchange)

    def body(*refs):
        zx_ref, tail_ref, dt_ref, w_ref, b_ref, dtb_ref, al_ref, dsk_ref, gn_ref = refs[:9]
        cn = dict(zip(SSD_CONSTS, refs[9:N_SSD_IN]))
        st_ref, dy_ref = refs[N_SSD_IN:N_SSD_IN + 2]
        k0 = N_SSD_IN + 3
        (ddt_ref, dp_ref, ddtb_ref, dal_ref, ddsk_ref, dgn_ref, dcw_ref, dcb_ref) = refs[k0 + ne:k0 + 8 + ne]
        ds_scr, cst_scr, carry = refs[k0 + 8 + 2 * ne:k0 + 11 + 2 * ne]
        c = pl.program_id(0)
        n = NB - 1 - c
        if ne:
            ex_start, ex_finish = _direct_program(refs[k0:k0 + ne], refs[k0 + 8 + ne:k0 + 8 + 2 * ne],
                                                  refs[k0 + 11 + 2 * ne:])
            pl.when(c == 0)(ex_start)

        @pl.when(c == 0)
        def _():
            for ref in (ds_scr, carry, dgn_ref, ddtb_ref, dal_ref, ddsk_ref, dcw_ref, dcb_ref):
                ref[...] = jnp.zeros_like(ref)

        taps, conv, valid, act = _xbc_act(zx_ref, tail_ref, w_ref, b_ref, n)
        dt_raw = dt_ref[...]
        dxs, dbs, dcs = [], [], []
        for i in range(SSM_GROUPS):
            x, bm, cm = _grp_cols(act, i)
            z, gn, st_prev = zx_ref[:, i * GRP_W:(i + 1) * GRP_W], gn_ref[:, i * GRP_W:(i + 1) * GRP_W], st_ref[i, 0]
            f = _ssd_forward(x, z, bm, cm, dt_raw, st_prev, dtb_ref[...], al_ref[...], dsk_ref[...], gn, i,
                             cst_scr.at[i], cn)
            d = _ssd_backward(f, x, z, bm, cm, dt_raw, st_prev, dtb_ref[...], gn, i,
                              dy_ref[:, i * GRP_W:(i + 1) * GRP_W].astype(F32), ds_scr[i], cst_scr.at[i], cn)
            dxs.append(d["dx"])
            dbs.append(d["dbm"])
            dcs.append(d["dcm"])
            dp_ref[:, i * GRP_W:(i + 1) * GRP_W] = d["dz"].astype(BF16)
            ds_scr[i] = d["dst_prev"]
            ddt_ref[:, i * 128:(i + 1) * 128] = d["ddt_raw"]
            dgn_ref[0:1, i * GRP_W:(i + 1) * GRP_W] += d["dgn"]
            ddtb_ref[0:1, :] += d["ddtb"]
            dal_ref[0:1, :] += d["dalog"]
            ddsk_ref[0:1, :] += d["ddskip"]
        dconv = jnp.where(valid, jnp.concatenate(dxs + dbs + dcs, axis=1) * _dsilu(conv), 0.0)
        dext = jnp.concatenate([dconv, carry[...]], axis=0)
        dp_ref[:, SSM_INNER:] = _conv_taps([_rows_from(dext, 3 - k) for k in range(4)], w_ref[...]).astype(BF16)
        carry[...] = dconv[0:8]
        dcw_ref[...] += jnp.concatenate(
            [jnp.sum(dconv * taps[k], axis=0, keepdims=True) for k in range(4)]
            + [jnp.zeros((4, CONV_DIM), F32)], axis=0)
        dcb_ref[0:1, :] += jnp.sum(dconv, axis=0, keepdims=True)
        if ne:
            pl.when(c == NB - 1)(ex_finish)

    rc = lambda c: NB - 1 - c
    small = pl.BlockSpec((8, 128), lambda c: (0, 0))
    wide = lambda w: pl.BlockSpec((8, w), lambda c: (0, 0))
    return pl.pallas_call(
        body, grid=(NB,),
        in_specs=_ssd_in_specs(True) + [
            pl.BlockSpec((SSM_GROUPS, 1, SSM_STATE, GRP_W), lambda c: (0, rc(c), 0, 0)),
            pl.BlockSpec((BLK, SSM_INNER), lambda c: (rc(c), 0)), ANY] + [ANY] * ne,
        out_specs=[pl.BlockSpec((BLK, SSM_GROUPS * 128), lambda c: (rc(c), 0)),
                   pl.BlockSpec((BLK, ZX_W), lambda c: (rc(c), 0)),
                   small, small, small, wide(SSM_INNER), wide(CONV_DIM), wide(CONV_DIM)] + [ANY] * ne,
        out_shape=[SDS((T, GRP_W), F32), SDS((T, PW), BF16), SDS((8, 128), F32), SDS((8, 128), F32),
                   SDS((8, 128), F32), SDS((8, SSM_INNER), F32), SDS((8, CONV_DIM), F32), SDS((8, CONV_DIM), F32)]
        + [SDS(p.shape, p.dtype) for p in exchange],
        scratch_shapes=[pltpu.VMEM((SSM_GROUPS, SSM_STATE, GRP_W), F32), pltpu.VMEM((SSM_GROUPS, BLK, BLK), F32),
                        pltpu.VMEM((8, CONV_DIM), F32)] + (_direct_scratch(exchange) if ne else []),
        input_output_aliases={N_SSD_IN + 2: 1},
        compiler_params=_cparams(),
        name="ssd_bwd")(proj, proj, proj, conv_w, conv_b, dt_bias, a_log, d_skip, g_norm,
                        *[_ssd_consts()[k] for k in SSD_CONSTS], states, dy, dproj, *exchange)


POST_R = 272


def _post_a(o, proj, sn, w_att, w_ssm, w_o):
    def body(o_ref, za_ref, ga_ref, gs_ref, sn_ref, wa_ref, ws_ref, wo_ref, a_ref, mg_ref, ya_ref, ys_ref, out_ref):
        a = (o_ref[...] * _silu(za_ref[...])).astype(BF16)
        a_ref[...] = a
        ya = jnp.dot(a, wa_ref[...], preferred_element_type=F32)
        ys = jnp.dot(sn_ref[...], ws_ref[...], preferred_element_type=F32)
        ya_ref[...] = ya.astype(BF16)
        ys_ref[...] = ys.astype(BF16)
        mg = (jax.nn.sigmoid(ga_ref[...]) * ya + jax.nn.sigmoid(gs_ref[...]) * ys).astype(BF16)
        mg_ref[...] = mg
        out_ref[...] = jnp.dot(mg, wo_ref[...], preferred_element_type=F32)

    row = pl.BlockSpec((POST_R, D_MODEL), lambda i: (i, 0))
    pcol = lambda c0: pl.BlockSpec((POST_R, D_MODEL), lambda i: (i, c0 // D_MODEL))
    full = lambda r: pl.BlockSpec((r, D_MODEL), lambda i: (0, 0))
    return pl.pallas_call(
        body, grid=(T // POST_R,),
        in_specs=[row, pcol(C_ZA), pcol(C_GA), pcol(C_GS), pl.BlockSpec((POST_R, SSM_INNER), lambda i: (i, 0)),
                  full(D_MODEL), full(SSM_INNER), full(D_MODEL)],
        out_specs=[row, row, row, row, row],
        out_shape=[SDS((T, D_MODEL), BF16), SDS((T, D_MODEL), BF16), SDS((T, D_MODEL), BF16), SDS((T, D_MODEL), BF16),
                   SDS((T, D_MODEL), F32)],
        compiler_params=_cparams(), name="post_a")(o, proj, proj, proj, sn, w_att, w_ssm, w_o)


def _post_b(out, h, tgt, proj, ya, ys, o, g_post, w_att, w_ssm, w_o):
    def body(out_ref, h_ref, t_ref, za_ref, ga_ref, gs_ref, ya_ref, ys_ref, o_ref, gp_ref, wa_ref, ws_ref, wo_ref,
             loss_ref, dres_ref, dout_ref, dya_ref, dys_ref, do_ref, dp_ref, dsn_ref, dgp_ref):
        i = pl.program_id(0)
        x = out_ref[...]
        gp = gp_ref[...]
        r = lax.rsqrt(jnp.mean(x * x, axis=-1, keepdims=True) + EPS)
        row = i * POST_R + lax.broadcasted_iota(jnp.int32, (POST_R, 1), 0)
        res = h_ref[...] + jnp.where(row >= PAD, x * r * gp, 0.0)
        live = row >= PAD + N_META
        err = jnp.where(live, res - t_ref[...], 0.0)
        lpart = 0.5 * jnp.sum(jnp.sum(err * err, axis=1, keepdims=True) / D_MODEL, axis=0, keepdims=True)
        dres = err / D_MODEL
        dres_ref[...] = dres
        gpart = jnp.sum(dres * x * r, axis=0, keepdims=True)

        @pl.when(i == 0)
        def _():
            loss_ref[...] = jnp.zeros_like(loss_ref)
            dgp_ref[...] = jnp.zeros_like(dgp_ref)

        loss_ref[...] += jnp.broadcast_to(lpart, loss_ref.shape)
        dgp_ref[0:1, :] += gpart
        gd = gp * dres
        dout = (r * gd - x * (r * r * r) * jnp.mean(x * gd, axis=-1, keepdims=True)).astype(BF16)
        dout_ref[...] = dout
        dmg = lax.dot_general(dout, wo_ref[...], (((1,), (1,)), ((), ())), preferred_element_type=F32)
        sga = jax.nn.sigmoid(ga_ref[...])
        sgs = jax.nn.sigmoid(gs_ref[...])
        dya = (dmg * sga).astype(BF16)
        dys = (dmg * sgs).astype(BF16)
        dya_ref[...] = dya
        dys_ref[...] = dys
        dp_ref[:, C_GA - C_ZA:C_GA - C_ZA + D_MODEL] = (dmg * ya_ref[...].astype(F32) * sga * (1.0 - sga)).astype(BF16)
        dp_ref[:, C_GS - C_ZA:C_GS - C_ZA + D_MODEL] = (dmg * ys_ref[...].astype(F32) * sgs * (1.0 - sgs)).astype(BF16)
        da = lax.dot_general(dya, wa_ref[...], (((1,), (1,)), ((), ())), preferred_element_type=F32)
        za = za_ref[...]
        do_ref[...] = (da * _silu(za)).astype(BF16)
        dp_ref[:, 0:D_MODEL] = (da * o_ref[...] * _dsilu(za)).astype(BF16)
        dsn_ref[...] = lax.dot_general(dys, ws_ref[...], (((1,), (1,)), ((), ())),
                                       preferred_element_type=F32).astype(BF16)

    row = pl.BlockSpec((POST_R, D_MODEL), lambda i: (i, 0))
    pcol = lambda c0: pl.BlockSpec((POST_R, D_MODEL), lambda i: (i, c0 // D_MODEL))
    full = lambda r: pl.BlockSpec((r, D_MODEL), lambda i: (0, 0))
    small = pl.BlockSpec((8, D_MODEL), lambda i: (0, 0))
    return pl.pallas_call(
        body, grid=(T // POST_R,),
        in_specs=[row, row, row, pcol(C_ZA), pcol(C_GA), pcol(C_GS), row, row, row,
                  pl.BlockSpec((1, D_MODEL), lambda i: (0, 0)), full(D_MODEL), full(SSM_INNER), full(D_MODEL)],
        out_specs=[pl.BlockSpec((8, 128), lambda i: (0, 0)), row, row, row, row, row,
                   pl.BlockSpec((POST_R, GATES_W), lambda i: (i, C_ZA // GATES_W)),
                   pl.BlockSpec((POST_R, SSM_INNER), lambda i: (i, 0)), small],
        out_shape=[SDS((8, 128), F32), SDS((T, D_MODEL), F32), SDS((T, D_MODEL), BF16), SDS((T, D_MODEL), BF16),
                   SDS((T, D_MODEL), BF16), SDS((T, D_MODEL), BF16), SDS((T, PW), BF16),
                   SDS((T, SSM_INNER), BF16), SDS((8, D_MODEL), F32)],
        compiler_params=_cparams(), name="post_b")(out, h, tgt, proj, proj, proj, ya, ys, o, g_post, w_att, w_ssm, w_o)


TAIL_W = PW - C_K


def _dproj_tail(dproj, dk, dv, ddt4):
    rows = T // 4

    def body(_, dk_ref, dv_ref, ddt_ref, o_ref, buf, sem):
        n = pl.program_id(0)
        d4 = ddt_ref[...]
        buf[:, 0:KV_W] = dk_ref[...].astype(BF16)
        buf[:, KV_W:2 * KV_W] = dv_ref[...].astype(BF16)
        buf[:, 2 * KV_W:TAIL_W] = (d4[:, 0:128] + d4[:, 128:256] + d4[:, 256:384] + d4[:, 384:512]).astype(BF16)
        cp = pltpu.make_async_copy(buf, o_ref.at[pl.ds(pl.multiple_of(n * rows, 16), rows), pl.ds(C_K, TAIL_W)], sem)
        cp.start()
        cp.wait()

    spec = lambda w: pl.BlockSpec((rows, w), lambda i: (i, 0))
    return pl.pallas_call(
        body, grid=(T // rows,), in_specs=[ANY, spec(KV_W), spec(KV_W), spec(GRP_W)], out_specs=ANY,
        out_shape=SDS((T, PW), BF16), input_output_aliases={0: 0},
        scratch_shapes=[pltpu.VMEM((rows, TAIL_W), BF16), pltpu.SemaphoreType.DMA],
        name="dproj_tail")(dproj, dk, dv, ddt4)


def _adamw_math(w, g, m, v):
    m = ADAM_B1 * m + (1.0 - ADAM_B1) * g
    v = ADAM_B2 * v + (1.0 - ADAM_B2) * (g * g)
    m_hat = m / (1.0 - ADAM_B1 ** ADAM_STEP)
    v_hat = v / (1.0 - ADAM_B2 ** ADAM_STEP)
    delta = -ADAM_LR * (m_hat / (jnp.sqrt(v_hat) + ADAM_EPS) + ADAM_WD * w)
    return delta, m, v


def _sum_adamw(recv, w, m, v, tc, name):
    rows, cols = w.shape
    nslab = recv.shape[0]
    assert cols % tc == 0

    def body(r_ref, w_ref, m_ref, v_ref, g_ref, d_ref, nm_ref, nv_ref):
        g = r_ref[0].astype(F32)
        for d in range(1, nslab):
            g = g + r_ref[d].astype(F32)
        g_ref[...] = g
        delta, nm, nv = _adamw_math(w_ref[...], g, m_ref[...], v_ref[...])
        d_ref[...] = delta
        nm_ref[...] = nm
        nv_ref[...] = nv

    blk = pl.BlockSpec((rows, tc), lambda i: (0, i))
    return pl.pallas_call(
        body, grid=(cols // tc,),
        in_specs=[pl.BlockSpec((nslab, rows, tc), lambda i: (0, 0, i)), blk, blk, blk],
        out_specs=[blk, blk, blk, blk], out_shape=[SDS((rows, cols), F32)] * 4,
        compiler_params=_cparams(), name=name)(recv, w, m, v)


def _sum_adamw_rows3(recv, w3, m3, v3, name, exchange=()):
    pairs = 61
    assert (SHARD_IN // 2) % pairs == 0
    nsteps = SHARD_IN // 2 // pairs
    ne = len(exchange)

    def body(*refs):
        r_ref, w_ref, m_ref, v_ref = refs[:4]
        g_ref, d_ref, nm_ref, nv_ref = refs[4 + ne:8 + ne]
        if ne:
            ex_start, ex_finish = _direct_program(refs[4:4 + ne], refs[8 + ne:8 + 2 * ne], refs[8 + 2 * ne:])
            pl.when(pl.program_id(0) == 0)(ex_start)
        g = r_ref[0].astype(F32)
        for d in range(1, N_CHIP):
            g = g + r_ref[d].astype(F32)
        g = g.reshape(2 * pairs, ROW_TILES, 128)
        g_ref[...] = g
        delta, nm, nv = _adamw_math(w_ref[...], g, m_ref[...], v_ref[...])
        d_ref[...] = delta
        nm_ref[...] = nm
        nv_ref[...] = nv
        if ne:
            pl.when(pl.program_id(0) == nsteps - 1)(ex_finish)

    blk = pl.BlockSpec((2 * pairs, ROW_TILES, 128), lambda i: (i, 0, 0))
    return pl.pallas_call(
        body, grid=(nsteps,),
        in_specs=[pl.BlockSpec((N_CHIP, pairs, 2 * ROW_TILES, 128), lambda i: (0, i, 0, 0)), blk, blk, blk]
        + [ANY] * ne,
        out_specs=[blk, blk, blk, blk] + [ANY] * ne,
        out_shape=[SDS(w3.shape, F32)] * 4 + [SDS(p.shape, p.dtype) for p in exchange],
        scratch_shapes=_direct_scratch(exchange) if ne else [],
        compiler_params=_cparams(), name=name)(recv, w3, m3, v3, *exchange)


ROW_GPRE, ROW_CONVB, ROW_DTB, ROW_ALOG, ROW_DSKIP, ROW_SINK, ROW_GSSM, ROW_GPOST = 0, 1, 4, 5, 6, 7, 8, 10
ROW_LOSS = 11
REP_ROWS, ROW_CONVW, ROW_META, SM_ROWS = 16, 16, 24, 40
CW_SHARD = CONV_DIM // N_DEV
META_SHARD = D_MODEL // N_DEV


def _small_pack(dgpre, db, ddtb, dal, ddsk, dsink, dgn, dgp, dw, loss, dh):
    def body(dgpre_ref, db_ref, ddtb_ref, dal_ref, ddsk_ref, dsink_ref, dgn_ref, dgp_ref, dw_ref, loss_ref, dh_ref,
             o_ref, rep):
        rep[...] = jnp.zeros_like(rep)
        rep[ROW_LOSS:ROW_LOSS + 1, 0:128] = loss_ref[0:1, :]
        rep[ROW_GPRE:ROW_GPRE + 1, :] = dgpre_ref[0:1, :]
        for k in range(3):
            rep[ROW_CONVB + k:ROW_CONVB + k + 1, :] = db_ref[0:1, 1024 * k:1024 * (k + 1)]
        rep[ROW_DTB:ROW_DTB + 1, 0:128] = ddtb_ref[0:1, :]
        rep[ROW_ALOG:ROW_ALOG + 1, 0:128] = dal_ref[0:1, :]
        rep[ROW_DSKIP:ROW_DSKIP + 1, 0:128] = ddsk_ref[0:1, :]
        rep[ROW_SINK:ROW_SINK + 1, 0:128] = dsink_ref[0:1, :]
        rep[ROW_GSSM:ROW_GSSM + 1, :] = dgn_ref[0:1, 0:1024]
        rep[ROW_GSSM + 1:ROW_GSSM + 2, :] = dgn_ref[0:1, 1024:2048]
        rep[ROW_GPOST:ROW_GPOST + 1, :] = dgp_ref[0:1, :]
        cw = dw_ref[...]
        mh = dh_ref[...]
        o_ref[...] = jnp.zeros_like(o_ref)
        for p in range(N_DEV):
            o_ref[p, 0:REP_ROWS, :] = rep[...]
            o_ref[p, ROW_CONVW:ROW_CONVW + 8, 0:CW_SHARD] = cw[:, p * CW_SHARD:(p + 1) * CW_SHARD]
            o_ref[p, ROW_META:ROW_META + N_META, 0:META_SHARD] = mh[:, p * META_SHARD:(p + 1) * META_SHARD]

    ins = [dgpre, db, ddtb, dal, ddsk, dsink, dgn, dgp, dw, loss]
    return pl.pallas_call(
        body, grid=(1,),
        in_specs=[pl.BlockSpec(a.shape, lambda i: (0, 0)) for a in ins]
        + [pl.BlockSpec((N_META, D_MODEL), lambda i: (PAD // N_META, 0))],
        out_specs=pl.BlockSpec((N_DEV, SM_ROWS, 1024), lambda i: (0, 0, 0)),
        out_shape=SDS((N_DEV, SM_ROWS, 1024), F32), scratch_shapes=[pltpu.VMEM((REP_ROWS, 1024), F32)],
        name="small_pack")(*ins, dh)


def _small_finish(recv, params):
    npar = len(params)

    def body(*refs):
        r_ref = refs[0]
        wmv = refs[1:1 + 3 * npar]
        outs = refs[1 + 3 * npar:1 + 7 * npar]
        loss_ref = refs[1 + 7 * npar]
        gs = refs[-1]
        g = r_ref[0]
        for d in range(1, recv.shape[0]):
            g = g + r_ref[d]
        gs[...] = g
        loss_ref[...] = gs[ROW_LOSS:ROW_LOSS + 1, 0:128]
        grads = [
            gs[ROW_GPRE:ROW_GPRE + 1, :],
            jnp.concatenate([gs[ROW_CONVB + k:ROW_CONVB + k + 1, :] for k in range(3)], axis=1),
            gs[ROW_DTB:ROW_DTB + 1, 0:SSM_HEADS], gs[ROW_ALOG:ROW_ALOG + 1, 0:SSM_HEADS],
            gs[ROW_DSKIP:ROW_DSKIP + 1, 0:SSM_HEADS], gs[ROW_SINK:ROW_SINK + 1, 0:Q_HEADS],
            jnp.concatenate([gs[ROW_GSSM:ROW_GSSM + 1, :], gs[ROW_GSSM + 1:ROW_GSSM + 2, :]], axis=1),
            gs[ROW_GPOST:ROW_GPOST + 1, :],
            gs[ROW_CONVW:ROW_CONVW + 4, 0:CW_SHARD],
            gs[ROW_META:ROW_META + N_META, 0:META_SHARD]]
        for i in range(npar):
            w_ref, m_ref, v_ref = wmv[3 * i:3 * i + 3]
            delta, nm, nv = _adamw_math(w_ref[...], grads[i], m_ref[...], v_ref[...])
            outs[4 * i][...] = grads[i]
            outs[4 * i + 1][...] = delta
            outs[4 * i + 2][...] = nm
            outs[4 * i + 3][...] = nv

    flat = [a for wmv in params for a in wmv]
    res = pl.pallas_call(
        body, out_shape=[SDS(wmv[0].shape, F32) for wmv in params for _ in range(4)] + [SDS((1, 128), F32)],
        scratch_shapes=[pltpu.VMEM((SM_ROWS, 1024), F32)], name="small_finish")(recv, *flat)
    return [tuple(res[4 * i:4 * i + 4]) for i in range(npar)], res[4 * npar]


def _slab(ref, px, py, pc):
    return ref.at[4 * px + 2 * py + pc]


def _bounce(src, dst, buf, sem):
    cp = pltpu.make_async_copy(src, buf, sem)
    cp.start()
    cp.wait()
    cp = pltpu.make_async_copy(buf, dst, sem)
    cp.start()
    cp.wait()


def _ag_program(ins, outs, scratch):
    na = len(ins)
    send_sems, recv_sems, local_sems = scratch[:3]
    bufs = scratch[3:]
    x, y, c = lax.axis_index("x"), lax.axis_index("y"), lax.axis_index("c")
    me, sibling = (x, y, c), (x, y, 1 - c)
    chips = [(1 - x, y), (x, 1 - y), (1 - x, 1 - y)]

    def copy(a, k, block, to, src=None):
        dst = _slab(outs[a], *block)
        return pltpu.make_async_remote_copy(
            src_ref=dst if src is None else src, dst_ref=dst, send_sem=send_sems.at[a, k],
            recv_sem=recv_sems.at[a, k], device_id=to, device_id_type=MESH)

    def own_sends():
        out = []
        for a in range(na):
            out.append(copy(a, 0, me, sibling, src=ins[a]))
            out += [copy(a, 1 + j, me, (*chip, c), src=ins[a]) for j, chip in enumerate(chips)]
        return out

    def start():
        for cp in own_sends():
            cp.start()
        for a in range(na):
            _bounce(ins[a], _slab(outs[a], *me), bufs[a], local_sems.at[a])

    def forward():
        for j, chip in enumerate(chips):
            for a in range(na):
                copy(a, 1 + j, (*chip, c), me).wait_recv()
                copy(a, 4 + j, (*chip, c), sibling).start()

    def finish():
        for a in range(na):
            copy(a, 0, sibling, me).wait_recv()
            for j, chip in enumerate(chips):
                copy(a, 4 + j, (*chip, 1 - c), me).wait_recv()
        for cp in own_sends():
            cp.wait_send()
        for j, chip in enumerate(chips):
            for a in range(na):
                copy(a, 4 + j, (*chip, c), sibling).wait_send()

    return start, forward, finish


def _ag_scratch(shards):
    na = len(shards)
    return [pltpu.SemaphoreType.DMA((na, 7)), pltpu.SemaphoreType.DMA((na, 7)),
            pltpu.SemaphoreType.DMA((na,))] + [pltpu.VMEM(s.shape, s.dtype) for s in shards]


def _all_gather(shards):
    na = len(shards)

    def body(*refs):
        start, forward, finish = _ag_program(refs[:na], refs[na:2 * na], refs[2 * na:])
        start()
        forward()
        finish()

    return pl.pallas_call(
        body, in_specs=[ANY] * na, out_specs=[ANY] * na,
        out_shape=[SDS((N_DEV,) + s.shape, s.dtype) for s in shards],
        scratch_shapes=_ag_scratch(shards), name="all_gather")(*shards)


N_CHIP = 4


def _pair_sum(own, got, name):
    na = len(own)

    def body(*refs):
        for a in range(na):
            o_ref, g_ref, s_ref = refs[a], refs[na + a], refs[2 * na + a]
            s_ref[...] = (o_ref[...].astype(F32) + g_ref[...].astype(F32)).astype(s_ref.dtype)

    def spec(p):
        nd = len(p.shape) - 1
        return pl.BlockSpec((1,) + p.shape[1:], lambda k, nd=nd: (k,) + (0,) * nd)

    return pl.pallas_call(
        body, grid=(N_CHIP,), in_specs=[spec(p) for p in own] + [spec(p) for p in got],
        out_specs=[spec(p) for p in own], out_shape=[SDS(p.shape, p.dtype) for p in own],
        compiler_params=_cparams(), name=name)(*own, *got)


def _chips_program(ins, outs, scratch):
    na = len(ins)
    send_sems, recv_sems, local_sems = scratch[:3]
    bufs = scratch[3:]
    x, y, c = lax.axis_index("x"), lax.axis_index("y"), lax.axis_index("c")
    mine = 2 * x + y
    chips = [(1 - x, y), (x, 1 - y), (1 - x, 1 - y)]

    def send(a, j):
        px, py = chips[j]
        return pltpu.make_async_remote_copy(
            src_ref=ins[a].at[2 * px + py], dst_ref=outs[a].at[mine], send_sem=send_sems.at[a, j],
            recv_sem=recv_sems.at[a, j], device_id=(px, py, c), device_id_type=MESH)

    def arrival(a, j):
        px, py = chips[j]
        return pltpu.make_async_remote_copy(
            src_ref=ins[a].at[2 * px + py], dst_ref=outs[a].at[2 * px + py], send_sem=send_sems.at[a, j],
            recv_sem=recv_sems.at[a, j], device_id=(px, py, c), device_id_type=MESH)

    def start():
        for a in range(na):
            for j in range(3):
                send(a, j).start()
        for a in range(na):
            _bounce(ins[a].at[mine], outs[a].at[mine], bufs[a], local_sems.at[a])

    def finish():
        for a in range(na):
            for j in range(3):
                arrival(a, j).wait_recv()
        for a in range(na):
            for j in range(3):
                send(a, j).wait_send()

    return start, finish


def _chips_scratch(parts):
    na = len(parts)
    return [pltpu.SemaphoreType.DMA((na, 3)), pltpu.SemaphoreType.DMA((na, 3)),
            pltpu.SemaphoreType.DMA((na,))] + [pltpu.VMEM(p.shape[1:], p.dtype) for p in parts]


def _direct_program(ins, outs, scratch):
    na = len(ins)
    send_sems, recv_sems, local_sems = scratch[:3]
    bufs = scratch[3:]
    x, y, c = lax.axis_index("x"), lax.axis_index("y"), lax.axis_index("c")
    me = (x, y, c)
    peers = []
    for k in range(1, N_DEV):
        dx, dy, dc = (k >> 2) & 1, (k >> 1) & 1, k & 1
        peers.append(((1 - x) if dx else x, (1 - y) if dy else y, (1 - c) if dc else c))

    def send(a, k):
        return pltpu.make_async_remote_copy(
            src_ref=_slab(ins[a], *peers[k]), dst_ref=_slab(outs[a], *me), send_sem=send_sems.at[a, k],
            recv_sem=recv_sems.at[a, k], device_id=peers[k], device_id_type=MESH)

    def arrival(a, k):
        return pltpu.make_async_remote_copy(
            src_ref=_slab(ins[a], *peers[k]), dst_ref=_slab(outs[a], *peers[k]), send_sem=send_sems.at[a, k],
            recv_sem=recv_sems.at[a, k], device_id=peers[k], device_id_type=MESH)

    def start():
        for a in range(na):
            for k in range(N_DEV - 1):
                send(a, k).start()
        for a in range(na):
            _bounce(_slab(ins[a], *me), _slab(outs[a], *me), bufs[a], local_sems.at[a])

    def finish():
        for a in range(na):
            for k in range(N_DEV - 1):
                arrival(a, k).wait_recv()
        for a in range(na):
            for k in range(N_DEV - 1):
                send(a, k).wait_send()

    return start, finish


def _direct_scratch(parts):
    na = len(parts)
    return [pltpu.SemaphoreType.DMA((na, N_DEV - 1)), pltpu.SemaphoreType.DMA((na, N_DEV - 1)),
            pltpu.SemaphoreType.DMA((na,))] + [pltpu.VMEM(p.shape[1:], p.dtype) for p in parts]


ROW_TILES = D_MODEL // 128


def _rows3(t):
    return jnp.transpose(t[0]).reshape(t.shape[2], ROW_TILES, 128)


def _unrows3(t):
    return jnp.transpose(t.reshape(t.shape[0], D_MODEL))[None]


def _cast_shards(w_in3, w_att, w_ssm, w_o):
    def body(wi_ref, wa_ref, ws_ref, wo_ref, a_ref, b_ref, c_ref, d_ref):
        a_ref[...] = wi_ref[...].reshape(SHARD_IN // 2, 2 * ROW_TILES, 128).astype(BF16)
        b_ref[...] = wa_ref[...].astype(BF16)
        c_ref[...] = ws_ref[...].astype(BF16)
        d_ref[...] = wo_ref[...].astype(BF16)

    return pl.pallas_call(
        body, out_shape=[SDS((SHARD_IN // 2, 2 * ROW_TILES, 128), BF16), SDS(w_att.shape, BF16),
                         SDS(w_ssm.shape, BF16), SDS(w_o.shape, BF16)],
        compiler_params=_cparams(), name="cast_shards")(w_in3, w_att, w_ssm, w_o)


def _pieces():
    out = []
    for r0, c0, w in _SEGS:
        r = r0
        while r < r0 + w:
            d = r // SHARD_IN
            n = min(r0 + w, (d + 1) * SHARD_IN) - r
            out.append((c0 + (r - r0), d, r - d * SHARD_IN, n))
            r += n
    return out


def _to_aligned_t(slabs):
    def body(a_ref, o_ref):
        for (t, d, s, n) in _pieces():
            o_ref[t:t + n, :] = a_ref[d, s // 2:(s + n) // 2].reshape(n, D_MODEL)
        o_ref[C_DT + 32:C_DT + 128, :] = jnp.zeros((96, D_MODEL), slabs.dtype)

    return pl.pallas_call(body, out_shape=SDS((PW, D_MODEL), slabs.dtype), compiler_params=_cparams(),
                          name="to_aligned")(slabs)


def _from_aligned_pair(g):
    slab = (SHARD_IN // 2, 2 * ROW_TILES, 128)
    by_slab = [[p for p in _pieces() if p[1] == d] for d in range(N_DEV)]

    def body(g_ref, own_ref, got_ref, slabs, send_sems, recv_sems, local_sems):
        x, y, c = lax.axis_index("x"), lax.axis_index("y"), lax.axis_index("c")
        sibling = (x, y, 1 - c)

        def to_own(d, k):
            return pltpu.make_async_copy(slabs.at[d], own_ref.at[k], local_sems.at[k])

        def to_sibling(d, k):
            return pltpu.make_async_remote_copy(
                src_ref=slabs.at[d], dst_ref=got_ref.at[k], send_sem=send_sems.at[k], recv_sem=recv_sems.at[k],
                device_id=sibling, device_id_type=MESH)

        for d in range(N_DEV):
            for (t, _, s, n) in by_slab[d]:
                slabs[d, s // 2:(s + n) // 2] = g_ref[t:t + n, :].reshape(n // 2, 2 * ROW_TILES, 128)
            k, side = d // 2, d % 2
            pl.when(c == side)(to_own(d, k).start)
            pl.when(c != side)(to_sibling(d, k).start)
        for k in range(N_CHIP):
            to_own(0, k).wait()
            to_sibling(0, k).wait()

    half = SDS((N_CHIP,) + slab, g.dtype)
    return pl.pallas_call(
        body, in_specs=[pl.BlockSpec(memory_space=pltpu.VMEM)], out_specs=[ANY, ANY], out_shape=[half, half],
        scratch_shapes=[pltpu.VMEM((N_DEV,) + slab, g.dtype), pltpu.SemaphoreType.DMA((N_CHIP,)),
                        pltpu.SemaphoreType.DMA((N_CHIP,)), pltpu.SemaphoreType.DMA((N_CHIP,))],
        compiler_params=_cparams(), name="from_aligned_pair")(g)


_SEGS = [
    (R_Q, C_Q, 1024), (R_K, C_K, 256), (R_V, C_V, 256), (R_ZA, C_ZA, 1024), (R_ZS, C_ZS, 2048),
    (R_XBC, C_XBC, 3072), (R_DT, C_DT, 32), (R_GA, C_GA, 1024), (R_GS, C_GS, 1024)]


def _pad_lanes(v, n=128):
    return jnp.pad(v, ((0, 0), (0, n - v.shape[1])))


def _device_step(h, tgt, w_alt, w_out, g_pre, conv_w8, conv_b, dt_bias, a_log, d_skip, sinks, g_ssm, g_post, on_mesh):
    dtb, al, dsk, snk = _pad_lanes(dt_bias), _pad_lanes(a_log), _pad_lanes(d_skip), _pad_lanes(sinks)
    u = _norm_u(h, g_pre)
    proj = _matmul(u, w_alt, "nt", F32, T, PROJ_TILE, "in_proj")
    o = _attn_fwd(proj, snk)
    if on_mesh:
        sn, states, att_all, ssm_all, o_all = _ssd_fwd(proj, conv_w8, conv_b, dtb, al, dsk, g_ssm, gather=w_out)
        w_att = att_all.reshape(D_MODEL, D_MODEL)
        w_ssm = ssm_all.reshape(SSM_INNER, D_MODEL)
        w_o = o_all.reshape(D_MODEL, D_MODEL)
    else:
        sn, states = _ssd_fwd(proj, conv_w8, conv_b, dtb, al, dsk, g_ssm)
        w_att, w_ssm, w_o = w_out
    a_in, mg, ya, ys, out = _post_a(o, proj, sn, w_att, w_ssm, w_o)
    (loss, dres, dout, dya, dys, do, dproj, dsn, dgp) = _post_b(
        out, h, tgt, proj, ya, ys, o, g_post, w_att, w_ssm, w_o)
    dw_att = _matmul(a_in, dya, "tn", BF16, D_MODEL, D_MODEL, "d_w_att")
    dw_ssm = _matmul(sn, dys, "tn", BF16, D_MODEL, D_MODEL, "d_w_ssm")
    dw_o = _matmul(mg, dout, "tn", BF16, D_MODEL, D_MODEL, "d_w_o")
    res = {}
    if on_mesh:
        parts = [dw_att.reshape(N_DEV, 128, D_MODEL), dw_ssm.reshape(N_DEV, 256, D_MODEL),
                 dw_o.reshape(N_DEV, 128, D_MODEL)]
        (ddt4, dproj, ddtb, dal, ddsk, dgn, dcw, dcb, res["r_att"], res["r_ssm"], res["r_o"]) = _ssd_bwd(
            proj, conv_w8, conv_b, dtb, al, dsk, g_ssm, states, dsn, dproj, exchange=parts)
    else:
        ddt4, dproj, ddtb, dal, ddsk, dgn, dcw, dcb = _ssd_bwd(proj, conv_w8, conv_b, dtb, al, dsk, g_ssm, states,
                                                               dsn, dproj)
        res.update(dw_att=dw_att, dw_ssm=dw_ssm, dw_o=dw_o)
    dproj, dk, dv, dsink = _attn_bwd(proj, snk, do, dproj)
    dproj = _dproj_tail(dproj, dk, dv, ddt4)
    dw_alt = _matmul(dproj, u, "tn", BF16, PROJ_TILE, D_MODEL, "d_w_in")
    if on_mesh:
        own, got = _from_aligned_pair(dw_alt)
        dh, dgpre, res["r_in"] = _d_u_norm(dproj, w_alt, h, g_pre, dres,
                                           chips=_pair_sum([own], [got], "pair_sum_w_in"))
    else:
        dh, dgpre = _d_u_norm(dproj, w_alt, h, g_pre, dres)
        res["dw_alt"] = dw_alt
    small = (dgpre, dcb, ddtb, dal, ddsk, dsink, dgn, dgp, dcw)
    if on_mesh:
        res["small_pack"] = _small_pack(*small, loss, dh)
    else:
        res["small"] = small
    res.update(loss=loss[0, 0], dh=dh)
    return res


def kernel(x, meta_tokens, g_pre, w_in, conv_w, conv_b, dt_bias, a_log, d_skip, attn_sinks, g_ssm_norm, w_out_att, w_out_ssm, w_out, g_post, loss_target, m_meta_tokens, m_g_pre, m_w_in, m_conv_w, m_conv_b, m_dt_bias, m_a_log, m_d_skip, m_attn_sinks, m_g_ssm_norm, m_w_out_att, m_w_out_ssm, m_w_out, m_g_post, v_meta_tokens, v_g_pre, v_w_in, v_conv_w, v_conv_b, v_dt_bias, v_a_log, v_d_skip, v_attn_sinks, v_g_ssm_norm, v_w_out_att, v_w_out_ssm, v_w_out, v_g_post):
    w_in3, m_in3, v_in3 = _rows3(w_in), _rows3(m_w_in), _rows3(v_w_in)
    a_sh, att_sh, ssm_sh, o_sh = _cast_shards(w_in3, w_out_att[0], w_out_ssm[0], w_out[0])
    cw_sh = jnp.pad(conv_w[0], ((0, 4), (0, 0)))
    a_all, meta_all, cw_all = _all_gather([a_sh, meta_tokens, cw_sh])
    w_alt = _to_aligned_t(a_all)
    meta_full = meta_all.transpose(1, 0, 2).reshape(N_META, D_MODEL)
    conv_w8 = cw_all.transpose(1, 0, 2).reshape(8, CONV_DIM)

    h = jnp.concatenate([jnp.zeros((PAD, D_MODEL), F32), meta_full, x[0]], axis=0)
    tgt = jnp.concatenate([jnp.zeros((PAD + N_META, D_MODEL), F32), loss_target[0]], axis=0)
    r = _device_step(h, tgt, w_alt, (att_sh, ssm_sh, o_sh), g_pre, conv_w8, conv_b, dt_bias, a_log, d_skip,
                     attn_sinks, g_ssm_norm, g_post, True)
    grad_x = r["dh"][PAD + N_META:][None]

    *res_in, r_small = _sum_adamw_rows3(r["r_in"], w_in3, m_in3, v_in3, "adamw_w_in", exchange=[r["small_pack"]])
    res_in = [_unrows3(t) for t in res_in]
    res_att = [t[None] for t in _sum_adamw(r["r_att"], w_out_att[0], m_w_out_att[0], v_w_out_att[0], 512,
                                           "adamw_w_att")]
    res_ssm = [t[None] for t in _sum_adamw(r["r_ssm"], w_out_ssm[0], m_w_out_ssm[0], v_w_out_ssm[0], 512,
                                           "adamw_w_ssm")]
    res_o = [t[None] for t in _sum_adamw(r["r_o"], w_out[0], m_w_out[0], v_w_out[0], 512, "adamw_w_o")]
    (res_gpre, res_convb, res_dtb, res_alog, res_dskip, res_sink, res_gssm, res_gpost, res_cw, res_meta), loss = _small_finish(
        r_small, [(g_pre, m_g_pre, v_g_pre), (conv_b, m_conv_b, v_conv_b), (dt_bias, m_dt_bias, v_dt_bias),
                       (a_log, m_a_log, v_a_log), (d_skip, m_d_skip, v_d_skip),
                       (attn_sinks, m_attn_sinks, v_attn_sinks), (g_ssm_norm, m_g_ssm_norm, v_g_ssm_norm),
                       (g_post, m_g_post, v_g_post), (conv_w[0], m_conv_w[0], v_conv_w[0]),
                       (meta_tokens, m_meta_tokens, v_meta_tokens)])
    res_cw = [t[None] for t in res_cw]
    per_weight = [res_meta, res_gpre, res_in, res_cw, res_convb, res_dtb, res_alog, res_dskip, res_sink, res_gssm,
                  res_att, res_ssm, res_o, res_gpost]
    return (loss[0, 0], grad_x, *[p[0] for p in per_weight], *[p[1] for p in per_weight], *[p[2] for p in per_weight],
            *[p[3] for p in per_weight])
```

```python
import jax
import jax.numpy as jnp
import numpy as np
from jax import lax
from jax.experimental import pallas as pl
from jax.experimental.pallas import tpu as pltpu

F32 = jnp.float32
BF16 = jnp.bfloat16
SDS = jax.ShapeDtypeStruct
MESH = pl.DeviceIdType.MESH
ANY = pl.BlockSpec(memory_space=pl.ANY)

N_DEV = 8
D_MODEL = 1024
SEQ = 2048
N_META = 16
BLK = 128
PAD = 112
T = PAD + N_META + SEQ
NB = T // BLK
EPS = 1e-6
HEAD = 64
Q_HEADS = 16
KV_HEADS = 4
GROUP = 4
KV_W = 256
SSM_INNER = 2048
SSM_HEADS = 32
SSM_GROUPS = 4
GRP_W = 512
SSM_STATE = 128
CONV_DIM = 3072
IN_PROJ = 9760
SHARD_IN = IN_PROJ // N_DEV
NEG = -1e30

C_ZS, C_XBC, C_Q, C_ZA, C_GA, C_GS, C_K, C_V, C_DT = 0, 2048, 5120, 6144, 7168, 8192, 9216, 9472, 9728
PW = 9856
GATES_W = 3 * D_MODEL
PROJ_TILE = 1408
OUT_W_TILE = 512
R_Q, R_K, R_V, R_ZA, R_ZS, R_XBC, R_DT, R_GA, R_GS = 0, 1024, 1280, 1536, 2560, 4608, 7680, 7712, 8736

ADAM_LR, ADAM_B1, ADAM_B2, ADAM_EPS, ADAM_WD, ADAM_STEP = 0.001, 0.9, 0.999, 1e-08, 0.01, 10

VMEM_LIMIT = 56 * 1024 * 1024


def _cparams():
    return pltpu.CompilerParams(vmem_limit_bytes=VMEM_LIMIT)


def _silu(x):
    return x * jax.nn.sigmoid(x)


def _dsilu(x):
    s = jax.nn.sigmoid(x)
    return s * (1.0 + x * (1.0 - s))


def _matmul(a, b, mode, out_dtype, tm, tn, name):
    if mode == "nt":
        (m, k), n = a.shape, b.shape[0]
        a_spec = pl.BlockSpec((tm, k), lambda i, j: (i, 0))
        b_spec = pl.BlockSpec((tn, k), lambda i, j: (j, 0))
        dims = (((1,), (1,)), ((), ()))
    else:
        assert mode == "tn"
        (k, m), n = a.shape, b.shape[1]
        a_spec = pl.BlockSpec((k, tm), lambda i, j: (0, i))
        b_spec = pl.BlockSpec((k, tn), lambda i, j: (0, j))
        dims = (((0,), (0,)), ((), ()))
    assert m % tm == 0 and n % tn == 0, (a.shape, b.shape, tm, tn)

    def body(a_ref, b_ref, o_ref):
        o_ref[...] = lax.dot_general(a_ref[...], b_ref[...], dims, preferred_element_type=F32).astype(out_dtype)

    return pl.pallas_call(
        body, grid=(m // tm, n // tn), in_specs=[a_spec, b_spec],
        out_specs=pl.BlockSpec((tm, tn), lambda i, j: (i, j)), out_shape=SDS((m, n), out_dtype),
        compiler_params=_cparams(), name=name)(a, b)


def _norm_u(h, g_pre):
    def body(h_ref, g_ref, u_ref):
        x = h_ref[...]
        r = lax.rsqrt(jnp.mean(x * x, axis=-1, keepdims=True) + EPS)
        u_ref[...] = (x * r * g_ref[...]).astype(BF16)

    return pl.pallas_call(
        body, grid=(NB,),
        in_specs=[pl.BlockSpec((BLK, D_MODEL), lambda i: (i, 0)), pl.BlockSpec((1, D_MODEL), lambda i: (0, 0))],
        out_specs=pl.BlockSpec((BLK, D_MODEL), lambda i: (i, 0)),
        out_shape=SDS((T, D_MODEL), BF16), name="norm_u")(h, g_pre)


DU_TM, DU_TK = T // 2, PROJ_TILE


def _d_u_norm(dproj, w_alt, h, g_pre, dres, chips=()):
    nk = PW // DU_TK
    ni = T // DU_TM
    nc = len(chips)

    def body(*refs):
        a_ref, b_ref, h_ref, g_ref, dres_ref = refs[:5]
        dh_ref, dg_ref = refs[5 + nc:7 + nc]
        acc_ref = refs[7 + 2 * nc]
        i, kk = pl.program_id(0), pl.program_id(1)
        if nc:
            ch_start, ch_finish = _chips_program(refs[5:5 + nc], refs[7 + nc:7 + 2 * nc], refs[8 + 2 * nc:])
            pl.when((i == 0) & (kk == 0))(ch_start)
        part = jnp.dot(a_ref[...], b_ref[...], preferred_element_type=F32)

        @pl.when(kk == 0)
        def _():
            acc_ref[...] = part

        @pl.when((kk > 0) & (kk < nk - 1))
        def _():
            acc_ref[...] += part

        @pl.when(kk == nk - 1)
        def _():
            du_ = acc_ref[...] + part
            x = h_ref[...]
            r = lax.rsqrt(jnp.mean(x * x, axis=-1, keepdims=True) + EPS)
            gd = g_ref[...] * du_
            dx = r * gd - x * (r * r * r) * jnp.mean(x * gd, axis=-1, keepdims=True)
            dh_ref[...] = dx + dres_ref[...]
            gpart = jnp.concatenate([jnp.sum(du_ * x * r, axis=0, keepdims=True), jnp.zeros((7, D_MODEL), F32)],
                                    axis=0)

            @pl.when(i == 0)
            def _():
                dg_ref[...] = gpart

            @pl.when(i > 0)
            def _():
                dg_ref[...] += gpart

        if nc:
            pl.when((i == ni - 1) & (kk == nk - 1))(ch_finish)

    row = pl.BlockSpec((DU_TM, D_MODEL), lambda i, kk: (i, 0))
    return pl.pallas_call(
        body, grid=(ni, nk),
        in_specs=[pl.BlockSpec((DU_TM, DU_TK), lambda i, kk: (i, kk)),
                  pl.BlockSpec((DU_TK, D_MODEL), lambda i, kk: (kk, 0)),
                  row, pl.BlockSpec((1, D_MODEL), lambda i, kk: (0, 0)), row] + [ANY] * nc,
        out_specs=[row, pl.BlockSpec((8, D_MODEL), lambda i, kk: (0, 0))] + [ANY] * nc,
        out_shape=[SDS((T, D_MODEL), F32), SDS((8, D_MODEL), F32)] + [SDS(p.shape, p.dtype) for p in chips],
        scratch_shapes=[pltpu.VMEM((DU_TM, D_MODEL), F32)] + (_chips_scratch(chips) if nc else []),
        compiler_params=_cparams(), name="d_u_norm")(dproj, w_alt, h, g_pre, dres, *chips)


def _lane_pick(row, h):
    lane = lax.broadcasted_iota(jnp.int32, row.shape, 1)
    return jnp.sum(jnp.where(lane == h, row, 0.0), axis=1, keepdims=True)


def _alibi_band():
    r = np.arange(GROUP * BLK)[:, None]
    rel = (r % BLK) - np.arange(2 * BLK)[None, :] + BLK
    out = np.empty((KV_HEADS, GROUP * BLK, 2 * BLK), np.float32)
    for kh in range(KV_HEADS):
        slope = (2.0 ** (-8.0 * (kh * GROUP + r // BLK + 1) / Q_HEADS)).astype(np.float32)
        out[kh] = np.where((rel >= 0) & (rel < BLK), -slope * rel.astype(np.float32), np.float32(NEG))
    return jnp.asarray(out)


def _attn_fn(q4s, kcats, vcats, kms, vms, sinks, n, band):
    s = lax.broadcasted_iota(jnp.int32, (GROUP * BLK, 2 * BLK), 1)
    key_off = jnp.where(n * BLK - BLK + s >= PAD + N_META, 0.0, NEG)
    rm = lax.broadcasted_iota(jnp.int32, (GROUP * BLK, N_META), 0)
    mm = lax.broadcasted_iota(jnp.int32, (GROUP * BLK, N_META), 1)
    meta_ok = (PAD + mm) <= (n * BLK + jnp.bitwise_and(rm, BLK - 1))
    gcol = jnp.right_shift(lax.broadcasted_iota(jnp.int32, (GROUP * BLK, 1), 0), 7)
    outs = []
    for kh in range(KV_HEADS):
        sk = [_lane_pick(sinks, kh * GROUP + g) for g in range(GROUP)]
        sink = jnp.where(gcol == 0, sk[0], jnp.where(gcol == 1, sk[1], jnp.where(gcol == 2, sk[2], sk[3])))
        qb = (q4s[kh] * (HEAD ** -0.5)).astype(BF16)
        sb = lax.dot_general(qb, kcats[kh].astype(BF16), (((1,), (1,)), ((), ())), preferred_element_type=F32)
        sb = sb + (band[kh] + key_off)
        sm = lax.dot_general(qb, kms[kh].astype(BF16), (((1,), (1,)), ((), ())), preferred_element_type=F32)
        sm = jnp.where(meta_ok, sm, NEG)
        mx = jnp.maximum(jnp.maximum(jnp.max(sb, axis=1, keepdims=True), jnp.max(sm, axis=1, keepdims=True)), sink)
        mx = lax.stop_gradient(mx)
        eb = jnp.exp(sb - mx)
        em = jnp.exp(sm - mx)
        es = jnp.exp(sink - mx)
        inv = 1.0 / (jnp.sum(eb, axis=1, keepdims=True) + jnp.sum(em, axis=1, keepdims=True) + es)
        pb = (eb * inv).astype(BF16)
        pm = (em * inv).astype(BF16)
        o4 = (jnp.dot(pm, vms[kh].astype(BF16), preferred_element_type=F32)
              + jnp.dot(pb, vcats[kh].astype(BF16), preferred_element_type=F32))
        outs.append(o4)
    return outs


def _attn_specs():
    prev = lambda n: jnp.maximum(n - 1, 0)
    return [
        pl.BlockSpec((BLK, D_MODEL), lambda n: (n, C_Q // D_MODEL)),
        pl.BlockSpec((BLK, KV_W), lambda n: (prev(n), C_K // KV_W)),
        pl.BlockSpec((BLK, KV_W), lambda n: (n, C_K // KV_W)),
        pl.BlockSpec((BLK, KV_W), lambda n: (prev(n), C_V // KV_W)),
        pl.BlockSpec((BLK, KV_W), lambda n: (n, C_V // KV_W)),
        pl.BlockSpec((N_META, KV_W), lambda n: (PAD // N_META, C_K // KV_W)),
        pl.BlockSpec((N_META, KV_W), lambda n: (PAD // N_META, C_V // KV_W)),
        pl.BlockSpec((1, 128), lambda n: (0, 0)),
        pl.BlockSpec((KV_HEADS, GROUP * BLK, 2 * BLK), lambda n: (0, 0, 0)),
    ]


def _attn_load(q_ref, kp_ref, kc_ref, vp_ref, vc_ref, km_ref, vm_ref):
    q4s, kcats, vcats, kms, vms = [], [], [], [], []
    for kh in range(KV_HEADS):
        q4s.append(jnp.concatenate(
            [q_ref[:, (kh * GROUP + g) * HEAD:(kh * GROUP + g + 1) * HEAD] for g in range(GROUP)], axis=0))
        cs = slice(kh * HEAD, (kh + 1) * HEAD)
        kcats.append(jnp.concatenate([kp_ref[:, cs], kc_ref[:, cs]], axis=0))
        vcats.append(jnp.concatenate([vp_ref[:, cs], vc_ref[:, cs]], axis=0))
        kms.append(km_ref[:, cs])
        vms.append(vm_ref[:, cs])
    return q4s, kcats, vcats, kms, vms


def _attn_fwd(proj, sinks):
    def body(q_ref, kp_ref, kc_ref, vp_ref, vc_ref, km_ref, vm_ref, s_ref, band_ref, o_ref):
        n = pl.program_id(0)
        args = _attn_load(q_ref, kp_ref, kc_ref, vp_ref, vc_ref, km_ref, vm_ref)
        outs = _attn_fn(*args, s_ref[...], n, [band_ref[kh] for kh in range(KV_HEADS)])
        for kh in range(KV_HEADS):
            for g in range(GROUP):
                hh = kh * GROUP + g
                o_ref[:, hh * HEAD:(hh + 1) * HEAD] = outs[kh][g * BLK:(g + 1) * BLK]

    return pl.pallas_call(
        body, grid=(NB,), in_specs=_attn_specs(),
        out_specs=pl.BlockSpec((BLK, D_MODEL), lambda n: (n, 0)),
        out_shape=SDS((T, D_MODEL), F32), name="attn_fwd")(proj, proj, proj, proj, proj, proj, proj, sinks,
                                                            _alibi_band())


def _attn_bwd(proj, sinks, do, dproj):
    def body(q_ref, kp_ref, kc_ref, vp_ref, vc_ref, km_ref, vm_ref, s_ref, band_ref, do_ref, _, dq_ref, dk_ref, dv_ref,
             ds_ref):
        n = pl.program_id(0)
        band = [band_ref[kh] for kh in range(KV_HEADS)]

        @pl.when(n == 0)
        def _():
            dk_ref[...] = jnp.zeros_like(dk_ref)
            dv_ref[...] = jnp.zeros_like(dv_ref)
            ds_ref[...] = jnp.zeros_like(ds_ref)

        args = _attn_load(q_ref, kp_ref, kc_ref, vp_ref, vc_ref, km_ref, vm_ref)
        _, vjp = jax.vjp(lambda a, b, c, d, e, f: _attn_fn(a, b, c, d, e, f, n, band), *args, s_ref[...])
        do_f = do_ref[...].astype(F32)
        cot = [jnp.concatenate([do_f[:, (kh * GROUP + g) * HEAD:(kh * GROUP + g + 1) * HEAD] for g in range(GROUP)],
                               axis=0) for kh in range(KV_HEADS)]
        dq4s, dkcats, dvcats, dkms, dvms, dsk = vjp(cot)
        ds_ref[0:1, :] += dsk
        cur = pl.ds(pl.multiple_of(n * BLK, BLK), BLK)
        meta = slice(PAD, PAD + N_META)
        for kh in range(KV_HEADS):
            cs = slice(kh * HEAD, (kh + 1) * HEAD)
            for g in range(GROUP):
                hh = kh * GROUP + g
                dq_ref[:, hh * HEAD:(hh + 1) * HEAD] = dq4s[kh][g * BLK:(g + 1) * BLK].astype(BF16)
            dk_ref[cur, cs] += dkcats[kh][BLK:]
            dv_ref[cur, cs] += dvcats[kh][BLK:]
            dk_ref[meta, cs] += dkms[kh]
            dv_ref[meta, cs] += dvms[kh]

        @pl.when(n > 0)
        def _():
            prv = pl.ds(pl.multiple_of((n - 1) * BLK, BLK), BLK)
            for kh in range(KV_HEADS):
                cs = slice(kh * HEAD, (kh + 1) * HEAD)
                dk_ref[prv, cs] += dkcats[kh][:BLK]
                dv_ref[prv, cs] += dvcats[kh][:BLK]

    full_kv = pl.BlockSpec((T, KV_W), lambda n: (0, 0))
    return pl.pallas_call(
        body, grid=(NB,),
        in_specs=_attn_specs() + [pl.BlockSpec((BLK, D_MODEL), lambda n: (n, 0)), ANY],
        out_specs=[pl.BlockSpec((BLK, D_MODEL), lambda n: (n, C_Q // D_MODEL)), full_kv, full_kv,
                   pl.BlockSpec((8, 128), lambda n: (0, 0))],
        out_shape=[SDS((T, PW), BF16), SDS((T, KV_W), F32), SDS((T, KV_W), F32), SDS((8, 128), F32)],
        input_output_aliases={10: 0},
        name="attn_bwd")(proj, proj, proj, proj, proj, proj, proj, sinks, _alibi_band(), do, dproj)


def _rows_from(ext, start):
    if start % 8 == 0:
        return ext[start:start + BLK]
    return pltpu.roll(ext, (8 - start) % (BLK + 8), 0)[8:8 + BLK]


def _conv_taps(taps, w):
    return w[0:1] * taps[0] + w[1:2] * taps[1] + w[2:3] * taps[2] + w[3:4] * taps[3]


HPG = SSM_HEADS // SSM_GROUPS


def _iota(shape, dim):
    return lax.broadcasted_iota(jnp.int32, shape, dim)


def _mm(a, b, ca=1, cb=0):
    return lax.dot_general(a.astype(BF16), b.astype(BF16), (((ca,), (cb,)), ((), ())), preferred_element_type=F32)


def _split3(v):
    hi = v.astype(BF16)
    r1 = v - hi.astype(F32)
    mid = r1.astype(BF16)
    lo = (r1 - mid.astype(F32)).astype(BF16)
    return hi, mid, lo


def _split2(v):
    hi = v.astype(BF16)
    return hi, (v - hi.astype(F32)).astype(BF16)


def _sel_r(parts, onehot, ca=1, cb=0):
    out = lax.dot_general(parts[0], onehot, (((ca,), (cb,)), ((), ())), preferred_element_type=F32)
    for p in parts[1:]:
        out = out + lax.dot_general(p, onehot, (((ca,), (cb,)), ((), ())), preferred_element_type=F32)
    return out


def _sel_l(onehot, parts):
    out = jnp.dot(onehot, parts[0], preferred_element_type=F32)
    for p in parts[1:]:
        out = out + jnp.dot(onehot, p, preferred_element_type=F32)
    return out


def _rows8(*rows):
    r = _iota((8, rows[0].shape[1]), 0)
    out = jnp.zeros((8, rows[0].shape[1]), F32)
    for k, v in enumerate(rows):
        out = jnp.where(r == k, v, out)
    return out


def _ssd_consts():
    r, c = np.arange(BLK)[:, None], np.arange(BLK)[None, :]
    tri_l = (c <= r).astype(np.float32)
    q = np.arange(GRP_W)[None, :]
    spread = np.stack([(r == g * HPG + q // HEAD) for g in range(SSM_GROUPS)]).astype(np.float32)
    pick = np.stack([np.stack([(r == g * HPG + c - k * HPG) & (c >= k * HPG) & (c < (k + 1) * HPG) for k in range(3)])
                     for g in range(SSM_GROUPS)]).astype(np.float32)
    causal = np.where(r >= (np.arange(HPG * BLK)[None, :] % BLK), 0.0, NEG).astype(np.float32)
    bf = lambda a: jnp.asarray(a, BF16)
    return dict(tri_l=bf(tri_l), tri_u=bf(tri_l.T), spread=bf(spread), unspread=bf(spread.transpose(0, 2, 1)),
                pick=bf(pick), causal=jnp.asarray(causal))


SSD_CONSTS = ("tri_l", "tri_u", "spread", "unspread", "pick", "causal")


def _ssd_forward(x, z, bm, cm, dt_raw, st_prev, dtb, alog, dskip, gn, g, cst_scr, cn):
    dt_all = jax.nn.softplus(dt_raw + dtb)
    a_row = -jnp.exp(alog)
    a_all = dt_all * a_row
    cs_all = _sel_l(cn["tri_l"][...], _split3(a_all))
    cs_parts = _split3(cs_all)
    spread = cn["spread"][g]
    dt_e = _sel_r(_split2(dt_all), spread)
    cs_e = _sel_r(cs_parts, spread)
    d_e = _sel_r(_split2(_rows8(dskip)), spread)[0:1]
    cs_last_e = jnp.sum(jnp.where(_iota((BLK, GRP_W), 0) == BLK - 1, cs_e, 0.0), axis=0, keepdims=True)
    p_e = jnp.exp(cs_e)
    w_e = jnp.exp(cs_last_e - cs_e)
    cd_e = jnp.exp(cs_last_e)
    xr = x * dt_e
    cst_scr[...] = cs_all.T
    cst_g = cst_scr[g * HPG:(g + 1) * HPG, :]
    own = jnp.right_shift(_iota((HPG, HPG * BLK), 1), 7) == _iota((HPG, HPG * BLK), 0)
    ownf = own.astype(F32)
    q_rows = [ownf, ownf, ownf] + [jnp.where(own, jnp.concatenate([p.astype(F32)] * HPG, axis=1), 0.0)
                                   for p in _split3(cst_g)]
    q2 = jnp.concatenate(q_rows + [jnp.zeros((BLK - 6 * HPG, HPG * BLK), F32)], axis=0).astype(BF16)
    lane1 = _iota((1, BLK), 1)
    p2 = jnp.where((lane1 >= 3 * HPG) & (lane1 < 6 * HPG), -1.0, 0.0)
    for k, part in enumerate(cs_parts):
        p2 = p2 + jnp.dot(part, cn["pick"][g, k], preferred_element_type=F32)
    dmat = jnp.dot(p2.astype(BF16), q2, preferred_element_type=F32)
    lam = jnp.exp(dmat + cn["causal"][...])
    gmat = _mm(cm, bm, 1, 1)
    m_all = lam * jnp.concatenate([gmat] * HPG, axis=1)
    mb = m_all.astype(BF16)
    lo = _iota((BLK, BLK), 1) < HEAD
    xrb = xr.astype(BF16)
    zero = jnp.zeros((BLK, BLK), BF16)
    bds, yd = [], []
    for i in range(HPG // 2):
        t = xrb[:, BLK * i:BLK * (i + 1)]
        bd = jnp.concatenate([jnp.where(lo, t, zero), jnp.where(lo, zero, t)], axis=0)
        bds.append(bd)
        yd.append(jnp.dot(mb[:, 2 * BLK * i:2 * BLK * (i + 1)], bd, preferred_element_type=F32))
    cs_st = _mm(cm, st_prev)
    y = jnp.concatenate(yd, axis=1) + cs_st * p_e + d_e * x
    xrw = xr * w_e
    st_new = cd_e * st_prev + _mm(bm, xrw, 0, 0)
    yz = y * _silu(z)
    rn = lax.rsqrt(jnp.sum(yz * yz, axis=1, keepdims=True) / GRP_W + EPS)
    return dict(out=yz * rn * gn, st_new=st_new, dt_all=dt_all, a_row=a_row, dt_e=dt_e, d_e=d_e, p_e=p_e, w_e=w_e,
                cd_e=cd_e, xr=xr, xrw=xrw, lam=lam, m_all=m_all, mb=mb, bds=bds, cs_st=cs_st, y=y, yz=yz, rn=rn, lo=lo)


def _ssd_backward(f, x, z, bm, cm, dt_raw, st_prev, dtb, gn, g, dout, dst_next, cst_scr, cn):
    li, si = _iota((BLK, BLK), 0), _iota((BLK, BLK), 1)
    yz, rn, y, p_e, w_e, cd_e, xr = f["yz"], f["rn"], f["y"], f["p_e"], f["w_e"], f["cd_e"], f["xr"]
    dgn = jnp.sum(dout * yz * rn, axis=0, keepdims=True)
    t = dout * gn
    dyz = rn * t - yz * (rn * rn * rn) * (jnp.sum(yz * t, axis=1, keepdims=True) / GRP_W)
    dy = dyz * _silu(z)
    dz = dyz * y * _dsilu(z)
    dx = f["d_e"] * dy
    dd_e = jnp.sum(dy * x, axis=0, keepdims=True)
    dcsst = dy * p_e
    dp_e = dy * f["cs_st"]
    dcm = _mm(dcsst, st_prev, 1, 1)
    dst_prev = _mm(cm, dcsst, 0, 0) + cd_e * dst_next
    dcd_e = jnp.sum(dst_next * st_prev, axis=0, keepdims=True)
    dbm = _mm(f["xrw"], dst_next, 1, 1)
    dxrw = _mm(bm, dst_next)
    dxr = dxrw * w_e
    dw_e = dxrw * xr
    dyb = dy.astype(BF16)
    dms, dxr_d = [], []
    for i in range(HPG // 2):
        dyp = dyb[:, BLK * i:BLK * (i + 1)]
        dms.append(lax.dot_general(dyp, f["bds"][i], (((1,), (1,)), ((), ())), preferred_element_type=F32))
        r = lax.dot_general(f["mb"][:, 2 * BLK * i:2 * BLK * (i + 1)], dyp, (((0,), (0,)), ((), ())),
                            preferred_element_type=F32)
        dxr_d.append(jnp.where(f["lo"], r[0:BLK], r[BLK:2 * BLK]))
    dm_all = jnp.concatenate(dms, axis=1)
    dxr = dxr + jnp.concatenate(dxr_d, axis=1)
    dlg = dm_all * f["lam"]
    dg = dlg[:, 0:BLK]
    for j in range(1, HPG):
        dg = dg + dlg[:, BLK * j:BLK * (j + 1)]
    dcm = dcm + _mm(dg, bm)
    dbm = dbm + _mm(dg, cm, 0, 0)
    q_all = dm_all * f["m_all"]
    col_sums = jnp.sum(q_all, axis=0, keepdims=True)
    cst_scr[...] = jnp.zeros_like(cst_scr)
    cst_scr[g * HPG:(g + 1) * HPG, :] = _rows8(
        *[col_sums[:, BLK * j:BLK * (j + 1)] for j in range(HPG)])
    dcs = -cst_scr[...].T
    for j in range(HPG):
        dcs = dcs + jnp.where(si == g * HPG + j,
                              jnp.sum(q_all[:, BLK * j:BLK * (j + 1)], axis=1, keepdims=True), 0.0)
    unspread = cn["unspread"][g]
    dww = dw_e * w_e
    per_head = _sel_r(_split2(jnp.concatenate([dp_e * p_e - dww, dxr * x], axis=0)), unspread)
    last = _sel_r(_split2(_rows8(jnp.sum(dww, axis=0, keepdims=True) + dcd_e * cd_e, dd_e)), unspread)
    dcs = dcs + per_head[0:BLK] + jnp.where(li == BLK - 1, last[0:1], 0.0)
    da = _sel_l(cn["tri_u"][...], _split2(dcs))
    ddt_all = da * f["a_row"] + per_head[BLK:2 * BLK]
    dalog = jnp.sum(da * f["dt_all"], axis=0, keepdims=True) * f["a_row"]
    dx = dx + dxr * f["dt_e"]
    ddt_raw = ddt_all * jax.nn.sigmoid(dt_raw + dtb)
    ddtb = jnp.sum(ddt_raw, axis=0, keepdims=True)
    ddskip = last[1:2]
    return dict(dx=dx, dz=dz, dbm=dbm, dcm=dcm, ddt_raw=ddt_raw, dst_prev=dst_prev, ddtb=ddtb, dalog=dalog,
                ddskip=ddskip, dgn=dgn)


ZX_W = SSM_INNER + CONV_DIM
assert C_ZS == 0 and C_XBC == SSM_INNER


def _ssd_in_specs(rev):
    cidx = (lambda c: NB - 1 - c) if rev else (lambda c: c)
    return [
        pl.BlockSpec((BLK, ZX_W), lambda c: (cidx(c), 0)),
        pl.BlockSpec((8, ZX_W), lambda c: (jnp.maximum(cidx(c) * (BLK // 8) - 1, 0), 0)),
        pl.BlockSpec((BLK, 128), lambda c: (cidx(c), C_DT // 128)),
        pl.BlockSpec((8, CONV_DIM), lambda c: (0, 0)),
        pl.BlockSpec((1, CONV_DIM), lambda c: (0, 0)),
        pl.BlockSpec((1, 128), lambda c: (0, 0)),
        pl.BlockSpec((1, 128), lambda c: (0, 0)),
        pl.BlockSpec((1, 128), lambda c: (0, 0)),
        pl.BlockSpec((1, SSM_INNER), lambda c: (0, 0)),
        pl.BlockSpec((BLK, BLK), lambda c: (0, 0)),
        pl.BlockSpec((BLK, BLK), lambda c: (0, 0)),
        pl.BlockSpec((SSM_GROUPS, BLK, GRP_W), lambda c: (0, 0, 0)),
        pl.BlockSpec((SSM_GROUPS, GRP_W, BLK), lambda c: (0, 0, 0)),
        pl.BlockSpec((SSM_GROUPS, 3, BLK, BLK), lambda c: (0, 0, 0, 0)),
        pl.BlockSpec((BLK, HPG * BLK), lambda c: (0, 0)),
    ]


N_SSD_IN = 15


def _xbc_act(zx_ref, tail_ref, w_ref, b_ref, n):
    tail = jnp.where(n > 0, tail_ref[:, SSM_INNER:], 0.0)
    xp = jnp.concatenate([tail, zx_ref[:, SSM_INNER:]], axis=0)
    taps = [_rows_from(xp, 5 + k) for k in range(4)]
    conv = _conv_taps(taps, w_ref[...]) + b_ref[...]
    valid = n * BLK + _iota((BLK, 1), 0) >= PAD
    return taps, conv, valid, jnp.where(valid, _silu(conv), 0.0)


def _grp_cols(act, i):
    b0, c0 = SSM_INNER + i * SSM_STATE, SSM_INNER + (SSM_GROUPS + i) * SSM_STATE
    return act[:, i * GRP_W:(i + 1) * GRP_W], act[:, b0:b0 + SSM_STATE], act[:, c0:c0 + SSM_STATE]


def _ssd_fwd(proj, conv_w, conv_b, dt_bias, a_log, d_skip, g_norm, gather=()):
    ng = len(gather)

    def body(*refs):
        zx_ref, tail_ref, dt_ref, w_ref, b_ref, dtb_ref, al_ref, dsk_ref, gn_ref = refs[:9]
        cn = dict(zip(SSD_CONSTS, refs[9:N_SSD_IN]))
        k0 = N_SSD_IN
        y_ref, st_ref = refs[k0 + ng:k0 + 2 + ng]
        s_scr, cst_scr = refs[k0 + 2 + 2 * ng:k0 + 4 + 2 * ng]
        c = pl.program_id(0)
        if ng:
            ag_start, ag_forward, ag_finish = _ag_program(refs[k0:k0 + ng], refs[k0 + 2 + ng:k0 + 2 + 2 * ng],
                                                          refs[k0 + 4 + 2 * ng:])
            pl.when(c == 0)(ag_start)
            pl.when(c == (3 * NB) // 4)(ag_forward)

        @pl.when(c == 0)
        def _():
            s_scr[...] = jnp.zeros_like(s_scr)

        _, _, _, act = _xbc_act(zx_ref, tail_ref, w_ref, b_ref, c)
        for i in range(SSM_GROUPS):
            st_prev = s_scr[i]
            st_ref[i, 0] = st_prev
            x, bm, cm = _grp_cols(act, i)
            f = _ssd_forward(x, zx_ref[:, i * GRP_W:(i + 1) * GRP_W], bm, cm, dt_ref[...], st_prev, dtb_ref[...],
                             al_ref[...], dsk_ref[...], gn_ref[:, i * GRP_W:(i + 1) * GRP_W], i, cst_scr.at[i], cn)
            y_ref[:, i * GRP_W:(i + 1) * GRP_W] = f["out"].astype(BF16)
            s_scr[i] = f["st_new"]
        if ng:
            pl.when(c == NB - 1)(ag_finish)

    return pl.pallas_call(
        body, grid=(NB,), in_specs=_ssd_in_specs(False) + [ANY] * ng,
        out_specs=[pl.BlockSpec((BLK, SSM_INNER), lambda c: (c, 0)),
                   pl.BlockSpec((SSM_GROUPS, 1, SSM_STATE, GRP_W), lambda c: (0, c, 0, 0))] + [ANY] * ng,
        out_shape=[SDS((T, SSM_INNER), BF16), SDS((SSM_GROUPS, NB, SSM_STATE, GRP_W), F32)]
        + [SDS((N_DEV,) + s.shape, s.dtype) for s in gather],
        scratch_shapes=[pltpu.VMEM((SSM_GROUPS, SSM_STATE, GRP_W), F32), pltpu.VMEM((SSM_GROUPS, BLK, BLK), F32)]
        + (_ag_scratch(gather) if ng else []),
        compiler_params=_cparams(),
        name="ssd_fwd")(proj, proj, proj, conv_w, conv_b, dt_bias, a_log, d_skip, g_norm,
                        *[_ssd_consts()[k] for k in SSD_CONSTS], *gather)


def _ssd_bwd(proj, conv_w, conv_b, dt_bias, a_log, d_skip, g_norm, states, dy, dproj, exchange=()):
    ne = len(exchange)

    def body(*refs):
        zx_ref, tail_ref, dt_ref, w_ref, b_ref, dtb_ref, al_ref, dsk_ref, gn_ref = refs[:9]
        cn = dict(zip(SSD_CONSTS, refs[9:N_SSD_IN]))
        st_ref, dy_ref = refs[N_SSD_IN:N_SSD_IN + 2]
        k0 = N_SSD_IN + 3
        (ddt_ref, dp_ref, ddtb_ref, dal_ref, ddsk_ref, dgn_ref, dcw_ref, dcb_ref) = refs[k0 + ne:k0 + 8 + ne]
        ds_scr, cst_scr, carry = refs[k0 + 8 + 2 * ne:k0 + 11 + 2 * ne]
        c = pl.program_id(0)
        n = NB - 1 - c
        if ne:
            ex_start, ex_finish = _direct_program(refs[k0:k0 + ne], refs[k0 + 8 + ne:k0 + 8 + 2 * ne],
                                                  refs[k0 + 11 + 2 * ne:])
            pl.when(c == 0)(ex_start)

        @pl.when(c == 0)
        def _():
            for ref in (ds_scr, carry, dgn_ref, ddtb_ref, dal_ref, ddsk_ref, dcw_ref, dcb_ref):
                ref[...] = jnp.zeros_like(ref)

        taps, conv, valid, act = _xbc_act(zx_ref, tail_ref, w_ref, b_ref, n)
        dt_raw = dt_ref[...]
        dxs, dbs, dcs = [], [], []
        for i in range(SSM_GROUPS):
            x, bm, cm = _grp_cols(act, i)
            z, gn, st_prev = zx_ref[:, i * GRP_W:(i + 1) * GRP_W], gn_ref[:, i * GRP_W:(i + 1) * GRP_W], st_ref[i, 0]
            f = _ssd_forward(x, z, bm, cm, dt_raw, st_prev, dtb_ref[...], al_ref[...], dsk_ref[...], gn, i,
                             cst_scr.at[i], cn)
            d = _ssd_backward(f, x, z, bm, cm, dt_raw, st_prev, dtb_ref[...], gn, i,
                              dy_ref[:, i * GRP_W:(i + 1) * GRP_W].astype(F32), ds_scr[i], cst_scr.at[i], cn)
            dxs.append(d["dx"])
            dbs.append(d["dbm"])
            dcs.append(d["dcm"])
            dp_ref[:, i * GRP_W:(i + 1) * GRP_W] = d["dz"].astype(BF16)
            ds_scr[i] = d["dst_prev"]
            ddt_ref[:, i * 128:(i + 1) * 128] = d["ddt_raw"]
            dgn_ref[0:1, i * GRP_W:(i + 1) * GRP_W] += d["dgn"]
            ddtb_ref[0:1, :] += d["ddtb"]
            dal_ref[0:1, :] += d["dalog"]
            ddsk_ref[0:1, :] += d["ddskip"]
        dconv = jnp.where(valid, jnp.concatenate(dxs + dbs + dcs, axis=1) * _dsilu(conv), 0.0)
        dext = jnp.concatenate([dconv, carry[...]], axis=0)
        dp_ref[:, SSM_INNER:] = _conv_taps([_rows_from(dext, 3 - k) for k in range(4)], w_ref[...]).astype(BF16)
        carry[...] = dconv[0:8]
        dcw_ref[...] += jnp.concatenate(
            [jnp.sum(dconv * taps[k], axis=0, keepdims=True) for k in range(4)]
            + [jnp.zeros((4, CONV_DIM), F32)], axis=0)
        dcb_ref[0:1, :] += jnp.sum(dconv, axis=0, keepdims=True)
        if ne:
            pl.when(c == NB - 1)(ex_finish)

    rc = lambda c: NB - 1 - c
    small = pl.BlockSpec((8, 128), lambda c: (0, 0))
    wide = lambda w: pl.BlockSpec((8, w), lambda c: (0, 0))
    return pl.pallas_call(
        body, grid=(NB,),
        in_specs=_ssd_in_specs(True) + [
            pl.BlockSpec((SSM_GROUPS, 1, SSM_STATE, GRP_W), lambda c: (0, rc(c), 0, 0)),
            pl.BlockSpec((BLK, SSM_INNER), lambda c: (rc(c), 0)), ANY] + [ANY] * ne,
        out_specs=[pl.BlockSpec((BLK, SSM_GROUPS * 128), lambda c: (rc(c), 0)),
                   pl.BlockSpec((BLK, ZX_W), lambda c: (rc(c), 0)),
                   small, small, small, wide(SSM_INNER), wide(CONV_DIM), wide(CONV_DIM)] + [ANY] * ne,
        out_shape=[SDS((T, GRP_W), F32), SDS((T, PW), BF16), SDS((8, 128), F32), SDS((8, 128), F32),
                   SDS((8, 128), F32), SDS((8, SSM_INNER), F32), SDS((8, CONV_DIM), F32), SDS((8, CONV_DIM), F32)]
        + [SDS(p.shape, p.dtype) for p in exchange],
        scratch_shapes=[pltpu.VMEM((SSM_GROUPS, SSM_STATE, GRP_W), F32), pltpu.VMEM((SSM_GROUPS, BLK, BLK), F32),
                        pltpu.VMEM((8, CONV_DIM), F32)] + (_direct_scratch(exchange) if ne else []),
        input_output_aliases={N_SSD_IN + 2: 1},
        compiler_params=_cparams(),
        name="ssd_bwd")(proj, proj, proj, conv_w, conv_b, dt_bias, a_log, d_skip, g_norm,
                        *[_ssd_consts()[k] for k in SSD_CONSTS], states, dy, dproj, *exchange)


POST_R = 272


def _post_a(o, proj, sn, w_att, w_ssm, w_o):
    def body(o_ref, za_ref, ga_ref, gs_ref, sn_ref, wa_ref, ws_ref, wo_ref, a_ref, mg_ref, ya_ref, ys_ref, out_ref):
        a = (o_ref[...] * _silu(za_ref[...])).astype(BF16)
        a_ref[...] = a
        ya = jnp.dot(a, wa_ref[...], preferred_element_type=F32)
        ys = jnp.dot(sn_ref[...], ws_ref[...], preferred_element_type=F32)
        ya_ref[...] = ya.astype(BF16)
        ys_ref[...] = ys.astype(BF16)
        mg = (jax.nn.sigmoid(ga_ref[...]) * ya + jax.nn.sigmoid(gs_ref[...]) * ys).astype(BF16)
        mg_ref[...] = mg
        out_ref[...] = jnp.dot(mg, wo_ref[...], preferred_element_type=F32)

    row = pl.BlockSpec((POST_R, D_MODEL), lambda i: (i, 0))
    pcol = lambda c0: pl.BlockSpec((POST_R, D_MODEL), lambda i: (i, c0 // D_MODEL))
    full = lambda r: pl.BlockSpec((r, D_MODEL), lambda i: (0, 0))
    return pl.pallas_call(
        body, grid=(T // POST_R,),
        in_specs=[row, pcol(C_ZA), pcol(C_GA), pcol(C_GS), pl.BlockSpec((POST_R, SSM_INNER), lambda i: (i, 0)),
                  full(D_MODEL), full(SSM_INNER), full(D_MODEL)],
        out_specs=[row, row, row, row, row],
        out_shape=[SDS((T, D_MODEL), BF16), SDS((T, D_MODEL), BF16), SDS((T, D_MODEL), BF16), SDS((T, D_MODEL), BF16),
                   SDS((T, D_MODEL), F32)],
        compiler_params=_cparams(), name="post_a")(o, proj, proj, proj, sn, w_att, w_ssm, w_o)


def _post_b(out, h, tgt, proj, ya, ys, o, g_post, w_att, w_ssm, w_o):
    def body(out_ref, h_ref, t_ref, za_ref, ga_ref, gs_ref, ya_ref, ys_ref, o_ref, gp_ref, wa_ref, ws_ref, wo_ref,
             loss_ref, dres_ref, dout_ref, dya_ref, dys_ref, do_ref, dp_ref, dsn_ref, dgp_ref):
        i = pl.program_id(0)
        x = out_ref[...]
        gp = gp_ref[...]
        r = lax.rsqrt(jnp.mean(x * x, axis=-1, keepdims=True) + EPS)
        row = i * POST_R + lax.broadcasted_iota(jnp.int32, (POST_R, 1), 0)
        res = h_ref[...] + jnp.where(row >= PAD, x * r * gp, 0.0)
        live = row >= PAD + N_META
        err = jnp.where(live, res - t_ref[...], 0.0)
        lpart = 0.5 * jnp.sum(jnp.sum(err * err, axis=1, keepdims=True) / D_MODEL, axis=0, keepdims=True)
        dres = err / D_MODEL
        dres_ref[...] = dres
        gpart = jnp.sum(dres * x * r, axis=0, keepdims=True)

        @pl.when(i == 0)
        def _():
            loss_ref[...] = jnp.zeros_like(loss_ref)
            dgp_ref[...] = jnp.zeros_like(dgp_ref)

        loss_ref[...] += jnp.broadcast_to(lpart, loss_ref.shape)
        dgp_ref[0:1, :] += gpart
        gd = gp * dres
        dout = (r * gd - x * (r * r * r) * jnp.mean(x * gd, axis=-1, keepdims=True)).astype(BF16)
        dout_ref[...] = dout
        dmg = lax.dot_general(dout, wo_ref[...], (((1,), (1,)), ((), ())), preferred_element_type=F32)
        sga = jax.nn.sigmoid(ga_ref[...])
        sgs = jax.nn.sigmoid(gs_ref[...])
        dya = (dmg * sga).astype(BF16)
        dys = (dmg * sgs).astype(BF16)
        dya_ref[...] = dya
        dys_ref[...] = dys
        dp_ref[:, C_GA - C_ZA:C_GA - C_ZA + D_MODEL] = (dmg * ya_ref[...].astype(F32) * sga * (1.0 - sga)).astype(BF16)
        dp_ref[:, C_GS - C_ZA:C_GS - C_ZA + D_MODEL] = (dmg * ys_ref[...].astype(F32) * sgs * (1.0 - sgs)).astype(BF16)
        da = lax.dot_general(dya, wa_ref[...], (((1,), (1,)), ((), ())), preferred_element_type=F32)
        za = za_ref[...]
        do_ref[...] = (da * _silu(za)).astype(BF16)
        dp_ref[:, 0:D_MODEL] = (da * o_ref[...] * _dsilu(za)).astype(BF16)
        dsn_ref[...] = lax.dot_general(dys, ws_ref[...], (((1,), (1,)), ((), ())),
                                       preferred_element_type=F32).astype(BF16)

    row = pl.BlockSpec((POST_R, D_MODEL), lambda i: (i, 0))
    pcol = lambda c0: pl.BlockSpec((POST_R, D_MODEL), lambda i: (i, c0 // D_MODEL))
    full = lambda r: pl.BlockSpec((r, D_MODEL), lambda i: (0, 0))
    small = pl.BlockSpec((8, D_MODEL), lambda i: (0, 0))
    return pl.pallas_call(
        body, grid=(T // POST_R,),
        in_specs=[row, row, row, pcol(C_ZA), pcol(C_GA), pcol(C_GS), row, row, row,
                  pl.BlockSpec((1, D_MODEL), lambda i: (0, 0)), full(D_MODEL), full(SSM_INNER), full(D_MODEL)],
        out_specs=[pl.BlockSpec((8, 128), lambda i: (0, 0)), row, row, row, row, row,
                   pl.BlockSpec((POST_R, GATES_W), lambda i: (i, C_ZA // GATES_W)),
                   pl.BlockSpec((POST_R, SSM_INNER), lambda i: (i, 0)), small],
        out_shape=[SDS((8, 128), F32), SDS((T, D_MODEL), F32), SDS((T, D_MODEL), BF16), SDS((T, D_MODEL), BF16),
                   SDS((T, D_MODEL), BF16), SDS((T, D_MODEL), BF16), SDS((T, PW), BF16),
                   SDS((T, SSM_INNER), BF16), SDS((8, D_MODEL), F32)],
        compiler_params=_cparams(), name="post_b")(out, h, tgt, proj, proj, proj, ya, ys, o, g_post, w_att, w_ssm, w_o)


TAIL_W = PW - C_K


def _dproj_tail(dproj, dk, dv, ddt4):
    rows = T // 4

    def body(_, dk_ref, dv_ref, ddt_ref, o_ref, buf, sem):
        n = pl.program_id(0)
        d4 = ddt_ref[...]
        buf[:, 0:KV_W] = dk_ref[...].astype(BF16)
        buf[:, KV_W:2 * KV_W] = dv_ref[...].astype(BF16)
        buf[:, 2 * KV_W:TAIL_W] = (d4[:, 0:128] + d4[:, 128:256] + d4[:, 256:384] + d4[:, 384:512]).astype(BF16)
        cp = pltpu.make_async_copy(buf, o_ref.at[pl.ds(pl.multiple_of(n * rows, 16), rows), pl.ds(C_K, TAIL_W)], sem)
        cp.start()
        cp.wait()

    spec = lambda w: pl.BlockSpec((rows, w), lambda i: (i, 0))
    return pl.pallas_call(
        body, grid=(T // rows,), in_specs=[ANY, spec(KV_W), spec(KV_W), spec(GRP_W)], out_specs=ANY,
        out_shape=SDS((T, PW), BF16), input_output_aliases={0: 0},
        scratch_shapes=[pltpu.VMEM((rows, TAIL_W), BF16), pltpu.SemaphoreType.DMA],
        name="dproj_tail")(dproj, dk, dv, ddt4)


def _adamw_math(w, g, m, v):
    m = ADAM_B1 * m + (1.0 - ADAM_B1) * g
    v = ADAM_B2 * v + (1.0 - ADAM_B2) * (g * g)
    m_hat = m / (1.0 - ADAM_B1 ** ADAM_STEP)
    v_hat = v / (1.0 - ADAM_B2 ** ADAM_STEP)
    delta = -ADAM_LR * (m_hat / (jnp.sqrt(v_hat) + ADAM_EPS) + ADAM_WD * w)
    return delta, m, v


def _sum_adamw(recv, w, m, v, tc, name):
    rows, cols = w.shape
    nslab = recv.shape[0]
    assert cols % tc == 0

    def body(r_ref, w_ref, m_ref, v_ref, g_ref, d_ref, nm_ref, nv_ref):
        g = r_ref[0].astype(F32)
        for d in range(1, nslab):
            g = g + r_ref[d].astype(F32)
        g_ref[...] = g
        delta, nm, nv = _adamw_math(w_ref[...], g, m_ref[...], v_ref[...])
        d_ref[...] = delta
        nm_ref[...] = nm
        nv_ref[...] = nv

    blk = pl.BlockSpec((rows, tc), lambda i: (0, i))
    return pl.pallas_call(
        body, grid=(cols // tc,),
        in_specs=[pl.BlockSpec((nslab, rows, tc), lambda i: (0, 0, i)), blk, blk, blk],
        out_specs=[blk, blk, blk, blk], out_shape=[SDS((rows, cols), F32)] * 4,
        compiler_params=_cparams(), name=name)(recv, w, m, v)


def _sum_adamw_rows3(recv, w3, m3, v3, name, exchange=()):
    pairs = 61
    assert (SHARD_IN // 2) % pairs == 0
    nsteps = SHARD_IN // 2 // pairs
    ne = len(exchange)

    def body(*refs):
        r_ref, w_ref, m_ref, v_ref = refs[:4]
        g_ref, d_ref, nm_ref, nv_ref = refs[4 + ne:8 + ne]
        if ne:
            ex_start, ex_finish = _direct_program(refs[4:4 + ne], refs[8 + ne:8 + 2 * ne], refs[8 + 2 * ne:])
            pl.when(pl.program_id(0) == 0)(ex_start)
        g = r_ref[0].astype(F32)
        for d in range(1, N_CHIP):
            g = g + r_ref[d].astype(F32)
        g = g.reshape(2 * pairs, ROW_TILES, 128)
        g_ref[...] = g
        delta, nm, nv = _adamw_math(w_ref[...], g, m_ref[...], v_ref[...])
        d_ref[...] = delta
        nm_ref[...] = nm
        nv_ref[...] = nv
        if ne:
            pl.when(pl.program_id(0) == nsteps - 1)(ex_finish)

    blk = pl.BlockSpec((2 * pairs, ROW_TILES, 128), lambda i: (i, 0, 0))
    return pl.pallas_call(
        body, grid=(nsteps,),
        in_specs=[pl.BlockSpec((N_CHIP, pairs, 2 * ROW_TILES, 128), lambda i: (0, i, 0, 0)), blk, blk, blk]
        + [ANY] * ne,
        out_specs=[blk, blk, blk, blk] + [ANY] * ne,
        out_shape=[SDS(w3.shape, F32)] * 4 + [SDS(p.shape, p.dtype) for p in exchange],
        scratch_shapes=_direct_scratch(exchange) if ne else [],
        compiler_params=_cparams(), name=name)(recv, w3, m3, v3, *exchange)


ROW_GPRE, ROW_CONVB, ROW_DTB, ROW_ALOG, ROW_DSKIP, ROW_SINK, ROW_GSSM, ROW_GPOST = 0, 1, 4, 5, 6, 7, 8, 10
ROW_LOSS = 11
REP_ROWS, ROW_CONVW, ROW_META, SM_ROWS = 16, 16, 24, 40
CW_SHARD = CONV_DIM // N_DEV
META_SHARD = D_MODEL // N_DEV


def _small_pack(dgpre, db, ddtb, dal, ddsk, dsink, dgn, dgp, dw, loss, dh):
    def body(dgpre_ref, db_ref, ddtb_ref, dal_ref, ddsk_ref, dsink_ref, dgn_ref, dgp_ref, dw_ref, loss_ref, dh_ref,
             o_ref, rep):
        rep[...] = jnp.zeros_like(rep)
        rep[ROW_LOSS:ROW_LOSS + 1, 0:128] = loss_ref[0:1, :]
        rep[ROW_GPRE:ROW_GPRE + 1, :] = dgpre_ref[0:1, :]
        for k in range(3):
            rep[ROW_CONVB + k:ROW_CONVB + k + 1, :] = db_ref[0:1, 1024 * k:1024 * (k + 1)]
        rep[ROW_DTB:ROW_DTB + 1, 0:128] = ddtb_ref[0:1, :]
        rep[ROW_ALOG:ROW_ALOG + 1, 0:128] = dal_ref[0:1, :]
        rep[ROW_DSKIP:ROW_DSKIP + 1, 0:128] = ddsk_ref[0:1, :]
        rep[ROW_SINK:ROW_SINK + 1, 0:128] = dsink_ref[0:1, :]
        rep[ROW_GSSM:ROW_GSSM + 1, :] = dgn_ref[0:1, 0:1024]
        rep[ROW_GSSM + 1:ROW_GSSM + 2, :] = dgn_ref[0:1, 1024:2048]
        rep[ROW_GPOST:ROW_GPOST + 1, :] = dgp_ref[0:1, :]
        cw = dw_ref[...]
        mh = dh_ref[...]
        o_ref[...] = jnp.zeros_like(o_ref)
        for p in range(N_DEV):
            o_ref[p, 0:REP_ROWS, :] = rep[...]
            o_ref[p, ROW_CONVW:ROW_CONVW + 8, 0:CW_SHARD] = cw[:, p * CW_SHARD:(p + 1) * CW_SHARD]
            o_ref[p, ROW_META:ROW_META + N_META, 0:META_SHARD] = mh[:, p * META_SHARD:(p + 1) * META_SHARD]

    ins = [dgpre, db, ddtb, dal, ddsk, dsink, dgn, dgp, dw, loss]
    return pl.pallas_call(
        body, grid=(1,),
        in_specs=[pl.BlockSpec(a.shape, lambda i: (0, 0)) for a in ins]
        + [pl.BlockSpec((N_META, D_MODEL), lambda i: (PAD // N_META, 0))],
        out_specs=pl.BlockSpec((N_DEV, SM_ROWS, 1024), lambda i: (0, 0, 0)),
        out_shape=SDS((N_DEV, SM_ROWS, 1024), F32), scratch_shapes=[pltpu.VMEM((REP_ROWS, 1024), F32)],
        name="small_pack")(*ins, dh)


def _small_finish(recv, params):
    npar = len(params)

    def body(*refs):
        r_ref = refs[0]
        wmv = refs[1:1 + 3 * npar]
        outs = refs[1 + 3 * npar:1 + 7 * npar]
        loss_ref = refs[1 + 7 * npar]
        gs = refs[-1]
        g = r_ref[0]
        for d in range(1, recv.shape[0]):
            g = g + r_ref[d]
        gs[...] = g
        loss_ref[...] = gs[ROW_LOSS:ROW_LOSS + 1, 0:128]
        grads = [
            gs[ROW_GPRE:ROW_GPRE + 1, :],
            jnp.concatenate([gs[ROW_CONVB + k:ROW_CONVB + k + 1, :] for k in range(3)], axis=1),
            gs[ROW_DTB:ROW_DTB + 1, 0:SSM_HEADS], gs[ROW_ALOG:ROW_ALOG + 1, 0:SSM_HEADS],
            gs[ROW_DSKIP:ROW_DSKIP + 1, 0:SSM_HEADS], gs[ROW_SINK:ROW_SINK + 1, 0:Q_HEADS],
            jnp.concatenate([gs[ROW_GSSM:ROW_GSSM + 1, :], gs[ROW_GSSM + 1:ROW_GSSM + 2, :]], axis=1),
            gs[ROW_GPOST:ROW_GPOST + 1, :],
            gs[ROW_CONVW:ROW_CONVW + 4, 0:CW_SHARD],
            gs[ROW_META:ROW_META + N_META, 0:META_SHARD]]
        for i in range(npar):
            w_ref, m_ref, v_ref = wmv[3 * i:3 * i + 3]
            delta, nm, nv = _adamw_math(w_ref[...], grads[i], m_ref[...], v_ref[...])
            outs[4 * i][...] = grads[i]
            outs[4 * i + 1][...] = delta
            outs[4 * i + 2][...] = nm
            outs[4 * i + 3][...] = nv

    flat = [a for wmv in params for a in wmv]
    res = pl.pallas_call(
        body, out_shape=[SDS(wmv[0].shape, F32) for wmv in params for _ in range(4)] + [SDS((1, 128), F32)],
        scratch_shapes=[pltpu.VMEM((SM_ROWS, 1024), F32)], name="small_finish")(recv, *flat)
    return [tuple(res[4 * i:4 * i + 4]) for i in range(npar)], res[4 * npar]


def _slab(ref, px, py, pc):
    return ref.at[4 * px + 2 * py + pc]


def _bounce(src, dst, buf, sem):
    cp = pltpu.make_async_copy(src, buf, sem)
    cp.start()
    cp.wait()
    cp = pltpu.make_async_copy(buf, dst, sem)
    cp.start()
    cp.wait()


def _ag_program(ins, outs, scratch):
    na = len(ins)
    send_sems, recv_sems, local_sems = scratch[:3]
    bufs = scratch[3:]
    x, y, c = lax.axis_index("x"), lax.axis_index("y"), lax.axis_index("c")
    me, sibling = (x, y, c), (x, y, 1 - c)
    chips = [(1 - x, y), (x, 1 - y), (1 - x, 1 - y)]

    def copy(a, k, block, to, src=None):
        dst = _slab(outs[a], *block)
        return pltpu.make_async_remote_copy(
            src_ref=dst if src is None else src, dst_ref=dst, send_sem=send_sems.at[a, k],
            recv_sem=recv_sems.at[a, k], device_id=to, device_id_type=MESH)

    def own_sends():
        out = []
        for a in range(na):
            out.append(copy(a, 0, me, sibling, src=ins[a]))
            out += [copy(a, 1 + j, me, (*chip, c), src=ins[a]) for j, chip in enumerate(chips)]
        return out

    def start():
        for cp in own_sends():
            cp.start()
        for a in range(na):
            _bounce(ins[a], _slab(outs[a], *me), bufs[a], local_sems.at[a])

    def forward():
        for j, chip in enumerate(chips):
            for a in range(na):
                copy(a, 1 + j, (*chip, c), me).wait_recv()
                copy(a, 4 + j, (*chip, c), sibling).start()

    def finish():
        for a in range(na):
            copy(a, 0, sibling, me).wait_recv()
            for j, chip in enumerate(chips):
                copy(a, 4 + j, (*chip, 1 - c), me).wait_recv()
        for cp in own_sends():
            cp.wait_send()
        for j, chip in enumerate(chips):
            for a in range(na):
                copy(a, 4 + j, (*chip, c), sibling).wait_send()

    return start, forward, finish


def _ag_scratch(shards):
    na = len(shards)
    return [pltpu.SemaphoreType.DMA((na, 7)), pltpu.SemaphoreType.DMA((na, 7)),
            pltpu.SemaphoreType.DMA((na,))] + [pltpu.VMEM(s.shape, s.dtype) for s in shards]


def _all_gather(shards):
    na = len(shards)

    def body(*refs):
        start, forward, finish = _ag_program(refs[:na], refs[na:2 * na], refs[2 * na:])
        start()
        forward()
        finish()

    return pl.pallas_call(
        body, in_specs=[ANY] * na, out_specs=[ANY] * na,
        out_shape=[SDS((N_DEV,) + s.shape, s.dtype) for s in shards],
        scratch_shapes=_ag_scratch(shards), name="all_gather")(*shards)


N_CHIP = 4


def _pair_sum(own, got, name):
    na = len(own)

    def body(*refs):
        for a in range(na):
            o_ref, g_ref, s_ref = refs[a], refs[na + a], refs[2 * na + a]
            s_ref[...] = (o_ref[...].astype(F32) + g_ref[...].astype(F32)).astype(s_ref.dtype)

    def spec(p):
        nd = len(p.shape) - 1
        return pl.BlockSpec((1,) + p.shape[1:], lambda k, nd=nd: (k,) + (0,) * nd)

    return pl.pallas_call(
        body, grid=(N_CHIP,), in_specs=[spec(p) for p in own] + [spec(p) for p in got],
        out_specs=[spec(p) for p in own], out_shape=[SDS(p.shape, p.dtype) for p in own],
        compiler_params=_cparams(), name=name)(*own, *got)


def _chips_program(ins, outs, scratch):
    na = len(ins)
    send_sems, recv_sems, local_sems = scratch[:3]
    bufs = scratch[3:]
    x, y, c = lax.axis_index("x"), lax.axis_index("y"), lax.axis_index("c")
    mine = 2 * x + y
    chips = [(1 - x, y), (x, 1 - y), (1 - x, 1 - y)]

    def send(a, j):
        px, py = chips[j]
        return pltpu.make_async_remote_copy(
            src_ref=ins[a].at[2 * px + py], dst_ref=outs[a].at[mine], send_sem=send_sems.at[a, j],
            recv_sem=recv_sems.at[a, j], device_id=(px, py, c), device_id_type=MESH)

    def arrival(a, j):
        px, py = chips[j]
        return pltpu.make_async_remote_copy(
            src_ref=ins[a].at[2 * px + py], dst_ref=outs[a].at[2 * px + py], send_sem=send_sems.at[a, j],
            recv_sem=recv_sems.at[a, j], device_id=(px, py, c), device_id_type=MESH)

    def start():
        for a in range(na):
            for j in range(3):
                send(a, j).start()
        for a in range(na):
            _bounce(ins[a].at[mine], outs[a].at[mine], bufs[a], local_sems.at[a])

    def finish():
        for a in range(na):
            for j in range(3):
                arrival(a, j).wait_recv()
        for a in range(na):
            for j in range(3):
                send(a, j).wait_send()

    return start, finish


def _chips_scratch(parts):
    na = len(parts)
    return [pltpu.SemaphoreType.DMA((na, 3)), pltpu.SemaphoreType.DMA((na, 3)),
            pltpu.SemaphoreType.DMA((na,))] + [pltpu.VMEM(p.shape[1:], p.dtype) for p in parts]


def _direct_program(ins, outs, scratch):
    na = len(ins)
    send_sems, recv_sems, local_sems = scratch[:3]
    bufs = scratch[3:]
    x, y, c = lax.axis_index("x"), lax.axis_index("y"), lax.axis_index("c")
    me = (x, y, c)
    peers = []
    for k in range(1, N_DEV):
        dx, dy, dc = (k >> 2) & 1, (k >> 1) & 1, k & 1
        peers.append(((1 - x) if dx else x, (1 - y) if dy else y, (1 - c) if dc else c))

    def send(a, k):
        return pltpu.make_async_remote_copy(
            src_ref=_slab(ins[a], *peers[k]), dst_ref=_slab(outs[a], *me), send_sem=send_sems.at[a, k],
            recv_sem=recv_sems.at[a, k], device_id=peers[k], device_id_type=MESH)

    def arrival(a, k):
        return pltpu.make_async_remote_copy(
            src_ref=_slab(ins[a], *peers[k]), dst_ref=_slab(outs[a], *peers[k]), send_sem=send_sems.at[a, k],
            recv_sem=recv_sems.at[a, k], device_id=peers[k], device_id_type=MESH)

    def start():
        for a in range(na):
            for k in range(N_DEV - 1):
                send(a, k).start()
        for a in range(na):
            _bounce(_slab(ins[a], *me), _slab(outs[a], *me), bufs[a], local_sems.at[a])

    def finish():
        for a in range(na):
            for k in range(N_DEV - 1):
                arrival(a, k).wait_recv()
        for a in range(na):
            for k in range(N_DEV - 1):
                send(a, k).wait_send()

    return start, finish


def _direct_scratch(parts):
    na = len(parts)
    return [pltpu.SemaphoreType.DMA((na, N_DEV - 1)), pltpu.SemaphoreType.DMA((na, N_DEV - 1)),
            pltpu.SemaphoreType.DMA((na,))] + [pltpu.VMEM(p.shape[1:], p.dtype) for p in parts]


ROW_TILES = D_MODEL // 128


def _rows3(t):
    return jnp.transpose(t[0]).reshape(t.shape[2], ROW_TILES, 128)


def _unrows3(t):
    return jnp.transpose(t.reshape(t.shape[0], D_MODEL))[None]


def _cast_shards(w_in3, w_att, w_ssm, w_o):
    def body(wi_ref, wa_ref, ws_ref, wo_ref, a_ref, b_ref, c_ref, d_ref):
        a_ref[...] = wi_ref[...].reshape(SHARD_IN // 2, 2 * ROW_TILES, 128).astype(BF16)
        b_ref[...] = wa_ref[...].astype(BF16)
        c_ref[...] = ws_ref[...].astype(BF16)
        d_ref[...] = wo_ref[...].astype(BF16)

    return pl.pallas_call(
        body, out_shape=[SDS((SHARD_IN // 2, 2 * ROW_TILES, 128), BF16), SDS(w_att.shape, BF16),
                         SDS(w_ssm.shape, BF16), SDS(w_o.shape, BF16)],
        compiler_params=_cparams(), name="cast_shards")(w_in3, w_att, w_ssm, w_o)


def _pieces():
    out = []
    for r0, c0, w in _SEGS:
        r = r0
        while r < r0 + w:
            d = r // SHARD_IN
            n = min(r0 + w, (d + 1) * SHARD_IN) - r
            out.append((c0 + (r - r0), d, r - d * SHARD_IN, n))
            r += n
    return out


def _to_aligned_t(slabs):
    def body(a_ref, o_ref):
        for (t, d, s, n) in _pieces():
            o_ref[t:t + n, :] = a_ref[d, s // 2:(s + n) // 2].reshape(n, D_MODEL)
        o_ref[C_DT + 32:C_DT + 128, :] = jnp.zeros((96, D_MODEL), slabs.dtype)

    return pl.pallas_call(body, out_shape=SDS((PW, D_MODEL), slabs.dtype), compiler_params=_cparams(),
                          name="to_aligned")(slabs)


def _from_aligned_pair(g):
    slab = (SHARD_IN // 2, 2 * ROW_TILES, 128)
    by_slab = [[p for p in _pieces() if p[1] == d] for d in range(N_DEV)]

    def body(g_ref, own_ref, got_ref, slabs, send_sems, recv_sems, local_sems):
        x, y, c = lax.axis_index("x"), lax.axis_index("y"), lax.axis_index("c")
        sibling = (x, y, 1 - c)

        def to_own(d, k):
            return pltpu.make_async_copy(slabs.at[d], own_ref.at[k], local_sems.at[k])

        def to_sibling(d, k):
            return pltpu.make_async_remote_copy(
                src_ref=slabs.at[d], dst_ref=got_ref.at[k], send_sem=send_sems.at[k], recv_sem=recv_sems.at[k],
                device_id=sibling, device_id_type=MESH)

        for d in range(N_DEV):
            for (t, _, s, n) in by_slab[d]:
                slabs[d, s // 2:(s + n) // 2] = g_ref[t:t + n, :].reshape(n // 2, 2 * ROW_TILES, 128)
            k, side = d // 2, d % 2
            pl.when(c == side)(to_own(d, k).start)
            pl.when(c != side)(to_sibling(d, k).start)
        for k in range(N_CHIP):
            to_own(0, k).wait()
            to_sibling(0, k).wait()

    half = SDS((N_CHIP,) + slab, g.dtype)
    return pl.pallas_call(
        body, in_specs=[pl.BlockSpec(memory_space=pltpu.VMEM)], out_specs=[ANY, ANY], out_shape=[half, half],
        scratch_shapes=[pltpu.VMEM((N_DEV,) + slab, g.dtype), pltpu.SemaphoreType.DMA((N_CHIP,)),
                        pltpu.SemaphoreType.DMA((N_CHIP,)), pltpu.SemaphoreType.DMA((N_CHIP,))],
        compiler_params=_cparams(), name="from_aligned_pair")(g)


_SEGS = [
    (R_Q, C_Q, 1024), (R_K, C_K, 256), (R_V, C_V, 256), (R_ZA, C_ZA, 1024), (R_ZS, C_ZS, 2048),
    (R_XBC, C_XBC, 3072), (R_DT, C_DT, 32), (R_GA, C_GA, 1024), (R_GS, C_GS, 1024)]


def _pad_lanes(v, n=128):
    return jnp.pad(v, ((0, 0), (0, n - v.shape[1])))


def _device_step(h, tgt, w_alt, w_out, g_pre, conv_w8, conv_b, dt_bias, a_log, d_skip, sinks, g_ssm, g_post, on_mesh):
    dtb, al, dsk, snk = _pad_lanes(dt_bias), _pad_lanes(a_log), _pad_lanes(d_skip), _pad_lanes(sinks)
    u = _norm_u(h, g_pre)
    proj = _matmul(u, w_alt, "nt", F32, T, PROJ_TILE, "in_proj")
    o = _attn_fwd(proj, snk)
    if on_mesh:
        sn, states, att_all, ssm_all, o_all = _ssd_fwd(proj, conv_w8, conv_b, dtb, al, dsk, g_ssm, gather=w_out)
        w_att = att_all.reshape(D_MODEL, D_MODEL)
        w_ssm = ssm_all.reshape(SSM_INNER, D_MODEL)
        w_o = o_all.reshape(D_MODEL, D_MODEL)
    else:
        sn, states = _ssd_fwd(proj, conv_w8, conv_b, dtb, al, dsk, g_ssm)
        w_att, w_ssm, w_o = w_out
    a_in, mg, ya, ys, out = _post_a(o, proj, sn, w_att, w_ssm, w_o)
    (loss, dres, dout, dya, dys, do, dproj, dsn, dgp) = _post_b(
        out, h, tgt, proj, ya, ys, o, g_post, w_att, w_ssm, w_o)
    dw_att = _matmul(a_in, dya, "tn", BF16, OUT_W_TILE, D_MODEL, "d_w_att")
    dw_ssm = _matmul(sn, dys, "tn", BF16, OUT_W_TILE, D_MODEL, "d_w_ssm")
    dw_o = _matmul(mg, dout, "tn", BF16, OUT_W_TILE, D_MODEL, "d_w_o")
    res = {}
    if on_mesh:
        parts = [dw_att.reshape(N_DEV, 128, D_MODEL), dw_ssm.reshape(N_DEV, 256, D_MODEL),
                 dw_o.reshape(N_DEV, 128, D_MODEL)]
        (ddt4, dproj, ddtb, dal, ddsk, dgn, dcw, dcb, res["r_att"], res["r_ssm"], res["r_o"]) = _ssd_bwd(
            proj, conv_w8, conv_b, dtb, al, dsk, g_ssm, states, dsn, dproj, exchange=parts)
    else:
        ddt4, dproj, ddtb, dal, ddsk, dgn, dcw, dcb = _ssd_bwd(proj, conv_w8, conv_b, dtb, al, dsk, g_ssm, states,
                                                               dsn, dproj)
        res.update(dw_att=dw_att, dw_ssm=dw_ssm, dw_o=dw_o)
    dproj, dk, dv, dsink = _attn_bwd(proj, snk, do, dproj)
    dproj = _dproj_tail(dproj, dk, dv, ddt4)
    dw_alt = _matmul(dproj, u, "tn", BF16, PROJ_TILE, D_MODEL, "d_w_in")
    if on_mesh:
        own, got = _from_aligned_pair(dw_alt)
        dh, dgpre, res["r_in"] = _d_u_norm(dproj, w_alt, h, g_pre, dres,
                                           chips=_pair_sum([own], [got], "pair_sum_w_in"))
    else:
        dh, dgpre = _d_u_norm(dproj, w_alt, h, g_pre, dres)
        res["dw_alt"] = dw_alt
    small = (dgpre, dcb, ddtb, dal, ddsk, dsink, dgn, dgp, dcw)
    if on_mesh:
        res["small_pack"] = _small_pack(*small, loss, dh)
    else:
        res["small"] = small
    res.update(loss=loss[0, 0], dh=dh)
    return res


def kernel(x, meta_tokens, g_pre, w_in, conv_w, conv_b, dt_bias, a_log, d_skip, attn_sinks, g_ssm_norm, w_out_att, w_out_ssm, w_out, g_post, loss_target, m_meta_tokens, m_g_pre, m_w_in, m_conv_w, m_conv_b, m_dt_bias, m_a_log, m_d_skip, m_attn_sinks, m_g_ssm_norm, m_w_out_att, m_w_out_ssm, m_w_out, m_g_post, v_meta_tokens, v_g_pre, v_w_in, v_conv_w, v_conv_b, v_dt_bias, v_a_log, v_d_skip, v_attn_sinks, v_g_ssm_norm, v_w_out_att, v_w_out_ssm, v_w_out, v_g_post):
    w_in3, m_in3, v_in3 = _rows3(w_in), _rows3(m_w_in), _rows3(v_w_in)
    a_sh, att_sh, ssm_sh, o_sh = _cast_shards(w_in3, w_out_att[0], w_out_ssm[0], w_out[0])
    cw_sh = jnp.pad(conv_w[0], ((0, 4), (0, 0)))
    a_all, meta_all, cw_all = _all_gather([a_sh, meta_tokens, cw_sh])
    w_alt = _to_aligned_t(a_all)
    meta_full = meta_all.transpose(1, 0, 2).reshape(N_META, D_MODEL)
    conv_w8 = cw_all.transpose(1, 0, 2).reshape(8, CONV_DIM)

    h = jnp.concatenate([jnp.zeros((PAD, D_MODEL), F32), meta_full, x[0]], axis=0)
    tgt = jnp.concatenate([jnp.zeros((PAD + N_META, D_MODEL), F32), loss_target[0]], axis=0)
    r = _device_step(h, tgt, w_alt, (att_sh, ssm_sh, o_sh), g_pre, conv_w8, conv_b, dt_bias, a_log, d_skip,
                     attn_sinks, g_ssm_norm, g_post, True)
    grad_x = r["dh"][PAD + N_META:][None]

    *res_in, r_small = _sum_adamw_rows3(r["r_in"], w_in3, m_in3, v_in3, "adamw_w_in", exchange=[r["small_pack"]])
    res_in = [_unrows3(t) for t in res_in]
    res_att = [t[None] for t in _sum_adamw(r["r_att"], w_out_att[0], m_w_out_att[0], v_w_out_att[0], 512,
                                           "adamw_w_att")]
    res_ssm = [t[None] for t in _sum_adamw(r["r_ssm"], w_out_ssm[0], m_w_out_ssm[0], v_w_out_ssm[0], 512,
                                           "adamw_w_ssm")]
    res_o = [t[None] for t in _sum_adamw(r["r_o"], w_out[0], m_w_out[0], v_w_out[0], 512, "adamw_w_o")]
    (res_gpre, res_convb, res_dtb, res_alog, res_dskip, res_sink, res_gssm, res_gpost, res_cw, res_meta), loss = _small_finish(
        r_small, [(g_pre, m_g_pre, v_g_pre), (conv_b, m_conv_b, v_conv_b), (dt_bias, m_dt_bias, v_dt_bias),
                       (a_log, m_a_log, v_a_log), (d_skip, m_d_skip, v_d_skip),
                       (attn_sinks, m_attn_sinks, v_attn_sinks), (g_ssm_norm, m_g_ssm_norm, v_g_ssm_norm),
                       (g_post, m_g_post, v_g_post), (conv_w[0], m_conv_w[0], v_conv_w[0]),
                       (meta_tokens, m_meta_tokens, v_meta_tokens)])
    res_cw = [t[None] for t in res_cw]
    per_weight = [res_meta, res_gpre, res_in, res_cw, res_convb, res_dtb, res_alog, res_dskip, res_sink, res_gssm,
                  res_att, res_ssm, res_o, res_gpost]
    return (loss[0, 0], grad_x, *[p[0] for p in per_weight], *[p[1] for p in per_weight], *[p[2] for p in per_weight],
            *[p[3] for p in per_weight])
```

```python
import jax
import jax.numpy as jnp
import numpy as np
from jax import lax
from jax.experimental import pallas as pl
from jax.experimental.pallas import tpu as pltpu

F32 = jnp.float32
BF16 = jnp.bfloat16
SDS = jax.ShapeDtypeStruct
MESH = pl.DeviceIdType.MESH
ANY = pl.BlockSpec(memory_space=pl.ANY)

N_DEV = 8
D_MODEL = 1024
SEQ = 2048
N_META = 16
BLK = 128
PAD = 112
T = PAD + N_META + SEQ
NB = T // BLK
EPS = 1e-6
HEAD = 64
Q_HEADS = 16
KV_HEADS = 4
GROUP = 4
KV_W = 256
SSM_INNER = 2048
SSM_HEADS = 32
SSM_GROUPS = 4
GRP_W = 512
SSM_STATE = 128
CONV_DIM = 3072
IN_PROJ = 9760
SHARD_IN = IN_PROJ // N_DEV
NEG = -1e30

C_ZS, C_XBC, C_Q, C_ZA, C_GA, C_GS, C_K, C_V, C_DT = 0, 2048, 5120, 6144, 7168, 8192, 9216, 9472, 9728
PW = 9856
GATES_W = 3 * D_MODEL
PROJ_TILE = 1408
OUT_W_TILE = 512
R_Q, R_K, R_V, R_ZA, R_ZS, R_XBC, R_DT, R_GA, R_GS = 0, 1024, 1280, 1536, 2560, 4608, 7680, 7712, 8736

ADAM_LR, ADAM_B1, ADAM_B2, ADAM_EPS, ADAM_WD, ADAM_STEP = 0.001, 0.9, 0.999, 1e-08, 0.01, 10

VMEM_LIMIT = 56 * 1024 * 1024


def _cparams():
    return pltpu.CompilerParams(vmem_limit_bytes=VMEM_LIMIT)


def _silu(x):
    return x * jax.nn.sigmoid(x)


def _dsilu(x):
    s = jax.nn.sigmoid(x)
    return s * (1.0 + x * (1.0 - s))


def _matmul(a, b, mode, out_dtype, tm, tn, name):
    if mode == "nt":
        (m, k), n = a.shape, b.shape[0]
        a_spec = pl.BlockSpec((tm, k), lambda i, j: (i, 0))
        b_spec = pl.BlockSpec((tn, k), lambda i, j: (j, 0))
        dims = (((1,), (1,)), ((), ()))
    else:
        assert mode == "tn"
        (k, m), n = a.shape, b.shape[1]
        a_spec = pl.BlockSpec((k, tm), lambda i, j: (0, i))
        b_spec = pl.BlockSpec((k, tn), lambda i, j: (0, j))
        dims = (((0,), (0,)), ((), ()))
    assert m % tm == 0 and n % tn == 0, (a.shape, b.shape, tm, tn)

    def body(a_ref, b_ref, o_ref):
        o_ref[...] = lax.dot_general(a_ref[...], b_ref[...], dims, preferred_element_type=F32).astype(out_dtype)

    return pl.pallas_call(
        body, grid=(m // tm, n // tn), in_specs=[a_spec, b_spec],
        out_specs=pl.BlockSpec((tm, tn), lambda i, j: (i, j)), out_shape=SDS((m, n), out_dtype),
        compiler_params=_cparams(), name=name)(a, b)


def _norm_u(h, g_pre):
    def body(h_ref, g_ref, u_ref):
        x = h_ref[...]
        r = lax.rsqrt(jnp.mean(x * x, axis=-1, keepdims=True) + EPS)
        u_ref[...] = (x * r * g_ref[...]).astype(BF16)

    rows = T // 4
    return pl.pallas_call(
        body, grid=(T // rows,),
        in_specs=[pl.BlockSpec((rows, D_MODEL), lambda i: (i, 0)), pl.BlockSpec((1, D_MODEL), lambda i: (0, 0))],
        out_specs=pl.BlockSpec((rows, D_MODEL), lambda i: (i, 0)),
        out_shape=SDS((T, D_MODEL), BF16), name="norm_u")(h, g_pre)


DU_TM, DU_TK = T // 2, PROJ_TILE


def _d_u_norm(dproj, w_alt, h, g_pre, dres, chips=()):
    nk = PW // DU_TK
    ni = T // DU_TM
    nc = len(chips)

    def body(*refs):
        a_ref, b_ref, h_ref, g_ref, dres_ref = refs[:5]
        dh_ref, dg_ref = refs[5 + nc:7 + nc]
        acc_ref = refs[7 + 2 * nc]
        i, kk = pl.program_id(0), pl.program_id(1)
        if nc:
            ch_start, ch_finish = _chips_program(refs[5:5 + nc], refs[7 + nc:7 + 2 * nc], refs[8 + 2 * nc:])
            pl.when((i == 0) & (kk == 0))(ch_start)
        part = jnp.dot(a_ref[...], b_ref[...], preferred_element_type=F32)

        @pl.when(kk == 0)
        def _():
            acc_ref[...] = part

        @pl.when((kk > 0) & (kk < nk - 1))
        def _():
            acc_ref[...] += part

        @pl.when(kk == nk - 1)
        def _():
            du_ = acc_ref[...] + part
            x = h_ref[...]
            r = lax.rsqrt(jnp.mean(x * x, axis=-1, keepdims=True) + EPS)
            gd = g_ref[...] * du_
            dx = r * gd - x * (r * r * r) * jnp.mean(x * gd, axis=-1, keepdims=True)
            dh_ref[...] = dx + dres_ref[...]
            gpart = jnp.concatenate([jnp.sum(du_ * x * r, axis=0, keepdims=True), jnp.zeros((7, D_MODEL), F32)],
                                    axis=0)

            @pl.when(i == 0)
            def _():
                dg_ref[...] = gpart

            @pl.when(i > 0)
            def _():
                dg_ref[...] += gpart

        if nc:
            pl.when((i == ni - 1) & (kk == nk - 1))(ch_finish)

    row = pl.BlockSpec((DU_TM, D_MODEL), lambda i, kk: (i, 0))
    return pl.pallas_call(
        body, grid=(ni, nk),
        in_specs=[pl.BlockSpec((DU_TM, DU_TK), lambda i, kk: (i, kk)),
                  pl.BlockSpec((DU_TK, D_MODEL), lambda i, kk: (kk, 0)),
                  row, pl.BlockSpec((1, D_MODEL), lambda i, kk: (0, 0)), row] + [ANY] * nc,
        out_specs=[row, pl.BlockSpec((8, D_MODEL), lambda i, kk: (0, 0))] + [ANY] * nc,
        out_shape=[SDS((T, D_MODEL), F32), SDS((8, D_MODEL), F32)] + [SDS(p.shape, p.dtype) for p in chips],
        scratch_shapes=[pltpu.VMEM((DU_TM, D_MODEL), F32)] + (_chips_scratch(chips) if nc else []),
        compiler_params=_cparams(), name="d_u_norm")(dproj, w_alt, h, g_pre, dres, *chips)


def _lane_pick(row, h):
    lane = lax.broadcasted_iota(jnp.int32, row.shape, 1)
    return jnp.sum(jnp.where(lane == h, row, 0.0), axis=1, keepdims=True)


def _alibi_band():
    r = np.arange(GROUP * BLK)[:, None]
    rel = (r % BLK) - np.arange(2 * BLK)[None, :] + BLK
    out = np.empty((KV_HEADS, GROUP * BLK, 2 * BLK), np.float32)
    for kh in range(KV_HEADS):
        slope = (2.0 ** (-8.0 * (kh * GROUP + r // BLK + 1) / Q_HEADS)).astype(np.float32)
        out[kh] = np.where((rel >= 0) & (rel < BLK), -slope * rel.astype(np.float32), np.float32(NEG))
    return jnp.asarray(out)


def _attn_fn(q4s, kcats, vcats, kms, vms, sinks, n, band):
    s = lax.broadcasted_iota(jnp.int32, (GROUP * BLK, 2 * BLK), 1)
    key_off = jnp.where(n * BLK - BLK + s >= PAD + N_META, 0.0, NEG)
    rm = lax.broadcasted_iota(jnp.int32, (GROUP * BLK, N_META), 0)
    mm = lax.broadcasted_iota(jnp.int32, (GROUP * BLK, N_META), 1)
    meta_ok = (PAD + mm) <= (n * BLK + jnp.bitwise_and(rm, BLK - 1))
    gcol = jnp.right_shift(lax.broadcasted_iota(jnp.int32, (GROUP * BLK, 1), 0), 7)
    outs = []
    for kh in range(KV_HEADS):
        sk = [_lane_pick(sinks, kh * GROUP + g) for g in range(GROUP)]
        sink = jnp.where(gcol == 0, sk[0], jnp.where(gcol == 1, sk[1], jnp.where(gcol == 2, sk[2], sk[3])))
        qb = (q4s[kh] * (HEAD ** -0.5)).astype(BF16)
        sb = lax.dot_general(qb, kcats[kh].astype(BF16), (((1,), (1,)), ((), ())), preferred_element_type=F32)
        sb = sb + (band[kh] + key_off)
        sm = lax.dot_general(qb, kms[kh].astype(BF16), (((1,), (1,)), ((), ())), preferred_element_type=F32)
        sm = jnp.where(meta_ok, sm, NEG)
        mx = jnp.maximum(jnp.maximum(jnp.max(sb, axis=1, keepdims=True), jnp.max(sm, axis=1, keepdims=True)), sink)
        mx = lax.stop_gradient(mx)
        eb = jnp.exp(sb - mx)
        em = jnp.exp(sm - mx)
        es = jnp.exp(sink - mx)
        inv = 1.0 / (jnp.sum(eb, axis=1, keepdims=True) + jnp.sum(em, axis=1, keepdims=True) + es)
        pb = (eb * inv).astype(BF16)
        pm = (em * inv).astype(BF16)
        o4 = (jnp.dot(pm, vms[kh].astype(BF16), preferred_element_type=F32)
              + jnp.dot(pb, vcats[kh].astype(BF16), preferred_element_type=F32))
        outs.append(o4)
    return outs


def _attn_specs():
    prev = lambda n: jnp.maximum(n - 1, 0)
    return [
        pl.BlockSpec((BLK, D_MODEL), lambda n: (n, C_Q // D_MODEL)),
        pl.BlockSpec((BLK, KV_W), lambda n: (prev(n), C_K // KV_W)),
        pl.BlockSpec((BLK, KV_W), lambda n: (n, C_K // KV_W)),
        pl.BlockSpec((BLK, KV_W), lambda n: (prev(n), C_V // KV_W)),
        pl.BlockSpec((BLK, KV_W), lambda n: (n, C_V // KV_W)),
        pl.BlockSpec((N_META, KV_W), lambda n: (PAD // N_META, C_K // KV_W)),
        pl.BlockSpec((N_META, KV_W), lambda n: (PAD // N_META, C_V // KV_W)),
        pl.BlockSpec((1, 128), lambda n: (0, 0)),
        pl.BlockSpec((KV_HEADS, GROUP * BLK, 2 * BLK), lambda n: (0, 0, 0)),
    ]


def _attn_load(q_ref, kp_ref, kc_ref, vp_ref, vc_ref, km_ref, vm_ref):
    q4s, kcats, vcats, kms, vms = [], [], [], [], []
    for kh in range(KV_HEADS):
        q4s.append(jnp.concatenate(
            [q_ref[:, (kh * GROUP + g) * HEAD:(kh * GROUP + g + 1) * HEAD] for g in range(GROUP)], axis=0))
        cs = slice(kh * HEAD, (kh + 1) * HEAD)
        kcats.append(jnp.concatenate([kp_ref[:, cs], kc_ref[:, cs]], axis=0))
        vcats.append(jnp.concatenate([vp_ref[:, cs], vc_ref[:, cs]], axis=0))
        kms.append(km_ref[:, cs])
        vms.append(vm_ref[:, cs])
    return q4s, kcats, vcats, kms, vms


def _attn_fwd(proj, sinks):
    def body(q_ref, kp_ref, kc_ref, vp_ref, vc_ref, km_ref, vm_ref, s_ref, band_ref, o_ref):
        n = pl.program_id(0)
        args = _attn_load(q_ref, kp_ref, kc_ref, vp_ref, vc_ref, km_ref, vm_ref)
        outs = _attn_fn(*args, s_ref[...], n, [band_ref[kh] for kh in range(KV_HEADS)])
        for kh in range(KV_HEADS):
            for g in range(GROUP):
                hh = kh * GROUP + g
                o_ref[:, hh * HEAD:(hh + 1) * HEAD] = outs[kh][g * BLK:(g + 1) * BLK]

    return pl.pallas_call(
        body, grid=(NB,), in_specs=_attn_specs(),
        out_specs=pl.BlockSpec((BLK, D_MODEL), lambda n: (n, 0)),
        out_shape=SDS((T, D_MODEL), F32), name="attn_fwd")(proj, proj, proj, proj, proj, proj, proj, sinks,
                                                            _alibi_band())


def _attn_bwd(proj, sinks, do, dproj):
    def body(q_ref, kp_ref, kc_ref, vp_ref, vc_ref, km_ref, vm_ref, s_ref, band_ref, do_ref, _, dq_ref, dk_ref, dv_ref,
             ds_ref):
        n = pl.program_id(0)
        band = [band_ref[kh] for kh in range(KV_HEADS)]

        @pl.when(n == 0)
        def _():
            dk_ref[...] = jnp.zeros_like(dk_ref)
            dv_ref[...] = jnp.zeros_like(dv_ref)
            ds_ref[...] = jnp.zeros_like(ds_ref)

        args = _attn_load(q_ref, kp_ref, kc_ref, vp_ref, vc_ref, km_ref, vm_ref)
        _, vjp = jax.vjp(lambda a, b, c, d, e, f: _attn_fn(a, b, c, d, e, f, n, band), *args, s_ref[...])
        do_f = do_ref[...].astype(F32)
        cot = [jnp.concatenate([do_f[:, (kh * GROUP + g) * HEAD:(kh * GROUP + g + 1) * HEAD] for g in range(GROUP)],
                               axis=0) for kh in range(KV_HEADS)]
        dq4s, dkcats, dvcats, dkms, dvms, dsk = vjp(cot)
        ds_ref[0:1, :] += dsk
        cur = pl.ds(pl.multiple_of(n * BLK, BLK), BLK)
        meta = slice(PAD, PAD + N_META)
        for kh in range(KV_HEADS):
            cs = slice(kh * HEAD, (kh + 1) * HEAD)
            for g in range(GROUP):
                hh = kh * GROUP + g
                dq_ref[:, hh * HEAD:(hh + 1) * HEAD] = dq4s[kh][g * BLK:(g + 1) * BLK].astype(BF16)
            dk_ref[cur, cs] += dkcats[kh][BLK:]
            dv_ref[cur, cs] += dvcats[kh][BLK:]
            dk_ref[meta, cs] += dkms[kh]
            dv_ref[meta, cs] += dvms[kh]

        @pl.when(n > 0)
        def _():
            prv = pl.ds(pl.multiple_of((n - 1) * BLK, BLK), BLK)
            for kh in range(KV_HEADS):
                cs = slice(kh * HEAD, (kh + 1) * HEAD)
                dk_ref[prv, cs] += dkcats[kh][:BLK]
                dv_ref[prv, cs] += dvcats[kh][:BLK]

    full_kv = pl.BlockSpec((T, KV_W), lambda n: (0, 0))
    return pl.pallas_call(
        body, grid=(NB,),
        in_specs=_attn_specs() + [pl.BlockSpec((BLK, D_MODEL), lambda n: (n, 0)), ANY],
        out_specs=[pl.BlockSpec((BLK, D_MODEL), lambda n: (n, C_Q // D_MODEL)), full_kv, full_kv,
                   pl.BlockSpec((8, 128), lambda n: (0, 0))],
        out_shape=[SDS((T, PW), BF16), SDS((T, KV_W), F32), SDS((T, KV_W), F32), SDS((8, 128), F32)],
        input_output_aliases={10: 0},
        name="attn_bwd")(proj, proj, proj, proj, proj, proj, proj, sinks, _alibi_band(), do, dproj)


def _rows_from(ext, start):
    if start % 8 == 0:
        return ext[start:start + BLK]
    return pltpu.roll(ext, (8 - start) % (BLK + 8), 0)[8:8 + BLK]


def _conv_taps(taps, w):
    return w[0:1] * taps[0] + w[1:2] * taps[1] + w[2:3] * taps[2] + w[3:4] * taps[3]


HPG = SSM_HEADS // SSM_GROUPS


def _iota(shape, dim):
    return lax.broadcasted_iota(jnp.int32, shape, dim)


def _mm(a, b, ca=1, cb=0):
    return lax.dot_general(a.astype(BF16), b.astype(BF16), (((ca,), (cb,)), ((), ())), preferred_element_type=F32)


def _split3(v):
    hi = v.astype(BF16)
    r1 = v - hi.astype(F32)
    mid = r1.astype(BF16)
    lo = (r1 - mid.astype(F32)).astype(BF16)
    return hi, mid, lo


def _split2(v):
    hi = v.astype(BF16)
    return hi, (v - hi.astype(F32)).astype(BF16)


def _sel_r(parts, onehot, ca=1, cb=0):
    out = lax.dot_general(parts[0], onehot, (((ca,), (cb,)), ((), ())), preferred_element_type=F32)
    for p in parts[1:]:
        out = out + lax.dot_general(p, onehot, (((ca,), (cb,)), ((), ())), preferred_element_type=F32)
    return out


def _sel_l(onehot, parts):
    out = jnp.dot(onehot, parts[0], preferred_element_type=F32)
    for p in parts[1:]:
        out = out + jnp.dot(onehot, p, preferred_element_type=F32)
    return out


def _rows8(*rows):
    r = _iota((8, rows[0].shape[1]), 0)
    out = jnp.zeros((8, rows[0].shape[1]), F32)
    for k, v in enumerate(rows):
        out = jnp.where(r == k, v, out)
    return out


def _ssd_consts():
    r, c = np.arange(BLK)[:, None], np.arange(BLK)[None, :]
    tri_l = (c <= r).astype(np.float32)
    q = np.arange(GRP_W)[None, :]
    spread = np.stack([(r == g * HPG + q // HEAD) for g in range(SSM_GROUPS)]).astype(np.float32)
    pick = np.stack([np.stack([(r == g * HPG + c - k * HPG) & (c >= k * HPG) & (c < (k + 1) * HPG) for k in range(3)])
                     for g in range(SSM_GROUPS)]).astype(np.float32)
    causal = np.where(r >= (np.arange(HPG * BLK)[None, :] % BLK), 0.0, NEG).astype(np.float32)
    bf = lambda a: jnp.asarray(a, BF16)
    return dict(tri_l=bf(tri_l), tri_u=bf(tri_l.T), spread=bf(spread), unspread=bf(spread.transpose(0, 2, 1)),
                pick=bf(pick), causal=jnp.asarray(causal))


SSD_CONSTS = ("tri_l", "tri_u", "spread", "unspread", "pick", "causal")


def _ssd_forward(x, z, bm, cm, dt_raw, st_prev, dtb, alog, dskip, gn, g, cst_scr, cn):
    dt_all = jax.nn.softplus(dt_raw + dtb)
    a_row = -jnp.exp(alog)
    a_all = dt_all * a_row
    cs_all = _sel_l(cn["tri_l"][...], _split3(a_all))
    cs_parts = _split3(cs_all)
    spread = cn["spread"][g]
    dt_e = _sel_r(_split2(dt_all), spread)
    cs_e = _sel_r(cs_parts, spread)
    d_e = _sel_r(_split2(_rows8(dskip)), spread)[0:1]
    cs_last_e = jnp.sum(jnp.where(_iota((BLK, GRP_W), 0) == BLK - 1, cs_e, 0.0), axis=0, keepdims=True)
    p_e = jnp.exp(cs_e)
    w_e = jnp.exp(cs_last_e - cs_e)
    cd_e = jnp.exp(cs_last_e)
    xr = x * dt_e
    cst_scr[...] = cs_all.T
    cst_g = cst_scr[g * HPG:(g + 1) * HPG, :]
    own = jnp.right_shift(_iota((HPG, HPG * BLK), 1), 7) == _iota((HPG, HPG * BLK), 0)
    ownf = own.astype(F32)
    q_rows = [ownf, ownf, ownf] + [jnp.where(own, jnp.concatenate([p.astype(F32)] * HPG, axis=1), 0.0)
                                   for p in _split3(cst_g)]
    q2 = jnp.concatenate(q_rows + [jnp.zeros((BLK - 6 * HPG, HPG * BLK), F32)], axis=0).astype(BF16)
    lane1 = _iota((1, BLK), 1)
    p2 = jnp.where((lane1 >= 3 * HPG) & (lane1 < 6 * HPG), -1.0, 0.0)
    for k, part in enumerate(cs_parts):
        p2 = p2 + jnp.dot(part, cn["pick"][g, k], preferred_element_type=F32)
    dmat = jnp.dot(p2.astype(BF16), q2, preferred_element_type=F32)
    lam = jnp.exp(dmat + cn["causal"][...])
    gmat = _mm(cm, bm, 1, 1)
    m_all = lam * jnp.concatenate([gmat] * HPG, axis=1)
    mb = m_all.astype(BF16)
    lo = _iota((BLK, BLK), 1) < HEAD
    xrb = xr.astype(BF16)
    zero = jnp.zeros((BLK, BLK), BF16)
    bds, yd = [], []
    for i in range(HPG // 2):
        t = xrb[:, BLK * i:BLK * (i + 1)]
        bd = jnp.concatenate([jnp.where(lo, t, zero), jnp.where(lo, zero, t)], axis=0)
        bds.append(bd)
        yd.append(jnp.dot(mb[:, 2 * BLK * i:2 * BLK * (i + 1)], bd, preferred_element_type=F32))
    cs_st = _mm(cm, st_prev)
    y = jnp.concatenate(yd, axis=1) + cs_st * p_e + d_e * x
    xrw = xr * w_e
    st_new = cd_e * st_prev + _mm(bm, xrw, 0, 0)
    yz = y * _silu(z)
    rn = lax.rsqrt(jnp.sum(yz * yz, axis=1, keepdims=True) / GRP_W + EPS)
    return dict(out=yz * rn * gn, st_new=st_new, dt_all=dt_all, a_row=a_row, dt_e=dt_e, d_e=d_e, p_e=p_e, w_e=w_e,
                cd_e=cd_e, xr=xr, xrw=xrw, lam=lam, m_all=m_all, mb=mb, bds=bds, cs_st=cs_st, y=y, yz=yz, rn=rn, lo=lo)


def _ssd_backward(f, x, z, bm, cm, dt_raw, st_prev, dtb, gn, g, dout, dst_next, cst_scr, cn):
    li, si = _iota((BLK, BLK), 0), _iota((BLK, BLK), 1)
    yz, rn, y, p_e, w_e, cd_e, xr = f["yz"], f["rn"], f["y"], f["p_e"], f["w_e"], f["cd_e"], f["xr"]
    dgn = jnp.sum(dout * yz * rn, axis=0, keepdims=True)
    t = dout * gn
    dyz = rn * t - yz * (rn * rn * rn) * (jnp.sum(yz * t, axis=1, keepdims=True) / GRP_W)
    dy = dyz * _silu(z)
    dz = dyz * y * _dsilu(z)
    dx = f["d_e"] * dy
    dd_e = jnp.sum(dy * x, axis=0, keepdims=True)
    dcsst = dy * p_e
    dp_e = dy * f["cs_st"]
    dcm = _mm(dcsst, st_prev, 1, 1)
    dst_prev = _mm(cm, dcsst, 0, 0) + cd_e * dst_next
    dcd_e = jnp.sum(dst_next * st_prev, axis=0, keepdims=True)
    dbm = _mm(f["xrw"], dst_next, 1, 1)
    dxrw = _mm(bm, dst_next)
    dxr = dxrw * w_e
    dw_e = dxrw * xr
    dyb = dy.astype(BF16)
    dms, dxr_d = [], []
    for i in range(HPG // 2):
        dyp = dyb[:, BLK * i:BLK * (i + 1)]
        dms.append(lax.dot_general(dyp, f["bds"][i], (((1,), (1,)), ((), ())), preferred_element_type=F32))
        r = lax.dot_general(f["mb"][:, 2 * BLK * i:2 * BLK * (i + 1)], dyp, (((0,), (0,)), ((), ())),
                            preferred_element_type=F32)
        dxr_d.append(jnp.where(f["lo"], r[0:BLK], r[BLK:2 * BLK]))
    dm_all = jnp.concatenate(dms, axis=1)
    dxr = dxr + jnp.concatenate(dxr_d, axis=1)
    dlg = dm_all * f["lam"]
    dg = dlg[:, 0:BLK]
    for j in range(1, HPG):
        dg = dg + dlg[:, BLK * j:BLK * (j + 1)]
    dcm = dcm + _mm(dg, bm)
    dbm = dbm + _mm(dg, cm, 0, 0)
    q_all = dm_all * f["m_all"]
    col_sums = jnp.sum(q_all, axis=0, keepdims=True)
    cst_scr[...] = jnp.zeros_like(cst_scr)
    cst_scr[g * HPG:(g + 1) * HPG, :] = _rows8(
        *[col_sums[:, BLK * j:BLK * (j + 1)] for j in range(HPG)])
    dcs = -cst_scr[...].T
    for j in range(HPG):
        dcs = dcs + jnp.where(si == g * HPG + j,
                              jnp.sum(q_all[:, BLK * j:BLK * (j + 1)], axis=1, keepdims=True), 0.0)
    unspread = cn["unspread"][g]
    dww = dw_e * w_e
    per_head = _sel_r(_split2(jnp.concatenate([dp_e * p_e - dww, dxr * x], axis=0)), unspread)
    last = _sel_r(_split2(_rows8(jnp.sum(dww, axis=0, keepdims=True) + dcd_e * cd_e, dd_e)), unspread)
    dcs = dcs + per_head[0:BLK] + jnp.where(li == BLK - 1, last[0:1], 0.0)
    da = _sel_l(cn["tri_u"][...], _split2(dcs))
    ddt_all = da * f["a_row"] + per_head[BLK:2 * BLK]
    dalog = jnp.sum(da * f["dt_all"], axis=0, keepdims=True) * f["a_row"]
    dx = dx + dxr * f["dt_e"]
    ddt_raw = ddt_all * jax.nn.sigmoid(dt_raw + dtb)
    ddtb = jnp.sum(ddt_raw, axis=0, keepdims=True)
    ddskip = last[1:2]
    return dict(dx=dx, dz=dz, dbm=dbm, dcm=dcm, ddt_raw=ddt_raw, dst_prev=dst_prev, ddtb=ddtb, dalog=dalog,
                ddskip=ddskip, dgn=dgn)


ZX_W = SSM_INNER + CONV_DIM
assert C_ZS == 0 and C_XBC == SSM_INNER


def _ssd_in_specs(rev):
    cidx = (lambda c: NB - 1 - c) if rev else (lambda c: c)
    return [
        pl.BlockSpec((BLK, ZX_W), lambda c: (cidx(c), 0)),
        pl.BlockSpec((8, ZX_W), lambda c: (jnp.maximum(cidx(c) * (BLK // 8) - 1, 0), 0)),
        pl.BlockSpec((BLK, 128), lambda c: (cidx(c), C_DT // 128)),
        pl.BlockSpec((8, CONV_DIM), lambda c: (0, 0)),
        pl.BlockSpec((1, CONV_DIM), lambda c: (0, 0)),
        pl.BlockSpec((1, 128), lambda c: (0, 0)),
        pl.BlockSpec((1, 128), lambda c: (0, 0)),
        pl.BlockSpec((1, 128), lambda c: (0, 0)),
        pl.BlockSpec((1, SSM_INNER), lambda c: (0, 0)),
        pl.BlockSpec((BLK, BLK), lambda c: (0, 0)),
        pl.BlockSpec((BLK, BLK), lambda c: (0, 0)),
        pl.BlockSpec((SSM_GROUPS, BLK, GRP_W), lambda c: (0, 0, 0)),
        pl.BlockSpec((SSM_GROUPS, GRP_W, BLK), lambda c: (0, 0, 0)),
        pl.BlockSpec((SSM_GROUPS, 3, BLK, BLK), lambda c: (0, 0, 0, 0)),
        pl.BlockSpec((BLK, HPG * BLK), lambda c: (0, 0)),
    ]


N_SSD_IN = 15


def _xbc_act(zx_ref, tail_ref, w_ref, b_ref, n):
    tail = jnp.where(n > 0, tail_ref[:, SSM_INNER:], 0.0)
    xp = jnp.concatenate([tail, zx_ref[:, SSM_INNER:]], axis=0)
    taps = [_rows_from(xp, 5 + k) for k in range(4)]
    conv = _conv_taps(taps, w_ref[...]) + b_ref[...]
    valid = n * BLK + _iota((BLK, 1), 0) >= PAD
    return taps, conv, valid, jnp.where(valid, _silu(conv), 0.0)


def _grp_cols(act, i):
    b0, c0 = SSM_INNER + i * SSM_STATE, SSM_INNER + (SSM_GROUPS + i) * SSM_STATE
    return act[:, i * GRP_W:(i + 1) * GRP_W], act[:, b0:b0 + SSM_STATE], act[:, c0:c0 + SSM_STATE]


def _ssd_fwd(proj, conv_w, conv_b, dt_bias, a_log, d_skip, g_norm, gather=()):
    ng = len(gather)

    def body(*refs):
        zx_ref, tail_ref, dt_ref, w_ref, b_ref, dtb_ref, al_ref, dsk_ref, gn_ref = refs[:9]
        cn = dict(zip(SSD_CONSTS, refs[9:N_SSD_IN]))
        k0 = N_SSD_IN
        y_ref, st_ref = refs[k0 + ng:k0 + 2 + ng]
        s_scr, cst_scr = refs[k0 + 2 + 2 * ng:k0 + 4 + 2 * ng]
        c = pl.program_id(0)
        if ng:
            ag_start, ag_forward, ag_finish = _ag_program(refs[k0:k0 + ng], refs[k0 + 2 + ng:k0 + 2 + 2 * ng],
                                                          refs[k0 + 4 + 2 * ng:])
            pl.when(c == 0)(ag_start)
            pl.when(c == (3 * NB) // 4)(ag_forward)

        @pl.when(c == 0)
        def _():
            s_scr[...] = jnp.zeros_like(s_scr)

        _, _, _, act = _xbc_act(zx_ref, tail_ref, w_ref, b_ref, c)
        for i in range(SSM_GROUPS):
            st_prev = s_scr[i]
            st_ref[i, 0] = st_prev
            x, bm, cm = _grp_cols(act, i)
            f = _ssd_forward(x, zx_ref[:, i * GRP_W:(i + 1) * GRP_W], bm, cm, dt_ref[...], st_prev, dtb_ref[...],
                             al_ref[...], dsk_ref[...], gn_ref[:, i * GRP_W:(i + 1) * GRP_W], i, cst_scr.at[i], cn)
            y_ref[:, i * GRP_W:(i + 1) * GRP_W] = f["out"].astype(BF16)
            s_scr[i] = f["st_new"]
        if ng:
            pl.when(c == NB - 1)(ag_finish)

    return pl.pallas_call(
        body, grid=(NB,), in_specs=_ssd_in_specs(False) + [ANY] * ng,
        out_specs=[pl.BlockSpec((BLK, SSM_INNER), lambda c: (c, 0)),
                   pl.BlockSpec((SSM_GROUPS, 1, SSM_STATE, GRP_W), lambda c: (0, c, 0, 0))] + [ANY] * ng,
        out_shape=[SDS((T, SSM_INNER), BF16), SDS((SSM_GROUPS, NB, SSM_STATE, GRP_W), F32)]
        + [SDS((N_DEV,) + s.shape, s.dtype) for s in gather],
        scratch_shapes=[pltpu.VMEM((SSM_GROUPS, SSM_STATE, GRP_W), F32), pltpu.VMEM((SSM_GROUPS, BLK, BLK), F32)]
        + (_ag_scratch(gather) if ng else []),
        compiler_params=_cparams(),
        name="ssd_fwd")(proj, proj, proj, conv_w, conv_b, dt_bias, a_log, d_skip, g_norm,
                        *[_ssd_consts()[k] for k in SSD_CONSTS], *gather)


def _ssd_bwd(proj, conv_w, conv_b, dt_bias, a_log, d_skip, g_norm, states, dy, dproj, exchange=()):
    ne = len(exchange)

    def body(*refs):
        zx_ref, tail_ref, dt_ref, w_ref, b_ref, dtb_ref, al_ref, dsk_ref, gn_ref = refs[:9]
        cn = dict(zip(SSD_CONSTS, refs[9:N_SSD_IN]))
        st_ref, dy_ref = refs[N_SSD_IN:N_SSD_IN + 2]
        k0 = N_SSD_IN + 3
        (ddt_ref, dp_ref, ddtb_ref, dal_ref, ddsk_ref, dgn_ref, dcw_ref, dcb_ref) = refs[k0 + ne:k0 + 8 + ne]
        ds_scr, cst_scr, carry = refs[k0 + 8 + 2 * ne:k0 + 11 + 2 * ne]
        c = pl.program_id(0)
        n = NB - 1 - c
        if ne:
            ex_start, ex_finish = _direct_program(refs[k0:k0 + ne], refs[k0 + 8 + ne:k0 + 8 + 2 * ne],
                                                  refs[k0 + 11 + 2 * ne:])
            pl.when(c == 0)(ex_start)

        @pl.when(c == 0)
        def _():
            for ref in (ds_scr, carry, dgn_ref, ddtb_ref, dal_ref, ddsk_ref, dcw_ref, dcb_ref):
                ref[...] = jnp.zeros_like(ref)

        taps, conv, valid, act = _xbc_act(zx_ref, tail_ref, w_ref, b_ref, n)
        dt_raw = dt_ref[...]
        dxs, dbs, dcs = [], [], []
        for i in range(SSM_GROUPS):
            x, bm, cm = _grp_cols(act, i)
            z, gn, st_prev = zx_ref[:, i * GRP_W:(i + 1) * GRP_W], gn_ref[:, i * GRP_W:(i + 1) * GRP_W], st_ref[i, 0]
            f = _ssd_forward(x, z, bm, cm, dt_raw, st_prev, dtb_ref[...], al_ref[...], dsk_ref[...], gn, i,
                             cst_scr.at[i], cn)
            d = _ssd_backward(f, x, z, bm, cm, dt_raw, st_prev, dtb_ref[...], gn, i,
                              dy_ref[:, i * GRP_W:(i + 1) * GRP_W].astype(F32), ds_scr[i], cst_scr.at[i], cn)
            dxs.append(d["dx"])
            dbs.append(d["dbm"])
            dcs.append(d["dcm"])
            dp_ref[:, i * GRP_W:(i + 1) * GRP_W] = d["dz"].astype(BF16)
            ds_scr[i] = d["dst_prev"]
            ddt_ref[:, i * 128:(i + 1) * 128] = d["ddt_raw"]
            dgn_ref[0:1, i * GRP_W:(i + 1) * GRP_W] += d["dgn"]
            ddtb_ref[0:1, :] += d["ddtb"]
            dal_ref[0:1, :] += d["dalog"]
            ddsk_ref[0:1, :] += d["ddskip"]
        dconv = jnp.where(valid, jnp.concatenate(dxs + dbs + dcs, axis=1) * _dsilu(conv), 0.0)
        dext = jnp.concatenate([dconv, carry[...]], axis=0)
        dp_ref[:, SSM_INNER:] = _conv_taps([_rows_from(dext, 3 - k) for k in range(4)], w_ref[...]).astype(BF16)
        carry[...] = dconv[0:8]
        dcw_ref[...] += jnp.concatenate(
            [jnp.sum(dconv * taps[k], axis=0, keepdims=True) for k in range(4)]
            + [jnp.zeros((4, CONV_DIM), F32)], axis=0)
        dcb_ref[0:1, :] += jnp.sum(dconv, axis=0, keepdims=True)
        if ne:
            pl.when(c == NB - 1)(ex_finish)

    rc = lambda c: NB - 1 - c
    small = pl.BlockSpec((8, 128), lambda c: (0, 0))
    wide = lambda w: pl.BlockSpec((8, w), lambda c: (0, 0))
    return pl.pallas_call(
        body, grid=(NB,),
        in_specs=_ssd_in_specs(True) + [
            pl.BlockSpec((SSM_GROUPS, 1, SSM_STATE, GRP_W), lambda c: (0, rc(c), 0, 0)),
            pl.BlockSpec((BLK, SSM_INNER), lambda c: (rc(c), 0)), ANY] + [ANY] * ne,
        out_specs=[pl.BlockSpec((BLK, SSM_GROUPS * 128), lambda c: (rc(c), 0)),
                   pl.BlockSpec((BLK, ZX_W), lambda c: (rc(c), 0)),
                   small, small, small, wide(SSM_INNER), wide(CONV_DIM), wide(CONV_DIM)] + [ANY] * ne,
        out_shape=[SDS((T, GRP_W), F32), SDS((T, PW), BF16), SDS((8, 128), F32), SDS((8, 128), F32),
                   SDS((8, 128), F32), SDS((8, SSM_INNER), F32), SDS((8, CONV_DIM), F32), SDS((8, CONV_DIM), F32)]
        + [SDS(p.shape, p.dtype) for p in exchange],
        scratch_shapes=[pltpu.VMEM((SSM_GROUPS, SSM_STATE, GRP_W), F32), pltpu.VMEM((SSM_GROUPS, BLK, BLK), F32),
                        pltpu.VMEM((8, CONV_DIM), F32)] + (_direct_scratch(exchange) if ne else []),
        input_output_aliases={N_SSD_IN + 2: 1},
        compiler_params=_cparams(),
        name="ssd_bwd")(proj, proj, proj, conv_w, conv_b, dt_bias, a_log, d_skip, g_norm,
                        *[_ssd_consts()[k] for k in SSD_CONSTS], states, dy, dproj, *exchange)


POST_R = 272


def _post_a(o, proj, sn, w_att, w_ssm, w_o):
    def body(o_ref, za_ref, ga_ref, gs_ref, sn_ref, wa_ref, ws_ref, wo_ref, a_ref, mg_ref, ya_ref, ys_ref, out_ref):
        a = (o_ref[...] * _silu(za_ref[...])).astype(BF16)
        a_ref[...] = a
        ya = jnp.dot(a, wa_ref[...], preferred_element_type=F32)
        ys = jnp.dot(sn_ref[...], ws_ref[...], preferred_element_type=F32)
        ya_ref[...] = ya.astype(BF16)
        ys_ref[...] = ys.astype(BF16)
        mg = (jax.nn.sigmoid(ga_ref[...]) * ya + jax.nn.sigmoid(gs_ref[...]) * ys).astype(BF16)
        mg_ref[...] = mg
        out_ref[...] = jnp.dot(mg, wo_ref[...], preferred_element_type=F32)

    row = pl.BlockSpec((POST_R, D_MODEL), lambda i: (i, 0))
    pcol = lambda c0: pl.BlockSpec((POST_R, D_MODEL), lambda i: (i, c0 // D_MODEL))
    full = lambda r: pl.BlockSpec((r, D_MODEL), lambda i: (0, 0))
    return pl.pallas_call(
        body, grid=(T // POST_R,),
        in_specs=[row, pcol(C_ZA), pcol(C_GA), pcol(C_GS), pl.BlockSpec((POST_R, SSM_INNER), lambda i: (i, 0)),
                  full(D_MODEL), full(SSM_INNER), full(D_MODEL)],
        out_specs=[row, row, row, row, row],
        out_shape=[SDS((T, D_MODEL), BF16), SDS((T, D_MODEL), BF16), SDS((T, D_MODEL), BF16), SDS((T, D_MODEL), BF16),
                   SDS((T, D_MODEL), F32)],
        compiler_params=_cparams(), name="post_a")(o, proj, proj, proj, sn, w_att, w_ssm, w_o)


def _post_b(out, h, tgt, proj, ya, ys, o, g_post, w_att, w_ssm, w_o):
    def body(out_ref, h_ref, t_ref, za_ref, ga_ref, gs_ref, ya_ref, ys_ref, o_ref, gp_ref, wa_ref, ws_ref, wo_ref,
             loss_ref, dres_ref, dout_ref, dya_ref, dys_ref, do_ref, dp_ref, dsn_ref, dgp_ref):
        i = pl.program_id(0)
        x = out_ref[...]
        gp = gp_ref[...]
        r = lax.rsqrt(jnp.mean(x * x, axis=-1, keepdims=True) + EPS)
        row = i * POST_R + lax.broadcasted_iota(jnp.int32, (POST_R, 1), 0)
        res = h_ref[...] + jnp.where(row >= PAD, x * r * gp, 0.0)
        live = row >= PAD + N_META
        err = jnp.where(live, res - t_ref[...], 0.0)
        lpart = 0.5 * jnp.sum(jnp.sum(err * err, axis=1, keepdims=True) / D_MODEL, axis=0, keepdims=True)
        dres = err / D_MODEL
        dres_ref[...] = dres
        gpart = jnp.sum(dres * x * r, axis=0, keepdims=True)

        @pl.when(i == 0)
        def _():
            loss_ref[...] = jnp.zeros_like(loss_ref)
            dgp_ref[...] = jnp.zeros_like(dgp_ref)

        loss_ref[...] += jnp.broadcast_to(lpart, loss_ref.shape)
        dgp_ref[0:1, :] += gpart
        gd = gp * dres
        dout = (r * gd - x * (r * r * r) * jnp.mean(x * gd, axis=-1, keepdims=True)).astype(BF16)
        dout_ref[...] = dout
        dmg = lax.dot_general(dout, wo_ref[...], (((1,), (1,)), ((), ())), preferred_element_type=F32)
        sga = jax.nn.sigmoid(ga_ref[...])
        sgs = jax.nn.sigmoid(gs_ref[...])
        dya = (dmg * sga).astype(BF16)
        dys = (dmg * sgs).astype(BF16)
        dya_ref[...] = dya
        dys_ref[...] = dys
        dp_ref[:, C_GA - C_ZA:C_GA - C_ZA + D_MODEL] = (dmg * ya_ref[...].astype(F32) * sga * (1.0 - sga)).astype(BF16)
        dp_ref[:, C_GS - C_ZA:C_GS - C_ZA + D_MODEL] = (dmg * ys_ref[...].astype(F32) * sgs * (1.0 - sgs)).astype(BF16)
        da = lax.dot_general(dya, wa_ref[...], (((1,), (1,)), ((), ())), preferred_element_type=F32)
        za = za_ref[...]
        do_ref[...] = (da * _silu(za)).astype(BF16)
        dp_ref[:, 0:D_MODEL] = (da * o_ref[...] * _dsilu(za)).astype(BF16)
        dsn_ref[...] = lax.dot_general(dys, ws_ref[...], (((1,), (1,)), ((), ())),
                                       preferred_element_type=F32).astype(BF16)

    row = pl.BlockSpec((POST_R, D_MODEL), lambda i: (i, 0))
    pcol = lambda c0: pl.BlockSpec((POST_R, D_MODEL), lambda i: (i, c0 // D_MODEL))
    full = lambda r: pl.BlockSpec((r, D_MODEL), lambda i: (0, 0))
    small = pl.BlockSpec((8, D_MODEL), lambda i: (0, 0))
    return pl.pallas_call(
        body, grid=(T // POST_R,),
        in_specs=[row, row, row, pcol(C_ZA), pcol(C_GA), pcol(C_GS), row, row, row,
                  pl.BlockSpec((1, D_MODEL), lambda i: (0, 0)), full(D_MODEL), full(SSM_INNER), full(D_MODEL)],
        out_specs=[pl.BlockSpec((8, 128), lambda i: (0, 0)), row, row, row, row, row,
                   pl.BlockSpec((POST_R, GATES_W), lambda i: (i, C_ZA // GATES_W)),
                   pl.BlockSpec((POST_R, SSM_INNER), lambda i: (i, 0)), small],
        out_shape=[SDS((8, 128), F32), SDS((T, D_MODEL), F32), SDS((T, D_MODEL), BF16), SDS((T, D_MODEL), BF16),
                   SDS((T, D_MODEL), BF16), SDS((T, D_MODEL), BF16), SDS((T, PW), BF16),
                   SDS((T, SSM_INNER), BF16), SDS((8, D_MODEL), F32)],
        compiler_params=_cparams(), name="post_b")(out, h, tgt, proj, proj, proj, ya, ys, o, g_post, w_att, w_ssm, w_o)


TAIL_W = PW - C_K


def _dproj_tail(dproj, dk, dv, ddt4):
    rows = T // 4

    def body(_, dk_ref, dv_ref, ddt_ref, o_ref, buf, sem):
        n = pl.program_id(0)
        d4 = ddt_ref[...]
        buf[:, 0:KV_W] = dk_ref[...].astype(BF16)
        buf[:, KV_W:2 * KV_W] = dv_ref[...].astype(BF16)
        buf[:, 2 * KV_W:TAIL_W] = (d4[:, 0:128] + d4[:, 128:256] + d4[:, 256:384] + d4[:, 384:512]).astype(BF16)
        cp = pltpu.make_async_copy(buf, o_ref.at[pl.ds(pl.multiple_of(n * rows, 16), rows), pl.ds(C_K, TAIL_W)], sem)
        cp.start()
        cp.wait()

    spec = lambda w: pl.BlockSpec((rows, w), lambda i: (i, 0))
    return pl.pallas_call(
        body, grid=(T // rows,), in_specs=[ANY, spec(KV_W), spec(KV_W), spec(GRP_W)], out_specs=ANY,
        out_shape=SDS((T, PW), BF16), input_output_aliases={0: 0},
        scratch_shapes=[pltpu.VMEM((rows, TAIL_W), BF16), pltpu.SemaphoreType.DMA],
        name="dproj_tail")(dproj, dk, dv, ddt4)


def _adamw_math(w, g, m, v):
    m = ADAM_B1 * m + (1.0 - ADAM_B1) * g
    v = ADAM_B2 * v + (1.0 - ADAM_B2) * (g * g)
    m_hat = m / (1.0 - ADAM_B1 ** ADAM_STEP)
    v_hat = v / (1.0 - ADAM_B2 ** ADAM_STEP)
    delta = -ADAM_LR * (m_hat / (jnp.sqrt(v_hat) + ADAM_EPS) + ADAM_WD * w)
    return delta, m, v


def _sum_adamw(recv, w, m, v, tc, name):
    rows, cols = w.shape
    nslab = recv.shape[0]
    assert cols % tc == 0

    def body(r_ref, w_ref, m_ref, v_ref, g_ref, d_ref, nm_ref, nv_ref):
        g = r_ref[0].astype(F32)
        for d in range(1, nslab):
            g = g + r_ref[d].astype(F32)
        g_ref[...] = g
        delta, nm, nv = _adamw_math(w_ref[...], g, m_ref[...], v_ref[...])
        d_ref[...] = delta
        nm_ref[...] = nm
        nv_ref[...] = nv

    blk = pl.BlockSpec((rows, tc), lambda i: (0, i))
    return pl.pallas_call(
        body, grid=(cols // tc,),
        in_specs=[pl.BlockSpec((nslab, rows, tc), lambda i: (0, 0, i)), blk, blk, blk],
        out_specs=[blk, blk, blk, blk], out_shape=[SDS((rows, cols), F32)] * 4,
        compiler_params=_cparams(), name=name)(recv, w, m, v)


def _sum_adamw_rows3(recv, w3, m3, v3, name, exchange=()):
    pairs = 61
    assert (SHARD_IN // 2) % pairs == 0
    nsteps = SHARD_IN // 2 // pairs
    ne = len(exchange)

    def body(*refs):
        r_ref, w_ref, m_ref, v_ref = refs[:4]
        g_ref, d_ref, nm_ref, nv_ref = refs[4 + ne:8 + ne]
        if ne:
            ex_start, ex_finish = _direct_program(refs[4:4 + ne], refs[8 + ne:8 + 2 * ne], refs[8 + 2 * ne:])
            pl.when(pl.program_id(0) == 0)(ex_start)
        g = r_ref[0].astype(F32)
        for d in range(1, N_CHIP):
            g = g + r_ref[d].astype(F32)
        g = g.reshape(2 * pairs, ROW_TILES, 128)
        g_ref[...] = g
        delta, nm, nv = _adamw_math(w_ref[...], g, m_ref[...], v_ref[...])
        d_ref[...] = delta
        nm_ref[...] = nm
        nv_ref[...] = nv
        if ne:
            pl.when(pl.program_id(0) == nsteps - 1)(ex_finish)

    blk = pl.BlockSpec((2 * pairs, ROW_TILES, 128), lambda i: (i, 0, 0))
    return pl.pallas_call(
        body, grid=(nsteps,),
        in_specs=[pl.BlockSpec((N_CHIP, pairs, 2 * ROW_TILES, 128), lambda i: (0, i, 0, 0)), blk, blk, blk]
        + [ANY] * ne,
        out_specs=[blk, blk, blk, blk] + [ANY] * ne,
        out_shape=[SDS(w3.shape, F32)] * 4 + [SDS(p.shape, p.dtype) for p in exchange],
        scratch_shapes=_direct_scratch(exchange) if ne else [],
        compiler_params=_cparams(), name=name)(recv, w3, m3, v3, *exchange)


ROW_GPRE, ROW_CONVB, ROW_DTB, ROW_ALOG, ROW_DSKIP, ROW_SINK, ROW_GSSM, ROW_GPOST = 0, 1, 4, 5, 6, 7, 8, 10
ROW_LOSS = 11
REP_ROWS, ROW_CONVW, ROW_META, SM_ROWS = 16, 16, 24, 40
CW_SHARD = CONV_DIM // N_DEV
META_SHARD = D_MODEL // N_DEV


def _small_pack(dgpre, db, ddtb, dal, ddsk, dsink, dgn, dgp, dw, loss, dh):
    def body(dgpre_ref, db_ref, ddtb_ref, dal_ref, ddsk_ref, dsink_ref, dgn_ref, dgp_ref, dw_ref, loss_ref, dh_ref,
             o_ref, rep):
        rep[...] = jnp.zeros_like(rep)
        rep[ROW_LOSS:ROW_LOSS + 1, 0:128] = loss_ref[0:1, :]
        rep[ROW_GPRE:ROW_GPRE + 1, :] = dgpre_ref[0:1, :]
        for k in range(3):
            rep[ROW_CONVB + k:ROW_CONVB + k + 1, :] = db_ref[0:1, 1024 * k:1024 * (k + 1)]
        rep[ROW_DTB:ROW_DTB + 1, 0:128] = ddtb_ref[0:1, :]
        rep[ROW_ALOG:ROW_ALOG + 1, 0:128] = dal_ref[0:1, :]
        rep[ROW_DSKIP:ROW_DSKIP + 1, 0:128] = ddsk_ref[0:1, :]
        rep[ROW_SINK:ROW_SINK + 1, 0:128] = dsink_ref[0:1, :]
        rep[ROW_GSSM:ROW_GSSM + 1, :] = dgn_ref[0:1, 0:1024]
        rep[ROW_GSSM + 1:ROW_GSSM + 2, :] = dgn_ref[0:1, 1024:2048]
        rep[ROW_GPOST:ROW_GPOST + 1, :] = dgp_ref[0:1, :]
        cw = dw_ref[...]
        mh = dh_ref[...]
        o_ref[...] = jnp.zeros_like(o_ref)
        for p in range(N_DEV):
            o_ref[p, 0:REP_ROWS, :] = rep[...]
            o_ref[p, ROW_CONVW:ROW_CONVW + 8, 0:CW_SHARD] = cw[:, p * CW_SHARD:(p + 1) * CW_SHARD]
            o_ref[p, ROW_META:ROW_META + N_META, 0:META_SHARD] = mh[:, p * META_SHARD:(p + 1) * META_SHARD]

    ins = [dgpre, db, ddtb, dal, ddsk, dsink, dgn, dgp, dw, loss]
    return pl.pallas_call(
        body, grid=(1,),
        in_specs=[pl.BlockSpec(a.shape, lambda i: (0, 0)) for a in ins]
        + [pl.BlockSpec((N_META, D_MODEL), lambda i: (PAD // N_META, 0))],
        out_specs=pl.BlockSpec((N_DEV, SM_ROWS, 1024), lambda i: (0, 0, 0)),
        out_shape=SDS((N_DEV, SM_ROWS, 1024), F32), scratch_shapes=[pltpu.VMEM((REP_ROWS, 1024), F32)],
        name="small_pack")(*ins, dh)


def _small_finish(recv, params):
    npar = len(params)

    def body(*refs):
        r_ref = refs[0]
        wmv = refs[1:1 + 3 * npar]
        outs = refs[1 + 3 * npar:1 + 7 * npar]
        loss_ref = refs[1 + 7 * npar]
        gs = refs[-1]
        g = r_ref[0]
        for d in range(1, recv.shape[0]):
            g = g + r_ref[d]
        gs[...] = g
        loss_ref[...] = gs[ROW_LOSS:ROW_LOSS + 1, 0:128]
        grads = [
            gs[ROW_GPRE:ROW_GPRE + 1, :],
            jnp.concatenate([gs[ROW_CONVB + k:ROW_CONVB + k + 1, :] for k in range(3)], axis=1),
            gs[ROW_DTB:ROW_DTB + 1, 0:SSM_HEADS], gs[ROW_ALOG:ROW_ALOG + 1, 0:SSM_HEADS],
            gs[ROW_DSKIP:ROW_DSKIP + 1, 0:SSM_HEADS], gs[ROW_SINK:ROW_SINK + 1, 0:Q_HEADS],
            jnp.concatenate([gs[ROW_GSSM:ROW_GSSM + 1, :], gs[ROW_GSSM + 1:ROW_GSSM + 2, :]], axis=1),
            gs[ROW_GPOST:ROW_GPOST + 1, :],
            gs[ROW_CONVW:ROW_CONVW + 4, 0:CW_SHARD],
            gs[ROW_META:ROW_META + N_META, 0:META_SHARD]]
        for i in range(npar):
            w_ref, m_ref, v_ref = wmv[3 * i:3 * i + 3]
            delta, nm, nv = _adamw_math(w_ref[...], grads[i], m_ref[...], v_ref[...])
            outs[4 * i][...] = grads[i]
            outs[4 * i + 1][...] = delta
            outs[4 * i + 2][...] = nm
            outs[4 * i + 3][...] = nv

    flat = [a for wmv in params for a in wmv]
    res = pl.pallas_call(
        body, out_shape=[SDS(wmv[0].shape, F32) for wmv in params for _ in range(4)] + [SDS((1, 128), F32)],
        scratch_shapes=[pltpu.VMEM((SM_ROWS, 1024), F32)], name="small_finish")(recv, *flat)
    return [tuple(res[4 * i:4 * i + 4]) for i in range(npar)], res[4 * npar]


def _slab(ref, px, py, pc):
    return ref.at[4 * px + 2 * py + pc]


def _bounce(src, dst, buf, sem):
    cp = pltpu.make_async_copy(src, buf, sem)
    cp.start()
    cp.wait()
    cp = pltpu.make_async_copy(buf, dst, sem)
    cp.start()
    cp.wait()


def _ag_program(ins, outs, scratch):
    na = len(ins)
    send_sems, recv_sems, local_sems = scratch[:3]
    bufs = scratch[3:]
    x, y, c = lax.axis_index("x"), lax.axis_index("y"), lax.axis_index("c")
    me, sibling = (x, y, c), (x, y, 1 - c)
    chips = [(1 - x, y), (x, 1 - y), (1 - x, 1 - y)]

    def copy(a, k, block, to, src=None):
        dst = _slab(outs[a], *block)
        return pltpu.make_async_remote_copy(
            src_ref=dst if src is None else src, dst_ref=dst, send_sem=send_sems.at[a, k],
            recv_sem=recv_sems.at[a, k], device_id=to, device_id_type=MESH)

    def own_sends():
        out = []
        for a in range(na):
            out.append(copy(a, 0, me, sibling, src=ins[a]))
            out += [copy(a, 1 + j, me, (*chip, c), src=ins[a]) for j, chip in enumerate(chips)]
        return out

    def start():
        for cp in own_sends():
            cp.start()
        for a in range(na):
            _bounce(ins[a], _slab(outs[a], *me), bufs[a], local_sems.at[a])

    def forward():
        for j, chip in enumerate(chips):
            for a in range(na):
                copy(a, 1 + j, (*chip, c), me).wait_recv()
                copy(a, 4 + j, (*chip, c), sibling).start()

    def finish():
        for a in range(na):
            copy(a, 0, sibling, me).wait_recv()
            for j, chip in enumerate(chips):
                copy(a, 4 + j, (*chip, 1 - c), me).wait_recv()
        for cp in own_sends():
            cp.wait_send()
        for j, chip in enumerate(chips):
            for a in range(na):
                copy(a, 4 + j, (*chip, c), sibling).wait_send()

    return start, forward, finish


def _ag_scratch(shards):
    na = len(shards)
    return [pltpu.SemaphoreType.DMA((na, 7)), pltpu.SemaphoreType.DMA((na, 7)),
            pltpu.SemaphoreType.DMA((na,))] + [pltpu.VMEM(s.shape, s.dtype) for s in shards]


def _all_gather(shards):
    na = len(shards)

    def body(*refs):
        start, forward, finish = _ag_program(refs[:na], refs[na:2 * na], refs[2 * na:])
        start()
        forward()
        finish()

    return pl.pallas_call(
        body, in_specs=[ANY] * na, out_specs=[ANY] * na,
        out_shape=[SDS((N_DEV,) + s.shape, s.dtype) for s in shards],
        scratch_shapes=_ag_scratch(shards), name="all_gather")(*shards)


N_CHIP = 4


def _pair_sum(own, got, name):
    na = len(own)

    def body(*refs):
        for a in range(na):
            o_ref, g_ref, s_ref = refs[a], refs[na + a], refs[2 * na + a]
            s_ref[...] = (o_ref[...].astype(F32) + g_ref[...].astype(F32)).astype(s_ref.dtype)

    def spec(p):
        nd = len(p.shape) - 1
        return pl.BlockSpec((1,) + p.shape[1:], lambda k, nd=nd: (k,) + (0,) * nd)

    return pl.pallas_call(
        body, grid=(N_CHIP,), in_specs=[spec(p) for p in own] + [spec(p) for p in got],
        out_specs=[spec(p) for p in own], out_shape=[SDS(p.shape, p.dtype) for p in own],
        compiler_params=_cparams(), name=name)(*own, *got)


def _chips_program(ins, outs, scratch):
    na = len(ins)
    send_sems, recv_sems, local_sems = scratch[:3]
    bufs = scratch[3:]
    x, y, c = lax.axis_index("x"), lax.axis_index("y"), lax.axis_index("c")
    mine = 2 * x + y
    chips = [(1 - x, y), (x, 1 - y), (1 - x, 1 - y)]

    def send(a, j):
        px, py = chips[j]
        return pltpu.make_async_remote_copy(
            src_ref=ins[a].at[2 * px + py], dst_ref=outs[a].at[mine], send_sem=send_sems.at[a, j],
            recv_sem=recv_sems.at[a, j], device_id=(px, py, c), device_id_type=MESH)

    def arrival(a, j):
        px, py = chips[j]
        return pltpu.make_async_remote_copy(
            src_ref=ins[a].at[2 * px + py], dst_ref=outs[a].at[2 * px + py], send_sem=send_sems.at[a, j],
            recv_sem=recv_sems.at[a, j], device_id=(px, py, c), device_id_type=MESH)

    def start():
        for a in range(na):
            for j in range(3):
                send(a, j).start()
        for a in range(na):
            _bounce(ins[a].at[mine], outs[a].at[mine], bufs[a], local_sems.at[a])

    def finish():
        for a in range(na):
            for j in range(3):
                arrival(a, j).wait_recv()
        for a in range(na):
            for j in range(3):
                send(a, j).wait_send()

    return start, finish


def _chips_scratch(parts):
    na = len(parts)
    return [pltpu.SemaphoreType.DMA((na, 3)), pltpu.SemaphoreType.DMA((na, 3)),
            pltpu.SemaphoreType.DMA((na,))] + [pltpu.VMEM(p.shape[1:], p.dtype) for p in parts]


def _direct_program(ins, outs, scratch):
    na = len(ins)
    send_sems, recv_sems, local_sems = scratch[:3]
    bufs = scratch[3:]
    x, y, c = lax.axis_index("x"), lax.axis_index("y"), lax.axis_index("c")
    me = (x, y, c)
    peers = []
    for k in range(1, N_DEV):
        dx, dy, dc = (k >> 2) & 1, (k >> 1) & 1, k & 1
        peers.append(((1 - x) if dx else x, (1 - y) if dy else y, (1 - c) if dc else c))

    def send(a, k):
        return pltpu.make_async_remote_copy(
            src_ref=_slab(ins[a], *peers[k]), dst_ref=_slab(outs[a], *me), send_sem=send_sems.at[a, k],
            recv_sem=recv_sems.at[a, k], device_id=peers[k], device_id_type=MESH)

    def arrival(a, k):
        return pltpu.make_async_remote_copy(
            src_ref=_slab(ins[a], *peers[k]), dst_ref=_slab(outs[a], *peers[k]), send_sem=send_sems.at[a, k],
            recv_sem=recv_sems.at[a, k], device_id=peers[k], device_id_type=MESH)

    def start():
        for a in range(na):
            for k in range(N_DEV - 1):
                send(a, k).start()
        for a in range(na):
            _bounce(_slab(ins[a], *me), _slab(outs[a], *me), bufs[a], local_sems.at[a])

    def finish():
        for a in range(na):
            for k in range(N_DEV - 1):
                arrival(a, k).wait_recv()
        for a in range(na):
            for k in range(N_DEV - 1):
                send(a, k).wait_send()

    return start, finish


def _direct_scratch(parts):
    na = len(parts)
    return [pltpu.SemaphoreType.DMA((na, N_DEV - 1)), pltpu.SemaphoreType.DMA((na, N_DEV - 1)),
            pltpu.SemaphoreType.DMA((na,))] + [pltpu.VMEM(p.shape[1:], p.dtype) for p in parts]


ROW_TILES = D_MODEL // 128


def _rows3(t):
    return jnp.transpose(t[0]).reshape(t.shape[2], ROW_TILES, 128)


def _unrows3(t):
    return jnp.transpose(t.reshape(t.shape[0], D_MODEL))[None]


def _cast_shards(w_in3, w_att, w_ssm, w_o):
    def body(wi_ref, wa_ref, ws_ref, wo_ref, a_ref, b_ref, c_ref, d_ref):
        a_ref[...] = wi_ref[...].reshape(SHARD_IN // 2, 2 * ROW_TILES, 128).astype(BF16)
        b_ref[...] = wa_ref[...].astype(BF16)
        c_ref[...] = ws_ref[...].astype(BF16)
        d_ref[...] = wo_ref[...].astype(BF16)

    return pl.pallas_call(
        body, out_shape=[SDS((SHARD_IN // 2, 2 * ROW_TILES, 128), BF16), SDS(w_att.shape, BF16),
                         SDS(w_ssm.shape, BF16), SDS(w_o.shape, BF16)],
        compiler_params=_cparams(), name="cast_shards")(w_in3, w_att, w_ssm, w_o)


def _pieces():
    out = []
    for r0, c0, w in _SEGS:
        r = r0
        while r < r0 + w:
            d = r // SHARD_IN
            n = min(r0 + w, (d + 1) * SHARD_IN) - r
            out.append((c0 + (r - r0), d, r - d * SHARD_IN, n))
            r += n
    return out


def _to_aligned_t(slabs):
    def body(a_ref, o_ref):
        for (t, d, s, n) in _pieces():
            o_ref[t:t + n, :] = a_ref[d, s // 2:(s + n) // 2].reshape(n, D_MODEL)
        o_ref[C_DT + 32:C_DT + 128, :] = jnp.zeros((96, D_MODEL), slabs.dtype)

    return pl.pallas_call(body, out_shape=SDS((PW, D_MODEL), slabs.dtype), compiler_params=_cparams(),
                          name="to_aligned")(slabs)


def _from_aligned_pair(g):
    slab = (SHARD_IN // 2, 2 * ROW_TILES, 128)
    by_slab = [[p for p in _pieces() if p[1] == d] for d in range(N_DEV)]

    def body(g_ref, own_ref, got_ref, slabs, send_sems, recv_sems, local_sems):
        x, y, c = lax.axis_index("x"), lax.axis_index("y"), lax.axis_index("c")
        sibling = (x, y, 1 - c)

        def to_own(d, k):
            return pltpu.make_async_copy(slabs.at[d], own_ref.at[k], local_sems.at[k])

        def to_sibling(d, k):
            return pltpu.make_async_remote_copy(
                src_ref=slabs.at[d], dst_ref=got_ref.at[k], send_sem=send_sems.at[k], recv_sem=recv_sems.at[k],
                device_id=sibling, device_id_type=MESH)

        for d in range(N_DEV):
            for (t, _, s, n) in by_slab[d]:
                slabs[d, s // 2:(s + n) // 2] = g_ref[t:t + n, :].reshape(n // 2, 2 * ROW_TILES, 128)
            k, side = d // 2, d % 2
            pl.when(c == side)(to_own(d, k).start)
            pl.when(c != side)(to_sibling(d, k).start)
        for k in range(N_CHIP):
            to_own(0, k).wait()
            to_sibling(0, k).wait()

    half = SDS((N_CHIP,) + slab, g.dtype)
    return pl.pallas_call(
        body, in_specs=[pl.BlockSpec(memory_space=pltpu.VMEM)], out_specs=[ANY, ANY], out_shape=[half, half],
        scratch_shapes=[pltpu.VMEM((N_DEV,) + slab, g.dtype), pltpu.SemaphoreType.DMA((N_CHIP,)),
                        pltpu.SemaphoreType.DMA((N_CHIP,)), pltpu.SemaphoreType.DMA((N_CHIP,))],
        compiler_params=_cparams(), name="from_aligned_pair")(g)


_SEGS = [
    (R_Q, C_Q, 1024), (R_K, C_K, 256), (R_V, C_V, 256), (R_ZA, C_ZA, 1024), (R_ZS, C_ZS, 2048),
    (R_XBC, C_XBC, 3072), (R_DT, C_DT, 32), (R_GA, C_GA, 1024), (R_GS, C_GS, 1024)]


def _pad_lanes(v, n=128):
    return jnp.pad(v, ((0, 0), (0, n - v.shape[1])))


def _device_step(h, tgt, w_alt, w_out, g_pre, conv_w8, conv_b, dt_bias, a_log, d_skip, sinks, g_ssm, g_post, on_mesh):
    dtb, al, dsk, snk = _pad_lanes(dt_bias), _pad_lanes(a_log), _pad_lanes(d_skip), _pad_lanes(sinks)
    u = _norm_u(h, g_pre)
    proj = _matmul(u, w_alt, "nt", F32, T, PROJ_TILE, "in_proj")
    o = _attn_fwd(proj, snk)
    if on_mesh:
        sn, states, att_all, ssm_all, o_all = _ssd_fwd(proj, conv_w8, conv_b, dtb, al, dsk, g_ssm, gather=w_out)
        w_att = att_all.reshape(D_MODEL, D_MODEL)
        w_ssm = ssm_all.reshape(SSM_INNER, D_MODEL)
        w_o = o_all.reshape(D_MODEL, D_MODEL)
    else:
        sn, states = _ssd_fwd(proj, conv_w8, conv_b, dtb, al, dsk, g_ssm)
        w_att, w_ssm, w_o = w_out
    a_in, mg, ya, ys, out = _post_a(o, proj, sn, w_att, w_ssm, w_o)
    (loss, dres, dout, dya, dys, do, dproj, dsn, dgp) = _post_b(
        out, h, tgt, proj, ya, ys, o, g_post, w_att, w_ssm, w_o)
    dw_att = _matmul(a_in, dya, "tn", BF16, OUT_W_TILE, D_MODEL, "d_w_att")
    dw_ssm = _matmul(sn, dys, "tn", BF16, OUT_W_TILE, D_MODEL, "d_w_ssm")
    dw_o = _matmul(mg, dout, "tn", BF16, OUT_W_TILE, D_MODEL, "d_w_o")
    res = {}
    if on_mesh:
        parts = [dw_att.reshape(N_DEV, 128, D_MODEL), dw_ssm.reshape(N_DEV, 256, D_MODEL),
                 dw_o.reshape(N_DEV, 128, D_MODEL)]
        (ddt4, dproj, ddtb, dal, ddsk, dgn, dcw, dcb, res["r_att"], res["r_ssm"], res["r_o"]) = _ssd_bwd(
            proj, conv_w8, conv_b, dtb, al, dsk, g_ssm, states, dsn, dproj, exchange=parts)
    else:
        ddt4, dproj, ddtb, dal, ddsk, dgn, dcw, dcb = _ssd_bwd(proj, conv_w8, conv_b, dtb, al, dsk, g_ssm, states,
                                                               dsn, dproj)
        res.update(dw_att=dw_att, dw_ssm=dw_ssm, dw_o=dw_o)
    dproj, dk, dv, dsink = _attn_bwd(proj, snk, do, dproj)
    dproj = _dproj_tail(dproj, dk, dv, ddt4)
    dw_alt = _matmul(dproj, u, "tn", BF16, PROJ_TILE, D_MODEL, "d_w_in")
    if on_mesh:
        own, got = _from_aligned_pair(dw_alt)
        dh, dgpre, res["r_in"] = _d_u_norm(dproj, w_alt, h, g_pre, dres,
                                           chips=_pair_sum([own], [got], "pair_sum_w_in"))
    else:
        dh, dgpre = _d_u_norm(dproj, w_alt, h, g_pre, dres)
        res["dw_alt"] = dw_alt
    small = (dgpre, dcb, ddtb, dal, ddsk, dsink, dgn, dgp, dcw)
    if on_mesh:
        res["small_pack"] = _small_pack(*small, loss, dh)
    else:
        res["small"] = small
    res.update(loss=loss[0, 0], dh=dh)
    return res


def kernel(x, meta_tokens, g_pre, w_in, conv_w, conv_b, dt_bias, a_log, d_skip, attn_sinks, g_ssm_norm, w_out_att, w_out_ssm, w_out, g_post, loss_target, m_meta_tokens, m_g_pre, m_w_in, m_conv_w, m_conv_b, m_dt_bias, m_a_log, m_d_skip, m_attn_sinks, m_g_ssm_norm, m_w_out_att, m_w_out_ssm, m_w_out, m_g_post, v_meta_tokens, v_g_pre, v_w_in, v_conv_w, v_conv_b, v_dt_bias, v_a_log, v_d_skip, v_attn_sinks, v_g_ssm_norm, v_w_out_att, v_w_out_ssm, v_w_out, v_g_post):
    w_in3, m_in3, v_in3 = _rows3(w_in), _rows3(m_w_in), _rows3(v_w_in)
    a_sh, att_sh, ssm_sh, o_sh = _cast_shards(w_in3, w_out_att[0], w_out_ssm[0], w_out[0])
    cw_sh = jnp.pad(conv_w[0], ((0, 4), (0, 0)))
    a_all, meta_all, cw_all = _all_gather([a_sh, meta_tokens, cw_sh])
    w_alt = _to_aligned_t(a_all)
    meta_full = meta_all.transpose(1, 0, 2).reshape(N_META, D_MODEL)
    conv_w8 = cw_all.transpose(1, 0, 2).reshape(8, CONV_DIM)

    h = jnp.concatenate([jnp.zeros((PAD, D_MODEL), F32), meta_full, x[0]], axis=0)
    tgt = jnp.concatenate([jnp.zeros((PAD + N_META, D_MODEL), F32), loss_target[0]], axis=0)
    r = _device_step(h, tgt, w_alt, (att_sh, ssm_sh, o_sh), g_pre, conv_w8, conv_b, dt_bias, a_log, d_skip,
                     attn_sinks, g_ssm_norm, g_post, True)
    grad_x = r["dh"][PAD + N_META:][None]

    *res_in, r_small = _sum_adamw_rows3(r["r_in"], w_in3, m_in3, v_in3, "adamw_w_in", exchange=[r["small_pack"]])
    res_in = [_unrows3(t) for t in res_in]
    res_att = [t[None] for t in _sum_adamw(r["r_att"], w_out_att[0], m_w_out_att[0], v_w_out_att[0], 512,
                                           "adamw_w_att")]
    res_ssm = [t[None] for t in _sum_adamw(r["r_ssm"], w_out_ssm[0], m_w_out_ssm[0], v_w_out_ssm[0], 512,
                                           "adamw_w_ssm")]
    res_o = [t[None] for t in _sum_adamw(r["r_o"], w_out[0], m_w_out[0], v_w_out[0], 512, "adamw_w_o")]
    (res_gpre, res_convb, res_dtb, res_alog, res_dskip, res_sink, res_gssm, res_gpost, res_cw, res_meta), loss = _small_finish(
        r_small, [(g_pre, m_g_pre, v_g_pre), (conv_b, m_conv_b, v_conv_b), (dt_bias, m_dt_bias, v_dt_bias),
                       (a_log, m_a_log, v_a_log), (d_skip, m_d_skip, v_d_skip),
                       (attn_sinks, m_attn_sinks, v_attn_sinks), (g_ssm_norm, m_g_ssm_norm, v_g_ssm_norm),
                       (g_post, m_g_post, v_g_post), (conv_w[0], m_conv_w[0], v_conv_w[0]),
                       (meta_tokens, m_meta_tokens, v_meta_tokens)])
    res_cw = [t[None] for t in res_cw]
    per_weight = [res_meta, res_gpre, res_in, res_cw, res_convb, res_dtb, res_alog, res_dskip, res_sink, res_gssm,
                  res_att, res_ssm, res_o, res_gpost]
    return (loss[0, 0], grad_x, *[p[0] for p in per_weight], *[p[1] for p in per_weight], *[p[2] for p in per_weight],
            *[p[3] for p in per_weight])
```
